```python
import jax, jax.numpy as jnp
from jax import lax
import numpy as np

D_MODEL = 1024
BATCH = 8
SEQ = 4096
DEPTH = 2

N_A_LAYERS = DEPTH // 2
N_B_LAYERS = DEPTH - N_A_LAYERS

CHUNK = 128
SGU_WIDTH = D_MODEL
SGU_GROUPS = 8
SGU_GROUP_DIM = SGU_WIDTH // SGU_GROUPS

HEAD_DIM = 64
N_Q_HEADS = D_MODEL // HEAD_DIM
N_KV_HEADS = 4
Q_PER_KV = N_Q_HEADS // N_KV_HEADS
WINDOW = 128
BLOCK = WINDOW

D_FF = ((8 * D_MODEL // 3 + 255) // 256) * 256
CONV_WIDTH = 3

EPS = 1e-6

kernel_name = "yoco_gmlp_swa_sink_convffn"


def rms_norm(x, g):
    xf = x.astype(jnp.float32)
    y = xf * lax.rsqrt(jnp.mean(xf * xf, axis=-1, keepdims=True) + EPS)
    return (y * g.astype(jnp.float32)).astype(x.dtype)


def alibi_slopes(n_heads):
    h = jnp.arange(1, n_heads + 1, dtype=jnp.float32)
    return jnp.exp2(-8.0 * h / n_heads)


def causal_depthwise_conv(h, w, b):
    c = h.shape[-1]
    y = lax.conv_general_dilated(
        h, w[:, None, :].astype(h.dtype), window_strides=(1,),
        padding=[(CONV_WIDTH - 1, 0)], dimension_numbers=("NWC", "WIO", "NWC"),
        feature_group_count=c)
    return y + b.astype(h.dtype)


def conv_ffn(x, g_norm, w_in, conv_w, conv_b, w_out):
    h = rms_norm(x, g_norm)
    hu = causal_depthwise_conv(h @ w_in, conv_w, conv_b)
    gate, up = jnp.split(hu, 2, axis=-1)
    return (jax.nn.silu(gate) * up) @ w_out


def chunked_sgu(x, g_norm, w_in, g_v, w_s, b_s, w_out):
    b, s, _ = x.shape
    nc = s // CHUNK
    h = rms_norm(x, g_norm)
    z = jax.nn.gelu(h @ w_in, approximate=False)
    u, v = jnp.split(z, 2, axis=-1)
    v = rms_norm(v, g_v).reshape(b, nc, CHUNK, SGU_GROUPS, SGU_GROUP_DIM)
    tril = jnp.tril(jnp.ones((CHUNK, CHUNK), dtype=bool))
    w_causal = jnp.where(tril[None], w_s, 0).astype(v.dtype)
    sv = jnp.einsum("gts,bnsgc->bntgc", w_causal, v) + b_s.T.astype(v.dtype)[None, None, :, :, None]
    y = u.reshape(b, nc, CHUNK, SGU_GROUPS, SGU_GROUP_DIM) * sv
    return y.reshape(b, s, SGU_WIDTH) @ w_out


def to_band(t):
    b, s, hk, hd = t.shape
    blk = t.reshape(b, s // BLOCK, BLOCK, hk, hd)
    prev = jnp.pad(blk[:, :-1], ((0, 0), (1, 0), (0, 0), (0, 0), (0, 0)))
    return jnp.concatenate([prev, blk], axis=2)


def shared_kv(x, g_kv, w_kv, g_k):
    b, s, _ = x.shape
    kv = rms_norm(x, g_kv) @ w_kv
    k, v = jnp.split(kv, 2, axis=-1)
    k = rms_norm(k.reshape(b, s, N_KV_HEADS, HEAD_DIM), g_k)
    v = v.reshape(b, s, N_KV_HEADS, HEAD_DIM)
    return to_band(k), to_band(v)


def swa_sink_attention(x, k_band, v_band, g_norm, w_q, g_q, sinks, w_o):
    b, s, _ = x.shape
    nb = s // BLOCK
    q = (rms_norm(x, g_norm) @ w_q).reshape(b, s, N_Q_HEADS, HEAD_DIM)
    q = rms_norm(q, g_q).reshape(b, nb, BLOCK, N_KV_HEADS, Q_PER_KV, HEAD_DIM)
    logits = jnp.einsum("bntkgd,bnskd->bnkgts", q, k_band).astype(jnp.float32)
    logits = logits * (HEAD_DIM ** -0.5)
    t_idx = jnp.arange(BLOCK)[:, None]
    j_idx = jnp.arange(2 * BLOCK)[None, :]
    dist = t_idx + BLOCK - j_idx
    in_window = (dist >= 0) & (dist < WINDOW)
    first_blk = (jnp.arange(nb) == 0)[:, None, None] & (j_idx < BLOCK)[None]
    mask = in_window[None] & ~first_blk
    slopes = alibi_slopes(N_Q_HEADS).reshape(N_KV_HEADS, Q_PER_KV)
    bias = -slopes[:, :, None, None] * dist.astype(jnp.float32)[None, None]
    logits = jnp.where(mask[None, :, None, None], logits + bias[None, None], -jnp.inf)
    sink = sinks.astype(jnp.float32).reshape(N_KV_HEADS, Q_PER_KV)[None, None, :, :, None, None]
    m = jnp.maximum(jnp.max(logits, axis=-1, keepdims=True), sink)
    p = jnp.exp(logits - m)
    denom = jnp.sum(p, axis=-1, keepdims=True) + jnp.exp(sink - m)
    p = (p / denom).astype(v_band.dtype)
    o = jnp.einsum("bnkgts,bnskd->bntkgd", p, v_band).reshape(b, s, N_Q_HEADS * HEAD_DIM)
    return o @ w_o


def _fwd_setup_inputs(seed: int = 0) -> dict:
    key = jax.random.key(seed)
    ks = jax.random.split(key, 20)

    def nrm(k, shape, scale):
        return jax.random.normal(k, shape, jnp.float32) * scale

    def gain(k, shape):
        return 1.0 + 0.1 * jax.random.normal(k, shape, jnp.float32)

    qd = N_Q_HEADS * HEAD_DIM
    return {
        "x": nrm(ks[0], (BATCH, SEQ, D_MODEL), 1.0),
        "a_norm": gain(ks[1], (N_A_LAYERS, D_MODEL)),
        "a_w_in": nrm(ks[2], (N_A_LAYERS, D_MODEL, 2 * SGU_WIDTH), D_MODEL ** -0.5),
        "a_v_norm": gain(ks[3], (N_A_LAYERS, SGU_WIDTH)),
        "a_w_s": nrm(ks[4], (N_A_LAYERS, SGU_GROUPS, CHUNK, CHUNK), CHUNK ** -0.5),
        "a_b_s": gain(ks[5], (N_A_LAYERS, SGU_GROUPS, CHUNK)),
        "a_w_out": nrm(ks[6], (N_A_LAYERS, SGU_WIDTH, D_MODEL), SGU_WIDTH ** -0.5),
        "f_norm": gain(ks[7], (DEPTH, D_MODEL)),
        "f_w_in": nrm(ks[8], (DEPTH, D_MODEL, 2 * D_FF), D_MODEL ** -0.5),
        "f_conv_w": nrm(ks[9], (DEPTH, CONV_WIDTH, 2 * D_FF), CONV_WIDTH ** -0.5),
        "f_conv_b": nrm(ks[10], (DEPTH, 2 * D_FF), 0.02),
        "f_w_out": nrm(ks[11], (DEPTH, D_FF, D_MODEL), D_FF ** -0.5),
        "kv_norm": gain(ks[12], (D_MODEL,)),
        "w_kv": nrm(ks[13], (D_MODEL, 2 * N_KV_HEADS * HEAD_DIM), D_MODEL ** -0.5),
        "k_norm": gain(ks[14], (HEAD_DIM,)),
        "b_norm": gain(ks[15], (N_B_LAYERS, D_MODEL)),
        "b_w_q": nrm(ks[16], (N_B_LAYERS, D_MODEL, qd), D_MODEL ** -0.5),
        "b_q_norm": gain(ks[17], (N_B_LAYERS, HEAD_DIM)),
        "b_sinks": nrm(ks[18], (N_B_LAYERS, N_Q_HEADS), 1.0),
        "b_w_o": nrm(ks[19], (N_B_LAYERS, qd, D_MODEL), qd ** -0.5),
    }


def _fwd_reference(x, a_norm, a_w_in, a_v_norm, a_w_s, a_b_s, a_w_out,
              f_norm, f_w_in, f_conv_w, f_conv_b, f_w_out,
              kv_norm, w_kv, k_norm,
              b_norm, b_w_q, b_q_norm, b_sinks, b_w_o):
    k_band = None
    v_band = None
    for layer in range(DEPTH):
        if layer < N_A_LAYERS:
            i = layer
            x = x + chunked_sgu(x, a_norm[i], a_w_in[i], a_v_norm[i], a_w_s[i], a_b_s[i], a_w_out[i])
        else:
            j = layer - N_A_LAYERS
            x = x + swa_sink_attention(x, k_band, v_band, b_norm[j], b_w_q[j], b_q_norm[j],
                                       b_sinks[j], b_w_o[j])
        x = x + conv_ffn(x, f_norm[layer], f_w_in[layer], f_conv_w[layer], f_conv_b[layer],
                         f_w_out[layer])
        if layer == N_A_LAYERS - 1:
            k_band, v_band = shared_kv(x, kv_norm, w_kv, k_norm)
    return x


import jax as _jax
import jax.numpy as _jnp

TWIN_FORMAT = 'train_step'
FWD_PARAMS = ['x', 'a_norm', 'a_w_in', 'a_v_norm', 'a_w_s', 'a_b_s', 'a_w_out', 'f_norm', 'f_w_in', 'f_conv_w', 'f_conv_b', 'f_w_out', 'kv_norm', 'w_kv', 'k_norm', 'b_norm', 'b_w_q', 'b_q_norm', 'b_sinks', 'b_w_o']
TWIN_WEIGHTS = ['a_norm', 'a_w_in', 'a_v_norm', 'a_w_s', 'a_b_s', 'a_w_out', 'f_norm', 'f_w_in', 'f_conv_w', 'f_conv_b', 'f_w_out', 'kv_norm', 'w_kv', 'k_norm', 'b_norm', 'b_w_q', 'b_q_norm', 'b_sinks', 'b_w_o']
TWIN_DIFF_INPUT = 'x'
TWIN_INPUTS = ['x', 'a_norm', 'a_w_in', 'a_v_norm', 'a_w_s', 'a_b_s', 'a_w_out', 'f_norm', 'f_w_in', 'f_conv_w', 'f_conv_b', 'f_w_out', 'kv_norm', 'w_kv', 'k_norm', 'b_norm', 'b_w_q', 'b_q_norm', 'b_sinks', 'b_w_o', 'loss_target', 'm_a_norm', 'm_a_w_in', 'm_a_v_norm', 'm_a_w_s', 'm_a_b_s', 'm_a_w_out', 'm_f_norm', 'm_f_w_in', 'm_f_conv_w', 'm_f_conv_b', 'm_f_w_out', 'm_kv_norm', 'm_w_kv', 'm_k_norm', 'm_b_norm', 'm_b_w_q', 'm_b_q_norm', 'm_b_sinks', 'm_b_w_o', 'v_a_norm', 'v_a_w_in', 'v_a_v_norm', 'v_a_w_s', 'v_a_b_s', 'v_a_w_out', 'v_f_norm', 'v_f_w_in', 'v_f_conv_w', 'v_f_conv_b', 'v_f_w_out', 'v_kv_norm', 'v_w_kv', 'v_k_norm', 'v_b_norm', 'v_b_w_q', 'v_b_q_norm', 'v_b_sinks', 'v_b_w_o']
TWIN_OUTPUTS = ['loss', 'grad_x', 'grad_a_norm', 'grad_a_w_in', 'grad_a_v_norm', 'grad_a_w_s', 'grad_a_b_s', 'grad_a_w_out', 'grad_f_norm', 'grad_f_w_in', 'grad_f_conv_w', 'grad_f_conv_b', 'grad_f_w_out', 'grad_kv_norm', 'grad_w_kv', 'grad_k_norm', 'grad_b_norm', 'grad_b_w_q', 'grad_b_q_norm', 'grad_b_sinks', 'grad_b_w_o', 'delta_a_norm', 'delta_a_w_in', 'delta_a_v_norm', 'delta_a_w_s', 'delta_a_b_s', 'delta_a_w_out', 'delta_f_norm', 'delta_f_w_in', 'delta_f_conv_w', 'delta_f_conv_b', 'delta_f_w_out', 'delta_kv_norm', 'delta_w_kv', 'delta_k_norm', 'delta_b_norm', 'delta_b_w_q', 'delta_b_q_norm', 'delta_b_sinks', 'delta_b_w_o', 'new_m_a_norm', 'new_m_a_w_in', 'new_m_a_v_norm', 'new_m_a_w_s', 'new_m_a_b_s', 'new_m_a_w_out', 'new_m_f_norm', 'new_m_f_w_in', 'new_m_f_conv_w', 'new_m_f_conv_b', 'new_m_f_w_out', 'new_m_kv_norm', 'new_m_w_kv', 'new_m_k_norm', 'new_m_b_norm', 'new_m_b_w_q', 'new_m_b_q_norm', 'new_m_b_sinks', 'new_m_b_w_o', 'new_v_a_norm', 'new_v_a_w_in', 'new_v_a_v_norm', 'new_v_a_w_s', 'new_v_a_b_s', 'new_v_a_w_out', 'new_v_f_norm', 'new_v_f_w_in', 'new_v_f_conv_w', 'new_v_f_conv_b', 'new_v_f_w_out', 'new_v_kv_norm', 'new_v_w_kv', 'new_v_k_norm', 'new_v_b_norm', 'new_v_b_w_q', 'new_v_b_q_norm', 'new_v_b_sinks', 'new_v_b_w_o']
TWIN_LEAF_KINDS = {'loss': 'loss', 'grad_x': 'grad_x', 'grad_a_norm': 'grad_w', 'grad_a_w_in': 'grad_w', 'grad_a_v_norm': 'grad_w', 'grad_a_w_s': 'grad_w', 'grad_a_b_s': 'grad_w', 'grad_a_w_out': 'grad_w', 'grad_f_norm': 'grad_w', 'grad_f_w_in': 'grad_w', 'grad_f_conv_w': 'grad_w', 'grad_f_conv_b': 'grad_w', 'grad_f_w_out': 'grad_w', 'grad_kv_norm': 'grad_w', 'grad_w_kv': 'grad_w', 'grad_k_norm': 'grad_w', 'grad_b_norm': 'grad_w', 'grad_b_w_q': 'grad_w', 'grad_b_q_norm': 'grad_w', 'grad_b_sinks': 'grad_w', 'grad_b_w_o': 'grad_w', 'delta_a_norm': 'delta_w', 'delta_a_w_in': 'delta_w', 'delta_a_v_norm': 'delta_w', 'delta_a_w_s': 'delta_w', 'delta_a_b_s': 'delta_w', 'delta_a_w_out': 'delta_w', 'delta_f_norm': 'delta_w', 'delta_f_w_in': 'delta_w', 'delta_f_conv_w': 'delta_w', 'delta_f_conv_b': 'delta_w', 'delta_f_w_out': 'delta_w', 'delta_kv_norm': 'delta_w', 'delta_w_kv': 'delta_w', 'delta_k_norm': 'delta_w', 'delta_b_norm': 'delta_w', 'delta_b_w_q': 'delta_w', 'delta_b_q_norm': 'delta_w', 'delta_b_sinks': 'delta_w', 'delta_b_w_o': 'delta_w', 'new_m_a_norm': 'new_m', 'new_m_a_w_in': 'new_m', 'new_m_a_v_norm': 'new_m', 'new_m_a_w_s': 'new_m', 'new_m_a_b_s': 'new_m', 'new_m_a_w_out': 'new_m', 'new_m_f_norm': 'new_m', 'new_m_f_w_in': 'new_m', 'new_m_f_conv_w': 'new_m', 'new_m_f_conv_b': 'new_m', 'new_m_f_w_out': 'new_m', 'new_m_kv_norm': 'new_m', 'new_m_w_kv': 'new_m', 'new_m_k_norm': 'new_m', 'new_m_b_norm': 'new_m', 'new_m_b_w_q': 'new_m', 'new_m_b_q_norm': 'new_m', 'new_m_b_sinks': 'new_m', 'new_m_b_w_o': 'new_m', 'new_v_a_norm': 'new_v', 'new_v_a_w_in': 'new_v', 'new_v_a_v_norm': 'new_v', 'new_v_a_w_s': 'new_v', 'new_v_a_b_s': 'new_v', 'new_v_a_w_out': 'new_v', 'new_v_f_norm': 'new_v', 'new_v_f_w_in': 'new_v', 'new_v_f_conv_w': 'new_v', 'new_v_f_conv_b': 'new_v', 'new_v_f_w_out': 'new_v', 'new_v_kv_norm': 'new_v', 'new_v_w_kv': 'new_v', 'new_v_k_norm': 'new_v', 'new_v_b_norm': 'new_v', 'new_v_b_w_q': 'new_v', 'new_v_b_q_norm': 'new_v', 'new_v_b_sinks': 'new_v', 'new_v_b_w_o': 'new_v'}


def _forward(args):
    return _fwd_reference(*[args[k] for k in FWD_PARAMS])


def _output_shape():
    def fwd():
        inp = _fwd_setup_inputs(0)
        return _fwd_reference(*[inp[k] for k in FWD_PARAMS])
    out = _jax.eval_shape(fwd)
    return out.shape, out.dtype

N_MICROBATCH = 1
ADAM_LR = 0.001
ADAM_B1 = 0.9
ADAM_B2 = 0.999
ADAM_EPS = 1e-08
ADAM_WD = 0.01
ADAM_STEP = 10
PER_EXAMPLE_BATCH_AXIS = {'x': 0, 'loss_target': 0}
SHARED_INPUTS = []
_WEIGHT_DTYPES = {'a_norm': _jnp.float32, 'a_w_in': _jnp.float32, 'a_v_norm': _jnp.float32, 'a_w_s': _jnp.float32, 'a_b_s': _jnp.float32, 'a_w_out': _jnp.float32, 'f_norm': _jnp.float32, 'f_w_in': _jnp.float32, 'f_conv_w': _jnp.float32, 'f_conv_b': _jnp.float32, 'f_w_out': _jnp.float32, 'kv_norm': _jnp.float32, 'w_kv': _jnp.float32, 'k_norm': _jnp.float32, 'b_norm': _jnp.float32, 'b_w_q': _jnp.float32, 'b_q_norm': _jnp.float32, 'b_sinks': _jnp.float32, 'b_w_o': _jnp.float32}
MOMENT_SCALE = {'a_norm': 2.610177e+01, 'a_w_in': 7.593213e-01, 'a_v_norm': 7.132930e+00, 'a_w_s': 4.906190e+00, 'a_b_s': 1.569850e+01, 'a_w_out': 6.980365e+00, 'f_norm': 2.620109e+01, 'f_w_in': 1.424032e+00, 'f_conv_w': 4.188754e+00, 'f_conv_b': 4.282255e+00, 'f_w_out': 1.147220e+00, 'kv_norm': 5.723793e+00, 'w_kv': 5.726271e+00, 'k_norm': 2.013821e+01, 'b_norm': 2.900735e-01, 'b_w_q': 2.888369e-01, 'b_q_norm': 2.042196e+01, 'b_sinks': 4.170547e+01, 'b_w_o': 3.160540e+00}


def _to_microbatches(a, axis):
    t = _jnp.moveaxis(a, axis, 0)
    t = t.reshape((N_MICROBATCH, t.shape[0] // N_MICROBATCH) + t.shape[1:])
    return _jnp.moveaxis(t, 1, axis + 1)


def setup_inputs(seed: int = 0) -> dict:
    inp = _fwd_setup_inputs(seed)
    key = _jax.random.fold_in(_jax.random.key(seed), 7919)
    shape, _ = _output_shape()
    out = dict(inp)
    out["loss_target"] = _jax.random.normal(_jax.random.fold_in(key, 0), shape, _jnp.float32)
    for i, name in enumerate(TWIN_WEIGHTS):
        w = inp[name].astype(_jnp.float32)
        if MOMENT_SCALE is None:
            s = _jnp.sqrt(_jnp.mean(_jnp.square(w)) + 1e-30)
        else:
            s = MOMENT_SCALE[name]
        km, kv = _jax.random.split(_jax.random.fold_in(key, i + 1))
        out[name] = w
        out["m_" + name] = s * _jax.random.normal(km, w.shape, _jnp.float32)
        out["v_" + name] = (s * s) * _jax.random.uniform(kv, w.shape, _jnp.float32, 0.5, 1.5)
    if N_MICROBATCH > 1:
        for name, axis in PER_EXAMPLE_BATCH_AXIS.items():
            out[name] = _to_microbatches(out[name], axis)
    return {'x': out['x'], 'a_norm': out['a_norm'], 'a_w_in': out['a_w_in'], 'a_v_norm': out['a_v_norm'], 'a_w_s': out['a_w_s'], 'a_b_s': out['a_b_s'], 'a_w_out': out['a_w_out'], 'f_norm': out['f_norm'], 'f_w_in': out['f_w_in'], 'f_conv_w': out['f_conv_w'], 'f_conv_b': out['f_conv_b'], 'f_w_out': out['f_w_out'], 'kv_norm': out['kv_norm'], 'w_kv': out['w_kv'], 'k_norm': out['k_norm'], 'b_norm': out['b_norm'], 'b_w_q': out['b_w_q'], 'b_q_norm': out['b_q_norm'], 'b_sinks': out['b_sinks'], 'b_w_o': out['b_w_o'], 'loss_target': out['loss_target'], 'm_a_norm': out['m_a_norm'], 'm_a_w_in': out['m_a_w_in'], 'm_a_v_norm': out['m_a_v_norm'], 'm_a_w_s': out['m_a_w_s'], 'm_a_b_s': out['m_a_b_s'], 'm_a_w_out': out['m_a_w_out'], 'm_f_norm': out['m_f_norm'], 'm_f_w_in': out['m_f_w_in'], 'm_f_conv_w': out['m_f_conv_w'], 'm_f_conv_b': out['m_f_conv_b'], 'm_f_w_out': out['m_f_w_out'], 'm_kv_norm': out['m_kv_norm'], 'm_w_kv': out['m_w_kv'], 'm_k_norm': out['m_k_norm'], 'm_b_norm': out['m_b_norm'], 'm_b_w_q': out['m_b_w_q'], 'm_b_q_norm': out['m_b_q_norm'], 'm_b_sinks': out['m_b_sinks'], 'm_b_w_o': out['m_b_w_o'], 'v_a_norm': out['v_a_norm'], 'v_a_w_in': out['v_a_w_in'], 'v_a_v_norm': out['v_a_v_norm'], 'v_a_w_s': out['v_a_w_s'], 'v_a_b_s': out['v_a_b_s'], 'v_a_w_out': out['v_a_w_out'], 'v_f_norm': out['v_f_norm'], 'v_f_w_in': out['v_f_w_in'], 'v_f_conv_w': out['v_f_conv_w'], 'v_f_conv_b': out['v_f_conv_b'], 'v_f_w_out': out['v_f_w_out'], 'v_kv_norm': out['v_kv_norm'], 'v_w_kv': out['v_w_kv'], 'v_k_norm': out['v_k_norm'], 'v_b_norm': out['v_b_norm'], 'v_b_w_q': out['v_b_w_q'], 'v_b_q_norm': out['v_b_q_norm'], 'v_b_sinks': out['v_b_sinks'], 'v_b_w_o': out['v_b_w_o']}


def _loss(weights, diff, rest, loss_target):
    with _jax.named_scope("forward"):
        args = {**rest, TWIN_DIFF_INPUT: diff, **{k: w.astype(_WEIGHT_DTYPES[k]) for k, w in weights.items()}}
        y = _forward(args)
    with _jax.named_scope("loss_head"):
        err = _jnp.square(y.astype(_jnp.float32) - loss_target)
        return 0.5 * _jnp.sum(_jnp.mean(err, axis=-1)) if err.ndim else 0.5 * err


def _adamw(w, g, m, v):
    m = ADAM_B1 * m + (1.0 - ADAM_B1) * g
    v = ADAM_B2 * v + (1.0 - ADAM_B2) * _jnp.square(g)
    m_hat = m / (1.0 - ADAM_B1 ** ADAM_STEP)
    v_hat = v / (1.0 - ADAM_B2 ** ADAM_STEP)
    delta = -ADAM_LR * (m_hat / (_jnp.sqrt(v_hat) + ADAM_EPS) + ADAM_WD * w)
    return delta, m, v


def reference(x, a_norm, a_w_in, a_v_norm, a_w_s, a_b_s, a_w_out, f_norm, f_w_in, f_conv_w, f_conv_b, f_w_out, kv_norm, w_kv, k_norm, b_norm, b_w_q, b_q_norm, b_sinks, b_w_o, loss_target, m_a_norm, m_a_w_in, m_a_v_norm, m_a_w_s, m_a_b_s, m_a_w_out, m_f_norm, m_f_w_in, m_f_conv_w, m_f_conv_b, m_f_w_out, m_kv_norm, m_w_kv, m_k_norm, m_b_norm, m_b_w_q, m_b_q_norm, m_b_sinks, m_b_w_o, v_a_norm, v_a_w_in, v_a_v_norm, v_a_w_s, v_a_b_s, v_a_w_out, v_f_norm, v_f_w_in, v_f_conv_w, v_f_conv_b, v_f_w_out, v_kv_norm, v_w_kv, v_k_norm, v_b_norm, v_b_w_q, v_b_q_norm, v_b_sinks, v_b_w_o):
    given = dict(x=x, a_norm=a_norm, a_w_in=a_w_in, a_v_norm=a_v_norm, a_w_s=a_w_s, a_b_s=a_b_s, a_w_out=a_w_out, f_norm=f_norm, f_w_in=f_w_in, f_conv_w=f_conv_w, f_conv_b=f_conv_b, f_w_out=f_w_out, kv_norm=kv_norm, w_kv=w_kv, k_norm=k_norm, b_norm=b_norm, b_w_q=b_w_q, b_q_norm=b_q_norm, b_sinks=b_sinks, b_w_o=b_w_o, loss_target=loss_target, m_a_norm=m_a_norm, m_a_w_in=m_a_w_in, m_a_v_norm=m_a_v_norm, m_a_w_s=m_a_w_s, m_a_b_s=m_a_b_s, m_a_w_out=m_a_w_out, m_f_norm=m_f_norm, m_f_w_in=m_f_w_in, m_f_conv_w=m_f_conv_w, m_f_conv_b=m_f_conv_b, m_f_w_out=m_f_w_out, m_kv_norm=m_kv_norm, m_w_kv=m_w_kv, m_k_norm=m_k_norm, m_b_norm=m_b_norm, m_b_w_q=m_b_w_q, m_b_q_norm=m_b_q_norm, m_b_sinks=m_b_sinks, m_b_w_o=m_b_w_o, v_a_norm=v_a_norm, v_a_w_in=v_a_w_in, v_a_v_norm=v_a_v_norm, v_a_w_s=v_a_w_s, v_a_b_s=v_a_b_s, v_a_w_out=v_a_w_out, v_f_norm=v_f_norm, v_f_w_in=v_f_w_in, v_f_conv_w=v_f_conv_w, v_f_conv_b=v_f_conv_b, v_f_w_out=v_f_w_out, v_kv_norm=v_kv_norm, v_w_kv=v_w_kv, v_k_norm=v_k_norm, v_b_norm=v_b_norm, v_b_w_q=v_b_w_q, v_b_q_norm=v_b_q_norm, v_b_sinks=v_b_sinks, v_b_w_o=v_b_w_o)
    weights = {n: given[n] for n in TWIN_WEIGHTS}
    shared = {n: given[n] for n in SHARED_INPUTS}
    per_example = {n: given[n] for n in ['x']}
    grad_fn = _jax.value_and_grad(_loss, argnums=(0, 1))

    def one_microbatch(ex, loss_target):
        ex = dict(ex)
        diff = ex.pop(TWIN_DIFF_INPUT)
        return grad_fn(weights, diff, {**shared, **ex}, loss_target)

    if N_MICROBATCH == 1:
        loss, (grad_w, grad_x) = one_microbatch(per_example, given["loss_target"])
    else:
        def body(carry, xs):
            loss_sum, grad_sum = carry
            l_k, (gw_k, gx_k) = one_microbatch(xs[0], xs[1])
            with _jax.named_scope("update"):
                return (loss_sum + l_k, _jax.tree.map(_jnp.add, grad_sum, gw_k)), gx_k

        init = (_jnp.zeros((), _jnp.float32), _jax.tree.map(_jnp.zeros_like, weights))
        (loss, grad_w), grad_x = _jax.lax.scan(body, init, (per_example, given["loss_target"]))
    with _jax.named_scope("update"):
        delta_w, new_m, new_v = {}, {}, {}
        for n in TWIN_WEIGHTS:
            delta_w[n], new_m[n], new_v[n] = _adamw(weights[n], grad_w[n], given["m_" + n], given["v_" + n])
    return (loss, grad_x, *[grad_w[n] for n in TWIN_WEIGHTS], *[delta_w[n] for n in TWIN_WEIGHTS],
            *[new_m[n] for n in TWIN_WEIGHTS], *[new_v[n] for n in TWIN_WEIGHTS])
```

```python
import functools
import math

import jax
import jax.numpy as jnp
from jax import lax
from jax.experimental import pallas as pl
from jax.experimental.pallas import tpu as pltpu

f32 = jnp.float32
bf16 = jnp.bfloat16
MESH = pl.DeviceIdType.MESH
ANY = pl.BlockSpec(memory_space=pl.ANY)

EPS = 1e-6
LANES = 128
CHUNK = 128
HEAD_DIM = 64
N_Q_HEADS = 16
N_KV_HEADS = 4
Q_PER_KV = N_Q_HEADS // N_KV_HEADS
N_SHARDS = 4
N_DEV = 8

ADAM_LR = 0.001
ADAM_B1 = 0.9
ADAM_B2 = 0.999
ADAM_EPS = 1e-08
ADAM_WD = 0.01
ADAM_STEP = 10
ADAM_C1 = 1.0 - ADAM_B1 ** ADAM_STEP
ADAM_C2 = 1.0 - ADAM_B2 ** ADAM_STEP

_INV_SQRT2 = 1.0 / math.sqrt(2.0)
_INV_SQRT2PI = 1.0 / math.sqrt(2.0 * math.pi)


def _params(*sem):
    return pltpu.CompilerParams(dimension_semantics=sem)


def _gelu(z):
    return 0.5 * z * (1.0 + lax.erf(z * _INV_SQRT2))


def _gelu_grad(z):
    return 0.5 * (1.0 + lax.erf(z * _INV_SQRT2)) + z * jnp.exp(-0.5 * z * z) * _INV_SQRT2PI


def _dot(a, b):
    return jnp.dot(a, b, preferred_element_type=f32)


def _dot_nt(a, b):
    return lax.dot_general(a, b, (((1,), (1,)), ((), ())), preferred_element_type=f32)


def _dot_tn(a, b):
    return lax.dot_general(a, b, (((0,), (0,)), ((), ())), preferred_element_type=f32)


def _dot_exact(a, b):
    return jnp.dot(a, b, preferred_element_type=f32, precision=lax.Precision.HIGHEST)


def mm_nn(a, w3, *, name, s0=0, ns=None, tm=512, add=None, out_dtype=f32):
    m, k = a.shape
    s_all, _, n_s = w3.shape
    ns = s_all if ns is None else ns
    tm = min(tm, m)

    def body(*refs):
        if add is None:
            a_ref, w_ref, o_ref = refs
            acc = _dot(a_ref[...].astype(bf16), w_ref[0])
        else:
            a_ref, w_ref, add_ref, o_ref = refs
            acc = _dot(a_ref[...].astype(bf16), w_ref[0]) + add_ref[...]
        o_ref[...] = acc.astype(out_dtype)

    in_specs = [pl.BlockSpec((tm, k), lambda j, i: (i, 0)),
                pl.BlockSpec((1, k, n_s), lambda j, i: (s0 + j, 0, 0))]
    args = [a, w3]
    if add is not None:
        in_specs.append(pl.BlockSpec((tm, n_s), lambda j, i: (i, j)))
        args.append(add)
    return pl.pallas_call(
        body, name=name, grid=(ns, m // tm), in_specs=in_specs,
        out_specs=pl.BlockSpec((tm, n_s), lambda j, i: (i, j)),
        out_shape=jax.ShapeDtypeStruct((m, ns * n_s), out_dtype),
        compiler_params=_params("parallel", "parallel"))(*args)


def mm_nt(a_list, w3, *, name, tm=512, tko=None, add=None, out_dtype=f32):
    s_all, k_out, n_s = w3.shape
    m = a_list[0].shape[0]
    na = len(a_list)
    spa = s_all // na
    tm = min(tm, m)
    tko = k_out if tko is None else tko

    def body(*refs):
        a_refs = refs[:na]
        w_ref = refs[na]
        o_ref = refs[-1]
        s = pl.program_id(2)

        @pl.when(s == 0)
        def _():
            if add is None:
                o_ref[...] = jnp.zeros_like(o_ref)
            else:
                o_ref[...] = refs[na + 1][...]

        for idx in range(na):
            @pl.when(s // spa == idx)
            def _(idx=idx):
                o_ref[...] += _dot_nt(a_refs[idx][...].astype(bf16), w_ref[0])

    def a_map(idx):
        return lambda ko, i, s: (i, jnp.clip(s - idx * spa, 0, spa - 1))

    in_specs = [pl.BlockSpec((tm, n_s), a_map(idx)) for idx in range(na)]
    in_specs.append(pl.BlockSpec((1, tko, n_s), lambda ko, i, s: (s, ko, 0)))
    args = list(a_list) + [w3]
    if add is not None:
        in_specs.append(pl.BlockSpec((tm, tko), lambda ko, i, s: (i, ko)))
        args.append(add)
    return pl.pallas_call(
        body, name=name, grid=(k_out // tko, m // tm, s_all), in_specs=in_specs,
        out_specs=pl.BlockSpec((tm, tko), lambda ko, i, s: (i, ko)),
        out_shape=jax.ShapeDtypeStruct((m, k_out), out_dtype),
        compiler_params=_params("parallel", "parallel", "arbitrary"))(*args)


def mm_tn(a, b_list, *, name, n_s, tm=512, tki=None):
    m, k_in = a.shape
    na = len(b_list)
    s_all = sum(b.shape[1] for b in b_list) // n_s
    spa = s_all // na
    tm = min(tm, m)
    tki = k_in if tki is None else tki

    def body(*refs):
        a_ref = refs[0]
        b_refs = refs[1:1 + na]
        o_ref = refs[-1]
        s = pl.program_id(0)
        r = pl.program_id(2)

        @pl.when(r == 0)
        def _():
            o_ref[...] = jnp.zeros_like(o_ref)

        for idx in range(na):
            @pl.when(s // spa == idx)
            def _(idx=idx):
                o_ref[0] += _dot_tn(a_ref[...].astype(bf16), b_refs[idx][...].astype(bf16))

    def b_map(idx):
        def index(s, ki, r):
            active = (s // spa) == idx
            return (jnp.where(active, r, 0), jnp.clip(s - idx * spa, 0, spa - 1))
        return index

    in_specs = [pl.BlockSpec((tm, tki), lambda s, ki, r: (r, ki))]
    in_specs += [pl.BlockSpec((tm, n_s), b_map(idx)) for idx in range(na)]
    return pl.pallas_call(
        body, name=name, grid=(s_all, k_in // tki, m // tm), in_specs=in_specs,
        out_specs=pl.BlockSpec((1, tki, n_s), lambda s, ki, r: (s, ki, 0)),
        out_shape=jax.ShapeDtypeStruct((s_all, k_in, n_s), f32),
        compiler_params=_params("parallel", "parallel", "arbitrary"))(a, *b_list)


def rms_fwd(x, gains, *, name, tr=512):
    t, d = x.shape
    tr = min(tr, t)
    ng = len(gains)

    def body(*refs):
        x_ref = refs[0]
        g_refs = refs[1:1 + ng]
        h_refs = refs[1 + ng:1 + 2 * ng]
        r_ref = refs[-1]
        xv = x_ref[...]
        r = lax.rsqrt(jnp.mean(xv * xv, axis=1, keepdims=True) + EPS)
        xh = xv * r
        for g_ref, h_ref in zip(g_refs, h_refs):
            h_ref[...] = (xh * g_ref[...]).astype(bf16)
        r_ref[...] = r

    row = pl.BlockSpec((tr, d), lambda i: (i, 0))
    vec = pl.BlockSpec((1, d), lambda i: (0, 0))
    outs = pl.pallas_call(
        body, name=name, grid=(t // tr,), in_specs=[row] + [vec] * ng,
        out_specs=[row] * ng + [pl.BlockSpec((tr, 1), lambda i: (i, 0))],
        out_shape=[jax.ShapeDtypeStruct((t, d), bf16)] * ng + [jax.ShapeDtypeStruct((t, 1), f32)],
        compiler_params=_params("parallel"))(x, *gains)
    return outs[:ng], outs[ng]


def rms_bwd(dh_list, x, r, gains, dx_in, *, name, tr=512):
    t, d = x.shape
    tr = min(tr, t)
    ng = len(gains)

    def body(*refs):
        dh_refs = refs[:ng]
        x_ref, r_ref = refs[ng], refs[ng + 1]
        g_refs = refs[ng + 2:2 * ng + 2]
        dxin_ref = refs[2 * ng + 2]
        dx_ref = refs[2 * ng + 3]
        dg_refs = refs[2 * ng + 4:]
        i = pl.program_id(0)
        rv = r_ref[...]
        xh = x_ref[...] * rv
        acc = dxin_ref[...]
        for dh_ref, g_ref, dg_ref in zip(dh_refs, g_refs, dg_refs):
            dh = dh_ref[...]
            part = jnp.sum(dh * xh, axis=0, keepdims=True)

            @pl.when(i == 0)
            def _(dg_ref=dg_ref, part=part):
                dg_ref[...] = part

            @pl.when(i > 0)
            def _(dg_ref=dg_ref, part=part):
                dg_ref[...] += part

            tg = dh * g_ref[...]
            acc = acc + rv * (tg - xh * jnp.mean(tg * xh, axis=1, keepdims=True))
        dx_ref[...] = acc

    row = pl.BlockSpec((tr, d), lambda i: (i, 0))
    vec = pl.BlockSpec((1, d), lambda i: (0, 0))
    outs = pl.pallas_call(
        body, name=name, grid=(t // tr,),
        in_specs=[row] * ng + [row, pl.BlockSpec((tr, 1), lambda i: (i, 0))] + [vec] * ng + [row],
        out_specs=[row] + [vec] * ng,
        out_shape=[jax.ShapeDtypeStruct((t, d), f32)] + [jax.ShapeDtypeStruct((1, d), f32)] * ng,
        compiler_params=_params("arbitrary"))(*dh_list, x, r, *gains, dx_in)
    return outs[0], outs[1:]


def sgu_gate_fwd(zu, zv, gv, wc, bt, *, name, tr=512):
    t, w = zu.shape
    tr = min(tr, t)
    groups = w // LANES

    def body(zu_ref, zv_ref, gv_ref, wc_ref, bt_ref, y_ref):
        vp = _gelu(zv_ref[...])
        rv = lax.rsqrt(jnp.mean(vp * vp, axis=1, keepdims=True) + EPS)
        vb = (vp * rv * gv_ref[...]).astype(bf16)
        for c in range(tr // CHUNK):
            rows = slice(c * CHUNK, (c + 1) * CHUNK)
            for g in range(groups):
                cols = slice(g * LANES, (g + 1) * LANES)
                sv = _dot(wc_ref[g], vb[rows, cols]) + bt_ref[:, g:g + 1]
                y_ref[rows, cols] = (_gelu(zu_ref[rows, cols]) * sv).astype(bf16)

    row = pl.BlockSpec((tr, w), lambda i: (i, 0))
    return pl.pallas_call(
        body, name=name, grid=(t // tr,),
        in_specs=[row, row, pl.BlockSpec((1, w), lambda i: (0, 0)),
                  pl.BlockSpec((groups, CHUNK, CHUNK), lambda i: (0, 0, 0)),
                  pl.BlockSpec((CHUNK, groups), lambda i: (0, 0))],
        out_specs=row, out_shape=jax.ShapeDtypeStruct((t, w), bf16),
        compiler_params=_params("parallel"))(zu, zv, gv, wc, bt)


def sgu_gate_bwd(zu, zv, dy, gv, wc, bt, *, name, tr=512):
    t, w = zu.shape
    tr = min(tr, t)
    groups = w // LANES
    nsteps = t // tr

    def body(zu_ref, zv_ref, dy_ref, gv_ref, wc_ref, bt_ref,
             dzu_ref, dzv_ref, dgv_ref, dws_ref, dbt_ref, dv_ref, bacc_ref):
        i = pl.program_id(0)

        @pl.when(i == 0)
        def _():
            dgv_ref[...] = jnp.zeros_like(dgv_ref)
            dws_ref[...] = jnp.zeros_like(dws_ref)
            bacc_ref[...] = jnp.zeros_like(bacc_ref)

        zvv = zv_ref[...]
        vp = _gelu(zvv)
        rv = lax.rsqrt(jnp.mean(vp * vp, axis=1, keepdims=True) + EPS)
        vhat = vp * rv
        vb = (vhat * gv_ref[...]).astype(bf16)
        for c in range(tr // CHUNK):
            rows = slice(c * CHUNK, (c + 1) * CHUNK)
            for g in range(groups):
                cols = slice(g * LANES, (g + 1) * LANES)
                vblk = vb[rows, cols]
                sv = _dot(wc_ref[g], vblk) + bt_ref[:, g:g + 1]
                zub = zu_ref[rows, cols]
                dyb = dy_ref[rows, cols]
                dzu_ref[rows, cols] = (dyb * sv * _gelu_grad(zub)).astype(bf16)
                dsv = dyb * _gelu(zub)
                bacc_ref[:, cols] += dsv
                dsvb = dsv.astype(bf16)
                dv_ref[rows, cols] = _dot_tn(wc_ref[g], dsvb)
                dws_ref[g] += _dot_nt(dsvb, vblk)
        dv = dv_ref[...]
        dgv_ref[...] += jnp.sum(dv * vhat, axis=0, keepdims=True)
        tg = dv * gv_ref[...]
        dvp = rv * (tg - vhat * jnp.mean(tg * vhat, axis=1, keepdims=True))
        dzv_ref[...] = (dvp * _gelu_grad(zvv)).astype(bf16)

        @pl.when(i == nsteps - 1)
        def _():
            tt = lax.broadcasted_iota(jnp.int32, (CHUNK, CHUNK), 0)
            ss = lax.broadcasted_iota(jnp.int32, (CHUNK, CHUNK), 1)
            for g in range(groups):
                dws_ref[g] = jnp.where(ss <= tt, dws_ref[g], 0.0)
                dbt_ref[:, g:g + 1] = jnp.sum(bacc_ref[:, g * LANES:(g + 1) * LANES], axis=1, keepdims=True)

    row = pl.BlockSpec((tr, w), lambda i: (i, 0))
    full3 = pl.BlockSpec((groups, CHUNK, CHUNK), lambda i: (0, 0, 0))
    return pl.pallas_call(
        body, name=name, grid=(nsteps,),
        in_specs=[row, row, row, pl.BlockSpec((1, w), lambda i: (0, 0)), full3,
                  pl.BlockSpec((CHUNK, groups), lambda i: (0, 0))],
        out_specs=[row, row, pl.BlockSpec((1, w), lambda i: (0, 0)), full3,
                   pl.BlockSpec((CHUNK, groups), lambda i: (0, 0))],
        out_shape=[jax.ShapeDtypeStruct((t, w), bf16), jax.ShapeDtypeStruct((t, w), bf16),
                   jax.ShapeDtypeStruct((1, w), f32), jax.ShapeDtypeStruct((groups, CHUNK, CHUNK), f32),
                   jax.ShapeDtypeStruct((CHUNK, groups), f32)],
        scratch_shapes=[pltpu.VMEM((tr, w), f32), pltpu.VMEM((CHUNK, w), f32)],
        compiler_params=_params("arbitrary"))(zu, zv, dy, gv, wc, bt)


HALO = 8


def _shift_down(v, halo, k, first):
    r = pltpu.roll(v, k, 0)
    hh = jnp.where(first, 0.0, pltpu.roll(halo, k, 0))
    rid = lax.broadcasted_iota(jnp.int32, (HALO, v.shape[1]), 0)
    head = jnp.where(rid < k, hh, r[0:HALO])
    return jnp.concatenate([head, r[HALO:]], axis=0)


def _shift_up(v, halo, k, last):
    n = v.shape[0]
    r = pltpu.roll(v, n - k, 0)
    hh = jnp.where(last, 0.0, pltpu.roll(halo, HALO - k, 0))
    rid = lax.broadcasted_iota(jnp.int32, (HALO, v.shape[1]), 0)
    tail = jnp.where(rid >= HALO - k, hh, r[n - HALO:])
    return jnp.concatenate([r[:n - HALO], tail], axis=0)


def _conv(p, halo, w_ref, b_ref, first):
    return (w_ref[2:3, :] * p + w_ref[1:2, :] * _shift_down(p, halo, 1, first)
            + w_ref[0:1, :] * _shift_down(p, halo, 2, first) + b_ref[...])


def _conv_specs(t, tr, tc):
    tile = pl.BlockSpec((tr, tc), lambda j, i: (i, j))
    prev = pl.BlockSpec((HALO, tc), lambda j, i: (jnp.maximum(i * (tr // HALO) - 1, 0), j))
    nxt = pl.BlockSpec((HALO, tc), lambda j, i: (jnp.minimum((i + 1) * (tr // HALO), t // HALO - 1), j))
    wspec = pl.BlockSpec((3, tc), lambda j, i: (0, j))
    bspec = pl.BlockSpec((1, tc), lambda j, i: (0, j))
    return tile, prev, nxt, wspec, bspec


def ffn_gate_fwd(pg, pu, wg, wu, bg, bu, *, name, tr=1024, tc=256):
    t, f = pg.shape
    tr = min(tr, t)
    tile, prev, _, wspec, bspec = _conv_specs(t, tr, tc)

    def body(pg_ref, pgh_ref, pu_ref, puh_ref, wg_ref, wu_ref, bg_ref, bu_ref, a_ref):
        first = pl.program_id(1) == 0
        gate = _conv(pg_ref[...], pgh_ref[...], wg_ref, bg_ref, first)
        up = _conv(pu_ref[...], puh_ref[...], wu_ref, bu_ref, first)
        a_ref[...] = (gate * jax.nn.sigmoid(gate) * up).astype(bf16)

    return pl.pallas_call(
        body, name=name, grid=(f // tc, t // tr),
        in_specs=[tile, prev, tile, prev, wspec, wspec, bspec, bspec],
        out_specs=tile, out_shape=jax.ShapeDtypeStruct((t, f), bf16),
        compiler_params=_params("parallel", "parallel"))(pg, pg, pu, pu, wg, wu, bg, bu)


def ffn_gate_bwd(pg, pu, da, wg, wu, bg, bu, *, name, tr=1024, tc=256):
    t, f = pg.shape
    tr = min(tr, t)
    tile, prev, _, wspec, bspec = _conv_specs(t, tr, tc)

    def body(pg_ref, pgh_ref, pu_ref, puh_ref, da_ref, wg_ref, wu_ref, bg_ref, bu_ref,
             dg_ref, du_ref, sg_ref, su_ref):
        i = pl.program_id(1)
        first = i == 0
        pgv, puv = pg_ref[...], pu_ref[...]
        pg1, pg2 = _shift_down(pgv, pgh_ref[...], 1, first), _shift_down(pgv, pgh_ref[...], 2, first)
        pu1, pu2 = _shift_down(puv, puh_ref[...], 1, first), _shift_down(puv, puh_ref[...], 2, first)
        gate = wg_ref[2:3, :] * pgv + wg_ref[1:2, :] * pg1 + wg_ref[0:1, :] * pg2 + bg_ref[...]
        up = wu_ref[2:3, :] * puv + wu_ref[1:2, :] * pu1 + wu_ref[0:1, :] * pu2 + bu_ref[...]
        sg = jax.nn.sigmoid(gate)
        dav = da_ref[...]
        dgate = dav * up * (sg * (1.0 + gate * (1.0 - sg)))
        dup = dav * gate * sg
        dg_ref[...] = dgate
        du_ref[...] = dup
        rid = lax.broadcasted_iota(jnp.int32, (8, tc), 0)
        for d, p0, p1, p2, s_ref in ((dgate, pgv, pg1, pg2, sg_ref), (dup, puv, pu1, pu2, su_ref)):
            sums = [jnp.sum(d * p2, axis=0, keepdims=True), jnp.sum(d * p1, axis=0, keepdims=True),
                    jnp.sum(d * p0, axis=0, keepdims=True), jnp.sum(d, axis=0, keepdims=True)]
            part = jnp.zeros((8, tc), f32)
            for k, sk in enumerate(sums):
                part = jnp.where(rid == k, sk, part)

            @pl.when(first)
            def _(s_ref=s_ref, part=part):
                s_ref[...] = part

            @pl.when(i > 0)
            def _(s_ref=s_ref, part=part):
                s_ref[...] += part

    stat = pl.BlockSpec((8, tc), lambda j, i: (0, j))
    return pl.pallas_call(
        body, name=name, grid=(f // tc, t // tr),
        in_specs=[tile, prev, tile, prev, tile, wspec, wspec, bspec, bspec],
        out_specs=[tile, tile, stat, stat],
        out_shape=[jax.ShapeDtypeStruct((t, f), f32), jax.ShapeDtypeStruct((t, f), f32),
                   jax.ShapeDtypeStruct((8, f), f32), jax.ShapeDtypeStruct((8, f), f32)],
        compiler_params=_params("parallel", "arbitrary"))(pg, pg, pu, pu, da, wg, wu, bg, bu)


def conv_input_grad(dhu, w, *, name, tr=1024, tc=256):
    t, f = dhu.shape
    tr = min(tr, t)
    tile, _, nxt, wspec, _ = _conv_specs(t, tr, tc)
    nsteps = t // tr

    def body(d_ref, dh_ref, w_ref, o_ref):
        last = pl.program_id(1) == nsteps - 1
        d = d_ref[...]
        hv = dh_ref[...]
        o_ref[...] = (w_ref[2:3, :] * d + w_ref[1:2, :] * _shift_up(d, hv, 1, last)
                      + w_ref[0:1, :] * _shift_up(d, hv, 2, last)).astype(bf16)

    return pl.pallas_call(
        body, name=name, grid=(f // tc, nsteps), in_specs=[tile, nxt, wspec],
        out_specs=tile, out_shape=jax.ShapeDtypeStruct((t, f), bf16),
        compiler_params=_params("parallel", "parallel"))(dhu, dhu, w)


def _head_mean_matrix():
    i = lax.broadcasted_iota(jnp.int32, (LANES, LANES), 0) // HEAD_DIM
    j = lax.broadcasted_iota(jnp.int32, (LANES, LANES), 1) // HEAD_DIM
    return jnp.where(i == j, 1.0 / HEAD_DIM, 0.0).astype(f32)


def _lane_half(shape):
    return (lax.broadcasted_iota(jnp.int32, shape, 1) % LANES) // HEAD_DIM


def q_norm_fwd(qp, g2, *, name, scale, tr=512):
    t, w = qp.shape
    tr = min(tr, t)

    def body(x_ref, g_ref, o_ref):
        bd = _head_mean_matrix()
        for cb in range(w // LANES):
            cols = slice(cb * LANES, (cb + 1) * LANES)
            xc = x_ref[:, cols]
            rh = lax.rsqrt(_dot_exact(xc * xc, bd) + EPS)
            o_ref[:, cols] = (xc * rh * g_ref[...] * scale).astype(bf16)

    row = pl.BlockSpec((tr, w), lambda i: (i, 0))
    return pl.pallas_call(
        body, name=name, grid=(t // tr,), in_specs=[row, pl.BlockSpec((1, LANES), lambda i: (0, 0))],
        out_specs=row, out_shape=jax.ShapeDtypeStruct((t, w), bf16),
        compiler_params=_params("parallel"))(qp, g2)


def q_norm_bwd(dq, qp, g2, *, name, scale, tr=512):
    t, w = qp.shape
    tr = min(tr, t)

    def body(dq_ref, x_ref, g_ref, o_ref, dg_ref):
        i = pl.program_id(0)
        bd = _head_mean_matrix()
        acc = jnp.zeros((1, LANES), f32)
        for cb in range(w // LANES):
            cols = slice(cb * LANES, (cb + 1) * LANES)
            xc = x_ref[:, cols]
            rh = lax.rsqrt(_dot_exact(xc * xc, bd) + EPS)
            xh = xc * rh
            dy = dq_ref[:, cols] * scale
            acc = acc + jnp.sum(dy * xh, axis=0, keepdims=True)
            tg = dy * g_ref[...]
            o_ref[:, cols] = (rh * (tg - xh * _dot_exact(tg * xh, bd))).astype(bf16)

        @pl.when(i == 0)
        def _():
            dg_ref[...] = acc

        @pl.when(i > 0)
        def _():
            dg_ref[...] += acc

    row = pl.BlockSpec((tr, w), lambda i: (i, 0))
    vec = pl.BlockSpec((1, LANES), lambda i: (0, 0))
    return pl.pallas_call(
        body, name=name, grid=(t // tr,), in_specs=[row, row, vec], out_specs=[row, vec],
        out_shape=[jax.ShapeDtypeStruct((t, w), bf16), jax.ShapeDtypeStruct((1, LANES), f32)],
        compiler_params=_params("arbitrary"))(dq, qp, g2)


def kv_post_fwd(kv, g2, *, name, tr=512):
    t, w = kv.shape
    tr = min(tr, t)
    kw = w // 2

    def body(x_ref, g_ref, k_ref, v_ref):
        bd = _head_mean_matrix()
        half = _lane_half((tr, LANES))
        for cb in range(kw // LANES):
            xc = x_ref[:, cb * LANES:(cb + 1) * LANES]
            rh = lax.rsqrt(_dot_exact(xc * xc, bd) + EPS)
            kn = xc * rh * g_ref[...]
            vc = x_ref[:, kw + cb * LANES:kw + (cb + 1) * LANES]
            for src, dst in ((kn, k_ref), (vc, v_ref)):
                sw = pltpu.roll(src, HEAD_DIM, 1)
                for hf in range(2):
                    blk = 2 * cb + hf
                    dst[:, blk * LANES:(blk + 1) * LANES] = jnp.where(half == hf, src, sw).astype(bf16)

    return pl.pallas_call(
        body, name=name, grid=(t // tr,),
        in_specs=[pl.BlockSpec((tr, w), lambda i: (i, 0)), pl.BlockSpec((1, LANES), lambda i: (0, 0))],
        out_specs=[pl.BlockSpec((tr, 2 * kw), lambda i: (i, 0))] * 2,
        out_shape=[jax.ShapeDtypeStruct((t, 2 * kw), bf16)] * 2,
        compiler_params=_params("parallel"))(kv, g2)


def kv_post_bwd(dk2, dv2, kv, g2, *, name, tr=512):
    t, w = kv.shape
    tr = min(tr, t)
    kw = w // 2

    def body(dk_ref, dv_ref, x_ref, g_ref, o_ref, dg_ref):
        i = pl.program_id(0)
        bd = _head_mean_matrix()
        half = _lane_half((tr, LANES))
        acc = jnp.zeros((1, LANES), f32)

        def fold(ref, cb):
            a = ref[:, (2 * cb) * LANES:(2 * cb + 1) * LANES]
            b = ref[:, (2 * cb + 1) * LANES:(2 * cb + 2) * LANES]
            return jnp.where(half == 0, a + pltpu.roll(a, HEAD_DIM, 1), b + pltpu.roll(b, HEAD_DIM, 1))

        for cb in range(kw // LANES):
            cols = slice(cb * LANES, (cb + 1) * LANES)
            xc = x_ref[:, cols]
            rh = lax.rsqrt(_dot_exact(xc * xc, bd) + EPS)
            xh = xc * rh
            dy = fold(dk_ref, cb)
            acc = acc + jnp.sum(dy * xh, axis=0, keepdims=True)
            tg = dy * g_ref[...]
            o_ref[:, cols] = (rh * (tg - xh * _dot_exact(tg * xh, bd))).astype(bf16)
            o_ref[:, kw + cb * LANES:kw + (cb + 1) * LANES] = fold(dv_ref, cb).astype(bf16)

        @pl.when(i == 0)
        def _():
            dg_ref[...] = acc

        @pl.when(i > 0)
        def _():
            dg_ref[...] += acc

    dup = pl.BlockSpec((tr, 2 * kw), lambda i: (i, 0))
    row = pl.BlockSpec((tr, w), lambda i: (i, 0))
    vec = pl.BlockSpec((1, LANES), lambda i: (0, 0))
    return pl.pallas_call(
        body, name=name, grid=(t // tr,), in_specs=[dup, dup, row, vec], out_specs=[row, vec],
        out_shape=[jax.ShapeDtypeStruct((t, w), bf16), jax.ShapeDtypeStruct((1, LANES), f32)],
        compiler_params=_params("arbitrary"))(dk2, dv2, kv, g2)


def _slope(h):
    return 2.0 ** (-8.0 * (h + 1) / N_Q_HEADS)


def _band_mask(n):
    tq = lax.broadcasted_iota(jnp.int32, (CHUNK, 2 * CHUNK), 0)
    jk = lax.broadcasted_iota(jnp.int32, (CHUNK, 2 * CHUNK), 1)
    dist = tq + CHUNK - jk
    ok = (dist >= 0) & (dist < CHUNK) & jnp.logical_not((n == 0) & (jk < CHUNK))
    return dist.astype(f32), ok


def _band(ref, n, kh):
    p0 = pl.multiple_of(jnp.maximum(n - 1, 0) * CHUNK, CHUNK)
    c0 = pl.multiple_of(n * CHUNK, CHUNK)
    cols = slice(kh * LANES, (kh + 1) * LANES)
    return jnp.concatenate([ref[pl.ds(p0, CHUNK), cols], ref[pl.ds(c0, CHUNK), cols]], axis=0)


def _softmax_band(qm, kband, dist, ok, slope, sink):
    s = _dot_nt(qm, kband)
    s = jnp.where(ok, s - slope * dist, -jnp.inf)
    m = jnp.maximum(jnp.max(s, axis=1, keepdims=True), sink)
    e = jnp.exp(s - m)
    es = jnp.exp(sink - m)
    den = jnp.sum(e, axis=1, keepdims=True) + es
    return e / den, es / den


def attn_fwd(q, k2, v2, sinks, *, name):
    t, w = q.shape
    nb = t // CHUNK

    def body(sink_ref, q_ref, k_ref, v_ref, o_ref):
        n = pl.program_id(0)
        dist, ok = _band_mask(n)
        half = _lane_half((CHUNK, LANES))
        for cb in range(w // LANES):
            qc = q_ref[:, cb * LANES:(cb + 1) * LANES].astype(f32)
            outs = []
            for hf in range(2):
                h = 2 * cb + hf
                kh = h // Q_PER_KV
                qm = jnp.where(half == hf, qc, 0.0).astype(bf16)
                p, _ = _softmax_band(qm, _band(k_ref, n, kh), dist, ok, _slope(h), sink_ref[h])
                outs.append(_dot(p.astype(bf16), _band(v_ref, n, kh)))
            o_ref[:, cb * LANES:(cb + 1) * LANES] = jnp.where(half == 0, outs[0], outs[1]).astype(bf16)

    full = pl.BlockSpec((t, k2.shape[1]), lambda n: (0, 0))
    return pl.pallas_call(
        body, name=name, grid=(nb,),
        in_specs=[pl.BlockSpec(memory_space=pltpu.SMEM), pl.BlockSpec((CHUNK, w), lambda n: (n, 0)), full, full],
        out_specs=pl.BlockSpec((CHUNK, w), lambda n: (n, 0)),
        out_shape=jax.ShapeDtypeStruct((t, w), bf16),
        compiler_params=_params("parallel"))(sinks, q, k2, v2)


def attn_bwd(q, k2, v2, do, sinks, *, name):
    t, w = q.shape
    nb = t // CHUNK
    kw = k2.shape[1]

    def body(sink_ref, q_ref, k_ref, v_ref, do_ref, dq_ref, dk_ref, dv_ref, ds_ref, kc_ref, vc_ref):
        n = pl.program_id(0)

        @pl.when(n == 0)
        def _():
            ds_ref[...] = jnp.zeros_like(ds_ref)
            kc_ref[...] = jnp.zeros_like(kc_ref)
            vc_ref[...] = jnp.zeros_like(vc_ref)
            dk_ref[...] = jnp.zeros_like(dk_ref)
            dv_ref[...] = jnp.zeros_like(dv_ref)

        @pl.when(n == nb)
        def _():
            dk_ref[...] = kc_ref[...]
            dv_ref[...] = vc_ref[...]

        @pl.when(n < nb)
        def _():
            dist, ok = _band_mask(n)
            half = _lane_half((CHUNK, LANES))
            lane = lax.broadcasted_iota(jnp.int32, (1, LANES), 1)
            sink_acc = jnp.zeros((1, LANES), f32)
            dkb = [jnp.zeros((2 * CHUNK, LANES), f32) for _ in range(N_KV_HEADS)]
            dvb = [jnp.zeros((2 * CHUNK, LANES), f32) for _ in range(N_KV_HEADS)]
            for cb in range(w // LANES):
                cols = slice(cb * LANES, (cb + 1) * LANES)
                qc = q_ref[:, cols].astype(f32)
                doc = do_ref[:, cols]
                dqs = []
                for hf in range(2):
                    h = 2 * cb + hf
                    kh = h // Q_PER_KV
                    qm = jnp.where(half == hf, qc, 0.0).astype(bf16)
                    dom = jnp.where(half == hf, doc, 0.0).astype(bf16)
                    kband = _band(k_ref, n, kh)
                    vband = _band(v_ref, n, kh)
                    p, ps = _softmax_band(qm, kband, dist, ok, _slope(h), sink_ref[h])
                    dp = _dot_nt(dom, vband)
                    delta = jnp.sum(p * dp, axis=1, keepdims=True)
                    dsb = (p * (dp - delta)).astype(bf16)
                    sink_acc = sink_acc + jnp.where(lane == h, -jnp.sum(ps * delta, axis=0, keepdims=True), 0.0)
                    dqs.append(_dot(dsb, kband))
                    dkb[kh] = dkb[kh] + _dot_tn(dsb, qm)
                    dvb[kh] = dvb[kh] + _dot_tn(p.astype(bf16), dom)
                dq_ref[:, cols] = jnp.where(half == 0, dqs[0], dqs[1])
            ds_ref[...] += sink_acc
            for kh in range(N_KV_HEADS):
                cols = slice(kh * LANES, (kh + 1) * LANES)
                dk_ref[:, cols] = kc_ref[:, cols] + dkb[kh][0:CHUNK]
                dv_ref[:, cols] = vc_ref[:, cols] + dvb[kh][0:CHUNK]
                kc_ref[:, cols] = dkb[kh][CHUNK:]
                vc_ref[:, cols] = dvb[kh][CHUNK:]

    full = pl.BlockSpec((t, kw), lambda n: (0, 0))
    qblk = pl.BlockSpec((CHUNK, w), lambda n: (jnp.minimum(n, nb - 1), 0))
    kblk = pl.BlockSpec((CHUNK, kw), lambda n: (jnp.maximum(n - 1, 0), 0))
    return pl.pallas_call(
        body, name=name, grid=(nb + 1,),
        in_specs=[pl.BlockSpec(memory_space=pltpu.SMEM), qblk, full, full, qblk],
        out_specs=[qblk, kblk, kblk, pl.BlockSpec((1, LANES), lambda n: (0, 0))],
        out_shape=[jax.ShapeDtypeStruct((t, w), f32), jax.ShapeDtypeStruct((t, kw), f32),
                   jax.ShapeDtypeStruct((t, kw), f32), jax.ShapeDtypeStruct((1, LANES), f32)],
        scratch_shapes=[pltpu.VMEM((CHUNK, kw), f32), pltpu.VMEM((CHUNK, kw), f32)],
        compiler_params=_params("arbitrary"))(sinks, q, k2, v2, do)


def loss_head(y, target, *, name, tr=512):
    t, d = y.shape
    tr = min(tr, t)

    def body(y_ref, t_ref, dy_ref, s_ref):
        i = pl.program_id(0)
        e = y_ref[...] - t_ref[...]
        dy_ref[...] = e * (1.0 / d)
        part = jnp.sum(e * e, axis=0, keepdims=True)

        @pl.when(i == 0)
        def _():
            s_ref[...] = part

        @pl.when(i > 0)
        def _():
            s_ref[...] += part

    row = pl.BlockSpec((tr, d), lambda i: (i, 0))
    vec = pl.BlockSpec((1, d), lambda i: (0, 0))
    return pl.pallas_call(
        body, name=name, grid=(t // tr,), in_specs=[row, row], out_specs=[row, vec],
        out_shape=[jax.ShapeDtypeStruct((t, d), f32), jax.ShapeDtypeStruct((1, d), f32)],
        compiler_params=_params("arbitrary"))(y, target)


N_STEPS = 8


def _row_blocks(shape):
    if len(shape) == 2:
        r, c = shape
        return (r // N_STEPS, c), (lambda s: (s, 0))
    l, r, c = shape
    per = N_STEPS // l
    return (1, r // per, c), (lambda s: (s // per, s % per, 0))


def cast_bf16(arrays, *, name):
    n = len(arrays)

    def body(*refs):
        for i_ref, o_ref in zip(refs[:n], refs[n:]):
            o_ref[...] = i_ref[...].astype(bf16)

    specs = [pl.BlockSpec(*_row_blocks(a.shape)) for a in arrays]
    return pl.pallas_call(
        body, name=name, grid=(N_STEPS,), in_specs=specs, out_specs=specs,
        out_shape=[jax.ShapeDtypeStruct(a.shape, bf16) for a in arrays],
        compiler_params=_params("parallel"))(*arrays)


def adamw(ws, gs, ms, vs, *, name):
    n = len(ws)

    def body(*refs):
        for i in range(n):
            w_ref, g_ref, m_ref, v_ref = (refs[k * n + i] for k in range(4))
            d_ref, nm_ref, nv_ref = (refs[(4 + k) * n + i] for k in range(3))
            g = g_ref[...]
            m = ADAM_B1 * m_ref[...] + (1.0 - ADAM_B1) * g
            v = ADAM_B2 * v_ref[...] + (1.0 - ADAM_B2) * (g * g)
            m_hat = m / ADAM_C1
            v_hat = v / ADAM_C2
            d_ref[...] = -ADAM_LR * (m_hat / (jnp.sqrt(v_hat) + ADAM_EPS) + ADAM_WD * w_ref[...])
            nm_ref[...] = m
            nv_ref[...] = v

    specs = [pl.BlockSpec(*_row_blocks(a.shape)) for a in ws]
    outs = pl.pallas_call(
        body, name=name, grid=(N_STEPS,), in_specs=specs * 4, out_specs=specs * 3,
        out_shape=[jax.ShapeDtypeStruct(a.shape, f32) for a in ws] * 3,
        compiler_params=_params("parallel"))(*ws, *gs, *ms, *vs)
    return outs[:n], outs[n:2 * n], outs[2 * n:]


def _place():
    return lax.axis_index("x"), lax.axis_index("y"), lax.axis_index("c")


def gather_shards(shards, *, name):
    n = len(shards)

    def body(*refs):
        ins, outs = refs[:n], refs[n:2 * n]
        send, recv, loc = refs[2 * n:]
        x, y, c = _place()
        k = 2 * x + y
        peers = [(1 - x, y, c), (x, 1 - y, c), (1 - x, 1 - y, c)]
        peer_k = [2 * (1 - x) + y, 2 * x + (1 - y), 2 * (1 - x) + (1 - y)]

        def copy(a, j, slot):
            return pltpu.make_async_remote_copy(
                src_ref=ins[a], dst_ref=outs[a].at[slot], send_sem=send.at[3 * a + j], recv_sem=recv.at[3 * a + j],
                device_id=peers[j], device_id_type=MESH)

        local = [pltpu.make_async_copy(ins[a], outs[a].at[k], loc.at[a]) for a in range(n)]
        for a in range(n):
            local[a].start()
            for j in range(3):
                copy(a, j, k).start()
        for a in range(n):
            for j in range(3):
                copy(a, j, peer_k[j]).wait_recv()
        for a in range(n):
            for j in range(3):
                copy(a, j, k).wait_send()
            local[a].wait()

    return pl.pallas_call(
        body, name=name, in_specs=[ANY] * n, out_specs=[ANY] * n,
        out_shape=[jax.ShapeDtypeStruct((N_SHARDS,) + a.shape, a.dtype) for a in shards],
        scratch_shapes=[pltpu.SemaphoreType.DMA((3 * n,)), pltpu.SemaphoreType.DMA((3 * n,)),
                        pltpu.SemaphoreType.DMA((n,))])(*shards)


def sibling_exchange(arrays, *, name):
    n = len(arrays)

    def body(*refs):
        ins, outs = refs[:n], refs[n:2 * n]
        send, recv = refs[2 * n:]
        x, y, c = _place()

        def copy(a):
            return pltpu.make_async_remote_copy(
                src_ref=ins[a], dst_ref=outs[a], send_sem=send.at[a], recv_sem=recv.at[a],
                device_id=(x, y, 1 - c), device_id_type=MESH)

        for a in range(n):
            copy(a).start()
        for a in range(n):
            copy(a).wait_recv()
        for a in range(n):
            copy(a).wait_send()

    return pl.pallas_call(
        body, name=name, in_specs=[ANY] * n, out_specs=[ANY] * n,
        out_shape=[jax.ShapeDtypeStruct(a.shape, a.dtype) for a in arrays],
        scratch_shapes=[pltpu.SemaphoreType.DMA((n,)), pltpu.SemaphoreType.DMA((n,))])(*arrays)


def chip_scatter(arrays, *, name):
    n = len(arrays)

    def body(*refs):
        ins, outs = refs[:n], refs[n:2 * n]
        send, recv = refs[2 * n:]
        x, y, c = _place()
        peers = [(1 - x, y, c), (x, 1 - y, c), (1 - x, 1 - y, c)]
        peer_k = [2 * (1 - x) + y, 2 * x + (1 - y), 2 * (1 - x) + (1 - y)]

        def copy(a, j):
            return pltpu.make_async_remote_copy(
                src_ref=ins[a].at[peer_k[j]], dst_ref=outs[a].at[j], send_sem=send.at[3 * a + j],
                recv_sem=recv.at[3 * a + j], device_id=peers[j], device_id_type=MESH)

        for a in range(n):
            for j in range(3):
                copy(a, j).start()
        for a in range(n):
            for j in range(3):
                copy(a, j).wait_recv()
        for a in range(n):
            for j in range(3):
                copy(a, j).wait_send()

    return pl.pallas_call(
        body, name=name, in_specs=[ANY] * n, out_specs=[ANY] * n,
        out_shape=[jax.ShapeDtypeStruct((3,) + a.shape[1:], a.dtype) for a in arrays],
        scratch_shapes=[pltpu.SemaphoreType.DMA((3 * n,)), pltpu.SemaphoreType.DMA((3 * n,))])(*arrays)


def sibling_merge(halves, layout, *, name):
    n = len(halves)
    n_out = max(o for o, _ in layout) + 1
    out_shapes = [None] * n_out
    for u, (o, l) in enumerate(layout):
        layers = max(l2 for o2, l2 in layout if o2 == o) + 1
        out_shapes[o] = (layers, 2) + halves[u].shape

    def body(*refs):
        ins, outs = refs[:n], refs[n:n + n_out]
        send, recv, loc = refs[n + n_out:]
        x, y, c = _place()

        def copy(u, slot):
            o, l = layout[u]
            return pltpu.make_async_remote_copy(
                src_ref=ins[u], dst_ref=outs[o].at[l, slot], send_sem=send.at[u], recv_sem=recv.at[u],
                device_id=(x, y, 1 - c), device_id_type=MESH)

        local = [pltpu.make_async_copy(ins[u], outs[layout[u][0]].at[layout[u][1], c], loc.at[u]) for u in range(n)]
        for u in range(n):
            local[u].start()
            copy(u, c).start()
        for u in range(n):
            copy(u, 1 - c).wait_recv()
        for u in range(n):
            copy(u, c).wait_send()
            local[u].wait()

    return pl.pallas_call(
        body, name=name, in_specs=[ANY] * n, out_specs=[ANY] * n_out,
        out_shape=[jax.ShapeDtypeStruct(s, f32) for s in out_shapes],
        scratch_shapes=[pltpu.SemaphoreType.DMA((n,)), pltpu.SemaphoreType.DMA((n,)),
                        pltpu.SemaphoreType.DMA((n,))])(*halves)


def all_gather_rows(a, *, name):
    def body(a_ref, o_ref, send, recv, loc):
        x, y, c = _place()
        me = 4 * x + 2 * y + c
        masks = [(mx, my, mc) for mx in (0, 1) for my in (0, 1) for mc in (0, 1)][1:]

        def peer(mk):
            return (x ^ mk[0], y ^ mk[1], c ^ mk[2])

        def copy(j, slot):
            return pltpu.make_async_remote_copy(
                src_ref=a_ref, dst_ref=o_ref.at[slot], send_sem=send.at[j], recv_sem=recv.at[j],
                device_id=peer(masks[j]), device_id_type=MESH)

        local = pltpu.make_async_copy(a_ref, o_ref.at[me], loc)
        local.start()
        for j in range(7):
            copy(j, me).start()
        for j in range(7):
            px, py, pc = peer(masks[j])
            copy(j, 4 * px + 2 * py + pc).wait_recv()
        for j in range(7):
            copy(j, me).wait_send()
        local.wait()

    return pl.pallas_call(
        body, name=name, in_specs=[ANY], out_specs=ANY,
        out_shape=jax.ShapeDtypeStruct((N_DEV,) + a.shape, a.dtype),
        scratch_shapes=[pltpu.SemaphoreType.DMA((7,)), pltpu.SemaphoreType.DMA((7,)), pltpu.SemaphoreType.DMA])(a)


def sum_leading(a, *, name):
    n, r, c = a.shape

    def body(a_ref, o_ref):
        acc = a_ref[0]
        for i in range(1, n):
            acc = acc + a_ref[i]
        o_ref[...] = acc

    rb = r // 2 if r % 16 == 0 else r
    return pl.pallas_call(
        body, name=name, grid=(r // rb,), in_specs=[pl.BlockSpec((n, rb, c), lambda i: (0, i, 0))],
        out_specs=pl.BlockSpec((rb, c), lambda i: (i, 0)), out_shape=jax.ShapeDtypeStruct((r, c), f32),
        compiler_params=_params("parallel"))(a)


def _half_rows(shape):
    return shape[1] // 2 // 2


def rs_cast_other_half(grads, c_arr, *, name):
    n = len(grads)

    def body(c_ref, *refs):
        for i_ref, o_ref in zip(refs[:n], refs[n:]):
            o_ref[...] = i_ref[...].astype(bf16)

    in_specs = [pl.BlockSpec((1, _half_rows(g.shape), g.shape[2]), lambda s, r, c_ref: (s, (1 - c_ref[0]) * 2 + r, 0))
                for g in grads]
    out_specs = [pl.BlockSpec((1, _half_rows(g.shape), g.shape[2]), lambda s, r, c_ref: (s, r, 0)) for g in grads]
    return pl.pallas_call(
        body, name=name,
        grid_spec=pltpu.PrefetchScalarGridSpec(num_scalar_prefetch=1, grid=(N_SHARDS, 2),
                                               in_specs=in_specs, out_specs=out_specs),
        out_shape=[jax.ShapeDtypeStruct((N_SHARDS, g.shape[1] // 2, g.shape[2]), bf16) for g in grads],
        compiler_params=_params("parallel", "parallel"))(c_arr, *grads)


def rs_add_sibling(grads, recvd, ck_arr, *, name):
    n = len(grads)

    def body(ck_ref, *refs):
        s = pl.program_id(1)
        for u in range(n):
            g_ref, r_ref = refs[u], refs[n + u]
            qb_ref, own_ref = refs[2 * n + u], refs[3 * n + u]
            q = g_ref[0] + r_ref[0].astype(f32)
            qb_ref[0] = q.astype(bf16)

            @pl.when(s == ck_ref[1])
            def _(own_ref=own_ref, q=q):
                own_ref[...] = q

    in_specs = [pl.BlockSpec((1, _half_rows(g.shape), g.shape[2]), lambda r, s, ck: (s, ck[0] * 2 + r, 0)) for g in grads]
    in_specs += [pl.BlockSpec((1, _half_rows(g.shape), g.shape[2]), lambda r, s, ck: (s, r, 0)) for g in grads]
    out_specs = [pl.BlockSpec((1, _half_rows(g.shape), g.shape[2]), lambda r, s, ck: (s, r, 0)) for g in grads]
    out_specs += [pl.BlockSpec((_half_rows(g.shape), g.shape[2]), lambda r, s, ck: (r, 0)) for g in grads]
    outs = pl.pallas_call(
        body, name=name,
        grid_spec=pltpu.PrefetchScalarGridSpec(num_scalar_prefetch=1, grid=(2, N_SHARDS),
                                               in_specs=in_specs, out_specs=out_specs),
        out_shape=[jax.ShapeDtypeStruct((N_SHARDS, g.shape[1] // 2, g.shape[2]), bf16) for g in grads]
        + [jax.ShapeDtypeStruct((g.shape[1] // 2, g.shape[2]), f32) for g in grads],
        compiler_params=_params("parallel", "arbitrary"))(ck_arr, *grads, *recvd)
    return outs[:n], outs[n:]


def rs_sum_chips(owns, recvd, *, name):
    n = len(owns)

    def body(*refs):
        for u in range(n):
            own_ref, r_ref, o_ref = refs[u], refs[n + u], refs[2 * n + u]
            o_ref[...] = ((own_ref[...] + r_ref[0].astype(f32)) + r_ref[1].astype(f32)) + r_ref[2].astype(f32)

    in_specs = [pl.BlockSpec((o.shape[0] // 2, o.shape[1]), lambda r: (r, 0)) for o in owns]
    in_specs += [pl.BlockSpec((3, o.shape[0] // 2, o.shape[1]), lambda r: (0, r, 0)) for o in owns]
    out_specs = [pl.BlockSpec((o.shape[0] // 2, o.shape[1]), lambda r: (r, 0)) for o in owns]
    return pl.pallas_call(
        body, name=name, grid=(2,), in_specs=in_specs, out_specs=out_specs,
        out_shape=[jax.ShapeDtypeStruct(o.shape, f32) for o in owns],
        compiler_params=_params("parallel"))(*owns, *recvd)


SMALL = ("a_norm", "a_v_norm", "a_w_s", "a_b_s", "f_norm", "f_conv_w", "f_conv_b", "kv_norm", "k_norm",
         "b_norm", "b_q_norm", "b_sinks")
BIG = ("a_w_in", "a_w_out", "f_w_in", "f_w_out", "w_kv", "b_w_q", "b_w_o")
PACK_COLS = 1024
PACK_ROWS = 8 * N_STEPS


def _pack(parts, rows=PACK_ROWS):
    flat = jnp.concatenate([p.reshape(-1).astype(f32) for p in parts])
    pad = (-flat.shape[0]) % (rows * PACK_COLS)
    return jnp.pad(flat, (0, pad)).reshape(-1, PACK_COLS)


def _unpack(packed, shapes):
    flat = packed.reshape(-1)
    out, off = [], 0
    for s in shapes:
        size = math.prod(s)
        out.append(flat[off:off + size].reshape(s))
        off += size
    return out


def _ffn_fwd(x, g, w_in4, conv_w, conv_b, w_out, tag):
    f = w_out.shape[0]
    (h,), r = rms_fwd(x, [g], name=f"ffn{tag}_norm")
    pg = mm_nn(h, w_in4, name=f"ffn{tag}_in_gate", s0=0, ns=2)
    pu = mm_nn(h, w_in4, name=f"ffn{tag}_in_up", s0=2, ns=2)
    wg, wu = conv_w[:, :f], conv_w[:, f:]
    bg, bu = conv_b[None, :f], conv_b[None, f:]
    a = ffn_gate_fwd(pg, pu, wg, wu, bg, bu, name=f"ffn{tag}_gate")
    y = mm_nn(a, w_out[None], name=f"ffn{tag}_out", add=x)
    return y, (x, g, h, r, pg, pu, a, wg, wu, bg, bu)


def _ffn_bwd(dy, saved, w_in4, w_out, tag):
    x, g, h, r, pg, pu, a, wg, wu, bg, bu = saved
    f = w_out.shape[0]
    da = mm_nt([dy], w_out[None], name=f"ffn{tag}_dact", tko=f // 2)
    d_w_out = mm_tn(a, [dy], name=f"ffn{tag}_dwout", n_s=w_out.shape[1], tki=f // 2)
    dgate, dup, sg, su = ffn_gate_bwd(pg, pu, da, wg, wu, bg, bu, name=f"ffn{tag}_dgate")
    dpg = conv_input_grad(dgate, wg, name=f"ffn{tag}_dconv_gate")
    dpu = conv_input_grad(dup, wu, name=f"ffn{tag}_dconv_up")
    d_w_in = mm_tn(h, [dpg, dpu], name=f"ffn{tag}_dwin", n_s=w_in4.shape[2])
    dh = mm_nt([dpg, dpu], w_in4, name=f"ffn{tag}_dh")
    dx, (dg,) = rms_bwd([dh], x, r, [g], dy, name=f"ffn{tag}_dnorm")
    d_conv_w = jnp.concatenate([sg[0:3], su[0:3]], axis=1)
    d_conv_b = jnp.concatenate([sg[3], su[3]], axis=0)
    return dx, dg, d_w_in, d_conv_w, d_conv_b, d_w_out


def kernel(x, a_norm, a_w_in, a_v_norm, a_w_s, a_b_s, a_w_out, f_norm, f_w_in, f_conv_w, f_conv_b, f_w_out, kv_norm, w_kv, k_norm, b_norm, b_w_q, b_q_norm, b_sinks, b_w_o, loss_target, m_a_norm, m_a_w_in, m_a_v_norm, m_a_w_s, m_a_b_s, m_a_w_out, m_f_norm, m_f_w_in, m_f_conv_w, m_f_conv_b, m_f_w_out, m_kv_norm, m_w_kv, m_k_norm, m_b_norm, m_b_w_q, m_b_q_norm, m_b_sinks, m_b_w_o, v_a_norm, v_a_w_in, v_a_v_norm, v_a_w_s, v_a_b_s, v_a_w_out, v_f_norm, v_f_w_in, v_f_conv_w, v_f_conv_b, v_f_w_out, v_kv_norm, v_w_kv, v_k_norm, v_b_norm, v_b_w_q, v_b_q_norm, v_b_sinks, v_b_w_o):
    args = dict(locals())
    weights = {n: args[n] for n in SMALL + BIG}
    moms = {n: args["m_" + n] for n in SMALL + BIG}
    vars_ = {n: args["v_" + n] for n in SMALL + BIG}
    t, d = x.shape[1], x.shape[2]
    xi, yi, ci = _place()
    chip = 2 * xi + yi

    big_local = [a_w_in[0], a_w_out[0], f_w_in, f_w_out, w_kv, b_w_q[0], b_w_o[0]]
    big_bf = cast_bf16(big_local, name="cast_weights")
    small_cols = _pack([a_norm, a_v_norm, f_conv_w], rows=8)
    gathered = gather_shards(list(big_bf) + [small_cols], name="gather_weights")
    g_a_w_in, g_a_w_out, g_f_w_in, g_f_w_out, g_w_kv, g_b_w_q, g_b_w_o, g_small = gathered
    ns_cols = a_norm.shape[1]
    nf_cols = f_conv_w.shape[2]
    parts = [_unpack(g_small[k], [a_norm.shape, a_v_norm.shape, f_conv_w.shape]) for k in range(N_SHARDS)]
    a_norm_f = jnp.concatenate([p[0] for p in parts], axis=1)
    a_v_norm_f = jnp.concatenate([p[1] for p in parts], axis=1)
    conv_w_f = jnp.concatenate([p[2] for p in parts], axis=2)
    w_a_in = g_a_w_in
    w_a_out = g_a_w_out.reshape(1, -1, d)
    w_f_in = [g_f_w_in[:, l] for l in range(2)]
    w_f_out = [g_f_w_out[:, l].reshape(-1, d) for l in range(2)]
    w_kv_f = g_w_kv.reshape(1, d, -1)
    w_q_f = g_b_w_q.reshape(1, d, -1)
    w_o_f = g_b_w_o.reshape(1, -1, d)

    x0 = x[0]
    tril = jnp.tril(jnp.ones((CHUNK, CHUNK), dtype=bool))
    wc = jnp.where(tril[None], a_w_s[0], 0.0).astype(bf16)
    bt = a_b_s[0].T
    kg2 = jnp.tile(k_norm, 2)[None]
    qg2 = jnp.tile(b_q_norm[0], 2)[None]

    (h_a,), r_a = rms_fwd(x0, [a_norm_f], name="a_norm")
    zu = mm_nn(h_a, w_a_in, name="a_in_u", s0=0, ns=2)
    zv = mm_nn(h_a, w_a_in, name="a_in_v", s0=2, ns=2)
    y_a = sgu_gate_fwd(zu, zv, a_v_norm_f, wc, bt, name="a_gate")
    x1 = mm_nn(y_a, w_a_out, name="a_out", add=x0)
    x2, ffn0 = _ffn_fwd(x1, f_norm[0:1], w_f_in[0], conv_w_f[0], f_conv_b[0], w_f_out[0], "0")
    (h_k, h_q), r_b = rms_fwd(x2, [kv_norm[None], b_norm], name="b_norm")
    kv = mm_nn(h_k, w_kv_f, name="kv_proj")
    k2, v2 = kv_post_fwd(kv, kg2, name="kv_post")
    qp = mm_nn(h_q, w_q_f, name="q_proj")
    qn = q_norm_fwd(qp, qg2, name="q_norm", scale=HEAD_DIM ** -0.5)
    o = attn_fwd(qn, k2, v2, b_sinks[0], name="attn")
    x3 = mm_nn(o, w_o_f, name="o_proj", add=x2)
    x4, ffn1 = _ffn_fwd(x3, f_norm[1:2], w_f_in[1], conv_w_f[1], f_conv_b[1], w_f_out[1], "1")
    dx4, sq = loss_head(x4, loss_target[0], name="loss")
    loss = lax.psum(0.5 * jnp.sum(sq) / d, ("x", "y", "c"))

    dx3, d_fn1, d_fwin1, d_cw1, d_cb1, d_fwout1 = _ffn_bwd(dx4, ffn1, w_f_in[1], w_f_out[1], "1")
    do = mm_nt([dx3], w_o_f, name="o_proj_dx")
    d_w_o = mm_tn(o, [dx3], name="o_proj_dw", n_s=d)
    dqn, dk2, dv2, dsink = attn_bwd(qn, k2, v2, do, b_sinks[0], name="attn_bwd")
    dqp, dqg = q_norm_bwd(dqn, qp, qg2, name="q_norm_bwd", scale=HEAD_DIM ** -0.5)
    dkv, dkg = kv_post_bwd(dk2, dv2, kv, kg2, name="kv_post_bwd")
    d_w_q = mm_tn(h_q, [dqp], name="q_proj_dw", n_s=w_q_f.shape[2])
    dh_q = mm_nt([dqp], w_q_f, name="q_proj_dx")
    d_w_kv = mm_tn(h_k, [dkv], name="kv_proj_dw", n_s=w_kv_f.shape[2])
    dh_k = mm_nt([dkv], w_kv_f, name="kv_proj_dx")
    dx2, (d_kvn, d_bn) = rms_bwd([dh_k, dh_q], x2, r_b, [kv_norm[None], b_norm], dx3, name="b_norm_bwd")
    dx1, d_fn0, d_fwin0, d_cw0, d_cb0, d_fwout0 = _ffn_bwd(dx2, ffn0, w_f_in[0], w_f_out[0], "0")
    dy_a = mm_nt([dx1], w_a_out, name="a_out_dx")
    d_w_aout = mm_tn(y_a, [dx1], name="a_out_dw", n_s=d)
    dzu, dzv, d_avn, d_ws, d_bt = sgu_gate_bwd(zu, zv, dy_a, a_v_norm_f, wc, bt, name="a_gate_bwd")
    d_w_ain = mm_tn(h_a, [dzu, dzv], name="a_in_dw", n_s=w_a_in.shape[2])
    dh_a = mm_nt([dzu, dzv], w_a_in, name="a_in_dx")
    dx0, (d_an,) = rms_bwd([dh_a], x0, r_a, [a_norm_f], dx1, name="a_norm_bwd")
    grad_x = dx0[None]

    sh = N_SHARDS
    units = [d_w_ain, d_w_aout.reshape(sh, -1, d), d_fwin0, d_fwin1, d_fwout0.reshape(sh, -1, d),
             d_fwout1.reshape(sh, -1, d), d_w_kv.reshape(sh, -1, d_w_kv.shape[2]), d_w_q.reshape(sh, -1, d_w_q.shape[2]),
             d_w_o.reshape(sh, -1, d)]
    layout = [(0, 0), (1, 0), (2, 0), (2, 1), (3, 0), (3, 1), (4, 0), (5, 0), (6, 0)]
    c_arr = jnp.stack([ci, chip]).astype(jnp.int32)
    other_bf = rs_cast_other_half(units, c_arr, name="rs_cast")
    from_sib = sibling_exchange(list(other_bf), name="rs_sibling")
    chip_bf, own = rs_add_sibling(units, from_sib, c_arr, name="rs_add")
    from_chips = chip_scatter(list(chip_bf), name="rs_chips")
    halves = rs_sum_chips(list(own), from_chips, name="rs_sum")
    merged = sibling_merge(list(halves), layout, name="rs_merge")
    big_w = [weights[n] for n in BIG]
    big_g = [mg.reshape(w.shape) for mg, w in zip(merged, big_w)]
    big_d, big_m, big_v = adamw(big_w, big_g, [moms[n] for n in BIG], [vars_[n] for n in BIG], name="adamw_big")

    d_fn = jnp.concatenate([d_fn0, d_fn1], axis=0)
    d_cw = jnp.stack([d_cw0, d_cw1])
    d_cb = jnp.stack([d_cb0, d_cb1])
    d_kg = (dkg[0, :HEAD_DIM] + dkg[0, HEAD_DIM:])
    d_qg = (dqg[0, :HEAD_DIM] + dqg[0, HEAD_DIM:])[None]
    small_full = [d_an, d_avn, d_ws[None], d_bt.T[None], d_fn, d_cw, d_cb, d_kvn[0], d_kg, d_bn, d_qg,
                  dsink[:, :N_Q_HEADS]]
    full_shapes = [g.shape for g in small_full]
    packed = _pack(small_full)
    summed = sum_leading(all_gather_rows(packed, name="small_gather"), name="small_sum")
    small_g = _unpack(summed, full_shapes)
    small_g[0] = lax.dynamic_slice_in_dim(small_g[0], chip * ns_cols, ns_cols, axis=1)
    small_g[1] = lax.dynamic_slice_in_dim(small_g[1], chip * ns_cols, ns_cols, axis=1)
    small_g[5] = lax.dynamic_slice_in_dim(small_g[5], chip * nf_cols, nf_cols, axis=2)
    small_shapes = [weights[n].shape for n in SMALL]
    small_g = [g.reshape(s) for g, s in zip(small_g, small_shapes)]
    pw, pg_, pm, pv = (_pack(v) for v in ([weights[n] for n in SMALL], small_g, [moms[n] for n in SMALL],
                                          [vars_[n] for n in SMALL]))
    (sd,), (sm,), (sv,) = adamw([pw], [pg_], [pm], [pv], name="adamw_small")
    small_d, small_m, small_v = (_unpack(v, small_shapes) for v in (sd, sm, sv))

    out = {}
    for i, n in enumerate(SMALL):
        out[n] = (small_g[i], small_d[i], small_m[i], small_v[i])
    for i, n in enumerate(BIG):
        out[n] = (big_g[i], big_d[i], big_m[i], big_v[i])
    order = ["a_norm", "a_w_in", "a_v_norm", "a_w_s", "a_b_s", "a_w_out", "f_norm", "f_w_in", "f_conv_w", "f_conv_b",
             "f_w_out", "kv_norm", "w_kv", "k_norm", "b_norm", "b_w_q", "b_q_norm", "b_sinks", "b_w_o"]
    return (loss, grad_x, *[out[n][0] for n in order], *[out[n][1] for n in order],
            *[out[n][2] for n in order], *[out[n][3] for n in order])
```

```python
import functools
import math

import jax
import jax.numpy as jnp
from jax import lax
from jax.experimental import pallas as pl
from jax.experimental.pallas import tpu as pltpu

f32 = jnp.float32
bf16 = jnp.bfloat16
MESH = pl.DeviceIdType.MESH
ANY = pl.BlockSpec(memory_space=pl.ANY)

EPS = 1e-6
LANES = 128
CHUNK = 128
HEAD_DIM = 64
N_Q_HEADS = 16
N_KV_HEADS = 4
Q_PER_KV = N_Q_HEADS // N_KV_HEADS
N_SHARDS = 4
N_DEV = 8

ADAM_LR = 0.001
ADAM_B1 = 0.9
ADAM_B2 = 0.999
ADAM_EPS = 1e-08
ADAM_WD = 0.01
ADAM_STEP = 10
ADAM_C1 = 1.0 - ADAM_B1 ** ADAM_STEP
ADAM_C2 = 1.0 - ADAM_B2 ** ADAM_STEP

_INV_SQRT2 = 1.0 / math.sqrt(2.0)
_INV_SQRT2PI = 1.0 / math.sqrt(2.0 * math.pi)


def _params(*sem):
    return pltpu.CompilerParams(dimension_semantics=sem)


def _gelu(z):
    return 0.5 * z * (1.0 + lax.erf(z * _INV_SQRT2))


def _gelu_grad(z):
    return 0.5 * (1.0 + lax.erf(z * _INV_SQRT2)) + z * jnp.exp(-0.5 * z * z) * _INV_SQRT2PI


def _dot(a, b):
    return jnp.dot(a, b, preferred_element_type=f32)


def _dot_nt(a, b):
    return lax.dot_general(a, b, (((1,), (1,)), ((), ())), preferred_element_type=f32)


def _dot_tn(a, b):
    return lax.dot_general(a, b, (((0,), (0,)), ((), ())), preferred_element_type=f32)


def _dot_exact(a, b):
    return jnp.dot(a, b, preferred_element_type=f32, precision=lax.Precision.HIGHEST)


def mm_nn(a, w3, *, name, s0=0, ns=None, tm=512, add=None, out_dtype=f32):
    m, k = a.shape
    s_all, _, n_s = w3.shape
    ns = s_all if ns is None else ns
    tm = min(tm, m)

    def body(*refs):
        if add is None:
            a_ref, w_ref, o_ref = refs
            acc = _dot(a_ref[...].astype(bf16), w_ref[0])
        else:
            a_ref, w_ref, add_ref, o_ref = refs
            acc = _dot(a_ref[...].astype(bf16), w_ref[0]) + add_ref[...]
        o_ref[...] = acc.astype(out_dtype)

    in_specs = [pl.BlockSpec((tm, k), lambda j, i: (i, 0)),
                pl.BlockSpec((1, k, n_s), lambda j, i: (s0 + j, 0, 0))]
    args = [a, w3]
    if add is not None:
        in_specs.append(pl.BlockSpec((tm, n_s), lambda j, i: (i, j)))
        args.append(add)
    return pl.pallas_call(
        body, name=name, grid=(ns, m // tm), in_specs=in_specs,
        out_specs=pl.BlockSpec((tm, n_s), lambda j, i: (i, j)),
        out_shape=jax.ShapeDtypeStruct((m, ns * n_s), out_dtype),
        compiler_params=_params("parallel", "parallel"))(*args)


def mm_nt(a_list, w3, *, name, tm=512, tko=None, add=None, out_dtype=f32):
    s_all, k_out, n_s = w3.shape
    m = a_list[0].shape[0]
    na = len(a_list)
    spa = s_all // na
    tm = min(tm, m)
    tko = k_out if tko is None else tko

    def body(*refs):
        a_refs = refs[:na]
        w_ref = refs[na]
        o_ref = refs[-1]
        s = pl.program_id(2)

        @pl.when(s == 0)
        def _():
            if add is None:
                o_ref[...] = jnp.zeros_like(o_ref)
            else:
                o_ref[...] = refs[na + 1][...]

        for idx in range(na):
            @pl.when(s // spa == idx)
            def _(idx=idx):
                o_ref[...] += _dot_nt(a_refs[idx][...].astype(bf16), w_ref[0])

    def a_map(idx):
        return lambda ko, i, s: (i, jnp.clip(s - idx * spa, 0, spa - 1))

    in_specs = [pl.BlockSpec((tm, n_s), a_map(idx)) for idx in range(na)]
    in_specs.append(pl.BlockSpec((1, tko, n_s), lambda ko, i, s: (s, ko, 0)))
    args = list(a_list) + [w3]
    if add is not None:
        in_specs.append(pl.BlockSpec((tm, tko), lambda ko, i, s: (i, ko)))
        args.append(add)
    return pl.pallas_call(
        body, name=name, grid=(k_out // tko, m // tm, s_all), in_specs=in_specs,
        out_specs=pl.BlockSpec((tm, tko), lambda ko, i, s: (i, ko)),
        out_shape=jax.ShapeDtypeStruct((m, k_out), out_dtype),
        compiler_params=_params("parallel", "parallel", "arbitrary"))(*args)


def mm_tn(a, b_list, *, name, n_s, tm=512, tki=None):
    m, k_in = a.shape
    na = len(b_list)
    s_all = sum(b.shape[1] for b in b_list) // n_s
    spa = s_all // na
    tm = min(tm, m)
    tki = k_in if tki is None else tki

    def body(*refs):
        a_ref = refs[0]
        b_refs = refs[1:1 + na]
        o_ref = refs[-1]
        s = pl.program_id(0)
        r = pl.program_id(2)

        @pl.when(r == 0)
        def _():
            o_ref[...] = jnp.zeros_like(o_ref)

        for idx in range(na):
            @pl.when(s // spa == idx)
            def _(idx=idx):
                o_ref[0] += _dot_tn(a_ref[...].astype(bf16), b_refs[idx][...].astype(bf16))

    def b_map(idx):
        def index(s, ki, r):
            active = (s // spa) == idx
            return (jnp.where(active, r, 0), jnp.clip(s - idx * spa, 0, spa - 1))
        return index

    in_specs = [pl.BlockSpec((tm, tki), lambda s, ki, r: (r, ki))]
    in_specs += [pl.BlockSpec((tm, n_s), b_map(idx)) for idx in range(na)]
    return pl.pallas_call(
        body, name=name, grid=(s_all, k_in // tki, m // tm), in_specs=in_specs,
        out_specs=pl.BlockSpec((1, tki, n_s), lambda s, ki, r: (s, ki, 0)),
        out_shape=jax.ShapeDtypeStruct((s_all, k_in, n_s), f32),
        compiler_params=_params("parallel", "parallel", "arbitrary"))(a, *b_list)


def rms_fwd(x, gains, *, name, tr=512):
    t, d = x.shape
    tr = min(tr, t)
    ng = len(gains)

    def body(*refs):
        x_ref = refs[0]
        g_refs = refs[1:1 + ng]
        h_refs = refs[1 + ng:1 + 2 * ng]
        r_ref = refs[-1]
        xv = x_ref[...]
        r = lax.rsqrt(jnp.mean(xv * xv, axis=1, keepdims=True) + EPS)
        xh = xv * r
        for g_ref, h_ref in zip(g_refs, h_refs):
            h_ref[...] = (xh * g_ref[...]).astype(bf16)
        r_ref[...] = r

    row = pl.BlockSpec((tr, d), lambda i: (i, 0))
    vec = pl.BlockSpec((1, d), lambda i: (0, 0))
    outs = pl.pallas_call(
        body, name=name, grid=(t // tr,), in_specs=[row] + [vec] * ng,
        out_specs=[row] * ng + [pl.BlockSpec((tr, 1), lambda i: (i, 0))],
        out_shape=[jax.ShapeDtypeStruct((t, d), bf16)] * ng + [jax.ShapeDtypeStruct((t, 1), f32)],
        compiler_params=_params("parallel"))(x, *gains)
    return outs[:ng], outs[ng]


def rms_bwd(dh_list, x, r, gains, dx_in, *, name, tr=512):
    t, d = x.shape
    tr = min(tr, t)
    ng = len(gains)

    def body(*refs):
        dh_refs = refs[:ng]
        x_ref, r_ref = refs[ng], refs[ng + 1]
        g_refs = refs[ng + 2:2 * ng + 2]
        dxin_ref = refs[2 * ng + 2]
        dx_ref = refs[2 * ng + 3]
        dg_refs = refs[2 * ng + 4:]
        i = pl.program_id(0)
        rv = r_ref[...]
        xh = x_ref[...] * rv
        acc = dxin_ref[...]
        for dh_ref, g_ref, dg_ref in zip(dh_refs, g_refs, dg_refs):
            dh = dh_ref[...]
            part = jnp.sum(dh * xh, axis=0, keepdims=True)

            @pl.when(i == 0)
            def _(dg_ref=dg_ref, part=part):
                dg_ref[...] = part

            @pl.when(i > 0)
            def _(dg_ref=dg_ref, part=part):
                dg_ref[...] += part

            tg = dh * g_ref[...]
            acc = acc + rv * (tg - xh * jnp.mean(tg * xh, axis=1, keepdims=True))
        dx_ref[...] = acc

    row = pl.BlockSpec((tr, d), lambda i: (i, 0))
    vec = pl.BlockSpec((1, d), lambda i: (0, 0))
    outs = pl.pallas_call(
        body, name=name, grid=(t // tr,),
        in_specs=[row] * ng + [row, pl.BlockSpec((tr, 1), lambda i: (i, 0))] + [vec] * ng + [row],
        out_specs=[row] + [vec] * ng,
        out_shape=[jax.ShapeDtypeStruct((t, d), f32)] + [jax.ShapeDtypeStruct((1, d), f32)] * ng,
        compiler_params=_params("arbitrary"))(*dh_list, x, r, *gains, dx_in)
    return outs[0], outs[1:]


def sgu_gate_fwd(zu, zv, gv, wc, bt, *, name, tr=512):
    t, w = zu.shape
    tr = min(tr, t)
    groups = w // LANES

    def body(zu_ref, zv_ref, gv_ref, wc_ref, bt_ref, y_ref):
        vp = _gelu(zv_ref[...])
        rv = lax.rsqrt(jnp.mean(vp * vp, axis=1, keepdims=True) + EPS)
        vb = (vp * rv * gv_ref[...]).astype(bf16)
        for c in range(tr // CHUNK):
            rows = slice(c * CHUNK, (c + 1) * CHUNK)
            for g in range(groups):
                cols = slice(g * LANES, (g + 1) * LANES)
                sv = _dot(wc_ref[g], vb[rows, cols]) + bt_ref[:, g:g + 1]
                y_ref[rows, cols] = (_gelu(zu_ref[rows, cols]) * sv).astype(bf16)

    row = pl.BlockSpec((tr, w), lambda i: (i, 0))
    return pl.pallas_call(
        body, name=name, grid=(t // tr,),
        in_specs=[row, row, pl.BlockSpec((1, w), lambda i: (0, 0)),
                  pl.BlockSpec((groups, CHUNK, CHUNK), lambda i: (0, 0, 0)),
                  pl.BlockSpec((CHUNK, groups), lambda i: (0, 0))],
        out_specs=row, out_shape=jax.ShapeDtypeStruct((t, w), bf16),
        compiler_params=_params("parallel"))(zu, zv, gv, wc, bt)


def sgu_gate_bwd(zu, zv, dy, gv, wc, bt, *, name, tr=512):
    t, w = zu.shape
    tr = min(tr, t)
    groups = w // LANES
    nsteps = t // tr

    def body(zu_ref, zv_ref, dy_ref, gv_ref, wc_ref, bt_ref,
             dzu_ref, dzv_ref, dgv_ref, dws_ref, dbt_ref, dv_ref, bacc_ref):
        i = pl.program_id(0)

        @pl.when(i == 0)
        def _():
            dgv_ref[...] = jnp.zeros_like(dgv_ref)
            dws_ref[...] = jnp.zeros_like(dws_ref)
            bacc_ref[...] = jnp.zeros_like(bacc_ref)

        zvv = zv_ref[...]
        vp = _gelu(zvv)
        rv = lax.rsqrt(jnp.mean(vp * vp, axis=1, keepdims=True) + EPS)
        vhat = vp * rv
        vb = (vhat * gv_ref[...]).astype(bf16)
        for c in range(tr // CHUNK):
            rows = slice(c * CHUNK, (c + 1) * CHUNK)
            for g in range(groups):
                cols = slice(g * LANES, (g + 1) * LANES)
                vblk = vb[rows, cols]
                sv = _dot(wc_ref[g], vblk) + bt_ref[:, g:g + 1]
                zub = zu_ref[rows, cols]
                dyb = dy_ref[rows, cols]
                dzu_ref[rows, cols] = (dyb * sv * _gelu_grad(zub)).astype(bf16)
                dsv = dyb * _gelu(zub)
                bacc_ref[:, cols] += dsv
                dsvb = dsv.astype(bf16)
                dv_ref[rows, cols] = _dot_tn(wc_ref[g], dsvb)
                dws_ref[g] += _dot_nt(dsvb, vblk)
        dv = dv_ref[...]
        dgv_ref[...] += jnp.sum(dv * vhat, axis=0, keepdims=True)
        tg = dv * gv_ref[...]
        dvp = rv * (tg - vhat * jnp.mean(tg * vhat, axis=1, keepdims=True))
        dzv_ref[...] = (dvp * _gelu_grad(zvv)).astype(bf16)

        @pl.when(i == nsteps - 1)
        def _():
            tt = lax.broadcasted_iota(jnp.int32, (CHUNK, CHUNK), 0)
            ss = lax.broadcasted_iota(jnp.int32, (CHUNK, CHUNK), 1)
            for g in range(groups):
                dws_ref[g] = jnp.where(ss <= tt, dws_ref[g], 0.0)
                dbt_ref[:, g:g + 1] = jnp.sum(bacc_ref[:, g * LANES:(g + 1) * LANES], axis=1, keepdims=True)

    row = pl.BlockSpec((tr, w), lambda i: (i, 0))
    full3 = pl.BlockSpec((groups, CHUNK, CHUNK), lambda i: (0, 0, 0))
    return pl.pallas_call(
        body, name=name, grid=(nsteps,),
        in_specs=[row, row, row, pl.BlockSpec((1, w), lambda i: (0, 0)), full3,
                  pl.BlockSpec((CHUNK, groups), lambda i: (0, 0))],
        out_specs=[row, row, pl.BlockSpec((1, w), lambda i: (0, 0)), full3,
                   pl.BlockSpec((CHUNK, groups), lambda i: (0, 0))],
        out_shape=[jax.ShapeDtypeStruct((t, w), bf16), jax.ShapeDtypeStruct((t, w), bf16),
                   jax.ShapeDtypeStruct((1, w), f32), jax.ShapeDtypeStruct((groups, CHUNK, CHUNK), f32),
                   jax.ShapeDtypeStruct((CHUNK, groups), f32)],
        scratch_shapes=[pltpu.VMEM((tr, w), f32), pltpu.VMEM((CHUNK, w), f32)],
        compiler_params=_params("arbitrary"))(zu, zv, dy, gv, wc, bt)


HALO = 8


def _shift_down(v, halo, k, first):
    r = pltpu.roll(v, k, 0)
    hh = jnp.where(first, 0.0, pltpu.roll(halo, k, 0))
    rid = lax.broadcasted_iota(jnp.int32, (HALO, v.shape[1]), 0)
    head = jnp.where(rid < k, hh, r[0:HALO])
    return jnp.concatenate([head, r[HALO:]], axis=0)


def _shift_up(v, halo, k, last):
    n = v.shape[0]
    r = pltpu.roll(v, n - k, 0)
    hh = jnp.where(last, 0.0, pltpu.roll(halo, HALO - k, 0))
    rid = lax.broadcasted_iota(jnp.int32, (HALO, v.shape[1]), 0)
    tail = jnp.where(rid >= HALO - k, hh, r[n - HALO:])
    return jnp.concatenate([r[:n - HALO], tail], axis=0)


def _conv(p, halo, w_ref, b_ref, first):
    return (w_ref[2:3, :] * p + w_ref[1:2, :] * _shift_down(p, halo, 1, first)
            + w_ref[0:1, :] * _shift_down(p, halo, 2, first) + b_ref[...])


def _conv_specs(t, tr, tc):
    tile = pl.BlockSpec((tr, tc), lambda j, i: (i, j))
    prev = pl.BlockSpec((HALO, tc), lambda j, i: (jnp.maximum(i * (tr // HALO) - 1, 0), j))
    nxt = pl.BlockSpec((HALO, tc), lambda j, i: (jnp.minimum((i + 1) * (tr // HALO), t // HALO - 1), j))
    wspec = pl.BlockSpec((3, tc), lambda j, i: (0, j))
    bspec = pl.BlockSpec((1, tc), lambda j, i: (0, j))
    return tile, prev, nxt, wspec, bspec


def ffn_gate_fwd(pg, pu, wg, wu, bg, bu, *, name, tr=1024, tc=256):
    t, f = pg.shape
    tr = min(tr, t)
    tile, prev, _, wspec, bspec = _conv_specs(t, tr, tc)

    def body(pg_ref, pgh_ref, pu_ref, puh_ref, wg_ref, wu_ref, bg_ref, bu_ref, a_ref):
        first = pl.program_id(1) == 0
        gate = _conv(pg_ref[...], pgh_ref[...], wg_ref, bg_ref, first)
        up = _conv(pu_ref[...], puh_ref[...], wu_ref, bu_ref, first)
        a_ref[...] = (gate * jax.nn.sigmoid(gate) * up).astype(bf16)

    return pl.pallas_call(
        body, name=name, grid=(f // tc, t // tr),
        in_specs=[tile, prev, tile, prev, wspec, wspec, bspec, bspec],
        out_specs=tile, out_shape=jax.ShapeDtypeStruct((t, f), bf16),
        compiler_params=_params("parallel", "parallel"))(pg, pg, pu, pu, wg, wu, bg, bu)


def ffn_gate_bwd(pg, pu, da, wg, wu, bg, bu, *, name, tr=1024, tc=256):
    t, f = pg.shape
    tr = min(tr, t)
    tile, prev, _, wspec, bspec = _conv_specs(t, tr, tc)

    def body(pg_ref, pgh_ref, pu_ref, puh_ref, da_ref, wg_ref, wu_ref, bg_ref, bu_ref,
             dg_ref, du_ref, sg_ref, su_ref):
        i = pl.program_id(1)
        first = i == 0
        pgv, puv = pg_ref[...], pu_ref[...]
        pg1, pg2 = _shift_down(pgv, pgh_ref[...], 1, first), _shift_down(pgv, pgh_ref[...], 2, first)
        pu1, pu2 = _shift_down(puv, puh_ref[...], 1, first), _shift_down(puv, puh_ref[...], 2, first)
        gate = wg_ref[2:3, :] * pgv + wg_ref[1:2, :] * pg1 + wg_ref[0:1, :] * pg2 + bg_ref[...]
        up = wu_ref[2:3, :] * puv + wu_ref[1:2, :] * pu1 + wu_ref[0:1, :] * pu2 + bu_ref[...]
        sg = jax.nn.sigmoid(gate)
        dav = da_ref[...]
        dgate = dav * up * (sg * (1.0 + gate * (1.0 - sg)))
        dup = dav * gate * sg
        dg_ref[...] = dgate
        du_ref[...] = dup
        rid = lax.broadcasted_iota(jnp.int32, (8, tc), 0)
        for d, p0, p1, p2, s_ref in ((dgate, pgv, pg1, pg2, sg_ref), (dup, puv, pu1, pu2, su_ref)):
            sums = [jnp.sum(d * p2, axis=0, keepdims=True), jnp.sum(d * p1, axis=0, keepdims=True),
                    jnp.sum(d * p0, axis=0, keepdims=True), jnp.sum(d, axis=0, keepdims=True)]
            part = jnp.zeros((8, tc), f32)
            for k, sk in enumerate(sums):
                part = jnp.where(rid == k, sk, part)

            @pl.when(first)
            def _(s_ref=s_ref, part=part):
                s_ref[...] = part

            @pl.when(i > 0)
            def _(s_ref=s_ref, part=part):
                s_ref[...] += part

    stat = pl.BlockSpec((8, tc), lambda j, i: (0, j))
    return pl.pallas_call(
        body, name=name, grid=(f // tc, t // tr),
        in_specs=[tile, prev, tile, prev, tile, wspec, wspec, bspec, bspec],
        out_specs=[tile, tile, stat, stat],
        out_shape=[jax.ShapeDtypeStruct((t, f), f32), jax.ShapeDtypeStruct((t, f), f32),
                   jax.ShapeDtypeStruct((8, f), f32), jax.ShapeDtypeStruct((8, f), f32)],
        compiler_params=_params("parallel", "arbitrary"))(pg, pg, pu, pu, da, wg, wu, bg, bu)


def conv_input_grad(dhu, w, *, name, tr=1024, tc=256):
    t, f = dhu.shape
    tr = min(tr, t)
    tile, _, nxt, wspec, _ = _conv_specs(t, tr, tc)
    nsteps = t // tr

    def body(d_ref, dh_ref, w_ref, o_ref):
        last = pl.program_id(1) == nsteps - 1
        d = d_ref[...]
        hv = dh_ref[...]
        o_ref[...] = (w_ref[2:3, :] * d + w_ref[1:2, :] * _shift_up(d, hv, 1, last)
                      + w_ref[0:1, :] * _shift_up(d, hv, 2, last)).astype(bf16)

    return pl.pallas_call(
        body, name=name, grid=(f // tc, nsteps), in_specs=[tile, nxt, wspec],
        out_specs=tile, out_shape=jax.ShapeDtypeStruct((t, f), bf16),
        compiler_params=_params("parallel", "parallel"))(dhu, dhu, w)


def _head_mean_matrix():
    i = lax.broadcasted_iota(jnp.int32, (LANES, LANES), 0) // HEAD_DIM
    j = lax.broadcasted_iota(jnp.int32, (LANES, LANES), 1) // HEAD_DIM
    return jnp.where(i == j, 1.0 / HEAD_DIM, 0.0).astype(f32)


def _lane_half(shape):
    return (lax.broadcasted_iota(jnp.int32, shape, 1) % LANES) // HEAD_DIM


def q_norm_fwd(qp, g2, *, name, scale, tr=512):
    t, w = qp.shape
    tr = min(tr, t)

    def body(x_ref, g_ref, o_ref):
        bd = _head_mean_matrix()
        for cb in range(w // LANES):
            cols = slice(cb * LANES, (cb + 1) * LANES)
            xc = x_ref[:, cols]
            rh = lax.rsqrt(_dot_exact(xc * xc, bd) + EPS)
            o_ref[:, cols] = (xc * rh * g_ref[...] * scale).astype(bf16)

    row = pl.BlockSpec((tr, w), lambda i: (i, 0))
    return pl.pallas_call(
        body, name=name, grid=(t // tr,), in_specs=[row, pl.BlockSpec((1, LANES), lambda i: (0, 0))],
        out_specs=row, out_shape=jax.ShapeDtypeStruct((t, w), bf16),
        compiler_params=_params("parallel"))(qp, g2)


def q_norm_bwd(dq, qp, g2, *, name, scale, tr=512):
    t, w = qp.shape
    tr = min(tr, t)

    def body(dq_ref, x_ref, g_ref, o_ref, dg_ref):
        i = pl.program_id(0)
        bd = _head_mean_matrix()
        acc = jnp.zeros((1, LANES), f32)
        for cb in range(w // LANES):
            cols = slice(cb * LANES, (cb + 1) * LANES)
            xc = x_ref[:, cols]
            rh = lax.rsqrt(_dot_exact(xc * xc, bd) + EPS)
            xh = xc * rh
            dy = dq_ref[:, cols] * scale
            acc = acc + jnp.sum(dy * xh, axis=0, keepdims=True)
            tg = dy * g_ref[...]
            o_ref[:, cols] = (rh * (tg - xh * _dot_exact(tg * xh, bd))).astype(bf16)

        @pl.when(i == 0)
        def _():
            dg_ref[...] = acc

        @pl.when(i > 0)
        def _():
            dg_ref[...] += acc

    row = pl.BlockSpec((tr, w), lambda i: (i, 0))
    vec = pl.BlockSpec((1, LANES), lambda i: (0, 0))
    return pl.pallas_call(
        body, name=name, grid=(t // tr,), in_specs=[row, row, vec], out_specs=[row, vec],
        out_shape=[jax.ShapeDtypeStruct((t, w), bf16), jax.ShapeDtypeStruct((1, LANES), f32)],
        compiler_params=_params("arbitrary"))(dq, qp, g2)


def kv_post_fwd(kv, g2, *, name, tr=512):
    t, w = kv.shape
    tr = min(tr, t)
    kw = w // 2

    def body(x_ref, g_ref, k_ref, v_ref):
        bd = _head_mean_matrix()
        half = _lane_half((tr, LANES))
        for cb in range(kw // LANES):
            xc = x_ref[:, cb * LANES:(cb + 1) * LANES]
            rh = lax.rsqrt(_dot_exact(xc * xc, bd) + EPS)
            kn = xc * rh * g_ref[...]
            vc = x_ref[:, kw + cb * LANES:kw + (cb + 1) * LANES]
            for src, dst in ((kn, k_ref), (vc, v_ref)):
                sw = pltpu.roll(src, HEAD_DIM, 1)
                for hf in range(2):
                    blk = 2 * cb + hf
                    dst[:, blk * LANES:(blk + 1) * LANES] = jnp.where(half == hf, src, sw).astype(bf16)

    return pl.pallas_call(
        body, name=name, grid=(t // tr,),
        in_specs=[pl.BlockSpec((tr, w), lambda i: (i, 0)), pl.BlockSpec((1, LANES), lambda i: (0, 0))],
        out_specs=[pl.BlockSpec((tr, 2 * kw), lambda i: (i, 0))] * 2,
        out_shape=[jax.ShapeDtypeStruct((t, 2 * kw), bf16)] * 2,
        compiler_params=_params("parallel"))(kv, g2)


def kv_post_bwd(dk2, dv2, kv, g2, *, name, tr=512):
    t, w = kv.shape
    tr = min(tr, t)
    kw = w // 2

    def body(dk_ref, dv_ref, x_ref, g_ref, o_ref, dg_ref):
        i = pl.program_id(0)
        bd = _head_mean_matrix()
        half = _lane_half((tr, LANES))
        acc = jnp.zeros((1, LANES), f32)

        def fold(ref, cb):
            a = ref[:, (2 * cb) * LANES:(2 * cb + 1) * LANES]
            b = ref[:, (2 * cb + 1) * LANES:(2 * cb + 2) * LANES]
            return jnp.where(half == 0, a + pltpu.roll(a, HEAD_DIM, 1), b + pltpu.roll(b, HEAD_DIM, 1))

        for cb in range(kw // LANES):
            cols = slice(cb * LANES, (cb + 1) * LANES)
            xc = x_ref[:, cols]
            rh = lax.rsqrt(_dot_exact(xc * xc, bd) + EPS)
            xh = xc * rh
            dy = fold(dk_ref, cb)
            acc = acc + jnp.sum(dy * xh, axis=0, keepdims=True)
            tg = dy * g_ref[...]
            o_ref[:, cols] = (rh * (tg - xh * _dot_exact(tg * xh, bd))).astype(bf16)
            o_ref[:, kw + cb * LANES:kw + (cb + 1) * LANES] = fold(dv_ref, cb).astype(bf16)

        @pl.when(i == 0)
        def _():
            dg_ref[...] = acc

        @pl.when(i > 0)
        def _():
            dg_ref[...] += acc

    dup = pl.BlockSpec((tr, 2 * kw), lambda i: (i, 0))
    row = pl.BlockSpec((tr, w), lambda i: (i, 0))
    vec = pl.BlockSpec((1, LANES), lambda i: (0, 0))
    return pl.pallas_call(
        body, name=name, grid=(t // tr,), in_specs=[dup, dup, row, vec], out_specs=[row, vec],
        out_shape=[jax.ShapeDtypeStruct((t, w), bf16), jax.ShapeDtypeStruct((1, LANES), f32)],
        compiler_params=_params("arbitrary"))(dk2, dv2, kv, g2)


def _slope(h):
    return 2.0 ** (-8.0 * (h + 1) / N_Q_HEADS)


def _band_mask(n):
    tq = lax.broadcasted_iota(jnp.int32, (CHUNK, 2 * CHUNK), 0)
    jk = lax.broadcasted_iota(jnp.int32, (CHUNK, 2 * CHUNK), 1)
    dist = tq + CHUNK - jk
    ok = (dist >= 0) & (dist < CHUNK) & jnp.logical_not((n == 0) & (jk < CHUNK))
    return dist.astype(f32), ok


def _band(ref, n, kh):
    p0 = pl.multiple_of(jnp.maximum(n - 1, 0) * CHUNK, CHUNK)
    c0 = pl.multiple_of(n * CHUNK, CHUNK)
    cols = slice(kh * LANES, (kh + 1) * LANES)
    return jnp.concatenate([ref[pl.ds(p0, CHUNK), cols], ref[pl.ds(c0, CHUNK), cols]], axis=0)


def _softmax_band(qm, kband, dist, ok, slope, sink):
    s = _dot_nt(qm, kband)
    s = jnp.where(ok, s - slope * dist, -jnp.inf)
    m = jnp.maximum(jnp.max(s, axis=1, keepdims=True), sink)
    e = jnp.exp(s - m)
    es = jnp.exp(sink - m)
    den = jnp.sum(e, axis=1, keepdims=True) + es
    return e / den, es / den


def attn_fwd(q, k2, v2, sinks, *, name):
    t, w = q.shape
    nb = t // CHUNK

    def body(sink_ref, q_ref, k_ref, v_ref, o_ref):
        n = pl.program_id(0)
        dist, ok = _band_mask(n)
        half = _lane_half((CHUNK, LANES))
        for cb in range(w // LANES):
            qc = q_ref[:, cb * LANES:(cb + 1) * LANES].astype(f32)
            outs = []
            for hf in range(2):
                h = 2 * cb + hf
                kh = h // Q_PER_KV
                qm = jnp.where(half == hf, qc, 0.0).astype(bf16)
                p, _ = _softmax_band(qm, _band(k_ref, n, kh), dist, ok, _slope(h), sink_ref[h])
                outs.append(_dot(p.astype(bf16), _band(v_ref, n, kh)))
            o_ref[:, cb * LANES:(cb + 1) * LANES] = jnp.where(half == 0, outs[0], outs[1]).astype(bf16)

    full = pl.BlockSpec((t, k2.shape[1]), lambda n: (0, 0))
    return pl.pallas_call(
        body, name=name, grid=(nb,),
        in_specs=[pl.BlockSpec(memory_space=pltpu.SMEM), pl.BlockSpec((CHUNK, w), lambda n: (n, 0)), full, full],
        out_specs=pl.BlockSpec((CHUNK, w), lambda n: (n, 0)),
        out_shape=jax.ShapeDtypeStruct((t, w), bf16),
        compiler_params=_params("parallel"))(sinks, q, k2, v2)


def attn_bwd(q, k2, v2, do, sinks, *, name):
    t, w = q.shape
    nb = t // CHUNK
    kw = k2.shape[1]

    def body(sink_ref, q_ref, k_ref, v_ref, do_ref, dq_ref, dk_ref, dv_ref, ds_ref, kc_ref, vc_ref):
        n = pl.program_id(0)

        @pl.when(n == 0)
        def _():
            ds_ref[...] = jnp.zeros_like(ds_ref)
            kc_ref[...] = jnp.zeros_like(kc_ref)
            vc_ref[...] = jnp.zeros_like(vc_ref)
            dk_ref[...] = jnp.zeros_like(dk_ref)
            dv_ref[...] = jnp.zeros_like(dv_ref)

        @pl.when(n == nb)
        def _():
            dk_ref[...] = kc_ref[...]
            dv_ref[...] = vc_ref[...]

        @pl.when(n < nb)
        def _():
            dist, ok = _band_mask(n)
            half = _lane_half((CHUNK, LANES))
            lane = lax.broadcasted_iota(jnp.int32, (1, LANES), 1)
            sink_acc = jnp.zeros((1, LANES), f32)
            dkb = [jnp.zeros((2 * CHUNK, LANES), f32) for _ in range(N_KV_HEADS)]
            dvb = [jnp.zeros((2 * CHUNK, LANES), f32) for _ in range(N_KV_HEADS)]
            for cb in range(w // LANES):
                cols = slice(cb * LANES, (cb + 1) * LANES)
                qc = q_ref[:, cols].astype(f32)
                doc = do_ref[:, cols]
                dqs = []
                for hf in range(2):
                    h = 2 * cb + hf
                    kh = h // Q_PER_KV
                    qm = jnp.where(half == hf, qc, 0.0).astype(bf16)
                    dom = jnp.where(half == hf, doc, 0.0).astype(bf16)
                    kband = _band(k_ref, n, kh)
                    vband = _band(v_ref, n, kh)
                    p, ps = _softmax_band(qm, kband, dist, ok, _slope(h), sink_ref[h])
                    dp = _dot_nt(dom, vband)
                    delta = jnp.sum(p * dp, axis=1, keepdims=True)
                    dsb = (p * (dp - delta)).astype(bf16)
                    sink_acc = sink_acc + jnp.where(lane == h, -jnp.sum(ps * delta, axis=0, keepdims=True), 0.0)
                    dqs.append(_dot(dsb, kband))
                    dkb[kh] = dkb[kh] + _dot_tn(dsb, qm)
                    dvb[kh] = dvb[kh] + _dot_tn(p.astype(bf16), dom)
                dq_ref[:, cols] = jnp.where(half == 0, dqs[0], dqs[1])
            ds_ref[...] += sink_acc
            for kh in range(N_KV_HEADS):
                cols = slice(kh * LANES, (kh + 1) * LANES)
                dk_ref[:, cols] = kc_ref[:, cols] + dkb[kh][0:CHUNK]
                dv_ref[:, cols] = vc_ref[:, cols] + dvb[kh][0:CHUNK]
                kc_ref[:, cols] = dkb[kh][CHUNK:]
                vc_ref[:, cols] = dvb[kh][CHUNK:]

    full = pl.BlockSpec((t, kw), lambda n: (0, 0))
    qblk = pl.BlockSpec((CHUNK, w), lambda n: (jnp.minimum(n, nb - 1), 0))
    kblk = pl.BlockSpec((CHUNK, kw), lambda n: (jnp.maximum(n - 1, 0), 0))
    return pl.pallas_call(
        body, name=name, grid=(nb + 1,),
        in_specs=[pl.BlockSpec(memory_space=pltpu.SMEM), qblk, full, full, qblk],
        out_specs=[qblk, kblk, kblk, pl.BlockSpec((1, LANES), lambda n: (0, 0))],
        out_shape=[jax.ShapeDtypeStruct((t, w), f32), jax.ShapeDtypeStruct((t, kw), f32),
                   jax.ShapeDtypeStruct((t, kw), f32), jax.ShapeDtypeStruct((1, LANES), f32)],
        scratch_shapes=[pltpu.VMEM((CHUNK, kw), f32), pltpu.VMEM((CHUNK, kw), f32)],
        compiler_params=_params("arbitrary"))(sinks, q, k2, v2, do)


def loss_head(y, target, *, name, tr=512):
    t, d = y.shape
    tr = min(tr, t)

    def body(y_ref, t_ref, dy_ref, s_ref):
        i = pl.program_id(0)
        e = y_ref[...] - t_ref[...]
        dy_ref[...] = e * (1.0 / d)
        part = jnp.sum(e * e, axis=0, keepdims=True)

        @pl.when(i == 0)
        def _():
            s_ref[...] = part

        @pl.when(i > 0)
        def _():
            s_ref[...] += part

    row = pl.BlockSpec((tr, d), lambda i: (i, 0))
    vec = pl.BlockSpec((1, d), lambda i: (0, 0))
    return pl.pallas_call(
        body, name=name, grid=(t // tr,), in_specs=[row, row], out_specs=[row, vec],
        out_shape=[jax.ShapeDtypeStruct((t, d), f32), jax.ShapeDtypeStruct((1, d), f32)],
        compiler_params=_params("arbitrary"))(y, target)


N_STEPS = 8


def _row_blocks(shape):
    if len(shape) == 2:
        r, c = shape
        return (r // N_STEPS, c), (lambda s: (s, 0))
    l, r, c = shape
    per = N_STEPS // l
    return (1, r // per, c), (lambda s: (s // per, s % per, 0))


CAST_STEPS = 4


def cast_into_slot(arrays, k_arr, *, name):
    in_specs, out_specs, out_shape, layers = [], [], [], []
    for a in arrays:
        r, c = a.shape[-2:]
        rb = r // CAST_STEPS
        if a.ndim == 2:
            in_specs.append(pl.BlockSpec((rb, c), lambda s, k: (s, 0)))
            layers.append(None)
        else:
            for l in range(a.shape[0]):
                in_specs.append(pl.BlockSpec((1, rb, c), lambda s, k, l=l: (l, s, 0)))
                layers.append(l)
        for _ in range(1 if a.ndim == 2 else a.shape[0]):
            out_specs.append(pl.BlockSpec((1, rb, c), lambda s, k: (k[0], s, 0)))
            out_shape.append(jax.ShapeDtypeStruct((N_SHARDS, r, c), bf16))
    n = len(in_specs)

    def body(k_ref, *refs):
        for i_ref, o_ref, l in zip(refs[:n], refs[n:], layers):
            o_ref[0] = (i_ref[...] if l is None else i_ref[0]).astype(bf16)

    args = []
    for a in arrays:
        args += [a] * (1 if a.ndim == 2 else a.shape[0])
    return pl.pallas_call(
        body, name=name,
        grid_spec=pltpu.PrefetchScalarGridSpec(num_scalar_prefetch=1, grid=(CAST_STEPS,),
                                               in_specs=in_specs, out_specs=out_specs),
        out_shape=out_shape, compiler_params=_params("parallel"))(k_arr, *args)


def adamw(ws, gs, ms, vs, *, name):
    n = len(ws)
    specs, g_specs, g_count = [], [], []
    for w, g_list in zip(ws, gs):
        blk, index = _row_blocks(w.shape)
        specs.append(pl.BlockSpec(blk, index))
        layers = len(g_list)
        per = N_STEPS // layers
        g_count.append(layers)
        for l in range(layers):
            g_specs.append(pl.BlockSpec(blk[-2:], lambda s, l=l, per=per: (jnp.where(s // per == l, s % per, 0), 0)))
    ng = len(g_specs)

    def body(*refs):
        s = pl.program_id(0)
        g_refs = refs[3 * n:3 * n + ng]
        outs = refs[3 * n + ng:]
        off = 0
        for i in range(n):
            w_ref, m_ref, v_ref = refs[i], refs[n + i], refs[2 * n + i]
            go_ref, d_ref, nm_ref, nv_ref = (outs[k * n + i] for k in range(4))
            layers = g_count[i]
            g = g_refs[off][...]
            for l in range(1, layers):
                g = jnp.where(s // (N_STEPS // layers) == l, g_refs[off + l][...], g)
            off += layers
            g = g.reshape(w_ref.shape)
            m = ADAM_B1 * m_ref[...] + (1.0 - ADAM_B1) * g
            v = ADAM_B2 * v_ref[...] + (1.0 - ADAM_B2) * (g * g)
            m_hat = m / ADAM_C1
            v_hat = v / ADAM_C2
            go_ref[...] = g
            d_ref[...] = -ADAM_LR * (m_hat / (jnp.sqrt(v_hat) + ADAM_EPS) + ADAM_WD * w_ref[...])
            nm_ref[...] = m
            nv_ref[...] = v

    outs = pl.pallas_call(
        body, name=name, grid=(N_STEPS,), in_specs=specs * 3 + g_specs, out_specs=specs * 4,
        out_shape=[jax.ShapeDtypeStruct(a.shape, f32) for a in ws] * 4,
        compiler_params=_params("parallel"))(*ws, *ms, *vs, *[g for g_list in gs for g in g_list])
    return [outs[k * n:(k + 1) * n] for k in range(4)]


def _place():
    return lax.axis_index("x"), lax.axis_index("y"), lax.axis_index("c")


def gather_shards(bufs, *, name, split):
    n = len(bufs)

    def body(*refs):
        bufs_ = refs[:n]
        isend, irecv, dsend, drecv = refs[2 * n:]
        x, y, c = _place()
        k = 2 * x + y
        peers = [(1 - x, y, c), (x, 1 - y, c), (1 - x, 1 - y, c)]
        peer_k = [2 * (1 - x) + y, 2 * x + (1 - y), 2 * (1 - x) + (1 - y)]

        def slab(a, q, h):
            if not split[a]:
                return bufs_[a].at[q]
            half = bufs_[a].shape[1] // 2
            return bufs_[a].at[q, pl.ds(pl.multiple_of(h * half, 16), half)]

        def ici(a, j, q):
            return pltpu.make_async_remote_copy(
                src_ref=slab(a, q, c), dst_ref=slab(a, q, c), send_sem=isend.at[3 * a + j], recv_sem=irecv.at[3 * a + j],
                device_id=peers[j], device_id_type=MESH)

        def d2d(a, j, h):
            return pltpu.make_async_remote_copy(
                src_ref=slab(a, peer_k[j], h), dst_ref=slab(a, peer_k[j], h), send_sem=dsend.at[3 * a + j],
                recv_sem=drecv.at[3 * a + j], device_id=(x, y, 1 - c), device_id_type=MESH)

        for a in range(n):
            for j in range(3):
                ici(a, j, k).start()
        for a in range(n):
            for j in range(3):
                ici(a, j, peer_k[j]).wait_recv()
                if split[a]:
                    d2d(a, j, c).start()
        for a in range(n):
            for j in range(3):
                if split[a]:
                    d2d(a, j, 1 - c).wait_recv()
        for a in range(n):
            for j in range(3):
                ici(a, j, k).wait_send()
                if split[a]:
                    d2d(a, j, c).wait_send()

    return pl.pallas_call(
        body, name=name, in_specs=[ANY] * n, out_specs=[ANY] * n,
        out_shape=[jax.ShapeDtypeStruct(b.shape, b.dtype) for b in bufs],
        input_output_aliases={i: i for i in range(n)},
        scratch_shapes=[pltpu.SemaphoreType.DMA((3 * n,))] * 4)(*bufs)


def sibling_exchange(arrays, *, name):
    n = len(arrays)

    def body(*refs):
        ins, outs = refs[:n], refs[n:2 * n]
        send, recv = refs[2 * n:]
        x, y, c = _place()

        def copy(a):
            return pltpu.make_async_remote_copy(
                src_ref=ins[a], dst_ref=outs[a], send_sem=send.at[a], recv_sem=recv.at[a],
                device_id=(x, y, 1 - c), device_id_type=MESH)

        for a in range(n):
            copy(a).start()
        for a in range(n):
            copy(a).wait_recv()
        for a in range(n):
            copy(a).wait_send()

    return pl.pallas_call(
        body, name=name, in_specs=[ANY] * n, out_specs=[ANY] * n,
        out_shape=[jax.ShapeDtypeStruct(a.shape, a.dtype) for a in arrays],
        scratch_shapes=[pltpu.SemaphoreType.DMA((n,)), pltpu.SemaphoreType.DMA((n,))])(*arrays)


def chip_scatter(arrays, *, name):
    n = len(arrays)

    def body(*refs):
        ins, outs = refs[:n], refs[n:2 * n]
        send, recv = refs[2 * n:]
        x, y, c = _place()
        peers = [(1 - x, y, c), (x, 1 - y, c), (1 - x, 1 - y, c)]
        peer_k = [2 * (1 - x) + y, 2 * x + (1 - y), 2 * (1 - x) + (1 - y)]

        def copy(a, j):
            return pltpu.make_async_remote_copy(
                src_ref=ins[a].at[peer_k[j]], dst_ref=outs[a].at[j], send_sem=send.at[3 * a + j],
                recv_sem=recv.at[3 * a + j], device_id=peers[j], device_id_type=MESH)

        for a in range(n):
            for j in range(3):
                copy(a, j).start()
        for a in range(n):
            for j in range(3):
                copy(a, j).wait_recv()
        for a in range(n):
            for j in range(3):
                copy(a, j).wait_send()

    return pl.pallas_call(
        body, name=name, in_specs=[ANY] * n, out_specs=[ANY] * n,
        out_shape=[jax.ShapeDtypeStruct((3,) + a.shape[1:], a.dtype) for a in arrays],
        scratch_shapes=[pltpu.SemaphoreType.DMA((3 * n,)), pltpu.SemaphoreType.DMA((3 * n,))])(*arrays)


def sibling_merge(bufs, *, name):
    n = len(bufs)

    def body(*refs):
        bufs_ = refs[:n]
        send, recv = refs[2 * n:]
        x, y, c = _place()

        def copy(u, h):
            return pltpu.make_async_remote_copy(
                src_ref=bufs_[u].at[h], dst_ref=bufs_[u].at[h], send_sem=send.at[u], recv_sem=recv.at[u],
                device_id=(x, y, 1 - c), device_id_type=MESH)

        for u in range(n):
            copy(u, c).start()
        for u in range(n):
            copy(u, 1 - c).wait_recv()
        for u in range(n):
            copy(u, c).wait_send()

    return pl.pallas_call(
        body, name=name, in_specs=[ANY] * n, out_specs=[ANY] * n,
        out_shape=[jax.ShapeDtypeStruct(b.shape, b.dtype) for b in bufs],
        input_output_aliases={i: i for i in range(n)},
        scratch_shapes=[pltpu.SemaphoreType.DMA((n,)), pltpu.SemaphoreType.DMA((n,))])(*bufs)


def all_gather_rows(a, *, name):
    def body(a_ref, o_ref, send, recv, loc):
        x, y, c = _place()
        me = 4 * x + 2 * y + c
        masks = [(mx, my, mc) for mx in (0, 1) for my in (0, 1) for mc in (0, 1)][1:]

        def peer(mk):
            return (x ^ mk[0], y ^ mk[1], c ^ mk[2])

        def copy(j, slot):
            return pltpu.make_async_remote_copy(
                src_ref=a_ref, dst_ref=o_ref.at[slot], send_sem=send.at[j], recv_sem=recv.at[j],
                device_id=peer(masks[j]), device_id_type=MESH)

        local = pltpu.make_async_copy(a_ref, o_ref.at[me], loc)
        local.start()
        for j in range(7):
            copy(j, me).start()
        for j in range(7):
            px, py, pc = peer(masks[j])
            copy(j, 4 * px + 2 * py + pc).wait_recv()
        for j in range(7):
            copy(j, me).wait_send()
        local.wait()

    return pl.pallas_call(
        body, name=name, in_specs=[ANY], out_specs=ANY,
        out_shape=jax.ShapeDtypeStruct((N_DEV,) + a.shape, a.dtype),
        scratch_shapes=[pltpu.SemaphoreType.DMA((7,)), pltpu.SemaphoreType.DMA((7,)), pltpu.SemaphoreType.DMA])(a)


def sum_leading(a, *, name):
    n, r, c = a.shape

    def body(a_ref, o_ref):
        acc = a_ref[0]
        for i in range(1, n):
            acc = acc + a_ref[i]
        o_ref[...] = acc

    rb = r // 2 if r % 16 == 0 else r
    return pl.pallas_call(
        body, name=name, grid=(r // rb,), in_specs=[pl.BlockSpec((n, rb, c), lambda i: (0, i, 0))],
        out_specs=pl.BlockSpec((rb, c), lambda i: (i, 0)), out_shape=jax.ShapeDtypeStruct((r, c), f32),
        compiler_params=_params("parallel"))(a)


def _half_rows(shape):
    return shape[1] // 2 // 2


def rs_cast_other_half(grads, c_arr, *, name):
    n = len(grads)

    def body(c_ref, *refs):
        for i_ref, o_ref in zip(refs[:n], refs[n:]):
            o_ref[...] = i_ref[...].astype(bf16)

    in_specs = [pl.BlockSpec((1, _half_rows(g.shape), g.shape[2]), lambda s, r, c_ref: (s, (1 - c_ref[0]) * 2 + r, 0))
                for g in grads]
    out_specs = [pl.BlockSpec((1, _half_rows(g.shape), g.shape[2]), lambda s, r, c_ref: (s, r, 0)) for g in grads]
    return pl.pallas_call(
        body, name=name,
        grid_spec=pltpu.PrefetchScalarGridSpec(num_scalar_prefetch=1, grid=(N_SHARDS, 2),
                                               in_specs=in_specs, out_specs=out_specs),
        out_shape=[jax.ShapeDtypeStruct((N_SHARDS, g.shape[1] // 2, g.shape[2]), bf16) for g in grads],
        compiler_params=_params("parallel", "parallel"))(c_arr, *grads)


def rs_add_sibling(grads, recvd, ck_arr, *, name):
    n = len(grads)

    def body(ck_ref, *refs):
        s = pl.program_id(1)
        for u in range(n):
            g_ref, r_ref = refs[u], refs[n + u]
            qb_ref, own_ref = refs[2 * n + u], refs[3 * n + u]
            q = g_ref[0] + r_ref[0].astype(f32)
            qb_ref[0] = q.astype(bf16)

            @pl.when(s == ck_ref[1])
            def _(own_ref=own_ref, q=q):
                own_ref[...] = q

    in_specs = [pl.BlockSpec((1, _half_rows(g.shape), g.shape[2]), lambda r, s, ck: (s, ck[0] * 2 + r, 0)) for g in grads]
    in_specs += [pl.BlockSpec((1, _half_rows(g.shape), g.shape[2]), lambda r, s, ck: (s, r, 0)) for g in grads]
    out_specs = [pl.BlockSpec((1, _half_rows(g.shape), g.shape[2]), lambda r, s, ck: (s, r, 0)) for g in grads]
    out_specs += [pl.BlockSpec((_half_rows(g.shape), g.shape[2]), lambda r, s, ck: (r, 0)) for g in grads]
    outs = pl.pallas_call(
        body, name=name,
        grid_spec=pltpu.PrefetchScalarGridSpec(num_scalar_prefetch=1, grid=(2, N_SHARDS),
                                               in_specs=in_specs, out_specs=out_specs),
        out_shape=[jax.ShapeDtypeStruct((N_SHARDS, g.shape[1] // 2, g.shape[2]), bf16) for g in grads]
        + [jax.ShapeDtypeStruct((g.shape[1] // 2, g.shape[2]), f32) for g in grads],
        compiler_params=_params("parallel", "arbitrary"))(ck_arr, *grads, *recvd)
    return outs[:n], outs[n:]


def rs_sum_chips(owns, recvd, ck_arr, *, name):
    n = len(owns)

    def body(ck_ref, *refs):
        for u in range(n):
            own_ref, r_ref, o_ref = refs[u], refs[n + u], refs[2 * n + u]
            o_ref[0] = ((own_ref[...] + r_ref[0].astype(f32)) + r_ref[1].astype(f32)) + r_ref[2].astype(f32)

    in_specs = [pl.BlockSpec((o.shape[0] // 2, o.shape[1]), lambda r, ck: (r, 0)) for o in owns]
    in_specs += [pl.BlockSpec((3, o.shape[0] // 2, o.shape[1]), lambda r, ck: (0, r, 0)) for o in owns]
    out_specs = [pl.BlockSpec((1, o.shape[0] // 2, o.shape[1]), lambda r, ck: (ck[0], r, 0)) for o in owns]
    return pl.pallas_call(
        body, name=name,
        grid_spec=pltpu.PrefetchScalarGridSpec(num_scalar_prefetch=1, grid=(2,), in_specs=in_specs, out_specs=out_specs),
        out_shape=[jax.ShapeDtypeStruct((2,) + o.shape, f32) for o in owns],
        compiler_params=_params("parallel"))(ck_arr, *owns, *recvd)


SMALL = ("a_norm", "a_v_norm", "a_w_s", "a_b_s", "f_norm", "f_conv_w", "f_conv_b", "kv_norm", "k_norm",
         "b_norm", "b_q_norm", "b_sinks")
BIG = ("a_w_in", "a_w_out", "f_w_in", "f_w_out", "w_kv", "b_w_q", "b_w_o")
PACK_COLS = 1024
PACK_ROWS = 8 * N_STEPS


def _pack(parts, rows=PACK_ROWS):
    flat = jnp.concatenate([p.reshape(-1).astype(f32) for p in parts])
    pad = (-flat.shape[0]) % (rows * PACK_COLS)
    return jnp.pad(flat, (0, pad)).reshape(-1, PACK_COLS)


def _unpack(packed, shapes):
    flat = packed.reshape(-1)
    out, off = [], 0
    for s in shapes:
        size = math.prod(s)
        out.append(flat[off:off + size].reshape(s))
        off += size
    return out


def _ffn_fwd(x, g, w_in4, conv_w, conv_b, w_out, tag):
    f = w_out.shape[0]
    (h,), r = rms_fwd(x, [g], name=f"ffn{tag}_norm")
    pg = mm_nn(h, w_in4, name=f"ffn{tag}_in_gate", s0=0, ns=2)
    pu = mm_nn(h, w_in4, name=f"ffn{tag}_in_up", s0=2, ns=2)
    wg, wu = conv_w[:, :f], conv_w[:, f:]
    bg, bu = conv_b[None, :f], conv_b[None, f:]
    a = ffn_gate_fwd(pg, pu, wg, wu, bg, bu, name=f"ffn{tag}_gate")
    y = mm_nn(a, w_out[None], name=f"ffn{tag}_out", add=x)
    return y, (x, g, h, r, pg, pu, a, wg, wu, bg, bu)


def _ffn_bwd(dy, saved, w_in4, w_out, tag):
    x, g, h, r, pg, pu, a, wg, wu, bg, bu = saved
    f = w_out.shape[0]
    da = mm_nt([dy], w_out[None], name=f"ffn{tag}_dact", tko=f // 2)
    d_w_out = mm_tn(a, [dy], name=f"ffn{tag}_dwout", n_s=w_out.shape[1], tki=f // 2)
    dgate, dup, sg, su = ffn_gate_bwd(pg, pu, da, wg, wu, bg, bu, name=f"ffn{tag}_dgate")
    dpg = conv_input_grad(dgate, wg, name=f"ffn{tag}_dconv_gate")
    dpu = conv_input_grad(dup, wu, name=f"ffn{tag}_dconv_up")
    d_w_in = mm_tn(h, [dpg, dpu], name=f"ffn{tag}_dwin", n_s=w_in4.shape[2])
    dh = mm_nt([dpg, dpu], w_in4, name=f"ffn{tag}_dh")
    dx, (dg,) = rms_bwd([dh], x, r, [g], dy, name=f"ffn{tag}_dnorm")
    d_conv_w = jnp.concatenate([sg[0:3], su[0:3]], axis=1)
    d_conv_b = jnp.concatenate([sg[3], su[3]], axis=0)
    return dx, dg, d_w_in, d_conv_w, d_conv_b, d_w_out


def kernel(x, a_norm, a_w_in, a_v_norm, a_w_s, a_b_s, a_w_out, f_norm, f_w_in, f_conv_w, f_conv_b, f_w_out, kv_norm, w_kv, k_norm, b_norm, b_w_q, b_q_norm, b_sinks, b_w_o, loss_target, m_a_norm, m_a_w_in, m_a_v_norm, m_a_w_s, m_a_b_s, m_a_w_out, m_f_norm, m_f_w_in, m_f_conv_w, m_f_conv_b, m_f_w_out, m_kv_norm, m_w_kv, m_k_norm, m_b_norm, m_b_w_q, m_b_q_norm, m_b_sinks, m_b_w_o, v_a_norm, v_a_w_in, v_a_v_norm, v_a_w_s, v_a_b_s, v_a_w_out, v_f_norm, v_f_w_in, v_f_conv_w, v_f_conv_b, v_f_w_out, v_kv_norm, v_w_kv, v_k_norm, v_b_norm, v_b_w_q, v_b_q_norm, v_b_sinks, v_b_w_o):
    args = dict(locals())
    weights = {n: args[n] for n in SMALL + BIG}
    moms = {n: args["m_" + n] for n in SMALL + BIG}
    vars_ = {n: args["v_" + n] for n in SMALL + BIG}
    t, d = x.shape[1], x.shape[2]
    xi, yi, ci = _place()
    chip = 2 * xi + yi

    big_local = [a_w_in[0], a_w_out[0], f_w_in, f_w_out, w_kv, b_w_q[0], b_w_o[0]]
    c_arr = jnp.stack([ci, chip]).astype(jnp.int32)
    k_arr = jnp.stack([chip]).astype(jnp.int32)
    b_ain, b_aout, b_fin0, b_fin1, b_fout0, b_fout1, b_kv, b_q, b_o = cast_into_slot(big_local, k_arr, name="cast_weights")
    small_cols = _pack([a_norm, a_v_norm, f_conv_w], rows=8)
    b_small = lax.dynamic_update_slice(jnp.zeros((N_SHARDS,) + small_cols.shape, f32), small_cols[None], (chip, 0, 0))
    gathered = gather_shards([b_small, b_ain, b_aout, b_fin0, b_fout0, b_kv, b_q, b_o, b_fin1, b_fout1],
                             name="gather_weights", split=[False] + [True] * 9)
    g_small, g_a_w_in, g_a_w_out, g_fin0, g_fout0, g_w_kv, g_b_w_q, g_b_w_o, g_fin1, g_fout1 = gathered
    ns_cols = a_norm.shape[1]
    nf_cols = f_conv_w.shape[2]
    parts = [_unpack(g_small[k], [a_norm.shape, a_v_norm.shape, f_conv_w.shape]) for k in range(N_SHARDS)]
    a_norm_f = jnp.concatenate([p[0] for p in parts], axis=1)
    a_v_norm_f = jnp.concatenate([p[1] for p in parts], axis=1)
    conv_w_f = jnp.concatenate([p[2] for p in parts], axis=2)
    w_a_in = g_a_w_in
    w_a_out = g_a_w_out.reshape(1, -1, d)
    w_f_in = [g_fin0, g_fin1]
    w_f_out = [g_fout0.reshape(-1, d), g_fout1.reshape(-1, d)]
    w_kv_f = g_w_kv.reshape(1, d, -1)
    w_q_f = g_b_w_q.reshape(1, d, -1)
    w_o_f = g_b_w_o.reshape(1, -1, d)

    x0 = x[0]
    tril = jnp.tril(jnp.ones((CHUNK, CHUNK), dtype=bool))
    wc = jnp.where(tril[None], a_w_s[0], 0.0).astype(bf16)
    bt = a_b_s[0].T
    kg2 = jnp.tile(k_norm, 2)[None]
    qg2 = jnp.tile(b_q_norm[0], 2)[None]

    (h_a,), r_a = rms_fwd(x0, [a_norm_f], name="a_norm")
    zu = mm_nn(h_a, w_a_in, name="a_in_u", s0=0, ns=2)
    zv = mm_nn(h_a, w_a_in, name="a_in_v", s0=2, ns=2)
    y_a = sgu_gate_fwd(zu, zv, a_v_norm_f, wc, bt, name="a_gate")
    x1 = mm_nn(y_a, w_a_out, name="a_out", add=x0)
    x2, ffn0 = _ffn_fwd(x1, f_norm[0:1], w_f_in[0], conv_w_f[0], f_conv_b[0], w_f_out[0], "0")
    (h_k, h_q), r_b = rms_fwd(x2, [kv_norm[None], b_norm], name="b_norm")
    kv = mm_nn(h_k, w_kv_f, name="kv_proj")
    k2, v2 = kv_post_fwd(kv, kg2, name="kv_post")
    qp = mm_nn(h_q, w_q_f, name="q_proj")
    qn = q_norm_fwd(qp, qg2, name="q_norm", scale=HEAD_DIM ** -0.5)
    o = attn_fwd(qn, k2, v2, b_sinks[0], name="attn")
    x3 = mm_nn(o, w_o_f, name="o_proj", add=x2)
    x4, ffn1 = _ffn_fwd(x3, f_norm[1:2], w_f_in[1], conv_w_f[1], f_conv_b[1], w_f_out[1], "1")
    dx4, sq = loss_head(x4, loss_target[0], name="loss")
    loss = lax.psum(0.5 * jnp.sum(sq) / d, ("x", "y", "c"))

    dx3, d_fn1, d_fwin1, d_cw1, d_cb1, d_fwout1 = _ffn_bwd(dx4, ffn1, w_f_in[1], w_f_out[1], "1")
    do = mm_nt([dx3], w_o_f, name="o_proj_dx")
    d_w_o = mm_tn(o, [dx3], name="o_proj_dw", n_s=d)
    dqn, dk2, dv2, dsink = attn_bwd(qn, k2, v2, do, b_sinks[0], name="attn_bwd")
    dqp, dqg = q_norm_bwd(dqn, qp, qg2, name="q_norm_bwd", scale=HEAD_DIM ** -0.5)
    dkv, dkg = kv_post_bwd(dk2, dv2, kv, kg2, name="kv_post_bwd")
    d_w_q = mm_tn(h_q, [dqp], name="q_proj_dw", n_s=w_q_f.shape[2])
    dh_q = mm_nt([dqp], w_q_f, name="q_proj_dx")
    d_w_kv = mm_tn(h_k, [dkv], name="kv_proj_dw", n_s=w_kv_f.shape[2])
    dh_k = mm_nt([dkv], w_kv_f, name="kv_proj_dx")
    dx2, (d_kvn, d_bn) = rms_bwd([dh_k, dh_q], x2, r_b, [kv_norm[None], b_norm], dx3, name="b_norm_bwd")
    dx1, d_fn0, d_fwin0, d_cw0, d_cb0, d_fwout0 = _ffn_bwd(dx2, ffn0, w_f_in[0], w_f_out[0], "0")
    dy_a = mm_nt([dx1], w_a_out, name="a_out_dx")
    d_w_aout = mm_tn(y_a, [dx1], name="a_out_dw", n_s=d)
    dzu, dzv, d_avn, d_ws, d_bt = sgu_gate_bwd(zu, zv, dy_a, a_v_norm_f, wc, bt, name="a_gate_bwd")
    d_w_ain = mm_tn(h_a, [dzu, dzv], name="a_in_dw", n_s=w_a_in.shape[2])
    dh_a = mm_nt([dzu, dzv], w_a_in, name="a_in_dx")
    dx0, (d_an,) = rms_bwd([dh_a], x0, r_a, [a_norm_f], dx1, name="a_norm_bwd")
    grad_x = dx0[None]

    sh = N_SHARDS
    units = [d_w_ain, d_w_aout.reshape(sh, -1, d), d_fwin0, d_fwin1, d_fwout0.reshape(sh, -1, d),
             d_fwout1.reshape(sh, -1, d), d_w_kv.reshape(sh, -1, d_w_kv.shape[2]), d_w_q.reshape(sh, -1, d_w_q.shape[2]),
             d_w_o.reshape(sh, -1, d)]
    other_bf = rs_cast_other_half(units, c_arr, name="rs_cast")
    from_sib = sibling_exchange(list(other_bf), name="rs_sibling")
    chip_bf, own = rs_add_sibling(units, from_sib, c_arr, name="rs_add")
    from_chips = chip_scatter(list(chip_bf), name="rs_chips")
    halves = rs_sum_chips(list(own), from_chips, c_arr, name="rs_sum")
    mg = [m.reshape(-1, m.shape[2]) for m in sibling_merge(list(halves), name="rs_merge")]
    big_w = [weights[n] for n in BIG]
    big_gs = [[mg[0]], [mg[1]], [mg[2], mg[3]], [mg[4], mg[5]], [mg[6]], [mg[7]], [mg[8]]]
    big_g, big_d, big_m, big_v = adamw(big_w, big_gs, [moms[n] for n in BIG], [vars_[n] for n in BIG], name="adamw_big")

    d_fn = jnp.concatenate([d_fn0, d_fn1], axis=0)
    d_cw = jnp.stack([d_cw0, d_cw1])
    d_cb = jnp.stack([d_cb0, d_cb1])
    d_kg = (dkg[0, :HEAD_DIM] + dkg[0, HEAD_DIM:])
    d_qg = (dqg[0, :HEAD_DIM] + dqg[0, HEAD_DIM:])[None]
    small_full = [d_an, d_avn, d_ws[None], d_bt.T[None], d_fn, d_cw, d_cb, d_kvn[0], d_kg, d_bn, d_qg,
                  dsink[:, :N_Q_HEADS]]
    full_shapes = [g.shape for g in small_full]
    packed = _pack(small_full)
    summed = sum_leading(all_gather_rows(packed, name="small_gather"), name="small_sum")
    small_g = _unpack(summed, full_shapes)
    small_g[0] = lax.dynamic_slice_in_dim(small_g[0], chip * ns_cols, ns_cols, axis=1)
    small_g[1] = lax.dynamic_slice_in_dim(small_g[1], chip * ns_cols, ns_cols, axis=1)
    small_g[5] = lax.dynamic_slice_in_dim(small_g[5], chip * nf_cols, nf_cols, axis=2)
    small_shapes = [weights[n].shape for n in SMALL]
    small_g = [g.reshape(s) for g, s in zip(small_g, small_shapes)]
    pw, pg_, pm, pv = (_pack(v) for v in ([weights[n] for n in SMALL], small_g, [moms[n] for n in SMALL],
                                          [vars_[n] for n in SMALL]))
    _, (sd,), (sm,), (sv,) = adamw([pw], [[pg_]], [pm], [pv], name="adamw_small")
    small_d, small_m, small_v = (_unpack(v, small_shapes) for v in (sd, sm, sv))

    out = {}
    for i, n in enumerate(SMALL):
        out[n] = (small_g[i], small_d[i], small_m[i], small_v[i])
    for i, n in enumerate(BIG):
        out[n] = (big_g[i], big_d[i], big_m[i], big_v[i])
    order = ["a_norm", "a_w_in", "a_v_norm", "a_w_s", "a_b_s", "a_w_out", "f_norm", "f_w_in", "f_conv_w", "f_conv_b",
             "f_w_out", "kv_norm", "w_kv", "k_norm", "b_norm", "b_w_q", "b_q_norm", "b_sinks", "b_w_o"]
    return (loss, grad_x, *[out[n][0] for n in order], *[out[n][1] for n in order],
            *[out[n][2] for n in order], *[out[n][3] for n in order])
```

```python
import functools
import math

import jax
import jax.numpy as jnp
from jax import lax
from jax.experimental import pallas as pl
from jax.experimental.pallas import tpu as pltpu

f32 = jnp.float32
bf16 = jnp.bfloat16
MESH = pl.DeviceIdType.MESH
ANY = pl.BlockSpec(memory_space=pl.ANY)

EPS = 1e-6
LANES = 128
CHUNK = 128
HEAD_DIM = 64
N_Q_HEADS = 16
N_KV_HEADS = 4
Q_PER_KV = N_Q_HEADS // N_KV_HEADS
N_SHARDS = 4
N_DEV = 8

ADAM_LR = 0.001
ADAM_B1 = 0.9
ADAM_B2 = 0.999
ADAM_EPS = 1e-08
ADAM_WD = 0.01
ADAM_STEP = 10
ADAM_C1 = 1.0 - ADAM_B1 ** ADAM_STEP
ADAM_C2 = 1.0 - ADAM_B2 ** ADAM_STEP

_INV_SQRT2 = 1.0 / math.sqrt(2.0)
_INV_SQRT2PI = 1.0 / math.sqrt(2.0 * math.pi)


def _params(*sem):
    return pltpu.CompilerParams(dimension_semantics=sem)


def _gelu(z):
    return 0.5 * z * (1.0 + lax.erf(z * _INV_SQRT2))


def _gelu_grad(z):
    return 0.5 * (1.0 + lax.erf(z * _INV_SQRT2)) + z * jnp.exp(-0.5 * z * z) * _INV_SQRT2PI


def _dot(a, b):
    return jnp.dot(a, b, preferred_element_type=f32)


def _dot_nt(a, b):
    return lax.dot_general(a, b, (((1,), (1,)), ((), ())), preferred_element_type=f32)


def _dot_tn(a, b):
    return lax.dot_general(a, b, (((0,), (0,)), ((), ())), preferred_element_type=f32)


def _dot_exact(a, b):
    return jnp.dot(a, b, preferred_element_type=f32, precision=lax.Precision.HIGHEST)


VMEM_TILE_BUDGET = 36 * 1024 * 1024
MAX_ROW_TILE = 2048


def _row_tile(m, fixed_bytes, row_bytes):
    tm = min(m, MAX_ROW_TILE)
    while tm > 256 and 2 * (fixed_bytes + tm * row_bytes) > VMEM_TILE_BUDGET:
        tm //= 2
    return tm


def _isz(a):
    return jnp.dtype(a.dtype).itemsize


def mm_nn(a, w3, *, name, s0=0, ns=None, add=None, out_dtype=f32):
    m, k = a.shape
    s_all, _, n_s = w3.shape
    ns = s_all if ns is None else ns
    tm = _row_tile(m, k * n_s * 2, k * _isz(a) + n_s * jnp.dtype(out_dtype).itemsize + (0 if add is None else n_s * 4))

    def body(*refs):
        if add is None:
            a_ref, w_ref, o_ref = refs
            acc = _dot(a_ref[...].astype(bf16), w_ref[0])
        else:
            a_ref, w_ref, add_ref, o_ref = refs
            acc = _dot(a_ref[...].astype(bf16), w_ref[0]) + add_ref[...]
        o_ref[...] = acc.astype(out_dtype)

    in_specs = [pl.BlockSpec((tm, k), lambda j, i: (i, 0)),
                pl.BlockSpec((1, k, n_s), lambda j, i: (s0 + j, 0, 0))]
    args = [a, w3]
    if add is not None:
        in_specs.append(pl.BlockSpec((tm, n_s), lambda j, i: (i, j)))
        args.append(add)
    return pl.pallas_call(
        body, name=name, grid=(ns, m // tm), in_specs=in_specs,
        out_specs=pl.BlockSpec((tm, n_s), lambda j, i: (i, j)),
        out_shape=jax.ShapeDtypeStruct((m, ns * n_s), out_dtype),
        compiler_params=_params("parallel", "parallel"))(*args)


def mm_nt(a_list, w3, *, name, tko=None, add=None, out_dtype=f32):
    s_all, k_out, n_s = w3.shape
    m = a_list[0].shape[0]
    na = len(a_list)
    spa = s_all // na
    tko = k_out if tko is None else tko
    tm = _row_tile(m, tko * n_s * 2, na * n_s * _isz(a_list[0]) + tko * 4 * (1 if add is None else 2))

    def body(*refs):
        a_refs = refs[:na]
        w_ref = refs[na]
        o_ref = refs[-1]
        s = pl.program_id(2)

        @pl.when(s == 0)
        def _():
            if add is None:
                o_ref[...] = jnp.zeros_like(o_ref)
            else:
                o_ref[...] = refs[na + 1][...]

        for idx in range(na):
            @pl.when(s // spa == idx)
            def _(idx=idx):
                o_ref[...] += _dot_nt(a_refs[idx][...].astype(bf16), w_ref[0])

    def a_map(idx):
        return lambda ko, i, s: (i, jnp.clip(s - idx * spa, 0, spa - 1))

    in_specs = [pl.BlockSpec((tm, n_s), a_map(idx)) for idx in range(na)]
    in_specs.append(pl.BlockSpec((1, tko, n_s), lambda ko, i, s: (s, ko, 0)))
    args = list(a_list) + [w3]
    if add is not None:
        in_specs.append(pl.BlockSpec((tm, tko), lambda ko, i, s: (i, ko)))
        args.append(add)
    return pl.pallas_call(
        body, name=name, grid=(k_out // tko, m // tm, s_all), in_specs=in_specs,
        out_specs=pl.BlockSpec((tm, tko), lambda ko, i, s: (i, ko)),
        out_shape=jax.ShapeDtypeStruct((m, k_out), out_dtype),
        compiler_params=_params("parallel", "parallel", "arbitrary"))(*args)


def mm_tn(a, b_list, *, name, n_s, tki=None):
    m, k_in = a.shape
    na = len(b_list)
    s_all = sum(b.shape[1] for b in b_list) // n_s
    spa = s_all // na
    tki = k_in if tki is None else tki
    tm = _row_tile(m, tki * n_s * 4, tki * _isz(a) + na * n_s * _isz(b_list[0]))

    def body(*refs):
        a_ref = refs[0]
        b_refs = refs[1:1 + na]
        o_ref = refs[-1]
        s = pl.program_id(0)
        r = pl.program_id(2)

        @pl.when(r == 0)
        def _():
            o_ref[...] = jnp.zeros_like(o_ref)

        for idx in range(na):
            @pl.when(s // spa == idx)
            def _(idx=idx):
                o_ref[0] += _dot_tn(a_ref[...].astype(bf16), b_refs[idx][...].astype(bf16))

    def b_map(idx):
        def index(s, ki, r):
            active = (s // spa) == idx
            return (jnp.where(active, r, 0), jnp.clip(s - idx * spa, 0, spa - 1))
        return index

    in_specs = [pl.BlockSpec((tm, tki), lambda s, ki, r: (r, ki))]
    in_specs += [pl.BlockSpec((tm, n_s), b_map(idx)) for idx in range(na)]
    return pl.pallas_call(
        body, name=name, grid=(s_all, k_in // tki, m // tm), in_specs=in_specs,
        out_specs=pl.BlockSpec((1, tki, n_s), lambda s, ki, r: (s, ki, 0)),
        out_shape=jax.ShapeDtypeStruct((s_all, k_in, n_s), f32),
        compiler_params=_params("parallel", "parallel", "arbitrary"))(a, *b_list)


def rms_fwd(x, gains, *, name, tr=512):
    t, d = x.shape
    tr = min(tr, t)
    ng = len(gains)

    def body(*refs):
        x_ref = refs[0]
        g_refs = refs[1:1 + ng]
        h_refs = refs[1 + ng:1 + 2 * ng]
        r_ref = refs[-1]
        xv = x_ref[...]
        r = lax.rsqrt(jnp.mean(xv * xv, axis=1, keepdims=True) + EPS)
        xh = xv * r
        for g_ref, h_ref in zip(g_refs, h_refs):
            h_ref[...] = (xh * g_ref[...]).astype(bf16)
        r_ref[...] = r

    row = pl.BlockSpec((tr, d), lambda i: (i, 0))
    vec = pl.BlockSpec((1, d), lambda i: (0, 0))
    outs = pl.pallas_call(
        body, name=name, grid=(t // tr,), in_specs=[row] + [vec] * ng,
        out_specs=[row] * ng + [pl.BlockSpec((tr, 1), lambda i: (i, 0))],
        out_shape=[jax.ShapeDtypeStruct((t, d), bf16)] * ng + [jax.ShapeDtypeStruct((t, 1), f32)],
        compiler_params=_params("parallel"))(x, *gains)
    return outs[:ng], outs[ng]


def rms_bwd(dh_list, x, r, gains, dx_in, *, name, tr=512):
    t, d = x.shape
    tr = min(tr, t)
    ng = len(gains)

    def body(*refs):
        dh_refs = refs[:ng]
        x_ref, r_ref = refs[ng], refs[ng + 1]
        g_refs = refs[ng + 2:2 * ng + 2]
        dxin_ref = refs[2 * ng + 2]
        dx_ref = refs[2 * ng + 3]
        dg_refs = refs[2 * ng + 4:]
        i = pl.program_id(0)
        rv = r_ref[...]
        xh = x_ref[...] * rv
        acc = dxin_ref[...]
        for dh_ref, g_ref, dg_ref in zip(dh_refs, g_refs, dg_refs):
            dh = dh_ref[...]
            part = jnp.sum(dh * xh, axis=0, keepdims=True)

            @pl.when(i == 0)
            def _(dg_ref=dg_ref, part=part):
                dg_ref[...] = part

            @pl.when(i > 0)
            def _(dg_ref=dg_ref, part=part):
                dg_ref[...] += part

            tg = dh * g_ref[...]
            acc = acc + rv * (tg - xh * jnp.mean(tg * xh, axis=1, keepdims=True))
        dx_ref[...] = acc

    row = pl.BlockSpec((tr, d), lambda i: (i, 0))
    vec = pl.BlockSpec((1, d), lambda i: (0, 0))
    outs = pl.pallas_call(
        body, name=name, grid=(t // tr,),
        in_specs=[row] * ng + [row, pl.BlockSpec((tr, 1), lambda i: (i, 0))] + [vec] * ng + [row],
        out_specs=[row] + [vec] * ng,
        out_shape=[jax.ShapeDtypeStruct((t, d), f32)] + [jax.ShapeDtypeStruct((1, d), f32)] * ng,
        compiler_params=_params("arbitrary"))(*dh_list, x, r, *gains, dx_in)
    return outs[0], outs[1:]


def sgu_gate_fwd(zu, zv, gv, wc, bt, *, name, tr=512):
    t, w = zu.shape
    tr = min(tr, t)
    groups = w // LANES

    def body(zu_ref, zv_ref, gv_ref, wc_ref, bt_ref, y_ref):
        vp = _gelu(zv_ref[...])
        rv = lax.rsqrt(jnp.mean(vp * vp, axis=1, keepdims=True) + EPS)
        vb = (vp * rv * gv_ref[...]).astype(bf16)
        for c in range(tr // CHUNK):
            rows = slice(c * CHUNK, (c + 1) * CHUNK)
            for g in range(groups):
                cols = slice(g * LANES, (g + 1) * LANES)
                sv = _dot(wc_ref[g], vb[rows, cols]) + bt_ref[:, g:g + 1]
                y_ref[rows, cols] = (_gelu(zu_ref[rows, cols]) * sv).astype(bf16)

    row = pl.BlockSpec((tr, w), lambda i: (i, 0))
    return pl.pallas_call(
        body, name=name, grid=(t // tr,),
        in_specs=[row, row, pl.BlockSpec((1, w), lambda i: (0, 0)),
                  pl.BlockSpec((groups, CHUNK, CHUNK), lambda i: (0, 0, 0)),
                  pl.BlockSpec((CHUNK, groups), lambda i: (0, 0))],
        out_specs=row, out_shape=jax.ShapeDtypeStruct((t, w), bf16),
        compiler_params=_params("parallel"))(zu, zv, gv, wc, bt)


def sgu_gate_bwd(zu, zv, dy, gv, wc, bt, *, name, tr=512):
    t, w = zu.shape
    tr = min(tr, t)
    groups = w // LANES
    nsteps = t // tr

    def body(zu_ref, zv_ref, dy_ref, gv_ref, wc_ref, bt_ref,
             dzu_ref, dzv_ref, dgv_ref, dws_ref, dbt_ref, dv_ref, bacc_ref):
        i = pl.program_id(0)

        @pl.when(i == 0)
        def _():
            dgv_ref[...] = jnp.zeros_like(dgv_ref)
            dws_ref[...] = jnp.zeros_like(dws_ref)
            bacc_ref[...] = jnp.zeros_like(bacc_ref)

        zvv = zv_ref[...]
        vp = _gelu(zvv)
        rv = lax.rsqrt(jnp.mean(vp * vp, axis=1, keepdims=True) + EPS)
        vhat = vp * rv
        vb = (vhat * gv_ref[...]).astype(bf16)
        for c in range(tr // CHUNK):
            rows = slice(c * CHUNK, (c + 1) * CHUNK)
            for g in range(groups):
                cols = slice(g * LANES, (g + 1) * LANES)
                vblk = vb[rows, cols]
                sv = _dot(wc_ref[g], vblk) + bt_ref[:, g:g + 1]
                zub = zu_ref[rows, cols]
                dyb = dy_ref[rows, cols]
                dzu_ref[rows, cols] = (dyb * sv * _gelu_grad(zub)).astype(bf16)
                dsv = dyb * _gelu(zub)
                bacc_ref[:, cols] += dsv
                dsvb = dsv.astype(bf16)
                dv_ref[rows, cols] = _dot_tn(wc_ref[g], dsvb)
                dws_ref[g] += _dot_nt(dsvb, vblk)
        dv = dv_ref[...]
        dgv_ref[...] += jnp.sum(dv * vhat, axis=0, keepdims=True)
        tg = dv * gv_ref[...]
        dvp = rv * (tg - vhat * jnp.mean(tg * vhat, axis=1, keepdims=True))
        dzv_ref[...] = (dvp * _gelu_grad(zvv)).astype(bf16)

        @pl.when(i == nsteps - 1)
        def _():
            tt = lax.broadcasted_iota(jnp.int32, (CHUNK, CHUNK), 0)
            ss = lax.broadcasted_iota(jnp.int32, (CHUNK, CHUNK), 1)
            for g in range(groups):
                dws_ref[g] = jnp.where(ss <= tt, dws_ref[g], 0.0)
                dbt_ref[:, g:g + 1] = jnp.sum(bacc_ref[:, g * LANES:(g + 1) * LANES], axis=1, keepdims=True)

    row = pl.BlockSpec((tr, w), lambda i: (i, 0))
    full3 = pl.BlockSpec((groups, CHUNK, CHUNK), lambda i: (0, 0, 0))
    return pl.pallas_call(
        body, name=name, grid=(nsteps,),
        in_specs=[row, row, row, pl.BlockSpec((1, w), lambda i: (0, 0)), full3,
                  pl.BlockSpec((CHUNK, groups), lambda i: (0, 0))],
        out_specs=[row, row, pl.BlockSpec((1, w), lambda i: (0, 0)), full3,
                   pl.BlockSpec((CHUNK, groups), lambda i: (0, 0))],
        out_shape=[jax.ShapeDtypeStruct((t, w), bf16), jax.ShapeDtypeStruct((t, w), bf16),
                   jax.ShapeDtypeStruct((1, w), f32), jax.ShapeDtypeStruct((groups, CHUNK, CHUNK), f32),
                   jax.ShapeDtypeStruct((CHUNK, groups), f32)],
        scratch_shapes=[pltpu.VMEM((tr, w), f32), pltpu.VMEM((CHUNK, w), f32)],
        compiler_params=_params("arbitrary"))(zu, zv, dy, gv, wc, bt)


HALO = 8


def _shift_down(v, halo, k, first):
    r = pltpu.roll(v, k, 0)
    hh = jnp.where(first, 0.0, pltpu.roll(halo, k, 0))
    rid = lax.broadcasted_iota(jnp.int32, (HALO, v.shape[1]), 0)
    head = jnp.where(rid < k, hh, r[0:HALO])
    if v.shape[0] == HALO:
        return head
    return jnp.concatenate([head, r[HALO:]], axis=0)


def _shift_up(v, halo, k, last):
    n = v.shape[0]
    r = pltpu.roll(v, n - k, 0)
    hh = jnp.where(last, 0.0, pltpu.roll(halo, HALO - k, 0))
    rid = lax.broadcasted_iota(jnp.int32, (HALO, v.shape[1]), 0)
    tail = jnp.where(rid >= HALO - k, hh, r[n - HALO:])
    return jnp.concatenate([r[:n - HALO], tail], axis=0)


def _conv(p, halo, w_ref, b_ref, first):
    return (w_ref[2:3, :] * p + w_ref[1:2, :] * _shift_down(p, halo, 1, first)
            + w_ref[0:1, :] * _shift_down(p, halo, 2, first) + b_ref[...])


def _conv_specs(t, tr, tc):
    tile = pl.BlockSpec((tr, tc), lambda j, i: (i, j))
    prev = pl.BlockSpec((HALO, tc), lambda j, i: (jnp.maximum(i * (tr // HALO) - 1, 0), j))
    nxt = pl.BlockSpec((HALO, tc), lambda j, i: (jnp.minimum((i + 1) * (tr // HALO), t // HALO - 1), j))
    wspec = pl.BlockSpec((3, tc), lambda j, i: (0, j))
    bspec = pl.BlockSpec((1, tc), lambda j, i: (0, j))
    return tile, prev, nxt, wspec, bspec


def ffn_gate_fwd(pg, pu, wg, wu, bg, bu, *, name, tr=1024, tc=256):
    t, f = pg.shape
    tr = min(tr, t)
    tile, prev, _, wspec, bspec = _conv_specs(t, tr, tc)

    def body(pg_ref, pgh_ref, pu_ref, puh_ref, wg_ref, wu_ref, bg_ref, bu_ref, a_ref):
        first = pl.program_id(1) == 0
        gate = _conv(pg_ref[...], pgh_ref[...], wg_ref, bg_ref, first)
        up = _conv(pu_ref[...], puh_ref[...], wu_ref, bu_ref, first)
        a_ref[...] = (gate * jax.nn.sigmoid(gate) * up).astype(bf16)

    return pl.pallas_call(
        body, name=name, grid=(f // tc, t // tr),
        in_specs=[tile, prev, tile, prev, wspec, wspec, bspec, bspec],
        out_specs=tile, out_shape=jax.ShapeDtypeStruct((t, f), bf16),
        compiler_params=_params("parallel", "parallel"))(pg, pg, pu, pu, wg, wu, bg, bu)


def _gate_grads(gate, up, dav):
    sg = jax.nn.sigmoid(gate)
    return dav * up * (sg * (1.0 + gate * (1.0 - sg))), dav * gate * sg


def ffn_gate_bwd(pg, pu, da, wg, wu, bg, bu, *, name, tr=1024, tc=256):
    t, f = pg.shape
    tr = min(tr, t)
    tile, prev, nxt, wspec, bspec = _conv_specs(t, tr, tc)
    nsteps = t // tr

    def body(pg_ref, pgh_ref, pgn_ref, pu_ref, puh_ref, pun_ref, da_ref, dan_ref, wg_ref, wu_ref, bg_ref, bu_ref,
             dg_ref, du_ref, sg_ref, su_ref):
        i = pl.program_id(1)
        first = i == 0
        last = i == nsteps - 1
        pgv, puv = pg_ref[...], pu_ref[...]
        pg1, pg2 = _shift_down(pgv, pgh_ref[...], 1, first), _shift_down(pgv, pgh_ref[...], 2, first)
        pu1, pu2 = _shift_down(puv, puh_ref[...], 1, first), _shift_down(puv, puh_ref[...], 2, first)
        gate = wg_ref[2:3, :] * pgv + wg_ref[1:2, :] * pg1 + wg_ref[0:1, :] * pg2 + bg_ref[...]
        up = wu_ref[2:3, :] * puv + wu_ref[1:2, :] * pu1 + wu_ref[0:1, :] * pu2 + bu_ref[...]
        dgate, dup = _gate_grads(gate, up, da_ref[...])
        gate_n = _conv(pgn_ref[...], pgv[tr - HALO:], wg_ref, bg_ref, False)
        up_n = _conv(pun_ref[...], puv[tr - HALO:], wu_ref, bu_ref, False)
        dgate_n, dup_n = _gate_grads(gate_n, up_n, dan_ref[...])
        for d, d_n, w_ref, o_ref in ((dgate, dgate_n, wg_ref, dg_ref), (dup, dup_n, wu_ref, du_ref)):
            o_ref[...] = (w_ref[2:3, :] * d + w_ref[1:2, :] * _shift_up(d, d_n, 1, last)
                          + w_ref[0:1, :] * _shift_up(d, d_n, 2, last)).astype(bf16)
        rid = lax.broadcasted_iota(jnp.int32, (8, tc), 0)
        for d, p0, p1, p2, s_ref in ((dgate, pgv, pg1, pg2, sg_ref), (dup, puv, pu1, pu2, su_ref)):
            sums = [jnp.sum(d * p2, axis=0, keepdims=True), jnp.sum(d * p1, axis=0, keepdims=True),
                    jnp.sum(d * p0, axis=0, keepdims=True), jnp.sum(d, axis=0, keepdims=True)]
            part = jnp.zeros((8, tc), f32)
            for k, sk in enumerate(sums):
                part = jnp.where(rid == k, sk, part)

            @pl.when(first)
            def _(s_ref=s_ref, part=part):
                s_ref[...] = part

            @pl.when(i > 0)
            def _(s_ref=s_ref, part=part):
                s_ref[...] += part

    stat = pl.BlockSpec((8, tc), lambda j, i: (0, j))
    return pl.pallas_call(
        body, name=name, grid=(f // tc, nsteps),
        in_specs=[tile, prev, nxt, tile, prev, nxt, tile, nxt, wspec, wspec, bspec, bspec],
        out_specs=[tile, tile, stat, stat],
        out_shape=[jax.ShapeDtypeStruct((t, f), bf16), jax.ShapeDtypeStruct((t, f), bf16),
                   jax.ShapeDtypeStruct((8, f), f32), jax.ShapeDtypeStruct((8, f), f32)],
        compiler_params=_params("parallel", "arbitrary"))(pg, pg, pg, pu, pu, pu, da, da, wg, wu, bg, bu)


def _head_mean_matrix():
    i = lax.broadcasted_iota(jnp.int32, (LANES, LANES), 0) // HEAD_DIM
    j = lax.broadcasted_iota(jnp.int32, (LANES, LANES), 1) // HEAD_DIM
    return jnp.where(i == j, 1.0 / HEAD_DIM, 0.0).astype(f32)


def _lane_half(shape):
    return (lax.broadcasted_iota(jnp.int32, shape, 1) % LANES) // HEAD_DIM


def q_norm_fwd(qp, g2, *, name, scale, tr=512):
    t, w = qp.shape
    tr = min(tr, t)

    def body(x_ref, g_ref, o_ref):
        bd = _head_mean_matrix()
        for cb in range(w // LANES):
            cols = slice(cb * LANES, (cb + 1) * LANES)
            xc = x_ref[:, cols]
            rh = lax.rsqrt(_dot_exact(xc * xc, bd) + EPS)
            o_ref[:, cols] = (xc * rh * g_ref[...] * scale).astype(bf16)

    row = pl.BlockSpec((tr, w), lambda i: (i, 0))
    return pl.pallas_call(
        body, name=name, grid=(t // tr,), in_specs=[row, pl.BlockSpec((1, LANES), lambda i: (0, 0))],
        out_specs=row, out_shape=jax.ShapeDtypeStruct((t, w), bf16),
        compiler_params=_params("parallel"))(qp, g2)


def q_norm_bwd(dq, qp, g2, *, name, scale, tr=512):
    t, w = qp.shape
    tr = min(tr, t)

    def body(dq_ref, x_ref, g_ref, o_ref, dg_ref):
        i = pl.program_id(0)
        bd = _head_mean_matrix()
        acc = jnp.zeros((1, LANES), f32)
        for cb in range(w // LANES):
            cols = slice(cb * LANES, (cb + 1) * LANES)
            xc = x_ref[:, cols]
            rh = lax.rsqrt(_dot_exact(xc * xc, bd) + EPS)
            xh = xc * rh
            dy = dq_ref[:, cols] * scale
            acc = acc + jnp.sum(dy * xh, axis=0, keepdims=True)
            tg = dy * g_ref[...]
            o_ref[:, cols] = (rh * (tg - xh * _dot_exact(tg * xh, bd))).astype(bf16)

        @pl.when(i == 0)
        def _():
            dg_ref[...] = acc

        @pl.when(i > 0)
        def _():
            dg_ref[...] += acc

    row = pl.BlockSpec((tr, w), lambda i: (i, 0))
    vec = pl.BlockSpec((1, LANES), lambda i: (0, 0))
    return pl.pallas_call(
        body, name=name, grid=(t // tr,), in_specs=[row, row, vec], out_specs=[row, vec],
        out_shape=[jax.ShapeDtypeStruct((t, w), bf16), jax.ShapeDtypeStruct((1, LANES), f32)],
        compiler_params=_params("arbitrary"))(dq, qp, g2)


def kv_post_fwd(kv, g2, *, name, tr=512):
    t, w = kv.shape
    tr = min(tr, t)
    kw = w // 2

    def body(x_ref, g_ref, k_ref, v_ref):
        bd = _head_mean_matrix()
        half = _lane_half((tr, LANES))
        for cb in range(kw // LANES):
            xc = x_ref[:, cb * LANES:(cb + 1) * LANES]
            rh = lax.rsqrt(_dot_exact(xc * xc, bd) + EPS)
            kn = xc * rh * g_ref[...]
            vc = x_ref[:, kw + cb * LANES:kw + (cb + 1) * LANES]
            for src, dst in ((kn, k_ref), (vc, v_ref)):
                sw = pltpu.roll(src, HEAD_DIM, 1)
                for hf in range(2):
                    blk = 2 * cb + hf
                    dst[:, blk * LANES:(blk + 1) * LANES] = jnp.where(half == hf, src, sw).astype(bf16)

    return pl.pallas_call(
        body, name=name, grid=(t // tr,),
        in_specs=[pl.BlockSpec((tr, w), lambda i: (i, 0)), pl.BlockSpec((1, LANES), lambda i: (0, 0))],
        out_specs=[pl.BlockSpec((tr, 2 * kw), lambda i: (i, 0))] * 2,
        out_shape=[jax.ShapeDtypeStruct((t, 2 * kw), bf16)] * 2,
        compiler_params=_params("parallel"))(kv, g2)


def kv_post_bwd(dk2, dv2, kv, g2, *, name, tr=512):
    t, w = kv.shape
    tr = min(tr, t)
    kw = w // 2

    def body(dk_ref, dv_ref, x_ref, g_ref, o_ref, dg_ref):
        i = pl.program_id(0)
        bd = _head_mean_matrix()
        half = _lane_half((tr, LANES))
        acc = jnp.zeros((1, LANES), f32)

        def fold(ref, cb):
            a = ref[:, (2 * cb) * LANES:(2 * cb + 1) * LANES]
            b = ref[:, (2 * cb + 1) * LANES:(2 * cb + 2) * LANES]
            return jnp.where(half == 0, a + pltpu.roll(a, HEAD_DIM, 1), b + pltpu.roll(b, HEAD_DIM, 1))

        for cb in range(kw // LANES):
            cols = slice(cb * LANES, (cb + 1) * LANES)
            xc = x_ref[:, cols]
            rh = lax.rsqrt(_dot_exact(xc * xc, bd) + EPS)
            xh = xc * rh
            dy = fold(dk_ref, cb)
            acc = acc + jnp.sum(dy * xh, axis=0, keepdims=True)
            tg = dy * g_ref[...]
            o_ref[:, cols] = (rh * (tg - xh * _dot_exact(tg * xh, bd))).astype(bf16)
            o_ref[:, kw + cb * LANES:kw + (cb + 1) * LANES] = fold(dv_ref, cb).astype(bf16)

        @pl.when(i == 0)
        def _():
            dg_ref[...] = acc

        @pl.when(i > 0)
        def _():
            dg_ref[...] += acc

    dup = pl.BlockSpec((tr, 2 * kw), lambda i: (i, 0))
    row = pl.BlockSpec((tr, w), lambda i: (i, 0))
    vec = pl.BlockSpec((1, LANES), lambda i: (0, 0))
    return pl.pallas_call(
        body, name=name, grid=(t // tr,), in_specs=[dup, dup, row, vec], out_specs=[row, vec],
        out_shape=[jax.ShapeDtypeStruct((t, w), bf16), jax.ShapeDtypeStruct((1, LANES), f32)],
        compiler_params=_params("arbitrary"))(dk2, dv2, kv, g2)


def _slope(h):
    return 2.0 ** (-8.0 * (h + 1) / N_Q_HEADS)


GROUP_ROWS = Q_PER_KV * CHUNK


def _band_mask(n):
    tq = lax.broadcasted_iota(jnp.int32, (GROUP_ROWS, 2 * CHUNK), 0) % CHUNK
    jk = lax.broadcasted_iota(jnp.int32, (GROUP_ROWS, 2 * CHUNK), 1)
    dist = tq + CHUNK - jk
    ok = (dist >= 0) & (dist < CHUNK) & jnp.logical_not((n == 0) & (jk < CHUNK))
    return dist.astype(f32), ok


def _band(ref, n, kh):
    p0 = pl.multiple_of(jnp.maximum(n - 1, 0) * CHUNK, CHUNK)
    c0 = pl.multiple_of(n * CHUNK, CHUNK)
    cols = slice(kh * LANES, (kh + 1) * LANES)
    return jnp.concatenate([ref[pl.ds(p0, CHUNK), cols], ref[pl.ds(c0, CHUNK), cols]], axis=0)


def _stack_heads(ref, kh, half):
    parts = []
    for cb in (2 * kh, 2 * kh + 1):
        xc = ref[:, cb * LANES:(cb + 1) * LANES].astype(f32)
        parts += [jnp.where(half == hf, xc, 0.0).astype(bf16) for hf in range(2)]
    return jnp.concatenate(parts, axis=0)


def _unstack_heads(x4, half):
    return (jnp.where(half == 0, x4[0:CHUNK], x4[CHUNK:2 * CHUNK]),
            jnp.where(half == 0, x4[2 * CHUNK:3 * CHUNK], x4[3 * CHUNK:]))


def _per_head_column(kh, values):
    grp = lax.broadcasted_iota(jnp.int32, (GROUP_ROWS, 1), 0) // CHUNK
    col = jnp.full((GROUP_ROWS, 1), values[0], f32)
    for g in range(1, Q_PER_KV):
        col = jnp.where(grp == g, values[g], col)
    return col


def _softmax_band(q4, kband, dist, ok, slope, sink):
    s = _dot_nt(q4, kband)
    s = jnp.where(ok, s - slope * dist, -jnp.inf)
    m = jnp.maximum(jnp.max(s, axis=1, keepdims=True), sink)
    e = jnp.exp(s - m)
    es = jnp.exp(sink - m)
    den = jnp.sum(e, axis=1, keepdims=True) + es
    return e / den, es / den


def attn_fwd(q, k2, v2, sinks, *, name):
    t, w = q.shape
    nb = t // CHUNK

    def body(sink_ref, q_ref, k_ref, v_ref, o_ref):
        n = pl.program_id(0)
        dist, ok = _band_mask(n)
        half = _lane_half((CHUNK, LANES))
        for kh in range(N_KV_HEADS):
            heads = [Q_PER_KV * kh + g for g in range(Q_PER_KV)]
            slope = _per_head_column(kh, [_slope(h) for h in heads])
            sink = _per_head_column(kh, [sink_ref[h] for h in heads])
            q4 = _stack_heads(q_ref, kh, half)
            p, _ = _softmax_band(q4, _band(k_ref, n, kh), dist, ok, slope, sink)
            o4 = _dot(p.astype(bf16), _band(v_ref, n, kh))
            lo, hi = _unstack_heads(o4, half)
            o_ref[:, (2 * kh) * LANES:(2 * kh + 1) * LANES] = lo.astype(bf16)
            o_ref[:, (2 * kh + 1) * LANES:(2 * kh + 2) * LANES] = hi.astype(bf16)

    full = pl.BlockSpec((t, k2.shape[1]), lambda n: (0, 0))
    return pl.pallas_call(
        body, name=name, grid=(nb,),
        in_specs=[pl.BlockSpec(memory_space=pltpu.SMEM), pl.BlockSpec((CHUNK, w), lambda n: (n, 0)), full, full],
        out_specs=pl.BlockSpec((CHUNK, w), lambda n: (n, 0)),
        out_shape=jax.ShapeDtypeStruct((t, w), bf16),
        compiler_params=_params("parallel"))(sinks, q, k2, v2)


def attn_bwd(q, k2, v2, do, sinks, *, name):
    t, w = q.shape
    nb = t // CHUNK
    kw = k2.shape[1]

    def body(sink_ref, q_ref, k_ref, v_ref, do_ref, dq_ref, dk_ref, dv_ref, ds_ref, kc_ref, vc_ref):
        n = pl.program_id(0)

        @pl.when(n == 0)
        def _():
            ds_ref[...] = jnp.zeros_like(ds_ref)
            kc_ref[...] = jnp.zeros_like(kc_ref)
            vc_ref[...] = jnp.zeros_like(vc_ref)
            dk_ref[...] = jnp.zeros_like(dk_ref)
            dv_ref[...] = jnp.zeros_like(dv_ref)

        @pl.when(n == nb)
        def _():
            dk_ref[...] = kc_ref[...]
            dv_ref[...] = vc_ref[...]

        @pl.when(n < nb)
        def _():
            dist, ok = _band_mask(n)
            half = _lane_half((CHUNK, LANES))
            lane = lax.broadcasted_iota(jnp.int32, (1, LANES), 1)
            sink_acc = jnp.zeros((1, LANES), f32)
            for kh in range(N_KV_HEADS):
                heads = [Q_PER_KV * kh + g for g in range(Q_PER_KV)]
                slope = _per_head_column(kh, [_slope(h) for h in heads])
                sink = _per_head_column(kh, [sink_ref[h] for h in heads])
                q4 = _stack_heads(q_ref, kh, half)
                do4 = _stack_heads(do_ref, kh, half)
                kband = _band(k_ref, n, kh)
                vband = _band(v_ref, n, kh)
                p, ps = _softmax_band(q4, kband, dist, ok, slope, sink)
                dp = _dot_nt(do4, vband)
                delta = jnp.sum(p * dp, axis=1, keepdims=True)
                dsb = (p * (dp - delta)).astype(bf16)
                sd = ps * delta
                for g, h in enumerate(heads):
                    part = jnp.sum(sd[g * CHUNK:(g + 1) * CHUNK], axis=0, keepdims=True)
                    sink_acc = sink_acc + jnp.where(lane == h, -part, 0.0)
                lo, hi = _unstack_heads(_dot(dsb, kband), half)
                dq_ref[:, (2 * kh) * LANES:(2 * kh + 1) * LANES] = lo
                dq_ref[:, (2 * kh + 1) * LANES:(2 * kh + 2) * LANES] = hi
                dkb = _dot_tn(dsb, q4)
                dvb = _dot_tn(p.astype(bf16), do4)
                cols = slice(kh * LANES, (kh + 1) * LANES)
                dk_ref[:, cols] = kc_ref[:, cols] + dkb[0:CHUNK]
                dv_ref[:, cols] = vc_ref[:, cols] + dvb[0:CHUNK]
                kc_ref[:, cols] = dkb[CHUNK:]
                vc_ref[:, cols] = dvb[CHUNK:]
            ds_ref[...] += sink_acc

    full = pl.BlockSpec((t, kw), lambda n: (0, 0))
    qblk = pl.BlockSpec((CHUNK, w), lambda n: (jnp.minimum(n, nb - 1), 0))
    kblk = pl.BlockSpec((CHUNK, kw), lambda n: (jnp.maximum(n - 1, 0), 0))
    return pl.pallas_call(
        body, name=name, grid=(nb + 1,),
        in_specs=[pl.BlockSpec(memory_space=pltpu.SMEM), qblk, full, full, qblk],
        out_specs=[qblk, kblk, kblk, pl.BlockSpec((1, LANES), lambda n: (0, 0))],
        out_shape=[jax.ShapeDtypeStruct((t, w), f32), jax.ShapeDtypeStruct((t, kw), f32),
                   jax.ShapeDtypeStruct((t, kw), f32), jax.ShapeDtypeStruct((1, LANES), f32)],
        scratch_shapes=[pltpu.VMEM((CHUNK, kw), f32), pltpu.VMEM((CHUNK, kw), f32)],
        compiler_params=_params("arbitrary"))(sinks, q, k2, v2, do)


def loss_head(y, target, *, name, tr=512):
    t, d = y.shape
    tr = min(tr, t)

    def body(y_ref, t_ref, dy_ref, s_ref):
        i = pl.program_id(0)
        e = y_ref[...] - t_ref[...]
        dy_ref[...] = e * (1.0 / d)
        part = jnp.sum(e * e, axis=0, keepdims=True)

        @pl.when(i == 0)
        def _():
            s_ref[...] = part

        @pl.when(i > 0)
        def _():
            s_ref[...] += part

    row = pl.BlockSpec((tr, d), lambda i: (i, 0))
    vec = pl.BlockSpec((1, d), lambda i: (0, 0))
    return pl.pallas_call(
        body, name=name, grid=(t // tr,), in_specs=[row, row], out_specs=[row, vec],
        out_shape=[jax.ShapeDtypeStruct((t, d), f32), jax.ShapeDtypeStruct((1, d), f32)],
        compiler_params=_params("arbitrary"))(y, target)


N_STEPS = 8


def _row_blocks(shape):
    if len(shape) == 2:
        r, c = shape
        return (r // N_STEPS, c), (lambda s: (s, 0))
    l, r, c = shape
    per = N_STEPS // l
    return (1, r // per, c), (lambda s: (s // per, s % per, 0))


CAST_STEPS = 4


def cast_into_slot(arrays, k_arr, *, name):
    in_specs, out_specs, out_shape, layers = [], [], [], []
    for a in arrays:
        r, c = a.shape[-2:]
        rb = r // CAST_STEPS
        if a.ndim == 2:
            in_specs.append(pl.BlockSpec((rb, c), lambda s, k: (s, 0)))
            layers.append(None)
        else:
            for l in range(a.shape[0]):
                in_specs.append(pl.BlockSpec((1, rb, c), lambda s, k, l=l: (l, s, 0)))
                layers.append(l)
        for _ in range(1 if a.ndim == 2 else a.shape[0]):
            out_specs.append(pl.BlockSpec((1, rb, c), lambda s, k: (k[0], s, 0)))
            out_shape.append(jax.ShapeDtypeStruct((N_SHARDS, r, c), bf16))
    n = len(in_specs)

    def body(k_ref, *refs):
        for i_ref, o_ref, l in zip(refs[:n], refs[n:], layers):
            o_ref[0] = (i_ref[...] if l is None else i_ref[0]).astype(bf16)

    args = []
    for a in arrays:
        args += [a] * (1 if a.ndim == 2 else a.shape[0])
    return pl.pallas_call(
        body, name=name,
        grid_spec=pltpu.PrefetchScalarGridSpec(num_scalar_prefetch=1, grid=(CAST_STEPS,),
                                               in_specs=in_specs, out_specs=out_specs),
        out_shape=out_shape, compiler_params=_params("parallel"))(k_arr, *args)


def adamw(ws, gs, ms, vs, *, name):
    n = len(ws)
    specs, g_specs, g_count = [], [], []
    for w, g_list in zip(ws, gs):
        blk, index = _row_blocks(w.shape)
        specs.append(pl.BlockSpec(blk, index))
        layers = len(g_list)
        per = N_STEPS // layers
        g_count.append(layers)
        for l in range(layers):
            g_specs.append(pl.BlockSpec(blk[-2:], lambda s, l=l, per=per: (jnp.where(s // per == l, s % per, 0), 0)))
    ng = len(g_specs)

    def body(*refs):
        s = pl.program_id(0)
        g_refs = refs[3 * n:3 * n + ng]
        outs = refs[3 * n + ng:]
        off = 0
        for i in range(n):
            w_ref, m_ref, v_ref = refs[i], refs[n + i], refs[2 * n + i]
            go_ref, d_ref, nm_ref, nv_ref = (outs[k * n + i] for k in range(4))
            layers = g_count[i]
            g = g_refs[off][...]
            for l in range(1, layers):
                g = jnp.where(s // (N_STEPS // layers) == l, g_refs[off + l][...], g)
            off += layers
            g = g.reshape(w_ref.shape)
            m = ADAM_B1 * m_ref[...] + (1.0 - ADAM_B1) * g
            v = ADAM_B2 * v_ref[...] + (1.0 - ADAM_B2) * (g * g)
            m_hat = m / ADAM_C1
            v_hat = v / ADAM_C2
            go_ref[...] = g
            d_ref[...] = -ADAM_LR * (m_hat / (jnp.sqrt(v_hat) + ADAM_EPS) + ADAM_WD * w_ref[...])
            nm_ref[...] = m
            nv_ref[...] = v

    outs = pl.pallas_call(
        body, name=name, grid=(N_STEPS,), in_specs=specs * 3 + g_specs, out_specs=specs * 4,
        out_shape=[jax.ShapeDtypeStruct(a.shape, f32) for a in ws] * 4,
        compiler_params=_params("parallel"))(*ws, *ms, *vs, *[g for g_list in gs for g in g_list])
    return [outs[k * n:(k + 1) * n] for k in range(4)]


def _place():
    return lax.axis_index("x"), lax.axis_index("y"), lax.axis_index("c")


def gather_shards(bufs, *, name, split):
    n = len(bufs)

    def body(*refs):
        bufs_ = refs[:n]
        isend, irecv, dsend, drecv = refs[2 * n:]
        x, y, c = _place()
        k = 2 * x + y
        peers = [(1 - x, y, c), (x, 1 - y, c), (1 - x, 1 - y, c)]
        peer_k = [2 * (1 - x) + y, 2 * x + (1 - y), 2 * (1 - x) + (1 - y)]

        def slab(a, q, h):
            if not split[a]:
                return bufs_[a].at[q]
            half = bufs_[a].shape[1] // 2
            return bufs_[a].at[q, pl.ds(pl.multiple_of(h * half, 16), half)]

        def ici(a, j, q):
            return pltpu.make_async_remote_copy(
                src_ref=slab(a, q, c), dst_ref=slab(a, q, c), send_sem=isend.at[3 * a + j], recv_sem=irecv.at[3 * a + j],
                device_id=peers[j], device_id_type=MESH)

        def d2d(a, j, h):
            return pltpu.make_async_remote_copy(
                src_ref=slab(a, peer_k[j], h), dst_ref=slab(a, peer_k[j], h), send_sem=dsend.at[3 * a + j],
                recv_sem=drecv.at[3 * a + j], device_id=(x, y, 1 - c), device_id_type=MESH)

        for a in range(n):
            for j in range(3):
                ici(a, j, k).start()
        for a in range(n):
            for j in range(3):
                ici(a, j, peer_k[j]).wait_recv()
                if split[a]:
                    d2d(a, j, c).start()
        for a in range(n):
            for j in range(3):
                if split[a]:
                    d2d(a, j, 1 - c).wait_recv()
        for a in range(n):
            for j in range(3):
                ici(a, j, k).wait_send()
                if split[a]:
                    d2d(a, j, c).wait_send()

    return pl.pallas_call(
        body, name=name, in_specs=[ANY] * n, out_specs=[ANY] * n,
        out_shape=[jax.ShapeDtypeStruct(b.shape, b.dtype) for b in bufs],
        input_output_aliases={i: i for i in range(n)},
        scratch_shapes=[pltpu.SemaphoreType.DMA((3 * n,))] * 4)(*bufs)


def sibling_exchange(arrays, *, name):
    n = len(arrays)

    def body(*refs):
        ins, outs = refs[:n], refs[n:2 * n]
        send, recv = refs[2 * n:]
        x, y, c = _place()

        def copy(a):
            return pltpu.make_async_remote_copy(
                src_ref=ins[a], dst_ref=outs[a], send_sem=send.at[a], recv_sem=recv.at[a],
                device_id=(x, y, 1 - c), device_id_type=MESH)

        for a in range(n):
            copy(a).start()
        for a in range(n):
            copy(a).wait_recv()
        for a in range(n):
            copy(a).wait_send()

    return pl.pallas_call(
        body, name=name, in_specs=[ANY] * n, out_specs=[ANY] * n,
        out_shape=[jax.ShapeDtypeStruct(a.shape, a.dtype) for a in arrays],
        scratch_shapes=[pltpu.SemaphoreType.DMA((n,)), pltpu.SemaphoreType.DMA((n,))])(*arrays)


def chip_scatter(arrays, *, name):
    n = len(arrays)

    def body(*refs):
        ins, outs = refs[:n], refs[n:2 * n]
        send, recv = refs[2 * n:]
        x, y, c = _place()
        peers = [(1 - x, y, c), (x, 1 - y, c), (1 - x, 1 - y, c)]
        peer_k = [2 * (1 - x) + y, 2 * x + (1 - y), 2 * (1 - x) + (1 - y)]

        def copy(a, j):
            return pltpu.make_async_remote_copy(
                src_ref=ins[a].at[peer_k[j]], dst_ref=outs[a].at[j], send_sem=send.at[3 * a + j],
                recv_sem=recv.at[3 * a + j], device_id=peers[j], device_id_type=MESH)

        for a in range(n):
            for j in range(3):
                copy(a, j).start()
        for a in range(n):
            for j in range(3):
                copy(a, j).wait_recv()
        for a in range(n):
            for j in range(3):
                copy(a, j).wait_send()

    return pl.pallas_call(
        body, name=name, in_specs=[ANY] * n, out_specs=[ANY] * n,
        out_shape=[jax.ShapeDtypeStruct((3,) + a.shape[1:], a.dtype) for a in arrays],
        scratch_shapes=[pltpu.SemaphoreType.DMA((3 * n,)), pltpu.SemaphoreType.DMA((3 * n,))])(*arrays)


def sibling_merge(bufs, *, name):
    n = len(bufs)

    def body(*refs):
        bufs_ = refs[:n]
        send, recv = refs[2 * n:]
        x, y, c = _place()

        def copy(u, h):
            return pltpu.make_async_remote_copy(
                src_ref=bufs_[u].at[h], dst_ref=bufs_[u].at[h], send_sem=send.at[u], recv_sem=recv.at[u],
                device_id=(x, y, 1 - c), device_id_type=MESH)

        for u in range(n):
            copy(u, c).start()
        for u in range(n):
            copy(u, 1 - c).wait_recv()
        for u in range(n):
            copy(u, c).wait_send()

    return pl.pallas_call(
        body, name=name, in_specs=[ANY] * n, out_specs=[ANY] * n,
        out_shape=[jax.ShapeDtypeStruct(b.shape, b.dtype) for b in bufs],
        input_output_aliases={i: i for i in range(n)},
        scratch_shapes=[pltpu.SemaphoreType.DMA((n,)), pltpu.SemaphoreType.DMA((n,))])(*bufs)


def all_gather_rows(a, *, name):
    def body(a_ref, o_ref, send, recv, loc):
        x, y, c = _place()
        me = 4 * x + 2 * y + c
        masks = [(mx, my, mc) for mx in (0, 1) for my in (0, 1) for mc in (0, 1)][1:]

        def peer(mk):
            return (x ^ mk[0], y ^ mk[1], c ^ mk[2])

        def copy(j, slot):
            return pltpu.make_async_remote_copy(
                src_ref=a_ref, dst_ref=o_ref.at[slot], send_sem=send.at[j], recv_sem=recv.at[j],
                device_id=peer(masks[j]), device_id_type=MESH)

        local = pltpu.make_async_copy(a_ref, o_ref.at[me], loc)
        local.start()
        for j in range(7):
            copy(j, me).start()
        for j in range(7):
            px, py, pc = peer(masks[j])
            copy(j, 4 * px + 2 * py + pc).wait_recv()
        for j in range(7):
            copy(j, me).wait_send()
        local.wait()

    return pl.pallas_call(
        body, name=name, in_specs=[ANY], out_specs=ANY,
        out_shape=jax.ShapeDtypeStruct((N_DEV,) + a.shape, a.dtype),
        scratch_shapes=[pltpu.SemaphoreType.DMA((7,)), pltpu.SemaphoreType.DMA((7,)), pltpu.SemaphoreType.DMA])(a)


def sum_leading(a, *, name):
    n, r, c = a.shape

    def body(a_ref, o_ref):
        acc = a_ref[0]
        for i in range(1, n):
            acc = acc + a_ref[i]
        o_ref[...] = acc

    rb = r // 2 if r % 16 == 0 else r
    return pl.pallas_call(
        body, name=name, grid=(r // rb,), in_specs=[pl.BlockSpec((n, rb, c), lambda i: (0, i, 0))],
        out_specs=pl.BlockSpec((rb, c), lambda i: (i, 0)), out_shape=jax.ShapeDtypeStruct((r, c), f32),
        compiler_params=_params("parallel"))(a)


def _half_rows(shape):
    return shape[1] // 2 // 2


def rs_cast_other_half(grads, c_arr, *, name):
    n = len(grads)

    def body(c_ref, *refs):
        for i_ref, o_ref in zip(refs[:n], refs[n:]):
            o_ref[...] = i_ref[...].astype(bf16)

    in_specs = [pl.BlockSpec((1, _half_rows(g.shape), g.shape[2]), lambda s, r, c_ref: (s, (1 - c_ref[0]) * 2 + r, 0))
                for g in grads]
    out_specs = [pl.BlockSpec((1, _half_rows(g.shape), g.shape[2]), lambda s, r, c_ref: (s, r, 0)) for g in grads]
    return pl.pallas_call(
        body, name=name,
        grid_spec=pltpu.PrefetchScalarGridSpec(num_scalar_prefetch=1, grid=(N_SHARDS, 2),
                                               in_specs=in_specs, out_specs=out_specs),
        out_shape=[jax.ShapeDtypeStruct((N_SHARDS, g.shape[1] // 2, g.shape[2]), bf16) for g in grads],
        compiler_params=_params("parallel", "parallel"))(c_arr, *grads)


def rs_add_sibling(grads, recvd, ck_arr, *, name):
    n = len(grads)

    def body(ck_ref, *refs):
        s = pl.program_id(1)
        for u in range(n):
            g_ref, r_ref = refs[u], refs[n + u]
            qb_ref, own_ref = refs[2 * n + u], refs[3 * n + u]
            q = g_ref[0] + r_ref[0].astype(f32)
            qb_ref[0] = q.astype(bf16)

            @pl.when(s == ck_ref[1])
            def _(own_ref=own_ref, q=q):
                own_ref[...] = q

    in_specs = [pl.BlockSpec((1, _half_rows(g.shape), g.shape[2]), lambda r, s, ck: (s, ck[0] * 2 + r, 0)) for g in grads]
    in_specs += [pl.BlockSpec((1, _half_rows(g.shape), g.shape[2]), lambda r, s, ck: (s, r, 0)) for g in grads]
    out_specs = [pl.BlockSpec((1, _half_rows(g.shape), g.shape[2]), lambda r, s, ck: (s, r, 0)) for g in grads]
    out_specs += [pl.BlockSpec((_half_rows(g.shape), g.shape[2]), lambda r, s, ck: (r, 0)) for g in grads]
    outs = pl.pallas_call(
        body, name=name,
        grid_spec=pltpu.PrefetchScalarGridSpec(num_scalar_prefetch=1, grid=(2, N_SHARDS),
                                               in_specs=in_specs, out_specs=out_specs),
        out_shape=[jax.ShapeDtypeStruct((N_SHARDS, g.shape[1] // 2, g.shape[2]), bf16) for g in grads]
        + [jax.ShapeDtypeStruct((g.shape[1] // 2, g.shape[2]), f32) for g in grads],
        compiler_params=_params("parallel", "arbitrary"))(ck_arr, *grads, *recvd)
    return outs[:n], outs[n:]


def rs_sum_chips(owns, recvd, ck_arr, *, name):
    n = len(owns)

    def body(ck_ref, *refs):
        for u in range(n):
            own_ref, r_ref, o_ref = refs[u], refs[n + u], refs[2 * n + u]
            o_ref[0] = ((own_ref[...] + r_ref[0].astype(f32)) + r_ref[1].astype(f32)) + r_ref[2].astype(f32)

    in_specs = [pl.BlockSpec((o.shape[0] // 2, o.shape[1]), lambda r, ck: (r, 0)) for o in owns]
    in_specs += [pl.BlockSpec((3, o.shape[0] // 2, o.shape[1]), lambda r, ck: (0, r, 0)) for o in owns]
    out_specs = [pl.BlockSpec((1, o.shape[0] // 2, o.shape[1]), lambda r, ck: (ck[0], r, 0)) for o in owns]
    return pl.pallas_call(
        body, name=name,
        grid_spec=pltpu.PrefetchScalarGridSpec(num_scalar_prefetch=1, grid=(2,), in_specs=in_specs, out_specs=out_specs),
        out_shape=[jax.ShapeDtypeStruct((2,) + o.shape, f32) for o in owns],
        compiler_params=_params("parallel"))(ck_arr, *owns, *recvd)


SMALL = ("a_norm", "a_v_norm", "a_w_s", "a_b_s", "f_norm", "f_conv_w", "f_conv_b", "kv_norm", "k_norm",
         "b_norm", "b_q_norm", "b_sinks")
BIG = ("a_w_in", "a_w_out", "f_w_in", "f_w_out", "w_kv", "b_w_q", "b_w_o")
PACK_COLS = 1024
PACK_ROWS = 8 * N_STEPS


def _pack(parts, rows=PACK_ROWS):
    flat = jnp.concatenate([p.reshape(-1).astype(f32) for p in parts])
    pad = (-flat.shape[0]) % (rows * PACK_COLS)
    return jnp.pad(flat, (0, pad)).reshape(-1, PACK_COLS)


def _unpack(packed, shapes):
    flat = packed.reshape(-1)
    out, off = [], 0
    for s in shapes:
        size = math.prod(s)
        out.append(flat[off:off + size].reshape(s))
        off += size
    return out


def _ffn_fwd(x, g, w_in4, conv_w, conv_b, w_out, tag):
    f = w_out.shape[0]
    (h,), r = rms_fwd(x, [g], name=f"ffn{tag}_norm")
    pg = mm_nn(h, w_in4, name=f"ffn{tag}_in_gate", s0=0, ns=2)
    pu = mm_nn(h, w_in4, name=f"ffn{tag}_in_up", s0=2, ns=2)
    wg, wu = conv_w[:, :f], conv_w[:, f:]
    bg, bu = conv_b[None, :f], conv_b[None, f:]
    a = ffn_gate_fwd(pg, pu, wg, wu, bg, bu, name=f"ffn{tag}_gate")
    y = mm_nn(a, w_out[None], name=f"ffn{tag}_out", add=x)
    return y, (x, g, h, r, pg, pu, a, wg, wu, bg, bu)


def _ffn_bwd(dy, saved, w_in4, w_out, tag):
    x, g, h, r, pg, pu, a, wg, wu, bg, bu = saved
    f = w_out.shape[0]
    da = mm_nt([dy], w_out[None], name=f"ffn{tag}_dact", tko=f // 2)
    d_w_out = mm_tn(a, [dy], name=f"ffn{tag}_dwout", n_s=w_out.shape[1], tki=f // 2)
    dpg, dpu, sg, su = ffn_gate_bwd(pg, pu, da, wg, wu, bg, bu, name=f"ffn{tag}_dgate")
    d_w_in = mm_tn(h, [dpg, dpu], name=f"ffn{tag}_dwin", n_s=w_in4.shape[2])
    dh = mm_nt([dpg, dpu], w_in4, name=f"ffn{tag}_dh")
    dx, (dg,) = rms_bwd([dh], x, r, [g], dy, name=f"ffn{tag}_dnorm")
    d_conv_w = jnp.concatenate([sg[0:3], su[0:3]], axis=1)
    d_conv_b = jnp.concatenate([sg[3], su[3]], axis=0)
    return dx, dg, d_w_in, d_conv_w, d_conv_b, d_w_out


def kernel(x, a_norm, a_w_in, a_v_norm, a_w_s, a_b_s, a_w_out, f_norm, f_w_in, f_conv_w, f_conv_b, f_w_out, kv_norm, w_kv, k_norm, b_norm, b_w_q, b_q_norm, b_sinks, b_w_o, loss_target, m_a_norm, m_a_w_in, m_a_v_norm, m_a_w_s, m_a_b_s, m_a_w_out, m_f_norm, m_f_w_in, m_f_conv_w, m_f_conv_b, m_f_w_out, m_kv_norm, m_w_kv, m_k_norm, m_b_norm, m_b_w_q, m_b_q_norm, m_b_sinks, m_b_w_o, v_a_norm, v_a_w_in, v_a_v_norm, v_a_w_s, v_a_b_s, v_a_w_out, v_f_norm, v_f_w_in, v_f_conv_w, v_f_conv_b, v_f_w_out, v_kv_norm, v_w_kv, v_k_norm, v_b_norm, v_b_w_q, v_b_q_norm, v_b_sinks, v_b_w_o):
    args = dict(locals())
    weights = {n: args[n] for n in SMALL + BIG}
    moms = {n: args["m_" + n] for n in SMALL + BIG}
    vars_ = {n: args["v_" + n] for n in SMALL + BIG}
    t, d = x.shape[1], x.shape[2]
    xi, yi, ci = _place()
    chip = 2 * xi + yi

    big_local = [a_w_in[0], a_w_out[0], f_w_in, f_w_out, w_kv, b_w_q[0], b_w_o[0]]
    c_arr = jnp.stack([ci, chip]).astype(jnp.int32)
    k_arr = jnp.stack([chip]).astype(jnp.int32)
    b_ain, b_aout, b_fin0, b_fin1, b_fout0, b_fout1, b_kv, b_q, b_o = cast_into_slot(big_local, k_arr, name="cast_weights")
    small_cols = _pack([a_norm, a_v_norm, f_conv_w], rows=8)
    b_small = lax.dynamic_update_slice(jnp.zeros((N_SHARDS,) + small_cols.shape, f32), small_cols[None], (chip, 0, 0))
    gathered = gather_shards([b_small, b_ain, b_aout, b_fin0, b_fout0, b_kv, b_q, b_o, b_fin1, b_fout1],
                             name="gather_weights", split=[False] + [True] * 9)
    g_small, g_a_w_in, g_a_w_out, g_fin0, g_fout0, g_w_kv, g_b_w_q, g_b_w_o, g_fin1, g_fout1 = gathered
    ns_cols = a_norm.shape[1]
    nf_cols = f_conv_w.shape[2]
    parts = [_unpack(g_small[k], [a_norm.shape, a_v_norm.shape, f_conv_w.shape]) for k in range(N_SHARDS)]
    a_norm_f = jnp.concatenate([p[0] for p in parts], axis=1)
    a_v_norm_f = jnp.concatenate([p[1] for p in parts], axis=1)
    conv_w_f = jnp.concatenate([p[2] for p in parts], axis=2)
    w_a_in = g_a_w_in
    w_a_out = g_a_w_out.reshape(1, -1, d)
    w_f_in = [g_fin0, g_fin1]
    w_f_out = [g_fout0.reshape(-1, d), g_fout1.reshape(-1, d)]
    w_kv_f = g_w_kv.reshape(1, d, -1)
    w_q_f = g_b_w_q.reshape(1, d, -1)
    w_o_f = g_b_w_o.reshape(1, -1, d)

    x0 = x[0]
    tril = jnp.tril(jnp.ones((CHUNK, CHUNK), dtype=bool))
    wc = jnp.where(tril[None], a_w_s[0], 0.0).astype(bf16)
    bt = a_b_s[0].T
    kg2 = jnp.tile(k_norm, 2)[None]
    qg2 = jnp.tile(b_q_norm[0], 2)[None]

    (h_a,), r_a = rms_fwd(x0, [a_norm_f], name="a_norm")
    zu = mm_nn(h_a, w_a_in, name="a_in_u", s0=0, ns=2)
    zv = mm_nn(h_a, w_a_in, name="a_in_v", s0=2, ns=2)
    y_a = sgu_gate_fwd(zu, zv, a_v_norm_f, wc, bt, name="a_gate")
    x1 = mm_nn(y_a, w_a_out, name="a_out", add=x0)
    x2, ffn0 = _ffn_fwd(x1, f_norm[0:1], w_f_in[0], conv_w_f[0], f_conv_b[0], w_f_out[0], "0")
    (h_k, h_q), r_b = rms_fwd(x2, [kv_norm[None], b_norm], name="b_norm")
    kv = mm_nn(h_k, w_kv_f, name="kv_proj")
    k2, v2 = kv_post_fwd(kv, kg2, name="kv_post")
    qp = mm_nn(h_q, w_q_f, name="q_proj")
    qn = q_norm_fwd(qp, qg2, name="q_norm", scale=HEAD_DIM ** -0.5)
    o = attn_fwd(qn, k2, v2, b_sinks[0], name="attn")
    x3 = mm_nn(o, w_o_f, name="o_proj", add=x2)
    x4, ffn1 = _ffn_fwd(x3, f_norm[1:2], w_f_in[1], conv_w_f[1], f_conv_b[1], w_f_out[1], "1")
    dx4, sq = loss_head(x4, loss_target[0], name="loss")
    loss = lax.psum(0.5 * jnp.sum(sq) / d, ("x", "y", "c"))

    dx3, d_fn1, d_fwin1, d_cw1, d_cb1, d_fwout1 = _ffn_bwd(dx4, ffn1, w_f_in[1], w_f_out[1], "1")
    do = mm_nt([dx3], w_o_f, name="o_proj_dx")
    d_w_o = mm_tn(o, [dx3], name="o_proj_dw", n_s=d)
    dqn, dk2, dv2, dsink = attn_bwd(qn, k2, v2, do, b_sinks[0], name="attn_bwd")
    dqp, dqg = q_norm_bwd(dqn, qp, qg2, name="q_norm_bwd", scale=HEAD_DIM ** -0.5)
    dkv, dkg = kv_post_bwd(dk2, dv2, kv, kg2, name="kv_post_bwd")
    d_w_q = mm_tn(h_q, [dqp], name="q_proj_dw", n_s=w_q_f.shape[2])
    dh_q = mm_nt([dqp], w_q_f, name="q_proj_dx")
    d_w_kv = mm_tn(h_k, [dkv], name="kv_proj_dw", n_s=w_kv_f.shape[2])
    dh_k = mm_nt([dkv], w_kv_f, name="kv_proj_dx")
    dx2, (d_kvn, d_bn) = rms_bwd([dh_k, dh_q], x2, r_b, [kv_norm[None], b_norm], dx3, name="b_norm_bwd")
    dx1, d_fn0, d_fwin0, d_cw0, d_cb0, d_fwout0 = _ffn_bwd(dx2, ffn0, w_f_in[0], w_f_out[0], "0")
    dy_a = mm_nt([dx1], w_a_out, name="a_out_dx")
    d_w_aout = mm_tn(y_a, [dx1], name="a_out_dw", n_s=d)
    dzu, dzv, d_avn, d_ws, d_bt = sgu_gate_bwd(zu, zv, dy_a, a_v_norm_f, wc, bt, name="a_gate_bwd")
    d_w_ain = mm_tn(h_a, [dzu, dzv], name="a_in_dw", n_s=w_a_in.shape[2])
    dh_a = mm_nt([dzu, dzv], w_a_in, name="a_in_dx")
    dx0, (d_an,) = rms_bwd([dh_a], x0, r_a, [a_norm_f], dx1, name="a_norm_bwd")
    grad_x = dx0[None]

    sh = N_SHARDS
    units = [d_w_ain, d_w_aout.reshape(sh, -1, d), d_fwin0, d_fwin1, d_fwout0.reshape(sh, -1, d),
             d_fwout1.reshape(sh, -1, d), d_w_kv.reshape(sh, -1, d_w_kv.shape[2]), d_w_q.reshape(sh, -1, d_w_q.shape[2]),
             d_w_o.reshape(sh, -1, d)]
    other_bf = rs_cast_other_half(units, c_arr, name="rs_cast")
    from_sib = sibling_exchange(list(other_bf), name="rs_sibling")
    chip_bf, own = rs_add_sibling(units, from_sib, c_arr, name="rs_add")
    from_chips = chip_scatter(list(chip_bf), name="rs_chips")
    halves = rs_sum_chips(list(own), from_chips, c_arr, name="rs_sum")
    mg = [m.reshape(-1, m.shape[2]) for m in sibling_merge(list(halves), name="rs_merge")]
    big_w = [weights[n] for n in BIG]
    big_gs = [[mg[0]], [mg[1]], [mg[2], mg[3]], [mg[4], mg[5]], [mg[6]], [mg[7]], [mg[8]]]
    big_g, big_d, big_m, big_v = adamw(big_w, big_gs, [moms[n] for n in BIG], [vars_[n] for n in BIG], name="adamw_big")

    d_fn = jnp.concatenate([d_fn0, d_fn1], axis=0)
    d_cw = jnp.stack([d_cw0, d_cw1])
    d_cb = jnp.stack([d_cb0, d_cb1])
    d_kg = (dkg[0, :HEAD_DIM] + dkg[0, HEAD_DIM:])
    d_qg = (dqg[0, :HEAD_DIM] + dqg[0, HEAD_DIM:])[None]
    small_full = [d_an, d_avn, d_ws[None], d_bt.T[None], d_fn, d_cw, d_cb, d_kvn[0], d_kg, d_bn, d_qg,
                  dsink[:, :N_Q_HEADS]]
    full_shapes = [g.shape for g in small_full]
    packed = _pack(small_full)
    summed = sum_leading(all_gather_rows(packed, name="small_gather"), name="small_sum")
    small_g = _unpack(summed, full_shapes)
    small_g[0] = lax.dynamic_slice_in_dim(small_g[0], chip * ns_cols, ns_cols, axis=1)
    small_g[1] = lax.dynamic_slice_in_dim(small_g[1], chip * ns_cols, ns_cols, axis=1)
    small_g[5] = lax.dynamic_slice_in_dim(small_g[5], chip * nf_cols, nf_cols, axis=2)
    small_shapes = [weights[n].shape for n in SMALL]
    small_g = [g.reshape(s) for g, s in zip(small_g, small_shapes)]
    pw, pg_, pm, pv = (_pack(v) for v in ([weights[n] for n in SMALL], small_g, [moms[n] for n in SMALL],
                                          [vars_[n] for n in SMALL]))
    _, (sd,), (sm,), (sv,) = adamw([pw], [[pg_]], [pm], [pv], name="adamw_small")
    small_d, small_m, small_v = (_unpack(v, small_shapes) for v in (sd, sm, sv))

    out = {}
    for i, n in enumerate(SMALL):
        out[n] = (small_g[i], small_d[i], small_m[i], small_v[i])
    for i, n in enumerate(BIG):
        out[n] = (big_g[i], big_d[i], big_m[i], big_v[i])
    order = ["a_norm", "a_w_in", "a_v_norm", "a_w_s", "a_b_s", "a_w_out", "f_norm", "f_w_in", "f_conv_w", "f_conv_b",
             "f_w_out", "kv_norm", "w_kv", "k_norm", "b_norm", "b_w_q", "b_q_norm", "b_sinks", "b_w_o"]
    return (loss, grad_x, *[out[n][0] for n in order], *[out[n][1] for n in order],
            *[out[n][2] for n in order], *[out[n][3] for n in order])
```

```python
import functools
import math

import jax
import jax.numpy as jnp
from jax import lax
from jax.experimental import pallas as pl
from jax.experimental.pallas import tpu as pltpu

f32 = jnp.float32
bf16 = jnp.bfloat16
MESH = pl.DeviceIdType.MESH
ANY = pl.BlockSpec(memory_space=pl.ANY)

EPS = 1e-6
LANES = 128
CHUNK = 128
HEAD_DIM = 64
N_Q_HEADS = 16
N_KV_HEADS = 4
Q_PER_KV = N_Q_HEADS // N_KV_HEADS
N_SHARDS = 4
N_DEV = 8

ADAM_LR = 0.001
ADAM_B1 = 0.9
ADAM_B2 = 0.999
ADAM_EPS = 1e-08
ADAM_WD = 0.01
ADAM_STEP = 10
ADAM_C1 = 1.0 - ADAM_B1 ** ADAM_STEP
ADAM_C2 = 1.0 - ADAM_B2 ** ADAM_STEP

_INV_SQRT2 = 1.0 / math.sqrt(2.0)
_INV_SQRT2PI = 1.0 / math.sqrt(2.0 * math.pi)


def _params(*sem):
    return pltpu.CompilerParams(dimension_semantics=sem)


def _gelu(z):
    return 0.5 * z * (1.0 + lax.erf(z * _INV_SQRT2))


def _gelu_grad(z):
    return 0.5 * (1.0 + lax.erf(z * _INV_SQRT2)) + z * jnp.exp(-0.5 * z * z) * _INV_SQRT2PI


def _dot(a, b):
    return jnp.dot(a, b, preferred_element_type=f32)


def _dot_nt(a, b):
    return lax.dot_general(a, b, (((1,), (1,)), ((), ())), preferred_element_type=f32)


def _dot_tn(a, b):
    return lax.dot_general(a, b, (((0,), (0,)), ((), ())), preferred_element_type=f32)


def _dot_exact(a, b):
    return jnp.dot(a, b, preferred_element_type=f32, precision=lax.Precision.HIGHEST)


VMEM_TILE_BUDGET = 36 * 1024 * 1024
MAX_ROW_TILE = 2048


def _row_tile(m, fixed_bytes, row_bytes):
    tm = min(m, MAX_ROW_TILE)
    while tm > 256 and 2 * (fixed_bytes + tm * row_bytes) > VMEM_TILE_BUDGET:
        tm //= 2
    return tm


def _isz(a):
    return jnp.dtype(a.dtype).itemsize


def mm_nn(a, w3, *, name, s0=0, ns=None, add=None, out_dtype=f32):
    m, k = a.shape
    s_all, _, n_s = w3.shape
    ns = s_all if ns is None else ns
    tm = _row_tile(m, k * n_s * 2, k * _isz(a) + n_s * jnp.dtype(out_dtype).itemsize + (0 if add is None else n_s * 4))

    def body(*refs):
        if add is None:
            a_ref, w_ref, o_ref = refs
            acc = _dot(a_ref[...].astype(bf16), w_ref[0])
        else:
            a_ref, w_ref, add_ref, o_ref = refs
            acc = _dot(a_ref[...].astype(bf16), w_ref[0]) + add_ref[...]
        o_ref[...] = acc.astype(out_dtype)

    in_specs = [pl.BlockSpec((tm, k), lambda j, i: (i, 0)),
                pl.BlockSpec((1, k, n_s), lambda j, i: (s0 + j, 0, 0))]
    args = [a, w3]
    if add is not None:
        in_specs.append(pl.BlockSpec((tm, n_s), lambda j, i: (i, j)))
        args.append(add)
    return pl.pallas_call(
        body, name=name, grid=(ns, m // tm), in_specs=in_specs,
        out_specs=pl.BlockSpec((tm, n_s), lambda j, i: (i, j)),
        out_shape=jax.ShapeDtypeStruct((m, ns * n_s), out_dtype),
        compiler_params=_params("parallel", "parallel"))(*args)


def mm_nt(a_list, w3, *, name, tko=None, add=None, out_dtype=f32):
    s_all, k_out, n_s = w3.shape
    m = a_list[0].shape[0]
    na = len(a_list)
    spa = s_all // na
    tko = k_out if tko is None else tko
    tm = _row_tile(m, tko * n_s * 2, na * n_s * _isz(a_list[0]) + tko * 4 * (1 if add is None else 2))

    def body(*refs):
        a_refs = refs[:na]
        w_ref = refs[na]
        o_ref = refs[-1]
        s = pl.program_id(2)

        @pl.when(s == 0)
        def _():
            if add is None:
                o_ref[...] = jnp.zeros_like(o_ref)
            else:
                o_ref[...] = refs[na + 1][...]

        for idx in range(na):
            @pl.when(s // spa == idx)
            def _(idx=idx):
                o_ref[...] += _dot_nt(a_refs[idx][...].astype(bf16), w_ref[0])

    def a_map(idx):
        return lambda ko, i, s: (i, jnp.clip(s - idx * spa, 0, spa - 1))

    in_specs = [pl.BlockSpec((tm, n_s), a_map(idx)) for idx in range(na)]
    in_specs.append(pl.BlockSpec((1, tko, n_s), lambda ko, i, s: (s, ko, 0)))
    args = list(a_list) + [w3]
    if add is not None:
        in_specs.append(pl.BlockSpec((tm, tko), lambda ko, i, s: (i, ko)))
        args.append(add)
    return pl.pallas_call(
        body, name=name, grid=(k_out // tko, m // tm, s_all), in_specs=in_specs,
        out_specs=pl.BlockSpec((tm, tko), lambda ko, i, s: (i, ko)),
        out_shape=jax.ShapeDtypeStruct((m, k_out), out_dtype),
        compiler_params=_params("parallel", "parallel", "arbitrary"))(*args)


def mm_tn(a, b_list, *, name, n_s, tki=None):
    m, k_in = a.shape
    na = len(b_list)
    s_all = sum(b.shape[1] for b in b_list) // n_s
    spa = s_all // na
    tki = k_in if tki is None else tki
    tm = _row_tile(m, tki * n_s * 4, tki * _isz(a) + na * n_s * _isz(b_list[0]))

    def body(*refs):
        a_ref = refs[0]
        b_refs = refs[1:1 + na]
        o_ref = refs[-1]
        s = pl.program_id(0)
        r = pl.program_id(2)

        @pl.when(r == 0)
        def _():
            o_ref[...] = jnp.zeros_like(o_ref)

        for idx in range(na):
            @pl.when(s // spa == idx)
            def _(idx=idx):
                o_ref[0] += _dot_tn(a_ref[...].astype(bf16), b_refs[idx][...].astype(bf16))

    def b_map(idx):
        def index(s, ki, r):
            active = (s // spa) == idx
            return (jnp.where(active, r, 0), jnp.clip(s - idx * spa, 0, spa - 1))
        return index

    in_specs = [pl.BlockSpec((tm, tki), lambda s, ki, r: (r, ki))]
    in_specs += [pl.BlockSpec((tm, n_s), b_map(idx)) for idx in range(na)]
    return pl.pallas_call(
        body, name=name, grid=(s_all, k_in // tki, m // tm), in_specs=in_specs,
        out_specs=pl.BlockSpec((1, tki, n_s), lambda s, ki, r: (s, ki, 0)),
        out_shape=jax.ShapeDtypeStruct((s_all, k_in, n_s), f32),
        compiler_params=_params("parallel", "parallel", "arbitrary"))(a, *b_list)


def mm_residual(a, w, x, *, name, gains=(), target=None):
    m, k = a.shape
    d = w.shape[1]
    ng = len(gains)
    tm = _row_tile(m, k * d * 2, k * _isz(a) + d * 4 * 3 + ng * d * 2)

    def body(*refs):
        a_ref, w_ref, x_ref = refs[:3]
        y = _dot(a_ref[...].astype(bf16), w_ref[...]) + x_ref[...]
        if target is None:
            g_refs = refs[3:3 + ng]
            y_ref = refs[3 + ng]
            h_refs = refs[4 + ng:4 + 2 * ng]
            r_ref = refs[-1]
            y_ref[...] = y
            r = lax.rsqrt(jnp.mean(y * y, axis=1, keepdims=True) + EPS)
            yh = y * r
            for g_ref, h_ref in zip(g_refs, h_refs):
                h_ref[...] = (yh * g_ref[...]).astype(bf16)
            r_ref[...] = r
        else:
            t_ref, dy_ref, s_ref = refs[3:]
            i = pl.program_id(0)
            e = y - t_ref[...]
            dy_ref[...] = e * (1.0 / d)
            part = jnp.sum(e * e, axis=0, keepdims=True)

            @pl.when(i == 0)
            def _():
                s_ref[...] = part

            @pl.when(i > 0)
            def _():
                s_ref[...] += part

    row = pl.BlockSpec((tm, d), lambda i: (i, 0))
    vec = pl.BlockSpec((1, d), lambda i: (0, 0))
    in_specs = [pl.BlockSpec((tm, k), lambda i: (i, 0)), pl.BlockSpec((k, d), lambda i: (0, 0)), row]
    if target is None:
        outs = pl.pallas_call(
            body, name=name, grid=(m // tm,), in_specs=in_specs + [vec] * ng,
            out_specs=[row] * (1 + ng) + [pl.BlockSpec((tm, 1), lambda i: (i, 0))],
            out_shape=[jax.ShapeDtypeStruct((m, d), f32)] + [jax.ShapeDtypeStruct((m, d), bf16)] * ng
            + [jax.ShapeDtypeStruct((m, 1), f32)],
            compiler_params=_params("parallel"))(a, w, x, *gains)
        return outs[0], outs[1:1 + ng], outs[-1]
    return pl.pallas_call(
        body, name=name, grid=(m // tm,), in_specs=in_specs + [row], out_specs=[row, vec],
        out_shape=[jax.ShapeDtypeStruct((m, d), f32), jax.ShapeDtypeStruct((1, d), f32)],
        compiler_params=_params("arbitrary"))(a, w, x, target)


def rms_fwd(x, gains, *, name, tr=512):
    t, d = x.shape
    tr = min(tr, t)
    ng = len(gains)

    def body(*refs):
        x_ref = refs[0]
        g_refs = refs[1:1 + ng]
        h_refs = refs[1 + ng:1 + 2 * ng]
        r_ref = refs[-1]
        xv = x_ref[...]
        r = lax.rsqrt(jnp.mean(xv * xv, axis=1, keepdims=True) + EPS)
        xh = xv * r
        for g_ref, h_ref in zip(g_refs, h_refs):
            h_ref[...] = (xh * g_ref[...]).astype(bf16)
        r_ref[...] = r

    row = pl.BlockSpec((tr, d), lambda i: (i, 0))
    vec = pl.BlockSpec((1, d), lambda i: (0, 0))
    outs = pl.pallas_call(
        body, name=name, grid=(t // tr,), in_specs=[row] + [vec] * ng,
        out_specs=[row] * ng + [pl.BlockSpec((tr, 1), lambda i: (i, 0))],
        out_shape=[jax.ShapeDtypeStruct((t, d), bf16)] * ng + [jax.ShapeDtypeStruct((t, 1), f32)],
        compiler_params=_params("parallel"))(x, *gains)
    return outs[:ng], outs[ng]


def rms_bwd(dh_list, x, r, gains, dx_in, *, name, tr=512):
    t, d = x.shape
    tr = min(tr, t)
    ng = len(gains)

    def body(*refs):
        dh_refs = refs[:ng]
        x_ref, r_ref = refs[ng], refs[ng + 1]
        g_refs = refs[ng + 2:2 * ng + 2]
        dxin_ref = refs[2 * ng + 2]
        dx_ref = refs[2 * ng + 3]
        dg_refs = refs[2 * ng + 4:]
        i = pl.program_id(0)
        rv = r_ref[...]
        xh = x_ref[...] * rv
        acc = dxin_ref[...]
        for dh_ref, g_ref, dg_ref in zip(dh_refs, g_refs, dg_refs):
            dh = dh_ref[...]
            part = jnp.sum(dh * xh, axis=0, keepdims=True)

            @pl.when(i == 0)
            def _(dg_ref=dg_ref, part=part):
                dg_ref[...] = part

            @pl.when(i > 0)
            def _(dg_ref=dg_ref, part=part):
                dg_ref[...] += part

            tg = dh * g_ref[...]
            acc = acc + rv * (tg - xh * jnp.mean(tg * xh, axis=1, keepdims=True))
        dx_ref[...] = acc

    row = pl.BlockSpec((tr, d), lambda i: (i, 0))
    vec = pl.BlockSpec((1, d), lambda i: (0, 0))
    outs = pl.pallas_call(
        body, name=name, grid=(t // tr,),
        in_specs=[row] * ng + [row, pl.BlockSpec((tr, 1), lambda i: (i, 0))] + [vec] * ng + [row],
        out_specs=[row] + [vec] * ng,
        out_shape=[jax.ShapeDtypeStruct((t, d), f32)] + [jax.ShapeDtypeStruct((1, d), f32)] * ng,
        compiler_params=_params("arbitrary"))(*dh_list, x, r, *gains, dx_in)
    return outs[0], outs[1:]


def sgu_gate_fwd(zu, zv, gv, wc, bt, *, name, tr=512):
    t, w = zu.shape
    tr = min(tr, t)
    groups = w // LANES

    def body(zu_ref, zv_ref, gv_ref, wc_ref, bt_ref, y_ref):
        vp = _gelu(zv_ref[...])
        rv = lax.rsqrt(jnp.mean(vp * vp, axis=1, keepdims=True) + EPS)
        vb = (vp * rv * gv_ref[...]).astype(bf16)
        for c in range(tr // CHUNK):
            rows = slice(c * CHUNK, (c + 1) * CHUNK)
            for g in range(groups):
                cols = slice(g * LANES, (g + 1) * LANES)
                sv = _dot(wc_ref[g], vb[rows, cols]) + bt_ref[:, g:g + 1]
                y_ref[rows, cols] = (_gelu(zu_ref[rows, cols]) * sv).astype(bf16)

    row = pl.BlockSpec((tr, w), lambda i: (i, 0))
    return pl.pallas_call(
        body, name=name, grid=(t // tr,),
        in_specs=[row, row, pl.BlockSpec((1, w), lambda i: (0, 0)),
                  pl.BlockSpec((groups, CHUNK, CHUNK), lambda i: (0, 0, 0)),
                  pl.BlockSpec((CHUNK, groups), lambda i: (0, 0))],
        out_specs=row, out_shape=jax.ShapeDtypeStruct((t, w), bf16),
        compiler_params=_params("parallel"))(zu, zv, gv, wc, bt)


def sgu_gate_bwd(zu, zv, dy, gv, wc, bt, *, name, tr=512):
    t, w = zu.shape
    tr = min(tr, t)
    groups = w // LANES
    nsteps = t // tr

    def body(zu_ref, zv_ref, dy_ref, gv_ref, wc_ref, bt_ref,
             dzu_ref, dzv_ref, dgv_ref, dws_ref, dbt_ref, dv_ref, bacc_ref):
        i = pl.program_id(0)

        @pl.when(i == 0)
        def _():
            dgv_ref[...] = jnp.zeros_like(dgv_ref)
            dws_ref[...] = jnp.zeros_like(dws_ref)
            bacc_ref[...] = jnp.zeros_like(bacc_ref)

        zvv = zv_ref[...]
        vp = _gelu(zvv)
        rv = lax.rsqrt(jnp.mean(vp * vp, axis=1, keepdims=True) + EPS)
        vhat = vp * rv
        vb = (vhat * gv_ref[...]).astype(bf16)
        for c in range(tr // CHUNK):
            rows = slice(c * CHUNK, (c + 1) * CHUNK)
            for g in range(groups):
                cols = slice(g * LANES, (g + 1) * LANES)
                vblk = vb[rows, cols]
                sv = _dot(wc_ref[g], vblk) + bt_ref[:, g:g + 1]
                zub = zu_ref[rows, cols]
                dyb = dy_ref[rows, cols]
                dzu_ref[rows, cols] = (dyb * sv * _gelu_grad(zub)).astype(bf16)
                dsv = dyb * _gelu(zub)
                bacc_ref[:, cols] += dsv
                dsvb = dsv.astype(bf16)
                dv_ref[rows, cols] = _dot_tn(wc_ref[g], dsvb)
                dws_ref[g] += _dot_nt(dsvb, vblk)
        dv = dv_ref[...]
        dgv_ref[...] += jnp.sum(dv * vhat, axis=0, keepdims=True)
        tg = dv * gv_ref[...]
        dvp = rv * (tg - vhat * jnp.mean(tg * vhat, axis=1, keepdims=True))
        dzv_ref[...] = (dvp * _gelu_grad(zvv)).astype(bf16)

        @pl.when(i == nsteps - 1)
        def _():
            tt = lax.broadcasted_iota(jnp.int32, (CHUNK, CHUNK), 0)
            ss = lax.broadcasted_iota(jnp.int32, (CHUNK, CHUNK), 1)
            for g in range(groups):
                dws_ref[g] = jnp.where(ss <= tt, dws_ref[g], 0.0)
                dbt_ref[:, g:g + 1] = jnp.sum(bacc_ref[:, g * LANES:(g + 1) * LANES], axis=1, keepdims=True)

    row = pl.BlockSpec((tr, w), lambda i: (i, 0))
    full3 = pl.BlockSpec((groups, CHUNK, CHUNK), lambda i: (0, 0, 0))
    return pl.pallas_call(
        body, name=name, grid=(nsteps,),
        in_specs=[row, row, row, pl.BlockSpec((1, w), lambda i: (0, 0)), full3,
                  pl.BlockSpec((CHUNK, groups), lambda i: (0, 0))],
        out_specs=[row, row, pl.BlockSpec((1, w), lambda i: (0, 0)), full3,
                   pl.BlockSpec((CHUNK, groups), lambda i: (0, 0))],
        out_shape=[jax.ShapeDtypeStruct((t, w), bf16), jax.ShapeDtypeStruct((t, w), bf16),
                   jax.ShapeDtypeStruct((1, w), f32), jax.ShapeDtypeStruct((groups, CHUNK, CHUNK), f32),
                   jax.ShapeDtypeStruct((CHUNK, groups), f32)],
        scratch_shapes=[pltpu.VMEM((tr, w), f32), pltpu.VMEM((CHUNK, w), f32)],
        compiler_params=_params("arbitrary"))(zu, zv, dy, gv, wc, bt)


HALO = 8


def _shift_down(v, halo, k, first):
    r = pltpu.roll(v, k, 0)
    hh = jnp.where(first, 0.0, pltpu.roll(halo, k, 0))
    rid = lax.broadcasted_iota(jnp.int32, (HALO, v.shape[1]), 0)
    head = jnp.where(rid < k, hh, r[0:HALO])
    if v.shape[0] == HALO:
        return head
    return jnp.concatenate([head, r[HALO:]], axis=0)


def _shift_up(v, halo, k, last):
    n = v.shape[0]
    r = pltpu.roll(v, n - k, 0)
    hh = jnp.where(last, 0.0, pltpu.roll(halo, HALO - k, 0))
    rid = lax.broadcasted_iota(jnp.int32, (HALO, v.shape[1]), 0)
    tail = jnp.where(rid >= HALO - k, hh, r[n - HALO:])
    return jnp.concatenate([r[:n - HALO], tail], axis=0)


def _conv(p, halo, w_ref, b_ref, first):
    return (w_ref[2:3, :] * p + w_ref[1:2, :] * _shift_down(p, halo, 1, first)
            + w_ref[0:1, :] * _shift_down(p, halo, 2, first) + b_ref[...])


def _conv_specs(t, tr, tc):
    tile = pl.BlockSpec((tr, tc), lambda j, i: (i, j))
    prev = pl.BlockSpec((HALO, tc), lambda j, i: (jnp.maximum(i * (tr // HALO) - 1, 0), j))
    nxt = pl.BlockSpec((HALO, tc), lambda j, i: (jnp.minimum((i + 1) * (tr // HALO), t // HALO - 1), j))
    wspec = pl.BlockSpec((3, tc), lambda j, i: (0, j))
    bspec = pl.BlockSpec((1, tc), lambda j, i: (0, j))
    return tile, prev, nxt, wspec, bspec


BF16_ROWS = 16


def ffn_in_fused(h, w_in4, wg, wu, bg, bu, *, name):
    t, k = h.shape
    s_all, _, n_s = w_in4.shape
    half = s_all // 2
    tm = _row_tile(t, 2 * k * n_s * 2, k * 2 + 2 * n_s * 4 + n_s * 2)

    def body(h_ref, hh_ref, wg_ref, wu_ref, cg_ref, cu_ref, bg_ref, bu_ref, pg_ref, pu_ref, a_ref):
        first = pl.program_id(1) == 0
        hv, hh = h_ref[...], hh_ref[...]
        outs = []
        for w_ref, c_ref, b_ref, p_ref in ((wg_ref, cg_ref, bg_ref, pg_ref), (wu_ref, cu_ref, bu_ref, pu_ref)):
            p = _dot(hv, w_ref[0])
            p_ref[...] = p
            outs.append(_conv(p, _dot(hh, w_ref[0])[BF16_ROWS - HALO:], c_ref, b_ref, first))
        gate, up = outs
        a_ref[...] = (gate * jax.nn.sigmoid(gate) * up).astype(bf16)

    tile = pl.BlockSpec((tm, n_s), lambda j, i: (i, j))
    cw = pl.BlockSpec((3, n_s), lambda j, i: (0, j))
    cb = pl.BlockSpec((1, n_s), lambda j, i: (0, j))
    f = half * n_s
    return pl.pallas_call(
        body, name=name, grid=(half, t // tm),
        in_specs=[pl.BlockSpec((tm, k), lambda j, i: (i, 0)),
                  pl.BlockSpec((BF16_ROWS, k), lambda j, i: (jnp.maximum(i * (tm // BF16_ROWS) - 1, 0), 0)),
                  pl.BlockSpec((1, k, n_s), lambda j, i: (j, 0, 0)),
                  pl.BlockSpec((1, k, n_s), lambda j, i: (j + half, 0, 0)), cw, cw, cb, cb],
        out_specs=[tile, tile, tile],
        out_shape=[jax.ShapeDtypeStruct((t, f), f32), jax.ShapeDtypeStruct((t, f), f32),
                   jax.ShapeDtypeStruct((t, f), bf16)],
        compiler_params=_params("parallel", "parallel"))(h, h, w_in4, w_in4, wg, wu, bg, bu)


def _gate_grads(gate, up, dav):
    sg = jax.nn.sigmoid(gate)
    return dav * up * (sg * (1.0 + gate * (1.0 - sg))), dav * gate * sg


def ffn_gate_bwd(pg, pu, da, wg, wu, bg, bu, *, name, tr=1024, tc=256):
    t, f = pg.shape
    tr = min(tr, t)
    tile, prev, nxt, wspec, bspec = _conv_specs(t, tr, tc)
    nsteps = t // tr

    def body(pg_ref, pgh_ref, pgn_ref, pu_ref, puh_ref, pun_ref, da_ref, dan_ref, wg_ref, wu_ref, bg_ref, bu_ref,
             dg_ref, du_ref, sg_ref, su_ref):
        i = pl.program_id(1)
        first = i == 0
        last = i == nsteps - 1
        pgv, puv = pg_ref[...], pu_ref[...]
        pg1, pg2 = _shift_down(pgv, pgh_ref[...], 1, first), _shift_down(pgv, pgh_ref[...], 2, first)
        pu1, pu2 = _shift_down(puv, puh_ref[...], 1, first), _shift_down(puv, puh_ref[...], 2, first)
        gate = wg_ref[2:3, :] * pgv + wg_ref[1:2, :] * pg1 + wg_ref[0:1, :] * pg2 + bg_ref[...]
        up = wu_ref[2:3, :] * puv + wu_ref[1:2, :] * pu1 + wu_ref[0:1, :] * pu2 + bu_ref[...]
        dgate, dup = _gate_grads(gate, up, da_ref[...])
        gate_n = _conv(pgn_ref[...], pgv[tr - HALO:], wg_ref, bg_ref, False)
        up_n = _conv(pun_ref[...], puv[tr - HALO:], wu_ref, bu_ref, False)
        dgate_n, dup_n = _gate_grads(gate_n, up_n, dan_ref[...])
        for d, d_n, w_ref, o_ref in ((dgate, dgate_n, wg_ref, dg_ref), (dup, dup_n, wu_ref, du_ref)):
            o_ref[...] = (w_ref[2:3, :] * d + w_ref[1:2, :] * _shift_up(d, d_n, 1, last)
                          + w_ref[0:1, :] * _shift_up(d, d_n, 2, last)).astype(bf16)
        rid = lax.broadcasted_iota(jnp.int32, (8, tc), 0)
        for d, p0, p1, p2, s_ref in ((dgate, pgv, pg1, pg2, sg_ref), (dup, puv, pu1, pu2, su_ref)):
            sums = [jnp.sum(d * p2, axis=0, keepdims=True), jnp.sum(d * p1, axis=0, keepdims=True),
                    jnp.sum(d * p0, axis=0, keepdims=True), jnp.sum(d, axis=0, keepdims=True)]
            part = jnp.zeros((8, tc), f32)
            for k, sk in enumerate(sums):
                part = jnp.where(rid == k, sk, part)

            @pl.when(first)
            def _(s_ref=s_ref, part=part):
                s_ref[...] = part

            @pl.when(i > 0)
            def _(s_ref=s_ref, part=part):
                s_ref[...] += part

    stat = pl.BlockSpec((8, tc), lambda j, i: (0, j))
    return pl.pallas_call(
        body, name=name, grid=(f // tc, nsteps),
        in_specs=[tile, prev, nxt, tile, prev, nxt, tile, nxt, wspec, wspec, bspec, bspec],
        out_specs=[tile, tile, stat, stat],
        out_shape=[jax.ShapeDtypeStruct((t, f), bf16), jax.ShapeDtypeStruct((t, f), bf16),
                   jax.ShapeDtypeStruct((8, f), f32), jax.ShapeDtypeStruct((8, f), f32)],
        compiler_params=_params("parallel", "arbitrary"))(pg, pg, pg, pu, pu, pu, da, da, wg, wu, bg, bu)


def _head_mean_matrix():
    i = lax.broadcasted_iota(jnp.int32, (LANES, LANES), 0) // HEAD_DIM
    j = lax.broadcasted_iota(jnp.int32, (LANES, LANES), 1) // HEAD_DIM
    return jnp.where(i == j, 1.0 / HEAD_DIM, 0.0).astype(f32)


def _lane_half(shape):
    return (lax.broadcasted_iota(jnp.int32, shape, 1) % LANES) // HEAD_DIM


def q_norm_fwd(qp, g2, *, name, scale, tr=512):
    t, w = qp.shape
    tr = min(tr, t)

    def body(x_ref, g_ref, o_ref):
        bd = _head_mean_matrix()
        for cb in range(w // LANES):
            cols = slice(cb * LANES, (cb + 1) * LANES)
            xc = x_ref[:, cols]
            rh = lax.rsqrt(_dot_exact(xc * xc, bd) + EPS)
            o_ref[:, cols] = (xc * rh * g_ref[...] * scale).astype(bf16)

    row = pl.BlockSpec((tr, w), lambda i: (i, 0))
    return pl.pallas_call(
        body, name=name, grid=(t // tr,), in_specs=[row, pl.BlockSpec((1, LANES), lambda i: (0, 0))],
        out_specs=row, out_shape=jax.ShapeDtypeStruct((t, w), bf16),
        compiler_params=_params("parallel"))(qp, g2)


def q_norm_bwd(dq, qp, g2, *, name, scale, tr=512):
    t, w = qp.shape
    tr = min(tr, t)

    def body(dq_ref, x_ref, g_ref, o_ref, dg_ref):
        i = pl.program_id(0)
        bd = _head_mean_matrix()
        acc = jnp.zeros((1, LANES), f32)
        for cb in range(w // LANES):
            cols = slice(cb * LANES, (cb + 1) * LANES)
            xc = x_ref[:, cols]
            rh = lax.rsqrt(_dot_exact(xc * xc, bd) + EPS)
            xh = xc * rh
            dy = dq_ref[:, cols] * scale
            acc = acc + jnp.sum(dy * xh, axis=0, keepdims=True)
            tg = dy * g_ref[...]
            o_ref[:, cols] = (rh * (tg - xh * _dot_exact(tg * xh, bd))).astype(bf16)

        @pl.when(i == 0)
        def _():
            dg_ref[...] = acc

        @pl.when(i > 0)
        def _():
            dg_ref[...] += acc

    row = pl.BlockSpec((tr, w), lambda i: (i, 0))
    vec = pl.BlockSpec((1, LANES), lambda i: (0, 0))
    return pl.pallas_call(
        body, name=name, grid=(t // tr,), in_specs=[row, row, vec], out_specs=[row, vec],
        out_shape=[jax.ShapeDtypeStruct((t, w), bf16), jax.ShapeDtypeStruct((1, LANES), f32)],
        compiler_params=_params("arbitrary"))(dq, qp, g2)


def kv_post_fwd(kv, g2, *, name, tr=512):
    t, w = kv.shape
    tr = min(tr, t)
    kw = w // 2

    def body(x_ref, g_ref, k_ref, v_ref):
        bd = _head_mean_matrix()
        half = _lane_half((tr, LANES))
        for cb in range(kw // LANES):
            xc = x_ref[:, cb * LANES:(cb + 1) * LANES]
            rh = lax.rsqrt(_dot_exact(xc * xc, bd) + EPS)
            kn = xc * rh * g_ref[...]
            vc = x_ref[:, kw + cb * LANES:kw + (cb + 1) * LANES]
            for src, dst in ((kn, k_ref), (vc, v_ref)):
                sw = pltpu.roll(src, HEAD_DIM, 1)
                for hf in range(2):
                    blk = 2 * cb + hf
                    dst[:, blk * LANES:(blk + 1) * LANES] = jnp.where(half == hf, src, sw).astype(bf16)

    return pl.pallas_call(
        body, name=name, grid=(t // tr,),
        in_specs=[pl.BlockSpec((tr, w), lambda i: (i, 0)), pl.BlockSpec((1, LANES), lambda i: (0, 0))],
        out_specs=[pl.BlockSpec((tr, 2 * kw), lambda i: (i, 0))] * 2,
        out_shape=[jax.ShapeDtypeStruct((t, 2 * kw), bf16)] * 2,
        compiler_params=_params("parallel"))(kv, g2)


def kv_post_bwd(dk2, dv2, kv, g2, *, name, tr=512):
    t, w = kv.shape
    tr = min(tr, t)
    kw = w // 2

    def body(dk_ref, dv_ref, x_ref, g_ref, o_ref, dg_ref):
        i = pl.program_id(0)
        bd = _head_mean_matrix()
        half = _lane_half((tr, LANES))
        acc = jnp.zeros((1, LANES), f32)

        def fold(ref, cb):
            a = ref[:, (2 * cb) * LANES:(2 * cb + 1) * LANES]
            b = ref[:, (2 * cb + 1) * LANES:(2 * cb + 2) * LANES]
            return jnp.where(half == 0, a + pltpu.roll(a, HEAD_DIM, 1), b + pltpu.roll(b, HEAD_DIM, 1))

        for cb in range(kw // LANES):
            cols = slice(cb * LANES, (cb + 1) * LANES)
            xc = x_ref[:, cols]
            rh = lax.rsqrt(_dot_exact(xc * xc, bd) + EPS)
            xh = xc * rh
            dy = fold(dk_ref, cb)
            acc = acc + jnp.sum(dy * xh, axis=0, keepdims=True)
            tg = dy * g_ref[...]
            o_ref[:, cols] = (rh * (tg - xh * _dot_exact(tg * xh, bd))).astype(bf16)
            o_ref[:, kw + cb * LANES:kw + (cb + 1) * LANES] = fold(dv_ref, cb).astype(bf16)

        @pl.when(i == 0)
        def _():
            dg_ref[...] = acc

        @pl.when(i > 0)
        def _():
            dg_ref[...] += acc

    dup = pl.BlockSpec((tr, 2 * kw), lambda i: (i, 0))
    row = pl.BlockSpec((tr, w), lambda i: (i, 0))
    vec = pl.BlockSpec((1, LANES), lambda i: (0, 0))
    return pl.pallas_call(
        body, name=name, grid=(t // tr,), in_specs=[dup, dup, row, vec], out_specs=[row, vec],
        out_shape=[jax.ShapeDtypeStruct((t, w), bf16), jax.ShapeDtypeStruct((1, LANES), f32)],
        compiler_params=_params("arbitrary"))(dk2, dv2, kv, g2)


def _slope(h):
    return 2.0 ** (-8.0 * (h + 1) / N_Q_HEADS)


GROUP_ROWS = Q_PER_KV * CHUNK


def _band_mask(n):
    tq = lax.broadcasted_iota(jnp.int32, (GROUP_ROWS, 2 * CHUNK), 0) % CHUNK
    jk = lax.broadcasted_iota(jnp.int32, (GROUP_ROWS, 2 * CHUNK), 1)
    dist = tq + CHUNK - jk
    ok = (dist >= 0) & (dist < CHUNK) & jnp.logical_not((n == 0) & (jk < CHUNK))
    return dist.astype(f32), ok


def _band(ref, n, kh):
    p0 = pl.multiple_of(jnp.maximum(n - 1, 0) * CHUNK, CHUNK)
    c0 = pl.multiple_of(n * CHUNK, CHUNK)
    cols = slice(kh * LANES, (kh + 1) * LANES)
    return jnp.concatenate([ref[pl.ds(p0, CHUNK), cols], ref[pl.ds(c0, CHUNK), cols]], axis=0)


def _stack_heads(ref, kh, half):
    parts = []
    for cb in (2 * kh, 2 * kh + 1):
        xc = ref[:, cb * LANES:(cb + 1) * LANES].astype(f32)
        parts += [jnp.where(half == hf, xc, 0.0).astype(bf16) for hf in range(2)]
    return jnp.concatenate(parts, axis=0)


def _unstack_heads(x4, half):
    return (jnp.where(half == 0, x4[0:CHUNK], x4[CHUNK:2 * CHUNK]),
            jnp.where(half == 0, x4[2 * CHUNK:3 * CHUNK], x4[3 * CHUNK:]))


def _per_head_column(kh, values):
    grp = lax.broadcasted_iota(jnp.int32, (GROUP_ROWS, 1), 0) // CHUNK
    col = jnp.full((GROUP_ROWS, 1), values[0], f32)
    for g in range(1, Q_PER_KV):
        col = jnp.where(grp == g, values[g], col)
    return col


def _softmax_band(q4, kband, dist, ok, slope, sink):
    s = _dot_nt(q4, kband)
    s = jnp.where(ok, s - slope * dist, -jnp.inf)
    m = jnp.maximum(jnp.max(s, axis=1, keepdims=True), sink)
    e = jnp.exp(s - m)
    es = jnp.exp(sink - m)
    den = jnp.sum(e, axis=1, keepdims=True) + es
    return e / den, es / den


def attn_fwd(q, k2, v2, sinks, *, name):
    t, w = q.shape
    nb = t // CHUNK

    def body(sink_ref, q_ref, k_ref, v_ref, o_ref):
        n = pl.program_id(0)
        dist, ok = _band_mask(n)
        half = _lane_half((CHUNK, LANES))
        for kh in range(N_KV_HEADS):
            heads = [Q_PER_KV * kh + g for g in range(Q_PER_KV)]
            slope = _per_head_column(kh, [_slope(h) for h in heads])
            sink = _per_head_column(kh, [sink_ref[h] for h in heads])
            q4 = _stack_heads(q_ref, kh, half)
            p, _ = _softmax_band(q4, _band(k_ref, n, kh), dist, ok, slope, sink)
            o4 = _dot(p.astype(bf16), _band(v_ref, n, kh))
            lo, hi = _unstack_heads(o4, half)
            o_ref[:, (2 * kh) * LANES:(2 * kh + 1) * LANES] = lo.astype(bf16)
            o_ref[:, (2 * kh + 1) * LANES:(2 * kh + 2) * LANES] = hi.astype(bf16)

    full = pl.BlockSpec((t, k2.shape[1]), lambda n: (0, 0))
    return pl.pallas_call(
        body, name=name, grid=(nb,),
        in_specs=[pl.BlockSpec(memory_space=pltpu.SMEM), pl.BlockSpec((CHUNK, w), lambda n: (n, 0)), full, full],
        out_specs=pl.BlockSpec((CHUNK, w), lambda n: (n, 0)),
        out_shape=jax.ShapeDtypeStruct((t, w), bf16),
        compiler_params=_params("parallel"))(sinks, q, k2, v2)


def attn_bwd(q, k2, v2, do, sinks, *, name):
    t, w = q.shape
    nb = t // CHUNK
    kw = k2.shape[1]

    def body(sink_ref, q_ref, k_ref, v_ref, do_ref, dq_ref, dk_ref, dv_ref, ds_ref, kc_ref, vc_ref):
        n = pl.program_id(0)

        @pl.when(n == 0)
        def _():
            ds_ref[...] = jnp.zeros_like(ds_ref)
            kc_ref[...] = jnp.zeros_like(kc_ref)
            vc_ref[...] = jnp.zeros_like(vc_ref)
            dk_ref[...] = jnp.zeros_like(dk_ref)
            dv_ref[...] = jnp.zeros_like(dv_ref)

        @pl.when(n == nb)
        def _():
            dk_ref[...] = kc_ref[...]
            dv_ref[...] = vc_ref[...]

        @pl.when(n < nb)
        def _():
            dist, ok = _band_mask(n)
            half = _lane_half((CHUNK, LANES))
            lane = lax.broadcasted_iota(jnp.int32, (1, LANES), 1)
            sink_acc = jnp.zeros((1, LANES), f32)
            for kh in range(N_KV_HEADS):
                heads = [Q_PER_KV * kh + g for g in range(Q_PER_KV)]
                slope = _per_head_column(kh, [_slope(h) for h in heads])
                sink = _per_head_column(kh, [sink_ref[h] for h in heads])
                q4 = _stack_heads(q_ref, kh, half)
                do4 = _stack_heads(do_ref, kh, half)
                kband = _band(k_ref, n, kh)
                vband = _band(v_ref, n, kh)
                p, ps = _softmax_band(q4, kband, dist, ok, slope, sink)
                dp = _dot_nt(do4, vband)
                delta = jnp.sum(p * dp, axis=1, keepdims=True)
                dsb = (p * (dp - delta)).astype(bf16)
                sd = ps * delta
                for g, h in enumerate(heads):
                    part = jnp.sum(sd[g * CHUNK:(g + 1) * CHUNK], axis=0, keepdims=True)
                    sink_acc = sink_acc + jnp.where(lane == h, -part, 0.0)
                lo, hi = _unstack_heads(_dot(dsb, kband), half)
                dq_ref[:, (2 * kh) * LANES:(2 * kh + 1) * LANES] = lo
                dq_ref[:, (2 * kh + 1) * LANES:(2 * kh + 2) * LANES] = hi
                dkb = _dot_tn(dsb, q4)
                dvb = _dot_tn(p.astype(bf16), do4)
                cols = slice(kh * LANES, (kh + 1) * LANES)
                dk_ref[:, cols] = kc_ref[:, cols] + dkb[0:CHUNK]
                dv_ref[:, cols] = vc_ref[:, cols] + dvb[0:CHUNK]
                kc_ref[:, cols] = dkb[CHUNK:]
                vc_ref[:, cols] = dvb[CHUNK:]
            ds_ref[...] += sink_acc

    full = pl.BlockSpec((t, kw), lambda n: (0, 0))
    qblk = pl.BlockSpec((CHUNK, w), lambda n: (jnp.minimum(n, nb - 1), 0))
    kblk = pl.BlockSpec((CHUNK, kw), lambda n: (jnp.maximum(n - 1, 0), 0))
    return pl.pallas_call(
        body, name=name, grid=(nb + 1,),
        in_specs=[pl.BlockSpec(memory_space=pltpu.SMEM), qblk, full, full, qblk],
        out_specs=[qblk, kblk, kblk, pl.BlockSpec((1, LANES), lambda n: (0, 0))],
        out_shape=[jax.ShapeDtypeStruct((t, w), f32), jax.ShapeDtypeStruct((t, kw), f32),
                   jax.ShapeDtypeStruct((t, kw), f32), jax.ShapeDtypeStruct((1, LANES), f32)],
        scratch_shapes=[pltpu.VMEM((CHUNK, kw), f32), pltpu.VMEM((CHUNK, kw), f32)],
        compiler_params=_params("arbitrary"))(sinks, q, k2, v2, do)


def loss_head(y, target, *, name, tr=512):
    t, d = y.shape
    tr = min(tr, t)

    def body(y_ref, t_ref, dy_ref, s_ref):
        i = pl.program_id(0)
        e = y_ref[...] - t_ref[...]
        dy_ref[...] = e * (1.0 / d)
        part = jnp.sum(e * e, axis=0, keepdims=True)

        @pl.when(i == 0)
        def _():
            s_ref[...] = part

        @pl.when(i > 0)
        def _():
            s_ref[...] += part

    row = pl.BlockSpec((tr, d), lambda i: (i, 0))
    vec = pl.BlockSpec((1, d), lambda i: (0, 0))
    return pl.pallas_call(
        body, name=name, grid=(t // tr,), in_specs=[row, row], out_specs=[row, vec],
        out_shape=[jax.ShapeDtypeStruct((t, d), f32), jax.ShapeDtypeStruct((1, d), f32)],
        compiler_params=_params("arbitrary"))(y, target)


N_STEPS = 8


def _row_blocks(shape):
    if len(shape) == 2:
        r, c = shape
        return (r // N_STEPS, c), (lambda s: (s, 0))
    l, r, c = shape
    per = N_STEPS // l
    return (1, r // per, c), (lambda s: (s // per, s % per, 0))


CAST_STEPS = 4


def cast_into_slot(arrays, k_arr, *, name):
    in_specs, out_specs, out_shape, layers = [], [], [], []
    for a in arrays:
        r, c = a.shape[-2:]
        rb = r // CAST_STEPS
        if a.ndim == 2:
            in_specs.append(pl.BlockSpec((rb, c), lambda s, k: (s, 0)))
            layers.append(None)
        else:
            for l in range(a.shape[0]):
                in_specs.append(pl.BlockSpec((1, rb, c), lambda s, k, l=l: (l, s, 0)))
                layers.append(l)
        for _ in range(1 if a.ndim == 2 else a.shape[0]):
            out_specs.append(pl.BlockSpec((1, rb, c), lambda s, k: (k[0], s, 0)))
            out_shape.append(jax.ShapeDtypeStruct((N_SHARDS, r, c), bf16))
    n = len(in_specs)

    def body(k_ref, *refs):
        for i_ref, o_ref, l in zip(refs[:n], refs[n:], layers):
            o_ref[0] = (i_ref[...] if l is None else i_ref[0]).astype(bf16)

    args = []
    for a in arrays:
        args += [a] * (1 if a.ndim == 2 else a.shape[0])
    return pl.pallas_call(
        body, name=name,
        grid_spec=pltpu.PrefetchScalarGridSpec(num_scalar_prefetch=1, grid=(CAST_STEPS,),
                                               in_specs=in_specs, out_specs=out_specs),
        out_shape=out_shape, compiler_params=_params("parallel"))(k_arr, *args)


def adamw(ws, gs, ms, vs, *, name):
    n = len(ws)
    specs, g_specs, g_count = [], [], []
    for w, g_list in zip(ws, gs):
        blk, index = _row_blocks(w.shape)
        specs.append(pl.BlockSpec(blk, index))
        layers = len(g_list)
        per = N_STEPS // layers
        g_count.append(layers)
        for l in range(layers):
            g_specs.append(pl.BlockSpec(blk[-2:], lambda s, l=l, per=per: (jnp.where(s // per == l, s % per, 0), 0)))
    ng = len(g_specs)

    def body(*refs):
        s = pl.program_id(0)
        g_refs = refs[3 * n:3 * n + ng]
        outs = refs[3 * n + ng:]
        off = 0
        for i in range(n):
            w_ref, m_ref, v_ref = refs[i], refs[n + i], refs[2 * n + i]
            go_ref, d_ref, nm_ref, nv_ref = (outs[k * n + i] for k in range(4))
            layers = g_count[i]
            g = g_refs[off][...]
            for l in range(1, layers):
                g = jnp.where(s // (N_STEPS // layers) == l, g_refs[off + l][...], g)
            off += layers
            g = g.reshape(w_ref.shape)
            m = ADAM_B1 * m_ref[...] + (1.0 - ADAM_B1) * g
            v = ADAM_B2 * v_ref[...] + (1.0 - ADAM_B2) * (g * g)
            m_hat = m / ADAM_C1
            v_hat = v / ADAM_C2
            go_ref[...] = g
            d_ref[...] = -ADAM_LR * (m_hat / (jnp.sqrt(v_hat) + ADAM_EPS) + ADAM_WD * w_ref[...])
            nm_ref[...] = m
            nv_ref[...] = v

    outs = pl.pallas_call(
        body, name=name, grid=(N_STEPS,), in_specs=specs * 3 + g_specs, out_specs=specs * 4,
        out_shape=[jax.ShapeDtypeStruct(a.shape, f32) for a in ws] * 4,
        compiler_params=_params("parallel"))(*ws, *ms, *vs, *[g for g_list in gs for g in g_list])
    return [outs[k * n:(k + 1) * n] for k in range(4)]


def _place():
    return lax.axis_index("x"), lax.axis_index("y"), lax.axis_index("c")


def gather_shards(bufs, *, name, split):
    n = len(bufs)

    def body(*refs):
        bufs_ = refs[:n]
        isend, irecv, dsend, drecv = refs[2 * n:]
        x, y, c = _place()
        k = 2 * x + y
        peers = [(1 - x, y, c), (x, 1 - y, c), (1 - x, 1 - y, c)]
        peer_k = [2 * (1 - x) + y, 2 * x + (1 - y), 2 * (1 - x) + (1 - y)]

        def slab(a, q, h):
            if not split[a]:
                return bufs_[a].at[q]
            half = bufs_[a].shape[1] // 2
            return bufs_[a].at[q, pl.ds(pl.multiple_of(h * half, 16), half)]

        def ici(a, j, q):
            return pltpu.make_async_remote_copy(
                src_ref=slab(a, q, c), dst_ref=slab(a, q, c), send_sem=isend.at[3 * a + j], recv_sem=irecv.at[3 * a + j],
                device_id=peers[j], device_id_type=MESH)

        def d2d(a, j, h):
            return pltpu.make_async_remote_copy(
                src_ref=slab(a, peer_k[j], h), dst_ref=slab(a, peer_k[j], h), send_sem=dsend.at[3 * a + j],
                recv_sem=drecv.at[3 * a + j], device_id=(x, y, 1 - c), device_id_type=MESH)

        for a in range(n):
            for j in range(3):
                ici(a, j, k).start()
        for a in range(n):
            for j in range(3):
                ici(a, j, peer_k[j]).wait_recv()
                if split[a]:
                    d2d(a, j, c).start()
        for a in range(n):
            for j in range(3):
                if split[a]:
                    d2d(a, j, 1 - c).wait_recv()
        for a in range(n):
            for j in range(3):
                ici(a, j, k).wait_send()
                if split[a]:
                    d2d(a, j, c).wait_send()

    return pl.pallas_call(
        body, name=name, in_specs=[ANY] * n, out_specs=[ANY] * n,
        out_shape=[jax.ShapeDtypeStruct(b.shape, b.dtype) for b in bufs],
        input_output_aliases={i: i for i in range(n)},
        scratch_shapes=[pltpu.SemaphoreType.DMA((3 * n,))] * 4)(*bufs)


def sibling_exchange(arrays, *, name):
    n = len(arrays)

    def body(*refs):
        ins, outs = refs[:n], refs[n:2 * n]
        send, recv = refs[2 * n:]
        x, y, c = _place()

        def copy(a):
            return pltpu.make_async_remote_copy(
                src_ref=ins[a], dst_ref=outs[a], send_sem=send.at[a], recv_sem=recv.at[a],
                device_id=(x, y, 1 - c), device_id_type=MESH)

        for a in range(n):
            copy(a).start()
        for a in range(n):
            copy(a).wait_recv()
        for a in range(n):
            copy(a).wait_send()

    return pl.pallas_call(
        body, name=name, in_specs=[ANY] * n, out_specs=[ANY] * n,
        out_shape=[jax.ShapeDtypeStruct(a.shape, a.dtype) for a in arrays],
        scratch_shapes=[pltpu.SemaphoreType.DMA((n,)), pltpu.SemaphoreType.DMA((n,))])(*arrays)


def chip_scatter(arrays, everyone, *, name):
    n = len(arrays)

    def body(*refs):
        ins, ev_ref, outs, evo_ref = refs[:n], refs[n], refs[n + 1:2 * n + 1], refs[2 * n + 1]
        send, recv, esend, erecv, loc = refs[2 * n + 2:]
        x, y, c = _place()
        me = 4 * x + 2 * y + c
        peers = [(1 - x, y, c), (x, 1 - y, c), (1 - x, 1 - y, c)]
        peer_k = [2 * (1 - x) + y, 2 * x + (1 - y), 2 * (1 - x) + (1 - y)]
        masks = [(mx, my, mc) for mx in (0, 1) for my in (0, 1) for mc in (0, 1)][1:]

        def copy(a, j):
            return pltpu.make_async_remote_copy(
                src_ref=ins[a].at[peer_k[j]], dst_ref=outs[a].at[j], send_sem=send.at[3 * a + j],
                recv_sem=recv.at[3 * a + j], device_id=peers[j], device_id_type=MESH)

        def to_all(j, slot):
            return pltpu.make_async_remote_copy(
                src_ref=ev_ref, dst_ref=evo_ref.at[slot], send_sem=esend.at[j], recv_sem=erecv.at[j],
                device_id=(x ^ masks[j][0], y ^ masks[j][1], c ^ masks[j][2]), device_id_type=MESH)

        local = pltpu.make_async_copy(ev_ref, evo_ref.at[me], loc)
        local.start()
        for j in range(7):
            to_all(j, me).start()
        for a in range(n):
            for j in range(3):
                copy(a, j).start()
        for j in range(7):
            to_all(j, me ^ (4 * masks[j][0] + 2 * masks[j][1] + masks[j][2])).wait_recv()
        for a in range(n):
            for j in range(3):
                copy(a, j).wait_recv()
        for j in range(7):
            to_all(j, me).wait_send()
        for a in range(n):
            for j in range(3):
                copy(a, j).wait_send()
        local.wait()

    outs = pl.pallas_call(
        body, name=name, in_specs=[ANY] * (n + 1), out_specs=[ANY] * (n + 1),
        out_shape=[jax.ShapeDtypeStruct((3,) + a.shape[1:], a.dtype) for a in arrays]
        + [jax.ShapeDtypeStruct((N_DEV,) + everyone.shape, everyone.dtype)],
        scratch_shapes=[pltpu.SemaphoreType.DMA((3 * n,)), pltpu.SemaphoreType.DMA((3 * n,)),
                        pltpu.SemaphoreType.DMA((7,)), pltpu.SemaphoreType.DMA((7,)), pltpu.SemaphoreType.DMA])(
                            *arrays, everyone)
    return outs[:n], outs[n]


def sibling_merge(bufs, *, name):
    n = len(bufs)

    def body(*refs):
        bufs_ = refs[:n]
        send, recv = refs[2 * n:]
        x, y, c = _place()

        def copy(u, h):
            return pltpu.make_async_remote_copy(
                src_ref=bufs_[u].at[h], dst_ref=bufs_[u].at[h], send_sem=send.at[u], recv_sem=recv.at[u],
                device_id=(x, y, 1 - c), device_id_type=MESH)

        for u in range(n):
            copy(u, c).start()
        for u in range(n):
            copy(u, 1 - c).wait_recv()
        for u in range(n):
            copy(u, c).wait_send()

    return pl.pallas_call(
        body, name=name, in_specs=[ANY] * n, out_specs=[ANY] * n,
        out_shape=[jax.ShapeDtypeStruct(b.shape, b.dtype) for b in bufs],
        input_output_aliases={i: i for i in range(n)},
        scratch_shapes=[pltpu.SemaphoreType.DMA((n,)), pltpu.SemaphoreType.DMA((n,))])(*bufs)


def sum_leading(a, *, name):
    n, r, c = a.shape

    def body(a_ref, o_ref):
        acc = a_ref[0]
        for i in range(1, n):
            acc = acc + a_ref[i]
        o_ref[...] = acc

    rb = r // 2 if r % 16 == 0 else r
    return pl.pallas_call(
        body, name=name, grid=(r // rb,), in_specs=[pl.BlockSpec((n, rb, c), lambda i: (0, i, 0))],
        out_specs=pl.BlockSpec((rb, c), lambda i: (i, 0)), out_shape=jax.ShapeDtypeStruct((r, c), f32),
        compiler_params=_params("parallel"))(a)


def _half_rows(shape):
    return shape[1] // 2 // 2


def rs_cast_other_half(grads, c_arr, *, name):
    n = len(grads)

    def body(c_ref, *refs):
        for i_ref, o_ref in zip(refs[:n], refs[n:]):
            o_ref[...] = i_ref[...].astype(bf16)

    in_specs = [pl.BlockSpec((1, _half_rows(g.shape), g.shape[2]), lambda s, r, c_ref: (s, (1 - c_ref[0]) * 2 + r, 0))
                for g in grads]
    out_specs = [pl.BlockSpec((1, _half_rows(g.shape), g.shape[2]), lambda s, r, c_ref: (s, r, 0)) for g in grads]
    return pl.pallas_call(
        body, name=name,
        grid_spec=pltpu.PrefetchScalarGridSpec(num_scalar_prefetch=1, grid=(N_SHARDS, 2),
                                               in_specs=in_specs, out_specs=out_specs),
        out_shape=[jax.ShapeDtypeStruct((N_SHARDS, g.shape[1] // 2, g.shape[2]), bf16) for g in grads],
        compiler_params=_params("parallel", "parallel"))(c_arr, *grads)


def rs_add_sibling(grads, recvd, ck_arr, *, name):
    n = len(grads)

    def body(ck_ref, *refs):
        s = pl.program_id(1)
        for u in range(n):
            g_ref, r_ref = refs[u], refs[n + u]
            qb_ref, own_ref = refs[2 * n + u], refs[3 * n + u]
            q = g_ref[0] + r_ref[0].astype(f32)
            qb_ref[0] = q.astype(bf16)

            @pl.when(s == ck_ref[1])
            def _(own_ref=own_ref, q=q):
                own_ref[...] = q

    in_specs = [pl.BlockSpec((1, _half_rows(g.shape), g.shape[2]), lambda r, s, ck: (s, ck[0] * 2 + r, 0)) for g in grads]
    in_specs += [pl.BlockSpec((1, _half_rows(g.shape), g.shape[2]), lambda r, s, ck: (s, r, 0)) for g in grads]
    out_specs = [pl.BlockSpec((1, _half_rows(g.shape), g.shape[2]), lambda r, s, ck: (s, r, 0)) for g in grads]
    out_specs += [pl.BlockSpec((_half_rows(g.shape), g.shape[2]), lambda r, s, ck: (r, 0)) for g in grads]
    outs = pl.pallas_call(
        body, name=name,
        grid_spec=pltpu.PrefetchScalarGridSpec(num_scalar_prefetch=1, grid=(2, N_SHARDS),
                                               in_specs=in_specs, out_specs=out_specs),
        out_shape=[jax.ShapeDtypeStruct((N_SHARDS, g.shape[1] // 2, g.shape[2]), bf16) for g in grads]
        + [jax.ShapeDtypeStruct((g.shape[1] // 2, g.shape[2]), f32) for g in grads],
        compiler_params=_params("parallel", "arbitrary"))(ck_arr, *grads, *recvd)
    return outs[:n], outs[n:]


def rs_sum_chips(owns, recvd, ck_arr, *, name):
    n = len(owns)

    def body(ck_ref, *refs):
        for u in range(n):
            own_ref, r_ref, o_ref = refs[u], refs[n + u], refs[2 * n + u]
            o_ref[0] = ((own_ref[...] + r_ref[0].astype(f32)) + r_ref[1].astype(f32)) + r_ref[2].astype(f32)

    in_specs = [pl.BlockSpec((o.shape[0] // 2, o.shape[1]), lambda r, ck: (r, 0)) for o in owns]
    in_specs += [pl.BlockSpec((3, o.shape[0] // 2, o.shape[1]), lambda r, ck: (0, r, 0)) for o in owns]
    out_specs = [pl.BlockSpec((1, o.shape[0] // 2, o.shape[1]), lambda r, ck: (ck[0], r, 0)) for o in owns]
    return pl.pallas_call(
        body, name=name,
        grid_spec=pltpu.PrefetchScalarGridSpec(num_scalar_prefetch=1, grid=(2,), in_specs=in_specs, out_specs=out_specs),
        out_shape=[jax.ShapeDtypeStruct((2,) + o.shape, f32) for o in owns],
        compiler_params=_params("parallel"))(ck_arr, *owns, *recvd)


SMALL = ("a_norm", "a_v_norm", "a_w_s", "a_b_s", "f_norm", "f_conv_w", "f_conv_b", "kv_norm", "k_norm",
         "b_norm", "b_q_norm", "b_sinks")
BIG = ("a_w_in", "a_w_out", "f_w_in", "f_w_out", "w_kv", "b_w_q", "b_w_o")
PACK_COLS = 1024
PACK_ROWS = 8 * N_STEPS


def _pack(parts, rows=PACK_ROWS):
    flat = jnp.concatenate([p.reshape(-1).astype(f32) for p in parts])
    pad = (-flat.shape[0]) % (rows * PACK_COLS)
    return jnp.pad(flat, (0, pad)).reshape(-1, PACK_COLS)


def _unpack(packed, shapes):
    flat = packed.reshape(-1)
    out, off = [], 0
    for s in shapes:
        size = math.prod(s)
        out.append(flat[off:off + size].reshape(s))
        off += size
    return out


def _ffn_fwd(x, g, h, r, w_in4, conv_w, conv_b, f, tag):
    wg, wu = conv_w[:, :f], conv_w[:, f:]
    bg, bu = conv_b[None, :f], conv_b[None, f:]
    pg, pu, a = ffn_in_fused(h, w_in4, wg, wu, bg, bu, name=f"ffn{tag}_in")
    return a, (x, g, h, r, pg, pu, a, wg, wu, bg, bu)


def _ffn_bwd(dy, saved, w_in4, w_out, tag):
    x, g, h, r, pg, pu, a, wg, wu, bg, bu = saved
    f = w_out.shape[0]
    da = mm_nt([dy], w_out[None], name=f"ffn{tag}_dact", tko=f // 2)
    d_w_out = mm_tn(a, [dy], name=f"ffn{tag}_dwout", n_s=w_out.shape[1], tki=f // 2)
    dpg, dpu, sg, su = ffn_gate_bwd(pg, pu, da, wg, wu, bg, bu, name=f"ffn{tag}_dgate")
    d_w_in = mm_tn(h, [dpg, dpu], name=f"ffn{tag}_dwin", n_s=w_in4.shape[2])
    dh = mm_nt([dpg, dpu], w_in4, name=f"ffn{tag}_dh")
    dx, (dg,) = rms_bwd([dh], x, r, [g], dy, name=f"ffn{tag}_dnorm")
    d_conv_w = jnp.concatenate([sg[0:3], su[0:3]], axis=1)
    d_conv_b = jnp.concatenate([sg[3], su[3]], axis=0)
    return dx, dg, d_w_in, d_conv_w, d_conv_b, d_w_out


def kernel(x, a_norm, a_w_in, a_v_norm, a_w_s, a_b_s, a_w_out, f_norm, f_w_in, f_conv_w, f_conv_b, f_w_out, kv_norm, w_kv, k_norm, b_norm, b_w_q, b_q_norm, b_sinks, b_w_o, loss_target, m_a_norm, m_a_w_in, m_a_v_norm, m_a_w_s, m_a_b_s, m_a_w_out, m_f_norm, m_f_w_in, m_f_conv_w, m_f_conv_b, m_f_w_out, m_kv_norm, m_w_kv, m_k_norm, m_b_norm, m_b_w_q, m_b_q_norm, m_b_sinks, m_b_w_o, v_a_norm, v_a_w_in, v_a_v_norm, v_a_w_s, v_a_b_s, v_a_w_out, v_f_norm, v_f_w_in, v_f_conv_w, v_f_conv_b, v_f_w_out, v_kv_norm, v_w_kv, v_k_norm, v_b_norm, v_b_w_q, v_b_q_norm, v_b_sinks, v_b_w_o):
    args = dict(locals())
    weights = {n: args[n] for n in SMALL + BIG}
    moms = {n: args["m_" + n] for n in SMALL + BIG}
    vars_ = {n: args["v_" + n] for n in SMALL + BIG}
    t, d = x.shape[1], x.shape[2]
    xi, yi, ci = _place()
    chip = 2 * xi + yi

    big_local = [a_w_in[0], a_w_out[0], f_w_in, f_w_out, w_kv, b_w_q[0], b_w_o[0]]
    c_arr = jnp.stack([ci, chip]).astype(jnp.int32)
    k_arr = jnp.stack([chip]).astype(jnp.int32)
    b_ain, b_aout, b_fin0, b_fin1, b_fout0, b_fout1, b_kv, b_q, b_o = cast_into_slot(big_local, k_arr, name="cast_weights")
    small_cols = _pack([a_norm, a_v_norm, f_conv_w], rows=8)
    b_small = lax.dynamic_update_slice(jnp.zeros((N_SHARDS,) + small_cols.shape, f32), small_cols[None], (chip, 0, 0))
    gathered = gather_shards([b_small, b_ain, b_aout, b_fin0, b_fout0, b_kv, b_q, b_o, b_fin1, b_fout1],
                             name="gather_weights", split=[False] + [True] * 9)
    g_small, g_a_w_in, g_a_w_out, g_fin0, g_fout0, g_w_kv, g_b_w_q, g_b_w_o, g_fin1, g_fout1 = gathered
    ns_cols = a_norm.shape[1]
    nf_cols = f_conv_w.shape[2]
    parts = [_unpack(g_small[k], [a_norm.shape, a_v_norm.shape, f_conv_w.shape]) for k in range(N_SHARDS)]
    a_norm_f = jnp.concatenate([p[0] for p in parts], axis=1)
    a_v_norm_f = jnp.concatenate([p[1] for p in parts], axis=1)
    conv_w_f = jnp.concatenate([p[2] for p in parts], axis=2)
    w_a_in = g_a_w_in
    w_a_out = g_a_w_out.reshape(1, -1, d)
    w_f_in = [g_fin0, g_fin1]
    w_f_out = [g_fout0.reshape(-1, d), g_fout1.reshape(-1, d)]
    w_kv_f = g_w_kv.reshape(1, d, -1)
    w_q_f = g_b_w_q.reshape(1, d, -1)
    w_o_f = g_b_w_o.reshape(1, -1, d)

    x0 = x[0]
    tril = jnp.tril(jnp.ones((CHUNK, CHUNK), dtype=bool))
    wc = jnp.where(tril[None], a_w_s[0], 0.0).astype(bf16)
    bt = a_b_s[0].T
    kg2 = jnp.tile(k_norm, 2)[None]
    qg2 = jnp.tile(b_q_norm[0], 2)[None]

    (h_a,), r_a = rms_fwd(x0, [a_norm_f], name="a_norm")
    zu = mm_nn(h_a, w_a_in, name="a_in_u", s0=0, ns=2)
    zv = mm_nn(h_a, w_a_in, name="a_in_v", s0=2, ns=2)
    y_a = sgu_gate_fwd(zu, zv, a_v_norm_f, wc, bt, name="a_gate")
    f = w_f_out[0].shape[0]
    x1, (h_f0,), r_f0 = mm_residual(y_a, w_a_out[0], x0, name="a_out", gains=[f_norm[0:1]])
    a0, ffn0 = _ffn_fwd(x1, f_norm[0:1], h_f0, r_f0, w_f_in[0], conv_w_f[0], f_conv_b[0], f, "0")
    x2, (h_k, h_q), r_b = mm_residual(a0, w_f_out[0], x1, name="ffn0_out", gains=[kv_norm[None], b_norm])
    kv = mm_nn(h_k, w_kv_f, name="kv_proj")
    k2, v2 = kv_post_fwd(kv, kg2, name="kv_post")
    qp = mm_nn(h_q, w_q_f, name="q_proj")
    qn = q_norm_fwd(qp, qg2, name="q_norm", scale=HEAD_DIM ** -0.5)
    o = attn_fwd(qn, k2, v2, b_sinks[0], name="attn")
    x3, (h_f1,), r_f1 = mm_residual(o, w_o_f[0], x2, name="o_proj", gains=[f_norm[1:2]])
    a1, ffn1 = _ffn_fwd(x3, f_norm[1:2], h_f1, r_f1, w_f_in[1], conv_w_f[1], f_conv_b[1], f, "1")
    dx4, sq = mm_residual(a1, w_f_out[1], x3, name="ffn1_out", target=loss_target[0])
    loss_part = (0.5 * jnp.sum(sq) / d).reshape(1)

    dx3, d_fn1, d_fwin1, d_cw1, d_cb1, d_fwout1 = _ffn_bwd(dx4, ffn1, w_f_in[1], w_f_out[1], "1")
    do = mm_nt([dx3], w_o_f, name="o_proj_dx")
    d_w_o = mm_tn(o, [dx3], name="o_proj_dw", n_s=d)
    dqn, dk2, dv2, dsink = attn_bwd(qn, k2, v2, do, b_sinks[0], name="attn_bwd")
    dqp, dqg = q_norm_bwd(dqn, qp, qg2, name="q_norm_bwd", scale=HEAD_DIM ** -0.5)
    dkv, dkg = kv_post_bwd(dk2, dv2, kv, kg2, name="kv_post_bwd")
    d_w_q = mm_tn(h_q, [dqp], name="q_proj_dw", n_s=w_q_f.shape[2])
    dh_q = mm_nt([dqp], w_q_f, name="q_proj_dx")
    d_w_kv = mm_tn(h_k, [dkv], name="kv_proj_dw", n_s=w_kv_f.shape[2])
    dh_k = mm_nt([dkv], w_kv_f, name="kv_proj_dx")
    dx2, (d_kvn, d_bn) = rms_bwd([dh_k, dh_q], x2, r_b, [kv_norm[None], b_norm], dx3, name="b_norm_bwd")
    dx1, d_fn0, d_fwin0, d_cw0, d_cb0, d_fwout0 = _ffn_bwd(dx2, ffn0, w_f_in[0], w_f_out[0], "0")
    dy_a = mm_nt([dx1], w_a_out, name="a_out_dx")
    d_w_aout = mm_tn(y_a, [dx1], name="a_out_dw", n_s=d)
    dzu, dzv, d_avn, d_ws, d_bt = sgu_gate_bwd(zu, zv, dy_a, a_v_norm_f, wc, bt, name="a_gate_bwd")
    d_w_ain = mm_tn(h_a, [dzu, dzv], name="a_in_dw", n_s=w_a_in.shape[2])
    dh_a = mm_nt([dzu, dzv], w_a_in, name="a_in_dx")
    dx0, (d_an,) = rms_bwd([dh_a], x0, r_a, [a_norm_f], dx1, name="a_norm_bwd")
    grad_x = dx0[None]

    sh = N_SHARDS
    units = [d_w_ain, d_w_aout.reshape(sh, -1, d), d_fwin0, d_fwin1, d_fwout0.reshape(sh, -1, d),
             d_fwout1.reshape(sh, -1, d), d_w_kv.reshape(sh, -1, d_w_kv.shape[2]), d_w_q.reshape(sh, -1, d_w_q.shape[2]),
             d_w_o.reshape(sh, -1, d)]
    other_bf = rs_cast_other_half(units, c_arr, name="rs_cast")
    from_sib = sibling_exchange(list(other_bf), name="rs_sibling")
    chip_bf, own = rs_add_sibling(units, from_sib, c_arr, name="rs_add")
    d_fn = jnp.concatenate([d_fn0, d_fn1], axis=0)
    d_cw = jnp.stack([d_cw0, d_cw1])
    d_cb = jnp.stack([d_cb0, d_cb1])
    d_kg = (dkg[0, :HEAD_DIM] + dkg[0, HEAD_DIM:])
    d_qg = (dqg[0, :HEAD_DIM] + dqg[0, HEAD_DIM:])[None]
    small_full = [d_an, d_avn, d_ws[None], d_bt.T[None], d_fn, d_cw, d_cb, d_kvn[0], d_kg, d_bn, d_qg,
                  dsink[:, :N_Q_HEADS], loss_part]
    from_chips, from_all = chip_scatter(list(chip_bf), _pack(small_full), name="rs_chips")
    halves = rs_sum_chips(list(own), from_chips, c_arr, name="rs_sum")
    mg = [m.reshape(-1, m.shape[2]) for m in sibling_merge(list(halves), name="rs_merge")]
    big_w = [weights[n] for n in BIG]
    big_gs = [[mg[0]], [mg[1]], [mg[2], mg[3]], [mg[4], mg[5]], [mg[6]], [mg[7]], [mg[8]]]
    big_g, big_d, big_m, big_v = adamw(big_w, big_gs, [moms[n] for n in BIG], [vars_[n] for n in BIG], name="adamw_big")

    full_shapes = [g.shape for g in small_full]
    small_g = _unpack(sum_leading(from_all, name="small_sum"), full_shapes)
    loss = small_g.pop()[0]
    small_g[0] = lax.dynamic_slice_in_dim(small_g[0], chip * ns_cols, ns_cols, axis=1)
    small_g[1] = lax.dynamic_slice_in_dim(small_g[1], chip * ns_cols, ns_cols, axis=1)
    small_g[5] = lax.dynamic_slice_in_dim(small_g[5], chip * nf_cols, nf_cols, axis=2)
    small_shapes = [weights[n].shape for n in SMALL]
    small_g = [g.reshape(s) for g, s in zip(small_g, small_shapes)]
    pw, pg_, pm, pv = (_pack(v) for v in ([weights[n] for n in SMALL], small_g, [moms[n] for n in SMALL],
                                          [vars_[n] for n in SMALL]))
    _, (sd,), (sm,), (sv,) = adamw([pw], [[pg_]], [pm], [pv], name="adamw_small")
    small_d, small_m, small_v = (_unpack(v, small_shapes) for v in (sd, sm, sv))

    out = {}
    for i, n in enumerate(SMALL):
        out[n] = (small_g[i], small_d[i], small_m[i], small_v[i])
    for i, n in enumerate(BIG):
        out[n] = (big_g[i], big_d[i], big_m[i], big_v[i])
    order = ["a_norm", "a_w_in", "a_v_norm", "a_w_s", "a_b_s", "a_w_out", "f_norm", "f_w_in", "f_conv_w", "f_conv_b",
             "f_w_out", "kv_norm", "w_kv", "k_norm", "b_norm", "b_w_q", "b_q_norm", "b_sinks", "b_w_o"]
    return (loss, grad_x, *[out[n][0] for n in order], *[out[n][1] for n in order],
            *[out[n][2] for n in order], *[out[n][3] for n in order])
```

```python
import functools
import math

import jax
import jax.numpy as jnp
from jax import lax
from jax.experimental import pallas as pl
from jax.experimental.pallas import tpu as pltpu

f32 = jnp.float32
bf16 = jnp.bfloat16
MESH = pl.DeviceIdType.MESH
ANY = pl.BlockSpec(memory_space=pl.ANY)

EPS = 1e-6
LANES = 128
CHUNK = 128
HEAD_DIM = 64
N_Q_HEADS = 16
N_KV_HEADS = 4
Q_PER_KV = N_Q_HEADS // N_KV_HEADS
N_SHARDS = 4
N_DEV = 8

ADAM_LR = 0.001
ADAM_B1 = 0.9
ADAM_B2 = 0.999
ADAM_EPS = 1e-08
ADAM_WD = 0.01
ADAM_STEP = 10
ADAM_C1 = 1.0 - ADAM_B1 ** ADAM_STEP
ADAM_C2 = 1.0 - ADAM_B2 ** ADAM_STEP

_INV_SQRT2 = 1.0 / math.sqrt(2.0)
_INV_SQRT2PI = 1.0 / math.sqrt(2.0 * math.pi)


def _params(*sem):
    return pltpu.CompilerParams(dimension_semantics=sem)


def _gelu(z):
    return 0.5 * z * (1.0 + lax.erf(z * _INV_SQRT2))


def _gelu_grad(z):
    return 0.5 * (1.0 + lax.erf(z * _INV_SQRT2)) + z * jnp.exp(-0.5 * z * z) * _INV_SQRT2PI


def _dot(a, b):
    return jnp.dot(a, b, preferred_element_type=f32)


def _dot_nt(a, b):
    return lax.dot_general(a, b, (((1,), (1,)), ((), ())), preferred_element_type=f32)


def _dot_tn(a, b):
    return lax.dot_general(a, b, (((0,), (0,)), ((), ())), preferred_element_type=f32)


def _dot_exact(a, b):
    return jnp.dot(a, b, preferred_element_type=f32, precision=lax.Precision.HIGHEST)


VMEM_TILE_BUDGET = 36 * 1024 * 1024
MAX_ROW_TILE = 2048


def _row_tile(m, fixed_bytes, row_bytes):
    tm = min(m, MAX_ROW_TILE)
    while tm > 256 and 2 * (fixed_bytes + tm * row_bytes) > VMEM_TILE_BUDGET:
        tm //= 2
    return tm


def _isz(a):
    return jnp.dtype(a.dtype).itemsize


def mm_nn(a, w3, *, name, s0=0, ns=None, add=None, out_dtype=f32):
    m, k = a.shape
    s_all, _, n_s = w3.shape
    ns = s_all if ns is None else ns
    tm = _row_tile(m, k * n_s * 2, k * _isz(a) + n_s * jnp.dtype(out_dtype).itemsize + (0 if add is None else n_s * 4))

    def body(*refs):
        if add is None:
            a_ref, w_ref, o_ref = refs
            acc = _dot(a_ref[...].astype(bf16), w_ref[0])
        else:
            a_ref, w_ref, add_ref, o_ref = refs
            acc = _dot(a_ref[...].astype(bf16), w_ref[0]) + add_ref[...]
        o_ref[...] = acc.astype(out_dtype)

    in_specs = [pl.BlockSpec((tm, k), lambda j, i: (i, 0)),
                pl.BlockSpec((1, k, n_s), lambda j, i: (s0 + j, 0, 0))]
    args = [a, w3]
    if add is not None:
        in_specs.append(pl.BlockSpec((tm, n_s), lambda j, i: (i, j)))
        args.append(add)
    return pl.pallas_call(
        body, name=name, grid=(ns, m // tm), in_specs=in_specs,
        out_specs=pl.BlockSpec((tm, n_s), lambda j, i: (i, j)),
        out_shape=jax.ShapeDtypeStruct((m, ns * n_s), out_dtype),
        compiler_params=_params("parallel", "parallel"))(*args)


def mm_nt(a_list, w3, *, name, tko=None, add=None, out_dtype=f32):
    s_all, k_out, n_s = w3.shape
    m = a_list[0].shape[0]
    na = len(a_list)
    spa = s_all // na
    tko = k_out if tko is None else tko
    tm = _row_tile(m, tko * n_s * 2, na * n_s * _isz(a_list[0]) + tko * 4 * (1 if add is None else 2))

    def body(*refs):
        a_refs = refs[:na]
        w_ref = refs[na]
        o_ref = refs[-1]
        s = pl.program_id(2)

        @pl.when(s == 0)
        def _():
            if add is None:
                o_ref[...] = jnp.zeros_like(o_ref)
            else:
                o_ref[...] = refs[na + 1][...]

        for idx in range(na):
            @pl.when(s // spa == idx)
            def _(idx=idx):
                o_ref[...] += _dot_nt(a_refs[idx][...].astype(bf16), w_ref[0])

    def a_map(idx):
        return lambda ko, i, s: (i, jnp.clip(s - idx * spa, 0, spa - 1))

    in_specs = [pl.BlockSpec((tm, n_s), a_map(idx)) for idx in range(na)]
    in_specs.append(pl.BlockSpec((1, tko, n_s), lambda ko, i, s: (s, ko, 0)))
    args = list(a_list) + [w3]
    if add is not None:
        in_specs.append(pl.BlockSpec((tm, tko), lambda ko, i, s: (i, ko)))
        args.append(add)
    return pl.pallas_call(
        body, name=name, grid=(k_out // tko, m // tm, s_all), in_specs=in_specs,
        out_specs=pl.BlockSpec((tm, tko), lambda ko, i, s: (i, ko)),
        out_shape=jax.ShapeDtypeStruct((m, k_out), out_dtype),
        compiler_params=_params("parallel", "parallel", "arbitrary"))(*args)


def mm_tn(a, b_list, *, name, n_s, tki=None):
    m, k_in = a.shape
    na = len(b_list)
    s_all = sum(b.shape[1] for b in b_list) // n_s
    spa = s_all // na
    tki = k_in if tki is None else tki
    tm = _row_tile(m, tki * n_s * 4, tki * _isz(a) + na * n_s * _isz(b_list[0]))

    def body(*refs):
        a_ref = refs[0]
        b_refs = refs[1:1 + na]
        o_ref = refs[-1]
        s = pl.program_id(0)
        r = pl.program_id(2)

        @pl.when(r == 0)
        def _():
            o_ref[...] = jnp.zeros_like(o_ref)

        for idx in range(na):
            @pl.when(s // spa == idx)
            def _(idx=idx):
                o_ref[0] += _dot_tn(a_ref[...].astype(bf16), b_refs[idx][...].astype(bf16))

    def b_map(idx):
        def index(s, ki, r):
            active = (s // spa) == idx
            return (jnp.where(active, r, 0), jnp.clip(s - idx * spa, 0, spa - 1))
        return index

    in_specs = [pl.BlockSpec((tm, tki), lambda s, ki, r: (r, ki))]
    in_specs += [pl.BlockSpec((tm, n_s), b_map(idx)) for idx in range(na)]
    return pl.pallas_call(
        body, name=name, grid=(s_all, k_in // tki, m // tm), in_specs=in_specs,
        out_specs=pl.BlockSpec((1, tki, n_s), lambda s, ki, r: (s, ki, 0)),
        out_shape=jax.ShapeDtypeStruct((s_all, k_in, n_s), f32),
        compiler_params=_params("parallel", "parallel", "arbitrary"))(a, *b_list)


def mm_residual(a, w, x, *, name, gains=(), target=None):
    m, k = a.shape
    d = w.shape[1]
    ng = len(gains)
    tm = _row_tile(m, k * d * 2, k * _isz(a) + d * 4 * 3 + ng * d * 2)

    def body(*refs):
        a_ref, w_ref, x_ref = refs[:3]
        y = _dot(a_ref[...].astype(bf16), w_ref[...]) + x_ref[...]
        if target is None:
            g_refs = refs[3:3 + ng]
            y_ref = refs[3 + ng]
            h_refs = refs[4 + ng:4 + 2 * ng]
            r_ref = refs[-1]
            y_ref[...] = y
            r = lax.rsqrt(jnp.mean(y * y, axis=1, keepdims=True) + EPS)
            yh = y * r
            for g_ref, h_ref in zip(g_refs, h_refs):
                h_ref[...] = (yh * g_ref[...]).astype(bf16)
            r_ref[...] = r
        else:
            t_ref, dy_ref, s_ref = refs[3:]
            i = pl.program_id(0)
            e = y - t_ref[...]
            dy_ref[...] = e * (1.0 / d)
            part = jnp.sum(e * e, axis=0, keepdims=True)

            @pl.when(i == 0)
            def _():
                s_ref[...] = part

            @pl.when(i > 0)
            def _():
                s_ref[...] += part

    row = pl.BlockSpec((tm, d), lambda i: (i, 0))
    vec = pl.BlockSpec((1, d), lambda i: (0, 0))
    in_specs = [pl.BlockSpec((tm, k), lambda i: (i, 0)), pl.BlockSpec((k, d), lambda i: (0, 0)), row]
    if target is None:
        outs = pl.pallas_call(
            body, name=name, grid=(m // tm,), in_specs=in_specs + [vec] * ng,
            out_specs=[row] * (1 + ng) + [pl.BlockSpec((tm, 1), lambda i: (i, 0))],
            out_shape=[jax.ShapeDtypeStruct((m, d), f32)] + [jax.ShapeDtypeStruct((m, d), bf16)] * ng
            + [jax.ShapeDtypeStruct((m, 1), f32)],
            compiler_params=_params("parallel"))(a, w, x, *gains)
        return outs[0], outs[1:1 + ng], outs[-1]
    return pl.pallas_call(
        body, name=name, grid=(m // tm,), in_specs=in_specs + [row], out_specs=[row, vec],
        out_shape=[jax.ShapeDtypeStruct((m, d), f32), jax.ShapeDtypeStruct((1, d), f32)],
        compiler_params=_params("arbitrary"))(a, w, x, target)


def rms_fwd(x, gains, *, name, tr=512):
    t, d = x.shape
    tr = min(tr, t)
    ng = len(gains)

    def body(*refs):
        x_ref = refs[0]
        g_refs = refs[1:1 + ng]
        h_refs = refs[1 + ng:1 + 2 * ng]
        r_ref = refs[-1]
        xv = x_ref[...]
        r = lax.rsqrt(jnp.mean(xv * xv, axis=1, keepdims=True) + EPS)
        xh = xv * r
        for g_ref, h_ref in zip(g_refs, h_refs):
            h_ref[...] = (xh * g_ref[...]).astype(bf16)
        r_ref[...] = r

    row = pl.BlockSpec((tr, d), lambda i: (i, 0))
    vec = pl.BlockSpec((1, d), lambda i: (0, 0))
    outs = pl.pallas_call(
        body, name=name, grid=(t // tr,), in_specs=[row] + [vec] * ng,
        out_specs=[row] * ng + [pl.BlockSpec((tr, 1), lambda i: (i, 0))],
        out_shape=[jax.ShapeDtypeStruct((t, d), bf16)] * ng + [jax.ShapeDtypeStruct((t, 1), f32)],
        compiler_params=_params("parallel"))(x, *gains)
    return outs[:ng], outs[ng]


def rms_bwd(dh_list, x, r, gains, dx_in, *, name, tr=512):
    t, d = x.shape
    tr = min(tr, t)
    ng = len(gains)

    def body(*refs):
        dh_refs = refs[:ng]
        x_ref, r_ref = refs[ng], refs[ng + 1]
        g_refs = refs[ng + 2:2 * ng + 2]
        dxin_ref = refs[2 * ng + 2]
        dx_ref = refs[2 * ng + 3]
        dg_refs = refs[2 * ng + 4:]
        i = pl.program_id(0)
        rv = r_ref[...]
        xh = x_ref[...] * rv
        acc = dxin_ref[...]
        for dh_ref, g_ref, dg_ref in zip(dh_refs, g_refs, dg_refs):
            dh = dh_ref[...]
            part = jnp.sum(dh * xh, axis=0, keepdims=True)

            @pl.when(i == 0)
            def _(dg_ref=dg_ref, part=part):
                dg_ref[...] = part

            @pl.when(i > 0)
            def _(dg_ref=dg_ref, part=part):
                dg_ref[...] += part

            tg = dh * g_ref[...]
            acc = acc + rv * (tg - xh * jnp.mean(tg * xh, axis=1, keepdims=True))
        dx_ref[...] = acc

    row = pl.BlockSpec((tr, d), lambda i: (i, 0))
    vec = pl.BlockSpec((1, d), lambda i: (0, 0))
    outs = pl.pallas_call(
        body, name=name, grid=(t // tr,),
        in_specs=[row] * ng + [row, pl.BlockSpec((tr, 1), lambda i: (i, 0))] + [vec] * ng + [row],
        out_specs=[row] + [vec] * ng,
        out_shape=[jax.ShapeDtypeStruct((t, d), f32)] + [jax.ShapeDtypeStruct((1, d), f32)] * ng,
        compiler_params=_params("arbitrary"))(*dh_list, x, r, *gains, dx_in)
    return outs[0], outs[1:]


def sgu_gate_fwd(zu, zv, gv, wc, bt, *, name, tr=512):
    t, w = zu.shape
    tr = min(tr, t)
    groups = w // LANES

    def body(zu_ref, zv_ref, gv_ref, wc_ref, bt_ref, y_ref):
        vp = _gelu(zv_ref[...])
        rv = lax.rsqrt(jnp.mean(vp * vp, axis=1, keepdims=True) + EPS)
        vb = (vp * rv * gv_ref[...]).astype(bf16)
        for c in range(tr // CHUNK):
            rows = slice(c * CHUNK, (c + 1) * CHUNK)
            for g in range(groups):
                cols = slice(g * LANES, (g + 1) * LANES)
                sv = _dot(wc_ref[g], vb[rows, cols]) + bt_ref[:, g:g + 1]
                y_ref[rows, cols] = (_gelu(zu_ref[rows, cols]) * sv).astype(bf16)

    row = pl.BlockSpec((tr, w), lambda i: (i, 0))
    return pl.pallas_call(
        body, name=name, grid=(t // tr,),
        in_specs=[row, row, pl.BlockSpec((1, w), lambda i: (0, 0)),
                  pl.BlockSpec((groups, CHUNK, CHUNK), lambda i: (0, 0, 0)),
                  pl.BlockSpec((CHUNK, groups), lambda i: (0, 0))],
        out_specs=row, out_shape=jax.ShapeDtypeStruct((t, w), bf16),
        compiler_params=_params("parallel"))(zu, zv, gv, wc, bt)


def sgu_gate_bwd(zu, zv, dy, gv, wc, bt, *, name, tr=512):
    t, w = zu.shape
    tr = min(tr, t)
    groups = w // LANES
    nsteps = t // tr

    def body(zu_ref, zv_ref, dy_ref, gv_ref, wc_ref, bt_ref,
             dzu_ref, dzv_ref, dgv_ref, dws_ref, dbt_ref, dv_ref, bacc_ref):
        i = pl.program_id(0)

        @pl.when(i == 0)
        def _():
            dgv_ref[...] = jnp.zeros_like(dgv_ref)
            dws_ref[...] = jnp.zeros_like(dws_ref)
            bacc_ref[...] = jnp.zeros_like(bacc_ref)

        zvv = zv_ref[...]
        vp = _gelu(zvv)
        rv = lax.rsqrt(jnp.mean(vp * vp, axis=1, keepdims=True) + EPS)
        vhat = vp * rv
        vb = (vhat * gv_ref[...]).astype(bf16)
        for c in range(tr // CHUNK):
            rows = slice(c * CHUNK, (c + 1) * CHUNK)
            for g in range(groups):
                cols = slice(g * LANES, (g + 1) * LANES)
                vblk = vb[rows, cols]
                sv = _dot(wc_ref[g], vblk) + bt_ref[:, g:g + 1]
                zub = zu_ref[rows, cols]
                dyb = dy_ref[rows, cols]
                dzu_ref[rows, cols] = (dyb * sv * _gelu_grad(zub)).astype(bf16)
                dsv = dyb * _gelu(zub)
                bacc_ref[:, cols] += dsv
                dsvb = dsv.astype(bf16)
                dv_ref[rows, cols] = _dot_tn(wc_ref[g], dsvb)
                dws_ref[g] += _dot_nt(dsvb, vblk)
        dv = dv_ref[...]
        dgv_ref[...] += jnp.sum(dv * vhat, axis=0, keepdims=True)
        tg = dv * gv_ref[...]
        dvp = rv * (tg - vhat * jnp.mean(tg * vhat, axis=1, keepdims=True))
        dzv_ref[...] = (dvp * _gelu_grad(zvv)).astype(bf16)

        @pl.when(i == nsteps - 1)
        def _():
            tt = lax.broadcasted_iota(jnp.int32, (CHUNK, CHUNK), 0)
            ss = lax.broadcasted_iota(jnp.int32, (CHUNK, CHUNK), 1)
            for g in range(groups):
                dws_ref[g] = jnp.where(ss <= tt, dws_ref[g], 0.0)
                dbt_ref[:, g:g + 1] = jnp.sum(bacc_ref[:, g * LANES:(g + 1) * LANES], axis=1, keepdims=True)

    row = pl.BlockSpec((tr, w), lambda i: (i, 0))
    full3 = pl.BlockSpec((groups, CHUNK, CHUNK), lambda i: (0, 0, 0))
    return pl.pallas_call(
        body, name=name, grid=(nsteps,),
        in_specs=[row, row, row, pl.BlockSpec((1, w), lambda i: (0, 0)), full3,
                  pl.BlockSpec((CHUNK, groups), lambda i: (0, 0))],
        out_specs=[row, row, pl.BlockSpec((1, w), lambda i: (0, 0)), full3,
                   pl.BlockSpec((CHUNK, groups), lambda i: (0, 0))],
        out_shape=[jax.ShapeDtypeStruct((t, w), bf16), jax.ShapeDtypeStruct((t, w), bf16),
                   jax.ShapeDtypeStruct((1, w), f32), jax.ShapeDtypeStruct((groups, CHUNK, CHUNK), f32),
                   jax.ShapeDtypeStruct((CHUNK, groups), f32)],
        scratch_shapes=[pltpu.VMEM((tr, w), f32), pltpu.VMEM((CHUNK, w), f32)],
        compiler_params=_params("arbitrary"))(zu, zv, dy, gv, wc, bt)


HALO = 8


def _shift_down(v, halo, k, first):
    r = pltpu.roll(v, k, 0)
    hh = jnp.where(first, 0.0, pltpu.roll(halo, k, 0))
    rid = lax.broadcasted_iota(jnp.int32, (HALO, v.shape[1]), 0)
    head = jnp.where(rid < k, hh, r[0:HALO])
    if v.shape[0] == HALO:
        return head
    return jnp.concatenate([head, r[HALO:]], axis=0)


def _shift_up(v, halo, k, last):
    n = v.shape[0]
    r = pltpu.roll(v, n - k, 0)
    hh = jnp.where(last, 0.0, pltpu.roll(halo, HALO - k, 0))
    rid = lax.broadcasted_iota(jnp.int32, (HALO, v.shape[1]), 0)
    tail = jnp.where(rid >= HALO - k, hh, r[n - HALO:])
    return jnp.concatenate([r[:n - HALO], tail], axis=0)


def _conv(p, halo, w_ref, b_ref, first):
    return (w_ref[2:3, :] * p + w_ref[1:2, :] * _shift_down(p, halo, 1, first)
            + w_ref[0:1, :] * _shift_down(p, halo, 2, first) + b_ref[...])


def _conv_specs(t, tr, tc):
    tile = pl.BlockSpec((tr, tc), lambda j, i: (i, j))
    prev = pl.BlockSpec((HALO, tc), lambda j, i: (jnp.maximum(i * (tr // HALO) - 1, 0), j))
    nxt = pl.BlockSpec((HALO, tc), lambda j, i: (jnp.minimum((i + 1) * (tr // HALO), t // HALO - 1), j))
    wspec = pl.BlockSpec((3, tc), lambda j, i: (0, j))
    bspec = pl.BlockSpec((1, tc), lambda j, i: (0, j))
    return tile, prev, nxt, wspec, bspec


BF16_ROWS = 16


def ffn_in_fused(h, w_in4, wg, wu, bg, bu, *, name):
    t, k = h.shape
    s_all, _, n_s = w_in4.shape
    half = s_all // 2
    tm = _row_tile(t, 2 * k * n_s * 2, k * 2 + 2 * n_s * 4 + n_s * 2)

    def body(h_ref, hh_ref, wg_ref, wu_ref, cg_ref, cu_ref, bg_ref, bu_ref, pg_ref, pu_ref, a_ref):
        first = pl.program_id(1) == 0
        hv, hh = h_ref[...], hh_ref[...]
        outs = []
        for w_ref, c_ref, b_ref, p_ref in ((wg_ref, cg_ref, bg_ref, pg_ref), (wu_ref, cu_ref, bu_ref, pu_ref)):
            p = _dot(hv, w_ref[0])
            p_ref[...] = p
            outs.append(_conv(p, _dot(hh, w_ref[0])[BF16_ROWS - HALO:], c_ref, b_ref, first))
        gate, up = outs
        a_ref[...] = (gate * jax.nn.sigmoid(gate) * up).astype(bf16)

    tile = pl.BlockSpec((tm, n_s), lambda j, i: (i, j))
    cw = pl.BlockSpec((3, n_s), lambda j, i: (0, j))
    cb = pl.BlockSpec((1, n_s), lambda j, i: (0, j))
    f = half * n_s
    return pl.pallas_call(
        body, name=name, grid=(half, t // tm),
        in_specs=[pl.BlockSpec((tm, k), lambda j, i: (i, 0)),
                  pl.BlockSpec((BF16_ROWS, k), lambda j, i: (jnp.maximum(i * (tm // BF16_ROWS) - 1, 0), 0)),
                  pl.BlockSpec((1, k, n_s), lambda j, i: (j, 0, 0)),
                  pl.BlockSpec((1, k, n_s), lambda j, i: (j + half, 0, 0)), cw, cw, cb, cb],
        out_specs=[tile, tile, tile],
        out_shape=[jax.ShapeDtypeStruct((t, f), f32), jax.ShapeDtypeStruct((t, f), f32),
                   jax.ShapeDtypeStruct((t, f), bf16)],
        compiler_params=_params("parallel", "parallel"))(h, h, w_in4, w_in4, wg, wu, bg, bu)


def _gate_grads(gate, up, dav):
    sg = jax.nn.sigmoid(gate)
    return dav * up * (sg * (1.0 + gate * (1.0 - sg))), dav * gate * sg


def ffn_gate_bwd(pg, pu, da, wg, wu, bg, bu, *, name, tr=1024, tc=256):
    t, f = pg.shape
    tr = min(tr, t)
    tile, prev, nxt, wspec, bspec = _conv_specs(t, tr, tc)
    nsteps = t // tr

    def body(pg_ref, pgh_ref, pgn_ref, pu_ref, puh_ref, pun_ref, da_ref, dan_ref, wg_ref, wu_ref, bg_ref, bu_ref,
             dg_ref, du_ref, sg_ref, su_ref):
        i = pl.program_id(1)
        first = i == 0
        last = i == nsteps - 1
        pgv, puv = pg_ref[...], pu_ref[...]
        pg1, pg2 = _shift_down(pgv, pgh_ref[...], 1, first), _shift_down(pgv, pgh_ref[...], 2, first)
        pu1, pu2 = _shift_down(puv, puh_ref[...], 1, first), _shift_down(puv, puh_ref[...], 2, first)
        gate = wg_ref[2:3, :] * pgv + wg_ref[1:2, :] * pg1 + wg_ref[0:1, :] * pg2 + bg_ref[...]
        up = wu_ref[2:3, :] * puv + wu_ref[1:2, :] * pu1 + wu_ref[0:1, :] * pu2 + bu_ref[...]
        dgate, dup = _gate_grads(gate, up, da_ref[...])
        gate_n = _conv(pgn_ref[...], pgv[tr - HALO:], wg_ref, bg_ref, False)
        up_n = _conv(pun_ref[...], puv[tr - HALO:], wu_ref, bu_ref, False)
        dgate_n, dup_n = _gate_grads(gate_n, up_n, dan_ref[...])
        for d, d_n, w_ref, o_ref in ((dgate, dgate_n, wg_ref, dg_ref), (dup, dup_n, wu_ref, du_ref)):
            o_ref[...] = (w_ref[2:3, :] * d + w_ref[1:2, :] * _shift_up(d, d_n, 1, last)
                          + w_ref[0:1, :] * _shift_up(d, d_n, 2, last)).astype(bf16)
        rid = lax.broadcasted_iota(jnp.int32, (8, tc), 0)
        for d, p0, p1, p2, s_ref in ((dgate, pgv, pg1, pg2, sg_ref), (dup, puv, pu1, pu2, su_ref)):
            sums = [jnp.sum(d * p2, axis=0, keepdims=True), jnp.sum(d * p1, axis=0, keepdims=True),
                    jnp.sum(d * p0, axis=0, keepdims=True), jnp.sum(d, axis=0, keepdims=True)]
            part = jnp.zeros((8, tc), f32)
            for k, sk in enumerate(sums):
                part = jnp.where(rid == k, sk, part)

            @pl.when(first)
            def _(s_ref=s_ref, part=part):
                s_ref[...] = part

            @pl.when(i > 0)
            def _(s_ref=s_ref, part=part):
                s_ref[...] += part

    stat = pl.BlockSpec((8, tc), lambda j, i: (0, j))
    return pl.pallas_call(
        body, name=name, grid=(f // tc, nsteps),
        in_specs=[tile, prev, nxt, tile, prev, nxt, tile, nxt, wspec, wspec, bspec, bspec],
        out_specs=[tile, tile, stat, stat],
        out_shape=[jax.ShapeDtypeStruct((t, f), bf16), jax.ShapeDtypeStruct((t, f), bf16),
                   jax.ShapeDtypeStruct((8, f), f32), jax.ShapeDtypeStruct((8, f), f32)],
        compiler_params=_params("parallel", "arbitrary"))(pg, pg, pg, pu, pu, pu, da, da, wg, wu, bg, bu)


def _head_mean_matrix():
    i = lax.broadcasted_iota(jnp.int32, (LANES, LANES), 0) // HEAD_DIM
    j = lax.broadcasted_iota(jnp.int32, (LANES, LANES), 1) // HEAD_DIM
    return jnp.where(i == j, 1.0 / HEAD_DIM, 0.0).astype(f32)


def _lane_half(shape):
    return (lax.broadcasted_iota(jnp.int32, shape, 1) % LANES) // HEAD_DIM


def q_norm_fwd(qp, g2, *, name, scale, tr=512):
    t, w = qp.shape
    tr = min(tr, t)

    def body(x_ref, g_ref, o_ref):
        bd = _head_mean_matrix()
        for cb in range(w // LANES):
            cols = slice(cb * LANES, (cb + 1) * LANES)
            xc = x_ref[:, cols]
            rh = lax.rsqrt(_dot_exact(xc * xc, bd) + EPS)
            o_ref[:, cols] = (xc * rh * g_ref[...] * scale).astype(bf16)

    row = pl.BlockSpec((tr, w), lambda i: (i, 0))
    return pl.pallas_call(
        body, name=name, grid=(t // tr,), in_specs=[row, pl.BlockSpec((1, LANES), lambda i: (0, 0))],
        out_specs=row, out_shape=jax.ShapeDtypeStruct((t, w), bf16),
        compiler_params=_params("parallel"))(qp, g2)


def q_norm_bwd(dq, qp, g2, *, name, scale, tr=512):
    t, w = qp.shape
    tr = min(tr, t)

    def body(dq_ref, x_ref, g_ref, o_ref, dg_ref):
        i = pl.program_id(0)
        bd = _head_mean_matrix()
        acc = jnp.zeros((1, LANES), f32)
        for cb in range(w // LANES):
            cols = slice(cb * LANES, (cb + 1) * LANES)
            xc = x_ref[:, cols]
            rh = lax.rsqrt(_dot_exact(xc * xc, bd) + EPS)
            xh = xc * rh
            dy = dq_ref[:, cols] * scale
            acc = acc + jnp.sum(dy * xh, axis=0, keepdims=True)
            tg = dy * g_ref[...]
            o_ref[:, cols] = (rh * (tg - xh * _dot_exact(tg * xh, bd))).astype(bf16)

        @pl.when(i == 0)
        def _():
            dg_ref[...] = acc

        @pl.when(i > 0)
        def _():
            dg_ref[...] += acc

    row = pl.BlockSpec((tr, w), lambda i: (i, 0))
    vec = pl.BlockSpec((1, LANES), lambda i: (0, 0))
    return pl.pallas_call(
        body, name=name, grid=(t // tr,), in_specs=[row, row, vec], out_specs=[row, vec],
        out_shape=[jax.ShapeDtypeStruct((t, w), bf16), jax.ShapeDtypeStruct((1, LANES), f32)],
        compiler_params=_params("arbitrary"))(dq, qp, g2)


def kv_post_fwd(kv, g2, *, name, tr=512):
    t, w = kv.shape
    tr = min(tr, t)
    kw = w // 2

    def body(x_ref, g_ref, k_ref, v_ref):
        bd = _head_mean_matrix()
        half = _lane_half((tr, LANES))
        for cb in range(kw // LANES):
            xc = x_ref[:, cb * LANES:(cb + 1) * LANES]
            rh = lax.rsqrt(_dot_exact(xc * xc, bd) + EPS)
            kn = xc * rh * g_ref[...]
            vc = x_ref[:, kw + cb * LANES:kw + (cb + 1) * LANES]
            for src, dst in ((kn, k_ref), (vc, v_ref)):
                sw = pltpu.roll(src, HEAD_DIM, 1)
                for hf in range(2):
                    blk = 2 * cb + hf
                    dst[:, blk * LANES:(blk + 1) * LANES] = jnp.where(half == hf, src, sw).astype(bf16)

    return pl.pallas_call(
        body, name=name, grid=(t // tr,),
        in_specs=[pl.BlockSpec((tr, w), lambda i: (i, 0)), pl.BlockSpec((1, LANES), lambda i: (0, 0))],
        out_specs=[pl.BlockSpec((tr, 2 * kw), lambda i: (i, 0))] * 2,
        out_shape=[jax.ShapeDtypeStruct((t, 2 * kw), bf16)] * 2,
        compiler_params=_params("parallel"))(kv, g2)


def kv_post_bwd(dk2, dv2, kv, g2, *, name, tr=512):
    t, w = kv.shape
    tr = min(tr, t)
    kw = w // 2

    def body(dk_ref, dv_ref, x_ref, g_ref, o_ref, dg_ref):
        i = pl.program_id(0)
        bd = _head_mean_matrix()
        half = _lane_half((tr, LANES))
        acc = jnp.zeros((1, LANES), f32)

        def fold(ref, cb):
            a = ref[:, (2 * cb) * LANES:(2 * cb + 1) * LANES]
            b = ref[:, (2 * cb + 1) * LANES:(2 * cb + 2) * LANES]
            return jnp.where(half == 0, a + pltpu.roll(a, HEAD_DIM, 1), b + pltpu.roll(b, HEAD_DIM, 1))

        for cb in range(kw // LANES):
            cols = slice(cb * LANES, (cb + 1) * LANES)
            xc = x_ref[:, cols]
            rh = lax.rsqrt(_dot_exact(xc * xc, bd) + EPS)
            xh = xc * rh
            dy = fold(dk_ref, cb)
            acc = acc + jnp.sum(dy * xh, axis=0, keepdims=True)
            tg = dy * g_ref[...]
            o_ref[:, cols] = (rh * (tg - xh * _dot_exact(tg * xh, bd))).astype(bf16)
            o_ref[:, kw + cb * LANES:kw + (cb + 1) * LANES] = fold(dv_ref, cb).astype(bf16)

        @pl.when(i == 0)
        def _():
            dg_ref[...] = acc

        @pl.when(i > 0)
        def _():
            dg_ref[...] += acc

    dup = pl.BlockSpec((tr, 2 * kw), lambda i: (i, 0))
    row = pl.BlockSpec((tr, w), lambda i: (i, 0))
    vec = pl.BlockSpec((1, LANES), lambda i: (0, 0))
    return pl.pallas_call(
        body, name=name, grid=(t // tr,), in_specs=[dup, dup, row, vec], out_specs=[row, vec],
        out_shape=[jax.ShapeDtypeStruct((t, w), bf16), jax.ShapeDtypeStruct((1, LANES), f32)],
        compiler_params=_params("arbitrary"))(dk2, dv2, kv, g2)


def _slope(h):
    return 2.0 ** (-8.0 * (h + 1) / N_Q_HEADS)


GROUP_ROWS = Q_PER_KV * CHUNK


def _band_mask(n):
    tq = lax.broadcasted_iota(jnp.int32, (GROUP_ROWS, 2 * CHUNK), 0) % CHUNK
    jk = lax.broadcasted_iota(jnp.int32, (GROUP_ROWS, 2 * CHUNK), 1)
    dist = tq + CHUNK - jk
    ok = (dist >= 0) & (dist < CHUNK) & jnp.logical_not((n == 0) & (jk < CHUNK))
    return dist.astype(f32), ok


def _band(ref, n, kh):
    p0 = pl.multiple_of(jnp.maximum(n - 1, 0) * CHUNK, CHUNK)
    c0 = pl.multiple_of(n * CHUNK, CHUNK)
    cols = slice(kh * LANES, (kh + 1) * LANES)
    return jnp.concatenate([ref[pl.ds(p0, CHUNK), cols], ref[pl.ds(c0, CHUNK), cols]], axis=0)


def _stack_heads(ref, kh, half):
    parts = []
    for cb in (2 * kh, 2 * kh + 1):
        xc = ref[:, cb * LANES:(cb + 1) * LANES].astype(f32)
        parts += [jnp.where(half == hf, xc, 0.0).astype(bf16) for hf in range(2)]
    return jnp.concatenate(parts, axis=0)


def _unstack_heads(x4, half):
    return (jnp.where(half == 0, x4[0:CHUNK], x4[CHUNK:2 * CHUNK]),
            jnp.where(half == 0, x4[2 * CHUNK:3 * CHUNK], x4[3 * CHUNK:]))


def _per_head_column(kh, values):
    grp = lax.broadcasted_iota(jnp.int32, (GROUP_ROWS, 1), 0) // CHUNK
    col = jnp.full((GROUP_ROWS, 1), values[0], f32)
    for g in range(1, Q_PER_KV):
        col = jnp.where(grp == g, values[g], col)
    return col


def _softmax_band(q4, kband, dist, ok, slope, sink):
    s = _dot_nt(q4, kband)
    s = jnp.where(ok, s - slope * dist, -jnp.inf)
    m = jnp.maximum(jnp.max(s, axis=1, keepdims=True), sink)
    e = jnp.exp(s - m)
    es = jnp.exp(sink - m)
    den = jnp.sum(e, axis=1, keepdims=True) + es
    return e / den, es / den


def attn_fwd(q, k2, v2, sinks, *, name):
    t, w = q.shape
    nb = t // CHUNK

    def body(sink_ref, q_ref, k_ref, v_ref, o_ref):
        n = pl.program_id(0)
        dist, ok = _band_mask(n)
        half = _lane_half((CHUNK, LANES))
        for kh in range(N_KV_HEADS):
            heads = [Q_PER_KV * kh + g for g in range(Q_PER_KV)]
            slope = _per_head_column(kh, [_slope(h) for h in heads])
            sink = _per_head_column(kh, [sink_ref[h] for h in heads])
            q4 = _stack_heads(q_ref, kh, half)
            p, _ = _softmax_band(q4, _band(k_ref, n, kh), dist, ok, slope, sink)
            o4 = _dot(p.astype(bf16), _band(v_ref, n, kh))
            lo, hi = _unstack_heads(o4, half)
            o_ref[:, (2 * kh) * LANES:(2 * kh + 1) * LANES] = lo.astype(bf16)
            o_ref[:, (2 * kh + 1) * LANES:(2 * kh + 2) * LANES] = hi.astype(bf16)

    full = pl.BlockSpec((t, k2.shape[1]), lambda n: (0, 0))
    return pl.pallas_call(
        body, name=name, grid=(nb,),
        in_specs=[pl.BlockSpec(memory_space=pltpu.SMEM), pl.BlockSpec((CHUNK, w), lambda n: (n, 0)), full, full],
        out_specs=pl.BlockSpec((CHUNK, w), lambda n: (n, 0)),
        out_shape=jax.ShapeDtypeStruct((t, w), bf16),
        compiler_params=_params("parallel"))(sinks, q, k2, v2)


def attn_bwd(q, k2, v2, do, sinks, *, name):
    t, w = q.shape
    nb = t // CHUNK
    kw = k2.shape[1]

    def body(sink_ref, q_ref, k_ref, v_ref, do_ref, dq_ref, dk_ref, dv_ref, ds_ref, kc_ref, vc_ref):
        n = pl.program_id(0)

        @pl.when(n == 0)
        def _():
            ds_ref[...] = jnp.zeros_like(ds_ref)
            kc_ref[...] = jnp.zeros_like(kc_ref)
            vc_ref[...] = jnp.zeros_like(vc_ref)
            dk_ref[...] = jnp.zeros_like(dk_ref)
            dv_ref[...] = jnp.zeros_like(dv_ref)

        @pl.when(n == nb)
        def _():
            dk_ref[...] = kc_ref[...]
            dv_ref[...] = vc_ref[...]

        @pl.when(n < nb)
        def _():
            dist, ok = _band_mask(n)
            half = _lane_half((CHUNK, LANES))
            lane = lax.broadcasted_iota(jnp.int32, (1, LANES), 1)
            sink_acc = jnp.zeros((1, LANES), f32)
            for kh in range(N_KV_HEADS):
                heads = [Q_PER_KV * kh + g for g in range(Q_PER_KV)]
                slope = _per_head_column(kh, [_slope(h) for h in heads])
                sink = _per_head_column(kh, [sink_ref[h] for h in heads])
                q4 = _stack_heads(q_ref, kh, half)
                do4 = _stack_heads(do_ref, kh, half)
                kband = _band(k_ref, n, kh)
                vband = _band(v_ref, n, kh)
                p, ps = _softmax_band(q4, kband, dist, ok, slope, sink)
                dp = _dot_nt(do4, vband)
                delta = jnp.sum(p * dp, axis=1, keepdims=True)
                dsb = (p * (dp - delta)).astype(bf16)
                sd = ps * delta
                for g, h in enumerate(heads):
                    part = jnp.sum(sd[g * CHUNK:(g + 1) * CHUNK], axis=0, keepdims=True)
                    sink_acc = sink_acc + jnp.where(lane == h, -part, 0.0)
                lo, hi = _unstack_heads(_dot(dsb, kband), half)
                dq_ref[:, (2 * kh) * LANES:(2 * kh + 1) * LANES] = lo
                dq_ref[:, (2 * kh + 1) * LANES:(2 * kh + 2) * LANES] = hi
                dkb = _dot_tn(dsb, q4)
                dvb = _dot_tn(p.astype(bf16), do4)
                cols = slice(kh * LANES, (kh + 1) * LANES)
                dk_ref[:, cols] = kc_ref[:, cols] + dkb[0:CHUNK]
                dv_ref[:, cols] = vc_ref[:, cols] + dvb[0:CHUNK]
                kc_ref[:, cols] = dkb[CHUNK:]
                vc_ref[:, cols] = dvb[CHUNK:]
            ds_ref[...] += sink_acc

    full = pl.BlockSpec((t, kw), lambda n: (0, 0))
    qblk = pl.BlockSpec((CHUNK, w), lambda n: (jnp.minimum(n, nb - 1), 0))
    kblk = pl.BlockSpec((CHUNK, kw), lambda n: (jnp.maximum(n - 1, 0), 0))
    return pl.pallas_call(
        body, name=name, grid=(nb + 1,),
        in_specs=[pl.BlockSpec(memory_space=pltpu.SMEM), qblk, full, full, qblk],
        out_specs=[qblk, kblk, kblk, pl.BlockSpec((1, LANES), lambda n: (0, 0))],
        out_shape=[jax.ShapeDtypeStruct((t, w), f32), jax.ShapeDtypeStruct((t, kw), f32),
                   jax.ShapeDtypeStruct((t, kw), f32), jax.ShapeDtypeStruct((1, LANES), f32)],
        scratch_shapes=[pltpu.VMEM((CHUNK, kw), f32), pltpu.VMEM((CHUNK, kw), f32)],
        compiler_params=_params("arbitrary"))(sinks, q, k2, v2, do)


def loss_head(y, target, *, name, tr=512):
    t, d = y.shape
    tr = min(tr, t)

    def body(y_ref, t_ref, dy_ref, s_ref):
        i = pl.program_id(0)
        e = y_ref[...] - t_ref[...]
        dy_ref[...] = e * (1.0 / d)
        part = jnp.sum(e * e, axis=0, keepdims=True)

        @pl.when(i == 0)
        def _():
            s_ref[...] = part

        @pl.when(i > 0)
        def _():
            s_ref[...] += part

    row = pl.BlockSpec((tr, d), lambda i: (i, 0))
    vec = pl.BlockSpec((1, d), lambda i: (0, 0))
    return pl.pallas_call(
        body, name=name, grid=(t // tr,), in_specs=[row, row], out_specs=[row, vec],
        out_shape=[jax.ShapeDtypeStruct((t, d), f32), jax.ShapeDtypeStruct((1, d), f32)],
        compiler_params=_params("arbitrary"))(y, target)


N_STEPS = 8


def _row_blocks(shape):
    if len(shape) == 2:
        r, c = shape
        return (r // N_STEPS, c), (lambda s: (s, 0))
    l, r, c = shape
    per = N_STEPS // l
    return (1, r // per, c), (lambda s: (s // per, s % per, 0))


CAST_STEPS = 4


def cast_into_slot(arrays, k_arr, *, name):
    in_specs, out_specs, out_shape, layers = [], [], [], []
    for a in arrays:
        r, c = a.shape[-2:]
        rb = r // CAST_STEPS
        if a.ndim == 2:
            in_specs.append(pl.BlockSpec((rb, c), lambda s, k: (s, 0)))
            layers.append(None)
        else:
            for l in range(a.shape[0]):
                in_specs.append(pl.BlockSpec((1, rb, c), lambda s, k, l=l: (l, s, 0)))
                layers.append(l)
        for _ in range(1 if a.ndim == 2 else a.shape[0]):
            out_specs.append(pl.BlockSpec((1, rb, c), lambda s, k: (k[0], s, 0)))
            out_shape.append(jax.ShapeDtypeStruct((N_SHARDS, r, c), bf16))
    n = len(in_specs)

    def body(k_ref, *refs):
        for i_ref, o_ref, l in zip(refs[:n], refs[n:], layers):
            o_ref[0] = (i_ref[...] if l is None else i_ref[0]).astype(bf16)

    args = []
    for a in arrays:
        args += [a] * (1 if a.ndim == 2 else a.shape[0])
    return pl.pallas_call(
        body, name=name,
        grid_spec=pltpu.PrefetchScalarGridSpec(num_scalar_prefetch=1, grid=(CAST_STEPS,),
                                               in_specs=in_specs, out_specs=out_specs),
        out_shape=out_shape, compiler_params=_params("parallel"))(k_arr, *args)


def adamw(ws, gs, ms, vs, *, name):
    n = len(ws)
    specs, g_specs, g_count = [], [], []
    for w, g_list in zip(ws, gs):
        blk, index = _row_blocks(w.shape)
        specs.append(pl.BlockSpec(blk, index))
        layers = len(g_list)
        per = N_STEPS // layers
        g_count.append(layers)
        for l in range(layers):
            g_specs.append(pl.BlockSpec(blk[-2:], lambda s, l=l, per=per: (jnp.where(s // per == l, s % per, 0), 0)))
    ng = len(g_specs)

    def body(*refs):
        s = pl.program_id(0)
        g_refs = refs[3 * n:3 * n + ng]
        outs = refs[3 * n + ng:]
        off = 0
        for i in range(n):
            w_ref, m_ref, v_ref = refs[i], refs[n + i], refs[2 * n + i]
            go_ref, d_ref, nm_ref, nv_ref = (outs[k * n + i] for k in range(4))
            layers = g_count[i]
            g = g_refs[off][...]
            for l in range(1, layers):
                g = jnp.where(s // (N_STEPS // layers) == l, g_refs[off + l][...], g)
            off += layers
            g = g.reshape(w_ref.shape)
            m = ADAM_B1 * m_ref[...] + (1.0 - ADAM_B1) * g
            v = ADAM_B2 * v_ref[...] + (1.0 - ADAM_B2) * (g * g)
            m_hat = m / ADAM_C1
            v_hat = v / ADAM_C2
            go_ref[...] = g
            d_ref[...] = -ADAM_LR * (m_hat / (jnp.sqrt(v_hat) + ADAM_EPS) + ADAM_WD * w_ref[...])
            nm_ref[...] = m
            nv_ref[...] = v

    outs = pl.pallas_call(
        body, name=name, grid=(N_STEPS,), in_specs=specs * 3 + g_specs, out_specs=specs * 4,
        out_shape=[jax.ShapeDtypeStruct(a.shape, f32) for a in ws] * 4,
        compiler_params=_params("parallel"))(*ws, *ms, *vs, *[g for g_list in gs for g in g_list])
    return [outs[k * n:(k + 1) * n] for k in range(4)]


def _place():
    return lax.axis_index("x"), lax.axis_index("y"), lax.axis_index("c")


def gather_shards(bufs, *, name, split):
    n = len(bufs)

    def body(*refs):
        bufs_ = refs[:n]
        isend, irecv, dsend, drecv = refs[2 * n:]
        x, y, c = _place()
        k = 2 * x + y
        peers = [(1 - x, y, c), (x, 1 - y, c), (1 - x, 1 - y, c)]
        peer_k = [2 * (1 - x) + y, 2 * x + (1 - y), 2 * (1 - x) + (1 - y)]

        def slab(a, q, h):
            if not split[a]:
                return bufs_[a].at[q]
            half = bufs_[a].shape[1] // 2
            return bufs_[a].at[q, pl.ds(pl.multiple_of(h * half, 16), half)]

        def ici(a, j, q):
            return pltpu.make_async_remote_copy(
                src_ref=slab(a, q, c), dst_ref=slab(a, q, c), send_sem=isend.at[3 * a + j], recv_sem=irecv.at[3 * a + j],
                device_id=peers[j], device_id_type=MESH)

        def d2d(a, j, h):
            return pltpu.make_async_remote_copy(
                src_ref=slab(a, peer_k[j], h), dst_ref=slab(a, peer_k[j], h), send_sem=dsend.at[3 * a + j],
                recv_sem=drecv.at[3 * a + j], device_id=(x, y, 1 - c), device_id_type=MESH)

        for a in range(n):
            for j in range(3):
                ici(a, j, k).start()
        for a in range(n):
            for j in range(3):
                ici(a, j, peer_k[j]).wait_recv()
                if split[a]:
                    d2d(a, j, c).start()
        for a in range(n):
            for j in range(3):
                if split[a]:
                    d2d(a, j, 1 - c).wait_recv()
        for a in range(n):
            for j in range(3):
                ici(a, j, k).wait_send()
                if split[a]:
                    d2d(a, j, c).wait_send()

    return pl.pallas_call(
        body, name=name, in_specs=[ANY] * n, out_specs=[ANY] * n,
        out_shape=[jax.ShapeDtypeStruct(b.shape, b.dtype) for b in bufs],
        input_output_aliases={i: i for i in range(n)},
        scratch_shapes=[pltpu.SemaphoreType.DMA((3 * n,))] * 4)(*bufs)


HBM = pl.BlockSpec(memory_space=pltpu.HBM)
SEM = pl.BlockSpec(memory_space=pltpu.SEMAPHORE)
DATAFLOW = pltpu.SideEffectType.DATAFLOW_SIDE_EFFECTING


def _chip_peers():
    x, y, c = _place()
    return 2 * x + y, [(1 - x, y, c), (x, 1 - y, c), (1 - x, 1 - y, c)], [2 * (1 - x) + y, 2 * x + (1 - y), 2 * (1 - x) + (1 - y)]


def gather_start(bufs, groups, after, *, name):
    n = len(bufs)
    ng = len(groups)

    def body(*refs):
        ins = refs[:n]
        sends, recvs = refs[2 * n + 1:2 * n + 1 + ng], refs[2 * n + 1 + ng:2 * n + 1 + 2 * ng]
        token = refs[-1]
        k, peers, _ = _chip_peers()
        for gi, grp in enumerate(groups):
            for pos, a in enumerate(grp):
                for j in range(3):
                    pltpu.make_async_remote_copy(
                        src_ref=ins[a].at[k], dst_ref=ins[a].at[k], send_sem=sends[gi].at[3 * pos + j],
                        recv_sem=recvs[gi].at[3 * pos + j], device_id=peers[j], device_id_type=MESH).start()
        token[...] = jnp.zeros_like(token)

    sems = [pltpu.SemaphoreType.DMA((3 * len(grp),)) for grp in groups]
    outs = pl.pallas_call(
        body, name=name, in_specs=[HBM] * n + [ANY],
        out_specs=[HBM] * n + [SEM] * (2 * ng) + [pl.BlockSpec(memory_space=pltpu.VMEM)],
        out_shape=[pltpu.HBM(b.shape, b.dtype) for b in bufs] + sems + sems + [jax.ShapeDtypeStruct((8, LANES), f32)],
        input_output_aliases={i: i for i in range(n)},
        compiler_params=pltpu.CompilerParams(has_side_effects=DATAFLOW))(
            *[pltpu.with_memory_space_constraint(b, pltpu.HBM) for b in bufs], after)
    return outs[:n], outs[n:n + ng], outs[n + ng:n + 2 * ng], outs[-1]


def gather_wait(bufs, send_sems, recv_sems, after, *, name):
    n = len(bufs)

    def body(*refs):
        ins = refs[:n]
        send, recv = refs[n], refs[n + 1]
        k, peers, peer_k = _chip_peers()
        for a in range(n):
            for j in range(3):
                copy = pltpu.make_async_remote_copy(
                    src_ref=ins[a].at[k], dst_ref=ins[a].at[peer_k[j]], send_sem=send.at[3 * a + j],
                    recv_sem=recv.at[3 * a + j], device_id=peers[j], device_id_type=MESH)
                copy.wait_send()
                copy.wait_recv()

    return pl.pallas_call(
        body, name=name, in_specs=[HBM] * n + [SEM, SEM, ANY], out_specs=[HBM] * n,
        out_shape=[pltpu.HBM(b.shape, b.dtype) for b in bufs],
        input_output_aliases={i: i for i in range(n)},
        compiler_params=pltpu.CompilerParams(has_side_effects=DATAFLOW))(*bufs, send_sems, recv_sems, after)


def sibling_exchange(arrays, *, name):
    n = len(arrays)

    def body(*refs):
        ins, outs = refs[:n], refs[n:2 * n]
        send, recv = refs[2 * n:]
        x, y, c = _place()

        def copy(a):
            return pltpu.make_async_remote_copy(
                src_ref=ins[a], dst_ref=outs[a], send_sem=send.at[a], recv_sem=recv.at[a],
                device_id=(x, y, 1 - c), device_id_type=MESH)

        for a in range(n):
            copy(a).start()
        for a in range(n):
            copy(a).wait_recv()
        for a in range(n):
            copy(a).wait_send()

    return pl.pallas_call(
        body, name=name, in_specs=[ANY] * n, out_specs=[ANY] * n,
        out_shape=[jax.ShapeDtypeStruct(a.shape, a.dtype) for a in arrays],
        scratch_shapes=[pltpu.SemaphoreType.DMA((n,)), pltpu.SemaphoreType.DMA((n,))])(*arrays)


def chip_scatter(arrays, everyone, *, name):
    n = len(arrays)

    def body(*refs):
        ins, ev_ref, outs, evo_ref = refs[:n], refs[n], refs[n + 1:2 * n + 1], refs[2 * n + 1]
        send, recv, esend, erecv, loc = refs[2 * n + 2:]
        x, y, c = _place()
        me = 4 * x + 2 * y + c
        peers = [(1 - x, y, c), (x, 1 - y, c), (1 - x, 1 - y, c)]
        peer_k = [2 * (1 - x) + y, 2 * x + (1 - y), 2 * (1 - x) + (1 - y)]
        masks = [(mx, my, mc) for mx in (0, 1) for my in (0, 1) for mc in (0, 1)][1:]

        def copy(a, j):
            return pltpu.make_async_remote_copy(
                src_ref=ins[a].at[peer_k[j]], dst_ref=outs[a].at[j], send_sem=send.at[3 * a + j],
                recv_sem=recv.at[3 * a + j], device_id=peers[j], device_id_type=MESH)

        def to_all(j, slot):
            return pltpu.make_async_remote_copy(
                src_ref=ev_ref, dst_ref=evo_ref.at[slot], send_sem=esend.at[j], recv_sem=erecv.at[j],
                device_id=(x ^ masks[j][0], y ^ masks[j][1], c ^ masks[j][2]), device_id_type=MESH)

        local = pltpu.make_async_copy(ev_ref, evo_ref.at[me], loc)
        local.start()
        for j in range(7):
            to_all(j, me).start()
        for a in range(n):
            for j in range(3):
                copy(a, j).start()
        for j in range(7):
            to_all(j, me ^ (4 * masks[j][0] + 2 * masks[j][1] + masks[j][2])).wait_recv()
        for a in range(n):
            for j in range(3):
                copy(a, j).wait_recv()
        for j in range(7):
            to_all(j, me).wait_send()
        for a in range(n):
            for j in range(3):
                copy(a, j).wait_send()
        local.wait()

    outs = pl.pallas_call(
        body, name=name, in_specs=[ANY] * (n + 1), out_specs=[ANY] * (n + 1),
        out_shape=[jax.ShapeDtypeStruct((3,) + a.shape[1:], a.dtype) for a in arrays]
        + [jax.ShapeDtypeStruct((N_DEV,) + everyone.shape, everyone.dtype)],
        scratch_shapes=[pltpu.SemaphoreType.DMA((3 * n,)), pltpu.SemaphoreType.DMA((3 * n,)),
                        pltpu.SemaphoreType.DMA((7,)), pltpu.SemaphoreType.DMA((7,)), pltpu.SemaphoreType.DMA])(
                            *arrays, everyone)
    return outs[:n], outs[n]


def sibling_merge(bufs, *, name):
    n = len(bufs)

    def body(*refs):
        bufs_ = refs[:n]
        send, recv = refs[2 * n:]
        x, y, c = _place()

        def copy(u, h):
            return pltpu.make_async_remote_copy(
                src_ref=bufs_[u].at[h], dst_ref=bufs_[u].at[h], send_sem=send.at[u], recv_sem=recv.at[u],
                device_id=(x, y, 1 - c), device_id_type=MESH)

        for u in range(n):
            copy(u, c).start()
        for u in range(n):
            copy(u, 1 - c).wait_recv()
        for u in range(n):
            copy(u, c).wait_send()

    return pl.pallas_call(
        body, name=name, in_specs=[ANY] * n, out_specs=[ANY] * n,
        out_shape=[jax.ShapeDtypeStruct(b.shape, b.dtype) for b in bufs],
        input_output_aliases={i: i for i in range(n)},
        scratch_shapes=[pltpu.SemaphoreType.DMA((n,)), pltpu.SemaphoreType.DMA((n,))])(*bufs)


def sum_leading(a, *, name):
    n, r, c = a.shape

    def body(a_ref, o_ref):
        acc = a_ref[0]
        for i in range(1, n):
            acc = acc + a_ref[i]
        o_ref[...] = acc

    rb = r // 2 if r % 16 == 0 else r
    return pl.pallas_call(
        body, name=name, grid=(r // rb,), in_specs=[pl.BlockSpec((n, rb, c), lambda i: (0, i, 0))],
        out_specs=pl.BlockSpec((rb, c), lambda i: (i, 0)), out_shape=jax.ShapeDtypeStruct((r, c), f32),
        compiler_params=_params("parallel"))(a)


def _half_rows(shape):
    return shape[1] // 2 // 2


def rs_cast_other_half(grads, c_arr, *, name):
    n = len(grads)

    def body(c_ref, *refs):
        for i_ref, o_ref in zip(refs[:n], refs[n:]):
            o_ref[...] = i_ref[...].astype(bf16)

    in_specs = [pl.BlockSpec((1, _half_rows(g.shape), g.shape[2]), lambda s, r, c_ref: (s, (1 - c_ref[0]) * 2 + r, 0))
                for g in grads]
    out_specs = [pl.BlockSpec((1, _half_rows(g.shape), g.shape[2]), lambda s, r, c_ref: (s, r, 0)) for g in grads]
    return pl.pallas_call(
        body, name=name,
        grid_spec=pltpu.PrefetchScalarGridSpec(num_scalar_prefetch=1, grid=(N_SHARDS, 2),
                                               in_specs=in_specs, out_specs=out_specs),
        out_shape=[jax.ShapeDtypeStruct((N_SHARDS, g.shape[1] // 2, g.shape[2]), bf16) for g in grads],
        compiler_params=_params("parallel", "parallel"))(c_arr, *grads)


def rs_add_sibling(grads, recvd, ck_arr, *, name):
    n = len(grads)

    def body(ck_ref, *refs):
        s = pl.program_id(1)
        for u in range(n):
            g_ref, r_ref = refs[u], refs[n + u]
            qb_ref, own_ref = refs[2 * n + u], refs[3 * n + u]
            q = g_ref[0] + r_ref[0].astype(f32)
            qb_ref[0] = q.astype(bf16)

            @pl.when(s == ck_ref[1])
            def _(own_ref=own_ref, q=q):
                own_ref[...] = q

    in_specs = [pl.BlockSpec((1, _half_rows(g.shape), g.shape[2]), lambda r, s, ck: (s, ck[0] * 2 + r, 0)) for g in grads]
    in_specs += [pl.BlockSpec((1, _half_rows(g.shape), g.shape[2]), lambda r, s, ck: (s, r, 0)) for g in grads]
    out_specs = [pl.BlockSpec((1, _half_rows(g.shape), g.shape[2]), lambda r, s, ck: (s, r, 0)) for g in grads]
    out_specs += [pl.BlockSpec((_half_rows(g.shape), g.shape[2]), lambda r, s, ck: (r, 0)) for g in grads]
    outs = pl.pallas_call(
        body, name=name,
        grid_spec=pltpu.PrefetchScalarGridSpec(num_scalar_prefetch=1, grid=(2, N_SHARDS),
                                               in_specs=in_specs, out_specs=out_specs),
        out_shape=[jax.ShapeDtypeStruct((N_SHARDS, g.shape[1] // 2, g.shape[2]), bf16) for g in grads]
        + [jax.ShapeDtypeStruct((g.shape[1] // 2, g.shape[2]), f32) for g in grads],
        compiler_params=_params("parallel", "arbitrary"))(ck_arr, *grads, *recvd)
    return outs[:n], outs[n:]


def rs_sum_chips(owns, recvd, ck_arr, *, name):
    n = len(owns)

    def body(ck_ref, *refs):
        for u in range(n):
            own_ref, r_ref, o_ref = refs[u], refs[n + u], refs[2 * n + u]
            o_ref[0] = ((own_ref[...] + r_ref[0].astype(f32)) + r_ref[1].astype(f32)) + r_ref[2].astype(f32)

    in_specs = [pl.BlockSpec((o.shape[0] // 2, o.shape[1]), lambda r, ck: (r, 0)) for o in owns]
    in_specs += [pl.BlockSpec((3, o.shape[0] // 2, o.shape[1]), lambda r, ck: (0, r, 0)) for o in owns]
    out_specs = [pl.BlockSpec((1, o.shape[0] // 2, o.shape[1]), lambda r, ck: (ck[0], r, 0)) for o in owns]
    return pl.pallas_call(
        body, name=name,
        grid_spec=pltpu.PrefetchScalarGridSpec(num_scalar_prefetch=1, grid=(2,), in_specs=in_specs, out_specs=out_specs),
        out_shape=[jax.ShapeDtypeStruct((2,) + o.shape, f32) for o in owns],
        compiler_params=_params("parallel"))(ck_arr, *owns, *recvd)


SMALL = ("a_norm", "a_v_norm", "a_w_s", "a_b_s", "f_norm", "f_conv_w", "f_conv_b", "kv_norm", "k_norm",
         "b_norm", "b_q_norm", "b_sinks")
BIG = ("a_w_in", "a_w_out", "f_w_in", "f_w_out", "w_kv", "b_w_q", "b_w_o")
PACK_COLS = 1024
PACK_ROWS = 8 * N_STEPS


def _pack(parts, rows=PACK_ROWS):
    flat = jnp.concatenate([p.reshape(-1).astype(f32) for p in parts])
    pad = (-flat.shape[0]) % (rows * PACK_COLS)
    return jnp.pad(flat, (0, pad)).reshape(-1, PACK_COLS)


def _unpack(packed, shapes):
    flat = packed.reshape(-1)
    out, off = [], 0
    for s in shapes:
        size = math.prod(s)
        out.append(flat[off:off + size].reshape(s))
        off += size
    return out


def _ffn_fwd(x, g, h, r, w_in4, conv_w, conv_b, f, tag):
    wg, wu = conv_w[:, :f], conv_w[:, f:]
    bg, bu = conv_b[None, :f], conv_b[None, f:]
    pg, pu, a = ffn_in_fused(h, w_in4, wg, wu, bg, bu, name=f"ffn{tag}_in")
    return a, (x, g, h, r, pg, pu, a, wg, wu, bg, bu)


def _ffn_bwd(dy, saved, w_in4, w_out, tag):
    x, g, h, r, pg, pu, a, wg, wu, bg, bu = saved
    f = w_out.shape[0]
    da = mm_nt([dy], w_out[None], name=f"ffn{tag}_dact", tko=f // 2)
    d_w_out = mm_tn(a, [dy], name=f"ffn{tag}_dwout", n_s=w_out.shape[1], tki=f // 2)
    dpg, dpu, sg, su = ffn_gate_bwd(pg, pu, da, wg, wu, bg, bu, name=f"ffn{tag}_dgate")
    d_w_in = mm_tn(h, [dpg, dpu], name=f"ffn{tag}_dwin", n_s=w_in4.shape[2])
    dh = mm_nt([dpg, dpu], w_in4, name=f"ffn{tag}_dh")
    dx, (dg,) = rms_bwd([dh], x, r, [g], dy, name=f"ffn{tag}_dnorm")
    d_conv_w = jnp.concatenate([sg[0:3], su[0:3]], axis=1)
    d_conv_b = jnp.concatenate([sg[3], su[3]], axis=0)
    return dx, dg, d_w_in, d_conv_w, d_conv_b, d_w_out


def kernel(x, a_norm, a_w_in, a_v_norm, a_w_s, a_b_s, a_w_out, f_norm, f_w_in, f_conv_w, f_conv_b, f_w_out, kv_norm, w_kv, k_norm, b_norm, b_w_q, b_q_norm, b_sinks, b_w_o, loss_target, m_a_norm, m_a_w_in, m_a_v_norm, m_a_w_s, m_a_b_s, m_a_w_out, m_f_norm, m_f_w_in, m_f_conv_w, m_f_conv_b, m_f_w_out, m_kv_norm, m_w_kv, m_k_norm, m_b_norm, m_b_w_q, m_b_q_norm, m_b_sinks, m_b_w_o, v_a_norm, v_a_w_in, v_a_v_norm, v_a_w_s, v_a_b_s, v_a_w_out, v_f_norm, v_f_w_in, v_f_conv_w, v_f_conv_b, v_f_w_out, v_kv_norm, v_w_kv, v_k_norm, v_b_norm, v_b_w_q, v_b_q_norm, v_b_sinks, v_b_w_o):
    args = dict(locals())
    weights = {n: args[n] for n in SMALL + BIG}
    moms = {n: args["m_" + n] for n in SMALL + BIG}
    vars_ = {n: args["v_" + n] for n in SMALL + BIG}
    t, d = x.shape[1], x.shape[2]
    xi, yi, ci = _place()
    chip = 2 * xi + yi

    big_local = [a_w_in[0], a_w_out[0], f_w_in, f_w_out, w_kv, b_w_q[0], b_w_o[0]]
    c_arr = jnp.stack([ci, chip]).astype(jnp.int32)
    k_arr = jnp.stack([chip]).astype(jnp.int32)
    b_ain, b_aout, b_fin0, b_fin1, b_fout0, b_fout1, b_kv, b_q, b_o = cast_into_slot(big_local, k_arr, name="cast_weights")
    small_cols = _pack([a_norm, a_v_norm, f_conv_w], rows=8)
    b_small = lax.dynamic_update_slice(jnp.zeros((N_SHARDS,) + small_cols.shape, f32), small_cols[None], (chip, 0, 0))
    g_small, g_a_w_in, g_a_w_out = gather_shards([b_small, b_ain, b_aout], name="gather_first", split=[False, True, True])
    later, send_sems, recv_sems, token = gather_start([b_fin0, b_fout0, b_kv, b_q, b_o, b_fin1, b_fout1],
                                                      [[0], [1, 2, 3, 4], [5, 6]], g_small, name="gather_start")
    ns_cols = a_norm.shape[1]
    nf_cols = f_conv_w.shape[2]
    parts = [_unpack(g_small[k], [a_norm.shape, a_v_norm.shape, f_conv_w.shape]) for k in range(N_SHARDS)]
    a_norm_f = jnp.concatenate([p[0] for p in parts], axis=1) + token[0, 0]
    a_v_norm_f = jnp.concatenate([p[1] for p in parts], axis=1)
    conv_w_f = jnp.concatenate([p[2] for p in parts], axis=2)
    w_a_in = g_a_w_in
    w_a_out = g_a_w_out.reshape(1, -1, d)

    x0 = x[0]
    tril = jnp.tril(jnp.ones((CHUNK, CHUNK), dtype=bool))
    wc = jnp.where(tril[None], a_w_s[0], 0.0).astype(bf16)
    bt = a_b_s[0].T
    kg2 = jnp.tile(k_norm, 2)[None]
    qg2 = jnp.tile(b_q_norm[0], 2)[None]

    (h_a,), r_a = rms_fwd(x0, [a_norm_f], name="a_norm")
    zu = mm_nn(h_a, w_a_in, name="a_in_u", s0=0, ns=2)
    zv = mm_nn(h_a, w_a_in, name="a_in_v", s0=2, ns=2)
    y_a = sgu_gate_fwd(zu, zv, a_v_norm_f, wc, bt, name="a_gate")
    f = f_w_out.shape[1] * N_SHARDS
    x1, (h_f0,), r_f0 = mm_residual(y_a, w_a_out[0], x0, name="a_out", gains=[f_norm[0:1]])
    (g_fin0,) = gather_wait(later[0:1], send_sems[0], recv_sems[0], x1, name="gather_wait_0")
    w_f_in = [g_fin0, None]
    a0, ffn0 = _ffn_fwd(x1, f_norm[0:1], h_f0, r_f0, w_f_in[0], conv_w_f[0], f_conv_b[0], f, "0")
    g_fout0, g_w_kv, g_b_w_q, g_b_w_o = gather_wait(later[1:5], send_sems[1], recv_sems[1], a0, name="gather_wait_1")
    w_f_out = [g_fout0.reshape(-1, d), None]
    w_kv_f = g_w_kv.reshape(1, d, -1)
    w_q_f = g_b_w_q.reshape(1, d, -1)
    w_o_f = g_b_w_o.reshape(1, -1, d)
    x2, (h_k, h_q), r_b = mm_residual(a0, w_f_out[0], x1, name="ffn0_out", gains=[kv_norm[None], b_norm])
    kv = mm_nn(h_k, w_kv_f, name="kv_proj")
    k2, v2 = kv_post_fwd(kv, kg2, name="kv_post")
    qp = mm_nn(h_q, w_q_f, name="q_proj")
    qn = q_norm_fwd(qp, qg2, name="q_norm", scale=HEAD_DIM ** -0.5)
    o = attn_fwd(qn, k2, v2, b_sinks[0], name="attn")
    x3, (h_f1,), r_f1 = mm_residual(o, w_o_f[0], x2, name="o_proj", gains=[f_norm[1:2]])
    g_fin1, g_fout1 = gather_wait(later[5:7], send_sems[2], recv_sems[2], x3, name="gather_wait_2")
    w_f_in[1] = g_fin1
    w_f_out[1] = g_fout1.reshape(-1, d)
    a1, ffn1 = _ffn_fwd(x3, f_norm[1:2], h_f1, r_f1, w_f_in[1], conv_w_f[1], f_conv_b[1], f, "1")
    dx4, sq = mm_residual(a1, w_f_out[1], x3, name="ffn1_out", target=loss_target[0])
    loss_part = (0.5 * jnp.sum(sq) / d).reshape(1)

    dx3, d_fn1, d_fwin1, d_cw1, d_cb1, d_fwout1 = _ffn_bwd(dx4, ffn1, w_f_in[1], w_f_out[1], "1")
    do = mm_nt([dx3], w_o_f, name="o_proj_dx")
    d_w_o = mm_tn(o, [dx3], name="o_proj_dw", n_s=d)
    dqn, dk2, dv2, dsink = attn_bwd(qn, k2, v2, do, b_sinks[0], name="attn_bwd")
    dqp, dqg = q_norm_bwd(dqn, qp, qg2, name="q_norm_bwd", scale=HEAD_DIM ** -0.5)
    dkv, dkg = kv_post_bwd(dk2, dv2, kv, kg2, name="kv_post_bwd")
    d_w_q = mm_tn(h_q, [dqp], name="q_proj_dw", n_s=w_q_f.shape[2])
    dh_q = mm_nt([dqp], w_q_f, name="q_proj_dx")
    d_w_kv = mm_tn(h_k, [dkv], name="kv_proj_dw", n_s=w_kv_f.shape[2])
    dh_k = mm_nt([dkv], w_kv_f, name="kv_proj_dx")
    dx2, (d_kvn, d_bn) = rms_bwd([dh_k, dh_q], x2, r_b, [kv_norm[None], b_norm], dx3, name="b_norm_bwd")
    dx1, d_fn0, d_fwin0, d_cw0, d_cb0, d_fwout0 = _ffn_bwd(dx2, ffn0, w_f_in[0], w_f_out[0], "0")
    dy_a = mm_nt([dx1], w_a_out, name="a_out_dx")
    d_w_aout = mm_tn(y_a, [dx1], name="a_out_dw", n_s=d)
    dzu, dzv, d_avn, d_ws, d_bt = sgu_gate_bwd(zu, zv, dy_a, a_v_norm_f, wc, bt, name="a_gate_bwd")
    d_w_ain = mm_tn(h_a, [dzu, dzv], name="a_in_dw", n_s=w_a_in.shape[2])
    dh_a = mm_nt([dzu, dzv], w_a_in, name="a_in_dx")
    dx0, (d_an,) = rms_bwd([dh_a], x0, r_a, [a_norm_f], dx1, name="a_norm_bwd")
    grad_x = dx0[None]

    sh = N_SHARDS
    units = [d_w_ain, d_w_aout.reshape(sh, -1, d), d_fwin0, d_fwin1, d_fwout0.reshape(sh, -1, d),
             d_fwout1.reshape(sh, -1, d), d_w_kv.reshape(sh, -1, d_w_kv.shape[2]), d_w_q.reshape(sh, -1, d_w_q.shape[2]),
             d_w_o.reshape(sh, -1, d)]
    other_bf = rs_cast_other_half(units, c_arr, name="rs_cast")
    from_sib = sibling_exchange(list(other_bf), name="rs_sibling")
    chip_bf, own = rs_add_sibling(units, from_sib, c_arr, name="rs_add")
    d_fn = jnp.concatenate([d_fn0, d_fn1], axis=0)
    d_cw = jnp.stack([d_cw0, d_cw1])
    d_cb = jnp.stack([d_cb0, d_cb1])
    d_kg = (dkg[0, :HEAD_DIM] + dkg[0, HEAD_DIM:])
    d_qg = (dqg[0, :HEAD_DIM] + dqg[0, HEAD_DIM:])[None]
    small_full = [d_an, d_avn, d_ws[None], d_bt.T[None], d_fn, d_cw, d_cb, d_kvn[0], d_kg, d_bn, d_qg,
                  dsink[:, :N_Q_HEADS], loss_part]
    from_chips, from_all = chip_scatter(list(chip_bf), _pack(small_full), name="rs_chips")
    halves = rs_sum_chips(list(own), from_chips, c_arr, name="rs_sum")
    mg = [m.reshape(-1, m.shape[2]) for m in sibling_merge(list(halves), name="rs_merge")]
    big_w = [weights[n] for n in BIG]
    big_gs = [[mg[0]], [mg[1]], [mg[2], mg[3]], [mg[4], mg[5]], [mg[6]], [mg[7]], [mg[8]]]
    big_g, big_d, big_m, big_v = adamw(big_w, big_gs, [moms[n] for n in BIG], [vars_[n] for n in BIG], name="adamw_big")

    full_shapes = [g.shape for g in small_full]
    small_g = _unpack(sum_leading(from_all, name="small_sum"), full_shapes)
    loss = small_g.pop()[0]
    small_g[0] = lax.dynamic_slice_in_dim(small_g[0], chip * ns_cols, ns_cols, axis=1)
    small_g[1] = lax.dynamic_slice_in_dim(small_g[1], chip * ns_cols, ns_cols, axis=1)
    small_g[5] = lax.dynamic_slice_in_dim(small_g[5], chip * nf_cols, nf_cols, axis=2)
    small_shapes = [weights[n].shape for n in SMALL]
    small_g = [g.reshape(s) for g, s in zip(small_g, small_shapes)]
    pw, pg_, pm, pv = (_pack(v) for v in ([weights[n] for n in SMALL], small_g, [moms[n] for n in SMALL],
                                          [vars_[n] for n in SMALL]))
    _, (sd,), (sm,), (sv,) = adamw([pw], [[pg_]], [pm], [pv], name="adamw_small")
    small_d, small_m, small_v = (_unpack(v, small_shapes) for v in (sd, sm, sv))

    out = {}
    for i, n in enumerate(SMALL):
        out[n] = (small_g[i], small_d[i], small_m[i], small_v[i])
    for i, n in enumerate(BIG):
        out[n] = (big_g[i], big_d[i], big_m[i], big_v[i])
    order = ["a_norm", "a_w_in", "a_v_norm", "a_w_s", "a_b_s", "a_w_out", "f_norm", "f_w_in", "f_conv_w", "f_conv_b",
             "f_w_out", "kv_norm", "w_kv", "k_norm", "b_norm", "b_w_q", "b_q_norm", "b_sinks", "b_w_o"]
    return (loss, grad_x, *[out[n][0] for n in order], *[out[n][1] for n in order],
            *[out[n][2] for n in order], *[out[n][3] for n in order])
```

```python
import functools
import math

import jax
import jax.numpy as jnp
from jax import lax
from jax.experimental import pallas as pl
from jax.experimental.pallas import tpu as pltpu

f32 = jnp.float32
bf16 = jnp.bfloat16
MESH = pl.DeviceIdType.MESH
ANY = pl.BlockSpec(memory_space=pl.ANY)

EPS = 1e-6
LANES = 128
CHUNK = 128
HEAD_DIM = 64
N_Q_HEADS = 16
N_KV_HEADS = 4
Q_PER_KV = N_Q_HEADS // N_KV_HEADS
N_SHARDS = 4
N_DEV = 8

ADAM_LR = 0.001
ADAM_B1 = 0.9
ADAM_B2 = 0.999
ADAM_EPS = 1e-08
ADAM_WD = 0.01
ADAM_STEP = 10
ADAM_C1 = 1.0 - ADAM_B1 ** ADAM_STEP
ADAM_C2 = 1.0 - ADAM_B2 ** ADAM_STEP

_INV_SQRT2 = 1.0 / math.sqrt(2.0)
_INV_SQRT2PI = 1.0 / math.sqrt(2.0 * math.pi)


def _params(*sem):
    return pltpu.CompilerParams(dimension_semantics=sem)


def _gelu(z):
    return 0.5 * z * (1.0 + lax.erf(z * _INV_SQRT2))


def _gelu_grad(z):
    return 0.5 * (1.0 + lax.erf(z * _INV_SQRT2)) + z * jnp.exp(-0.5 * z * z) * _INV_SQRT2PI


def _dot(a, b):
    return jnp.dot(a, b, preferred_element_type=f32)


def _dot_nt(a, b):
    return lax.dot_general(a, b, (((1,), (1,)), ((), ())), preferred_element_type=f32)


def _dot_tn(a, b):
    return lax.dot_general(a, b, (((0,), (0,)), ((), ())), preferred_element_type=f32)


def _dot_exact(a, b):
    return jnp.dot(a, b, preferred_element_type=f32, precision=lax.Precision.HIGHEST)


VMEM_TILE_BUDGET = 36 * 1024 * 1024
MAX_ROW_TILE = 2048


def _row_tile(m, fixed_bytes, row_bytes):
    tm = min(m, MAX_ROW_TILE)
    while tm > 256 and 2 * (fixed_bytes + tm * row_bytes) > VMEM_TILE_BUDGET:
        tm //= 2
    return tm


def _isz(a):
    return jnp.dtype(a.dtype).itemsize


def mm_nn(a, w3, *, name, s0=0, ns=None, add=None, out_dtype=f32):
    m, k = a.shape
    s_all, _, n_s = w3.shape
    ns = s_all if ns is None else ns
    tm = _row_tile(m, k * n_s * 2, k * _isz(a) + n_s * jnp.dtype(out_dtype).itemsize + (0 if add is None else n_s * 4))

    def body(*refs):
        if add is None:
            a_ref, w_ref, o_ref = refs
            acc = _dot(a_ref[...].astype(bf16), w_ref[0])
        else:
            a_ref, w_ref, add_ref, o_ref = refs
            acc = _dot(a_ref[...].astype(bf16), w_ref[0]) + add_ref[...]
        o_ref[...] = acc.astype(out_dtype)

    in_specs = [pl.BlockSpec((tm, k), lambda j, i: (i, 0)),
                pl.BlockSpec((1, k, n_s), lambda j, i: (s0 + j, 0, 0))]
    args = [a, w3]
    if add is not None:
        in_specs.append(pl.BlockSpec((tm, n_s), lambda j, i: (i, j)))
        args.append(add)
    return pl.pallas_call(
        body, name=name, grid=(ns, m // tm), in_specs=in_specs,
        out_specs=pl.BlockSpec((tm, n_s), lambda j, i: (i, j)),
        out_shape=jax.ShapeDtypeStruct((m, ns * n_s), out_dtype),
        compiler_params=_params("parallel", "parallel"))(*args)


def mm_nt(a_list, w3, *, name, tko=None, add=None, out_dtype=f32):
    s_all, k_out, n_s = w3.shape
    m = a_list[0].shape[0]
    na = len(a_list)
    spa = s_all // na
    tko = k_out if tko is None else tko
    tm = _row_tile(m, tko * n_s * 2, na * n_s * _isz(a_list[0]) + tko * 4 * (1 if add is None else 2))

    def body(*refs):
        a_refs = refs[:na]
        w_ref = refs[na]
        o_ref = refs[-1]
        s = pl.program_id(2)

        @pl.when(s == 0)
        def _():
            if add is None:
                o_ref[...] = jnp.zeros_like(o_ref)
            else:
                o_ref[...] = refs[na + 1][...]

        for idx in range(na):
            @pl.when(s // spa == idx)
            def _(idx=idx):
                o_ref[...] += _dot_nt(a_refs[idx][...].astype(bf16), w_ref[0])

    def a_map(idx):
        return lambda ko, i, s: (i, jnp.clip(s - idx * spa, 0, spa - 1))

    in_specs = [pl.BlockSpec((tm, n_s), a_map(idx)) for idx in range(na)]
    in_specs.append(pl.BlockSpec((1, tko, n_s), lambda ko, i, s: (s, ko, 0)))
    args = list(a_list) + [w3]
    if add is not None:
        in_specs.append(pl.BlockSpec((tm, tko), lambda ko, i, s: (i, ko)))
        args.append(add)
    return pl.pallas_call(
        body, name=name, grid=(k_out // tko, m // tm, s_all), in_specs=in_specs,
        out_specs=pl.BlockSpec((tm, tko), lambda ko, i, s: (i, ko)),
        out_shape=jax.ShapeDtypeStruct((m, k_out), out_dtype),
        compiler_params=_params("parallel", "parallel", "arbitrary"))(*args)


def mm_tn(a, b_list, *, name, n_s, tki=None):
    m, k_in = a.shape
    na = len(b_list)
    s_all = sum(b.shape[1] for b in b_list) // n_s
    spa = s_all // na
    tki = k_in if tki is None else tki
    tm = _row_tile(m, tki * n_s * 4, tki * _isz(a) + na * n_s * _isz(b_list[0]))

    def body(*refs):
        a_ref = refs[0]
        b_refs = refs[1:1 + na]
        o_ref = refs[-1]
        s = pl.program_id(0)
        r = pl.program_id(2)

        @pl.when(r == 0)
        def _():
            o_ref[...] = jnp.zeros_like(o_ref)

        for idx in range(na):
            @pl.when(s // spa == idx)
            def _(idx=idx):
                o_ref[0] += _dot_tn(a_ref[...].astype(bf16), b_refs[idx][...].astype(bf16))

    def b_map(idx):
        def index(s, ki, r):
            active = (s // spa) == idx
            return (jnp.where(active, r, 0), jnp.clip(s - idx * spa, 0, spa - 1))
        return index

    in_specs = [pl.BlockSpec((tm, tki), lambda s, ki, r: (r, ki))]
    in_specs += [pl.BlockSpec((tm, n_s), b_map(idx)) for idx in range(na)]
    return pl.pallas_call(
        body, name=name, grid=(s_all, k_in // tki, m // tm), in_specs=in_specs,
        out_specs=pl.BlockSpec((1, tki, n_s), lambda s, ki, r: (s, ki, 0)),
        out_shape=jax.ShapeDtypeStruct((s_all, k_in, n_s), f32),
        compiler_params=_params("parallel", "parallel", "arbitrary"))(a, *b_list)


def mm_residual(a, w, x, *, name, gains=(), target=None):
    m, k = a.shape
    d = w.shape[1]
    ng = len(gains)
    tm = _row_tile(m, k * d * 2, k * _isz(a) + d * 4 * 3 + ng * d * 2)

    def body(*refs):
        a_ref, w_ref, x_ref = refs[:3]
        y = _dot(a_ref[...].astype(bf16), w_ref[...]) + x_ref[...]
        if target is None:
            g_refs = refs[3:3 + ng]
            y_ref = refs[3 + ng]
            h_refs = refs[4 + ng:4 + 2 * ng]
            r_ref = refs[-1]
            y_ref[...] = y
            r = lax.rsqrt(jnp.mean(y * y, axis=1, keepdims=True) + EPS)
            yh = y * r
            for g_ref, h_ref in zip(g_refs, h_refs):
                h_ref[...] = (yh * g_ref[...]).astype(bf16)
            r_ref[...] = r
        else:
            t_ref, dy_ref, s_ref = refs[3:]
            i = pl.program_id(0)
            e = y - t_ref[...]
            dy_ref[...] = e * (1.0 / d)
            part = jnp.sum(e * e, axis=0, keepdims=True)

            @pl.when(i == 0)
            def _():
                s_ref[...] = part

            @pl.when(i > 0)
            def _():
                s_ref[...] += part

    row = pl.BlockSpec((tm, d), lambda i: (i, 0))
    vec = pl.BlockSpec((1, d), lambda i: (0, 0))
    in_specs = [pl.BlockSpec((tm, k), lambda i: (i, 0)), pl.BlockSpec((k, d), lambda i: (0, 0)), row]
    if target is None:
        outs = pl.pallas_call(
            body, name=name, grid=(m // tm,), in_specs=in_specs + [vec] * ng,
            out_specs=[row] * (1 + ng) + [pl.BlockSpec((tm, 1), lambda i: (i, 0))],
            out_shape=[jax.ShapeDtypeStruct((m, d), f32)] + [jax.ShapeDtypeStruct((m, d), bf16)] * ng
            + [jax.ShapeDtypeStruct((m, 1), f32)],
            compiler_params=_params("parallel"))(a, w, x, *gains)
        return outs[0], outs[1:1 + ng], outs[-1]
    return pl.pallas_call(
        body, name=name, grid=(m // tm,), in_specs=in_specs + [row], out_specs=[row, vec],
        out_shape=[jax.ShapeDtypeStruct((m, d), f32), jax.ShapeDtypeStruct((1, d), f32)],
        compiler_params=_params("arbitrary"))(a, w, x, target)


def rms_fwd(x, gains, *, name, tr=512):
    t, d = x.shape
    tr = min(tr, t)
    ng = len(gains)

    def body(*refs):
        x_ref = refs[0]
        g_refs = refs[1:1 + ng]
        h_refs = refs[1 + ng:1 + 2 * ng]
        r_ref = refs[-1]
        xv = x_ref[...]
        r = lax.rsqrt(jnp.mean(xv * xv, axis=1, keepdims=True) + EPS)
        xh = xv * r
        for g_ref, h_ref in zip(g_refs, h_refs):
            h_ref[...] = (xh * g_ref[...]).astype(bf16)
        r_ref[...] = r

    row = pl.BlockSpec((tr, d), lambda i: (i, 0))
    vec = pl.BlockSpec((1, d), lambda i: (0, 0))
    outs = pl.pallas_call(
        body, name=name, grid=(t // tr,), in_specs=[row] + [vec] * ng,
        out_specs=[row] * ng + [pl.BlockSpec((tr, 1), lambda i: (i, 0))],
        out_shape=[jax.ShapeDtypeStruct((t, d), bf16)] * ng + [jax.ShapeDtypeStruct((t, 1), f32)],
        compiler_params=_params("parallel"))(x, *gains)
    return outs[:ng], outs[ng]


def rms_bwd(dh_list, x, r, gains, dx_in, *, name, tr=512):
    t, d = x.shape
    tr = min(tr, t)
    ng = len(gains)

    def body(*refs):
        dh_refs = refs[:ng]
        x_ref, r_ref = refs[ng], refs[ng + 1]
        g_refs = refs[ng + 2:2 * ng + 2]
        dxin_ref = refs[2 * ng + 2]
        dx_ref = refs[2 * ng + 3]
        dg_refs = refs[2 * ng + 4:]
        i = pl.program_id(0)
        rv = r_ref[...]
        xh = x_ref[...] * rv
        acc = dxin_ref[...]
        for dh_ref, g_ref, dg_ref in zip(dh_refs, g_refs, dg_refs):
            dh = dh_ref[...]
            part = jnp.sum(dh * xh, axis=0, keepdims=True)

            @pl.when(i == 0)
            def _(dg_ref=dg_ref, part=part):
                dg_ref[...] = part

            @pl.when(i > 0)
            def _(dg_ref=dg_ref, part=part):
                dg_ref[...] += part

            tg = dh * g_ref[...]
            acc = acc + rv * (tg - xh * jnp.mean(tg * xh, axis=1, keepdims=True))
        dx_ref[...] = acc

    row = pl.BlockSpec((tr, d), lambda i: (i, 0))
    vec = pl.BlockSpec((1, d), lambda i: (0, 0))
    outs = pl.pallas_call(
        body, name=name, grid=(t // tr,),
        in_specs=[row] * ng + [row, pl.BlockSpec((tr, 1), lambda i: (i, 0))] + [vec] * ng + [row],
        out_specs=[row] + [vec] * ng,
        out_shape=[jax.ShapeDtypeStruct((t, d), f32)] + [jax.ShapeDtypeStruct((1, d), f32)] * ng,
        compiler_params=_params("arbitrary"))(*dh_list, x, r, *gains, dx_in)
    return outs[0], outs[1:]


def sgu_gate_fwd(zu, zv, gv, wc, bt, *, name, tr=512):
    t, w = zu.shape
    tr = min(tr, t)
    groups = w // LANES

    def body(zu_ref, zv_ref, gv_ref, wc_ref, bt_ref, y_ref):
        vp = _gelu(zv_ref[...])
        rv = lax.rsqrt(jnp.mean(vp * vp, axis=1, keepdims=True) + EPS)
        vb = (vp * rv * gv_ref[...]).astype(bf16)
        for c in range(tr // CHUNK):
            rows = slice(c * CHUNK, (c + 1) * CHUNK)
            for g in range(groups):
                cols = slice(g * LANES, (g + 1) * LANES)
                sv = _dot(wc_ref[g], vb[rows, cols]) + bt_ref[:, g:g + 1]
                y_ref[rows, cols] = (_gelu(zu_ref[rows, cols]) * sv).astype(bf16)

    row = pl.BlockSpec((tr, w), lambda i: (i, 0))
    return pl.pallas_call(
        body, name=name, grid=(t // tr,),
        in_specs=[row, row, pl.BlockSpec((1, w), lambda i: (0, 0)),
                  pl.BlockSpec((groups, CHUNK, CHUNK), lambda i: (0, 0, 0)),
                  pl.BlockSpec((CHUNK, groups), lambda i: (0, 0))],
        out_specs=row, out_shape=jax.ShapeDtypeStruct((t, w), bf16),
        compiler_params=_params("parallel"))(zu, zv, gv, wc, bt)


def sgu_gate_bwd(zu, zv, dy, gv, wc, bt, *, name, tr=512):
    t, w = zu.shape
    tr = min(tr, t)
    groups = w // LANES
    nsteps = t // tr

    def body(zu_ref, zv_ref, dy_ref, gv_ref, wc_ref, bt_ref,
             dzu_ref, dzv_ref, dgv_ref, dws_ref, dbt_ref, dv_ref, bacc_ref):
        i = pl.program_id(0)

        @pl.when(i == 0)
        def _():
            dgv_ref[...] = jnp.zeros_like(dgv_ref)
            dws_ref[...] = jnp.zeros_like(dws_ref)
            bacc_ref[...] = jnp.zeros_like(bacc_ref)

        zvv = zv_ref[...]
        vp = _gelu(zvv)
        rv = lax.rsqrt(jnp.mean(vp * vp, axis=1, keepdims=True) + EPS)
        vhat = vp * rv
        vb = (vhat * gv_ref[...]).astype(bf16)
        for c in range(tr // CHUNK):
            rows = slice(c * CHUNK, (c + 1) * CHUNK)
            for g in range(groups):
                cols = slice(g * LANES, (g + 1) * LANES)
                vblk = vb[rows, cols]
                sv = _dot(wc_ref[g], vblk) + bt_ref[:, g:g + 1]
                zub = zu_ref[rows, cols]
                dyb = dy_ref[rows, cols]
                dzu_ref[rows, cols] = (dyb * sv * _gelu_grad(zub)).astype(bf16)
                dsv = dyb * _gelu(zub)
                bacc_ref[:, cols] += dsv
                dsvb = dsv.astype(bf16)
                dv_ref[rows, cols] = _dot_tn(wc_ref[g], dsvb)
                dws_ref[g] += _dot_nt(dsvb, vblk)
        dv = dv_ref[...]
        dgv_ref[...] += jnp.sum(dv * vhat, axis=0, keepdims=True)
        tg = dv * gv_ref[...]
        dvp = rv * (tg - vhat * jnp.mean(tg * vhat, axis=1, keepdims=True))
        dzv_ref[...] = (dvp * _gelu_grad(zvv)).astype(bf16)

        @pl.when(i == nsteps - 1)
        def _():
            tt = lax.broadcasted_iota(jnp.int32, (CHUNK, CHUNK), 0)
            ss = lax.broadcasted_iota(jnp.int32, (CHUNK, CHUNK), 1)
            for g in range(groups):
                dws_ref[g] = jnp.where(ss <= tt, dws_ref[g], 0.0)
                dbt_ref[:, g:g + 1] = jnp.sum(bacc_ref[:, g * LANES:(g + 1) * LANES], axis=1, keepdims=True)

    row = pl.BlockSpec((tr, w), lambda i: (i, 0))
    full3 = pl.BlockSpec((groups, CHUNK, CHUNK), lambda i: (0, 0, 0))
    return pl.pallas_call(
        body, name=name, grid=(nsteps,),
        in_specs=[row, row, row, pl.BlockSpec((1, w), lambda i: (0, 0)), full3,
                  pl.BlockSpec((CHUNK, groups), lambda i: (0, 0))],
        out_specs=[row, row, pl.BlockSpec((1, w), lambda i: (0, 0)), full3,
                   pl.BlockSpec((CHUNK, groups), lambda i: (0, 0))],
        out_shape=[jax.ShapeDtypeStruct((t, w), bf16), jax.ShapeDtypeStruct((t, w), bf16),
                   jax.ShapeDtypeStruct((1, w), f32), jax.ShapeDtypeStruct((groups, CHUNK, CHUNK), f32),
                   jax.ShapeDtypeStruct((CHUNK, groups), f32)],
        scratch_shapes=[pltpu.VMEM((tr, w), f32), pltpu.VMEM((CHUNK, w), f32)],
        compiler_params=_params("arbitrary"))(zu, zv, dy, gv, wc, bt)


HALO = 8


def _shift_down(v, halo, k, first):
    r = pltpu.roll(v, k, 0)
    hh = jnp.where(first, 0.0, pltpu.roll(halo, k, 0))
    rid = lax.broadcasted_iota(jnp.int32, (HALO, v.shape[1]), 0)
    head = jnp.where(rid < k, hh, r[0:HALO])
    if v.shape[0] == HALO:
        return head
    return jnp.concatenate([head, r[HALO:]], axis=0)


def _shift_up(v, halo, k, last):
    n = v.shape[0]
    r = pltpu.roll(v, n - k, 0)
    hh = jnp.where(last, 0.0, pltpu.roll(halo, HALO - k, 0))
    rid = lax.broadcasted_iota(jnp.int32, (HALO, v.shape[1]), 0)
    tail = jnp.where(rid >= HALO - k, hh, r[n - HALO:])
    return jnp.concatenate([r[:n - HALO], tail], axis=0)


def _conv(p, halo, w_ref, b_ref, first):
    return (w_ref[2:3, :] * p + w_ref[1:2, :] * _shift_down(p, halo, 1, first)
            + w_ref[0:1, :] * _shift_down(p, halo, 2, first) + b_ref[...])


def _conv_specs(t, tr, tc):
    tile = pl.BlockSpec((tr, tc), lambda j, i: (i, j))
    prev = pl.BlockSpec((HALO, tc), lambda j, i: (jnp.maximum(i * (tr // HALO) - 1, 0), j))
    nxt = pl.BlockSpec((HALO, tc), lambda j, i: (jnp.minimum((i + 1) * (tr // HALO), t // HALO - 1), j))
    wspec = pl.BlockSpec((3, tc), lambda j, i: (0, j))
    bspec = pl.BlockSpec((1, tc), lambda j, i: (0, j))
    return tile, prev, nxt, wspec, bspec


BF16_ROWS = 16


def ffn_in_fused(h, w_in4, wg, wu, bg, bu, *, name):
    t, k = h.shape
    s_all, _, n_s = w_in4.shape
    half = s_all // 2
    tm = _row_tile(t, 2 * k * n_s * 2, k * 2 + 2 * n_s * 4 + n_s * 2)

    def body(h_ref, hh_ref, wg_ref, wu_ref, cg_ref, cu_ref, bg_ref, bu_ref, pg_ref, pu_ref, a_ref):
        first = pl.program_id(1) == 0
        hv, hh = h_ref[...], hh_ref[...]
        outs = []
        for w_ref, c_ref, b_ref, p_ref in ((wg_ref, cg_ref, bg_ref, pg_ref), (wu_ref, cu_ref, bu_ref, pu_ref)):
            p = _dot(hv, w_ref[0])
            p_ref[...] = p
            outs.append(_conv(p, _dot(hh, w_ref[0])[BF16_ROWS - HALO:], c_ref, b_ref, first))
        gate, up = outs
        a_ref[...] = (gate * jax.nn.sigmoid(gate) * up).astype(bf16)

    tile = pl.BlockSpec((tm, n_s), lambda j, i: (i, j))
    cw = pl.BlockSpec((3, n_s), lambda j, i: (0, j))
    cb = pl.BlockSpec((1, n_s), lambda j, i: (0, j))
    f = half * n_s
    return pl.pallas_call(
        body, name=name, grid=(half, t // tm),
        in_specs=[pl.BlockSpec((tm, k), lambda j, i: (i, 0)),
                  pl.BlockSpec((BF16_ROWS, k), lambda j, i: (jnp.maximum(i * (tm // BF16_ROWS) - 1, 0), 0)),
                  pl.BlockSpec((1, k, n_s), lambda j, i: (j, 0, 0)),
                  pl.BlockSpec((1, k, n_s), lambda j, i: (j + half, 0, 0)), cw, cw, cb, cb],
        out_specs=[tile, tile, tile],
        out_shape=[jax.ShapeDtypeStruct((t, f), f32), jax.ShapeDtypeStruct((t, f), f32),
                   jax.ShapeDtypeStruct((t, f), bf16)],
        compiler_params=_params("parallel", "parallel"))(h, h, w_in4, w_in4, wg, wu, bg, bu)


def _gate_grads(gate, up, dav):
    sg = jax.nn.sigmoid(gate)
    return dav * up * (sg * (1.0 + gate * (1.0 - sg))), dav * gate * sg


def ffn_gate_bwd(pg, pu, da, wg, wu, bg, bu, *, name, tr=1024, tc=256):
    t, f = pg.shape
    tr = min(tr, t)
    tile, prev, nxt, wspec, bspec = _conv_specs(t, tr, tc)
    nsteps = t // tr

    def body(pg_ref, pgh_ref, pgn_ref, pu_ref, puh_ref, pun_ref, da_ref, dan_ref, wg_ref, wu_ref, bg_ref, bu_ref,
             dg_ref, du_ref, sg_ref, su_ref):
        i = pl.program_id(1)
        first = i == 0
        last = i == nsteps - 1
        pgv, puv = pg_ref[...], pu_ref[...]
        pg1, pg2 = _shift_down(pgv, pgh_ref[...], 1, first), _shift_down(pgv, pgh_ref[...], 2, first)
        pu1, pu2 = _shift_down(puv, puh_ref[...], 1, first), _shift_down(puv, puh_ref[...], 2, first)
        gate = wg_ref[2:3, :] * pgv + wg_ref[1:2, :] * pg1 + wg_ref[0:1, :] * pg2 + bg_ref[...]
        up = wu_ref[2:3, :] * puv + wu_ref[1:2, :] * pu1 + wu_ref[0:1, :] * pu2 + bu_ref[...]
        dgate, dup = _gate_grads(gate, up, da_ref[...])
        gate_n = _conv(pgn_ref[...], pgv[tr - HALO:], wg_ref, bg_ref, False)
        up_n = _conv(pun_ref[...], puv[tr - HALO:], wu_ref, bu_ref, False)
        dgate_n, dup_n = _gate_grads(gate_n, up_n, dan_ref[...])
        for d, d_n, w_ref, o_ref in ((dgate, dgate_n, wg_ref, dg_ref), (dup, dup_n, wu_ref, du_ref)):
            o_ref[...] = (w_ref[2:3, :] * d + w_ref[1:2, :] * _shift_up(d, d_n, 1, last)
                          + w_ref[0:1, :] * _shift_up(d, d_n, 2, last)).astype(bf16)
        rid = lax.broadcasted_iota(jnp.int32, (8, tc), 0)
        for d, p0, p1, p2, s_ref in ((dgate, pgv, pg1, pg2, sg_ref), (dup, puv, pu1, pu2, su_ref)):
            sums = [jnp.sum(d * p2, axis=0, keepdims=True), jnp.sum(d * p1, axis=0, keepdims=True),
                    jnp.sum(d * p0, axis=0, keepdims=True), jnp.sum(d, axis=0, keepdims=True)]
            part = jnp.zeros((8, tc), f32)
            for k, sk in enumerate(sums):
                part = jnp.where(rid == k, sk, part)

            @pl.when(first)
            def _(s_ref=s_ref, part=part):
                s_ref[...] = part

            @pl.when(i > 0)
            def _(s_ref=s_ref, part=part):
                s_ref[...] += part

    stat = pl.BlockSpec((8, tc), lambda j, i: (0, j))
    return pl.pallas_call(
        body, name=name, grid=(f // tc, nsteps),
        in_specs=[tile, prev, nxt, tile, prev, nxt, tile, nxt, wspec, wspec, bspec, bspec],
        out_specs=[tile, tile, stat, stat],
        out_shape=[jax.ShapeDtypeStruct((t, f), bf16), jax.ShapeDtypeStruct((t, f), bf16),
                   jax.ShapeDtypeStruct((8, f), f32), jax.ShapeDtypeStruct((8, f), f32)],
        compiler_params=_params("parallel", "arbitrary"))(pg, pg, pg, pu, pu, pu, da, da, wg, wu, bg, bu)


def _head_mean_matrix():
    i = lax.broadcasted_iota(jnp.int32, (LANES, LANES), 0) // HEAD_DIM
    j = lax.broadcasted_iota(jnp.int32, (LANES, LANES), 1) // HEAD_DIM
    return jnp.where(i == j, 1.0 / HEAD_DIM, 0.0).astype(f32)


def _lane_half(shape):
    return (lax.broadcasted_iota(jnp.int32, shape, 1) % LANES) // HEAD_DIM


def q_norm_fwd(qp, g2, *, name, scale, tr=512):
    t, w = qp.shape
    tr = min(tr, t)

    def body(x_ref, g_ref, o_ref):
        bd = _head_mean_matrix()
        for cb in range(w // LANES):
            cols = slice(cb * LANES, (cb + 1) * LANES)
            xc = x_ref[:, cols]
            rh = lax.rsqrt(_dot_exact(xc * xc, bd) + EPS)
            o_ref[:, cols] = (xc * rh * g_ref[...] * scale).astype(bf16)

    row = pl.BlockSpec((tr, w), lambda i: (i, 0))
    return pl.pallas_call(
        body, name=name, grid=(t // tr,), in_specs=[row, pl.BlockSpec((1, LANES), lambda i: (0, 0))],
        out_specs=row, out_shape=jax.ShapeDtypeStruct((t, w), bf16),
        compiler_params=_params("parallel"))(qp, g2)


def q_norm_bwd(dq, qp, g2, *, name, scale, tr=512):
    t, w = qp.shape
    tr = min(tr, t)

    def body(dq_ref, x_ref, g_ref, o_ref, dg_ref):
        i = pl.program_id(0)
        bd = _head_mean_matrix()
        acc = jnp.zeros((1, LANES), f32)
        for cb in range(w // LANES):
            cols = slice(cb * LANES, (cb + 1) * LANES)
            xc = x_ref[:, cols]
            rh = lax.rsqrt(_dot_exact(xc * xc, bd) + EPS)
            xh = xc * rh
            dy = dq_ref[:, cols] * scale
            acc = acc + jnp.sum(dy * xh, axis=0, keepdims=True)
            tg = dy * g_ref[...]
            o_ref[:, cols] = (rh * (tg - xh * _dot_exact(tg * xh, bd))).astype(bf16)

        @pl.when(i == 0)
        def _():
            dg_ref[...] = acc

        @pl.when(i > 0)
        def _():
            dg_ref[...] += acc

    row = pl.BlockSpec((tr, w), lambda i: (i, 0))
    vec = pl.BlockSpec((1, LANES), lambda i: (0, 0))
    return pl.pallas_call(
        body, name=name, grid=(t // tr,), in_specs=[row, row, vec], out_specs=[row, vec],
        out_shape=[jax.ShapeDtypeStruct((t, w), bf16), jax.ShapeDtypeStruct((1, LANES), f32)],
        compiler_params=_params("arbitrary"))(dq, qp, g2)


def kv_post_fwd(kv, g2, *, name, tr=512):
    t, w = kv.shape
    tr = min(tr, t)
    kw = w // 2

    def body(x_ref, g_ref, k_ref, v_ref):
        bd = _head_mean_matrix()
        half = _lane_half((tr, LANES))
        for cb in range(kw // LANES):
            xc = x_ref[:, cb * LANES:(cb + 1) * LANES]
            rh = lax.rsqrt(_dot_exact(xc * xc, bd) + EPS)
            kn = xc * rh * g_ref[...]
            vc = x_ref[:, kw + cb * LANES:kw + (cb + 1) * LANES]
            for src, dst in ((kn, k_ref), (vc, v_ref)):
                sw = pltpu.roll(src, HEAD_DIM, 1)
                for hf in range(2):
                    blk = 2 * cb + hf
                    dst[:, blk * LANES:(blk + 1) * LANES] = jnp.where(half == hf, src, sw).astype(bf16)

    return pl.pallas_call(
        body, name=name, grid=(t // tr,),
        in_specs=[pl.BlockSpec((tr, w), lambda i: (i, 0)), pl.BlockSpec((1, LANES), lambda i: (0, 0))],
        out_specs=[pl.BlockSpec((tr, 2 * kw), lambda i: (i, 0))] * 2,
        out_shape=[jax.ShapeDtypeStruct((t, 2 * kw), bf16)] * 2,
        compiler_params=_params("parallel"))(kv, g2)


def kv_post_bwd(dk2, dv2, kv, g2, *, name, tr=512):
    t, w = kv.shape
    tr = min(tr, t)
    kw = w // 2

    def body(dk_ref, dv_ref, x_ref, g_ref, o_ref, dg_ref):
        i = pl.program_id(0)
        bd = _head_mean_matrix()
        half = _lane_half((tr, LANES))
        acc = jnp.zeros((1, LANES), f32)

        def fold(ref, cb):
            a = ref[:, (2 * cb) * LANES:(2 * cb + 1) * LANES]
            b = ref[:, (2 * cb + 1) * LANES:(2 * cb + 2) * LANES]
            return jnp.where(half == 0, a + pltpu.roll(a, HEAD_DIM, 1), b + pltpu.roll(b, HEAD_DIM, 1))

        for cb in range(kw // LANES):
            cols = slice(cb * LANES, (cb + 1) * LANES)
            xc = x_ref[:, cols]
            rh = lax.rsqrt(_dot_exact(xc * xc, bd) + EPS)
            xh = xc * rh
            dy = fold(dk_ref, cb)
            acc = acc + jnp.sum(dy * xh, axis=0, keepdims=True)
            tg = dy * g_ref[...]
            o_ref[:, cols] = (rh * (tg - xh * _dot_exact(tg * xh, bd))).astype(bf16)
            o_ref[:, kw + cb * LANES:kw + (cb + 1) * LANES] = fold(dv_ref, cb).astype(bf16)

        @pl.when(i == 0)
        def _():
            dg_ref[...] = acc

        @pl.when(i > 0)
        def _():
            dg_ref[...] += acc

    dup = pl.BlockSpec((tr, 2 * kw), lambda i: (i, 0))
    row = pl.BlockSpec((tr, w), lambda i: (i, 0))
    vec = pl.BlockSpec((1, LANES), lambda i: (0, 0))
    return pl.pallas_call(
        body, name=name, grid=(t // tr,), in_specs=[dup, dup, row, vec], out_specs=[row, vec],
        out_shape=[jax.ShapeDtypeStruct((t, w), bf16), jax.ShapeDtypeStruct((1, LANES), f32)],
        compiler_params=_params("arbitrary"))(dk2, dv2, kv, g2)


def _slope(h):
    return 2.0 ** (-8.0 * (h + 1) / N_Q_HEADS)


GROUP_ROWS = Q_PER_KV * CHUNK


def _band_mask(n):
    tq = lax.broadcasted_iota(jnp.int32, (GROUP_ROWS, 2 * CHUNK), 0) % CHUNK
    jk = lax.broadcasted_iota(jnp.int32, (GROUP_ROWS, 2 * CHUNK), 1)
    dist = tq + CHUNK - jk
    ok = (dist >= 0) & (dist < CHUNK) & jnp.logical_not((n == 0) & (jk < CHUNK))
    return dist.astype(f32), ok


def _band(ref, n, kh):
    p0 = pl.multiple_of(jnp.maximum(n - 1, 0) * CHUNK, CHUNK)
    c0 = pl.multiple_of(n * CHUNK, CHUNK)
    cols = slice(kh * LANES, (kh + 1) * LANES)
    return jnp.concatenate([ref[pl.ds(p0, CHUNK), cols], ref[pl.ds(c0, CHUNK), cols]], axis=0)


def _stack_heads(ref, kh, half):
    parts = []
    for cb in (2 * kh, 2 * kh + 1):
        xc = ref[:, cb * LANES:(cb + 1) * LANES].astype(f32)
        parts += [jnp.where(half == hf, xc, 0.0).astype(bf16) for hf in range(2)]
    return jnp.concatenate(parts, axis=0)


def _unstack_heads(x4, half):
    return (jnp.where(half == 0, x4[0:CHUNK], x4[CHUNK:2 * CHUNK]),
            jnp.where(half == 0, x4[2 * CHUNK:3 * CHUNK], x4[3 * CHUNK:]))


def _per_head_column(kh, values):
    grp = lax.broadcasted_iota(jnp.int32, (GROUP_ROWS, 1), 0) // CHUNK
    col = jnp.full((GROUP_ROWS, 1), values[0], f32)
    for g in range(1, Q_PER_KV):
        col = jnp.where(grp == g, values[g], col)
    return col


def _softmax_band(q4, kband, dist, ok, slope, sink):
    s = _dot_nt(q4, kband)
    s = jnp.where(ok, s - slope * dist, -jnp.inf)
    m = jnp.maximum(jnp.max(s, axis=1, keepdims=True), sink)
    e = jnp.exp(s - m)
    es = jnp.exp(sink - m)
    den = jnp.sum(e, axis=1, keepdims=True) + es
    return e / den, es / den


def attn_fwd(q, k2, v2, sinks, *, name):
    t, w = q.shape
    nb = t // CHUNK

    def body(sink_ref, q_ref, k_ref, v_ref, o_ref):
        n = pl.program_id(0)
        dist, ok = _band_mask(n)
        half = _lane_half((CHUNK, LANES))
        for kh in range(N_KV_HEADS):
            heads = [Q_PER_KV * kh + g for g in range(Q_PER_KV)]
            slope = _per_head_column(kh, [_slope(h) for h in heads])
            sink = _per_head_column(kh, [sink_ref[h] for h in heads])
            q4 = _stack_heads(q_ref, kh, half)
            p, _ = _softmax_band(q4, _band(k_ref, n, kh), dist, ok, slope, sink)
            o4 = _dot(p.astype(bf16), _band(v_ref, n, kh))
            lo, hi = _unstack_heads(o4, half)
            o_ref[:, (2 * kh) * LANES:(2 * kh + 1) * LANES] = lo.astype(bf16)
            o_ref[:, (2 * kh + 1) * LANES:(2 * kh + 2) * LANES] = hi.astype(bf16)

    full = pl.BlockSpec((t, k2.shape[1]), lambda n: (0, 0))
    return pl.pallas_call(
        body, name=name, grid=(nb,),
        in_specs=[pl.BlockSpec(memory_space=pltpu.SMEM), pl.BlockSpec((CHUNK, w), lambda n: (n, 0)), full, full],
        out_specs=pl.BlockSpec((CHUNK, w), lambda n: (n, 0)),
        out_shape=jax.ShapeDtypeStruct((t, w), bf16),
        compiler_params=_params("parallel"))(sinks, q, k2, v2)


def attn_bwd(q, k2, v2, do, sinks, *, name):
    t, w = q.shape
    nb = t // CHUNK
    kw = k2.shape[1]

    def body(sink_ref, q_ref, k_ref, v_ref, do_ref, dq_ref, dk_ref, dv_ref, ds_ref, kc_ref, vc_ref):
        n = pl.program_id(0)

        @pl.when(n == 0)
        def _():
            ds_ref[...] = jnp.zeros_like(ds_ref)
            kc_ref[...] = jnp.zeros_like(kc_ref)
            vc_ref[...] = jnp.zeros_like(vc_ref)
            dk_ref[...] = jnp.zeros_like(dk_ref)
            dv_ref[...] = jnp.zeros_like(dv_ref)

        @pl.when(n == nb)
        def _():
            dk_ref[...] = kc_ref[...]
            dv_ref[...] = vc_ref[...]

        @pl.when(n < nb)
        def _():
            dist, ok = _band_mask(n)
            half = _lane_half((CHUNK, LANES))
            lane = lax.broadcasted_iota(jnp.int32, (1, LANES), 1)
            sink_acc = jnp.zeros((1, LANES), f32)
            for kh in range(N_KV_HEADS):
                heads = [Q_PER_KV * kh + g for g in range(Q_PER_KV)]
                slope = _per_head_column(kh, [_slope(h) for h in heads])
                sink = _per_head_column(kh, [sink_ref[h] for h in heads])
                q4 = _stack_heads(q_ref, kh, half)
                do4 = _stack_heads(do_ref, kh, half)
                kband = _band(k_ref, n, kh)
                vband = _band(v_ref, n, kh)
                p, ps = _softmax_band(q4, kband, dist, ok, slope, sink)
                dp = _dot_nt(do4, vband)
                delta = jnp.sum(p * dp, axis=1, keepdims=True)
                dsb = (p * (dp - delta)).astype(bf16)
                sd = ps * delta
                for g, h in enumerate(heads):
                    part = jnp.sum(sd[g * CHUNK:(g + 1) * CHUNK], axis=0, keepdims=True)
                    sink_acc = sink_acc + jnp.where(lane == h, -part, 0.0)
                lo, hi = _unstack_heads(_dot(dsb, kband), half)
                dq_ref[:, (2 * kh) * LANES:(2 * kh + 1) * LANES] = lo
                dq_ref[:, (2 * kh + 1) * LANES:(2 * kh + 2) * LANES] = hi
                dkb = _dot_tn(dsb, q4)
                dvb = _dot_tn(p.astype(bf16), do4)
                cols = slice(kh * LANES, (kh + 1) * LANES)
                dk_ref[:, cols] = kc_ref[:, cols] + dkb[0:CHUNK]
                dv_ref[:, cols] = vc_ref[:, cols] + dvb[0:CHUNK]
                kc_ref[:, cols] = dkb[CHUNK:]
                vc_ref[:, cols] = dvb[CHUNK:]
            ds_ref[...] += sink_acc

    full = pl.BlockSpec((t, kw), lambda n: (0, 0))
    qblk = pl.BlockSpec((CHUNK, w), lambda n: (jnp.minimum(n, nb - 1), 0))
    kblk = pl.BlockSpec((CHUNK, kw), lambda n: (jnp.maximum(n - 1, 0), 0))
    return pl.pallas_call(
        body, name=name, grid=(nb + 1,),
        in_specs=[pl.BlockSpec(memory_space=pltpu.SMEM), qblk, full, full, qblk],
        out_specs=[qblk, kblk, kblk, pl.BlockSpec((1, LANES), lambda n: (0, 0))],
        out_shape=[jax.ShapeDtypeStruct((t, w), f32), jax.ShapeDtypeStruct((t, kw), f32),
                   jax.ShapeDtypeStruct((t, kw), f32), jax.ShapeDtypeStruct((1, LANES), f32)],
        scratch_shapes=[pltpu.VMEM((CHUNK, kw), f32), pltpu.VMEM((CHUNK, kw), f32)],
        compiler_params=_params("arbitrary"))(sinks, q, k2, v2, do)


def loss_head(y, target, *, name, tr=512):
    t, d = y.shape
    tr = min(tr, t)

    def body(y_ref, t_ref, dy_ref, s_ref):
        i = pl.program_id(0)
        e = y_ref[...] - t_ref[...]
        dy_ref[...] = e * (1.0 / d)
        part = jnp.sum(e * e, axis=0, keepdims=True)

        @pl.when(i == 0)
        def _():
            s_ref[...] = part

        @pl.when(i > 0)
        def _():
            s_ref[...] += part

    row = pl.BlockSpec((tr, d), lambda i: (i, 0))
    vec = pl.BlockSpec((1, d), lambda i: (0, 0))
    return pl.pallas_call(
        body, name=name, grid=(t // tr,), in_specs=[row, row], out_specs=[row, vec],
        out_shape=[jax.ShapeDtypeStruct((t, d), f32), jax.ShapeDtypeStruct((1, d), f32)],
        compiler_params=_params("arbitrary"))(y, target)


N_STEPS = 8


def _row_blocks(shape):
    if len(shape) == 2:
        r, c = shape
        return (r // N_STEPS, c), (lambda s: (s, 0))
    l, r, c = shape
    per = N_STEPS // l
    return (1, r // per, c), (lambda s: (s // per, s % per, 0))


CAST_STEPS = 4


def cast_into_slot(arrays, k_arr, *, name):
    in_specs, out_specs, out_shape, layers = [], [], [], []
    for a in arrays:
        r, c = a.shape[-2:]
        rb = r // CAST_STEPS
        if a.ndim == 2:
            in_specs.append(pl.BlockSpec((rb, c), lambda s, k: (s, 0)))
            layers.append(None)
        else:
            for l in range(a.shape[0]):
                in_specs.append(pl.BlockSpec((1, rb, c), lambda s, k, l=l: (l, s, 0)))
                layers.append(l)
        for _ in range(1 if a.ndim == 2 else a.shape[0]):
            out_specs.append(pl.BlockSpec((1, rb, c), lambda s, k: (k[0], s, 0)))
            out_shape.append(jax.ShapeDtypeStruct((N_SHARDS, r, c), bf16))
    n = len(in_specs)

    def body(k_ref, *refs):
        for i_ref, o_ref, l in zip(refs[:n], refs[n:], layers):
            o_ref[0] = (i_ref[...] if l is None else i_ref[0]).astype(bf16)

    args = []
    for a in arrays:
        args += [a] * (1 if a.ndim == 2 else a.shape[0])
    return pl.pallas_call(
        body, name=name,
        grid_spec=pltpu.PrefetchScalarGridSpec(num_scalar_prefetch=1, grid=(CAST_STEPS,),
                                               in_specs=in_specs, out_specs=out_specs),
        out_shape=out_shape, compiler_params=_params("parallel"))(k_arr, *args)


def adamw(ws, gs, ms, vs, *, name):
    n = len(ws)
    specs, g_specs, g_count = [], [], []
    for w, g_list in zip(ws, gs):
        blk, index = _row_blocks(w.shape)
        specs.append(pl.BlockSpec(blk, index))
        layers = len(g_list)
        per = N_STEPS // layers
        g_count.append(layers)
        for l in range(layers):
            g_specs.append(pl.BlockSpec(blk[-2:], lambda s, l=l, per=per: (jnp.where(s // per == l, s % per, 0), 0)))
    ng = len(g_specs)

    def body(*refs):
        s = pl.program_id(0)
        g_refs = refs[3 * n:3 * n + ng]
        outs = refs[3 * n + ng:]
        off = 0
        for i in range(n):
            w_ref, m_ref, v_ref = refs[i], refs[n + i], refs[2 * n + i]
            go_ref, d_ref, nm_ref, nv_ref = (outs[k * n + i] for k in range(4))
            layers = g_count[i]
            g = g_refs[off][...]
            for l in range(1, layers):
                g = jnp.where(s // (N_STEPS // layers) == l, g_refs[off + l][...], g)
            off += layers
            g = g.reshape(w_ref.shape)
            m = ADAM_B1 * m_ref[...] + (1.0 - ADAM_B1) * g
            v = ADAM_B2 * v_ref[...] + (1.0 - ADAM_B2) * (g * g)
            m_hat = m / ADAM_C1
            v_hat = v / ADAM_C2
            go_ref[...] = g
            d_ref[...] = -ADAM_LR * (m_hat / (jnp.sqrt(v_hat) + ADAM_EPS) + ADAM_WD * w_ref[...])
            nm_ref[...] = m
            nv_ref[...] = v

    outs = pl.pallas_call(
        body, name=name, grid=(N_STEPS,), in_specs=specs * 3 + g_specs, out_specs=specs * 4,
        out_shape=[jax.ShapeDtypeStruct(a.shape, f32) for a in ws] * 4,
        compiler_params=_params("parallel"))(*ws, *ms, *vs, *[g for g_list in gs for g in g_list])
    return [outs[k * n:(k + 1) * n] for k in range(4)]


def _place():
    return lax.axis_index("x"), lax.axis_index("y"), lax.axis_index("c")


def gather_shards(bufs, *, name, split):
    n = len(bufs)

    def body(*refs):
        bufs_ = refs[:n]
        isend, irecv, dsend, drecv = refs[2 * n:]
        x, y, c = _place()
        k = 2 * x + y
        peers = [(1 - x, y, c), (x, 1 - y, c), (1 - x, 1 - y, c)]
        peer_k = [2 * (1 - x) + y, 2 * x + (1 - y), 2 * (1 - x) + (1 - y)]

        def slab(a, q, h):
            if not split[a]:
                return bufs_[a].at[q]
            half = bufs_[a].shape[1] // 2
            return bufs_[a].at[q, pl.ds(pl.multiple_of(h * half, 16), half)]

        def ici(a, j, q):
            return pltpu.make_async_remote_copy(
                src_ref=slab(a, q, c), dst_ref=slab(a, q, c), send_sem=isend.at[3 * a + j], recv_sem=irecv.at[3 * a + j],
                device_id=peers[j], device_id_type=MESH)

        def d2d(a, j, h):
            return pltpu.make_async_remote_copy(
                src_ref=slab(a, peer_k[j], h), dst_ref=slab(a, peer_k[j], h), send_sem=dsend.at[3 * a + j],
                recv_sem=drecv.at[3 * a + j], device_id=(x, y, 1 - c), device_id_type=MESH)

        for a in range(n):
            for j in range(3):
                ici(a, j, k).start()
        for a in range(n):
            for j in range(3):
                ici(a, j, peer_k[j]).wait_recv()
                if split[a]:
                    d2d(a, j, c).start()
        for a in range(n):
            for j in range(3):
                if split[a]:
                    d2d(a, j, 1 - c).wait_recv()
        for a in range(n):
            for j in range(3):
                ici(a, j, k).wait_send()
                if split[a]:
                    d2d(a, j, c).wait_send()

    return pl.pallas_call(
        body, name=name, in_specs=[ANY] * n, out_specs=[ANY] * n,
        out_shape=[jax.ShapeDtypeStruct(b.shape, b.dtype) for b in bufs],
        input_output_aliases={i: i for i in range(n)},
        scratch_shapes=[pltpu.SemaphoreType.DMA((3 * n,))] * 4)(*bufs)


HBM = pl.BlockSpec(memory_space=pltpu.HBM)
SEM = pl.BlockSpec(memory_space=pltpu.SEMAPHORE)
DATAFLOW = pltpu.SideEffectType.DATAFLOW_SIDE_EFFECTING


def _chip_peers():
    x, y, c = _place()
    return 2 * x + y, [(1 - x, y, c), (x, 1 - y, c), (1 - x, 1 - y, c)], [2 * (1 - x) + y, 2 * x + (1 - y), 2 * (1 - x) + (1 - y)]


def gather_start(bufs, groups, after, *, name):
    n = len(bufs)
    ng = len(groups)

    def body(*refs):
        ins = refs[:n]
        sends, recvs = refs[2 * n + 1:2 * n + 1 + ng], refs[2 * n + 1 + ng:2 * n + 1 + 2 * ng]
        token = refs[-1]
        k, peers, _ = _chip_peers()
        for gi, grp in enumerate(groups):
            for pos, a in enumerate(grp):
                for j in range(3):
                    pltpu.make_async_remote_copy(
                        src_ref=ins[a].at[k], dst_ref=ins[a].at[k], send_sem=sends[gi].at[3 * pos + j],
                        recv_sem=recvs[gi].at[3 * pos + j], device_id=peers[j], device_id_type=MESH).start()
        token[...] = jnp.zeros_like(token)

    sems = [pltpu.SemaphoreType.DMA((3 * len(grp),)) for grp in groups]
    outs = pl.pallas_call(
        body, name=name, in_specs=[HBM] * n + [ANY],
        out_specs=[HBM] * n + [SEM] * (2 * ng) + [pl.BlockSpec(memory_space=pltpu.VMEM)],
        out_shape=[pltpu.HBM(b.shape, b.dtype) for b in bufs] + sems + sems + [jax.ShapeDtypeStruct((8, LANES), f32)],
        input_output_aliases={i: i for i in range(n)},
        compiler_params=pltpu.CompilerParams(has_side_effects=DATAFLOW))(
            *[pltpu.with_memory_space_constraint(b, pltpu.HBM) for b in bufs], after)
    return outs[:n], outs[n:n + ng], outs[n + ng:n + 2 * ng], outs[-1]


def gather_wait(bufs, send_sems, recv_sems, after, *, name):
    n = len(bufs)

    def body(*refs):
        ins = refs[:n]
        send, recv = refs[n], refs[n + 1]
        k, peers, peer_k = _chip_peers()
        for a in range(n):
            for j in range(3):
                copy = pltpu.make_async_remote_copy(
                    src_ref=ins[a].at[k], dst_ref=ins[a].at[peer_k[j]], send_sem=send.at[3 * a + j],
                    recv_sem=recv.at[3 * a + j], device_id=peers[j], device_id_type=MESH)
                copy.wait_send()
                copy.wait_recv()

    return pl.pallas_call(
        body, name=name, in_specs=[HBM] * n + [SEM, SEM, ANY], out_specs=[HBM] * n,
        out_shape=[pltpu.HBM(b.shape, b.dtype) for b in bufs],
        input_output_aliases={i: i for i in range(n)},
        compiler_params=pltpu.CompilerParams(has_side_effects=DATAFLOW))(*bufs, send_sems, recv_sems, after)


def sibling_exchange(arrays, *, name):
    n = len(arrays)

    def body(*refs):
        ins, outs = refs[:n], refs[n:2 * n]
        send, recv = refs[2 * n:]
        x, y, c = _place()

        def copy(a):
            return pltpu.make_async_remote_copy(
                src_ref=ins[a], dst_ref=outs[a], send_sem=send.at[a], recv_sem=recv.at[a],
                device_id=(x, y, 1 - c), device_id_type=MESH)

        for a in range(n):
            copy(a).start()
        for a in range(n):
            copy(a).wait_recv()
        for a in range(n):
            copy(a).wait_send()

    return pl.pallas_call(
        body, name=name, in_specs=[ANY] * n, out_specs=[ANY] * n,
        out_shape=[jax.ShapeDtypeStruct(a.shape, a.dtype) for a in arrays],
        scratch_shapes=[pltpu.SemaphoreType.DMA((n,)), pltpu.SemaphoreType.DMA((n,))])(*arrays)


ALL_MASKS = [(mx, my, mc) for mx in (0, 1) for my in (0, 1) for mc in (0, 1)][1:]


def _scatter_copies(srcs, lands, ev, send, recv, esend, erecv):
    x, y, c = _place()
    me = 4 * x + 2 * y + c
    k, peers, peer_k = _chip_peers()
    out = []
    for a in range(len(srcs)):
        for j in range(3):
            out.append(pltpu.make_async_remote_copy(
                src_ref=srcs[a].at[peer_k[j]], dst_ref=lands[a].at[j], send_sem=send.at[3 * a + j],
                recv_sem=recv.at[3 * a + j], device_id=peers[j], device_id_type=MESH))
    start_ev, wait_ev = [], []
    if ev is not None:
        for j, (mx, my, mc) in enumerate(ALL_MASKS):
            peer = (x ^ mx, y ^ my, c ^ mc)
            start_ev.append(pltpu.make_async_remote_copy(
                src_ref=ev.at[me], dst_ref=ev.at[me], send_sem=esend.at[j], recv_sem=erecv.at[j],
                device_id=peer, device_id_type=MESH))
            wait_ev.append(pltpu.make_async_remote_copy(
                src_ref=ev.at[me], dst_ref=ev.at[me ^ (4 * mx + 2 * my + mc)], send_sem=esend.at[j],
                recv_sem=erecv.at[j], device_id=peer, device_id_type=MESH))
    return out, start_ev, wait_ev


def chip_scatter_start(arrays, everyone, after, *, name):
    n = len(arrays)
    ne = 0 if everyone is None else 1
    lands = [pltpu.with_memory_space_constraint(lax.empty((3,) + a.shape[1:], a.dtype), pltpu.HBM) for a in arrays]

    def body(*refs):
        srcs, lands_ = refs[:n], refs[n:2 * n]
        ev = refs[2 * n] if ne else None
        sems = refs[2 * n + ne + 1 + 2 * n + ne:]
        send, recv = sems[0], sems[1]
        esend, erecv = (sems[2], sems[3]) if ne else (None, None)
        copies, start_ev, _ = _scatter_copies(srcs, lands_, ev, send, recv, esend, erecv)
        for cp in start_ev + copies:
            cp.start()

    sem_shapes = [pltpu.SemaphoreType.DMA((3 * n,))] * 2 + [pltpu.SemaphoreType.DMA((7,))] * (2 * ne)
    bufs = list(arrays) + lands + ([everyone] if ne else [])
    outs = pl.pallas_call(
        body, name=name, in_specs=[HBM] * len(bufs) + [ANY],
        out_specs=[HBM] * len(bufs) + [SEM] * len(sem_shapes),
        out_shape=[pltpu.HBM(b.shape, b.dtype) for b in bufs] + sem_shapes,
        input_output_aliases={i: i for i in range(len(bufs))},
        compiler_params=pltpu.CompilerParams(has_side_effects=DATAFLOW))(
            *[pltpu.with_memory_space_constraint(b, pltpu.HBM) for b in bufs], after)
    return n, ne, outs


def chip_scatter_wait(state, after, *, name):
    n, ne, held = state
    nb = 2 * n + ne
    bufs, sems = held[:nb], held[nb:]

    def body(*refs):
        srcs, lands_ = refs[:n], refs[n:2 * n]
        ev = refs[2 * n] if ne else None
        sems_ = refs[nb:nb + len(sems)]
        esend, erecv = (sems_[2], sems_[3]) if ne else (None, None)
        copies, _, wait_ev = _scatter_copies(srcs, lands_, ev, sems_[0], sems_[1], esend, erecv)
        for cp in wait_ev + copies:
            cp.wait_send()
            cp.wait_recv()

    outs = pl.pallas_call(
        body, name=name, in_specs=[HBM] * nb + [SEM] * len(sems) + [ANY], out_specs=[HBM] * nb,
        out_shape=[pltpu.HBM(b.shape, b.dtype) for b in bufs],
        input_output_aliases={i: i for i in range(nb)},
        compiler_params=pltpu.CompilerParams(has_side_effects=DATAFLOW))(*bufs, *sems, after)
    return outs[n:2 * n], (outs[2 * n] if ne else None)


def sibling_merge(bufs, *, name):
    n = len(bufs)

    def body(*refs):
        bufs_ = refs[:n]
        send, recv = refs[2 * n:]
        x, y, c = _place()

        def copy(u, h):
            return pltpu.make_async_remote_copy(
                src_ref=bufs_[u].at[h], dst_ref=bufs_[u].at[h], send_sem=send.at[u], recv_sem=recv.at[u],
                device_id=(x, y, 1 - c), device_id_type=MESH)

        for u in range(n):
            copy(u, c).start()
        for u in range(n):
            copy(u, 1 - c).wait_recv()
        for u in range(n):
            copy(u, c).wait_send()

    return pl.pallas_call(
        body, name=name, in_specs=[ANY] * n, out_specs=[ANY] * n,
        out_shape=[jax.ShapeDtypeStruct(b.shape, b.dtype) for b in bufs],
        input_output_aliases={i: i for i in range(n)},
        scratch_shapes=[pltpu.SemaphoreType.DMA((n,)), pltpu.SemaphoreType.DMA((n,))])(*bufs)


def sum_leading(a, *, name):
    n, r, c = a.shape

    def body(a_ref, o_ref):
        acc = a_ref[0]
        for i in range(1, n):
            acc = acc + a_ref[i]
        o_ref[...] = acc

    rb = r // 2 if r % 16 == 0 else r
    return pl.pallas_call(
        body, name=name, grid=(r // rb,), in_specs=[pl.BlockSpec((n, rb, c), lambda i: (0, i, 0))],
        out_specs=pl.BlockSpec((rb, c), lambda i: (i, 0)), out_shape=jax.ShapeDtypeStruct((r, c), f32),
        compiler_params=_params("parallel"))(a)


def _half_rows(shape):
    return shape[1] // 2 // 2


def rs_cast_other_half(grads, c_arr, *, name):
    n = len(grads)

    def body(c_ref, *refs):
        for i_ref, o_ref in zip(refs[:n], refs[n:]):
            o_ref[...] = i_ref[...].astype(bf16)

    in_specs = [pl.BlockSpec((1, _half_rows(g.shape), g.shape[2]), lambda s, r, c_ref: (s, (1 - c_ref[0]) * 2 + r, 0))
                for g in grads]
    out_specs = [pl.BlockSpec((1, _half_rows(g.shape), g.shape[2]), lambda s, r, c_ref: (s, r, 0)) for g in grads]
    return pl.pallas_call(
        body, name=name,
        grid_spec=pltpu.PrefetchScalarGridSpec(num_scalar_prefetch=1, grid=(N_SHARDS, 2),
                                               in_specs=in_specs, out_specs=out_specs),
        out_shape=[jax.ShapeDtypeStruct((N_SHARDS, g.shape[1] // 2, g.shape[2]), bf16) for g in grads],
        compiler_params=_params("parallel", "parallel"))(c_arr, *grads)


def rs_add_sibling(grads, recvd, ck_arr, *, name):
    n = len(grads)

    def body(ck_ref, *refs):
        s = pl.program_id(1)
        for u in range(n):
            g_ref, r_ref = refs[u], refs[n + u]
            qb_ref, own_ref = refs[2 * n + u], refs[3 * n + u]
            q = g_ref[0] + r_ref[0].astype(f32)
            qb_ref[0] = q.astype(bf16)

            @pl.when(s == ck_ref[1])
            def _(own_ref=own_ref, q=q):
                own_ref[...] = q

    in_specs = [pl.BlockSpec((1, _half_rows(g.shape), g.shape[2]), lambda r, s, ck: (s, ck[0] * 2 + r, 0)) for g in grads]
    in_specs += [pl.BlockSpec((1, _half_rows(g.shape), g.shape[2]), lambda r, s, ck: (s, r, 0)) for g in grads]
    out_specs = [pl.BlockSpec((1, _half_rows(g.shape), g.shape[2]), lambda r, s, ck: (s, r, 0)) for g in grads]
    out_specs += [pl.BlockSpec((_half_rows(g.shape), g.shape[2]), lambda r, s, ck: (r, 0)) for g in grads]
    outs = pl.pallas_call(
        body, name=name,
        grid_spec=pltpu.PrefetchScalarGridSpec(num_scalar_prefetch=1, grid=(2, N_SHARDS),
                                               in_specs=in_specs, out_specs=out_specs),
        out_shape=[jax.ShapeDtypeStruct((N_SHARDS, g.shape[1] // 2, g.shape[2]), bf16) for g in grads]
        + [jax.ShapeDtypeStruct((g.shape[1] // 2, g.shape[2]), f32) for g in grads],
        compiler_params=_params("parallel", "arbitrary"))(ck_arr, *grads, *recvd)
    return outs[:n], outs[n:]


def rs_sum_chips(owns, recvd, ck_arr, *, name):
    n = len(owns)

    def body(ck_ref, *refs):
        for u in range(n):
            own_ref, r_ref, o_ref = refs[u], refs[n + u], refs[2 * n + u]
            o_ref[0] = ((own_ref[...] + r_ref[0].astype(f32)) + r_ref[1].astype(f32)) + r_ref[2].astype(f32)

    in_specs = [pl.BlockSpec((o.shape[0] // 2, o.shape[1]), lambda r, ck: (r, 0)) for o in owns]
    in_specs += [pl.BlockSpec((3, o.shape[0] // 2, o.shape[1]), lambda r, ck: (0, r, 0)) for o in owns]
    out_specs = [pl.BlockSpec((1, o.shape[0] // 2, o.shape[1]), lambda r, ck: (ck[0], r, 0)) for o in owns]
    return pl.pallas_call(
        body, name=name,
        grid_spec=pltpu.PrefetchScalarGridSpec(num_scalar_prefetch=1, grid=(2,), in_specs=in_specs, out_specs=out_specs),
        out_shape=[jax.ShapeDtypeStruct((2,) + o.shape, f32) for o in owns],
        compiler_params=_params("parallel"))(ck_arr, *owns, *recvd)


SMALL = ("a_norm", "a_v_norm", "a_w_s", "a_b_s", "f_norm", "f_conv_w", "f_conv_b", "kv_norm", "k_norm",
         "b_norm", "b_q_norm", "b_sinks")
BIG = ("a_w_in", "a_w_out", "f_w_in", "f_w_out", "w_kv", "b_w_q", "b_w_o")
PACK_COLS = 1024
PACK_ROWS = 8 * N_STEPS


def _pack(parts, rows=PACK_ROWS):
    flat = jnp.concatenate([p.reshape(-1).astype(f32) for p in parts])
    pad = (-flat.shape[0]) % (rows * PACK_COLS)
    return jnp.pad(flat, (0, pad)).reshape(-1, PACK_COLS)


def _unpack(packed, shapes):
    flat = packed.reshape(-1)
    out, off = [], 0
    for s in shapes:
        size = math.prod(s)
        out.append(flat[off:off + size].reshape(s))
        off += size
    return out


def _ffn_fwd(x, g, h, r, w_in4, conv_w, conv_b, f, tag):
    wg, wu = conv_w[:, :f], conv_w[:, f:]
    bg, bu = conv_b[None, :f], conv_b[None, f:]
    pg, pu, a = ffn_in_fused(h, w_in4, wg, wu, bg, bu, name=f"ffn{tag}_in")
    return a, (x, g, h, r, pg, pu, a, wg, wu, bg, bu)


def _ffn_bwd(dy, saved, w_in4, w_out, tag):
    x, g, h, r, pg, pu, a, wg, wu, bg, bu = saved
    f = w_out.shape[0]
    da = mm_nt([dy], w_out[None], name=f"ffn{tag}_dact", tko=f // 2)
    d_w_out = mm_tn(a, [dy], name=f"ffn{tag}_dwout", n_s=w_out.shape[1], tki=f // 2)
    dpg, dpu, sg, su = ffn_gate_bwd(pg, pu, da, wg, wu, bg, bu, name=f"ffn{tag}_dgate")
    d_w_in = mm_tn(h, [dpg, dpu], name=f"ffn{tag}_dwin", n_s=w_in4.shape[2])
    dh = mm_nt([dpg, dpu], w_in4, name=f"ffn{tag}_dh")
    dx, (dg,) = rms_bwd([dh], x, r, [g], dy, name=f"ffn{tag}_dnorm")
    d_conv_w = jnp.concatenate([sg[0:3], su[0:3]], axis=1)
    d_conv_b = jnp.concatenate([sg[3], su[3]], axis=0)
    return dx, dg, d_w_in, d_conv_w, d_conv_b, d_w_out


def _rs_front(units, c_arr, tag):
    other_bf = rs_cast_other_half(units, c_arr, name=f"rs_cast{tag}")
    from_sib = sibling_exchange(list(other_bf), name=f"rs_sibling{tag}")
    return rs_add_sibling(units, from_sib, c_arr, name=f"rs_add{tag}")


def _rs_back(own, from_chips, c_arr, tag):
    halves = rs_sum_chips(list(own), list(from_chips), c_arr, name=f"rs_sum{tag}")
    return [m.reshape(-1, m.shape[2]) for m in sibling_merge(list(halves), name=f"rs_merge{tag}")]


def kernel(x, a_norm, a_w_in, a_v_norm, a_w_s, a_b_s, a_w_out, f_norm, f_w_in, f_conv_w, f_conv_b, f_w_out, kv_norm, w_kv, k_norm, b_norm, b_w_q, b_q_norm, b_sinks, b_w_o, loss_target, m_a_norm, m_a_w_in, m_a_v_norm, m_a_w_s, m_a_b_s, m_a_w_out, m_f_norm, m_f_w_in, m_f_conv_w, m_f_conv_b, m_f_w_out, m_kv_norm, m_w_kv, m_k_norm, m_b_norm, m_b_w_q, m_b_q_norm, m_b_sinks, m_b_w_o, v_a_norm, v_a_w_in, v_a_v_norm, v_a_w_s, v_a_b_s, v_a_w_out, v_f_norm, v_f_w_in, v_f_conv_w, v_f_conv_b, v_f_w_out, v_kv_norm, v_w_kv, v_k_norm, v_b_norm, v_b_w_q, v_b_q_norm, v_b_sinks, v_b_w_o):
    args = dict(locals())
    weights = {n: args[n] for n in SMALL + BIG}
    moms = {n: args["m_" + n] for n in SMALL + BIG}
    vars_ = {n: args["v_" + n] for n in SMALL + BIG}
    t, d = x.shape[1], x.shape[2]
    xi, yi, ci = _place()
    chip = 2 * xi + yi

    big_local = [a_w_in[0], a_w_out[0], f_w_in, f_w_out, w_kv, b_w_q[0], b_w_o[0]]
    c_arr = jnp.stack([ci, chip]).astype(jnp.int32)
    k_arr = jnp.stack([chip]).astype(jnp.int32)
    b_ain, b_aout, b_fin0, b_fin1, b_fout0, b_fout1, b_kv, b_q, b_o = cast_into_slot(big_local, k_arr, name="cast_weights")
    small_cols = _pack([a_norm, a_v_norm, f_conv_w], rows=8)
    b_small = lax.dynamic_update_slice(jnp.zeros((N_SHARDS,) + small_cols.shape, f32), small_cols[None], (chip, 0, 0))
    g_small, g_a_w_in, g_a_w_out = gather_shards([b_small, b_ain, b_aout], name="gather_first", split=[False, True, True])
    later, send_sems, recv_sems, token = gather_start([b_fin0, b_fout0, b_kv, b_q, b_o, b_fin1, b_fout1],
                                                      [[0], [1, 2, 3, 4], [5, 6]], g_small, name="gather_start")
    ns_cols = a_norm.shape[1]
    nf_cols = f_conv_w.shape[2]
    parts = [_unpack(g_small[k], [a_norm.shape, a_v_norm.shape, f_conv_w.shape]) for k in range(N_SHARDS)]
    a_norm_f = jnp.concatenate([p[0] for p in parts], axis=1) + token[0, 0]
    a_v_norm_f = jnp.concatenate([p[1] for p in parts], axis=1)
    conv_w_f = jnp.concatenate([p[2] for p in parts], axis=2)
    w_a_in = g_a_w_in
    w_a_out = g_a_w_out.reshape(1, -1, d)

    x0 = x[0]
    tril = jnp.tril(jnp.ones((CHUNK, CHUNK), dtype=bool))
    wc = jnp.where(tril[None], a_w_s[0], 0.0).astype(bf16)
    bt = a_b_s[0].T
    kg2 = jnp.tile(k_norm, 2)[None]
    qg2 = jnp.tile(b_q_norm[0], 2)[None]

    (h_a,), r_a = rms_fwd(x0, [a_norm_f], name="a_norm")
    zu = mm_nn(h_a, w_a_in, name="a_in_u", s0=0, ns=2)
    zv = mm_nn(h_a, w_a_in, name="a_in_v", s0=2, ns=2)
    y_a = sgu_gate_fwd(zu, zv, a_v_norm_f, wc, bt, name="a_gate")
    f = f_w_out.shape[1] * N_SHARDS
    x1, (h_f0,), r_f0 = mm_residual(y_a, w_a_out[0], x0, name="a_out", gains=[f_norm[0:1]])
    (g_fin0,) = gather_wait(later[0:1], send_sems[0], recv_sems[0], x1, name="gather_wait_0")
    w_f_in = [g_fin0, None]
    a0, ffn0 = _ffn_fwd(x1, f_norm[0:1], h_f0, r_f0, w_f_in[0], conv_w_f[0], f_conv_b[0], f, "0")
    g_fout0, g_w_kv, g_b_w_q, g_b_w_o = gather_wait(later[1:5], send_sems[1], recv_sems[1], a0, name="gather_wait_1")
    w_f_out = [g_fout0.reshape(-1, d), None]
    w_kv_f = g_w_kv.reshape(1, d, -1)
    w_q_f = g_b_w_q.reshape(1, d, -1)
    w_o_f = g_b_w_o.reshape(1, -1, d)
    x2, (h_k, h_q), r_b = mm_residual(a0, w_f_out[0], x1, name="ffn0_out", gains=[kv_norm[None], b_norm])
    kv = mm_nn(h_k, w_kv_f, name="kv_proj")
    k2, v2 = kv_post_fwd(kv, kg2, name="kv_post")
    qp = mm_nn(h_q, w_q_f, name="q_proj")
    qn = q_norm_fwd(qp, qg2, name="q_norm", scale=HEAD_DIM ** -0.5)
    o = attn_fwd(qn, k2, v2, b_sinks[0], name="attn")
    x3, (h_f1,), r_f1 = mm_residual(o, w_o_f[0], x2, name="o_proj", gains=[f_norm[1:2]])
    g_fin1, g_fout1 = gather_wait(later[5:7], send_sems[2], recv_sems[2], x3, name="gather_wait_2")
    w_f_in[1] = g_fin1
    w_f_out[1] = g_fout1.reshape(-1, d)
    a1, ffn1 = _ffn_fwd(x3, f_norm[1:2], h_f1, r_f1, w_f_in[1], conv_w_f[1], f_conv_b[1], f, "1")
    dx4, sq = mm_residual(a1, w_f_out[1], x3, name="ffn1_out", target=loss_target[0])
    loss_part = (0.5 * jnp.sum(sq) / d).reshape(1)

    dx3, d_fn1, d_fwin1, d_cw1, d_cb1, d_fwout1 = _ffn_bwd(dx4, ffn1, w_f_in[1], w_f_out[1], "1")
    do = mm_nt([dx3], w_o_f, name="o_proj_dx")
    d_w_o = mm_tn(o, [dx3], name="o_proj_dw", n_s=d)
    dqn, dk2, dv2, dsink = attn_bwd(qn, k2, v2, do, b_sinks[0], name="attn_bwd")
    dqp, dqg = q_norm_bwd(dqn, qp, qg2, name="q_norm_bwd", scale=HEAD_DIM ** -0.5)
    dkv, dkg = kv_post_bwd(dk2, dv2, kv, kg2, name="kv_post_bwd")
    d_w_q = mm_tn(h_q, [dqp], name="q_proj_dw", n_s=w_q_f.shape[2])
    dh_q = mm_nt([dqp], w_q_f, name="q_proj_dx")
    d_w_kv = mm_tn(h_k, [dkv], name="kv_proj_dw", n_s=w_kv_f.shape[2])
    dh_k = mm_nt([dkv], w_kv_f, name="kv_proj_dx")
    dx2, (d_kvn, d_bn) = rms_bwd([dh_k, dh_q], x2, r_b, [kv_norm[None], b_norm], dx3, name="b_norm_bwd")
    sh = N_SHARDS
    units1 = [d_fwin1, d_fwout1.reshape(sh, -1, d), d_w_kv.reshape(sh, -1, d_w_kv.shape[2]),
              d_w_q.reshape(sh, -1, d_w_q.shape[2]), d_w_o.reshape(sh, -1, d)]
    chip_bf1, own1 = _rs_front(units1, c_arr, "1")
    scatter1 = chip_scatter_start(list(chip_bf1), None, dx2, name="rs_chips_start1")
    dx1, d_fn0, d_fwin0, d_cw0, d_cb0, d_fwout0 = _ffn_bwd(dx2, ffn0, w_f_in[0], w_f_out[0], "0")
    dy_a = mm_nt([dx1], w_a_out, name="a_out_dx")
    d_w_aout = mm_tn(y_a, [dx1], name="a_out_dw", n_s=d)
    dzu, dzv, d_avn, d_ws, d_bt = sgu_gate_bwd(zu, zv, dy_a, a_v_norm_f, wc, bt, name="a_gate_bwd")
    d_w_ain = mm_tn(h_a, [dzu, dzv], name="a_in_dw", n_s=w_a_in.shape[2])
    dh_a = mm_nt([dzu, dzv], w_a_in, name="a_in_dx")
    dx0, (d_an,) = rms_bwd([dh_a], x0, r_a, [a_norm_f], dx1, name="a_norm_bwd")
    grad_x = dx0[None]

    units2 = [d_w_ain, d_w_aout.reshape(sh, -1, d), d_fwin0, d_fwout0.reshape(sh, -1, d)]
    chip_bf2, own2 = _rs_front(units2, c_arr, "2")
    d_fn = jnp.concatenate([d_fn0, d_fn1], axis=0)
    d_cw = jnp.stack([d_cw0, d_cw1])
    d_cb = jnp.stack([d_cb0, d_cb1])
    d_kg = (dkg[0, :HEAD_DIM] + dkg[0, HEAD_DIM:])
    d_qg = (dqg[0, :HEAD_DIM] + dqg[0, HEAD_DIM:])[None]
    small_full = [d_an, d_avn, d_ws[None], d_bt.T[None], d_fn, d_cw, d_cb, d_kvn[0], d_kg, d_bn, d_qg,
                  dsink[:, :N_Q_HEADS], loss_part]
    packed = _pack(small_full)
    me = 4 * xi + 2 * yi + ci
    everyone = lax.dynamic_update_slice(lax.empty((N_DEV,) + packed.shape, f32), packed[None], (me, 0, 0))
    scatter2 = chip_scatter_start(list(chip_bf2), everyone, chip_bf2[0], name="rs_chips_start2")
    from_chips1, _ = chip_scatter_wait(scatter1, scatter2[2][0], name="rs_chips_wait1")
    fin1, fout1, gkv, gq, go = _rs_back(own1, from_chips1, c_arr, "1")
    from_chips2, from_all = chip_scatter_wait(scatter2, fin1, name="rs_chips_wait2")
    ain, aout, fin0, fout0 = _rs_back(own2, from_chips2, c_arr, "2")
    big_w = [weights[n] for n in BIG]
    big_gs = [[ain], [aout], [fin0, fin1], [fout0, fout1], [gkv], [gq], [go]]
    big_g, big_d, big_m, big_v = adamw(big_w, big_gs, [moms[n] for n in BIG], [vars_[n] for n in BIG], name="adamw_big")

    full_shapes = [g.shape for g in small_full]
    small_g = _unpack(sum_leading(from_all, name="small_sum"), full_shapes)
    loss = small_g.pop()[0]
    small_g[0] = lax.dynamic_slice_in_dim(small_g[0], chip * ns_cols, ns_cols, axis=1)
    small_g[1] = lax.dynamic_slice_in_dim(small_g[1], chip * ns_cols, ns_cols, axis=1)
    small_g[5] = lax.dynamic_slice_in_dim(small_g[5], chip * nf_cols, nf_cols, axis=2)
    small_shapes = [weights[n].shape for n in SMALL]
    small_g = [g.reshape(s) for g, s in zip(small_g, small_shapes)]
    pw, pg_, pm, pv = (_pack(v) for v in ([weights[n] for n in SMALL], small_g, [moms[n] for n in SMALL],
                                          [vars_[n] for n in SMALL]))
    _, (sd,), (sm,), (sv,) = adamw([pw], [[pg_]], [pm], [pv], name="adamw_small")
    small_d, small_m, small_v = (_unpack(v, small_shapes) for v in (sd, sm, sv))

    out = {}
    for i, n in enumerate(SMALL):
        out[n] = (small_g[i], small_d[i], small_m[i], small_v[i])
    for i, n in enumerate(BIG):
        out[n] = (big_g[i], big_d[i], big_m[i], big_v[i])
    order = ["a_norm", "a_w_in", "a_v_norm", "a_w_s", "a_b_s", "a_w_out", "f_norm", "f_w_in", "f_conv_w", "f_conv_b",
             "f_w_out", "kv_norm", "w_kv", "k_norm", "b_norm", "b_w_q", "b_q_norm", "b_sinks", "b_w_o"]
    return (loss, grad_x, *[out[n][0] for n in order], *[out[n][1] for n in order],
            *[out[n][2] for n in order], *[out[n][3] for n in order])
```

```python
import functools
import math

import jax
import jax.numpy as jnp
from jax import lax
from jax.experimental import pallas as pl
from jax.experimental.pallas import tpu as pltpu

f32 = jnp.float32
bf16 = jnp.bfloat16
MESH = pl.DeviceIdType.MESH
ANY = pl.BlockSpec(memory_space=pl.ANY)

EPS = 1e-6
LANES = 128
CHUNK = 128
HEAD_DIM = 64
N_Q_HEADS = 16
N_KV_HEADS = 4
Q_PER_KV = N_Q_HEADS // N_KV_HEADS
N_SHARDS = 4
N_DEV = 8

ADAM_LR = 0.001
ADAM_B1 = 0.9
ADAM_B2 = 0.999
ADAM_EPS = 1e-08
ADAM_WD = 0.01
ADAM_STEP = 10
ADAM_C1 = 1.0 - ADAM_B1 ** ADAM_STEP
ADAM_C2 = 1.0 - ADAM_B2 ** ADAM_STEP

_INV_SQRT2 = 1.0 / math.sqrt(2.0)
_INV_SQRT2PI = 1.0 / math.sqrt(2.0 * math.pi)


def _params(*sem):
    return pltpu.CompilerParams(dimension_semantics=sem)


def _gelu(z):
    return 0.5 * z * (1.0 + lax.erf(z * _INV_SQRT2))


def _gelu_grad(z):
    return 0.5 * (1.0 + lax.erf(z * _INV_SQRT2)) + z * jnp.exp(-0.5 * z * z) * _INV_SQRT2PI


def _dot(a, b):
    return jnp.dot(a, b, preferred_element_type=f32)


def _dot_nt(a, b):
    return lax.dot_general(a, b, (((1,), (1,)), ((), ())), preferred_element_type=f32)


def _dot_tn(a, b):
    return lax.dot_general(a, b, (((0,), (0,)), ((), ())), preferred_element_type=f32)


def _dot_exact(a, b):
    return jnp.dot(a, b, preferred_element_type=f32, precision=lax.Precision.HIGHEST)


VMEM_TILE_BUDGET = 36 * 1024 * 1024
MAX_ROW_TILE = 2048


def _row_tile(m, fixed_bytes, row_bytes):
    tm = min(m, MAX_ROW_TILE)
    while tm > 256 and 2 * (fixed_bytes + tm * row_bytes) > VMEM_TILE_BUDGET:
        tm //= 2
    return tm


def _isz(a):
    return jnp.dtype(a.dtype).itemsize


def mm_nn(a, w3, *, name, s0=0, ns=None, add=None, out_dtype=f32):
    m, k = a.shape
    s_all, _, n_s = w3.shape
    ns = s_all if ns is None else ns
    tm = _row_tile(m, k * n_s * 2, k * _isz(a) + n_s * jnp.dtype(out_dtype).itemsize + (0 if add is None else n_s * 4))

    def body(*refs):
        if add is None:
            a_ref, w_ref, o_ref = refs
            acc = _dot(a_ref[...].astype(bf16), w_ref[0])
        else:
            a_ref, w_ref, add_ref, o_ref = refs
            acc = _dot(a_ref[...].astype(bf16), w_ref[0]) + add_ref[...]
        o_ref[...] = acc.astype(out_dtype)

    in_specs = [pl.BlockSpec((tm, k), lambda j, i: (i, 0)),
                pl.BlockSpec((1, k, n_s), lambda j, i: (s0 + j, 0, 0))]
    args = [a, w3]
    if add is not None:
        in_specs.append(pl.BlockSpec((tm, n_s), lambda j, i: (i, j)))
        args.append(add)
    return pl.pallas_call(
        body, name=name, grid=(ns, m // tm), in_specs=in_specs,
        out_specs=pl.BlockSpec((tm, n_s), lambda j, i: (i, j)),
        out_shape=jax.ShapeDtypeStruct((m, ns * n_s), out_dtype),
        compiler_params=_params("parallel", "parallel"))(*args)


def mm_nt(a_list, w3, *, name, tko=None, add=None, out_dtype=f32):
    s_all, k_out, n_s = w3.shape
    m = a_list[0].shape[0]
    na = len(a_list)
    spa = s_all // na
    tko = k_out if tko is None else tko
    tm = _row_tile(m, tko * n_s * 2, na * n_s * _isz(a_list[0]) + tko * 4 * (1 if add is None else 2))

    def body(*refs):
        a_refs = refs[:na]
        w_ref = refs[na]
        o_ref = refs[-1]
        s = pl.program_id(2)

        @pl.when(s == 0)
        def _():
            if add is None:
                o_ref[...] = jnp.zeros_like(o_ref)
            else:
                o_ref[...] = refs[na + 1][...]

        for idx in range(na):
            @pl.when(s // spa == idx)
            def _(idx=idx):
                o_ref[...] += _dot_nt(a_refs[idx][...].astype(bf16), w_ref[0])

    def a_map(idx):
        return lambda ko, i, s: (i, jnp.clip(s - idx * spa, 0, spa - 1))

    in_specs = [pl.BlockSpec((tm, n_s), a_map(idx)) for idx in range(na)]
    in_specs.append(pl.BlockSpec((1, tko, n_s), lambda ko, i, s: (s, ko, 0)))
    args = list(a_list) + [w3]
    if add is not None:
        in_specs.append(pl.BlockSpec((tm, tko), lambda ko, i, s: (i, ko)))
        args.append(add)
    return pl.pallas_call(
        body, name=name, grid=(k_out // tko, m // tm, s_all), in_specs=in_specs,
        out_specs=pl.BlockSpec((tm, tko), lambda ko, i, s: (i, ko)),
        out_shape=jax.ShapeDtypeStruct((m, k_out), out_dtype),
        compiler_params=_params("parallel", "parallel", "arbitrary"))(*args)


def mm_tn(a, b_list, *, name, n_s, tki=None):
    m, k_in = a.shape
    na = len(b_list)
    s_all = sum(b.shape[1] for b in b_list) // n_s
    spa = s_all // na
    tki = k_in if tki is None else tki
    tm = _row_tile(m, tki * n_s * 4, tki * _isz(a) + na * n_s * _isz(b_list[0]))

    def body(*refs):
        a_ref = refs[0]
        b_refs = refs[1:1 + na]
        o_ref = refs[-1]
        s = pl.program_id(0)
        r = pl.program_id(2)

        @pl.when(r == 0)
        def _():
            o_ref[...] = jnp.zeros_like(o_ref)

        for idx in range(na):
            @pl.when(s // spa == idx)
            def _(idx=idx):
                o_ref[0] += _dot_tn(a_ref[...].astype(bf16), b_refs[idx][...].astype(bf16))

    def b_map(idx):
        def index(s, ki, r):
            active = (s // spa) == idx
            return (jnp.where(active, r, 0), jnp.clip(s - idx * spa, 0, spa - 1))
        return index

    in_specs = [pl.BlockSpec((tm, tki), lambda s, ki, r: (r, ki))]
    in_specs += [pl.BlockSpec((tm, n_s), b_map(idx)) for idx in range(na)]
    return pl.pallas_call(
        body, name=name, grid=(s_all, k_in // tki, m // tm), in_specs=in_specs,
        out_specs=pl.BlockSpec((1, tki, n_s), lambda s, ki, r: (s, ki, 0)),
        out_shape=jax.ShapeDtypeStruct((s_all, k_in, n_s), f32),
        compiler_params=_params("parallel", "parallel", "arbitrary"))(a, *b_list)


def mm_residual(a, w, x, *, name, gains=(), target=None):
    m, k = a.shape
    d = w.shape[1]
    ng = len(gains)
    tm = _row_tile(m, k * d * 2, k * _isz(a) + d * 4 * 3 + ng * d * 2)

    def body(*refs):
        a_ref, w_ref, x_ref = refs[:3]
        y = _dot(a_ref[...].astype(bf16), w_ref[...]) + x_ref[...]
        if target is None:
            g_refs = refs[3:3 + ng]
            y_ref = refs[3 + ng]
            h_refs = refs[4 + ng:4 + 2 * ng]
            r_ref = refs[-1]
            y_ref[...] = y
            r = lax.rsqrt(jnp.mean(y * y, axis=1, keepdims=True) + EPS)
            yh = y * r
            for g_ref, h_ref in zip(g_refs, h_refs):
                h_ref[...] = (yh * g_ref[...]).astype(bf16)
            r_ref[...] = r
        else:
            t_ref, dy_ref, s_ref = refs[3:]
            i = pl.program_id(0)
            e = y - t_ref[...]
            dy_ref[...] = e * (1.0 / d)
            part = jnp.sum(e * e, axis=0, keepdims=True)

            @pl.when(i == 0)
            def _():
                s_ref[...] = part

            @pl.when(i > 0)
            def _():
                s_ref[...] += part

    row = pl.BlockSpec((tm, d), lambda i: (i, 0))
    vec = pl.BlockSpec((1, d), lambda i: (0, 0))
    in_specs = [pl.BlockSpec((tm, k), lambda i: (i, 0)), pl.BlockSpec((k, d), lambda i: (0, 0)), row]
    if target is None:
        outs = pl.pallas_call(
            body, name=name, grid=(m // tm,), in_specs=in_specs + [vec] * ng,
            out_specs=[row] * (1 + ng) + [pl.BlockSpec((tm, 1), lambda i: (i, 0))],
            out_shape=[jax.ShapeDtypeStruct((m, d), f32)] + [jax.ShapeDtypeStruct((m, d), bf16)] * ng
            + [jax.ShapeDtypeStruct((m, 1), f32)],
            compiler_params=_params("parallel"))(a, w, x, *gains)
        return outs[0], outs[1:1 + ng], outs[-1]
    return pl.pallas_call(
        body, name=name, grid=(m // tm,), in_specs=in_specs + [row], out_specs=[row, vec],
        out_shape=[jax.ShapeDtypeStruct((m, d), f32), jax.ShapeDtypeStruct((1, d), f32)],
        compiler_params=_params("arbitrary"))(a, w, x, target)


def rms_fwd(x, gains, *, name, tr=512):
    t, d = x.shape
    tr = min(tr, t)
    ng = len(gains)

    def body(*refs):
        x_ref = refs[0]
        g_refs = refs[1:1 + ng]
        h_refs = refs[1 + ng:1 + 2 * ng]
        r_ref = refs[-1]
        xv = x_ref[...]
        r = lax.rsqrt(jnp.mean(xv * xv, axis=1, keepdims=True) + EPS)
        xh = xv * r
        for g_ref, h_ref in zip(g_refs, h_refs):
            h_ref[...] = (xh * g_ref[...]).astype(bf16)
        r_ref[...] = r

    row = pl.BlockSpec((tr, d), lambda i: (i, 0))
    vec = pl.BlockSpec((1, d), lambda i: (0, 0))
    outs = pl.pallas_call(
        body, name=name, grid=(t // tr,), in_specs=[row] + [vec] * ng,
        out_specs=[row] * ng + [pl.BlockSpec((tr, 1), lambda i: (i, 0))],
        out_shape=[jax.ShapeDtypeStruct((t, d), bf16)] * ng + [jax.ShapeDtypeStruct((t, 1), f32)],
        compiler_params=_params("parallel"))(x, *gains)
    return outs[:ng], outs[ng]


def rms_bwd(dh_list, x, r, gains, dx_in, *, name, tr=512):
    t, d = x.shape
    tr = min(tr, t)
    ng = len(gains)

    def body(*refs):
        dh_refs = refs[:ng]
        x_ref, r_ref = refs[ng], refs[ng + 1]
        g_refs = refs[ng + 2:2 * ng + 2]
        dxin_ref = refs[2 * ng + 2]
        dx_ref = refs[2 * ng + 3]
        dg_refs = refs[2 * ng + 4:]
        i = pl.program_id(0)
        rv = r_ref[...]
        xh = x_ref[...] * rv
        acc = dxin_ref[...]
        for dh_ref, g_ref, dg_ref in zip(dh_refs, g_refs, dg_refs):
            dh = dh_ref[...]
            part = jnp.sum(dh * xh, axis=0, keepdims=True)

            @pl.when(i == 0)
            def _(dg_ref=dg_ref, part=part):
                dg_ref[...] = part

            @pl.when(i > 0)
            def _(dg_ref=dg_ref, part=part):
                dg_ref[...] += part

            tg = dh * g_ref[...]
            acc = acc + rv * (tg - xh * jnp.mean(tg * xh, axis=1, keepdims=True))
        dx_ref[...] = acc

    row = pl.BlockSpec((tr, d), lambda i: (i, 0))
    vec = pl.BlockSpec((1, d), lambda i: (0, 0))
    outs = pl.pallas_call(
        body, name=name, grid=(t // tr,),
        in_specs=[row] * ng + [row, pl.BlockSpec((tr, 1), lambda i: (i, 0))] + [vec] * ng + [row],
        out_specs=[row] + [vec] * ng,
        out_shape=[jax.ShapeDtypeStruct((t, d), f32)] + [jax.ShapeDtypeStruct((1, d), f32)] * ng,
        compiler_params=_params("arbitrary"))(*dh_list, x, r, *gains, dx_in)
    return outs[0], outs[1:]


def sgu_gate_fwd(zu, zv, gv, wc, bt, *, name, tr=512):
    t, w = zu.shape
    tr = min(tr, t)
    groups = w // LANES

    def body(zu_ref, zv_ref, gv_ref, wc_ref, bt_ref, y_ref):
        vp = _gelu(zv_ref[...])
        rv = lax.rsqrt(jnp.mean(vp * vp, axis=1, keepdims=True) + EPS)
        vb = (vp * rv * gv_ref[...]).astype(bf16)
        for c in range(tr // CHUNK):
            rows = slice(c * CHUNK, (c + 1) * CHUNK)
            for g in range(groups):
                cols = slice(g * LANES, (g + 1) * LANES)
                sv = _dot(wc_ref[g], vb[rows, cols]) + bt_ref[:, g:g + 1]
                y_ref[rows, cols] = (_gelu(zu_ref[rows, cols]) * sv).astype(bf16)

    row = pl.BlockSpec((tr, w), lambda i: (i, 0))
    return pl.pallas_call(
        body, name=name, grid=(t // tr,),
        in_specs=[row, row, pl.BlockSpec((1, w), lambda i: (0, 0)),
                  pl.BlockSpec((groups, CHUNK, CHUNK), lambda i: (0, 0, 0)),
                  pl.BlockSpec((CHUNK, groups), lambda i: (0, 0))],
        out_specs=row, out_shape=jax.ShapeDtypeStruct((t, w), bf16),
        compiler_params=_params("parallel"))(zu, zv, gv, wc, bt)


def sgu_gate_bwd(zu, zv, dy, gv, wc, bt, *, name, tr=512):
    t, w = zu.shape
    tr = min(tr, t)
    groups = w // LANES
    nsteps = t // tr

    def body(zu_ref, zv_ref, dy_ref, gv_ref, wc_ref, bt_ref,
             dzu_ref, dzv_ref, dgv_ref, dws_ref, dbt_ref, dv_ref, bacc_ref):
        i = pl.program_id(0)

        @pl.when(i == 0)
        def _():
            dgv_ref[...] = jnp.zeros_like(dgv_ref)
            dws_ref[...] = jnp.zeros_like(dws_ref)
            bacc_ref[...] = jnp.zeros_like(bacc_ref)

        zvv = zv_ref[...]
        vp = _gelu(zvv)
        rv = lax.rsqrt(jnp.mean(vp * vp, axis=1, keepdims=True) + EPS)
        vhat = vp * rv
        vb = (vhat * gv_ref[...]).astype(bf16)
        for c in range(tr // CHUNK):
            rows = slice(c * CHUNK, (c + 1) * CHUNK)
            for g in range(groups):
                cols = slice(g * LANES, (g + 1) * LANES)
                vblk = vb[rows, cols]
                sv = _dot(wc_ref[g], vblk) + bt_ref[:, g:g + 1]
                zub = zu_ref[rows, cols]
                dyb = dy_ref[rows, cols]
                dzu_ref[rows, cols] = (dyb * sv * _gelu_grad(zub)).astype(bf16)
                dsv = dyb * _gelu(zub)
                bacc_ref[:, cols] += dsv
                dsvb = dsv.astype(bf16)
                dv_ref[rows, cols] = _dot_tn(wc_ref[g], dsvb)
                dws_ref[g] += _dot_nt(dsvb, vblk)
        dv = dv_ref[...]
        dgv_ref[...] += jnp.sum(dv * vhat, axis=0, keepdims=True)
        tg = dv * gv_ref[...]
        dvp = rv * (tg - vhat * jnp.mean(tg * vhat, axis=1, keepdims=True))
        dzv_ref[...] = (dvp * _gelu_grad(zvv)).astype(bf16)

        @pl.when(i == nsteps - 1)
        def _():
            tt = lax.broadcasted_iota(jnp.int32, (CHUNK, CHUNK), 0)
            ss = lax.broadcasted_iota(jnp.int32, (CHUNK, CHUNK), 1)
            for g in range(groups):
                dws_ref[g] = jnp.where(ss <= tt, dws_ref[g], 0.0)
                dbt_ref[:, g:g + 1] = jnp.sum(bacc_ref[:, g * LANES:(g + 1) * LANES], axis=1, keepdims=True)

    row = pl.BlockSpec((tr, w), lambda i: (i, 0))
    full3 = pl.BlockSpec((groups, CHUNK, CHUNK), lambda i: (0, 0, 0))
    return pl.pallas_call(
        body, name=name, grid=(nsteps,),
        in_specs=[row, row, row, pl.BlockSpec((1, w), lambda i: (0, 0)), full3,
                  pl.BlockSpec((CHUNK, groups), lambda i: (0, 0))],
        out_specs=[row, row, pl.BlockSpec((1, w), lambda i: (0, 0)), full3,
                   pl.BlockSpec((CHUNK, groups), lambda i: (0, 0))],
        out_shape=[jax.ShapeDtypeStruct((t, w), bf16), jax.ShapeDtypeStruct((t, w), bf16),
                   jax.ShapeDtypeStruct((1, w), f32), jax.ShapeDtypeStruct((groups, CHUNK, CHUNK), f32),
                   jax.ShapeDtypeStruct((CHUNK, groups), f32)],
        scratch_shapes=[pltpu.VMEM((tr, w), f32), pltpu.VMEM((CHUNK, w), f32)],
        compiler_params=_params("arbitrary"))(zu, zv, dy, gv, wc, bt)


HALO = 8


def _shift_down(v, halo, k, first):
    r = pltpu.roll(v, k, 0)
    hh = jnp.where(first, 0.0, pltpu.roll(halo, k, 0))
    rid = lax.broadcasted_iota(jnp.int32, (HALO, v.shape[1]), 0)
    head = jnp.where(rid < k, hh, r[0:HALO])
    if v.shape[0] == HALO:
        return head
    return jnp.concatenate([head, r[HALO:]], axis=0)


def _shift_up(v, halo, k, last):
    n = v.shape[0]
    r = pltpu.roll(v, n - k, 0)
    hh = jnp.where(last, 0.0, pltpu.roll(halo, HALO - k, 0))
    rid = lax.broadcasted_iota(jnp.int32, (HALO, v.shape[1]), 0)
    tail = jnp.where(rid >= HALO - k, hh, r[n - HALO:])
    return jnp.concatenate([r[:n - HALO], tail], axis=0)


def _conv(p, halo, w_ref, b_ref, first):
    return (w_ref[2:3, :] * p + w_ref[1:2, :] * _shift_down(p, halo, 1, first)
            + w_ref[0:1, :] * _shift_down(p, halo, 2, first) + b_ref[...])


def _conv_specs(t, tr, tc):
    tile = pl.BlockSpec((tr, tc), lambda j, i: (i, j))
    prev = pl.BlockSpec((HALO, tc), lambda j, i: (jnp.maximum(i * (tr // HALO) - 1, 0), j))
    nxt = pl.BlockSpec((HALO, tc), lambda j, i: (jnp.minimum((i + 1) * (tr // HALO), t // HALO - 1), j))
    wspec = pl.BlockSpec((3, tc), lambda j, i: (0, j))
    bspec = pl.BlockSpec((1, tc), lambda j, i: (0, j))
    return tile, prev, nxt, wspec, bspec


BF16_ROWS = 16


def ffn_in_fused(h, w_in4, wg, wu, bg, bu, *, name):
    t, k = h.shape
    s_all, _, n_s = w_in4.shape
    half = s_all // 2
    tm = _row_tile(t, 2 * k * n_s * 2, k * 2 + 2 * n_s * 4 + n_s * 2)

    def body(h_ref, hh_ref, wg_ref, wu_ref, cg_ref, cu_ref, bg_ref, bu_ref, pg_ref, pu_ref, a_ref):
        first = pl.program_id(1) == 0
        hv, hh = h_ref[...], hh_ref[...]
        outs = []
        for w_ref, c_ref, b_ref, p_ref in ((wg_ref, cg_ref, bg_ref, pg_ref), (wu_ref, cu_ref, bu_ref, pu_ref)):
            p = _dot(hv, w_ref[0])
            p_ref[...] = p
            outs.append(_conv(p, _dot(hh, w_ref[0])[BF16_ROWS - HALO:], c_ref, b_ref, first))
        gate, up = outs
        a_ref[...] = (gate * jax.nn.sigmoid(gate) * up).astype(bf16)

    tile = pl.BlockSpec((tm, n_s), lambda j, i: (i, j))
    cw = pl.BlockSpec((3, n_s), lambda j, i: (0, j))
    cb = pl.BlockSpec((1, n_s), lambda j, i: (0, j))
    f = half * n_s
    return pl.pallas_call(
        body, name=name, grid=(half, t // tm),
        in_specs=[pl.BlockSpec((tm, k), lambda j, i: (i, 0)),
                  pl.BlockSpec((BF16_ROWS, k), lambda j, i: (jnp.maximum(i * (tm // BF16_ROWS) - 1, 0), 0)),
                  pl.BlockSpec((1, k, n_s), lambda j, i: (j, 0, 0)),
                  pl.BlockSpec((1, k, n_s), lambda j, i: (j + half, 0, 0)), cw, cw, cb, cb],
        out_specs=[tile, tile, tile],
        out_shape=[jax.ShapeDtypeStruct((t, f), f32), jax.ShapeDtypeStruct((t, f), f32),
                   jax.ShapeDtypeStruct((t, f), bf16)],
        compiler_params=_params("parallel", "parallel"))(h, h, w_in4, w_in4, wg, wu, bg, bu)


def _gate_grads(gate, up, dav):
    sg = jax.nn.sigmoid(gate)
    return dav * up * (sg * (1.0 + gate * (1.0 - sg))), dav * gate * sg


def ffn_gate_bwd(pg, pu, da, wg, wu, bg, bu, *, name, tr=1024, tc=256):
    t, f = pg.shape
    tr = min(tr, t)
    tile, prev, nxt, wspec, bspec = _conv_specs(t, tr, tc)
    nsteps = t // tr

    def body(pg_ref, pgh_ref, pgn_ref, pu_ref, puh_ref, pun_ref, da_ref, dan_ref, wg_ref, wu_ref, bg_ref, bu_ref,
             dg_ref, du_ref, sg_ref, su_ref):
        i = pl.program_id(1)
        first = i == 0
        last = i == nsteps - 1
        pgv, puv = pg_ref[...], pu_ref[...]
        pg1, pg2 = _shift_down(pgv, pgh_ref[...], 1, first), _shift_down(pgv, pgh_ref[...], 2, first)
        pu1, pu2 = _shift_down(puv, puh_ref[...], 1, first), _shift_down(puv, puh_ref[...], 2, first)
        gate = wg_ref[2:3, :] * pgv + wg_ref[1:2, :] * pg1 + wg_ref[0:1, :] * pg2 + bg_ref[...]
        up = wu_ref[2:3, :] * puv + wu_ref[1:2, :] * pu1 + wu_ref[0:1, :] * pu2 + bu_ref[...]
        dgate, dup = _gate_grads(gate, up, da_ref[...])
        gate_n = _conv(pgn_ref[...], pgv[tr - HALO:], wg_ref, bg_ref, False)
        up_n = _conv(pun_ref[...], puv[tr - HALO:], wu_ref, bu_ref, False)
        dgate_n, dup_n = _gate_grads(gate_n, up_n, dan_ref[...])
        for d, d_n, w_ref, o_ref in ((dgate, dgate_n, wg_ref, dg_ref), (dup, dup_n, wu_ref, du_ref)):
            o_ref[...] = (w_ref[2:3, :] * d + w_ref[1:2, :] * _shift_up(d, d_n, 1, last)
                          + w_ref[0:1, :] * _shift_up(d, d_n, 2, last)).astype(bf16)
        rid = lax.broadcasted_iota(jnp.int32, (8, tc), 0)
        for d, p0, p1, p2, s_ref in ((dgate, pgv, pg1, pg2, sg_ref), (dup, puv, pu1, pu2, su_ref)):
            sums = [jnp.sum(d * p2, axis=0, keepdims=True), jnp.sum(d * p1, axis=0, keepdims=True),
                    jnp.sum(d * p0, axis=0, keepdims=True), jnp.sum(d, axis=0, keepdims=True)]
            part = jnp.zeros((8, tc), f32)
            for k, sk in enumerate(sums):
                part = jnp.where(rid == k, sk, part)

            @pl.when(first)
            def _(s_ref=s_ref, part=part):
                s_ref[...] = part

            @pl.when(i > 0)
            def _(s_ref=s_ref, part=part):
                s_ref[...] += part

    stat = pl.BlockSpec((8, tc), lambda j, i: (0, j))
    return pl.pallas_call(
        body, name=name, grid=(f // tc, nsteps),
        in_specs=[tile, prev, nxt, tile, prev, nxt, tile, nxt, wspec, wspec, bspec, bspec],
        out_specs=[tile, tile, stat, stat],
        out_shape=[jax.ShapeDtypeStruct((t, f), bf16), jax.ShapeDtypeStruct((t, f), bf16),
                   jax.ShapeDtypeStruct((8, f), f32), jax.ShapeDtypeStruct((8, f), f32)],
        compiler_params=_params("parallel", "arbitrary"))(pg, pg, pg, pu, pu, pu, da, da, wg, wu, bg, bu)


def _head_mean_matrix():
    i = lax.broadcasted_iota(jnp.int32, (LANES, LANES), 0) // HEAD_DIM
    j = lax.broadcasted_iota(jnp.int32, (LANES, LANES), 1) // HEAD_DIM
    return jnp.where(i == j, 1.0 / HEAD_DIM, 0.0).astype(f32)


def _lane_half(shape):
    return (lax.broadcasted_iota(jnp.int32, shape, 1) % LANES) // HEAD_DIM


def q_norm_fwd(qp, g2, *, name, scale, tr=512):
    t, w = qp.shape
    tr = min(tr, t)

    def body(x_ref, g_ref, o_ref):
        bd = _head_mean_matrix()
        for cb in range(w // LANES):
            cols = slice(cb * LANES, (cb + 1) * LANES)
            xc = x_ref[:, cols]
            rh = lax.rsqrt(_dot_exact(xc * xc, bd) + EPS)
            o_ref[:, cols] = (xc * rh * g_ref[...] * scale).astype(bf16)

    row = pl.BlockSpec((tr, w), lambda i: (i, 0))
    return pl.pallas_call(
        body, name=name, grid=(t // tr,), in_specs=[row, pl.BlockSpec((1, LANES), lambda i: (0, 0))],
        out_specs=row, out_shape=jax.ShapeDtypeStruct((t, w), bf16),
        compiler_params=_params("parallel"))(qp, g2)


def q_norm_bwd(dq, qp, g2, *, name, scale, tr=512):
    t, w = qp.shape
    tr = min(tr, t)

    def body(dq_ref, x_ref, g_ref, o_ref, dg_ref):
        i = pl.program_id(0)
        bd = _head_mean_matrix()
        acc = jnp.zeros((1, LANES), f32)
        for cb in range(w // LANES):
            cols = slice(cb * LANES, (cb + 1) * LANES)
            xc = x_ref[:, cols]
            rh = lax.rsqrt(_dot_exact(xc * xc, bd) + EPS)
            xh = xc * rh
            dy = dq_ref[:, cols] * scale
            acc = acc + jnp.sum(dy * xh, axis=0, keepdims=True)
            tg = dy * g_ref[...]
            o_ref[:, cols] = (rh * (tg - xh * _dot_exact(tg * xh, bd))).astype(bf16)

        @pl.when(i == 0)
        def _():
            dg_ref[...] = acc

        @pl.when(i > 0)
        def _():
            dg_ref[...] += acc

    row = pl.BlockSpec((tr, w), lambda i: (i, 0))
    vec = pl.BlockSpec((1, LANES), lambda i: (0, 0))
    return pl.pallas_call(
        body, name=name, grid=(t // tr,), in_specs=[row, row, vec], out_specs=[row, vec],
        out_shape=[jax.ShapeDtypeStruct((t, w), bf16), jax.ShapeDtypeStruct((1, LANES), f32)],
        compiler_params=_params("arbitrary"))(dq, qp, g2)


def kv_post_fwd(kv, g2, *, name, tr=512):
    t, w = kv.shape
    tr = min(tr, t)
    kw = w // 2

    def body(x_ref, g_ref, k_ref, v_ref):
        bd = _head_mean_matrix()
        half = _lane_half((tr, LANES))
        for cb in range(kw // LANES):
            xc = x_ref[:, cb * LANES:(cb + 1) * LANES]
            rh = lax.rsqrt(_dot_exact(xc * xc, bd) + EPS)
            kn = xc * rh * g_ref[...]
            vc = x_ref[:, kw + cb * LANES:kw + (cb + 1) * LANES]
            for src, dst in ((kn, k_ref), (vc, v_ref)):
                sw = pltpu.roll(src, HEAD_DIM, 1)
                for hf in range(2):
                    blk = 2 * cb + hf
                    dst[:, blk * LANES:(blk + 1) * LANES] = jnp.where(half == hf, src, sw).astype(bf16)

    return pl.pallas_call(
        body, name=name, grid=(t // tr,),
        in_specs=[pl.BlockSpec((tr, w), lambda i: (i, 0)), pl.BlockSpec((1, LANES), lambda i: (0, 0))],
        out_specs=[pl.BlockSpec((tr, 2 * kw), lambda i: (i, 0))] * 2,
        out_shape=[jax.ShapeDtypeStruct((t, 2 * kw), bf16)] * 2,
        compiler_params=_params("parallel"))(kv, g2)


def kv_post_bwd(dk2, dv2, kv, g2, *, name, tr=512):
    t, w = kv.shape
    tr = min(tr, t)
    kw = w // 2

    def body(dk_ref, dv_ref, x_ref, g_ref, o_ref, dg_ref):
        i = pl.program_id(0)
        bd = _head_mean_matrix()
        half = _lane_half((tr, LANES))
        acc = jnp.zeros((1, LANES), f32)

        def fold(ref, cb):
            a = ref[:, (2 * cb) * LANES:(2 * cb + 1) * LANES]
            b = ref[:, (2 * cb + 1) * LANES:(2 * cb + 2) * LANES]
            return jnp.where(half == 0, a + pltpu.roll(a, HEAD_DIM, 1), b + pltpu.roll(b, HEAD_DIM, 1))

        for cb in range(kw // LANES):
            cols = slice(cb * LANES, (cb + 1) * LANES)
            xc = x_ref[:, cols]
            rh = lax.rsqrt(_dot_exact(xc * xc, bd) + EPS)
            xh = xc * rh
            dy = fold(dk_ref, cb)
            acc = acc + jnp.sum(dy * xh, axis=0, keepdims=True)
            tg = dy * g_ref[...]
            o_ref[:, cols] = (rh * (tg - xh * _dot_exact(tg * xh, bd))).astype(bf16)
            o_ref[:, kw + cb * LANES:kw + (cb + 1) * LANES] = fold(dv_ref, cb).astype(bf16)

        @pl.when(i == 0)
        def _():
            dg_ref[...] = acc

        @pl.when(i > 0)
        def _():
            dg_ref[...] += acc

    dup = pl.BlockSpec((tr, 2 * kw), lambda i: (i, 0))
    row = pl.BlockSpec((tr, w), lambda i: (i, 0))
    vec = pl.BlockSpec((1, LANES), lambda i: (0, 0))
    return pl.pallas_call(
        body, name=name, grid=(t // tr,), in_specs=[dup, dup, row, vec], out_specs=[row, vec],
        out_shape=[jax.ShapeDtypeStruct((t, w), bf16), jax.ShapeDtypeStruct((1, LANES), f32)],
        compiler_params=_params("arbitrary"))(dk2, dv2, kv, g2)


def _slope(h):
    return 2.0 ** (-8.0 * (h + 1) / N_Q_HEADS)


GROUP_ROWS = Q_PER_KV * CHUNK


def _band_mask(n):
    tq = lax.broadcasted_iota(jnp.int32, (GROUP_ROWS, 2 * CHUNK), 0) % CHUNK
    jk = lax.broadcasted_iota(jnp.int32, (GROUP_ROWS, 2 * CHUNK), 1)
    dist = tq + CHUNK - jk
    ok = (dist >= 0) & (dist < CHUNK) & jnp.logical_not((n == 0) & (jk < CHUNK))
    return dist.astype(f32), ok


def _band(ref, n, kh):
    p0 = pl.multiple_of(jnp.maximum(n - 1, 0) * CHUNK, CHUNK)
    c0 = pl.multiple_of(n * CHUNK, CHUNK)
    cols = slice(kh * LANES, (kh + 1) * LANES)
    return jnp.concatenate([ref[pl.ds(p0, CHUNK), cols], ref[pl.ds(c0, CHUNK), cols]], axis=0)


def _stack_heads(ref, kh, half):
    parts = []
    for cb in (2 * kh, 2 * kh + 1):
        xc = ref[:, cb * LANES:(cb + 1) * LANES].astype(f32)
        parts += [jnp.where(half == hf, xc, 0.0).astype(bf16) for hf in range(2)]
    return jnp.concatenate(parts, axis=0)


def _unstack_heads(x4, half):
    return (jnp.where(half == 0, x4[0:CHUNK], x4[CHUNK:2 * CHUNK]),
            jnp.where(half == 0, x4[2 * CHUNK:3 * CHUNK], x4[3 * CHUNK:]))


def _per_head_column(kh, values):
    grp = lax.broadcasted_iota(jnp.int32, (GROUP_ROWS, 1), 0) // CHUNK
    col = jnp.full((GROUP_ROWS, 1), values[0], f32)
    for g in range(1, Q_PER_KV):
        col = jnp.where(grp == g, values[g], col)
    return col


def _softmax_band(q4, kband, dist, ok, slope, sink):
    s = _dot_nt(q4, kband)
    s = jnp.where(ok, s - slope * dist, -jnp.inf)
    m = jnp.maximum(jnp.max(s, axis=1, keepdims=True), sink)
    e = jnp.exp(s - m)
    es = jnp.exp(sink - m)
    den = jnp.sum(e, axis=1, keepdims=True) + es
    return e / den, es / den


def attn_fwd(q, k2, v2, sinks, *, name):
    t, w = q.shape
    nb = t // CHUNK

    def body(sink_ref, q_ref, k_ref, v_ref, o_ref):
        n = pl.program_id(0)
        dist, ok = _band_mask(n)
        half = _lane_half((CHUNK, LANES))
        for kh in range(N_KV_HEADS):
            heads = [Q_PER_KV * kh + g for g in range(Q_PER_KV)]
            slope = _per_head_column(kh, [_slope(h) for h in heads])
            sink = _per_head_column(kh, [sink_ref[h] for h in heads])
            q4 = _stack_heads(q_ref, kh, half)
            p, _ = _softmax_band(q4, _band(k_ref, n, kh), dist, ok, slope, sink)
            o4 = _dot(p.astype(bf16), _band(v_ref, n, kh))
            lo, hi = _unstack_heads(o4, half)
            o_ref[:, (2 * kh) * LANES:(2 * kh + 1) * LANES] = lo.astype(bf16)
            o_ref[:, (2 * kh + 1) * LANES:(2 * kh + 2) * LANES] = hi.astype(bf16)

    full = pl.BlockSpec((t, k2.shape[1]), lambda n: (0, 0))
    return pl.pallas_call(
        body, name=name, grid=(nb,),
        in_specs=[pl.BlockSpec(memory_space=pltpu.SMEM), pl.BlockSpec((CHUNK, w), lambda n: (n, 0)), full, full],
        out_specs=pl.BlockSpec((CHUNK, w), lambda n: (n, 0)),
        out_shape=jax.ShapeDtypeStruct((t, w), bf16),
        compiler_params=_params("parallel"))(sinks, q, k2, v2)


def attn_bwd(q, k2, v2, do, sinks, *, name):
    t, w = q.shape
    nb = t // CHUNK
    kw = k2.shape[1]

    def body(sink_ref, q_ref, k_ref, v_ref, do_ref, dq_ref, dk_ref, dv_ref, ds_ref, kc_ref, vc_ref):
        n = pl.program_id(0)

        @pl.when(n == 0)
        def _():
            ds_ref[...] = jnp.zeros_like(ds_ref)
            kc_ref[...] = jnp.zeros_like(kc_ref)
            vc_ref[...] = jnp.zeros_like(vc_ref)
            dk_ref[...] = jnp.zeros_like(dk_ref)
            dv_ref[...] = jnp.zeros_like(dv_ref)

        @pl.when(n == nb)
        def _():
            dk_ref[...] = kc_ref[...]
            dv_ref[...] = vc_ref[...]

        @pl.when(n < nb)
        def _():
            dist, ok = _band_mask(n)
            half = _lane_half((CHUNK, LANES))
            lane = lax.broadcasted_iota(jnp.int32, (1, LANES), 1)
            sink_acc = jnp.zeros((1, LANES), f32)
            for kh in range(N_KV_HEADS):
                heads = [Q_PER_KV * kh + g for g in range(Q_PER_KV)]
                slope = _per_head_column(kh, [_slope(h) for h in heads])
                sink = _per_head_column(kh, [sink_ref[h] for h in heads])
                q4 = _stack_heads(q_ref, kh, half)
                do4 = _stack_heads(do_ref, kh, half)
                kband = _band(k_ref, n, kh)
                vband = _band(v_ref, n, kh)
                p, ps = _softmax_band(q4, kband, dist, ok, slope, sink)
                dp = _dot_nt(do4, vband)
                delta = jnp.sum(p * dp, axis=1, keepdims=True)
                dsb = (p * (dp - delta)).astype(bf16)
                sd = ps * delta
                for g, h in enumerate(heads):
                    part = jnp.sum(sd[g * CHUNK:(g + 1) * CHUNK], axis=0, keepdims=True)
                    sink_acc = sink_acc + jnp.where(lane == h, -part, 0.0)
                lo, hi = _unstack_heads(_dot(dsb, kband), half)
                dq_ref[:, (2 * kh) * LANES:(2 * kh + 1) * LANES] = lo
                dq_ref[:, (2 * kh + 1) * LANES:(2 * kh + 2) * LANES] = hi
                dkb = _dot_tn(dsb, q4)
                dvb = _dot_tn(p.astype(bf16), do4)
                cols = slice(kh * LANES, (kh + 1) * LANES)
                dk_ref[:, cols] = kc_ref[:, cols] + dkb[0:CHUNK]
                dv_ref[:, cols] = vc_ref[:, cols] + dvb[0:CHUNK]
                kc_ref[:, cols] = dkb[CHUNK:]
                vc_ref[:, cols] = dvb[CHUNK:]
            ds_ref[...] += sink_acc

    full = pl.BlockSpec((t, kw), lambda n: (0, 0))
    qblk = pl.BlockSpec((CHUNK, w), lambda n: (jnp.minimum(n, nb - 1), 0))
    kblk = pl.BlockSpec((CHUNK, kw), lambda n: (jnp.maximum(n - 1, 0), 0))
    return pl.pallas_call(
        body, name=name, grid=(nb + 1,),
        in_specs=[pl.BlockSpec(memory_space=pltpu.SMEM), qblk, full, full, qblk],
        out_specs=[qblk, kblk, kblk, pl.BlockSpec((1, LANES), lambda n: (0, 0))],
        out_shape=[jax.ShapeDtypeStruct((t, w), f32), jax.ShapeDtypeStruct((t, kw), f32),
                   jax.ShapeDtypeStruct((t, kw), f32), jax.ShapeDtypeStruct((1, LANES), f32)],
        scratch_shapes=[pltpu.VMEM((CHUNK, kw), f32), pltpu.VMEM((CHUNK, kw), f32)],
        compiler_params=_params("arbitrary"))(sinks, q, k2, v2, do)


def loss_head(y, target, *, name, tr=512):
    t, d = y.shape
    tr = min(tr, t)

    def body(y_ref, t_ref, dy_ref, s_ref):
        i = pl.program_id(0)
        e = y_ref[...] - t_ref[...]
        dy_ref[...] = e * (1.0 / d)
        part = jnp.sum(e * e, axis=0, keepdims=True)

        @pl.when(i == 0)
        def _():
            s_ref[...] = part

        @pl.when(i > 0)
        def _():
            s_ref[...] += part

    row = pl.BlockSpec((tr, d), lambda i: (i, 0))
    vec = pl.BlockSpec((1, d), lambda i: (0, 0))
    return pl.pallas_call(
        body, name=name, grid=(t // tr,), in_specs=[row, row], out_specs=[row, vec],
        out_shape=[jax.ShapeDtypeStruct((t, d), f32), jax.ShapeDtypeStruct((1, d), f32)],
        compiler_params=_params("arbitrary"))(y, target)


N_STEPS = 8


def _row_blocks(shape):
    if len(shape) == 2:
        r, c = shape
        return (r // N_STEPS, c), (lambda s: (s, 0))
    l, r, c = shape
    per = N_STEPS // l
    return (1, r // per, c), (lambda s: (s // per, s % per, 0))


CAST_STEPS = 4


def cast_into_slot(arrays, k_arr, *, name):
    in_specs, out_specs, out_shape, layers = [], [], [], []
    for a in arrays:
        r, c = a.shape[-2:]
        rb = r // CAST_STEPS
        if a.ndim == 2:
            in_specs.append(pl.BlockSpec((rb, c), lambda s, k: (s, 0)))
            layers.append(None)
        else:
            for l in range(a.shape[0]):
                in_specs.append(pl.BlockSpec((1, rb, c), lambda s, k, l=l: (l, s, 0)))
                layers.append(l)
        for _ in range(1 if a.ndim == 2 else a.shape[0]):
            out_specs.append(pl.BlockSpec((1, rb, c), lambda s, k: (k[0], s, 0)))
            out_shape.append(jax.ShapeDtypeStruct((N_SHARDS, r, c), bf16))
    n = len(in_specs)

    def body(k_ref, *refs):
        for i_ref, o_ref, l in zip(refs[:n], refs[n:], layers):
            o_ref[0] = (i_ref[...] if l is None else i_ref[0]).astype(bf16)

    args = []
    for a in arrays:
        args += [a] * (1 if a.ndim == 2 else a.shape[0])
    return pl.pallas_call(
        body, name=name,
        grid_spec=pltpu.PrefetchScalarGridSpec(num_scalar_prefetch=1, grid=(CAST_STEPS,),
                                               in_specs=in_specs, out_specs=out_specs),
        out_shape=out_shape, compiler_params=_params("parallel"))(k_arr, *args)


def adamw(ws, gs, ms, vs, *, name):
    n = len(ws)
    specs, g_specs, g_count = [], [], []
    for w, g_list in zip(ws, gs):
        blk, index = _row_blocks(w.shape)
        specs.append(pl.BlockSpec(blk, index))
        layers = len(g_list)
        per = N_STEPS // layers
        g_count.append(layers)
        for l in range(layers):
            g_specs.append(pl.BlockSpec(blk[-2:], lambda s, l=l, per=per: (jnp.where(s // per == l, s % per, 0), 0)))
    ng = len(g_specs)

    def body(*refs):
        s = pl.program_id(0)
        g_refs = refs[3 * n:3 * n + ng]
        outs = refs[3 * n + ng:]
        off = 0
        for i in range(n):
            w_ref, m_ref, v_ref = refs[i], refs[n + i], refs[2 * n + i]
            go_ref, d_ref, nm_ref, nv_ref = (outs[k * n + i] for k in range(4))
            layers = g_count[i]
            g = g_refs[off][...]
            for l in range(1, layers):
                g = jnp.where(s // (N_STEPS // layers) == l, g_refs[off + l][...], g)
            off += layers
            g = g.reshape(w_ref.shape)
            m = ADAM_B1 * m_ref[...] + (1.0 - ADAM_B1) * g
            v = ADAM_B2 * v_ref[...] + (1.0 - ADAM_B2) * (g * g)
            m_hat = m / ADAM_C1
            v_hat = v / ADAM_C2
            go_ref[...] = g
            d_ref[...] = -ADAM_LR * (m_hat / (jnp.sqrt(v_hat) + ADAM_EPS) + ADAM_WD * w_ref[...])
            nm_ref[...] = m
            nv_ref[...] = v

    outs = pl.pallas_call(
        body, name=name, grid=(N_STEPS,), in_specs=specs * 3 + g_specs, out_specs=specs * 4,
        out_shape=[jax.ShapeDtypeStruct(a.shape, f32) for a in ws] * 4,
        compiler_params=_params("parallel"))(*ws, *ms, *vs, *[g for g_list in gs for g in g_list])
    return [outs[k * n:(k + 1) * n] for k in range(4)]


def _place():
    return lax.axis_index("x"), lax.axis_index("y"), lax.axis_index("c")


def gather_shards(bufs, *, name, split):
    n = len(bufs)

    def body(*refs):
        bufs_ = refs[:n]
        isend, irecv, dsend, drecv = refs[2 * n:]
        x, y, c = _place()
        k = 2 * x + y
        peers = [(1 - x, y, c), (x, 1 - y, c), (1 - x, 1 - y, c)]
        peer_k = [2 * (1 - x) + y, 2 * x + (1 - y), 2 * (1 - x) + (1 - y)]

        def slab(a, q, h):
            if not split[a]:
                return bufs_[a].at[q]
            half = bufs_[a].shape[1] // 2
            return bufs_[a].at[q, pl.ds(pl.multiple_of(h * half, 16), half)]

        def ici(a, j, q):
            return pltpu.make_async_remote_copy(
                src_ref=slab(a, q, c), dst_ref=slab(a, q, c), send_sem=isend.at[3 * a + j], recv_sem=irecv.at[3 * a + j],
                device_id=peers[j], device_id_type=MESH)

        def d2d(a, j, h):
            return pltpu.make_async_remote_copy(
                src_ref=slab(a, peer_k[j], h), dst_ref=slab(a, peer_k[j], h), send_sem=dsend.at[3 * a + j],
                recv_sem=drecv.at[3 * a + j], device_id=(x, y, 1 - c), device_id_type=MESH)

        for a in range(n):
            for j in range(3):
                ici(a, j, k).start()
        for a in range(n):
            for j in range(3):
                ici(a, j, peer_k[j]).wait_recv()
                if split[a]:
                    d2d(a, j, c).start()
        for a in range(n):
            for j in range(3):
                if split[a]:
                    d2d(a, j, 1 - c).wait_recv()
        for a in range(n):
            for j in range(3):
                ici(a, j, k).wait_send()
                if split[a]:
                    d2d(a, j, c).wait_send()

    return pl.pallas_call(
        body, name=name, in_specs=[ANY] * n, out_specs=[ANY] * n,
        out_shape=[jax.ShapeDtypeStruct(b.shape, b.dtype) for b in bufs],
        input_output_aliases={i: i for i in range(n)},
        scratch_shapes=[pltpu.SemaphoreType.DMA((3 * n,))] * 4)(*bufs)


HBM = pl.BlockSpec(memory_space=pltpu.HBM)
SEM = pl.BlockSpec(memory_space=pltpu.SEMAPHORE)
DATAFLOW = pltpu.SideEffectType.DATAFLOW_SIDE_EFFECTING


def _chip_peers():
    x, y, c = _place()
    return 2 * x + y, [(1 - x, y, c), (x, 1 - y, c), (1 - x, 1 - y, c)], [2 * (1 - x) + y, 2 * x + (1 - y), 2 * (1 - x) + (1 - y)]


def gather_start(bufs, groups, after, *, name):
    n = len(bufs)
    ng = len(groups)

    def body(*refs):
        ins = refs[:n]
        sends, recvs = refs[2 * n + 1:2 * n + 1 + ng], refs[2 * n + 1 + ng:2 * n + 1 + 2 * ng]
        token = refs[-1]
        k, peers, _ = _chip_peers()
        for gi, grp in enumerate(groups):
            for pos, a in enumerate(grp):
                for j in range(3):
                    pltpu.make_async_remote_copy(
                        src_ref=ins[a].at[k], dst_ref=ins[a].at[k], send_sem=sends[gi].at[3 * pos + j],
                        recv_sem=recvs[gi].at[3 * pos + j], device_id=peers[j], device_id_type=MESH).start()
        token[...] = jnp.zeros_like(token)

    sems = [pltpu.SemaphoreType.DMA((3 * len(grp),)) for grp in groups]
    outs = pl.pallas_call(
        body, name=name, in_specs=[HBM] * n + [ANY],
        out_specs=[HBM] * n + [SEM] * (2 * ng) + [pl.BlockSpec(memory_space=pltpu.VMEM)],
        out_shape=[pltpu.HBM(b.shape, b.dtype) for b in bufs] + sems + sems + [jax.ShapeDtypeStruct((8, LANES), f32)],
        input_output_aliases={i: i for i in range(n)},
        compiler_params=pltpu.CompilerParams(has_side_effects=DATAFLOW))(
            *[pltpu.with_memory_space_constraint(b, pltpu.HBM) for b in bufs], after)
    return outs[:n], outs[n:n + ng], outs[n + ng:n + 2 * ng], outs[-1]


def gather_wait(bufs, send_sems, recv_sems, after, *, name):
    n = len(bufs)

    def body(*refs):
        ins = refs[:n]
        send, recv = refs[n], refs[n + 1]
        k, peers, peer_k = _chip_peers()
        for a in range(n):
            for j in range(3):
                copy = pltpu.make_async_remote_copy(
                    src_ref=ins[a].at[k], dst_ref=ins[a].at[peer_k[j]], send_sem=send.at[3 * a + j],
                    recv_sem=recv.at[3 * a + j], device_id=peers[j], device_id_type=MESH)
                copy.wait_send()
                copy.wait_recv()

    return pl.pallas_call(
        body, name=name, in_specs=[HBM] * n + [SEM, SEM, ANY], out_specs=[HBM] * n,
        out_shape=[pltpu.HBM(b.shape, b.dtype) for b in bufs],
        input_output_aliases={i: i for i in range(n)},
        compiler_params=pltpu.CompilerParams(has_side_effects=DATAFLOW))(*bufs, send_sems, recv_sems, after)


def sibling_exchange(arrays, *, name):
    n = len(arrays)

    def body(*refs):
        ins, outs = refs[:n], refs[n:2 * n]
        send, recv = refs[2 * n:]
        x, y, c = _place()

        def copy(a):
            return pltpu.make_async_remote_copy(
                src_ref=ins[a], dst_ref=outs[a], send_sem=send.at[a], recv_sem=recv.at[a],
                device_id=(x, y, 1 - c), device_id_type=MESH)

        for a in range(n):
            copy(a).start()
        for a in range(n):
            copy(a).wait_recv()
        for a in range(n):
            copy(a).wait_send()

    return pl.pallas_call(
        body, name=name, in_specs=[ANY] * n, out_specs=[ANY] * n,
        out_shape=[jax.ShapeDtypeStruct(a.shape, a.dtype) for a in arrays],
        scratch_shapes=[pltpu.SemaphoreType.DMA((n,)), pltpu.SemaphoreType.DMA((n,))])(*arrays)


ALL_MASKS = [(mx, my, mc) for mx in (0, 1) for my in (0, 1) for mc in (0, 1)][1:]


def _scatter_copies(srcs, lands, ev, send, recv, esend, erecv):
    x, y, c = _place()
    me = 4 * x + 2 * y + c
    k, peers, peer_k = _chip_peers()
    out = []
    for a in range(len(srcs)):
        for j in range(3):
            out.append(pltpu.make_async_remote_copy(
                src_ref=srcs[a].at[peer_k[j]], dst_ref=lands[a].at[j], send_sem=send.at[3 * a + j],
                recv_sem=recv.at[3 * a + j], device_id=peers[j], device_id_type=MESH))
    start_ev, wait_ev = [], []
    if ev is not None:
        for j, (mx, my, mc) in enumerate(ALL_MASKS):
            peer = (x ^ mx, y ^ my, c ^ mc)
            start_ev.append(pltpu.make_async_remote_copy(
                src_ref=ev.at[me], dst_ref=ev.at[me], send_sem=esend.at[j], recv_sem=erecv.at[j],
                device_id=peer, device_id_type=MESH))
            wait_ev.append(pltpu.make_async_remote_copy(
                src_ref=ev.at[me], dst_ref=ev.at[me ^ (4 * mx + 2 * my + mc)], send_sem=esend.at[j],
                recv_sem=erecv.at[j], device_id=peer, device_id_type=MESH))
    return out, start_ev, wait_ev


def chip_scatter_start(arrays, everyone, after, *, name):
    n = len(arrays)
    ne = 0 if everyone is None else 1
    lands = [pltpu.with_memory_space_constraint(lax.empty((3,) + a.shape[1:], a.dtype), pltpu.HBM) for a in arrays]

    def body(*refs):
        srcs, lands_ = refs[:n], refs[n:2 * n]
        ev = refs[2 * n] if ne else None
        sems = refs[2 * n + ne + 1 + 2 * n + ne:]
        send, recv = sems[0], sems[1]
        esend, erecv = (sems[2], sems[3]) if ne else (None, None)
        copies, start_ev, _ = _scatter_copies(srcs, lands_, ev, send, recv, esend, erecv)
        for cp in start_ev + copies:
            cp.start()

    sem_shapes = [pltpu.SemaphoreType.DMA((3 * n,))] * 2 + [pltpu.SemaphoreType.DMA((7,))] * (2 * ne)
    bufs = list(arrays) + lands + ([everyone] if ne else [])
    outs = pl.pallas_call(
        body, name=name, in_specs=[HBM] * len(bufs) + [ANY],
        out_specs=[HBM] * len(bufs) + [SEM] * len(sem_shapes),
        out_shape=[pltpu.HBM(b.shape, b.dtype) for b in bufs] + sem_shapes,
        input_output_aliases={i: i for i in range(len(bufs))},
        compiler_params=pltpu.CompilerParams(has_side_effects=DATAFLOW))(
            *[pltpu.with_memory_space_constraint(b, pltpu.HBM) for b in bufs], after)
    return n, ne, outs


def chip_scatter_wait(state, after, *, name):
    n, ne, held = state
    nb = 2 * n + ne
    bufs, sems = held[:nb], held[nb:]

    def body(*refs):
        srcs, lands_ = refs[:n], refs[n:2 * n]
        ev = refs[2 * n] if ne else None
        sems_ = refs[nb:nb + len(sems)]
        esend, erecv = (sems_[2], sems_[3]) if ne else (None, None)
        copies, _, wait_ev = _scatter_copies(srcs, lands_, ev, sems_[0], sems_[1], esend, erecv)
        for cp in wait_ev + copies:
            cp.wait_send()
            cp.wait_recv()

    outs = pl.pallas_call(
        body, name=name, in_specs=[HBM] * nb + [SEM] * len(sems) + [ANY], out_specs=[HBM] * nb,
        out_shape=[pltpu.HBM(b.shape, b.dtype) for b in bufs],
        input_output_aliases={i: i for i in range(nb)},
        compiler_params=pltpu.CompilerParams(has_side_effects=DATAFLOW))(*bufs, *sems, after)
    return outs[n:2 * n], (outs[2 * n] if ne else None)


def sibling_merge(bufs, *, name):
    n = len(bufs)

    def body(*refs):
        bufs_ = refs[:n]
        send, recv = refs[2 * n:]
        x, y, c = _place()

        def copy(u, h):
            return pltpu.make_async_remote_copy(
                src_ref=bufs_[u].at[h], dst_ref=bufs_[u].at[h], send_sem=send.at[u], recv_sem=recv.at[u],
                device_id=(x, y, 1 - c), device_id_type=MESH)

        for u in range(n):
            copy(u, c).start()
        for u in range(n):
            copy(u, 1 - c).wait_recv()
        for u in range(n):
            copy(u, c).wait_send()

    return pl.pallas_call(
        body, name=name, in_specs=[ANY] * n, out_specs=[ANY] * n,
        out_shape=[jax.ShapeDtypeStruct(b.shape, b.dtype) for b in bufs],
        input_output_aliases={i: i for i in range(n)},
        scratch_shapes=[pltpu.SemaphoreType.DMA((n,)), pltpu.SemaphoreType.DMA((n,))])(*bufs)


def sum_leading(a, *, name):
    n, r, c = a.shape

    def body(a_ref, o_ref):
        acc = a_ref[0]
        for i in range(1, n):
            acc = acc + a_ref[i]
        o_ref[...] = acc

    rb = r // 2 if r % 16 == 0 else r
    return pl.pallas_call(
        body, name=name, grid=(r // rb,), in_specs=[pl.BlockSpec((n, rb, c), lambda i: (0, i, 0))],
        out_specs=pl.BlockSpec((rb, c), lambda i: (i, 0)), out_shape=jax.ShapeDtypeStruct((r, c), f32),
        compiler_params=_params("parallel"))(a)


def _half_rows(shape):
    return shape[1] // 2 // 2


def rs_cast_other_half(grads, c_arr, *, name):
    n = len(grads)

    def body(c_ref, *refs):
        for i_ref, o_ref in zip(refs[:n], refs[n:]):
            o_ref[...] = i_ref[...].astype(bf16)

    in_specs = [pl.BlockSpec((1, _half_rows(g.shape), g.shape[2]), lambda s, r, c_ref: (s, (1 - c_ref[0]) * 2 + r, 0))
                for g in grads]
    out_specs = [pl.BlockSpec((1, _half_rows(g.shape), g.shape[2]), lambda s, r, c_ref: (s, r, 0)) for g in grads]
    return pl.pallas_call(
        body, name=name,
        grid_spec=pltpu.PrefetchScalarGridSpec(num_scalar_prefetch=1, grid=(N_SHARDS, 2),
                                               in_specs=in_specs, out_specs=out_specs),
        out_shape=[jax.ShapeDtypeStruct((N_SHARDS, g.shape[1] // 2, g.shape[2]), bf16) for g in grads],
        compiler_params=_params("parallel", "parallel"))(c_arr, *grads)


def rs_add_sibling(grads, recvd, ck_arr, *, name):
    n = len(grads)

    def body(ck_ref, *refs):
        s = pl.program_id(1)
        for u in range(n):
            g_ref, r_ref = refs[u], refs[n + u]
            qb_ref, own_ref = refs[2 * n + u], refs[3 * n + u]
            q = g_ref[0] + r_ref[0].astype(f32)
            qb_ref[0] = q.astype(bf16)

            @pl.when(s == ck_ref[1])
            def _(own_ref=own_ref, q=q):
                own_ref[...] = q

    in_specs = [pl.BlockSpec((1, _half_rows(g.shape), g.shape[2]), lambda r, s, ck: (s, ck[0] * 2 + r, 0)) for g in grads]
    in_specs += [pl.BlockSpec((1, _half_rows(g.shape), g.shape[2]), lambda r, s, ck: (s, r, 0)) for g in grads]
    out_specs = [pl.BlockSpec((1, _half_rows(g.shape), g.shape[2]), lambda r, s, ck: (s, r, 0)) for g in grads]
    out_specs += [pl.BlockSpec((_half_rows(g.shape), g.shape[2]), lambda r, s, ck: (r, 0)) for g in grads]
    outs = pl.pallas_call(
        body, name=name,
        grid_spec=pltpu.PrefetchScalarGridSpec(num_scalar_prefetch=1, grid=(2, N_SHARDS),
                                               in_specs=in_specs, out_specs=out_specs),
        out_shape=[jax.ShapeDtypeStruct((N_SHARDS, g.shape[1] // 2, g.shape[2]), bf16) for g in grads]
        + [jax.ShapeDtypeStruct((g.shape[1] // 2, g.shape[2]), f32) for g in grads],
        compiler_params=_params("parallel", "arbitrary"))(ck_arr, *grads, *recvd)
    return outs[:n], outs[n:]


def rs_sum_chips(owns, recvd, ck_arr, *, name):
    n = len(owns)

    def body(ck_ref, *refs):
        for u in range(n):
            own_ref, r_ref, o_ref = refs[u], refs[n + u], refs[2 * n + u]
            o_ref[0] = ((own_ref[...] + r_ref[0].astype(f32)) + r_ref[1].astype(f32)) + r_ref[2].astype(f32)

    in_specs = [pl.BlockSpec((o.shape[0] // 2, o.shape[1]), lambda r, ck: (r, 0)) for o in owns]
    in_specs += [pl.BlockSpec((3, o.shape[0] // 2, o.shape[1]), lambda r, ck: (0, r, 0)) for o in owns]
    out_specs = [pl.BlockSpec((1, o.shape[0] // 2, o.shape[1]), lambda r, ck: (ck[0], r, 0)) for o in owns]
    return pl.pallas_call(
        body, name=name,
        grid_spec=pltpu.PrefetchScalarGridSpec(num_scalar_prefetch=1, grid=(2,), in_specs=in_specs, out_specs=out_specs),
        out_shape=[jax.ShapeDtypeStruct((2,) + o.shape, f32) for o in owns],
        compiler_params=_params("parallel"))(ck_arr, *owns, *recvd)


SMALL = ("a_norm", "a_v_norm", "a_w_s", "a_b_s", "f_norm", "f_conv_w", "f_conv_b", "kv_norm", "k_norm",
         "b_norm", "b_q_norm", "b_sinks")
BIG = ("a_w_in", "a_w_out", "f_w_in", "f_w_out", "w_kv", "b_w_q", "b_w_o")
PACK_COLS = 1024
PACK_ROWS = 8 * N_STEPS


def _pack(parts, rows=PACK_ROWS):
    flat = jnp.concatenate([p.reshape(-1).astype(f32) for p in parts])
    pad = (-flat.shape[0]) % (rows * PACK_COLS)
    return jnp.pad(flat, (0, pad)).reshape(-1, PACK_COLS)


def _unpack(packed, shapes):
    flat = packed.reshape(-1)
    out, off = [], 0
    for s in shapes:
        size = math.prod(s)
        out.append(flat[off:off + size].reshape(s))
        off += size
    return out


def _ffn_fwd(x, g, h, r, w_in4, conv_w, conv_b, f, tag):
    wg, wu = conv_w[:, :f], conv_w[:, f:]
    bg, bu = conv_b[None, :f], conv_b[None, f:]
    pg, pu, a = ffn_in_fused(h, w_in4, wg, wu, bg, bu, name=f"ffn{tag}_in")
    return a, (x, g, h, r, pg, pu, a, wg, wu, bg, bu)


def _ffn_bwd(dy, saved, w_in4, w_out, tag):
    x, g, h, r, pg, pu, a, wg, wu, bg, bu = saved
    f = w_out.shape[0]
    da = mm_nt([dy], w_out[None], name=f"ffn{tag}_dact", tko=f // 2)
    d_w_out = mm_tn(a, [dy], name=f"ffn{tag}_dwout", n_s=w_out.shape[1], tki=f // 2)
    dpg, dpu, sg, su = ffn_gate_bwd(pg, pu, da, wg, wu, bg, bu, name=f"ffn{tag}_dgate")
    d_w_in = mm_tn(h, [dpg, dpu], name=f"ffn{tag}_dwin", n_s=w_in4.shape[2])
    dh = mm_nt([dpg, dpu], w_in4, name=f"ffn{tag}_dh")
    dx, (dg,) = rms_bwd([dh], x, r, [g], dy, name=f"ffn{tag}_dnorm")
    d_conv_w = jnp.concatenate([sg[0:3], su[0:3]], axis=1)
    d_conv_b = jnp.concatenate([sg[3], su[3]], axis=0)
    return dx, dg, d_w_in, d_conv_w, d_conv_b, d_w_out


def _rs_front(units, c_arr, tag):
    other_bf = rs_cast_other_half(units, c_arr, name=f"rs_cast{tag}")
    from_sib = sibling_exchange(list(other_bf), name=f"rs_sibling{tag}")
    return rs_add_sibling(units, from_sib, c_arr, name=f"rs_add{tag}")


def _rs_back(own, from_chips, c_arr, tag):
    halves = rs_sum_chips(list(own), list(from_chips), c_arr, name=f"rs_sum{tag}")
    return [m.reshape(-1, m.shape[2]) for m in sibling_merge(list(halves), name=f"rs_merge{tag}")]


def kernel(x, a_norm, a_w_in, a_v_norm, a_w_s, a_b_s, a_w_out, f_norm, f_w_in, f_conv_w, f_conv_b, f_w_out, kv_norm, w_kv, k_norm, b_norm, b_w_q, b_q_norm, b_sinks, b_w_o, loss_target, m_a_norm, m_a_w_in, m_a_v_norm, m_a_w_s, m_a_b_s, m_a_w_out, m_f_norm, m_f_w_in, m_f_conv_w, m_f_conv_b, m_f_w_out, m_kv_norm, m_w_kv, m_k_norm, m_b_norm, m_b_w_q, m_b_q_norm, m_b_sinks, m_b_w_o, v_a_norm, v_a_w_in, v_a_v_norm, v_a_w_s, v_a_b_s, v_a_w_out, v_f_norm, v_f_w_in, v_f_conv_w, v_f_conv_b, v_f_w_out, v_kv_norm, v_w_kv, v_k_norm, v_b_norm, v_b_w_q, v_b_q_norm, v_b_sinks, v_b_w_o):
    args = dict(locals())
    weights = {n: args[n] for n in SMALL + BIG}
    moms = {n: args["m_" + n] for n in SMALL + BIG}
    vars_ = {n: args["v_" + n] for n in SMALL + BIG}
    t, d = x.shape[1], x.shape[2]
    xi, yi, ci = _place()
    chip = 2 * xi + yi

    big_local = [a_w_in[0], a_w_out[0], f_w_in, f_w_out, w_kv, b_w_q[0], b_w_o[0]]
    c_arr = jnp.stack([ci, chip]).astype(jnp.int32)
    k_arr = jnp.stack([chip]).astype(jnp.int32)
    b_ain, b_aout, b_fin0, b_fin1, b_fout0, b_fout1, b_kv, b_q, b_o = cast_into_slot(big_local, k_arr, name="cast_weights")
    small_cols = _pack([a_norm, a_v_norm, f_conv_w], rows=8)
    b_small = lax.dynamic_update_slice(jnp.zeros((N_SHARDS,) + small_cols.shape, f32), small_cols[None], (chip, 0, 0))
    g_small, g_a_w_in, g_a_w_out = gather_shards([b_small, b_ain, b_aout], name="gather_first", split=[False, True, True])
    later, send_sems, recv_sems, token = gather_start([b_fin0, b_fout0, b_kv, b_q, b_o, b_fin1, b_fout1],
                                                      [[0], [1, 2, 3, 4], [5, 6]], g_small, name="gather_start")
    ns_cols = a_norm.shape[1]
    nf_cols = f_conv_w.shape[2]
    parts = [_unpack(g_small[k], [a_norm.shape, a_v_norm.shape, f_conv_w.shape]) for k in range(N_SHARDS)]
    a_norm_f = jnp.concatenate([p[0] for p in parts], axis=1) + token[0, 0]
    a_v_norm_f = jnp.concatenate([p[1] for p in parts], axis=1)
    conv_w_f = jnp.concatenate([p[2] for p in parts], axis=2)
    w_a_in = g_a_w_in
    w_a_out = g_a_w_out.reshape(1, -1, d)

    x0 = x[0]
    tril = jnp.tril(jnp.ones((CHUNK, CHUNK), dtype=bool))
    wc = jnp.where(tril[None], a_w_s[0], 0.0).astype(bf16)
    bt = a_b_s[0].T
    kg2 = jnp.tile(k_norm, 2)[None]
    qg2 = jnp.tile(b_q_norm[0], 2)[None]

    (h_a,), r_a = rms_fwd(x0, [a_norm_f], name="a_norm")
    zu = mm_nn(h_a, w_a_in, name="a_in_u", s0=0, ns=2)
    zv = mm_nn(h_a, w_a_in, name="a_in_v", s0=2, ns=2)
    y_a = sgu_gate_fwd(zu, zv, a_v_norm_f, wc, bt, name="a_gate")
    f = f_w_out.shape[1] * N_SHARDS
    x1, (h_f0,), r_f0 = mm_residual(y_a, w_a_out[0], x0, name="a_out", gains=[f_norm[0:1]])
    (g_fin0,) = gather_wait(later[0:1], send_sems[0], recv_sems[0], x1, name="gather_wait_0")
    w_f_in = [g_fin0, None]
    a0, ffn0 = _ffn_fwd(x1, f_norm[0:1], h_f0, r_f0, w_f_in[0], conv_w_f[0], f_conv_b[0], f, "0")
    g_fout0, g_w_kv, g_b_w_q, g_b_w_o = gather_wait(later[1:5], send_sems[1], recv_sems[1], a0, name="gather_wait_1")
    w_f_out = [g_fout0.reshape(-1, d), None]
    w_kv_f = g_w_kv.reshape(1, d, -1)
    w_q_f = g_b_w_q.reshape(1, d, -1)
    w_o_f = g_b_w_o.reshape(1, -1, d)
    x2, (h_k, h_q), r_b = mm_residual(a0, w_f_out[0], x1, name="ffn0_out", gains=[kv_norm[None], b_norm])
    kv = mm_nn(h_k, w_kv_f, name="kv_proj")
    k2, v2 = kv_post_fwd(kv, kg2, name="kv_post")
    qp = mm_nn(h_q, w_q_f, name="q_proj")
    qn = q_norm_fwd(qp, qg2, name="q_norm", scale=HEAD_DIM ** -0.5)
    o = attn_fwd(qn, k2, v2, b_sinks[0], name="attn")
    x3, (h_f1,), r_f1 = mm_residual(o, w_o_f[0], x2, name="o_proj", gains=[f_norm[1:2]])
    g_fin1, g_fout1 = gather_wait(later[5:7], send_sems[2], recv_sems[2], x3, name="gather_wait_2")
    w_f_in[1] = g_fin1
    w_f_out[1] = g_fout1.reshape(-1, d)
    a1, ffn1 = _ffn_fwd(x3, f_norm[1:2], h_f1, r_f1, w_f_in[1], conv_w_f[1], f_conv_b[1], f, "1")
    dx4, sq = mm_residual(a1, w_f_out[1], x3, name="ffn1_out", target=loss_target[0])
    loss_part = (0.5 * jnp.sum(sq) / d).reshape(1)

    dx3, d_fn1, d_fwin1, d_cw1, d_cb1, d_fwout1 = _ffn_bwd(dx4, ffn1, w_f_in[1], w_f_out[1], "1")
    do = mm_nt([dx3], w_o_f, name="o_proj_dx")
    d_w_o = mm_tn(o, [dx3], name="o_proj_dw", n_s=d)
    dqn, dk2, dv2, dsink = attn_bwd(qn, k2, v2, do, b_sinks[0], name="attn_bwd")
    dqp, dqg = q_norm_bwd(dqn, qp, qg2, name="q_norm_bwd", scale=HEAD_DIM ** -0.5)
    dkv, dkg = kv_post_bwd(dk2, dv2, kv, kg2, name="kv_post_bwd")
    d_w_q = mm_tn(h_q, [dqp], name="q_proj_dw", n_s=w_q_f.shape[2])
    dh_q = mm_nt([dqp], w_q_f, name="q_proj_dx")
    d_w_kv = mm_tn(h_k, [dkv], name="kv_proj_dw", n_s=w_kv_f.shape[2])
    dh_k = mm_nt([dkv], w_kv_f, name="kv_proj_dx")
    dx2, (d_kvn, d_bn) = rms_bwd([dh_k, dh_q], x2, r_b, [kv_norm[None], b_norm], dx3, name="b_norm_bwd")
    sh = N_SHARDS
    units1 = [d_fwin1, d_fwout1.reshape(sh, -1, d), d_w_kv.reshape(sh, -1, d_w_kv.shape[2]),
              d_w_q.reshape(sh, -1, d_w_q.shape[2]), d_w_o.reshape(sh, -1, d)]
    chip_bf1, own1 = _rs_front(units1, c_arr, "1")
    scatter1 = chip_scatter_start(list(chip_bf1), None, dx2, name="rs_chips_start1")
    dx1, d_fn0, d_fwin0, d_cw0, d_cb0, d_fwout0 = _ffn_bwd(dx2, ffn0, w_f_in[0], w_f_out[0], "0")
    chip_bf2, own2 = _rs_front([d_fwin0, d_fwout0.reshape(sh, -1, d)], c_arr, "2")
    scatter2 = chip_scatter_start(list(chip_bf2), None, dx1, name="rs_chips_start2")
    dy_a = mm_nt([dx1], w_a_out, name="a_out_dx")
    d_w_aout = mm_tn(y_a, [dx1], name="a_out_dw", n_s=d)
    dzu, dzv, d_avn, d_ws, d_bt = sgu_gate_bwd(zu, zv, dy_a, a_v_norm_f, wc, bt, name="a_gate_bwd")
    d_w_ain = mm_tn(h_a, [dzu, dzv], name="a_in_dw", n_s=w_a_in.shape[2])
    dh_a = mm_nt([dzu, dzv], w_a_in, name="a_in_dx")
    dx0, (d_an,) = rms_bwd([dh_a], x0, r_a, [a_norm_f], dx1, name="a_norm_bwd")
    grad_x = dx0[None]

    chip_bf3, own3 = _rs_front([d_w_ain, d_w_aout.reshape(sh, -1, d)], c_arr, "3")
    d_fn = jnp.concatenate([d_fn0, d_fn1], axis=0)
    d_cw = jnp.stack([d_cw0, d_cw1])
    d_cb = jnp.stack([d_cb0, d_cb1])
    d_kg = (dkg[0, :HEAD_DIM] + dkg[0, HEAD_DIM:])
    d_qg = (dqg[0, :HEAD_DIM] + dqg[0, HEAD_DIM:])[None]
    small_full = [d_an, d_avn, d_ws[None], d_bt.T[None], d_fn, d_cw, d_cb, d_kvn[0], d_kg, d_bn, d_qg,
                  dsink[:, :N_Q_HEADS], loss_part]
    packed = _pack(small_full)
    me = 4 * xi + 2 * yi + ci
    everyone = lax.dynamic_update_slice(lax.empty((N_DEV,) + packed.shape, f32), packed[None], (me, 0, 0))
    scatter3 = chip_scatter_start(list(chip_bf3), everyone, chip_bf3[0], name="rs_chips_start3")
    from_chips1, _ = chip_scatter_wait(scatter1, scatter3[2][0], name="rs_chips_wait1")
    fin1, fout1, gkv, gq, go = _rs_back(own1, from_chips1, c_arr, "1")
    from_chips2, _ = chip_scatter_wait(scatter2, fin1, name="rs_chips_wait2")
    fin0, fout0 = _rs_back(own2, from_chips2, c_arr, "2")
    late = ("f_w_in", "f_w_out", "w_kv", "b_w_q", "b_w_o")
    res_late = adamw([weights[n] for n in late], [[fin0, fin1], [fout0, fout1], [gkv], [gq], [go]],
                     [moms[n] for n in late], [vars_[n] for n in late], name="adamw_late")
    from_chips3, from_all = chip_scatter_wait(scatter3, res_late[1][2], name="rs_chips_wait3")
    ain, aout = _rs_back(own3, from_chips3, c_arr, "3")
    first = ("a_w_in", "a_w_out")
    res_first = adamw([weights[n] for n in first], [[ain], [aout]], [moms[n] for n in first],
                      [vars_[n] for n in first], name="adamw_first")
    big = {n: tuple(r[i] for r in res_late) for i, n in enumerate(late)}
    big.update({n: tuple(r[i] for r in res_first) for i, n in enumerate(first)})

    full_shapes = [g.shape for g in small_full]
    small_g = _unpack(sum_leading(from_all, name="small_sum"), full_shapes)
    loss = small_g.pop()[0]
    small_g[0] = lax.dynamic_slice_in_dim(small_g[0], chip * ns_cols, ns_cols, axis=1)
    small_g[1] = lax.dynamic_slice_in_dim(small_g[1], chip * ns_cols, ns_cols, axis=1)
    small_g[5] = lax.dynamic_slice_in_dim(small_g[5], chip * nf_cols, nf_cols, axis=2)
    small_shapes = [weights[n].shape for n in SMALL]
    small_g = [g.reshape(s) for g, s in zip(small_g, small_shapes)]
    pw, pg_, pm, pv = (_pack(v) for v in ([weights[n] for n in SMALL], small_g, [moms[n] for n in SMALL],
                                          [vars_[n] for n in SMALL]))
    _, (sd,), (sm,), (sv,) = adamw([pw], [[pg_]], [pm], [pv], name="adamw_small")
    small_d, small_m, small_v = (_unpack(v, small_shapes) for v in (sd, sm, sv))

    out = {}
    for i, n in enumerate(SMALL):
        out[n] = (small_g[i], small_d[i], small_m[i], small_v[i])
    out.update(big)
    order = ["a_norm", "a_w_in", "a_v_norm", "a_w_s", "a_b_s", "a_w_out", "f_norm", "f_w_in", "f_conv_w", "f_conv_b",
             "f_w_out", "kv_norm", "w_kv", "k_norm", "b_norm", "b_w_q", "b_q_norm", "b_sinks", "b_w_o"]
    return (loss, grad_x, *[out[n][0] for n in order], *[out[n][1] for n in order],
            *[out[n][2] for n in order], *[out[n][3] for n in order])
```

```python
import functools
import math

import jax
import jax.numpy as jnp
from jax import lax
from jax.experimental import pallas as pl
from jax.experimental.pallas import tpu as pltpu

f32 = jnp.float32
bf16 = jnp.bfloat16
MESH = pl.DeviceIdType.MESH
ANY = pl.BlockSpec(memory_space=pl.ANY)

EPS = 1e-6
LANES = 128
CHUNK = 128
HEAD_DIM = 64
N_Q_HEADS = 16
N_KV_HEADS = 4
Q_PER_KV = N_Q_HEADS // N_KV_HEADS
N_SHARDS = 4
N_DEV = 8

ADAM_LR = 0.001
ADAM_B1 = 0.9
ADAM_B2 = 0.999
ADAM_EPS = 1e-08
ADAM_WD = 0.01
ADAM_STEP = 10
ADAM_C1 = 1.0 - ADAM_B1 ** ADAM_STEP
ADAM_C2 = 1.0 - ADAM_B2 ** ADAM_STEP

_INV_SQRT2 = 1.0 / math.sqrt(2.0)
_INV_SQRT2PI = 1.0 / math.sqrt(2.0 * math.pi)


def _params(*sem):
    return pltpu.CompilerParams(dimension_semantics=sem)


def _gelu(z):
    return 0.5 * z * (1.0 + lax.erf(z * _INV_SQRT2))


def _gelu_grad(z):
    return 0.5 * (1.0 + lax.erf(z * _INV_SQRT2)) + z * jnp.exp(-0.5 * z * z) * _INV_SQRT2PI


def _dot(a, b):
    return jnp.dot(a, b, preferred_element_type=f32)


def _dot_nt(a, b):
    return lax.dot_general(a, b, (((1,), (1,)), ((), ())), preferred_element_type=f32)


def _dot_tn(a, b):
    return lax.dot_general(a, b, (((0,), (0,)), ((), ())), preferred_element_type=f32)


def _dot_exact(a, b):
    return jnp.dot(a, b, preferred_element_type=f32, precision=lax.Precision.HIGHEST)


VMEM_TILE_BUDGET = 36 * 1024 * 1024
MAX_ROW_TILE = 2048


def _row_tile(m, fixed_bytes, row_bytes):
    tm = min(m, MAX_ROW_TILE)
    while tm > 256 and 2 * (fixed_bytes + tm * row_bytes) > VMEM_TILE_BUDGET:
        tm //= 2
    return tm


def _isz(a):
    return jnp.dtype(a.dtype).itemsize


def mm_nn(a, w3, *, name, s0=0, ns=None, add=None, out_dtype=f32):
    m, k = a.shape
    s_all, _, n_s = w3.shape
    ns = s_all if ns is None else ns
    tm = _row_tile(m, k * n_s * 2, k * _isz(a) + n_s * jnp.dtype(out_dtype).itemsize + (0 if add is None else n_s * 4))

    def body(*refs):
        if add is None:
            a_ref, w_ref, o_ref = refs
            acc = _dot(a_ref[...].astype(bf16), w_ref[0])
        else:
            a_ref, w_ref, add_ref, o_ref = refs
            acc = _dot(a_ref[...].astype(bf16), w_ref[0]) + add_ref[...]
        o_ref[...] = acc.astype(out_dtype)

    in_specs = [pl.BlockSpec((tm, k), lambda j, i: (i, 0)),
                pl.BlockSpec((1, k, n_s), lambda j, i: (s0 + j, 0, 0))]
    args = [a, w3]
    if add is not None:
        in_specs.append(pl.BlockSpec((tm, n_s), lambda j, i: (i, j)))
        args.append(add)
    return pl.pallas_call(
        body, name=name, grid=(ns, m // tm), in_specs=in_specs,
        out_specs=pl.BlockSpec((tm, n_s), lambda j, i: (i, j)),
        out_shape=jax.ShapeDtypeStruct((m, ns * n_s), out_dtype),
        compiler_params=_params("parallel", "parallel"))(*args)


def mm_nt(a_list, w3, *, name, tko=None, add=None, out_dtype=f32):
    s_all, k_out, n_s = w3.shape
    m = a_list[0].shape[0]
    na = len(a_list)
    spa = s_all // na
    tko = k_out if tko is None else tko
    tm = _row_tile(m, tko * n_s * 2, na * n_s * _isz(a_list[0]) + tko * 4 * (1 if add is None else 2))

    def body(*refs):
        a_refs = refs[:na]
        w_ref = refs[na]
        o_ref = refs[-1]
        s = pl.program_id(2)

        @pl.when(s == 0)
        def _():
            if add is None:
                o_ref[...] = jnp.zeros_like(o_ref)
            else:
                o_ref[...] = refs[na + 1][...]

        for idx in range(na):
            @pl.when(s // spa == idx)
            def _(idx=idx):
                o_ref[...] += _dot_nt(a_refs[idx][...].astype(bf16), w_ref[0])

    def a_map(idx):
        return lambda ko, i, s: (i, jnp.clip(s - idx * spa, 0, spa - 1))

    in_specs = [pl.BlockSpec((tm, n_s), a_map(idx)) for idx in range(na)]
    in_specs.append(pl.BlockSpec((1, tko, n_s), lambda ko, i, s: (s, ko, 0)))
    args = list(a_list) + [w3]
    if add is not None:
        in_specs.append(pl.BlockSpec((tm, tko), lambda ko, i, s: (i, ko)))
        args.append(add)
    return pl.pallas_call(
        body, name=name, grid=(k_out // tko, m // tm, s_all), in_specs=in_specs,
        out_specs=pl.BlockSpec((tm, tko), lambda ko, i, s: (i, ko)),
        out_shape=jax.ShapeDtypeStruct((m, k_out), out_dtype),
        compiler_params=_params("parallel", "parallel", "arbitrary"))(*args)


def mm_tn(a, b_list, *, name, n_s, tki=None):
    m, k_in = a.shape
    na = len(b_list)
    s_all = sum(b.shape[1] for b in b_list) // n_s
    spa = s_all // na
    tki = k_in if tki is None else tki
    tm = _row_tile(m, tki * n_s * 4, tki * _isz(a) + na * n_s * _isz(b_list[0]))

    def body(*refs):
        a_ref = refs[0]
        b_refs = refs[1:1 + na]
        o_ref = refs[-1]
        s = pl.program_id(0)
        r = pl.program_id(2)

        @pl.when(r == 0)
        def _():
            o_ref[...] = jnp.zeros_like(o_ref)

        for idx in range(na):
            @pl.when(s // spa == idx)
            def _(idx=idx):
                o_ref[0] += _dot_tn(a_ref[...].astype(bf16), b_refs[idx][...].astype(bf16))

    def b_map(idx):
        def index(s, ki, r):
            active = (s // spa) == idx
            return (jnp.where(active, r, 0), jnp.clip(s - idx * spa, 0, spa - 1))
        return index

    in_specs = [pl.BlockSpec((tm, tki), lambda s, ki, r: (r, ki))]
    in_specs += [pl.BlockSpec((tm, n_s), b_map(idx)) for idx in range(na)]
    return pl.pallas_call(
        body, name=name, grid=(s_all, k_in // tki, m // tm), in_specs=in_specs,
        out_specs=pl.BlockSpec((1, tki, n_s), lambda s, ki, r: (s, ki, 0)),
        out_shape=jax.ShapeDtypeStruct((s_all, k_in, n_s), f32),
        compiler_params=_params("parallel", "parallel", "arbitrary"))(a, *b_list)


def mm_residual(a, w, x, *, name, gains=(), target=None):
    m, k = a.shape
    d = w.shape[1]
    ng = len(gains)
    tm = _row_tile(m, k * d * 2, k * _isz(a) + d * 4 * 3 + ng * d * 2)

    def body(*refs):
        a_ref, w_ref, x_ref = refs[:3]
        y = _dot(a_ref[...].astype(bf16), w_ref[...]) + x_ref[...]
        if target is None:
            g_refs = refs[3:3 + ng]
            y_ref = refs[3 + ng]
            h_refs = refs[4 + ng:4 + 2 * ng]
            r_ref = refs[-1]
            y_ref[...] = y
            r = lax.rsqrt(jnp.mean(y * y, axis=1, keepdims=True) + EPS)
            yh = y * r
            for g_ref, h_ref in zip(g_refs, h_refs):
                h_ref[...] = (yh * g_ref[...]).astype(bf16)
            r_ref[...] = r
        else:
            t_ref, dy_ref, s_ref = refs[3:]
            i = pl.program_id(0)
            e = y - t_ref[...]
            dy_ref[...] = e * (1.0 / d)
            part = jnp.sum(e * e, axis=0, keepdims=True)

            @pl.when(i == 0)
            def _():
                s_ref[...] = part

            @pl.when(i > 0)
            def _():
                s_ref[...] += part

    row = pl.BlockSpec((tm, d), lambda i: (i, 0))
    vec = pl.BlockSpec((1, d), lambda i: (0, 0))
    in_specs = [pl.BlockSpec((tm, k), lambda i: (i, 0)), pl.BlockSpec((k, d), lambda i: (0, 0)), row]
    if target is None:
        outs = pl.pallas_call(
            body, name=name, grid=(m // tm,), in_specs=in_specs + [vec] * ng,
            out_specs=[row] * (1 + ng) + [pl.BlockSpec((tm, 1), lambda i: (i, 0))],
            out_shape=[jax.ShapeDtypeStruct((m, d), f32)] + [jax.ShapeDtypeStruct((m, d), bf16)] * ng
            + [jax.ShapeDtypeStruct((m, 1), f32)],
            compiler_params=_params("parallel"))(a, w, x, *gains)
        return outs[0], outs[1:1 + ng], outs[-1]
    return pl.pallas_call(
        body, name=name, grid=(m // tm,), in_specs=in_specs + [row], out_specs=[row, vec],
        out_shape=[jax.ShapeDtypeStruct((m, d), f32), jax.ShapeDtypeStruct((1, d), f32)],
        compiler_params=_params("arbitrary"))(a, w, x, target)


def rms_fwd(x, gains, *, name, tr=512):
    t, d = x.shape
    tr = min(tr, t)
    ng = len(gains)

    def body(*refs):
        x_ref = refs[0]
        g_refs = refs[1:1 + ng]
        h_refs = refs[1 + ng:1 + 2 * ng]
        r_ref = refs[-1]
        xv = x_ref[...]
        r = lax.rsqrt(jnp.mean(xv * xv, axis=1, keepdims=True) + EPS)
        xh = xv * r
        for g_ref, h_ref in zip(g_refs, h_refs):
            h_ref[...] = (xh * g_ref[...]).astype(bf16)
        r_ref[...] = r

    row = pl.BlockSpec((tr, d), lambda i: (i, 0))
    vec = pl.BlockSpec((1, d), lambda i: (0, 0))
    outs = pl.pallas_call(
        body, name=name, grid=(t // tr,), in_specs=[row] + [vec] * ng,
        out_specs=[row] * ng + [pl.BlockSpec((tr, 1), lambda i: (i, 0))],
        out_shape=[jax.ShapeDtypeStruct((t, d), bf16)] * ng + [jax.ShapeDtypeStruct((t, 1), f32)],
        compiler_params=_params("parallel"))(x, *gains)
    return outs[:ng], outs[ng]


def rms_bwd(dh_list, x, r, gains, dx_in, *, name, tr=512):
    t, d = x.shape
    tr = min(tr, t)
    ng = len(gains)

    def body(*refs):
        dh_refs = refs[:ng]
        x_ref, r_ref = refs[ng], refs[ng + 1]
        g_refs = refs[ng + 2:2 * ng + 2]
        dxin_ref = refs[2 * ng + 2]
        dx_ref = refs[2 * ng + 3]
        dg_refs = refs[2 * ng + 4:]
        i = pl.program_id(0)
        rv = r_ref[...]
        xh = x_ref[...] * rv
        acc = dxin_ref[...]
        for dh_ref, g_ref, dg_ref in zip(dh_refs, g_refs, dg_refs):
            dh = dh_ref[...]
            part = jnp.sum(dh * xh, axis=0, keepdims=True)

            @pl.when(i == 0)
            def _(dg_ref=dg_ref, part=part):
                dg_ref[...] = part

            @pl.when(i > 0)
            def _(dg_ref=dg_ref, part=part):
                dg_ref[...] += part

            tg = dh * g_ref[...]
            acc = acc + rv * (tg - xh * jnp.mean(tg * xh, axis=1, keepdims=True))
        dx_ref[...] = acc

    row = pl.BlockSpec((tr, d), lambda i: (i, 0))
    vec = pl.BlockSpec((1, d), lambda i: (0, 0))
    outs = pl.pallas_call(
        body, name=name, grid=(t // tr,),
        in_specs=[row] * ng + [row, pl.BlockSpec((tr, 1), lambda i: (i, 0))] + [vec] * ng + [row],
        out_specs=[row] + [vec] * ng,
        out_shape=[jax.ShapeDtypeStruct((t, d), f32)] + [jax.ShapeDtypeStruct((1, d), f32)] * ng,
        compiler_params=_params("arbitrary"))(*dh_list, x, r, *gains, dx_in)
    return outs[0], outs[1:]


def sgu_gate_fwd(zu, zv, gv, wc, bt, *, name, tr=512):
    t, w = zu.shape
    tr = min(tr, t)
    groups = w // LANES

    def body(zu_ref, zv_ref, gv_ref, wc_ref, bt_ref, y_ref):
        vp = _gelu(zv_ref[...])
        rv = lax.rsqrt(jnp.mean(vp * vp, axis=1, keepdims=True) + EPS)
        vb = (vp * rv * gv_ref[...]).astype(bf16)
        for c in range(tr // CHUNK):
            rows = slice(c * CHUNK, (c + 1) * CHUNK)
            for g in range(groups):
                cols = slice(g * LANES, (g + 1) * LANES)
                sv = _dot(wc_ref[g], vb[rows, cols]) + bt_ref[:, g:g + 1]
                y_ref[rows, cols] = (_gelu(zu_ref[rows, cols]) * sv).astype(bf16)

    row = pl.BlockSpec((tr, w), lambda i: (i, 0))
    return pl.pallas_call(
        body, name=name, grid=(t // tr,),
        in_specs=[row, row, pl.BlockSpec((1, w), lambda i: (0, 0)),
                  pl.BlockSpec((groups, CHUNK, CHUNK), lambda i: (0, 0, 0)),
                  pl.BlockSpec((CHUNK, groups), lambda i: (0, 0))],
        out_specs=row, out_shape=jax.ShapeDtypeStruct((t, w), bf16),
        compiler_params=_params("parallel"))(zu, zv, gv, wc, bt)


def sgu_gate_bwd(zu, zv, dy, gv, wc, bt, *, name, tr=512):
    t, w = zu.shape
    tr = min(tr, t)
    groups = w // LANES
    nsteps = t // tr

    def body(zu_ref, zv_ref, dy_ref, gv_ref, wc_ref, bt_ref,
             dzu_ref, dzv_ref, dgv_ref, dws_ref, dbt_ref, dv_ref, bacc_ref):
        i = pl.program_id(0)

        @pl.when(i == 0)
        def _():
            dgv_ref[...] = jnp.zeros_like(dgv_ref)
            dws_ref[...] = jnp.zeros_like(dws_ref)
            bacc_ref[...] = jnp.zeros_like(bacc_ref)

        zvv = zv_ref[...]
        vp = _gelu(zvv)
        rv = lax.rsqrt(jnp.mean(vp * vp, axis=1, keepdims=True) + EPS)
        vhat = vp * rv
        vb = (vhat * gv_ref[...]).astype(bf16)
        for c in range(tr // CHUNK):
            rows = slice(c * CHUNK, (c + 1) * CHUNK)
            for g in range(groups):
                cols = slice(g * LANES, (g + 1) * LANES)
                vblk = vb[rows, cols]
                sv = _dot(wc_ref[g], vblk) + bt_ref[:, g:g + 1]
                zub = zu_ref[rows, cols]
                dyb = dy_ref[rows, cols]
                dzu_ref[rows, cols] = (dyb * sv * _gelu_grad(zub)).astype(bf16)
                dsv = dyb * _gelu(zub)
                bacc_ref[:, cols] += dsv
                dsvb = dsv.astype(bf16)
                dv_ref[rows, cols] = _dot_tn(wc_ref[g], dsvb)
                dws_ref[g] += _dot_nt(dsvb, vblk)
        dv = dv_ref[...]
        dgv_ref[...] += jnp.sum(dv * vhat, axis=0, keepdims=True)
        tg = dv * gv_ref[...]
        dvp = rv * (tg - vhat * jnp.mean(tg * vhat, axis=1, keepdims=True))
        dzv_ref[...] = (dvp * _gelu_grad(zvv)).astype(bf16)

        @pl.when(i == nsteps - 1)
        def _():
            tt = lax.broadcasted_iota(jnp.int32, (CHUNK, CHUNK), 0)
            ss = lax.broadcasted_iota(jnp.int32, (CHUNK, CHUNK), 1)
            for g in range(groups):
                dws_ref[g] = jnp.where(ss <= tt, dws_ref[g], 0.0)
                dbt_ref[:, g:g + 1] = jnp.sum(bacc_ref[:, g * LANES:(g + 1) * LANES], axis=1, keepdims=True)

    row = pl.BlockSpec((tr, w), lambda i: (i, 0))
    full3 = pl.BlockSpec((groups, CHUNK, CHUNK), lambda i: (0, 0, 0))
    return pl.pallas_call(
        body, name=name, grid=(nsteps,),
        in_specs=[row, row, row, pl.BlockSpec((1, w), lambda i: (0, 0)), full3,
                  pl.BlockSpec((CHUNK, groups), lambda i: (0, 0))],
        out_specs=[row, row, pl.BlockSpec((1, w), lambda i: (0, 0)), full3,
                   pl.BlockSpec((CHUNK, groups), lambda i: (0, 0))],
        out_shape=[jax.ShapeDtypeStruct((t, w), bf16), jax.ShapeDtypeStruct((t, w), bf16),
                   jax.ShapeDtypeStruct((1, w), f32), jax.ShapeDtypeStruct((groups, CHUNK, CHUNK), f32),
                   jax.ShapeDtypeStruct((CHUNK, groups), f32)],
        scratch_shapes=[pltpu.VMEM((tr, w), f32), pltpu.VMEM((CHUNK, w), f32)],
        compiler_params=_params("arbitrary"))(zu, zv, dy, gv, wc, bt)


HALO = 8


def _shift_down(v, halo, k, first):
    r = pltpu.roll(v, k, 0)
    hh = jnp.where(first, 0.0, pltpu.roll(halo, k, 0))
    rid = lax.broadcasted_iota(jnp.int32, (HALO, v.shape[1]), 0)
    head = jnp.where(rid < k, hh, r[0:HALO])
    if v.shape[0] == HALO:
        return head
    return jnp.concatenate([head, r[HALO:]], axis=0)


def _shift_up(v, halo, k, last):
    n = v.shape[0]
    r = pltpu.roll(v, n - k, 0)
    hh = jnp.where(last, 0.0, pltpu.roll(halo, HALO - k, 0))
    rid = lax.broadcasted_iota(jnp.int32, (HALO, v.shape[1]), 0)
    tail = jnp.where(rid >= HALO - k, hh, r[n - HALO:])
    return jnp.concatenate([r[:n - HALO], tail], axis=0)


def _conv(p, halo, w_ref, b_ref, first):
    return (w_ref[2:3, :] * p + w_ref[1:2, :] * _shift_down(p, halo, 1, first)
            + w_ref[0:1, :] * _shift_down(p, halo, 2, first) + b_ref[...])


def _conv_specs(t, tr, tc):
    tile = pl.BlockSpec((tr, tc), lambda j, i: (i, j))
    prev = pl.BlockSpec((HALO, tc), lambda j, i: (jnp.maximum(i * (tr // HALO) - 1, 0), j))
    nxt = pl.BlockSpec((HALO, tc), lambda j, i: (jnp.minimum((i + 1) * (tr // HALO), t // HALO - 1), j))
    wspec = pl.BlockSpec((3, tc), lambda j, i: (0, j))
    bspec = pl.BlockSpec((1, tc), lambda j, i: (0, j))
    return tile, prev, nxt, wspec, bspec


BF16_ROWS = 16


def ffn_in_fused(h, w_in4, wg, wu, bg, bu, *, name):
    t, k = h.shape
    s_all, _, n_s = w_in4.shape
    half = s_all // 2
    tm = _row_tile(t, 2 * k * n_s * 2, k * 2 + 2 * n_s * 4 + n_s * 2)

    def body(h_ref, hh_ref, wg_ref, wu_ref, cg_ref, cu_ref, bg_ref, bu_ref, pg_ref, pu_ref, a_ref):
        first = pl.program_id(1) == 0
        hv, hh = h_ref[...], hh_ref[...]
        outs = []
        for w_ref, c_ref, b_ref, p_ref in ((wg_ref, cg_ref, bg_ref, pg_ref), (wu_ref, cu_ref, bu_ref, pu_ref)):
            p = _dot(hv, w_ref[0])
            p_ref[...] = p
            outs.append(_conv(p, _dot(hh, w_ref[0])[BF16_ROWS - HALO:], c_ref, b_ref, first))
        gate, up = outs
        a_ref[...] = (gate * jax.nn.sigmoid(gate) * up).astype(bf16)

    tile = pl.BlockSpec((tm, n_s), lambda j, i: (i, j))
    cw = pl.BlockSpec((3, n_s), lambda j, i: (0, j))
    cb = pl.BlockSpec((1, n_s), lambda j, i: (0, j))
    f = half * n_s
    return pl.pallas_call(
        body, name=name, grid=(half, t // tm),
        in_specs=[pl.BlockSpec((tm, k), lambda j, i: (i, 0)),
                  pl.BlockSpec((BF16_ROWS, k), lambda j, i: (jnp.maximum(i * (tm // BF16_ROWS) - 1, 0), 0)),
                  pl.BlockSpec((1, k, n_s), lambda j, i: (j, 0, 0)),
                  pl.BlockSpec((1, k, n_s), lambda j, i: (j + half, 0, 0)), cw, cw, cb, cb],
        out_specs=[tile, tile, tile],
        out_shape=[jax.ShapeDtypeStruct((t, f), f32), jax.ShapeDtypeStruct((t, f), f32),
                   jax.ShapeDtypeStruct((t, f), bf16)],
        compiler_params=_params("parallel", "parallel"))(h, h, w_in4, w_in4, wg, wu, bg, bu)


def _gate_grads(gate, up, dav):
    sg = jax.nn.sigmoid(gate)
    return dav * up * (sg * (1.0 + gate * (1.0 - sg))), dav * gate * sg


def ffn_gate_bwd(pg, pu, da, wg, wu, bg, bu, *, name, tr=1024, tc=256):
    t, f = pg.shape
    tr = min(tr, t)
    tile, prev, nxt, wspec, bspec = _conv_specs(t, tr, tc)
    nsteps = t // tr

    def body(pg_ref, pgh_ref, pgn_ref, pu_ref, puh_ref, pun_ref, da_ref, dan_ref, wg_ref, wu_ref, bg_ref, bu_ref,
             dg_ref, du_ref, sg_ref, su_ref):
        i = pl.program_id(1)
        first = i == 0
        last = i == nsteps - 1
        pgv, puv = pg_ref[...], pu_ref[...]
        pg1, pg2 = _shift_down(pgv, pgh_ref[...], 1, first), _shift_down(pgv, pgh_ref[...], 2, first)
        pu1, pu2 = _shift_down(puv, puh_ref[...], 1, first), _shift_down(puv, puh_ref[...], 2, first)
        gate = wg_ref[2:3, :] * pgv + wg_ref[1:2, :] * pg1 + wg_ref[0:1, :] * pg2 + bg_ref[...]
        up = wu_ref[2:3, :] * puv + wu_ref[1:2, :] * pu1 + wu_ref[0:1, :] * pu2 + bu_ref[...]
        dgate, dup = _gate_grads(gate, up, da_ref[...])
        gate_n = _conv(pgn_ref[...], pgv[tr - HALO:], wg_ref, bg_ref, False)
        up_n = _conv(pun_ref[...], puv[tr - HALO:], wu_ref, bu_ref, False)
        dgate_n, dup_n = _gate_grads(gate_n, up_n, dan_ref[...])
        for d, d_n, w_ref, o_ref in ((dgate, dgate_n, wg_ref, dg_ref), (dup, dup_n, wu_ref, du_ref)):
            o_ref[...] = (w_ref[2:3, :] * d + w_ref[1:2, :] * _shift_up(d, d_n, 1, last)
                          + w_ref[0:1, :] * _shift_up(d, d_n, 2, last)).astype(bf16)
        rid = lax.broadcasted_iota(jnp.int32, (8, tc), 0)
        for d, p0, p1, p2, s_ref in ((dgate, pgv, pg1, pg2, sg_ref), (dup, puv, pu1, pu2, su_ref)):
            sums = [jnp.sum(d * p2, axis=0, keepdims=True), jnp.sum(d * p1, axis=0, keepdims=True),
                    jnp.sum(d * p0, axis=0, keepdims=True), jnp.sum(d, axis=0, keepdims=True)]
            part = jnp.zeros((8, tc), f32)
            for k, sk in enumerate(sums):
                part = jnp.where(rid == k, sk, part)

            @pl.when(first)
            def _(s_ref=s_ref, part=part):
                s_ref[...] = part

            @pl.when(i > 0)
            def _(s_ref=s_ref, part=part):
                s_ref[...] += part

    stat = pl.BlockSpec((8, tc), lambda j, i: (0, j))
    return pl.pallas_call(
        body, name=name, grid=(f // tc, nsteps),
        in_specs=[tile, prev, nxt, tile, prev, nxt, tile, nxt, wspec, wspec, bspec, bspec],
        out_specs=[tile, tile, stat, stat],
        out_shape=[jax.ShapeDtypeStruct((t, f), bf16), jax.ShapeDtypeStruct((t, f), bf16),
                   jax.ShapeDtypeStruct((8, f), f32), jax.ShapeDtypeStruct((8, f), f32)],
        compiler_params=_params("parallel", "arbitrary"))(pg, pg, pg, pu, pu, pu, da, da, wg, wu, bg, bu)


def _head_mean_matrix():
    i = lax.broadcasted_iota(jnp.int32, (LANES, LANES), 0) // HEAD_DIM
    j = lax.broadcasted_iota(jnp.int32, (LANES, LANES), 1) // HEAD_DIM
    return jnp.where(i == j, 1.0 / HEAD_DIM, 0.0).astype(f32)


def _lane_half(shape):
    return (lax.broadcasted_iota(jnp.int32, shape, 1) % LANES) // HEAD_DIM


def q_norm_fwd(qp, g2, *, name, scale, tr=512):
    t, w = qp.shape
    tr = min(tr, t)

    def body(x_ref, g_ref, o_ref):
        bd = _head_mean_matrix()
        for cb in range(w // LANES):
            cols = slice(cb * LANES, (cb + 1) * LANES)
            xc = x_ref[:, cols]
            rh = lax.rsqrt(_dot_exact(xc * xc, bd) + EPS)
            o_ref[:, cols] = (xc * rh * g_ref[...] * scale).astype(bf16)

    row = pl.BlockSpec((tr, w), lambda i: (i, 0))
    return pl.pallas_call(
        body, name=name, grid=(t // tr,), in_specs=[row, pl.BlockSpec((1, LANES), lambda i: (0, 0))],
        out_specs=row, out_shape=jax.ShapeDtypeStruct((t, w), bf16),
        compiler_params=_params("parallel"))(qp, g2)


def q_norm_bwd(dq, qp, g2, *, name, scale, tr=512):
    t, w = qp.shape
    tr = min(tr, t)

    def body(dq_ref, x_ref, g_ref, o_ref, dg_ref):
        i = pl.program_id(0)
        bd = _head_mean_matrix()
        acc = jnp.zeros((1, LANES), f32)
        for cb in range(w // LANES):
            cols = slice(cb * LANES, (cb + 1) * LANES)
            xc = x_ref[:, cols]
            rh = lax.rsqrt(_dot_exact(xc * xc, bd) + EPS)
            xh = xc * rh
            dy = dq_ref[:, cols] * scale
            acc = acc + jnp.sum(dy * xh, axis=0, keepdims=True)
            tg = dy * g_ref[...]
            o_ref[:, cols] = (rh * (tg - xh * _dot_exact(tg * xh, bd))).astype(bf16)

        @pl.when(i == 0)
        def _():
            dg_ref[...] = acc

        @pl.when(i > 0)
        def _():
            dg_ref[...] += acc

    row = pl.BlockSpec((tr, w), lambda i: (i, 0))
    vec = pl.BlockSpec((1, LANES), lambda i: (0, 0))
    return pl.pallas_call(
        body, name=name, grid=(t // tr,), in_specs=[row, row, vec], out_specs=[row, vec],
        out_shape=[jax.ShapeDtypeStruct((t, w), bf16), jax.ShapeDtypeStruct((1, LANES), f32)],
        compiler_params=_params("arbitrary"))(dq, qp, g2)


def kv_post_fwd(kv, g2, *, name, tr=512):
    t, w = kv.shape
    tr = min(tr, t)
    kw = w // 2

    def body(x_ref, g_ref, k_ref, v_ref):
        bd = _head_mean_matrix()
        half = _lane_half((tr, LANES))
        for cb in range(kw // LANES):
            xc = x_ref[:, cb * LANES:(cb + 1) * LANES]
            rh = lax.rsqrt(_dot_exact(xc * xc, bd) + EPS)
            kn = xc * rh * g_ref[...]
            vc = x_ref[:, kw + cb * LANES:kw + (cb + 1) * LANES]
            for src, dst in ((kn, k_ref), (vc, v_ref)):
                sw = pltpu.roll(src, HEAD_DIM, 1)
                for hf in range(2):
                    blk = 2 * cb + hf
                    dst[:, blk * LANES:(blk + 1) * LANES] = jnp.where(half == hf, src, sw).astype(bf16)

    return pl.pallas_call(
        body, name=name, grid=(t // tr,),
        in_specs=[pl.BlockSpec((tr, w), lambda i: (i, 0)), pl.BlockSpec((1, LANES), lambda i: (0, 0))],
        out_specs=[pl.BlockSpec((tr, 2 * kw), lambda i: (i, 0))] * 2,
        out_shape=[jax.ShapeDtypeStruct((t, 2 * kw), bf16)] * 2,
        compiler_params=_params("parallel"))(kv, g2)


def kv_post_bwd(dk2, dv2, kv, g2, *, name, tr=512):
    t, w = kv.shape
    tr = min(tr, t)
    kw = w // 2

    def body(dk_ref, dv_ref, x_ref, g_ref, o_ref, dg_ref):
        i = pl.program_id(0)
        bd = _head_mean_matrix()
        half = _lane_half((tr, LANES))
        acc = jnp.zeros((1, LANES), f32)

        def fold(ref, cb):
            a = ref[:, (2 * cb) * LANES:(2 * cb + 1) * LANES]
            b = ref[:, (2 * cb + 1) * LANES:(2 * cb + 2) * LANES]
            return jnp.where(half == 0, a + pltpu.roll(a, HEAD_DIM, 1), b + pltpu.roll(b, HEAD_DIM, 1))

        for cb in range(kw // LANES):
            cols = slice(cb * LANES, (cb + 1) * LANES)
            xc = x_ref[:, cols]
            rh = lax.rsqrt(_dot_exact(xc * xc, bd) + EPS)
            xh = xc * rh
            dy = fold(dk_ref, cb)
            acc = acc + jnp.sum(dy * xh, axis=0, keepdims=True)
            tg = dy * g_ref[...]
            o_ref[:, cols] = (rh * (tg - xh * _dot_exact(tg * xh, bd))).astype(bf16)
            o_ref[:, kw + cb * LANES:kw + (cb + 1) * LANES] = fold(dv_ref, cb).astype(bf16)

        @pl.when(i == 0)
        def _():
            dg_ref[...] = acc

        @pl.when(i > 0)
        def _():
            dg_ref[...] += acc

    dup = pl.BlockSpec((tr, 2 * kw), lambda i: (i, 0))
    row = pl.BlockSpec((tr, w), lambda i: (i, 0))
    vec = pl.BlockSpec((1, LANES), lambda i: (0, 0))
    return pl.pallas_call(
        body, name=name, grid=(t // tr,), in_specs=[dup, dup, row, vec], out_specs=[row, vec],
        out_shape=[jax.ShapeDtypeStruct((t, w), bf16), jax.ShapeDtypeStruct((1, LANES), f32)],
        compiler_params=_params("arbitrary"))(dk2, dv2, kv, g2)


def _slope(h):
    return 2.0 ** (-8.0 * (h + 1) / N_Q_HEADS)


GROUP_ROWS = Q_PER_KV * CHUNK


def _band_mask(n):
    tq = lax.broadcasted_iota(jnp.int32, (GROUP_ROWS, 2 * CHUNK), 0) % CHUNK
    jk = lax.broadcasted_iota(jnp.int32, (GROUP_ROWS, 2 * CHUNK), 1)
    dist = tq + CHUNK - jk
    ok = (dist >= 0) & (dist < CHUNK) & jnp.logical_not((n == 0) & (jk < CHUNK))
    return dist.astype(f32), ok


def _band(ref, n, kh):
    p0 = pl.multiple_of(jnp.maximum(n - 1, 0) * CHUNK, CHUNK)
    c0 = pl.multiple_of(n * CHUNK, CHUNK)
    cols = slice(kh * LANES, (kh + 1) * LANES)
    return jnp.concatenate([ref[pl.ds(p0, CHUNK), cols], ref[pl.ds(c0, CHUNK), cols]], axis=0)


def _stack_heads(ref, kh, half):
    parts = []
    for cb in (2 * kh, 2 * kh + 1):
        xc = ref[:, cb * LANES:(cb + 1) * LANES].astype(f32)
        parts += [jnp.where(half == hf, xc, 0.0).astype(bf16) for hf in range(2)]
    return jnp.concatenate(parts, axis=0)


def _unstack_heads(x4, half):
    return (jnp.where(half == 0, x4[0:CHUNK], x4[CHUNK:2 * CHUNK]),
            jnp.where(half == 0, x4[2 * CHUNK:3 * CHUNK], x4[3 * CHUNK:]))


def _per_head_column(kh, values):
    grp = lax.broadcasted_iota(jnp.int32, (GROUP_ROWS, 1), 0) // CHUNK
    col = jnp.full((GROUP_ROWS, 1), values[0], f32)
    for g in range(1, Q_PER_KV):
        col = jnp.where(grp == g, values[g], col)
    return col


def _softmax_band(q4, kband, dist, ok, slope, sink):
    s = _dot_nt(q4, kband)
    s = jnp.where(ok, s - slope * dist, -jnp.inf)
    m = jnp.maximum(jnp.max(s, axis=1, keepdims=True), sink)
    e = jnp.exp(s - m)
    es = jnp.exp(sink - m)
    den = jnp.sum(e, axis=1, keepdims=True) + es
    return e / den, es / den


def attn_fwd(q, k2, v2, sinks, *, name):
    t, w = q.shape
    nb = t // CHUNK

    def body(sink_ref, q_ref, k_ref, v_ref, o_ref):
        n = pl.program_id(0)
        dist, ok = _band_mask(n)
        half = _lane_half((CHUNK, LANES))
        for kh in range(N_KV_HEADS):
            heads = [Q_PER_KV * kh + g for g in range(Q_PER_KV)]
            slope = _per_head_column(kh, [_slope(h) for h in heads])
            sink = _per_head_column(kh, [sink_ref[h] for h in heads])
            q4 = _stack_heads(q_ref, kh, half)
            p, _ = _softmax_band(q4, _band(k_ref, n, kh), dist, ok, slope, sink)
            o4 = _dot(p.astype(bf16), _band(v_ref, n, kh))
            lo, hi = _unstack_heads(o4, half)
            o_ref[:, (2 * kh) * LANES:(2 * kh + 1) * LANES] = lo.astype(bf16)
            o_ref[:, (2 * kh + 1) * LANES:(2 * kh + 2) * LANES] = hi.astype(bf16)

    full = pl.BlockSpec((t, k2.shape[1]), lambda n: (0, 0))
    return pl.pallas_call(
        body, name=name, grid=(nb,),
        in_specs=[pl.BlockSpec(memory_space=pltpu.SMEM), pl.BlockSpec((CHUNK, w), lambda n: (n, 0)), full, full],
        out_specs=pl.BlockSpec((CHUNK, w), lambda n: (n, 0)),
        out_shape=jax.ShapeDtypeStruct((t, w), bf16),
        compiler_params=_params("parallel"))(sinks, q, k2, v2)


def attn_bwd(q, k2, v2, do, sinks, *, name):
    t, w = q.shape
    nb = t // CHUNK
    kw = k2.shape[1]

    def body(sink_ref, q_ref, k_ref, v_ref, do_ref, dq_ref, dk_ref, dv_ref, ds_ref, kc_ref, vc_ref):
        n = pl.program_id(0)

        @pl.when(n == 0)
        def _():
            ds_ref[...] = jnp.zeros_like(ds_ref)
            kc_ref[...] = jnp.zeros_like(kc_ref)
            vc_ref[...] = jnp.zeros_like(vc_ref)
            dk_ref[...] = jnp.zeros_like(dk_ref)
            dv_ref[...] = jnp.zeros_like(dv_ref)

        @pl.when(n == nb)
        def _():
            dk_ref[...] = kc_ref[...]
            dv_ref[...] = vc_ref[...]

        @pl.when(n < nb)
        def _():
            dist, ok = _band_mask(n)
            half = _lane_half((CHUNK, LANES))
            lane = lax.broadcasted_iota(jnp.int32, (1, LANES), 1)
            sink_acc = jnp.zeros((1, LANES), f32)
            for kh in range(N_KV_HEADS):
                heads = [Q_PER_KV * kh + g for g in range(Q_PER_KV)]
                slope = _per_head_column(kh, [_slope(h) for h in heads])
                sink = _per_head_column(kh, [sink_ref[h] for h in heads])
                q4 = _stack_heads(q_ref, kh, half)
                do4 = _stack_heads(do_ref, kh, half)
                kband = _band(k_ref, n, kh)
                vband = _band(v_ref, n, kh)
                p, ps = _softmax_band(q4, kband, dist, ok, slope, sink)
                dp = _dot_nt(do4, vband)
                delta = jnp.sum(p * dp, axis=1, keepdims=True)
                dsb = (p * (dp - delta)).astype(bf16)
                sd = ps * delta
                for g, h in enumerate(heads):
                    part = jnp.sum(sd[g * CHUNK:(g + 1) * CHUNK], axis=0, keepdims=True)
                    sink_acc = sink_acc + jnp.where(lane == h, -part, 0.0)
                lo, hi = _unstack_heads(_dot(dsb, kband), half)
                dq_ref[:, (2 * kh) * LANES:(2 * kh + 1) * LANES] = lo
                dq_ref[:, (2 * kh + 1) * LANES:(2 * kh + 2) * LANES] = hi
                dkb = _dot_tn(dsb, q4)
                dvb = _dot_tn(p.astype(bf16), do4)
                cols = slice(kh * LANES, (kh + 1) * LANES)
                dk_ref[:, cols] = kc_ref[:, cols] + dkb[0:CHUNK]
                dv_ref[:, cols] = vc_ref[:, cols] + dvb[0:CHUNK]
                kc_ref[:, cols] = dkb[CHUNK:]
                vc_ref[:, cols] = dvb[CHUNK:]
            ds_ref[...] += sink_acc

    full = pl.BlockSpec((t, kw), lambda n: (0, 0))
    qblk = pl.BlockSpec((CHUNK, w), lambda n: (jnp.minimum(n, nb - 1), 0))
    kblk = pl.BlockSpec((CHUNK, kw), lambda n: (jnp.maximum(n - 1, 0), 0))
    return pl.pallas_call(
        body, name=name, grid=(nb + 1,),
        in_specs=[pl.BlockSpec(memory_space=pltpu.SMEM), qblk, full, full, qblk],
        out_specs=[qblk, kblk, kblk, pl.BlockSpec((1, LANES), lambda n: (0, 0))],
        out_shape=[jax.ShapeDtypeStruct((t, w), f32), jax.ShapeDtypeStruct((t, kw), f32),
                   jax.ShapeDtypeStruct((t, kw), f32), jax.ShapeDtypeStruct((1, LANES), f32)],
        scratch_shapes=[pltpu.VMEM((CHUNK, kw), f32), pltpu.VMEM((CHUNK, kw), f32)],
        compiler_params=_params("arbitrary"))(sinks, q, k2, v2, do)


def loss_head(y, target, *, name, tr=512):
    t, d = y.shape
    tr = min(tr, t)

    def body(y_ref, t_ref, dy_ref, s_ref):
        i = pl.program_id(0)
        e = y_ref[...] - t_ref[...]
        dy_ref[...] = e * (1.0 / d)
        part = jnp.sum(e * e, axis=0, keepdims=True)

        @pl.when(i == 0)
        def _():
            s_ref[...] = part

        @pl.when(i > 0)
        def _():
            s_ref[...] += part

    row = pl.BlockSpec((tr, d), lambda i: (i, 0))
    vec = pl.BlockSpec((1, d), lambda i: (0, 0))
    return pl.pallas_call(
        body, name=name, grid=(t // tr,), in_specs=[row, row], out_specs=[row, vec],
        out_shape=[jax.ShapeDtypeStruct((t, d), f32), jax.ShapeDtypeStruct((1, d), f32)],
        compiler_params=_params("arbitrary"))(y, target)


N_STEPS = 8


def _row_blocks(shape):
    if len(shape) == 2:
        r, c = shape
        return (r // N_STEPS, c), (lambda s: (s, 0))
    l, r, c = shape
    per = N_STEPS // l
    return (1, r // per, c), (lambda s: (s // per, s % per, 0))


CAST_STEPS = 4


def cast_into_slot(arrays, k_arr, *, name):
    in_specs, out_specs, out_shape, layers = [], [], [], []
    for a in arrays:
        r, c = a.shape[-2:]
        rb = r // CAST_STEPS
        if a.ndim == 2:
            in_specs.append(pl.BlockSpec((rb, c), lambda s, k: (s, 0)))
            layers.append(None)
        else:
            for l in range(a.shape[0]):
                in_specs.append(pl.BlockSpec((1, rb, c), lambda s, k, l=l: (l, s, 0)))
                layers.append(l)
        for _ in range(1 if a.ndim == 2 else a.shape[0]):
            out_specs.append(pl.BlockSpec((1, rb, c), lambda s, k: (k[0], s, 0)))
            out_shape.append(jax.ShapeDtypeStruct((N_SHARDS, r, c), bf16))
    n = len(in_specs)

    def body(k_ref, *refs):
        for i_ref, o_ref, l in zip(refs[:n], refs[n:], layers):
            o_ref[0] = (i_ref[...] if l is None else i_ref[0]).astype(bf16)

    args = []
    for a in arrays:
        args += [a] * (1 if a.ndim == 2 else a.shape[0])
    return pl.pallas_call(
        body, name=name,
        grid_spec=pltpu.PrefetchScalarGridSpec(num_scalar_prefetch=1, grid=(CAST_STEPS,),
                                               in_specs=in_specs, out_specs=out_specs),
        out_shape=out_shape, compiler_params=_params("parallel"))(k_arr, *args)


def adamw(ws, gs, ms, vs, *, name):
    n = len(ws)
    specs, g_specs, g_count = [], [], []
    for w, g_list in zip(ws, gs):
        blk, index = _row_blocks(w.shape)
        specs.append(pl.BlockSpec(blk, index))
        layers = len(g_list)
        per = N_STEPS // layers
        g_count.append(layers)
        for l in range(layers):
            g_specs.append(pl.BlockSpec(blk[-2:], lambda s, l=l, per=per: (jnp.where(s // per == l, s % per, 0), 0)))
    ng = len(g_specs)

    def body(*refs):
        s = pl.program_id(0)
        g_refs = refs[3 * n:3 * n + ng]
        outs = refs[3 * n + ng:]
        off = 0
        for i in range(n):
            w_ref, m_ref, v_ref = refs[i], refs[n + i], refs[2 * n + i]
            go_ref, d_ref, nm_ref, nv_ref = (outs[k * n + i] for k in range(4))
            layers = g_count[i]
            g = g_refs[off][...]
            for l in range(1, layers):
                g = jnp.where(s // (N_STEPS // layers) == l, g_refs[off + l][...], g)
            off += layers
            g = g.reshape(w_ref.shape)
            m = ADAM_B1 * m_ref[...] + (1.0 - ADAM_B1) * g
            v = ADAM_B2 * v_ref[...] + (1.0 - ADAM_B2) * (g * g)
            m_hat = m / ADAM_C1
            v_hat = v / ADAM_C2
            go_ref[...] = g
            d_ref[...] = -ADAM_LR * (m_hat / (jnp.sqrt(v_hat) + ADAM_EPS) + ADAM_WD * w_ref[...])
            nm_ref[...] = m
            nv_ref[...] = v

    outs = pl.pallas_call(
        body, name=name, grid=(N_STEPS,), in_specs=specs * 3 + g_specs, out_specs=specs * 4,
        out_shape=[jax.ShapeDtypeStruct(a.shape, f32) for a in ws] * 4,
        compiler_params=_params("parallel"))(*ws, *ms, *vs, *[g for g_list in gs for g in g_list])
    return [outs[k * n:(k + 1) * n] for k in range(4)]


def _place():
    return lax.axis_index("x"), lax.axis_index("y"), lax.axis_index("c")


def gather_shards(bufs, *, name, split):
    n = len(bufs)

    def body(*refs):
        bufs_ = refs[:n]
        isend, irecv, dsend, drecv = refs[2 * n:]
        x, y, c = _place()
        k = 2 * x + y
        peers = [(1 - x, y, c), (x, 1 - y, c), (1 - x, 1 - y, c)]
        peer_k = [2 * (1 - x) + y, 2 * x + (1 - y), 2 * (1 - x) + (1 - y)]

        def slab(a, q, h):
            if not split[a]:
                return bufs_[a].at[q]
            half = bufs_[a].shape[1] // 2
            return bufs_[a].at[q, pl.ds(pl.multiple_of(h * half, 16), half)]

        def ici(a, j, q):
            return pltpu.make_async_remote_copy(
                src_ref=slab(a, q, c), dst_ref=slab(a, q, c), send_sem=isend.at[3 * a + j], recv_sem=irecv.at[3 * a + j],
                device_id=peers[j], device_id_type=MESH)

        def d2d(a, j, h):
            return pltpu.make_async_remote_copy(
                src_ref=slab(a, peer_k[j], h), dst_ref=slab(a, peer_k[j], h), send_sem=dsend.at[3 * a + j],
                recv_sem=drecv.at[3 * a + j], device_id=(x, y, 1 - c), device_id_type=MESH)

        for a in range(n):
            for j in range(3):
                ici(a, j, k).start()
        for a in range(n):
            for j in range(3):
                ici(a, j, peer_k[j]).wait_recv()
                if split[a]:
                    d2d(a, j, c).start()
        for a in range(n):
            for j in range(3):
                if split[a]:
                    d2d(a, j, 1 - c).wait_recv()
        for a in range(n):
            for j in range(3):
                ici(a, j, k).wait_send()
                if split[a]:
                    d2d(a, j, c).wait_send()

    return pl.pallas_call(
        body, name=name, in_specs=[ANY] * n, out_specs=[ANY] * n,
        out_shape=[jax.ShapeDtypeStruct(b.shape, b.dtype) for b in bufs],
        input_output_aliases={i: i for i in range(n)},
        scratch_shapes=[pltpu.SemaphoreType.DMA((3 * n,))] * 4)(*bufs)


HBM = pl.BlockSpec(memory_space=pltpu.HBM)
SEM = pl.BlockSpec(memory_space=pltpu.SEMAPHORE)
DATAFLOW = pltpu.SideEffectType.DATAFLOW_SIDE_EFFECTING


def _chip_peers():
    x, y, c = _place()
    return 2 * x + y, [(1 - x, y, c), (x, 1 - y, c), (1 - x, 1 - y, c)], [2 * (1 - x) + y, 2 * x + (1 - y), 2 * (1 - x) + (1 - y)]


def gather_start(bufs, groups, after, *, name):
    n = len(bufs)
    ng = len(groups)

    def body(*refs):
        ins = refs[:n]
        sends, recvs = refs[2 * n + 1:2 * n + 1 + ng], refs[2 * n + 1 + ng:2 * n + 1 + 2 * ng]
        token = refs[-1]
        k, peers, _ = _chip_peers()
        for gi, grp in enumerate(groups):
            for pos, a in enumerate(grp):
                for j in range(3):
                    pltpu.make_async_remote_copy(
                        src_ref=ins[a].at[k], dst_ref=ins[a].at[k], send_sem=sends[gi].at[3 * pos + j],
                        recv_sem=recvs[gi].at[3 * pos + j], device_id=peers[j], device_id_type=MESH).start()
        token[...] = jnp.zeros_like(token)

    sems = [pltpu.SemaphoreType.DMA((3 * len(grp),)) for grp in groups]
    outs = pl.pallas_call(
        body, name=name, in_specs=[HBM] * n + [ANY],
        out_specs=[HBM] * n + [SEM] * (2 * ng) + [pl.BlockSpec(memory_space=pltpu.VMEM)],
        out_shape=[pltpu.HBM(b.shape, b.dtype) for b in bufs] + sems + sems + [jax.ShapeDtypeStruct((8, LANES), f32)],
        input_output_aliases={i: i for i in range(n)},
        compiler_params=pltpu.CompilerParams(has_side_effects=DATAFLOW))(
            *[pltpu.with_memory_space_constraint(b, pltpu.HBM) for b in bufs], after)
    return outs[:n], outs[n:n + ng], outs[n + ng:n + 2 * ng], outs[-1]


def gather_wait(bufs, send_sems, recv_sems, after, *, name):
    n = len(bufs)

    def body(*refs):
        ins = refs[:n]
        send, recv = refs[n], refs[n + 1]
        k, peers, peer_k = _chip_peers()
        for a in range(n):
            for j in range(3):
                copy = pltpu.make_async_remote_copy(
                    src_ref=ins[a].at[k], dst_ref=ins[a].at[peer_k[j]], send_sem=send.at[3 * a + j],
                    recv_sem=recv.at[3 * a + j], device_id=peers[j], device_id_type=MESH)
                copy.wait_send()
                copy.wait_recv()

    return pl.pallas_call(
        body, name=name, in_specs=[HBM] * n + [SEM, SEM, ANY], out_specs=[HBM] * n,
        out_shape=[pltpu.HBM(b.shape, b.dtype) for b in bufs],
        input_output_aliases={i: i for i in range(n)},
        compiler_params=pltpu.CompilerParams(has_side_effects=DATAFLOW))(*bufs, send_sems, recv_sems, after)


def sibling_exchange(arrays, *, name):
    n = len(arrays)

    def body(*refs):
        ins, outs = refs[:n], refs[n:2 * n]
        send, recv = refs[2 * n:]
        x, y, c = _place()

        def copy(a):
            return pltpu.make_async_remote_copy(
                src_ref=ins[a], dst_ref=outs[a], send_sem=send.at[a], recv_sem=recv.at[a],
                device_id=(x, y, 1 - c), device_id_type=MESH)

        for a in range(n):
            copy(a).start()
        for a in range(n):
            copy(a).wait_recv()
        for a in range(n):
            copy(a).wait_send()

    return pl.pallas_call(
        body, name=name, in_specs=[ANY] * n, out_specs=[ANY] * n,
        out_shape=[jax.ShapeDtypeStruct(a.shape, a.dtype) for a in arrays],
        scratch_shapes=[pltpu.SemaphoreType.DMA((n,)), pltpu.SemaphoreType.DMA((n,))])(*arrays)


ALL_MASKS = [(mx, my, mc) for mx in (0, 1) for my in (0, 1) for mc in (0, 1)][1:]


def _scatter_copies(srcs, lands, ev, send, recv, esend, erecv):
    x, y, c = _place()
    me = 4 * x + 2 * y + c
    k, peers, peer_k = _chip_peers()
    out = []
    for a in range(len(srcs)):
        for j in range(3):
            out.append(pltpu.make_async_remote_copy(
                src_ref=srcs[a].at[peer_k[j]], dst_ref=lands[a].at[j], send_sem=send.at[3 * a + j],
                recv_sem=recv.at[3 * a + j], device_id=peers[j], device_id_type=MESH))
    start_ev, wait_ev = [], []
    if ev is not None:
        for j, (mx, my, mc) in enumerate(ALL_MASKS):
            peer = (x ^ mx, y ^ my, c ^ mc)
            start_ev.append(pltpu.make_async_remote_copy(
                src_ref=ev.at[me], dst_ref=ev.at[me], send_sem=esend.at[j], recv_sem=erecv.at[j],
                device_id=peer, device_id_type=MESH))
            wait_ev.append(pltpu.make_async_remote_copy(
                src_ref=ev.at[me], dst_ref=ev.at[me ^ (4 * mx + 2 * my + mc)], send_sem=esend.at[j],
                recv_sem=erecv.at[j], device_id=peer, device_id_type=MESH))
    return out, start_ev, wait_ev


def chip_scatter_start(arrays, everyone, after, *, name):
    n = len(arrays)
    ne = 0 if everyone is None else 1
    lands = [pltpu.with_memory_space_constraint(lax.empty((3,) + a.shape[1:], a.dtype), pltpu.HBM) for a in arrays]

    def body(*refs):
        srcs, lands_ = refs[:n], refs[n:2 * n]
        ev = refs[2 * n] if ne else None
        sems = refs[2 * n + ne + 1 + 2 * n + ne:-1]
        send, recv = sems[0], sems[1]
        esend, erecv = (sems[2], sems[3]) if ne else (None, None)
        copies, start_ev, _ = _scatter_copies(srcs, lands_, ev, send, recv, esend, erecv)
        for cp in start_ev + copies:
            cp.start()
        refs[-1][...] = jnp.zeros_like(refs[-1])

    sem_shapes = [pltpu.SemaphoreType.DMA((3 * n,))] * 2 + [pltpu.SemaphoreType.DMA((7,))] * (2 * ne)
    bufs = list(arrays) + lands + ([everyone] if ne else [])
    outs = pl.pallas_call(
        body, name=name, in_specs=[HBM] * len(bufs) + [ANY],
        out_specs=[HBM] * len(bufs) + [SEM] * len(sem_shapes) + [pl.BlockSpec(memory_space=pltpu.VMEM)],
        out_shape=[pltpu.HBM(b.shape, b.dtype) for b in bufs] + sem_shapes + [jax.ShapeDtypeStruct((8, LANES), f32)],
        input_output_aliases={i: i for i in range(len(bufs))},
        compiler_params=pltpu.CompilerParams(has_side_effects=DATAFLOW))(
            *[pltpu.with_memory_space_constraint(b, pltpu.HBM) for b in bufs], after)
    return (n, ne, outs[:-1]), outs[-1]


def chip_scatter_wait(state, after, *, name):
    n, ne, held = state
    nb = 2 * n + ne
    bufs, sems = held[:nb], held[nb:]

    def body(*refs):
        srcs, lands_ = refs[:n], refs[n:2 * n]
        ev = refs[2 * n] if ne else None
        sems_ = refs[nb:nb + len(sems)]
        esend, erecv = (sems_[2], sems_[3]) if ne else (None, None)
        copies, _, wait_ev = _scatter_copies(srcs, lands_, ev, sems_[0], sems_[1], esend, erecv)
        for cp in wait_ev + copies:
            cp.wait_send()
            cp.wait_recv()

    outs = pl.pallas_call(
        body, name=name, in_specs=[HBM] * nb + [SEM] * len(sems) + [ANY], out_specs=[HBM] * nb,
        out_shape=[pltpu.HBM(b.shape, b.dtype) for b in bufs],
        input_output_aliases={i: i for i in range(nb)},
        compiler_params=pltpu.CompilerParams(has_side_effects=DATAFLOW))(*bufs, *sems, after)
    return outs[n:2 * n], (outs[2 * n] if ne else None)


def sibling_merge(bufs, *, name):
    n = len(bufs)

    def body(*refs):
        bufs_ = refs[:n]
        send, recv = refs[2 * n:]
        x, y, c = _place()

        def copy(u, h):
            return pltpu.make_async_remote_copy(
                src_ref=bufs_[u].at[h], dst_ref=bufs_[u].at[h], send_sem=send.at[u], recv_sem=recv.at[u],
                device_id=(x, y, 1 - c), device_id_type=MESH)

        for u in range(n):
            copy(u, c).start()
        for u in range(n):
            copy(u, 1 - c).wait_recv()
        for u in range(n):
            copy(u, c).wait_send()

    return pl.pallas_call(
        body, name=name, in_specs=[ANY] * n, out_specs=[ANY] * n,
        out_shape=[jax.ShapeDtypeStruct(b.shape, b.dtype) for b in bufs],
        input_output_aliases={i: i for i in range(n)},
        scratch_shapes=[pltpu.SemaphoreType.DMA((n,)), pltpu.SemaphoreType.DMA((n,))])(*bufs)


def sum_leading(a, *, name):
    n, r, c = a.shape

    def body(a_ref, o_ref):
        acc = a_ref[0]
        for i in range(1, n):
            acc = acc + a_ref[i]
        o_ref[...] = acc

    rb = r // 2 if r % 16 == 0 else r
    return pl.pallas_call(
        body, name=name, grid=(r // rb,), in_specs=[pl.BlockSpec((n, rb, c), lambda i: (0, i, 0))],
        out_specs=pl.BlockSpec((rb, c), lambda i: (i, 0)), out_shape=jax.ShapeDtypeStruct((r, c), f32),
        compiler_params=_params("parallel"))(a)


def _half_rows(shape):
    return shape[1] // 2 // 2


def rs_cast_other_half(grads, c_arr, *, name):
    n = len(grads)

    def body(c_ref, *refs):
        for i_ref, o_ref in zip(refs[:n], refs[n:]):
            o_ref[...] = i_ref[...].astype(bf16)

    in_specs = [pl.BlockSpec((1, _half_rows(g.shape), g.shape[2]), lambda s, r, c_ref: (s, (1 - c_ref[0]) * 2 + r, 0))
                for g in grads]
    out_specs = [pl.BlockSpec((1, _half_rows(g.shape), g.shape[2]), lambda s, r, c_ref: (s, r, 0)) for g in grads]
    return pl.pallas_call(
        body, name=name,
        grid_spec=pltpu.PrefetchScalarGridSpec(num_scalar_prefetch=1, grid=(N_SHARDS, 2),
                                               in_specs=in_specs, out_specs=out_specs),
        out_shape=[jax.ShapeDtypeStruct((N_SHARDS, g.shape[1] // 2, g.shape[2]), bf16) for g in grads],
        compiler_params=_params("parallel", "parallel"))(c_arr, *grads)


def rs_add_sibling(grads, recvd, ck_arr, *, name):
    n = len(grads)

    def body(ck_ref, *refs):
        s = pl.program_id(1)
        for u in range(n):
            g_ref, r_ref = refs[u], refs[n + u]
            qb_ref, own_ref = refs[2 * n + u], refs[3 * n + u]
            q = g_ref[0] + r_ref[0].astype(f32)
            qb_ref[0] = q.astype(bf16)

            @pl.when(s == ck_ref[1])
            def _(own_ref=own_ref, q=q):
                own_ref[...] = q

    in_specs = [pl.BlockSpec((1, _half_rows(g.shape), g.shape[2]), lambda r, s, ck: (s, ck[0] * 2 + r, 0)) for g in grads]
    in_specs += [pl.BlockSpec((1, _half_rows(g.shape), g.shape[2]), lambda r, s, ck: (s, r, 0)) for g in grads]
    out_specs = [pl.BlockSpec((1, _half_rows(g.shape), g.shape[2]), lambda r, s, ck: (s, r, 0)) for g in grads]
    out_specs += [pl.BlockSpec((_half_rows(g.shape), g.shape[2]), lambda r, s, ck: (r, 0)) for g in grads]
    outs = pl.pallas_call(
        body, name=name,
        grid_spec=pltpu.PrefetchScalarGridSpec(num_scalar_prefetch=1, grid=(2, N_SHARDS),
                                               in_specs=in_specs, out_specs=out_specs),
        out_shape=[jax.ShapeDtypeStruct((N_SHARDS, g.shape[1] // 2, g.shape[2]), bf16) for g in grads]
        + [jax.ShapeDtypeStruct((g.shape[1] // 2, g.shape[2]), f32) for g in grads],
        compiler_params=_params("parallel", "arbitrary"))(ck_arr, *grads, *recvd)
    return outs[:n], outs[n:]


def rs_sum_chips(owns, recvd, ck_arr, *, name):
    n = len(owns)

    def body(ck_ref, *refs):
        for u in range(n):
            own_ref, r_ref, o_ref = refs[u], refs[n + u], refs[2 * n + u]
            o_ref[0] = ((own_ref[...] + r_ref[0].astype(f32)) + r_ref[1].astype(f32)) + r_ref[2].astype(f32)

    in_specs = [pl.BlockSpec((o.shape[0] // 2, o.shape[1]), lambda r, ck: (r, 0)) for o in owns]
    in_specs += [pl.BlockSpec((3, o.shape[0] // 2, o.shape[1]), lambda r, ck: (0, r, 0)) for o in owns]
    out_specs = [pl.BlockSpec((1, o.shape[0] // 2, o.shape[1]), lambda r, ck: (ck[0], r, 0)) for o in owns]
    return pl.pallas_call(
        body, name=name,
        grid_spec=pltpu.PrefetchScalarGridSpec(num_scalar_prefetch=1, grid=(2,), in_specs=in_specs, out_specs=out_specs),
        out_shape=[jax.ShapeDtypeStruct((2,) + o.shape, f32) for o in owns],
        compiler_params=_params("parallel"))(ck_arr, *owns, *recvd)


SMALL = ("a_norm", "a_v_norm", "a_w_s", "a_b_s", "f_norm", "f_conv_w", "f_conv_b", "kv_norm", "k_norm",
         "b_norm", "b_q_norm", "b_sinks")
BIG = ("a_w_in", "a_w_out", "f_w_in", "f_w_out", "w_kv", "b_w_q", "b_w_o")
PACK_COLS = 1024
PACK_ROWS = 8 * N_STEPS


def _pack(parts, rows=PACK_ROWS):
    flat = jnp.concatenate([p.reshape(-1).astype(f32) for p in parts])
    pad = (-flat.shape[0]) % (rows * PACK_COLS)
    return jnp.pad(flat, (0, pad)).reshape(-1, PACK_COLS)


def _unpack(packed, shapes):
    flat = packed.reshape(-1)
    out, off = [], 0
    for s in shapes:
        size = math.prod(s)
        out.append(flat[off:off + size].reshape(s))
        off += size
    return out


def _ffn_fwd(x, g, h, r, w_in4, conv_w, conv_b, f, tag):
    wg, wu = conv_w[:, :f], conv_w[:, f:]
    bg, bu = conv_b[None, :f], conv_b[None, f:]
    pg, pu, a = ffn_in_fused(h, w_in4, wg, wu, bg, bu, name=f"ffn{tag}_in")
    return a, (x, g, h, r, pg, pu, a, wg, wu, bg, bu)


def _ffn_bwd(dy, saved, w_in4, w_out, tag):
    x, g, h, r, pg, pu, a, wg, wu, bg, bu = saved
    f = w_out.shape[0]
    da = mm_nt([dy], w_out[None], name=f"ffn{tag}_dact", tko=f // 2)
    d_w_out = mm_tn(a, [dy], name=f"ffn{tag}_dwout", n_s=w_out.shape[1], tki=f // 2)
    dpg, dpu, sg, su = ffn_gate_bwd(pg, pu, da, wg, wu, bg, bu, name=f"ffn{tag}_dgate")
    d_w_in = mm_tn(h, [dpg, dpu], name=f"ffn{tag}_dwin", n_s=w_in4.shape[2])
    dh = mm_nt([dpg, dpu], w_in4, name=f"ffn{tag}_dh")
    dx, (dg,) = rms_bwd([dh], x, r, [g], dy, name=f"ffn{tag}_dnorm")
    d_conv_w = jnp.concatenate([sg[0:3], su[0:3]], axis=1)
    d_conv_b = jnp.concatenate([sg[3], su[3]], axis=0)
    return dx, dg, d_w_in, d_conv_w, d_conv_b, d_w_out


def _rs_front(units, c_arr, tag):
    other_bf = rs_cast_other_half(units, c_arr, name=f"rs_cast{tag}")
    from_sib = sibling_exchange(list(other_bf), name=f"rs_sibling{tag}")
    return rs_add_sibling(units, from_sib, c_arr, name=f"rs_add{tag}")


def _rs_back(own, from_chips, c_arr, tag):
    halves = rs_sum_chips(list(own), list(from_chips), c_arr, name=f"rs_sum{tag}")
    return [m.reshape(-1, m.shape[2]) for m in sibling_merge(list(halves), name=f"rs_merge{tag}")]


def kernel(x, a_norm, a_w_in, a_v_norm, a_w_s, a_b_s, a_w_out, f_norm, f_w_in, f_conv_w, f_conv_b, f_w_out, kv_norm, w_kv, k_norm, b_norm, b_w_q, b_q_norm, b_sinks, b_w_o, loss_target, m_a_norm, m_a_w_in, m_a_v_norm, m_a_w_s, m_a_b_s, m_a_w_out, m_f_norm, m_f_w_in, m_f_conv_w, m_f_conv_b, m_f_w_out, m_kv_norm, m_w_kv, m_k_norm, m_b_norm, m_b_w_q, m_b_q_norm, m_b_sinks, m_b_w_o, v_a_norm, v_a_w_in, v_a_v_norm, v_a_w_s, v_a_b_s, v_a_w_out, v_f_norm, v_f_w_in, v_f_conv_w, v_f_conv_b, v_f_w_out, v_kv_norm, v_w_kv, v_k_norm, v_b_norm, v_b_w_q, v_b_q_norm, v_b_sinks, v_b_w_o):
    args = dict(locals())
    weights = {n: args[n] for n in SMALL + BIG}
    moms = {n: args["m_" + n] for n in SMALL + BIG}
    vars_ = {n: args["v_" + n] for n in SMALL + BIG}
    t, d = x.shape[1], x.shape[2]
    xi, yi, ci = _place()
    chip = 2 * xi + yi

    big_local = [a_w_in[0], a_w_out[0], f_w_in, f_w_out, w_kv, b_w_q[0], b_w_o[0]]
    c_arr = jnp.stack([ci, chip]).astype(jnp.int32)
    k_arr = jnp.stack([chip]).astype(jnp.int32)
    b_ain, b_aout, b_fin0, b_fin1, b_fout0, b_fout1, b_kv, b_q, b_o = cast_into_slot(big_local, k_arr, name="cast_weights")
    small_cols = _pack([a_norm, a_v_norm, f_conv_w], rows=8)
    b_small = lax.dynamic_update_slice(jnp.zeros((N_SHARDS,) + small_cols.shape, f32), small_cols[None], (chip, 0, 0))
    g_small, g_a_w_in, g_a_w_out = gather_shards([b_small, b_ain, b_aout], name="gather_first", split=[False, True, True])
    later, send_sems, recv_sems, token = gather_start([b_fin0, b_fout0, b_kv, b_q, b_o, b_fin1, b_fout1],
                                                      [[0], [1, 2, 3, 4], [5, 6]], g_small, name="gather_start")
    ns_cols = a_norm.shape[1]
    nf_cols = f_conv_w.shape[2]
    parts = [_unpack(g_small[k], [a_norm.shape, a_v_norm.shape, f_conv_w.shape]) for k in range(N_SHARDS)]
    a_norm_f = jnp.concatenate([p[0] for p in parts], axis=1) + token[0, 0]
    a_v_norm_f = jnp.concatenate([p[1] for p in parts], axis=1)
    conv_w_f = jnp.concatenate([p[2] for p in parts], axis=2)
    w_a_in = g_a_w_in
    w_a_out = g_a_w_out.reshape(1, -1, d)

    x0 = x[0]
    tril = jnp.tril(jnp.ones((CHUNK, CHUNK), dtype=bool))
    wc = jnp.where(tril[None], a_w_s[0], 0.0).astype(bf16)
    bt = a_b_s[0].T
    kg2 = jnp.tile(k_norm, 2)[None]
    qg2 = jnp.tile(b_q_norm[0], 2)[None]

    (h_a,), r_a = rms_fwd(x0, [a_norm_f], name="a_norm")
    zu = mm_nn(h_a, w_a_in, name="a_in_u", s0=0, ns=2)
    zv = mm_nn(h_a, w_a_in, name="a_in_v", s0=2, ns=2)
    y_a = sgu_gate_fwd(zu, zv, a_v_norm_f, wc, bt, name="a_gate")
    f = f_w_out.shape[1] * N_SHARDS
    x1, (h_f0,), r_f0 = mm_residual(y_a, w_a_out[0], x0, name="a_out", gains=[f_norm[0:1]])
    (g_fin0,) = gather_wait(later[0:1], send_sems[0], recv_sems[0], x1, name="gather_wait_0")
    w_f_in = [g_fin0, None]
    a0, ffn0 = _ffn_fwd(x1, f_norm[0:1], h_f0, r_f0, w_f_in[0], conv_w_f[0], f_conv_b[0], f, "0")
    g_fout0, g_w_kv, g_b_w_q, g_b_w_o = gather_wait(later[1:5], send_sems[1], recv_sems[1], a0, name="gather_wait_1")
    w_f_out = [g_fout0.reshape(-1, d), None]
    w_kv_f = g_w_kv.reshape(1, d, -1)
    w_q_f = g_b_w_q.reshape(1, d, -1)
    w_o_f = g_b_w_o.reshape(1, -1, d)
    x2, (h_k, h_q), r_b = mm_residual(a0, w_f_out[0], x1, name="ffn0_out", gains=[kv_norm[None], b_norm])
    kv = mm_nn(h_k, w_kv_f, name="kv_proj")
    k2, v2 = kv_post_fwd(kv, kg2, name="kv_post")
    qp = mm_nn(h_q, w_q_f, name="q_proj")
    qn = q_norm_fwd(qp, qg2, name="q_norm", scale=HEAD_DIM ** -0.5)
    o = attn_fwd(qn, k2, v2, b_sinks[0], name="attn")
    x3, (h_f1,), r_f1 = mm_residual(o, w_o_f[0], x2, name="o_proj", gains=[f_norm[1:2]])
    g_fin1, g_fout1 = gather_wait(later[5:7], send_sems[2], recv_sems[2], x3, name="gather_wait_2")
    w_f_in[1] = g_fin1
    w_f_out[1] = g_fout1.reshape(-1, d)
    a1, ffn1 = _ffn_fwd(x3, f_norm[1:2], h_f1, r_f1, w_f_in[1], conv_w_f[1], f_conv_b[1], f, "1")
    dx4, sq = mm_residual(a1, w_f_out[1], x3, name="ffn1_out", target=loss_target[0])
    loss_part = (0.5 * jnp.sum(sq) / d).reshape(1)

    dx3, d_fn1, d_fwin1, d_cw1, d_cb1, d_fwout1 = _ffn_bwd(dx4, ffn1, w_f_in[1], w_f_out[1], "1")
    do = mm_nt([dx3], w_o_f, name="o_proj_dx")
    d_w_o = mm_tn(o, [dx3], name="o_proj_dw", n_s=d)
    dqn, dk2, dv2, dsink = attn_bwd(qn, k2, v2, do, b_sinks[0], name="attn_bwd")
    dqp, dqg = q_norm_bwd(dqn, qp, qg2, name="q_norm_bwd", scale=HEAD_DIM ** -0.5)
    dkv, dkg = kv_post_bwd(dk2, dv2, kv, kg2, name="kv_post_bwd")
    d_w_q = mm_tn(h_q, [dqp], name="q_proj_dw", n_s=w_q_f.shape[2])
    dh_q = mm_nt([dqp], w_q_f, name="q_proj_dx")
    d_w_kv = mm_tn(h_k, [dkv], name="kv_proj_dw", n_s=w_kv_f.shape[2])
    dh_k = mm_nt([dkv], w_kv_f, name="kv_proj_dx")
    dx2, (d_kvn, d_bn) = rms_bwd([dh_k, dh_q], x2, r_b, [kv_norm[None], b_norm], dx3, name="b_norm_bwd")
    sh = N_SHARDS
    units1 = [d_fwin1, d_fwout1.reshape(sh, -1, d), d_w_kv.reshape(sh, -1, d_w_kv.shape[2]),
              d_w_q.reshape(sh, -1, d_w_q.shape[2]), d_w_o.reshape(sh, -1, d)]
    chip_bf1, own1 = _rs_front(units1, c_arr, "1")
    scatter1, token1 = chip_scatter_start(list(chip_bf1), None, dx2, name="rs_chips_start1")
    ffn0 = ffn0[:9] + (ffn0[9] + token1[0, 0],) + ffn0[10:]
    dx1, d_fn0, d_fwin0, d_cw0, d_cb0, d_fwout0 = _ffn_bwd(dx2, ffn0, w_f_in[0], w_f_out[0], "0")
    chip_bf2, own2 = _rs_front([d_fwin0, d_fwout0.reshape(sh, -1, d)], c_arr, "2")
    scatter2, token2 = chip_scatter_start(list(chip_bf2), None, dx1, name="rs_chips_start2")
    a_v_norm_f = a_v_norm_f + token2[0, 0]
    dy_a = mm_nt([dx1], w_a_out, name="a_out_dx")
    d_w_aout = mm_tn(y_a, [dx1], name="a_out_dw", n_s=d)
    dzu, dzv, d_avn, d_ws, d_bt = sgu_gate_bwd(zu, zv, dy_a, a_v_norm_f, wc, bt, name="a_gate_bwd")
    d_w_ain = mm_tn(h_a, [dzu, dzv], name="a_in_dw", n_s=w_a_in.shape[2])
    dh_a = mm_nt([dzu, dzv], w_a_in, name="a_in_dx")
    dx0, (d_an,) = rms_bwd([dh_a], x0, r_a, [a_norm_f], dx1, name="a_norm_bwd")
    grad_x = dx0[None]

    chip_bf3, own3 = _rs_front([d_w_ain, d_w_aout.reshape(sh, -1, d)], c_arr, "3")
    d_fn = jnp.concatenate([d_fn0, d_fn1], axis=0)
    d_cw = jnp.stack([d_cw0, d_cw1])
    d_cb = jnp.stack([d_cb0, d_cb1])
    d_kg = (dkg[0, :HEAD_DIM] + dkg[0, HEAD_DIM:])
    d_qg = (dqg[0, :HEAD_DIM] + dqg[0, HEAD_DIM:])[None]
    small_full = [d_an, d_avn, d_ws[None], d_bt.T[None], d_fn, d_cw, d_cb, d_kvn[0], d_kg, d_bn, d_qg,
                  dsink[:, :N_Q_HEADS], loss_part]
    packed = _pack(small_full)
    me = 4 * xi + 2 * yi + ci
    everyone = lax.dynamic_update_slice(lax.empty((N_DEV,) + packed.shape, f32), packed[None], (me, 0, 0))
    scatter3, _ = chip_scatter_start(list(chip_bf3), everyone, chip_bf3[0], name="rs_chips_start3")
    from_chips1, _ = chip_scatter_wait(scatter1, scatter3[2][0], name="rs_chips_wait1")
    fin1, fout1, gkv, gq, go = _rs_back(own1, from_chips1, c_arr, "1")
    from_chips2, _ = chip_scatter_wait(scatter2, fin1, name="rs_chips_wait2")
    fin0, fout0 = _rs_back(own2, from_chips2, c_arr, "2")
    late = ("f_w_in", "f_w_out", "w_kv", "b_w_q", "b_w_o")
    res_late = adamw([weights[n] for n in late], [[fin0, fin1], [fout0, fout1], [gkv], [gq], [go]],
                     [moms[n] for n in late], [vars_[n] for n in late], name="adamw_late")
    from_chips3, from_all = chip_scatter_wait(scatter3, res_late[1][2], name="rs_chips_wait3")
    ain, aout = _rs_back(own3, from_chips3, c_arr, "3")
    first = ("a_w_in", "a_w_out")
    res_first = adamw([weights[n] for n in first], [[ain], [aout]], [moms[n] for n in first],
                      [vars_[n] for n in first], name="adamw_first")
    big = {n: tuple(r[i] for r in res_late) for i, n in enumerate(late)}
    big.update({n: tuple(r[i] for r in res_first) for i, n in enumerate(first)})

    full_shapes = [g.shape for g in small_full]
    small_g = _unpack(sum_leading(from_all, name="small_sum"), full_shapes)
    loss = small_g.pop()[0]
    small_g[0] = lax.dynamic_slice_in_dim(small_g[0], chip * ns_cols, ns_cols, axis=1)
    small_g[1] = lax.dynamic_slice_in_dim(small_g[1], chip * ns_cols, ns_cols, axis=1)
    small_g[5] = lax.dynamic_slice_in_dim(small_g[5], chip * nf_cols, nf_cols, axis=2)
    small_shapes = [weights[n].shape for n in SMALL]
    small_g = [g.reshape(s) for g, s in zip(small_g, small_shapes)]
    pw, pg_, pm, pv = (_pack(v) for v in ([weights[n] for n in SMALL], small_g, [moms[n] for n in SMALL],
                                          [vars_[n] for n in SMALL]))
    _, (sd,), (sm,), (sv,) = adamw([pw], [[pg_]], [pm], [pv], name="adamw_small")
    small_d, small_m, small_v = (_unpack(v, small_shapes) for v in (sd, sm, sv))

    out = {}
    for i, n in enumerate(SMALL):
        out[n] = (small_g[i], small_d[i], small_m[i], small_v[i])
    out.update(big)
    order = ["a_norm", "a_w_in", "a_v_norm", "a_w_s", "a_b_s", "a_w_out", "f_norm", "f_w_in", "f_conv_w", "f_conv_b",
             "f_w_out", "kv_norm", "w_kv", "k_norm", "b_norm", "b_w_q", "b_q_norm", "b_sinks", "b_w_o"]
    return (loss, grad_x, *[out[n][0] for n in order], *[out[n][1] for n in order],
            *[out[n][2] for n in order], *[out[n][3] for n in order])
```

```python
import functools
import math

import jax
import jax.numpy as jnp
from jax import lax
from jax.experimental import pallas as pl
from jax.experimental.pallas import tpu as pltpu

f32 = jnp.float32
bf16 = jnp.bfloat16
MESH = pl.DeviceIdType.MESH
ANY = pl.BlockSpec(memory_space=pl.ANY)

EPS = 1e-6
LANES = 128
CHUNK = 128
HEAD_DIM = 64
N_Q_HEADS = 16
N_KV_HEADS = 4
Q_PER_KV = N_Q_HEADS // N_KV_HEADS
N_SHARDS = 4
N_DEV = 8

ADAM_LR = 0.001
ADAM_B1 = 0.9
ADAM_B2 = 0.999
ADAM_EPS = 1e-08
ADAM_WD = 0.01
ADAM_STEP = 10
ADAM_C1 = 1.0 - ADAM_B1 ** ADAM_STEP
ADAM_C2 = 1.0 - ADAM_B2 ** ADAM_STEP

_INV_SQRT2 = 1.0 / math.sqrt(2.0)
_INV_SQRT2PI = 1.0 / math.sqrt(2.0 * math.pi)


def _params(*sem):
    return pltpu.CompilerParams(dimension_semantics=sem)


def _gelu(z):
    return 0.5 * z * (1.0 + lax.erf(z * _INV_SQRT2))


def _gelu_grad(z):
    return 0.5 * (1.0 + lax.erf(z * _INV_SQRT2)) + z * jnp.exp(-0.5 * z * z) * _INV_SQRT2PI


def _dot(a, b):
    return jnp.dot(a, b, preferred_element_type=f32)


def _dot_nt(a, b):
    return lax.dot_general(a, b, (((1,), (1,)), ((), ())), preferred_element_type=f32)


def _dot_tn(a, b):
    return lax.dot_general(a, b, (((0,), (0,)), ((), ())), preferred_element_type=f32)


def _dot_exact(a, b):
    return jnp.dot(a, b, preferred_element_type=f32, precision=lax.Precision.HIGHEST)


VMEM_TILE_BUDGET = 36 * 1024 * 1024
MAX_ROW_TILE = 2048


def _row_tile(m, fixed_bytes, row_bytes):
    tm = min(m, MAX_ROW_TILE)
    while tm > 256 and 2 * (fixed_bytes + tm * row_bytes) > VMEM_TILE_BUDGET:
        tm //= 2
    return tm


def _isz(a):
    return jnp.dtype(a.dtype).itemsize


def mm_nn(a, w3, *, name, s0=0, ns=None, add=None, out_dtype=f32):
    m, k = a.shape
    s_all, _, n_s = w3.shape
    ns = s_all if ns is None else ns
    tm = _row_tile(m, k * n_s * 2, k * _isz(a) + n_s * jnp.dtype(out_dtype).itemsize + (0 if add is None else n_s * 4))

    def body(*refs):
        if add is None:
            a_ref, w_ref, o_ref = refs
            acc = _dot(a_ref[...].astype(bf16), w_ref[0])
        else:
            a_ref, w_ref, add_ref, o_ref = refs
            acc = _dot(a_ref[...].astype(bf16), w_ref[0]) + add_ref[...]
        o_ref[...] = acc.astype(out_dtype)

    in_specs = [pl.BlockSpec((tm, k), lambda j, i: (i, 0)),
                pl.BlockSpec((1, k, n_s), lambda j, i: (s0 + j, 0, 0))]
    args = [a, w3]
    if add is not None:
        in_specs.append(pl.BlockSpec((tm, n_s), lambda j, i: (i, j)))
        args.append(add)
    return pl.pallas_call(
        body, name=name, grid=(ns, m // tm), in_specs=in_specs,
        out_specs=pl.BlockSpec((tm, n_s), lambda j, i: (i, j)),
        out_shape=jax.ShapeDtypeStruct((m, ns * n_s), out_dtype),
        compiler_params=_params("parallel", "parallel"))(*args)


def mm_nt(a_list, w3, *, name, tko=None, add=None, out_dtype=f32):
    s_all, k_out, n_s = w3.shape
    m = a_list[0].shape[0]
    na = len(a_list)
    spa = s_all // na
    tko = k_out if tko is None else tko
    tm = _row_tile(m, tko * n_s * 2, na * n_s * _isz(a_list[0]) + tko * 4 * (1 if add is None else 2))

    def body(*refs):
        a_refs = refs[:na]
        w_ref = refs[na]
        o_ref = refs[-1]
        s = pl.program_id(2)

        @pl.when(s == 0)
        def _():
            if add is None:
                o_ref[...] = jnp.zeros_like(o_ref)
            else:
                o_ref[...] = refs[na + 1][...]

        for idx in range(na):
            @pl.when(s // spa == idx)
            def _(idx=idx):
                o_ref[...] += _dot_nt(a_refs[idx][...].astype(bf16), w_ref[0])

    def a_map(idx):
        return lambda ko, i, s: (i, jnp.clip(s - idx * spa, 0, spa - 1))

    in_specs = [pl.BlockSpec((tm, n_s), a_map(idx)) for idx in range(na)]
    in_specs.append(pl.BlockSpec((1, tko, n_s), lambda ko, i, s: (s, ko, 0)))
    args = list(a_list) + [w3]
    if add is not None:
        in_specs.append(pl.BlockSpec((tm, tko), lambda ko, i, s: (i, ko)))
        args.append(add)
    return pl.pallas_call(
        body, name=name, grid=(k_out // tko, m // tm, s_all), in_specs=in_specs,
        out_specs=pl.BlockSpec((tm, tko), lambda ko, i, s: (i, ko)),
        out_shape=jax.ShapeDtypeStruct((m, k_out), out_dtype),
        compiler_params=_params("parallel", "parallel", "arbitrary"))(*args)


def mm_tn(a, b_list, *, name, n_s, tki=None):
    m, k_in = a.shape
    na = len(b_list)
    s_all = sum(b.shape[1] for b in b_list) // n_s
    spa = s_all // na
    tki = k_in if tki is None else tki
    tm = _row_tile(m, tki * n_s * 4, tki * _isz(a) + na * n_s * _isz(b_list[0]))

    def body(*refs):
        a_ref = refs[0]
        b_refs = refs[1:1 + na]
        o_ref = refs[-1]
        s = pl.program_id(0)
        r = pl.program_id(2)

        @pl.when(r == 0)
        def _():
            o_ref[...] = jnp.zeros_like(o_ref)

        for idx in range(na):
            @pl.when(s // spa == idx)
            def _(idx=idx):
                o_ref[0] += _dot_tn(a_ref[...].astype(bf16), b_refs[idx][...].astype(bf16))

    def b_map(idx):
        def index(s, ki, r):
            active = (s // spa) == idx
            return (jnp.where(active, r, 0), jnp.clip(s - idx * spa, 0, spa - 1))
        return index

    in_specs = [pl.BlockSpec((tm, tki), lambda s, ki, r: (r, ki))]
    in_specs += [pl.BlockSpec((tm, n_s), b_map(idx)) for idx in range(na)]
    return pl.pallas_call(
        body, name=name, grid=(s_all, k_in // tki, m // tm), in_specs=in_specs,
        out_specs=pl.BlockSpec((1, tki, n_s), lambda s, ki, r: (s, ki, 0)),
        out_shape=jax.ShapeDtypeStruct((s_all, k_in, n_s), f32),
        compiler_params=_params("parallel", "parallel", "arbitrary"))(a, *b_list)


def mm_residual(a, w, x, *, name, gains=(), target=None):
    m, k = a.shape
    d = w.shape[1]
    ng = len(gains)
    tm = _row_tile(m, k * d * 2, k * _isz(a) + d * 4 * 3 + ng * d * 2)

    def body(*refs):
        a_ref, w_ref, x_ref = refs[:3]
        y = _dot(a_ref[...].astype(bf16), w_ref[...]) + x_ref[...]
        if target is None:
            g_refs = refs[3:3 + ng]
            y_ref = refs[3 + ng]
            h_refs = refs[4 + ng:4 + 2 * ng]
            r_ref = refs[-1]
            y_ref[...] = y
            r = lax.rsqrt(jnp.mean(y * y, axis=1, keepdims=True) + EPS)
            yh = y * r
            for g_ref, h_ref in zip(g_refs, h_refs):
                h_ref[...] = (yh * g_ref[...]).astype(bf16)
            r_ref[...] = r
        else:
            t_ref, dy_ref, s_ref = refs[3:]
            i = pl.program_id(0)
            e = y - t_ref[...]
            dy_ref[...] = e * (1.0 / d)
            part = jnp.sum(e * e, axis=0, keepdims=True)

            @pl.when(i == 0)
            def _():
                s_ref[...] = part

            @pl.when(i > 0)
            def _():
                s_ref[...] += part

    row = pl.BlockSpec((tm, d), lambda i: (i, 0))
    vec = pl.BlockSpec((1, d), lambda i: (0, 0))
    in_specs = [pl.BlockSpec((tm, k), lambda i: (i, 0)), pl.BlockSpec((k, d), lambda i: (0, 0)), row]
    if target is None:
        outs = pl.pallas_call(
            body, name=name, grid=(m // tm,), in_specs=in_specs + [vec] * ng,
            out_specs=[row] * (1 + ng) + [pl.BlockSpec((tm, 1), lambda i: (i, 0))],
            out_shape=[jax.ShapeDtypeStruct((m, d), f32)] + [jax.ShapeDtypeStruct((m, d), bf16)] * ng
            + [jax.ShapeDtypeStruct((m, 1), f32)],
            compiler_params=_params("parallel"))(a, w, x, *gains)
        return outs[0], outs[1:1 + ng], outs[-1]
    return pl.pallas_call(
        body, name=name, grid=(m // tm,), in_specs=in_specs + [row], out_specs=[row, vec],
        out_shape=[jax.ShapeDtypeStruct((m, d), f32), jax.ShapeDtypeStruct((1, d), f32)],
        compiler_params=_params("arbitrary"))(a, w, x, target)


def rms_fwd(x, gains, *, name, tr=512):
    t, d = x.shape
    tr = min(tr, t)
    ng = len(gains)

    def body(*refs):
        x_ref = refs[0]
        g_refs = refs[1:1 + ng]
        h_refs = refs[1 + ng:1 + 2 * ng]
        r_ref = refs[-1]
        xv = x_ref[...]
        r = lax.rsqrt(jnp.mean(xv * xv, axis=1, keepdims=True) + EPS)
        xh = xv * r
        for g_ref, h_ref in zip(g_refs, h_refs):
            h_ref[...] = (xh * g_ref[...]).astype(bf16)
        r_ref[...] = r

    row = pl.BlockSpec((tr, d), lambda i: (i, 0))
    vec = pl.BlockSpec((1, d), lambda i: (0, 0))
    outs = pl.pallas_call(
        body, name=name, grid=(t // tr,), in_specs=[row] + [vec] * ng,
        out_specs=[row] * ng + [pl.BlockSpec((tr, 1), lambda i: (i, 0))],
        out_shape=[jax.ShapeDtypeStruct((t, d), bf16)] * ng + [jax.ShapeDtypeStruct((t, 1), f32)],
        compiler_params=_params("parallel"))(x, *gains)
    return outs[:ng], outs[ng]


def rms_bwd(dh_list, x, r, gains, dx_in, *, name, tr=512):
    t, d = x.shape
    tr = min(tr, t)
    ng = len(gains)

    def body(*refs):
        dh_refs = refs[:ng]
        x_ref, r_ref = refs[ng], refs[ng + 1]
        g_refs = refs[ng + 2:2 * ng + 2]
        dxin_ref = refs[2 * ng + 2]
        dx_ref = refs[2 * ng + 3]
        dg_refs = refs[2 * ng + 4:]
        i = pl.program_id(0)
        rv = r_ref[...]
        xh = x_ref[...] * rv
        acc = dxin_ref[...]
        for dh_ref, g_ref, dg_ref in zip(dh_refs, g_refs, dg_refs):
            dh = dh_ref[...]
            part = jnp.sum(dh * xh, axis=0, keepdims=True)

            @pl.when(i == 0)
            def _(dg_ref=dg_ref, part=part):
                dg_ref[...] = part

            @pl.when(i > 0)
            def _(dg_ref=dg_ref, part=part):
                dg_ref[...] += part

            tg = dh * g_ref[...]
            acc = acc + rv * (tg - xh * jnp.mean(tg * xh, axis=1, keepdims=True))
        dx_ref[...] = acc

    row = pl.BlockSpec((tr, d), lambda i: (i, 0))
    vec = pl.BlockSpec((1, d), lambda i: (0, 0))
    outs = pl.pallas_call(
        body, name=name, grid=(t // tr,),
        in_specs=[row] * ng + [row, pl.BlockSpec((tr, 1), lambda i: (i, 0))] + [vec] * ng + [row],
        out_specs=[row] + [vec] * ng,
        out_shape=[jax.ShapeDtypeStruct((t, d), f32)] + [jax.ShapeDtypeStruct((1, d), f32)] * ng,
        compiler_params=_params("arbitrary"))(*dh_list, x, r, *gains, dx_in)
    return outs[0], outs[1:]


def sgu_gate_fwd(zu, zv, gv, wc, bt, *, name, tr=512):
    t, w = zu.shape
    tr = min(tr, t)
    groups = w // LANES

    def body(zu_ref, zv_ref, gv_ref, wc_ref, bt_ref, y_ref):
        vp = _gelu(zv_ref[...])
        rv = lax.rsqrt(jnp.mean(vp * vp, axis=1, keepdims=True) + EPS)
        vb = (vp * rv * gv_ref[...]).astype(bf16)
        for c in range(tr // CHUNK):
            rows = slice(c * CHUNK, (c + 1) * CHUNK)
            for g in range(groups):
                cols = slice(g * LANES, (g + 1) * LANES)
                sv = _dot(wc_ref[g], vb[rows, cols]) + bt_ref[:, g:g + 1]
                y_ref[rows, cols] = (_gelu(zu_ref[rows, cols]) * sv).astype(bf16)

    row = pl.BlockSpec((tr, w), lambda i: (i, 0))
    return pl.pallas_call(
        body, name=name, grid=(t // tr,),
        in_specs=[row, row, pl.BlockSpec((1, w), lambda i: (0, 0)),
                  pl.BlockSpec((groups, CHUNK, CHUNK), lambda i: (0, 0, 0)),
                  pl.BlockSpec((CHUNK, groups), lambda i: (0, 0))],
        out_specs=row, out_shape=jax.ShapeDtypeStruct((t, w), bf16),
        compiler_params=_params("parallel"))(zu, zv, gv, wc, bt)


def sgu_gate_bwd(zu, zv, dy, gv, wc, bt, *, name, tr=512):
    t, w = zu.shape
    tr = min(tr, t)
    groups = w // LANES
    nsteps = t // tr

    def body(zu_ref, zv_ref, dy_ref, gv_ref, wc_ref, bt_ref,
             dzu_ref, dzv_ref, dgv_ref, dws_ref, dbt_ref, dv_ref, bacc_ref):
        i = pl.program_id(0)

        @pl.when(i == 0)
        def _():
            dgv_ref[...] = jnp.zeros_like(dgv_ref)
            dws_ref[...] = jnp.zeros_like(dws_ref)
            bacc_ref[...] = jnp.zeros_like(bacc_ref)

        zvv = zv_ref[...]
        vp = _gelu(zvv)
        rv = lax.rsqrt(jnp.mean(vp * vp, axis=1, keepdims=True) + EPS)
        vhat = vp * rv
        vb = (vhat * gv_ref[...]).astype(bf16)
        for c in range(tr // CHUNK):
            rows = slice(c * CHUNK, (c + 1) * CHUNK)
            for g in range(groups):
                cols = slice(g * LANES, (g + 1) * LANES)
                vblk = vb[rows, cols]
                sv = _dot(wc_ref[g], vblk) + bt_ref[:, g:g + 1]
                zub = zu_ref[rows, cols]
                dyb = dy_ref[rows, cols]
                dzu_ref[rows, cols] = (dyb * sv * _gelu_grad(zub)).astype(bf16)
                dsv = dyb * _gelu(zub)
                bacc_ref[:, cols] += dsv
                dsvb = dsv.astype(bf16)
                dv_ref[rows, cols] = _dot_tn(wc_ref[g], dsvb)
                dws_ref[g] += _dot_nt(dsvb, vblk)
        dv = dv_ref[...]
        dgv_ref[...] += jnp.sum(dv * vhat, axis=0, keepdims=True)
        tg = dv * gv_ref[...]
        dvp = rv * (tg - vhat * jnp.mean(tg * vhat, axis=1, keepdims=True))
        dzv_ref[...] = (dvp * _gelu_grad(zvv)).astype(bf16)

        @pl.when(i == nsteps - 1)
        def _():
            tt = lax.broadcasted_iota(jnp.int32, (CHUNK, CHUNK), 0)
            ss = lax.broadcasted_iota(jnp.int32, (CHUNK, CHUNK), 1)
            for g in range(groups):
                dws_ref[g] = jnp.where(ss <= tt, dws_ref[g], 0.0)
                dbt_ref[:, g:g + 1] = jnp.sum(bacc_ref[:, g * LANES:(g + 1) * LANES], axis=1, keepdims=True)

    row = pl.BlockSpec((tr, w), lambda i: (i, 0))
    full3 = pl.BlockSpec((groups, CHUNK, CHUNK), lambda i: (0, 0, 0))
    return pl.pallas_call(
        body, name=name, grid=(nsteps,),
        in_specs=[row, row, row, pl.BlockSpec((1, w), lambda i: (0, 0)), full3,
                  pl.BlockSpec((CHUNK, groups), lambda i: (0, 0))],
        out_specs=[row, row, pl.BlockSpec((1, w), lambda i: (0, 0)), full3,
                   pl.BlockSpec((CHUNK, groups), lambda i: (0, 0))],
        out_shape=[jax.ShapeDtypeStruct((t, w), bf16), jax.ShapeDtypeStruct((t, w), bf16),
                   jax.ShapeDtypeStruct((1, w), f32), jax.ShapeDtypeStruct((groups, CHUNK, CHUNK), f32),
                   jax.ShapeDtypeStruct((CHUNK, groups), f32)],
        scratch_shapes=[pltpu.VMEM((tr, w), f32), pltpu.VMEM((CHUNK, w), f32)],
        compiler_params=_params("arbitrary"))(zu, zv, dy, gv, wc, bt)


HALO = 8


def _shift_down(v, halo, k, first):
    r = pltpu.roll(v, k, 0)
    hh = jnp.where(first, 0.0, pltpu.roll(halo, k, 0))
    rid = lax.broadcasted_iota(jnp.int32, (HALO, v.shape[1]), 0)
    head = jnp.where(rid < k, hh, r[0:HALO])
    if v.shape[0] == HALO:
        return head
    return jnp.concatenate([head, r[HALO:]], axis=0)


def _shift_up(v, halo, k, last):
    n = v.shape[0]
    r = pltpu.roll(v, n - k, 0)
    hh = jnp.where(last, 0.0, pltpu.roll(halo, HALO - k, 0))
    rid = lax.broadcasted_iota(jnp.int32, (HALO, v.shape[1]), 0)
    tail = jnp.where(rid >= HALO - k, hh, r[n - HALO:])
    return jnp.concatenate([r[:n - HALO], tail], axis=0)


def _conv(p, halo, w_ref, b_ref, first):
    return (w_ref[2:3, :] * p + w_ref[1:2, :] * _shift_down(p, halo, 1, first)
            + w_ref[0:1, :] * _shift_down(p, halo, 2, first) + b_ref[...])


BF16_ROWS = 16


def ffn_in_fused(h, w_in4, wg, wu, bg, bu, *, name):
    t, k = h.shape
    s_all, _, n_s = w_in4.shape
    half = s_all // 2
    tm = _row_tile(t, 2 * k * n_s * 2, k * 2 + 2 * n_s * 4 + n_s * 2)

    def body(h_ref, hh_ref, wg_ref, wu_ref, cg_ref, cu_ref, bg_ref, bu_ref, pg_ref, pu_ref, gate_ref, up_ref, a_ref):
        first = pl.program_id(1) == 0
        hv, hh = h_ref[...], hh_ref[...]
        outs = []
        for w_ref, c_ref, b_ref, p_ref, o_ref in ((wg_ref, cg_ref, bg_ref, pg_ref, gate_ref),
                                                  (wu_ref, cu_ref, bu_ref, pu_ref, up_ref)):
            p = _dot(hv, w_ref[0])
            p_ref[...] = p
            hu = _conv(p, _dot(hh, w_ref[0])[BF16_ROWS - HALO:], c_ref, b_ref, first)
            o_ref[...] = hu
            outs.append(hu)
        gate, up = outs
        a_ref[...] = (gate * jax.nn.sigmoid(gate) * up).astype(bf16)

    tile = pl.BlockSpec((tm, n_s), lambda j, i: (i, j))
    cw = pl.BlockSpec((3, n_s), lambda j, i: (0, j))
    cb = pl.BlockSpec((1, n_s), lambda j, i: (0, j))
    f = half * n_s
    return pl.pallas_call(
        body, name=name, grid=(half, t // tm),
        in_specs=[pl.BlockSpec((tm, k), lambda j, i: (i, 0)),
                  pl.BlockSpec((BF16_ROWS, k), lambda j, i: (jnp.maximum(i * (tm // BF16_ROWS) - 1, 0), 0)),
                  pl.BlockSpec((1, k, n_s), lambda j, i: (j, 0, 0)),
                  pl.BlockSpec((1, k, n_s), lambda j, i: (j + half, 0, 0)), cw, cw, cb, cb],
        out_specs=[tile] * 5,
        out_shape=[jax.ShapeDtypeStruct((t, f), f32)] * 4 + [jax.ShapeDtypeStruct((t, f), bf16)],
        compiler_params=_params("parallel", "parallel"))(h, h, w_in4, w_in4, wg, wu, bg, bu)


def _gate_grads(gate, up, dav):
    sg = jax.nn.sigmoid(gate)
    return dav * up * (sg * (1.0 + gate * (1.0 - sg))), dav * gate * sg


GATE_BWD_ROWS = 256


def ffn_gate_bwd(dy, w_out, pg, pu, gate, up, wg, wu, *, name):
    t, f = pg.shape
    d = dy.shape[1]
    tr = min(GATE_BWD_ROWS, t)
    nsteps = t // tr
    tc = f // 2

    def body(dy_ref, dyn_ref, w_ref, pg_ref, pu_ref, gate_ref, gaten_ref, up_ref, upn_ref, wg_ref, wu_ref,
             dg_ref, du_ref, sg_ref, su_ref):
        i = pl.program_id(1)
        last = i == nsteps - 1
        w = w_ref[0]
        da = _dot_nt(dy_ref[...].astype(bf16), w)
        da_n = _dot_nt(dyn_ref[...].astype(bf16), w)
        dgate, dup = _gate_grads(gate_ref[...], up_ref[...], da)
        dgate_n, dup_n = _gate_grads(gaten_ref[...], upn_ref[...], da_n)
        rid = lax.broadcasted_iota(jnp.int32, (8, tc), 0)
        for dd, d_n, c_ref, p_ref, o_ref, s_ref in ((dgate, dgate_n, wg_ref, pg_ref, dg_ref, sg_ref),
                                                    (dup, dup_n, wu_ref, pu_ref, du_ref, su_ref)):
            d1, d2 = _shift_up(dd, d_n, 1, last), _shift_up(dd, d_n, 2, last)
            o_ref[...] = (c_ref[2:3, :] * dd + c_ref[1:2, :] * d1 + c_ref[0:1, :] * d2).astype(bf16)
            p = p_ref[...]
            sums = [jnp.sum(d2 * p, axis=0, keepdims=True), jnp.sum(d1 * p, axis=0, keepdims=True),
                    jnp.sum(dd * p, axis=0, keepdims=True), jnp.sum(dd, axis=0, keepdims=True)]
            part = jnp.zeros((8, tc), f32)
            for k, sk in enumerate(sums):
                part = jnp.where(rid == k, sk, part)

            @pl.when(i == 0)
            def _(s_ref=s_ref, part=part):
                s_ref[...] = part

            @pl.when(i > 0)
            def _(s_ref=s_ref, part=part):
                s_ref[...] += part

    def nxt_rows(j, i):
        return (jnp.minimum((i + 1) * (tr // HALO), t // HALO - 1), j)

    tile = pl.BlockSpec((tr, tc), lambda j, i: (i, j))
    nxt = pl.BlockSpec((HALO, tc), nxt_rows)
    wspec = pl.BlockSpec((3, tc), lambda j, i: (0, j))
    stat = pl.BlockSpec((8, tc), lambda j, i: (0, j))
    return pl.pallas_call(
        body, name=name, grid=(2, nsteps),
        in_specs=[pl.BlockSpec((tr, d), lambda j, i: (i, 0)),
                  pl.BlockSpec((HALO, d), lambda j, i: (nxt_rows(j, i)[0], 0)),
                  pl.BlockSpec((1, tc, d), lambda j, i: (j, 0, 0)),
                  tile, tile, tile, nxt, tile, nxt, wspec, wspec],
        out_specs=[tile, tile, stat, stat],
        out_shape=[jax.ShapeDtypeStruct((t, f), bf16), jax.ShapeDtypeStruct((t, f), bf16),
                   jax.ShapeDtypeStruct((8, f), f32), jax.ShapeDtypeStruct((8, f), f32)],
        compiler_params=_params("parallel", "arbitrary"))(
            dy, dy, w_out.reshape(2, tc, d), pg, pu, gate, gate, up, up, wg, wu)


def _head_mean_matrix():
    i = lax.broadcasted_iota(jnp.int32, (LANES, LANES), 0) // HEAD_DIM
    j = lax.broadcasted_iota(jnp.int32, (LANES, LANES), 1) // HEAD_DIM
    return jnp.where(i == j, 1.0 / HEAD_DIM, 0.0).astype(f32)


def _lane_half(shape):
    return (lax.broadcasted_iota(jnp.int32, shape, 1) % LANES) // HEAD_DIM


def q_norm_fwd(qp, g2, *, name, scale, tr=512):
    t, w = qp.shape
    tr = min(tr, t)

    def body(x_ref, g_ref, o_ref):
        bd = _head_mean_matrix()
        for cb in range(w // LANES):
            cols = slice(cb * LANES, (cb + 1) * LANES)
            xc = x_ref[:, cols]
            rh = lax.rsqrt(_dot_exact(xc * xc, bd) + EPS)
            o_ref[:, cols] = (xc * rh * g_ref[...] * scale).astype(bf16)

    row = pl.BlockSpec((tr, w), lambda i: (i, 0))
    return pl.pallas_call(
        body, name=name, grid=(t // tr,), in_specs=[row, pl.BlockSpec((1, LANES), lambda i: (0, 0))],
        out_specs=row, out_shape=jax.ShapeDtypeStruct((t, w), bf16),
        compiler_params=_params("parallel"))(qp, g2)


def q_norm_bwd(dq, qp, g2, *, name, scale, tr=512):
    t, w = qp.shape
    tr = min(tr, t)

    def body(dq_ref, x_ref, g_ref, o_ref, dg_ref):
        i = pl.program_id(0)
        bd = _head_mean_matrix()
        acc = jnp.zeros((1, LANES), f32)
        for cb in range(w // LANES):
            cols = slice(cb * LANES, (cb + 1) * LANES)
            xc = x_ref[:, cols]
            rh = lax.rsqrt(_dot_exact(xc * xc, bd) + EPS)
            xh = xc * rh
            dy = dq_ref[:, cols] * scale
            acc = acc + jnp.sum(dy * xh, axis=0, keepdims=True)
            tg = dy * g_ref[...]
            o_ref[:, cols] = (rh * (tg - xh * _dot_exact(tg * xh, bd))).astype(bf16)

        @pl.when(i == 0)
        def _():
            dg_ref[...] = acc

        @pl.when(i > 0)
        def _():
            dg_ref[...] += acc

    row = pl.BlockSpec((tr, w), lambda i: (i, 0))
    vec = pl.BlockSpec((1, LANES), lambda i: (0, 0))
    return pl.pallas_call(
        body, name=name, grid=(t // tr,), in_specs=[row, row, vec], out_specs=[row, vec],
        out_shape=[jax.ShapeDtypeStruct((t, w), bf16), jax.ShapeDtypeStruct((1, LANES), f32)],
        compiler_params=_params("arbitrary"))(dq, qp, g2)


def kv_post_fwd(kv, g2, *, name, tr=512):
    t, w = kv.shape
    tr = min(tr, t)
    kw = w // 2

    def body(x_ref, g_ref, k_ref, v_ref):
        bd = _head_mean_matrix()
        half = _lane_half((tr, LANES))
        for cb in range(kw // LANES):
            xc = x_ref[:, cb * LANES:(cb + 1) * LANES]
            rh = lax.rsqrt(_dot_exact(xc * xc, bd) + EPS)
            kn = xc * rh * g_ref[...]
            vc = x_ref[:, kw + cb * LANES:kw + (cb + 1) * LANES]
            for src, dst in ((kn, k_ref), (vc, v_ref)):
                sw = pltpu.roll(src, HEAD_DIM, 1)
                for hf in range(2):
                    blk = 2 * cb + hf
                    dst[:, blk * LANES:(blk + 1) * LANES] = jnp.where(half == hf, src, sw).astype(bf16)

    return pl.pallas_call(
        body, name=name, grid=(t // tr,),
        in_specs=[pl.BlockSpec((tr, w), lambda i: (i, 0)), pl.BlockSpec((1, LANES), lambda i: (0, 0))],
        out_specs=[pl.BlockSpec((tr, 2 * kw), lambda i: (i, 0))] * 2,
        out_shape=[jax.ShapeDtypeStruct((t, 2 * kw), bf16)] * 2,
        compiler_params=_params("parallel"))(kv, g2)


def kv_post_bwd(dk2, dv2, kv, g2, *, name, tr=512):
    t, w = kv.shape
    tr = min(tr, t)
    kw = w // 2

    def body(dk_ref, dv_ref, x_ref, g_ref, o_ref, dg_ref):
        i = pl.program_id(0)
        bd = _head_mean_matrix()
        half = _lane_half((tr, LANES))
        acc = jnp.zeros((1, LANES), f32)

        def fold(ref, cb):
            a = ref[:, (2 * cb) * LANES:(2 * cb + 1) * LANES]
            b = ref[:, (2 * cb + 1) * LANES:(2 * cb + 2) * LANES]
            return jnp.where(half == 0, a + pltpu.roll(a, HEAD_DIM, 1), b + pltpu.roll(b, HEAD_DIM, 1))

        for cb in range(kw // LANES):
            cols = slice(cb * LANES, (cb + 1) * LANES)
            xc = x_ref[:, cols]
            rh = lax.rsqrt(_dot_exact(xc * xc, bd) + EPS)
            xh = xc * rh
            dy = fold(dk_ref, cb)
            acc = acc + jnp.sum(dy * xh, axis=0, keepdims=True)
            tg = dy * g_ref[...]
            o_ref[:, cols] = (rh * (tg - xh * _dot_exact(tg * xh, bd))).astype(bf16)
            o_ref[:, kw + cb * LANES:kw + (cb + 1) * LANES] = fold(dv_ref, cb).astype(bf16)

        @pl.when(i == 0)
        def _():
            dg_ref[...] = acc

        @pl.when(i > 0)
        def _():
            dg_ref[...] += acc

    dup = pl.BlockSpec((tr, 2 * kw), lambda i: (i, 0))
    row = pl.BlockSpec((tr, w), lambda i: (i, 0))
    vec = pl.BlockSpec((1, LANES), lambda i: (0, 0))
    return pl.pallas_call(
        body, name=name, grid=(t // tr,), in_specs=[dup, dup, row, vec], out_specs=[row, vec],
        out_shape=[jax.ShapeDtypeStruct((t, w), bf16), jax.ShapeDtypeStruct((1, LANES), f32)],
        compiler_params=_params("arbitrary"))(dk2, dv2, kv, g2)


def _slope(h):
    return 2.0 ** (-8.0 * (h + 1) / N_Q_HEADS)


GROUP_ROWS = Q_PER_KV * CHUNK


def _band_mask(n):
    tq = lax.broadcasted_iota(jnp.int32, (GROUP_ROWS, 2 * CHUNK), 0) % CHUNK
    jk = lax.broadcasted_iota(jnp.int32, (GROUP_ROWS, 2 * CHUNK), 1)
    dist = tq + CHUNK - jk
    ok = (dist >= 0) & (dist < CHUNK) & jnp.logical_not((n == 0) & (jk < CHUNK))
    return dist.astype(f32), ok


def _band(ref, n, kh):
    p0 = pl.multiple_of(jnp.maximum(n - 1, 0) * CHUNK, CHUNK)
    c0 = pl.multiple_of(n * CHUNK, CHUNK)
    cols = slice(kh * LANES, (kh + 1) * LANES)
    return jnp.concatenate([ref[pl.ds(p0, CHUNK), cols], ref[pl.ds(c0, CHUNK), cols]], axis=0)


def _stack_heads(ref, kh, half):
    parts = []
    for cb in (2 * kh, 2 * kh + 1):
        xc = ref[:, cb * LANES:(cb + 1) * LANES].astype(f32)
        parts += [jnp.where(half == hf, xc, 0.0).astype(bf16) for hf in range(2)]
    return jnp.concatenate(parts, axis=0)


def _unstack_heads(x4, half):
    return (jnp.where(half == 0, x4[0:CHUNK], x4[CHUNK:2 * CHUNK]),
            jnp.where(half == 0, x4[2 * CHUNK:3 * CHUNK], x4[3 * CHUNK:]))


def _per_head_column(kh, values):
    grp = lax.broadcasted_iota(jnp.int32, (GROUP_ROWS, 1), 0) // CHUNK
    col = jnp.full((GROUP_ROWS, 1), values[0], f32)
    for g in range(1, Q_PER_KV):
        col = jnp.where(grp == g, values[g], col)
    return col


def _softmax_band(q4, kband, dist, ok, slope, sink):
    s = _dot_nt(q4, kband)
    s = jnp.where(ok, s - slope * dist, -jnp.inf)
    m = jnp.maximum(jnp.max(s, axis=1, keepdims=True), sink)
    e = jnp.exp(s - m)
    es = jnp.exp(sink - m)
    den = jnp.sum(e, axis=1, keepdims=True) + es
    return e / den, es / den


def attn_fwd(q, k2, v2, sinks, *, name):
    t, w = q.shape
    nb = t // CHUNK

    def body(sink_ref, q_ref, k_ref, v_ref, o_ref):
        n = pl.program_id(0)
        dist, ok = _band_mask(n)
        half = _lane_half((CHUNK, LANES))
        for kh in range(N_KV_HEADS):
            heads = [Q_PER_KV * kh + g for g in range(Q_PER_KV)]
            slope = _per_head_column(kh, [_slope(h) for h in heads])
            sink = _per_head_column(kh, [sink_ref[h] for h in heads])
            q4 = _stack_heads(q_ref, kh, half)
            p, _ = _softmax_band(q4, _band(k_ref, n, kh), dist, ok, slope, sink)
            o4 = _dot(p.astype(bf16), _band(v_ref, n, kh))
            lo, hi = _unstack_heads(o4, half)
            o_ref[:, (2 * kh) * LANES:(2 * kh + 1) * LANES] = lo.astype(bf16)
            o_ref[:, (2 * kh + 1) * LANES:(2 * kh + 2) * LANES] = hi.astype(bf16)

    full = pl.BlockSpec((t, k2.shape[1]), lambda n: (0, 0))
    return pl.pallas_call(
        body, name=name, grid=(nb,),
        in_specs=[pl.BlockSpec(memory_space=pltpu.SMEM), pl.BlockSpec((CHUNK, w), lambda n: (n, 0)), full, full],
        out_specs=pl.BlockSpec((CHUNK, w), lambda n: (n, 0)),
        out_shape=jax.ShapeDtypeStruct((t, w), bf16),
        compiler_params=_params("parallel"))(sinks, q, k2, v2)


def attn_bwd(q, k2, v2, do, sinks, *, name):
    t, w = q.shape
    nb = t // CHUNK
    kw = k2.shape[1]

    def body(sink_ref, q_ref, k_ref, v_ref, do_ref, dq_ref, dk_ref, dv_ref, ds_ref, kc_ref, vc_ref):
        n = pl.program_id(0)

        @pl.when(n == 0)
        def _():
            ds_ref[...] = jnp.zeros_like(ds_ref)
            kc_ref[...] = jnp.zeros_like(kc_ref)
            vc_ref[...] = jnp.zeros_like(vc_ref)
            dk_ref[...] = jnp.zeros_like(dk_ref)
            dv_ref[...] = jnp.zeros_like(dv_ref)

        @pl.when(n == nb)
        def _():
            dk_ref[...] = kc_ref[...]
            dv_ref[...] = vc_ref[...]

        @pl.when(n < nb)
        def _():
            dist, ok = _band_mask(n)
            half = _lane_half((CHUNK, LANES))
            lane = lax.broadcasted_iota(jnp.int32, (1, LANES), 1)
            sink_acc = jnp.zeros((1, LANES), f32)
            for kh in range(N_KV_HEADS):
                heads = [Q_PER_KV * kh + g for g in range(Q_PER_KV)]
                slope = _per_head_column(kh, [_slope(h) for h in heads])
                sink = _per_head_column(kh, [sink_ref[h] for h in heads])
                q4 = _stack_heads(q_ref, kh, half)
                do4 = _stack_heads(do_ref, kh, half)
                kband = _band(k_ref, n, kh)
                vband = _band(v_ref, n, kh)
                p, ps = _softmax_band(q4, kband, dist, ok, slope, sink)
                dp = _dot_nt(do4, vband)
                delta = jnp.sum(p * dp, axis=1, keepdims=True)
                dsb = (p * (dp - delta)).astype(bf16)
                sd = ps * delta
                for g, h in enumerate(heads):
                    part = jnp.sum(sd[g * CHUNK:(g + 1) * CHUNK], axis=0, keepdims=True)
                    sink_acc = sink_acc + jnp.where(lane == h, -part, 0.0)
                lo, hi = _unstack_heads(_dot(dsb, kband), half)
                dq_ref[:, (2 * kh) * LANES:(2 * kh + 1) * LANES] = lo
                dq_ref[:, (2 * kh + 1) * LANES:(2 * kh + 2) * LANES] = hi
                dkb = _dot_tn(dsb, q4)
                dvb = _dot_tn(p.astype(bf16), do4)
                cols = slice(kh * LANES, (kh + 1) * LANES)
                dk_ref[:, cols] = kc_ref[:, cols] + dkb[0:CHUNK]
                dv_ref[:, cols] = vc_ref[:, cols] + dvb[0:CHUNK]
                kc_ref[:, cols] = dkb[CHUNK:]
                vc_ref[:, cols] = dvb[CHUNK:]
            ds_ref[...] += sink_acc

    full = pl.BlockSpec((t, kw), lambda n: (0, 0))
    qblk = pl.BlockSpec((CHUNK, w), lambda n: (jnp.minimum(n, nb - 1), 0))
    kblk = pl.BlockSpec((CHUNK, kw), lambda n: (jnp.maximum(n - 1, 0), 0))
    return pl.pallas_call(
        body, name=name, grid=(nb + 1,),
        in_specs=[pl.BlockSpec(memory_space=pltpu.SMEM), qblk, full, full, qblk],
        out_specs=[qblk, kblk, kblk, pl.BlockSpec((1, LANES), lambda n: (0, 0))],
        out_shape=[jax.ShapeDtypeStruct((t, w), f32), jax.ShapeDtypeStruct((t, kw), f32),
                   jax.ShapeDtypeStruct((t, kw), f32), jax.ShapeDtypeStruct((1, LANES), f32)],
        scratch_shapes=[pltpu.VMEM((CHUNK, kw), f32), pltpu.VMEM((CHUNK, kw), f32)],
        compiler_params=_params("arbitrary"))(sinks, q, k2, v2, do)


def loss_head(y, target, *, name, tr=512):
    t, d = y.shape
    tr = min(tr, t)

    def body(y_ref, t_ref, dy_ref, s_ref):
        i = pl.program_id(0)
        e = y_ref[...] - t_ref[...]
        dy_ref[...] = e * (1.0 / d)
        part = jnp.sum(e * e, axis=0, keepdims=True)

        @pl.when(i == 0)
        def _():
            s_ref[...] = part

        @pl.when(i > 0)
        def _():
            s_ref[...] += part

    row = pl.BlockSpec((tr, d), lambda i: (i, 0))
    vec = pl.BlockSpec((1, d), lambda i: (0, 0))
    return pl.pallas_call(
        body, name=name, grid=(t // tr,), in_specs=[row, row], out_specs=[row, vec],
        out_shape=[jax.ShapeDtypeStruct((t, d), f32), jax.ShapeDtypeStruct((1, d), f32)],
        compiler_params=_params("arbitrary"))(y, target)


N_STEPS = 8


def _row_blocks(shape):
    if len(shape) == 2:
        r, c = shape
        return (r // N_STEPS, c), (lambda s: (s, 0))
    l, r, c = shape
    per = N_STEPS // l
    return (1, r // per, c), (lambda s: (s // per, s % per, 0))


CAST_STEPS = 4


def cast_into_slot(arrays, k_arr, *, name):
    in_specs, out_specs, out_shape, layers = [], [], [], []
    for a in arrays:
        r, c = a.shape[-2:]
        rb = r // CAST_STEPS
        if a.ndim == 2:
            in_specs.append(pl.BlockSpec((rb, c), lambda s, k: (s, 0)))
            layers.append(None)
        else:
            for l in range(a.shape[0]):
                in_specs.append(pl.BlockSpec((1, rb, c), lambda s, k, l=l: (l, s, 0)))
                layers.append(l)
        for _ in range(1 if a.ndim == 2 else a.shape[0]):
            out_specs.append(pl.BlockSpec((1, rb, c), lambda s, k: (k[0], s, 0)))
            out_shape.append(jax.ShapeDtypeStruct((N_SHARDS, r, c), bf16))
    n = len(in_specs)

    def body(k_ref, *refs):
        for i_ref, o_ref, l in zip(refs[:n], refs[n:], layers):
            o_ref[0] = (i_ref[...] if l is None else i_ref[0]).astype(bf16)

    args = []
    for a in arrays:
        args += [a] * (1 if a.ndim == 2 else a.shape[0])
    return pl.pallas_call(
        body, name=name,
        grid_spec=pltpu.PrefetchScalarGridSpec(num_scalar_prefetch=1, grid=(CAST_STEPS,),
                                               in_specs=in_specs, out_specs=out_specs),
        out_shape=out_shape, compiler_params=_params("parallel"))(k_arr, *args)


def adamw(ws, gs, ms, vs, *, name):
    n = len(ws)
    specs, g_specs, g_count = [], [], []
    for w, g_list in zip(ws, gs):
        blk, index = _row_blocks(w.shape)
        specs.append(pl.BlockSpec(blk, index))
        layers = len(g_list)
        per = N_STEPS // layers
        g_count.append(layers)
        for l in range(layers):
            g_specs.append(pl.BlockSpec(blk[-2:], lambda s, l=l, per=per: (jnp.where(s // per == l, s % per, 0), 0)))
    ng = len(g_specs)

    def body(*refs):
        s = pl.program_id(0)
        g_refs = refs[3 * n:3 * n + ng]
        outs = refs[3 * n + ng:]
        off = 0
        for i in range(n):
            w_ref, m_ref, v_ref = refs[i], refs[n + i], refs[2 * n + i]
            go_ref, d_ref, nm_ref, nv_ref = (outs[k * n + i] for k in range(4))
            layers = g_count[i]
            g = g_refs[off][...]
            for l in range(1, layers):
                g = jnp.where(s // (N_STEPS // layers) == l, g_refs[off + l][...], g)
            off += layers
            g = g.reshape(w_ref.shape)
            m = ADAM_B1 * m_ref[...] + (1.0 - ADAM_B1) * g
            v = ADAM_B2 * v_ref[...] + (1.0 - ADAM_B2) * (g * g)
            m_hat = m / ADAM_C1
            v_hat = v / ADAM_C2
            go_ref[...] = g
            d_ref[...] = -ADAM_LR * (m_hat / (jnp.sqrt(v_hat) + ADAM_EPS) + ADAM_WD * w_ref[...])
            nm_ref[...] = m
            nv_ref[...] = v

    outs = pl.pallas_call(
        body, name=name, grid=(N_STEPS,), in_specs=specs * 3 + g_specs, out_specs=specs * 4,
        out_shape=[jax.ShapeDtypeStruct(a.shape, f32) for a in ws] * 4,
        compiler_params=_params("parallel"))(*ws, *ms, *vs, *[g for g_list in gs for g in g_list])
    return [outs[k * n:(k + 1) * n] for k in range(4)]


def _place():
    return lax.axis_index("x"), lax.axis_index("y"), lax.axis_index("c")


def gather_shards(bufs, *, name, split):
    n = len(bufs)

    def body(*refs):
        bufs_ = refs[:n]
        isend, irecv, dsend, drecv = refs[2 * n:]
        x, y, c = _place()
        k = 2 * x + y
        peers = [(1 - x, y, c), (x, 1 - y, c), (1 - x, 1 - y, c)]
        peer_k = [2 * (1 - x) + y, 2 * x + (1 - y), 2 * (1 - x) + (1 - y)]

        def slab(a, q, h):
            if not split[a]:
                return bufs_[a].at[q]
            half = bufs_[a].shape[1] // 2
            return bufs_[a].at[q, pl.ds(pl.multiple_of(h * half, 16), half)]

        def ici(a, j, q):
            return pltpu.make_async_remote_copy(
                src_ref=slab(a, q, c), dst_ref=slab(a, q, c), send_sem=isend.at[3 * a + j], recv_sem=irecv.at[3 * a + j],
                device_id=peers[j], device_id_type=MESH)

        def d2d(a, j, h):
            return pltpu.make_async_remote_copy(
                src_ref=slab(a, peer_k[j], h), dst_ref=slab(a, peer_k[j], h), send_sem=dsend.at[3 * a + j],
                recv_sem=drecv.at[3 * a + j], device_id=(x, y, 1 - c), device_id_type=MESH)

        for a in range(n):
            for j in range(3):
                ici(a, j, k).start()
        for a in range(n):
            for j in range(3):
                ici(a, j, peer_k[j]).wait_recv()
                if split[a]:
                    d2d(a, j, c).start()
        for a in range(n):
            for j in range(3):
                if split[a]:
                    d2d(a, j, 1 - c).wait_recv()
        for a in range(n):
            for j in range(3):
                ici(a, j, k).wait_send()
                if split[a]:
                    d2d(a, j, c).wait_send()

    return pl.pallas_call(
        body, name=name, in_specs=[ANY] * n, out_specs=[ANY] * n,
        out_shape=[jax.ShapeDtypeStruct(b.shape, b.dtype) for b in bufs],
        input_output_aliases={i: i for i in range(n)},
        scratch_shapes=[pltpu.SemaphoreType.DMA((3 * n,))] * 4)(*bufs)


HBM = pl.BlockSpec(memory_space=pltpu.HBM)
SEM = pl.BlockSpec(memory_space=pltpu.SEMAPHORE)
DATAFLOW = pltpu.SideEffectType.DATAFLOW_SIDE_EFFECTING


def _chip_peers():
    x, y, c = _place()
    return 2 * x + y, [(1 - x, y, c), (x, 1 - y, c), (1 - x, 1 - y, c)], [2 * (1 - x) + y, 2 * x + (1 - y), 2 * (1 - x) + (1 - y)]


def gather_start(bufs, groups, after, *, name):
    n = len(bufs)
    ng = len(groups)

    def body(*refs):
        ins = refs[:n]
        sends, recvs = refs[2 * n + 1:2 * n + 1 + ng], refs[2 * n + 1 + ng:2 * n + 1 + 2 * ng]
        token = refs[-1]
        k, peers, _ = _chip_peers()
        for gi, grp in enumerate(groups):
            for pos, a in enumerate(grp):
                for j in range(3):
                    pltpu.make_async_remote_copy(
                        src_ref=ins[a].at[k], dst_ref=ins[a].at[k], send_sem=sends[gi].at[3 * pos + j],
                        recv_sem=recvs[gi].at[3 * pos + j], device_id=peers[j], device_id_type=MESH).start()
        token[...] = jnp.zeros_like(token)

    sems = [pltpu.SemaphoreType.DMA((3 * len(grp),)) for grp in groups]
    outs = pl.pallas_call(
        body, name=name, in_specs=[HBM] * n + [ANY],
        out_specs=[HBM] * n + [SEM] * (2 * ng) + [pl.BlockSpec(memory_space=pltpu.VMEM)],
        out_shape=[pltpu.HBM(b.shape, b.dtype) for b in bufs] + sems + sems + [jax.ShapeDtypeStruct((8, LANES), f32)],
        input_output_aliases={i: i for i in range(n)},
        compiler_params=pltpu.CompilerParams(has_side_effects=DATAFLOW))(
            *[pltpu.with_memory_space_constraint(b, pltpu.HBM) for b in bufs], after)
    return outs[:n], outs[n:n + ng], outs[n + ng:n + 2 * ng], outs[-1]


def gather_wait(bufs, send_sems, recv_sems, after, *, name):
    n = len(bufs)

    def body(*refs):
        ins = refs[:n]
        send, recv = refs[n], refs[n + 1]
        k, peers, peer_k = _chip_peers()
        for a in range(n):
            for j in range(3):
                copy = pltpu.make_async_remote_copy(
                    src_ref=ins[a].at[k], dst_ref=ins[a].at[peer_k[j]], send_sem=send.at[3 * a + j],
                    recv_sem=recv.at[3 * a + j], device_id=peers[j], device_id_type=MESH)
                copy.wait_send()
                copy.wait_recv()

    return pl.pallas_call(
        body, name=name, in_specs=[HBM] * n + [SEM, SEM, ANY], out_specs=[HBM] * n,
        out_shape=[pltpu.HBM(b.shape, b.dtype) for b in bufs],
        input_output_aliases={i: i for i in range(n)},
        compiler_params=pltpu.CompilerParams(has_side_effects=DATAFLOW))(*bufs, send_sems, recv_sems, after)


def sibling_exchange(arrays, *, name):
    n = len(arrays)

    def body(*refs):
        ins, outs = refs[:n], refs[n:2 * n]
        send, recv = refs[2 * n:]
        x, y, c = _place()

        def copy(a):
            return pltpu.make_async_remote_copy(
                src_ref=ins[a], dst_ref=outs[a], send_sem=send.at[a], recv_sem=recv.at[a],
                device_id=(x, y, 1 - c), device_id_type=MESH)

        for a in range(n):
            copy(a).start()
        for a in range(n):
            copy(a).wait_recv()
        for a in range(n):
            copy(a).wait_send()

    return pl.pallas_call(
        body, name=name, in_specs=[ANY] * n, out_specs=[ANY] * n,
        out_shape=[jax.ShapeDtypeStruct(a.shape, a.dtype) for a in arrays],
        scratch_shapes=[pltpu.SemaphoreType.DMA((n,)), pltpu.SemaphoreType.DMA((n,))])(*arrays)


ALL_MASKS = [(mx, my, mc) for mx in (0, 1) for my in (0, 1) for mc in (0, 1)][1:]


def _scatter_copies(srcs, lands, ev, send, recv, esend, erecv):
    x, y, c = _place()
    me = 4 * x + 2 * y + c
    k, peers, peer_k = _chip_peers()
    out = []
    for a in range(len(srcs)):
        for j in range(3):
            out.append(pltpu.make_async_remote_copy(
                src_ref=srcs[a].at[peer_k[j]], dst_ref=lands[a].at[j], send_sem=send.at[3 * a + j],
                recv_sem=recv.at[3 * a + j], device_id=peers[j], device_id_type=MESH))
    start_ev, wait_ev = [], []
    if ev is not None:
        for j, (mx, my, mc) in enumerate(ALL_MASKS):
            peer = (x ^ mx, y ^ my, c ^ mc)
            start_ev.append(pltpu.make_async_remote_copy(
                src_ref=ev.at[me], dst_ref=ev.at[me], send_sem=esend.at[j], recv_sem=erecv.at[j],
                device_id=peer, device_id_type=MESH))
            wait_ev.append(pltpu.make_async_remote_copy(
                src_ref=ev.at[me], dst_ref=ev.at[me ^ (4 * mx + 2 * my + mc)], send_sem=esend.at[j],
                recv_sem=erecv.at[j], device_id=peer, device_id_type=MESH))
    return out, start_ev, wait_ev


def chip_scatter_start(arrays, everyone, after, *, name):
    n = len(arrays)
    ne = 0 if everyone is None else 1
    lands = [pltpu.with_memory_space_constraint(lax.empty((3,) + a.shape[1:], a.dtype), pltpu.HBM) for a in arrays]

    def body(*refs):
        srcs, lands_ = refs[:n], refs[n:2 * n]
        ev = refs[2 * n] if ne else None
        sems = refs[2 * n + ne + 1 + 2 * n + ne:-1]
        send, recv = sems[0], sems[1]
        esend, erecv = (sems[2], sems[3]) if ne else (None, None)
        copies, start_ev, _ = _scatter_copies(srcs, lands_, ev, send, recv, esend, erecv)
        for cp in start_ev + copies:
            cp.start()
        refs[-1][...] = jnp.zeros_like(refs[-1])

    sem_shapes = [pltpu.SemaphoreType.DMA((3 * n,))] * 2 + [pltpu.SemaphoreType.DMA((7,))] * (2 * ne)
    bufs = list(arrays) + lands + ([everyone] if ne else [])
    outs = pl.pallas_call(
        body, name=name, in_specs=[HBM] * len(bufs) + [ANY],
        out_specs=[HBM] * len(bufs) + [SEM] * len(sem_shapes) + [pl.BlockSpec(memory_space=pltpu.VMEM)],
        out_shape=[pltpu.HBM(b.shape, b.dtype) for b in bufs] + sem_shapes + [jax.ShapeDtypeStruct((8, LANES), f32)],
        input_output_aliases={i: i for i in range(len(bufs))},
        compiler_params=pltpu.CompilerParams(has_side_effects=DATAFLOW))(
            *[pltpu.with_memory_space_constraint(b, pltpu.HBM) for b in bufs], after)
    return (n, ne, outs[:-1]), outs[-1]


def chip_scatter_wait(state, after, *, name):
    n, ne, held = state
    nb = 2 * n + ne
    bufs, sems = held[:nb], held[nb:]

    def body(*refs):
        srcs, lands_ = refs[:n], refs[n:2 * n]
        ev = refs[2 * n] if ne else None
        sems_ = refs[nb:nb + len(sems)]
        esend, erecv = (sems_[2], sems_[3]) if ne else (None, None)
        copies, _, wait_ev = _scatter_copies(srcs, lands_, ev, sems_[0], sems_[1], esend, erecv)
        for cp in wait_ev + copies:
            cp.wait_send()
            cp.wait_recv()

    outs = pl.pallas_call(
        body, name=name, in_specs=[HBM] * nb + [SEM] * len(sems) + [ANY], out_specs=[HBM] * nb,
        out_shape=[pltpu.HBM(b.shape, b.dtype) for b in bufs],
        input_output_aliases={i: i for i in range(nb)},
        compiler_params=pltpu.CompilerParams(has_side_effects=DATAFLOW))(*bufs, *sems, after)
    return outs[n:2 * n], (outs[2 * n] if ne else None)


def sibling_merge(bufs, *, name):
    n = len(bufs)

    def body(*refs):
        bufs_ = refs[:n]
        send, recv = refs[2 * n:]
        x, y, c = _place()

        def copy(u, h):
            return pltpu.make_async_remote_copy(
                src_ref=bufs_[u].at[h], dst_ref=bufs_[u].at[h], send_sem=send.at[u], recv_sem=recv.at[u],
                device_id=(x, y, 1 - c), device_id_type=MESH)

        for u in range(n):
            copy(u, c).start()
        for u in range(n):
            copy(u, 1 - c).wait_recv()
        for u in range(n):
            copy(u, c).wait_send()

    return pl.pallas_call(
        body, name=name, in_specs=[ANY] * n, out_specs=[ANY] * n,
        out_shape=[jax.ShapeDtypeStruct(b.shape, b.dtype) for b in bufs],
        input_output_aliases={i: i for i in range(n)},
        scratch_shapes=[pltpu.SemaphoreType.DMA((n,)), pltpu.SemaphoreType.DMA((n,))])(*bufs)


def sum_leading(a, *, name):
    n, r, c = a.shape

    def body(a_ref, o_ref):
        acc = a_ref[0]
        for i in range(1, n):
            acc = acc + a_ref[i]
        o_ref[...] = acc

    rb = r // 2 if r % 16 == 0 else r
    return pl.pallas_call(
        body, name=name, grid=(r // rb,), in_specs=[pl.BlockSpec((n, rb, c), lambda i: (0, i, 0))],
        out_specs=pl.BlockSpec((rb, c), lambda i: (i, 0)), out_shape=jax.ShapeDtypeStruct((r, c), f32),
        compiler_params=_params("parallel"))(a)


def _half_rows(shape):
    return shape[1] // 2 // 2


def rs_cast_other_half(grads, c_arr, *, name):
    n = len(grads)

    def body(c_ref, *refs):
        for i_ref, o_ref in zip(refs[:n], refs[n:]):
            o_ref[...] = i_ref[...].astype(bf16)

    in_specs = [pl.BlockSpec((1, _half_rows(g.shape), g.shape[2]), lambda s, r, c_ref: (s, (1 - c_ref[0]) * 2 + r, 0))
                for g in grads]
    out_specs = [pl.BlockSpec((1, _half_rows(g.shape), g.shape[2]), lambda s, r, c_ref: (s, r, 0)) for g in grads]
    return pl.pallas_call(
        body, name=name,
        grid_spec=pltpu.PrefetchScalarGridSpec(num_scalar_prefetch=1, grid=(N_SHARDS, 2),
                                               in_specs=in_specs, out_specs=out_specs),
        out_shape=[jax.ShapeDtypeStruct((N_SHARDS, g.shape[1] // 2, g.shape[2]), bf16) for g in grads],
        compiler_params=_params("parallel", "parallel"))(c_arr, *grads)


def rs_add_sibling(grads, recvd, ck_arr, *, name):
    n = len(grads)

    def body(ck_ref, *refs):
        s = pl.program_id(1)
        for u in range(n):
            g_ref, r_ref = refs[u], refs[n + u]
            qb_ref, own_ref = refs[2 * n + u], refs[3 * n + u]
            q = g_ref[0] + r_ref[0].astype(f32)
            qb_ref[0] = q.astype(bf16)

            @pl.when(s == ck_ref[1])
            def _(own_ref=own_ref, q=q):
                own_ref[...] = q

    in_specs = [pl.BlockSpec((1, _half_rows(g.shape), g.shape[2]), lambda r, s, ck: (s, ck[0] * 2 + r, 0)) for g in grads]
    in_specs += [pl.BlockSpec((1, _half_rows(g.shape), g.shape[2]), lambda r, s, ck: (s, r, 0)) for g in grads]
    out_specs = [pl.BlockSpec((1, _half_rows(g.shape), g.shape[2]), lambda r, s, ck: (s, r, 0)) for g in grads]
    out_specs += [pl.BlockSpec((_half_rows(g.shape), g.shape[2]), lambda r, s, ck: (r, 0)) for g in grads]
    outs = pl.pallas_call(
        body, name=name,
        grid_spec=pltpu.PrefetchScalarGridSpec(num_scalar_prefetch=1, grid=(2, N_SHARDS),
                                               in_specs=in_specs, out_specs=out_specs),
        out_shape=[jax.ShapeDtypeStruct((N_SHARDS, g.shape[1] // 2, g.shape[2]), bf16) for g in grads]
        + [jax.ShapeDtypeStruct((g.shape[1] // 2, g.shape[2]), f32) for g in grads],
        compiler_params=_params("parallel", "arbitrary"))(ck_arr, *grads, *recvd)
    return outs[:n], outs[n:]


def rs_sum_chips(owns, recvd, ck_arr, *, name):
    n = len(owns)

    def body(ck_ref, *refs):
        for u in range(n):
            own_ref, r_ref, o_ref = refs[u], refs[n + u], refs[2 * n + u]
            o_ref[0] = ((own_ref[...] + r_ref[0].astype(f32)) + r_ref[1].astype(f32)) + r_ref[2].astype(f32)

    in_specs = [pl.BlockSpec((o.shape[0] // 2, o.shape[1]), lambda r, ck: (r, 0)) for o in owns]
    in_specs += [pl.BlockSpec((3, o.shape[0] // 2, o.shape[1]), lambda r, ck: (0, r, 0)) for o in owns]
    out_specs = [pl.BlockSpec((1, o.shape[0] // 2, o.shape[1]), lambda r, ck: (ck[0], r, 0)) for o in owns]
    return pl.pallas_call(
        body, name=name,
        grid_spec=pltpu.PrefetchScalarGridSpec(num_scalar_prefetch=1, grid=(2,), in_specs=in_specs, out_specs=out_specs),
        out_shape=[jax.ShapeDtypeStruct((2,) + o.shape, f32) for o in owns],
        compiler_params=_params("parallel"))(ck_arr, *owns, *recvd)


SMALL = ("a_norm", "a_v_norm", "a_w_s", "a_b_s", "f_norm", "f_conv_w", "f_conv_b", "kv_norm", "k_norm",
         "b_norm", "b_q_norm", "b_sinks")
BIG = ("a_w_in", "a_w_out", "f_w_in", "f_w_out", "w_kv", "b_w_q", "b_w_o")
PACK_COLS = 1024
PACK_ROWS = 8 * N_STEPS


def _pack(parts, rows=PACK_ROWS):
    flat = jnp.concatenate([p.reshape(-1).astype(f32) for p in parts])
    pad = (-flat.shape[0]) % (rows * PACK_COLS)
    return jnp.pad(flat, (0, pad)).reshape(-1, PACK_COLS)


def _unpack(packed, shapes):
    flat = packed.reshape(-1)
    out, off = [], 0
    for s in shapes:
        size = math.prod(s)
        out.append(flat[off:off + size].reshape(s))
        off += size
    return out


def _ffn_fwd(x, g, h, r, w_in4, conv_w, conv_b, f, tag):
    wg, wu = conv_w[:, :f], conv_w[:, f:]
    bg, bu = conv_b[None, :f], conv_b[None, f:]
    pg, pu, gate, up, a = ffn_in_fused(h, w_in4, wg, wu, bg, bu, name=f"ffn{tag}_in")
    return a, (x, g, h, r, pg, pu, gate, up, a, wg, wu)


def _ffn_bwd(dy, saved, w_in4, w_out, tag):
    x, g, h, r, pg, pu, gate, up, a, wg, wu = saved
    f = w_out.shape[0]
    d_w_out = mm_tn(a, [dy], name=f"ffn{tag}_dwout", n_s=w_out.shape[1], tki=f // 2)
    dpg, dpu, sg, su = ffn_gate_bwd(dy, w_out, pg, pu, gate, up, wg, wu, name=f"ffn{tag}_dgate")
    d_w_in = mm_tn(h, [dpg, dpu], name=f"ffn{tag}_dwin", n_s=w_in4.shape[2])
    dh = mm_nt([dpg, dpu], w_in4, name=f"ffn{tag}_dh")
    dx, (dg,) = rms_bwd([dh], x, r, [g], dy, name=f"ffn{tag}_dnorm")
    d_conv_w = jnp.concatenate([sg[0:3], su[0:3]], axis=1)
    d_conv_b = jnp.concatenate([sg[3], su[3]], axis=0)
    return dx, dg, d_w_in, d_conv_w, d_conv_b, d_w_out


def _rs_front(units, c_arr, tag):
    other_bf = rs_cast_other_half(units, c_arr, name=f"rs_cast{tag}")
    from_sib = sibling_exchange(list(other_bf), name=f"rs_sibling{tag}")
    return rs_add_sibling(units, from_sib, c_arr, name=f"rs_add{tag}")


def _rs_back(own, from_chips, c_arr, tag):
    halves = rs_sum_chips(list(own), list(from_chips), c_arr, name=f"rs_sum{tag}")
    return [m.reshape(-1, m.shape[2]) for m in sibling_merge(list(halves), name=f"rs_merge{tag}")]


def kernel(x, a_norm, a_w_in, a_v_norm, a_w_s, a_b_s, a_w_out, f_norm, f_w_in, f_conv_w, f_conv_b, f_w_out, kv_norm, w_kv, k_norm, b_norm, b_w_q, b_q_norm, b_sinks, b_w_o, loss_target, m_a_norm, m_a_w_in, m_a_v_norm, m_a_w_s, m_a_b_s, m_a_w_out, m_f_norm, m_f_w_in, m_f_conv_w, m_f_conv_b, m_f_w_out, m_kv_norm, m_w_kv, m_k_norm, m_b_norm, m_b_w_q, m_b_q_norm, m_b_sinks, m_b_w_o, v_a_norm, v_a_w_in, v_a_v_norm, v_a_w_s, v_a_b_s, v_a_w_out, v_f_norm, v_f_w_in, v_f_conv_w, v_f_conv_b, v_f_w_out, v_kv_norm, v_w_kv, v_k_norm, v_b_norm, v_b_w_q, v_b_q_norm, v_b_sinks, v_b_w_o):
    args = dict(locals())
    weights = {n: args[n] for n in SMALL + BIG}
    moms = {n: args["m_" + n] for n in SMALL + BIG}
    vars_ = {n: args["v_" + n] for n in SMALL + BIG}
    t, d = x.shape[1], x.shape[2]
    xi, yi, ci = _place()
    chip = 2 * xi + yi

    big_local = [a_w_in[0], a_w_out[0], f_w_in, f_w_out, w_kv, b_w_q[0], b_w_o[0]]
    c_arr = jnp.stack([ci, chip]).astype(jnp.int32)
    k_arr = jnp.stack([chip]).astype(jnp.int32)
    b_ain, b_aout, b_fin0, b_fin1, b_fout0, b_fout1, b_kv, b_q, b_o = cast_into_slot(big_local, k_arr, name="cast_weights")
    small_cols = _pack([a_norm, a_v_norm, f_conv_w], rows=8)
    b_small = lax.dynamic_update_slice(jnp.zeros((N_SHARDS,) + small_cols.shape, f32), small_cols[None], (chip, 0, 0))
    g_small, g_a_w_in, g_a_w_out = gather_shards([b_small, b_ain, b_aout], name="gather_first", split=[False, True, True])
    later, send_sems, recv_sems, token = gather_start([b_fin0, b_fout0, b_kv, b_q, b_o, b_fin1, b_fout1],
                                                      [[0], [1, 2, 3, 4], [5, 6]], g_small, name="gather_start")
    ns_cols = a_norm.shape[1]
    nf_cols = f_conv_w.shape[2]
    parts = [_unpack(g_small[k], [a_norm.shape, a_v_norm.shape, f_conv_w.shape]) for k in range(N_SHARDS)]
    a_norm_f = jnp.concatenate([p[0] for p in parts], axis=1) + token[0, 0]
    a_v_norm_f = jnp.concatenate([p[1] for p in parts], axis=1)
    conv_w_f = jnp.concatenate([p[2] for p in parts], axis=2)
    w_a_in = g_a_w_in
    w_a_out = g_a_w_out.reshape(1, -1, d)

    x0 = x[0]
    tril = jnp.tril(jnp.ones((CHUNK, CHUNK), dtype=bool))
    wc = jnp.where(tril[None], a_w_s[0], 0.0).astype(bf16)
    bt = a_b_s[0].T
    kg2 = jnp.tile(k_norm, 2)[None]
    qg2 = jnp.tile(b_q_norm[0], 2)[None]

    (h_a,), r_a = rms_fwd(x0, [a_norm_f], name="a_norm")
    zu = mm_nn(h_a, w_a_in, name="a_in_u", s0=0, ns=2)
    zv = mm_nn(h_a, w_a_in, name="a_in_v", s0=2, ns=2)
    y_a = sgu_gate_fwd(zu, zv, a_v_norm_f, wc, bt, name="a_gate")
    f = f_w_out.shape[1] * N_SHARDS
    x1, (h_f0,), r_f0 = mm_residual(y_a, w_a_out[0], x0, name="a_out", gains=[f_norm[0:1]])
    (g_fin0,) = gather_wait(later[0:1], send_sems[0], recv_sems[0], x1, name="gather_wait_0")
    w_f_in = [g_fin0, None]
    a0, ffn0 = _ffn_fwd(x1, f_norm[0:1], h_f0, r_f0, w_f_in[0], conv_w_f[0], f_conv_b[0], f, "0")
    g_fout0, g_w_kv, g_b_w_q, g_b_w_o = gather_wait(later[1:5], send_sems[1], recv_sems[1], a0, name="gather_wait_1")
    w_f_out = [g_fout0.reshape(-1, d), None]
    w_kv_f = g_w_kv.reshape(1, d, -1)
    w_q_f = g_b_w_q.reshape(1, d, -1)
    w_o_f = g_b_w_o.reshape(1, -1, d)
    x2, (h_k, h_q), r_b = mm_residual(a0, w_f_out[0], x1, name="ffn0_out", gains=[kv_norm[None], b_norm])
    kv = mm_nn(h_k, w_kv_f, name="kv_proj")
    k2, v2 = kv_post_fwd(kv, kg2, name="kv_post")
    qp = mm_nn(h_q, w_q_f, name="q_proj")
    qn = q_norm_fwd(qp, qg2, name="q_norm", scale=HEAD_DIM ** -0.5)
    o = attn_fwd(qn, k2, v2, b_sinks[0], name="attn")
    x3, (h_f1,), r_f1 = mm_residual(o, w_o_f[0], x2, name="o_proj", gains=[f_norm[1:2]])
    g_fin1, g_fout1 = gather_wait(later[5:7], send_sems[2], recv_sems[2], x3, name="gather_wait_2")
    w_f_in[1] = g_fin1
    w_f_out[1] = g_fout1.reshape(-1, d)
    a1, ffn1 = _ffn_fwd(x3, f_norm[1:2], h_f1, r_f1, w_f_in[1], conv_w_f[1], f_conv_b[1], f, "1")
    dx4, sq = mm_residual(a1, w_f_out[1], x3, name="ffn1_out", target=loss_target[0])
    loss_part = (0.5 * jnp.sum(sq) / d).reshape(1)

    dx3, d_fn1, d_fwin1, d_cw1, d_cb1, d_fwout1 = _ffn_bwd(dx4, ffn1, w_f_in[1], w_f_out[1], "1")
    do = mm_nt([dx3], w_o_f, name="o_proj_dx")
    d_w_o = mm_tn(o, [dx3], name="o_proj_dw", n_s=d)
    dqn, dk2, dv2, dsink = attn_bwd(qn, k2, v2, do, b_sinks[0], name="attn_bwd")
    dqp, dqg = q_norm_bwd(dqn, qp, qg2, name="q_norm_bwd", scale=HEAD_DIM ** -0.5)
    dkv, dkg = kv_post_bwd(dk2, dv2, kv, kg2, name="kv_post_bwd")
    d_w_q = mm_tn(h_q, [dqp], name="q_proj_dw", n_s=w_q_f.shape[2])
    dh_q = mm_nt([dqp], w_q_f, name="q_proj_dx")
    d_w_kv = mm_tn(h_k, [dkv], name="kv_proj_dw", n_s=w_kv_f.shape[2])
    dh_k = mm_nt([dkv], w_kv_f, name="kv_proj_dx")
    dx2, (d_kvn, d_bn) = rms_bwd([dh_k, dh_q], x2, r_b, [kv_norm[None], b_norm], dx3, name="b_norm_bwd")
    sh = N_SHARDS
    units1 = [d_fwin1, d_fwout1.reshape(sh, -1, d), d_w_kv.reshape(sh, -1, d_w_kv.shape[2]),
              d_w_q.reshape(sh, -1, d_w_q.shape[2]), d_w_o.reshape(sh, -1, d)]
    chip_bf1, own1 = _rs_front(units1, c_arr, "1")
    scatter1, token1 = chip_scatter_start(list(chip_bf1), None, dx2, name="rs_chips_start1")
    ffn0 = ffn0[:9] + (ffn0[9] + token1[0, 0],) + ffn0[10:]
    dx1, d_fn0, d_fwin0, d_cw0, d_cb0, d_fwout0 = _ffn_bwd(dx2, ffn0, w_f_in[0], w_f_out[0], "0")
    chip_bf2, own2 = _rs_front([d_fwin0, d_fwout0.reshape(sh, -1, d)], c_arr, "2")
    scatter2, token2 = chip_scatter_start(list(chip_bf2), None, dx1, name="rs_chips_start2")
    a_v_norm_f = a_v_norm_f + token2[0, 0]
    dy_a = mm_nt([dx1], w_a_out, name="a_out_dx")
    d_w_aout = mm_tn(y_a, [dx1], name="a_out_dw", n_s=d)
    dzu, dzv, d_avn, d_ws, d_bt = sgu_gate_bwd(zu, zv, dy_a, a_v_norm_f, wc, bt, name="a_gate_bwd")
    d_w_ain = mm_tn(h_a, [dzu, dzv], name="a_in_dw", n_s=w_a_in.shape[2])
    dh_a = mm_nt([dzu, dzv], w_a_in, name="a_in_dx")
    dx0, (d_an,) = rms_bwd([dh_a], x0, r_a, [a_norm_f], dx1, name="a_norm_bwd")
    grad_x = dx0[None]

    chip_bf3, own3 = _rs_front([d_w_ain, d_w_aout.reshape(sh, -1, d)], c_arr, "3")
    d_fn = jnp.concatenate([d_fn0, d_fn1], axis=0)
    d_cw = jnp.stack([d_cw0, d_cw1])
    d_cb = jnp.stack([d_cb0, d_cb1])
    d_kg = (dkg[0, :HEAD_DIM] + dkg[0, HEAD_DIM:])
    d_qg = (dqg[0, :HEAD_DIM] + dqg[0, HEAD_DIM:])[None]
    small_full = [d_an, d_avn, d_ws[None], d_bt.T[None], d_fn, d_cw, d_cb, d_kvn[0], d_kg, d_bn, d_qg,
                  dsink[:, :N_Q_HEADS], loss_part]
    packed = _pack(small_full)
    me = 4 * xi + 2 * yi + ci
    everyone = lax.dynamic_update_slice(lax.empty((N_DEV,) + packed.shape, f32), packed[None], (me, 0, 0))
    scatter3, _ = chip_scatter_start(list(chip_bf3), everyone, chip_bf3[0], name="rs_chips_start3")
    from_chips1, _ = chip_scatter_wait(scatter1, scatter3[2][0], name="rs_chips_wait1")
    fin1, fout1, gkv, gq, go = _rs_back(own1, from_chips1, c_arr, "1")
    from_chips2, _ = chip_scatter_wait(scatter2, fin1, name="rs_chips_wait2")
    fin0, fout0 = _rs_back(own2, from_chips2, c_arr, "2")
    late = ("f_w_in", "f_w_out", "w_kv", "b_w_q", "b_w_o")
    res_late = adamw([weights[n] for n in late], [[fin0, fin1], [fout0, fout1], [gkv], [gq], [go]],
                     [moms[n] for n in late], [vars_[n] for n in late], name="adamw_late")
    from_chips3, from_all = chip_scatter_wait(scatter3, res_late[1][2], name="rs_chips_wait3")
    ain, aout = _rs_back(own3, from_chips3, c_arr, "3")
    first = ("a_w_in", "a_w_out")
    res_first = adamw([weights[n] for n in first], [[ain], [aout]], [moms[n] for n in first],
                      [vars_[n] for n in first], name="adamw_first")
    big = {n: tuple(r[i] for r in res_late) for i, n in enumerate(late)}
    big.update({n: tuple(r[i] for r in res_first) for i, n in enumerate(first)})

    full_shapes = [g.shape for g in small_full]
    small_g = _unpack(sum_leading(from_all, name="small_sum"), full_shapes)
    loss = small_g.pop()[0]
    small_g[0] = lax.dynamic_slice_in_dim(small_g[0], chip * ns_cols, ns_cols, axis=1)
    small_g[1] = lax.dynamic_slice_in_dim(small_g[1], chip * ns_cols, ns_cols, axis=1)
    small_g[5] = lax.dynamic_slice_in_dim(small_g[5], chip * nf_cols, nf_cols, axis=2)
    small_shapes = [weights[n].shape for n in SMALL]
    small_g = [g.reshape(s) for g, s in zip(small_g, small_shapes)]
    pw, pg_, pm, pv = (_pack(v) for v in ([weights[n] for n in SMALL], small_g, [moms[n] for n in SMALL],
                                          [vars_[n] for n in SMALL]))
    _, (sd,), (sm,), (sv,) = adamw([pw], [[pg_]], [pm], [pv], name="adamw_small")
    small_d, small_m, small_v = (_unpack(v, small_shapes) for v in (sd, sm, sv))

    out = {}
    for i, n in enumerate(SMALL):
        out[n] = (small_g[i], small_d[i], small_m[i], small_v[i])
    out.update(big)
    order = ["a_norm", "a_w_in", "a_v_norm", "a_w_s", "a_b_s", "a_w_out", "f_norm", "f_w_in", "f_conv_w", "f_conv_b",
             "f_w_out", "kv_norm", "w_kv", "k_norm", "b_norm", "b_w_q", "b_q_norm", "b_sinks", "b_w_o"]
    return (loss, grad_x, *[out[n][0] for n in order], *[out[n][1] for n in order],
            *[out[n][2] for n in order], *[out[n][3] for n in order])
```

```python
import functools
import math

import jax
import jax.numpy as jnp
from jax import lax
from jax.experimental import pallas as pl
from jax.experimental.pallas import tpu as pltpu

f32 = jnp.float32
bf16 = jnp.bfloat16
MESH = pl.DeviceIdType.MESH
ANY = pl.BlockSpec(memory_space=pl.ANY)

EPS = 1e-6
LANES = 128
CHUNK = 128
HEAD_DIM = 64
N_Q_HEADS = 16
N_KV_HEADS = 4
Q_PER_KV = N_Q_HEADS // N_KV_HEADS
N_SHARDS = 4
N_DEV = 8

ADAM_LR = 0.001
ADAM_B1 = 0.9
ADAM_B2 = 0.999
ADAM_EPS = 1e-08
ADAM_WD = 0.01
ADAM_STEP = 10
ADAM_C1 = 1.0 - ADAM_B1 ** ADAM_STEP
ADAM_C2 = 1.0 - ADAM_B2 ** ADAM_STEP

_INV_SQRT2 = 1.0 / math.sqrt(2.0)
_INV_SQRT2PI = 1.0 / math.sqrt(2.0 * math.pi)


def _params(*sem):
    return pltpu.CompilerParams(dimension_semantics=sem)


def _gelu(z):
    return 0.5 * z * (1.0 + lax.erf(z * _INV_SQRT2))


def _gelu_grad(z):
    return 0.5 * (1.0 + lax.erf(z * _INV_SQRT2)) + z * jnp.exp(-0.5 * z * z) * _INV_SQRT2PI


def _dot(a, b):
    return jnp.dot(a, b, preferred_element_type=f32)


def _dot_nt(a, b):
    return lax.dot_general(a, b, (((1,), (1,)), ((), ())), preferred_element_type=f32)


def _dot_tn(a, b):
    return lax.dot_general(a, b, (((0,), (0,)), ((), ())), preferred_element_type=f32)


def _dot_exact(a, b):
    return jnp.dot(a, b, preferred_element_type=f32, precision=lax.Precision.HIGHEST)


VMEM_TILE_BUDGET = 36 * 1024 * 1024
MAX_ROW_TILE = 2048


def _row_tile(m, fixed_bytes, row_bytes):
    tm = min(m, MAX_ROW_TILE)
    while tm > 256 and 2 * (fixed_bytes + tm * row_bytes) > VMEM_TILE_BUDGET:
        tm //= 2
    return tm


def _isz(a):
    return jnp.dtype(a.dtype).itemsize


def mm_nn(a, w3, *, name, s0=0, ns=None, add=None, out_dtype=f32):
    m, k = a.shape
    s_all, _, n_s = w3.shape
    ns = s_all if ns is None else ns
    tm = _row_tile(m, k * n_s * 2, k * _isz(a) + n_s * jnp.dtype(out_dtype).itemsize + (0 if add is None else n_s * 4))

    def body(*refs):
        if add is None:
            a_ref, w_ref, o_ref = refs
            acc = _dot(a_ref[...].astype(bf16), w_ref[0])
        else:
            a_ref, w_ref, add_ref, o_ref = refs
            acc = _dot(a_ref[...].astype(bf16), w_ref[0]) + add_ref[...]
        o_ref[...] = acc.astype(out_dtype)

    in_specs = [pl.BlockSpec((tm, k), lambda j, i: (i, 0)),
                pl.BlockSpec((1, k, n_s), lambda j, i: (s0 + j, 0, 0))]
    args = [a, w3]
    if add is not None:
        in_specs.append(pl.BlockSpec((tm, n_s), lambda j, i: (i, j)))
        args.append(add)
    return pl.pallas_call(
        body, name=name, grid=(ns, m // tm), in_specs=in_specs,
        out_specs=pl.BlockSpec((tm, n_s), lambda j, i: (i, j)),
        out_shape=jax.ShapeDtypeStruct((m, ns * n_s), out_dtype),
        compiler_params=_params("parallel", "parallel"))(*args)


def mm_nt(a_list, w3, *, name, tko=None, add=None, out_dtype=f32):
    s_all, k_out, n_s = w3.shape
    m = a_list[0].shape[0]
    na = len(a_list)
    spa = s_all // na
    tko = k_out if tko is None else tko
    tm = _row_tile(m, tko * n_s * 2, na * n_s * _isz(a_list[0]) + tko * 4 * (1 if add is None else 2))

    def body(*refs):
        a_refs = refs[:na]
        w_ref = refs[na]
        o_ref = refs[-1]
        s = pl.program_id(2)

        @pl.when(s == 0)
        def _():
            if add is None:
                o_ref[...] = jnp.zeros_like(o_ref)
            else:
                o_ref[...] = refs[na + 1][...]

        for idx in range(na):
            @pl.when(s // spa == idx)
            def _(idx=idx):
                o_ref[...] += _dot_nt(a_refs[idx][...].astype(bf16), w_ref[0])

    def a_map(idx):
        return lambda ko, i, s: (i, jnp.clip(s - idx * spa, 0, spa - 1))

    in_specs = [pl.BlockSpec((tm, n_s), a_map(idx)) for idx in range(na)]
    in_specs.append(pl.BlockSpec((1, tko, n_s), lambda ko, i, s: (s, ko, 0)))
    args = list(a_list) + [w3]
    if add is not None:
        in_specs.append(pl.BlockSpec((tm, tko), lambda ko, i, s: (i, ko)))
        args.append(add)
    return pl.pallas_call(
        body, name=name, grid=(k_out // tko, m // tm, s_all), in_specs=in_specs,
        out_specs=pl.BlockSpec((tm, tko), lambda ko, i, s: (i, ko)),
        out_shape=jax.ShapeDtypeStruct((m, k_out), out_dtype),
        compiler_params=_params("parallel", "parallel", "arbitrary"))(*args)


def mm_tn(a, b_list, *, name, n_s, tki=None):
    m, k_in = a.shape
    na = len(b_list)
    s_all = sum(b.shape[1] for b in b_list) // n_s
    spa = s_all // na
    tki = k_in if tki is None else tki
    tm = _row_tile(m, tki * n_s * 4, tki * _isz(a) + na * n_s * _isz(b_list[0]))

    def body(*refs):
        a_ref = refs[0]
        b_refs = refs[1:1 + na]
        o_ref = refs[-1]
        s = pl.program_id(0)
        r = pl.program_id(2)

        @pl.when(r == 0)
        def _():
            o_ref[...] = jnp.zeros_like(o_ref)

        for idx in range(na):
            @pl.when(s // spa == idx)
            def _(idx=idx):
                o_ref[0] += _dot_tn(a_ref[...].astype(bf16), b_refs[idx][...].astype(bf16))

    def b_map(idx):
        def index(s, ki, r):
            active = (s // spa) == idx
            return (jnp.where(active, r, 0), jnp.clip(s - idx * spa, 0, spa - 1))
        return index

    in_specs = [pl.BlockSpec((tm, tki), lambda s, ki, r: (r, ki))]
    in_specs += [pl.BlockSpec((tm, n_s), b_map(idx)) for idx in range(na)]
    return pl.pallas_call(
        body, name=name, grid=(s_all, k_in // tki, m // tm), in_specs=in_specs,
        out_specs=pl.BlockSpec((1, tki, n_s), lambda s, ki, r: (s, ki, 0)),
        out_shape=jax.ShapeDtypeStruct((s_all, k_in, n_s), f32),
        compiler_params=_params("parallel", "parallel", "arbitrary"))(a, *b_list)


def mm_residual(a, w, x, *, name, gains=(), target=None):
    m, k = a.shape
    d = w.shape[1]
    ng = len(gains)
    tm = _row_tile(m, k * d * 2, k * _isz(a) + d * 4 * 3 + ng * d * 2)

    def body(*refs):
        a_ref, w_ref, x_ref = refs[:3]
        y = _dot(a_ref[...].astype(bf16), w_ref[...]) + x_ref[...]
        if target is None:
            g_refs = refs[3:3 + ng]
            y_ref = refs[3 + ng]
            h_refs = refs[4 + ng:4 + 2 * ng]
            r_ref = refs[-1]
            y_ref[...] = y
            r = lax.rsqrt(jnp.mean(y * y, axis=1, keepdims=True) + EPS)
            yh = y * r
            for g_ref, h_ref in zip(g_refs, h_refs):
                h_ref[...] = (yh * g_ref[...]).astype(bf16)
            r_ref[...] = r
        else:
            t_ref, dy_ref, s_ref = refs[3:]
            i = pl.program_id(0)
            e = y - t_ref[...]
            dy_ref[...] = e * (1.0 / d)
            part = jnp.sum(e * e, axis=0, keepdims=True)

            @pl.when(i == 0)
            def _():
                s_ref[...] = part

            @pl.when(i > 0)
            def _():
                s_ref[...] += part

    row = pl.BlockSpec((tm, d), lambda i: (i, 0))
    vec = pl.BlockSpec((1, d), lambda i: (0, 0))
    in_specs = [pl.BlockSpec((tm, k), lambda i: (i, 0)), pl.BlockSpec((k, d), lambda i: (0, 0)), row]
    if target is None:
        outs = pl.pallas_call(
            body, name=name, grid=(m // tm,), in_specs=in_specs + [vec] * ng,
            out_specs=[row] * (1 + ng) + [pl.BlockSpec((tm, 1), lambda i: (i, 0))],
            out_shape=[jax.ShapeDtypeStruct((m, d), f32)] + [jax.ShapeDtypeStruct((m, d), bf16)] * ng
            + [jax.ShapeDtypeStruct((m, 1), f32)],
            compiler_params=_params("parallel"))(a, w, x, *gains)
        return outs[0], outs[1:1 + ng], outs[-1]
    return pl.pallas_call(
        body, name=name, grid=(m // tm,), in_specs=in_specs + [row], out_specs=[row, vec],
        out_shape=[jax.ShapeDtypeStruct((m, d), f32), jax.ShapeDtypeStruct((1, d), f32)],
        compiler_params=_params("arbitrary"))(a, w, x, target)


def rms_fwd(x, gains, *, name, tr=512):
    t, d = x.shape
    tr = min(tr, t)
    ng = len(gains)

    def body(*refs):
        x_ref = refs[0]
        g_refs = refs[1:1 + ng]
        h_refs = refs[1 + ng:1 + 2 * ng]
        r_ref = refs[-1]
        xv = x_ref[...]
        r = lax.rsqrt(jnp.mean(xv * xv, axis=1, keepdims=True) + EPS)
        xh = xv * r
        for g_ref, h_ref in zip(g_refs, h_refs):
            h_ref[...] = (xh * g_ref[...]).astype(bf16)
        r_ref[...] = r

    row = pl.BlockSpec((tr, d), lambda i: (i, 0))
    vec = pl.BlockSpec((1, d), lambda i: (0, 0))
    outs = pl.pallas_call(
        body, name=name, grid=(t // tr,), in_specs=[row] + [vec] * ng,
        out_specs=[row] * ng + [pl.BlockSpec((tr, 1), lambda i: (i, 0))],
        out_shape=[jax.ShapeDtypeStruct((t, d), bf16)] * ng + [jax.ShapeDtypeStruct((t, 1), f32)],
        compiler_params=_params("parallel"))(x, *gains)
    return outs[:ng], outs[ng]


def rms_bwd(dh_list, x, r, gains, dx_in, *, name, tr=512):
    t, d = x.shape
    tr = min(tr, t)
    ng = len(gains)

    def body(*refs):
        dh_refs = refs[:ng]
        x_ref, r_ref = refs[ng], refs[ng + 1]
        g_refs = refs[ng + 2:2 * ng + 2]
        dxin_ref = refs[2 * ng + 2]
        dx_ref = refs[2 * ng + 3]
        dg_refs = refs[2 * ng + 4:]
        i = pl.program_id(0)
        rv = r_ref[...]
        xh = x_ref[...] * rv
        acc = dxin_ref[...]
        for dh_ref, g_ref, dg_ref in zip(dh_refs, g_refs, dg_refs):
            dh = dh_ref[...]
            part = jnp.sum(dh * xh, axis=0, keepdims=True)

            @pl.when(i == 0)
            def _(dg_ref=dg_ref, part=part):
                dg_ref[...] = part

            @pl.when(i > 0)
            def _(dg_ref=dg_ref, part=part):
                dg_ref[...] += part

            tg = dh * g_ref[...]
            acc = acc + rv * (tg - xh * jnp.mean(tg * xh, axis=1, keepdims=True))
        dx_ref[...] = acc

    row = pl.BlockSpec((tr, d), lambda i: (i, 0))
    vec = pl.BlockSpec((1, d), lambda i: (0, 0))
    outs = pl.pallas_call(
        body, name=name, grid=(t // tr,),
        in_specs=[row] * ng + [row, pl.BlockSpec((tr, 1), lambda i: (i, 0))] + [vec] * ng + [row],
        out_specs=[row] + [vec] * ng,
        out_shape=[jax.ShapeDtypeStruct((t, d), f32)] + [jax.ShapeDtypeStruct((1, d), f32)] * ng,
        compiler_params=_params("arbitrary"))(*dh_list, x, r, *gains, dx_in)
    return outs[0], outs[1:]


def sgu_gate_fwd(zu, zv, gv, wc, bt, *, name, tr=512):
    t, w = zu.shape
    tr = min(tr, t)
    groups = w // LANES

    def body(zu_ref, zv_ref, gv_ref, wc_ref, bt_ref, y_ref):
        vp = _gelu(zv_ref[...])
        rv = lax.rsqrt(jnp.mean(vp * vp, axis=1, keepdims=True) + EPS)
        vb = (vp * rv * gv_ref[...]).astype(bf16)
        for c in range(tr // CHUNK):
            rows = slice(c * CHUNK, (c + 1) * CHUNK)
            for g in range(groups):
                cols = slice(g * LANES, (g + 1) * LANES)
                sv = _dot(wc_ref[g], vb[rows, cols]) + bt_ref[:, g:g + 1]
                y_ref[rows, cols] = (_gelu(zu_ref[rows, cols]) * sv).astype(bf16)

    row = pl.BlockSpec((tr, w), lambda i: (i, 0))
    return pl.pallas_call(
        body, name=name, grid=(t // tr,),
        in_specs=[row, row, pl.BlockSpec((1, w), lambda i: (0, 0)),
                  pl.BlockSpec((groups, CHUNK, CHUNK), lambda i: (0, 0, 0)),
                  pl.BlockSpec((CHUNK, groups), lambda i: (0, 0))],
        out_specs=row, out_shape=jax.ShapeDtypeStruct((t, w), bf16),
        compiler_params=_params("parallel"))(zu, zv, gv, wc, bt)


def sgu_gate_bwd(zu, zv, dy, gv, wc, bt, *, name, tr=512):
    t, w = zu.shape
    tr = min(tr, t)
    groups = w // LANES
    nsteps = t // tr

    def body(zu_ref, zv_ref, dy_ref, gv_ref, wc_ref, bt_ref,
             dzu_ref, dzv_ref, dgv_ref, dws_ref, dbt_ref, dv_ref, bacc_ref):
        i = pl.program_id(0)

        @pl.when(i == 0)
        def _():
            dgv_ref[...] = jnp.zeros_like(dgv_ref)
            dws_ref[...] = jnp.zeros_like(dws_ref)
            bacc_ref[...] = jnp.zeros_like(bacc_ref)

        zvv = zv_ref[...]
        vp = _gelu(zvv)
        rv = lax.rsqrt(jnp.mean(vp * vp, axis=1, keepdims=True) + EPS)
        vhat = vp * rv
        vb = (vhat * gv_ref[...]).astype(bf16)
        for c in range(tr // CHUNK):
            rows = slice(c * CHUNK, (c + 1) * CHUNK)
            for g in range(groups):
                cols = slice(g * LANES, (g + 1) * LANES)
                vblk = vb[rows, cols]
                sv = _dot(wc_ref[g], vblk) + bt_ref[:, g:g + 1]
                zub = zu_ref[rows, cols]
                dyb = dy_ref[rows, cols]
                dzu_ref[rows, cols] = (dyb * sv * _gelu_grad(zub)).astype(bf16)
                dsv = dyb * _gelu(zub)
                bacc_ref[:, cols] += dsv
                dsvb = dsv.astype(bf16)
                dv_ref[rows, cols] = _dot_tn(wc_ref[g], dsvb)
                dws_ref[g] += _dot_nt(dsvb, vblk)
        dv = dv_ref[...]
        dgv_ref[...] += jnp.sum(dv * vhat, axis=0, keepdims=True)
        tg = dv * gv_ref[...]
        dvp = rv * (tg - vhat * jnp.mean(tg * vhat, axis=1, keepdims=True))
        dzv_ref[...] = (dvp * _gelu_grad(zvv)).astype(bf16)

        @pl.when(i == nsteps - 1)
        def _():
            tt = lax.broadcasted_iota(jnp.int32, (CHUNK, CHUNK), 0)
            ss = lax.broadcasted_iota(jnp.int32, (CHUNK, CHUNK), 1)
            for g in range(groups):
                dws_ref[g] = jnp.where(ss <= tt, dws_ref[g], 0.0)
                dbt_ref[:, g:g + 1] = jnp.sum(bacc_ref[:, g * LANES:(g + 1) * LANES], axis=1, keepdims=True)

    row = pl.BlockSpec((tr, w), lambda i: (i, 0))
    full3 = pl.BlockSpec((groups, CHUNK, CHUNK), lambda i: (0, 0, 0))
    return pl.pallas_call(
        body, name=name, grid=(nsteps,),
        in_specs=[row, row, row, pl.BlockSpec((1, w), lambda i: (0, 0)), full3,
                  pl.BlockSpec((CHUNK, groups), lambda i: (0, 0))],
        out_specs=[row, row, pl.BlockSpec((1, w), lambda i: (0, 0)), full3,
                   pl.BlockSpec((CHUNK, groups), lambda i: (0, 0))],
        out_shape=[jax.ShapeDtypeStruct((t, w), bf16), jax.ShapeDtypeStruct((t, w), bf16),
                   jax.ShapeDtypeStruct((1, w), f32), jax.ShapeDtypeStruct((groups, CHUNK, CHUNK), f32),
                   jax.ShapeDtypeStruct((CHUNK, groups), f32)],
        scratch_shapes=[pltpu.VMEM((tr, w), f32), pltpu.VMEM((CHUNK, w), f32)],
        compiler_params=_params("arbitrary"))(zu, zv, dy, gv, wc, bt)


HALO = 8


def _shift_down(v, halo, k, first):
    r = pltpu.roll(v, k, 0)
    hh = jnp.where(first, 0.0, pltpu.roll(halo, k, 0))
    rid = lax.broadcasted_iota(jnp.int32, (HALO, v.shape[1]), 0)
    head = jnp.where(rid < k, hh, r[0:HALO])
    if v.shape[0] == HALO:
        return head
    return jnp.concatenate([head, r[HALO:]], axis=0)


def _shift_up(v, halo, k, last):
    n = v.shape[0]
    r = pltpu.roll(v, n - k, 0)
    hh = jnp.where(last, 0.0, pltpu.roll(halo, HALO - k, 0))
    rid = lax.broadcasted_iota(jnp.int32, (HALO, v.shape[1]), 0)
    tail = jnp.where(rid >= HALO - k, hh, r[n - HALO:])
    return jnp.concatenate([r[:n - HALO], tail], axis=0)


def _conv(p, halo, w_ref, b_ref, first):
    return (w_ref[2:3, :] * p + w_ref[1:2, :] * _shift_down(p, halo, 1, first)
            + w_ref[0:1, :] * _shift_down(p, halo, 2, first) + b_ref[...])


BF16_ROWS = 16


def ffn_in_fused(h, w_in4, wg, wu, bg, bu, *, name):
    t, k = h.shape
    s_all, _, n_s = w_in4.shape
    half = s_all // 2
    tm = _row_tile(t, 2 * k * n_s * 2, k * 2 + 2 * n_s * 4 + 5 * n_s * 2)

    def body(h_ref, hh_ref, wg_ref, wu_ref, cg_ref, cu_ref, bg_ref, bu_ref, pg_ref, pu_ref, gate_ref, up_ref, a_ref):
        first = pl.program_id(1) == 0
        hv, hh = h_ref[...], hh_ref[...]
        outs = []
        for w_ref, c_ref, b_ref, p_ref, o_ref in ((wg_ref, cg_ref, bg_ref, pg_ref, gate_ref),
                                                  (wu_ref, cu_ref, bu_ref, pu_ref, up_ref)):
            p = _dot(hv, w_ref[0])
            p_ref[...] = p.astype(bf16)
            hu = _conv(p, _dot(hh, w_ref[0])[BF16_ROWS - HALO:], c_ref, b_ref, first)
            o_ref[...] = hu.astype(bf16)
            outs.append(hu)
        gate, up = outs
        a_ref[...] = (gate * jax.nn.sigmoid(gate) * up).astype(bf16)

    tile = pl.BlockSpec((tm, n_s), lambda j, i: (i, j))
    cw = pl.BlockSpec((3, n_s), lambda j, i: (0, j))
    cb = pl.BlockSpec((1, n_s), lambda j, i: (0, j))
    f = half * n_s
    return pl.pallas_call(
        body, name=name, grid=(half, t // tm),
        in_specs=[pl.BlockSpec((tm, k), lambda j, i: (i, 0)),
                  pl.BlockSpec((BF16_ROWS, k), lambda j, i: (jnp.maximum(i * (tm // BF16_ROWS) - 1, 0), 0)),
                  pl.BlockSpec((1, k, n_s), lambda j, i: (j, 0, 0)),
                  pl.BlockSpec((1, k, n_s), lambda j, i: (j + half, 0, 0)), cw, cw, cb, cb],
        out_specs=[tile] * 5,
        out_shape=[jax.ShapeDtypeStruct((t, f), bf16)] * 5,
        compiler_params=_params("parallel", "parallel"))(h, h, w_in4, w_in4, wg, wu, bg, bu)


def _gate_grads(gate, up, dav):
    sg = jax.nn.sigmoid(gate)
    return dav * up * (sg * (1.0 + gate * (1.0 - sg))), dav * gate * sg


GATE_BWD_ROWS = 256


def ffn_gate_bwd(dy, w_out, pg, pu, gate, up, wg, wu, *, name):
    t, f = pg.shape
    d = dy.shape[1]
    tr = min(GATE_BWD_ROWS, t)
    nsteps = t // tr
    tc = f // 2

    def body(dy_ref, dyn_ref, w_ref, pg_ref, pu_ref, gate_ref, gaten_ref, up_ref, upn_ref, wg_ref, wu_ref,
             dg_ref, du_ref, sg_ref, su_ref):
        i = pl.program_id(1)
        last = i == nsteps - 1
        w = w_ref[0]
        da = _dot_nt(dy_ref[...].astype(bf16), w)
        da_n = _dot_nt(dyn_ref[...].astype(bf16), w)
        dgate, dup = _gate_grads(gate_ref[...].astype(f32), up_ref[...].astype(f32), da)
        dgate_n, dup_n = _gate_grads(gaten_ref[...].astype(f32)[:HALO], upn_ref[...].astype(f32)[:HALO], da_n)
        rid = lax.broadcasted_iota(jnp.int32, (8, tc), 0)
        for dd, d_n, c_ref, p_ref, o_ref, s_ref in ((dgate, dgate_n, wg_ref, pg_ref, dg_ref, sg_ref),
                                                    (dup, dup_n, wu_ref, pu_ref, du_ref, su_ref)):
            d1, d2 = _shift_up(dd, d_n, 1, last), _shift_up(dd, d_n, 2, last)
            o_ref[...] = (c_ref[2:3, :] * dd + c_ref[1:2, :] * d1 + c_ref[0:1, :] * d2).astype(bf16)
            p = p_ref[...].astype(f32)
            sums = [jnp.sum(d2 * p, axis=0, keepdims=True), jnp.sum(d1 * p, axis=0, keepdims=True),
                    jnp.sum(dd * p, axis=0, keepdims=True), jnp.sum(dd, axis=0, keepdims=True)]
            part = jnp.zeros((8, tc), f32)
            for k, sk in enumerate(sums):
                part = jnp.where(rid == k, sk, part)

            @pl.when(i == 0)
            def _(s_ref=s_ref, part=part):
                s_ref[...] = part

            @pl.when(i > 0)
            def _(s_ref=s_ref, part=part):
                s_ref[...] += part

    def nxt_rows(j, i):
        return (jnp.minimum((i + 1) * (tr // HALO), t // HALO - 1), j)

    tile = pl.BlockSpec((tr, tc), lambda j, i: (i, j))
    nxt = pl.BlockSpec((BF16_ROWS, tc), lambda j, i: (jnp.minimum((i + 1) * (tr // BF16_ROWS), t // BF16_ROWS - 1), j))
    wspec = pl.BlockSpec((3, tc), lambda j, i: (0, j))
    stat = pl.BlockSpec((8, tc), lambda j, i: (0, j))
    return pl.pallas_call(
        body, name=name, grid=(2, nsteps),
        in_specs=[pl.BlockSpec((tr, d), lambda j, i: (i, 0)),
                  pl.BlockSpec((HALO, d), lambda j, i: (nxt_rows(j, i)[0], 0)),
                  pl.BlockSpec((1, tc, d), lambda j, i: (j, 0, 0)),
                  tile, tile, tile, nxt, tile, nxt, wspec, wspec],
        out_specs=[tile, tile, stat, stat],
        out_shape=[jax.ShapeDtypeStruct((t, f), bf16), jax.ShapeDtypeStruct((t, f), bf16),
                   jax.ShapeDtypeStruct((8, f), f32), jax.ShapeDtypeStruct((8, f), f32)],
        compiler_params=_params("parallel", "arbitrary"))(
            dy, dy, w_out.reshape(2, tc, d), pg, pu, gate, gate, up, up, wg, wu)


def _head_mean_matrix():
    i = lax.broadcasted_iota(jnp.int32, (LANES, LANES), 0) // HEAD_DIM
    j = lax.broadcasted_iota(jnp.int32, (LANES, LANES), 1) // HEAD_DIM
    return jnp.where(i == j, 1.0 / HEAD_DIM, 0.0).astype(f32)


def _lane_half(shape):
    return (lax.broadcasted_iota(jnp.int32, shape, 1) % LANES) // HEAD_DIM


def q_norm_fwd(qp, g2, *, name, scale, tr=512):
    t, w = qp.shape
    tr = min(tr, t)

    def body(x_ref, g_ref, o_ref):
        bd = _head_mean_matrix()
        for cb in range(w // LANES):
            cols = slice(cb * LANES, (cb + 1) * LANES)
            xc = x_ref[:, cols]
            rh = lax.rsqrt(_dot_exact(xc * xc, bd) + EPS)
            o_ref[:, cols] = (xc * rh * g_ref[...] * scale).astype(bf16)

    row = pl.BlockSpec((tr, w), lambda i: (i, 0))
    return pl.pallas_call(
        body, name=name, grid=(t // tr,), in_specs=[row, pl.BlockSpec((1, LANES), lambda i: (0, 0))],
        out_specs=row, out_shape=jax.ShapeDtypeStruct((t, w), bf16),
        compiler_params=_params("parallel"))(qp, g2)


def q_norm_bwd(dq, qp, g2, *, name, scale, tr=512):
    t, w = qp.shape
    tr = min(tr, t)

    def body(dq_ref, x_ref, g_ref, o_ref, dg_ref):
        i = pl.program_id(0)
        bd = _head_mean_matrix()
        acc = jnp.zeros((1, LANES), f32)
        for cb in range(w // LANES):
            cols = slice(cb * LANES, (cb + 1) * LANES)
            xc = x_ref[:, cols]
            rh = lax.rsqrt(_dot_exact(xc * xc, bd) + EPS)
            xh = xc * rh
            dy = dq_ref[:, cols] * scale
            acc = acc + jnp.sum(dy * xh, axis=0, keepdims=True)
            tg = dy * g_ref[...]
            o_ref[:, cols] = (rh * (tg - xh * _dot_exact(tg * xh, bd))).astype(bf16)

        @pl.when(i == 0)
        def _():
            dg_ref[...] = acc

        @pl.when(i > 0)
        def _():
            dg_ref[...] += acc

    row = pl.BlockSpec((tr, w), lambda i: (i, 0))
    vec = pl.BlockSpec((1, LANES), lambda i: (0, 0))
    return pl.pallas_call(
        body, name=name, grid=(t // tr,), in_specs=[row, row, vec], out_specs=[row, vec],
        out_shape=[jax.ShapeDtypeStruct((t, w), bf16), jax.ShapeDtypeStruct((1, LANES), f32)],
        compiler_params=_params("arbitrary"))(dq, qp, g2)


def kv_post_fwd(kv, g2, *, name, tr=512):
    t, w = kv.shape
    tr = min(tr, t)
    kw = w // 2

    def body(x_ref, g_ref, k_ref, v_ref):
        bd = _head_mean_matrix()
        half = _lane_half((tr, LANES))
        for cb in range(kw // LANES):
            xc = x_ref[:, cb * LANES:(cb + 1) * LANES]
            rh = lax.rsqrt(_dot_exact(xc * xc, bd) + EPS)
            kn = xc * rh * g_ref[...]
            vc = x_ref[:, kw + cb * LANES:kw + (cb + 1) * LANES]
            for src, dst in ((kn, k_ref), (vc, v_ref)):
                sw = pltpu.roll(src, HEAD_DIM, 1)
                for hf in range(2):
                    blk = 2 * cb + hf
                    dst[:, blk * LANES:(blk + 1) * LANES] = jnp.where(half == hf, src, sw).astype(bf16)

    return pl.pallas_call(
        body, name=name, grid=(t // tr,),
        in_specs=[pl.BlockSpec((tr, w), lambda i: (i, 0)), pl.BlockSpec((1, LANES), lambda i: (0, 0))],
        out_specs=[pl.BlockSpec((tr, 2 * kw), lambda i: (i, 0))] * 2,
        out_shape=[jax.ShapeDtypeStruct((t, 2 * kw), bf16)] * 2,
        compiler_params=_params("parallel"))(kv, g2)


def kv_post_bwd(dk2, dv2, kv, g2, *, name, tr=512):
    t, w = kv.shape
    tr = min(tr, t)
    kw = w // 2

    def body(dk_ref, dv_ref, x_ref, g_ref, o_ref, dg_ref):
        i = pl.program_id(0)
        bd = _head_mean_matrix()
        half = _lane_half((tr, LANES))
        acc = jnp.zeros((1, LANES), f32)

        def fold(ref, cb):
            a = ref[:, (2 * cb) * LANES:(2 * cb + 1) * LANES]
            b = ref[:, (2 * cb + 1) * LANES:(2 * cb + 2) * LANES]
            return jnp.where(half == 0, a + pltpu.roll(a, HEAD_DIM, 1), b + pltpu.roll(b, HEAD_DIM, 1))

        for cb in range(kw // LANES):
            cols = slice(cb * LANES, (cb + 1) * LANES)
            xc = x_ref[:, cols]
            rh = lax.rsqrt(_dot_exact(xc * xc, bd) + EPS)
            xh = xc * rh
            dy = fold(dk_ref, cb)
            acc = acc + jnp.sum(dy * xh, axis=0, keepdims=True)
            tg = dy * g_ref[...]
            o_ref[:, cols] = (rh * (tg - xh * _dot_exact(tg * xh, bd))).astype(bf16)
            o_ref[:, kw + cb * LANES:kw + (cb + 1) * LANES] = fold(dv_ref, cb).astype(bf16)

        @pl.when(i == 0)
        def _():
            dg_ref[...] = acc

        @pl.when(i > 0)
        def _():
            dg_ref[...] += acc

    dup = pl.BlockSpec((tr, 2 * kw), lambda i: (i, 0))
    row = pl.BlockSpec((tr, w), lambda i: (i, 0))
    vec = pl.BlockSpec((1, LANES), lambda i: (0, 0))
    return pl.pallas_call(
        body, name=name, grid=(t // tr,), in_specs=[dup, dup, row, vec], out_specs=[row, vec],
        out_shape=[jax.ShapeDtypeStruct((t, w), bf16), jax.ShapeDtypeStruct((1, LANES), f32)],
        compiler_params=_params("arbitrary"))(dk2, dv2, kv, g2)


def _slope(h):
    return 2.0 ** (-8.0 * (h + 1) / N_Q_HEADS)


GROUP_ROWS = Q_PER_KV * CHUNK


def _band_mask(n):
    tq = lax.broadcasted_iota(jnp.int32, (GROUP_ROWS, 2 * CHUNK), 0) % CHUNK
    jk = lax.broadcasted_iota(jnp.int32, (GROUP_ROWS, 2 * CHUNK), 1)
    dist = tq + CHUNK - jk
    ok = (dist >= 0) & (dist < CHUNK) & jnp.logical_not((n == 0) & (jk < CHUNK))
    return dist.astype(f32), ok


def _band(ref, n, kh):
    p0 = pl.multiple_of(jnp.maximum(n - 1, 0) * CHUNK, CHUNK)
    c0 = pl.multiple_of(n * CHUNK, CHUNK)
    cols = slice(kh * LANES, (kh + 1) * LANES)
    return jnp.concatenate([ref[pl.ds(p0, CHUNK), cols], ref[pl.ds(c0, CHUNK), cols]], axis=0)


def _stack_heads(ref, kh, half):
    parts = []
    for cb in (2 * kh, 2 * kh + 1):
        xc = ref[:, cb * LANES:(cb + 1) * LANES].astype(f32)
        parts += [jnp.where(half == hf, xc, 0.0).astype(bf16) for hf in range(2)]
    return jnp.concatenate(parts, axis=0)


def _unstack_heads(x4, half):
    return (jnp.where(half == 0, x4[0:CHUNK], x4[CHUNK:2 * CHUNK]),
            jnp.where(half == 0, x4[2 * CHUNK:3 * CHUNK], x4[3 * CHUNK:]))


def _per_head_column(kh, values):
    grp = lax.broadcasted_iota(jnp.int32, (GROUP_ROWS, 1), 0) // CHUNK
    col = jnp.full((GROUP_ROWS, 1), values[0], f32)
    for g in range(1, Q_PER_KV):
        col = jnp.where(grp == g, values[g], col)
    return col


def _softmax_band(q4, kband, dist, ok, slope, sink):
    s = _dot_nt(q4, kband)
    s = jnp.where(ok, s - slope * dist, -jnp.inf)
    m = jnp.maximum(jnp.max(s, axis=1, keepdims=True), sink)
    e = jnp.exp(s - m)
    es = jnp.exp(sink - m)
    den = jnp.sum(e, axis=1, keepdims=True) + es
    return e / den, es / den


def attn_fwd(q, k2, v2, sinks, *, name):
    t, w = q.shape
    nb = t // CHUNK

    def body(sink_ref, q_ref, k_ref, v_ref, o_ref):
        n = pl.program_id(0)
        dist, ok = _band_mask(n)
        half = _lane_half((CHUNK, LANES))
        for kh in range(N_KV_HEADS):
            heads = [Q_PER_KV * kh + g for g in range(Q_PER_KV)]
            slope = _per_head_column(kh, [_slope(h) for h in heads])
            sink = _per_head_column(kh, [sink_ref[h] for h in heads])
            q4 = _stack_heads(q_ref, kh, half)
            p, _ = _softmax_band(q4, _band(k_ref, n, kh), dist, ok, slope, sink)
            o4 = _dot(p.astype(bf16), _band(v_ref, n, kh))
            lo, hi = _unstack_heads(o4, half)
            o_ref[:, (2 * kh) * LANES:(2 * kh + 1) * LANES] = lo.astype(bf16)
            o_ref[:, (2 * kh + 1) * LANES:(2 * kh + 2) * LANES] = hi.astype(bf16)

    full = pl.BlockSpec((t, k2.shape[1]), lambda n: (0, 0))
    return pl.pallas_call(
        body, name=name, grid=(nb,),
        in_specs=[pl.BlockSpec(memory_space=pltpu.SMEM), pl.BlockSpec((CHUNK, w), lambda n: (n, 0)), full, full],
        out_specs=pl.BlockSpec((CHUNK, w), lambda n: (n, 0)),
        out_shape=jax.ShapeDtypeStruct((t, w), bf16),
        compiler_params=_params("parallel"))(sinks, q, k2, v2)


def attn_bwd(q, k2, v2, do, sinks, *, name):
    t, w = q.shape
    nb = t // CHUNK
    kw = k2.shape[1]

    def body(sink_ref, q_ref, k_ref, v_ref, do_ref, dq_ref, dk_ref, dv_ref, ds_ref, kc_ref, vc_ref):
        n = pl.program_id(0)

        @pl.when(n == 0)
        def _():
            ds_ref[...] = jnp.zeros_like(ds_ref)
            kc_ref[...] = jnp.zeros_like(kc_ref)
            vc_ref[...] = jnp.zeros_like(vc_ref)
            dk_ref[...] = jnp.zeros_like(dk_ref)
            dv_ref[...] = jnp.zeros_like(dv_ref)

        @pl.when(n == nb)
        def _():
            dk_ref[...] = kc_ref[...]
            dv_ref[...] = vc_ref[...]

        @pl.when(n < nb)
        def _():
            dist, ok = _band_mask(n)
            half = _lane_half((CHUNK, LANES))
            lane = lax.broadcasted_iota(jnp.int32, (1, LANES), 1)
            sink_acc = jnp.zeros((1, LANES), f32)
            for kh in range(N_KV_HEADS):
                heads = [Q_PER_KV * kh + g for g in range(Q_PER_KV)]
                slope = _per_head_column(kh, [_slope(h) for h in heads])
                sink = _per_head_column(kh, [sink_ref[h] for h in heads])
                q4 = _stack_heads(q_ref, kh, half)
                do4 = _stack_heads(do_ref, kh, half)
                kband = _band(k_ref, n, kh)
                vband = _band(v_ref, n, kh)
                p, ps = _softmax_band(q4, kband, dist, ok, slope, sink)
                dp = _dot_nt(do4, vband)
                delta = jnp.sum(p * dp, axis=1, keepdims=True)
                dsb = (p * (dp - delta)).astype(bf16)
                sd = ps * delta
                for g, h in enumerate(heads):
                    part = jnp.sum(sd[g * CHUNK:(g + 1) * CHUNK], axis=0, keepdims=True)
                    sink_acc = sink_acc + jnp.where(lane == h, -part, 0.0)
                lo, hi = _unstack_heads(_dot(dsb, kband), half)
                dq_ref[:, (2 * kh) * LANES:(2 * kh + 1) * LANES] = lo
                dq_ref[:, (2 * kh + 1) * LANES:(2 * kh + 2) * LANES] = hi
                dkb = _dot_tn(dsb, q4)
                dvb = _dot_tn(p.astype(bf16), do4)
                cols = slice(kh * LANES, (kh + 1) * LANES)
                dk_ref[:, cols] = kc_ref[:, cols] + dkb[0:CHUNK]
                dv_ref[:, cols] = vc_ref[:, cols] + dvb[0:CHUNK]
                kc_ref[:, cols] = dkb[CHUNK:]
                vc_ref[:, cols] = dvb[CHUNK:]
            ds_ref[...] += sink_acc

    full = pl.BlockSpec((t, kw), lambda n: (0, 0))
    qblk = pl.BlockSpec((CHUNK, w), lambda n: (jnp.minimum(n, nb - 1), 0))
    kblk = pl.BlockSpec((CHUNK, kw), lambda n: (jnp.maximum(n - 1, 0), 0))
    return pl.pallas_call(
        body, name=name, grid=(nb + 1,),
        in_specs=[pl.BlockSpec(memory_space=pltpu.SMEM), qblk, full, full, qblk],
        out_specs=[qblk, kblk, kblk, pl.BlockSpec((1, LANES), lambda n: (0, 0))],
        out_shape=[jax.ShapeDtypeStruct((t, w), f32), jax.ShapeDtypeStruct((t, kw), f32),
                   jax.ShapeDtypeStruct((t, kw), f32), jax.ShapeDtypeStruct((1, LANES), f32)],
        scratch_shapes=[pltpu.VMEM((CHUNK, kw), f32), pltpu.VMEM((CHUNK, kw), f32)],
        compiler_params=_params("arbitrary"))(sinks, q, k2, v2, do)


def loss_head(y, target, *, name, tr=512):
    t, d = y.shape
    tr = min(tr, t)

    def body(y_ref, t_ref, dy_ref, s_ref):
        i = pl.program_id(0)
        e = y_ref[...] - t_ref[...]
        dy_ref[...] = e * (1.0 / d)
        part = jnp.sum(e * e, axis=0, keepdims=True)

        @pl.when(i == 0)
        def _():
            s_ref[...] = part

        @pl.when(i > 0)
        def _():
            s_ref[...] += part

    row = pl.BlockSpec((tr, d), lambda i: (i, 0))
    vec = pl.BlockSpec((1, d), lambda i: (0, 0))
    return pl.pallas_call(
        body, name=name, grid=(t // tr,), in_specs=[row, row], out_specs=[row, vec],
        out_shape=[jax.ShapeDtypeStruct((t, d), f32), jax.ShapeDtypeStruct((1, d), f32)],
        compiler_params=_params("arbitrary"))(y, target)


N_STEPS = 8


def _row_blocks(shape):
    if len(shape) == 2:
        r, c = shape
        return (r // N_STEPS, c), (lambda s: (s, 0))
    l, r, c = shape
    per = N_STEPS // l
    return (1, r // per, c), (lambda s: (s // per, s % per, 0))


CAST_STEPS = 4


def cast_into_slot(arrays, k_arr, *, name):
    in_specs, out_specs, out_shape, layers = [], [], [], []
    for a in arrays:
        r, c = a.shape[-2:]
        rb = r // CAST_STEPS
        if a.ndim == 2:
            in_specs.append(pl.BlockSpec((rb, c), lambda s, k: (s, 0)))
            layers.append(None)
        else:
            for l in range(a.shape[0]):
                in_specs.append(pl.BlockSpec((1, rb, c), lambda s, k, l=l: (l, s, 0)))
                layers.append(l)
        for _ in range(1 if a.ndim == 2 else a.shape[0]):
            out_specs.append(pl.BlockSpec((1, rb, c), lambda s, k: (k[0], s, 0)))
            out_shape.append(jax.ShapeDtypeStruct((N_SHARDS, r, c), bf16))
    n = len(in_specs)

    def body(k_ref, *refs):
        for i_ref, o_ref, l in zip(refs[:n], refs[n:], layers):
            o_ref[0] = (i_ref[...] if l is None else i_ref[0]).astype(bf16)

    args = []
    for a in arrays:
        args += [a] * (1 if a.ndim == 2 else a.shape[0])
    return pl.pallas_call(
        body, name=name,
        grid_spec=pltpu.PrefetchScalarGridSpec(num_scalar_prefetch=1, grid=(CAST_STEPS,),
                                               in_specs=in_specs, out_specs=out_specs),
        out_shape=out_shape, compiler_params=_params("parallel"))(k_arr, *args)


def adamw(ws, gs, ms, vs, *, name):
    n = len(ws)
    specs, g_specs, g_count = [], [], []
    for w, g_list in zip(ws, gs):
        blk, index = _row_blocks(w.shape)
        specs.append(pl.BlockSpec(blk, index))
        layers = len(g_list)
        per = N_STEPS // layers
        g_count.append(layers)
        for l in range(layers):
            g_specs.append(pl.BlockSpec(blk[-2:], lambda s, l=l, per=per: (jnp.where(s // per == l, s % per, 0), 0)))
    ng = len(g_specs)

    def body(*refs):
        s = pl.program_id(0)
        g_refs = refs[3 * n:3 * n + ng]
        outs = refs[3 * n + ng:]
        off = 0
        for i in range(n):
            w_ref, m_ref, v_ref = refs[i], refs[n + i], refs[2 * n + i]
            go_ref, d_ref, nm_ref, nv_ref = (outs[k * n + i] for k in range(4))
            layers = g_count[i]
            g = g_refs[off][...]
            for l in range(1, layers):
                g = jnp.where(s // (N_STEPS // layers) == l, g_refs[off + l][...], g)
            off += layers
            g = g.reshape(w_ref.shape)
            m = ADAM_B1 * m_ref[...] + (1.0 - ADAM_B1) * g
            v = ADAM_B2 * v_ref[...] + (1.0 - ADAM_B2) * (g * g)
            m_hat = m / ADAM_C1
            v_hat = v / ADAM_C2
            go_ref[...] = g
            d_ref[...] = -ADAM_LR * (m_hat / (jnp.sqrt(v_hat) + ADAM_EPS) + ADAM_WD * w_ref[...])
            nm_ref[...] = m
            nv_ref[...] = v

    outs = pl.pallas_call(
        body, name=name, grid=(N_STEPS,), in_specs=specs * 3 + g_specs, out_specs=specs * 4,
        out_shape=[jax.ShapeDtypeStruct(a.shape, f32) for a in ws] * 4,
        compiler_params=_params("parallel"))(*ws, *ms, *vs, *[g for g_list in gs for g in g_list])
    return [outs[k * n:(k + 1) * n] for k in range(4)]


def _place():
    return lax.axis_index("x"), lax.axis_index("y"), lax.axis_index("c")


def gather_shards(bufs, *, name, split):
    n = len(bufs)

    def body(*refs):
        bufs_ = refs[:n]
        isend, irecv, dsend, drecv = refs[2 * n:]
        x, y, c = _place()
        k = 2 * x + y
        peers = [(1 - x, y, c), (x, 1 - y, c), (1 - x, 1 - y, c)]
        peer_k = [2 * (1 - x) + y, 2 * x + (1 - y), 2 * (1 - x) + (1 - y)]

        def slab(a, q, h):
            if not split[a]:
                return bufs_[a].at[q]
            half = bufs_[a].shape[1] // 2
            return bufs_[a].at[q, pl.ds(pl.multiple_of(h * half, 16), half)]

        def ici(a, j, q):
            return pltpu.make_async_remote_copy(
                src_ref=slab(a, q, c), dst_ref=slab(a, q, c), send_sem=isend.at[3 * a + j], recv_sem=irecv.at[3 * a + j],
                device_id=peers[j], device_id_type=MESH)

        def d2d(a, j, h):
            return pltpu.make_async_remote_copy(
                src_ref=slab(a, peer_k[j], h), dst_ref=slab(a, peer_k[j], h), send_sem=dsend.at[3 * a + j],
                recv_sem=drecv.at[3 * a + j], device_id=(x, y, 1 - c), device_id_type=MESH)

        for a in range(n):
            for j in range(3):
                ici(a, j, k).start()
        for a in range(n):
            for j in range(3):
                ici(a, j, peer_k[j]).wait_recv()
                if split[a]:
                    d2d(a, j, c).start()
        for a in range(n):
            for j in range(3):
                if split[a]:
                    d2d(a, j, 1 - c).wait_recv()
        for a in range(n):
            for j in range(3):
                ici(a, j, k).wait_send()
                if split[a]:
                    d2d(a, j, c).wait_send()

    return pl.pallas_call(
        body, name=name, in_specs=[ANY] * n, out_specs=[ANY] * n,
        out_shape=[jax.ShapeDtypeStruct(b.shape, b.dtype) for b in bufs],
        input_output_aliases={i: i for i in range(n)},
        scratch_shapes=[pltpu.SemaphoreType.DMA((3 * n,))] * 4)(*bufs)


HBM = pl.BlockSpec(memory_space=pltpu.HBM)
SEM = pl.BlockSpec(memory_space=pltpu.SEMAPHORE)
DATAFLOW = pltpu.SideEffectType.DATAFLOW_SIDE_EFFECTING


def _chip_peers():
    x, y, c = _place()
    return 2 * x + y, [(1 - x, y, c), (x, 1 - y, c), (1 - x, 1 - y, c)], [2 * (1 - x) + y, 2 * x + (1 - y), 2 * (1 - x) + (1 - y)]


def gather_start(bufs, groups, after, *, name):
    n = len(bufs)
    ng = len(groups)

    def body(*refs):
        ins = refs[:n]
        sends, recvs = refs[2 * n + 1:2 * n + 1 + ng], refs[2 * n + 1 + ng:2 * n + 1 + 2 * ng]
        token = refs[-1]
        k, peers, _ = _chip_peers()
        for gi, grp in enumerate(groups):
            for pos, a in enumerate(grp):
                for j in range(3):
                    pltpu.make_async_remote_copy(
                        src_ref=ins[a].at[k], dst_ref=ins[a].at[k], send_sem=sends[gi].at[3 * pos + j],
                        recv_sem=recvs[gi].at[3 * pos + j], device_id=peers[j], device_id_type=MESH).start()
        token[...] = jnp.zeros_like(token)

    sems = [pltpu.SemaphoreType.DMA((3 * len(grp),)) for grp in groups]
    outs = pl.pallas_call(
        body, name=name, in_specs=[HBM] * n + [ANY],
        out_specs=[HBM] * n + [SEM] * (2 * ng) + [pl.BlockSpec(memory_space=pltpu.VMEM)],
        out_shape=[pltpu.HBM(b.shape, b.dtype) for b in bufs] + sems + sems + [jax.ShapeDtypeStruct((8, LANES), f32)],
        input_output_aliases={i: i for i in range(n)},
        compiler_params=pltpu.CompilerParams(has_side_effects=DATAFLOW))(
            *[pltpu.with_memory_space_constraint(b, pltpu.HBM) for b in bufs], after)
    return outs[:n], outs[n:n + ng], outs[n + ng:n + 2 * ng], outs[-1]


def gather_wait(bufs, send_sems, recv_sems, after, *, name):
    n = len(bufs)

    def body(*refs):
        ins = refs[:n]
        send, recv = refs[n], refs[n + 1]
        k, peers, peer_k = _chip_peers()
        for a in range(n):
            for j in range(3):
                copy = pltpu.make_async_remote_copy(
                    src_ref=ins[a].at[k], dst_ref=ins[a].at[peer_k[j]], send_sem=send.at[3 * a + j],
                    recv_sem=recv.at[3 * a + j], device_id=peers[j], device_id_type=MESH)
                copy.wait_send()
                copy.wait_recv()

    return pl.pallas_call(
        body, name=name, in_specs=[HBM] * n + [SEM, SEM, ANY], out_specs=[HBM] * n,
        out_shape=[pltpu.HBM(b.shape, b.dtype) for b in bufs],
        input_output_aliases={i: i for i in range(n)},
        compiler_params=pltpu.CompilerParams(has_side_effects=DATAFLOW))(*bufs, send_sems, recv_sems, after)


def sibling_exchange(arrays, *, name):
    n = len(arrays)

    def body(*refs):
        ins, outs = refs[:n], refs[n:2 * n]
        send, recv = refs[2 * n:]
        x, y, c = _place()

        def copy(a):
            return pltpu.make_async_remote_copy(
                src_ref=ins[a], dst_ref=outs[a], send_sem=send.at[a], recv_sem=recv.at[a],
                device_id=(x, y, 1 - c), device_id_type=MESH)

        for a in range(n):
            copy(a).start()
        for a in range(n):
            copy(a).wait_recv()
        for a in range(n):
            copy(a).wait_send()

    return pl.pallas_call(
        body, name=name, in_specs=[ANY] * n, out_specs=[ANY] * n,
        out_shape=[jax.ShapeDtypeStruct(a.shape, a.dtype) for a in arrays],
        scratch_shapes=[pltpu.SemaphoreType.DMA((n,)), pltpu.SemaphoreType.DMA((n,))])(*arrays)


ALL_MASKS = [(mx, my, mc) for mx in (0, 1) for my in (0, 1) for mc in (0, 1)][1:]


def _scatter_copies(srcs, lands, ev, send, recv, esend, erecv):
    x, y, c = _place()
    me = 4 * x + 2 * y + c
    k, peers, peer_k = _chip_peers()
    out = []
    for a in range(len(srcs)):
        for j in range(3):
            out.append(pltpu.make_async_remote_copy(
                src_ref=srcs[a].at[peer_k[j]], dst_ref=lands[a].at[j], send_sem=send.at[3 * a + j],
                recv_sem=recv.at[3 * a + j], device_id=peers[j], device_id_type=MESH))
    start_ev, wait_ev = [], []
    if ev is not None:
        for j, (mx, my, mc) in enumerate(ALL_MASKS):
            peer = (x ^ mx, y ^ my, c ^ mc)
            start_ev.append(pltpu.make_async_remote_copy(
                src_ref=ev.at[me], dst_ref=ev.at[me], send_sem=esend.at[j], recv_sem=erecv.at[j],
                device_id=peer, device_id_type=MESH))
            wait_ev.append(pltpu.make_async_remote_copy(
                src_ref=ev.at[me], dst_ref=ev.at[me ^ (4 * mx + 2 * my + mc)], send_sem=esend.at[j],
                recv_sem=erecv.at[j], device_id=peer, device_id_type=MESH))
    return out, start_ev, wait_ev


def chip_scatter_start(arrays, everyone, after, *, name):
    n = len(arrays)
    ne = 0 if everyone is None else 1
    lands = [pltpu.with_memory_space_constraint(lax.empty((3,) + a.shape[1:], a.dtype), pltpu.HBM) for a in arrays]

    def body(*refs):
        srcs, lands_ = refs[:n], refs[n:2 * n]
        ev = refs[2 * n] if ne else None
        sems = refs[2 * n + ne + 1 + 2 * n + ne:-1]
        send, recv = sems[0], sems[1]
        esend, erecv = (sems[2], sems[3]) if ne else (None, None)
        copies, start_ev, _ = _scatter_copies(srcs, lands_, ev, send, recv, esend, erecv)
        for cp in start_ev + copies:
            cp.start()
        refs[-1][...] = jnp.zeros_like(refs[-1])

    sem_shapes = [pltpu.SemaphoreType.DMA((3 * n,))] * 2 + [pltpu.SemaphoreType.DMA((7,))] * (2 * ne)
    bufs = list(arrays) + lands + ([everyone] if ne else [])
    outs = pl.pallas_call(
        body, name=name, in_specs=[HBM] * len(bufs) + [ANY],
        out_specs=[HBM] * len(bufs) + [SEM] * len(sem_shapes) + [pl.BlockSpec(memory_space=pltpu.VMEM)],
        out_shape=[pltpu.HBM(b.shape, b.dtype) for b in bufs] + sem_shapes + [jax.ShapeDtypeStruct((8, LANES), f32)],
        input_output_aliases={i: i for i in range(len(bufs))},
        compiler_params=pltpu.CompilerParams(has_side_effects=DATAFLOW))(
            *[pltpu.with_memory_space_constraint(b, pltpu.HBM) for b in bufs], after)
    return (n, ne, outs[:-1]), outs[-1]


def chip_scatter_wait(state, after, *, name):
    n, ne, held = state
    nb = 2 * n + ne
    bufs, sems = held[:nb], held[nb:]

    def body(*refs):
        srcs, lands_ = refs[:n], refs[n:2 * n]
        ev = refs[2 * n] if ne else None
        sems_ = refs[nb:nb + len(sems)]
        esend, erecv = (sems_[2], sems_[3]) if ne else (None, None)
        copies, _, wait_ev = _scatter_copies(srcs, lands_, ev, sems_[0], sems_[1], esend, erecv)
        for cp in wait_ev + copies:
            cp.wait_send()
            cp.wait_recv()

    outs = pl.pallas_call(
        body, name=name, in_specs=[HBM] * nb + [SEM] * len(sems) + [ANY], out_specs=[HBM] * nb,
        out_shape=[pltpu.HBM(b.shape, b.dtype) for b in bufs],
        input_output_aliases={i: i for i in range(nb)},
        compiler_params=pltpu.CompilerParams(has_side_effects=DATAFLOW))(*bufs, *sems, after)
    return outs[n:2 * n], (outs[2 * n] if ne else None)


def sibling_merge(bufs, *, name):
    n = len(bufs)

    def body(*refs):
        bufs_ = refs[:n]
        send, recv = refs[2 * n:]
        x, y, c = _place()

        def copy(u, h):
            return pltpu.make_async_remote_copy(
                src_ref=bufs_[u].at[h], dst_ref=bufs_[u].at[h], send_sem=send.at[u], recv_sem=recv.at[u],
                device_id=(x, y, 1 - c), device_id_type=MESH)

        for u in range(n):
            copy(u, c).start()
        for u in range(n):
            copy(u, 1 - c).wait_recv()
        for u in range(n):
            copy(u, c).wait_send()

    return pl.pallas_call(
        body, name=name, in_specs=[ANY] * n, out_specs=[ANY] * n,
        out_shape=[jax.ShapeDtypeStruct(b.shape, b.dtype) for b in bufs],
        input_output_aliases={i: i for i in range(n)},
        scratch_shapes=[pltpu.SemaphoreType.DMA((n,)), pltpu.SemaphoreType.DMA((n,))])(*bufs)


def sum_leading(a, *, name):
    n, r, c = a.shape

    def body(a_ref, o_ref):
        acc = a_ref[0]
        for i in range(1, n):
            acc = acc + a_ref[i]
        o_ref[...] = acc

    rb = r // 2 if r % 16 == 0 else r
    return pl.pallas_call(
        body, name=name, grid=(r // rb,), in_specs=[pl.BlockSpec((n, rb, c), lambda i: (0, i, 0))],
        out_specs=pl.BlockSpec((rb, c), lambda i: (i, 0)), out_shape=jax.ShapeDtypeStruct((r, c), f32),
        compiler_params=_params("parallel"))(a)


def _half_rows(shape):
    return shape[1] // 2 // 2


def rs_cast_other_half(grads, c_arr, *, name):
    n = len(grads)

    def body(c_ref, *refs):
        for i_ref, o_ref in zip(refs[:n], refs[n:]):
            o_ref[...] = i_ref[...].astype(bf16)

    in_specs = [pl.BlockSpec((1, _half_rows(g.shape), g.shape[2]), lambda s, r, c_ref: (s, (1 - c_ref[0]) * 2 + r, 0))
                for g in grads]
    out_specs = [pl.BlockSpec((1, _half_rows(g.shape), g.shape[2]), lambda s, r, c_ref: (s, r, 0)) for g in grads]
    return pl.pallas_call(
        body, name=name,
        grid_spec=pltpu.PrefetchScalarGridSpec(num_scalar_prefetch=1, grid=(N_SHARDS, 2),
                                               in_specs=in_specs, out_specs=out_specs),
        out_shape=[jax.ShapeDtypeStruct((N_SHARDS, g.shape[1] // 2, g.shape[2]), bf16) for g in grads],
        compiler_params=_params("parallel", "parallel"))(c_arr, *grads)


def rs_add_sibling(grads, recvd, ck_arr, *, name):
    n = len(grads)

    def body(ck_ref, *refs):
        s = pl.program_id(1)
        for u in range(n):
            g_ref, r_ref = refs[u], refs[n + u]
            qb_ref, own_ref = refs[2 * n + u], refs[3 * n + u]
            q = g_ref[0] + r_ref[0].astype(f32)
            qb_ref[0] = q.astype(bf16)

            @pl.when(s == ck_ref[1])
            def _(own_ref=own_ref, q=q):
                own_ref[...] = q

    in_specs = [pl.BlockSpec((1, _half_rows(g.shape), g.shape[2]), lambda r, s, ck: (s, ck[0] * 2 + r, 0)) for g in grads]
    in_specs += [pl.BlockSpec((1, _half_rows(g.shape), g.shape[2]), lambda r, s, ck: (s, r, 0)) for g in grads]
    out_specs = [pl.BlockSpec((1, _half_rows(g.shape), g.shape[2]), lambda r, s, ck: (s, r, 0)) for g in grads]
    out_specs += [pl.BlockSpec((_half_rows(g.shape), g.shape[2]), lambda r, s, ck: (r, 0)) for g in grads]
    outs = pl.pallas_call(
        body, name=name,
        grid_spec=pltpu.PrefetchScalarGridSpec(num_scalar_prefetch=1, grid=(2, N_SHARDS),
                                               in_specs=in_specs, out_specs=out_specs),
        out_shape=[jax.ShapeDtypeStruct((N_SHARDS, g.shape[1] // 2, g.shape[2]), bf16) for g in grads]
        + [jax.ShapeDtypeStruct((g.shape[1] // 2, g.shape[2]), f32) for g in grads],
        compiler_params=_params("parallel", "arbitrary"))(ck_arr, *grads, *recvd)
    return outs[:n], outs[n:]


def rs_sum_chips(owns, recvd, ck_arr, *, name):
    n = len(owns)

    def body(ck_ref, *refs):
        for u in range(n):
            own_ref, r_ref, o_ref = refs[u], refs[n + u], refs[2 * n + u]
            o_ref[0] = ((own_ref[...] + r_ref[0].astype(f32)) + r_ref[1].astype(f32)) + r_ref[2].astype(f32)

    in_specs = [pl.BlockSpec((o.shape[0] // 2, o.shape[1]), lambda r, ck: (r, 0)) for o in owns]
    in_specs += [pl.BlockSpec((3, o.shape[0] // 2, o.shape[1]), lambda r, ck: (0, r, 0)) for o in owns]
    out_specs = [pl.BlockSpec((1, o.shape[0] // 2, o.shape[1]), lambda r, ck: (ck[0], r, 0)) for o in owns]
    return pl.pallas_call(
        body, name=name,
        grid_spec=pltpu.PrefetchScalarGridSpec(num_scalar_prefetch=1, grid=(2,), in_specs=in_specs, out_specs=out_specs),
        out_shape=[jax.ShapeDtypeStruct((2,) + o.shape, f32) for o in owns],
        compiler_params=_params("parallel"))(ck_arr, *owns, *recvd)


SMALL = ("a_norm", "a_v_norm", "a_w_s", "a_b_s", "f_norm", "f_conv_w", "f_conv_b", "kv_norm", "k_norm",
         "b_norm", "b_q_norm", "b_sinks")
BIG = ("a_w_in", "a_w_out", "f_w_in", "f_w_out", "w_kv", "b_w_q", "b_w_o")
PACK_COLS = 1024
PACK_ROWS = 8 * N_STEPS


def _pack(parts, rows=PACK_ROWS):
    flat = jnp.concatenate([p.reshape(-1).astype(f32) for p in parts])
    pad = (-flat.shape[0]) % (rows * PACK_COLS)
    return jnp.pad(flat, (0, pad)).reshape(-1, PACK_COLS)


def _unpack(packed, shapes):
    flat = packed.reshape(-1)
    out, off = [], 0
    for s in shapes:
        size = math.prod(s)
        out.append(flat[off:off + size].reshape(s))
        off += size
    return out


def _ffn_fwd(x, g, h, r, w_in4, conv_w, conv_b, f, tag):
    wg, wu = conv_w[:, :f], conv_w[:, f:]
    bg, bu = conv_b[None, :f], conv_b[None, f:]
    pg, pu, gate, up, a = ffn_in_fused(h, w_in4, wg, wu, bg, bu, name=f"ffn{tag}_in")
    return a, (x, g, h, r, pg, pu, gate, up, a, wg, wu)


def _ffn_bwd(dy, saved, w_in4, w_out, tag):
    x, g, h, r, pg, pu, gate, up, a, wg, wu = saved
    f = w_out.shape[0]
    d_w_out = mm_tn(a, [dy], name=f"ffn{tag}_dwout", n_s=w_out.shape[1], tki=f // 2)
    dpg, dpu, sg, su = ffn_gate_bwd(dy, w_out, pg, pu, gate, up, wg, wu, name=f"ffn{tag}_dgate")
    d_w_in = mm_tn(h, [dpg, dpu], name=f"ffn{tag}_dwin", n_s=w_in4.shape[2])
    dh = mm_nt([dpg, dpu], w_in4, name=f"ffn{tag}_dh")
    dx, (dg,) = rms_bwd([dh], x, r, [g], dy, name=f"ffn{tag}_dnorm")
    d_conv_w = jnp.concatenate([sg[0:3], su[0:3]], axis=1)
    d_conv_b = jnp.concatenate([sg[3], su[3]], axis=0)
    return dx, dg, d_w_in, d_conv_w, d_conv_b, d_w_out


def _rs_front(units, c_arr, tag):
    other_bf = rs_cast_other_half(units, c_arr, name=f"rs_cast{tag}")
    from_sib = sibling_exchange(list(other_bf), name=f"rs_sibling{tag}")
    return rs_add_sibling(units, from_sib, c_arr, name=f"rs_add{tag}")


def _rs_back(own, from_chips, c_arr, tag):
    halves = rs_sum_chips(list(own), list(from_chips), c_arr, name=f"rs_sum{tag}")
    return [m.reshape(-1, m.shape[2]) for m in sibling_merge(list(halves), name=f"rs_merge{tag}")]


def kernel(x, a_norm, a_w_in, a_v_norm, a_w_s, a_b_s, a_w_out, f_norm, f_w_in, f_conv_w, f_conv_b, f_w_out, kv_norm, w_kv, k_norm, b_norm, b_w_q, b_q_norm, b_sinks, b_w_o, loss_target, m_a_norm, m_a_w_in, m_a_v_norm, m_a_w_s, m_a_b_s, m_a_w_out, m_f_norm, m_f_w_in, m_f_conv_w, m_f_conv_b, m_f_w_out, m_kv_norm, m_w_kv, m_k_norm, m_b_norm, m_b_w_q, m_b_q_norm, m_b_sinks, m_b_w_o, v_a_norm, v_a_w_in, v_a_v_norm, v_a_w_s, v_a_b_s, v_a_w_out, v_f_norm, v_f_w_in, v_f_conv_w, v_f_conv_b, v_f_w_out, v_kv_norm, v_w_kv, v_k_norm, v_b_norm, v_b_w_q, v_b_q_norm, v_b_sinks, v_b_w_o):
    args = dict(locals())
    weights = {n: args[n] for n in SMALL + BIG}
    moms = {n: args["m_" + n] for n in SMALL + BIG}
    vars_ = {n: args["v_" + n] for n in SMALL + BIG}
    t, d = x.shape[1], x.shape[2]
    xi, yi, ci = _place()
    chip = 2 * xi + yi

    big_local = [a_w_in[0], a_w_out[0], f_w_in, f_w_out, w_kv, b_w_q[0], b_w_o[0]]
    c_arr = jnp.stack([ci, chip]).astype(jnp.int32)
    k_arr = jnp.stack([chip]).astype(jnp.int32)
    b_ain, b_aout, b_fin0, b_fin1, b_fout0, b_fout1, b_kv, b_q, b_o = cast_into_slot(big_local, k_arr, name="cast_weights")
    small_cols = _pack([a_norm, a_v_norm, f_conv_w], rows=8)
    b_small = lax.dynamic_update_slice(jnp.zeros((N_SHARDS,) + small_cols.shape, f32), small_cols[None], (chip, 0, 0))
    g_small, g_a_w_in, g_a_w_out = gather_shards([b_small, b_ain, b_aout], name="gather_first", split=[False, True, True])
    later, send_sems, recv_sems, token = gather_start([b_fin0, b_fout0, b_kv, b_q, b_o, b_fin1, b_fout1],
                                                      [[0], [1, 2, 3, 4], [5, 6]], g_small, name="gather_start")
    ns_cols = a_norm.shape[1]
    nf_cols = f_conv_w.shape[2]
    parts = [_unpack(g_small[k], [a_norm.shape, a_v_norm.shape, f_conv_w.shape]) for k in range(N_SHARDS)]
    a_norm_f = jnp.concatenate([p[0] for p in parts], axis=1) + token[0, 0]
    a_v_norm_f = jnp.concatenate([p[1] for p in parts], axis=1)
    conv_w_f = jnp.concatenate([p[2] for p in parts], axis=2)
    w_a_in = g_a_w_in
    w_a_out = g_a_w_out.reshape(1, -1, d)

    x0 = x[0]
    tril = jnp.tril(jnp.ones((CHUNK, CHUNK), dtype=bool))
    wc = jnp.where(tril[None], a_w_s[0], 0.0).astype(bf16)
    bt = a_b_s[0].T
    kg2 = jnp.tile(k_norm, 2)[None]
    qg2 = jnp.tile(b_q_norm[0], 2)[None]

    (h_a,), r_a = rms_fwd(x0, [a_norm_f], name="a_norm")
    zu = mm_nn(h_a, w_a_in, name="a_in_u", s0=0, ns=2)
    zv = mm_nn(h_a, w_a_in, name="a_in_v", s0=2, ns=2)
    y_a = sgu_gate_fwd(zu, zv, a_v_norm_f, wc, bt, name="a_gate")
    f = f_w_out.shape[1] * N_SHARDS
    x1, (h_f0,), r_f0 = mm_residual(y_a, w_a_out[0], x0, name="a_out", gains=[f_norm[0:1]])
    (g_fin0,) = gather_wait(later[0:1], send_sems[0], recv_sems[0], x1, name="gather_wait_0")
    w_f_in = [g_fin0, None]
    a0, ffn0 = _ffn_fwd(x1, f_norm[0:1], h_f0, r_f0, w_f_in[0], conv_w_f[0], f_conv_b[0], f, "0")
    g_fout0, g_w_kv, g_b_w_q, g_b_w_o = gather_wait(later[1:5], send_sems[1], recv_sems[1], a0, name="gather_wait_1")
    w_f_out = [g_fout0.reshape(-1, d), None]
    w_kv_f = g_w_kv.reshape(1, d, -1)
    w_q_f = g_b_w_q.reshape(1, d, -1)
    w_o_f = g_b_w_o.reshape(1, -1, d)
    x2, (h_k, h_q), r_b = mm_residual(a0, w_f_out[0], x1, name="ffn0_out", gains=[kv_norm[None], b_norm])
    kv = mm_nn(h_k, w_kv_f, name="kv_proj")
    k2, v2 = kv_post_fwd(kv, kg2, name="kv_post")
    qp = mm_nn(h_q, w_q_f, name="q_proj")
    qn = q_norm_fwd(qp, qg2, name="q_norm", scale=HEAD_DIM ** -0.5)
    o = attn_fwd(qn, k2, v2, b_sinks[0], name="attn")
    x3, (h_f1,), r_f1 = mm_residual(o, w_o_f[0], x2, name="o_proj", gains=[f_norm[1:2]])
    g_fin1, g_fout1 = gather_wait(later[5:7], send_sems[2], recv_sems[2], x3, name="gather_wait_2")
    w_f_in[1] = g_fin1
    w_f_out[1] = g_fout1.reshape(-1, d)
    a1, ffn1 = _ffn_fwd(x3, f_norm[1:2], h_f1, r_f1, w_f_in[1], conv_w_f[1], f_conv_b[1], f, "1")
    dx4, sq = mm_residual(a1, w_f_out[1], x3, name="ffn1_out", target=loss_target[0])
    loss_part = (0.5 * jnp.sum(sq) / d).reshape(1)

    dx3, d_fn1, d_fwin1, d_cw1, d_cb1, d_fwout1 = _ffn_bwd(dx4, ffn1, w_f_in[1], w_f_out[1], "1")
    do = mm_nt([dx3], w_o_f, name="o_proj_dx")
    d_w_o = mm_tn(o, [dx3], name="o_proj_dw", n_s=d)
    dqn, dk2, dv2, dsink = attn_bwd(qn, k2, v2, do, b_sinks[0], name="attn_bwd")
    dqp, dqg = q_norm_bwd(dqn, qp, qg2, name="q_norm_bwd", scale=HEAD_DIM ** -0.5)
    dkv, dkg = kv_post_bwd(dk2, dv2, kv, kg2, name="kv_post_bwd")
    d_w_q = mm_tn(h_q, [dqp], name="q_proj_dw", n_s=w_q_f.shape[2])
    dh_q = mm_nt([dqp], w_q_f, name="q_proj_dx")
    d_w_kv = mm_tn(h_k, [dkv], name="kv_proj_dw", n_s=w_kv_f.shape[2])
    dh_k = mm_nt([dkv], w_kv_f, name="kv_proj_dx")
    dx2, (d_kvn, d_bn) = rms_bwd([dh_k, dh_q], x2, r_b, [kv_norm[None], b_norm], dx3, name="b_norm_bwd")
    sh = N_SHARDS
    units1 = [d_fwin1, d_fwout1.reshape(sh, -1, d), d_w_kv.reshape(sh, -1, d_w_kv.shape[2]),
              d_w_q.reshape(sh, -1, d_w_q.shape[2]), d_w_o.reshape(sh, -1, d)]
    chip_bf1, own1 = _rs_front(units1, c_arr, "1")
    scatter1, token1 = chip_scatter_start(list(chip_bf1), None, dx2, name="rs_chips_start1")
    ffn0 = ffn0[:9] + (ffn0[9] + token1[0, 0],) + ffn0[10:]
    dx1, d_fn0, d_fwin0, d_cw0, d_cb0, d_fwout0 = _ffn_bwd(dx2, ffn0, w_f_in[0], w_f_out[0], "0")
    chip_bf2, own2 = _rs_front([d_fwin0, d_fwout0.reshape(sh, -1, d)], c_arr, "2")
    scatter2, token2 = chip_scatter_start(list(chip_bf2), None, dx1, name="rs_chips_start2")
    a_v_norm_f = a_v_norm_f + token2[0, 0]
    dy_a = mm_nt([dx1], w_a_out, name="a_out_dx")
    d_w_aout = mm_tn(y_a, [dx1], name="a_out_dw", n_s=d)
    dzu, dzv, d_avn, d_ws, d_bt = sgu_gate_bwd(zu, zv, dy_a, a_v_norm_f, wc, bt, name="a_gate_bwd")
    d_w_ain = mm_tn(h_a, [dzu, dzv], name="a_in_dw", n_s=w_a_in.shape[2])
    dh_a = mm_nt([dzu, dzv], w_a_in, name="a_in_dx")
    dx0, (d_an,) = rms_bwd([dh_a], x0, r_a, [a_norm_f], dx1, name="a_norm_bwd")
    grad_x = dx0[None]

    chip_bf3, own3 = _rs_front([d_w_ain, d_w_aout.reshape(sh, -1, d)], c_arr, "3")
    d_fn = jnp.concatenate([d_fn0, d_fn1], axis=0)
    d_cw = jnp.stack([d_cw0, d_cw1])
    d_cb = jnp.stack([d_cb0, d_cb1])
    d_kg = (dkg[0, :HEAD_DIM] + dkg[0, HEAD_DIM:])
    d_qg = (dqg[0, :HEAD_DIM] + dqg[0, HEAD_DIM:])[None]
    small_full = [d_an, d_avn, d_ws[None], d_bt.T[None], d_fn, d_cw, d_cb, d_kvn[0], d_kg, d_bn, d_qg,
                  dsink[:, :N_Q_HEADS], loss_part]
    packed = _pack(small_full)
    me = 4 * xi + 2 * yi + ci
    everyone = lax.dynamic_update_slice(lax.empty((N_DEV,) + packed.shape, f32), packed[None], (me, 0, 0))
    scatter3, _ = chip_scatter_start(list(chip_bf3), everyone, chip_bf3[0], name="rs_chips_start3")
    from_chips1, _ = chip_scatter_wait(scatter1, scatter3[2][0], name="rs_chips_wait1")
    fin1, fout1, gkv, gq, go = _rs_back(own1, from_chips1, c_arr, "1")
    from_chips2, _ = chip_scatter_wait(scatter2, fin1, name="rs_chips_wait2")
    fin0, fout0 = _rs_back(own2, from_chips2, c_arr, "2")
    late = ("f_w_in", "f_w_out", "w_kv", "b_w_q", "b_w_o")
    res_late = adamw([weights[n] for n in late], [[fin0, fin1], [fout0, fout1], [gkv], [gq], [go]],
                     [moms[n] for n in late], [vars_[n] for n in late], name="adamw_late")
    from_chips3, from_all = chip_scatter_wait(scatter3, res_late[1][2], name="rs_chips_wait3")
    ain, aout = _rs_back(own3, from_chips3, c_arr, "3")
    first = ("a_w_in", "a_w_out")
    res_first = adamw([weights[n] for n in first], [[ain], [aout]], [moms[n] for n in first],
                      [vars_[n] for n in first], name="adamw_first")
    big = {n: tuple(r[i] for r in res_late) for i, n in enumerate(late)}
    big.update({n: tuple(r[i] for r in res_first) for i, n in enumerate(first)})

    full_shapes = [g.shape for g in small_full]
    small_g = _unpack(sum_leading(from_all, name="small_sum"), full_shapes)
    loss = small_g.pop()[0]
    small_g[0] = lax.dynamic_slice_in_dim(small_g[0], chip * ns_cols, ns_cols, axis=1)
    small_g[1] = lax.dynamic_slice_in_dim(small_g[1], chip * ns_cols, ns_cols, axis=1)
    small_g[5] = lax.dynamic_slice_in_dim(small_g[5], chip * nf_cols, nf_cols, axis=2)
    small_shapes = [weights[n].shape for n in SMALL]
    small_g = [g.reshape(s) for g, s in zip(small_g, small_shapes)]
    pw, pg_, pm, pv = (_pack(v) for v in ([weights[n] for n in SMALL], small_g, [moms[n] for n in SMALL],
                                          [vars_[n] for n in SMALL]))
    _, (sd,), (sm,), (sv,) = adamw([pw], [[pg_]], [pm], [pv], name="adamw_small")
    small_d, small_m, small_v = (_unpack(v, small_shapes) for v in (sd, sm, sv))

    out = {}
    for i, n in enumerate(SMALL):
        out[n] = (small_g[i], small_d[i], small_m[i], small_v[i])
    out.update(big)
    order = ["a_norm", "a_w_in", "a_v_norm", "a_w_s", "a_b_s", "a_w_out", "f_norm", "f_w_in", "f_conv_w", "f_conv_b",
             "f_w_out", "kv_norm", "w_kv", "k_norm", "b_norm", "b_w_q", "b_q_norm", "b_sinks", "b_w_o"]
    return (loss, grad_x, *[out[n][0] for n in order], *[out[n][1] for n in order],
            *[out[n][2] for n in order], *[out[n][3] for n in order])
```

```python
import functools
import math

import jax
import jax.numpy as jnp
from jax import lax
from jax.experimental import pallas as pl
from jax.experimental.pallas import tpu as pltpu

f32 = jnp.float32
bf16 = jnp.bfloat16
MESH = pl.DeviceIdType.MESH
ANY = pl.BlockSpec(memory_space=pl.ANY)

EPS = 1e-6
LANES = 128
CHUNK = 128
HEAD_DIM = 64
N_Q_HEADS = 16
N_KV_HEADS = 4
Q_PER_KV = N_Q_HEADS // N_KV_HEADS
N_SHARDS = 4
N_DEV = 8

ADAM_LR = 0.001
ADAM_B1 = 0.9
ADAM_B2 = 0.999
ADAM_EPS = 1e-08
ADAM_WD = 0.01
ADAM_STEP = 10
ADAM_C1 = 1.0 - ADAM_B1 ** ADAM_STEP
ADAM_C2 = 1.0 - ADAM_B2 ** ADAM_STEP

_INV_SQRT2 = 1.0 / math.sqrt(2.0)
_INV_SQRT2PI = 1.0 / math.sqrt(2.0 * math.pi)


def _params(*sem):
    return pltpu.CompilerParams(dimension_semantics=sem)


def _gelu(z):
    return 0.5 * z * (1.0 + lax.erf(z * _INV_SQRT2))


def _gelu_grad(z):
    return 0.5 * (1.0 + lax.erf(z * _INV_SQRT2)) + z * jnp.exp(-0.5 * z * z) * _INV_SQRT2PI


def _dot(a, b):
    return jnp.dot(a, b, preferred_element_type=f32)


def _dot_nt(a, b):
    return lax.dot_general(a, b, (((1,), (1,)), ((), ())), preferred_element_type=f32)


def _dot_tn(a, b):
    return lax.dot_general(a, b, (((0,), (0,)), ((), ())), preferred_element_type=f32)


def _dot_exact(a, b):
    return jnp.dot(a, b, preferred_element_type=f32, precision=lax.Precision.HIGHEST)


VMEM_TILE_BUDGET = 40 * 1024 * 1024
MAX_ROW_TILE = 2048


def _row_tile(m, fixed_bytes, row_bytes):
    tm = min(m, MAX_ROW_TILE)
    while tm > 256 and 2 * (fixed_bytes + tm * row_bytes) > VMEM_TILE_BUDGET:
        tm //= 2
    return tm


def _isz(a):
    return jnp.dtype(a.dtype).itemsize


def mm_nn(a, w3, *, name, s0=0, ns=None, add=None, out_dtype=f32):
    m, k = a.shape
    s_all, _, n_s = w3.shape
    ns = s_all if ns is None else ns
    tm = _row_tile(m, k * n_s * 2, k * _isz(a) + n_s * jnp.dtype(out_dtype).itemsize + (0 if add is None else n_s * 4))

    def body(*refs):
        if add is None:
            a_ref, w_ref, o_ref = refs
            acc = _dot(a_ref[...].astype(bf16), w_ref[0])
        else:
            a_ref, w_ref, add_ref, o_ref = refs
            acc = _dot(a_ref[...].astype(bf16), w_ref[0]) + add_ref[...]
        o_ref[...] = acc.astype(out_dtype)

    in_specs = [pl.BlockSpec((tm, k), lambda j, i: (i, 0)),
                pl.BlockSpec((1, k, n_s), lambda j, i: (s0 + j, 0, 0))]
    args = [a, w3]
    if add is not None:
        in_specs.append(pl.BlockSpec((tm, n_s), lambda j, i: (i, j)))
        args.append(add)
    return pl.pallas_call(
        body, name=name, grid=(ns, m // tm), in_specs=in_specs,
        out_specs=pl.BlockSpec((tm, n_s), lambda j, i: (i, j)),
        out_shape=jax.ShapeDtypeStruct((m, ns * n_s), out_dtype),
        compiler_params=_params("parallel", "parallel"))(*args)


def mm_nt(a_list, w3, *, name, tko=None, add=None, out_dtype=f32):
    s_all, k_out, n_s = w3.shape
    m = a_list[0].shape[0]
    na = len(a_list)
    spa = s_all // na
    tko = k_out if tko is None else tko
    tm = _row_tile(m, tko * n_s * 2, na * n_s * _isz(a_list[0]) + tko * 4 * (1 if add is None else 2))

    def body(*refs):
        a_refs = refs[:na]
        w_ref = refs[na]
        o_ref = refs[-1]
        s = pl.program_id(2)

        @pl.when(s == 0)
        def _():
            if add is None:
                o_ref[...] = jnp.zeros_like(o_ref)
            else:
                o_ref[...] = refs[na + 1][...]

        for idx in range(na):
            @pl.when(s // spa == idx)
            def _(idx=idx):
                o_ref[...] += _dot_nt(a_refs[idx][...].astype(bf16), w_ref[0])

    def a_map(idx):
        return lambda ko, i, s: (i, jnp.clip(s - idx * spa, 0, spa - 1))

    in_specs = [pl.BlockSpec((tm, n_s), a_map(idx)) for idx in range(na)]
    in_specs.append(pl.BlockSpec((1, tko, n_s), lambda ko, i, s: (s, ko, 0)))
    args = list(a_list) + [w3]
    if add is not None:
        in_specs.append(pl.BlockSpec((tm, tko), lambda ko, i, s: (i, ko)))
        args.append(add)
    return pl.pallas_call(
        body, name=name, grid=(k_out // tko, m // tm, s_all), in_specs=in_specs,
        out_specs=pl.BlockSpec((tm, tko), lambda ko, i, s: (i, ko)),
        out_shape=jax.ShapeDtypeStruct((m, k_out), out_dtype),
        compiler_params=_params("parallel", "parallel", "arbitrary"))(*args)


def mm_tn(a, b_list, *, name, n_s, tki=None):
    m, k_in = a.shape
    na = len(b_list)
    s_all = sum(b.shape[1] for b in b_list) // n_s
    spa = s_all // na
    tki = k_in if tki is None else tki
    tm = _row_tile(m, tki * n_s * 4, tki * _isz(a) + na * n_s * _isz(b_list[0]))

    def body(*refs):
        a_ref = refs[0]
        b_refs = refs[1:1 + na]
        o_ref = refs[-1]
        s = pl.program_id(0)
        r = pl.program_id(2)

        @pl.when(r == 0)
        def _():
            o_ref[...] = jnp.zeros_like(o_ref)

        for idx in range(na):
            @pl.when(s // spa == idx)
            def _(idx=idx):
                o_ref[0] += _dot_tn(a_ref[...].astype(bf16), b_refs[idx][...].astype(bf16))

    def b_map(idx):
        def index(s, ki, r):
            active = (s // spa) == idx
            return (jnp.where(active, r, 0), jnp.clip(s - idx * spa, 0, spa - 1))
        return index

    in_specs = [pl.BlockSpec((tm, tki), lambda s, ki, r: (r, ki))]
    in_specs += [pl.BlockSpec((tm, n_s), b_map(idx)) for idx in range(na)]
    return pl.pallas_call(
        body, name=name, grid=(s_all, k_in // tki, m // tm), in_specs=in_specs,
        out_specs=pl.BlockSpec((1, tki, n_s), lambda s, ki, r: (s, ki, 0)),
        out_shape=jax.ShapeDtypeStruct((s_all, k_in, n_s), f32),
        compiler_params=_params("parallel", "parallel", "arbitrary"))(a, *b_list)


def mm_nt_rms_bwd(a_list, w3, x, r, g, dx_in, *, name, extra=None):
    s_all, d, n_s = w3.shape
    m = a_list[0].shape[0]
    na = len(a_list)
    spa = s_all // na
    ne = 0 if extra is None else 1
    tm = _row_tile(m, d * n_s * 2, na * n_s * _isz(a_list[0]) + d * 4 * (4 + ne))

    def body(*refs):
        a_refs, w_ref = refs[:na], refs[na]
        x_ref, r_ref, g_ref, dxin_ref = refs[na + 1:na + 5]
        dh2_ref, g2_ref = (refs[na + 5], refs[na + 6]) if ne else (None, None)
        outs = refs[na + 5 + 2 * ne:]
        dx_ref, dg_ref = outs[0], outs[1]
        dg2_ref = outs[2] if ne else None
        acc_ref = outs[-1]
        i, s = pl.program_id(0), pl.program_id(1)

        @pl.when(s == 0)
        def _():
            acc_ref[...] = jnp.zeros_like(acc_ref)

        for idx in range(na):
            @pl.when(s // spa == idx)
            def _(idx=idx):
                acc_ref[...] += _dot_nt(a_refs[idx][...].astype(bf16), w_ref[0])

        @pl.when(s == s_all - 1)
        def _():
            rv = r_ref[...]
            xh = x_ref[...] * rv
            total = dxin_ref[...]
            pairs = [(acc_ref[...], g_ref, dg_ref)] + ([(dh2_ref[...], g2_ref, dg2_ref)] if ne else [])
            for dh, gain_ref, dgain_ref in pairs:
                part = jnp.sum(dh * xh, axis=0, keepdims=True)

                @pl.when(i == 0)
                def _(dgain_ref=dgain_ref, part=part):
                    dgain_ref[...] = part

                @pl.when(i > 0)
                def _(dgain_ref=dgain_ref, part=part):
                    dgain_ref[...] += part

                tg = dh * gain_ref[...]
                total = total + rv * (tg - xh * jnp.mean(tg * xh, axis=1, keepdims=True))
            dx_ref[...] = total

    def a_map(idx):
        return lambda i, s: (i, jnp.clip(s - idx * spa, 0, spa - 1))

    row = pl.BlockSpec((tm, d), lambda i, s: (i, 0))
    vec = pl.BlockSpec((1, d), lambda i, s: (0, 0))
    in_specs = [pl.BlockSpec((tm, n_s), a_map(idx)) for idx in range(na)]
    in_specs += [pl.BlockSpec((1, d, n_s), lambda i, s: (s, 0, 0)), row, pl.BlockSpec((tm, 1), lambda i, s: (i, 0)), vec, row]
    args = list(a_list) + [w3, x, r, g, dx_in]
    if ne:
        in_specs += [row, vec]
        args += list(extra)
    outs = pl.pallas_call(
        body, name=name, grid=(m // tm, s_all), in_specs=in_specs, out_specs=[row] + [vec] * (1 + ne),
        out_shape=[jax.ShapeDtypeStruct((m, d), f32)] + [jax.ShapeDtypeStruct((1, d), f32)] * (1 + ne),
        scratch_shapes=[pltpu.VMEM((tm, d), f32)],
        compiler_params=_params("arbitrary", "arbitrary"))(*args)
    return outs


def mm_residual(a, w, x, *, name, gains=(), target=None):
    m, k = a.shape
    d = w.shape[1]
    ng = len(gains)
    tm = _row_tile(m, k * d * 2, k * _isz(a) + d * 4 * 3 + ng * d * 2)

    def body(*refs):
        a_ref, w_ref, x_ref = refs[:3]
        y = _dot(a_ref[...].astype(bf16), w_ref[...]) + x_ref[...]
        if target is None:
            g_refs = refs[3:3 + ng]
            y_ref = refs[3 + ng]
            h_refs = refs[4 + ng:4 + 2 * ng]
            r_ref = refs[-1]
            y_ref[...] = y
            r = lax.rsqrt(jnp.mean(y * y, axis=1, keepdims=True) + EPS)
            yh = y * r
            for g_ref, h_ref in zip(g_refs, h_refs):
                h_ref[...] = (yh * g_ref[...]).astype(bf16)
            r_ref[...] = r
        else:
            t_ref, dy_ref, s_ref = refs[3:]
            i = pl.program_id(0)
            e = y - t_ref[...]
            dy_ref[...] = e * (1.0 / d)
            part = jnp.sum(e * e, axis=0, keepdims=True)

            @pl.when(i == 0)
            def _():
                s_ref[...] = part

            @pl.when(i > 0)
            def _():
                s_ref[...] += part

    row = pl.BlockSpec((tm, d), lambda i: (i, 0))
    vec = pl.BlockSpec((1, d), lambda i: (0, 0))
    in_specs = [pl.BlockSpec((tm, k), lambda i: (i, 0)), pl.BlockSpec((k, d), lambda i: (0, 0)), row]
    if target is None:
        outs = pl.pallas_call(
            body, name=name, grid=(m // tm,), in_specs=in_specs + [vec] * ng,
            out_specs=[row] * (1 + ng) + [pl.BlockSpec((tm, 1), lambda i: (i, 0))],
            out_shape=[jax.ShapeDtypeStruct((m, d), f32)] + [jax.ShapeDtypeStruct((m, d), bf16)] * ng
            + [jax.ShapeDtypeStruct((m, 1), f32)],
            compiler_params=_params("parallel"))(a, w, x, *gains)
        return outs[0], outs[1:1 + ng], outs[-1]
    return pl.pallas_call(
        body, name=name, grid=(m // tm,), in_specs=in_specs + [row], out_specs=[row, vec],
        out_shape=[jax.ShapeDtypeStruct((m, d), f32), jax.ShapeDtypeStruct((1, d), f32)],
        compiler_params=_params("arbitrary"))(a, w, x, target)


def rms_fwd(x, gains, *, name, tr=512):
    t, d = x.shape
    tr = min(tr, t)
    ng = len(gains)

    def body(*refs):
        x_ref = refs[0]
        g_refs = refs[1:1 + ng]
        h_refs = refs[1 + ng:1 + 2 * ng]
        r_ref = refs[-1]
        xv = x_ref[...]
        r = lax.rsqrt(jnp.mean(xv * xv, axis=1, keepdims=True) + EPS)
        xh = xv * r
        for g_ref, h_ref in zip(g_refs, h_refs):
            h_ref[...] = (xh * g_ref[...]).astype(bf16)
        r_ref[...] = r

    row = pl.BlockSpec((tr, d), lambda i: (i, 0))
    vec = pl.BlockSpec((1, d), lambda i: (0, 0))
    outs = pl.pallas_call(
        body, name=name, grid=(t // tr,), in_specs=[row] + [vec] * ng,
        out_specs=[row] * ng + [pl.BlockSpec((tr, 1), lambda i: (i, 0))],
        out_shape=[jax.ShapeDtypeStruct((t, d), bf16)] * ng + [jax.ShapeDtypeStruct((t, 1), f32)],
        compiler_params=_params("parallel"))(x, *gains)
    return outs[:ng], outs[ng]


def rms_bwd(dh_list, x, r, gains, dx_in, *, name, tr=512):
    t, d = x.shape
    tr = min(tr, t)
    ng = len(gains)

    def body(*refs):
        dh_refs = refs[:ng]
        x_ref, r_ref = refs[ng], refs[ng + 1]
        g_refs = refs[ng + 2:2 * ng + 2]
        dxin_ref = refs[2 * ng + 2]
        dx_ref = refs[2 * ng + 3]
        dg_refs = refs[2 * ng + 4:]
        i = pl.program_id(0)
        rv = r_ref[...]
        xh = x_ref[...] * rv
        acc = dxin_ref[...]
        for dh_ref, g_ref, dg_ref in zip(dh_refs, g_refs, dg_refs):
            dh = dh_ref[...]
            part = jnp.sum(dh * xh, axis=0, keepdims=True)

            @pl.when(i == 0)
            def _(dg_ref=dg_ref, part=part):
                dg_ref[...] = part

            @pl.when(i > 0)
            def _(dg_ref=dg_ref, part=part):
                dg_ref[...] += part

            tg = dh * g_ref[...]
            acc = acc + rv * (tg - xh * jnp.mean(tg * xh, axis=1, keepdims=True))
        dx_ref[...] = acc

    row = pl.BlockSpec((tr, d), lambda i: (i, 0))
    vec = pl.BlockSpec((1, d), lambda i: (0, 0))
    outs = pl.pallas_call(
        body, name=name, grid=(t // tr,),
        in_specs=[row] * ng + [row, pl.BlockSpec((tr, 1), lambda i: (i, 0))] + [vec] * ng + [row],
        out_specs=[row] + [vec] * ng,
        out_shape=[jax.ShapeDtypeStruct((t, d), f32)] + [jax.ShapeDtypeStruct((1, d), f32)] * ng,
        compiler_params=_params("arbitrary"))(*dh_list, x, r, *gains, dx_in)
    return outs[0], outs[1:]


def sgu_gate_fwd(zu, zv, gv, wc, bt, *, name, tr=512):
    t, w = zu.shape
    tr = min(tr, t)
    groups = w // LANES

    def body(zu_ref, zv_ref, gv_ref, wc_ref, bt_ref, y_ref):
        vp = _gelu(zv_ref[...])
        rv = lax.rsqrt(jnp.mean(vp * vp, axis=1, keepdims=True) + EPS)
        vb = (vp * rv * gv_ref[...]).astype(bf16)
        for c in range(tr // CHUNK):
            rows = slice(c * CHUNK, (c + 1) * CHUNK)
            for g in range(groups):
                cols = slice(g * LANES, (g + 1) * LANES)
                sv = _dot(wc_ref[g], vb[rows, cols]) + bt_ref[:, g:g + 1]
                y_ref[rows, cols] = (_gelu(zu_ref[rows, cols]) * sv).astype(bf16)

    row = pl.BlockSpec((tr, w), lambda i: (i, 0))
    return pl.pallas_call(
        body, name=name, grid=(t // tr,),
        in_specs=[row, row, pl.BlockSpec((1, w), lambda i: (0, 0)),
                  pl.BlockSpec((groups, CHUNK, CHUNK), lambda i: (0, 0, 0)),
                  pl.BlockSpec((CHUNK, groups), lambda i: (0, 0))],
        out_specs=row, out_shape=jax.ShapeDtypeStruct((t, w), bf16),
        compiler_params=_params("parallel"))(zu, zv, gv, wc, bt)


def sgu_gate_bwd(zu, zv, dy, gv, wc, bt, *, name, tr=512):
    t, w = zu.shape
    tr = min(tr, t)
    groups = w // LANES
    nsteps = t // tr

    def body(zu_ref, zv_ref, dy_ref, gv_ref, wc_ref, bt_ref,
             dzu_ref, dzv_ref, dgv_ref, dws_ref, dbt_ref, dv_ref, bacc_ref):
        i = pl.program_id(0)

        @pl.when(i == 0)
        def _():
            dgv_ref[...] = jnp.zeros_like(dgv_ref)
            dws_ref[...] = jnp.zeros_like(dws_ref)
            bacc_ref[...] = jnp.zeros_like(bacc_ref)

        zvv = zv_ref[...]
        vp = _gelu(zvv)
        rv = lax.rsqrt(jnp.mean(vp * vp, axis=1, keepdims=True) + EPS)
        vhat = vp * rv
        vb = (vhat * gv_ref[...]).astype(bf16)
        for c in range(tr // CHUNK):
            rows = slice(c * CHUNK, (c + 1) * CHUNK)
            for g in range(groups):
                cols = slice(g * LANES, (g + 1) * LANES)
                vblk = vb[rows, cols]
                sv = _dot(wc_ref[g], vblk) + bt_ref[:, g:g + 1]
                zub = zu_ref[rows, cols]
                dyb = dy_ref[rows, cols]
                dzu_ref[rows, cols] = (dyb * sv * _gelu_grad(zub)).astype(bf16)
                dsv = dyb * _gelu(zub)
                bacc_ref[:, cols] += dsv
                dsvb = dsv.astype(bf16)
                dv_ref[rows, cols] = _dot_tn(wc_ref[g], dsvb)
                dws_ref[g] += _dot_nt(dsvb, vblk)
        dv = dv_ref[...]
        dgv_ref[...] += jnp.sum(dv * vhat, axis=0, keepdims=True)
        tg = dv * gv_ref[...]
        dvp = rv * (tg - vhat * jnp.mean(tg * vhat, axis=1, keepdims=True))
        dzv_ref[...] = (dvp * _gelu_grad(zvv)).astype(bf16)

        @pl.when(i == nsteps - 1)
        def _():
            tt = lax.broadcasted_iota(jnp.int32, (CHUNK, CHUNK), 0)
            ss = lax.broadcasted_iota(jnp.int32, (CHUNK, CHUNK), 1)
            for g in range(groups):
                dws_ref[g] = jnp.where(ss <= tt, dws_ref[g], 0.0)
                dbt_ref[:, g:g + 1] = jnp.sum(bacc_ref[:, g * LANES:(g + 1) * LANES], axis=1, keepdims=True)

    row = pl.BlockSpec((tr, w), lambda i: (i, 0))
    full3 = pl.BlockSpec((groups, CHUNK, CHUNK), lambda i: (0, 0, 0))
    return pl.pallas_call(
        body, name=name, grid=(nsteps,),
        in_specs=[row, row, row, pl.BlockSpec((1, w), lambda i: (0, 0)), full3,
                  pl.BlockSpec((CHUNK, groups), lambda i: (0, 0))],
        out_specs=[row, row, pl.BlockSpec((1, w), lambda i: (0, 0)), full3,
                   pl.BlockSpec((CHUNK, groups), lambda i: (0, 0))],
        out_shape=[jax.ShapeDtypeStruct((t, w), bf16), jax.ShapeDtypeStruct((t, w), bf16),
                   jax.ShapeDtypeStruct((1, w), f32), jax.ShapeDtypeStruct((groups, CHUNK, CHUNK), f32),
                   jax.ShapeDtypeStruct((CHUNK, groups), f32)],
        scratch_shapes=[pltpu.VMEM((tr, w), f32), pltpu.VMEM((CHUNK, w), f32)],
        compiler_params=_params("arbitrary"))(zu, zv, dy, gv, wc, bt)


HALO = 8


def _shift_down(v, halo, k, first):
    r = pltpu.roll(v, k, 0)
    hh = jnp.where(first, 0.0, pltpu.roll(halo, k, 0))
    rid = lax.broadcasted_iota(jnp.int32, (HALO, v.shape[1]), 0)
    head = jnp.where(rid < k, hh, r[0:HALO])
    if v.shape[0] == HALO:
        return head
    return jnp.concatenate([head, r[HALO:]], axis=0)


def _shift_up(v, halo, k, last):
    n = v.shape[0]
    r = pltpu.roll(v, n - k, 0)
    hh = jnp.where(last, 0.0, pltpu.roll(halo, HALO - k, 0))
    rid = lax.broadcasted_iota(jnp.int32, (HALO, v.shape[1]), 0)
    tail = jnp.where(rid >= HALO - k, hh, r[n - HALO:])
    return jnp.concatenate([r[:n - HALO], tail], axis=0)


def _conv(p, halo, w_ref, b_ref, first):
    return (w_ref[2:3, :] * p + w_ref[1:2, :] * _shift_down(p, halo, 1, first)
            + w_ref[0:1, :] * _shift_down(p, halo, 2, first) + b_ref[...])


BF16_ROWS = 16


def ffn_in_fused(h, w_in4, wg, wu, bg, bu, *, name):
    t, k = h.shape
    s_all, _, n_s = w_in4.shape
    half = s_all // 2
    tm = _row_tile(t, 2 * k * n_s * 2, k * 2 + 4 * n_s * 4 + n_s * 2)

    def body(h_ref, hh_ref, wg_ref, wu_ref, cg_ref, cu_ref, bg_ref, bu_ref, pg_ref, pu_ref, gate_ref, up_ref, a_ref):
        first = pl.program_id(1) == 0
        hv, hh = h_ref[...], hh_ref[...]
        outs = []
        for w_ref, c_ref, b_ref, p_ref, o_ref in ((wg_ref, cg_ref, bg_ref, pg_ref, gate_ref),
                                                  (wu_ref, cu_ref, bu_ref, pu_ref, up_ref)):
            p = _dot(hv, w_ref[0])
            p_ref[...] = p
            hu = _conv(p, _dot(hh, w_ref[0])[BF16_ROWS - HALO:], c_ref, b_ref, first)
            o_ref[...] = hu
            outs.append(hu)
        gate, up = outs
        a_ref[...] = (gate * jax.nn.sigmoid(gate) * up).astype(bf16)

    tile = pl.BlockSpec((tm, n_s), lambda j, i: (i, j))
    cw = pl.BlockSpec((3, n_s), lambda j, i: (0, j))
    cb = pl.BlockSpec((1, n_s), lambda j, i: (0, j))
    f = half * n_s
    return pl.pallas_call(
        body, name=name, grid=(half, t // tm),
        in_specs=[pl.BlockSpec((tm, k), lambda j, i: (i, 0)),
                  pl.BlockSpec((BF16_ROWS, k), lambda j, i: (jnp.maximum(i * (tm // BF16_ROWS) - 1, 0), 0)),
                  pl.BlockSpec((1, k, n_s), lambda j, i: (j, 0, 0)),
                  pl.BlockSpec((1, k, n_s), lambda j, i: (j + half, 0, 0)), cw, cw, cb, cb],
        out_specs=[tile] * 5,
        out_shape=[jax.ShapeDtypeStruct((t, f), f32)] * 4 + [jax.ShapeDtypeStruct((t, f), bf16)],
        compiler_params=_params("parallel", "parallel"))(h, h, w_in4, w_in4, wg, wu, bg, bu)


def _gate_grads(gate, up, dav):
    sg = jax.nn.sigmoid(gate)
    return dav * up * (sg * (1.0 + gate * (1.0 - sg))), dav * gate * sg


GATE_BWD_ROWS = 256


def ffn_gate_bwd(dy, w_out, pg, pu, gate, up, wg, wu, *, name):
    t, f = pg.shape
    d = dy.shape[1]
    tr = min(GATE_BWD_ROWS, t)
    nsteps = t // tr
    tc = f // 2

    def body(dy_ref, dyn_ref, w_ref, pg_ref, pu_ref, gate_ref, gaten_ref, up_ref, upn_ref, wg_ref, wu_ref,
             dg_ref, du_ref, sg_ref, su_ref):
        i = pl.program_id(1)
        last = i == nsteps - 1
        w = w_ref[0]
        da = _dot_nt(dy_ref[...].astype(bf16), w)
        da_n = _dot_nt(dyn_ref[...].astype(bf16), w)
        dgate, dup = _gate_grads(gate_ref[...], up_ref[...], da)
        dgate_n, dup_n = _gate_grads(gaten_ref[...], upn_ref[...], da_n)
        rid = lax.broadcasted_iota(jnp.int32, (8, tc), 0)
        for dd, d_n, c_ref, p_ref, o_ref, s_ref in ((dgate, dgate_n, wg_ref, pg_ref, dg_ref, sg_ref),
                                                    (dup, dup_n, wu_ref, pu_ref, du_ref, su_ref)):
            d1, d2 = _shift_up(dd, d_n, 1, last), _shift_up(dd, d_n, 2, last)
            o_ref[...] = (c_ref[2:3, :] * dd + c_ref[1:2, :] * d1 + c_ref[0:1, :] * d2).astype(bf16)
            p = p_ref[...]
            sums = [jnp.sum(d2 * p, axis=0, keepdims=True), jnp.sum(d1 * p, axis=0, keepdims=True),
                    jnp.sum(dd * p, axis=0, keepdims=True), jnp.sum(dd, axis=0, keepdims=True)]
            part = jnp.zeros((8, tc), f32)
            for k, sk in enumerate(sums):
                part = jnp.where(rid == k, sk, part)

            @pl.when(i == 0)
            def _(s_ref=s_ref, part=part):
                s_ref[...] = part

            @pl.when(i > 0)
            def _(s_ref=s_ref, part=part):
                s_ref[...] += part

    def nxt_rows(j, i):
        return (jnp.minimum((i + 1) * (tr // HALO), t // HALO - 1), j)

    tile = pl.BlockSpec((tr, tc), lambda j, i: (i, j))
    nxt = pl.BlockSpec((HALO, tc), nxt_rows)
    wspec = pl.BlockSpec((3, tc), lambda j, i: (0, j))
    stat = pl.BlockSpec((8, tc), lambda j, i: (0, j))
    return pl.pallas_call(
        body, name=name, grid=(2, nsteps),
        in_specs=[pl.BlockSpec((tr, d), lambda j, i: (i, 0)),
                  pl.BlockSpec((HALO, d), lambda j, i: (nxt_rows(j, i)[0], 0)),
                  pl.BlockSpec((1, tc, d), lambda j, i: (j, 0, 0)),
                  tile, tile, tile, nxt, tile, nxt, wspec, wspec],
        out_specs=[tile, tile, stat, stat],
        out_shape=[jax.ShapeDtypeStruct((t, f), bf16), jax.ShapeDtypeStruct((t, f), bf16),
                   jax.ShapeDtypeStruct((8, f), f32), jax.ShapeDtypeStruct((8, f), f32)],
        compiler_params=_params("parallel", "arbitrary"))(
            dy, dy, w_out.reshape(2, tc, d), pg, pu, gate, gate, up, up, wg, wu)


def _head_mean_matrix():
    i = lax.broadcasted_iota(jnp.int32, (LANES, LANES), 0) // HEAD_DIM
    j = lax.broadcasted_iota(jnp.int32, (LANES, LANES), 1) // HEAD_DIM
    return jnp.where(i == j, 1.0 / HEAD_DIM, 0.0).astype(f32)


def _lane_half(shape):
    return (lax.broadcasted_iota(jnp.int32, shape, 1) % LANES) // HEAD_DIM


def q_norm_fwd(qp, g2, *, name, scale, tr=512):
    t, w = qp.shape
    tr = min(tr, t)

    def body(x_ref, g_ref, o_ref):
        bd = _head_mean_matrix()
        for cb in range(w // LANES):
            cols = slice(cb * LANES, (cb + 1) * LANES)
            xc = x_ref[:, cols]
            rh = lax.rsqrt(_dot_exact(xc * xc, bd) + EPS)
            o_ref[:, cols] = (xc * rh * g_ref[...] * scale).astype(bf16)

    row = pl.BlockSpec((tr, w), lambda i: (i, 0))
    return pl.pallas_call(
        body, name=name, grid=(t // tr,), in_specs=[row, pl.BlockSpec((1, LANES), lambda i: (0, 0))],
        out_specs=row, out_shape=jax.ShapeDtypeStruct((t, w), bf16),
        compiler_params=_params("parallel"))(qp, g2)


def q_norm_bwd(dq, qp, g2, *, name, scale, tr=512):
    t, w = qp.shape
    tr = min(tr, t)

    def body(dq_ref, x_ref, g_ref, o_ref, dg_ref):
        i = pl.program_id(0)
        bd = _head_mean_matrix()
        acc = jnp.zeros((1, LANES), f32)
        for cb in range(w // LANES):
            cols = slice(cb * LANES, (cb + 1) * LANES)
            xc = x_ref[:, cols]
            rh = lax.rsqrt(_dot_exact(xc * xc, bd) + EPS)
            xh = xc * rh
            dy = dq_ref[:, cols] * scale
            acc = acc + jnp.sum(dy * xh, axis=0, keepdims=True)
            tg = dy * g_ref[...]
            o_ref[:, cols] = (rh * (tg - xh * _dot_exact(tg * xh, bd))).astype(bf16)

        @pl.when(i == 0)
        def _():
            dg_ref[...] = acc

        @pl.when(i > 0)
        def _():
            dg_ref[...] += acc

    row = pl.BlockSpec((tr, w), lambda i: (i, 0))
    vec = pl.BlockSpec((1, LANES), lambda i: (0, 0))
    return pl.pallas_call(
        body, name=name, grid=(t // tr,), in_specs=[row, row, vec], out_specs=[row, vec],
        out_shape=[jax.ShapeDtypeStruct((t, w), bf16), jax.ShapeDtypeStruct((1, LANES), f32)],
        compiler_params=_params("arbitrary"))(dq, qp, g2)


def kv_post_fwd(kv, g2, *, name, tr=512):
    t, w = kv.shape
    tr = min(tr, t)
    kw = w // 2

    def body(x_ref, g_ref, k_ref, v_ref):
        bd = _head_mean_matrix()
        half = _lane_half((tr, LANES))
        for cb in range(kw // LANES):
            xc = x_ref[:, cb * LANES:(cb + 1) * LANES]
            rh = lax.rsqrt(_dot_exact(xc * xc, bd) + EPS)
            kn = xc * rh * g_ref[...]
            vc = x_ref[:, kw + cb * LANES:kw + (cb + 1) * LANES]
            for src, dst in ((kn, k_ref), (vc, v_ref)):
                sw = pltpu.roll(src, HEAD_DIM, 1)
                for hf in range(2):
                    blk = 2 * cb + hf
                    dst[:, blk * LANES:(blk + 1) * LANES] = jnp.where(half == hf, src, sw).astype(bf16)

    return pl.pallas_call(
        body, name=name, grid=(t // tr,),
        in_specs=[pl.BlockSpec((tr, w), lambda i: (i, 0)), pl.BlockSpec((1, LANES), lambda i: (0, 0))],
        out_specs=[pl.BlockSpec((tr, 2 * kw), lambda i: (i, 0))] * 2,
        out_shape=[jax.ShapeDtypeStruct((t, 2 * kw), bf16)] * 2,
        compiler_params=_params("parallel"))(kv, g2)


def kv_post_bwd(dk2, dv2, kv, g2, *, name, tr=512):
    t, w = kv.shape
    tr = min(tr, t)
    kw = w // 2

    def body(dk_ref, dv_ref, x_ref, g_ref, o_ref, dg_ref):
        i = pl.program_id(0)
        bd = _head_mean_matrix()
        half = _lane_half((tr, LANES))
        acc = jnp.zeros((1, LANES), f32)

        def fold(ref, cb):
            a = ref[:, (2 * cb) * LANES:(2 * cb + 1) * LANES]
            b = ref[:, (2 * cb + 1) * LANES:(2 * cb + 2) * LANES]
            return jnp.where(half == 0, a + pltpu.roll(a, HEAD_DIM, 1), b + pltpu.roll(b, HEAD_DIM, 1))

        for cb in range(kw // LANES):
            cols = slice(cb * LANES, (cb + 1) * LANES)
            xc = x_ref[:, cols]
            rh = lax.rsqrt(_dot_exact(xc * xc, bd) + EPS)
            xh = xc * rh
            dy = fold(dk_ref, cb)
            acc = acc + jnp.sum(dy * xh, axis=0, keepdims=True)
            tg = dy * g_ref[...]
            o_ref[:, cols] = (rh * (tg - xh * _dot_exact(tg * xh, bd))).astype(bf16)
            o_ref[:, kw + cb * LANES:kw + (cb + 1) * LANES] = fold(dv_ref, cb).astype(bf16)

        @pl.when(i == 0)
        def _():
            dg_ref[...] = acc

        @pl.when(i > 0)
        def _():
            dg_ref[...] += acc

    dup = pl.BlockSpec((tr, 2 * kw), lambda i: (i, 0))
    row = pl.BlockSpec((tr, w), lambda i: (i, 0))
    vec = pl.BlockSpec((1, LANES), lambda i: (0, 0))
    return pl.pallas_call(
        body, name=name, grid=(t // tr,), in_specs=[dup, dup, row, vec], out_specs=[row, vec],
        out_shape=[jax.ShapeDtypeStruct((t, w), bf16), jax.ShapeDtypeStruct((1, LANES), f32)],
        compiler_params=_params("arbitrary"))(dk2, dv2, kv, g2)


def _slope(h):
    return 2.0 ** (-8.0 * (h + 1) / N_Q_HEADS)


GROUP_ROWS = Q_PER_KV * CHUNK


def _band_mask(n):
    tq = lax.broadcasted_iota(jnp.int32, (GROUP_ROWS, 2 * CHUNK), 0) % CHUNK
    jk = lax.broadcasted_iota(jnp.int32, (GROUP_ROWS, 2 * CHUNK), 1)
    dist = tq + CHUNK - jk
    ok = (dist >= 0) & (dist < CHUNK) & jnp.logical_not((n == 0) & (jk < CHUNK))
    return dist.astype(f32), ok


def _band(ref, n, kh):
    p0 = pl.multiple_of(jnp.maximum(n - 1, 0) * CHUNK, CHUNK)
    c0 = pl.multiple_of(n * CHUNK, CHUNK)
    cols = slice(kh * LANES, (kh + 1) * LANES)
    return jnp.concatenate([ref[pl.ds(p0, CHUNK), cols], ref[pl.ds(c0, CHUNK), cols]], axis=0)


def _stack_heads(ref, kh, half):
    parts = []
    for cb in (2 * kh, 2 * kh + 1):
        xc = ref[:, cb * LANES:(cb + 1) * LANES].astype(f32)
        parts += [jnp.where(half == hf, xc, 0.0).astype(bf16) for hf in range(2)]
    return jnp.concatenate(parts, axis=0)


def _unstack_heads(x4, half):
    return (jnp.where(half == 0, x4[0:CHUNK], x4[CHUNK:2 * CHUNK]),
            jnp.where(half == 0, x4[2 * CHUNK:3 * CHUNK], x4[3 * CHUNK:]))


def _per_head_column(kh, values):
    grp = lax.broadcasted_iota(jnp.int32, (GROUP_ROWS, 1), 0) // CHUNK
    col = jnp.full((GROUP_ROWS, 1), values[0], f32)
    for g in range(1, Q_PER_KV):
        col = jnp.where(grp == g, values[g], col)
    return col


def _softmax_band(q4, kband, dist, ok, slope, sink):
    s = _dot_nt(q4, kband)
    s = jnp.where(ok, s - slope * dist, -jnp.inf)
    m = jnp.maximum(jnp.max(s, axis=1, keepdims=True), sink)
    e = jnp.exp(s - m)
    es = jnp.exp(sink - m)
    den = jnp.sum(e, axis=1, keepdims=True) + es
    return e / den, es / den


def attn_fwd(q, k2, v2, sinks, *, name):
    t, w = q.shape
    nb = t // CHUNK

    def body(sink_ref, q_ref, k_ref, v_ref, o_ref):
        n = pl.program_id(0)
        dist, ok = _band_mask(n)
        half = _lane_half((CHUNK, LANES))
        for kh in range(N_KV_HEADS):
            heads = [Q_PER_KV * kh + g for g in range(Q_PER_KV)]
            slope = _per_head_column(kh, [_slope(h) for h in heads])
            sink = _per_head_column(kh, [sink_ref[h] for h in heads])
            q4 = _stack_heads(q_ref, kh, half)
            p, _ = _softmax_band(q4, _band(k_ref, n, kh), dist, ok, slope, sink)
            o4 = _dot(p.astype(bf16), _band(v_ref, n, kh))
            lo, hi = _unstack_heads(o4, half)
            o_ref[:, (2 * kh) * LANES:(2 * kh + 1) * LANES] = lo.astype(bf16)
            o_ref[:, (2 * kh + 1) * LANES:(2 * kh + 2) * LANES] = hi.astype(bf16)

    full = pl.BlockSpec((t, k2.shape[1]), lambda n: (0, 0))
    return pl.pallas_call(
        body, name=name, grid=(nb,),
        in_specs=[pl.BlockSpec(memory_space=pltpu.SMEM), pl.BlockSpec((CHUNK, w), lambda n: (n, 0)), full, full],
        out_specs=pl.BlockSpec((CHUNK, w), lambda n: (n, 0)),
        out_shape=jax.ShapeDtypeStruct((t, w), bf16),
        compiler_params=_params("parallel"))(sinks, q, k2, v2)


def attn_bwd(q, k2, v2, do, sinks, *, name):
    t, w = q.shape
    nb = t // CHUNK
    kw = k2.shape[1]

    def body(sink_ref, q_ref, k_ref, v_ref, do_ref, dq_ref, dk_ref, dv_ref, ds_ref, kc_ref, vc_ref):
        n = pl.program_id(0)

        @pl.when(n == 0)
        def _():
            ds_ref[...] = jnp.zeros_like(ds_ref)
            kc_ref[...] = jnp.zeros_like(kc_ref)
            vc_ref[...] = jnp.zeros_like(vc_ref)
            dk_ref[...] = jnp.zeros_like(dk_ref)
            dv_ref[...] = jnp.zeros_like(dv_ref)

        @pl.when(n == nb)
        def _():
            dk_ref[...] = kc_ref[...]
            dv_ref[...] = vc_ref[...]

        @pl.when(n < nb)
        def _():
            dist, ok = _band_mask(n)
            half = _lane_half((CHUNK, LANES))
            lane = lax.broadcasted_iota(jnp.int32, (1, LANES), 1)
            sink_acc = jnp.zeros((1, LANES), f32)
            for kh in range(N_KV_HEADS):
                heads = [Q_PER_KV * kh + g for g in range(Q_PER_KV)]
                slope = _per_head_column(kh, [_slope(h) for h in heads])
                sink = _per_head_column(kh, [sink_ref[h] for h in heads])
                q4 = _stack_heads(q_ref, kh, half)
                do4 = _stack_heads(do_ref, kh, half)
                kband = _band(k_ref, n, kh)
                vband = _band(v_ref, n, kh)
                p, ps = _softmax_band(q4, kband, dist, ok, slope, sink)
                dp = _dot_nt(do4, vband)
                delta = jnp.sum(p * dp, axis=1, keepdims=True)
                dsb = (p * (dp - delta)).astype(bf16)
                sd = ps * delta
                for g, h in enumerate(heads):
                    part = jnp.sum(sd[g * CHUNK:(g + 1) * CHUNK], axis=0, keepdims=True)
                    sink_acc = sink_acc + jnp.where(lane == h, -part, 0.0)
                lo, hi = _unstack_heads(_dot(dsb, kband), half)
                dq_ref[:, (2 * kh) * LANES:(2 * kh + 1) * LANES] = lo
                dq_ref[:, (2 * kh + 1) * LANES:(2 * kh + 2) * LANES] = hi
                dkb = _dot_tn(dsb, q4)
                dvb = _dot_tn(p.astype(bf16), do4)
                cols = slice(kh * LANES, (kh + 1) * LANES)
                dk_ref[:, cols] = kc_ref[:, cols] + dkb[0:CHUNK]
                dv_ref[:, cols] = vc_ref[:, cols] + dvb[0:CHUNK]
                kc_ref[:, cols] = dkb[CHUNK:]
                vc_ref[:, cols] = dvb[CHUNK:]
            ds_ref[...] += sink_acc

    full = pl.BlockSpec((t, kw), lambda n: (0, 0))
    qblk = pl.BlockSpec((CHUNK, w), lambda n: (jnp.minimum(n, nb - 1), 0))
    kblk = pl.BlockSpec((CHUNK, kw), lambda n: (jnp.maximum(n - 1, 0), 0))
    return pl.pallas_call(
        body, name=name, grid=(nb + 1,),
        in_specs=[pl.BlockSpec(memory_space=pltpu.SMEM), qblk, full, full, qblk],
        out_specs=[qblk, kblk, kblk, pl.BlockSpec((1, LANES), lambda n: (0, 0))],
        out_shape=[jax.ShapeDtypeStruct((t, w), f32), jax.ShapeDtypeStruct((t, kw), f32),
                   jax.ShapeDtypeStruct((t, kw), f32), jax.ShapeDtypeStruct((1, LANES), f32)],
        scratch_shapes=[pltpu.VMEM((CHUNK, kw), f32), pltpu.VMEM((CHUNK, kw), f32)],
        compiler_params=_params("arbitrary"))(sinks, q, k2, v2, do)


def loss_head(y, target, *, name, tr=512):
    t, d = y.shape
    tr = min(tr, t)

    def body(y_ref, t_ref, dy_ref, s_ref):
        i = pl.program_id(0)
        e = y_ref[...] - t_ref[...]
        dy_ref[...] = e * (1.0 / d)
        part = jnp.sum(e * e, axis=0, keepdims=True)

        @pl.when(i == 0)
        def _():
            s_ref[...] = part

        @pl.when(i > 0)
        def _():
            s_ref[...] += part

    row = pl.BlockSpec((tr, d), lambda i: (i, 0))
    vec = pl.BlockSpec((1, d), lambda i: (0, 0))
    return pl.pallas_call(
        body, name=name, grid=(t // tr,), in_specs=[row, row], out_specs=[row, vec],
        out_shape=[jax.ShapeDtypeStruct((t, d), f32), jax.ShapeDtypeStruct((1, d), f32)],
        compiler_params=_params("arbitrary"))(y, target)


N_STEPS = 8


def _row_blocks(shape):
    if len(shape) == 2:
        r, c = shape
        return (r // N_STEPS, c), (lambda s: (s, 0))
    l, r, c = shape
    per = N_STEPS // l
    return (1, r // per, c), (lambda s: (s // per, s % per, 0))


CAST_STEPS = 4


def cast_into_slot(arrays, k_arr, *, name):
    in_specs, out_specs, out_shape, layers = [], [], [], []
    for a in arrays:
        r, c = a.shape[-2:]
        rb = r // CAST_STEPS
        if a.ndim == 2:
            in_specs.append(pl.BlockSpec((rb, c), lambda s, k: (s, 0)))
            layers.append(None)
        else:
            for l in range(a.shape[0]):
                in_specs.append(pl.BlockSpec((1, rb, c), lambda s, k, l=l: (l, s, 0)))
                layers.append(l)
        for _ in range(1 if a.ndim == 2 else a.shape[0]):
            out_specs.append(pl.BlockSpec((1, rb, c), lambda s, k: (k[0], s, 0)))
            out_shape.append(jax.ShapeDtypeStruct((N_SHARDS, r, c), bf16))
    n = len(in_specs)

    def body(k_ref, *refs):
        for i_ref, o_ref, l in zip(refs[:n], refs[n:], layers):
            o_ref[0] = (i_ref[...] if l is None else i_ref[0]).astype(bf16)

    args = []
    for a in arrays:
        args += [a] * (1 if a.ndim == 2 else a.shape[0])
    return pl.pallas_call(
        body, name=name,
        grid_spec=pltpu.PrefetchScalarGridSpec(num_scalar_prefetch=1, grid=(CAST_STEPS,),
                                               in_specs=in_specs, out_specs=out_specs),
        out_shape=out_shape, compiler_params=_params("parallel"))(k_arr, *args)


def adamw(ws, gs, ms, vs, *, name):
    n = len(ws)
    specs, g_specs, g_count = [], [], []
    for w, g_list in zip(ws, gs):
        blk, index = _row_blocks(w.shape)
        specs.append(pl.BlockSpec(blk, index))
        layers = len(g_list)
        per = N_STEPS // layers
        g_count.append(layers)
        for l in range(layers):
            g_specs.append(pl.BlockSpec(blk[-2:], lambda s, l=l, per=per: (jnp.where(s // per == l, s % per, 0), 0)))
    ng = len(g_specs)

    def body(*refs):
        s = pl.program_id(0)
        g_refs = refs[3 * n:3 * n + ng]
        outs = refs[3 * n + ng:]
        off = 0
        for i in range(n):
            w_ref, m_ref, v_ref = refs[i], refs[n + i], refs[2 * n + i]
            go_ref, d_ref, nm_ref, nv_ref = (outs[k * n + i] for k in range(4))
            layers = g_count[i]
            g = g_refs[off][...]
            for l in range(1, layers):
                g = jnp.where(s // (N_STEPS // layers) == l, g_refs[off + l][...], g)
            off += layers
            g = g.reshape(w_ref.shape)
            m = ADAM_B1 * m_ref[...] + (1.0 - ADAM_B1) * g
            v = ADAM_B2 * v_ref[...] + (1.0 - ADAM_B2) * (g * g)
            m_hat = m / ADAM_C1
            v_hat = v / ADAM_C2
            go_ref[...] = g
            d_ref[...] = -ADAM_LR * (m_hat / (jnp.sqrt(v_hat) + ADAM_EPS) + ADAM_WD * w_ref[...])
            nm_ref[...] = m
            nv_ref[...] = v

    outs = pl.pallas_call(
        body, name=name, grid=(N_STEPS,), in_specs=specs * 3 + g_specs, out_specs=specs * 4,
        out_shape=[jax.ShapeDtypeStruct(a.shape, f32) for a in ws] * 4,
        compiler_params=_params("parallel"))(*ws, *ms, *vs, *[g for g_list in gs for g in g_list])
    return [outs[k * n:(k + 1) * n] for k in range(4)]


def _place():
    return lax.axis_index("x"), lax.axis_index("y"), lax.axis_index("c")


def gather_shards(bufs, *, name, split):
    n = len(bufs)

    def body(*refs):
        bufs_ = refs[:n]
        isend, irecv, dsend, drecv = refs[2 * n:]
        x, y, c = _place()
        k = 2 * x + y
        peers = [(1 - x, y, c), (x, 1 - y, c), (1 - x, 1 - y, c)]
        peer_k = [2 * (1 - x) + y, 2 * x + (1 - y), 2 * (1 - x) + (1 - y)]

        def slab(a, q, h):
            if not split[a]:
                return bufs_[a].at[q]
            half = bufs_[a].shape[1] // 2
            return bufs_[a].at[q, pl.ds(pl.multiple_of(h * half, 16), half)]

        def ici(a, j, q):
            return pltpu.make_async_remote_copy(
                src_ref=slab(a, q, c), dst_ref=slab(a, q, c), send_sem=isend.at[3 * a + j], recv_sem=irecv.at[3 * a + j],
                device_id=peers[j], device_id_type=MESH)

        def d2d(a, j, h):
            return pltpu.make_async_remote_copy(
                src_ref=slab(a, peer_k[j], h), dst_ref=slab(a, peer_k[j], h), send_sem=dsend.at[3 * a + j],
                recv_sem=drecv.at[3 * a + j], device_id=(x, y, 1 - c), device_id_type=MESH)

        for a in range(n):
            for j in range(3):
                ici(a, j, k).start()
        for a in range(n):
            for j in range(3):
                ici(a, j, peer_k[j]).wait_recv()
                if split[a]:
                    d2d(a, j, c).start()
        for a in range(n):
            for j in range(3):
                if split[a]:
                    d2d(a, j, 1 - c).wait_recv()
        for a in range(n):
            for j in range(3):
                ici(a, j, k).wait_send()
                if split[a]:
                    d2d(a, j, c).wait_send()

    return pl.pallas_call(
        body, name=name, in_specs=[ANY] * n, out_specs=[ANY] * n,
        out_shape=[jax.ShapeDtypeStruct(b.shape, b.dtype) for b in bufs],
        input_output_aliases={i: i for i in range(n)},
        scratch_shapes=[pltpu.SemaphoreType.DMA((3 * n,))] * 4)(*bufs)


HBM = pl.BlockSpec(memory_space=pltpu.HBM)
SEM = pl.BlockSpec(memory_space=pltpu.SEMAPHORE)
DATAFLOW = pltpu.SideEffectType.DATAFLOW_SIDE_EFFECTING


def _chip_peers():
    x, y, c = _place()
    return 2 * x + y, [(1 - x, y, c), (x, 1 - y, c), (1 - x, 1 - y, c)], [2 * (1 - x) + y, 2 * x + (1 - y), 2 * (1 - x) + (1 - y)]


def gather_start(bufs, groups, after, *, name):
    n = len(bufs)
    ng = len(groups)

    def body(*refs):
        ins = refs[:n]
        sends, recvs = refs[2 * n + 1:2 * n + 1 + ng], refs[2 * n + 1 + ng:2 * n + 1 + 2 * ng]
        token = refs[-1]
        k, peers, _ = _chip_peers()
        for gi, grp in enumerate(groups):
            for pos, a in enumerate(grp):
                for j in range(3):
                    pltpu.make_async_remote_copy(
                        src_ref=ins[a].at[k], dst_ref=ins[a].at[k], send_sem=sends[gi].at[3 * pos + j],
                        recv_sem=recvs[gi].at[3 * pos + j], device_id=peers[j], device_id_type=MESH).start()
        token[...] = jnp.zeros_like(token)

    sems = [pltpu.SemaphoreType.DMA((3 * len(grp),)) for grp in groups]
    outs = pl.pallas_call(
        body, name=name, in_specs=[HBM] * n + [ANY],
        out_specs=[HBM] * n + [SEM] * (2 * ng) + [pl.BlockSpec(memory_space=pltpu.VMEM)],
        out_shape=[pltpu.HBM(b.shape, b.dtype) for b in bufs] + sems + sems + [jax.ShapeDtypeStruct((8, LANES), f32)],
        input_output_aliases={i: i for i in range(n)},
        compiler_params=pltpu.CompilerParams(has_side_effects=DATAFLOW))(
            *[pltpu.with_memory_space_constraint(b, pltpu.HBM) for b in bufs], after)
    return outs[:n], outs[n:n + ng], outs[n + ng:n + 2 * ng], outs[-1]


def gather_wait(bufs, send_sems, recv_sems, after, *, name):
    n = len(bufs)

    def body(*refs):
        ins = refs[:n]
        send, recv = refs[n], refs[n + 1]
        k, peers, peer_k = _chip_peers()
        for a in range(n):
            for j in range(3):
                copy = pltpu.make_async_remote_copy(
                    src_ref=ins[a].at[k], dst_ref=ins[a].at[peer_k[j]], send_sem=send.at[3 * a + j],
                    recv_sem=recv.at[3 * a + j], device_id=peers[j], device_id_type=MESH)
                copy.wait_send()
                copy.wait_recv()

    return pl.pallas_call(
        body, name=name, in_specs=[HBM] * n + [SEM, SEM, ANY], out_specs=[HBM] * n,
        out_shape=[pltpu.HBM(b.shape, b.dtype) for b in bufs],
        input_output_aliases={i: i for i in range(n)},
        compiler_params=pltpu.CompilerParams(has_side_effects=DATAFLOW))(*bufs, send_sems, recv_sems, after)


def sibling_exchange(arrays, *, name):
    n = len(arrays)

    def body(*refs):
        ins, outs = refs[:n], refs[n:2 * n]
        send, recv = refs[2 * n:]
        x, y, c = _place()

        def copy(a):
            return pltpu.make_async_remote_copy(
                src_ref=ins[a], dst_ref=outs[a], send_sem=send.at[a], recv_sem=recv.at[a],
                device_id=(x, y, 1 - c), device_id_type=MESH)

        for a in range(n):
            copy(a).start()
        for a in range(n):
            copy(a).wait_recv()
        for a in range(n):
            copy(a).wait_send()

    return pl.pallas_call(
        body, name=name, in_specs=[ANY] * n, out_specs=[ANY] * n,
        out_shape=[jax.ShapeDtypeStruct(a.shape, a.dtype) for a in arrays],
        scratch_shapes=[pltpu.SemaphoreType.DMA((n,)), pltpu.SemaphoreType.DMA((n,))])(*arrays)


ALL_MASKS = [(mx, my, mc) for mx in (0, 1) for my in (0, 1) for mc in (0, 1)][1:]


def _scatter_copies(srcs, lands, ev, send, recv, esend, erecv):
    x, y, c = _place()
    me = 4 * x + 2 * y + c
    k, peers, peer_k = _chip_peers()
    out = []
    for a in range(len(srcs)):
        for j in range(3):
            out.append(pltpu.make_async_remote_copy(
                src_ref=srcs[a].at[peer_k[j]], dst_ref=lands[a].at[j], send_sem=send.at[3 * a + j],
                recv_sem=recv.at[3 * a + j], device_id=peers[j], device_id_type=MESH))
    start_ev, wait_ev = [], []
    if ev is not None:
        for j, (mx, my, mc) in enumerate(ALL_MASKS):
            peer = (x ^ mx, y ^ my, c ^ mc)
            start_ev.append(pltpu.make_async_remote_copy(
                src_ref=ev.at[me], dst_ref=ev.at[me], send_sem=esend.at[j], recv_sem=erecv.at[j],
                device_id=peer, device_id_type=MESH))
            wait_ev.append(pltpu.make_async_remote_copy(
                src_ref=ev.at[me], dst_ref=ev.at[me ^ (4 * mx + 2 * my + mc)], send_sem=esend.at[j],
                recv_sem=erecv.at[j], device_id=peer, device_id_type=MESH))
    return out, start_ev, wait_ev


def chip_scatter_start(arrays, everyone, after, *, name):
    n = len(arrays)
    ne = 0 if everyone is None else 1
    lands = [pltpu.with_memory_space_constraint(lax.empty((3,) + a.shape[1:], a.dtype), pltpu.HBM) for a in arrays]

    def body(*refs):
        srcs, lands_ = refs[:n], refs[n:2 * n]
        ev = refs[2 * n] if ne else None
        sems = refs[2 * n + ne + 1 + 2 * n + ne:-1]
        send, recv = sems[0], sems[1]
        esend, erecv = (sems[2], sems[3]) if ne else (None, None)
        copies, start_ev, _ = _scatter_copies(srcs, lands_, ev, send, recv, esend, erecv)
        for cp in start_ev + copies:
            cp.start()
        refs[-1][...] = jnp.zeros_like(refs[-1])

    sem_shapes = [pltpu.SemaphoreType.DMA((3 * n,))] * 2 + [pltpu.SemaphoreType.DMA((7,))] * (2 * ne)
    bufs = list(arrays) + lands + ([everyone] if ne else [])
    outs = pl.pallas_call(
        body, name=name, in_specs=[HBM] * len(bufs) + [ANY],
        out_specs=[HBM] * len(bufs) + [SEM] * len(sem_shapes) + [pl.BlockSpec(memory_space=pltpu.VMEM)],
        out_shape=[pltpu.HBM(b.shape, b.dtype) for b in bufs] + sem_shapes + [jax.ShapeDtypeStruct((8, LANES), f32)],
        input_output_aliases={i: i for i in range(len(bufs))},
        compiler_params=pltpu.CompilerParams(has_side_effects=DATAFLOW))(
            *[pltpu.with_memory_space_constraint(b, pltpu.HBM) for b in bufs], after)
    return (n, ne, outs[:-1]), outs[-1]


def chip_scatter_wait(state, after, *, name):
    n, ne, held = state
    nb = 2 * n + ne
    bufs, sems = held[:nb], held[nb:]

    def body(*refs):
        srcs, lands_ = refs[:n], refs[n:2 * n]
        ev = refs[2 * n] if ne else None
        sems_ = refs[nb:nb + len(sems)]
        esend, erecv = (sems_[2], sems_[3]) if ne else (None, None)
        copies, _, wait_ev = _scatter_copies(srcs, lands_, ev, sems_[0], sems_[1], esend, erecv)
        for cp in wait_ev + copies:
            cp.wait_send()
            cp.wait_recv()

    outs = pl.pallas_call(
        body, name=name, in_specs=[HBM] * nb + [SEM] * len(sems) + [ANY], out_specs=[HBM] * nb,
        out_shape=[pltpu.HBM(b.shape, b.dtype) for b in bufs],
        input_output_aliases={i: i for i in range(nb)},
        compiler_params=pltpu.CompilerParams(has_side_effects=DATAFLOW))(*bufs, *sems, after)
    return outs[n:2 * n], (outs[2 * n] if ne else None)


def sibling_merge(bufs, *, name):
    n = len(bufs)

    def body(*refs):
        bufs_ = refs[:n]
        send, recv = refs[2 * n:]
        x, y, c = _place()

        def copy(u, h):
            return pltpu.make_async_remote_copy(
                src_ref=bufs_[u].at[h], dst_ref=bufs_[u].at[h], send_sem=send.at[u], recv_sem=recv.at[u],
                device_id=(x, y, 1 - c), device_id_type=MESH)

        for u in range(n):
            copy(u, c).start()
        for u in range(n):
            copy(u, 1 - c).wait_recv()
        for u in range(n):
            copy(u, c).wait_send()

    return pl.pallas_call(
        body, name=name, in_specs=[ANY] * n, out_specs=[ANY] * n,
        out_shape=[jax.ShapeDtypeStruct(b.shape, b.dtype) for b in bufs],
        input_output_aliases={i: i for i in range(n)},
        scratch_shapes=[pltpu.SemaphoreType.DMA((n,)), pltpu.SemaphoreType.DMA((n,))])(*bufs)


def sum_leading(a, *, name):
    n, r, c = a.shape

    def body(a_ref, o_ref):
        acc = a_ref[0]
        for i in range(1, n):
            acc = acc + a_ref[i]
        o_ref[...] = acc

    rb = r // 2 if r % 16 == 0 else r
    return pl.pallas_call(
        body, name=name, grid=(r // rb,), in_specs=[pl.BlockSpec((n, rb, c), lambda i: (0, i, 0))],
        out_specs=pl.BlockSpec((rb, c), lambda i: (i, 0)), out_shape=jax.ShapeDtypeStruct((r, c), f32),
        compiler_params=_params("parallel"))(a)


def _half_rows(shape):
    return shape[1] // 2 // 2


def rs_cast_other_half(grads, c_arr, *, name):
    n = len(grads)

    def body(c_ref, *refs):
        for i_ref, o_ref in zip(refs[:n], refs[n:]):
            o_ref[...] = i_ref[...].astype(bf16)

    in_specs = [pl.BlockSpec((1, _half_rows(g.shape), g.shape[2]), lambda s, r, c_ref: (s, (1 - c_ref[0]) * 2 + r, 0))
                for g in grads]
    out_specs = [pl.BlockSpec((1, _half_rows(g.shape), g.shape[2]), lambda s, r, c_ref: (s, r, 0)) for g in grads]
    return pl.pallas_call(
        body, name=name,
        grid_spec=pltpu.PrefetchScalarGridSpec(num_scalar_prefetch=1, grid=(N_SHARDS, 2),
                                               in_specs=in_specs, out_specs=out_specs),
        out_shape=[jax.ShapeDtypeStruct((N_SHARDS, g.shape[1] // 2, g.shape[2]), bf16) for g in grads],
        compiler_params=_params("parallel", "parallel"))(c_arr, *grads)


def rs_add_sibling(grads, recvd, ck_arr, *, name):
    n = len(grads)

    def body(ck_ref, *refs):
        s = pl.program_id(1)
        for u in range(n):
            g_ref, r_ref = refs[u], refs[n + u]
            qb_ref, own_ref = refs[2 * n + u], refs[3 * n + u]
            q = g_ref[0] + r_ref[0].astype(f32)
            qb_ref[0] = q.astype(bf16)

            @pl.when(s == ck_ref[1])
            def _(own_ref=own_ref, q=q):
                own_ref[...] = q

    in_specs = [pl.BlockSpec((1, _half_rows(g.shape), g.shape[2]), lambda r, s, ck: (s, ck[0] * 2 + r, 0)) for g in grads]
    in_specs += [pl.BlockSpec((1, _half_rows(g.shape), g.shape[2]), lambda r, s, ck: (s, r, 0)) for g in grads]
    out_specs = [pl.BlockSpec((1, _half_rows(g.shape), g.shape[2]), lambda r, s, ck: (s, r, 0)) for g in grads]
    out_specs += [pl.BlockSpec((_half_rows(g.shape), g.shape[2]), lambda r, s, ck: (r, 0)) for g in grads]
    outs = pl.pallas_call(
        body, name=name,
        grid_spec=pltpu.PrefetchScalarGridSpec(num_scalar_prefetch=1, grid=(2, N_SHARDS),
                                               in_specs=in_specs, out_specs=out_specs),
        out_shape=[jax.ShapeDtypeStruct((N_SHARDS, g.shape[1] // 2, g.shape[2]), bf16) for g in grads]
        + [jax.ShapeDtypeStruct((g.shape[1] // 2, g.shape[2]), f32) for g in grads],
        compiler_params=_params("parallel", "arbitrary"))(ck_arr, *grads, *recvd)
    return outs[:n], outs[n:]


def rs_sum_chips(owns, recvd, ck_arr, *, name):
    n = len(owns)

    def body(ck_ref, *refs):
        for u in range(n):
            own_ref, r_ref, o_ref = refs[u], refs[n + u], refs[2 * n + u]
            o_ref[0] = ((own_ref[...] + r_ref[0].astype(f32)) + r_ref[1].astype(f32)) + r_ref[2].astype(f32)

    in_specs = [pl.BlockSpec((o.shape[0] // 2, o.shape[1]), lambda r, ck: (r, 0)) for o in owns]
    in_specs += [pl.BlockSpec((3, o.shape[0] // 2, o.shape[1]), lambda r, ck: (0, r, 0)) for o in owns]
    out_specs = [pl.BlockSpec((1, o.shape[0] // 2, o.shape[1]), lambda r, ck: (ck[0], r, 0)) for o in owns]
    return pl.pallas_call(
        body, name=name,
        grid_spec=pltpu.PrefetchScalarGridSpec(num_scalar_prefetch=1, grid=(2,), in_specs=in_specs, out_specs=out_specs),
        out_shape=[jax.ShapeDtypeStruct((2,) + o.shape, f32) for o in owns],
        compiler_params=_params("parallel"))(ck_arr, *owns, *recvd)


SMALL = ("a_norm", "a_v_norm", "a_w_s", "a_b_s", "f_norm", "f_conv_w", "f_conv_b", "kv_norm", "k_norm",
         "b_norm", "b_q_norm", "b_sinks")
BIG = ("a_w_in", "a_w_out", "f_w_in", "f_w_out", "w_kv", "b_w_q", "b_w_o")
PACK_COLS = 1024
PACK_ROWS = 8 * N_STEPS


def _pack(parts, rows=PACK_ROWS):
    flat = jnp.concatenate([p.reshape(-1).astype(f32) for p in parts])
    pad = (-flat.shape[0]) % (rows * PACK_COLS)
    return jnp.pad(flat, (0, pad)).reshape(-1, PACK_COLS)


def _unpack(packed, shapes):
    flat = packed.reshape(-1)
    out, off = [], 0
    for s in shapes:
        size = math.prod(s)
        out.append(flat[off:off + size].reshape(s))
        off += size
    return out


def _ffn_fwd(x, g, h, r, w_in4, conv_w, conv_b, f, tag):
    wg, wu = conv_w[:, :f], conv_w[:, f:]
    bg, bu = conv_b[None, :f], conv_b[None, f:]
    pg, pu, gate, up, a = ffn_in_fused(h, w_in4, wg, wu, bg, bu, name=f"ffn{tag}_in")
    return a, (x, g, h, r, pg, pu, gate, up, a, wg, wu)


def _ffn_bwd(dy, saved, w_in4, w_out, tag):
    x, g, h, r, pg, pu, gate, up, a, wg, wu = saved
    f = w_out.shape[0]
    d_w_out = mm_tn(a, [dy], name=f"ffn{tag}_dwout", n_s=w_out.shape[1], tki=f // 2)
    dpg, dpu, sg, su = ffn_gate_bwd(dy, w_out, pg, pu, gate, up, wg, wu, name=f"ffn{tag}_dgate")
    d_w_in = mm_tn(h, [dpg, dpu], name=f"ffn{tag}_dwin", n_s=w_in4.shape[2])
    dx, dg = mm_nt_rms_bwd([dpg, dpu], w_in4, x, r, g, dy, name=f"ffn{tag}_dh")
    d_conv_w = jnp.concatenate([sg[0:3], su[0:3]], axis=1)
    d_conv_b = jnp.concatenate([sg[3], su[3]], axis=0)
    return dx, dg, d_w_in, d_conv_w, d_conv_b, d_w_out


def _rs_front(units, c_arr, tag):
    other_bf = rs_cast_other_half(units, c_arr, name=f"rs_cast{tag}")
    from_sib = sibling_exchange(list(other_bf), name=f"rs_sibling{tag}")
    return rs_add_sibling(units, from_sib, c_arr, name=f"rs_add{tag}")


def _rs_back(own, from_chips, c_arr, tag):
    halves = rs_sum_chips(list(own), list(from_chips), c_arr, name=f"rs_sum{tag}")
    return [m.reshape(-1, m.shape[2]) for m in sibling_merge(list(halves), name=f"rs_merge{tag}")]


def kernel(x, a_norm, a_w_in, a_v_norm, a_w_s, a_b_s, a_w_out, f_norm, f_w_in, f_conv_w, f_conv_b, f_w_out, kv_norm, w_kv, k_norm, b_norm, b_w_q, b_q_norm, b_sinks, b_w_o, loss_target, m_a_norm, m_a_w_in, m_a_v_norm, m_a_w_s, m_a_b_s, m_a_w_out, m_f_norm, m_f_w_in, m_f_conv_w, m_f_conv_b, m_f_w_out, m_kv_norm, m_w_kv, m_k_norm, m_b_norm, m_b_w_q, m_b_q_norm, m_b_sinks, m_b_w_o, v_a_norm, v_a_w_in, v_a_v_norm, v_a_w_s, v_a_b_s, v_a_w_out, v_f_norm, v_f_w_in, v_f_conv_w, v_f_conv_b, v_f_w_out, v_kv_norm, v_w_kv, v_k_norm, v_b_norm, v_b_w_q, v_b_q_norm, v_b_sinks, v_b_w_o):
    args = dict(locals())
    weights = {n: args[n] for n in SMALL + BIG}
    moms = {n: args["m_" + n] for n in SMALL + BIG}
    vars_ = {n: args["v_" + n] for n in SMALL + BIG}
    t, d = x.shape[1], x.shape[2]
    xi, yi, ci = _place()
    chip = 2 * xi + yi

    big_local = [a_w_in[0], a_w_out[0], f_w_in, f_w_out, w_kv, b_w_q[0], b_w_o[0]]
    c_arr = jnp.stack([ci, chip]).astype(jnp.int32)
    k_arr = jnp.stack([chip]).astype(jnp.int32)
    b_ain, b_aout, b_fin0, b_fin1, b_fout0, b_fout1, b_kv, b_q, b_o = cast_into_slot(big_local, k_arr, name="cast_weights")
    small_cols = _pack([a_norm, a_v_norm, f_conv_w], rows=8)
    b_small = lax.dynamic_update_slice(jnp.zeros((N_SHARDS,) + small_cols.shape, f32), small_cols[None], (chip, 0, 0))
    g_small, g_a_w_in, g_a_w_out = gather_shards([b_small, b_ain, b_aout], name="gather_first", split=[False, True, True])
    later, send_sems, recv_sems, token = gather_start([b_fin0, b_fout0, b_kv, b_q, b_o, b_fin1, b_fout1],
                                                      [[0], [1, 2, 3, 4], [5, 6]], g_small, name="gather_start")
    ns_cols = a_norm.shape[1]
    nf_cols = f_conv_w.shape[2]
    parts = [_unpack(g_small[k], [a_norm.shape, a_v_norm.shape, f_conv_w.shape]) for k in range(N_SHARDS)]
    a_norm_f = jnp.concatenate([p[0] for p in parts], axis=1) + token[0, 0]
    a_v_norm_f = jnp.concatenate([p[1] for p in parts], axis=1)
    conv_w_f = jnp.concatenate([p[2] for p in parts], axis=2)
    w_a_in = g_a_w_in
    w_a_out = g_a_w_out.reshape(1, -1, d)

    x0 = x[0]
    tril = jnp.tril(jnp.ones((CHUNK, CHUNK), dtype=bool))
    wc = jnp.where(tril[None], a_w_s[0], 0.0).astype(bf16)
    bt = a_b_s[0].T
    kg2 = jnp.tile(k_norm, 2)[None]
    qg2 = jnp.tile(b_q_norm[0], 2)[None]

    (h_a,), r_a = rms_fwd(x0, [a_norm_f], name="a_norm")
    zu = mm_nn(h_a, w_a_in, name="a_in_u", s0=0, ns=2)
    zv = mm_nn(h_a, w_a_in, name="a_in_v", s0=2, ns=2)
    y_a = sgu_gate_fwd(zu, zv, a_v_norm_f, wc, bt, name="a_gate")
    f = f_w_out.shape[1] * N_SHARDS
    x1, (h_f0,), r_f0 = mm_residual(y_a, w_a_out[0], x0, name="a_out", gains=[f_norm[0:1]])
    (g_fin0,) = gather_wait(later[0:1], send_sems[0], recv_sems[0], x1, name="gather_wait_0")
    w_f_in = [g_fin0, None]
    a0, ffn0 = _ffn_fwd(x1, f_norm[0:1], h_f0, r_f0, w_f_in[0], conv_w_f[0], f_conv_b[0], f, "0")
    g_fout0, g_w_kv, g_b_w_q, g_b_w_o = gather_wait(later[1:5], send_sems[1], recv_sems[1], a0, name="gather_wait_1")
    w_f_out = [g_fout0.reshape(-1, d), None]
    w_kv_f = g_w_kv.reshape(1, d, -1)
    w_q_f = g_b_w_q.reshape(1, d, -1)
    w_o_f = g_b_w_o.reshape(1, -1, d)
    x2, (h_k, h_q), r_b = mm_residual(a0, w_f_out[0], x1, name="ffn0_out", gains=[kv_norm[None], b_norm])
    kv = mm_nn(h_k, w_kv_f, name="kv_proj")
    k2, v2 = kv_post_fwd(kv, kg2, name="kv_post")
    qp = mm_nn(h_q, w_q_f, name="q_proj")
    qn = q_norm_fwd(qp, qg2, name="q_norm", scale=HEAD_DIM ** -0.5)
    o = attn_fwd(qn, k2, v2, b_sinks[0], name="attn")
    x3, (h_f1,), r_f1 = mm_residual(o, w_o_f[0], x2, name="o_proj", gains=[f_norm[1:2]])
    g_fin1, g_fout1 = gather_wait(later[5:7], send_sems[2], recv_sems[2], x3, name="gather_wait_2")
    w_f_in[1] = g_fin1
    w_f_out[1] = g_fout1.reshape(-1, d)
    a1, ffn1 = _ffn_fwd(x3, f_norm[1:2], h_f1, r_f1, w_f_in[1], conv_w_f[1], f_conv_b[1], f, "1")
    dx4, sq = mm_residual(a1, w_f_out[1], x3, name="ffn1_out", target=loss_target[0])
    loss_part = (0.5 * jnp.sum(sq) / d).reshape(1)

    dx3, d_fn1, d_fwin1, d_cw1, d_cb1, d_fwout1 = _ffn_bwd(dx4, ffn1, w_f_in[1], w_f_out[1], "1")
    do = mm_nt([dx3], w_o_f, name="o_proj_dx")
    d_w_o = mm_tn(o, [dx3], name="o_proj_dw", n_s=d)
    dqn, dk2, dv2, dsink = attn_bwd(qn, k2, v2, do, b_sinks[0], name="attn_bwd")
    dqp, dqg = q_norm_bwd(dqn, qp, qg2, name="q_norm_bwd", scale=HEAD_DIM ** -0.5)
    dkv, dkg = kv_post_bwd(dk2, dv2, kv, kg2, name="kv_post_bwd")
    d_w_q = mm_tn(h_q, [dqp], name="q_proj_dw", n_s=w_q_f.shape[2])
    d_w_kv = mm_tn(h_k, [dkv], name="kv_proj_dw", n_s=w_kv_f.shape[2])
    dh_k = mm_nt([dkv], w_kv_f, name="kv_proj_dx")
    dx2, d_bn, d_kvn = mm_nt_rms_bwd([dqp], w_q_f, x2, r_b, b_norm, dx3, name="q_proj_dx", extra=(dh_k, kv_norm[None]))
    sh = N_SHARDS
    units1 = [d_fwin1, d_fwout1.reshape(sh, -1, d), d_w_kv.reshape(sh, -1, d_w_kv.shape[2]),
              d_w_q.reshape(sh, -1, d_w_q.shape[2]), d_w_o.reshape(sh, -1, d)]
    chip_bf1, own1 = _rs_front(units1, c_arr, "1")
    scatter1, token1 = chip_scatter_start(list(chip_bf1), None, dx2, name="rs_chips_start1")
    ffn0 = ffn0[:9] + (ffn0[9] + token1[0, 0],) + ffn0[10:]
    dx1, d_fn0, d_fwin0, d_cw0, d_cb0, d_fwout0 = _ffn_bwd(dx2, ffn0, w_f_in[0], w_f_out[0], "0")
    chip_bf2, own2 = _rs_front([d_fwin0, d_fwout0.reshape(sh, -1, d)], c_arr, "2")
    scatter2, token2 = chip_scatter_start(list(chip_bf2), None, dx1, name="rs_chips_start2")
    a_v_norm_f = a_v_norm_f + token2[0, 0]
    dy_a = mm_nt([dx1], w_a_out, name="a_out_dx")
    d_w_aout = mm_tn(y_a, [dx1], name="a_out_dw", n_s=d)
    dzu, dzv, d_avn, d_ws, d_bt = sgu_gate_bwd(zu, zv, dy_a, a_v_norm_f, wc, bt, name="a_gate_bwd")
    d_w_ain = mm_tn(h_a, [dzu, dzv], name="a_in_dw", n_s=w_a_in.shape[2])
    dx0, d_an = mm_nt_rms_bwd([dzu, dzv], w_a_in, x0, r_a, a_norm_f, dx1, name="a_in_dx")
    grad_x = dx0[None]

    chip_bf3, own3 = _rs_front([d_w_ain, d_w_aout.reshape(sh, -1, d)], c_arr, "3")
    d_fn = jnp.concatenate([d_fn0, d_fn1], axis=0)
    d_cw = jnp.stack([d_cw0, d_cw1])
    d_cb = jnp.stack([d_cb0, d_cb1])
    d_kg = (dkg[0, :HEAD_DIM] + dkg[0, HEAD_DIM:])
    d_qg = (dqg[0, :HEAD_DIM] + dqg[0, HEAD_DIM:])[None]
    small_full = [d_an, d_avn, d_ws[None], d_bt.T[None], d_fn, d_cw, d_cb, d_kvn[0], d_kg, d_bn, d_qg,
                  dsink[:, :N_Q_HEADS], loss_part]
    packed = _pack(small_full)
    me = 4 * xi + 2 * yi + ci
    everyone = lax.dynamic_update_slice(lax.empty((N_DEV,) + packed.shape, f32), packed[None], (me, 0, 0))
    scatter3, _ = chip_scatter_start(list(chip_bf3), everyone, chip_bf3[0], name="rs_chips_start3")
    from_chips1, _ = chip_scatter_wait(scatter1, scatter3[2][0], name="rs_chips_wait1")
    fin1, fout1, gkv, gq, go = _rs_back(own1, from_chips1, c_arr, "1")
    from_chips2, _ = chip_scatter_wait(scatter2, fin1, name="rs_chips_wait2")
    fin0, fout0 = _rs_back(own2, from_chips2, c_arr, "2")
    late = ("f_w_in", "f_w_out", "w_kv", "b_w_q", "b_w_o")
    res_late = adamw([weights[n] for n in late], [[fin0, fin1], [fout0, fout1], [gkv], [gq], [go]],
                     [moms[n] for n in late], [vars_[n] for n in late], name="adamw_late")
    from_chips3, from_all = chip_scatter_wait(scatter3, res_late[1][2], name="rs_chips_wait3")
    ain, aout = _rs_back(own3, from_chips3, c_arr, "3")
    first = ("a_w_in", "a_w_out")
    res_first = adamw([weights[n] for n in first], [[ain], [aout]], [moms[n] for n in first],
                      [vars_[n] for n in first], name="adamw_first")
    big = {n: tuple(r[i] for r in res_late) for i, n in enumerate(late)}
    big.update({n: tuple(r[i] for r in res_first) for i, n in enumerate(first)})

    full_shapes = [g.shape for g in small_full]
    small_g = _unpack(sum_leading(from_all, name="small_sum"), full_shapes)
    loss = small_g.pop()[0]
    small_g[0] = lax.dynamic_slice_in_dim(small_g[0], chip * ns_cols, ns_cols, axis=1)
    small_g[1] = lax.dynamic_slice_in_dim(small_g[1], chip * ns_cols, ns_cols, axis=1)
    small_g[5] = lax.dynamic_slice_in_dim(small_g[5], chip * nf_cols, nf_cols, axis=2)
    small_shapes = [weights[n].shape for n in SMALL]
    small_g = [g.reshape(s) for g, s in zip(small_g, small_shapes)]
    pw, pg_, pm, pv = (_pack(v) for v in ([weights[n] for n in SMALL], small_g, [moms[n] for n in SMALL],
                                          [vars_[n] for n in SMALL]))
    _, (sd,), (sm,), (sv,) = adamw([pw], [[pg_]], [pm], [pv], name="adamw_small")
    small_d, small_m, small_v = (_unpack(v, small_shapes) for v in (sd, sm, sv))

    out = {}
    for i, n in enumerate(SMALL):
        out[n] = (small_g[i], small_d[i], small_m[i], small_v[i])
    out.update(big)
    order = ["a_norm", "a_w_in", "a_v_norm", "a_w_s", "a_b_s", "a_w_out", "f_norm", "f_w_in", "f_conv_w", "f_conv_b",
             "f_w_out", "kv_norm", "w_kv", "k_norm", "b_norm", "b_w_q", "b_q_norm", "b_sinks", "b_w_o"]
    return (loss, grad_x, *[out[n][0] for n in order], *[out[n][1] for n in order],
            *[out[n][2] for n in order], *[out[n][3] for n in order])
```

```python
import functools
import math

import jax
import jax.numpy as jnp
from jax import lax
from jax.experimental import pallas as pl
from jax.experimental.pallas import tpu as pltpu

f32 = jnp.float32
bf16 = jnp.bfloat16
MESH = pl.DeviceIdType.MESH
ANY = pl.BlockSpec(memory_space=pl.ANY)

EPS = 1e-6
LANES = 128
CHUNK = 128
HEAD_DIM = 64
N_Q_HEADS = 16
N_KV_HEADS = 4
Q_PER_KV = N_Q_HEADS // N_KV_HEADS
N_SHARDS = 4
N_DEV = 8

ADAM_LR = 0.001
ADAM_B1 = 0.9
ADAM_B2 = 0.999
ADAM_EPS = 1e-08
ADAM_WD = 0.01
ADAM_STEP = 10
ADAM_C1 = 1.0 - ADAM_B1 ** ADAM_STEP
ADAM_C2 = 1.0 - ADAM_B2 ** ADAM_STEP

_INV_SQRT2 = 1.0 / math.sqrt(2.0)
_INV_SQRT2PI = 1.0 / math.sqrt(2.0 * math.pi)


def _params(*sem):
    return pltpu.CompilerParams(dimension_semantics=sem)


def _gelu(z):
    return 0.5 * z * (1.0 + lax.erf(z * _INV_SQRT2))


def _gelu_and_grad(z):
    cdf = 0.5 * (1.0 + lax.erf(z * _INV_SQRT2))
    return z * cdf, cdf + z * jnp.exp(-0.5 * z * z) * _INV_SQRT2PI


def _dot(a, b):
    return jnp.dot(a, b, preferred_element_type=f32)


def _dot_nt(a, b):
    return lax.dot_general(a, b, (((1,), (1,)), ((), ())), preferred_element_type=f32)


def _dot_tn(a, b):
    return lax.dot_general(a, b, (((0,), (0,)), ((), ())), preferred_element_type=f32)


def _dot_split(a, b):
    hi = a.astype(bf16)
    lo = (a - hi.astype(f32)).astype(bf16)
    return _dot(hi, b) + _dot(lo, b)


VMEM_TILE_BUDGET = 40 * 1024 * 1024
MAX_ROW_TILE = 2048


def _row_tile(m, fixed_bytes, row_bytes):
    tm = min(m, MAX_ROW_TILE)
    while tm > 256 and 2 * (fixed_bytes + tm * row_bytes) > VMEM_TILE_BUDGET:
        tm //= 2
    return tm


def _isz(a):
    return jnp.dtype(a.dtype).itemsize


def mm_nn(a, w3, *, name, s0=0, ns=None, add=None, out_dtype=f32):
    m, k = a.shape
    s_all, _, n_s = w3.shape
    ns = s_all if ns is None else ns
    tm = _row_tile(m, k * n_s * 2, k * _isz(a) + n_s * jnp.dtype(out_dtype).itemsize + (0 if add is None else n_s * 4))

    def body(*refs):
        if add is None:
            a_ref, w_ref, o_ref = refs
            acc = _dot(a_ref[...].astype(bf16), w_ref[0])
        else:
            a_ref, w_ref, add_ref, o_ref = refs
            acc = _dot(a_ref[...].astype(bf16), w_ref[0]) + add_ref[...]
        o_ref[...] = acc.astype(out_dtype)

    in_specs = [pl.BlockSpec((tm, k), lambda j, i: (i, 0)),
                pl.BlockSpec((1, k, n_s), lambda j, i: (s0 + j, 0, 0))]
    args = [a, w3]
    if add is not None:
        in_specs.append(pl.BlockSpec((tm, n_s), lambda j, i: (i, j)))
        args.append(add)
    return pl.pallas_call(
        body, name=name, grid=(ns, m // tm), in_specs=in_specs,
        out_specs=pl.BlockSpec((tm, n_s), lambda j, i: (i, j)),
        out_shape=jax.ShapeDtypeStruct((m, ns * n_s), out_dtype),
        compiler_params=_params("parallel", "parallel"))(*args)


def mm_nt(a_list, w3, *, name, tko=None, add=None, out_dtype=f32):
    s_all, k_out, n_s = w3.shape
    m = a_list[0].shape[0]
    na = len(a_list)
    spa = s_all // na
    tko = k_out if tko is None else tko
    tm = _row_tile(m, tko * n_s * 2, na * n_s * _isz(a_list[0]) + tko * 4 * (1 if add is None else 2))

    def body(*refs):
        a_refs = refs[:na]
        w_ref = refs[na]
        o_ref = refs[-1]
        s = pl.program_id(2)

        @pl.when(s == 0)
        def _():
            if add is None:
                o_ref[...] = jnp.zeros_like(o_ref)
            else:
                o_ref[...] = refs[na + 1][...]

        for idx in range(na):
            @pl.when(s // spa == idx)
            def _(idx=idx):
                o_ref[...] += _dot_nt(a_refs[idx][...].astype(bf16), w_ref[0])

    def a_map(idx):
        return lambda ko, i, s: (i, jnp.clip(s - idx * spa, 0, spa - 1))

    in_specs = [pl.BlockSpec((tm, n_s), a_map(idx)) for idx in range(na)]
    in_specs.append(pl.BlockSpec((1, tko, n_s), lambda ko, i, s: (s, ko, 0)))
    args = list(a_list) + [w3]
    if add is not None:
        in_specs.append(pl.BlockSpec((tm, tko), lambda ko, i, s: (i, ko)))
        args.append(add)
    return pl.pallas_call(
        body, name=name, grid=(k_out // tko, m // tm, s_all), in_specs=in_specs,
        out_specs=pl.BlockSpec((tm, tko), lambda ko, i, s: (i, ko)),
        out_shape=jax.ShapeDtypeStruct((m, k_out), out_dtype),
        compiler_params=_params("parallel", "parallel", "arbitrary"))(*args)


def mm_tn(a, b_list, *, name, n_s, tki=None):
    m, k_in = a.shape
    na = len(b_list)
    s_all = sum(b.shape[1] for b in b_list) // n_s
    spa = s_all // na
    tki = k_in if tki is None else tki
    tm = _row_tile(m, tki * n_s * 4, tki * _isz(a) + na * n_s * _isz(b_list[0]))

    def body(*refs):
        a_ref = refs[0]
        b_refs = refs[1:1 + na]
        o_ref = refs[-1]
        s = pl.program_id(0)
        r = pl.program_id(2)

        @pl.when(r == 0)
        def _():
            o_ref[...] = jnp.zeros_like(o_ref)

        for idx in range(na):
            @pl.when(s // spa == idx)
            def _(idx=idx):
                o_ref[0] += _dot_tn(a_ref[...].astype(bf16), b_refs[idx][...].astype(bf16))

    def b_map(idx):
        def index(s, ki, r):
            active = (s // spa) == idx
            return (jnp.where(active, r, 0), jnp.clip(s - idx * spa, 0, spa - 1))
        return index

    in_specs = [pl.BlockSpec((tm, tki), lambda s, ki, r: (r, ki))]
    in_specs += [pl.BlockSpec((tm, n_s), b_map(idx)) for idx in range(na)]
    return pl.pallas_call(
        body, name=name, grid=(s_all, k_in // tki, m // tm), in_specs=in_specs,
        out_specs=pl.BlockSpec((1, tki, n_s), lambda s, ki, r: (s, ki, 0)),
        out_shape=jax.ShapeDtypeStruct((s_all, k_in, n_s), f32),
        compiler_params=_params("parallel", "parallel", "arbitrary"))(a, *b_list)


def mm_nt_rms_bwd(a_list, w3, x, r, g, dx_in, *, name, extra=None):
    s_all, d, n_s = w3.shape
    m = a_list[0].shape[0]
    na = len(a_list)
    spa = s_all // na
    ne = 0 if extra is None else 1
    tm = _row_tile(m, d * n_s * 2, na * n_s * _isz(a_list[0]) + d * 4 * (4 + ne))

    def body(*refs):
        a_refs, w_ref = refs[:na], refs[na]
        x_ref, r_ref, g_ref, dxin_ref = refs[na + 1:na + 5]
        dh2_ref, g2_ref = (refs[na + 5], refs[na + 6]) if ne else (None, None)
        outs = refs[na + 5 + 2 * ne:]
        dx_ref, dg_ref = outs[0], outs[1]
        dg2_ref = outs[2] if ne else None
        acc_ref = outs[-1]
        i, s = pl.program_id(0), pl.program_id(1)

        @pl.when(s == 0)
        def _():
            acc_ref[...] = jnp.zeros_like(acc_ref)

        for idx in range(na):
            @pl.when(s // spa == idx)
            def _(idx=idx):
                acc_ref[...] += _dot_nt(a_refs[idx][...].astype(bf16), w_ref[0])

        @pl.when(s == s_all - 1)
        def _():
            rv = r_ref[...]
            xh = x_ref[...] * rv
            total = dxin_ref[...]
            pairs = [(acc_ref[...], g_ref, dg_ref)] + ([(dh2_ref[...], g2_ref, dg2_ref)] if ne else [])
            for dh, gain_ref, dgain_ref in pairs:
                part = jnp.sum(dh * xh, axis=0, keepdims=True)

                @pl.when(i == 0)
                def _(dgain_ref=dgain_ref, part=part):
                    dgain_ref[...] = part

                @pl.when(i > 0)
                def _(dgain_ref=dgain_ref, part=part):
                    dgain_ref[...] += part

                tg = dh * gain_ref[...]
                total = total + rv * (tg - xh * jnp.mean(tg * xh, axis=1, keepdims=True))
            dx_ref[...] = total

    def a_map(idx):
        return lambda i, s: (i, jnp.clip(s - idx * spa, 0, spa - 1))

    row = pl.BlockSpec((tm, d), lambda i, s: (i, 0))
    vec = pl.BlockSpec((1, d), lambda i, s: (0, 0))
    in_specs = [pl.BlockSpec((tm, n_s), a_map(idx)) for idx in range(na)]
    in_specs += [pl.BlockSpec((1, d, n_s), lambda i, s: (s, 0, 0)), row, pl.BlockSpec((tm, 1), lambda i, s: (i, 0)), vec, row]
    args = list(a_list) + [w3, x, r, g, dx_in]
    if ne:
        in_specs += [row, vec]
        args += list(extra)
    outs = pl.pallas_call(
        body, name=name, grid=(m // tm, s_all), in_specs=in_specs, out_specs=[row] + [vec] * (1 + ne),
        out_shape=[jax.ShapeDtypeStruct((m, d), f32)] + [jax.ShapeDtypeStruct((1, d), f32)] * (1 + ne),
        scratch_shapes=[pltpu.VMEM((tm, d), f32)],
        compiler_params=_params("arbitrary", "arbitrary"))(*args)
    return outs


def mm_residual(a, w, x, *, name, gains=(), target=None):
    m, k = a.shape
    d = w.shape[1]
    ng = len(gains)
    tm = _row_tile(m, k * d * 2, k * _isz(a) + d * 4 * 3 + ng * d * 2)

    def body(*refs):
        a_ref, w_ref, x_ref = refs[:3]
        y = _dot(a_ref[...].astype(bf16), w_ref[...]) + x_ref[...]
        if target is None:
            g_refs = refs[3:3 + ng]
            y_ref = refs[3 + ng]
            h_refs = refs[4 + ng:4 + 2 * ng]
            r_ref = refs[-1]
            y_ref[...] = y
            r = lax.rsqrt(jnp.mean(y * y, axis=1, keepdims=True) + EPS)
            yh = y * r
            for g_ref, h_ref in zip(g_refs, h_refs):
                h_ref[...] = (yh * g_ref[...]).astype(bf16)
            r_ref[...] = r
        else:
            t_ref, dy_ref, s_ref = refs[3:]
            i = pl.program_id(0)
            e = y - t_ref[...]
            dy_ref[...] = e * (1.0 / d)
            part = jnp.sum(e * e, axis=0, keepdims=True)

            @pl.when(i == 0)
            def _():
                s_ref[...] = part

            @pl.when(i > 0)
            def _():
                s_ref[...] += part

    row = pl.BlockSpec((tm, d), lambda i: (i, 0))
    vec = pl.BlockSpec((1, d), lambda i: (0, 0))
    in_specs = [pl.BlockSpec((tm, k), lambda i: (i, 0)), pl.BlockSpec((k, d), lambda i: (0, 0)), row]
    if target is None:
        outs = pl.pallas_call(
            body, name=name, grid=(m // tm,), in_specs=in_specs + [vec] * ng,
            out_specs=[row] * (1 + ng) + [pl.BlockSpec((tm, 1), lambda i: (i, 0))],
            out_shape=[jax.ShapeDtypeStruct((m, d), f32)] + [jax.ShapeDtypeStruct((m, d), bf16)] * ng
            + [jax.ShapeDtypeStruct((m, 1), f32)],
            compiler_params=_params("parallel"))(a, w, x, *gains)
        return outs[0], outs[1:1 + ng], outs[-1]
    return pl.pallas_call(
        body, name=name, grid=(m // tm,), in_specs=in_specs + [row], out_specs=[row, vec],
        out_shape=[jax.ShapeDtypeStruct((m, d), f32), jax.ShapeDtypeStruct((1, d), f32)],
        compiler_params=_params("arbitrary"))(a, w, x, target)


def rms_fwd(x, gains, *, name, tr=512):
    t, d = x.shape
    tr = min(tr, t)
    ng = len(gains)

    def body(*refs):
        x_ref = refs[0]
        g_refs = refs[1:1 + ng]
        h_refs = refs[1 + ng:1 + 2 * ng]
        r_ref = refs[-1]
        xv = x_ref[...]
        r = lax.rsqrt(jnp.mean(xv * xv, axis=1, keepdims=True) + EPS)
        xh = xv * r
        for g_ref, h_ref in zip(g_refs, h_refs):
            h_ref[...] = (xh * g_ref[...]).astype(bf16)
        r_ref[...] = r

    row = pl.BlockSpec((tr, d), lambda i: (i, 0))
    vec = pl.BlockSpec((1, d), lambda i: (0, 0))
    outs = pl.pallas_call(
        body, name=name, grid=(t // tr,), in_specs=[row] + [vec] * ng,
        out_specs=[row] * ng + [pl.BlockSpec((tr, 1), lambda i: (i, 0))],
        out_shape=[jax.ShapeDtypeStruct((t, d), bf16)] * ng + [jax.ShapeDtypeStruct((t, 1), f32)],
        compiler_params=_params("parallel"))(x, *gains)
    return outs[:ng], outs[ng]


def rms_bwd(dh_list, x, r, gains, dx_in, *, name, tr=512):
    t, d = x.shape
    tr = min(tr, t)
    ng = len(gains)

    def body(*refs):
        dh_refs = refs[:ng]
        x_ref, r_ref = refs[ng], refs[ng + 1]
        g_refs = refs[ng + 2:2 * ng + 2]
        dxin_ref = refs[2 * ng + 2]
        dx_ref = refs[2 * ng + 3]
        dg_refs = refs[2 * ng + 4:]
        i = pl.program_id(0)
        rv = r_ref[...]
        xh = x_ref[...] * rv
        acc = dxin_ref[...]
        for dh_ref, g_ref, dg_ref in zip(dh_refs, g_refs, dg_refs):
            dh = dh_ref[...]
            part = jnp.sum(dh * xh, axis=0, keepdims=True)

            @pl.when(i == 0)
            def _(dg_ref=dg_ref, part=part):
                dg_ref[...] = part

            @pl.when(i > 0)
            def _(dg_ref=dg_ref, part=part):
                dg_ref[...] += part

            tg = dh * g_ref[...]
            acc = acc + rv * (tg - xh * jnp.mean(tg * xh, axis=1, keepdims=True))
        dx_ref[...] = acc

    row = pl.BlockSpec((tr, d), lambda i: (i, 0))
    vec = pl.BlockSpec((1, d), lambda i: (0, 0))
    outs = pl.pallas_call(
        body, name=name, grid=(t // tr,),
        in_specs=[row] * ng + [row, pl.BlockSpec((tr, 1), lambda i: (i, 0))] + [vec] * ng + [row],
        out_specs=[row] + [vec] * ng,
        out_shape=[jax.ShapeDtypeStruct((t, d), f32)] + [jax.ShapeDtypeStruct((1, d), f32)] * ng,
        compiler_params=_params("arbitrary"))(*dh_list, x, r, *gains, dx_in)
    return outs[0], outs[1:]


def sgu_gate_fwd(zu, zv, gv, wc, bt, *, name, tr=512):
    t, w = zu.shape
    tr = min(tr, t)
    groups = w // LANES

    def body(zu_ref, zv_ref, gv_ref, wc_ref, bt_ref, y_ref):
        vp = _gelu(zv_ref[...])
        rv = lax.rsqrt(jnp.mean(vp * vp, axis=1, keepdims=True) + EPS)
        vb = (vp * rv * gv_ref[...]).astype(bf16)
        for c in range(tr // CHUNK):
            rows = slice(c * CHUNK, (c + 1) * CHUNK)
            for g in range(groups):
                cols = slice(g * LANES, (g + 1) * LANES)
                sv = _dot(wc_ref[g], vb[rows, cols]) + bt_ref[:, g:g + 1]
                y_ref[rows, cols] = (_gelu(zu_ref[rows, cols]) * sv).astype(bf16)

    row = pl.BlockSpec((tr, w), lambda i: (i, 0))
    return pl.pallas_call(
        body, name=name, grid=(t // tr,),
        in_specs=[row, row, pl.BlockSpec((1, w), lambda i: (0, 0)),
                  pl.BlockSpec((groups, CHUNK, CHUNK), lambda i: (0, 0, 0)),
                  pl.BlockSpec((CHUNK, groups), lambda i: (0, 0))],
        out_specs=row, out_shape=jax.ShapeDtypeStruct((t, w), bf16),
        compiler_params=_params("parallel"))(zu, zv, gv, wc, bt)


def sgu_gate_bwd(zu, zv, dy, gv, wc, bt, *, name, tr=512):
    t, w = zu.shape
    tr = min(tr, t)
    groups = w // LANES
    nsteps = t // tr

    def body(zu_ref, zv_ref, dy_ref, gv_ref, wc_ref, bt_ref,
             dzu_ref, dzv_ref, dgv_ref, dws_ref, dbt_ref, dv_ref, bacc_ref):
        i = pl.program_id(0)

        @pl.when(i == 0)
        def _():
            dgv_ref[...] = jnp.zeros_like(dgv_ref)
            dws_ref[...] = jnp.zeros_like(dws_ref)
            bacc_ref[...] = jnp.zeros_like(bacc_ref)

        vp, vp_grad = _gelu_and_grad(zv_ref[...])
        rv = lax.rsqrt(jnp.mean(vp * vp, axis=1, keepdims=True) + EPS)
        vhat = vp * rv
        vb = (vhat * gv_ref[...]).astype(bf16)
        for c in range(tr // CHUNK):
            rows = slice(c * CHUNK, (c + 1) * CHUNK)
            for g in range(groups):
                cols = slice(g * LANES, (g + 1) * LANES)
                vblk = vb[rows, cols]
                sv = _dot(wc_ref[g], vblk) + bt_ref[:, g:g + 1]
                zub = zu_ref[rows, cols]
                dyb = dy_ref[rows, cols]
                ub, ub_grad = _gelu_and_grad(zub)
                dzu_ref[rows, cols] = (dyb * sv * ub_grad).astype(bf16)
                dsv = dyb * ub
                bacc_ref[:, cols] += dsv
                dsvb = dsv.astype(bf16)
                dv_ref[rows, cols] = _dot_tn(wc_ref[g], dsvb)
                dws_ref[g] += _dot_nt(dsvb, vblk)
        dv = dv_ref[...]
        dgv_ref[...] += jnp.sum(dv * vhat, axis=0, keepdims=True)
        tg = dv * gv_ref[...]
        dvp = rv * (tg - vhat * jnp.mean(tg * vhat, axis=1, keepdims=True))
        dzv_ref[...] = (dvp * vp_grad).astype(bf16)

        @pl.when(i == nsteps - 1)
        def _():
            tt = lax.broadcasted_iota(jnp.int32, (CHUNK, CHUNK), 0)
            ss = lax.broadcasted_iota(jnp.int32, (CHUNK, CHUNK), 1)
            for g in range(groups):
                dws_ref[g] = jnp.where(ss <= tt, dws_ref[g], 0.0)
                dbt_ref[:, g:g + 1] = jnp.sum(bacc_ref[:, g * LANES:(g + 1) * LANES], axis=1, keepdims=True)

    row = pl.BlockSpec((tr, w), lambda i: (i, 0))
    full3 = pl.BlockSpec((groups, CHUNK, CHUNK), lambda i: (0, 0, 0))
    return pl.pallas_call(
        body, name=name, grid=(nsteps,),
        in_specs=[row, row, row, pl.BlockSpec((1, w), lambda i: (0, 0)), full3,
                  pl.BlockSpec((CHUNK, groups), lambda i: (0, 0))],
        out_specs=[row, row, pl.BlockSpec((1, w), lambda i: (0, 0)), full3,
                   pl.BlockSpec((CHUNK, groups), lambda i: (0, 0))],
        out_shape=[jax.ShapeDtypeStruct((t, w), bf16), jax.ShapeDtypeStruct((t, w), bf16),
                   jax.ShapeDtypeStruct((1, w), f32), jax.ShapeDtypeStruct((groups, CHUNK, CHUNK), f32),
                   jax.ShapeDtypeStruct((CHUNK, groups), f32)],
        scratch_shapes=[pltpu.VMEM((tr, w), f32), pltpu.VMEM((CHUNK, w), f32)],
        compiler_params=_params("arbitrary"))(zu, zv, dy, gv, wc, bt)


HALO = 8


def _shift_down(v, halo, k, first):
    r = pltpu.roll(v, k, 0)
    hh = jnp.where(first, 0.0, pltpu.roll(halo, k, 0))
    rid = lax.broadcasted_iota(jnp.int32, (HALO, v.shape[1]), 0)
    head = jnp.where(rid < k, hh, r[0:HALO])
    if v.shape[0] == HALO:
        return head
    return jnp.concatenate([head, r[HALO:]], axis=0)


def _shift_up(v, halo, k, last):
    n = v.shape[0]
    r = pltpu.roll(v, n - k, 0)
    hh = jnp.where(last, 0.0, pltpu.roll(halo, HALO - k, 0))
    rid = lax.broadcasted_iota(jnp.int32, (HALO, v.shape[1]), 0)
    tail = jnp.where(rid >= HALO - k, hh, r[n - HALO:])
    return jnp.concatenate([r[:n - HALO], tail], axis=0)


def _conv(p, halo, w_ref, b_ref, first):
    return (w_ref[2:3, :] * p + w_ref[1:2, :] * _shift_down(p, halo, 1, first)
            + w_ref[0:1, :] * _shift_down(p, halo, 2, first) + b_ref[...])


BF16_ROWS = 16


def ffn_in_fused(h, w_in4, wg, wu, bg, bu, *, name):
    t, k = h.shape
    s_all, _, n_s = w_in4.shape
    half = s_all // 2
    tm = _row_tile(t, 2 * k * n_s * 2, k * 2 + 4 * n_s * 4 + n_s * 2)

    def body(h_ref, hh_ref, wg_ref, wu_ref, cg_ref, cu_ref, bg_ref, bu_ref, pg_ref, pu_ref, gate_ref, up_ref, a_ref):
        first = pl.program_id(1) == 0
        hv, hh = h_ref[...], hh_ref[...]
        outs = []
        for w_ref, c_ref, b_ref, p_ref, o_ref in ((wg_ref, cg_ref, bg_ref, pg_ref, gate_ref),
                                                  (wu_ref, cu_ref, bu_ref, pu_ref, up_ref)):
            p = _dot(hv, w_ref[0])
            p_ref[...] = p
            hu = _conv(p, _dot(hh, w_ref[0])[BF16_ROWS - HALO:], c_ref, b_ref, first)
            o_ref[...] = hu
            outs.append(hu)
        gate, up = outs
        a_ref[...] = (gate * jax.nn.sigmoid(gate) * up).astype(bf16)

    tile = pl.BlockSpec((tm, n_s), lambda j, i: (i, j))
    cw = pl.BlockSpec((3, n_s), lambda j, i: (0, j))
    cb = pl.BlockSpec((1, n_s), lambda j, i: (0, j))
    f = half * n_s
    return pl.pallas_call(
        body, name=name, grid=(half, t // tm),
        in_specs=[pl.BlockSpec((tm, k), lambda j, i: (i, 0)),
                  pl.BlockSpec((BF16_ROWS, k), lambda j, i: (jnp.maximum(i * (tm // BF16_ROWS) - 1, 0), 0)),
                  pl.BlockSpec((1, k, n_s), lambda j, i: (j, 0, 0)),
                  pl.BlockSpec((1, k, n_s), lambda j, i: (j + half, 0, 0)), cw, cw, cb, cb],
        out_specs=[tile] * 5,
        out_shape=[jax.ShapeDtypeStruct((t, f), f32)] * 4 + [jax.ShapeDtypeStruct((t, f), bf16)],
        compiler_params=_params("parallel", "parallel"))(h, h, w_in4, w_in4, wg, wu, bg, bu)


def _gate_grads(gate, up, dav):
    sg = jax.nn.sigmoid(gate)
    return dav * up * (sg * (1.0 + gate * (1.0 - sg))), dav * gate * sg


GATE_BWD_ROWS = 512


def ffn_gate_bwd(dy, w_out, pg, pu, gate, up, wg, wu, *, name):
    t, f = pg.shape
    d = dy.shape[1]
    tr = min(GATE_BWD_ROWS, t)
    nsteps = t // tr
    tc = f // 2

    def body(dy_ref, dyn_ref, w_ref, pg_ref, pu_ref, gate_ref, gaten_ref, up_ref, upn_ref, wg_ref, wu_ref,
             dg_ref, du_ref, sg_ref, su_ref):
        i = pl.program_id(1)
        last = i == nsteps - 1
        w = w_ref[0]
        da = _dot_nt(dy_ref[...].astype(bf16), w)
        da_n = _dot_nt(dyn_ref[...].astype(bf16), w)
        dgate, dup = _gate_grads(gate_ref[...], up_ref[...], da)
        dgate_n, dup_n = _gate_grads(gaten_ref[...], upn_ref[...], da_n)
        rid = lax.broadcasted_iota(jnp.int32, (8, tc), 0)
        for dd, d_n, c_ref, p_ref, o_ref, s_ref in ((dgate, dgate_n, wg_ref, pg_ref, dg_ref, sg_ref),
                                                    (dup, dup_n, wu_ref, pu_ref, du_ref, su_ref)):
            d1, d2 = _shift_up(dd, d_n, 1, last), _shift_up(dd, d_n, 2, last)
            o_ref[...] = (c_ref[2:3, :] * dd + c_ref[1:2, :] * d1 + c_ref[0:1, :] * d2).astype(bf16)
            p = p_ref[...]
            sums = [jnp.sum(d2 * p, axis=0, keepdims=True), jnp.sum(d1 * p, axis=0, keepdims=True),
                    jnp.sum(dd * p, axis=0, keepdims=True), jnp.sum(dd, axis=0, keepdims=True)]
            part = jnp.zeros((8, tc), f32)
            for k, sk in enumerate(sums):
                part = jnp.where(rid == k, sk, part)

            @pl.when(i == 0)
            def _(s_ref=s_ref, part=part):
                s_ref[...] = part

            @pl.when(i > 0)
            def _(s_ref=s_ref, part=part):
                s_ref[...] += part

    def nxt_rows(j, i):
        return (jnp.minimum((i + 1) * (tr // HALO), t // HALO - 1), j)

    tile = pl.BlockSpec((tr, tc), lambda j, i: (i, j))
    nxt = pl.BlockSpec((HALO, tc), nxt_rows)
    wspec = pl.BlockSpec((3, tc), lambda j, i: (0, j))
    stat = pl.BlockSpec((8, tc), lambda j, i: (0, j))
    return pl.pallas_call(
        body, name=name, grid=(2, nsteps),
        in_specs=[pl.BlockSpec((tr, d), lambda j, i: (i, 0)),
                  pl.BlockSpec((HALO, d), lambda j, i: (nxt_rows(j, i)[0], 0)),
                  pl.BlockSpec((1, tc, d), lambda j, i: (j, 0, 0)),
                  tile, tile, tile, nxt, tile, nxt, wspec, wspec],
        out_specs=[tile, tile, stat, stat],
        out_shape=[jax.ShapeDtypeStruct((t, f), bf16), jax.ShapeDtypeStruct((t, f), bf16),
                   jax.ShapeDtypeStruct((8, f), f32), jax.ShapeDtypeStruct((8, f), f32)],
        compiler_params=_params("parallel", "arbitrary"))(
            dy, dy, w_out.reshape(2, tc, d), pg, pu, gate, gate, up, up, wg, wu)


def _head_mean_matrix():
    i = lax.broadcasted_iota(jnp.int32, (LANES, LANES), 0) // HEAD_DIM
    j = lax.broadcasted_iota(jnp.int32, (LANES, LANES), 1) // HEAD_DIM
    return jnp.where(i == j, 1.0 / HEAD_DIM, 0.0).astype(bf16)


def _lane_half(shape):
    return (lax.broadcasted_iota(jnp.int32, shape, 1) % LANES) // HEAD_DIM


def q_norm_fwd(qp, g2, *, name, scale, tr=512):
    t, w = qp.shape
    tr = min(tr, t)

    def body(x_ref, g_ref, o_ref):
        bd = _head_mean_matrix()
        for cb in range(w // LANES):
            cols = slice(cb * LANES, (cb + 1) * LANES)
            xc = x_ref[:, cols]
            rh = lax.rsqrt(_dot_split(xc * xc, bd) + EPS)
            o_ref[:, cols] = (xc * rh * g_ref[...] * scale).astype(bf16)

    row = pl.BlockSpec((tr, w), lambda i: (i, 0))
    return pl.pallas_call(
        body, name=name, grid=(t // tr,), in_specs=[row, pl.BlockSpec((1, LANES), lambda i: (0, 0))],
        out_specs=row, out_shape=jax.ShapeDtypeStruct((t, w), bf16),
        compiler_params=_params("parallel"))(qp, g2)


def q_norm_bwd(dq, qp, g2, *, name, scale, tr=512):
    t, w = qp.shape
    tr = min(tr, t)

    def body(dq_ref, x_ref, g_ref, o_ref, dg_ref):
        i = pl.program_id(0)
        bd = _head_mean_matrix()
        acc = jnp.zeros((1, LANES), f32)
        for cb in range(w // LANES):
            cols = slice(cb * LANES, (cb + 1) * LANES)
            xc = x_ref[:, cols]
            rh = lax.rsqrt(_dot_split(xc * xc, bd) + EPS)
            xh = xc * rh
            dy = dq_ref[:, cols] * scale
            acc = acc + jnp.sum(dy * xh, axis=0, keepdims=True)
            tg = dy * g_ref[...]
            o_ref[:, cols] = (rh * (tg - xh * _dot_split(tg * xh, bd))).astype(bf16)

        @pl.when(i == 0)
        def _():
            dg_ref[...] = acc

        @pl.when(i > 0)
        def _():
            dg_ref[...] += acc

    row = pl.BlockSpec((tr, w), lambda i: (i, 0))
    vec = pl.BlockSpec((1, LANES), lambda i: (0, 0))
    return pl.pallas_call(
        body, name=name, grid=(t // tr,), in_specs=[row, row, vec], out_specs=[row, vec],
        out_shape=[jax.ShapeDtypeStruct((t, w), bf16), jax.ShapeDtypeStruct((1, LANES), f32)],
        compiler_params=_params("arbitrary"))(dq, qp, g2)


def kv_post_fwd(kv, g2, *, name, tr=512):
    t, w = kv.shape
    tr = min(tr, t)
    kw = w // 2

    def body(x_ref, g_ref, k_ref, v_ref):
        bd = _head_mean_matrix()
        half = _lane_half((tr, LANES))
        for cb in range(kw // LANES):
            xc = x_ref[:, cb * LANES:(cb + 1) * LANES]
            rh = lax.rsqrt(_dot_split(xc * xc, bd) + EPS)
            kn = xc * rh * g_ref[...]
            vc = x_ref[:, kw + cb * LANES:kw + (cb + 1) * LANES]
            for src, dst in ((kn, k_ref), (vc, v_ref)):
                sw = pltpu.roll(src, HEAD_DIM, 1)
                for hf in range(2):
                    blk = 2 * cb + hf
                    dst[:, blk * LANES:(blk + 1) * LANES] = jnp.where(half == hf, src, sw).astype(bf16)

    return pl.pallas_call(
        body, name=name, grid=(t // tr,),
        in_specs=[pl.BlockSpec((tr, w), lambda i: (i, 0)), pl.BlockSpec((1, LANES), lambda i: (0, 0))],
        out_specs=[pl.BlockSpec((tr, 2 * kw), lambda i: (i, 0))] * 2,
        out_shape=[jax.ShapeDtypeStruct((t, 2 * kw), bf16)] * 2,
        compiler_params=_params("parallel"))(kv, g2)


def kv_post_bwd(dk2, dv2, kv, g2, *, name, tr=512):
    t, w = kv.shape
    tr = min(tr, t)
    kw = w // 2

    def body(dk_ref, dv_ref, x_ref, g_ref, o_ref, dg_ref):
        i = pl.program_id(0)
        bd = _head_mean_matrix()
        half = _lane_half((tr, LANES))
        acc = jnp.zeros((1, LANES), f32)

        def fold(ref, cb):
            a = ref[:, (2 * cb) * LANES:(2 * cb + 1) * LANES]
            b = ref[:, (2 * cb + 1) * LANES:(2 * cb + 2) * LANES]
            return jnp.where(half == 0, a + pltpu.roll(a, HEAD_DIM, 1), b + pltpu.roll(b, HEAD_DIM, 1))

        for cb in range(kw // LANES):
            cols = slice(cb * LANES, (cb + 1) * LANES)
            xc = x_ref[:, cols]
            rh = lax.rsqrt(_dot_split(xc * xc, bd) + EPS)
            xh = xc * rh
            dy = fold(dk_ref, cb)
            acc = acc + jnp.sum(dy * xh, axis=0, keepdims=True)
            tg = dy * g_ref[...]
            o_ref[:, cols] = (rh * (tg - xh * _dot_split(tg * xh, bd))).astype(bf16)
            o_ref[:, kw + cb * LANES:kw + (cb + 1) * LANES] = fold(dv_ref, cb).astype(bf16)

        @pl.when(i == 0)
        def _():
            dg_ref[...] = acc

        @pl.when(i > 0)
        def _():
            dg_ref[...] += acc

    dup = pl.BlockSpec((tr, 2 * kw), lambda i: (i, 0))
    row = pl.BlockSpec((tr, w), lambda i: (i, 0))
    vec = pl.BlockSpec((1, LANES), lambda i: (0, 0))
    return pl.pallas_call(
        body, name=name, grid=(t // tr,), in_specs=[dup, dup, row, vec], out_specs=[row, vec],
        out_shape=[jax.ShapeDtypeStruct((t, w), bf16), jax.ShapeDtypeStruct((1, LANES), f32)],
        compiler_params=_params("arbitrary"))(dk2, dv2, kv, g2)


def _slope(h):
    return 2.0 ** (-8.0 * (h + 1) / N_Q_HEADS)


GROUP_ROWS = Q_PER_KV * CHUNK


def _band_mask(n):
    tq = lax.broadcasted_iota(jnp.int32, (GROUP_ROWS, 2 * CHUNK), 0) % CHUNK
    jk = lax.broadcasted_iota(jnp.int32, (GROUP_ROWS, 2 * CHUNK), 1)
    dist = tq + CHUNK - jk
    ok = (dist >= 0) & (dist < CHUNK) & jnp.logical_not((n == 0) & (jk < CHUNK))
    return dist.astype(f32), ok


def _band(ref, n, kh):
    p0 = pl.multiple_of(jnp.maximum(n - 1, 0) * CHUNK, CHUNK)
    c0 = pl.multiple_of(n * CHUNK, CHUNK)
    cols = slice(kh * LANES, (kh + 1) * LANES)
    return jnp.concatenate([ref[pl.ds(p0, CHUNK), cols], ref[pl.ds(c0, CHUNK), cols]], axis=0)


def _stack_heads(ref, kh, half):
    parts = []
    for cb in (2 * kh, 2 * kh + 1):
        xc = ref[:, cb * LANES:(cb + 1) * LANES].astype(f32)
        parts += [jnp.where(half == hf, xc, 0.0).astype(bf16) for hf in range(2)]
    return jnp.concatenate(parts, axis=0)


def _unstack_heads(x4, half):
    return (jnp.where(half == 0, x4[0:CHUNK], x4[CHUNK:2 * CHUNK]),
            jnp.where(half == 0, x4[2 * CHUNK:3 * CHUNK], x4[3 * CHUNK:]))


def _per_head_column(kh, values):
    grp = lax.broadcasted_iota(jnp.int32, (GROUP_ROWS, 1), 0) // CHUNK
    col = jnp.full((GROUP_ROWS, 1), values[0], f32)
    for g in range(1, Q_PER_KV):
        col = jnp.where(grp == g, values[g], col)
    return col


def _softmax_band(q4, kband, dist, ok, slope, sink):
    s = _dot_nt(q4, kband)
    s = jnp.where(ok, s - slope * dist, -jnp.inf)
    m = jnp.maximum(jnp.max(s, axis=1, keepdims=True), sink)
    e = jnp.exp(s - m)
    es = jnp.exp(sink - m)
    den = jnp.sum(e, axis=1, keepdims=True) + es
    return e / den, es / den


def attn_fwd(q, k2, v2, sinks, *, name):
    t, w = q.shape
    nb = t // CHUNK

    def body(sink_ref, q_ref, k_ref, v_ref, o_ref):
        n = pl.program_id(0)
        dist, ok = _band_mask(n)
        half = _lane_half((CHUNK, LANES))
        for kh in range(N_KV_HEADS):
            heads = [Q_PER_KV * kh + g for g in range(Q_PER_KV)]
            slope = _per_head_column(kh, [_slope(h) for h in heads])
            sink = _per_head_column(kh, [sink_ref[h] for h in heads])
            q4 = _stack_heads(q_ref, kh, half)
            p, _ = _softmax_band(q4, _band(k_ref, n, kh), dist, ok, slope, sink)
            o4 = _dot(p.astype(bf16), _band(v_ref, n, kh))
            lo, hi = _unstack_heads(o4, half)
            o_ref[:, (2 * kh) * LANES:(2 * kh + 1) * LANES] = lo.astype(bf16)
            o_ref[:, (2 * kh + 1) * LANES:(2 * kh + 2) * LANES] = hi.astype(bf16)

    full = pl.BlockSpec((t, k2.shape[1]), lambda n: (0, 0))
    return pl.pallas_call(
        body, name=name, grid=(nb,),
        in_specs=[pl.BlockSpec(memory_space=pltpu.SMEM), pl.BlockSpec((CHUNK, w), lambda n: (n, 0)), full, full],
        out_specs=pl.BlockSpec((CHUNK, w), lambda n: (n, 0)),
        out_shape=jax.ShapeDtypeStruct((t, w), bf16),
        compiler_params=_params("parallel"))(sinks, q, k2, v2)


def attn_bwd(q, k2, v2, do, sinks, *, name):
    t, w = q.shape
    nb = t // CHUNK
    kw = k2.shape[1]

    def body(sink_ref, q_ref, k_ref, v_ref, do_ref, dq_ref, dk_ref, dv_ref, ds_ref, kc_ref, vc_ref):
        n = pl.program_id(0)

        @pl.when(n == 0)
        def _():
            ds_ref[...] = jnp.zeros_like(ds_ref)
            kc_ref[...] = jnp.zeros_like(kc_ref)
            vc_ref[...] = jnp.zeros_like(vc_ref)
            dk_ref[...] = jnp.zeros_like(dk_ref)
            dv_ref[...] = jnp.zeros_like(dv_ref)

        @pl.when(n == nb)
        def _():
            dk_ref[...] = kc_ref[...]
            dv_ref[...] = vc_ref[...]

        @pl.when(n < nb)
        def _():
            dist, ok = _band_mask(n)
            half = _lane_half((CHUNK, LANES))
            lane = lax.broadcasted_iota(jnp.int32, (1, LANES), 1)
            sink_acc = jnp.zeros((1, LANES), f32)
            for kh in range(N_KV_HEADS):
                heads = [Q_PER_KV * kh + g for g in range(Q_PER_KV)]
                slope = _per_head_column(kh, [_slope(h) for h in heads])
                sink = _per_head_column(kh, [sink_ref[h] for h in heads])
                q4 = _stack_heads(q_ref, kh, half)
                do4 = _stack_heads(do_ref, kh, half)
                kband = _band(k_ref, n, kh)
                vband = _band(v_ref, n, kh)
                p, ps = _softmax_band(q4, kband, dist, ok, slope, sink)
                dp = _dot_nt(do4, vband)
                delta = jnp.sum(p * dp, axis=1, keepdims=True)
                dsb = (p * (dp - delta)).astype(bf16)
                sd = ps * delta
                for g, h in enumerate(heads):
                    part = jnp.sum(sd[g * CHUNK:(g + 1) * CHUNK], axis=0, keepdims=True)
                    sink_acc = sink_acc + jnp.where(lane == h, -part, 0.0)
                lo, hi = _unstack_heads(_dot(dsb, kband), half)
                dq_ref[:, (2 * kh) * LANES:(2 * kh + 1) * LANES] = lo
                dq_ref[:, (2 * kh + 1) * LANES:(2 * kh + 2) * LANES] = hi
                dkb = _dot_tn(dsb, q4)
                dvb = _dot_tn(p.astype(bf16), do4)
                cols = slice(kh * LANES, (kh + 1) * LANES)
                dk_ref[:, cols] = kc_ref[:, cols] + dkb[0:CHUNK]
                dv_ref[:, cols] = vc_ref[:, cols] + dvb[0:CHUNK]
                kc_ref[:, cols] = dkb[CHUNK:]
                vc_ref[:, cols] = dvb[CHUNK:]
            ds_ref[...] += sink_acc

    full = pl.BlockSpec((t, kw), lambda n: (0, 0))
    qblk = pl.BlockSpec((CHUNK, w), lambda n: (jnp.minimum(n, nb - 1), 0))
    kblk = pl.BlockSpec((CHUNK, kw), lambda n: (jnp.maximum(n - 1, 0), 0))
    return pl.pallas_call(
        body, name=name, grid=(nb + 1,),
        in_specs=[pl.BlockSpec(memory_space=pltpu.SMEM), qblk, full, full, qblk],
        out_specs=[qblk, kblk, kblk, pl.BlockSpec((1, LANES), lambda n: (0, 0))],
        out_shape=[jax.ShapeDtypeStruct((t, w), f32), jax.ShapeDtypeStruct((t, kw), f32),
                   jax.ShapeDtypeStruct((t, kw), f32), jax.ShapeDtypeStruct((1, LANES), f32)],
        scratch_shapes=[pltpu.VMEM((CHUNK, kw), f32), pltpu.VMEM((CHUNK, kw), f32)],
        compiler_params=_params("arbitrary"))(sinks, q, k2, v2, do)


def loss_head(y, target, *, name, tr=512):
    t, d = y.shape
    tr = min(tr, t)

    def body(y_ref, t_ref, dy_ref, s_ref):
        i = pl.program_id(0)
        e = y_ref[...] - t_ref[...]
        dy_ref[...] = e * (1.0 / d)
        part = jnp.sum(e * e, axis=0, keepdims=True)

        @pl.when(i == 0)
        def _():
            s_ref[...] = part

        @pl.when(i > 0)
        def _():
            s_ref[...] += part

    row = pl.BlockSpec((tr, d), lambda i: (i, 0))
    vec = pl.BlockSpec((1, d), lambda i: (0, 0))
    return pl.pallas_call(
        body, name=name, grid=(t // tr,), in_specs=[row, row], out_specs=[row, vec],
        out_shape=[jax.ShapeDtypeStruct((t, d), f32), jax.ShapeDtypeStruct((1, d), f32)],
        compiler_params=_params("arbitrary"))(y, target)


N_STEPS = 8


def _row_blocks(shape):
    if len(shape) == 2:
        r, c = shape
        return (r // N_STEPS, c), (lambda s: (s, 0))
    l, r, c = shape
    per = N_STEPS // l
    return (1, r // per, c), (lambda s: (s // per, s % per, 0))


CAST_STEPS = 4


def cast_into_slot(arrays, k_arr, *, name):
    in_specs, out_specs, out_shape, layers = [], [], [], []
    for a in arrays:
        r, c = a.shape[-2:]
        rb = r // CAST_STEPS
        if a.ndim == 2:
            in_specs.append(pl.BlockSpec((rb, c), lambda s, k: (s, 0)))
            layers.append(None)
        else:
            for l in range(a.shape[0]):
                in_specs.append(pl.BlockSpec((1, rb, c), lambda s, k, l=l: (l, s, 0)))
                layers.append(l)
        for _ in range(1 if a.ndim == 2 else a.shape[0]):
            out_specs.append(pl.BlockSpec((1, rb, c), lambda s, k: (k[0], s, 0)))
            out_shape.append(jax.ShapeDtypeStruct((N_SHARDS, r, c), bf16))
    n = len(in_specs)

    def body(k_ref, *refs):
        for i_ref, o_ref, l in zip(refs[:n], refs[n:], layers):
            o_ref[0] = (i_ref[...] if l is None else i_ref[0]).astype(bf16)

    args = []
    for a in arrays:
        args += [a] * (1 if a.ndim == 2 else a.shape[0])
    return pl.pallas_call(
        body, name=name,
        grid_spec=pltpu.PrefetchScalarGridSpec(num_scalar_prefetch=1, grid=(CAST_STEPS,),
                                               in_specs=in_specs, out_specs=out_specs),
        out_shape=out_shape, compiler_params=_params("parallel"))(k_arr, *args)


def adamw(ws, gs, ms, vs, *, name):
    n = len(ws)
    specs, g_specs, g_count = [], [], []
    for w, g_list in zip(ws, gs):
        blk, index = _row_blocks(w.shape)
        specs.append(pl.BlockSpec(blk, index))
        layers = len(g_list)
        per = N_STEPS // layers
        g_count.append(layers)
        for l in range(layers):
            g_specs.append(pl.BlockSpec(blk[-2:], lambda s, l=l, per=per: (jnp.where(s // per == l, s % per, 0), 0)))
    ng = len(g_specs)

    def body(*refs):
        s = pl.program_id(0)
        g_refs = refs[3 * n:3 * n + ng]
        outs = refs[3 * n + ng:]
        off = 0
        for i in range(n):
            w_ref, m_ref, v_ref = refs[i], refs[n + i], refs[2 * n + i]
            go_ref, d_ref, nm_ref, nv_ref = (outs[k * n + i] for k in range(4))
            layers = g_count[i]
            g = g_refs[off][...]
            for l in range(1, layers):
                g = jnp.where(s // (N_STEPS // layers) == l, g_refs[off + l][...], g)
            off += layers
            g = g.reshape(w_ref.shape)
            m = ADAM_B1 * m_ref[...] + (1.0 - ADAM_B1) * g
            v = ADAM_B2 * v_ref[...] + (1.0 - ADAM_B2) * (g * g)
            m_hat = m / ADAM_C1
            v_hat = v / ADAM_C2
            go_ref[...] = g
            d_ref[...] = -ADAM_LR * (m_hat / (jnp.sqrt(v_hat) + ADAM_EPS) + ADAM_WD * w_ref[...])
            nm_ref[...] = m
            nv_ref[...] = v

    outs = pl.pallas_call(
        body, name=name, grid=(N_STEPS,), in_specs=specs * 3 + g_specs, out_specs=specs * 4,
        out_shape=[jax.ShapeDtypeStruct(a.shape, f32) for a in ws] * 4,
        compiler_params=_params("parallel"))(*ws, *ms, *vs, *[g for g_list in gs for g in g_list])
    return [outs[k * n:(k + 1) * n] for k in range(4)]


def _place():
    return lax.axis_index("x"), lax.axis_index("y"), lax.axis_index("c")


def gather_shards(bufs, *, name, split):
    n = len(bufs)

    def body(*refs):
        bufs_ = refs[:n]
        isend, irecv, dsend, drecv = refs[2 * n:]
        x, y, c = _place()
        k = 2 * x + y
        peers = [(1 - x, y, c), (x, 1 - y, c), (1 - x, 1 - y, c)]
        peer_k = [2 * (1 - x) + y, 2 * x + (1 - y), 2 * (1 - x) + (1 - y)]

        def slab(a, q, h):
            if not split[a]:
                return bufs_[a].at[q]
            half = bufs_[a].shape[1] // 2
            return bufs_[a].at[q, pl.ds(pl.multiple_of(h * half, 16), half)]

        def ici(a, j, q):
            return pltpu.make_async_remote_copy(
                src_ref=slab(a, q, c), dst_ref=slab(a, q, c), send_sem=isend.at[3 * a + j], recv_sem=irecv.at[3 * a + j],
                device_id=peers[j], device_id_type=MESH)

        def d2d(a, j, h):
            return pltpu.make_async_remote_copy(
                src_ref=slab(a, peer_k[j], h), dst_ref=slab(a, peer_k[j], h), send_sem=dsend.at[3 * a + j],
                recv_sem=drecv.at[3 * a + j], device_id=(x, y, 1 - c), device_id_type=MESH)

        for a in range(n):
            for j in range(3):
                ici(a, j, k).start()
        for a in range(n):
            for j in range(3):
                ici(a, j, peer_k[j]).wait_recv()
                if split[a]:
                    d2d(a, j, c).start()
        for a in range(n):
            for j in range(3):
                if split[a]:
                    d2d(a, j, 1 - c).wait_recv()
        for a in range(n):
            for j in range(3):
                ici(a, j, k).wait_send()
                if split[a]:
                    d2d(a, j, c).wait_send()

    return pl.pallas_call(
        body, name=name, in_specs=[ANY] * n, out_specs=[ANY] * n,
        out_shape=[jax.ShapeDtypeStruct(b.shape, b.dtype) for b in bufs],
        input_output_aliases={i: i for i in range(n)},
        scratch_shapes=[pltpu.SemaphoreType.DMA((3 * n,))] * 4)(*bufs)


HBM = pl.BlockSpec(memory_space=pltpu.HBM)
SEM = pl.BlockSpec(memory_space=pltpu.SEMAPHORE)
DATAFLOW = pltpu.SideEffectType.DATAFLOW_SIDE_EFFECTING


def _chip_peers():
    x, y, c = _place()
    return 2 * x + y, [(1 - x, y, c), (x, 1 - y, c), (1 - x, 1 - y, c)], [2 * (1 - x) + y, 2 * x + (1 - y), 2 * (1 - x) + (1 - y)]


def gather_start(bufs, groups, after, *, name):
    n = len(bufs)
    ng = len(groups)

    def body(*refs):
        ins = refs[:n]
        sends, recvs = refs[2 * n + 1:2 * n + 1 + ng], refs[2 * n + 1 + ng:2 * n + 1 + 2 * ng]
        token = refs[-1]
        k, peers, _ = _chip_peers()
        for gi, grp in enumerate(groups):
            for pos, a in enumerate(grp):
                for j in range(3):
                    pltpu.make_async_remote_copy(
                        src_ref=ins[a].at[k], dst_ref=ins[a].at[k], send_sem=sends[gi].at[3 * pos + j],
                        recv_sem=recvs[gi].at[3 * pos + j], device_id=peers[j], device_id_type=MESH).start()
        token[...] = jnp.zeros_like(token)

    sems = [pltpu.SemaphoreType.DMA((3 * len(grp),)) for grp in groups]
    outs = pl.pallas_call(
        body, name=name, in_specs=[HBM] * n + [ANY],
        out_specs=[HBM] * n + [SEM] * (2 * ng) + [pl.BlockSpec(memory_space=pltpu.VMEM)],
        out_shape=[pltpu.HBM(b.shape, b.dtype) for b in bufs] + sems + sems + [jax.ShapeDtypeStruct((8, LANES), f32)],
        input_output_aliases={i: i for i in range(n)},
        compiler_params=pltpu.CompilerParams(has_side_effects=DATAFLOW))(
            *[pltpu.with_memory_space_constraint(b, pltpu.HBM) for b in bufs], after)
    return outs[:n], outs[n:n + ng], outs[n + ng:n + 2 * ng], outs[-1]


def gather_wait(bufs, send_sems, recv_sems, after, *, name):
    n = len(bufs)

    def body(*refs):
        ins = refs[:n]
        send, recv = refs[n], refs[n + 1]
        k, peers, peer_k = _chip_peers()
        for a in range(n):
            for j in range(3):
                copy = pltpu.make_async_remote_copy(
                    src_ref=ins[a].at[k], dst_ref=ins[a].at[peer_k[j]], send_sem=send.at[3 * a + j],
                    recv_sem=recv.at[3 * a + j], device_id=peers[j], device_id_type=MESH)
                copy.wait_send()
                copy.wait_recv()

    return pl.pallas_call(
        body, name=name, in_specs=[HBM] * n + [SEM, SEM, ANY], out_specs=[HBM] * n,
        out_shape=[pltpu.HBM(b.shape, b.dtype) for b in bufs],
        input_output_aliases={i: i for i in range(n)},
        compiler_params=pltpu.CompilerParams(has_side_effects=DATAFLOW))(*bufs, send_sems, recv_sems, after)


def sibling_exchange(arrays, *, name):
    n = len(arrays)

    def body(*refs):
        ins, outs = refs[:n], refs[n:2 * n]
        send, recv = refs[2 * n:]
        x, y, c = _place()

        def copy(a):
            return pltpu.make_async_remote_copy(
                src_ref=ins[a], dst_ref=outs[a], send_sem=send.at[a], recv_sem=recv.at[a],
                device_id=(x, y, 1 - c), device_id_type=MESH)

        for a in range(n):
            copy(a).start()
        for a in range(n):
            copy(a).wait_recv()
        for a in range(n):
            copy(a).wait_send()

    return pl.pallas_call(
        body, name=name, in_specs=[ANY] * n, out_specs=[ANY] * n,
        out_shape=[jax.ShapeDtypeStruct(a.shape, a.dtype) for a in arrays],
        scratch_shapes=[pltpu.SemaphoreType.DMA((n,)), pltpu.SemaphoreType.DMA((n,))])(*arrays)


ALL_MASKS = [(mx, my, mc) for mx in (0, 1) for my in (0, 1) for mc in (0, 1)][1:]


def _scatter_copies(srcs, lands, ev, send, recv, esend, erecv):
    x, y, c = _place()
    me = 4 * x + 2 * y + c
    k, peers, peer_k = _chip_peers()
    out = []
    for a in range(len(srcs)):
        for j in range(3):
            out.append(pltpu.make_async_remote_copy(
                src_ref=srcs[a].at[peer_k[j]], dst_ref=lands[a].at[j], send_sem=send.at[3 * a + j],
                recv_sem=recv.at[3 * a + j], device_id=peers[j], device_id_type=MESH))
    start_ev, wait_ev = [], []
    if ev is not None:
        for j, (mx, my, mc) in enumerate(ALL_MASKS):
            peer = (x ^ mx, y ^ my, c ^ mc)
            start_ev.append(pltpu.make_async_remote_copy(
                src_ref=ev.at[me], dst_ref=ev.at[me], send_sem=esend.at[j], recv_sem=erecv.at[j],
                device_id=peer, device_id_type=MESH))
            wait_ev.append(pltpu.make_async_remote_copy(
                src_ref=ev.at[me], dst_ref=ev.at[me ^ (4 * mx + 2 * my + mc)], send_sem=esend.at[j],
                recv_sem=erecv.at[j], device_id=peer, device_id_type=MESH))
    return out, start_ev, wait_ev


def chip_scatter_start(arrays, everyone, after, *, name):
    n = len(arrays)
    ne = 0 if everyone is None else 1
    lands = [pltpu.with_memory_space_constraint(lax.empty((3,) + a.shape[1:], a.dtype), pltpu.HBM) for a in arrays]

    def body(*refs):
        srcs, lands_ = refs[:n], refs[n:2 * n]
        ev = refs[2 * n] if ne else None
        sems = refs[2 * n + ne + 1 + 2 * n + ne:-1]
        send, recv = sems[0], sems[1]
        esend, erecv = (sems[2], sems[3]) if ne else (None, None)
        copies, start_ev, _ = _scatter_copies(srcs, lands_, ev, send, recv, esend, erecv)
        for cp in start_ev + copies:
            cp.start()
        refs[-1][...] = jnp.zeros_like(refs[-1])

    sem_shapes = [pltpu.SemaphoreType.DMA((3 * n,))] * 2 + [pltpu.SemaphoreType.DMA((7,))] * (2 * ne)
    bufs = list(arrays) + lands + ([everyone] if ne else [])
    outs = pl.pallas_call(
        body, name=name, in_specs=[HBM] * len(bufs) + [ANY],
        out_specs=[HBM] * len(bufs) + [SEM] * len(sem_shapes) + [pl.BlockSpec(memory_space=pltpu.VMEM)],
        out_shape=[pltpu.HBM(b.shape, b.dtype) for b in bufs] + sem_shapes + [jax.ShapeDtypeStruct((8, LANES), f32)],
        input_output_aliases={i: i for i in range(len(bufs))},
        compiler_params=pltpu.CompilerParams(has_side_effects=DATAFLOW))(
            *[pltpu.with_memory_space_constraint(b, pltpu.HBM) for b in bufs], after)
    return (n, ne, outs[:-1]), outs[-1]


def chip_scatter_wait(state, after, *, name):
    n, ne, held = state
    nb = 2 * n + ne
    bufs, sems = held[:nb], held[nb:]

    def body(*refs):
        srcs, lands_ = refs[:n], refs[n:2 * n]
        ev = refs[2 * n] if ne else None
        sems_ = refs[nb:nb + len(sems)]
        esend, erecv = (sems_[2], sems_[3]) if ne else (None, None)
        copies, _, wait_ev = _scatter_copies(srcs, lands_, ev, sems_[0], sems_[1], esend, erecv)
        for cp in wait_ev + copies:
            cp.wait_send()
            cp.wait_recv()

    outs = pl.pallas_call(
        body, name=name, in_specs=[HBM] * nb + [SEM] * len(sems) + [ANY], out_specs=[HBM] * nb,
        out_shape=[pltpu.HBM(b.shape, b.dtype) for b in bufs],
        input_output_aliases={i: i for i in range(nb)},
        compiler_params=pltpu.CompilerParams(has_side_effects=DATAFLOW))(*bufs, *sems, after)
    return outs[n:2 * n], (outs[2 * n] if ne else None)


def sibling_merge(bufs, *, name):
    n = len(bufs)

    def body(*refs):
        bufs_ = refs[:n]
        send, recv = refs[2 * n:]
        x, y, c = _place()

        def copy(u, h):
            return pltpu.make_async_remote_copy(
                src_ref=bufs_[u].at[h], dst_ref=bufs_[u].at[h], send_sem=send.at[u], recv_sem=recv.at[u],
                device_id=(x, y, 1 - c), device_id_type=MESH)

        for u in range(n):
            copy(u, c).start()
        for u in range(n):
            copy(u, 1 - c).wait_recv()
        for u in range(n):
            copy(u, c).wait_send()

    return pl.pallas_call(
        body, name=name, in_specs=[ANY] * n, out_specs=[ANY] * n,
        out_shape=[jax.ShapeDtypeStruct(b.shape, b.dtype) for b in bufs],
        input_output_aliases={i: i for i in range(n)},
        scratch_shapes=[pltpu.SemaphoreType.DMA((n,)), pltpu.SemaphoreType.DMA((n,))])(*bufs)


def sum_leading(a, *, name):
    n, r, c = a.shape

    def body(a_ref, o_ref):
        acc = a_ref[0]
        for i in range(1, n):
            acc = acc + a_ref[i]
        o_ref[...] = acc

    rb = r // 2 if r % 16 == 0 else r
    return pl.pallas_call(
        body, name=name, grid=(r // rb,), in_specs=[pl.BlockSpec((n, rb, c), lambda i: (0, i, 0))],
        out_specs=pl.BlockSpec((rb, c), lambda i: (i, 0)), out_shape=jax.ShapeDtypeStruct((r, c), f32),
        compiler_params=_params("parallel"))(a)


def _half_rows(shape):
    return shape[1] // 2 // 2


def rs_cast_other_half(grads, c_arr, *, name):
    n = len(grads)

    def body(c_ref, *refs):
        for i_ref, o_ref in zip(refs[:n], refs[n:]):
            o_ref[...] = i_ref[...].astype(bf16)

    in_specs = [pl.BlockSpec((1, _half_rows(g.shape), g.shape[2]), lambda s, r, c_ref: (s, (1 - c_ref[0]) * 2 + r, 0))
                for g in grads]
    out_specs = [pl.BlockSpec((1, _half_rows(g.shape), g.shape[2]), lambda s, r, c_ref: (s, r, 0)) for g in grads]
    return pl.pallas_call(
        body, name=name,
        grid_spec=pltpu.PrefetchScalarGridSpec(num_scalar_prefetch=1, grid=(N_SHARDS, 2),
                                               in_specs=in_specs, out_specs=out_specs),
        out_shape=[jax.ShapeDtypeStruct((N_SHARDS, g.shape[1] // 2, g.shape[2]), bf16) for g in grads],
        compiler_params=_params("parallel", "parallel"))(c_arr, *grads)


def rs_add_sibling(grads, recvd, ck_arr, *, name):
    n = len(grads)

    def body(ck_ref, *refs):
        s = pl.program_id(1)
        for u in range(n):
            g_ref, r_ref = refs[u], refs[n + u]
            qb_ref, own_ref = refs[2 * n + u], refs[3 * n + u]
            q = g_ref[0] + r_ref[0].astype(f32)
            qb_ref[0] = q.astype(bf16)

            @pl.when(s == ck_ref[1])
            def _(own_ref=own_ref, q=q):
                own_ref[...] = q

    in_specs = [pl.BlockSpec((1, _half_rows(g.shape), g.shape[2]), lambda r, s, ck: (s, ck[0] * 2 + r, 0)) for g in grads]
    in_specs += [pl.BlockSpec((1, _half_rows(g.shape), g.shape[2]), lambda r, s, ck: (s, r, 0)) for g in grads]
    out_specs = [pl.BlockSpec((1, _half_rows(g.shape), g.shape[2]), lambda r, s, ck: (s, r, 0)) for g in grads]
    out_specs += [pl.BlockSpec((_half_rows(g.shape), g.shape[2]), lambda r, s, ck: (r, 0)) for g in grads]
    outs = pl.pallas_call(
        body, name=name,
        grid_spec=pltpu.PrefetchScalarGridSpec(num_scalar_prefetch=1, grid=(2, N_SHARDS),
                                               in_specs=in_specs, out_specs=out_specs),
        out_shape=[jax.ShapeDtypeStruct((N_SHARDS, g.shape[1] // 2, g.shape[2]), bf16) for g in grads]
        + [jax.ShapeDtypeStruct((g.shape[1] // 2, g.shape[2]), f32) for g in grads],
        compiler_params=_params("parallel", "arbitrary"))(ck_arr, *grads, *recvd)
    return outs[:n], outs[n:]


def rs_sum_chips(owns, recvd, ck_arr, *, name):
    n = len(owns)

    def body(ck_ref, *refs):
        for u in range(n):
            own_ref, r_ref, o_ref = refs[u], refs[n + u], refs[2 * n + u]
            o_ref[0] = ((own_ref[...] + r_ref[0].astype(f32)) + r_ref[1].astype(f32)) + r_ref[2].astype(f32)

    in_specs = [pl.BlockSpec((o.shape[0] // 2, o.shape[1]), lambda r, ck: (r, 0)) for o in owns]
    in_specs += [pl.BlockSpec((3, o.shape[0] // 2, o.shape[1]), lambda r, ck: (0, r, 0)) for o in owns]
    out_specs = [pl.BlockSpec((1, o.shape[0] // 2, o.shape[1]), lambda r, ck: (ck[0], r, 0)) for o in owns]
    return pl.pallas_call(
        body, name=name,
        grid_spec=pltpu.PrefetchScalarGridSpec(num_scalar_prefetch=1, grid=(2,), in_specs=in_specs, out_specs=out_specs),
        out_shape=[jax.ShapeDtypeStruct((2,) + o.shape, f32) for o in owns],
        compiler_params=_params("parallel"))(ck_arr, *owns, *recvd)


SMALL = ("a_norm", "a_v_norm", "a_w_s", "a_b_s", "f_norm", "f_conv_w", "f_conv_b", "kv_norm", "k_norm",
         "b_norm", "b_q_norm", "b_sinks")
BIG = ("a_w_in", "a_w_out", "f_w_in", "f_w_out", "w_kv", "b_w_q", "b_w_o")
PACK_COLS = 1024
PACK_ROWS = 8 * N_STEPS


def _pack(parts, rows=PACK_ROWS):
    flat = jnp.concatenate([p.reshape(-1).astype(f32) for p in parts])
    pad = (-flat.shape[0]) % (rows * PACK_COLS)
    return jnp.pad(flat, (0, pad)).reshape(-1, PACK_COLS)


def _unpack(packed, shapes):
    flat = packed.reshape(-1)
    out, off = [], 0
    for s in shapes:
        size = math.prod(s)
        out.append(flat[off:off + size].reshape(s))
        off += size
    return out


def _ffn_fwd(x, g, h, r, w_in4, conv_w, conv_b, f, tag):
    wg, wu = conv_w[:, :f], conv_w[:, f:]
    bg, bu = conv_b[None, :f], conv_b[None, f:]
    pg, pu, gate, up, a = ffn_in_fused(h, w_in4, wg, wu, bg, bu, name=f"ffn{tag}_in")
    return a, (x, g, h, r, pg, pu, gate, up, a, wg, wu)


def _ffn_bwd(dy, saved, w_in4, w_out, tag):
    x, g, h, r, pg, pu, gate, up, a, wg, wu = saved
    f = w_out.shape[0]
    d_w_out = mm_tn(a, [dy], name=f"ffn{tag}_dwout", n_s=w_out.shape[1], tki=f // 2)
    dpg, dpu, sg, su = ffn_gate_bwd(dy, w_out, pg, pu, gate, up, wg, wu, name=f"ffn{tag}_dgate")
    d_w_in = mm_tn(h, [dpg, dpu], name=f"ffn{tag}_dwin", n_s=w_in4.shape[2])
    dx, dg = mm_nt_rms_bwd([dpg, dpu], w_in4, x, r, g, dy, name=f"ffn{tag}_dh")
    d_conv_w = jnp.concatenate([sg[0:3], su[0:3]], axis=1)
    d_conv_b = jnp.concatenate([sg[3], su[3]], axis=0)
    return dx, dg, d_w_in, d_conv_w, d_conv_b, d_w_out


def _rs_front(units, c_arr, tag):
    other_bf = rs_cast_other_half(units, c_arr, name=f"rs_cast{tag}")
    from_sib = sibling_exchange(list(other_bf), name=f"rs_sibling{tag}")
    return rs_add_sibling(units, from_sib, c_arr, name=f"rs_add{tag}")


def _rs_back(own, from_chips, c_arr, tag):
    halves = rs_sum_chips(list(own), list(from_chips), c_arr, name=f"rs_sum{tag}")
    return [m.reshape(-1, m.shape[2]) for m in sibling_merge(list(halves), name=f"rs_merge{tag}")]


def kernel(x, a_norm, a_w_in, a_v_norm, a_w_s, a_b_s, a_w_out, f_norm, f_w_in, f_conv_w, f_conv_b, f_w_out, kv_norm, w_kv, k_norm, b_norm, b_w_q, b_q_norm, b_sinks, b_w_o, loss_target, m_a_norm, m_a_w_in, m_a_v_norm, m_a_w_s, m_a_b_s, m_a_w_out, m_f_norm, m_f_w_in, m_f_conv_w, m_f_conv_b, m_f_w_out, m_kv_norm, m_w_kv, m_k_norm, m_b_norm, m_b_w_q, m_b_q_norm, m_b_sinks, m_b_w_o, v_a_norm, v_a_w_in, v_a_v_norm, v_a_w_s, v_a_b_s, v_a_w_out, v_f_norm, v_f_w_in, v_f_conv_w, v_f_conv_b, v_f_w_out, v_kv_norm, v_w_kv, v_k_norm, v_b_norm, v_b_w_q, v_b_q_norm, v_b_sinks, v_b_w_o):
    args = dict(locals())
    weights = {n: args[n] for n in SMALL + BIG}
    moms = {n: args["m_" + n] for n in SMALL + BIG}
    vars_ = {n: args["v_" + n] for n in SMALL + BIG}
    t, d = x.shape[1], x.shape[2]
    xi, yi, ci = _place()
    chip = 2 * xi + yi

    big_local = [a_w_in[0], a_w_out[0], f_w_in, f_w_out, w_kv, b_w_q[0], b_w_o[0]]
    c_arr = jnp.stack([ci, chip]).astype(jnp.int32)
    k_arr = jnp.stack([chip]).astype(jnp.int32)
    b_ain, b_aout, b_fin0, b_fin1, b_fout0, b_fout1, b_kv, b_q, b_o = cast_into_slot(big_local, k_arr, name="cast_weights")
    small_cols = _pack([a_norm, a_v_norm, f_conv_w], rows=8)
    b_small = lax.dynamic_update_slice(jnp.zeros((N_SHARDS,) + small_cols.shape, f32), small_cols[None], (chip, 0, 0))
    g_small, g_a_w_in, g_a_w_out = gather_shards([b_small, b_ain, b_aout], name="gather_first", split=[False, True, True])
    later, send_sems, recv_sems, token = gather_start([b_fin0, b_fout0, b_kv, b_q, b_o, b_fin1, b_fout1],
                                                      [[0], [1, 2, 3, 4], [5, 6]], g_small, name="gather_start")
    ns_cols = a_norm.shape[1]
    nf_cols = f_conv_w.shape[2]
    parts = [_unpack(g_small[k], [a_norm.shape, a_v_norm.shape, f_conv_w.shape]) for k in range(N_SHARDS)]
    a_norm_f = jnp.concatenate([p[0] for p in parts], axis=1) + token[0, 0]
    a_v_norm_f = jnp.concatenate([p[1] for p in parts], axis=1)
    conv_w_f = jnp.concatenate([p[2] for p in parts], axis=2)
    w_a_in = g_a_w_in
    w_a_out = g_a_w_out.reshape(1, -1, d)

    x0 = x[0]
    tril = jnp.tril(jnp.ones((CHUNK, CHUNK), dtype=bool))
    wc = jnp.where(tril[None], a_w_s[0], 0.0).astype(bf16)
    bt = a_b_s[0].T
    kg2 = jnp.tile(k_norm, 2)[None]
    qg2 = jnp.tile(b_q_norm[0], 2)[None]

    (h_a,), r_a = rms_fwd(x0, [a_norm_f], name="a_norm")
    zu = mm_nn(h_a, w_a_in, name="a_in_u", s0=0, ns=2)
    zv = mm_nn(h_a, w_a_in, name="a_in_v", s0=2, ns=2)
    y_a = sgu_gate_fwd(zu, zv, a_v_norm_f, wc, bt, name="a_gate")
    f = f_w_out.shape[1] * N_SHARDS
    x1, (h_f0,), r_f0 = mm_residual(y_a, w_a_out[0], x0, name="a_out", gains=[f_norm[0:1]])
    (g_fin0,) = gather_wait(later[0:1], send_sems[0], recv_sems[0], x1, name="gather_wait_0")
    w_f_in = [g_fin0, None]
    a0, ffn0 = _ffn_fwd(x1, f_norm[0:1], h_f0, r_f0, w_f_in[0], conv_w_f[0], f_conv_b[0], f, "0")
    g_fout0, g_w_kv, g_b_w_q, g_b_w_o = gather_wait(later[1:5], send_sems[1], recv_sems[1], a0, name="gather_wait_1")
    w_f_out = [g_fout0.reshape(-1, d), None]
    w_kv_f = g_w_kv.reshape(1, d, -1)
    w_q_f = g_b_w_q.reshape(1, d, -1)
    w_o_f = g_b_w_o.reshape(1, -1, d)
    x2, (h_k, h_q), r_b = mm_residual(a0, w_f_out[0], x1, name="ffn0_out", gains=[kv_norm[None], b_norm])
    kv = mm_nn(h_k, w_kv_f, name="kv_proj")
    k2, v2 = kv_post_fwd(kv, kg2, name="kv_post")
    qp = mm_nn(h_q, w_q_f, name="q_proj")
    qn = q_norm_fwd(qp, qg2, name="q_norm", scale=HEAD_DIM ** -0.5)
    o = attn_fwd(qn, k2, v2, b_sinks[0], name="attn")
    x3, (h_f1,), r_f1 = mm_residual(o, w_o_f[0], x2, name="o_proj", gains=[f_norm[1:2]])
    g_fin1, g_fout1 = gather_wait(later[5:7], send_sems[2], recv_sems[2], x3, name="gather_wait_2")
    w_f_in[1] = g_fin1
    w_f_out[1] = g_fout1.reshape(-1, d)
    a1, ffn1 = _ffn_fwd(x3, f_norm[1:2], h_f1, r_f1, w_f_in[1], conv_w_f[1], f_conv_b[1], f, "1")
    dx4, sq = mm_residual(a1, w_f_out[1], x3, name="ffn1_out", target=loss_target[0])
    loss_part = (0.5 * jnp.sum(sq) / d).reshape(1)

    dx3, d_fn1, d_fwin1, d_cw1, d_cb1, d_fwout1 = _ffn_bwd(dx4, ffn1, w_f_in[1], w_f_out[1], "1")
    do = mm_nt([dx3], w_o_f, name="o_proj_dx")
    d_w_o = mm_tn(o, [dx3], name="o_proj_dw", n_s=d)
    dqn, dk2, dv2, dsink = attn_bwd(qn, k2, v2, do, b_sinks[0], name="attn_bwd")
    dqp, dqg = q_norm_bwd(dqn, qp, qg2, name="q_norm_bwd", scale=HEAD_DIM ** -0.5)
    dkv, dkg = kv_post_bwd(dk2, dv2, kv, kg2, name="kv_post_bwd")
    d_w_q = mm_tn(h_q, [dqp], name="q_proj_dw", n_s=w_q_f.shape[2])
    d_w_kv = mm_tn(h_k, [dkv], name="kv_proj_dw", n_s=w_kv_f.shape[2])
    dh_k = mm_nt([dkv], w_kv_f, name="kv_proj_dx")
    dx2, d_bn, d_kvn = mm_nt_rms_bwd([dqp], w_q_f, x2, r_b, b_norm, dx3, name="q_proj_dx", extra=(dh_k, kv_norm[None]))
    sh = N_SHARDS
    units1 = [d_fwin1, d_fwout1.reshape(sh, -1, d), d_w_kv.reshape(sh, -1, d_w_kv.shape[2]),
              d_w_q.reshape(sh, -1, d_w_q.shape[2]), d_w_o.reshape(sh, -1, d)]
    chip_bf1, own1 = _rs_front(units1, c_arr, "1")
    scatter1, token1 = chip_scatter_start(list(chip_bf1), None, dx2, name="rs_chips_start1")
    ffn0 = ffn0[:9] + (ffn0[9] + token1[0, 0],) + ffn0[10:]
    dx1, d_fn0, d_fwin0, d_cw0, d_cb0, d_fwout0 = _ffn_bwd(dx2, ffn0, w_f_in[0], w_f_out[0], "0")
    chip_bf2, own2 = _rs_front([d_fwin0, d_fwout0.reshape(sh, -1, d)], c_arr, "2")
    scatter2, token2 = chip_scatter_start(list(chip_bf2), None, dx1, name="rs_chips_start2")
    a_v_norm_f = a_v_norm_f + token2[0, 0]
    dy_a = mm_nt([dx1], w_a_out, name="a_out_dx")
    d_w_aout = mm_tn(y_a, [dx1], name="a_out_dw", n_s=d)
    dzu, dzv, d_avn, d_ws, d_bt = sgu_gate_bwd(zu, zv, dy_a, a_v_norm_f, wc, bt, name="a_gate_bwd")
    d_w_ain = mm_tn(h_a, [dzu, dzv], name="a_in_dw", n_s=w_a_in.shape[2])
    dx0, d_an = mm_nt_rms_bwd([dzu, dzv], w_a_in, x0, r_a, a_norm_f, dx1, name="a_in_dx")
    grad_x = dx0[None]

    chip_bf3, own3 = _rs_front([d_w_ain, d_w_aout.reshape(sh, -1, d)], c_arr, "3")
    d_fn = jnp.concatenate([d_fn0, d_fn1], axis=0)
    d_cw = jnp.stack([d_cw0, d_cw1])
    d_cb = jnp.stack([d_cb0, d_cb1])
    d_kg = (dkg[0, :HEAD_DIM] + dkg[0, HEAD_DIM:])
    d_qg = (dqg[0, :HEAD_DIM] + dqg[0, HEAD_DIM:])[None]
    small_full = [d_an, d_avn, d_ws[None], d_bt.T[None], d_fn, d_cw, d_cb, d_kvn[0], d_kg, d_bn, d_qg,
                  dsink[:, :N_Q_HEADS], loss_part]
    packed = _pack(small_full)
    me = 4 * xi + 2 * yi + ci
    everyone = lax.dynamic_update_slice(lax.empty((N_DEV,) + packed.shape, f32), packed[None], (me, 0, 0))
    scatter3, _ = chip_scatter_start(list(chip_bf3), everyone, chip_bf3[0], name="rs_chips_start3")
    from_chips1, _ = chip_scatter_wait(scatter1, scatter3[2][0], name="rs_chips_wait1")
    fin1, fout1, gkv, gq, go = _rs_back(own1, from_chips1, c_arr, "1")
    from_chips2, _ = chip_scatter_wait(scatter2, fin1, name="rs_chips_wait2")
    fin0, fout0 = _rs_back(own2, from_chips2, c_arr, "2")
    late = ("f_w_in", "f_w_out", "w_kv", "b_w_q", "b_w_o")
    res_late = adamw([weights[n] for n in late], [[fin0, fin1], [fout0, fout1], [gkv], [gq], [go]],
                     [moms[n] for n in late], [vars_[n] for n in late], name="adamw_late")
    from_chips3, from_all = chip_scatter_wait(scatter3, res_late[1][2], name="rs_chips_wait3")
    ain, aout = _rs_back(own3, from_chips3, c_arr, "3")
    first = ("a_w_in", "a_w_out")
    res_first = adamw([weights[n] for n in first], [[ain], [aout]], [moms[n] for n in first],
                      [vars_[n] for n in first], name="adamw_first")
    big = {n: tuple(r[i] for r in res_late) for i, n in enumerate(late)}
    big.update({n: tuple(r[i] for r in res_first) for i, n in enumerate(first)})

    full_shapes = [g.shape for g in small_full]
    small_g = _unpack(sum_leading(from_all, name="small_sum"), full_shapes)
    loss = small_g.pop()[0]
    small_g[0] = lax.dynamic_slice_in_dim(small_g[0], chip * ns_cols, ns_cols, axis=1)
    small_g[1] = lax.dynamic_slice_in_dim(small_g[1], chip * ns_cols, ns_cols, axis=1)
    small_g[5] = lax.dynamic_slice_in_dim(small_g[5], chip * nf_cols, nf_cols, axis=2)
    small_shapes = [weights[n].shape for n in SMALL]
    small_g = [g.reshape(s) for g, s in zip(small_g, small_shapes)]
    pw, pg_, pm, pv = (_pack(v) for v in ([weights[n] for n in SMALL], small_g, [moms[n] for n in SMALL],
                                          [vars_[n] for n in SMALL]))
    _, (sd,), (sm,), (sv,) = adamw([pw], [[pg_]], [pm], [pv], name="adamw_small")
    small_d, small_m, small_v = (_unpack(v, small_shapes) for v in (sd, sm, sv))

    out = {}
    for i, n in enumerate(SMALL):
        out[n] = (small_g[i], small_d[i], small_m[i], small_v[i])
    out.update(big)
    order = ["a_norm", "a_w_in", "a_v_norm", "a_w_s", "a_b_s", "a_w_out", "f_norm", "f_w_in", "f_conv_w", "f_conv_b",
             "f_w_out", "kv_norm", "w_kv", "k_norm", "b_norm", "b_w_q", "b_q_norm", "b_sinks", "b_w_o"]
    return (loss, grad_x, *[out[n][0] for n in order], *[out[n][1] for n in order],
            *[out[n][2] for n in order], *[out[n][3] for n in order])
```

```python
import functools
import math

import jax
import jax.numpy as jnp
from jax import lax
from jax.experimental import pallas as pl
from jax.experimental.pallas import tpu as pltpu

f32 = jnp.float32
bf16 = jnp.bfloat16
MESH = pl.DeviceIdType.MESH
ANY = pl.BlockSpec(memory_space=pl.ANY)

EPS = 1e-6
LANES = 128
CHUNK = 128
HEAD_DIM = 64
N_Q_HEADS = 16
N_KV_HEADS = 4
Q_PER_KV = N_Q_HEADS // N_KV_HEADS
N_SHARDS = 4
N_DEV = 8

ADAM_LR = 0.001
ADAM_B1 = 0.9
ADAM_B2 = 0.999
ADAM_EPS = 1e-08
ADAM_WD = 0.01
ADAM_STEP = 10
ADAM_C1 = 1.0 - ADAM_B1 ** ADAM_STEP
ADAM_C2 = 1.0 - ADAM_B2 ** ADAM_STEP

_INV_SQRT2 = 1.0 / math.sqrt(2.0)
_INV_SQRT2PI = 1.0 / math.sqrt(2.0 * math.pi)


def _params(*sem):
    return pltpu.CompilerParams(dimension_semantics=sem)


def _gelu(z):
    return 0.5 * z * (1.0 + lax.erf(z * _INV_SQRT2))


def _gelu_and_grad(z):
    cdf = 0.5 * (1.0 + lax.erf(z * _INV_SQRT2))
    return z * cdf, cdf + z * jnp.exp(-0.5 * z * z) * _INV_SQRT2PI


def _dot(a, b):
    return jnp.dot(a, b, preferred_element_type=f32)


def _dot_nt(a, b):
    return lax.dot_general(a, b, (((1,), (1,)), ((), ())), preferred_element_type=f32)


def _dot_tn(a, b):
    return lax.dot_general(a, b, (((0,), (0,)), ((), ())), preferred_element_type=f32)


def _dot_split(a, b):
    hi = a.astype(bf16)
    lo = (a - hi.astype(f32)).astype(bf16)
    return _dot(hi, b) + _dot(lo, b)


VMEM_TILE_BUDGET = 40 * 1024 * 1024
MAX_ROW_TILE = 2048


def _row_tile(m, fixed_bytes, row_bytes):
    tm = min(m, MAX_ROW_TILE)
    while tm > 256 and 2 * (fixed_bytes + tm * row_bytes) > VMEM_TILE_BUDGET:
        tm //= 2
    return tm


def _isz(a):
    return jnp.dtype(a.dtype).itemsize


def mm_nn(a, w3, *, name, s0=0, ns=None, add=None, out_dtype=f32):
    m, k = a.shape
    s_all, _, n_s = w3.shape
    ns = s_all if ns is None else ns
    tm = _row_tile(m, k * n_s * 2, k * _isz(a) + n_s * jnp.dtype(out_dtype).itemsize + (0 if add is None else n_s * 4))

    def body(*refs):
        if add is None:
            a_ref, w_ref, o_ref = refs
            acc = _dot(a_ref[...].astype(bf16), w_ref[0])
        else:
            a_ref, w_ref, add_ref, o_ref = refs
            acc = _dot(a_ref[...].astype(bf16), w_ref[0]) + add_ref[...]
        o_ref[...] = acc.astype(out_dtype)

    in_specs = [pl.BlockSpec((tm, k), lambda j, i: (i, 0)),
                pl.BlockSpec((1, k, n_s), lambda j, i: (s0 + j, 0, 0))]
    args = [a, w3]
    if add is not None:
        in_specs.append(pl.BlockSpec((tm, n_s), lambda j, i: (i, j)))
        args.append(add)
    return pl.pallas_call(
        body, name=name, grid=(ns, m // tm), in_specs=in_specs,
        out_specs=pl.BlockSpec((tm, n_s), lambda j, i: (i, j)),
        out_shape=jax.ShapeDtypeStruct((m, ns * n_s), out_dtype),
        compiler_params=_params("parallel", "parallel"))(*args)


def mm_nt(a_list, w3, *, name, tko=None, add=None, out_dtype=f32):
    s_all, k_out, n_s = w3.shape
    m = a_list[0].shape[0]
    na = len(a_list)
    spa = s_all // na
    tko = k_out if tko is None else tko
    tm = _row_tile(m, tko * n_s * 2, na * n_s * _isz(a_list[0]) + tko * 4 * (1 if add is None else 2))

    def body(*refs):
        a_refs = refs[:na]
        w_ref = refs[na]
        o_ref = refs[-1]
        s = pl.program_id(2)

        @pl.when(s == 0)
        def _():
            if add is None:
                o_ref[...] = jnp.zeros_like(o_ref)
            else:
                o_ref[...] = refs[na + 1][...]

        for idx in range(na):
            @pl.when(s // spa == idx)
            def _(idx=idx):
                o_ref[...] += _dot_nt(a_refs[idx][...].astype(bf16), w_ref[0])

    def a_map(idx):
        return lambda ko, i, s: (i, jnp.clip(s - idx * spa, 0, spa - 1))

    in_specs = [pl.BlockSpec((tm, n_s), a_map(idx)) for idx in range(na)]
    in_specs.append(pl.BlockSpec((1, tko, n_s), lambda ko, i, s: (s, ko, 0)))
    args = list(a_list) + [w3]
    if add is not None:
        in_specs.append(pl.BlockSpec((tm, tko), lambda ko, i, s: (i, ko)))
        args.append(add)
    return pl.pallas_call(
        body, name=name, grid=(k_out // tko, m // tm, s_all), in_specs=in_specs,
        out_specs=pl.BlockSpec((tm, tko), lambda ko, i, s: (i, ko)),
        out_shape=jax.ShapeDtypeStruct((m, k_out), out_dtype),
        compiler_params=_params("parallel", "parallel", "arbitrary"))(*args)


def mm_tn(a, b_list, *, name, n_s, tki=None):
    m, k_in = a.shape
    na = len(b_list)
    s_all = sum(b.shape[1] for b in b_list) // n_s
    spa = s_all // na
    tki = k_in if tki is None else tki
    tm = _row_tile(m, tki * n_s * 4, tki * _isz(a) + na * n_s * _isz(b_list[0]))

    def body(*refs):
        a_ref = refs[0]
        b_refs = refs[1:1 + na]
        o_ref = refs[-1]
        s = pl.program_id(0)
        r = pl.program_id(2)

        @pl.when(r == 0)
        def _():
            o_ref[...] = jnp.zeros_like(o_ref)

        for idx in range(na):
            @pl.when(s // spa == idx)
            def _(idx=idx):
                o_ref[0] += _dot_tn(a_ref[...].astype(bf16), b_refs[idx][...].astype(bf16))

    def b_map(idx):
        def index(s, ki, r):
            active = (s // spa) == idx
            return (jnp.where(active, r, 0), jnp.clip(s - idx * spa, 0, spa - 1))
        return index

    in_specs = [pl.BlockSpec((tm, tki), lambda s, ki, r: (r, ki))]
    in_specs += [pl.BlockSpec((tm, n_s), b_map(idx)) for idx in range(na)]
    return pl.pallas_call(
        body, name=name, grid=(s_all, k_in // tki, m // tm), in_specs=in_specs,
        out_specs=pl.BlockSpec((1, tki, n_s), lambda s, ki, r: (s, ki, 0)),
        out_shape=jax.ShapeDtypeStruct((s_all, k_in, n_s), f32),
        compiler_params=_params("parallel", "parallel", "arbitrary"))(a, *b_list)


def mm_nt_rms_bwd(a_list, w3, x, r, g, dx_in, *, name, extra=None):
    s_all, d, n_s = w3.shape
    m = a_list[0].shape[0]
    na = len(a_list)
    spa = s_all // na
    ne = 0 if extra is None else 1
    tm = _row_tile(m, d * n_s * 2, na * n_s * _isz(a_list[0]) + d * 4 * (4 + ne))

    def body(*refs):
        a_refs, w_ref = refs[:na], refs[na]
        x_ref, r_ref, g_ref, dxin_ref = refs[na + 1:na + 5]
        dh2_ref, g2_ref = (refs[na + 5], refs[na + 6]) if ne else (None, None)
        outs = refs[na + 5 + 2 * ne:]
        dx_ref, dg_ref = outs[0], outs[1]
        dg2_ref = outs[2] if ne else None
        acc_ref = outs[-1]
        i, s = pl.program_id(0), pl.program_id(1)

        @pl.when(s == 0)
        def _():
            acc_ref[...] = jnp.zeros_like(acc_ref)

        for idx in range(na):
            @pl.when(s // spa == idx)
            def _(idx=idx):
                acc_ref[...] += _dot_nt(a_refs[idx][...].astype(bf16), w_ref[0])

        @pl.when(s == s_all - 1)
        def _():
            rv = r_ref[...]
            xh = x_ref[...] * rv
            total = dxin_ref[...]
            pairs = [(acc_ref[...], g_ref, dg_ref)] + ([(dh2_ref[...], g2_ref, dg2_ref)] if ne else [])
            for dh, gain_ref, dgain_ref in pairs:
                part = jnp.sum(dh * xh, axis=0, keepdims=True)

                @pl.when(i == 0)
                def _(dgain_ref=dgain_ref, part=part):
                    dgain_ref[...] = part

                @pl.when(i > 0)
                def _(dgain_ref=dgain_ref, part=part):
                    dgain_ref[...] += part

                tg = dh * gain_ref[...]
                total = total + rv * (tg - xh * jnp.mean(tg * xh, axis=1, keepdims=True))
            dx_ref[...] = total

    def a_map(idx):
        return lambda i, s: (i, jnp.clip(s - idx * spa, 0, spa - 1))

    row = pl.BlockSpec((tm, d), lambda i, s: (i, 0))
    vec = pl.BlockSpec((1, d), lambda i, s: (0, 0))
    in_specs = [pl.BlockSpec((tm, n_s), a_map(idx)) for idx in range(na)]
    in_specs += [pl.BlockSpec((1, d, n_s), lambda i, s: (s, 0, 0)), row, pl.BlockSpec((tm, 1), lambda i, s: (i, 0)), vec, row]
    args = list(a_list) + [w3, x, r, g, dx_in]
    if ne:
        in_specs += [row, vec]
        args += list(extra)
    outs = pl.pallas_call(
        body, name=name, grid=(m // tm, s_all), in_specs=in_specs, out_specs=[row] + [vec] * (1 + ne),
        out_shape=[jax.ShapeDtypeStruct((m, d), f32)] + [jax.ShapeDtypeStruct((1, d), f32)] * (1 + ne),
        scratch_shapes=[pltpu.VMEM((tm, d), f32)],
        compiler_params=_params("arbitrary", "arbitrary"))(*args)
    return outs


def mm_residual(a, w, x, *, name, gains=(), target=None):
    m, k = a.shape
    d = w.shape[1]
    ng = len(gains)
    tm = _row_tile(m, k * d * 2, k * _isz(a) + d * 4 * 3 + ng * d * 2)

    def body(*refs):
        a_ref, w_ref, x_ref = refs[:3]
        y = _dot(a_ref[...].astype(bf16), w_ref[...]) + x_ref[...]
        if target is None:
            g_refs = refs[3:3 + ng]
            y_ref = refs[3 + ng]
            h_refs = refs[4 + ng:4 + 2 * ng]
            r_ref = refs[-1]
            y_ref[...] = y
            r = lax.rsqrt(jnp.mean(y * y, axis=1, keepdims=True) + EPS)
            yh = y * r
            for g_ref, h_ref in zip(g_refs, h_refs):
                h_ref[...] = (yh * g_ref[...]).astype(bf16)
            r_ref[...] = r
        else:
            t_ref, dy_ref, s_ref = refs[3:]
            i = pl.program_id(0)
            e = y - t_ref[...]
            dy_ref[...] = e * (1.0 / d)
            part = jnp.sum(e * e, axis=0, keepdims=True)

            @pl.when(i == 0)
            def _():
                s_ref[...] = part

            @pl.when(i > 0)
            def _():
                s_ref[...] += part

    row = pl.BlockSpec((tm, d), lambda i: (i, 0))
    vec = pl.BlockSpec((1, d), lambda i: (0, 0))
    in_specs = [pl.BlockSpec((tm, k), lambda i: (i, 0)), pl.BlockSpec((k, d), lambda i: (0, 0)), row]
    if target is None:
        outs = pl.pallas_call(
            body, name=name, grid=(m // tm,), in_specs=in_specs + [vec] * ng,
            out_specs=[row] * (1 + ng) + [pl.BlockSpec((tm, 1), lambda i: (i, 0))],
            out_shape=[jax.ShapeDtypeStruct((m, d), f32)] + [jax.ShapeDtypeStruct((m, d), bf16)] * ng
            + [jax.ShapeDtypeStruct((m, 1), f32)],
            compiler_params=_params("parallel"))(a, w, x, *gains)
        return outs[0], outs[1:1 + ng], outs[-1]
    return pl.pallas_call(
        body, name=name, grid=(m // tm,), in_specs=in_specs + [row], out_specs=[row, vec],
        out_shape=[jax.ShapeDtypeStruct((m, d), f32), jax.ShapeDtypeStruct((1, d), f32)],
        compiler_params=_params("arbitrary"))(a, w, x, target)


def rms_fwd(x, gains, *, name, tr=512):
    t, d = x.shape
    tr = min(tr, t)
    ng = len(gains)

    def body(*refs):
        x_ref = refs[0]
        g_refs = refs[1:1 + ng]
        h_refs = refs[1 + ng:1 + 2 * ng]
        r_ref = refs[-1]
        xv = x_ref[...]
        r = lax.rsqrt(jnp.mean(xv * xv, axis=1, keepdims=True) + EPS)
        xh = xv * r
        for g_ref, h_ref in zip(g_refs, h_refs):
            h_ref[...] = (xh * g_ref[...]).astype(bf16)
        r_ref[...] = r

    row = pl.BlockSpec((tr, d), lambda i: (i, 0))
    vec = pl.BlockSpec((1, d), lambda i: (0, 0))
    outs = pl.pallas_call(
        body, name=name, grid=(t // tr,), in_specs=[row] + [vec] * ng,
        out_specs=[row] * ng + [pl.BlockSpec((tr, 1), lambda i: (i, 0))],
        out_shape=[jax.ShapeDtypeStruct((t, d), bf16)] * ng + [jax.ShapeDtypeStruct((t, 1), f32)],
        compiler_params=_params("parallel"))(x, *gains)
    return outs[:ng], outs[ng]


def rms_bwd(dh_list, x, r, gains, dx_in, *, name, tr=512):
    t, d = x.shape
    tr = min(tr, t)
    ng = len(gains)

    def body(*refs):
        dh_refs = refs[:ng]
        x_ref, r_ref = refs[ng], refs[ng + 1]
        g_refs = refs[ng + 2:2 * ng + 2]
        dxin_ref = refs[2 * ng + 2]
        dx_ref = refs[2 * ng + 3]
        dg_refs = refs[2 * ng + 4:]
        i = pl.program_id(0)
        rv = r_ref[...]
        xh = x_ref[...] * rv
        acc = dxin_ref[...]
        for dh_ref, g_ref, dg_ref in zip(dh_refs, g_refs, dg_refs):
            dh = dh_ref[...]
            part = jnp.sum(dh * xh, axis=0, keepdims=True)

            @pl.when(i == 0)
            def _(dg_ref=dg_ref, part=part):
                dg_ref[...] = part

            @pl.when(i > 0)
            def _(dg_ref=dg_ref, part=part):
                dg_ref[...] += part

            tg = dh * g_ref[...]
            acc = acc + rv * (tg - xh * jnp.mean(tg * xh, axis=1, keepdims=True))
        dx_ref[...] = acc

    row = pl.BlockSpec((tr, d), lambda i: (i, 0))
    vec = pl.BlockSpec((1, d), lambda i: (0, 0))
    outs = pl.pallas_call(
        body, name=name, grid=(t // tr,),
        in_specs=[row] * ng + [row, pl.BlockSpec((tr, 1), lambda i: (i, 0))] + [vec] * ng + [row],
        out_specs=[row] + [vec] * ng,
        out_shape=[jax.ShapeDtypeStruct((t, d), f32)] + [jax.ShapeDtypeStruct((1, d), f32)] * ng,
        compiler_params=_params("arbitrary"))(*dh_list, x, r, *gains, dx_in)
    return outs[0], outs[1:]


def sgu_gate_fwd(zu, zv, gv, wc, bt, *, name, tr=512):
    t, w = zu.shape
    tr = min(tr, t)
    groups = w // LANES

    def body(zu_ref, zv_ref, gv_ref, wc_ref, bt_ref, y_ref):
        vp = _gelu(zv_ref[...])
        rv = lax.rsqrt(jnp.mean(vp * vp, axis=1, keepdims=True) + EPS)
        vb = (vp * rv * gv_ref[...]).astype(bf16)
        for c in range(tr // CHUNK):
            rows = slice(c * CHUNK, (c + 1) * CHUNK)
            for g in range(groups):
                cols = slice(g * LANES, (g + 1) * LANES)
                sv = _dot(wc_ref[g], vb[rows, cols]) + bt_ref[:, g:g + 1]
                y_ref[rows, cols] = (_gelu(zu_ref[rows, cols]) * sv).astype(bf16)

    row = pl.BlockSpec((tr, w), lambda i: (i, 0))
    return pl.pallas_call(
        body, name=name, grid=(t // tr,),
        in_specs=[row, row, pl.BlockSpec((1, w), lambda i: (0, 0)),
                  pl.BlockSpec((groups, CHUNK, CHUNK), lambda i: (0, 0, 0)),
                  pl.BlockSpec((CHUNK, groups), lambda i: (0, 0))],
        out_specs=row, out_shape=jax.ShapeDtypeStruct((t, w), bf16),
        compiler_params=_params("parallel"))(zu, zv, gv, wc, bt)


def sgu_gate_bwd(zu, zv, dy, gv, wc, bt, *, name, tr=512):
    t, w = zu.shape
    tr = min(tr, t)
    groups = w // LANES
    nsteps = t // tr

    def body(zu_ref, zv_ref, dy_ref, gv_ref, wc_ref, bt_ref,
             dzu_ref, dzv_ref, dgv_ref, dws_ref, dbt_ref, dv_ref, bacc_ref):
        i = pl.program_id(0)

        @pl.when(i == 0)
        def _():
            dgv_ref[...] = jnp.zeros_like(dgv_ref)
            dws_ref[...] = jnp.zeros_like(dws_ref)
            bacc_ref[...] = jnp.zeros_like(bacc_ref)

        vp, vp_grad = _gelu_and_grad(zv_ref[...])
        rv = lax.rsqrt(jnp.mean(vp * vp, axis=1, keepdims=True) + EPS)
        vhat = vp * rv
        vb = (vhat * gv_ref[...]).astype(bf16)
        for c in range(tr // CHUNK):
            rows = slice(c * CHUNK, (c + 1) * CHUNK)
            for g in range(groups):
                cols = slice(g * LANES, (g + 1) * LANES)
                vblk = vb[rows, cols]
                sv = _dot(wc_ref[g], vblk) + bt_ref[:, g:g + 1]
                zub = zu_ref[rows, cols]
                dyb = dy_ref[rows, cols]
                ub, ub_grad = _gelu_and_grad(zub)
                dzu_ref[rows, cols] = (dyb * sv * ub_grad).astype(bf16)
                dsv = dyb * ub
                bacc_ref[:, cols] += dsv
                dsvb = dsv.astype(bf16)
                dv_ref[rows, cols] = _dot_tn(wc_ref[g], dsvb)
                dws_ref[g] += _dot_nt(dsvb, vblk)
        dv = dv_ref[...]
        dgv_ref[...] += jnp.sum(dv * vhat, axis=0, keepdims=True)
        tg = dv * gv_ref[...]
        dvp = rv * (tg - vhat * jnp.mean(tg * vhat, axis=1, keepdims=True))
        dzv_ref[...] = (dvp * vp_grad).astype(bf16)

        @pl.when(i == nsteps - 1)
        def _():
            tt = lax.broadcasted_iota(jnp.int32, (CHUNK, CHUNK), 0)
            ss = lax.broadcasted_iota(jnp.int32, (CHUNK, CHUNK), 1)
            for g in range(groups):
                dws_ref[g] = jnp.where(ss <= tt, dws_ref[g], 0.0)
                dbt_ref[:, g:g + 1] = jnp.sum(bacc_ref[:, g * LANES:(g + 1) * LANES], axis=1, keepdims=True)

    row = pl.BlockSpec((tr, w), lambda i: (i, 0))
    full3 = pl.BlockSpec((groups, CHUNK, CHUNK), lambda i: (0, 0, 0))
    return pl.pallas_call(
        body, name=name, grid=(nsteps,),
        in_specs=[row, row, row, pl.BlockSpec((1, w), lambda i: (0, 0)), full3,
                  pl.BlockSpec((CHUNK, groups), lambda i: (0, 0))],
        out_specs=[row, row, pl.BlockSpec((1, w), lambda i: (0, 0)), full3,
                   pl.BlockSpec((CHUNK, groups), lambda i: (0, 0))],
        out_shape=[jax.ShapeDtypeStruct((t, w), bf16), jax.ShapeDtypeStruct((t, w), bf16),
                   jax.ShapeDtypeStruct((1, w), f32), jax.ShapeDtypeStruct((groups, CHUNK, CHUNK), f32),
                   jax.ShapeDtypeStruct((CHUNK, groups), f32)],
        scratch_shapes=[pltpu.VMEM((tr, w), f32), pltpu.VMEM((CHUNK, w), f32)],
        compiler_params=_params("arbitrary"))(zu, zv, dy, gv, wc, bt)


HALO = 8


def _shift_down(v, halo, k, first):
    r = pltpu.roll(v, k, 0)
    hh = jnp.where(first, 0.0, pltpu.roll(halo, k, 0))
    rid = lax.broadcasted_iota(jnp.int32, (HALO, v.shape[1]), 0)
    head = jnp.where(rid < k, hh, r[0:HALO])
    if v.shape[0] == HALO:
        return head
    return jnp.concatenate([head, r[HALO:]], axis=0)


def _shift_up(v, halo, k, last):
    n = v.shape[0]
    r = pltpu.roll(v, n - k, 0)
    hh = jnp.where(last, 0.0, pltpu.roll(halo, HALO - k, 0))
    rid = lax.broadcasted_iota(jnp.int32, (HALO, v.shape[1]), 0)
    tail = jnp.where(rid >= HALO - k, hh, r[n - HALO:])
    return jnp.concatenate([r[:n - HALO], tail], axis=0)


def _conv(p, halo, w_ref, b_ref, first):
    return (w_ref[2:3, :] * p + w_ref[1:2, :] * _shift_down(p, halo, 1, first)
            + w_ref[0:1, :] * _shift_down(p, halo, 2, first) + b_ref[...])


BF16_ROWS = 16


def ffn_in_fused(h, w_in4, wg, wu, bg, bu, *, name):
    t, k = h.shape
    s_all, _, n_s = w_in4.shape
    half = s_all // 2
    tm = _row_tile(t, 2 * k * n_s * 2, k * 2 + 4 * n_s * 4 + n_s * 2)

    def body(h_ref, hh_ref, wg_ref, wu_ref, cg_ref, cu_ref, bg_ref, bu_ref, pg_ref, pu_ref, gate_ref, up_ref, a_ref):
        first = pl.program_id(1) == 0
        hv, hh = h_ref[...], hh_ref[...]
        outs = []
        for w_ref, c_ref, b_ref, p_ref, o_ref in ((wg_ref, cg_ref, bg_ref, pg_ref, gate_ref),
                                                  (wu_ref, cu_ref, bu_ref, pu_ref, up_ref)):
            p = _dot(hv, w_ref[0])
            p_ref[...] = p
            hu = _conv(p, _dot(hh, w_ref[0])[BF16_ROWS - HALO:], c_ref, b_ref, first)
            o_ref[...] = hu
            outs.append(hu)
        gate, up = outs
        a_ref[...] = (gate * jax.nn.sigmoid(gate) * up).astype(bf16)

    tile = pl.BlockSpec((tm, n_s), lambda j, i: (i, j))
    cw = pl.BlockSpec((3, n_s), lambda j, i: (0, j))
    cb = pl.BlockSpec((1, n_s), lambda j, i: (0, j))
    f = half * n_s
    return pl.pallas_call(
        body, name=name, grid=(half, t // tm),
        in_specs=[pl.BlockSpec((tm, k), lambda j, i: (i, 0)),
                  pl.BlockSpec((BF16_ROWS, k), lambda j, i: (jnp.maximum(i * (tm // BF16_ROWS) - 1, 0), 0)),
                  pl.BlockSpec((1, k, n_s), lambda j, i: (j, 0, 0)),
                  pl.BlockSpec((1, k, n_s), lambda j, i: (j + half, 0, 0)), cw, cw, cb, cb],
        out_specs=[tile] * 5,
        out_shape=[jax.ShapeDtypeStruct((t, f), f32)] * 4 + [jax.ShapeDtypeStruct((t, f), bf16)],
        compiler_params=_params("parallel", "parallel"))(h, h, w_in4, w_in4, wg, wu, bg, bu)


def _gate_grads(gate, up, dav):
    sg = jax.nn.sigmoid(gate)
    return dav * up * (sg * (1.0 + gate * (1.0 - sg))), dav * gate * sg


GATE_BWD_ROWS = 512


def ffn_gate_bwd(dy, w_out, pg, pu, gate, up, wg, wu, *, name):
    t, f = pg.shape
    d = dy.shape[1]
    tr = min(GATE_BWD_ROWS, t)
    nsteps = t // tr
    tc = f // 2

    def body(dy_ref, dyn_ref, w_ref, pg_ref, pu_ref, gate_ref, gaten_ref, up_ref, upn_ref, wg_ref, wu_ref,
             dg_ref, du_ref, sg_ref, su_ref):
        i = pl.program_id(1)
        last = i == nsteps - 1
        w = w_ref[0]
        da = _dot_nt(dy_ref[...].astype(bf16), w)
        da_n = _dot_nt(dyn_ref[...].astype(bf16), w)
        dgate, dup = _gate_grads(gate_ref[...], up_ref[...], da)
        dgate_n, dup_n = _gate_grads(gaten_ref[...], upn_ref[...], da_n)
        rid = lax.broadcasted_iota(jnp.int32, (8, tc), 0)
        for dd, d_n, c_ref, p_ref, o_ref, s_ref in ((dgate, dgate_n, wg_ref, pg_ref, dg_ref, sg_ref),
                                                    (dup, dup_n, wu_ref, pu_ref, du_ref, su_ref)):
            d1, d2 = _shift_up(dd, d_n, 1, last), _shift_up(dd, d_n, 2, last)
            o_ref[...] = (c_ref[2:3, :] * dd + c_ref[1:2, :] * d1 + c_ref[0:1, :] * d2).astype(bf16)
            p = p_ref[...]
            sums = [jnp.sum(d2 * p, axis=0, keepdims=True), jnp.sum(d1 * p, axis=0, keepdims=True),
                    jnp.sum(dd * p, axis=0, keepdims=True), jnp.sum(dd, axis=0, keepdims=True)]
            part = jnp.zeros((8, tc), f32)
            for k, sk in enumerate(sums):
                part = jnp.where(rid == k, sk, part)

            @pl.when(i == 0)
            def _(s_ref=s_ref, part=part):
                s_ref[...] = part

            @pl.when(i > 0)
            def _(s_ref=s_ref, part=part):
                s_ref[...] += part

    def nxt_rows(j, i):
        return (jnp.minimum((i + 1) * (tr // HALO), t // HALO - 1), j)

    tile = pl.BlockSpec((tr, tc), lambda j, i: (i, j))
    nxt = pl.BlockSpec((HALO, tc), nxt_rows)
    wspec = pl.BlockSpec((3, tc), lambda j, i: (0, j))
    stat = pl.BlockSpec((8, tc), lambda j, i: (0, j))
    return pl.pallas_call(
        body, name=name, grid=(2, nsteps),
        in_specs=[pl.BlockSpec((tr, d), lambda j, i: (i, 0)),
                  pl.BlockSpec((HALO, d), lambda j, i: (nxt_rows(j, i)[0], 0)),
                  pl.BlockSpec((1, tc, d), lambda j, i: (j, 0, 0)),
                  tile, tile, tile, nxt, tile, nxt, wspec, wspec],
        out_specs=[tile, tile, stat, stat],
        out_shape=[jax.ShapeDtypeStruct((t, f), bf16), jax.ShapeDtypeStruct((t, f), bf16),
                   jax.ShapeDtypeStruct((8, f), f32), jax.ShapeDtypeStruct((8, f), f32)],
        compiler_params=_params("parallel", "arbitrary"))(
            dy, dy, w_out.reshape(2, tc, d), pg, pu, gate, gate, up, up, wg, wu)


def _head_mean_matrix():
    i = lax.broadcasted_iota(jnp.int32, (LANES, LANES), 0) // HEAD_DIM
    j = lax.broadcasted_iota(jnp.int32, (LANES, LANES), 1) // HEAD_DIM
    return jnp.where(i == j, 1.0 / HEAD_DIM, 0.0).astype(bf16)


def _lane_half(shape):
    return (lax.broadcasted_iota(jnp.int32, shape, 1) % LANES) // HEAD_DIM


def q_norm_fwd(qp, g2, *, name, scale, tr=512):
    t, w = qp.shape
    tr = min(tr, t)

    def body(x_ref, g_ref, o_ref):
        bd = _head_mean_matrix()
        for cb in range(w // LANES):
            cols = slice(cb * LANES, (cb + 1) * LANES)
            xc = x_ref[:, cols]
            rh = lax.rsqrt(_dot_split(xc * xc, bd) + EPS)
            o_ref[:, cols] = (xc * rh * g_ref[...] * scale).astype(bf16)

    row = pl.BlockSpec((tr, w), lambda i: (i, 0))
    return pl.pallas_call(
        body, name=name, grid=(t // tr,), in_specs=[row, pl.BlockSpec((1, LANES), lambda i: (0, 0))],
        out_specs=row, out_shape=jax.ShapeDtypeStruct((t, w), bf16),
        compiler_params=_params("parallel"))(qp, g2)


def q_norm_bwd(dq, qp, g2, *, name, scale, tr=512):
    t, w = qp.shape
    tr = min(tr, t)

    def body(dq_ref, x_ref, g_ref, o_ref, dg_ref):
        i = pl.program_id(0)
        bd = _head_mean_matrix()
        acc = jnp.zeros((1, LANES), f32)
        for cb in range(w // LANES):
            cols = slice(cb * LANES, (cb + 1) * LANES)
            xc = x_ref[:, cols]
            rh = lax.rsqrt(_dot_split(xc * xc, bd) + EPS)
            xh = xc * rh
            dy = dq_ref[:, cols] * scale
            acc = acc + jnp.sum(dy * xh, axis=0, keepdims=True)
            tg = dy * g_ref[...]
            o_ref[:, cols] = (rh * (tg - xh * _dot_split(tg * xh, bd))).astype(bf16)

        @pl.when(i == 0)
        def _():
            dg_ref[...] = acc

        @pl.when(i > 0)
        def _():
            dg_ref[...] += acc

    row = pl.BlockSpec((tr, w), lambda i: (i, 0))
    vec = pl.BlockSpec((1, LANES), lambda i: (0, 0))
    return pl.pallas_call(
        body, name=name, grid=(t // tr,), in_specs=[row, row, vec], out_specs=[row, vec],
        out_shape=[jax.ShapeDtypeStruct((t, w), bf16), jax.ShapeDtypeStruct((1, LANES), f32)],
        compiler_params=_params("arbitrary"))(dq, qp, g2)


def kv_post_fwd(kv, g2, *, name, tr=512):
    t, w = kv.shape
    tr = min(tr, t)
    kw = w // 2

    def body(x_ref, g_ref, k_ref, v_ref):
        bd = _head_mean_matrix()
        half = _lane_half((tr, LANES))
        for cb in range(kw // LANES):
            xc = x_ref[:, cb * LANES:(cb + 1) * LANES]
            rh = lax.rsqrt(_dot_split(xc * xc, bd) + EPS)
            kn = xc * rh * g_ref[...]
            vc = x_ref[:, kw + cb * LANES:kw + (cb + 1) * LANES]
            for src, dst in ((kn, k_ref), (vc, v_ref)):
                sw = pltpu.roll(src, HEAD_DIM, 1)
                for hf in range(2):
                    blk = 2 * cb + hf
                    dst[:, blk * LANES:(blk + 1) * LANES] = jnp.where(half == hf, src, sw).astype(bf16)

    return pl.pallas_call(
        body, name=name, grid=(t // tr,),
        in_specs=[pl.BlockSpec((tr, w), lambda i: (i, 0)), pl.BlockSpec((1, LANES), lambda i: (0, 0))],
        out_specs=[pl.BlockSpec((tr, 2 * kw), lambda i: (i, 0))] * 2,
        out_shape=[jax.ShapeDtypeStruct((t, 2 * kw), bf16)] * 2,
        compiler_params=_params("parallel"))(kv, g2)


def kv_post_bwd(dk2, dv2, kv, g2, *, name, tr=512):
    t, w = kv.shape
    tr = min(tr, t)
    kw = w // 2

    def body(dk_ref, dv_ref, x_ref, g_ref, o_ref, dg_ref):
        i = pl.program_id(0)
        bd = _head_mean_matrix()
        half = _lane_half((tr, LANES))
        acc = jnp.zeros((1, LANES), f32)

        def fold(ref, cb):
            a = ref[:, (2 * cb) * LANES:(2 * cb + 1) * LANES]
            b = ref[:, (2 * cb + 1) * LANES:(2 * cb + 2) * LANES]
            return jnp.where(half == 0, a + pltpu.roll(a, HEAD_DIM, 1), b + pltpu.roll(b, HEAD_DIM, 1))

        for cb in range(kw // LANES):
            cols = slice(cb * LANES, (cb + 1) * LANES)
            xc = x_ref[:, cols]
            rh = lax.rsqrt(_dot_split(xc * xc, bd) + EPS)
            xh = xc * rh
            dy = fold(dk_ref, cb)
            acc = acc + jnp.sum(dy * xh, axis=0, keepdims=True)
            tg = dy * g_ref[...]
            o_ref[:, cols] = (rh * (tg - xh * _dot_split(tg * xh, bd))).astype(bf16)
            o_ref[:, kw + cb * LANES:kw + (cb + 1) * LANES] = fold(dv_ref, cb).astype(bf16)

        @pl.when(i == 0)
        def _():
            dg_ref[...] = acc

        @pl.when(i > 0)
        def _():
            dg_ref[...] += acc

    dup = pl.BlockSpec((tr, 2 * kw), lambda i: (i, 0))
    row = pl.BlockSpec((tr, w), lambda i: (i, 0))
    vec = pl.BlockSpec((1, LANES), lambda i: (0, 0))
    return pl.pallas_call(
        body, name=name, grid=(t // tr,), in_specs=[dup, dup, row, vec], out_specs=[row, vec],
        out_shape=[jax.ShapeDtypeStruct((t, w), bf16), jax.ShapeDtypeStruct((1, LANES), f32)],
        compiler_params=_params("arbitrary"))(dk2, dv2, kv, g2)


def _slope(h):
    return 2.0 ** (-8.0 * (h + 1) / N_Q_HEADS)


GROUP_ROWS = Q_PER_KV * CHUNK


def _band_mask(n):
    tq = lax.broadcasted_iota(jnp.int32, (GROUP_ROWS, 2 * CHUNK), 0) % CHUNK
    jk = lax.broadcasted_iota(jnp.int32, (GROUP_ROWS, 2 * CHUNK), 1)
    dist = tq + CHUNK - jk
    ok = (dist >= 0) & (dist < CHUNK) & jnp.logical_not((n == 0) & (jk < CHUNK))
    return dist.astype(f32), ok


def _band(ref, n, kh):
    p0 = pl.multiple_of(jnp.maximum(n - 1, 0) * CHUNK, CHUNK)
    c0 = pl.multiple_of(n * CHUNK, CHUNK)
    cols = slice(kh * LANES, (kh + 1) * LANES)
    return jnp.concatenate([ref[pl.ds(p0, CHUNK), cols], ref[pl.ds(c0, CHUNK), cols]], axis=0)


def _stack_heads(ref, kh, half):
    parts = []
    for cb in (2 * kh, 2 * kh + 1):
        xc = ref[:, cb * LANES:(cb + 1) * LANES].astype(f32)
        parts += [jnp.where(half == hf, xc, 0.0).astype(bf16) for hf in range(2)]
    return jnp.concatenate(parts, axis=0)


def _unstack_heads(x4, half):
    return (jnp.where(half == 0, x4[0:CHUNK], x4[CHUNK:2 * CHUNK]),
            jnp.where(half == 0, x4[2 * CHUNK:3 * CHUNK], x4[3 * CHUNK:]))


def _per_head_column(kh, values):
    grp = lax.broadcasted_iota(jnp.int32, (GROUP_ROWS, 1), 0) // CHUNK
    col = jnp.full((GROUP_ROWS, 1), values[0], f32)
    for g in range(1, Q_PER_KV):
        col = jnp.where(grp == g, values[g], col)
    return col


def _softmax_band(q4, kband, dist, ok, slope, sink):
    s = _dot_nt(q4, kband)
    s = jnp.where(ok, s - slope * dist, -jnp.inf)
    m = jnp.maximum(jnp.max(s, axis=1, keepdims=True), sink)
    e = jnp.exp(s - m)
    es = jnp.exp(sink - m)
    den = jnp.sum(e, axis=1, keepdims=True) + es
    return e / den, es / den


def attn_fwd(q, k2, v2, sinks, *, name):
    t, w = q.shape
    nb = t // CHUNK

    def body(sink_ref, q_ref, k_ref, v_ref, o_ref):
        n = pl.program_id(0)
        dist, ok = _band_mask(n)
        half = _lane_half((CHUNK, LANES))
        for kh in range(N_KV_HEADS):
            heads = [Q_PER_KV * kh + g for g in range(Q_PER_KV)]
            slope = _per_head_column(kh, [_slope(h) for h in heads])
            sink = _per_head_column(kh, [sink_ref[h] for h in heads])
            q4 = _stack_heads(q_ref, kh, half)
            p, _ = _softmax_band(q4, _band(k_ref, n, kh), dist, ok, slope, sink)
            o4 = _dot(p.astype(bf16), _band(v_ref, n, kh))
            lo, hi = _unstack_heads(o4, half)
            o_ref[:, (2 * kh) * LANES:(2 * kh + 1) * LANES] = lo.astype(bf16)
            o_ref[:, (2 * kh + 1) * LANES:(2 * kh + 2) * LANES] = hi.astype(bf16)

    full = pl.BlockSpec((t, k2.shape[1]), lambda n: (0, 0))
    return pl.pallas_call(
        body, name=name, grid=(nb,),
        in_specs=[pl.BlockSpec(memory_space=pltpu.SMEM), pl.BlockSpec((CHUNK, w), lambda n: (n, 0)), full, full],
        out_specs=pl.BlockSpec((CHUNK, w), lambda n: (n, 0)),
        out_shape=jax.ShapeDtypeStruct((t, w), bf16),
        compiler_params=_params("parallel"))(sinks, q, k2, v2)


def attn_bwd(q, k2, v2, do, sinks, *, name):
    t, w = q.shape
    nb = t // CHUNK
    kw = k2.shape[1]

    def body(sink_ref, q_ref, k_ref, v_ref, do_ref, dq_ref, dk_ref, dv_ref, ds_ref, kc_ref, vc_ref):
        n = pl.program_id(0)

        @pl.when(n == 0)
        def _():
            ds_ref[...] = jnp.zeros_like(ds_ref)
            kc_ref[...] = jnp.zeros_like(kc_ref)
            vc_ref[...] = jnp.zeros_like(vc_ref)
            dk_ref[...] = jnp.zeros_like(dk_ref)
            dv_ref[...] = jnp.zeros_like(dv_ref)

        @pl.when(n == nb)
        def _():
            dk_ref[...] = kc_ref[...]
            dv_ref[...] = vc_ref[...]

        @pl.when(n < nb)
        def _():
            dist, ok = _band_mask(n)
            half = _lane_half((CHUNK, LANES))
            lane = lax.broadcasted_iota(jnp.int32, (1, LANES), 1)
            sink_acc = jnp.zeros((1, LANES), f32)
            for kh in range(N_KV_HEADS):
                heads = [Q_PER_KV * kh + g for g in range(Q_PER_KV)]
                slope = _per_head_column(kh, [_slope(h) for h in heads])
                sink = _per_head_column(kh, [sink_ref[h] for h in heads])
                q4 = _stack_heads(q_ref, kh, half)
                do4 = _stack_heads(do_ref, kh, half)
                kband = _band(k_ref, n, kh)
                vband = _band(v_ref, n, kh)
                p, ps = _softmax_band(q4, kband, dist, ok, slope, sink)
                dp = _dot_nt(do4, vband)
                delta = jnp.sum(p * dp, axis=1, keepdims=True)
                dsb = (p * (dp - delta)).astype(bf16)
                sd = ps * delta
                for g, h in enumerate(heads):
                    part = jnp.sum(sd[g * CHUNK:(g + 1) * CHUNK], axis=0, keepdims=True)
                    sink_acc = sink_acc + jnp.where(lane == h, -part, 0.0)
                lo, hi = _unstack_heads(_dot(dsb, kband), half)
                dq_ref[:, (2 * kh) * LANES:(2 * kh + 1) * LANES] = lo
                dq_ref[:, (2 * kh + 1) * LANES:(2 * kh + 2) * LANES] = hi
                dkb = _dot_tn(dsb, q4)
                dvb = _dot_tn(p.astype(bf16), do4)
                cols = slice(kh * LANES, (kh + 1) * LANES)
                dk_ref[:, cols] = kc_ref[:, cols] + dkb[0:CHUNK]
                dv_ref[:, cols] = vc_ref[:, cols] + dvb[0:CHUNK]
                kc_ref[:, cols] = dkb[CHUNK:]
                vc_ref[:, cols] = dvb[CHUNK:]
            ds_ref[...] += sink_acc

    full = pl.BlockSpec((t, kw), lambda n: (0, 0))
    qblk = pl.BlockSpec((CHUNK, w), lambda n: (jnp.minimum(n, nb - 1), 0))
    kblk = pl.BlockSpec((CHUNK, kw), lambda n: (jnp.maximum(n - 1, 0), 0))
    return pl.pallas_call(
        body, name=name, grid=(nb + 1,),
        in_specs=[pl.BlockSpec(memory_space=pltpu.SMEM), qblk, full, full, qblk],
        out_specs=[qblk, kblk, kblk, pl.BlockSpec((1, LANES), lambda n: (0, 0))],
        out_shape=[jax.ShapeDtypeStruct((t, w), f32), jax.ShapeDtypeStruct((t, kw), f32),
                   jax.ShapeDtypeStruct((t, kw), f32), jax.ShapeDtypeStruct((1, LANES), f32)],
        scratch_shapes=[pltpu.VMEM((CHUNK, kw), f32), pltpu.VMEM((CHUNK, kw), f32)],
        compiler_params=_params("arbitrary"))(sinks, q, k2, v2, do)


def loss_head(y, target, *, name, tr=512):
    t, d = y.shape
    tr = min(tr, t)

    def body(y_ref, t_ref, dy_ref, s_ref):
        i = pl.program_id(0)
        e = y_ref[...] - t_ref[...]
        dy_ref[...] = e * (1.0 / d)
        part = jnp.sum(e * e, axis=0, keepdims=True)

        @pl.when(i == 0)
        def _():
            s_ref[...] = part

        @pl.when(i > 0)
        def _():
            s_ref[...] += part

    row = pl.BlockSpec((tr, d), lambda i: (i, 0))
    vec = pl.BlockSpec((1, d), lambda i: (0, 0))
    return pl.pallas_call(
        body, name=name, grid=(t // tr,), in_specs=[row, row], out_specs=[row, vec],
        out_shape=[jax.ShapeDtypeStruct((t, d), f32), jax.ShapeDtypeStruct((1, d), f32)],
        compiler_params=_params("arbitrary"))(y, target)


N_STEPS = 8


def _row_blocks(shape):
    if len(shape) == 2:
        r, c = shape
        return (r // N_STEPS, c), (lambda s: (s, 0))
    l, r, c = shape
    per = N_STEPS // l
    return (1, r // per, c), (lambda s: (s // per, s % per, 0))


CAST_STEPS = 4


def cast_into_slot(arrays, k_arr, *, name):
    in_specs, out_specs, out_shape, layers = [], [], [], []
    for a in arrays:
        r, c = a.shape[-2:]
        rb = r // CAST_STEPS
        if a.ndim == 2:
            in_specs.append(pl.BlockSpec((rb, c), lambda s, k: (s, 0)))
            layers.append(None)
        else:
            for l in range(a.shape[0]):
                in_specs.append(pl.BlockSpec((1, rb, c), lambda s, k, l=l: (l, s, 0)))
                layers.append(l)
        for _ in range(1 if a.ndim == 2 else a.shape[0]):
            out_specs.append(pl.BlockSpec((1, rb, c), lambda s, k: (k[0], s, 0)))
            out_shape.append(jax.ShapeDtypeStruct((N_SHARDS, r, c), bf16))
    n = len(in_specs)

    def body(k_ref, *refs):
        for i_ref, o_ref, l in zip(refs[:n], refs[n:], layers):
            o_ref[0] = (i_ref[...] if l is None else i_ref[0]).astype(bf16)

    args = []
    for a in arrays:
        args += [a] * (1 if a.ndim == 2 else a.shape[0])
    return pl.pallas_call(
        body, name=name,
        grid_spec=pltpu.PrefetchScalarGridSpec(num_scalar_prefetch=1, grid=(CAST_STEPS,),
                                               in_specs=in_specs, out_specs=out_specs),
        out_shape=out_shape, compiler_params=_params("parallel"))(k_arr, *args)


def adamw(ws, gs, ms, vs, *, name):
    n = len(ws)
    specs, g_specs, g_count = [], [], []
    for w, g_list in zip(ws, gs):
        blk, index = _row_blocks(w.shape)
        specs.append(pl.BlockSpec(blk, index))
        layers = len(g_list)
        per = N_STEPS // layers
        g_count.append(layers)
        for l in range(layers):
            g_specs.append(pl.BlockSpec(blk[-2:], lambda s, l=l, per=per: (jnp.where(s // per == l, s % per, 0), 0)))
    ng = len(g_specs)

    def body(*refs):
        s = pl.program_id(0)
        g_refs = refs[3 * n:3 * n + ng]
        outs = refs[3 * n + ng:]
        off = 0
        for i in range(n):
            w_ref, m_ref, v_ref = refs[i], refs[n + i], refs[2 * n + i]
            go_ref, d_ref, nm_ref, nv_ref = (outs[k * n + i] for k in range(4))
            layers = g_count[i]
            g = g_refs[off][...]
            for l in range(1, layers):
                g = jnp.where(s // (N_STEPS // layers) == l, g_refs[off + l][...], g)
            off += layers
            g = g.reshape(w_ref.shape)
            m = ADAM_B1 * m_ref[...] + (1.0 - ADAM_B1) * g
            v = ADAM_B2 * v_ref[...] + (1.0 - ADAM_B2) * (g * g)
            m_hat = m / ADAM_C1
            v_hat = v / ADAM_C2
            go_ref[...] = g
            d_ref[...] = -ADAM_LR * (m_hat / (jnp.sqrt(v_hat) + ADAM_EPS) + ADAM_WD * w_ref[...])
            nm_ref[...] = m
            nv_ref[...] = v

    outs = pl.pallas_call(
        body, name=name, grid=(N_STEPS,), in_specs=specs * 3 + g_specs, out_specs=specs * 4,
        out_shape=[jax.ShapeDtypeStruct(a.shape, f32) for a in ws] * 4,
        compiler_params=_params("parallel"))(*ws, *ms, *vs, *[g for g_list in gs for g in g_list])
    return [outs[k * n:(k + 1) * n] for k in range(4)]


def _adamw_update(w, g, m, v):
    m = ADAM_B1 * m + (1.0 - ADAM_B1) * g
    v = ADAM_B2 * v + (1.0 - ADAM_B2) * (g * g)
    m_hat = m / ADAM_C1
    v_hat = v / ADAM_C2
    return -ADAM_LR * (m_hat / (jnp.sqrt(v_hat) + ADAM_EPS) + ADAM_WD * w), m, v


def adamw_small(ws, gs, ms, vs, *, name):
    n = len(ws)

    def body(*refs):
        for i in range(n):
            w_ref, g_ref, m_ref, v_ref = (refs[k * n + i] for k in range(4))
            d_ref, nm_ref, nv_ref = (refs[(4 + k) * n + i] for k in range(3))
            d_ref[...], nm_ref[...], nv_ref[...] = _adamw_update(w_ref[...], g_ref[...], m_ref[...], v_ref[...])

    outs = pl.pallas_call(
        body, name=name, out_shape=[jax.ShapeDtypeStruct(a.shape, f32) for a in ws] * 3)(*ws, *gs, *ms, *vs)
    return outs[:n], outs[n:2 * n], outs[2 * n:]


def _place():
    return lax.axis_index("x"), lax.axis_index("y"), lax.axis_index("c")


def gather_shards(bufs, *, name, split):
    n = len(bufs)

    def body(*refs):
        bufs_ = refs[:n]
        isend, irecv, dsend, drecv = refs[2 * n:]
        x, y, c = _place()
        k = 2 * x + y
        peers = [(1 - x, y, c), (x, 1 - y, c), (1 - x, 1 - y, c)]
        peer_k = [2 * (1 - x) + y, 2 * x + (1 - y), 2 * (1 - x) + (1 - y)]

        def slab(a, q, h):
            if not split[a]:
                return bufs_[a].at[q]
            half = bufs_[a].shape[1] // 2
            return bufs_[a].at[q, pl.ds(pl.multiple_of(h * half, 16), half)]

        def ici(a, j, q):
            return pltpu.make_async_remote_copy(
                src_ref=slab(a, q, c), dst_ref=slab(a, q, c), send_sem=isend.at[3 * a + j], recv_sem=irecv.at[3 * a + j],
                device_id=peers[j], device_id_type=MESH)

        def d2d(a, j, h):
            return pltpu.make_async_remote_copy(
                src_ref=slab(a, peer_k[j], h), dst_ref=slab(a, peer_k[j], h), send_sem=dsend.at[3 * a + j],
                recv_sem=drecv.at[3 * a + j], device_id=(x, y, 1 - c), device_id_type=MESH)

        for a in range(n):
            for j in range(3):
                ici(a, j, k).start()
        for a in range(n):
            for j in range(3):
                ici(a, j, peer_k[j]).wait_recv()
                if split[a]:
                    d2d(a, j, c).start()
        for a in range(n):
            for j in range(3):
                if split[a]:
                    d2d(a, j, 1 - c).wait_recv()
        for a in range(n):
            for j in range(3):
                ici(a, j, k).wait_send()
                if split[a]:
                    d2d(a, j, c).wait_send()

    return pl.pallas_call(
        body, name=name, in_specs=[ANY] * n, out_specs=[ANY] * n,
        out_shape=[jax.ShapeDtypeStruct(b.shape, b.dtype) for b in bufs],
        input_output_aliases={i: i for i in range(n)},
        scratch_shapes=[pltpu.SemaphoreType.DMA((3 * n,))] * 4)(*bufs)


HBM = pl.BlockSpec(memory_space=pltpu.HBM)
SEM = pl.BlockSpec(memory_space=pltpu.SEMAPHORE)
DATAFLOW = pltpu.SideEffectType.DATAFLOW_SIDE_EFFECTING


def _chip_peers():
    x, y, c = _place()
    return 2 * x + y, [(1 - x, y, c), (x, 1 - y, c), (1 - x, 1 - y, c)], [2 * (1 - x) + y, 2 * x + (1 - y), 2 * (1 - x) + (1 - y)]


def gather_start(bufs, groups, after, *, name):
    n = len(bufs)
    ng = len(groups)

    def body(*refs):
        ins = refs[:n]
        sends, recvs = refs[2 * n + 1:2 * n + 1 + ng], refs[2 * n + 1 + ng:2 * n + 1 + 2 * ng]
        token = refs[-1]
        k, peers, _ = _chip_peers()
        for gi, grp in enumerate(groups):
            for pos, a in enumerate(grp):
                for j in range(3):
                    pltpu.make_async_remote_copy(
                        src_ref=ins[a].at[k], dst_ref=ins[a].at[k], send_sem=sends[gi].at[3 * pos + j],
                        recv_sem=recvs[gi].at[3 * pos + j], device_id=peers[j], device_id_type=MESH).start()
        token[...] = jnp.zeros_like(token)

    sems = [pltpu.SemaphoreType.DMA((3 * len(grp),)) for grp in groups]
    outs = pl.pallas_call(
        body, name=name, in_specs=[HBM] * n + [ANY],
        out_specs=[HBM] * n + [SEM] * (2 * ng) + [pl.BlockSpec(memory_space=pltpu.VMEM)],
        out_shape=[pltpu.HBM(b.shape, b.dtype) for b in bufs] + sems + sems + [jax.ShapeDtypeStruct((8, LANES), f32)],
        input_output_aliases={i: i for i in range(n)},
        compiler_params=pltpu.CompilerParams(has_side_effects=DATAFLOW))(
            *[pltpu.with_memory_space_constraint(b, pltpu.HBM) for b in bufs], after)
    return outs[:n], outs[n:n + ng], outs[n + ng:n + 2 * ng], outs[-1]


def gather_wait(bufs, send_sems, recv_sems, after, *, name):
    n = len(bufs)

    def body(*refs):
        ins = refs[:n]
        send, recv = refs[n], refs[n + 1]
        k, peers, peer_k = _chip_peers()
        for a in range(n):
            for j in range(3):
                copy = pltpu.make_async_remote_copy(
                    src_ref=ins[a].at[k], dst_ref=ins[a].at[peer_k[j]], send_sem=send.at[3 * a + j],
                    recv_sem=recv.at[3 * a + j], device_id=peers[j], device_id_type=MESH)
                copy.wait_send()
                copy.wait_recv()

    return pl.pallas_call(
        body, name=name, in_specs=[HBM] * n + [SEM, SEM, ANY], out_specs=[HBM] * n,
        out_shape=[pltpu.HBM(b.shape, b.dtype) for b in bufs],
        input_output_aliases={i: i for i in range(n)},
        compiler_params=pltpu.CompilerParams(has_side_effects=DATAFLOW))(*bufs, send_sems, recv_sems, after)


def sibling_exchange(arrays, *, name):
    n = len(arrays)

    def body(*refs):
        ins, outs = refs[:n], refs[n:2 * n]
        send, recv = refs[2 * n:]
        x, y, c = _place()

        def copy(a):
            return pltpu.make_async_remote_copy(
                src_ref=ins[a], dst_ref=outs[a], send_sem=send.at[a], recv_sem=recv.at[a],
                device_id=(x, y, 1 - c), device_id_type=MESH)

        for a in range(n):
            copy(a).start()
        for a in range(n):
            copy(a).wait_recv()
        for a in range(n):
            copy(a).wait_send()

    return pl.pallas_call(
        body, name=name, in_specs=[ANY] * n, out_specs=[ANY] * n,
        out_shape=[jax.ShapeDtypeStruct(a.shape, a.dtype) for a in arrays],
        scratch_shapes=[pltpu.SemaphoreType.DMA((n,)), pltpu.SemaphoreType.DMA((n,))])(*arrays)


ALL_MASKS = [(mx, my, mc) for mx in (0, 1) for my in (0, 1) for mc in (0, 1)][1:]


def _scatter_copies(srcs, lands, ev, send, recv, esend, erecv):
    x, y, c = _place()
    me = 4 * x + 2 * y + c
    k, peers, peer_k = _chip_peers()
    out = []
    for a in range(len(srcs)):
        for j in range(3):
            out.append(pltpu.make_async_remote_copy(
                src_ref=srcs[a].at[peer_k[j]], dst_ref=lands[a].at[j], send_sem=send.at[3 * a + j],
                recv_sem=recv.at[3 * a + j], device_id=peers[j], device_id_type=MESH))
    start_ev, wait_ev = [], []
    if ev is not None:
        for j, (mx, my, mc) in enumerate(ALL_MASKS):
            peer = (x ^ mx, y ^ my, c ^ mc)
            start_ev.append(pltpu.make_async_remote_copy(
                src_ref=ev.at[me], dst_ref=ev.at[me], send_sem=esend.at[j], recv_sem=erecv.at[j],
                device_id=peer, device_id_type=MESH))
            wait_ev.append(pltpu.make_async_remote_copy(
                src_ref=ev.at[me], dst_ref=ev.at[me ^ (4 * mx + 2 * my + mc)], send_sem=esend.at[j],
                recv_sem=erecv.at[j], device_id=peer, device_id_type=MESH))
    return out, start_ev, wait_ev


def chip_scatter_start(arrays, everyone, after, *, name):
    n = len(arrays)
    ne = 0 if everyone is None else 1
    lands = [pltpu.with_memory_space_constraint(lax.empty((3,) + a.shape[1:], a.dtype), pltpu.HBM) for a in arrays]

    def body(*refs):
        srcs, lands_ = refs[:n], refs[n:2 * n]
        ev = refs[2 * n] if ne else None
        sems = refs[2 * n + ne + 1 + 2 * n + ne:-1]
        send, recv = sems[0], sems[1]
        esend, erecv = (sems[2], sems[3]) if ne else (None, None)
        copies, start_ev, _ = _scatter_copies(srcs, lands_, ev, send, recv, esend, erecv)
        for cp in start_ev + copies:
            cp.start()
        refs[-1][...] = jnp.zeros_like(refs[-1])

    sem_shapes = [pltpu.SemaphoreType.DMA((3 * n,))] * 2 + [pltpu.SemaphoreType.DMA((7,))] * (2 * ne)
    bufs = list(arrays) + lands + ([everyone] if ne else [])
    outs = pl.pallas_call(
        body, name=name, in_specs=[HBM] * len(bufs) + [ANY],
        out_specs=[HBM] * len(bufs) + [SEM] * len(sem_shapes) + [pl.BlockSpec(memory_space=pltpu.VMEM)],
        out_shape=[pltpu.HBM(b.shape, b.dtype) for b in bufs] + sem_shapes + [jax.ShapeDtypeStruct((8, LANES), f32)],
        input_output_aliases={i: i for i in range(len(bufs))},
        compiler_params=pltpu.CompilerParams(has_side_effects=DATAFLOW))(
            *[pltpu.with_memory_space_constraint(b, pltpu.HBM) for b in bufs], after)
    return (n, ne, outs[:-1]), outs[-1]


def chip_scatter_wait(state, after, *, name):
    n, ne, held = state
    nb = 2 * n + ne
    bufs, sems = held[:nb], held[nb:]

    def body(*refs):
        srcs, lands_ = refs[:n], refs[n:2 * n]
        ev = refs[2 * n] if ne else None
        sems_ = refs[nb:nb + len(sems)]
        esend, erecv = (sems_[2], sems_[3]) if ne else (None, None)
        copies, _, wait_ev = _scatter_copies(srcs, lands_, ev, sems_[0], sems_[1], esend, erecv)
        for cp in wait_ev + copies:
            cp.wait_send()
            cp.wait_recv()

    outs = pl.pallas_call(
        body, name=name, in_specs=[HBM] * nb + [SEM] * len(sems) + [ANY], out_specs=[HBM] * nb,
        out_shape=[pltpu.HBM(b.shape, b.dtype) for b in bufs],
        input_output_aliases={i: i for i in range(nb)},
        compiler_params=pltpu.CompilerParams(has_side_effects=DATAFLOW))(*bufs, *sems, after)
    return outs[n:2 * n], (outs[2 * n] if ne else None)


def sibling_merge(bufs, *, name):
    n = len(bufs)

    def body(*refs):
        bufs_ = refs[:n]
        send, recv = refs[2 * n:]
        x, y, c = _place()

        def copy(u, h):
            return pltpu.make_async_remote_copy(
                src_ref=bufs_[u].at[h], dst_ref=bufs_[u].at[h], send_sem=send.at[u], recv_sem=recv.at[u],
                device_id=(x, y, 1 - c), device_id_type=MESH)

        for u in range(n):
            copy(u, c).start()
        for u in range(n):
            copy(u, 1 - c).wait_recv()
        for u in range(n):
            copy(u, c).wait_send()

    return pl.pallas_call(
        body, name=name, in_specs=[ANY] * n, out_specs=[ANY] * n,
        out_shape=[jax.ShapeDtypeStruct(b.shape, b.dtype) for b in bufs],
        input_output_aliases={i: i for i in range(n)},
        scratch_shapes=[pltpu.SemaphoreType.DMA((n,)), pltpu.SemaphoreType.DMA((n,))])(*bufs)


def sum_leading(a, *, name):
    n, r, c = a.shape

    def body(a_ref, o_ref):
        acc = a_ref[0]
        for i in range(1, n):
            acc = acc + a_ref[i]
        o_ref[...] = acc

    rb = r // 2 if r % 16 == 0 else r
    return pl.pallas_call(
        body, name=name, grid=(r // rb,), in_specs=[pl.BlockSpec((n, rb, c), lambda i: (0, i, 0))],
        out_specs=pl.BlockSpec((rb, c), lambda i: (i, 0)), out_shape=jax.ShapeDtypeStruct((r, c), f32),
        compiler_params=_params("parallel"))(a)


def _half_rows(shape):
    return shape[1] // 2 // 2


def rs_cast_other_half(grads, c_arr, *, name):
    n = len(grads)

    def body(c_ref, *refs):
        for i_ref, o_ref in zip(refs[:n], refs[n:]):
            o_ref[...] = i_ref[...].astype(bf16)

    in_specs = [pl.BlockSpec((1, _half_rows(g.shape), g.shape[2]), lambda s, r, c_ref: (s, (1 - c_ref[0]) * 2 + r, 0))
                for g in grads]
    out_specs = [pl.BlockSpec((1, _half_rows(g.shape), g.shape[2]), lambda s, r, c_ref: (s, r, 0)) for g in grads]
    return pl.pallas_call(
        body, name=name,
        grid_spec=pltpu.PrefetchScalarGridSpec(num_scalar_prefetch=1, grid=(N_SHARDS, 2),
                                               in_specs=in_specs, out_specs=out_specs),
        out_shape=[jax.ShapeDtypeStruct((N_SHARDS, g.shape[1] // 2, g.shape[2]), bf16) for g in grads],
        compiler_params=_params("parallel", "parallel"))(c_arr, *grads)


def rs_add_sibling(grads, recvd, ck_arr, *, name):
    n = len(grads)

    def body(ck_ref, *refs):
        s = pl.program_id(1)
        for u in range(n):
            g_ref, r_ref = refs[u], refs[n + u]
            qb_ref, own_ref = refs[2 * n + u], refs[3 * n + u]
            q = g_ref[0] + r_ref[0].astype(f32)
            qb_ref[0] = q.astype(bf16)

            @pl.when(s == ck_ref[1])
            def _(own_ref=own_ref, q=q):
                own_ref[...] = q

    in_specs = [pl.BlockSpec((1, _half_rows(g.shape), g.shape[2]), lambda r, s, ck: (s, ck[0] * 2 + r, 0)) for g in grads]
    in_specs += [pl.BlockSpec((1, _half_rows(g.shape), g.shape[2]), lambda r, s, ck: (s, r, 0)) for g in grads]
    out_specs = [pl.BlockSpec((1, _half_rows(g.shape), g.shape[2]), lambda r, s, ck: (s, r, 0)) for g in grads]
    out_specs += [pl.BlockSpec((_half_rows(g.shape), g.shape[2]), lambda r, s, ck: (r, 0)) for g in grads]
    outs = pl.pallas_call(
        body, name=name,
        grid_spec=pltpu.PrefetchScalarGridSpec(num_scalar_prefetch=1, grid=(2, N_SHARDS),
                                               in_specs=in_specs, out_specs=out_specs),
        out_shape=[jax.ShapeDtypeStruct((N_SHARDS, g.shape[1] // 2, g.shape[2]), bf16) for g in grads]
        + [jax.ShapeDtypeStruct((g.shape[1] // 2, g.shape[2]), f32) for g in grads],
        compiler_params=_params("parallel", "arbitrary"))(ck_arr, *grads, *recvd)
    return outs[:n], outs[n:]


def rs_sum_chips(owns, recvd, ck_arr, *, name):
    n = len(owns)

    def body(ck_ref, *refs):
        for u in range(n):
            own_ref, r_ref, o_ref = refs[u], refs[n + u], refs[2 * n + u]
            o_ref[0] = ((own_ref[...] + r_ref[0].astype(f32)) + r_ref[1].astype(f32)) + r_ref[2].astype(f32)

    in_specs = [pl.BlockSpec((o.shape[0] // 2, o.shape[1]), lambda r, ck: (r, 0)) for o in owns]
    in_specs += [pl.BlockSpec((3, o.shape[0] // 2, o.shape[1]), lambda r, ck: (0, r, 0)) for o in owns]
    out_specs = [pl.BlockSpec((1, o.shape[0] // 2, o.shape[1]), lambda r, ck: (ck[0], r, 0)) for o in owns]
    return pl.pallas_call(
        body, name=name,
        grid_spec=pltpu.PrefetchScalarGridSpec(num_scalar_prefetch=1, grid=(2,), in_specs=in_specs, out_specs=out_specs),
        out_shape=[jax.ShapeDtypeStruct((2,) + o.shape, f32) for o in owns],
        compiler_params=_params("parallel"))(ck_arr, *owns, *recvd)


SMALL = ("a_norm", "a_v_norm", "a_w_s", "a_b_s", "f_norm", "f_conv_w", "f_conv_b", "kv_norm", "k_norm",
         "b_norm", "b_q_norm", "b_sinks")
BIG = ("a_w_in", "a_w_out", "f_w_in", "f_w_out", "w_kv", "b_w_q", "b_w_o")
PACK_COLS = 1024
PACK_ROWS = 8 * N_STEPS


def _pack(parts, rows=PACK_ROWS):
    flat = jnp.concatenate([p.reshape(-1).astype(f32) for p in parts])
    pad = (-flat.shape[0]) % (rows * PACK_COLS)
    return jnp.pad(flat, (0, pad)).reshape(-1, PACK_COLS)


def _unpack(packed, shapes):
    flat = packed.reshape(-1)
    out, off = [], 0
    for s in shapes:
        size = math.prod(s)
        out.append(flat[off:off + size].reshape(s))
        off += size
    return out


def _ffn_fwd(x, g, h, r, w_in4, conv_w, conv_b, f, tag):
    wg, wu = conv_w[:, :f], conv_w[:, f:]
    bg, bu = conv_b[None, :f], conv_b[None, f:]
    pg, pu, gate, up, a = ffn_in_fused(h, w_in4, wg, wu, bg, bu, name=f"ffn{tag}_in")
    return a, (x, g, h, r, pg, pu, gate, up, a, wg, wu)


def _ffn_bwd(dy, saved, w_in4, w_out, tag):
    x, g, h, r, pg, pu, gate, up, a, wg, wu = saved
    f = w_out.shape[0]
    d_w_out = mm_tn(a, [dy], name=f"ffn{tag}_dwout", n_s=w_out.shape[1], tki=f // 2)
    dpg, dpu, sg, su = ffn_gate_bwd(dy, w_out, pg, pu, gate, up, wg, wu, name=f"ffn{tag}_dgate")
    d_w_in = mm_tn(h, [dpg, dpu], name=f"ffn{tag}_dwin", n_s=w_in4.shape[2])
    dx, dg = mm_nt_rms_bwd([dpg, dpu], w_in4, x, r, g, dy, name=f"ffn{tag}_dh")
    d_conv_w = jnp.concatenate([sg[0:3], su[0:3]], axis=1)
    d_conv_b = jnp.concatenate([sg[3], su[3]], axis=0)
    return dx, dg, d_w_in, d_conv_w, d_conv_b, d_w_out


def _rs_front(units, c_arr, tag):
    other_bf = rs_cast_other_half(units, c_arr, name=f"rs_cast{tag}")
    from_sib = sibling_exchange(list(other_bf), name=f"rs_sibling{tag}")
    return rs_add_sibling(units, from_sib, c_arr, name=f"rs_add{tag}")


def _rs_back(own, from_chips, c_arr, tag):
    halves = rs_sum_chips(list(own), list(from_chips), c_arr, name=f"rs_sum{tag}")
    return [m.reshape(-1, m.shape[2]) for m in sibling_merge(list(halves), name=f"rs_merge{tag}")]


def kernel(x, a_norm, a_w_in, a_v_norm, a_w_s, a_b_s, a_w_out, f_norm, f_w_in, f_conv_w, f_conv_b, f_w_out, kv_norm, w_kv, k_norm, b_norm, b_w_q, b_q_norm, b_sinks, b_w_o, loss_target, m_a_norm, m_a_w_in, m_a_v_norm, m_a_w_s, m_a_b_s, m_a_w_out, m_f_norm, m_f_w_in, m_f_conv_w, m_f_conv_b, m_f_w_out, m_kv_norm, m_w_kv, m_k_norm, m_b_norm, m_b_w_q, m_b_q_norm, m_b_sinks, m_b_w_o, v_a_norm, v_a_w_in, v_a_v_norm, v_a_w_s, v_a_b_s, v_a_w_out, v_f_norm, v_f_w_in, v_f_conv_w, v_f_conv_b, v_f_w_out, v_kv_norm, v_w_kv, v_k_norm, v_b_norm, v_b_w_q, v_b_q_norm, v_b_sinks, v_b_w_o):
    args = dict(locals())
    weights = {n: args[n] for n in SMALL + BIG}
    moms = {n: args["m_" + n] for n in SMALL + BIG}
    vars_ = {n: args["v_" + n] for n in SMALL + BIG}
    t, d = x.shape[1], x.shape[2]
    xi, yi, ci = _place()
    chip = 2 * xi + yi

    big_local = [a_w_in[0], a_w_out[0], f_w_in, f_w_out, w_kv, b_w_q[0], b_w_o[0]]
    c_arr = jnp.stack([ci, chip]).astype(jnp.int32)
    k_arr = jnp.stack([chip]).astype(jnp.int32)
    b_ain, b_aout, b_fin0, b_fin1, b_fout0, b_fout1, b_kv, b_q, b_o = cast_into_slot(big_local, k_arr, name="cast_weights")
    small_cols = _pack([a_norm, a_v_norm, f_conv_w], rows=8)
    b_small = lax.dynamic_update_slice(jnp.zeros((N_SHARDS,) + small_cols.shape, f32), small_cols[None], (chip, 0, 0))
    (g_small,) = gather_shards([b_small], name="gather_small", split=[False])
    later, send_sems, recv_sems, token = gather_start([b_ain, b_aout, b_fin0, b_fout0, b_kv, b_q, b_o, b_fin1, b_fout1],
                                                      [[0], [1], [2], [3, 4, 5, 6], [7, 8]], g_small, name="gather_start")
    ns_cols = a_norm.shape[1]
    nf_cols = f_conv_w.shape[2]
    parts = [_unpack(g_small[k], [a_norm.shape, a_v_norm.shape, f_conv_w.shape]) for k in range(N_SHARDS)]
    a_norm_f = jnp.concatenate([p[0] for p in parts], axis=1) + token[0, 0]
    a_v_norm_f = jnp.concatenate([p[1] for p in parts], axis=1)
    conv_w_f = jnp.concatenate([p[2] for p in parts], axis=2)

    x0 = x[0]
    tril = jnp.tril(jnp.ones((CHUNK, CHUNK), dtype=bool))
    wc = jnp.where(tril[None], a_w_s[0], 0.0).astype(bf16)
    bt = a_b_s[0].T
    kg2 = jnp.tile(k_norm, 2)[None]
    qg2 = jnp.tile(b_q_norm[0], 2)[None]

    (h_a,), r_a = rms_fwd(x0, [a_norm_f], name="a_norm")
    (w_a_in,) = gather_wait(later[0:1], send_sems[0], recv_sems[0], h_a, name="gather_wait_a_in")
    zu = mm_nn(h_a, w_a_in, name="a_in_u", s0=0, ns=2)
    zv = mm_nn(h_a, w_a_in, name="a_in_v", s0=2, ns=2)
    y_a = sgu_gate_fwd(zu, zv, a_v_norm_f, wc, bt, name="a_gate")
    (g_a_w_out,) = gather_wait(later[1:2], send_sems[1], recv_sems[1], y_a, name="gather_wait_a_out")
    w_a_out = g_a_w_out.reshape(1, -1, d)
    f = f_w_out.shape[1] * N_SHARDS
    x1, (h_f0,), r_f0 = mm_residual(y_a, w_a_out[0], x0, name="a_out", gains=[f_norm[0:1]])
    (g_fin0,) = gather_wait(later[2:3], send_sems[2], recv_sems[2], x1, name="gather_wait_0")
    w_f_in = [g_fin0, None]
    a0, ffn0 = _ffn_fwd(x1, f_norm[0:1], h_f0, r_f0, w_f_in[0], conv_w_f[0], f_conv_b[0], f, "0")
    g_fout0, g_w_kv, g_b_w_q, g_b_w_o = gather_wait(later[3:7], send_sems[3], recv_sems[3], a0, name="gather_wait_1")
    w_f_out = [g_fout0.reshape(-1, d), None]
    w_kv_f = g_w_kv.reshape(1, d, -1)
    w_q_f = g_b_w_q.reshape(1, d, -1)
    w_o_f = g_b_w_o.reshape(1, -1, d)
    x2, (h_k, h_q), r_b = mm_residual(a0, w_f_out[0], x1, name="ffn0_out", gains=[kv_norm[None], b_norm])
    kv = mm_nn(h_k, w_kv_f, name="kv_proj")
    k2, v2 = kv_post_fwd(kv, kg2, name="kv_post")
    qp = mm_nn(h_q, w_q_f, name="q_proj")
    qn = q_norm_fwd(qp, qg2, name="q_norm", scale=HEAD_DIM ** -0.5)
    o = attn_fwd(qn, k2, v2, b_sinks[0], name="attn")
    x3, (h_f1,), r_f1 = mm_residual(o, w_o_f[0], x2, name="o_proj", gains=[f_norm[1:2]])
    g_fin1, g_fout1 = gather_wait(later[7:9], send_sems[4], recv_sems[4], x3, name="gather_wait_2")
    w_f_in[1] = g_fin1
    w_f_out[1] = g_fout1.reshape(-1, d)
    a1, ffn1 = _ffn_fwd(x3, f_norm[1:2], h_f1, r_f1, w_f_in[1], conv_w_f[1], f_conv_b[1], f, "1")
    dx4, sq = mm_residual(a1, w_f_out[1], x3, name="ffn1_out", target=loss_target[0])
    loss_part = (0.5 * jnp.sum(sq) / d).reshape(1)

    dx3, d_fn1, d_fwin1, d_cw1, d_cb1, d_fwout1 = _ffn_bwd(dx4, ffn1, w_f_in[1], w_f_out[1], "1")
    do = mm_nt([dx3], w_o_f, name="o_proj_dx")
    d_w_o = mm_tn(o, [dx3], name="o_proj_dw", n_s=d)
    dqn, dk2, dv2, dsink = attn_bwd(qn, k2, v2, do, b_sinks[0], name="attn_bwd")
    dqp, dqg = q_norm_bwd(dqn, qp, qg2, name="q_norm_bwd", scale=HEAD_DIM ** -0.5)
    dkv, dkg = kv_post_bwd(dk2, dv2, kv, kg2, name="kv_post_bwd")
    d_w_q = mm_tn(h_q, [dqp], name="q_proj_dw", n_s=w_q_f.shape[2])
    d_w_kv = mm_tn(h_k, [dkv], name="kv_proj_dw", n_s=w_kv_f.shape[2])
    dh_k = mm_nt([dkv], w_kv_f, name="kv_proj_dx")
    dx2, d_bn, d_kvn = mm_nt_rms_bwd([dqp], w_q_f, x2, r_b, b_norm, dx3, name="q_proj_dx", extra=(dh_k, kv_norm[None]))
    sh = N_SHARDS
    units1 = [d_fwin1, d_fwout1.reshape(sh, -1, d), d_w_kv.reshape(sh, -1, d_w_kv.shape[2]),
              d_w_q.reshape(sh, -1, d_w_q.shape[2]), d_w_o.reshape(sh, -1, d)]
    chip_bf1, own1 = _rs_front(units1, c_arr, "1")
    scatter1, token1 = chip_scatter_start(list(chip_bf1), None, dx2, name="rs_chips_start1")
    ffn0 = ffn0[:9] + (ffn0[9] + token1[0, 0],) + ffn0[10:]
    dx1, d_fn0, d_fwin0, d_cw0, d_cb0, d_fwout0 = _ffn_bwd(dx2, ffn0, w_f_in[0], w_f_out[0], "0")
    chip_bf2, own2 = _rs_front([d_fwin0, d_fwout0.reshape(sh, -1, d)], c_arr, "2")
    scatter2, token2 = chip_scatter_start(list(chip_bf2), None, dx1, name="rs_chips_start2")
    a_v_norm_f = a_v_norm_f + token2[0, 0]
    dy_a = mm_nt([dx1], w_a_out, name="a_out_dx")
    d_w_aout = mm_tn(y_a, [dx1], name="a_out_dw", n_s=d)
    dzu, dzv, d_avn, d_ws, d_bt = sgu_gate_bwd(zu, zv, dy_a, a_v_norm_f, wc, bt, name="a_gate_bwd")
    d_w_ain = mm_tn(h_a, [dzu, dzv], name="a_in_dw", n_s=w_a_in.shape[2])
    dx0, d_an = mm_nt_rms_bwd([dzu, dzv], w_a_in, x0, r_a, a_norm_f, dx1, name="a_in_dx")
    grad_x = dx0[None]

    chip_bf3, own3 = _rs_front([d_w_ain, d_w_aout.reshape(sh, -1, d)], c_arr, "3")
    d_fn = jnp.concatenate([d_fn0, d_fn1], axis=0)
    d_cw = jnp.stack([d_cw0, d_cw1])
    d_cb = jnp.stack([d_cb0, d_cb1])
    d_kg = (dkg[0, :HEAD_DIM] + dkg[0, HEAD_DIM:])
    d_qg = (dqg[0, :HEAD_DIM] + dqg[0, HEAD_DIM:])[None]
    small_full = [d_an, d_avn, d_ws[None], d_bt.T[None], d_fn, d_cw, d_cb, d_kvn[0], d_kg, d_bn, d_qg,
                  dsink[:, :N_Q_HEADS], loss_part]
    packed = _pack(small_full)
    me = 4 * xi + 2 * yi + ci
    everyone = lax.dynamic_update_slice(lax.empty((N_DEV,) + packed.shape, f32), packed[None], (me, 0, 0))
    scatter3, _ = chip_scatter_start(list(chip_bf3), everyone, chip_bf3[0], name="rs_chips_start3")
    from_chips1, _ = chip_scatter_wait(scatter1, scatter3[2][0], name="rs_chips_wait1")
    fin1, fout1, gkv, gq, go = _rs_back(own1, from_chips1, c_arr, "1")
    from_chips2, _ = chip_scatter_wait(scatter2, fin1, name="rs_chips_wait2")
    fin0, fout0 = _rs_back(own2, from_chips2, c_arr, "2")
    late = ("f_w_in", "f_w_out", "w_kv", "b_w_q", "b_w_o")
    res_late = adamw([weights[n] for n in late], [[fin0, fin1], [fout0, fout1], [gkv], [gq], [go]],
                     [moms[n] for n in late], [vars_[n] for n in late], name="adamw_late")
    from_chips3, from_all = chip_scatter_wait(scatter3, res_late[1][2], name="rs_chips_wait3")
    ain, aout = _rs_back(own3, from_chips3, c_arr, "3")
    first = ("a_w_in", "a_w_out")
    res_first = adamw([weights[n] for n in first], [[ain], [aout]], [moms[n] for n in first],
                      [vars_[n] for n in first], name="adamw_first")
    big = {n: tuple(r[i] for r in res_late) for i, n in enumerate(late)}
    big.update({n: tuple(r[i] for r in res_first) for i, n in enumerate(first)})

    full_shapes = [g.shape for g in small_full]
    small_g = _unpack(sum_leading(from_all, name="small_sum"), full_shapes)
    loss = small_g.pop()[0]
    small_g[0] = lax.dynamic_slice_in_dim(small_g[0], chip * ns_cols, ns_cols, axis=1)
    small_g[1] = lax.dynamic_slice_in_dim(small_g[1], chip * ns_cols, ns_cols, axis=1)
    small_g[5] = lax.dynamic_slice_in_dim(small_g[5], chip * nf_cols, nf_cols, axis=2)
    small_shapes = [weights[n].shape for n in SMALL]
    small_g = [g.reshape(s) for g, s in zip(small_g, small_shapes)]
    flat2 = [(math.prod(s[:-1]), s[-1]) for s in small_shapes]
    small_d, small_m, small_v = adamw_small(
        *[[a.reshape(s2) for a, s2 in zip(group, flat2)]
          for group in ([weights[n] for n in SMALL], small_g, [moms[n] for n in SMALL], [vars_[n] for n in SMALL])],
        name="adamw_small")
    small_d, small_m, small_v = ([a.reshape(s) for a, s in zip(group, small_shapes)]
                                 for group in (small_d, small_m, small_v))

    out = {}
    for i, n in enumerate(SMALL):
        out[n] = (small_g[i], small_d[i], small_m[i], small_v[i])
    out.update(big)
    order = ["a_norm", "a_w_in", "a_v_norm", "a_w_s", "a_b_s", "a_w_out", "f_norm", "f_w_in", "f_conv_w", "f_conv_b",
             "f_w_out", "kv_norm", "w_kv", "k_norm", "b_norm", "b_w_q", "b_q_norm", "b_sinks", "b_w_o"]
    return (loss, grad_x, *[out[n][0] for n in order], *[out[n][1] for n in order],
            *[out[n][2] for n in order], *[out[n][3] for n in order])
```

```python
import functools
import math

import jax
import jax.numpy as jnp
from jax import lax
from jax.experimental import pallas as pl
from jax.experimental.pallas import tpu as pltpu

f32 = jnp.float32
bf16 = jnp.bfloat16
MESH = pl.DeviceIdType.MESH
ANY = pl.BlockSpec(memory_space=pl.ANY)

EPS = 1e-6
LANES = 128
CHUNK = 128
HEAD_DIM = 64
N_Q_HEADS = 16
N_KV_HEADS = 4
Q_PER_KV = N_Q_HEADS // N_KV_HEADS
N_SHARDS = 4
N_DEV = 8

ADAM_LR = 0.001
ADAM_B1 = 0.9
ADAM_B2 = 0.999
ADAM_EPS = 1e-08
ADAM_WD = 0.01
ADAM_STEP = 10
ADAM_C1 = 1.0 - ADAM_B1 ** ADAM_STEP
ADAM_C2 = 1.0 - ADAM_B2 ** ADAM_STEP

_INV_SQRT2 = 1.0 / math.sqrt(2.0)
_INV_SQRT2PI = 1.0 / math.sqrt(2.0 * math.pi)


def _params(*sem):
    return pltpu.CompilerParams(dimension_semantics=sem)


def _gelu(z):
    return 0.5 * z * (1.0 + lax.erf(z * _INV_SQRT2))


def _gelu_and_grad(z):
    cdf = 0.5 * (1.0 + lax.erf(z * _INV_SQRT2))
    return z * cdf, cdf + z * jnp.exp(-0.5 * z * z) * _INV_SQRT2PI


def _dot(a, b):
    return jnp.dot(a, b, preferred_element_type=f32)


def _dot_nt(a, b):
    return lax.dot_general(a, b, (((1,), (1,)), ((), ())), preferred_element_type=f32)


def _dot_tn(a, b):
    return lax.dot_general(a, b, (((0,), (0,)), ((), ())), preferred_element_type=f32)


def _dot_split(a, b):
    hi = a.astype(bf16)
    lo = (a - hi.astype(f32)).astype(bf16)
    return _dot(hi, b) + _dot(lo, b)


VMEM_TILE_BUDGET = 40 * 1024 * 1024
MAX_ROW_TILE = 2048


def _row_tile(m, fixed_bytes, row_bytes):
    tm = min(m, MAX_ROW_TILE)
    while tm > 256 and 2 * (fixed_bytes + tm * row_bytes) > VMEM_TILE_BUDGET:
        tm //= 2
    return tm


def _isz(a):
    return jnp.dtype(a.dtype).itemsize


def mm_nn(a, w3, *, name, s0=0, ns=None, add=None, out_dtype=f32):
    m, k = a.shape
    s_all, _, n_s = w3.shape
    ns = s_all if ns is None else ns
    tm = _row_tile(m, k * n_s * 2, k * _isz(a) + n_s * jnp.dtype(out_dtype).itemsize + (0 if add is None else n_s * 4))

    def body(*refs):
        if add is None:
            a_ref, w_ref, o_ref = refs
            acc = _dot(a_ref[...].astype(bf16), w_ref[0])
        else:
            a_ref, w_ref, add_ref, o_ref = refs
            acc = _dot(a_ref[...].astype(bf16), w_ref[0]) + add_ref[...]
        o_ref[...] = acc.astype(out_dtype)

    in_specs = [pl.BlockSpec((tm, k), lambda j, i: (i, 0)),
                pl.BlockSpec((1, k, n_s), lambda j, i: (s0 + j, 0, 0))]
    args = [a, w3]
    if add is not None:
        in_specs.append(pl.BlockSpec((tm, n_s), lambda j, i: (i, j)))
        args.append(add)
    return pl.pallas_call(
        body, name=name, grid=(ns, m // tm), in_specs=in_specs,
        out_specs=pl.BlockSpec((tm, n_s), lambda j, i: (i, j)),
        out_shape=jax.ShapeDtypeStruct((m, ns * n_s), out_dtype),
        compiler_params=_params("parallel", "parallel"))(*args)


def mm_nt(a_list, w3, *, name, tko=None, add=None, out_dtype=f32):
    s_all, k_out, n_s = w3.shape
    m = a_list[0].shape[0]
    na = len(a_list)
    spa = s_all // na
    tko = k_out if tko is None else tko
    tm = _row_tile(m, tko * n_s * 2, na * n_s * _isz(a_list[0]) + tko * 4 * (1 if add is None else 2))

    def body(*refs):
        a_refs = refs[:na]
        w_ref = refs[na]
        o_ref = refs[-1]
        s = pl.program_id(2)

        @pl.when(s == 0)
        def _():
            if add is None:
                o_ref[...] = jnp.zeros_like(o_ref)
            else:
                o_ref[...] = refs[na + 1][...]

        for idx in range(na):
            @pl.when(s // spa == idx)
            def _(idx=idx):
                o_ref[...] += _dot_nt(a_refs[idx][...].astype(bf16), w_ref[0])

    def a_map(idx):
        return lambda ko, i, s: (i, jnp.clip(s - idx * spa, 0, spa - 1))

    in_specs = [pl.BlockSpec((tm, n_s), a_map(idx)) for idx in range(na)]
    in_specs.append(pl.BlockSpec((1, tko, n_s), lambda ko, i, s: (s, ko, 0)))
    args = list(a_list) + [w3]
    if add is not None:
        in_specs.append(pl.BlockSpec((tm, tko), lambda ko, i, s: (i, ko)))
        args.append(add)
    return pl.pallas_call(
        body, name=name, grid=(k_out // tko, m // tm, s_all), in_specs=in_specs,
        out_specs=pl.BlockSpec((tm, tko), lambda ko, i, s: (i, ko)),
        out_shape=jax.ShapeDtypeStruct((m, k_out), out_dtype),
        compiler_params=_params("parallel", "parallel", "arbitrary"))(*args)


def mm_tn(a, b_list, c_arr, *, name, n_s, shard_rows, tki=None):
    m, k_in = a.shape
    na = len(b_list)
    s_all = sum(b.shape[1] for b in b_list) // n_s
    spa = s_all // na
    tki = k_in if tki is None else tki
    tm = _row_tile(m, tki * n_s * 4, tki * _isz(a) + na * n_s * _isz(b_list[0]))

    nsteps = m // tm
    per_blk = tki // shard_rows
    half = shard_rows // 2

    def body(c_ref, *refs):
        a_ref = refs[0]
        b_refs = refs[1:1 + na]
        o_ref, ob_ref = refs[-2], refs[-1]
        s = pl.program_id(0)
        r = pl.program_id(2)

        @pl.when(r == 0)
        def _():
            o_ref[...] = jnp.zeros_like(o_ref)

        for idx in range(na):
            @pl.when(s // spa == idx)
            def _(idx=idx):
                o_ref[0] += _dot_tn(a_ref[...].astype(bf16), b_refs[idx][...].astype(bf16))

        @pl.when(r == nsteps - 1)
        def _():
            for q in range(per_blk):
                start = pl.multiple_of(q * shard_rows + (1 - c_ref[0]) * half, 16)
                ob_ref[q] = o_ref[0, pl.ds(start, half), :].astype(bf16)

    def b_map(idx):
        def index(s, ki, r, c_ref):
            active = (s // spa) == idx
            return (jnp.where(active, r, 0), jnp.clip(s - idx * spa, 0, spa - 1))
        return index

    in_specs = [pl.BlockSpec((tm, tki), lambda s, ki, r, c_ref: (r, ki))]
    in_specs += [pl.BlockSpec((tm, n_s), b_map(idx)) for idx in range(na)]
    n_blk = k_in // tki
    return pl.pallas_call(
        body, name=name,
        grid_spec=pltpu.PrefetchScalarGridSpec(
            num_scalar_prefetch=1, grid=(s_all, n_blk, nsteps), in_specs=in_specs,
            out_specs=[pl.BlockSpec((1, tki, n_s), lambda s, ki, r, c_ref: (s, ki, 0)),
                       pl.BlockSpec((per_blk, half, n_s), lambda s, ki, r, c_ref: (s * n_blk + ki, 0, 0))]),
        out_shape=[jax.ShapeDtypeStruct((s_all, k_in, n_s), f32),
                   jax.ShapeDtypeStruct((s_all * k_in // shard_rows, half, n_s), bf16)],
        compiler_params=_params("parallel", "parallel", "arbitrary"))(c_arr, a, *b_list)


def mm_nt_rms_bwd(a_list, w3, x, r, g, dx_in, *, name, extra=None):
    s_all, d, n_s = w3.shape
    m = a_list[0].shape[0]
    na = len(a_list)
    spa = s_all // na
    ne = 0 if extra is None else 1
    tm = _row_tile(m, d * n_s * 2, na * n_s * _isz(a_list[0]) + d * 4 * (4 + ne))

    def body(*refs):
        a_refs, w_ref = refs[:na], refs[na]
        x_ref, r_ref, g_ref, dxin_ref = refs[na + 1:na + 5]
        dh2_ref, g2_ref = (refs[na + 5], refs[na + 6]) if ne else (None, None)
        outs = refs[na + 5 + 2 * ne:]
        dx_ref, dg_ref = outs[0], outs[1]
        dg2_ref = outs[2] if ne else None
        acc_ref = outs[-1]
        i, s = pl.program_id(0), pl.program_id(1)

        @pl.when(s == 0)
        def _():
            acc_ref[...] = jnp.zeros_like(acc_ref)

        for idx in range(na):
            @pl.when(s // spa == idx)
            def _(idx=idx):
                acc_ref[...] += _dot_nt(a_refs[idx][...].astype(bf16), w_ref[0])

        @pl.when(s == s_all - 1)
        def _():
            rv = r_ref[...]
            xh = x_ref[...] * rv
            total = dxin_ref[...]
            pairs = [(acc_ref[...], g_ref, dg_ref)] + ([(dh2_ref[...], g2_ref, dg2_ref)] if ne else [])
            for dh, gain_ref, dgain_ref in pairs:
                part = jnp.sum(dh * xh, axis=0, keepdims=True)

                @pl.when(i == 0)
                def _(dgain_ref=dgain_ref, part=part):
                    dgain_ref[...] = part

                @pl.when(i > 0)
                def _(dgain_ref=dgain_ref, part=part):
                    dgain_ref[...] += part

                tg = dh * gain_ref[...]
                total = total + rv * (tg - xh * jnp.mean(tg * xh, axis=1, keepdims=True))
            dx_ref[...] = total

    def a_map(idx):
        return lambda i, s: (i, jnp.clip(s - idx * spa, 0, spa - 1))

    row = pl.BlockSpec((tm, d), lambda i, s: (i, 0))
    vec = pl.BlockSpec((1, d), lambda i, s: (0, 0))
    in_specs = [pl.BlockSpec((tm, n_s), a_map(idx)) for idx in range(na)]
    in_specs += [pl.BlockSpec((1, d, n_s), lambda i, s: (s, 0, 0)), row, pl.BlockSpec((tm, 1), lambda i, s: (i, 0)), vec, row]
    args = list(a_list) + [w3, x, r, g, dx_in]
    if ne:
        in_specs += [row, vec]
        args += list(extra)
    outs = pl.pallas_call(
        body, name=name, grid=(m // tm, s_all), in_specs=in_specs, out_specs=[row] + [vec] * (1 + ne),
        out_shape=[jax.ShapeDtypeStruct((m, d), f32)] + [jax.ShapeDtypeStruct((1, d), f32)] * (1 + ne),
        scratch_shapes=[pltpu.VMEM((tm, d), f32)],
        compiler_params=_params("arbitrary", "arbitrary"))(*args)
    return outs


def mm_residual(a, w, x, *, name, gains=(), target=None):
    m, k = a.shape
    d = w.shape[1]
    ng = len(gains)
    tm = _row_tile(m, k * d * 2, k * _isz(a) + d * 4 * 3 + ng * d * 2)

    def body(*refs):
        a_ref, w_ref, x_ref = refs[:3]
        y = _dot(a_ref[...].astype(bf16), w_ref[...]) + x_ref[...]
        if target is None:
            g_refs = refs[3:3 + ng]
            y_ref = refs[3 + ng]
            h_refs = refs[4 + ng:4 + 2 * ng]
            r_ref = refs[-1]
            y_ref[...] = y
            r = lax.rsqrt(jnp.mean(y * y, axis=1, keepdims=True) + EPS)
            yh = y * r
            for g_ref, h_ref in zip(g_refs, h_refs):
                h_ref[...] = (yh * g_ref[...]).astype(bf16)
            r_ref[...] = r
        else:
            t_ref, dy_ref, s_ref = refs[3:]
            i = pl.program_id(0)
            e = y - t_ref[...]
            dy_ref[...] = e * (1.0 / d)
            part = jnp.sum(e * e, axis=0, keepdims=True)

            @pl.when(i == 0)
            def _():
                s_ref[...] = part

            @pl.when(i > 0)
            def _():
                s_ref[...] += part

    row = pl.BlockSpec((tm, d), lambda i: (i, 0))
    vec = pl.BlockSpec((1, d), lambda i: (0, 0))
    in_specs = [pl.BlockSpec((tm, k), lambda i: (i, 0)), pl.BlockSpec((k, d), lambda i: (0, 0)), row]
    if target is None:
        outs = pl.pallas_call(
            body, name=name, grid=(m // tm,), in_specs=in_specs + [vec] * ng,
            out_specs=[row] * (1 + ng) + [pl.BlockSpec((tm, 1), lambda i: (i, 0))],
            out_shape=[jax.ShapeDtypeStruct((m, d), f32)] + [jax.ShapeDtypeStruct((m, d), bf16)] * ng
            + [jax.ShapeDtypeStruct((m, 1), f32)],
            compiler_params=_params("parallel"))(a, w, x, *gains)
        return outs[0], outs[1:1 + ng], outs[-1]
    return pl.pallas_call(
        body, name=name, grid=(m // tm,), in_specs=in_specs + [row], out_specs=[row, vec],
        out_shape=[jax.ShapeDtypeStruct((m, d), f32), jax.ShapeDtypeStruct((1, d), f32)],
        compiler_params=_params("arbitrary"))(a, w, x, target)


def rms_fwd(x, gains, *, name, tr=512):
    t, d = x.shape
    tr = min(tr, t)
    ng = len(gains)

    def body(*refs):
        x_ref = refs[0]
        g_refs = refs[1:1 + ng]
        h_refs = refs[1 + ng:1 + 2 * ng]
        r_ref = refs[-1]
        xv = x_ref[...]
        r = lax.rsqrt(jnp.mean(xv * xv, axis=1, keepdims=True) + EPS)
        xh = xv * r
        for g_ref, h_ref in zip(g_refs, h_refs):
            h_ref[...] = (xh * g_ref[...]).astype(bf16)
        r_ref[...] = r

    row = pl.BlockSpec((tr, d), lambda i: (i, 0))
    vec = pl.BlockSpec((1, d), lambda i: (0, 0))
    outs = pl.pallas_call(
        body, name=name, grid=(t // tr,), in_specs=[row] + [vec] * ng,
        out_specs=[row] * ng + [pl.BlockSpec((tr, 1), lambda i: (i, 0))],
        out_shape=[jax.ShapeDtypeStruct((t, d), bf16)] * ng + [jax.ShapeDtypeStruct((t, 1), f32)],
        compiler_params=_params("parallel"))(x, *gains)
    return outs[:ng], outs[ng]


def rms_bwd(dh_list, x, r, gains, dx_in, *, name, tr=512):
    t, d = x.shape
    tr = min(tr, t)
    ng = len(gains)

    def body(*refs):
        dh_refs = refs[:ng]
        x_ref, r_ref = refs[ng], refs[ng + 1]
        g_refs = refs[ng + 2:2 * ng + 2]
        dxin_ref = refs[2 * ng + 2]
        dx_ref = refs[2 * ng + 3]
        dg_refs = refs[2 * ng + 4:]
        i = pl.program_id(0)
        rv = r_ref[...]
        xh = x_ref[...] * rv
        acc = dxin_ref[...]
        for dh_ref, g_ref, dg_ref in zip(dh_refs, g_refs, dg_refs):
            dh = dh_ref[...]
            part = jnp.sum(dh * xh, axis=0, keepdims=True)

            @pl.when(i == 0)
            def _(dg_ref=dg_ref, part=part):
                dg_ref[...] = part

            @pl.when(i > 0)
            def _(dg_ref=dg_ref, part=part):
                dg_ref[...] += part

            tg = dh * g_ref[...]
            acc = acc + rv * (tg - xh * jnp.mean(tg * xh, axis=1, keepdims=True))
        dx_ref[...] = acc

    row = pl.BlockSpec((tr, d), lambda i: (i, 0))
    vec = pl.BlockSpec((1, d), lambda i: (0, 0))
    outs = pl.pallas_call(
        body, name=name, grid=(t // tr,),
        in_specs=[row] * ng + [row, pl.BlockSpec((tr, 1), lambda i: (i, 0))] + [vec] * ng + [row],
        out_specs=[row] + [vec] * ng,
        out_shape=[jax.ShapeDtypeStruct((t, d), f32)] + [jax.ShapeDtypeStruct((1, d), f32)] * ng,
        compiler_params=_params("arbitrary"))(*dh_list, x, r, *gains, dx_in)
    return outs[0], outs[1:]


def sgu_gate_fwd(zu, zv, gv, wc, bt, *, name, tr=512):
    t, w = zu.shape
    tr = min(tr, t)
    groups = w // LANES

    def body(zu_ref, zv_ref, gv_ref, wc_ref, bt_ref, y_ref):
        vp = _gelu(zv_ref[...])
        rv = lax.rsqrt(jnp.mean(vp * vp, axis=1, keepdims=True) + EPS)
        vb = (vp * rv * gv_ref[...]).astype(bf16)
        for c in range(tr // CHUNK):
            rows = slice(c * CHUNK, (c + 1) * CHUNK)
            for g in range(groups):
                cols = slice(g * LANES, (g + 1) * LANES)
                sv = _dot(wc_ref[g], vb[rows, cols]) + bt_ref[:, g:g + 1]
                y_ref[rows, cols] = (_gelu(zu_ref[rows, cols]) * sv).astype(bf16)

    row = pl.BlockSpec((tr, w), lambda i: (i, 0))
    return pl.pallas_call(
        body, name=name, grid=(t // tr,),
        in_specs=[row, row, pl.BlockSpec((1, w), lambda i: (0, 0)),
                  pl.BlockSpec((groups, CHUNK, CHUNK), lambda i: (0, 0, 0)),
                  pl.BlockSpec((CHUNK, groups), lambda i: (0, 0))],
        out_specs=row, out_shape=jax.ShapeDtypeStruct((t, w), bf16),
        compiler_params=_params("parallel"))(zu, zv, gv, wc, bt)


def sgu_gate_bwd(zu, zv, dy, gv, wc, bt, *, name, tr=512):
    t, w = zu.shape
    tr = min(tr, t)
    groups = w // LANES
    nsteps = t // tr

    def body(zu_ref, zv_ref, dy_ref, gv_ref, wc_ref, bt_ref,
             dzu_ref, dzv_ref, dgv_ref, dws_ref, dbt_ref, dv_ref, bacc_ref):
        i = pl.program_id(0)

        @pl.when(i == 0)
        def _():
            dgv_ref[...] = jnp.zeros_like(dgv_ref)
            dws_ref[...] = jnp.zeros_like(dws_ref)
            bacc_ref[...] = jnp.zeros_like(bacc_ref)

        vp, vp_grad = _gelu_and_grad(zv_ref[...])
        rv = lax.rsqrt(jnp.mean(vp * vp, axis=1, keepdims=True) + EPS)
        vhat = vp * rv
        vb = (vhat * gv_ref[...]).astype(bf16)
        for c in range(tr // CHUNK):
            rows = slice(c * CHUNK, (c + 1) * CHUNK)
            for g in range(groups):
                cols = slice(g * LANES, (g + 1) * LANES)
                vblk = vb[rows, cols]
                sv = _dot(wc_ref[g], vblk) + bt_ref[:, g:g + 1]
                zub = zu_ref[rows, cols]
                dyb = dy_ref[rows, cols]
                ub, ub_grad = _gelu_and_grad(zub)
                dzu_ref[rows, cols] = (dyb * sv * ub_grad).astype(bf16)
                dsv = dyb * ub
                bacc_ref[:, cols] += dsv
                dsvb = dsv.astype(bf16)
                dv_ref[rows, cols] = _dot_tn(wc_ref[g], dsvb)
                dws_ref[g] += _dot_nt(dsvb, vblk)
        dv = dv_ref[...]
        dgv_ref[...] += jnp.sum(dv * vhat, axis=0, keepdims=True)
        tg = dv * gv_ref[...]
        dvp = rv * (tg - vhat * jnp.mean(tg * vhat, axis=1, keepdims=True))
        dzv_ref[...] = (dvp * vp_grad).astype(bf16)

        @pl.when(i == nsteps - 1)
        def _():
            tt = lax.broadcasted_iota(jnp.int32, (CHUNK, CHUNK), 0)
            ss = lax.broadcasted_iota(jnp.int32, (CHUNK, CHUNK), 1)
            for g in range(groups):
                dws_ref[g] = jnp.where(ss <= tt, dws_ref[g], 0.0)
                dbt_ref[:, g:g + 1] = jnp.sum(bacc_ref[:, g * LANES:(g + 1) * LANES], axis=1, keepdims=True)

    row = pl.BlockSpec((tr, w), lambda i: (i, 0))
    full3 = pl.BlockSpec((groups, CHUNK, CHUNK), lambda i: (0, 0, 0))
    return pl.pallas_call(
        body, name=name, grid=(nsteps,),
        in_specs=[row, row, row, pl.BlockSpec((1, w), lambda i: (0, 0)), full3,
                  pl.BlockSpec((CHUNK, groups), lambda i: (0, 0))],
        out_specs=[row, row, pl.BlockSpec((1, w), lambda i: (0, 0)), full3,
                   pl.BlockSpec((CHUNK, groups), lambda i: (0, 0))],
        out_shape=[jax.ShapeDtypeStruct((t, w), bf16), jax.ShapeDtypeStruct((t, w), bf16),
                   jax.ShapeDtypeStruct((1, w), f32), jax.ShapeDtypeStruct((groups, CHUNK, CHUNK), f32),
                   jax.ShapeDtypeStruct((CHUNK, groups), f32)],
        scratch_shapes=[pltpu.VMEM((tr, w), f32), pltpu.VMEM((CHUNK, w), f32)],
        compiler_params=_params("arbitrary"))(zu, zv, dy, gv, wc, bt)


HALO = 8


def _shift_down(v, halo, k, first):
    r = pltpu.roll(v, k, 0)
    hh = jnp.where(first, 0.0, pltpu.roll(halo, k, 0))
    rid = lax.broadcasted_iota(jnp.int32, (HALO, v.shape[1]), 0)
    head = jnp.where(rid < k, hh, r[0:HALO])
    if v.shape[0] == HALO:
        return head
    return jnp.concatenate([head, r[HALO:]], axis=0)


def _shift_up(v, halo, k, last):
    n = v.shape[0]
    r = pltpu.roll(v, n - k, 0)
    hh = jnp.where(last, 0.0, pltpu.roll(halo, HALO - k, 0))
    rid = lax.broadcasted_iota(jnp.int32, (HALO, v.shape[1]), 0)
    tail = jnp.where(rid >= HALO - k, hh, r[n - HALO:])
    return jnp.concatenate([r[:n - HALO], tail], axis=0)


def _conv(p, halo, w_ref, b_ref, first):
    return (w_ref[2:3, :] * p + w_ref[1:2, :] * _shift_down(p, halo, 1, first)
            + w_ref[0:1, :] * _shift_down(p, halo, 2, first) + b_ref[...])


BF16_ROWS = 16


def ffn_in_fused(h, w_in4, wg, wu, bg, bu, *, name):
    t, k = h.shape
    s_all, _, n_s = w_in4.shape
    half = s_all // 2
    tm = _row_tile(t, 2 * k * n_s * 2, k * 2 + 4 * n_s * 4 + n_s * 2)

    def body(h_ref, hh_ref, wg_ref, wu_ref, cg_ref, cu_ref, bg_ref, bu_ref, pg_ref, pu_ref, gate_ref, up_ref, a_ref):
        first = pl.program_id(1) == 0
        hv, hh = h_ref[...], hh_ref[...]
        outs = []
        for w_ref, c_ref, b_ref, p_ref, o_ref in ((wg_ref, cg_ref, bg_ref, pg_ref, gate_ref),
                                                  (wu_ref, cu_ref, bu_ref, pu_ref, up_ref)):
            p = _dot(hv, w_ref[0])
            p_ref[...] = p
            hu = _conv(p, _dot(hh, w_ref[0])[BF16_ROWS - HALO:], c_ref, b_ref, first)
            o_ref[...] = hu
            outs.append(hu)
        gate, up = outs
        a_ref[...] = (gate * jax.nn.sigmoid(gate) * up).astype(bf16)

    tile = pl.BlockSpec((tm, n_s), lambda j, i: (i, j))
    cw = pl.BlockSpec((3, n_s), lambda j, i: (0, j))
    cb = pl.BlockSpec((1, n_s), lambda j, i: (0, j))
    f = half * n_s
    return pl.pallas_call(
        body, name=name, grid=(half, t // tm),
        in_specs=[pl.BlockSpec((tm, k), lambda j, i: (i, 0)),
                  pl.BlockSpec((BF16_ROWS, k), lambda j, i: (jnp.maximum(i * (tm // BF16_ROWS) - 1, 0), 0)),
                  pl.BlockSpec((1, k, n_s), lambda j, i: (j, 0, 0)),
                  pl.BlockSpec((1, k, n_s), lambda j, i: (j + half, 0, 0)), cw, cw, cb, cb],
        out_specs=[tile] * 5,
        out_shape=[jax.ShapeDtypeStruct((t, f), f32)] * 4 + [jax.ShapeDtypeStruct((t, f), bf16)],
        compiler_params=_params("parallel", "parallel"))(h, h, w_in4, w_in4, wg, wu, bg, bu)


def _gate_grads(gate, up, dav):
    sg = jax.nn.sigmoid(gate)
    return dav * up * (sg * (1.0 + gate * (1.0 - sg))), dav * gate * sg


GATE_BWD_ROWS = 512


def ffn_gate_bwd(dy, w_out, pg, pu, gate, up, wg, wu, *, name):
    t, f = pg.shape
    d = dy.shape[1]
    tr = min(GATE_BWD_ROWS, t)
    nsteps = t // tr
    tc = f // 2

    def body(dy_ref, dyn_ref, w_ref, pg_ref, pu_ref, gate_ref, gaten_ref, up_ref, upn_ref, wg_ref, wu_ref,
             dg_ref, du_ref, sg_ref, su_ref):
        i = pl.program_id(1)
        last = i == nsteps - 1
        w = w_ref[0]
        da = _dot_nt(dy_ref[...].astype(bf16), w)
        da_n = _dot_nt(dyn_ref[...].astype(bf16), w)
        dgate, dup = _gate_grads(gate_ref[...], up_ref[...], da)
        dgate_n, dup_n = _gate_grads(gaten_ref[...], upn_ref[...], da_n)
        rid = lax.broadcasted_iota(jnp.int32, (8, tc), 0)
        for dd, d_n, c_ref, p_ref, o_ref, s_ref in ((dgate, dgate_n, wg_ref, pg_ref, dg_ref, sg_ref),
                                                    (dup, dup_n, wu_ref, pu_ref, du_ref, su_ref)):
            d1, d2 = _shift_up(dd, d_n, 1, last), _shift_up(dd, d_n, 2, last)
            o_ref[...] = (c_ref[2:3, :] * dd + c_ref[1:2, :] * d1 + c_ref[0:1, :] * d2).astype(bf16)
            p = p_ref[...]
            sums = [jnp.sum(d2 * p, axis=0, keepdims=True), jnp.sum(d1 * p, axis=0, keepdims=True),
                    jnp.sum(dd * p, axis=0, keepdims=True), jnp.sum(dd, axis=0, keepdims=True)]
            part = jnp.zeros((8, tc), f32)
            for k, sk in enumerate(sums):
                part = jnp.where(rid == k, sk, part)

            @pl.when(i == 0)
            def _(s_ref=s_ref, part=part):
                s_ref[...] = part

            @pl.when(i > 0)
            def _(s_ref=s_ref, part=part):
                s_ref[...] += part

    def nxt_rows(j, i):
        return (jnp.minimum((i + 1) * (tr // HALO), t // HALO - 1), j)

    tile = pl.BlockSpec((tr, tc), lambda j, i: (i, j))
    nxt = pl.BlockSpec((HALO, tc), nxt_rows)
    wspec = pl.BlockSpec((3, tc), lambda j, i: (0, j))
    stat = pl.BlockSpec((8, tc), lambda j, i: (0, j))
    return pl.pallas_call(
        body, name=name, grid=(2, nsteps),
        in_specs=[pl.BlockSpec((tr, d), lambda j, i: (i, 0)),
                  pl.BlockSpec((HALO, d), lambda j, i: (nxt_rows(j, i)[0], 0)),
                  pl.BlockSpec((1, tc, d), lambda j, i: (j, 0, 0)),
                  tile, tile, tile, nxt, tile, nxt, wspec, wspec],
        out_specs=[tile, tile, stat, stat],
        out_shape=[jax.ShapeDtypeStruct((t, f), bf16), jax.ShapeDtypeStruct((t, f), bf16),
                   jax.ShapeDtypeStruct((8, f), f32), jax.ShapeDtypeStruct((8, f), f32)],
        compiler_params=_params("parallel", "arbitrary"))(
            dy, dy, w_out.reshape(2, tc, d), pg, pu, gate, gate, up, up, wg, wu)


def _head_mean_matrix():
    i = lax.broadcasted_iota(jnp.int32, (LANES, LANES), 0) // HEAD_DIM
    j = lax.broadcasted_iota(jnp.int32, (LANES, LANES), 1) // HEAD_DIM
    return jnp.where(i == j, 1.0 / HEAD_DIM, 0.0).astype(bf16)


def _lane_half(shape):
    return (lax.broadcasted_iota(jnp.int32, shape, 1) % LANES) // HEAD_DIM


def q_norm_fwd(qp, g2, *, name, scale, tr=512):
    t, w = qp.shape
    tr = min(tr, t)

    def body(x_ref, g_ref, o_ref):
        bd = _head_mean_matrix()
        for cb in range(w // LANES):
            cols = slice(cb * LANES, (cb + 1) * LANES)
            xc = x_ref[:, cols]
            rh = lax.rsqrt(_dot_split(xc * xc, bd) + EPS)
            o_ref[:, cols] = (xc * rh * g_ref[...] * scale).astype(bf16)

    row = pl.BlockSpec((tr, w), lambda i: (i, 0))
    return pl.pallas_call(
        body, name=name, grid=(t // tr,), in_specs=[row, pl.BlockSpec((1, LANES), lambda i: (0, 0))],
        out_specs=row, out_shape=jax.ShapeDtypeStruct((t, w), bf16),
        compiler_params=_params("parallel"))(qp, g2)


def q_norm_bwd(dq, qp, g2, *, name, scale, tr=512):
    t, w = qp.shape
    tr = min(tr, t)

    def body(dq_ref, x_ref, g_ref, o_ref, dg_ref):
        i = pl.program_id(0)
        bd = _head_mean_matrix()
        acc = jnp.zeros((1, LANES), f32)
        for cb in range(w // LANES):
            cols = slice(cb * LANES, (cb + 1) * LANES)
            xc = x_ref[:, cols]
            rh = lax.rsqrt(_dot_split(xc * xc, bd) + EPS)
            xh = xc * rh
            dy = dq_ref[:, cols] * scale
            acc = acc + jnp.sum(dy * xh, axis=0, keepdims=True)
            tg = dy * g_ref[...]
            o_ref[:, cols] = (rh * (tg - xh * _dot_split(tg * xh, bd))).astype(bf16)

        @pl.when(i == 0)
        def _():
            dg_ref[...] = acc

        @pl.when(i > 0)
        def _():
            dg_ref[...] += acc

    row = pl.BlockSpec((tr, w), lambda i: (i, 0))
    vec = pl.BlockSpec((1, LANES), lambda i: (0, 0))
    return pl.pallas_call(
        body, name=name, grid=(t // tr,), in_specs=[row, row, vec], out_specs=[row, vec],
        out_shape=[jax.ShapeDtypeStruct((t, w), bf16), jax.ShapeDtypeStruct((1, LANES), f32)],
        compiler_params=_params("arbitrary"))(dq, qp, g2)


def kv_post_fwd(kv, g2, *, name, tr=512):
    t, w = kv.shape
    tr = min(tr, t)
    kw = w // 2

    def body(x_ref, g_ref, k_ref, v_ref):
        bd = _head_mean_matrix()
        half = _lane_half((tr, LANES))
        for cb in range(kw // LANES):
            xc = x_ref[:, cb * LANES:(cb + 1) * LANES]
            rh = lax.rsqrt(_dot_split(xc * xc, bd) + EPS)
            kn = xc * rh * g_ref[...]
            vc = x_ref[:, kw + cb * LANES:kw + (cb + 1) * LANES]
            for src, dst in ((kn, k_ref), (vc, v_ref)):
                sw = pltpu.roll(src, HEAD_DIM, 1)
                for hf in range(2):
                    blk = 2 * cb + hf
                    dst[:, blk * LANES:(blk + 1) * LANES] = jnp.where(half == hf, src, sw).astype(bf16)

    return pl.pallas_call(
        body, name=name, grid=(t // tr,),
        in_specs=[pl.BlockSpec((tr, w), lambda i: (i, 0)), pl.BlockSpec((1, LANES), lambda i: (0, 0))],
        out_specs=[pl.BlockSpec((tr, 2 * kw), lambda i: (i, 0))] * 2,
        out_shape=[jax.ShapeDtypeStruct((t, 2 * kw), bf16)] * 2,
        compiler_params=_params("parallel"))(kv, g2)


def kv_post_bwd(dk2, dv2, kv, g2, *, name, tr=512):
    t, w = kv.shape
    tr = min(tr, t)
    kw = w // 2

    def body(dk_ref, dv_ref, x_ref, g_ref, o_ref, dg_ref):
        i = pl.program_id(0)
        bd = _head_mean_matrix()
        half = _lane_half((tr, LANES))
        acc = jnp.zeros((1, LANES), f32)

        def fold(ref, cb):
            a = ref[:, (2 * cb) * LANES:(2 * cb + 1) * LANES]
            b = ref[:, (2 * cb + 1) * LANES:(2 * cb + 2) * LANES]
            return jnp.where(half == 0, a + pltpu.roll(a, HEAD_DIM, 1), b + pltpu.roll(b, HEAD_DIM, 1))

        for cb in range(kw // LANES):
            cols = slice(cb * LANES, (cb + 1) * LANES)
            xc = x_ref[:, cols]
            rh = lax.rsqrt(_dot_split(xc * xc, bd) + EPS)
            xh = xc * rh
            dy = fold(dk_ref, cb)
            acc = acc + jnp.sum(dy * xh, axis=0, keepdims=True)
            tg = dy * g_ref[...]
            o_ref[:, cols] = (rh * (tg - xh * _dot_split(tg * xh, bd))).astype(bf16)
            o_ref[:, kw + cb * LANES:kw + (cb + 1) * LANES] = fold(dv_ref, cb).astype(bf16)

        @pl.when(i == 0)
        def _():
            dg_ref[...] = acc

        @pl.when(i > 0)
        def _():
            dg_ref[...] += acc

    dup = pl.BlockSpec((tr, 2 * kw), lambda i: (i, 0))
    row = pl.BlockSpec((tr, w), lambda i: (i, 0))
    vec = pl.BlockSpec((1, LANES), lambda i: (0, 0))
    return pl.pallas_call(
        body, name=name, grid=(t // tr,), in_specs=[dup, dup, row, vec], out_specs=[row, vec],
        out_shape=[jax.ShapeDtypeStruct((t, w), bf16), jax.ShapeDtypeStruct((1, LANES), f32)],
        compiler_params=_params("arbitrary"))(dk2, dv2, kv, g2)


def _slope(h):
    return 2.0 ** (-8.0 * (h + 1) / N_Q_HEADS)


GROUP_ROWS = Q_PER_KV * CHUNK


def _band_mask(n):
    tq = lax.broadcasted_iota(jnp.int32, (GROUP_ROWS, 2 * CHUNK), 0) % CHUNK
    jk = lax.broadcasted_iota(jnp.int32, (GROUP_ROWS, 2 * CHUNK), 1)
    dist = tq + CHUNK - jk
    ok = (dist >= 0) & (dist < CHUNK) & jnp.logical_not((n == 0) & (jk < CHUNK))
    return dist.astype(f32), ok


def _band(ref, n, kh):
    p0 = pl.multiple_of(jnp.maximum(n - 1, 0) * CHUNK, CHUNK)
    c0 = pl.multiple_of(n * CHUNK, CHUNK)
    cols = slice(kh * LANES, (kh + 1) * LANES)
    return jnp.concatenate([ref[pl.ds(p0, CHUNK), cols], ref[pl.ds(c0, CHUNK), cols]], axis=0)


def _stack_heads(ref, kh, half):
    parts = []
    for cb in (2 * kh, 2 * kh + 1):
        xc = ref[:, cb * LANES:(cb + 1) * LANES].astype(f32)
        parts += [jnp.where(half == hf, xc, 0.0).astype(bf16) for hf in range(2)]
    return jnp.concatenate(parts, axis=0)


def _unstack_heads(x4, half):
    return (jnp.where(half == 0, x4[0:CHUNK], x4[CHUNK:2 * CHUNK]),
            jnp.where(half == 0, x4[2 * CHUNK:3 * CHUNK], x4[3 * CHUNK:]))


def _per_head_column(kh, values):
    grp = lax.broadcasted_iota(jnp.int32, (GROUP_ROWS, 1), 0) // CHUNK
    col = jnp.full((GROUP_ROWS, 1), values[0], f32)
    for g in range(1, Q_PER_KV):
        col = jnp.where(grp == g, values[g], col)
    return col


def _softmax_band(q4, kband, dist, ok, slope, sink):
    s = _dot_nt(q4, kband)
    s = jnp.where(ok, s - slope * dist, -jnp.inf)
    m = jnp.maximum(jnp.max(s, axis=1, keepdims=True), sink)
    e = jnp.exp(s - m)
    es = jnp.exp(sink - m)
    den = jnp.sum(e, axis=1, keepdims=True) + es
    return e / den, es / den


def attn_fwd(q, k2, v2, sinks, *, name):
    t, w = q.shape
    nb = t // CHUNK

    def body(sink_ref, q_ref, k_ref, v_ref, o_ref):
        n = pl.program_id(0)
        dist, ok = _band_mask(n)
        half = _lane_half((CHUNK, LANES))
        for kh in range(N_KV_HEADS):
            heads = [Q_PER_KV * kh + g for g in range(Q_PER_KV)]
            slope = _per_head_column(kh, [_slope(h) for h in heads])
            sink = _per_head_column(kh, [sink_ref[h] for h in heads])
            q4 = _stack_heads(q_ref, kh, half)
            p, _ = _softmax_band(q4, _band(k_ref, n, kh), dist, ok, slope, sink)
            o4 = _dot(p.astype(bf16), _band(v_ref, n, kh))
            lo, hi = _unstack_heads(o4, half)
            o_ref[:, (2 * kh) * LANES:(2 * kh + 1) * LANES] = lo.astype(bf16)
            o_ref[:, (2 * kh + 1) * LANES:(2 * kh + 2) * LANES] = hi.astype(bf16)

    full = pl.BlockSpec((t, k2.shape[1]), lambda n: (0, 0))
    return pl.pallas_call(
        body, name=name, grid=(nb,),
        in_specs=[pl.BlockSpec(memory_space=pltpu.SMEM), pl.BlockSpec((CHUNK, w), lambda n: (n, 0)), full, full],
        out_specs=pl.BlockSpec((CHUNK, w), lambda n: (n, 0)),
        out_shape=jax.ShapeDtypeStruct((t, w), bf16),
        compiler_params=_params("parallel"))(sinks, q, k2, v2)


def attn_bwd(q, k2, v2, do, sinks, *, name):
    t, w = q.shape
    nb = t // CHUNK
    kw = k2.shape[1]

    def body(sink_ref, q_ref, k_ref, v_ref, do_ref, dq_ref, dk_ref, dv_ref, ds_ref, kc_ref, vc_ref):
        n = pl.program_id(0)

        @pl.when(n == 0)
        def _():
            ds_ref[...] = jnp.zeros_like(ds_ref)
            kc_ref[...] = jnp.zeros_like(kc_ref)
            vc_ref[...] = jnp.zeros_like(vc_ref)
            dk_ref[...] = jnp.zeros_like(dk_ref)
            dv_ref[...] = jnp.zeros_like(dv_ref)

        @pl.when(n == nb)
        def _():
            dk_ref[...] = kc_ref[...]
            dv_ref[...] = vc_ref[...]

        @pl.when(n < nb)
        def _():
            dist, ok = _band_mask(n)
            half = _lane_half((CHUNK, LANES))
            lane = lax.broadcasted_iota(jnp.int32, (1, LANES), 1)
            sink_acc = jnp.zeros((1, LANES), f32)
            for kh in range(N_KV_HEADS):
                heads = [Q_PER_KV * kh + g for g in range(Q_PER_KV)]
                slope = _per_head_column(kh, [_slope(h) for h in heads])
                sink = _per_head_column(kh, [sink_ref[h] for h in heads])
                q4 = _stack_heads(q_ref, kh, half)
                do4 = _stack_heads(do_ref, kh, half)
                kband = _band(k_ref, n, kh)
                vband = _band(v_ref, n, kh)
                p, ps = _softmax_band(q4, kband, dist, ok, slope, sink)
                dp = _dot_nt(do4, vband)
                delta = jnp.sum(p * dp, axis=1, keepdims=True)
                dsb = (p * (dp - delta)).astype(bf16)
                sd = ps * delta
                for g, h in enumerate(heads):
                    part = jnp.sum(sd[g * CHUNK:(g + 1) * CHUNK], axis=0, keepdims=True)
                    sink_acc = sink_acc + jnp.where(lane == h, -part, 0.0)
                lo, hi = _unstack_heads(_dot(dsb, kband), half)
                dq_ref[:, (2 * kh) * LANES:(2 * kh + 1) * LANES] = lo
                dq_ref[:, (2 * kh + 1) * LANES:(2 * kh + 2) * LANES] = hi
                dkb = _dot_tn(dsb, q4)
                dvb = _dot_tn(p.astype(bf16), do4)
                cols = slice(kh * LANES, (kh + 1) * LANES)
                dk_ref[:, cols] = kc_ref[:, cols] + dkb[0:CHUNK]
                dv_ref[:, cols] = vc_ref[:, cols] + dvb[0:CHUNK]
                kc_ref[:, cols] = dkb[CHUNK:]
                vc_ref[:, cols] = dvb[CHUNK:]
            ds_ref[...] += sink_acc

    full = pl.BlockSpec((t, kw), lambda n: (0, 0))
    qblk = pl.BlockSpec((CHUNK, w), lambda n: (jnp.minimum(n, nb - 1), 0))
    kblk = pl.BlockSpec((CHUNK, kw), lambda n: (jnp.maximum(n - 1, 0), 0))
    return pl.pallas_call(
        body, name=name, grid=(nb + 1,),
        in_specs=[pl.BlockSpec(memory_space=pltpu.SMEM), qblk, full, full, qblk],
        out_specs=[qblk, kblk, kblk, pl.BlockSpec((1, LANES), lambda n: (0, 0))],
        out_shape=[jax.ShapeDtypeStruct((t, w), f32), jax.ShapeDtypeStruct((t, kw), f32),
                   jax.ShapeDtypeStruct((t, kw), f32), jax.ShapeDtypeStruct((1, LANES), f32)],
        scratch_shapes=[pltpu.VMEM((CHUNK, kw), f32), pltpu.VMEM((CHUNK, kw), f32)],
        compiler_params=_params("arbitrary"))(sinks, q, k2, v2, do)


def loss_head(y, target, *, name, tr=512):
    t, d = y.shape
    tr = min(tr, t)

    def body(y_ref, t_ref, dy_ref, s_ref):
        i = pl.program_id(0)
        e = y_ref[...] - t_ref[...]
        dy_ref[...] = e * (1.0 / d)
        part = jnp.sum(e * e, axis=0, keepdims=True)

        @pl.when(i == 0)
        def _():
            s_ref[...] = part

        @pl.when(i > 0)
        def _():
            s_ref[...] += part

    row = pl.BlockSpec((tr, d), lambda i: (i, 0))
    vec = pl.BlockSpec((1, d), lambda i: (0, 0))
    return pl.pallas_call(
        body, name=name, grid=(t // tr,), in_specs=[row, row], out_specs=[row, vec],
        out_shape=[jax.ShapeDtypeStruct((t, d), f32), jax.ShapeDtypeStruct((1, d), f32)],
        compiler_params=_params("arbitrary"))(y, target)


N_STEPS = 8


def _row_blocks(shape):
    if len(shape) == 2:
        r, c = shape
        return (r // N_STEPS, c), (lambda s: (s, 0))
    l, r, c = shape
    per = N_STEPS // l
    return (1, r // per, c), (lambda s: (s // per, s % per, 0))


CAST_STEPS = 4


def cast_into_slot(arrays, k_arr, *, name):
    in_specs, out_specs, out_shape, layers = [], [], [], []
    for a in arrays:
        r, c = a.shape[-2:]
        rb = r // CAST_STEPS
        if a.ndim == 2:
            in_specs.append(pl.BlockSpec((rb, c), lambda s, k: (s, 0)))
            layers.append(None)
        else:
            for l in range(a.shape[0]):
                in_specs.append(pl.BlockSpec((1, rb, c), lambda s, k, l=l: (l, s, 0)))
                layers.append(l)
        for _ in range(1 if a.ndim == 2 else a.shape[0]):
            out_specs.append(pl.BlockSpec((1, rb, c), lambda s, k: (k[0], s, 0)))
            out_shape.append(jax.ShapeDtypeStruct((N_SHARDS, r, c), bf16))
    n = len(in_specs)

    def body(k_ref, *refs):
        for i_ref, o_ref, l in zip(refs[:n], refs[n:], layers):
            o_ref[0] = (i_ref[...] if l is None else i_ref[0]).astype(bf16)

    args = []
    for a in arrays:
        args += [a] * (1 if a.ndim == 2 else a.shape[0])
    return pl.pallas_call(
        body, name=name,
        grid_spec=pltpu.PrefetchScalarGridSpec(num_scalar_prefetch=1, grid=(CAST_STEPS,),
                                               in_specs=in_specs, out_specs=out_specs),
        out_shape=out_shape, compiler_params=_params("parallel"))(k_arr, *args)


def adamw(ws, gs, ms, vs, *, name):
    n = len(ws)
    specs, g_specs, g_count = [], [], []
    for w, g_list in zip(ws, gs):
        blk, index = _row_blocks(w.shape)
        specs.append(pl.BlockSpec(blk, index))
        layers = len(g_list)
        per = N_STEPS // layers
        g_count.append(layers)
        for l in range(layers):
            g_specs.append(pl.BlockSpec(blk[-2:], lambda s, l=l, per=per: (jnp.where(s // per == l, s % per, 0), 0)))
    ng = len(g_specs)

    def body(*refs):
        s = pl.program_id(0)
        g_refs = refs[3 * n:3 * n + ng]
        outs = refs[3 * n + ng:]
        off = 0
        for i in range(n):
            w_ref, m_ref, v_ref = refs[i], refs[n + i], refs[2 * n + i]
            go_ref, d_ref, nm_ref, nv_ref = (outs[k * n + i] for k in range(4))
            layers = g_count[i]
            g = g_refs[off][...]
            for l in range(1, layers):
                g = jnp.where(s // (N_STEPS // layers) == l, g_refs[off + l][...], g)
            off += layers
            g = g.reshape(w_ref.shape)
            m = ADAM_B1 * m_ref[...] + (1.0 - ADAM_B1) * g
            v = ADAM_B2 * v_ref[...] + (1.0 - ADAM_B2) * (g * g)
            m_hat = m / ADAM_C1
            v_hat = v / ADAM_C2
            go_ref[...] = g
            d_ref[...] = -ADAM_LR * (m_hat / (jnp.sqrt(v_hat) + ADAM_EPS) + ADAM_WD * w_ref[...])
            nm_ref[...] = m
            nv_ref[...] = v

    outs = pl.pallas_call(
        body, name=name, grid=(N_STEPS,), in_specs=specs * 3 + g_specs, out_specs=specs * 4,
        out_shape=[jax.ShapeDtypeStruct(a.shape, f32) for a in ws] * 4,
        compiler_params=_params("parallel"))(*ws, *ms, *vs, *[g for g_list in gs for g in g_list])
    return [outs[k * n:(k + 1) * n] for k in range(4)]


def _adamw_update(w, g, m, v):
    m = ADAM_B1 * m + (1.0 - ADAM_B1) * g
    v = ADAM_B2 * v + (1.0 - ADAM_B2) * (g * g)
    m_hat = m / ADAM_C1
    v_hat = v / ADAM_C2
    return -ADAM_LR * (m_hat / (jnp.sqrt(v_hat) + ADAM_EPS) + ADAM_WD * w), m, v


def adamw_small(ws, gs, ms, vs, *, name):
    n = len(ws)

    def body(*refs):
        for i in range(n):
            w_ref, g_ref, m_ref, v_ref = (refs[k * n + i] for k in range(4))
            d_ref, nm_ref, nv_ref = (refs[(4 + k) * n + i] for k in range(3))
            d_ref[...], nm_ref[...], nv_ref[...] = _adamw_update(w_ref[...], g_ref[...], m_ref[...], v_ref[...])

    outs = pl.pallas_call(
        body, name=name, out_shape=[jax.ShapeDtypeStruct(a.shape, f32) for a in ws] * 3)(*ws, *gs, *ms, *vs)
    return outs[:n], outs[n:2 * n], outs[2 * n:]


def _place():
    return lax.axis_index("x"), lax.axis_index("y"), lax.axis_index("c")


def gather_shards(bufs, *, name, split):
    n = len(bufs)

    def body(*refs):
        bufs_ = refs[:n]
        isend, irecv, dsend, drecv = refs[2 * n:]
        x, y, c = _place()
        k = 2 * x + y
        peers = [(1 - x, y, c), (x, 1 - y, c), (1 - x, 1 - y, c)]
        peer_k = [2 * (1 - x) + y, 2 * x + (1 - y), 2 * (1 - x) + (1 - y)]

        def slab(a, q, h):
            if not split[a]:
                return bufs_[a].at[q]
            half = bufs_[a].shape[1] // 2
            return bufs_[a].at[q, pl.ds(pl.multiple_of(h * half, 16), half)]

        def ici(a, j, q):
            return pltpu.make_async_remote_copy(
                src_ref=slab(a, q, c), dst_ref=slab(a, q, c), send_sem=isend.at[3 * a + j], recv_sem=irecv.at[3 * a + j],
                device_id=peers[j], device_id_type=MESH)

        def d2d(a, j, h):
            return pltpu.make_async_remote_copy(
                src_ref=slab(a, peer_k[j], h), dst_ref=slab(a, peer_k[j], h), send_sem=dsend.at[3 * a + j],
                recv_sem=drecv.at[3 * a + j], device_id=(x, y, 1 - c), device_id_type=MESH)

        for a in range(n):
            for j in range(3):
                ici(a, j, k).start()
        for a in range(n):
            for j in range(3):
                ici(a, j, peer_k[j]).wait_recv()
                if split[a]:
                    d2d(a, j, c).start()
        for a in range(n):
            for j in range(3):
                if split[a]:
                    d2d(a, j, 1 - c).wait_recv()
        for a in range(n):
            for j in range(3):
                ici(a, j, k).wait_send()
                if split[a]:
                    d2d(a, j, c).wait_send()

    return pl.pallas_call(
        body, name=name, in_specs=[ANY] * n, out_specs=[ANY] * n,
        out_shape=[jax.ShapeDtypeStruct(b.shape, b.dtype) for b in bufs],
        input_output_aliases={i: i for i in range(n)},
        scratch_shapes=[pltpu.SemaphoreType.DMA((3 * n,))] * 4)(*bufs)


HBM = pl.BlockSpec(memory_space=pltpu.HBM)
SEM = pl.BlockSpec(memory_space=pltpu.SEMAPHORE)
DATAFLOW = pltpu.SideEffectType.DATAFLOW_SIDE_EFFECTING


def _chip_peers():
    x, y, c = _place()
    return 2 * x + y, [(1 - x, y, c), (x, 1 - y, c), (1 - x, 1 - y, c)], [2 * (1 - x) + y, 2 * x + (1 - y), 2 * (1 - x) + (1 - y)]


def gather_start(bufs, groups, after, *, name):
    n = len(bufs)
    ng = len(groups)

    def body(*refs):
        ins = refs[:n]
        sends, recvs = refs[2 * n + 1:2 * n + 1 + ng], refs[2 * n + 1 + ng:2 * n + 1 + 2 * ng]
        token = refs[-1]
        k, peers, _ = _chip_peers()
        for gi, grp in enumerate(groups):
            for pos, a in enumerate(grp):
                for j in range(3):
                    pltpu.make_async_remote_copy(
                        src_ref=ins[a].at[k], dst_ref=ins[a].at[k], send_sem=sends[gi].at[3 * pos + j],
                        recv_sem=recvs[gi].at[3 * pos + j], device_id=peers[j], device_id_type=MESH).start()
        token[...] = jnp.zeros_like(token)

    sems = [pltpu.SemaphoreType.DMA((3 * len(grp),)) for grp in groups]
    outs = pl.pallas_call(
        body, name=name, in_specs=[HBM] * n + [ANY],
        out_specs=[HBM] * n + [SEM] * (2 * ng) + [pl.BlockSpec(memory_space=pltpu.VMEM)],
        out_shape=[pltpu.HBM(b.shape, b.dtype) for b in bufs] + sems + sems + [jax.ShapeDtypeStruct((8, LANES), f32)],
        input_output_aliases={i: i for i in range(n)},
        compiler_params=pltpu.CompilerParams(has_side_effects=DATAFLOW))(
            *[pltpu.with_memory_space_constraint(b, pltpu.HBM) for b in bufs], after)
    return outs[:n], outs[n:n + ng], outs[n + ng:n + 2 * ng], outs[-1]


def gather_wait(bufs, send_sems, recv_sems, after, *, name):
    n = len(bufs)

    def body(*refs):
        ins = refs[:n]
        send, recv = refs[n], refs[n + 1]
        k, peers, peer_k = _chip_peers()
        for a in range(n):
            for j in range(3):
                copy = pltpu.make_async_remote_copy(
                    src_ref=ins[a].at[k], dst_ref=ins[a].at[peer_k[j]], send_sem=send.at[3 * a + j],
                    recv_sem=recv.at[3 * a + j], device_id=peers[j], device_id_type=MESH)
                copy.wait_send()
                copy.wait_recv()

    return pl.pallas_call(
        body, name=name, in_specs=[HBM] * n + [SEM, SEM, ANY], out_specs=[HBM] * n,
        out_shape=[pltpu.HBM(b.shape, b.dtype) for b in bufs],
        input_output_aliases={i: i for i in range(n)},
        compiler_params=pltpu.CompilerParams(has_side_effects=DATAFLOW))(*bufs, send_sems, recv_sems, after)


def sibling_exchange(arrays, *, name):
    n = len(arrays)

    def body(*refs):
        ins, outs = refs[:n], refs[n:2 * n]
        send, recv = refs[2 * n:]
        x, y, c = _place()

        def copy(a):
            return pltpu.make_async_remote_copy(
                src_ref=ins[a], dst_ref=outs[a], send_sem=send.at[a], recv_sem=recv.at[a],
                device_id=(x, y, 1 - c), device_id_type=MESH)

        for a in range(n):
            copy(a).start()
        for a in range(n):
            copy(a).wait_recv()
        for a in range(n):
            copy(a).wait_send()

    return pl.pallas_call(
        body, name=name, in_specs=[ANY] * n, out_specs=[ANY] * n,
        out_shape=[jax.ShapeDtypeStruct(a.shape, a.dtype) for a in arrays],
        scratch_shapes=[pltpu.SemaphoreType.DMA((n,)), pltpu.SemaphoreType.DMA((n,))])(*arrays)


ALL_MASKS = [(mx, my, mc) for mx in (0, 1) for my in (0, 1) for mc in (0, 1)][1:]


def _scatter_copies(srcs, lands, ev, send, recv, esend, erecv):
    x, y, c = _place()
    me = 4 * x + 2 * y + c
    k, peers, peer_k = _chip_peers()
    out = []
    for a in range(len(srcs)):
        for j in range(3):
            out.append(pltpu.make_async_remote_copy(
                src_ref=srcs[a].at[peer_k[j]], dst_ref=lands[a].at[j], send_sem=send.at[3 * a + j],
                recv_sem=recv.at[3 * a + j], device_id=peers[j], device_id_type=MESH))
    start_ev, wait_ev = [], []
    if ev is not None:
        for j, (mx, my, mc) in enumerate(ALL_MASKS):
            peer = (x ^ mx, y ^ my, c ^ mc)
            start_ev.append(pltpu.make_async_remote_copy(
                src_ref=ev.at[me], dst_ref=ev.at[me], send_sem=esend.at[j], recv_sem=erecv.at[j],
                device_id=peer, device_id_type=MESH))
            wait_ev.append(pltpu.make_async_remote_copy(
                src_ref=ev.at[me], dst_ref=ev.at[me ^ (4 * mx + 2 * my + mc)], send_sem=esend.at[j],
                recv_sem=erecv.at[j], device_id=peer, device_id_type=MESH))
    return out, start_ev, wait_ev


def chip_scatter_start(arrays, everyone, after, *, name):
    n = len(arrays)
    ne = 0 if everyone is None else 1
    lands = [pltpu.with_memory_space_constraint(lax.empty((3,) + a.shape[1:], a.dtype), pltpu.HBM) for a in arrays]

    def body(*refs):
        srcs, lands_ = refs[:n], refs[n:2 * n]
        ev = refs[2 * n] if ne else None
        sems = refs[2 * n + ne + 1 + 2 * n + ne:-1]
        send, recv = sems[0], sems[1]
        esend, erecv = (sems[2], sems[3]) if ne else (None, None)
        copies, start_ev, _ = _scatter_copies(srcs, lands_, ev, send, recv, esend, erecv)
        for cp in start_ev + copies:
            cp.start()
        refs[-1][...] = jnp.zeros_like(refs[-1])

    sem_shapes = [pltpu.SemaphoreType.DMA((3 * n,))] * 2 + [pltpu.SemaphoreType.DMA((7,))] * (2 * ne)
    bufs = list(arrays) + lands + ([everyone] if ne else [])
    outs = pl.pallas_call(
        body, name=name, in_specs=[HBM] * len(bufs) + [ANY],
        out_specs=[HBM] * len(bufs) + [SEM] * len(sem_shapes) + [pl.BlockSpec(memory_space=pltpu.VMEM)],
        out_shape=[pltpu.HBM(b.shape, b.dtype) for b in bufs] + sem_shapes + [jax.ShapeDtypeStruct((8, LANES), f32)],
        input_output_aliases={i: i for i in range(len(bufs))},
        compiler_params=pltpu.CompilerParams(has_side_effects=DATAFLOW))(
            *[pltpu.with_memory_space_constraint(b, pltpu.HBM) for b in bufs], after)
    return (n, ne, outs[:-1]), outs[-1]


def chip_scatter_wait(state, after, *, name):
    n, ne, held = state
    nb = 2 * n + ne
    bufs, sems = held[:nb], held[nb:]

    def body(*refs):
        srcs, lands_ = refs[:n], refs[n:2 * n]
        ev = refs[2 * n] if ne else None
        sems_ = refs[nb:nb + len(sems)]
        esend, erecv = (sems_[2], sems_[3]) if ne else (None, None)
        copies, _, wait_ev = _scatter_copies(srcs, lands_, ev, sems_[0], sems_[1], esend, erecv)
        for cp in wait_ev + copies:
            cp.wait_send()
            cp.wait_recv()

    outs = pl.pallas_call(
        body, name=name, in_specs=[HBM] * nb + [SEM] * len(sems) + [ANY], out_specs=[HBM] * nb,
        out_shape=[pltpu.HBM(b.shape, b.dtype) for b in bufs],
        input_output_aliases={i: i for i in range(nb)},
        compiler_params=pltpu.CompilerParams(has_side_effects=DATAFLOW))(*bufs, *sems, after)
    return outs[n:2 * n], (outs[2 * n] if ne else None)


def sibling_merge(bufs, *, name):
    n = len(bufs)

    def body(*refs):
        bufs_ = refs[:n]
        send, recv = refs[2 * n:]
        x, y, c = _place()

        def copy(u, h):
            return pltpu.make_async_remote_copy(
                src_ref=bufs_[u].at[h], dst_ref=bufs_[u].at[h], send_sem=send.at[u], recv_sem=recv.at[u],
                device_id=(x, y, 1 - c), device_id_type=MESH)

        for u in range(n):
            copy(u, c).start()
        for u in range(n):
            copy(u, 1 - c).wait_recv()
        for u in range(n):
            copy(u, c).wait_send()

    return pl.pallas_call(
        body, name=name, in_specs=[ANY] * n, out_specs=[ANY] * n,
        out_shape=[jax.ShapeDtypeStruct(b.shape, b.dtype) for b in bufs],
        input_output_aliases={i: i for i in range(n)},
        scratch_shapes=[pltpu.SemaphoreType.DMA((n,)), pltpu.SemaphoreType.DMA((n,))])(*bufs)


def sum_leading(a, *, name):
    n, r, c = a.shape

    def body(a_ref, o_ref):
        acc = a_ref[0]
        for i in range(1, n):
            acc = acc + a_ref[i]
        o_ref[...] = acc

    rb = r // 2 if r % 16 == 0 else r
    return pl.pallas_call(
        body, name=name, grid=(r // rb,), in_specs=[pl.BlockSpec((n, rb, c), lambda i: (0, i, 0))],
        out_specs=pl.BlockSpec((rb, c), lambda i: (i, 0)), out_shape=jax.ShapeDtypeStruct((r, c), f32),
        compiler_params=_params("parallel"))(a)


def _half_rows(shape):
    return shape[1] // 2 // 2


def rs_cast_other_half(grads, c_arr, *, name):
    n = len(grads)

    def body(c_ref, *refs):
        for i_ref, o_ref in zip(refs[:n], refs[n:]):
            o_ref[...] = i_ref[...].astype(bf16)

    in_specs = [pl.BlockSpec((1, _half_rows(g.shape), g.shape[2]), lambda s, r, c_ref: (s, (1 - c_ref[0]) * 2 + r, 0))
                for g in grads]
    out_specs = [pl.BlockSpec((1, _half_rows(g.shape), g.shape[2]), lambda s, r, c_ref: (s, r, 0)) for g in grads]
    return pl.pallas_call(
        body, name=name,
        grid_spec=pltpu.PrefetchScalarGridSpec(num_scalar_prefetch=1, grid=(N_SHARDS, 2),
                                               in_specs=in_specs, out_specs=out_specs),
        out_shape=[jax.ShapeDtypeStruct((N_SHARDS, g.shape[1] // 2, g.shape[2]), bf16) for g in grads],
        compiler_params=_params("parallel", "parallel"))(c_arr, *grads)


def rs_add_sibling(grads, recvd, ck_arr, *, name):
    n = len(grads)

    def body(ck_ref, *refs):
        s = pl.program_id(1)
        for u in range(n):
            g_ref, r_ref = refs[u], refs[n + u]
            qb_ref, own_ref = refs[2 * n + u], refs[3 * n + u]
            q = g_ref[0] + r_ref[0].astype(f32)
            qb_ref[0] = q.astype(bf16)

            @pl.when(s == ck_ref[1])
            def _(own_ref=own_ref, q=q):
                own_ref[...] = q

    in_specs = [pl.BlockSpec((1, _half_rows(g.shape), g.shape[2]), lambda r, s, ck: (s, ck[0] * 2 + r, 0)) for g in grads]
    in_specs += [pl.BlockSpec((1, _half_rows(g.shape), g.shape[2]), lambda r, s, ck: (s, r, 0)) for g in grads]
    out_specs = [pl.BlockSpec((1, _half_rows(g.shape), g.shape[2]), lambda r, s, ck: (s, r, 0)) for g in grads]
    out_specs += [pl.BlockSpec((_half_rows(g.shape), g.shape[2]), lambda r, s, ck: (r, 0)) for g in grads]
    outs = pl.pallas_call(
        body, name=name,
        grid_spec=pltpu.PrefetchScalarGridSpec(num_scalar_prefetch=1, grid=(2, N_SHARDS),
                                               in_specs=in_specs, out_specs=out_specs),
        out_shape=[jax.ShapeDtypeStruct((N_SHARDS, g.shape[1] // 2, g.shape[2]), bf16) for g in grads]
        + [jax.ShapeDtypeStruct((g.shape[1] // 2, g.shape[2]), f32) for g in grads],
        compiler_params=_params("parallel", "arbitrary"))(ck_arr, *grads, *recvd)
    return outs[:n], outs[n:]


def rs_sum_chips(owns, recvd, ck_arr, *, name):
    n = len(owns)

    def body(ck_ref, *refs):
        for u in range(n):
            own_ref, r_ref, o_ref = refs[u], refs[n + u], refs[2 * n + u]
            o_ref[0] = ((own_ref[...] + r_ref[0].astype(f32)) + r_ref[1].astype(f32)) + r_ref[2].astype(f32)

    in_specs = [pl.BlockSpec((o.shape[0] // 2, o.shape[1]), lambda r, ck: (r, 0)) for o in owns]
    in_specs += [pl.BlockSpec((3, o.shape[0] // 2, o.shape[1]), lambda r, ck: (0, r, 0)) for o in owns]
    out_specs = [pl.BlockSpec((1, o.shape[0] // 2, o.shape[1]), lambda r, ck: (ck[0], r, 0)) for o in owns]
    return pl.pallas_call(
        body, name=name,
        grid_spec=pltpu.PrefetchScalarGridSpec(num_scalar_prefetch=1, grid=(2,), in_specs=in_specs, out_specs=out_specs),
        out_shape=[jax.ShapeDtypeStruct((2,) + o.shape, f32) for o in owns],
        compiler_params=_params("parallel"))(ck_arr, *owns, *recvd)


SMALL = ("a_norm", "a_v_norm", "a_w_s", "a_b_s", "f_norm", "f_conv_w", "f_conv_b", "kv_norm", "k_norm",
         "b_norm", "b_q_norm", "b_sinks")
BIG = ("a_w_in", "a_w_out", "f_w_in", "f_w_out", "w_kv", "b_w_q", "b_w_o")
PACK_COLS = 1024
PACK_ROWS = 8 * N_STEPS


def _pack(parts, rows=PACK_ROWS):
    flat = jnp.concatenate([p.reshape(-1).astype(f32) for p in parts])
    pad = (-flat.shape[0]) % (rows * PACK_COLS)
    return jnp.pad(flat, (0, pad)).reshape(-1, PACK_COLS)


def _unpack(packed, shapes):
    flat = packed.reshape(-1)
    out, off = [], 0
    for s in shapes:
        size = math.prod(s)
        out.append(flat[off:off + size].reshape(s))
        off += size
    return out


def _ffn_fwd(x, g, h, r, w_in4, conv_w, conv_b, f, tag):
    wg, wu = conv_w[:, :f], conv_w[:, f:]
    bg, bu = conv_b[None, :f], conv_b[None, f:]
    pg, pu, gate, up, a = ffn_in_fused(h, w_in4, wg, wu, bg, bu, name=f"ffn{tag}_in")
    return a, (x, g, h, r, pg, pu, gate, up, a, wg, wu)


def _ffn_bwd(dy, saved, w_in4, w_out, c_arr, tag):
    x, g, h, r, pg, pu, gate, up, a, wg, wu = saved
    f = w_out.shape[0]
    d_w_out = mm_tn(a, [dy], c_arr, name=f"ffn{tag}_dwout", n_s=w_out.shape[1], shard_rows=f // N_SHARDS, tki=f // 2)
    dpg, dpu, sg, su = ffn_gate_bwd(dy, w_out, pg, pu, gate, up, wg, wu, name=f"ffn{tag}_dgate")
    d_w_in = mm_tn(h, [dpg, dpu], c_arr, name=f"ffn{tag}_dwin", n_s=w_in4.shape[2], shard_rows=h.shape[1])
    dx, dg = mm_nt_rms_bwd([dpg, dpu], w_in4, x, r, g, dy, name=f"ffn{tag}_dh")
    d_conv_w = jnp.concatenate([sg[0:3], su[0:3]], axis=1)
    d_conv_b = jnp.concatenate([sg[3], su[3]], axis=0)
    return dx, dg, d_w_in, d_conv_w, d_conv_b, d_w_out


def _rs_front(pairs, c_arr, tag):
    units = [full.reshape(N_SHARDS, -1, full.shape[-1]) for full, _ in pairs]
    from_sib = sibling_exchange([half for _, half in pairs], name=f"rs_sibling{tag}")
    return rs_add_sibling(units, from_sib, c_arr, name=f"rs_add{tag}")


def _rs_back(own, from_chips, c_arr, tag):
    halves = rs_sum_chips(list(own), list(from_chips), c_arr, name=f"rs_sum{tag}")
    return [m.reshape(-1, m.shape[2]) for m in sibling_merge(list(halves), name=f"rs_merge{tag}")]


def kernel(x, a_norm, a_w_in, a_v_norm, a_w_s, a_b_s, a_w_out, f_norm, f_w_in, f_conv_w, f_conv_b, f_w_out, kv_norm, w_kv, k_norm, b_norm, b_w_q, b_q_norm, b_sinks, b_w_o, loss_target, m_a_norm, m_a_w_in, m_a_v_norm, m_a_w_s, m_a_b_s, m_a_w_out, m_f_norm, m_f_w_in, m_f_conv_w, m_f_conv_b, m_f_w_out, m_kv_norm, m_w_kv, m_k_norm, m_b_norm, m_b_w_q, m_b_q_norm, m_b_sinks, m_b_w_o, v_a_norm, v_a_w_in, v_a_v_norm, v_a_w_s, v_a_b_s, v_a_w_out, v_f_norm, v_f_w_in, v_f_conv_w, v_f_conv_b, v_f_w_out, v_kv_norm, v_w_kv, v_k_norm, v_b_norm, v_b_w_q, v_b_q_norm, v_b_sinks, v_b_w_o):
    args = dict(locals())
    weights = {n: args[n] for n in SMALL + BIG}
    moms = {n: args["m_" + n] for n in SMALL + BIG}
    vars_ = {n: args["v_" + n] for n in SMALL + BIG}
    t, d = x.shape[1], x.shape[2]
    xi, yi, ci = _place()
    chip = 2 * xi + yi

    big_local = [a_w_in[0], a_w_out[0], f_w_in, f_w_out, w_kv, b_w_q[0], b_w_o[0]]
    c_arr = jnp.stack([ci, chip]).astype(jnp.int32)
    k_arr = jnp.stack([chip]).astype(jnp.int32)
    b_ain, b_aout, b_fin0, b_fin1, b_fout0, b_fout1, b_kv, b_q, b_o = cast_into_slot(big_local, k_arr, name="cast_weights")
    small_cols = _pack([a_norm, a_v_norm, f_conv_w], rows=8)
    b_small = lax.dynamic_update_slice(jnp.zeros((N_SHARDS,) + small_cols.shape, f32), small_cols[None], (chip, 0, 0))
    g_small, w_a_in, g_a_w_out = gather_shards([b_small, b_ain, b_aout], name="gather_first", split=[False, True, True])
    later, send_sems, recv_sems, token = gather_start([b_fin0, b_fout0, b_kv, b_q, b_o, b_fin1, b_fout1],
                                                      [[0], [1, 2, 3, 4], [5, 6]], g_small, name="gather_start")
    ns_cols = a_norm.shape[1]
    nf_cols = f_conv_w.shape[2]
    parts = [_unpack(g_small[k], [a_norm.shape, a_v_norm.shape, f_conv_w.shape]) for k in range(N_SHARDS)]
    a_norm_f = jnp.concatenate([p[0] for p in parts], axis=1) + token[0, 0]
    a_v_norm_f = jnp.concatenate([p[1] for p in parts], axis=1)
    conv_w_f = jnp.concatenate([p[2] for p in parts], axis=2)

    x0 = x[0]
    tril = jnp.tril(jnp.ones((CHUNK, CHUNK), dtype=bool))
    wc = jnp.where(tril[None], a_w_s[0], 0.0).astype(bf16)
    bt = a_b_s[0].T
    kg2 = jnp.tile(k_norm, 2)[None]
    qg2 = jnp.tile(b_q_norm[0], 2)[None]

    (h_a,), r_a = rms_fwd(x0, [a_norm_f], name="a_norm")
    zu = mm_nn(h_a, w_a_in, name="a_in_u", s0=0, ns=2)
    zv = mm_nn(h_a, w_a_in, name="a_in_v", s0=2, ns=2)
    y_a = sgu_gate_fwd(zu, zv, a_v_norm_f, wc, bt, name="a_gate")
    w_a_out = g_a_w_out.reshape(1, -1, d)
    f = f_w_out.shape[1] * N_SHARDS
    x1, (h_f0,), r_f0 = mm_residual(y_a, w_a_out[0], x0, name="a_out", gains=[f_norm[0:1]])
    (g_fin0,) = gather_wait(later[0:1], send_sems[0], recv_sems[0], x1, name="gather_wait_0")
    w_f_in = [g_fin0, None]
    a0, ffn0 = _ffn_fwd(x1, f_norm[0:1], h_f0, r_f0, w_f_in[0], conv_w_f[0], f_conv_b[0], f, "0")
    g_fout0, g_w_kv, g_b_w_q, g_b_w_o = gather_wait(later[1:5], send_sems[1], recv_sems[1], a0, name="gather_wait_1")
    w_f_out = [g_fout0.reshape(-1, d), None]
    w_kv_f = g_w_kv.reshape(1, d, -1)
    w_q_f = g_b_w_q.reshape(1, d, -1)
    w_o_f = g_b_w_o.reshape(1, -1, d)
    x2, (h_k, h_q), r_b = mm_residual(a0, w_f_out[0], x1, name="ffn0_out", gains=[kv_norm[None], b_norm])
    kv = mm_nn(h_k, w_kv_f, name="kv_proj")
    k2, v2 = kv_post_fwd(kv, kg2, name="kv_post")
    qp = mm_nn(h_q, w_q_f, name="q_proj")
    qn = q_norm_fwd(qp, qg2, name="q_norm", scale=HEAD_DIM ** -0.5)
    o = attn_fwd(qn, k2, v2, b_sinks[0], name="attn")
    x3, (h_f1,), r_f1 = mm_residual(o, w_o_f[0], x2, name="o_proj", gains=[f_norm[1:2]])
    g_fin1, g_fout1 = gather_wait(later[5:7], send_sems[2], recv_sems[2], x3, name="gather_wait_2")
    w_f_in[1] = g_fin1
    w_f_out[1] = g_fout1.reshape(-1, d)
    a1, ffn1 = _ffn_fwd(x3, f_norm[1:2], h_f1, r_f1, w_f_in[1], conv_w_f[1], f_conv_b[1], f, "1")
    dx4, sq = mm_residual(a1, w_f_out[1], x3, name="ffn1_out", target=loss_target[0])
    loss_part = (0.5 * jnp.sum(sq) / d).reshape(1)

    proj_rows = d // N_SHARDS
    dx3, d_fn1, d_fwin1, d_cw1, d_cb1, d_fwout1 = _ffn_bwd(dx4, ffn1, w_f_in[1], w_f_out[1], c_arr, "1")
    do = mm_nt([dx3], w_o_f, name="o_proj_dx")
    d_w_o = mm_tn(o, [dx3], c_arr, name="o_proj_dw", n_s=d, shard_rows=o.shape[1] // N_SHARDS)
    dqn, dk2, dv2, dsink = attn_bwd(qn, k2, v2, do, b_sinks[0], name="attn_bwd")
    dqp, dqg = q_norm_bwd(dqn, qp, qg2, name="q_norm_bwd", scale=HEAD_DIM ** -0.5)
    dkv, dkg = kv_post_bwd(dk2, dv2, kv, kg2, name="kv_post_bwd")
    d_w_q = mm_tn(h_q, [dqp], c_arr, name="q_proj_dw", n_s=w_q_f.shape[2], shard_rows=proj_rows)
    d_w_kv = mm_tn(h_k, [dkv], c_arr, name="kv_proj_dw", n_s=w_kv_f.shape[2], shard_rows=proj_rows)
    dh_k = mm_nt([dkv], w_kv_f, name="kv_proj_dx")
    dx2, d_bn, d_kvn = mm_nt_rms_bwd([dqp], w_q_f, x2, r_b, b_norm, dx3, name="q_proj_dx", extra=(dh_k, kv_norm[None]))
    chip_bf1, own1 = _rs_front([d_fwin1, d_fwout1, d_w_kv, d_w_q, d_w_o], c_arr, "1")
    scatter1, token1 = chip_scatter_start(list(chip_bf1), None, dx2, name="rs_chips_start1")
    ffn0 = ffn0[:9] + (ffn0[9] + token1[0, 0],) + ffn0[10:]
    dx1, d_fn0, d_fwin0, d_cw0, d_cb0, d_fwout0 = _ffn_bwd(dx2, ffn0, w_f_in[0], w_f_out[0], c_arr, "0")
    chip_bf2, own2 = _rs_front([d_fwin0, d_fwout0], c_arr, "2")
    scatter2, token2 = chip_scatter_start(list(chip_bf2), None, dx1, name="rs_chips_start2")
    a_v_norm_f = a_v_norm_f + token2[0, 0]
    dy_a = mm_nt([dx1], w_a_out, name="a_out_dx")
    d_w_aout = mm_tn(y_a, [dx1], c_arr, name="a_out_dw", n_s=d, shard_rows=y_a.shape[1] // N_SHARDS)
    dzu, dzv, d_avn, d_ws, d_bt = sgu_gate_bwd(zu, zv, dy_a, a_v_norm_f, wc, bt, name="a_gate_bwd")
    d_w_ain = mm_tn(h_a, [dzu, dzv], c_arr, name="a_in_dw", n_s=w_a_in.shape[2], shard_rows=d)
    dx0, d_an = mm_nt_rms_bwd([dzu, dzv], w_a_in, x0, r_a, a_norm_f, dx1, name="a_in_dx")
    grad_x = dx0[None]

    chip_bf3, own3 = _rs_front([d_w_ain, d_w_aout], c_arr, "3")
    d_fn = jnp.concatenate([d_fn0, d_fn1], axis=0)
    d_cw = jnp.stack([d_cw0, d_cw1])
    d_cb = jnp.stack([d_cb0, d_cb1])
    d_kg = (dkg[0, :HEAD_DIM] + dkg[0, HEAD_DIM:])
    d_qg = (dqg[0, :HEAD_DIM] + dqg[0, HEAD_DIM:])[None]
    small_full = [d_an, d_avn, d_ws[None], d_bt.T[None], d_fn, d_cw, d_cb, d_kvn[0], d_kg, d_bn, d_qg,
                  dsink[:, :N_Q_HEADS], loss_part]
    packed = _pack(small_full)
    me = 4 * xi + 2 * yi + ci
    everyone = lax.dynamic_update_slice(lax.empty((N_DEV,) + packed.shape, f32), packed[None], (me, 0, 0))
    scatter3, _ = chip_scatter_start(list(chip_bf3), everyone, chip_bf3[0], name="rs_chips_start3")
    from_chips1, _ = chip_scatter_wait(scatter1, scatter3[2][0], name="rs_chips_wait1")
    fin1, fout1, gkv, gq, go = _rs_back(own1, from_chips1, c_arr, "1")
    from_chips2, _ = chip_scatter_wait(scatter2, fin1, name="rs_chips_wait2")
    fin0, fout0 = _rs_back(own2, from_chips2, c_arr, "2")
    late = ("f_w_in", "f_w_out", "w_kv", "b_w_q", "b_w_o")
    res_late = adamw([weights[n] for n in late], [[fin0, fin1], [fout0, fout1], [gkv], [gq], [go]],
                     [moms[n] for n in late], [vars_[n] for n in late], name="adamw_late")
    from_chips3, from_all = chip_scatter_wait(scatter3, res_late[1][2], name="rs_chips_wait3")
    ain, aout = _rs_back(own3, from_chips3, c_arr, "3")
    first = ("a_w_in", "a_w_out")
    res_first = adamw([weights[n] for n in first], [[ain], [aout]], [moms[n] for n in first],
                      [vars_[n] for n in first], name="adamw_first")
    big = {n: tuple(r[i] for r in res_late) for i, n in enumerate(late)}
    big.update({n: tuple(r[i] for r in res_first) for i, n in enumerate(first)})

    full_shapes = [g.shape for g in small_full]
    small_g = _unpack(sum_leading(from_all, name="small_sum"), full_shapes)
    loss = small_g.pop()[0]
    small_g[0] = lax.dynamic_slice_in_dim(small_g[0], chip * ns_cols, ns_cols, axis=1)
    small_g[1] = lax.dynamic_slice_in_dim(small_g[1], chip * ns_cols, ns_cols, axis=1)
    small_g[5] = lax.dynamic_slice_in_dim(small_g[5], chip * nf_cols, nf_cols, axis=2)
    small_shapes = [weights[n].shape for n in SMALL]
    small_g = [g.reshape(s) for g, s in zip(small_g, small_shapes)]
    flat2 = [(math.prod(s[:-1]), s[-1]) for s in small_shapes]
    small_d, small_m, small_v = adamw_small(
        *[[a.reshape(s2) for a, s2 in zip(group, flat2)]
          for group in ([weights[n] for n in SMALL], small_g, [moms[n] for n in SMALL], [vars_[n] for n in SMALL])],
        name="adamw_small")
    small_d, small_m, small_v = ([a.reshape(s) for a, s in zip(group, small_shapes)]
                                 for group in (small_d, small_m, small_v))

    out = {}
    for i, n in enumerate(SMALL):
        out[n] = (small_g[i], small_d[i], small_m[i], small_v[i])
    out.update(big)
    order = ["a_norm", "a_w_in", "a_v_norm", "a_w_s", "a_b_s", "a_w_out", "f_norm", "f_w_in", "f_conv_w", "f_conv_b",
             "f_w_out", "kv_norm", "w_kv", "k_norm", "b_norm", "b_w_q", "b_q_norm", "b_sinks", "b_w_o"]
    return (loss, grad_x, *[out[n][0] for n in order], *[out[n][1] for n in order],
            *[out[n][2] for n in order], *[out[n][3] for n in order])
```

```python
import functools
import math

import jax
import jax.numpy as jnp
from jax import lax
from jax.experimental import pallas as pl
from jax.experimental.pallas import tpu as pltpu

f32 = jnp.float32
bf16 = jnp.bfloat16
MESH = pl.DeviceIdType.MESH
ANY = pl.BlockSpec(memory_space=pl.ANY)

EPS = 1e-6
LANES = 128
CHUNK = 128
HEAD_DIM = 64
N_Q_HEADS = 16
N_KV_HEADS = 4
Q_PER_KV = N_Q_HEADS // N_KV_HEADS
N_SHARDS = 4
N_DEV = 8

ADAM_LR = 0.001
ADAM_B1 = 0.9
ADAM_B2 = 0.999
ADAM_EPS = 1e-08
ADAM_WD = 0.01
ADAM_STEP = 10
ADAM_C1 = 1.0 - ADAM_B1 ** ADAM_STEP
ADAM_C2 = 1.0 - ADAM_B2 ** ADAM_STEP

_INV_SQRT2 = 1.0 / math.sqrt(2.0)
_INV_SQRT2PI = 1.0 / math.sqrt(2.0 * math.pi)


def _params(*sem):
    return pltpu.CompilerParams(dimension_semantics=sem)


def _gelu(z):
    return 0.5 * z * (1.0 + lax.erf(z * _INV_SQRT2))


def _gelu_and_grad(z):
    cdf = 0.5 * (1.0 + lax.erf(z * _INV_SQRT2))
    return z * cdf, cdf + z * jnp.exp(-0.5 * z * z) * _INV_SQRT2PI


def _dot(a, b):
    return jnp.dot(a, b, preferred_element_type=f32)


def _dot_nt(a, b):
    return lax.dot_general(a, b, (((1,), (1,)), ((), ())), preferred_element_type=f32)


def _dot_tn(a, b):
    return lax.dot_general(a, b, (((0,), (0,)), ((), ())), preferred_element_type=f32)


def _dot_split(a, b):
    hi = a.astype(bf16)
    lo = (a - hi.astype(f32)).astype(bf16)
    return _dot(hi, b) + _dot(lo, b)


VMEM_TILE_BUDGET = 40 * 1024 * 1024
MAX_ROW_TILE = 2048


def _row_tile(m, fixed_bytes, row_bytes):
    tm = min(m, MAX_ROW_TILE)
    while tm > 256 and 2 * (fixed_bytes + tm * row_bytes) > VMEM_TILE_BUDGET:
        tm //= 2
    return tm


def _isz(a):
    return jnp.dtype(a.dtype).itemsize


def mm_nn(a, w3, *, name, s0=0, ns=None, add=None, out_dtype=f32):
    m, k = a.shape
    s_all, _, n_s = w3.shape
    ns = s_all if ns is None else ns
    tm = _row_tile(m, k * n_s * 2, k * _isz(a) + n_s * jnp.dtype(out_dtype).itemsize + (0 if add is None else n_s * 4))

    def body(*refs):
        if add is None:
            a_ref, w_ref, o_ref = refs
            acc = _dot(a_ref[...].astype(bf16), w_ref[0])
        else:
            a_ref, w_ref, add_ref, o_ref = refs
            acc = _dot(a_ref[...].astype(bf16), w_ref[0]) + add_ref[...]
        o_ref[...] = acc.astype(out_dtype)

    in_specs = [pl.BlockSpec((tm, k), lambda j, i: (i, 0)),
                pl.BlockSpec((1, k, n_s), lambda j, i: (s0 + j, 0, 0))]
    args = [a, w3]
    if add is not None:
        in_specs.append(pl.BlockSpec((tm, n_s), lambda j, i: (i, j)))
        args.append(add)
    return pl.pallas_call(
        body, name=name, grid=(ns, m // tm), in_specs=in_specs,
        out_specs=pl.BlockSpec((tm, n_s), lambda j, i: (i, j)),
        out_shape=jax.ShapeDtypeStruct((m, ns * n_s), out_dtype),
        compiler_params=_params("parallel", "parallel"))(*args)


def mm_nt(a_list, w3, *, name, tko=None, add=None, out_dtype=f32):
    s_all, k_out, n_s = w3.shape
    m = a_list[0].shape[0]
    na = len(a_list)
    spa = s_all // na
    tko = k_out if tko is None else tko
    tm = _row_tile(m, tko * n_s * 2, na * n_s * _isz(a_list[0]) + tko * 4 * (1 if add is None else 2))

    def body(*refs):
        a_refs = refs[:na]
        w_ref = refs[na]
        o_ref = refs[-1]
        s = pl.program_id(2)

        @pl.when(s == 0)
        def _():
            if add is None:
                o_ref[...] = jnp.zeros_like(o_ref)
            else:
                o_ref[...] = refs[na + 1][...]

        for idx in range(na):
            @pl.when(s // spa == idx)
            def _(idx=idx):
                o_ref[...] += _dot_nt(a_refs[idx][...].astype(bf16), w_ref[0])

    def a_map(idx):
        return lambda ko, i, s: (i, jnp.clip(s - idx * spa, 0, spa - 1))

    in_specs = [pl.BlockSpec((tm, n_s), a_map(idx)) for idx in range(na)]
    in_specs.append(pl.BlockSpec((1, tko, n_s), lambda ko, i, s: (s, ko, 0)))
    args = list(a_list) + [w3]
    if add is not None:
        in_specs.append(pl.BlockSpec((tm, tko), lambda ko, i, s: (i, ko)))
        args.append(add)
    return pl.pallas_call(
        body, name=name, grid=(k_out // tko, m // tm, s_all), in_specs=in_specs,
        out_specs=pl.BlockSpec((tm, tko), lambda ko, i, s: (i, ko)),
        out_shape=jax.ShapeDtypeStruct((m, k_out), out_dtype),
        compiler_params=_params("parallel", "parallel", "arbitrary"))(*args)


def mm_tn(a, b_list, c_arr, *, name, n_s, shard_rows, tki=None):
    m, k_in = a.shape
    na = len(b_list)
    s_all = sum(b.shape[1] for b in b_list) // n_s
    spa = s_all // na
    tki = k_in if tki is None else tki
    tm = _row_tile(m, tki * n_s * 4, tki * _isz(a) + na * n_s * _isz(b_list[0]))

    nsteps = m // tm
    per_blk = tki // shard_rows
    half = shard_rows // 2

    def body(c_ref, *refs):
        a_ref = refs[0]
        b_refs = refs[1:1 + na]
        o_ref, ob_ref = refs[-2], refs[-1]
        s = pl.program_id(0)
        r = pl.program_id(2)

        @pl.when(r == 0)
        def _():
            o_ref[...] = jnp.zeros_like(o_ref)

        for idx in range(na):
            @pl.when(s // spa == idx)
            def _(idx=idx):
                o_ref[0] += _dot_tn(a_ref[...].astype(bf16), b_refs[idx][...].astype(bf16))

        @pl.when(r == nsteps - 1)
        def _():
            for q in range(per_blk):
                start = pl.multiple_of(q * shard_rows + (1 - c_ref[0]) * half, 16)
                ob_ref[q] = o_ref[0, pl.ds(start, half), :].astype(bf16)

    def b_map(idx):
        def index(s, ki, r, c_ref):
            active = (s // spa) == idx
            return (jnp.where(active, r, 0), jnp.clip(s - idx * spa, 0, spa - 1))
        return index

    in_specs = [pl.BlockSpec((tm, tki), lambda s, ki, r, c_ref: (r, ki))]
    in_specs += [pl.BlockSpec((tm, n_s), b_map(idx)) for idx in range(na)]
    n_blk = k_in // tki
    return pl.pallas_call(
        body, name=name,
        grid_spec=pltpu.PrefetchScalarGridSpec(
            num_scalar_prefetch=1, grid=(s_all, n_blk, nsteps), in_specs=in_specs,
            out_specs=[pl.BlockSpec((1, tki, n_s), lambda s, ki, r, c_ref: (s, ki, 0)),
                       pl.BlockSpec((per_blk, half, n_s), lambda s, ki, r, c_ref: (s * n_blk + ki, 0, 0))]),
        out_shape=[jax.ShapeDtypeStruct((s_all, k_in, n_s), f32),
                   jax.ShapeDtypeStruct((s_all * k_in // shard_rows, half, n_s), bf16)],
        compiler_params=_params("parallel", "parallel", "arbitrary"))(c_arr, a, *b_list)


def mm_nt_rms_bwd(a_list, w3, x, r, g, dx_in, *, name, extra=None):
    s_all, d, n_s = w3.shape
    m = a_list[0].shape[0]
    na = len(a_list)
    spa = s_all // na
    ne = 0 if extra is None else 1
    tm = _row_tile(m, d * n_s * 2, na * n_s * _isz(a_list[0]) + d * 4 * (4 + ne))

    def body(*refs):
        a_refs, w_ref = refs[:na], refs[na]
        x_ref, r_ref, g_ref, dxin_ref = refs[na + 1:na + 5]
        dh2_ref, g2_ref = (refs[na + 5], refs[na + 6]) if ne else (None, None)
        outs = refs[na + 5 + 2 * ne:]
        dx_ref, dg_ref = outs[0], outs[1]
        dg2_ref = outs[2] if ne else None
        acc_ref = outs[-1]
        i, s = pl.program_id(0), pl.program_id(1)

        @pl.when(s == 0)
        def _():
            acc_ref[...] = jnp.zeros_like(acc_ref)

        for idx in range(na):
            @pl.when(s // spa == idx)
            def _(idx=idx):
                acc_ref[...] += _dot_nt(a_refs[idx][...].astype(bf16), w_ref[0])

        @pl.when(s == s_all - 1)
        def _():
            rv = r_ref[...]
            xh = x_ref[...] * rv
            total = dxin_ref[...]
            pairs = [(acc_ref[...], g_ref, dg_ref)] + ([(dh2_ref[...], g2_ref, dg2_ref)] if ne else [])
            for dh, gain_ref, dgain_ref in pairs:
                part = jnp.sum(dh * xh, axis=0, keepdims=True)

                @pl.when(i == 0)
                def _(dgain_ref=dgain_ref, part=part):
                    dgain_ref[...] = part

                @pl.when(i > 0)
                def _(dgain_ref=dgain_ref, part=part):
                    dgain_ref[...] += part

                tg = dh * gain_ref[...]
                total = total + rv * (tg - xh * jnp.mean(tg * xh, axis=1, keepdims=True))
            dx_ref[...] = total

    def a_map(idx):
        return lambda i, s: (i, jnp.clip(s - idx * spa, 0, spa - 1))

    row = pl.BlockSpec((tm, d), lambda i, s: (i, 0))
    vec = pl.BlockSpec((1, d), lambda i, s: (0, 0))
    in_specs = [pl.BlockSpec((tm, n_s), a_map(idx)) for idx in range(na)]
    in_specs += [pl.BlockSpec((1, d, n_s), lambda i, s: (s, 0, 0)), row, pl.BlockSpec((tm, 1), lambda i, s: (i, 0)), vec, row]
    args = list(a_list) + [w3, x, r, g, dx_in]
    if ne:
        in_specs += [row, vec]
        args += list(extra)
    outs = pl.pallas_call(
        body, name=name, grid=(m // tm, s_all), in_specs=in_specs, out_specs=[row] + [vec] * (1 + ne),
        out_shape=[jax.ShapeDtypeStruct((m, d), f32)] + [jax.ShapeDtypeStruct((1, d), f32)] * (1 + ne),
        scratch_shapes=[pltpu.VMEM((tm, d), f32)],
        compiler_params=_params("arbitrary", "arbitrary"))(*args)
    return outs


def mm_residual(a, w, x, *, name, gains=(), target=None):
    m, k = a.shape
    d = w.shape[1]
    ng = len(gains)
    tm = _row_tile(m, k * d * 2, k * _isz(a) + d * 4 * 3 + ng * d * 2)

    def body(*refs):
        a_ref, w_ref, x_ref = refs[:3]
        y = _dot(a_ref[...].astype(bf16), w_ref[...]) + x_ref[...]
        if target is None:
            g_refs = refs[3:3 + ng]
            y_ref = refs[3 + ng]
            h_refs = refs[4 + ng:4 + 2 * ng]
            r_ref = refs[-1]
            y_ref[...] = y
            r = lax.rsqrt(jnp.mean(y * y, axis=1, keepdims=True) + EPS)
            yh = y * r
            for g_ref, h_ref in zip(g_refs, h_refs):
                h_ref[...] = (yh * g_ref[...]).astype(bf16)
            r_ref[...] = r
        else:
            t_ref, dy_ref, s_ref = refs[3:]
            i = pl.program_id(0)
            e = y - t_ref[...]
            dy_ref[...] = e * (1.0 / d)
            part = jnp.sum(e * e, axis=0, keepdims=True)

            @pl.when(i == 0)
            def _():
                s_ref[...] = part

            @pl.when(i > 0)
            def _():
                s_ref[...] += part

    row = pl.BlockSpec((tm, d), lambda i: (i, 0))
    vec = pl.BlockSpec((1, d), lambda i: (0, 0))
    in_specs = [pl.BlockSpec((tm, k), lambda i: (i, 0)), pl.BlockSpec((k, d), lambda i: (0, 0)), row]
    if target is None:
        outs = pl.pallas_call(
            body, name=name, grid=(m // tm,), in_specs=in_specs + [vec] * ng,
            out_specs=[row] * (1 + ng) + [pl.BlockSpec((tm, 1), lambda i: (i, 0))],
            out_shape=[jax.ShapeDtypeStruct((m, d), f32)] + [jax.ShapeDtypeStruct((m, d), bf16)] * ng
            + [jax.ShapeDtypeStruct((m, 1), f32)],
            compiler_params=_params("parallel"))(a, w, x, *gains)
        return outs[0], outs[1:1 + ng], outs[-1]
    return pl.pallas_call(
        body, name=name, grid=(m // tm,), in_specs=in_specs + [row], out_specs=[row, vec],
        out_shape=[jax.ShapeDtypeStruct((m, d), f32), jax.ShapeDtypeStruct((1, d), f32)],
        compiler_params=_params("arbitrary"))(a, w, x, target)


def rms_fwd(x, gains, *, name, tr=512):
    t, d = x.shape
    tr = min(tr, t)
    ng = len(gains)

    def body(*refs):
        x_ref = refs[0]
        g_refs = refs[1:1 + ng]
        h_refs = refs[1 + ng:1 + 2 * ng]
        r_ref = refs[-1]
        xv = x_ref[...]
        r = lax.rsqrt(jnp.mean(xv * xv, axis=1, keepdims=True) + EPS)
        xh = xv * r
        for g_ref, h_ref in zip(g_refs, h_refs):
            h_ref[...] = (xh * g_ref[...]).astype(bf16)
        r_ref[...] = r

    row = pl.BlockSpec((tr, d), lambda i: (i, 0))
    vec = pl.BlockSpec((1, d), lambda i: (0, 0))
    outs = pl.pallas_call(
        body, name=name, grid=(t // tr,), in_specs=[row] + [vec] * ng,
        out_specs=[row] * ng + [pl.BlockSpec((tr, 1), lambda i: (i, 0))],
        out_shape=[jax.ShapeDtypeStruct((t, d), bf16)] * ng + [jax.ShapeDtypeStruct((t, 1), f32)],
        compiler_params=_params("parallel"))(x, *gains)
    return outs[:ng], outs[ng]


def rms_bwd(dh_list, x, r, gains, dx_in, *, name, tr=512):
    t, d = x.shape
    tr = min(tr, t)
    ng = len(gains)

    def body(*refs):
        dh_refs = refs[:ng]
        x_ref, r_ref = refs[ng], refs[ng + 1]
        g_refs = refs[ng + 2:2 * ng + 2]
        dxin_ref = refs[2 * ng + 2]
        dx_ref = refs[2 * ng + 3]
        dg_refs = refs[2 * ng + 4:]
        i = pl.program_id(0)
        rv = r_ref[...]
        xh = x_ref[...] * rv
        acc = dxin_ref[...]
        for dh_ref, g_ref, dg_ref in zip(dh_refs, g_refs, dg_refs):
            dh = dh_ref[...]
            part = jnp.sum(dh * xh, axis=0, keepdims=True)

            @pl.when(i == 0)
            def _(dg_ref=dg_ref, part=part):
                dg_ref[...] = part

            @pl.when(i > 0)
            def _(dg_ref=dg_ref, part=part):
                dg_ref[...] += part

            tg = dh * g_ref[...]
            acc = acc + rv * (tg - xh * jnp.mean(tg * xh, axis=1, keepdims=True))
        dx_ref[...] = acc

    row = pl.BlockSpec((tr, d), lambda i: (i, 0))
    vec = pl.BlockSpec((1, d), lambda i: (0, 0))
    outs = pl.pallas_call(
        body, name=name, grid=(t // tr,),
        in_specs=[row] * ng + [row, pl.BlockSpec((tr, 1), lambda i: (i, 0))] + [vec] * ng + [row],
        out_specs=[row] + [vec] * ng,
        out_shape=[jax.ShapeDtypeStruct((t, d), f32)] + [jax.ShapeDtypeStruct((1, d), f32)] * ng,
        compiler_params=_params("arbitrary"))(*dh_list, x, r, *gains, dx_in)
    return outs[0], outs[1:]


def sgu_gate_fwd(zu, zv, gv, wc, bt, *, name, tr=512):
    t, w = zu.shape
    tr = min(tr, t)
    groups = w // LANES

    def body(zu_ref, zv_ref, gv_ref, wc_ref, bt_ref, y_ref):
        vp = _gelu(zv_ref[...])
        rv = lax.rsqrt(jnp.mean(vp * vp, axis=1, keepdims=True) + EPS)
        vb = (vp * rv * gv_ref[...]).astype(bf16)
        for c in range(tr // CHUNK):
            rows = slice(c * CHUNK, (c + 1) * CHUNK)
            for g in range(groups):
                cols = slice(g * LANES, (g + 1) * LANES)
                sv = _dot(wc_ref[g], vb[rows, cols]) + bt_ref[:, g:g + 1]
                y_ref[rows, cols] = (_gelu(zu_ref[rows, cols]) * sv).astype(bf16)

    row = pl.BlockSpec((tr, w), lambda i: (i, 0))
    return pl.pallas_call(
        body, name=name, grid=(t // tr,),
        in_specs=[row, row, pl.BlockSpec((1, w), lambda i: (0, 0)),
                  pl.BlockSpec((groups, CHUNK, CHUNK), lambda i: (0, 0, 0)),
                  pl.BlockSpec((CHUNK, groups), lambda i: (0, 0))],
        out_specs=row, out_shape=jax.ShapeDtypeStruct((t, w), bf16),
        compiler_params=_params("parallel"))(zu, zv, gv, wc, bt)


def sgu_gate_bwd(zu, zv, dy, gv, wc, bt, *, name, tr=512):
    t, w = zu.shape
    tr = min(tr, t)
    groups = w // LANES
    nsteps = t // tr

    def body(zu_ref, zv_ref, dy_ref, gv_ref, wc_ref, bt_ref,
             dzu_ref, dzv_ref, dgv_ref, dws_ref, dbt_ref, dv_ref, bacc_ref):
        i = pl.program_id(0)

        @pl.when(i == 0)
        def _():
            dgv_ref[...] = jnp.zeros_like(dgv_ref)
            dws_ref[...] = jnp.zeros_like(dws_ref)
            bacc_ref[...] = jnp.zeros_like(bacc_ref)

        vp, vp_grad = _gelu_and_grad(zv_ref[...])
        rv = lax.rsqrt(jnp.mean(vp * vp, axis=1, keepdims=True) + EPS)
        vhat = vp * rv
        vb = (vhat * gv_ref[...]).astype(bf16)
        for c in range(tr // CHUNK):
            rows = slice(c * CHUNK, (c + 1) * CHUNK)
            for g in range(groups):
                cols = slice(g * LANES, (g + 1) * LANES)
                vblk = vb[rows, cols]
                sv = _dot(wc_ref[g], vblk) + bt_ref[:, g:g + 1]
                zub = zu_ref[rows, cols]
                dyb = dy_ref[rows, cols]
                ub, ub_grad = _gelu_and_grad(zub)
                dzu_ref[rows, cols] = (dyb * sv * ub_grad).astype(bf16)
                dsv = dyb * ub
                bacc_ref[:, cols] += dsv
                dsvb = dsv.astype(bf16)
                dv_ref[rows, cols] = _dot_tn(wc_ref[g], dsvb)
                dws_ref[g] += _dot_nt(dsvb, vblk)
        dv = dv_ref[...]
        dgv_ref[...] += jnp.sum(dv * vhat, axis=0, keepdims=True)
        tg = dv * gv_ref[...]
        dvp = rv * (tg - vhat * jnp.mean(tg * vhat, axis=1, keepdims=True))
        dzv_ref[...] = (dvp * vp_grad).astype(bf16)

        @pl.when(i == nsteps - 1)
        def _():
            tt = lax.broadcasted_iota(jnp.int32, (CHUNK, CHUNK), 0)
            ss = lax.broadcasted_iota(jnp.int32, (CHUNK, CHUNK), 1)
            for g in range(groups):
                dws_ref[g] = jnp.where(ss <= tt, dws_ref[g], 0.0)
                dbt_ref[:, g:g + 1] = jnp.sum(bacc_ref[:, g * LANES:(g + 1) * LANES], axis=1, keepdims=True)

    row = pl.BlockSpec((tr, w), lambda i: (i, 0))
    full3 = pl.BlockSpec((groups, CHUNK, CHUNK), lambda i: (0, 0, 0))
    return pl.pallas_call(
        body, name=name, grid=(nsteps,),
        in_specs=[row, row, row, pl.BlockSpec((1, w), lambda i: (0, 0)), full3,
                  pl.BlockSpec((CHUNK, groups), lambda i: (0, 0))],
        out_specs=[row, row, pl.BlockSpec((1, w), lambda i: (0, 0)), full3,
                   pl.BlockSpec((CHUNK, groups), lambda i: (0, 0))],
        out_shape=[jax.ShapeDtypeStruct((t, w), bf16), jax.ShapeDtypeStruct((t, w), bf16),
                   jax.ShapeDtypeStruct((1, w), f32), jax.ShapeDtypeStruct((groups, CHUNK, CHUNK), f32),
                   jax.ShapeDtypeStruct((CHUNK, groups), f32)],
        scratch_shapes=[pltpu.VMEM((tr, w), f32), pltpu.VMEM((CHUNK, w), f32)],
        compiler_params=_params("arbitrary"))(zu, zv, dy, gv, wc, bt)


HALO = 8


def _shift_down(v, halo, k, first):
    r = pltpu.roll(v, k, 0)
    hh = jnp.where(first, 0.0, pltpu.roll(halo, k, 0))
    rid = lax.broadcasted_iota(jnp.int32, (HALO, v.shape[1]), 0)
    head = jnp.where(rid < k, hh, r[0:HALO])
    if v.shape[0] == HALO:
        return head
    return jnp.concatenate([head, r[HALO:]], axis=0)


def _shift_up(v, halo, k, last):
    n = v.shape[0]
    r = pltpu.roll(v, n - k, 0)
    hh = jnp.where(last, 0.0, pltpu.roll(halo, HALO - k, 0))
    rid = lax.broadcasted_iota(jnp.int32, (HALO, v.shape[1]), 0)
    tail = jnp.where(rid >= HALO - k, hh, r[n - HALO:])
    return jnp.concatenate([r[:n - HALO], tail], axis=0)


def _conv(p, halo, w_ref, b_ref, first):
    return (w_ref[2:3, :] * p + w_ref[1:2, :] * _shift_down(p, halo, 1, first)
            + w_ref[0:1, :] * _shift_down(p, halo, 2, first) + b_ref[...])


BF16_ROWS = 16


def ffn_in_fused(h, w_in4, wg, wu, bg, bu, *, name):
    t, k = h.shape
    s_all, _, n_s = w_in4.shape
    half = s_all // 2
    tm = _row_tile(t, 2 * k * n_s * 2, k * 2 + 4 * n_s * 4 + n_s * 2)

    def body(h_ref, hh_ref, wg_ref, wu_ref, cg_ref, cu_ref, bg_ref, bu_ref, pg_ref, pu_ref, gate_ref, up_ref, a_ref):
        first = pl.program_id(1) == 0
        hv, hh = h_ref[...], hh_ref[...]
        outs = []
        for w_ref, c_ref, b_ref, p_ref, o_ref in ((wg_ref, cg_ref, bg_ref, pg_ref, gate_ref),
                                                  (wu_ref, cu_ref, bu_ref, pu_ref, up_ref)):
            p = _dot(hv, w_ref[0])
            p_ref[...] = p
            hu = _conv(p, _dot(hh, w_ref[0])[BF16_ROWS - HALO:], c_ref, b_ref, first)
            o_ref[...] = hu
            outs.append(hu)
        gate, up = outs
        a_ref[...] = (gate * jax.nn.sigmoid(gate) * up).astype(bf16)

    tile = pl.BlockSpec((tm, n_s), lambda j, i: (i, j))
    cw = pl.BlockSpec((3, n_s), lambda j, i: (0, j))
    cb = pl.BlockSpec((1, n_s), lambda j, i: (0, j))
    f = half * n_s
    return pl.pallas_call(
        body, name=name, grid=(half, t // tm),
        in_specs=[pl.BlockSpec((tm, k), lambda j, i: (i, 0)),
                  pl.BlockSpec((BF16_ROWS, k), lambda j, i: (jnp.maximum(i * (tm // BF16_ROWS) - 1, 0), 0)),
                  pl.BlockSpec((1, k, n_s), lambda j, i: (j, 0, 0)),
                  pl.BlockSpec((1, k, n_s), lambda j, i: (j + half, 0, 0)), cw, cw, cb, cb],
        out_specs=[tile] * 5,
        out_shape=[jax.ShapeDtypeStruct((t, f), f32)] * 4 + [jax.ShapeDtypeStruct((t, f), bf16)],
        compiler_params=_params("parallel", "parallel"))(h, h, w_in4, w_in4, wg, wu, bg, bu)


def _gate_grads(gate, up, dav):
    sg = jax.nn.sigmoid(gate)
    return dav * up * (sg * (1.0 + gate * (1.0 - sg))), dav * gate * sg


GATE_BWD_ROWS = 512


def ffn_gate_bwd(dy, w_out, pg, pu, gate, up, wg, wu, *, name):
    t, f = pg.shape
    d = dy.shape[1]
    tr = min(GATE_BWD_ROWS, t)
    nsteps = t // tr
    tc = f // 2

    def body(dy_ref, dyn_ref, w_ref, pg_ref, pu_ref, gate_ref, gaten_ref, up_ref, upn_ref, wg_ref, wu_ref,
             dg_ref, du_ref, sg_ref, su_ref):
        i = pl.program_id(1)
        last = i == nsteps - 1
        w = w_ref[0]
        da = _dot_nt(dy_ref[...].astype(bf16), w)
        da_n = _dot_nt(dyn_ref[...].astype(bf16), w)
        dgate, dup = _gate_grads(gate_ref[...], up_ref[...], da)
        dgate_n, dup_n = _gate_grads(gaten_ref[...], upn_ref[...], da_n)
        rid = lax.broadcasted_iota(jnp.int32, (8, tc), 0)
        for dd, d_n, c_ref, p_ref, o_ref, s_ref in ((dgate, dgate_n, wg_ref, pg_ref, dg_ref, sg_ref),
                                                    (dup, dup_n, wu_ref, pu_ref, du_ref, su_ref)):
            d1, d2 = _shift_up(dd, d_n, 1, last), _shift_up(dd, d_n, 2, last)
            o_ref[...] = (c_ref[2:3, :] * dd + c_ref[1:2, :] * d1 + c_ref[0:1, :] * d2).astype(bf16)
            p = p_ref[...]
            sums = [jnp.sum(d2 * p, axis=0, keepdims=True), jnp.sum(d1 * p, axis=0, keepdims=True),
                    jnp.sum(dd * p, axis=0, keepdims=True), jnp.sum(dd, axis=0, keepdims=True)]
            part = jnp.zeros((8, tc), f32)
            for k, sk in enumerate(sums):
                part = jnp.where(rid == k, sk, part)

            @pl.when(i == 0)
            def _(s_ref=s_ref, part=part):
                s_ref[...] = part

            @pl.when(i > 0)
            def _(s_ref=s_ref, part=part):
                s_ref[...] += part

    def nxt_rows(j, i):
        return (jnp.minimum((i + 1) * (tr // HALO), t // HALO - 1), j)

    tile = pl.BlockSpec((tr, tc), lambda j, i: (i, j))
    nxt = pl.BlockSpec((HALO, tc), nxt_rows)
    wspec = pl.BlockSpec((3, tc), lambda j, i: (0, j))
    stat = pl.BlockSpec((8, tc), lambda j, i: (0, j))
    return pl.pallas_call(
        body, name=name, grid=(2, nsteps),
        in_specs=[pl.BlockSpec((tr, d), lambda j, i: (i, 0)),
                  pl.BlockSpec((HALO, d), lambda j, i: (nxt_rows(j, i)[0], 0)),
                  pl.BlockSpec((1, tc, d), lambda j, i: (j, 0, 0)),
                  tile, tile, tile, nxt, tile, nxt, wspec, wspec],
        out_specs=[tile, tile, stat, stat],
        out_shape=[jax.ShapeDtypeStruct((t, f), bf16), jax.ShapeDtypeStruct((t, f), bf16),
                   jax.ShapeDtypeStruct((8, f), f32), jax.ShapeDtypeStruct((8, f), f32)],
        compiler_params=_params("parallel", "arbitrary"))(
            dy, dy, w_out.reshape(2, tc, d), pg, pu, gate, gate, up, up, wg, wu)


def _head_mean_matrix():
    i = lax.broadcasted_iota(jnp.int32, (LANES, LANES), 0) // HEAD_DIM
    j = lax.broadcasted_iota(jnp.int32, (LANES, LANES), 1) // HEAD_DIM
    return jnp.where(i == j, 1.0 / HEAD_DIM, 0.0).astype(bf16)


def _lane_half(shape):
    return (lax.broadcasted_iota(jnp.int32, shape, 1) % LANES) // HEAD_DIM


def q_norm_fwd(qp, g2, *, name, scale, tr=512):
    t, w = qp.shape
    tr = min(tr, t)

    def body(x_ref, g_ref, o_ref):
        bd = _head_mean_matrix()
        for cb in range(w // LANES):
            cols = slice(cb * LANES, (cb + 1) * LANES)
            xc = x_ref[:, cols]
            rh = lax.rsqrt(_dot_split(xc * xc, bd) + EPS)
            o_ref[:, cols] = (xc * rh * g_ref[...] * scale).astype(bf16)

    row = pl.BlockSpec((tr, w), lambda i: (i, 0))
    return pl.pallas_call(
        body, name=name, grid=(t // tr,), in_specs=[row, pl.BlockSpec((1, LANES), lambda i: (0, 0))],
        out_specs=row, out_shape=jax.ShapeDtypeStruct((t, w), bf16),
        compiler_params=_params("parallel"))(qp, g2)


def q_norm_bwd(dq, qp, g2, *, name, scale, tr=512):
    t, w = qp.shape
    tr = min(tr, t)

    def body(dq_ref, x_ref, g_ref, o_ref, dg_ref):
        i = pl.program_id(0)
        bd = _head_mean_matrix()
        acc = jnp.zeros((1, LANES), f32)
        for cb in range(w // LANES):
            cols = slice(cb * LANES, (cb + 1) * LANES)
            xc = x_ref[:, cols]
            rh = lax.rsqrt(_dot_split(xc * xc, bd) + EPS)
            xh = xc * rh
            dy = dq_ref[:, cols] * scale
            acc = acc + jnp.sum(dy * xh, axis=0, keepdims=True)
            tg = dy * g_ref[...]
            o_ref[:, cols] = (rh * (tg - xh * _dot_split(tg * xh, bd))).astype(bf16)

        @pl.when(i == 0)
        def _():
            dg_ref[...] = acc

        @pl.when(i > 0)
        def _():
            dg_ref[...] += acc

    row = pl.BlockSpec((tr, w), lambda i: (i, 0))
    vec = pl.BlockSpec((1, LANES), lambda i: (0, 0))
    return pl.pallas_call(
        body, name=name, grid=(t // tr,), in_specs=[row, row, vec], out_specs=[row, vec],
        out_shape=[jax.ShapeDtypeStruct((t, w), bf16), jax.ShapeDtypeStruct((1, LANES), f32)],
        compiler_params=_params("arbitrary"))(dq, qp, g2)


def kv_post_fwd(kv, g2, *, name, tr=512):
    t, w = kv.shape
    tr = min(tr, t)
    kw = w // 2

    def body(x_ref, g_ref, k_ref, v_ref):
        bd = _head_mean_matrix()
        half = _lane_half((tr, LANES))
        for cb in range(kw // LANES):
            xc = x_ref[:, cb * LANES:(cb + 1) * LANES]
            rh = lax.rsqrt(_dot_split(xc * xc, bd) + EPS)
            kn = xc * rh * g_ref[...]
            vc = x_ref[:, kw + cb * LANES:kw + (cb + 1) * LANES]
            for src, dst in ((kn, k_ref), (vc, v_ref)):
                sw = pltpu.roll(src, HEAD_DIM, 1)
                for hf in range(2):
                    blk = 2 * cb + hf
                    dst[:, blk * LANES:(blk + 1) * LANES] = jnp.where(half == hf, src, sw).astype(bf16)

    return pl.pallas_call(
        body, name=name, grid=(t // tr,),
        in_specs=[pl.BlockSpec((tr, w), lambda i: (i, 0)), pl.BlockSpec((1, LANES), lambda i: (0, 0))],
        out_specs=[pl.BlockSpec((tr, 2 * kw), lambda i: (i, 0))] * 2,
        out_shape=[jax.ShapeDtypeStruct((t, 2 * kw), bf16)] * 2,
        compiler_params=_params("parallel"))(kv, g2)


def kv_post_bwd(dk2, dv2, kv, g2, *, name, tr=512):
    t, w = kv.shape
    tr = min(tr, t)
    kw = w // 2

    def body(dk_ref, dv_ref, x_ref, g_ref, o_ref, dg_ref):
        i = pl.program_id(0)
        bd = _head_mean_matrix()
        half = _lane_half((tr, LANES))
        acc = jnp.zeros((1, LANES), f32)

        def fold(ref, cb):
            a = ref[:, (2 * cb) * LANES:(2 * cb + 1) * LANES]
            b = ref[:, (2 * cb + 1) * LANES:(2 * cb + 2) * LANES]
            return jnp.where(half == 0, a + pltpu.roll(a, HEAD_DIM, 1), b + pltpu.roll(b, HEAD_DIM, 1))

        for cb in range(kw // LANES):
            cols = slice(cb * LANES, (cb + 1) * LANES)
            xc = x_ref[:, cols]
            rh = lax.rsqrt(_dot_split(xc * xc, bd) + EPS)
            xh = xc * rh
            dy = fold(dk_ref, cb)
            acc = acc + jnp.sum(dy * xh, axis=0, keepdims=True)
            tg = dy * g_ref[...]
            o_ref[:, cols] = (rh * (tg - xh * _dot_split(tg * xh, bd))).astype(bf16)
            o_ref[:, kw + cb * LANES:kw + (cb + 1) * LANES] = fold(dv_ref, cb).astype(bf16)

        @pl.when(i == 0)
        def _():
            dg_ref[...] = acc

        @pl.when(i > 0)
        def _():
            dg_ref[...] += acc

    dup = pl.BlockSpec((tr, 2 * kw), lambda i: (i, 0))
    row = pl.BlockSpec((tr, w), lambda i: (i, 0))
    vec = pl.BlockSpec((1, LANES), lambda i: (0, 0))
    return pl.pallas_call(
        body, name=name, grid=(t // tr,), in_specs=[dup, dup, row, vec], out_specs=[row, vec],
        out_shape=[jax.ShapeDtypeStruct((t, w), bf16), jax.ShapeDtypeStruct((1, LANES), f32)],
        compiler_params=_params("arbitrary"))(dk2, dv2, kv, g2)


def _slope(h):
    return 2.0 ** (-8.0 * (h + 1) / N_Q_HEADS)


GROUP_ROWS = Q_PER_KV * CHUNK


def _band_mask(n):
    tq = lax.broadcasted_iota(jnp.int32, (GROUP_ROWS, 2 * CHUNK), 0) % CHUNK
    jk = lax.broadcasted_iota(jnp.int32, (GROUP_ROWS, 2 * CHUNK), 1)
    dist = tq + CHUNK - jk
    ok = (dist >= 0) & (dist < CHUNK) & jnp.logical_not((n == 0) & (jk < CHUNK))
    return dist.astype(f32), ok


def _band(ref, n, kh):
    p0 = pl.multiple_of(jnp.maximum(n - 1, 0) * CHUNK, CHUNK)
    c0 = pl.multiple_of(n * CHUNK, CHUNK)
    cols = slice(kh * LANES, (kh + 1) * LANES)
    return jnp.concatenate([ref[pl.ds(p0, CHUNK), cols], ref[pl.ds(c0, CHUNK), cols]], axis=0)


def _stack_heads(ref, kh, half):
    parts = []
    for cb in (2 * kh, 2 * kh + 1):
        xc = ref[:, cb * LANES:(cb + 1) * LANES].astype(f32)
        parts += [jnp.where(half == hf, xc, 0.0).astype(bf16) for hf in range(2)]
    return jnp.concatenate(parts, axis=0)


def _unstack_heads(x4, half):
    return (jnp.where(half == 0, x4[0:CHUNK], x4[CHUNK:2 * CHUNK]),
            jnp.where(half == 0, x4[2 * CHUNK:3 * CHUNK], x4[3 * CHUNK:]))


def _per_head_column(kh, values):
    grp = lax.broadcasted_iota(jnp.int32, (GROUP_ROWS, 1), 0) // CHUNK
    col = jnp.full((GROUP_ROWS, 1), values[0], f32)
    for g in range(1, Q_PER_KV):
        col = jnp.where(grp == g, values[g], col)
    return col


def _softmax_band(q4, kband, dist, ok, slope, sink):
    s = _dot_nt(q4, kband)
    s = jnp.where(ok, s - slope * dist, -jnp.inf)
    m = jnp.maximum(jnp.max(s, axis=1, keepdims=True), sink)
    e = jnp.exp(s - m)
    es = jnp.exp(sink - m)
    den = jnp.sum(e, axis=1, keepdims=True) + es
    return e / den, es / den


def attn_fwd(q, k2, v2, sinks, *, name):
    t, w = q.shape
    nb = t // CHUNK

    def body(sink_ref, q_ref, k_ref, v_ref, o_ref):
        n = pl.program_id(0)
        dist, ok = _band_mask(n)
        half = _lane_half((CHUNK, LANES))
        for kh in range(N_KV_HEADS):
            heads = [Q_PER_KV * kh + g for g in range(Q_PER_KV)]
            slope = _per_head_column(kh, [_slope(h) for h in heads])
            sink = _per_head_column(kh, [sink_ref[h] for h in heads])
            q4 = _stack_heads(q_ref, kh, half)
            p, _ = _softmax_band(q4, _band(k_ref, n, kh), dist, ok, slope, sink)
            o4 = _dot(p.astype(bf16), _band(v_ref, n, kh))
            lo, hi = _unstack_heads(o4, half)
            o_ref[:, (2 * kh) * LANES:(2 * kh + 1) * LANES] = lo.astype(bf16)
            o_ref[:, (2 * kh + 1) * LANES:(2 * kh + 2) * LANES] = hi.astype(bf16)

    full = pl.BlockSpec((t, k2.shape[1]), lambda n: (0, 0))
    return pl.pallas_call(
        body, name=name, grid=(nb,),
        in_specs=[pl.BlockSpec(memory_space=pltpu.SMEM), pl.BlockSpec((CHUNK, w), lambda n: (n, 0)), full, full],
        out_specs=pl.BlockSpec((CHUNK, w), lambda n: (n, 0)),
        out_shape=jax.ShapeDtypeStruct((t, w), bf16),
        compiler_params=_params("parallel"))(sinks, q, k2, v2)


def attn_bwd(q, k2, v2, do, sinks, *, name):
    t, w = q.shape
    nb = t // CHUNK
    kw = k2.shape[1]

    def body(sink_ref, q_ref, k_ref, v_ref, do_ref, dq_ref, dk_ref, dv_ref, ds_ref, kc_ref, vc_ref):
        n = pl.program_id(0)

        @pl.when(n == 0)
        def _():
            ds_ref[...] = jnp.zeros_like(ds_ref)
            kc_ref[...] = jnp.zeros_like(kc_ref)
            vc_ref[...] = jnp.zeros_like(vc_ref)
            dk_ref[...] = jnp.zeros_like(dk_ref)
            dv_ref[...] = jnp.zeros_like(dv_ref)

        @pl.when(n == nb)
        def _():
            dk_ref[...] = kc_ref[...]
            dv_ref[...] = vc_ref[...]

        @pl.when(n < nb)
        def _():
            dist, ok = _band_mask(n)
            half = _lane_half((CHUNK, LANES))
            lane = lax.broadcasted_iota(jnp.int32, (1, LANES), 1)
            sink_acc = jnp.zeros((1, LANES), f32)
            for kh in range(N_KV_HEADS):
                heads = [Q_PER_KV * kh + g for g in range(Q_PER_KV)]
                slope = _per_head_column(kh, [_slope(h) for h in heads])
                sink = _per_head_column(kh, [sink_ref[h] for h in heads])
                q4 = _stack_heads(q_ref, kh, half)
                do4 = _stack_heads(do_ref, kh, half)
                kband = _band(k_ref, n, kh)
                vband = _band(v_ref, n, kh)
                p, ps = _softmax_band(q4, kband, dist, ok, slope, sink)
                dp = _dot_nt(do4, vband)
                delta = jnp.sum(p * dp, axis=1, keepdims=True)
                dsb = (p * (dp - delta)).astype(bf16)
                sd = ps * delta
                for g, h in enumerate(heads):
                    part = jnp.sum(sd[g * CHUNK:(g + 1) * CHUNK], axis=0, keepdims=True)
                    sink_acc = sink_acc + jnp.where(lane == h, -part, 0.0)
                lo, hi = _unstack_heads(_dot(dsb, kband), half)
                dq_ref[:, (2 * kh) * LANES:(2 * kh + 1) * LANES] = lo
                dq_ref[:, (2 * kh + 1) * LANES:(2 * kh + 2) * LANES] = hi
                dkb = _dot_tn(dsb, q4)
                dvb = _dot_tn(p.astype(bf16), do4)
                cols = slice(kh * LANES, (kh + 1) * LANES)
                dk_ref[:, cols] = kc_ref[:, cols] + dkb[0:CHUNK]
                dv_ref[:, cols] = vc_ref[:, cols] + dvb[0:CHUNK]
                kc_ref[:, cols] = dkb[CHUNK:]
                vc_ref[:, cols] = dvb[CHUNK:]
            ds_ref[...] += sink_acc

    full = pl.BlockSpec((t, kw), lambda n: (0, 0))
    qblk = pl.BlockSpec((CHUNK, w), lambda n: (jnp.minimum(n, nb - 1), 0))
    kblk = pl.BlockSpec((CHUNK, kw), lambda n: (jnp.maximum(n - 1, 0), 0))
    return pl.pallas_call(
        body, name=name, grid=(nb + 1,),
        in_specs=[pl.BlockSpec(memory_space=pltpu.SMEM), qblk, full, full, qblk],
        out_specs=[qblk, kblk, kblk, pl.BlockSpec((1, LANES), lambda n: (0, 0))],
        out_shape=[jax.ShapeDtypeStruct((t, w), f32), jax.ShapeDtypeStruct((t, kw), f32),
                   jax.ShapeDtypeStruct((t, kw), f32), jax.ShapeDtypeStruct((1, LANES), f32)],
        scratch_shapes=[pltpu.VMEM((CHUNK, kw), f32), pltpu.VMEM((CHUNK, kw), f32)],
        compiler_params=_params("arbitrary"))(sinks, q, k2, v2, do)


def loss_head(y, target, *, name, tr=512):
    t, d = y.shape
    tr = min(tr, t)

    def body(y_ref, t_ref, dy_ref, s_ref):
        i = pl.program_id(0)
        e = y_ref[...] - t_ref[...]
        dy_ref[...] = e * (1.0 / d)
        part = jnp.sum(e * e, axis=0, keepdims=True)

        @pl.when(i == 0)
        def _():
            s_ref[...] = part

        @pl.when(i > 0)
        def _():
            s_ref[...] += part

    row = pl.BlockSpec((tr, d), lambda i: (i, 0))
    vec = pl.BlockSpec((1, d), lambda i: (0, 0))
    return pl.pallas_call(
        body, name=name, grid=(t // tr,), in_specs=[row, row], out_specs=[row, vec],
        out_shape=[jax.ShapeDtypeStruct((t, d), f32), jax.ShapeDtypeStruct((1, d), f32)],
        compiler_params=_params("arbitrary"))(y, target)


N_STEPS = 8


def _row_blocks(shape):
    if len(shape) == 2:
        r, c = shape
        return (r // N_STEPS, c), (lambda s: (s, 0))
    l, r, c = shape
    per = N_STEPS // l
    return (1, r // per, c), (lambda s: (s // per, s % per, 0))


CAST_STEPS = 4


def cast_into_slot(arrays, k_arr, *, name):
    in_specs, out_specs, out_shape, layers = [], [], [], []
    for a in arrays:
        r, c = a.shape[-2:]
        rb = r // CAST_STEPS
        if a.ndim == 2:
            in_specs.append(pl.BlockSpec((rb, c), lambda s, k: (s, 0)))
            layers.append(None)
        else:
            for l in range(a.shape[0]):
                in_specs.append(pl.BlockSpec((1, rb, c), lambda s, k, l=l: (l, s, 0)))
                layers.append(l)
        for _ in range(1 if a.ndim == 2 else a.shape[0]):
            out_specs.append(pl.BlockSpec((1, rb, c), lambda s, k: (k[0], s, 0)))
            out_shape.append(jax.ShapeDtypeStruct((N_SHARDS, r, c), bf16))
    n = len(in_specs)

    def body(k_ref, *refs):
        for i_ref, o_ref, l in zip(refs[:n], refs[n:], layers):
            o_ref[0] = (i_ref[...] if l is None else i_ref[0]).astype(bf16)

    args = []
    for a in arrays:
        args += [a] * (1 if a.ndim == 2 else a.shape[0])
    return pl.pallas_call(
        body, name=name,
        grid_spec=pltpu.PrefetchScalarGridSpec(num_scalar_prefetch=1, grid=(CAST_STEPS,),
                                               in_specs=in_specs, out_specs=out_specs),
        out_shape=out_shape, compiler_params=_params("parallel"))(k_arr, *args)


def adamw(ws, gs, ms, vs, *, name):
    n = len(ws)
    specs, g_specs, g_count = [], [], []
    for w, g_list in zip(ws, gs):
        blk, index = _row_blocks(w.shape)
        specs.append(pl.BlockSpec(blk, index))
        layers = len(g_list)
        per = N_STEPS // layers
        g_count.append(layers)
        for l in range(layers):
            g_specs.append(pl.BlockSpec(blk[-2:], lambda s, l=l, per=per: (jnp.where(s // per == l, s % per, 0), 0)))
    ng = len(g_specs)

    def body(*refs):
        s = pl.program_id(0)
        g_refs = refs[3 * n:3 * n + ng]
        outs = refs[3 * n + ng:]
        off = 0
        for i in range(n):
            w_ref, m_ref, v_ref = refs[i], refs[n + i], refs[2 * n + i]
            go_ref, d_ref, nm_ref, nv_ref = (outs[k * n + i] for k in range(4))
            layers = g_count[i]
            g = g_refs[off][...]
            for l in range(1, layers):
                g = jnp.where(s // (N_STEPS // layers) == l, g_refs[off + l][...], g)
            off += layers
            g = g.reshape(w_ref.shape)
            m = ADAM_B1 * m_ref[...] + (1.0 - ADAM_B1) * g
            v = ADAM_B2 * v_ref[...] + (1.0 - ADAM_B2) * (g * g)
            m_hat = m / ADAM_C1
            v_hat = v / ADAM_C2
            go_ref[...] = g
            d_ref[...] = -ADAM_LR * (m_hat / (jnp.sqrt(v_hat) + ADAM_EPS) + ADAM_WD * w_ref[...])
            nm_ref[...] = m
            nv_ref[...] = v

    outs = pl.pallas_call(
        body, name=name, grid=(N_STEPS,), in_specs=specs * 3 + g_specs, out_specs=specs * 4,
        out_shape=[jax.ShapeDtypeStruct(a.shape, f32) for a in ws] * 4,
        compiler_params=_params("parallel"))(*ws, *ms, *vs, *[g for g_list in gs for g in g_list])
    return [outs[k * n:(k + 1) * n] for k in range(4)]


def _adamw_update(w, g, m, v):
    m = ADAM_B1 * m + (1.0 - ADAM_B1) * g
    v = ADAM_B2 * v + (1.0 - ADAM_B2) * (g * g)
    m_hat = m / ADAM_C1
    v_hat = v / ADAM_C2
    return -ADAM_LR * (m_hat / (jnp.sqrt(v_hat) + ADAM_EPS) + ADAM_WD * w), m, v


def adamw_small(ws, gs, ms, vs, *, name):
    n = len(ws)

    def body(*refs):
        for i in range(n):
            w_ref, g_ref, m_ref, v_ref = (refs[k * n + i] for k in range(4))
            d_ref, nm_ref, nv_ref = (refs[(4 + k) * n + i] for k in range(3))
            d_ref[...], nm_ref[...], nv_ref[...] = _adamw_update(w_ref[...], g_ref[...], m_ref[...], v_ref[...])

    outs = pl.pallas_call(
        body, name=name, out_shape=[jax.ShapeDtypeStruct(a.shape, f32) for a in ws] * 3)(*ws, *gs, *ms, *vs)
    return outs[:n], outs[n:2 * n], outs[2 * n:]


def _place():
    return lax.axis_index("x"), lax.axis_index("y"), lax.axis_index("c")


def gather_shards(bufs, *, name, split):
    n = len(bufs)

    def body(*refs):
        bufs_ = refs[:n]
        isend, irecv, dsend, drecv = refs[2 * n:]
        x, y, c = _place()
        k = 2 * x + y
        peers = [(1 - x, y, c), (x, 1 - y, c), (1 - x, 1 - y, c)]
        peer_k = [2 * (1 - x) + y, 2 * x + (1 - y), 2 * (1 - x) + (1 - y)]

        def slab(a, q, h):
            if not split[a]:
                return bufs_[a].at[q]
            half = bufs_[a].shape[1] // 2
            return bufs_[a].at[q, pl.ds(pl.multiple_of(h * half, 16), half)]

        def ici(a, j, q):
            return pltpu.make_async_remote_copy(
                src_ref=slab(a, q, c), dst_ref=slab(a, q, c), send_sem=isend.at[3 * a + j], recv_sem=irecv.at[3 * a + j],
                device_id=peers[j], device_id_type=MESH)

        def d2d(a, j, h):
            return pltpu.make_async_remote_copy(
                src_ref=slab(a, peer_k[j], h), dst_ref=slab(a, peer_k[j], h), send_sem=dsend.at[3 * a + j],
                recv_sem=drecv.at[3 * a + j], device_id=(x, y, 1 - c), device_id_type=MESH)

        for a in range(n):
            for j in range(3):
                ici(a, j, k).start()
        for a in range(n):
            for j in range(3):
                ici(a, j, peer_k[j]).wait_recv()
                if split[a]:
                    d2d(a, j, c).start()
        for a in range(n):
            for j in range(3):
                if split[a]:
                    d2d(a, j, 1 - c).wait_recv()
        for a in range(n):
            for j in range(3):
                ici(a, j, k).wait_send()
                if split[a]:
                    d2d(a, j, c).wait_send()

    return pl.pallas_call(
        body, name=name, in_specs=[ANY] * n, out_specs=[ANY] * n,
        out_shape=[jax.ShapeDtypeStruct(b.shape, b.dtype) for b in bufs],
        input_output_aliases={i: i for i in range(n)},
        scratch_shapes=[pltpu.SemaphoreType.DMA((3 * n,))] * 4)(*bufs)


HBM = pl.BlockSpec(memory_space=pltpu.HBM)
SEM = pl.BlockSpec(memory_space=pltpu.SEMAPHORE)
DATAFLOW = pltpu.SideEffectType.DATAFLOW_SIDE_EFFECTING


def _chip_peers():
    x, y, c = _place()
    return 2 * x + y, [(1 - x, y, c), (x, 1 - y, c), (1 - x, 1 - y, c)], [2 * (1 - x) + y, 2 * x + (1 - y), 2 * (1 - x) + (1 - y)]


def gather_start(bufs, groups, after, *, name):
    n = len(bufs)
    ng = len(groups)

    def body(*refs):
        ins = refs[:n]
        sends, recvs = refs[2 * n + 1:2 * n + 1 + ng], refs[2 * n + 1 + ng:2 * n + 1 + 2 * ng]
        token = refs[-1]
        k, peers, _ = _chip_peers()
        for gi, grp in enumerate(groups):
            for pos, a in enumerate(grp):
                for j in range(3):
                    pltpu.make_async_remote_copy(
                        src_ref=ins[a].at[k], dst_ref=ins[a].at[k], send_sem=sends[gi].at[3 * pos + j],
                        recv_sem=recvs[gi].at[3 * pos + j], device_id=peers[j], device_id_type=MESH).start()
        token[...] = jnp.zeros_like(token)

    sems = [pltpu.SemaphoreType.DMA((3 * len(grp),)) for grp in groups]
    outs = pl.pallas_call(
        body, name=name, in_specs=[HBM] * n + [ANY],
        out_specs=[HBM] * n + [SEM] * (2 * ng) + [pl.BlockSpec(memory_space=pltpu.VMEM)],
        out_shape=[pltpu.HBM(b.shape, b.dtype) for b in bufs] + sems + sems + [jax.ShapeDtypeStruct((8, LANES), f32)],
        input_output_aliases={i: i for i in range(n)},
        compiler_params=pltpu.CompilerParams(has_side_effects=DATAFLOW))(
            *[pltpu.with_memory_space_constraint(b, pltpu.HBM) for b in bufs], after)
    return outs[:n], outs[n:n + ng], outs[n + ng:n + 2 * ng], outs[-1]


def gather_wait(bufs, send_sems, recv_sems, after, *, name):
    n = len(bufs)

    def body(*refs):
        ins = refs[:n]
        send, recv = refs[n], refs[n + 1]
        k, peers, peer_k = _chip_peers()
        for a in range(n):
            for j in range(3):
                copy = pltpu.make_async_remote_copy(
                    src_ref=ins[a].at[k], dst_ref=ins[a].at[peer_k[j]], send_sem=send.at[3 * a + j],
                    recv_sem=recv.at[3 * a + j], device_id=peers[j], device_id_type=MESH)
                copy.wait_send()
                copy.wait_recv()

    return pl.pallas_call(
        body, name=name, in_specs=[HBM] * n + [SEM, SEM, ANY], out_specs=[HBM] * n,
        out_shape=[pltpu.HBM(b.shape, b.dtype) for b in bufs],
        input_output_aliases={i: i for i in range(n)},
        compiler_params=pltpu.CompilerParams(has_side_effects=DATAFLOW))(*bufs, send_sems, recv_sems, after)


def _sibling_copies(srcs, lands, send, recv):
    x, y, c = _place()
    return [pltpu.make_async_remote_copy(src_ref=srcs[a], dst_ref=lands[a], send_sem=send.at[a], recv_sem=recv.at[a],
                                         device_id=(x, y, 1 - c), device_id_type=MESH) for a in range(len(srcs))]


def sibling_start(arrays, after, *, name):
    n = len(arrays)
    lands = [pltpu.with_memory_space_constraint(lax.empty(a.shape, a.dtype), pltpu.HBM) for a in arrays]

    def body(*refs):
        for cp in _sibling_copies(refs[:n], refs[n:2 * n], refs[4 * n + 1], refs[4 * n + 2]):
            cp.start()
        refs[-1][...] = jnp.zeros_like(refs[-1])

    bufs = list(arrays) + lands
    sems = [pltpu.SemaphoreType.DMA((n,))] * 2
    outs = pl.pallas_call(
        body, name=name, in_specs=[HBM] * (2 * n) + [ANY],
        out_specs=[HBM] * (2 * n) + [SEM] * 2 + [pl.BlockSpec(memory_space=pltpu.VMEM)],
        out_shape=[pltpu.HBM(b.shape, b.dtype) for b in bufs] + sems + [jax.ShapeDtypeStruct((8, LANES), f32)],
        input_output_aliases={i: i for i in range(2 * n)},
        compiler_params=pltpu.CompilerParams(has_side_effects=DATAFLOW))(
            *[pltpu.with_memory_space_constraint(b, pltpu.HBM) for b in bufs], after)
    return (n, outs[:-1]), outs[-1]


def sibling_wait(state, after, *, name):
    n, held = state

    def body(*refs):
        for cp in _sibling_copies(refs[:n], refs[n:2 * n], refs[2 * n], refs[2 * n + 1]):
            cp.wait_send()
            cp.wait_recv()

    outs = pl.pallas_call(
        body, name=name, in_specs=[HBM] * (2 * n) + [SEM] * 2 + [ANY], out_specs=[HBM] * (2 * n),
        out_shape=[pltpu.HBM(b.shape, b.dtype) for b in held[:2 * n]],
        input_output_aliases={i: i for i in range(2 * n)},
        compiler_params=pltpu.CompilerParams(has_side_effects=DATAFLOW))(*held, after)
    return outs[n:]


def sibling_exchange(arrays, *, name):
    n = len(arrays)

    def body(*refs):
        ins, outs = refs[:n], refs[n:2 * n]
        send, recv = refs[2 * n:]
        x, y, c = _place()

        def copy(a):
            return pltpu.make_async_remote_copy(
                src_ref=ins[a], dst_ref=outs[a], send_sem=send.at[a], recv_sem=recv.at[a],
                device_id=(x, y, 1 - c), device_id_type=MESH)

        for a in range(n):
            copy(a).start()
        for a in range(n):
            copy(a).wait_recv()
        for a in range(n):
            copy(a).wait_send()

    return pl.pallas_call(
        body, name=name, in_specs=[ANY] * n, out_specs=[ANY] * n,
        out_shape=[jax.ShapeDtypeStruct(a.shape, a.dtype) for a in arrays],
        scratch_shapes=[pltpu.SemaphoreType.DMA((n,)), pltpu.SemaphoreType.DMA((n,))])(*arrays)


ALL_MASKS = [(mx, my, mc) for mx in (0, 1) for my in (0, 1) for mc in (0, 1)][1:]


def _scatter_copies(srcs, lands, ev, send, recv, esend, erecv):
    x, y, c = _place()
    me = 4 * x + 2 * y + c
    k, peers, peer_k = _chip_peers()
    out = []
    for a in range(len(srcs)):
        for j in range(3):
            out.append(pltpu.make_async_remote_copy(
                src_ref=srcs[a].at[peer_k[j]], dst_ref=lands[a].at[j], send_sem=send.at[3 * a + j],
                recv_sem=recv.at[3 * a + j], device_id=peers[j], device_id_type=MESH))
    start_ev, wait_ev = [], []
    if ev is not None:
        for j, (mx, my, mc) in enumerate(ALL_MASKS):
            peer = (x ^ mx, y ^ my, c ^ mc)
            start_ev.append(pltpu.make_async_remote_copy(
                src_ref=ev.at[me], dst_ref=ev.at[me], send_sem=esend.at[j], recv_sem=erecv.at[j],
                device_id=peer, device_id_type=MESH))
            wait_ev.append(pltpu.make_async_remote_copy(
                src_ref=ev.at[me], dst_ref=ev.at[me ^ (4 * mx + 2 * my + mc)], send_sem=esend.at[j],
                recv_sem=erecv.at[j], device_id=peer, device_id_type=MESH))
    return out, start_ev, wait_ev


def chip_scatter_start(arrays, everyone, after, *, name):
    n = len(arrays)
    ne = 0 if everyone is None else 1
    lands = [pltpu.with_memory_space_constraint(lax.empty((3,) + a.shape[1:], a.dtype), pltpu.HBM) for a in arrays]

    def body(*refs):
        srcs, lands_ = refs[:n], refs[n:2 * n]
        ev = refs[2 * n] if ne else None
        sems = refs[2 * n + ne + 1 + 2 * n + ne:-1]
        send, recv = sems[0], sems[1]
        esend, erecv = (sems[2], sems[3]) if ne else (None, None)
        copies, start_ev, _ = _scatter_copies(srcs, lands_, ev, send, recv, esend, erecv)
        for cp in start_ev + copies:
            cp.start()
        refs[-1][...] = jnp.zeros_like(refs[-1])

    sem_shapes = [pltpu.SemaphoreType.DMA((3 * n,))] * 2 + [pltpu.SemaphoreType.DMA((7,))] * (2 * ne)
    bufs = list(arrays) + lands + ([everyone] if ne else [])
    outs = pl.pallas_call(
        body, name=name, in_specs=[HBM] * len(bufs) + [ANY],
        out_specs=[HBM] * len(bufs) + [SEM] * len(sem_shapes) + [pl.BlockSpec(memory_space=pltpu.VMEM)],
        out_shape=[pltpu.HBM(b.shape, b.dtype) for b in bufs] + sem_shapes + [jax.ShapeDtypeStruct((8, LANES), f32)],
        input_output_aliases={i: i for i in range(len(bufs))},
        compiler_params=pltpu.CompilerParams(has_side_effects=DATAFLOW))(
            *[pltpu.with_memory_space_constraint(b, pltpu.HBM) for b in bufs], after)
    return (n, ne, outs[:-1]), outs[-1]


def chip_scatter_wait(state, after, *, name):
    n, ne, held = state
    nb = 2 * n + ne
    bufs, sems = held[:nb], held[nb:]

    def body(*refs):
        srcs, lands_ = refs[:n], refs[n:2 * n]
        ev = refs[2 * n] if ne else None
        sems_ = refs[nb:nb + len(sems)]
        esend, erecv = (sems_[2], sems_[3]) if ne else (None, None)
        copies, _, wait_ev = _scatter_copies(srcs, lands_, ev, sems_[0], sems_[1], esend, erecv)
        for cp in wait_ev + copies:
            cp.wait_send()
            cp.wait_recv()

    outs = pl.pallas_call(
        body, name=name, in_specs=[HBM] * nb + [SEM] * len(sems) + [ANY], out_specs=[HBM] * nb,
        out_shape=[pltpu.HBM(b.shape, b.dtype) for b in bufs],
        input_output_aliases={i: i for i in range(nb)},
        compiler_params=pltpu.CompilerParams(has_side_effects=DATAFLOW))(*bufs, *sems, after)
    return outs[n:2 * n], (outs[2 * n] if ne else None)


def sibling_merge(bufs, *, name):
    n = len(bufs)

    def body(*refs):
        bufs_ = refs[:n]
        send, recv = refs[2 * n:]
        x, y, c = _place()

        def copy(u, h):
            return pltpu.make_async_remote_copy(
                src_ref=bufs_[u].at[h], dst_ref=bufs_[u].at[h], send_sem=send.at[u], recv_sem=recv.at[u],
                device_id=(x, y, 1 - c), device_id_type=MESH)

        for u in range(n):
            copy(u, c).start()
        for u in range(n):
            copy(u, 1 - c).wait_recv()
        for u in range(n):
            copy(u, c).wait_send()

    return pl.pallas_call(
        body, name=name, in_specs=[ANY] * n, out_specs=[ANY] * n,
        out_shape=[jax.ShapeDtypeStruct(b.shape, b.dtype) for b in bufs],
        input_output_aliases={i: i for i in range(n)},
        scratch_shapes=[pltpu.SemaphoreType.DMA((n,)), pltpu.SemaphoreType.DMA((n,))])(*bufs)


def sum_leading(a, *, name):
    n, r, c = a.shape

    def body(a_ref, o_ref):
        acc = a_ref[0]
        for i in range(1, n):
            acc = acc + a_ref[i]
        o_ref[...] = acc

    rb = r // 2 if r % 16 == 0 else r
    return pl.pallas_call(
        body, name=name, grid=(r // rb,), in_specs=[pl.BlockSpec((n, rb, c), lambda i: (0, i, 0))],
        out_specs=pl.BlockSpec((rb, c), lambda i: (i, 0)), out_shape=jax.ShapeDtypeStruct((r, c), f32),
        compiler_params=_params("parallel"))(a)


def _half_rows(shape):
    return shape[1] // 2 // 2


def rs_cast_other_half(grads, c_arr, *, name):
    n = len(grads)

    def body(c_ref, *refs):
        for i_ref, o_ref in zip(refs[:n], refs[n:]):
            o_ref[...] = i_ref[...].astype(bf16)

    in_specs = [pl.BlockSpec((1, _half_rows(g.shape), g.shape[2]), lambda s, r, c_ref: (s, (1 - c_ref[0]) * 2 + r, 0))
                for g in grads]
    out_specs = [pl.BlockSpec((1, _half_rows(g.shape), g.shape[2]), lambda s, r, c_ref: (s, r, 0)) for g in grads]
    return pl.pallas_call(
        body, name=name,
        grid_spec=pltpu.PrefetchScalarGridSpec(num_scalar_prefetch=1, grid=(N_SHARDS, 2),
                                               in_specs=in_specs, out_specs=out_specs),
        out_shape=[jax.ShapeDtypeStruct((N_SHARDS, g.shape[1] // 2, g.shape[2]), bf16) for g in grads],
        compiler_params=_params("parallel", "parallel"))(c_arr, *grads)


def rs_add_sibling(grads, recvd, ck_arr, *, name):
    n = len(grads)

    def body(ck_ref, *refs):
        s = pl.program_id(1)
        for u in range(n):
            g_ref, r_ref = refs[u], refs[n + u]
            qb_ref, own_ref = refs[2 * n + u], refs[3 * n + u]
            q = g_ref[0] + r_ref[0].astype(f32)
            qb_ref[0] = q.astype(bf16)

            @pl.when(s == ck_ref[1])
            def _(own_ref=own_ref, q=q):
                own_ref[...] = q

    in_specs = [pl.BlockSpec((1, _half_rows(g.shape), g.shape[2]), lambda r, s, ck: (s, ck[0] * 2 + r, 0)) for g in grads]
    in_specs += [pl.BlockSpec((1, _half_rows(g.shape), g.shape[2]), lambda r, s, ck: (s, r, 0)) for g in grads]
    out_specs = [pl.BlockSpec((1, _half_rows(g.shape), g.shape[2]), lambda r, s, ck: (s, r, 0)) for g in grads]
    out_specs += [pl.BlockSpec((_half_rows(g.shape), g.shape[2]), lambda r, s, ck: (r, 0)) for g in grads]
    outs = pl.pallas_call(
        body, name=name,
        grid_spec=pltpu.PrefetchScalarGridSpec(num_scalar_prefetch=1, grid=(2, N_SHARDS),
                                               in_specs=in_specs, out_specs=out_specs),
        out_shape=[jax.ShapeDtypeStruct((N_SHARDS, g.shape[1] // 2, g.shape[2]), bf16) for g in grads]
        + [jax.ShapeDtypeStruct((g.shape[1] // 2, g.shape[2]), f32) for g in grads],
        compiler_params=_params("parallel", "arbitrary"))(ck_arr, *grads, *recvd)
    return outs[:n], outs[n:]


def rs_sum_chips(owns, recvd, ck_arr, *, name):
    n = len(owns)

    def body(ck_ref, *refs):
        for u in range(n):
            own_ref, r_ref, o_ref = refs[u], refs[n + u], refs[2 * n + u]
            o_ref[0] = ((own_ref[...] + r_ref[0].astype(f32)) + r_ref[1].astype(f32)) + r_ref[2].astype(f32)

    in_specs = [pl.BlockSpec((o.shape[0] // 2, o.shape[1]), lambda r, ck: (r, 0)) for o in owns]
    in_specs += [pl.BlockSpec((3, o.shape[0] // 2, o.shape[1]), lambda r, ck: (0, r, 0)) for o in owns]
    out_specs = [pl.BlockSpec((1, o.shape[0] // 2, o.shape[1]), lambda r, ck: (ck[0], r, 0)) for o in owns]
    return pl.pallas_call(
        body, name=name,
        grid_spec=pltpu.PrefetchScalarGridSpec(num_scalar_prefetch=1, grid=(2,), in_specs=in_specs, out_specs=out_specs),
        out_shape=[jax.ShapeDtypeStruct((2,) + o.shape, f32) for o in owns],
        compiler_params=_params("parallel"))(ck_arr, *owns, *recvd)


SMALL = ("a_norm", "a_v_norm", "a_w_s", "a_b_s", "f_norm", "f_conv_w", "f_conv_b", "kv_norm", "k_norm",
         "b_norm", "b_q_norm", "b_sinks")
BIG = ("a_w_in", "a_w_out", "f_w_in", "f_w_out", "w_kv", "b_w_q", "b_w_o")
PACK_COLS = 1024
PACK_ROWS = 8 * N_STEPS


def _pack(parts, rows=PACK_ROWS):
    flat = jnp.concatenate([p.reshape(-1).astype(f32) for p in parts])
    pad = (-flat.shape[0]) % (rows * PACK_COLS)
    return jnp.pad(flat, (0, pad)).reshape(-1, PACK_COLS)


def _unpack(packed, shapes):
    flat = packed.reshape(-1)
    out, off = [], 0
    for s in shapes:
        size = math.prod(s)
        out.append(flat[off:off + size].reshape(s))
        off += size
    return out


def _ffn_fwd(x, g, h, r, w_in4, conv_w, conv_b, f, tag):
    wg, wu = conv_w[:, :f], conv_w[:, f:]
    bg, bu = conv_b[None, :f], conv_b[None, f:]
    pg, pu, gate, up, a = ffn_in_fused(h, w_in4, wg, wu, bg, bu, name=f"ffn{tag}_in")
    return a, (x, g, h, r, pg, pu, gate, up, a, wg, wu)


def _ffn_bwd(dy, saved, w_in4, w_out, c_arr, tag, exchange=False):
    x, g, h, r, pg, pu, gate, up, a, wg, wu = saved
    f = w_out.shape[0]
    d_w_out = mm_tn(a, [dy], c_arr, name=f"ffn{tag}_dwout", n_s=w_out.shape[1], shard_rows=f // N_SHARDS, tki=f // 2)
    dpg, dpu, sg, su = ffn_gate_bwd(dy, w_out, pg, pu, gate, up, wg, wu, name=f"ffn{tag}_dgate")
    d_w_in = mm_tn(h, [dpg, dpu], c_arr, name=f"ffn{tag}_dwin", n_s=w_in4.shape[2], shard_rows=h.shape[1])
    state = None
    if exchange:
        state, token = sibling_start([d_w_in[1], d_w_out[1]], d_w_in[1], name=f"rs_sibling_start_ffn{tag}")
        g = g + token[0, 0]
    dx, dg = mm_nt_rms_bwd([dpg, dpu], w_in4, x, r, g, dy, name=f"ffn{tag}_dh")
    d_conv_w = jnp.concatenate([sg[0:3], su[0:3]], axis=1)
    d_conv_b = jnp.concatenate([sg[3], su[3]], axis=0)
    return dx, dg, d_w_in, d_conv_w, d_conv_b, d_w_out, state


def _rs_front(pairs, sibling_state, after, c_arr, tag):
    units = [full.reshape(N_SHARDS, -1, full.shape[-1]) for full, _ in pairs]
    from_sib = sibling_wait(sibling_state, after, name=f"rs_sibling_wait{tag}")
    return rs_add_sibling(units, from_sib, c_arr, name=f"rs_add{tag}")


def _rs_back(own, from_chips, c_arr, tag):
    halves = rs_sum_chips(list(own), list(from_chips), c_arr, name=f"rs_sum{tag}")
    return [m.reshape(-1, m.shape[2]) for m in sibling_merge(list(halves), name=f"rs_merge{tag}")]


def kernel(x, a_norm, a_w_in, a_v_norm, a_w_s, a_b_s, a_w_out, f_norm, f_w_in, f_conv_w, f_conv_b, f_w_out, kv_norm, w_kv, k_norm, b_norm, b_w_q, b_q_norm, b_sinks, b_w_o, loss_target, m_a_norm, m_a_w_in, m_a_v_norm, m_a_w_s, m_a_b_s, m_a_w_out, m_f_norm, m_f_w_in, m_f_conv_w, m_f_conv_b, m_f_w_out, m_kv_norm, m_w_kv, m_k_norm, m_b_norm, m_b_w_q, m_b_q_norm, m_b_sinks, m_b_w_o, v_a_norm, v_a_w_in, v_a_v_norm, v_a_w_s, v_a_b_s, v_a_w_out, v_f_norm, v_f_w_in, v_f_conv_w, v_f_conv_b, v_f_w_out, v_kv_norm, v_w_kv, v_k_norm, v_b_norm, v_b_w_q, v_b_q_norm, v_b_sinks, v_b_w_o):
    args = dict(locals())
    weights = {n: args[n] for n in SMALL + BIG}
    moms = {n: args["m_" + n] for n in SMALL + BIG}
    vars_ = {n: args["v_" + n] for n in SMALL + BIG}
    t, d = x.shape[1], x.shape[2]
    xi, yi, ci = _place()
    chip = 2 * xi + yi

    big_local = [a_w_in[0], a_w_out[0], f_w_in, f_w_out, w_kv, b_w_q[0], b_w_o[0]]
    c_arr = jnp.stack([ci, chip]).astype(jnp.int32)
    k_arr = jnp.stack([chip]).astype(jnp.int32)
    b_ain, b_aout, b_fin0, b_fin1, b_fout0, b_fout1, b_kv, b_q, b_o = cast_into_slot(big_local, k_arr, name="cast_weights")
    small_cols = _pack([a_norm, a_v_norm, f_conv_w], rows=8)
    b_small = lax.dynamic_update_slice(jnp.zeros((N_SHARDS,) + small_cols.shape, f32), small_cols[None], (chip, 0, 0))
    g_small, w_a_in, g_a_w_out = gather_shards([b_small, b_ain, b_aout], name="gather_first", split=[False, True, True])
    later, send_sems, recv_sems, token = gather_start([b_fin0, b_fout0, b_kv, b_q, b_o, b_fin1, b_fout1],
                                                      [[0], [1, 2, 3, 4], [5, 6]], g_small, name="gather_start")
    ns_cols = a_norm.shape[1]
    nf_cols = f_conv_w.shape[2]
    parts = [_unpack(g_small[k], [a_norm.shape, a_v_norm.shape, f_conv_w.shape]) for k in range(N_SHARDS)]
    a_norm_f = jnp.concatenate([p[0] for p in parts], axis=1) + token[0, 0]
    a_v_norm_f = jnp.concatenate([p[1] for p in parts], axis=1)
    conv_w_f = jnp.concatenate([p[2] for p in parts], axis=2)

    x0 = x[0]
    tril = jnp.tril(jnp.ones((CHUNK, CHUNK), dtype=bool))
    wc = jnp.where(tril[None], a_w_s[0], 0.0).astype(bf16)
    bt = a_b_s[0].T
    kg2 = jnp.tile(k_norm, 2)[None]
    qg2 = jnp.tile(b_q_norm[0], 2)[None]

    (h_a,), r_a = rms_fwd(x0, [a_norm_f], name="a_norm")
    zu = mm_nn(h_a, w_a_in, name="a_in_u", s0=0, ns=2)
    zv = mm_nn(h_a, w_a_in, name="a_in_v", s0=2, ns=2)
    y_a = sgu_gate_fwd(zu, zv, a_v_norm_f, wc, bt, name="a_gate")
    w_a_out = g_a_w_out.reshape(1, -1, d)
    f = f_w_out.shape[1] * N_SHARDS
    x1, (h_f0,), r_f0 = mm_residual(y_a, w_a_out[0], x0, name="a_out", gains=[f_norm[0:1]])
    (g_fin0,) = gather_wait(later[0:1], send_sems[0], recv_sems[0], x1, name="gather_wait_0")
    w_f_in = [g_fin0, None]
    a0, ffn0 = _ffn_fwd(x1, f_norm[0:1], h_f0, r_f0, w_f_in[0], conv_w_f[0], f_conv_b[0], f, "0")
    g_fout0, g_w_kv, g_b_w_q, g_b_w_o = gather_wait(later[1:5], send_sems[1], recv_sems[1], a0, name="gather_wait_1")
    w_f_out = [g_fout0.reshape(-1, d), None]
    w_kv_f = g_w_kv.reshape(1, d, -1)
    w_q_f = g_b_w_q.reshape(1, d, -1)
    w_o_f = g_b_w_o.reshape(1, -1, d)
    x2, (h_k, h_q), r_b = mm_residual(a0, w_f_out[0], x1, name="ffn0_out", gains=[kv_norm[None], b_norm])
    kv = mm_nn(h_k, w_kv_f, name="kv_proj")
    k2, v2 = kv_post_fwd(kv, kg2, name="kv_post")
    qp = mm_nn(h_q, w_q_f, name="q_proj")
    qn = q_norm_fwd(qp, qg2, name="q_norm", scale=HEAD_DIM ** -0.5)
    o = attn_fwd(qn, k2, v2, b_sinks[0], name="attn")
    x3, (h_f1,), r_f1 = mm_residual(o, w_o_f[0], x2, name="o_proj", gains=[f_norm[1:2]])
    g_fin1, g_fout1 = gather_wait(later[5:7], send_sems[2], recv_sems[2], x3, name="gather_wait_2")
    w_f_in[1] = g_fin1
    w_f_out[1] = g_fout1.reshape(-1, d)
    a1, ffn1 = _ffn_fwd(x3, f_norm[1:2], h_f1, r_f1, w_f_in[1], conv_w_f[1], f_conv_b[1], f, "1")
    dx4, sq = mm_residual(a1, w_f_out[1], x3, name="ffn1_out", target=loss_target[0])
    loss_part = (0.5 * jnp.sum(sq) / d).reshape(1)

    proj_rows = d // N_SHARDS
    dx3, d_fn1, d_fwin1, d_cw1, d_cb1, d_fwout1, _ = _ffn_bwd(dx4, ffn1, w_f_in[1], w_f_out[1], c_arr, "1")
    do = mm_nt([dx3], w_o_f, name="o_proj_dx")
    d_w_o = mm_tn(o, [dx3], c_arr, name="o_proj_dw", n_s=d, shard_rows=o.shape[1] // N_SHARDS)
    dqn, dk2, dv2, dsink = attn_bwd(qn, k2, v2, do, b_sinks[0], name="attn_bwd")
    dqp, dqg = q_norm_bwd(dqn, qp, qg2, name="q_norm_bwd", scale=HEAD_DIM ** -0.5)
    dkv, dkg = kv_post_bwd(dk2, dv2, kv, kg2, name="kv_post_bwd")
    d_w_q = mm_tn(h_q, [dqp], c_arr, name="q_proj_dw", n_s=w_q_f.shape[2], shard_rows=proj_rows)
    d_w_kv = mm_tn(h_k, [dkv], c_arr, name="kv_proj_dw", n_s=w_kv_f.shape[2], shard_rows=proj_rows)
    group1 = [d_fwin1, d_fwout1, d_w_kv, d_w_q, d_w_o]
    sib1, token_s1 = sibling_start([half for _, half in group1], d_w_kv[1], name="rs_sibling_start1")
    dh_k = mm_nt([dkv], w_kv_f, name="kv_proj_dx")
    dx2, d_bn, d_kvn = mm_nt_rms_bwd([dqp], w_q_f, x2, r_b, b_norm + token_s1[0, 0], dx3, name="q_proj_dx",
                                     extra=(dh_k, kv_norm[None]))
    chip_bf1, own1 = _rs_front(group1, sib1, dx2, c_arr, "1")
    scatter1, token1 = chip_scatter_start(list(chip_bf1), None, dx2, name="rs_chips_start1")
    ffn0 = ffn0[:9] + (ffn0[9] + token1[0, 0],) + ffn0[10:]
    dx1, d_fn0, d_fwin0, d_cw0, d_cb0, d_fwout0, sib2 = _ffn_bwd(dx2, ffn0, w_f_in[0], w_f_out[0], c_arr, "0", exchange=True)
    chip_bf2, own2 = _rs_front([d_fwin0, d_fwout0], sib2, dx1, c_arr, "2")
    scatter2, token2 = chip_scatter_start(list(chip_bf2), None, dx1, name="rs_chips_start2")
    a_v_norm_f = a_v_norm_f + token2[0, 0]
    dy_a = mm_nt([dx1], w_a_out, name="a_out_dx")
    d_w_aout = mm_tn(y_a, [dx1], c_arr, name="a_out_dw", n_s=d, shard_rows=y_a.shape[1] // N_SHARDS)
    dzu, dzv, d_avn, d_ws, d_bt = sgu_gate_bwd(zu, zv, dy_a, a_v_norm_f, wc, bt, name="a_gate_bwd")
    d_w_ain = mm_tn(h_a, [dzu, dzv], c_arr, name="a_in_dw", n_s=w_a_in.shape[2], shard_rows=d)
    sib3, token_s3 = sibling_start([d_w_ain[1], d_w_aout[1]], d_w_ain[1], name="rs_sibling_start3")
    dx0, d_an = mm_nt_rms_bwd([dzu, dzv], w_a_in, x0, r_a, a_norm_f + token_s3[0, 0], dx1, name="a_in_dx")
    grad_x = dx0[None]

    chip_bf3, own3 = _rs_front([d_w_ain, d_w_aout], sib3, dx0, c_arr, "3")
    d_fn = jnp.concatenate([d_fn0, d_fn1], axis=0)
    d_cw = jnp.stack([d_cw0, d_cw1])
    d_cb = jnp.stack([d_cb0, d_cb1])
    d_kg = (dkg[0, :HEAD_DIM] + dkg[0, HEAD_DIM:])
    d_qg = (dqg[0, :HEAD_DIM] + dqg[0, HEAD_DIM:])[None]
    small_full = [d_an, d_avn, d_ws[None], d_bt.T[None], d_fn, d_cw, d_cb, d_kvn[0], d_kg, d_bn, d_qg,
                  dsink[:, :N_Q_HEADS], loss_part]
    packed = _pack(small_full)
    me = 4 * xi + 2 * yi + ci
    everyone = lax.dynamic_update_slice(lax.empty((N_DEV,) + packed.shape, f32), packed[None], (me, 0, 0))
    scatter3, _ = chip_scatter_start(list(chip_bf3), everyone, chip_bf3[0], name="rs_chips_start3")
    from_chips1, _ = chip_scatter_wait(scatter1, scatter3[2][0], name="rs_chips_wait1")
    fin1, fout1, gkv, gq, go = _rs_back(own1, from_chips1, c_arr, "1")
    from_chips2, _ = chip_scatter_wait(scatter2, fin1, name="rs_chips_wait2")
    fin0, fout0 = _rs_back(own2, from_chips2, c_arr, "2")
    late = ("f_w_in", "f_w_out", "w_kv", "b_w_q", "b_w_o")
    res_late = adamw([weights[n] for n in late], [[fin0, fin1], [fout0, fout1], [gkv], [gq], [go]],
                     [moms[n] for n in late], [vars_[n] for n in late], name="adamw_late")
    from_chips3, from_all = chip_scatter_wait(scatter3, res_late[1][2], name="rs_chips_wait3")
    ain, aout = _rs_back(own3, from_chips3, c_arr, "3")
    first = ("a_w_in", "a_w_out")
    res_first = adamw([weights[n] for n in first], [[ain], [aout]], [moms[n] for n in first],
                      [vars_[n] for n in first], name="adamw_first")
    big = {n: tuple(r[i] for r in res_late) for i, n in enumerate(late)}
    big.update({n: tuple(r[i] for r in res_first) for i, n in enumerate(first)})

    full_shapes = [g.shape for g in small_full]
    small_g = _unpack(sum_leading(from_all, name="small_sum"), full_shapes)
    loss = small_g.pop()[0]
    small_g[0] = lax.dynamic_slice_in_dim(small_g[0], chip * ns_cols, ns_cols, axis=1)
    small_g[1] = lax.dynamic_slice_in_dim(small_g[1], chip * ns_cols, ns_cols, axis=1)
    small_g[5] = lax.dynamic_slice_in_dim(small_g[5], chip * nf_cols, nf_cols, axis=2)
    small_shapes = [weights[n].shape for n in SMALL]
    small_g = [g.reshape(s) for g, s in zip(small_g, small_shapes)]
    flat2 = [(math.prod(s[:-1]), s[-1]) for s in small_shapes]
    small_d, small_m, small_v = adamw_small(
        *[[a.reshape(s2) for a, s2 in zip(group, flat2)]
          for group in ([weights[n] for n in SMALL], small_g, [moms[n] for n in SMALL], [vars_[n] for n in SMALL])],
        name="adamw_small")
    small_d, small_m, small_v = ([a.reshape(s) for a, s in zip(group, small_shapes)]
                                 for group in (small_d, small_m, small_v))

    out = {}
    for i, n in enumerate(SMALL):
        out[n] = (small_g[i], small_d[i], small_m[i], small_v[i])
    out.update(big)
    order = ["a_norm", "a_w_in", "a_v_norm", "a_w_s", "a_b_s", "a_w_out", "f_norm", "f_w_in", "f_conv_w", "f_conv_b",
             "f_w_out", "kv_norm", "w_kv", "k_norm", "b_norm", "b_w_q", "b_q_norm", "b_sinks", "b_w_o"]
    return (loss, grad_x, *[out[n][0] for n in order], *[out[n][1] for n in order],
            *[out[n][2] for n in order], *[out[n][3] for n in order])
```

```python
import functools
import math

import jax
import jax.numpy as jnp
from jax import lax
from jax.experimental import pallas as pl
from jax.experimental.pallas import tpu as pltpu

f32 = jnp.float32
bf16 = jnp.bfloat16
MESH = pl.DeviceIdType.MESH
ANY = pl.BlockSpec(memory_space=pl.ANY)

EPS = 1e-6
LANES = 128
CHUNK = 128
HEAD_DIM = 64
N_Q_HEADS = 16
N_KV_HEADS = 4
Q_PER_KV = N_Q_HEADS // N_KV_HEADS
N_SHARDS = 4
N_DEV = 8

ADAM_LR = 0.001
ADAM_B1 = 0.9
ADAM_B2 = 0.999
ADAM_EPS = 1e-08
ADAM_WD = 0.01
ADAM_STEP = 10
ADAM_C1 = 1.0 - ADAM_B1 ** ADAM_STEP
ADAM_C2 = 1.0 - ADAM_B2 ** ADAM_STEP

_INV_SQRT2 = 1.0 / math.sqrt(2.0)
_INV_SQRT2PI = 1.0 / math.sqrt(2.0 * math.pi)


def _params(*sem):
    return pltpu.CompilerParams(dimension_semantics=sem)


def _gelu(z):
    return 0.5 * z * (1.0 + lax.erf(z * _INV_SQRT2))


def _gelu_and_grad(z):
    cdf = 0.5 * (1.0 + lax.erf(z * _INV_SQRT2))
    return z * cdf, cdf + z * jnp.exp(-0.5 * z * z) * _INV_SQRT2PI


def _dot(a, b):
    return jnp.dot(a, b, preferred_element_type=f32)


def _dot_nt(a, b):
    return lax.dot_general(a, b, (((1,), (1,)), ((), ())), preferred_element_type=f32)


def _dot_tn(a, b):
    return lax.dot_general(a, b, (((0,), (0,)), ((), ())), preferred_element_type=f32)


def _dot_split(a, b):
    hi = a.astype(bf16)
    lo = (a - hi.astype(f32)).astype(bf16)
    return _dot(hi, b) + _dot(lo, b)


VMEM_TILE_BUDGET = 40 * 1024 * 1024
MAX_ROW_TILE = 2048


def _row_tile(m, fixed_bytes, row_bytes):
    tm = min(m, MAX_ROW_TILE)
    while tm > 256 and 2 * (fixed_bytes + tm * row_bytes) > VMEM_TILE_BUDGET:
        tm //= 2
    return tm


def _isz(a):
    return jnp.dtype(a.dtype).itemsize


def mm_nn(a, w3, *, name, s0=0, ns=None, add=None, out_dtype=f32):
    m, k = a.shape
    s_all, _, n_s = w3.shape
    ns = s_all if ns is None else ns
    tm = _row_tile(m, k * n_s * 2, k * _isz(a) + n_s * jnp.dtype(out_dtype).itemsize + (0 if add is None else n_s * 4))

    def body(*refs):
        if add is None:
            a_ref, w_ref, o_ref = refs
            acc = _dot(a_ref[...].astype(bf16), w_ref[0])
        else:
            a_ref, w_ref, add_ref, o_ref = refs
            acc = _dot(a_ref[...].astype(bf16), w_ref[0]) + add_ref[...]
        o_ref[...] = acc.astype(out_dtype)

    in_specs = [pl.BlockSpec((tm, k), lambda j, i: (i, 0)),
                pl.BlockSpec((1, k, n_s), lambda j, i: (s0 + j, 0, 0))]
    args = [a, w3]
    if add is not None:
        in_specs.append(pl.BlockSpec((tm, n_s), lambda j, i: (i, j)))
        args.append(add)
    return pl.pallas_call(
        body, name=name, grid=(ns, m // tm), in_specs=in_specs,
        out_specs=pl.BlockSpec((tm, n_s), lambda j, i: (i, j)),
        out_shape=jax.ShapeDtypeStruct((m, ns * n_s), out_dtype),
        compiler_params=_params("parallel", "parallel"))(*args)


def mm_nt(a_list, w3, *, name, tko=None, add=None, out_dtype=f32):
    s_all, k_out, n_s = w3.shape
    m = a_list[0].shape[0]
    na = len(a_list)
    spa = s_all // na
    tko = k_out if tko is None else tko
    tm = _row_tile(m, tko * n_s * 2, na * n_s * _isz(a_list[0]) + tko * 4 * (1 if add is None else 2))

    def body(*refs):
        a_refs = refs[:na]
        w_ref = refs[na]
        o_ref = refs[-1]
        s = pl.program_id(2)

        @pl.when(s == 0)
        def _():
            if add is None:
                o_ref[...] = jnp.zeros_like(o_ref)
            else:
                o_ref[...] = refs[na + 1][...]

        for idx in range(na):
            @pl.when(s // spa == idx)
            def _(idx=idx):
                o_ref[...] += _dot_nt(a_refs[idx][...].astype(bf16), w_ref[0])

    def a_map(idx):
        return lambda ko, i, s: (i, jnp.clip(s - idx * spa, 0, spa - 1))

    in_specs = [pl.BlockSpec((tm, n_s), a_map(idx)) for idx in range(na)]
    in_specs.append(pl.BlockSpec((1, tko, n_s), lambda ko, i, s: (s, ko, 0)))
    args = list(a_list) + [w3]
    if add is not None:
        in_specs.append(pl.BlockSpec((tm, tko), lambda ko, i, s: (i, ko)))
        args.append(add)
    return pl.pallas_call(
        body, name=name, grid=(k_out // tko, m // tm, s_all), in_specs=in_specs,
        out_specs=pl.BlockSpec((tm, tko), lambda ko, i, s: (i, ko)),
        out_shape=jax.ShapeDtypeStruct((m, k_out), out_dtype),
        compiler_params=_params("parallel", "parallel", "arbitrary"))(*args)


def mm_tn(a, b_list, c_arr, *, name, n_s, shard_rows, tki=None):
    m, k_in = a.shape
    na = len(b_list)
    s_all = sum(b.shape[1] for b in b_list) // n_s
    spa = s_all // na
    tki = k_in if tki is None else tki
    tm = _row_tile(m, tki * n_s * 4, tki * _isz(a) + na * n_s * _isz(b_list[0]))

    nsteps = m // tm
    per_blk = tki // shard_rows
    half = shard_rows // 2

    def body(c_ref, *refs):
        a_ref = refs[0]
        b_refs = refs[1:1 + na]
        o_ref, ob_ref = refs[-2], refs[-1]
        s = pl.program_id(0)
        r = pl.program_id(2)

        @pl.when(r == 0)
        def _():
            o_ref[...] = jnp.zeros_like(o_ref)

        for idx in range(na):
            @pl.when(s // spa == idx)
            def _(idx=idx):
                o_ref[0] += _dot_tn(a_ref[...].astype(bf16), b_refs[idx][...].astype(bf16))

        @pl.when(r == nsteps - 1)
        def _():
            for q in range(per_blk):
                start = pl.multiple_of(q * shard_rows + (1 - c_ref[0]) * half, 16)
                ob_ref[q] = o_ref[0, pl.ds(start, half), :].astype(bf16)

    def b_map(idx):
        def index(s, ki, r, c_ref):
            active = (s // spa) == idx
            return (jnp.where(active, r, 0), jnp.clip(s - idx * spa, 0, spa - 1))
        return index

    in_specs = [pl.BlockSpec((tm, tki), lambda s, ki, r, c_ref: (r, ki))]
    in_specs += [pl.BlockSpec((tm, n_s), b_map(idx)) for idx in range(na)]
    n_blk = k_in // tki
    return pl.pallas_call(
        body, name=name,
        grid_spec=pltpu.PrefetchScalarGridSpec(
            num_scalar_prefetch=1, grid=(s_all, n_blk, nsteps), in_specs=in_specs,
            out_specs=[pl.BlockSpec((1, tki, n_s), lambda s, ki, r, c_ref: (s, ki, 0)),
                       pl.BlockSpec((per_blk, half, n_s), lambda s, ki, r, c_ref: (s * n_blk + ki, 0, 0))]),
        out_shape=[jax.ShapeDtypeStruct((s_all, k_in, n_s), f32),
                   jax.ShapeDtypeStruct((s_all * k_in // shard_rows, half, n_s), bf16)],
        compiler_params=_params("parallel", "parallel", "arbitrary"))(c_arr, a, *b_list)


def mm_nt_rms_bwd(a_list, w3, x, r, g, dx_in, *, name, extra=None):
    s_all, d, n_s = w3.shape
    m = a_list[0].shape[0]
    na = len(a_list)
    spa = s_all // na
    ne = 0 if extra is None else 1
    tm = _row_tile(m, d * n_s * 2, na * n_s * _isz(a_list[0]) + d * 4 * (4 + ne))

    def body(*refs):
        a_refs, w_ref = refs[:na], refs[na]
        x_ref, r_ref, g_ref, dxin_ref = refs[na + 1:na + 5]
        dh2_ref, g2_ref = (refs[na + 5], refs[na + 6]) if ne else (None, None)
        outs = refs[na + 5 + 2 * ne:]
        dx_ref, dg_ref = outs[0], outs[1]
        dg2_ref = outs[2] if ne else None
        acc_ref = outs[-1]
        i, s = pl.program_id(0), pl.program_id(1)

        @pl.when(s == 0)
        def _():
            acc_ref[...] = jnp.zeros_like(acc_ref)

        for idx in range(na):
            @pl.when(s // spa == idx)
            def _(idx=idx):
                acc_ref[...] += _dot_nt(a_refs[idx][...].astype(bf16), w_ref[0])

        @pl.when(s == s_all - 1)
        def _():
            rv = r_ref[...]
            xh = x_ref[...] * rv
            total = dxin_ref[...]
            pairs = [(acc_ref[...], g_ref, dg_ref)] + ([(dh2_ref[...], g2_ref, dg2_ref)] if ne else [])
            for dh, gain_ref, dgain_ref in pairs:
                part = jnp.sum(dh * xh, axis=0, keepdims=True)

                @pl.when(i == 0)
                def _(dgain_ref=dgain_ref, part=part):
                    dgain_ref[...] = part

                @pl.when(i > 0)
                def _(dgain_ref=dgain_ref, part=part):
                    dgain_ref[...] += part

                tg = dh * gain_ref[...]
                total = total + rv * (tg - xh * jnp.mean(tg * xh, axis=1, keepdims=True))
            dx_ref[...] = total

    def a_map(idx):
        return lambda i, s: (i, jnp.clip(s - idx * spa, 0, spa - 1))

    row = pl.BlockSpec((tm, d), lambda i, s: (i, 0))
    vec = pl.BlockSpec((1, d), lambda i, s: (0, 0))
    in_specs = [pl.BlockSpec((tm, n_s), a_map(idx)) for idx in range(na)]
    in_specs += [pl.BlockSpec((1, d, n_s), lambda i, s: (s, 0, 0)), row, pl.BlockSpec((tm, 1), lambda i, s: (i, 0)), vec, row]
    args = list(a_list) + [w3, x, r, g, dx_in]
    if ne:
        in_specs += [row, vec]
        args += list(extra)
    outs = pl.pallas_call(
        body, name=name, grid=(m // tm, s_all), in_specs=in_specs, out_specs=[row] + [vec] * (1 + ne),
        out_shape=[jax.ShapeDtypeStruct((m, d), f32)] + [jax.ShapeDtypeStruct((1, d), f32)] * (1 + ne),
        scratch_shapes=[pltpu.VMEM((tm, d), f32)],
        compiler_params=_params("arbitrary", "arbitrary"))(*args)
    return outs


def mm_residual(a, w, x, *, name, gains=(), target=None):
    m, k = a.shape
    d = w.shape[1]
    ng = len(gains)
    tm = _row_tile(m, k * d * 2, k * _isz(a) + d * 4 * 3 + ng * d * 2)

    def body(*refs):
        a_ref, w_ref, x_ref = refs[:3]
        y = _dot(a_ref[...].astype(bf16), w_ref[...]) + x_ref[...]
        if target is None:
            g_refs = refs[3:3 + ng]
            y_ref = refs[3 + ng]
            h_refs = refs[4 + ng:4 + 2 * ng]
            r_ref = refs[-1]
            y_ref[...] = y
            r = lax.rsqrt(jnp.mean(y * y, axis=1, keepdims=True) + EPS)
            yh = y * r
            for g_ref, h_ref in zip(g_refs, h_refs):
                h_ref[...] = (yh * g_ref[...]).astype(bf16)
            r_ref[...] = r
        else:
            t_ref, dy_ref, s_ref = refs[3:]
            i = pl.program_id(0)
            e = y - t_ref[...]
            dy_ref[...] = e * (1.0 / d)
            part = jnp.sum(e * e, axis=0, keepdims=True)

            @pl.when(i == 0)
            def _():
                s_ref[...] = part

            @pl.when(i > 0)
            def _():
                s_ref[...] += part

    row = pl.BlockSpec((tm, d), lambda i: (i, 0))
    vec = pl.BlockSpec((1, d), lambda i: (0, 0))
    in_specs = [pl.BlockSpec((tm, k), lambda i: (i, 0)), pl.BlockSpec((k, d), lambda i: (0, 0)), row]
    if target is None:
        outs = pl.pallas_call(
            body, name=name, grid=(m // tm,), in_specs=in_specs + [vec] * ng,
            out_specs=[row] * (1 + ng) + [pl.BlockSpec((tm, 1), lambda i: (i, 0))],
            out_shape=[jax.ShapeDtypeStruct((m, d), f32)] + [jax.ShapeDtypeStruct((m, d), bf16)] * ng
            + [jax.ShapeDtypeStruct((m, 1), f32)],
            compiler_params=_params("parallel"))(a, w, x, *gains)
        return outs[0], outs[1:1 + ng], outs[-1]
    return pl.pallas_call(
        body, name=name, grid=(m // tm,), in_specs=in_specs + [row], out_specs=[row, vec],
        out_shape=[jax.ShapeDtypeStruct((m, d), f32), jax.ShapeDtypeStruct((1, d), f32)],
        compiler_params=_params("arbitrary"))(a, w, x, target)


def rms_fwd(x, gains, *, name, tr=512):
    t, d = x.shape
    tr = min(tr, t)
    ng = len(gains)

    def body(*refs):
        x_ref = refs[0]
        g_refs = refs[1:1 + ng]
        h_refs = refs[1 + ng:1 + 2 * ng]
        r_ref = refs[-1]
        xv = x_ref[...]
        r = lax.rsqrt(jnp.mean(xv * xv, axis=1, keepdims=True) + EPS)
        xh = xv * r
        for g_ref, h_ref in zip(g_refs, h_refs):
            h_ref[...] = (xh * g_ref[...]).astype(bf16)
        r_ref[...] = r

    row = pl.BlockSpec((tr, d), lambda i: (i, 0))
    vec = pl.BlockSpec((1, d), lambda i: (0, 0))
    outs = pl.pallas_call(
        body, name=name, grid=(t // tr,), in_specs=[row] + [vec] * ng,
        out_specs=[row] * ng + [pl.BlockSpec((tr, 1), lambda i: (i, 0))],
        out_shape=[jax.ShapeDtypeStruct((t, d), bf16)] * ng + [jax.ShapeDtypeStruct((t, 1), f32)],
        compiler_params=_params("parallel"))(x, *gains)
    return outs[:ng], outs[ng]


def rms_bwd(dh_list, x, r, gains, dx_in, *, name, tr=512):
    t, d = x.shape
    tr = min(tr, t)
    ng = len(gains)

    def body(*refs):
        dh_refs = refs[:ng]
        x_ref, r_ref = refs[ng], refs[ng + 1]
        g_refs = refs[ng + 2:2 * ng + 2]
        dxin_ref = refs[2 * ng + 2]
        dx_ref = refs[2 * ng + 3]
        dg_refs = refs[2 * ng + 4:]
        i = pl.program_id(0)
        rv = r_ref[...]
        xh = x_ref[...] * rv
        acc = dxin_ref[...]
        for dh_ref, g_ref, dg_ref in zip(dh_refs, g_refs, dg_refs):
            dh = dh_ref[...]
            part = jnp.sum(dh * xh, axis=0, keepdims=True)

            @pl.when(i == 0)
            def _(dg_ref=dg_ref, part=part):
                dg_ref[...] = part

            @pl.when(i > 0)
            def _(dg_ref=dg_ref, part=part):
                dg_ref[...] += part

            tg = dh * g_ref[...]
            acc = acc + rv * (tg - xh * jnp.mean(tg * xh, axis=1, keepdims=True))
        dx_ref[...] = acc

    row = pl.BlockSpec((tr, d), lambda i: (i, 0))
    vec = pl.BlockSpec((1, d), lambda i: (0, 0))
    outs = pl.pallas_call(
        body, name=name, grid=(t // tr,),
        in_specs=[row] * ng + [row, pl.BlockSpec((tr, 1), lambda i: (i, 0))] + [vec] * ng + [row],
        out_specs=[row] + [vec] * ng,
        out_shape=[jax.ShapeDtypeStruct((t, d), f32)] + [jax.ShapeDtypeStruct((1, d), f32)] * ng,
        compiler_params=_params("arbitrary"))(*dh_list, x, r, *gains, dx_in)
    return outs[0], outs[1:]


def sgu_gate_fwd(zu, zv, gv, wc, bt, *, name, tr=512):
    t, w = zu.shape
    tr = min(tr, t)
    groups = w // LANES

    def body(zu_ref, zv_ref, gv_ref, wc_ref, bt_ref, y_ref):
        vp = _gelu(zv_ref[...])
        rv = lax.rsqrt(jnp.mean(vp * vp, axis=1, keepdims=True) + EPS)
        vb = (vp * rv * gv_ref[...]).astype(bf16)
        for c in range(tr // CHUNK):
            rows = slice(c * CHUNK, (c + 1) * CHUNK)
            for g in range(groups):
                cols = slice(g * LANES, (g + 1) * LANES)
                sv = _dot(wc_ref[g], vb[rows, cols]) + bt_ref[:, g:g + 1]
                y_ref[rows, cols] = (_gelu(zu_ref[rows, cols]) * sv).astype(bf16)

    row = pl.BlockSpec((tr, w), lambda i: (i, 0))
    return pl.pallas_call(
        body, name=name, grid=(t // tr,),
        in_specs=[row, row, pl.BlockSpec((1, w), lambda i: (0, 0)),
                  pl.BlockSpec((groups, CHUNK, CHUNK), lambda i: (0, 0, 0)),
                  pl.BlockSpec((CHUNK, groups), lambda i: (0, 0))],
        out_specs=row, out_shape=jax.ShapeDtypeStruct((t, w), bf16),
        compiler_params=_params("parallel"))(zu, zv, gv, wc, bt)


def sgu_gate_bwd(zu, zv, dy, gv, wc, bt, *, name, tr=512):
    t, w = zu.shape
    tr = min(tr, t)
    groups = w // LANES
    nsteps = t // tr

    def body(zu_ref, zv_ref, dy_ref, gv_ref, wc_ref, bt_ref,
             dzu_ref, dzv_ref, dgv_ref, dws_ref, dbt_ref, dv_ref, bacc_ref):
        i = pl.program_id(0)

        @pl.when(i == 0)
        def _():
            dgv_ref[...] = jnp.zeros_like(dgv_ref)
            dws_ref[...] = jnp.zeros_like(dws_ref)
            bacc_ref[...] = jnp.zeros_like(bacc_ref)

        vp, vp_grad = _gelu_and_grad(zv_ref[...])
        rv = lax.rsqrt(jnp.mean(vp * vp, axis=1, keepdims=True) + EPS)
        vhat = vp * rv
        vb = (vhat * gv_ref[...]).astype(bf16)
        for c in range(tr // CHUNK):
            rows = slice(c * CHUNK, (c + 1) * CHUNK)
            for g in range(groups):
                cols = slice(g * LANES, (g + 1) * LANES)
                vblk = vb[rows, cols]
                sv = _dot(wc_ref[g], vblk) + bt_ref[:, g:g + 1]
                zub = zu_ref[rows, cols]
                dyb = dy_ref[rows, cols]
                ub, ub_grad = _gelu_and_grad(zub)
                dzu_ref[rows, cols] = (dyb * sv * ub_grad).astype(bf16)
                dsv = dyb * ub
                bacc_ref[:, cols] += dsv
                dsvb = dsv.astype(bf16)
                dv_ref[rows, cols] = _dot_tn(wc_ref[g], dsvb)
                dws_ref[g] += _dot_nt(dsvb, vblk)
        dv = dv_ref[...]
        dgv_ref[...] += jnp.sum(dv * vhat, axis=0, keepdims=True)
        tg = dv * gv_ref[...]
        dvp = rv * (tg - vhat * jnp.mean(tg * vhat, axis=1, keepdims=True))
        dzv_ref[...] = (dvp * vp_grad).astype(bf16)

        @pl.when(i == nsteps - 1)
        def _():
            tt = lax.broadcasted_iota(jnp.int32, (CHUNK, CHUNK), 0)
            ss = lax.broadcasted_iota(jnp.int32, (CHUNK, CHUNK), 1)
            for g in range(groups):
                dws_ref[g] = jnp.where(ss <= tt, dws_ref[g], 0.0)
                dbt_ref[:, g:g + 1] = jnp.sum(bacc_ref[:, g * LANES:(g + 1) * LANES], axis=1, keepdims=True)

    row = pl.BlockSpec((tr, w), lambda i: (i, 0))
    full3 = pl.BlockSpec((groups, CHUNK, CHUNK), lambda i: (0, 0, 0))
    return pl.pallas_call(
        body, name=name, grid=(nsteps,),
        in_specs=[row, row, row, pl.BlockSpec((1, w), lambda i: (0, 0)), full3,
                  pl.BlockSpec((CHUNK, groups), lambda i: (0, 0))],
        out_specs=[row, row, pl.BlockSpec((1, w), lambda i: (0, 0)), full3,
                   pl.BlockSpec((CHUNK, groups), lambda i: (0, 0))],
        out_shape=[jax.ShapeDtypeStruct((t, w), bf16), jax.ShapeDtypeStruct((t, w), bf16),
                   jax.ShapeDtypeStruct((1, w), f32), jax.ShapeDtypeStruct((groups, CHUNK, CHUNK), f32),
                   jax.ShapeDtypeStruct((CHUNK, groups), f32)],
        scratch_shapes=[pltpu.VMEM((tr, w), f32), pltpu.VMEM((CHUNK, w), f32)],
        compiler_params=_params("arbitrary"))(zu, zv, dy, gv, wc, bt)


HALO = 8


def _shift_down(v, halo, k, first):
    r = pltpu.roll(v, k, 0)
    hh = jnp.where(first, 0.0, pltpu.roll(halo, k, 0))
    rid = lax.broadcasted_iota(jnp.int32, (HALO, v.shape[1]), 0)
    head = jnp.where(rid < k, hh, r[0:HALO])
    if v.shape[0] == HALO:
        return head
    return jnp.concatenate([head, r[HALO:]], axis=0)


def _shift_up(v, halo, k, last):
    n = v.shape[0]
    r = pltpu.roll(v, n - k, 0)
    hh = jnp.where(last, 0.0, pltpu.roll(halo, HALO - k, 0))
    rid = lax.broadcasted_iota(jnp.int32, (HALO, v.shape[1]), 0)
    tail = jnp.where(rid >= HALO - k, hh, r[n - HALO:])
    return jnp.concatenate([r[:n - HALO], tail], axis=0)


def _conv(p, halo, w_ref, b_ref, first):
    return (w_ref[2:3, :] * p + w_ref[1:2, :] * _shift_down(p, halo, 1, first)
            + w_ref[0:1, :] * _shift_down(p, halo, 2, first) + b_ref[...])


BF16_ROWS = 16


def ffn_in_fused(h, w_in4, wg, wu, bg, bu, *, name):
    t, k = h.shape
    s_all, _, n_s = w_in4.shape
    half = s_all // 2
    tm = _row_tile(t, 2 * k * n_s * 2, k * 2 + 4 * n_s * 4 + n_s * 2)

    def body(h_ref, hh_ref, wg_ref, wu_ref, cg_ref, cu_ref, bg_ref, bu_ref, pg_ref, pu_ref, gate_ref, up_ref, a_ref):
        first = pl.program_id(1) == 0
        hv, hh = h_ref[...], hh_ref[...]
        outs = []
        for w_ref, c_ref, b_ref, p_ref, o_ref in ((wg_ref, cg_ref, bg_ref, pg_ref, gate_ref),
                                                  (wu_ref, cu_ref, bu_ref, pu_ref, up_ref)):
            p = _dot(hv, w_ref[0])
            p_ref[...] = p
            hu = _conv(p, _dot(hh, w_ref[0])[BF16_ROWS - HALO:], c_ref, b_ref, first)
            o_ref[...] = hu
            outs.append(hu)
        gate, up = outs
        a_ref[...] = (gate * jax.nn.sigmoid(gate) * up).astype(bf16)

    tile = pl.BlockSpec((tm, n_s), lambda j, i: (i, j))
    cw = pl.BlockSpec((3, n_s), lambda j, i: (0, j))
    cb = pl.BlockSpec((1, n_s), lambda j, i: (0, j))
    f = half * n_s
    return pl.pallas_call(
        body, name=name, grid=(half, t // tm),
        in_specs=[pl.BlockSpec((tm, k), lambda j, i: (i, 0)),
                  pl.BlockSpec((BF16_ROWS, k), lambda j, i: (jnp.maximum(i * (tm // BF16_ROWS) - 1, 0), 0)),
                  pl.BlockSpec((1, k, n_s), lambda j, i: (j, 0, 0)),
                  pl.BlockSpec((1, k, n_s), lambda j, i: (j + half, 0, 0)), cw, cw, cb, cb],
        out_specs=[tile] * 5,
        out_shape=[jax.ShapeDtypeStruct((t, f), f32)] * 4 + [jax.ShapeDtypeStruct((t, f), bf16)],
        compiler_params=_params("parallel", "parallel"))(h, h, w_in4, w_in4, wg, wu, bg, bu)


def _gate_grads(gate, up, dav):
    sg = jax.nn.sigmoid(gate)
    return dav * up * (sg * (1.0 + gate * (1.0 - sg))), dav * gate * sg


GATE_BWD_ROWS = 512


def ffn_gate_bwd(dy, w_out, pg, pu, gate, up, wg, wu, *, name):
    t, f = pg.shape
    d = dy.shape[1]
    tr = min(GATE_BWD_ROWS, t)
    nsteps = t // tr
    tc = f // 2

    def body(dy_ref, dyn_ref, w_ref, pg_ref, pu_ref, gate_ref, gaten_ref, up_ref, upn_ref, wg_ref, wu_ref,
             dg_ref, du_ref, sg_ref, su_ref):
        i = pl.program_id(1)
        last = i == nsteps - 1
        w = w_ref[0]
        da = _dot_nt(dy_ref[...].astype(bf16), w)
        da_n = _dot_nt(dyn_ref[...].astype(bf16), w)
        dgate, dup = _gate_grads(gate_ref[...], up_ref[...], da)
        dgate_n, dup_n = _gate_grads(gaten_ref[...], upn_ref[...], da_n)
        rid = lax.broadcasted_iota(jnp.int32, (8, tc), 0)
        for dd, d_n, c_ref, p_ref, o_ref, s_ref in ((dgate, dgate_n, wg_ref, pg_ref, dg_ref, sg_ref),
                                                    (dup, dup_n, wu_ref, pu_ref, du_ref, su_ref)):
            d1, d2 = _shift_up(dd, d_n, 1, last), _shift_up(dd, d_n, 2, last)
            o_ref[...] = (c_ref[2:3, :] * dd + c_ref[1:2, :] * d1 + c_ref[0:1, :] * d2).astype(bf16)
            p = p_ref[...]
            sums = [jnp.sum(d2 * p, axis=0, keepdims=True), jnp.sum(d1 * p, axis=0, keepdims=True),
                    jnp.sum(dd * p, axis=0, keepdims=True), jnp.sum(dd, axis=0, keepdims=True)]
            part = jnp.zeros((8, tc), f32)
            for k, sk in enumerate(sums):
                part = jnp.where(rid == k, sk, part)

            @pl.when(i == 0)
            def _(s_ref=s_ref, part=part):
                s_ref[...] = part

            @pl.when(i > 0)
            def _(s_ref=s_ref, part=part):
                s_ref[...] += part

    def nxt_rows(j, i):
        return (jnp.minimum((i + 1) * (tr // HALO), t // HALO - 1), j)

    tile = pl.BlockSpec((tr, tc), lambda j, i: (i, j))
    nxt = pl.BlockSpec((HALO, tc), nxt_rows)
    wspec = pl.BlockSpec((3, tc), lambda j, i: (0, j))
    stat = pl.BlockSpec((8, tc), lambda j, i: (0, j))
    return pl.pallas_call(
        body, name=name, grid=(2, nsteps),
        in_specs=[pl.BlockSpec((tr, d), lambda j, i: (i, 0)),
                  pl.BlockSpec((HALO, d), lambda j, i: (nxt_rows(j, i)[0], 0)),
                  pl.BlockSpec((1, tc, d), lambda j, i: (j, 0, 0)),
                  tile, tile, tile, nxt, tile, nxt, wspec, wspec],
        out_specs=[tile, tile, stat, stat],
        out_shape=[jax.ShapeDtypeStruct((t, f), bf16), jax.ShapeDtypeStruct((t, f), bf16),
                   jax.ShapeDtypeStruct((8, f), f32), jax.ShapeDtypeStruct((8, f), f32)],
        compiler_params=_params("parallel", "arbitrary"))(
            dy, dy, w_out.reshape(2, tc, d), pg, pu, gate, gate, up, up, wg, wu)


def _head_mean_matrix():
    i = lax.broadcasted_iota(jnp.int32, (LANES, LANES), 0) // HEAD_DIM
    j = lax.broadcasted_iota(jnp.int32, (LANES, LANES), 1) // HEAD_DIM
    return jnp.where(i == j, 1.0 / HEAD_DIM, 0.0).astype(bf16)


def _lane_half(shape):
    return (lax.broadcasted_iota(jnp.int32, shape, 1) % LANES) // HEAD_DIM


def q_norm_fwd(qp, g2, *, name, scale, tr=512):
    t, w = qp.shape
    tr = min(tr, t)

    def body(x_ref, g_ref, o_ref):
        bd = _head_mean_matrix()
        for cb in range(w // LANES):
            cols = slice(cb * LANES, (cb + 1) * LANES)
            xc = x_ref[:, cols]
            rh = lax.rsqrt(_dot_split(xc * xc, bd) + EPS)
            o_ref[:, cols] = (xc * rh * g_ref[...] * scale).astype(bf16)

    row = pl.BlockSpec((tr, w), lambda i: (i, 0))
    return pl.pallas_call(
        body, name=name, grid=(t // tr,), in_specs=[row, pl.BlockSpec((1, LANES), lambda i: (0, 0))],
        out_specs=row, out_shape=jax.ShapeDtypeStruct((t, w), bf16),
        compiler_params=_params("parallel"))(qp, g2)


def q_norm_bwd(dq, qp, g2, *, name, scale, tr=512):
    t, w = qp.shape
    tr = min(tr, t)

    def body(dq_ref, x_ref, g_ref, o_ref, dg_ref):
        i = pl.program_id(0)
        bd = _head_mean_matrix()
        acc = jnp.zeros((1, LANES), f32)
        for cb in range(w // LANES):
            cols = slice(cb * LANES, (cb + 1) * LANES)
            xc = x_ref[:, cols]
            rh = lax.rsqrt(_dot_split(xc * xc, bd) + EPS)
            xh = xc * rh
            dy = dq_ref[:, cols] * scale
            acc = acc + jnp.sum(dy * xh, axis=0, keepdims=True)
            tg = dy * g_ref[...]
            o_ref[:, cols] = (rh * (tg - xh * _dot_split(tg * xh, bd))).astype(bf16)

        @pl.when(i == 0)
        def _():
            dg_ref[...] = acc

        @pl.when(i > 0)
        def _():
            dg_ref[...] += acc

    row = pl.BlockSpec((tr, w), lambda i: (i, 0))
    vec = pl.BlockSpec((1, LANES), lambda i: (0, 0))
    return pl.pallas_call(
        body, name=name, grid=(t // tr,), in_specs=[row, row, vec], out_specs=[row, vec],
        out_shape=[jax.ShapeDtypeStruct((t, w), bf16), jax.ShapeDtypeStruct((1, LANES), f32)],
        compiler_params=_params("arbitrary"))(dq, qp, g2)


def kv_post_fwd(kv, g2, *, name, tr=512):
    t, w = kv.shape
    tr = min(tr, t)
    kw = w // 2

    def body(x_ref, g_ref, k_ref, v_ref):
        bd = _head_mean_matrix()
        half = _lane_half((tr, LANES))
        for cb in range(kw // LANES):
            xc = x_ref[:, cb * LANES:(cb + 1) * LANES]
            rh = lax.rsqrt(_dot_split(xc * xc, bd) + EPS)
            kn = xc * rh * g_ref[...]
            vc = x_ref[:, kw + cb * LANES:kw + (cb + 1) * LANES]
            for src, dst in ((kn, k_ref), (vc, v_ref)):
                sw = pltpu.roll(src, HEAD_DIM, 1)
                for hf in range(2):
                    blk = 2 * cb + hf
                    dst[:, blk * LANES:(blk + 1) * LANES] = jnp.where(half == hf, src, sw).astype(bf16)

    return pl.pallas_call(
        body, name=name, grid=(t // tr,),
        in_specs=[pl.BlockSpec((tr, w), lambda i: (i, 0)), pl.BlockSpec((1, LANES), lambda i: (0, 0))],
        out_specs=[pl.BlockSpec((tr, 2 * kw), lambda i: (i, 0))] * 2,
        out_shape=[jax.ShapeDtypeStruct((t, 2 * kw), bf16)] * 2,
        compiler_params=_params("parallel"))(kv, g2)


def kv_post_bwd(dk2, dv2, kv, g2, *, name, tr=512):
    t, w = kv.shape
    tr = min(tr, t)
    kw = w // 2

    def body(dk_ref, dv_ref, x_ref, g_ref, o_ref, dg_ref):
        i = pl.program_id(0)
        bd = _head_mean_matrix()
        half = _lane_half((tr, LANES))
        acc = jnp.zeros((1, LANES), f32)

        def fold(ref, cb):
            a = ref[:, (2 * cb) * LANES:(2 * cb + 1) * LANES]
            b = ref[:, (2 * cb + 1) * LANES:(2 * cb + 2) * LANES]
            return jnp.where(half == 0, a + pltpu.roll(a, HEAD_DIM, 1), b + pltpu.roll(b, HEAD_DIM, 1))

        for cb in range(kw // LANES):
            cols = slice(cb * LANES, (cb + 1) * LANES)
            xc = x_ref[:, cols]
            rh = lax.rsqrt(_dot_split(xc * xc, bd) + EPS)
            xh = xc * rh
            dy = fold(dk_ref, cb)
            acc = acc + jnp.sum(dy * xh, axis=0, keepdims=True)
            tg = dy * g_ref[...]
            o_ref[:, cols] = (rh * (tg - xh * _dot_split(tg * xh, bd))).astype(bf16)
            o_ref[:, kw + cb * LANES:kw + (cb + 1) * LANES] = fold(dv_ref, cb).astype(bf16)

        @pl.when(i == 0)
        def _():
            dg_ref[...] = acc

        @pl.when(i > 0)
        def _():
            dg_ref[...] += acc

    dup = pl.BlockSpec((tr, 2 * kw), lambda i: (i, 0))
    row = pl.BlockSpec((tr, w), lambda i: (i, 0))
    vec = pl.BlockSpec((1, LANES), lambda i: (0, 0))
    return pl.pallas_call(
        body, name=name, grid=(t // tr,), in_specs=[dup, dup, row, vec], out_specs=[row, vec],
        out_shape=[jax.ShapeDtypeStruct((t, w), bf16), jax.ShapeDtypeStruct((1, LANES), f32)],
        compiler_params=_params("arbitrary"))(dk2, dv2, kv, g2)


def _slope(h):
    return 2.0 ** (-8.0 * (h + 1) / N_Q_HEADS)


GROUP_ROWS = Q_PER_KV * CHUNK


def _band_mask(n):
    tq = lax.broadcasted_iota(jnp.int32, (GROUP_ROWS, 2 * CHUNK), 0) % CHUNK
    jk = lax.broadcasted_iota(jnp.int32, (GROUP_ROWS, 2 * CHUNK), 1)
    dist = tq + CHUNK - jk
    ok = (dist >= 0) & (dist < CHUNK) & jnp.logical_not((n == 0) & (jk < CHUNK))
    return dist.astype(f32), ok


def _band(ref, n, kh):
    p0 = pl.multiple_of(jnp.maximum(n - 1, 0) * CHUNK, CHUNK)
    c0 = pl.multiple_of(n * CHUNK, CHUNK)
    cols = slice(kh * LANES, (kh + 1) * LANES)
    return jnp.concatenate([ref[pl.ds(p0, CHUNK), cols], ref[pl.ds(c0, CHUNK), cols]], axis=0)


def _stack_heads(ref, kh, half):
    parts = []
    for cb in (2 * kh, 2 * kh + 1):
        xc = ref[:, cb * LANES:(cb + 1) * LANES].astype(f32)
        parts += [jnp.where(half == hf, xc, 0.0).astype(bf16) for hf in range(2)]
    return jnp.concatenate(parts, axis=0)


def _unstack_heads(x4, half):
    return (jnp.where(half == 0, x4[0:CHUNK], x4[CHUNK:2 * CHUNK]),
            jnp.where(half == 0, x4[2 * CHUNK:3 * CHUNK], x4[3 * CHUNK:]))


def _per_head_column(kh, values):
    grp = lax.broadcasted_iota(jnp.int32, (GROUP_ROWS, 1), 0) // CHUNK
    col = jnp.full((GROUP_ROWS, 1), values[0], f32)
    for g in range(1, Q_PER_KV):
        col = jnp.where(grp == g, values[g], col)
    return col


def _softmax_band(q4, kband, dist, ok, slope, sink):
    s = _dot_nt(q4, kband)
    s = jnp.where(ok, s - slope * dist, -jnp.inf)
    m = jnp.maximum(jnp.max(s, axis=1, keepdims=True), sink)
    e = jnp.exp(s - m)
    es = jnp.exp(sink - m)
    den = jnp.sum(e, axis=1, keepdims=True) + es
    return e / den, es / den


def attn_fwd(q, k2, v2, sinks, *, name):
    t, w = q.shape
    nb = t // CHUNK

    def body(sink_ref, q_ref, k_ref, v_ref, o_ref):
        n = pl.program_id(0)
        dist, ok = _band_mask(n)
        half = _lane_half((CHUNK, LANES))
        for kh in range(N_KV_HEADS):
            heads = [Q_PER_KV * kh + g for g in range(Q_PER_KV)]
            slope = _per_head_column(kh, [_slope(h) for h in heads])
            sink = _per_head_column(kh, [sink_ref[h] for h in heads])
            q4 = _stack_heads(q_ref, kh, half)
            p, _ = _softmax_band(q4, _band(k_ref, n, kh), dist, ok, slope, sink)
            o4 = _dot(p.astype(bf16), _band(v_ref, n, kh))
            lo, hi = _unstack_heads(o4, half)
            o_ref[:, (2 * kh) * LANES:(2 * kh + 1) * LANES] = lo.astype(bf16)
            o_ref[:, (2 * kh + 1) * LANES:(2 * kh + 2) * LANES] = hi.astype(bf16)

    full = pl.BlockSpec((t, k2.shape[1]), lambda n: (0, 0))
    return pl.pallas_call(
        body, name=name, grid=(nb,),
        in_specs=[pl.BlockSpec(memory_space=pltpu.SMEM), pl.BlockSpec((CHUNK, w), lambda n: (n, 0)), full, full],
        out_specs=pl.BlockSpec((CHUNK, w), lambda n: (n, 0)),
        out_shape=jax.ShapeDtypeStruct((t, w), bf16),
        compiler_params=_params("parallel"))(sinks, q, k2, v2)


def attn_bwd(q, k2, v2, do, sinks, *, name):
    t, w = q.shape
    nb = t // CHUNK
    kw = k2.shape[1]

    def body(sink_ref, q_ref, k_ref, v_ref, do_ref, dq_ref, dk_ref, dv_ref, ds_ref, kc_ref, vc_ref):
        n = pl.program_id(0)

        @pl.when(n == 0)
        def _():
            ds_ref[...] = jnp.zeros_like(ds_ref)
            kc_ref[...] = jnp.zeros_like(kc_ref)
            vc_ref[...] = jnp.zeros_like(vc_ref)
            dk_ref[...] = jnp.zeros_like(dk_ref)
            dv_ref[...] = jnp.zeros_like(dv_ref)

        @pl.when(n == nb)
        def _():
            dk_ref[...] = kc_ref[...]
            dv_ref[...] = vc_ref[...]

        @pl.when(n < nb)
        def _():
            dist, ok = _band_mask(n)
            half = _lane_half((CHUNK, LANES))
            lane = lax.broadcasted_iota(jnp.int32, (1, LANES), 1)
            sink_acc = jnp.zeros((1, LANES), f32)
            for kh in range(N_KV_HEADS):
                heads = [Q_PER_KV * kh + g for g in range(Q_PER_KV)]
                slope = _per_head_column(kh, [_slope(h) for h in heads])
                sink = _per_head_column(kh, [sink_ref[h] for h in heads])
                q4 = _stack_heads(q_ref, kh, half)
                do4 = _stack_heads(do_ref, kh, half)
                kband = _band(k_ref, n, kh)
                vband = _band(v_ref, n, kh)
                p, ps = _softmax_band(q4, kband, dist, ok, slope, sink)
                dp = _dot_nt(do4, vband)
                delta = jnp.sum(p * dp, axis=1, keepdims=True)
                dsb = (p * (dp - delta)).astype(bf16)
                sd = ps * delta
                for g, h in enumerate(heads):
                    part = jnp.sum(sd[g * CHUNK:(g + 1) * CHUNK], axis=0, keepdims=True)
                    sink_acc = sink_acc + jnp.where(lane == h, -part, 0.0)
                lo, hi = _unstack_heads(_dot(dsb, kband), half)
                dq_ref[:, (2 * kh) * LANES:(2 * kh + 1) * LANES] = lo
                dq_ref[:, (2 * kh + 1) * LANES:(2 * kh + 2) * LANES] = hi
                dkb = _dot_tn(dsb, q4)
                dvb = _dot_tn(p.astype(bf16), do4)
                cols = slice(kh * LANES, (kh + 1) * LANES)
                dk_ref[:, cols] = kc_ref[:, cols] + dkb[0:CHUNK]
                dv_ref[:, cols] = vc_ref[:, cols] + dvb[0:CHUNK]
                kc_ref[:, cols] = dkb[CHUNK:]
                vc_ref[:, cols] = dvb[CHUNK:]
            ds_ref[...] += sink_acc

    full = pl.BlockSpec((t, kw), lambda n: (0, 0))
    qblk = pl.BlockSpec((CHUNK, w), lambda n: (jnp.minimum(n, nb - 1), 0))
    kblk = pl.BlockSpec((CHUNK, kw), lambda n: (jnp.maximum(n - 1, 0), 0))
    return pl.pallas_call(
        body, name=name, grid=(nb + 1,),
        in_specs=[pl.BlockSpec(memory_space=pltpu.SMEM), qblk, full, full, qblk],
        out_specs=[qblk, kblk, kblk, pl.BlockSpec((1, LANES), lambda n: (0, 0))],
        out_shape=[jax.ShapeDtypeStruct((t, w), f32), jax.ShapeDtypeStruct((t, kw), f32),
                   jax.ShapeDtypeStruct((t, kw), f32), jax.ShapeDtypeStruct((1, LANES), f32)],
        scratch_shapes=[pltpu.VMEM((CHUNK, kw), f32), pltpu.VMEM((CHUNK, kw), f32)],
        compiler_params=_params("arbitrary"))(sinks, q, k2, v2, do)


def loss_head(y, target, *, name, tr=512):
    t, d = y.shape
    tr = min(tr, t)

    def body(y_ref, t_ref, dy_ref, s_ref):
        i = pl.program_id(0)
        e = y_ref[...] - t_ref[...]
        dy_ref[...] = e * (1.0 / d)
        part = jnp.sum(e * e, axis=0, keepdims=True)

        @pl.when(i == 0)
        def _():
            s_ref[...] = part

        @pl.when(i > 0)
        def _():
            s_ref[...] += part

    row = pl.BlockSpec((tr, d), lambda i: (i, 0))
    vec = pl.BlockSpec((1, d), lambda i: (0, 0))
    return pl.pallas_call(
        body, name=name, grid=(t // tr,), in_specs=[row, row], out_specs=[row, vec],
        out_shape=[jax.ShapeDtypeStruct((t, d), f32), jax.ShapeDtypeStruct((1, d), f32)],
        compiler_params=_params("arbitrary"))(y, target)


N_STEPS = 8


def _row_blocks(shape):
    if len(shape) == 2:
        r, c = shape
        return (r // N_STEPS, c), (lambda s: (s, 0))
    l, r, c = shape
    per = N_STEPS // l
    return (1, r // per, c), (lambda s: (s // per, s % per, 0))


CAST_STEPS = 4


def cast_into_slot(arrays, k_arr, *, name):
    in_specs, out_specs, out_shape, layers = [], [], [], []
    for a in arrays:
        r, c = a.shape[-2:]
        rb = r // CAST_STEPS
        if a.ndim == 2:
            in_specs.append(pl.BlockSpec((rb, c), lambda s, k: (s, 0)))
            layers.append(None)
        else:
            for l in range(a.shape[0]):
                in_specs.append(pl.BlockSpec((1, rb, c), lambda s, k, l=l: (l, s, 0)))
                layers.append(l)
        for _ in range(1 if a.ndim == 2 else a.shape[0]):
            out_specs.append(pl.BlockSpec((1, rb, c), lambda s, k: (k[0], s, 0)))
            out_shape.append(jax.ShapeDtypeStruct((N_SHARDS, r, c), bf16))
    n = len(in_specs)

    def body(k_ref, *refs):
        for i_ref, o_ref, l in zip(refs[:n], refs[n:], layers):
            o_ref[0] = (i_ref[...] if l is None else i_ref[0]).astype(bf16)

    args = []
    for a in arrays:
        args += [a] * (1 if a.ndim == 2 else a.shape[0])
    return pl.pallas_call(
        body, name=name,
        grid_spec=pltpu.PrefetchScalarGridSpec(num_scalar_prefetch=1, grid=(CAST_STEPS,),
                                               in_specs=in_specs, out_specs=out_specs),
        out_shape=out_shape, compiler_params=_params("parallel"))(k_arr, *args)


def adamw(ws, gs, ms, vs, *, name):
    n = len(ws)
    specs, g_specs, g_count = [], [], []
    for w, g_list in zip(ws, gs):
        blk, index = _row_blocks(w.shape)
        specs.append(pl.BlockSpec(blk, index))
        layers = len(g_list)
        per = N_STEPS // layers
        g_count.append(layers)
        for l in range(layers):
            g_specs.append(pl.BlockSpec(blk[-2:], lambda s, l=l, per=per: (jnp.where(s // per == l, s % per, 0), 0)))
    ng = len(g_specs)

    def body(*refs):
        s = pl.program_id(0)
        g_refs = refs[3 * n:3 * n + ng]
        outs = refs[3 * n + ng:]
        off = 0
        for i in range(n):
            w_ref, m_ref, v_ref = refs[i], refs[n + i], refs[2 * n + i]
            go_ref, d_ref, nm_ref, nv_ref = (outs[k * n + i] for k in range(4))
            layers = g_count[i]
            g = g_refs[off][...]
            for l in range(1, layers):
                g = jnp.where(s // (N_STEPS // layers) == l, g_refs[off + l][...], g)
            off += layers
            g = g.reshape(w_ref.shape)
            m = ADAM_B1 * m_ref[...] + (1.0 - ADAM_B1) * g
            v = ADAM_B2 * v_ref[...] + (1.0 - ADAM_B2) * (g * g)
            m_hat = m / ADAM_C1
            v_hat = v / ADAM_C2
            go_ref[...] = g
            d_ref[...] = -ADAM_LR * (m_hat / (jnp.sqrt(v_hat) + ADAM_EPS) + ADAM_WD * w_ref[...])
            nm_ref[...] = m
            nv_ref[...] = v

    outs = pl.pallas_call(
        body, name=name, grid=(N_STEPS,), in_specs=specs * 3 + g_specs, out_specs=specs * 4,
        out_shape=[jax.ShapeDtypeStruct(a.shape, f32) for a in ws] * 4,
        compiler_params=_params("parallel"))(*ws, *ms, *vs, *[g for g_list in gs for g in g_list])
    return [outs[k * n:(k + 1) * n] for k in range(4)]


def _adamw_update(w, g, m, v):
    m = ADAM_B1 * m + (1.0 - ADAM_B1) * g
    v = ADAM_B2 * v + (1.0 - ADAM_B2) * (g * g)
    m_hat = m / ADAM_C1
    v_hat = v / ADAM_C2
    return -ADAM_LR * (m_hat / (jnp.sqrt(v_hat) + ADAM_EPS) + ADAM_WD * w), m, v


def adamw_small(ws, gs, ms, vs, *, name):
    n = len(ws)

    def body(*refs):
        for i in range(n):
            w_ref, g_ref, m_ref, v_ref = (refs[k * n + i] for k in range(4))
            d_ref, nm_ref, nv_ref = (refs[(4 + k) * n + i] for k in range(3))
            d_ref[...], nm_ref[...], nv_ref[...] = _adamw_update(w_ref[...], g_ref[...], m_ref[...], v_ref[...])

    outs = pl.pallas_call(
        body, name=name, out_shape=[jax.ShapeDtypeStruct(a.shape, f32) for a in ws] * 3)(*ws, *gs, *ms, *vs)
    return outs[:n], outs[n:2 * n], outs[2 * n:]


def _place():
    return lax.axis_index("x"), lax.axis_index("y"), lax.axis_index("c")


def gather_shards(bufs, *, name, split):
    n = len(bufs)

    def body(*refs):
        bufs_ = refs[:n]
        isend, irecv, dsend, drecv = refs[2 * n:]
        x, y, c = _place()
        k = 2 * x + y
        peers = [(1 - x, y, c), (x, 1 - y, c), (1 - x, 1 - y, c)]
        peer_k = [2 * (1 - x) + y, 2 * x + (1 - y), 2 * (1 - x) + (1 - y)]

        def slab(a, q, h):
            if not split[a]:
                return bufs_[a].at[q]
            half = bufs_[a].shape[1] // 2
            return bufs_[a].at[q, pl.ds(pl.multiple_of(h * half, 16), half)]

        def ici(a, j, q):
            return pltpu.make_async_remote_copy(
                src_ref=slab(a, q, c), dst_ref=slab(a, q, c), send_sem=isend.at[3 * a + j], recv_sem=irecv.at[3 * a + j],
                device_id=peers[j], device_id_type=MESH)

        def d2d(a, j, h):
            return pltpu.make_async_remote_copy(
                src_ref=slab(a, peer_k[j], h), dst_ref=slab(a, peer_k[j], h), send_sem=dsend.at[3 * a + j],
                recv_sem=drecv.at[3 * a + j], device_id=(x, y, 1 - c), device_id_type=MESH)

        for a in range(n):
            for j in range(3):
                ici(a, j, k).start()
        for a in range(n):
            for j in range(3):
                ici(a, j, peer_k[j]).wait_recv()
                if split[a]:
                    d2d(a, j, c).start()
        for a in range(n):
            for j in range(3):
                if split[a]:
                    d2d(a, j, 1 - c).wait_recv()
        for a in range(n):
            for j in range(3):
                ici(a, j, k).wait_send()
                if split[a]:
                    d2d(a, j, c).wait_send()

    return pl.pallas_call(
        body, name=name, in_specs=[ANY] * n, out_specs=[ANY] * n,
        out_shape=[jax.ShapeDtypeStruct(b.shape, b.dtype) for b in bufs],
        input_output_aliases={i: i for i in range(n)},
        scratch_shapes=[pltpu.SemaphoreType.DMA((3 * n,))] * 4)(*bufs)


HBM = pl.BlockSpec(memory_space=pltpu.HBM)
SEM = pl.BlockSpec(memory_space=pltpu.SEMAPHORE)
DATAFLOW = pltpu.SideEffectType.DATAFLOW_SIDE_EFFECTING


def _chip_peers():
    x, y, c = _place()
    return 2 * x + y, [(1 - x, y, c), (x, 1 - y, c), (1 - x, 1 - y, c)], [2 * (1 - x) + y, 2 * x + (1 - y), 2 * (1 - x) + (1 - y)]


def gather_start(bufs, groups, after, *, name):
    n = len(bufs)
    ng = len(groups)

    def body(*refs):
        ins = refs[:n]
        sends, recvs = refs[2 * n + 1:2 * n + 1 + ng], refs[2 * n + 1 + ng:2 * n + 1 + 2 * ng]
        token = refs[-1]
        k, peers, _ = _chip_peers()
        for gi, grp in enumerate(groups):
            for pos, a in enumerate(grp):
                for j in range(3):
                    pltpu.make_async_remote_copy(
                        src_ref=ins[a].at[k], dst_ref=ins[a].at[k], send_sem=sends[gi].at[3 * pos + j],
                        recv_sem=recvs[gi].at[3 * pos + j], device_id=peers[j], device_id_type=MESH).start()
        token[...] = jnp.zeros_like(token)

    sems = [pltpu.SemaphoreType.DMA((3 * len(grp),)) for grp in groups]
    outs = pl.pallas_call(
        body, name=name, in_specs=[HBM] * n + [ANY],
        out_specs=[HBM] * n + [SEM] * (2 * ng) + [pl.BlockSpec(memory_space=pltpu.VMEM)],
        out_shape=[pltpu.HBM(b.shape, b.dtype) for b in bufs] + sems + sems + [jax.ShapeDtypeStruct((8, LANES), f32)],
        input_output_aliases={i: i for i in range(n)},
        compiler_params=pltpu.CompilerParams(has_side_effects=DATAFLOW))(
            *[pltpu.with_memory_space_constraint(b, pltpu.HBM) for b in bufs], after)
    return outs[:n], outs[n:n + ng], outs[n + ng:n + 2 * ng], outs[-1]


def gather_wait(bufs, send_sems, recv_sems, after, *, name):
    n = len(bufs)

    def body(*refs):
        ins = refs[:n]
        send, recv = refs[n], refs[n + 1]
        k, peers, peer_k = _chip_peers()
        for a in range(n):
            for j in range(3):
                copy = pltpu.make_async_remote_copy(
                    src_ref=ins[a].at[k], dst_ref=ins[a].at[peer_k[j]], send_sem=send.at[3 * a + j],
                    recv_sem=recv.at[3 * a + j], device_id=peers[j], device_id_type=MESH)
                copy.wait_send()
                copy.wait_recv()

    return pl.pallas_call(
        body, name=name, in_specs=[HBM] * n + [SEM, SEM, ANY], out_specs=[HBM] * n,
        out_shape=[pltpu.HBM(b.shape, b.dtype) for b in bufs],
        input_output_aliases={i: i for i in range(n)},
        compiler_params=pltpu.CompilerParams(has_side_effects=DATAFLOW))(*bufs, send_sems, recv_sems, after)


def _sibling_copies(srcs, lands, send, recv):
    x, y, c = _place()
    return [pltpu.make_async_remote_copy(src_ref=srcs[a], dst_ref=lands[a], send_sem=send.at[a], recv_sem=recv.at[a],
                                         device_id=(x, y, 1 - c), device_id_type=MESH) for a in range(len(srcs))]


def sibling_start(arrays, after, *, name):
    n = len(arrays)
    lands = [pltpu.with_memory_space_constraint(lax.empty(a.shape, a.dtype), pltpu.HBM) for a in arrays]

    def body(*refs):
        for cp in _sibling_copies(refs[:n], refs[n:2 * n], refs[4 * n + 1], refs[4 * n + 2]):
            cp.start()
        refs[-1][...] = jnp.zeros_like(refs[-1])

    bufs = list(arrays) + lands
    sems = [pltpu.SemaphoreType.DMA((n,))] * 2
    outs = pl.pallas_call(
        body, name=name, in_specs=[HBM] * (2 * n) + [ANY],
        out_specs=[HBM] * (2 * n) + [SEM] * 2 + [pl.BlockSpec(memory_space=pltpu.VMEM)],
        out_shape=[pltpu.HBM(b.shape, b.dtype) for b in bufs] + sems + [jax.ShapeDtypeStruct((8, LANES), f32)],
        input_output_aliases={i: i for i in range(2 * n)},
        compiler_params=pltpu.CompilerParams(has_side_effects=DATAFLOW))(
            *[pltpu.with_memory_space_constraint(b, pltpu.HBM) for b in bufs], after)
    return (n, outs[:-1]), outs[-1]


def sibling_wait(state, after, *, name):
    n, held = state

    def body(*refs):
        for cp in _sibling_copies(refs[:n], refs[n:2 * n], refs[2 * n], refs[2 * n + 1]):
            cp.wait_send()
            cp.wait_recv()

    outs = pl.pallas_call(
        body, name=name, in_specs=[HBM] * (2 * n) + [SEM] * 2 + [ANY], out_specs=[HBM] * (2 * n),
        out_shape=[pltpu.HBM(b.shape, b.dtype) for b in held[:2 * n]],
        input_output_aliases={i: i for i in range(2 * n)},
        compiler_params=pltpu.CompilerParams(has_side_effects=DATAFLOW))(*held, after)
    return outs[n:]


def sibling_exchange(arrays, *, name):
    n = len(arrays)

    def body(*refs):
        ins, outs = refs[:n], refs[n:2 * n]
        send, recv = refs[2 * n:]
        x, y, c = _place()

        def copy(a):
            return pltpu.make_async_remote_copy(
                src_ref=ins[a], dst_ref=outs[a], send_sem=send.at[a], recv_sem=recv.at[a],
                device_id=(x, y, 1 - c), device_id_type=MESH)

        for a in range(n):
            copy(a).start()
        for a in range(n):
            copy(a).wait_recv()
        for a in range(n):
            copy(a).wait_send()

    return pl.pallas_call(
        body, name=name, in_specs=[ANY] * n, out_specs=[ANY] * n,
        out_shape=[jax.ShapeDtypeStruct(a.shape, a.dtype) for a in arrays],
        scratch_shapes=[pltpu.SemaphoreType.DMA((n,)), pltpu.SemaphoreType.DMA((n,))])(*arrays)


ALL_MASKS = [(mx, my, mc) for mx in (0, 1) for my in (0, 1) for mc in (0, 1)][1:]


def _scatter_copies(srcs, lands, ev, send, recv, esend, erecv):
    x, y, c = _place()
    me = 4 * x + 2 * y + c
    k, peers, peer_k = _chip_peers()
    out = []
    for a in range(len(srcs)):
        for j in range(3):
            out.append(pltpu.make_async_remote_copy(
                src_ref=srcs[a].at[peer_k[j]], dst_ref=lands[a].at[j], send_sem=send.at[3 * a + j],
                recv_sem=recv.at[3 * a + j], device_id=peers[j], device_id_type=MESH))
    start_ev, wait_ev = [], []
    if ev is not None:
        for j, (mx, my, mc) in enumerate(ALL_MASKS):
            peer = (x ^ mx, y ^ my, c ^ mc)
            start_ev.append(pltpu.make_async_remote_copy(
                src_ref=ev.at[me], dst_ref=ev.at[me], send_sem=esend.at[j], recv_sem=erecv.at[j],
                device_id=peer, device_id_type=MESH))
            wait_ev.append(pltpu.make_async_remote_copy(
                src_ref=ev.at[me], dst_ref=ev.at[me ^ (4 * mx + 2 * my + mc)], send_sem=esend.at[j],
                recv_sem=erecv.at[j], device_id=peer, device_id_type=MESH))
    return out, start_ev, wait_ev


def chip_scatter_start(arrays, everyone, after, *, name):
    n = len(arrays)
    ne = 0 if everyone is None else 1
    lands = [pltpu.with_memory_space_constraint(lax.empty((3,) + a.shape[1:], a.dtype), pltpu.HBM) for a in arrays]

    def body(*refs):
        srcs, lands_ = refs[:n], refs[n:2 * n]
        ev = refs[2 * n] if ne else None
        sems = refs[2 * n + ne + 1 + 2 * n + ne:-1]
        send, recv = sems[0], sems[1]
        esend, erecv = (sems[2], sems[3]) if ne else (None, None)
        copies, start_ev, _ = _scatter_copies(srcs, lands_, ev, send, recv, esend, erecv)
        for cp in start_ev + copies:
            cp.start()
        refs[-1][...] = jnp.zeros_like(refs[-1])

    sem_shapes = [pltpu.SemaphoreType.DMA((3 * n,))] * 2 + [pltpu.SemaphoreType.DMA((7,))] * (2 * ne)
    bufs = list(arrays) + lands + ([everyone] if ne else [])
    outs = pl.pallas_call(
        body, name=name, in_specs=[HBM] * len(bufs) + [ANY],
        out_specs=[HBM] * len(bufs) + [SEM] * len(sem_shapes) + [pl.BlockSpec(memory_space=pltpu.VMEM)],
        out_shape=[pltpu.HBM(b.shape, b.dtype) for b in bufs] + sem_shapes + [jax.ShapeDtypeStruct((8, LANES), f32)],
        input_output_aliases={i: i for i in range(len(bufs))},
        compiler_params=pltpu.CompilerParams(has_side_effects=DATAFLOW))(
            *[pltpu.with_memory_space_constraint(b, pltpu.HBM) for b in bufs], after)
    return (n, ne, outs[:-1]), outs[-1]


def chip_scatter_wait(state, after, *, name):
    n, ne, held = state
    nb = 2 * n + ne
    bufs, sems = held[:nb], held[nb:]

    def body(*refs):
        srcs, lands_ = refs[:n], refs[n:2 * n]
        ev = refs[2 * n] if ne else None
        sems_ = refs[nb:nb + len(sems)]
        esend, erecv = (sems_[2], sems_[3]) if ne else (None, None)
        copies, _, wait_ev = _scatter_copies(srcs, lands_, ev, sems_[0], sems_[1], esend, erecv)
        for cp in wait_ev + copies:
            cp.wait_send()
            cp.wait_recv()

    outs = pl.pallas_call(
        body, name=name, in_specs=[HBM] * nb + [SEM] * len(sems) + [ANY], out_specs=[HBM] * nb,
        out_shape=[pltpu.HBM(b.shape, b.dtype) for b in bufs],
        input_output_aliases={i: i for i in range(nb)},
        compiler_params=pltpu.CompilerParams(has_side_effects=DATAFLOW))(*bufs, *sems, after)
    return outs[n:2 * n], (outs[2 * n] if ne else None)


def sibling_merge(bufs, *, name):
    n = len(bufs)

    def body(*refs):
        bufs_ = refs[:n]
        send, recv = refs[2 * n:]
        x, y, c = _place()

        def copy(u, h):
            return pltpu.make_async_remote_copy(
                src_ref=bufs_[u].at[h], dst_ref=bufs_[u].at[h], send_sem=send.at[u], recv_sem=recv.at[u],
                device_id=(x, y, 1 - c), device_id_type=MESH)

        for u in range(n):
            copy(u, c).start()
        for u in range(n):
            copy(u, 1 - c).wait_recv()
        for u in range(n):
            copy(u, c).wait_send()

    return pl.pallas_call(
        body, name=name, in_specs=[ANY] * n, out_specs=[ANY] * n,
        out_shape=[jax.ShapeDtypeStruct(b.shape, b.dtype) for b in bufs],
        input_output_aliases={i: i for i in range(n)},
        scratch_shapes=[pltpu.SemaphoreType.DMA((n,)), pltpu.SemaphoreType.DMA((n,))])(*bufs)


def sum_leading(a, *, name):
    n, r, c = a.shape

    def body(a_ref, o_ref):
        acc = a_ref[0]
        for i in range(1, n):
            acc = acc + a_ref[i]
        o_ref[...] = acc

    rb = r // 2 if r % 16 == 0 else r
    return pl.pallas_call(
        body, name=name, grid=(r // rb,), in_specs=[pl.BlockSpec((n, rb, c), lambda i: (0, i, 0))],
        out_specs=pl.BlockSpec((rb, c), lambda i: (i, 0)), out_shape=jax.ShapeDtypeStruct((r, c), f32),
        compiler_params=_params("parallel"))(a)


def _half_rows(shape):
    return shape[1] // 2 // 2


def rs_cast_other_half(grads, c_arr, *, name):
    n = len(grads)

    def body(c_ref, *refs):
        for i_ref, o_ref in zip(refs[:n], refs[n:]):
            o_ref[...] = i_ref[...].astype(bf16)

    in_specs = [pl.BlockSpec((1, _half_rows(g.shape), g.shape[2]), lambda s, r, c_ref: (s, (1 - c_ref[0]) * 2 + r, 0))
                for g in grads]
    out_specs = [pl.BlockSpec((1, _half_rows(g.shape), g.shape[2]), lambda s, r, c_ref: (s, r, 0)) for g in grads]
    return pl.pallas_call(
        body, name=name,
        grid_spec=pltpu.PrefetchScalarGridSpec(num_scalar_prefetch=1, grid=(N_SHARDS, 2),
                                               in_specs=in_specs, out_specs=out_specs),
        out_shape=[jax.ShapeDtypeStruct((N_SHARDS, g.shape[1] // 2, g.shape[2]), bf16) for g in grads],
        compiler_params=_params("parallel", "parallel"))(c_arr, *grads)


def rs_add_sibling(grads, recvd, ck_arr, *, name):
    n = len(grads)

    def body(ck_ref, *refs):
        s = pl.program_id(1)
        for u in range(n):
            g_ref, r_ref = refs[u], refs[n + u]
            qb_ref, own_ref = refs[2 * n + u], refs[3 * n + u]
            q = g_ref[0] + r_ref[0].astype(f32)
            qb_ref[0] = q.astype(bf16)

            @pl.when(s == ck_ref[1])
            def _(own_ref=own_ref, q=q):
                own_ref[...] = q

    in_specs = [pl.BlockSpec((1, _half_rows(g.shape), g.shape[2]), lambda r, s, ck: (s, ck[0] * 2 + r, 0)) for g in grads]
    in_specs += [pl.BlockSpec((1, _half_rows(g.shape), g.shape[2]), lambda r, s, ck: (s, r, 0)) for g in grads]
    out_specs = [pl.BlockSpec((1, _half_rows(g.shape), g.shape[2]), lambda r, s, ck: (s, r, 0)) for g in grads]
    out_specs += [pl.BlockSpec((_half_rows(g.shape), g.shape[2]), lambda r, s, ck: (r, 0)) for g in grads]
    outs = pl.pallas_call(
        body, name=name,
        grid_spec=pltpu.PrefetchScalarGridSpec(num_scalar_prefetch=1, grid=(2, N_SHARDS),
                                               in_specs=in_specs, out_specs=out_specs),
        out_shape=[jax.ShapeDtypeStruct((N_SHARDS, g.shape[1] // 2, g.shape[2]), bf16) for g in grads]
        + [jax.ShapeDtypeStruct((g.shape[1] // 2, g.shape[2]), f32) for g in grads],
        compiler_params=_params("parallel", "arbitrary"))(ck_arr, *grads, *recvd)
    return outs[:n], outs[n:]


def rs_sum_chips(owns, recvd, ck_arr, *, name):
    n = len(owns)

    def body(ck_ref, *refs):
        for u in range(n):
            own_ref, r_ref, o_ref = refs[u], refs[n + u], refs[2 * n + u]
            o_ref[0] = ((own_ref[...] + r_ref[0].astype(f32)) + r_ref[1].astype(f32)) + r_ref[2].astype(f32)

    in_specs = [pl.BlockSpec((o.shape[0] // 2, o.shape[1]), lambda r, ck: (r, 0)) for o in owns]
    in_specs += [pl.BlockSpec((3, o.shape[0] // 2, o.shape[1]), lambda r, ck: (0, r, 0)) for o in owns]
    out_specs = [pl.BlockSpec((1, o.shape[0] // 2, o.shape[1]), lambda r, ck: (ck[0], r, 0)) for o in owns]
    return pl.pallas_call(
        body, name=name,
        grid_spec=pltpu.PrefetchScalarGridSpec(num_scalar_prefetch=1, grid=(2,), in_specs=in_specs, out_specs=out_specs),
        out_shape=[jax.ShapeDtypeStruct((2,) + o.shape, f32) for o in owns],
        compiler_params=_params("parallel"))(ck_arr, *owns, *recvd)


SMALL = ("a_norm", "a_v_norm", "a_w_s", "a_b_s", "f_norm", "f_conv_w", "f_conv_b", "kv_norm", "k_norm",
         "b_norm", "b_q_norm", "b_sinks")
BIG = ("a_w_in", "a_w_out", "f_w_in", "f_w_out", "w_kv", "b_w_q", "b_w_o")
PACK_COLS = 1024
PACK_ROWS = 8 * N_STEPS


def _pack(parts, rows=PACK_ROWS):
    flat = jnp.concatenate([p.reshape(-1).astype(f32) for p in parts])
    pad = (-flat.shape[0]) % (rows * PACK_COLS)
    return jnp.pad(flat, (0, pad)).reshape(-1, PACK_COLS)


def _unpack(packed, shapes):
    flat = packed.reshape(-1)
    out, off = [], 0
    for s in shapes:
        size = math.prod(s)
        out.append(flat[off:off + size].reshape(s))
        off += size
    return out


def _ffn_fwd(x, g, h, r, w_in4, conv_w, conv_b, f, tag):
    wg, wu = conv_w[:, :f], conv_w[:, f:]
    bg, bu = conv_b[None, :f], conv_b[None, f:]
    pg, pu, gate, up, a = ffn_in_fused(h, w_in4, wg, wu, bg, bu, name=f"ffn{tag}_in")
    return a, (x, g, h, r, pg, pu, gate, up, a, wg, wu)


def _ffn_bwd(dy, saved, w_in4, w_out, c_arr, tag, exchange=False):
    x, g, h, r, pg, pu, gate, up, a, wg, wu = saved
    f = w_out.shape[0]
    d_w_out = mm_tn(a, [dy], c_arr, name=f"ffn{tag}_dwout", n_s=w_out.shape[1], shard_rows=f // N_SHARDS, tki=f // 2)
    dpg, dpu, sg, su = ffn_gate_bwd(dy, w_out, pg, pu, gate, up, wg, wu, name=f"ffn{tag}_dgate")
    d_w_in = mm_tn(h, [dpg, dpu], c_arr, name=f"ffn{tag}_dwin", n_s=w_in4.shape[2], shard_rows=h.shape[1])
    state = None
    if exchange:
        state, token = sibling_start([d_w_in[1], d_w_out[1]], d_w_in[0], name=f"rs_sibling_start_ffn{tag}")
        g = g + token[0, 0]
    dx, dg = mm_nt_rms_bwd([dpg, dpu], w_in4, x, r, g, dy, name=f"ffn{tag}_dh")
    d_conv_w = jnp.concatenate([sg[0:3], su[0:3]], axis=1)
    d_conv_b = jnp.concatenate([sg[3], su[3]], axis=0)
    return dx, dg, d_w_in, d_conv_w, d_conv_b, d_w_out, state


def _rs_front(pairs, sibling_state, after, c_arr, tag):
    units = [full.reshape(N_SHARDS, -1, full.shape[-1]) for full, _ in pairs]
    from_sib = sibling_wait(sibling_state, after, name=f"rs_sibling_wait{tag}")
    return rs_add_sibling(units, from_sib, c_arr, name=f"rs_add{tag}")


def _rs_back(own, from_chips, c_arr, tag):
    halves = rs_sum_chips(list(own), list(from_chips), c_arr, name=f"rs_sum{tag}")
    return [m.reshape(-1, m.shape[2]) for m in sibling_merge(list(halves), name=f"rs_merge{tag}")]


def kernel(x, a_norm, a_w_in, a_v_norm, a_w_s, a_b_s, a_w_out, f_norm, f_w_in, f_conv_w, f_conv_b, f_w_out, kv_norm, w_kv, k_norm, b_norm, b_w_q, b_q_norm, b_sinks, b_w_o, loss_target, m_a_norm, m_a_w_in, m_a_v_norm, m_a_w_s, m_a_b_s, m_a_w_out, m_f_norm, m_f_w_in, m_f_conv_w, m_f_conv_b, m_f_w_out, m_kv_norm, m_w_kv, m_k_norm, m_b_norm, m_b_w_q, m_b_q_norm, m_b_sinks, m_b_w_o, v_a_norm, v_a_w_in, v_a_v_norm, v_a_w_s, v_a_b_s, v_a_w_out, v_f_norm, v_f_w_in, v_f_conv_w, v_f_conv_b, v_f_w_out, v_kv_norm, v_w_kv, v_k_norm, v_b_norm, v_b_w_q, v_b_q_norm, v_b_sinks, v_b_w_o):
    args = dict(locals())
    weights = {n: args[n] for n in SMALL + BIG}
    moms = {n: args["m_" + n] for n in SMALL + BIG}
    vars_ = {n: args["v_" + n] for n in SMALL + BIG}
    t, d = x.shape[1], x.shape[2]
    xi, yi, ci = _place()
    chip = 2 * xi + yi

    big_local = [a_w_in[0], a_w_out[0], f_w_in, f_w_out, w_kv, b_w_q[0], b_w_o[0]]
    c_arr = jnp.stack([ci, chip]).astype(jnp.int32)
    k_arr = jnp.stack([chip]).astype(jnp.int32)
    b_ain, b_aout, b_fin0, b_fin1, b_fout0, b_fout1, b_kv, b_q, b_o = cast_into_slot(big_local, k_arr, name="cast_weights")
    small_cols = _pack([a_norm, a_v_norm, f_conv_w], rows=8)
    b_small = lax.dynamic_update_slice(jnp.zeros((N_SHARDS,) + small_cols.shape, f32), small_cols[None], (chip, 0, 0))
    g_small, w_a_in, g_a_w_out = gather_shards([b_small, b_ain, b_aout], name="gather_first", split=[False, True, True])
    later, send_sems, recv_sems, token = gather_start([b_fin0, b_fout0, b_kv, b_q, b_o, b_fin1, b_fout1],
                                                      [[0], [1, 2, 3, 4], [5, 6]], g_small, name="gather_start")
    ns_cols = a_norm.shape[1]
    nf_cols = f_conv_w.shape[2]
    parts = [_unpack(g_small[k], [a_norm.shape, a_v_norm.shape, f_conv_w.shape]) for k in range(N_SHARDS)]
    a_norm_f = jnp.concatenate([p[0] for p in parts], axis=1) + token[0, 0]
    a_v_norm_f = jnp.concatenate([p[1] for p in parts], axis=1)
    conv_w_f = jnp.concatenate([p[2] for p in parts], axis=2)

    x0 = x[0]
    tril = jnp.tril(jnp.ones((CHUNK, CHUNK), dtype=bool))
    wc = jnp.where(tril[None], a_w_s[0], 0.0).astype(bf16)
    bt = a_b_s[0].T
    kg2 = jnp.tile(k_norm, 2)[None]
    qg2 = jnp.tile(b_q_norm[0], 2)[None]

    (h_a,), r_a = rms_fwd(x0, [a_norm_f], name="a_norm")
    zu = mm_nn(h_a, w_a_in, name="a_in_u", s0=0, ns=2)
    zv = mm_nn(h_a, w_a_in, name="a_in_v", s0=2, ns=2)
    y_a = sgu_gate_fwd(zu, zv, a_v_norm_f, wc, bt, name="a_gate")
    w_a_out = g_a_w_out.reshape(1, -1, d)
    f = f_w_out.shape[1] * N_SHARDS
    x1, (h_f0,), r_f0 = mm_residual(y_a, w_a_out[0], x0, name="a_out", gains=[f_norm[0:1]])
    (g_fin0,) = gather_wait(later[0:1], send_sems[0], recv_sems[0], x1, name="gather_wait_0")
    w_f_in = [g_fin0, None]
    a0, ffn0 = _ffn_fwd(x1, f_norm[0:1], h_f0, r_f0, w_f_in[0], conv_w_f[0], f_conv_b[0], f, "0")
    g_fout0, g_w_kv, g_b_w_q, g_b_w_o = gather_wait(later[1:5], send_sems[1], recv_sems[1], a0, name="gather_wait_1")
    w_f_out = [g_fout0.reshape(-1, d), None]
    w_kv_f = g_w_kv.reshape(1, d, -1)
    w_q_f = g_b_w_q.reshape(1, d, -1)
    w_o_f = g_b_w_o.reshape(1, -1, d)
    x2, (h_k, h_q), r_b = mm_residual(a0, w_f_out[0], x1, name="ffn0_out", gains=[kv_norm[None], b_norm])
    kv = mm_nn(h_k, w_kv_f, name="kv_proj")
    k2, v2 = kv_post_fwd(kv, kg2, name="kv_post")
    qp = mm_nn(h_q, w_q_f, name="q_proj")
    qn = q_norm_fwd(qp, qg2, name="q_norm", scale=HEAD_DIM ** -0.5)
    o = attn_fwd(qn, k2, v2, b_sinks[0], name="attn")
    x3, (h_f1,), r_f1 = mm_residual(o, w_o_f[0], x2, name="o_proj", gains=[f_norm[1:2]])
    g_fin1, g_fout1 = gather_wait(later[5:7], send_sems[2], recv_sems[2], x3, name="gather_wait_2")
    w_f_in[1] = g_fin1
    w_f_out[1] = g_fout1.reshape(-1, d)
    a1, ffn1 = _ffn_fwd(x3, f_norm[1:2], h_f1, r_f1, w_f_in[1], conv_w_f[1], f_conv_b[1], f, "1")
    dx4, sq = mm_residual(a1, w_f_out[1], x3, name="ffn1_out", target=loss_target[0])
    loss_part = (0.5 * jnp.sum(sq) / d).reshape(1)

    proj_rows = d // N_SHARDS
    dx3, d_fn1, d_fwin1, d_cw1, d_cb1, d_fwout1, _ = _ffn_bwd(dx4, ffn1, w_f_in[1], w_f_out[1], c_arr, "1")
    do = mm_nt([dx3], w_o_f, name="o_proj_dx")
    d_w_o = mm_tn(o, [dx3], c_arr, name="o_proj_dw", n_s=d, shard_rows=o.shape[1] // N_SHARDS)
    dqn, dk2, dv2, dsink = attn_bwd(qn, k2, v2, do, b_sinks[0], name="attn_bwd")
    dqp, dqg = q_norm_bwd(dqn, qp, qg2, name="q_norm_bwd", scale=HEAD_DIM ** -0.5)
    dkv, dkg = kv_post_bwd(dk2, dv2, kv, kg2, name="kv_post_bwd")
    d_w_q = mm_tn(h_q, [dqp], c_arr, name="q_proj_dw", n_s=w_q_f.shape[2], shard_rows=proj_rows)
    d_w_kv = mm_tn(h_k, [dkv], c_arr, name="kv_proj_dw", n_s=w_kv_f.shape[2], shard_rows=proj_rows)
    group1 = [d_fwin1, d_fwout1, d_w_kv, d_w_q, d_w_o]
    sib1, token_s1 = sibling_start([half for _, half in group1], d_w_kv[0], name="rs_sibling_start1")
    dh_k = mm_nt([dkv], w_kv_f, name="kv_proj_dx")
    dx2, d_bn, d_kvn = mm_nt_rms_bwd([dqp], w_q_f, x2, r_b, b_norm + token_s1[0, 0], dx3, name="q_proj_dx",
                                     extra=(dh_k, kv_norm[None]))
    chip_bf1, own1 = _rs_front(group1, sib1, dx2, c_arr, "1")
    scatter1, token1 = chip_scatter_start(list(chip_bf1), None, dx2, name="rs_chips_start1")
    ffn0 = ffn0[:9] + (ffn0[9] + token1[0, 0],) + ffn0[10:]
    dx1, d_fn0, d_fwin0, d_cw0, d_cb0, d_fwout0, sib2 = _ffn_bwd(dx2, ffn0, w_f_in[0], w_f_out[0], c_arr, "0", exchange=True)
    chip_bf2, own2 = _rs_front([d_fwin0, d_fwout0], sib2, dx1, c_arr, "2")
    scatter2, token2 = chip_scatter_start(list(chip_bf2), None, dx1, name="rs_chips_start2")
    a_v_norm_f = a_v_norm_f + token2[0, 0]
    dy_a = mm_nt([dx1], w_a_out, name="a_out_dx")
    d_w_aout = mm_tn(y_a, [dx1], c_arr, name="a_out_dw", n_s=d, shard_rows=y_a.shape[1] // N_SHARDS)
    dzu, dzv, d_avn, d_ws, d_bt = sgu_gate_bwd(zu, zv, dy_a, a_v_norm_f, wc, bt, name="a_gate_bwd")
    d_w_ain = mm_tn(h_a, [dzu, dzv], c_arr, name="a_in_dw", n_s=w_a_in.shape[2], shard_rows=d)
    sib3, token_s3 = sibling_start([d_w_ain[1], d_w_aout[1]], d_w_ain[0], name="rs_sibling_start3")
    dx0, d_an = mm_nt_rms_bwd([dzu, dzv], w_a_in, x0, r_a, a_norm_f + token_s3[0, 0], dx1, name="a_in_dx")
    grad_x = dx0[None]

    chip_bf3, own3 = _rs_front([d_w_ain, d_w_aout], sib3, dx0, c_arr, "3")
    d_fn = jnp.concatenate([d_fn0, d_fn1], axis=0)
    d_cw = jnp.stack([d_cw0, d_cw1])
    d_cb = jnp.stack([d_cb0, d_cb1])
    d_kg = (dkg[0, :HEAD_DIM] + dkg[0, HEAD_DIM:])
    d_qg = (dqg[0, :HEAD_DIM] + dqg[0, HEAD_DIM:])[None]
    small_full = [d_an, d_avn, d_ws[None], d_bt.T[None], d_fn, d_cw, d_cb, d_kvn[0], d_kg, d_bn, d_qg,
                  dsink[:, :N_Q_HEADS], loss_part]
    packed = _pack(small_full)
    me = 4 * xi + 2 * yi + ci
    everyone = lax.dynamic_update_slice(lax.empty((N_DEV,) + packed.shape, f32), packed[None], (me, 0, 0))
    scatter3, token3 = chip_scatter_start(list(chip_bf3), everyone, own3[0], name="rs_chips_start3")
    from_chips1, _ = chip_scatter_wait(scatter1, token3, name="rs_chips_wait1")
    from_chips2, _ = chip_scatter_wait(scatter2, from_chips1[0], name="rs_chips_wait2")
    fin1, fout1, gkv, gq, go, fin0, fout0 = _rs_back(list(own1) + list(own2), list(from_chips1) + list(from_chips2),
                                                     c_arr, "12")
    late = ("f_w_in", "f_w_out", "w_kv", "b_w_q", "b_w_o")
    res_late = adamw([weights[n] for n in late], [[fin0, fin1], [fout0, fout1], [gkv], [gq], [go]],
                     [moms[n] for n in late], [vars_[n] for n in late], name="adamw_late")
    from_chips3, from_all = chip_scatter_wait(scatter3, res_late[1][2], name="rs_chips_wait3")
    ain, aout = _rs_back(own3, from_chips3, c_arr, "3")
    first = ("a_w_in", "a_w_out")
    res_first = adamw([weights[n] for n in first], [[ain], [aout]], [moms[n] for n in first],
                      [vars_[n] for n in first], name="adamw_first")
    big = {n: tuple(r[i] for r in res_late) for i, n in enumerate(late)}
    big.update({n: tuple(r[i] for r in res_first) for i, n in enumerate(first)})

    full_shapes = [g.shape for g in small_full]
    small_g = _unpack(sum_leading(from_all, name="small_sum"), full_shapes)
    loss = small_g.pop()[0]
    small_g[0] = lax.dynamic_slice_in_dim(small_g[0], chip * ns_cols, ns_cols, axis=1)
    small_g[1] = lax.dynamic_slice_in_dim(small_g[1], chip * ns_cols, ns_cols, axis=1)
    small_g[5] = lax.dynamic_slice_in_dim(small_g[5], chip * nf_cols, nf_cols, axis=2)
    small_shapes = [weights[n].shape for n in SMALL]
    small_g = [g.reshape(s) for g, s in zip(small_g, small_shapes)]
    flat2 = [(math.prod(s[:-1]), s[-1]) for s in small_shapes]
    small_d, small_m, small_v = adamw_small(
        *[[a.reshape(s2) for a, s2 in zip(group, flat2)]
          for group in ([weights[n] for n in SMALL], small_g, [moms[n] for n in SMALL], [vars_[n] for n in SMALL])],
        name="adamw_small")
    small_d, small_m, small_v = ([a.reshape(s) for a, s in zip(group, small_shapes)]
                                 for group in (small_d, small_m, small_v))

    out = {}
    for i, n in enumerate(SMALL):
        out[n] = (small_g[i], small_d[i], small_m[i], small_v[i])
    out.update(big)
    order = ["a_norm", "a_w_in", "a_v_norm", "a_w_s", "a_b_s", "a_w_out", "f_norm", "f_w_in", "f_conv_w", "f_conv_b",
             "f_w_out", "kv_norm", "w_kv", "k_norm", "b_norm", "b_w_q", "b_q_norm", "b_sinks", "b_w_o"]
    return (loss, grad_x, *[out[n][0] for n in order], *[out[n][1] for n in order],
            *[out[n][2] for n in order], *[out[n][3] for n in order])
```

```python
import functools
import math

import jax
import jax.numpy as jnp
from jax import lax
from jax.experimental import pallas as pl
from jax.experimental.pallas import tpu as pltpu

f32 = jnp.float32
bf16 = jnp.bfloat16
MESH = pl.DeviceIdType.MESH
ANY = pl.BlockSpec(memory_space=pl.ANY)

EPS = 1e-6
LANES = 128
CHUNK = 128
HEAD_DIM = 64
N_Q_HEADS = 16
N_KV_HEADS = 4
Q_PER_KV = N_Q_HEADS // N_KV_HEADS
N_SHARDS = 4
N_DEV = 8

ADAM_LR = 0.001
ADAM_B1 = 0.9
ADAM_B2 = 0.999
ADAM_EPS = 1e-08
ADAM_WD = 0.01
ADAM_STEP = 10
ADAM_C1 = 1.0 - ADAM_B1 ** ADAM_STEP
ADAM_C2 = 1.0 - ADAM_B2 ** ADAM_STEP

_INV_SQRT2 = 1.0 / math.sqrt(2.0)
_INV_SQRT2PI = 1.0 / math.sqrt(2.0 * math.pi)


def _params(*sem):
    return pltpu.CompilerParams(dimension_semantics=sem)


def _gelu(z):
    return 0.5 * z * (1.0 + lax.erf(z * _INV_SQRT2))


def _gelu_and_grad(z):
    cdf = 0.5 * (1.0 + lax.erf(z * _INV_SQRT2))
    return z * cdf, cdf + z * jnp.exp(-0.5 * z * z) * _INV_SQRT2PI


def _dot(a, b):
    return jnp.dot(a, b, preferred_element_type=f32)


def _dot_nt(a, b):
    return lax.dot_general(a, b, (((1,), (1,)), ((), ())), preferred_element_type=f32)


def _dot_tn(a, b):
    return lax.dot_general(a, b, (((0,), (0,)), ((), ())), preferred_element_type=f32)


def _dot_split(a, b):
    hi = a.astype(bf16)
    lo = (a - hi.astype(f32)).astype(bf16)
    return _dot(hi, b) + _dot(lo, b)


VMEM_TILE_BUDGET = 40 * 1024 * 1024
MAX_ROW_TILE = 2048


def _row_tile(m, fixed_bytes, row_bytes):
    tm = min(m, MAX_ROW_TILE)
    while tm > 256 and 2 * (fixed_bytes + tm * row_bytes) > VMEM_TILE_BUDGET:
        tm //= 2
    return tm


def _isz(a):
    return jnp.dtype(a.dtype).itemsize


def mm_nn(a, w3, *, name, s0=0, ns=None, add=None, out_dtype=f32):
    m, k = a.shape
    s_all, _, n_s = w3.shape
    ns = s_all if ns is None else ns
    tm = _row_tile(m, k * n_s * 2, k * _isz(a) + n_s * jnp.dtype(out_dtype).itemsize + (0 if add is None else n_s * 4))

    def body(*refs):
        if add is None:
            a_ref, w_ref, o_ref = refs
            acc = _dot(a_ref[...].astype(bf16), w_ref[0])
        else:
            a_ref, w_ref, add_ref, o_ref = refs
            acc = _dot(a_ref[...].astype(bf16), w_ref[0]) + add_ref[...]
        o_ref[...] = acc.astype(out_dtype)

    in_specs = [pl.BlockSpec((tm, k), lambda j, i: (i, 0)),
                pl.BlockSpec((1, k, n_s), lambda j, i: (s0 + j, 0, 0))]
    args = [a, w3]
    if add is not None:
        in_specs.append(pl.BlockSpec((tm, n_s), lambda j, i: (i, j)))
        args.append(add)
    return pl.pallas_call(
        body, name=name, grid=(ns, m // tm), in_specs=in_specs,
        out_specs=pl.BlockSpec((tm, n_s), lambda j, i: (i, j)),
        out_shape=jax.ShapeDtypeStruct((m, ns * n_s), out_dtype),
        compiler_params=_params("parallel", "parallel"))(*args)


def mm_nt(a_list, w3, *, name, tko=None, add=None, out_dtype=f32):
    s_all, k_out, n_s = w3.shape
    m = a_list[0].shape[0]
    na = len(a_list)
    spa = s_all // na
    tko = k_out if tko is None else tko
    tm = _row_tile(m, tko * n_s * 2, na * n_s * _isz(a_list[0]) + tko * 4 * (1 if add is None else 2))

    def body(*refs):
        a_refs = refs[:na]
        w_ref = refs[na]
        o_ref = refs[-1]
        s = pl.program_id(2)

        @pl.when(s == 0)
        def _():
            if add is None:
                o_ref[...] = jnp.zeros_like(o_ref)
            else:
                o_ref[...] = refs[na + 1][...]

        for idx in range(na):
            @pl.when(s // spa == idx)
            def _(idx=idx):
                o_ref[...] += _dot_nt(a_refs[idx][...].astype(bf16), w_ref[0])

    def a_map(idx):
        return lambda ko, i, s: (i, jnp.clip(s - idx * spa, 0, spa - 1))

    in_specs = [pl.BlockSpec((tm, n_s), a_map(idx)) for idx in range(na)]
    in_specs.append(pl.BlockSpec((1, tko, n_s), lambda ko, i, s: (s, ko, 0)))
    args = list(a_list) + [w3]
    if add is not None:
        in_specs.append(pl.BlockSpec((tm, tko), lambda ko, i, s: (i, ko)))
        args.append(add)
    return pl.pallas_call(
        body, name=name, grid=(k_out // tko, m // tm, s_all), in_specs=in_specs,
        out_specs=pl.BlockSpec((tm, tko), lambda ko, i, s: (i, ko)),
        out_shape=jax.ShapeDtypeStruct((m, k_out), out_dtype),
        compiler_params=_params("parallel", "parallel", "arbitrary"))(*args)


def mm_tn(a, b_list, c_arr, *, name, n_s, shard_rows, tki=None):
    m, k_in = a.shape
    na = len(b_list)
    s_all = sum(b.shape[1] for b in b_list) // n_s
    spa = s_all // na
    tki = k_in if tki is None else tki
    tm = _row_tile(m, tki * n_s * 4, tki * _isz(a) + na * n_s * _isz(b_list[0]))

    nsteps = m // tm
    per_blk = tki // shard_rows
    half = shard_rows // 2

    def body(c_ref, *refs):
        a_ref = refs[0]
        b_refs = refs[1:1 + na]
        o_ref, ob_ref = refs[-2], refs[-1]
        s = pl.program_id(0)
        r = pl.program_id(2)

        @pl.when(r == 0)
        def _():
            o_ref[...] = jnp.zeros_like(o_ref)

        for idx in range(na):
            @pl.when(s // spa == idx)
            def _(idx=idx):
                o_ref[0] += _dot_tn(a_ref[...].astype(bf16), b_refs[idx][...].astype(bf16))

        @pl.when(r == nsteps - 1)
        def _():
            for q in range(per_blk):
                start = pl.multiple_of(q * shard_rows + (1 - c_ref[0]) * half, 16)
                ob_ref[q] = o_ref[0, pl.ds(start, half), :].astype(bf16)

    def b_map(idx):
        def index(s, ki, r, c_ref):
            active = (s // spa) == idx
            return (jnp.where(active, r, 0), jnp.clip(s - idx * spa, 0, spa - 1))
        return index

    in_specs = [pl.BlockSpec((tm, tki), lambda s, ki, r, c_ref: (r, ki))]
    in_specs += [pl.BlockSpec((tm, n_s), b_map(idx)) for idx in range(na)]
    n_blk = k_in // tki
    return pl.pallas_call(
        body, name=name,
        grid_spec=pltpu.PrefetchScalarGridSpec(
            num_scalar_prefetch=1, grid=(s_all, n_blk, nsteps), in_specs=in_specs,
            out_specs=[pl.BlockSpec((1, tki, n_s), lambda s, ki, r, c_ref: (s, ki, 0)),
                       pl.BlockSpec((per_blk, half, n_s), lambda s, ki, r, c_ref: (s * n_blk + ki, 0, 0))]),
        out_shape=[jax.ShapeDtypeStruct((s_all, k_in, n_s), f32),
                   jax.ShapeDtypeStruct((s_all * k_in // shard_rows, half, n_s), bf16)],
        compiler_params=_params("parallel", "parallel", "arbitrary"))(c_arr, a, *b_list)


def mm_nt_rms_bwd(a_list, w3, x, r, g, dx_in, *, name, extra=None):
    s_all, d, n_s = w3.shape
    m = a_list[0].shape[0]
    na = len(a_list)
    spa = s_all // na
    ne = 0 if extra is None else 1
    tm = _row_tile(m, d * n_s * 2, na * n_s * _isz(a_list[0]) + d * 4 * (4 + ne))

    def body(*refs):
        a_refs, w_ref = refs[:na], refs[na]
        x_ref, r_ref, g_ref, dxin_ref = refs[na + 1:na + 5]
        dh2_ref, g2_ref = (refs[na + 5], refs[na + 6]) if ne else (None, None)
        outs = refs[na + 5 + 2 * ne:]
        dx_ref, dg_ref = outs[0], outs[1]
        dg2_ref = outs[2] if ne else None
        acc_ref = outs[-1]
        i, s = pl.program_id(0), pl.program_id(1)

        @pl.when(s == 0)
        def _():
            acc_ref[...] = jnp.zeros_like(acc_ref)

        for idx in range(na):
            @pl.when(s // spa == idx)
            def _(idx=idx):
                acc_ref[...] += _dot_nt(a_refs[idx][...].astype(bf16), w_ref[0])

        @pl.when(s == s_all - 1)
        def _():
            rv = r_ref[...]
            xh = x_ref[...] * rv
            total = dxin_ref[...]
            pairs = [(acc_ref[...], g_ref, dg_ref)] + ([(dh2_ref[...], g2_ref, dg2_ref)] if ne else [])
            for dh, gain_ref, dgain_ref in pairs:
                part = jnp.sum(dh * xh, axis=0, keepdims=True)

                @pl.when(i == 0)
                def _(dgain_ref=dgain_ref, part=part):
                    dgain_ref[...] = part

                @pl.when(i > 0)
                def _(dgain_ref=dgain_ref, part=part):
                    dgain_ref[...] += part

                tg = dh * gain_ref[...]
                total = total + rv * (tg - xh * jnp.mean(tg * xh, axis=1, keepdims=True))
            dx_ref[...] = total

    def a_map(idx):
        return lambda i, s: (i, jnp.clip(s - idx * spa, 0, spa - 1))

    row = pl.BlockSpec((tm, d), lambda i, s: (i, 0))
    vec = pl.BlockSpec((1, d), lambda i, s: (0, 0))
    in_specs = [pl.BlockSpec((tm, n_s), a_map(idx)) for idx in range(na)]
    in_specs += [pl.BlockSpec((1, d, n_s), lambda i, s: (s, 0, 0)), row, pl.BlockSpec((tm, 1), lambda i, s: (i, 0)), vec, row]
    args = list(a_list) + [w3, x, r, g, dx_in]
    if ne:
        in_specs += [row, vec]
        args += list(extra)
    outs = pl.pallas_call(
        body, name=name, grid=(m // tm, s_all), in_specs=in_specs, out_specs=[row] + [vec] * (1 + ne),
        out_shape=[jax.ShapeDtypeStruct((m, d), f32)] + [jax.ShapeDtypeStruct((1, d), f32)] * (1 + ne),
        scratch_shapes=[pltpu.VMEM((tm, d), f32)],
        compiler_params=_params("arbitrary", "arbitrary"))(*args)
    return outs


def mm_residual(a, w, x, *, name, gains=(), target=None):
    m, k = a.shape
    d = w.shape[1]
    ng = len(gains)
    tm = _row_tile(m, k * d * 2, k * _isz(a) + d * 4 * 3 + ng * d * 2)

    def body(*refs):
        a_ref, w_ref, x_ref = refs[:3]
        y = _dot(a_ref[...].astype(bf16), w_ref[...]) + x_ref[...]
        if target is None:
            g_refs = refs[3:3 + ng]
            y_ref = refs[3 + ng]
            h_refs = refs[4 + ng:4 + 2 * ng]
            r_ref = refs[-1]
            y_ref[...] = y
            r = lax.rsqrt(jnp.mean(y * y, axis=1, keepdims=True) + EPS)
            yh = y * r
            for g_ref, h_ref in zip(g_refs, h_refs):
                h_ref[...] = (yh * g_ref[...]).astype(bf16)
            r_ref[...] = r
        else:
            t_ref, dy_ref, s_ref = refs[3:]
            i = pl.program_id(0)
            e = y - t_ref[...]
            dy_ref[...] = e * (1.0 / d)
            part = jnp.sum(e * e, axis=0, keepdims=True)

            @pl.when(i == 0)
            def _():
                s_ref[...] = part

            @pl.when(i > 0)
            def _():
                s_ref[...] += part

    row = pl.BlockSpec((tm, d), lambda i: (i, 0))
    vec = pl.BlockSpec((1, d), lambda i: (0, 0))
    in_specs = [pl.BlockSpec((tm, k), lambda i: (i, 0)), pl.BlockSpec((k, d), lambda i: (0, 0)), row]
    if target is None:
        outs = pl.pallas_call(
            body, name=name, grid=(m // tm,), in_specs=in_specs + [vec] * ng,
            out_specs=[row] * (1 + ng) + [pl.BlockSpec((tm, 1), lambda i: (i, 0))],
            out_shape=[jax.ShapeDtypeStruct((m, d), f32)] + [jax.ShapeDtypeStruct((m, d), bf16)] * ng
            + [jax.ShapeDtypeStruct((m, 1), f32)],
            compiler_params=_params("parallel"))(a, w, x, *gains)
        return outs[0], outs[1:1 + ng], outs[-1]
    return pl.pallas_call(
        body, name=name, grid=(m // tm,), in_specs=in_specs + [row], out_specs=[row, vec],
        out_shape=[jax.ShapeDtypeStruct((m, d), f32), jax.ShapeDtypeStruct((1, d), f32)],
        compiler_params=_params("arbitrary"))(a, w, x, target)


def rms_fwd(x, gains, *, name, tr=512):
    t, d = x.shape
    tr = min(tr, t)
    ng = len(gains)

    def body(*refs):
        x_ref = refs[0]
        g_refs = refs[1:1 + ng]
        h_refs = refs[1 + ng:1 + 2 * ng]
        r_ref = refs[-1]
        xv = x_ref[...]
        r = lax.rsqrt(jnp.mean(xv * xv, axis=1, keepdims=True) + EPS)
        xh = xv * r
        for g_ref, h_ref in zip(g_refs, h_refs):
            h_ref[...] = (xh * g_ref[...]).astype(bf16)
        r_ref[...] = r

    row = pl.BlockSpec((tr, d), lambda i: (i, 0))
    vec = pl.BlockSpec((1, d), lambda i: (0, 0))
    outs = pl.pallas_call(
        body, name=name, grid=(t // tr,), in_specs=[row] + [vec] * ng,
        out_specs=[row] * ng + [pl.BlockSpec((tr, 1), lambda i: (i, 0))],
        out_shape=[jax.ShapeDtypeStruct((t, d), bf16)] * ng + [jax.ShapeDtypeStruct((t, 1), f32)],
        compiler_params=_params("parallel"))(x, *gains)
    return outs[:ng], outs[ng]


def rms_bwd(dh_list, x, r, gains, dx_in, *, name, tr=512):
    t, d = x.shape
    tr = min(tr, t)
    ng = len(gains)

    def body(*refs):
        dh_refs = refs[:ng]
        x_ref, r_ref = refs[ng], refs[ng + 1]
        g_refs = refs[ng + 2:2 * ng + 2]
        dxin_ref = refs[2 * ng + 2]
        dx_ref = refs[2 * ng + 3]
        dg_refs = refs[2 * ng + 4:]
        i = pl.program_id(0)
        rv = r_ref[...]
        xh = x_ref[...] * rv
        acc = dxin_ref[...]
        for dh_ref, g_ref, dg_ref in zip(dh_refs, g_refs, dg_refs):
            dh = dh_ref[...]
            part = jnp.sum(dh * xh, axis=0, keepdims=True)

            @pl.when(i == 0)
            def _(dg_ref=dg_ref, part=part):
                dg_ref[...] = part

            @pl.when(i > 0)
            def _(dg_ref=dg_ref, part=part):
                dg_ref[...] += part

            tg = dh * g_ref[...]
            acc = acc + rv * (tg - xh * jnp.mean(tg * xh, axis=1, keepdims=True))
        dx_ref[...] = acc

    row = pl.BlockSpec((tr, d), lambda i: (i, 0))
    vec = pl.BlockSpec((1, d), lambda i: (0, 0))
    outs = pl.pallas_call(
        body, name=name, grid=(t // tr,),
        in_specs=[row] * ng + [row, pl.BlockSpec((tr, 1), lambda i: (i, 0))] + [vec] * ng + [row],
        out_specs=[row] + [vec] * ng,
        out_shape=[jax.ShapeDtypeStruct((t, d), f32)] + [jax.ShapeDtypeStruct((1, d), f32)] * ng,
        compiler_params=_params("arbitrary"))(*dh_list, x, r, *gains, dx_in)
    return outs[0], outs[1:]


def sgu_gate_fwd(zu, zv, gv, wc, bt, *, name, tr=512):
    t, w = zu.shape
    tr = min(tr, t)
    groups = w // LANES

    def body(zu_ref, zv_ref, gv_ref, wc_ref, bt_ref, y_ref):
        vp = _gelu(zv_ref[...])
        rv = lax.rsqrt(jnp.mean(vp * vp, axis=1, keepdims=True) + EPS)
        vb = (vp * rv * gv_ref[...]).astype(bf16)
        for c in range(tr // CHUNK):
            rows = slice(c * CHUNK, (c + 1) * CHUNK)
            for g in range(groups):
                cols = slice(g * LANES, (g + 1) * LANES)
                sv = _dot(wc_ref[g], vb[rows, cols]) + bt_ref[:, g:g + 1]
                y_ref[rows, cols] = (_gelu(zu_ref[rows, cols]) * sv).astype(bf16)

    row = pl.BlockSpec((tr, w), lambda i: (i, 0))
    return pl.pallas_call(
        body, name=name, grid=(t // tr,),
        in_specs=[row, row, pl.BlockSpec((1, w), lambda i: (0, 0)),
                  pl.BlockSpec((groups, CHUNK, CHUNK), lambda i: (0, 0, 0)),
                  pl.BlockSpec((CHUNK, groups), lambda i: (0, 0))],
        out_specs=row, out_shape=jax.ShapeDtypeStruct((t, w), bf16),
        compiler_params=_params("parallel"))(zu, zv, gv, wc, bt)


def sgu_gate_bwd(zu, zv, dy, gv, wc, bt, *, name, tr=512):
    t, w = zu.shape
    tr = min(tr, t)
    groups = w // LANES
    nsteps = t // tr

    def body(zu_ref, zv_ref, dy_ref, gv_ref, wc_ref, bt_ref,
             dzu_ref, dzv_ref, dgv_ref, dws_ref, dbt_ref, dv_ref, bacc_ref):
        i = pl.program_id(0)

        @pl.when(i == 0)
        def _():
            dgv_ref[...] = jnp.zeros_like(dgv_ref)
            dws_ref[...] = jnp.zeros_like(dws_ref)
            bacc_ref[...] = jnp.zeros_like(bacc_ref)

        vp, vp_grad = _gelu_and_grad(zv_ref[...])
        rv = lax.rsqrt(jnp.mean(vp * vp, axis=1, keepdims=True) + EPS)
        vhat = vp * rv
        vb = (vhat * gv_ref[...]).astype(bf16)
        for c in range(tr // CHUNK):
            rows = slice(c * CHUNK, (c + 1) * CHUNK)
            for g in range(groups):
                cols = slice(g * LANES, (g + 1) * LANES)
                vblk = vb[rows, cols]
                sv = _dot(wc_ref[g], vblk) + bt_ref[:, g:g + 1]
                zub = zu_ref[rows, cols]
                dyb = dy_ref[rows, cols]
                ub, ub_grad = _gelu_and_grad(zub)
                dzu_ref[rows, cols] = (dyb * sv * ub_grad).astype(bf16)
                dsv = dyb * ub
                bacc_ref[:, cols] += dsv
                dsvb = dsv.astype(bf16)
                dv_ref[rows, cols] = _dot_tn(wc_ref[g], dsvb)
                dws_ref[g] += _dot_nt(dsvb, vblk)
        dv = dv_ref[...]
        dgv_ref[...] += jnp.sum(dv * vhat, axis=0, keepdims=True)
        tg = dv * gv_ref[...]
        dvp = rv * (tg - vhat * jnp.mean(tg * vhat, axis=1, keepdims=True))
        dzv_ref[...] = (dvp * vp_grad).astype(bf16)

        @pl.when(i == nsteps - 1)
        def _():
            tt = lax.broadcasted_iota(jnp.int32, (CHUNK, CHUNK), 0)
            ss = lax.broadcasted_iota(jnp.int32, (CHUNK, CHUNK), 1)
            for g in range(groups):
                dws_ref[g] = jnp.where(ss <= tt, dws_ref[g], 0.0)
                dbt_ref[:, g:g + 1] = jnp.sum(bacc_ref[:, g * LANES:(g + 1) * LANES], axis=1, keepdims=True)

    row = pl.BlockSpec((tr, w), lambda i: (i, 0))
    full3 = pl.BlockSpec((groups, CHUNK, CHUNK), lambda i: (0, 0, 0))
    return pl.pallas_call(
        body, name=name, grid=(nsteps,),
        in_specs=[row, row, row, pl.BlockSpec((1, w), lambda i: (0, 0)), full3,
                  pl.BlockSpec((CHUNK, groups), lambda i: (0, 0))],
        out_specs=[row, row, pl.BlockSpec((1, w), lambda i: (0, 0)), full3,
                   pl.BlockSpec((CHUNK, groups), lambda i: (0, 0))],
        out_shape=[jax.ShapeDtypeStruct((t, w), bf16), jax.ShapeDtypeStruct((t, w), bf16),
                   jax.ShapeDtypeStruct((1, w), f32), jax.ShapeDtypeStruct((groups, CHUNK, CHUNK), f32),
                   jax.ShapeDtypeStruct((CHUNK, groups), f32)],
        scratch_shapes=[pltpu.VMEM((tr, w), f32), pltpu.VMEM((CHUNK, w), f32)],
        compiler_params=_params("arbitrary"))(zu, zv, dy, gv, wc, bt)


HALO = 8


def _shift_down(v, halo, k, first):
    r = pltpu.roll(v, k, 0)
    hh = jnp.where(first, 0.0, pltpu.roll(halo, k, 0))
    rid = lax.broadcasted_iota(jnp.int32, (HALO, v.shape[1]), 0)
    head = jnp.where(rid < k, hh, r[0:HALO])
    if v.shape[0] == HALO:
        return head
    return jnp.concatenate([head, r[HALO:]], axis=0)


def _shift_up(v, halo, k, last):
    n = v.shape[0]
    r = pltpu.roll(v, n - k, 0)
    hh = jnp.where(last, 0.0, pltpu.roll(halo, HALO - k, 0))
    rid = lax.broadcasted_iota(jnp.int32, (HALO, v.shape[1]), 0)
    tail = jnp.where(rid >= HALO - k, hh, r[n - HALO:])
    return jnp.concatenate([r[:n - HALO], tail], axis=0)


def _conv(p, halo, w_ref, b_ref, first):
    return (w_ref[2:3, :] * p + w_ref[1:2, :] * _shift_down(p, halo, 1, first)
            + w_ref[0:1, :] * _shift_down(p, halo, 2, first) + b_ref[...])


BF16_ROWS = 16


def ffn_in_fused(h, w_in4, wg, wu, bg, bu, *, name):
    t, k = h.shape
    s_all, _, n_s = w_in4.shape
    half = s_all // 2
    tm = _row_tile(t, 2 * k * n_s * 2, k * 2 + 4 * n_s * 4 + n_s * 2)

    def body(h_ref, hh_ref, wg_ref, wu_ref, cg_ref, cu_ref, bg_ref, bu_ref, pg_ref, pu_ref, gate_ref, up_ref, a_ref):
        first = pl.program_id(1) == 0
        hv, hh = h_ref[...], hh_ref[...]
        outs = []
        for w_ref, c_ref, b_ref, p_ref, o_ref in ((wg_ref, cg_ref, bg_ref, pg_ref, gate_ref),
                                                  (wu_ref, cu_ref, bu_ref, pu_ref, up_ref)):
            p = _dot(hv, w_ref[0])
            p_ref[...] = p
            hu = _conv(p, _dot(hh, w_ref[0])[BF16_ROWS - HALO:], c_ref, b_ref, first)
            o_ref[...] = hu
            outs.append(hu)
        gate, up = outs
        a_ref[...] = (gate * jax.nn.sigmoid(gate) * up).astype(bf16)

    tile = pl.BlockSpec((tm, n_s), lambda j, i: (i, j))
    cw = pl.BlockSpec((3, n_s), lambda j, i: (0, j))
    cb = pl.BlockSpec((1, n_s), lambda j, i: (0, j))
    f = half * n_s
    return pl.pallas_call(
        body, name=name, grid=(half, t // tm),
        in_specs=[pl.BlockSpec((tm, k), lambda j, i: (i, 0)),
                  pl.BlockSpec((BF16_ROWS, k), lambda j, i: (jnp.maximum(i * (tm // BF16_ROWS) - 1, 0), 0)),
                  pl.BlockSpec((1, k, n_s), lambda j, i: (j, 0, 0)),
                  pl.BlockSpec((1, k, n_s), lambda j, i: (j + half, 0, 0)), cw, cw, cb, cb],
        out_specs=[tile] * 5,
        out_shape=[jax.ShapeDtypeStruct((t, f), f32)] * 4 + [jax.ShapeDtypeStruct((t, f), bf16)],
        compiler_params=_params("parallel", "parallel"))(h, h, w_in4, w_in4, wg, wu, bg, bu)


def _gate_grads(gate, up, dav):
    sg = jax.nn.sigmoid(gate)
    return dav * up * (sg * (1.0 + gate * (1.0 - sg))), dav * gate * sg


GATE_BWD_ROWS = 512


def ffn_gate_bwd(dy, w_out, pg, pu, gate, up, wg, wu, *, name):
    t, f = pg.shape
    d = dy.shape[1]
    tr = min(GATE_BWD_ROWS, t)
    nsteps = t // tr
    tc = f // 2

    def body(dy_ref, dyn_ref, w_ref, pg_ref, pu_ref, gate_ref, gaten_ref, up_ref, upn_ref, wg_ref, wu_ref,
             dg_ref, du_ref, sg_ref, su_ref):
        i = pl.program_id(1)
        last = i == nsteps - 1
        w = w_ref[0]
        da = _dot_nt(dy_ref[...].astype(bf16), w)
        da_n = _dot_nt(dyn_ref[...].astype(bf16), w)
        dgate, dup = _gate_grads(gate_ref[...], up_ref[...], da)
        dgate_n, dup_n = _gate_grads(gaten_ref[...], upn_ref[...], da_n)
        rid = lax.broadcasted_iota(jnp.int32, (8, tc), 0)
        for dd, d_n, c_ref, p_ref, o_ref, s_ref in ((dgate, dgate_n, wg_ref, pg_ref, dg_ref, sg_ref),
                                                    (dup, dup_n, wu_ref, pu_ref, du_ref, su_ref)):
            d1, d2 = _shift_up(dd, d_n, 1, last), _shift_up(dd, d_n, 2, last)
            o_ref[...] = (c_ref[2:3, :] * dd + c_ref[1:2, :] * d1 + c_ref[0:1, :] * d2).astype(bf16)
            p = p_ref[...]
            sums = [jnp.sum(d2 * p, axis=0, keepdims=True), jnp.sum(d1 * p, axis=0, keepdims=True),
                    jnp.sum(dd * p, axis=0, keepdims=True), jnp.sum(dd, axis=0, keepdims=True)]
            part = jnp.zeros((8, tc), f32)
            for k, sk in enumerate(sums):
                part = jnp.where(rid == k, sk, part)

            @pl.when(i == 0)
            def _(s_ref=s_ref, part=part):
                s_ref[...] = part

            @pl.when(i > 0)
            def _(s_ref=s_ref, part=part):
                s_ref[...] += part

    def nxt_rows(j, i):
        return (jnp.minimum((i + 1) * (tr // HALO), t // HALO - 1), j)

    tile = pl.BlockSpec((tr, tc), lambda j, i: (i, j))
    nxt = pl.BlockSpec((HALO, tc), nxt_rows)
    wspec = pl.BlockSpec((3, tc), lambda j, i: (0, j))
    stat = pl.BlockSpec((8, tc), lambda j, i: (0, j))
    return pl.pallas_call(
        body, name=name, grid=(2, nsteps),
        in_specs=[pl.BlockSpec((tr, d), lambda j, i: (i, 0)),
                  pl.BlockSpec((HALO, d), lambda j, i: (nxt_rows(j, i)[0], 0)),
                  pl.BlockSpec((1, tc, d), lambda j, i: (j, 0, 0)),
                  tile, tile, tile, nxt, tile, nxt, wspec, wspec],
        out_specs=[tile, tile, stat, stat],
        out_shape=[jax.ShapeDtypeStruct((t, f), bf16), jax.ShapeDtypeStruct((t, f), bf16),
                   jax.ShapeDtypeStruct((8, f), f32), jax.ShapeDtypeStruct((8, f), f32)],
        compiler_params=_params("parallel", "arbitrary"))(
            dy, dy, w_out.reshape(2, tc, d), pg, pu, gate, gate, up, up, wg, wu)


def _head_mean_matrix():
    i = lax.broadcasted_iota(jnp.int32, (LANES, LANES), 0) // HEAD_DIM
    j = lax.broadcasted_iota(jnp.int32, (LANES, LANES), 1) // HEAD_DIM
    return jnp.where(i == j, 1.0 / HEAD_DIM, 0.0).astype(bf16)


def _lane_half(shape):
    return (lax.broadcasted_iota(jnp.int32, shape, 1) % LANES) // HEAD_DIM


def q_norm_fwd(qp, g2, *, name, scale, tr=512):
    t, w = qp.shape
    tr = min(tr, t)

    def body(x_ref, g_ref, o_ref):
        bd = _head_mean_matrix()
        for cb in range(w // LANES):
            cols = slice(cb * LANES, (cb + 1) * LANES)
            xc = x_ref[:, cols]
            rh = lax.rsqrt(_dot_split(xc * xc, bd) + EPS)
            o_ref[:, cols] = (xc * rh * g_ref[...] * scale).astype(bf16)

    row = pl.BlockSpec((tr, w), lambda i: (i, 0))
    return pl.pallas_call(
        body, name=name, grid=(t // tr,), in_specs=[row, pl.BlockSpec((1, LANES), lambda i: (0, 0))],
        out_specs=row, out_shape=jax.ShapeDtypeStruct((t, w), bf16),
        compiler_params=_params("parallel"))(qp, g2)


def q_norm_bwd(dq, qp, g2, *, name, scale, tr=512):
    t, w = qp.shape
    tr = min(tr, t)

    def body(dq_ref, x_ref, g_ref, o_ref, dg_ref):
        i = pl.program_id(0)
        bd = _head_mean_matrix()
        acc = jnp.zeros((1, LANES), f32)
        for cb in range(w // LANES):
            cols = slice(cb * LANES, (cb + 1) * LANES)
            xc = x_ref[:, cols]
            rh = lax.rsqrt(_dot_split(xc * xc, bd) + EPS)
            xh = xc * rh
            dy = dq_ref[:, cols] * scale
            acc = acc + jnp.sum(dy * xh, axis=0, keepdims=True)
            tg = dy * g_ref[...]
            o_ref[:, cols] = (rh * (tg - xh * _dot_split(tg * xh, bd))).astype(bf16)

        @pl.when(i == 0)
        def _():
            dg_ref[...] = acc

        @pl.when(i > 0)
        def _():
            dg_ref[...] += acc

    row = pl.BlockSpec((tr, w), lambda i: (i, 0))
    vec = pl.BlockSpec((1, LANES), lambda i: (0, 0))
    return pl.pallas_call(
        body, name=name, grid=(t // tr,), in_specs=[row, row, vec], out_specs=[row, vec],
        out_shape=[jax.ShapeDtypeStruct((t, w), bf16), jax.ShapeDtypeStruct((1, LANES), f32)],
        compiler_params=_params("arbitrary"))(dq, qp, g2)


def kv_post_fwd(kv, g2, *, name, tr=512):
    t, w = kv.shape
    tr = min(tr, t)
    kw = w // 2

    def body(x_ref, g_ref, k_ref, v_ref):
        bd = _head_mean_matrix()
        half = _lane_half((tr, LANES))
        for cb in range(kw // LANES):
            xc = x_ref[:, cb * LANES:(cb + 1) * LANES]
            rh = lax.rsqrt(_dot_split(xc * xc, bd) + EPS)
            kn = xc * rh * g_ref[...]
            vc = x_ref[:, kw + cb * LANES:kw + (cb + 1) * LANES]
            for src, dst in ((kn, k_ref), (vc, v_ref)):
                sw = pltpu.roll(src, HEAD_DIM, 1)
                for hf in range(2):
                    blk = 2 * cb + hf
                    dst[:, blk * LANES:(blk + 1) * LANES] = jnp.where(half == hf, src, sw).astype(bf16)

    return pl.pallas_call(
        body, name=name, grid=(t // tr,),
        in_specs=[pl.BlockSpec((tr, w), lambda i: (i, 0)), pl.BlockSpec((1, LANES), lambda i: (0, 0))],
        out_specs=[pl.BlockSpec((tr, 2 * kw), lambda i: (i, 0))] * 2,
        out_shape=[jax.ShapeDtypeStruct((t, 2 * kw), bf16)] * 2,
        compiler_params=_params("parallel"))(kv, g2)


def kv_post_bwd(dk2, dv2, kv, g2, *, name, tr=512):
    t, w = kv.shape
    tr = min(tr, t)
    kw = w // 2

    def body(dk_ref, dv_ref, x_ref, g_ref, o_ref, dg_ref):
        i = pl.program_id(0)
        bd = _head_mean_matrix()
        half = _lane_half((tr, LANES))
        acc = jnp.zeros((1, LANES), f32)

        def fold(ref, cb):
            a = ref[:, (2 * cb) * LANES:(2 * cb + 1) * LANES]
            b = ref[:, (2 * cb + 1) * LANES:(2 * cb + 2) * LANES]
            return jnp.where(half == 0, a + pltpu.roll(a, HEAD_DIM, 1), b + pltpu.roll(b, HEAD_DIM, 1))

        for cb in range(kw // LANES):
            cols = slice(cb * LANES, (cb + 1) * LANES)
            xc = x_ref[:, cols]
            rh = lax.rsqrt(_dot_split(xc * xc, bd) + EPS)
            xh = xc * rh
            dy = fold(dk_ref, cb)
            acc = acc + jnp.sum(dy * xh, axis=0, keepdims=True)
            tg = dy * g_ref[...]
            o_ref[:, cols] = (rh * (tg - xh * _dot_split(tg * xh, bd))).astype(bf16)
            o_ref[:, kw + cb * LANES:kw + (cb + 1) * LANES] = fold(dv_ref, cb).astype(bf16)

        @pl.when(i == 0)
        def _():
            dg_ref[...] = acc

        @pl.when(i > 0)
        def _():
            dg_ref[...] += acc

    dup = pl.BlockSpec((tr, 2 * kw), lambda i: (i, 0))
    row = pl.BlockSpec((tr, w), lambda i: (i, 0))
    vec = pl.BlockSpec((1, LANES), lambda i: (0, 0))
    return pl.pallas_call(
        body, name=name, grid=(t // tr,), in_specs=[dup, dup, row, vec], out_specs=[row, vec],
        out_shape=[jax.ShapeDtypeStruct((t, w), bf16), jax.ShapeDtypeStruct((1, LANES), f32)],
        compiler_params=_params("arbitrary"))(dk2, dv2, kv, g2)


def _slope(h):
    return 2.0 ** (-8.0 * (h + 1) / N_Q_HEADS)


GROUP_ROWS = Q_PER_KV * CHUNK


def _band_mask(n):
    tq = lax.broadcasted_iota(jnp.int32, (GROUP_ROWS, 2 * CHUNK), 0) % CHUNK
    jk = lax.broadcasted_iota(jnp.int32, (GROUP_ROWS, 2 * CHUNK), 1)
    dist = tq + CHUNK - jk
    ok = (dist >= 0) & (dist < CHUNK) & jnp.logical_not((n == 0) & (jk < CHUNK))
    return dist.astype(f32), ok


def _band(ref, n, kh):
    p0 = pl.multiple_of(jnp.maximum(n - 1, 0) * CHUNK, CHUNK)
    c0 = pl.multiple_of(n * CHUNK, CHUNK)
    cols = slice(kh * LANES, (kh + 1) * LANES)
    return jnp.concatenate([ref[pl.ds(p0, CHUNK), cols], ref[pl.ds(c0, CHUNK), cols]], axis=0)


def _stack_heads(ref, kh, half):
    parts = []
    for cb in (2 * kh, 2 * kh + 1):
        xc = ref[:, cb * LANES:(cb + 1) * LANES].astype(f32)
        parts += [jnp.where(half == hf, xc, 0.0).astype(bf16) for hf in range(2)]
    return jnp.concatenate(parts, axis=0)


def _unstack_heads(x4, half):
    return (jnp.where(half == 0, x4[0:CHUNK], x4[CHUNK:2 * CHUNK]),
            jnp.where(half == 0, x4[2 * CHUNK:3 * CHUNK], x4[3 * CHUNK:]))


def _per_head_column(kh, values):
    grp = lax.broadcasted_iota(jnp.int32, (GROUP_ROWS, 1), 0) // CHUNK
    col = jnp.full((GROUP_ROWS, 1), values[0], f32)
    for g in range(1, Q_PER_KV):
        col = jnp.where(grp == g, values[g], col)
    return col


def _softmax_band(q4, kband, dist, ok, slope, sink):
    s = _dot_nt(q4, kband)
    s = jnp.where(ok, s - slope * dist, -jnp.inf)
    m = jnp.maximum(jnp.max(s, axis=1, keepdims=True), sink)
    e = jnp.exp(s - m)
    es = jnp.exp(sink - m)
    den = jnp.sum(e, axis=1, keepdims=True) + es
    return e / den, es / den


def attn_fwd(q, k2, v2, sinks, *, name):
    t, w = q.shape
    nb = t // CHUNK

    def body(sink_ref, q_ref, k_ref, v_ref, o_ref):
        n = pl.program_id(0)
        dist, ok = _band_mask(n)
        half = _lane_half((CHUNK, LANES))
        for kh in range(N_KV_HEADS):
            heads = [Q_PER_KV * kh + g for g in range(Q_PER_KV)]
            slope = _per_head_column(kh, [_slope(h) for h in heads])
            sink = _per_head_column(kh, [sink_ref[h] for h in heads])
            q4 = _stack_heads(q_ref, kh, half)
            p, _ = _softmax_band(q4, _band(k_ref, n, kh), dist, ok, slope, sink)
            o4 = _dot(p.astype(bf16), _band(v_ref, n, kh))
            lo, hi = _unstack_heads(o4, half)
            o_ref[:, (2 * kh) * LANES:(2 * kh + 1) * LANES] = lo.astype(bf16)
            o_ref[:, (2 * kh + 1) * LANES:(2 * kh + 2) * LANES] = hi.astype(bf16)

    full = pl.BlockSpec((t, k2.shape[1]), lambda n: (0, 0))
    return pl.pallas_call(
        body, name=name, grid=(nb,),
        in_specs=[pl.BlockSpec(memory_space=pltpu.SMEM), pl.BlockSpec((CHUNK, w), lambda n: (n, 0)), full, full],
        out_specs=pl.BlockSpec((CHUNK, w), lambda n: (n, 0)),
        out_shape=jax.ShapeDtypeStruct((t, w), bf16),
        compiler_params=_params("parallel"))(sinks, q, k2, v2)


def attn_bwd(q, k2, v2, do, sinks, *, name):
    t, w = q.shape
    nb = t // CHUNK
    kw = k2.shape[1]

    def body(sink_ref, q_ref, k_ref, v_ref, do_ref, dq_ref, dk_ref, dv_ref, ds_ref, kc_ref, vc_ref):
        n = pl.program_id(0)

        @pl.when(n == 0)
        def _():
            ds_ref[...] = jnp.zeros_like(ds_ref)
            kc_ref[...] = jnp.zeros_like(kc_ref)
            vc_ref[...] = jnp.zeros_like(vc_ref)
            dk_ref[...] = jnp.zeros_like(dk_ref)
            dv_ref[...] = jnp.zeros_like(dv_ref)

        @pl.when(n == nb)
        def _():
            dk_ref[...] = kc_ref[...]
            dv_ref[...] = vc_ref[...]

        @pl.when(n < nb)
        def _():
            dist, ok = _band_mask(n)
            half = _lane_half((CHUNK, LANES))
            lane = lax.broadcasted_iota(jnp.int32, (1, LANES), 1)
            sink_acc = jnp.zeros((1, LANES), f32)
            for kh in range(N_KV_HEADS):
                heads = [Q_PER_KV * kh + g for g in range(Q_PER_KV)]
                slope = _per_head_column(kh, [_slope(h) for h in heads])
                sink = _per_head_column(kh, [sink_ref[h] for h in heads])
                q4 = _stack_heads(q_ref, kh, half)
                do4 = _stack_heads(do_ref, kh, half)
                kband = _band(k_ref, n, kh)
                vband = _band(v_ref, n, kh)
                p, ps = _softmax_band(q4, kband, dist, ok, slope, sink)
                dp = _dot_nt(do4, vband)
                delta = jnp.sum(p * dp, axis=1, keepdims=True)
                dsb = (p * (dp - delta)).astype(bf16)
                sd = ps * delta
                for g, h in enumerate(heads):
                    part = jnp.sum(sd[g * CHUNK:(g + 1) * CHUNK], axis=0, keepdims=True)
                    sink_acc = sink_acc + jnp.where(lane == h, -part, 0.0)
                lo, hi = _unstack_heads(_dot(dsb, kband), half)
                dq_ref[:, (2 * kh) * LANES:(2 * kh + 1) * LANES] = lo
                dq_ref[:, (2 * kh + 1) * LANES:(2 * kh + 2) * LANES] = hi
                dkb = _dot_tn(dsb, q4)
                dvb = _dot_tn(p.astype(bf16), do4)
                cols = slice(kh * LANES, (kh + 1) * LANES)
                dk_ref[:, cols] = kc_ref[:, cols] + dkb[0:CHUNK]
                dv_ref[:, cols] = vc_ref[:, cols] + dvb[0:CHUNK]
                kc_ref[:, cols] = dkb[CHUNK:]
                vc_ref[:, cols] = dvb[CHUNK:]
            ds_ref[...] += sink_acc

    full = pl.BlockSpec((t, kw), lambda n: (0, 0))
    qblk = pl.BlockSpec((CHUNK, w), lambda n: (jnp.minimum(n, nb - 1), 0))
    kblk = pl.BlockSpec((CHUNK, kw), lambda n: (jnp.maximum(n - 1, 0), 0))
    return pl.pallas_call(
        body, name=name, grid=(nb + 1,),
        in_specs=[pl.BlockSpec(memory_space=pltpu.SMEM), qblk, full, full, qblk],
        out_specs=[qblk, kblk, kblk, pl.BlockSpec((1, LANES), lambda n: (0, 0))],
        out_shape=[jax.ShapeDtypeStruct((t, w), f32), jax.ShapeDtypeStruct((t, kw), f32),
                   jax.ShapeDtypeStruct((t, kw), f32), jax.ShapeDtypeStruct((1, LANES), f32)],
        scratch_shapes=[pltpu.VMEM((CHUNK, kw), f32), pltpu.VMEM((CHUNK, kw), f32)],
        compiler_params=_params("arbitrary"))(sinks, q, k2, v2, do)


def loss_head(y, target, *, name, tr=512):
    t, d = y.shape
    tr = min(tr, t)

    def body(y_ref, t_ref, dy_ref, s_ref):
        i = pl.program_id(0)
        e = y_ref[...] - t_ref[...]
        dy_ref[...] = e * (1.0 / d)
        part = jnp.sum(e * e, axis=0, keepdims=True)

        @pl.when(i == 0)
        def _():
            s_ref[...] = part

        @pl.when(i > 0)
        def _():
            s_ref[...] += part

    row = pl.BlockSpec((tr, d), lambda i: (i, 0))
    vec = pl.BlockSpec((1, d), lambda i: (0, 0))
    return pl.pallas_call(
        body, name=name, grid=(t // tr,), in_specs=[row, row], out_specs=[row, vec],
        out_shape=[jax.ShapeDtypeStruct((t, d), f32), jax.ShapeDtypeStruct((1, d), f32)],
        compiler_params=_params("arbitrary"))(y, target)


N_STEPS = 8


def _row_blocks(shape):
    if len(shape) == 2:
        r, c = shape
        return (r // N_STEPS, c), (lambda s: (s, 0))
    l, r, c = shape
    per = N_STEPS // l
    return (1, r // per, c), (lambda s: (s // per, s % per, 0))


CAST_STEPS = 4


def cast_into_slot(arrays, k_arr, *, name):
    in_specs, out_specs, out_shape, layers = [], [], [], []
    for a in arrays:
        r, c = a.shape[-2:]
        rb = r // CAST_STEPS
        if a.ndim == 2:
            in_specs.append(pl.BlockSpec((rb, c), lambda s, k: (s, 0)))
            layers.append(None)
        else:
            for l in range(a.shape[0]):
                in_specs.append(pl.BlockSpec((1, rb, c), lambda s, k, l=l: (l, s, 0)))
                layers.append(l)
        for _ in range(1 if a.ndim == 2 else a.shape[0]):
            out_specs.append(pl.BlockSpec((1, rb, c), lambda s, k: (k[0], s, 0)))
            out_shape.append(jax.ShapeDtypeStruct((N_SHARDS, r, c), bf16))
    n = len(in_specs)

    def body(k_ref, *refs):
        for i_ref, o_ref, l in zip(refs[:n], refs[n:], layers):
            o_ref[0] = (i_ref[...] if l is None else i_ref[0]).astype(bf16)

    args = []
    for a in arrays:
        args += [a] * (1 if a.ndim == 2 else a.shape[0])
    return pl.pallas_call(
        body, name=name,
        grid_spec=pltpu.PrefetchScalarGridSpec(num_scalar_prefetch=1, grid=(CAST_STEPS,),
                                               in_specs=in_specs, out_specs=out_specs),
        out_shape=out_shape, compiler_params=_params("parallel"))(k_arr, *args)


def adamw(ws, gs, ms, vs, *, name):
    n = len(ws)
    specs, g_specs, g_count = [], [], []
    for w, g_list in zip(ws, gs):
        blk, index = _row_blocks(w.shape)
        specs.append(pl.BlockSpec(blk, index))
        layers = len(g_list)
        per = N_STEPS // layers
        g_count.append(layers)
        for l in range(layers):
            g_specs.append(pl.BlockSpec(blk[-2:], lambda s, l=l, per=per: (jnp.where(s // per == l, s % per, 0), 0)))
    ng = len(g_specs)

    def body(*refs):
        s = pl.program_id(0)
        g_refs = refs[3 * n:3 * n + ng]
        outs = refs[3 * n + ng:]
        off = 0
        for i in range(n):
            w_ref, m_ref, v_ref = refs[i], refs[n + i], refs[2 * n + i]
            go_ref, d_ref, nm_ref, nv_ref = (outs[k * n + i] for k in range(4))
            layers = g_count[i]
            g = g_refs[off][...]
            for l in range(1, layers):
                g = jnp.where(s // (N_STEPS // layers) == l, g_refs[off + l][...], g)
            off += layers
            g = g.reshape(w_ref.shape)
            m = ADAM_B1 * m_ref[...] + (1.0 - ADAM_B1) * g
            v = ADAM_B2 * v_ref[...] + (1.0 - ADAM_B2) * (g * g)
            m_hat = m / ADAM_C1
            v_hat = v / ADAM_C2
            go_ref[...] = g
            d_ref[...] = -ADAM_LR * (m_hat / (jnp.sqrt(v_hat) + ADAM_EPS) + ADAM_WD * w_ref[...])
            nm_ref[...] = m
            nv_ref[...] = v

    outs = pl.pallas_call(
        body, name=name, grid=(N_STEPS,), in_specs=specs * 3 + g_specs, out_specs=specs * 4,
        out_shape=[jax.ShapeDtypeStruct(a.shape, f32) for a in ws] * 4,
        compiler_params=_params("parallel"))(*ws, *ms, *vs, *[g for g_list in gs for g in g_list])
    return [outs[k * n:(k + 1) * n] for k in range(4)]


def _adamw_update(w, g, m, v):
    m = ADAM_B1 * m + (1.0 - ADAM_B1) * g
    v = ADAM_B2 * v + (1.0 - ADAM_B2) * (g * g)
    m_hat = m / ADAM_C1
    v_hat = v / ADAM_C2
    return -ADAM_LR * (m_hat / (jnp.sqrt(v_hat) + ADAM_EPS) + ADAM_WD * w), m, v


def adamw_small(ws, gs, ms, vs, *, name):
    n = len(ws)

    def body(*refs):
        for i in range(n):
            w_ref, g_ref, m_ref, v_ref = (refs[k * n + i] for k in range(4))
            d_ref, nm_ref, nv_ref = (refs[(4 + k) * n + i] for k in range(3))
            d_ref[...], nm_ref[...], nv_ref[...] = _adamw_update(w_ref[...], g_ref[...], m_ref[...], v_ref[...])

    outs = pl.pallas_call(
        body, name=name, out_shape=[jax.ShapeDtypeStruct(a.shape, f32) for a in ws] * 3)(*ws, *gs, *ms, *vs)
    return outs[:n], outs[n:2 * n], outs[2 * n:]


def _place():
    return lax.axis_index("x"), lax.axis_index("y"), lax.axis_index("c")


def gather_shards(bufs, *, name, split):
    n = len(bufs)

    def body(*refs):
        bufs_ = refs[:n]
        isend, irecv, dsend, drecv = refs[2 * n:]
        x, y, c = _place()
        k = 2 * x + y
        peers = [(1 - x, y, c), (x, 1 - y, c), (1 - x, 1 - y, c)]
        peer_k = [2 * (1 - x) + y, 2 * x + (1 - y), 2 * (1 - x) + (1 - y)]

        def slab(a, q, h):
            if not split[a]:
                return bufs_[a].at[q]
            half = bufs_[a].shape[1] // 2
            return bufs_[a].at[q, pl.ds(pl.multiple_of(h * half, 16), half)]

        def ici(a, j, q):
            return pltpu.make_async_remote_copy(
                src_ref=slab(a, q, c), dst_ref=slab(a, q, c), send_sem=isend.at[3 * a + j], recv_sem=irecv.at[3 * a + j],
                device_id=peers[j], device_id_type=MESH)

        def d2d(a, j, h):
            return pltpu.make_async_remote_copy(
                src_ref=slab(a, peer_k[j], h), dst_ref=slab(a, peer_k[j], h), send_sem=dsend.at[3 * a + j],
                recv_sem=drecv.at[3 * a + j], device_id=(x, y, 1 - c), device_id_type=MESH)

        for a in range(n):
            for j in range(3):
                ici(a, j, k).start()
        for a in range(n):
            for j in range(3):
                ici(a, j, peer_k[j]).wait_recv()
                if split[a]:
                    d2d(a, j, c).start()
        for a in range(n):
            for j in range(3):
                if split[a]:
                    d2d(a, j, 1 - c).wait_recv()
        for a in range(n):
            for j in range(3):
                ici(a, j, k).wait_send()
                if split[a]:
                    d2d(a, j, c).wait_send()

    return pl.pallas_call(
        body, name=name, in_specs=[ANY] * n, out_specs=[ANY] * n,
        out_shape=[jax.ShapeDtypeStruct(b.shape, b.dtype) for b in bufs],
        input_output_aliases={i: i for i in range(n)},
        scratch_shapes=[pltpu.SemaphoreType.DMA((3 * n,))] * 4)(*bufs)


HBM = pl.BlockSpec(memory_space=pltpu.HBM)
SEM = pl.BlockSpec(memory_space=pltpu.SEMAPHORE)
DATAFLOW = pltpu.SideEffectType.DATAFLOW_SIDE_EFFECTING


def _chip_peers():
    x, y, c = _place()
    return 2 * x + y, [(1 - x, y, c), (x, 1 - y, c), (1 - x, 1 - y, c)], [2 * (1 - x) + y, 2 * x + (1 - y), 2 * (1 - x) + (1 - y)]


def _half_slab(ref, q, h):
    half = ref.shape[1] // 2
    return ref.at[q, pl.ds(pl.multiple_of(h * half, BF16_ROWS), half)]


def gather_start(bufs, groups, after, *, name):
    n = len(bufs)
    ng = len(groups)

    def body(*refs):
        ins = refs[:n]
        sends, recvs = refs[2 * n + 1:2 * n + 1 + ng], refs[2 * n + 1 + ng:2 * n + 1 + 2 * ng]
        token = refs[-1]
        c = lax.axis_index("c")
        k, peers, _ = _chip_peers()
        for gi, grp in enumerate(groups):
            for pos, a in enumerate(grp):
                for j in range(3):
                    pltpu.make_async_remote_copy(
                        src_ref=_half_slab(ins[a], k, c), dst_ref=_half_slab(ins[a], k, c), send_sem=sends[gi].at[3 * pos + j],
                        recv_sem=recvs[gi].at[3 * pos + j], device_id=peers[j], device_id_type=MESH).start()
        token[...] = jnp.zeros_like(token)

    sems = [pltpu.SemaphoreType.DMA((3 * len(grp),)) for grp in groups]
    outs = pl.pallas_call(
        body, name=name, in_specs=[HBM] * n + [ANY],
        out_specs=[HBM] * n + [SEM] * (2 * ng) + [pl.BlockSpec(memory_space=pltpu.VMEM)],
        out_shape=[pltpu.HBM(b.shape, b.dtype) for b in bufs] + sems + sems + [jax.ShapeDtypeStruct((8, LANES), f32)],
        input_output_aliases={i: i for i in range(n)},
        compiler_params=pltpu.CompilerParams(has_side_effects=DATAFLOW))(
            *[pltpu.with_memory_space_constraint(b, pltpu.HBM) for b in bufs], after)
    return outs[:n], outs[n:n + ng], outs[n + ng:n + 2 * ng], outs[-1]


def gather_wait(bufs, send_sems, recv_sems, after, *, name):
    n = len(bufs)

    def body(*refs):
        ins = refs[:n]
        send, recv = refs[n], refs[n + 1]
        c = lax.axis_index("c")
        k, peers, peer_k = _chip_peers()
        for a in range(n):
            for j in range(3):
                copy = pltpu.make_async_remote_copy(
                    src_ref=_half_slab(ins[a], k, c), dst_ref=_half_slab(ins[a], peer_k[j], c), send_sem=send.at[3 * a + j],
                    recv_sem=recv.at[3 * a + j], device_id=peers[j], device_id_type=MESH)
                copy.wait_send()
                copy.wait_recv()

    return pl.pallas_call(
        body, name=name, in_specs=[HBM] * n + [SEM, SEM, ANY], out_specs=[HBM] * n,
        out_shape=[pltpu.HBM(b.shape, b.dtype) for b in bufs],
        input_output_aliases={i: i for i in range(n)},
        compiler_params=pltpu.CompilerParams(has_side_effects=DATAFLOW))(*bufs, send_sems, recv_sems, after)


def forward_halves(bufs, *, name):
    n = len(bufs)

    def body(*refs):
        bufs_ = refs[:n]
        send, recv = refs[2 * n:]
        x, y, c = _place()
        _, _, peer_k = _chip_peers()

        def copy(a, j, h):
            return pltpu.make_async_remote_copy(
                src_ref=_half_slab(bufs_[a], peer_k[j], h), dst_ref=_half_slab(bufs_[a], peer_k[j], h),
                send_sem=send.at[3 * a + j], recv_sem=recv.at[3 * a + j], device_id=(x, y, 1 - c), device_id_type=MESH)

        for a in range(n):
            for j in range(3):
                copy(a, j, c).start()
        for a in range(n):
            for j in range(3):
                copy(a, j, 1 - c).wait_recv()
        for a in range(n):
            for j in range(3):
                copy(a, j, c).wait_send()

    return pl.pallas_call(
        body, name=name, in_specs=[ANY] * n, out_specs=[ANY] * n,
        out_shape=[jax.ShapeDtypeStruct(b.shape, b.dtype) for b in bufs],
        input_output_aliases={i: i for i in range(n)},
        scratch_shapes=[pltpu.SemaphoreType.DMA((3 * n,))] * 2)(*bufs)


def _sibling_copies(srcs, lands, send, recv):
    x, y, c = _place()
    return [pltpu.make_async_remote_copy(src_ref=srcs[a], dst_ref=lands[a], send_sem=send.at[a], recv_sem=recv.at[a],
                                         device_id=(x, y, 1 - c), device_id_type=MESH) for a in range(len(srcs))]


def sibling_start(arrays, after, *, name):
    n = len(arrays)
    lands = [pltpu.with_memory_space_constraint(lax.empty(a.shape, a.dtype), pltpu.HBM) for a in arrays]

    def body(*refs):
        for cp in _sibling_copies(refs[:n], refs[n:2 * n], refs[4 * n + 1], refs[4 * n + 2]):
            cp.start()
        refs[-1][...] = jnp.zeros_like(refs[-1])

    bufs = list(arrays) + lands
    sems = [pltpu.SemaphoreType.DMA((n,))] * 2
    outs = pl.pallas_call(
        body, name=name, in_specs=[HBM] * (2 * n) + [ANY],
        out_specs=[HBM] * (2 * n) + [SEM] * 2 + [pl.BlockSpec(memory_space=pltpu.VMEM)],
        out_shape=[pltpu.HBM(b.shape, b.dtype) for b in bufs] + sems + [jax.ShapeDtypeStruct((8, LANES), f32)],
        input_output_aliases={i: i for i in range(2 * n)},
        compiler_params=pltpu.CompilerParams(has_side_effects=DATAFLOW))(
            *[pltpu.with_memory_space_constraint(b, pltpu.HBM) for b in bufs], after)
    return (n, outs[:-1]), outs[-1]


def sibling_wait(state, after, *, name):
    n, held = state

    def body(*refs):
        for cp in _sibling_copies(refs[:n], refs[n:2 * n], refs[2 * n], refs[2 * n + 1]):
            cp.wait_send()
            cp.wait_recv()

    outs = pl.pallas_call(
        body, name=name, in_specs=[HBM] * (2 * n) + [SEM] * 2 + [ANY], out_specs=[HBM] * (2 * n),
        out_shape=[pltpu.HBM(b.shape, b.dtype) for b in held[:2 * n]],
        input_output_aliases={i: i for i in range(2 * n)},
        compiler_params=pltpu.CompilerParams(has_side_effects=DATAFLOW))(*held, after)
    return outs[n:]


def sibling_exchange(arrays, *, name):
    n = len(arrays)

    def body(*refs):
        ins, outs = refs[:n], refs[n:2 * n]
        send, recv = refs[2 * n:]
        x, y, c = _place()

        def copy(a):
            return pltpu.make_async_remote_copy(
                src_ref=ins[a], dst_ref=outs[a], send_sem=send.at[a], recv_sem=recv.at[a],
                device_id=(x, y, 1 - c), device_id_type=MESH)

        for a in range(n):
            copy(a).start()
        for a in range(n):
            copy(a).wait_recv()
        for a in range(n):
            copy(a).wait_send()

    return pl.pallas_call(
        body, name=name, in_specs=[ANY] * n, out_specs=[ANY] * n,
        out_shape=[jax.ShapeDtypeStruct(a.shape, a.dtype) for a in arrays],
        scratch_shapes=[pltpu.SemaphoreType.DMA((n,)), pltpu.SemaphoreType.DMA((n,))])(*arrays)


ALL_MASKS = [(mx, my, mc) for mx in (0, 1) for my in (0, 1) for mc in (0, 1)][1:]


def _scatter_copies(srcs, lands, ev, send, recv, esend, erecv):
    x, y, c = _place()
    me = 4 * x + 2 * y + c
    k, peers, peer_k = _chip_peers()
    out = []
    for a in range(len(srcs)):
        for j in range(3):
            out.append(pltpu.make_async_remote_copy(
                src_ref=srcs[a].at[peer_k[j]], dst_ref=lands[a].at[j], send_sem=send.at[3 * a + j],
                recv_sem=recv.at[3 * a + j], device_id=peers[j], device_id_type=MESH))
    start_ev, wait_ev = [], []
    if ev is not None:
        for j, (mx, my, mc) in enumerate(ALL_MASKS):
            peer = (x ^ mx, y ^ my, c ^ mc)
            start_ev.append(pltpu.make_async_remote_copy(
                src_ref=ev.at[me], dst_ref=ev.at[me], send_sem=esend.at[j], recv_sem=erecv.at[j],
                device_id=peer, device_id_type=MESH))
            wait_ev.append(pltpu.make_async_remote_copy(
                src_ref=ev.at[me], dst_ref=ev.at[me ^ (4 * mx + 2 * my + mc)], send_sem=esend.at[j],
                recv_sem=erecv.at[j], device_id=peer, device_id_type=MESH))
    return out, start_ev, wait_ev


def chip_scatter_start(arrays, everyone, after, *, name):
    n = len(arrays)
    ne = 0 if everyone is None else 1
    lands = [pltpu.with_memory_space_constraint(lax.empty((3,) + a.shape[1:], a.dtype), pltpu.HBM) for a in arrays]

    def body(*refs):
        srcs, lands_ = refs[:n], refs[n:2 * n]
        ev = refs[2 * n] if ne else None
        sems = refs[2 * n + ne + 1 + 2 * n + ne:-1]
        send, recv = sems[0], sems[1]
        esend, erecv = (sems[2], sems[3]) if ne else (None, None)
        copies, start_ev, _ = _scatter_copies(srcs, lands_, ev, send, recv, esend, erecv)
        for cp in start_ev + copies:
            cp.start()
        refs[-1][...] = jnp.zeros_like(refs[-1])

    sem_shapes = [pltpu.SemaphoreType.DMA((3 * n,))] * 2 + [pltpu.SemaphoreType.DMA((7,))] * (2 * ne)
    bufs = list(arrays) + lands + ([everyone] if ne else [])
    outs = pl.pallas_call(
        body, name=name, in_specs=[HBM] * len(bufs) + [ANY],
        out_specs=[HBM] * len(bufs) + [SEM] * len(sem_shapes) + [pl.BlockSpec(memory_space=pltpu.VMEM)],
        out_shape=[pltpu.HBM(b.shape, b.dtype) for b in bufs] + sem_shapes + [jax.ShapeDtypeStruct((8, LANES), f32)],
        input_output_aliases={i: i for i in range(len(bufs))},
        compiler_params=pltpu.CompilerParams(has_side_effects=DATAFLOW))(
            *[pltpu.with_memory_space_constraint(b, pltpu.HBM) for b in bufs], after)
    return (n, ne, outs[:-1]), outs[-1]


def chip_scatter_wait(state, after, *, name):
    n, ne, held = state
    nb = 2 * n + ne
    bufs, sems = held[:nb], held[nb:]

    def body(*refs):
        srcs, lands_ = refs[:n], refs[n:2 * n]
        ev = refs[2 * n] if ne else None
        sems_ = refs[nb:nb + len(sems)]
        esend, erecv = (sems_[2], sems_[3]) if ne else (None, None)
        copies, _, wait_ev = _scatter_copies(srcs, lands_, ev, sems_[0], sems_[1], esend, erecv)
        for cp in wait_ev + copies:
            cp.wait_send()
            cp.wait_recv()

    outs = pl.pallas_call(
        body, name=name, in_specs=[HBM] * nb + [SEM] * len(sems) + [ANY], out_specs=[HBM] * nb,
        out_shape=[pltpu.HBM(b.shape, b.dtype) for b in bufs],
        input_output_aliases={i: i for i in range(nb)},
        compiler_params=pltpu.CompilerParams(has_side_effects=DATAFLOW))(*bufs, *sems, after)
    return outs[n:2 * n], (outs[2 * n] if ne else None)


def sibling_merge(bufs, *, name):
    n = len(bufs)

    def body(*refs):
        bufs_ = refs[:n]
        send, recv = refs[2 * n:]
        x, y, c = _place()

        def copy(u, h):
            return pltpu.make_async_remote_copy(
                src_ref=bufs_[u].at[h], dst_ref=bufs_[u].at[h], send_sem=send.at[u], recv_sem=recv.at[u],
                device_id=(x, y, 1 - c), device_id_type=MESH)

        for u in range(n):
            copy(u, c).start()
        for u in range(n):
            copy(u, 1 - c).wait_recv()
        for u in range(n):
            copy(u, c).wait_send()

    return pl.pallas_call(
        body, name=name, in_specs=[ANY] * n, out_specs=[ANY] * n,
        out_shape=[jax.ShapeDtypeStruct(b.shape, b.dtype) for b in bufs],
        input_output_aliases={i: i for i in range(n)},
        scratch_shapes=[pltpu.SemaphoreType.DMA((n,)), pltpu.SemaphoreType.DMA((n,))])(*bufs)


def sum_leading(a, *, name):
    n, r, c = a.shape

    def body(a_ref, o_ref):
        acc = a_ref[0]
        for i in range(1, n):
            acc = acc + a_ref[i]
        o_ref[...] = acc

    rb = r // 2 if r % 16 == 0 else r
    return pl.pallas_call(
        body, name=name, grid=(r // rb,), in_specs=[pl.BlockSpec((n, rb, c), lambda i: (0, i, 0))],
        out_specs=pl.BlockSpec((rb, c), lambda i: (i, 0)), out_shape=jax.ShapeDtypeStruct((r, c), f32),
        compiler_params=_params("parallel"))(a)


def _half_rows(shape):
    return shape[1] // 2 // 2


def rs_cast_other_half(grads, c_arr, *, name):
    n = len(grads)

    def body(c_ref, *refs):
        for i_ref, o_ref in zip(refs[:n], refs[n:]):
            o_ref[...] = i_ref[...].astype(bf16)

    in_specs = [pl.BlockSpec((1, _half_rows(g.shape), g.shape[2]), lambda s, r, c_ref: (s, (1 - c_ref[0]) * 2 + r, 0))
                for g in grads]
    out_specs = [pl.BlockSpec((1, _half_rows(g.shape), g.shape[2]), lambda s, r, c_ref: (s, r, 0)) for g in grads]
    return pl.pallas_call(
        body, name=name,
        grid_spec=pltpu.PrefetchScalarGridSpec(num_scalar_prefetch=1, grid=(N_SHARDS, 2),
                                               in_specs=in_specs, out_specs=out_specs),
        out_shape=[jax.ShapeDtypeStruct((N_SHARDS, g.shape[1] // 2, g.shape[2]), bf16) for g in grads],
        compiler_params=_params("parallel", "parallel"))(c_arr, *grads)


def rs_add_sibling(grads, recvd, ck_arr, *, name):
    n = len(grads)

    def body(ck_ref, *refs):
        s = pl.program_id(1)
        for u in range(n):
            g_ref, r_ref = refs[u], refs[n + u]
            qb_ref, own_ref = refs[2 * n + u], refs[3 * n + u]
            q = g_ref[0] + r_ref[0].astype(f32)
            qb_ref[0] = q.astype(bf16)

            @pl.when(s == ck_ref[1])
            def _(own_ref=own_ref, q=q):
                own_ref[...] = q

    in_specs = [pl.BlockSpec((1, _half_rows(g.shape), g.shape[2]), lambda r, s, ck: (s, ck[0] * 2 + r, 0)) for g in grads]
    in_specs += [pl.BlockSpec((1, _half_rows(g.shape), g.shape[2]), lambda r, s, ck: (s, r, 0)) for g in grads]
    out_specs = [pl.BlockSpec((1, _half_rows(g.shape), g.shape[2]), lambda r, s, ck: (s, r, 0)) for g in grads]
    out_specs += [pl.BlockSpec((_half_rows(g.shape), g.shape[2]), lambda r, s, ck: (r, 0)) for g in grads]
    outs = pl.pallas_call(
        body, name=name,
        grid_spec=pltpu.PrefetchScalarGridSpec(num_scalar_prefetch=1, grid=(2, N_SHARDS),
                                               in_specs=in_specs, out_specs=out_specs),
        out_shape=[jax.ShapeDtypeStruct((N_SHARDS, g.shape[1] // 2, g.shape[2]), bf16) for g in grads]
        + [jax.ShapeDtypeStruct((g.shape[1] // 2, g.shape[2]), f32) for g in grads],
        compiler_params=_params("parallel", "arbitrary"))(ck_arr, *grads, *recvd)
    return outs[:n], outs[n:]


def rs_sum_chips(owns, recvd, ck_arr, *, name):
    n = len(owns)

    def body(ck_ref, *refs):
        for u in range(n):
            own_ref, r_ref, o_ref = refs[u], refs[n + u], refs[2 * n + u]
            o_ref[0] = ((own_ref[...] + r_ref[0].astype(f32)) + r_ref[1].astype(f32)) + r_ref[2].astype(f32)

    in_specs = [pl.BlockSpec((o.shape[0] // 2, o.shape[1]), lambda r, ck: (r, 0)) for o in owns]
    in_specs += [pl.BlockSpec((3, o.shape[0] // 2, o.shape[1]), lambda r, ck: (0, r, 0)) for o in owns]
    out_specs = [pl.BlockSpec((1, o.shape[0] // 2, o.shape[1]), lambda r, ck: (ck[0], r, 0)) for o in owns]
    return pl.pallas_call(
        body, name=name,
        grid_spec=pltpu.PrefetchScalarGridSpec(num_scalar_prefetch=1, grid=(2,), in_specs=in_specs, out_specs=out_specs),
        out_shape=[jax.ShapeDtypeStruct((2,) + o.shape, f32) for o in owns],
        compiler_params=_params("parallel"))(ck_arr, *owns, *recvd)


SMALL = ("a_norm", "a_v_norm", "a_w_s", "a_b_s", "f_norm", "f_conv_w", "f_conv_b", "kv_norm", "k_norm",
         "b_norm", "b_q_norm", "b_sinks")
BIG = ("a_w_in", "a_w_out", "f_w_in", "f_w_out", "w_kv", "b_w_q", "b_w_o")
PACK_COLS = 1024
PACK_ROWS = 8 * N_STEPS


def _pack(parts, rows=PACK_ROWS):
    flat = jnp.concatenate([p.reshape(-1).astype(f32) for p in parts])
    pad = (-flat.shape[0]) % (rows * PACK_COLS)
    return jnp.pad(flat, (0, pad)).reshape(-1, PACK_COLS)


def _unpack(packed, shapes):
    flat = packed.reshape(-1)
    out, off = [], 0
    for s in shapes:
        size = math.prod(s)
        out.append(flat[off:off + size].reshape(s))
        off += size
    return out


def _ffn_fwd(x, g, h, r, w_in4, conv_w, conv_b, f, tag):
    wg, wu = conv_w[:, :f], conv_w[:, f:]
    bg, bu = conv_b[None, :f], conv_b[None, f:]
    pg, pu, gate, up, a = ffn_in_fused(h, w_in4, wg, wu, bg, bu, name=f"ffn{tag}_in")
    return a, (x, g, h, r, pg, pu, gate, up, a, wg, wu)


def _ffn_bwd(dy, saved, w_in4, w_out, c_arr, tag, exchange=False):
    x, g, h, r, pg, pu, gate, up, a, wg, wu = saved
    f = w_out.shape[0]
    d_w_out = mm_tn(a, [dy], c_arr, name=f"ffn{tag}_dwout", n_s=w_out.shape[1], shard_rows=f // N_SHARDS, tki=f // 2)
    dpg, dpu, sg, su = ffn_gate_bwd(dy, w_out, pg, pu, gate, up, wg, wu, name=f"ffn{tag}_dgate")
    d_w_in = mm_tn(h, [dpg, dpu], c_arr, name=f"ffn{tag}_dwin", n_s=w_in4.shape[2], shard_rows=h.shape[1])
    state = None
    if exchange:
        state, token = sibling_start([d_w_in[1], d_w_out[1]], d_w_in[0], name=f"rs_sibling_start_ffn{tag}")
        g = g + token[0, 0]
    dx, dg = mm_nt_rms_bwd([dpg, dpu], w_in4, x, r, g, dy, name=f"ffn{tag}_dh")
    d_conv_w = jnp.concatenate([sg[0:3], su[0:3]], axis=1)
    d_conv_b = jnp.concatenate([sg[3], su[3]], axis=0)
    return dx, dg, d_w_in, d_conv_w, d_conv_b, d_w_out, state


def _rs_front(pairs, sibling_state, after, c_arr, tag):
    units = [full.reshape(N_SHARDS, -1, full.shape[-1]) for full, _ in pairs]
    from_sib = sibling_wait(sibling_state, after, name=f"rs_sibling_wait{tag}")
    return rs_add_sibling(units, from_sib, c_arr, name=f"rs_add{tag}")


def _rs_back(own, from_chips, c_arr, tag):
    halves = rs_sum_chips(list(own), list(from_chips), c_arr, name=f"rs_sum{tag}")
    return [m.reshape(-1, m.shape[2]) for m in sibling_merge(list(halves), name=f"rs_merge{tag}")]


def kernel(x, a_norm, a_w_in, a_v_norm, a_w_s, a_b_s, a_w_out, f_norm, f_w_in, f_conv_w, f_conv_b, f_w_out, kv_norm, w_kv, k_norm, b_norm, b_w_q, b_q_norm, b_sinks, b_w_o, loss_target, m_a_norm, m_a_w_in, m_a_v_norm, m_a_w_s, m_a_b_s, m_a_w_out, m_f_norm, m_f_w_in, m_f_conv_w, m_f_conv_b, m_f_w_out, m_kv_norm, m_w_kv, m_k_norm, m_b_norm, m_b_w_q, m_b_q_norm, m_b_sinks, m_b_w_o, v_a_norm, v_a_w_in, v_a_v_norm, v_a_w_s, v_a_b_s, v_a_w_out, v_f_norm, v_f_w_in, v_f_conv_w, v_f_conv_b, v_f_w_out, v_kv_norm, v_w_kv, v_k_norm, v_b_norm, v_b_w_q, v_b_q_norm, v_b_sinks, v_b_w_o):
    args = dict(locals())
    weights = {n: args[n] for n in SMALL + BIG}
    moms = {n: args["m_" + n] for n in SMALL + BIG}
    vars_ = {n: args["v_" + n] for n in SMALL + BIG}
    t, d = x.shape[1], x.shape[2]
    xi, yi, ci = _place()
    chip = 2 * xi + yi

    big_local = [a_w_in[0], a_w_out[0], f_w_in, f_w_out, w_kv, b_w_q[0], b_w_o[0]]
    c_arr = jnp.stack([ci, chip]).astype(jnp.int32)
    k_arr = jnp.stack([chip]).astype(jnp.int32)
    b_ain, b_aout, b_fin0, b_fin1, b_fout0, b_fout1, b_kv, b_q, b_o = cast_into_slot(big_local, k_arr, name="cast_weights")
    small_cols = _pack([a_norm, a_v_norm, f_conv_w], rows=8)
    b_small = lax.dynamic_update_slice(jnp.zeros((N_SHARDS,) + small_cols.shape, f32), small_cols[None], (chip, 0, 0))
    g_small, w_a_in, g_a_w_out = gather_shards([b_small, b_ain, b_aout], name="gather_first", split=[False, True, True])
    later, send_sems, recv_sems, token = gather_start([b_fin0, b_fout0, b_kv, b_q, b_o, b_fin1, b_fout1],
                                                      [[0], [1, 2, 3, 4], [5, 6]], g_small, name="gather_start")
    ns_cols = a_norm.shape[1]
    nf_cols = f_conv_w.shape[2]
    parts = [_unpack(g_small[k], [a_norm.shape, a_v_norm.shape, f_conv_w.shape]) for k in range(N_SHARDS)]
    a_norm_f = jnp.concatenate([p[0] for p in parts], axis=1) + token[0, 0]
    a_v_norm_f = jnp.concatenate([p[1] for p in parts], axis=1)
    conv_w_f = jnp.concatenate([p[2] for p in parts], axis=2)

    x0 = x[0]
    tril = jnp.tril(jnp.ones((CHUNK, CHUNK), dtype=bool))
    wc = jnp.where(tril[None], a_w_s[0], 0.0).astype(bf16)
    bt = a_b_s[0].T
    kg2 = jnp.tile(k_norm, 2)[None]
    qg2 = jnp.tile(b_q_norm[0], 2)[None]

    (h_a,), r_a = rms_fwd(x0, [a_norm_f], name="a_norm")
    zu = mm_nn(h_a, w_a_in, name="a_in_u", s0=0, ns=2)
    zv = mm_nn(h_a, w_a_in, name="a_in_v", s0=2, ns=2)
    y_a = sgu_gate_fwd(zu, zv, a_v_norm_f, wc, bt, name="a_gate")
    w_a_out = g_a_w_out.reshape(1, -1, d)
    f = f_w_out.shape[1] * N_SHARDS
    x1, (h_f0,), r_f0 = mm_residual(y_a, w_a_out[0], x0, name="a_out", gains=[f_norm[0:1]])
    (g_fin0,) = forward_halves(gather_wait(later[0:1], send_sems[0], recv_sems[0], x1, name="gather_wait_0"),
                               name="gather_forward_0")
    w_f_in = [g_fin0, None]
    a0, ffn0 = _ffn_fwd(x1, f_norm[0:1], h_f0, r_f0, w_f_in[0], conv_w_f[0], f_conv_b[0], f, "0")
    g_fout0, g_w_kv, g_b_w_q, g_b_w_o = forward_halves(
        gather_wait(later[1:5], send_sems[1], recv_sems[1], a0, name="gather_wait_1"), name="gather_forward_1")
    w_f_out = [g_fout0.reshape(-1, d), None]
    w_kv_f = g_w_kv.reshape(1, d, -1)
    w_q_f = g_b_w_q.reshape(1, d, -1)
    w_o_f = g_b_w_o.reshape(1, -1, d)
    x2, (h_k, h_q), r_b = mm_residual(a0, w_f_out[0], x1, name="ffn0_out", gains=[kv_norm[None], b_norm])
    kv = mm_nn(h_k, w_kv_f, name="kv_proj")
    k2, v2 = kv_post_fwd(kv, kg2, name="kv_post")
    qp = mm_nn(h_q, w_q_f, name="q_proj")
    qn = q_norm_fwd(qp, qg2, name="q_norm", scale=HEAD_DIM ** -0.5)
    o = attn_fwd(qn, k2, v2, b_sinks[0], name="attn")
    x3, (h_f1,), r_f1 = mm_residual(o, w_o_f[0], x2, name="o_proj", gains=[f_norm[1:2]])
    g_fin1, g_fout1 = forward_halves(gather_wait(later[5:7], send_sems[2], recv_sems[2], x3, name="gather_wait_2"),
                                     name="gather_forward_2")
    w_f_in[1] = g_fin1
    w_f_out[1] = g_fout1.reshape(-1, d)
    a1, ffn1 = _ffn_fwd(x3, f_norm[1:2], h_f1, r_f1, w_f_in[1], conv_w_f[1], f_conv_b[1], f, "1")
    dx4, sq = mm_residual(a1, w_f_out[1], x3, name="ffn1_out", target=loss_target[0])
    loss_part = (0.5 * jnp.sum(sq) / d).reshape(1)

    proj_rows = d // N_SHARDS
    dx3, d_fn1, d_fwin1, d_cw1, d_cb1, d_fwout1, _ = _ffn_bwd(dx4, ffn1, w_f_in[1], w_f_out[1], c_arr, "1")
    do = mm_nt([dx3], w_o_f, name="o_proj_dx")
    d_w_o = mm_tn(o, [dx3], c_arr, name="o_proj_dw", n_s=d, shard_rows=o.shape[1] // N_SHARDS)
    dqn, dk2, dv2, dsink = attn_bwd(qn, k2, v2, do, b_sinks[0], name="attn_bwd")
    dqp, dqg = q_norm_bwd(dqn, qp, qg2, name="q_norm_bwd", scale=HEAD_DIM ** -0.5)
    dkv, dkg = kv_post_bwd(dk2, dv2, kv, kg2, name="kv_post_bwd")
    d_w_q = mm_tn(h_q, [dqp], c_arr, name="q_proj_dw", n_s=w_q_f.shape[2], shard_rows=proj_rows)
    d_w_kv = mm_tn(h_k, [dkv], c_arr, name="kv_proj_dw", n_s=w_kv_f.shape[2], shard_rows=proj_rows)
    group1 = [d_fwin1, d_fwout1, d_w_kv, d_w_q, d_w_o]
    sib1, token_s1 = sibling_start([half for _, half in group1], d_w_kv[0], name="rs_sibling_start1")
    dh_k = mm_nt([dkv], w_kv_f, name="kv_proj_dx")
    dx2, d_bn, d_kvn = mm_nt_rms_bwd([dqp], w_q_f, x2, r_b, b_norm + token_s1[0, 0], dx3, name="q_proj_dx",
                                     extra=(dh_k, kv_norm[None]))
    chip_bf1, own1 = _rs_front(group1, sib1, dx2, c_arr, "1")
    scatter1, token1 = chip_scatter_start(list(chip_bf1), None, dx2, name="rs_chips_start1")
    ffn0 = ffn0[:9] + (ffn0[9] + token1[0, 0],) + ffn0[10:]
    dx1, d_fn0, d_fwin0, d_cw0, d_cb0, d_fwout0, sib2 = _ffn_bwd(dx2, ffn0, w_f_in[0], w_f_out[0], c_arr, "0", exchange=True)
    chip_bf2, own2 = _rs_front([d_fwin0, d_fwout0], sib2, dx1, c_arr, "2")
    scatter2, token2 = chip_scatter_start(list(chip_bf2), None, dx1, name="rs_chips_start2")
    a_v_norm_f = a_v_norm_f + token2[0, 0]
    dy_a = mm_nt([dx1], w_a_out, name="a_out_dx")
    d_w_aout = mm_tn(y_a, [dx1], c_arr, name="a_out_dw", n_s=d, shard_rows=y_a.shape[1] // N_SHARDS)
    dzu, dzv, d_avn, d_ws, d_bt = sgu_gate_bwd(zu, zv, dy_a, a_v_norm_f, wc, bt, name="a_gate_bwd")
    d_w_ain = mm_tn(h_a, [dzu, dzv], c_arr, name="a_in_dw", n_s=w_a_in.shape[2], shard_rows=d)
    sib3, token_s3 = sibling_start([d_w_ain[1], d_w_aout[1]], d_w_ain[0], name="rs_sibling_start3")
    dx0, d_an = mm_nt_rms_bwd([dzu, dzv], w_a_in, x0, r_a, a_norm_f + token_s3[0, 0], dx1, name="a_in_dx")
    grad_x = dx0[None]

    chip_bf3, own3 = _rs_front([d_w_ain, d_w_aout], sib3, dx0, c_arr, "3")
    d_fn = jnp.concatenate([d_fn0, d_fn1], axis=0)
    d_cw = jnp.stack([d_cw0, d_cw1])
    d_cb = jnp.stack([d_cb0, d_cb1])
    d_kg = (dkg[0, :HEAD_DIM] + dkg[0, HEAD_DIM:])
    d_qg = (dqg[0, :HEAD_DIM] + dqg[0, HEAD_DIM:])[None]
    small_full = [d_an, d_avn, d_ws[None], d_bt.T[None], d_fn, d_cw, d_cb, d_kvn[0], d_kg, d_bn, d_qg,
                  dsink[:, :N_Q_HEADS], loss_part]
    packed = _pack(small_full)
    me = 4 * xi + 2 * yi + ci
    everyone = lax.dynamic_update_slice(lax.empty((N_DEV,) + packed.shape, f32), packed[None], (me, 0, 0))
    scatter3, token3 = chip_scatter_start(list(chip_bf3), everyone, own3[0], name="rs_chips_start3")
    from_chips1, _ = chip_scatter_wait(scatter1, token3, name="rs_chips_wait1")
    from_chips2, _ = chip_scatter_wait(scatter2, from_chips1[0], name="rs_chips_wait2")
    fin1, fout1, gkv, gq, go, fin0, fout0 = _rs_back(list(own1) + list(own2), list(from_chips1) + list(from_chips2),
                                                     c_arr, "12")
    late = ("f_w_in", "f_w_out", "w_kv", "b_w_q", "b_w_o")
    res_late = adamw([weights[n] for n in late], [[fin0, fin1], [fout0, fout1], [gkv], [gq], [go]],
                     [moms[n] for n in late], [vars_[n] for n in late], name="adamw_late")
    from_chips3, from_all = chip_scatter_wait(scatter3, res_late[1][2], name="rs_chips_wait3")
    ain, aout = _rs_back(own3, from_chips3, c_arr, "3")
    first = ("a_w_in", "a_w_out")
    res_first = adamw([weights[n] for n in first], [[ain], [aout]], [moms[n] for n in first],
                      [vars_[n] for n in first], name="adamw_first")
    big = {n: tuple(r[i] for r in res_late) for i, n in enumerate(late)}
    big.update({n: tuple(r[i] for r in res_first) for i, n in enumerate(first)})

    full_shapes = [g.shape for g in small_full]
    small_g = _unpack(sum_leading(from_all, name="small_sum"), full_shapes)
    loss = small_g.pop()[0]
    small_g[0] = lax.dynamic_slice_in_dim(small_g[0], chip * ns_cols, ns_cols, axis=1)
    small_g[1] = lax.dynamic_slice_in_dim(small_g[1], chip * ns_cols, ns_cols, axis=1)
    small_g[5] = lax.dynamic_slice_in_dim(small_g[5], chip * nf_cols, nf_cols, axis=2)
    small_shapes = [weights[n].shape for n in SMALL]
    small_g = [g.reshape(s) for g, s in zip(small_g, small_shapes)]
    flat2 = [(math.prod(s[:-1]), s[-1]) for s in small_shapes]
    small_d, small_m, small_v = adamw_small(
        *[[a.reshape(s2) for a, s2 in zip(group, flat2)]
          for group in ([weights[n] for n in SMALL], small_g, [moms[n] for n in SMALL], [vars_[n] for n in SMALL])],
        name="adamw_small")
    small_d, small_m, small_v = ([a.reshape(s) for a, s in zip(group, small_shapes)]
                                 for group in (small_d, small_m, small_v))

    out = {}
    for i, n in enumerate(SMALL):
        out[n] = (small_g[i], small_d[i], small_m[i], small_v[i])
    out.update(big)
    order = ["a_norm", "a_w_in", "a_v_norm", "a_w_s", "a_b_s", "a_w_out", "f_norm", "f_w_in", "f_conv_w", "f_conv_b",
             "f_w_out", "kv_norm", "w_kv", "k_norm", "b_norm", "b_w_q", "b_q_norm", "b_sinks", "b_w_o"]
    return (loss, grad_x, *[out[n][0] for n in order], *[out[n][1] for n in order],
            *[out[n][2] for n in order], *[out[n][3] for n in order])
```

```python
import functools
import math

import jax
import jax.numpy as jnp
from jax import lax
from jax.experimental import pallas as pl
from jax.experimental.pallas import tpu as pltpu

f32 = jnp.float32
bf16 = jnp.bfloat16
MESH = pl.DeviceIdType.MESH
ANY = pl.BlockSpec(memory_space=pl.ANY)

EPS = 1e-6
LANES = 128
CHUNK = 128
HEAD_DIM = 64
N_Q_HEADS = 16
N_KV_HEADS = 4
Q_PER_KV = N_Q_HEADS // N_KV_HEADS
N_SHARDS = 4
N_DEV = 8

ADAM_LR = 0.001
ADAM_B1 = 0.9
ADAM_B2 = 0.999
ADAM_EPS = 1e-08
ADAM_WD = 0.01
ADAM_STEP = 10
ADAM_C1 = 1.0 - ADAM_B1 ** ADAM_STEP
ADAM_C2 = 1.0 - ADAM_B2 ** ADAM_STEP

_INV_SQRT2 = 1.0 / math.sqrt(2.0)
_INV_SQRT2PI = 1.0 / math.sqrt(2.0 * math.pi)


def _params(*sem):
    return pltpu.CompilerParams(dimension_semantics=sem)


def _gelu(z):
    return 0.5 * z * (1.0 + lax.erf(z * _INV_SQRT2))


def _gelu_and_grad(z):
    cdf = 0.5 * (1.0 + lax.erf(z * _INV_SQRT2))
    return z * cdf, cdf + z * jnp.exp(-0.5 * z * z) * _INV_SQRT2PI


def _dot(a, b):
    return jnp.dot(a, b, preferred_element_type=f32)


def _dot_nt(a, b):
    return lax.dot_general(a, b, (((1,), (1,)), ((), ())), preferred_element_type=f32)


def _dot_tn(a, b):
    return lax.dot_general(a, b, (((0,), (0,)), ((), ())), preferred_element_type=f32)


def _dot_split(a, b):
    hi = a.astype(bf16)
    lo = (a - hi.astype(f32)).astype(bf16)
    return _dot(hi, b) + _dot(lo, b)


VMEM_TILE_BUDGET = 40 * 1024 * 1024
MAX_ROW_TILE = 2048


def _row_tile(m, fixed_bytes, row_bytes):
    tm = min(m, MAX_ROW_TILE)
    while tm > 256 and 2 * (fixed_bytes + tm * row_bytes) > VMEM_TILE_BUDGET:
        tm //= 2
    return tm


def _isz(a):
    return jnp.dtype(a.dtype).itemsize


def mm_nn(a, w3, *, name, s0=0, ns=None, add=None, out_dtype=f32):
    m, k = a.shape
    s_all, _, n_s = w3.shape
    ns = s_all if ns is None else ns
    tm = _row_tile(m, k * n_s * 2, k * _isz(a) + n_s * jnp.dtype(out_dtype).itemsize + (0 if add is None else n_s * 4))

    def body(*refs):
        if add is None:
            a_ref, w_ref, o_ref = refs
            acc = _dot(a_ref[...].astype(bf16), w_ref[0])
        else:
            a_ref, w_ref, add_ref, o_ref = refs
            acc = _dot(a_ref[...].astype(bf16), w_ref[0]) + add_ref[...]
        o_ref[...] = acc.astype(out_dtype)

    in_specs = [pl.BlockSpec((tm, k), lambda j, i: (i, 0)),
                pl.BlockSpec((1, k, n_s), lambda j, i: (s0 + j, 0, 0))]
    args = [a, w3]
    if add is not None:
        in_specs.append(pl.BlockSpec((tm, n_s), lambda j, i: (i, j)))
        args.append(add)
    return pl.pallas_call(
        body, name=name, grid=(ns, m // tm), in_specs=in_specs,
        out_specs=pl.BlockSpec((tm, n_s), lambda j, i: (i, j)),
        out_shape=jax.ShapeDtypeStruct((m, ns * n_s), out_dtype),
        compiler_params=_params("parallel", "parallel"))(*args)


def mm_nt(a_list, w3, *, name, tko=None, add=None, out_dtype=f32):
    s_all, k_out, n_s = w3.shape
    m = a_list[0].shape[0]
    na = len(a_list)
    spa = s_all // na
    tko = k_out if tko is None else tko
    tm = _row_tile(m, tko * n_s * 2, na * n_s * _isz(a_list[0]) + tko * 4 * (1 if add is None else 2))

    def body(*refs):
        a_refs = refs[:na]
        w_ref = refs[na]
        o_ref = refs[-1]
        s = pl.program_id(2)

        @pl.when(s == 0)
        def _():
            if add is None:
                o_ref[...] = jnp.zeros_like(o_ref)
            else:
                o_ref[...] = refs[na + 1][...]

        for idx in range(na):
            @pl.when(s // spa == idx)
            def _(idx=idx):
                o_ref[...] += _dot_nt(a_refs[idx][...].astype(bf16), w_ref[0])

    def a_map(idx):
        return lambda ko, i, s: (i, jnp.clip(s - idx * spa, 0, spa - 1))

    in_specs = [pl.BlockSpec((tm, n_s), a_map(idx)) for idx in range(na)]
    in_specs.append(pl.BlockSpec((1, tko, n_s), lambda ko, i, s: (s, ko, 0)))
    args = list(a_list) + [w3]
    if add is not None:
        in_specs.append(pl.BlockSpec((tm, tko), lambda ko, i, s: (i, ko)))
        args.append(add)
    return pl.pallas_call(
        body, name=name, grid=(k_out // tko, m // tm, s_all), in_specs=in_specs,
        out_specs=pl.BlockSpec((tm, tko), lambda ko, i, s: (i, ko)),
        out_shape=jax.ShapeDtypeStruct((m, k_out), out_dtype),
        compiler_params=_params("parallel", "parallel", "arbitrary"))(*args)


def mm_tn(a, b_list, c_arr, *, name, n_s, shard_rows, tki=None):
    m, k_in = a.shape
    na = len(b_list)
    s_all = sum(b.shape[1] for b in b_list) // n_s
    spa = s_all // na
    tki = k_in if tki is None else tki
    tm = _row_tile(m, tki * n_s * 4, tki * _isz(a) + na * n_s * _isz(b_list[0]))

    nsteps = m // tm
    per_blk = tki // shard_rows
    half = shard_rows // 2

    def body(c_ref, *refs):
        a_ref = refs[0]
        b_refs = refs[1:1 + na]
        o_ref, ob_ref = refs[-2], refs[-1]
        s = pl.program_id(0)
        r = pl.program_id(2)

        @pl.when(r == 0)
        def _():
            o_ref[...] = jnp.zeros_like(o_ref)

        for idx in range(na):
            @pl.when(s // spa == idx)
            def _(idx=idx):
                o_ref[0] += _dot_tn(a_ref[...].astype(bf16), b_refs[idx][...].astype(bf16))

        @pl.when(r == nsteps - 1)
        def _():
            for q in range(per_blk):
                start = pl.multiple_of(q * shard_rows + (1 - c_ref[0]) * half, 16)
                ob_ref[q] = o_ref[0, pl.ds(start, half), :].astype(bf16)

    def b_map(idx):
        def index(s, ki, r, c_ref):
            active = (s // spa) == idx
            return (jnp.where(active, r, 0), jnp.clip(s - idx * spa, 0, spa - 1))
        return index

    in_specs = [pl.BlockSpec((tm, tki), lambda s, ki, r, c_ref: (r, ki))]
    in_specs += [pl.BlockSpec((tm, n_s), b_map(idx)) for idx in range(na)]
    n_blk = k_in // tki
    return pl.pallas_call(
        body, name=name,
        grid_spec=pltpu.PrefetchScalarGridSpec(
            num_scalar_prefetch=1, grid=(s_all, n_blk, nsteps), in_specs=in_specs,
            out_specs=[pl.BlockSpec((1, tki, n_s), lambda s, ki, r, c_ref: (s, ki, 0)),
                       pl.BlockSpec((per_blk, half, n_s), lambda s, ki, r, c_ref: (s * n_blk + ki, 0, 0))]),
        out_shape=[jax.ShapeDtypeStruct((s_all, k_in, n_s), f32),
                   jax.ShapeDtypeStruct((s_all * k_in // shard_rows, half, n_s), bf16)],
        compiler_params=_params("parallel", "parallel", "arbitrary"))(c_arr, a, *b_list)


def mm_nt_rms_bwd(a_list, w3, x, r, g, dx_in, *, name, extra=None):
    s_all, d, n_s = w3.shape
    m = a_list[0].shape[0]
    na = len(a_list)
    spa = s_all // na
    ne = 0 if extra is None else 1
    tm = _row_tile(m, d * n_s * 2, na * n_s * _isz(a_list[0]) + d * 4 * (4 + ne))

    def body(*refs):
        a_refs, w_ref = refs[:na], refs[na]
        x_ref, r_ref, g_ref, dxin_ref = refs[na + 1:na + 5]
        dh2_ref, g2_ref = (refs[na + 5], refs[na + 6]) if ne else (None, None)
        outs = refs[na + 5 + 2 * ne:]
        dx_ref, dg_ref = outs[0], outs[1]
        dg2_ref = outs[2] if ne else None
        acc_ref = outs[-1]
        i, s = pl.program_id(0), pl.program_id(1)

        @pl.when(s == 0)
        def _():
            acc_ref[...] = jnp.zeros_like(acc_ref)

        for idx in range(na):
            @pl.when(s // spa == idx)
            def _(idx=idx):
                acc_ref[...] += _dot_nt(a_refs[idx][...].astype(bf16), w_ref[0])

        @pl.when(s == s_all - 1)
        def _():
            rv = r_ref[...]
            xh = x_ref[...] * rv
            total = dxin_ref[...]
            pairs = [(acc_ref[...], g_ref, dg_ref)] + ([(dh2_ref[...], g2_ref, dg2_ref)] if ne else [])
            for dh, gain_ref, dgain_ref in pairs:
                part = jnp.sum(dh * xh, axis=0, keepdims=True)

                @pl.when(i == 0)
                def _(dgain_ref=dgain_ref, part=part):
                    dgain_ref[...] = part

                @pl.when(i > 0)
                def _(dgain_ref=dgain_ref, part=part):
                    dgain_ref[...] += part

                tg = dh * gain_ref[...]
                total = total + rv * (tg - xh * jnp.mean(tg * xh, axis=1, keepdims=True))
            dx_ref[...] = total

    def a_map(idx):
        return lambda i, s: (i, jnp.clip(s - idx * spa, 0, spa - 1))

    row = pl.BlockSpec((tm, d), lambda i, s: (i, 0))
    vec = pl.BlockSpec((1, d), lambda i, s: (0, 0))
    in_specs = [pl.BlockSpec((tm, n_s), a_map(idx)) for idx in range(na)]
    in_specs += [pl.BlockSpec((1, d, n_s), lambda i, s: (s, 0, 0)), row, pl.BlockSpec((tm, 1), lambda i, s: (i, 0)), vec, row]
    args = list(a_list) + [w3, x, r, g, dx_in]
    if ne:
        in_specs += [row, vec]
        args += list(extra)
    outs = pl.pallas_call(
        body, name=name, grid=(m // tm, s_all), in_specs=in_specs, out_specs=[row] + [vec] * (1 + ne),
        out_shape=[jax.ShapeDtypeStruct((m, d), f32)] + [jax.ShapeDtypeStruct((1, d), f32)] * (1 + ne),
        scratch_shapes=[pltpu.VMEM((tm, d), f32)],
        compiler_params=_params("arbitrary", "arbitrary"))(*args)
    return outs


def mm_residual(a, w, x, *, name, gains=(), target=None):
    m, k = a.shape
    d = w.shape[1]
    ng = len(gains)
    tm = _row_tile(m, k * d * 2, k * _isz(a) + d * 4 * 3 + ng * d * 2)

    def body(*refs):
        a_ref, w_ref, x_ref = refs[:3]
        y = _dot(a_ref[...].astype(bf16), w_ref[...]) + x_ref[...]
        if target is None:
            g_refs = refs[3:3 + ng]
            y_ref = refs[3 + ng]
            h_refs = refs[4 + ng:4 + 2 * ng]
            r_ref = refs[-1]
            y_ref[...] = y
            r = lax.rsqrt(jnp.mean(y * y, axis=1, keepdims=True) + EPS)
            yh = y * r
            for g_ref, h_ref in zip(g_refs, h_refs):
                h_ref[...] = (yh * g_ref[...]).astype(bf16)
            r_ref[...] = r
        else:
            t_ref, dy_ref, s_ref = refs[3:]
            i = pl.program_id(0)
            e = y - t_ref[...]
            dy_ref[...] = e * (1.0 / d)
            part = jnp.sum(e * e, axis=0, keepdims=True)

            @pl.when(i == 0)
            def _():
                s_ref[...] = part

            @pl.when(i > 0)
            def _():
                s_ref[...] += part

    row = pl.BlockSpec((tm, d), lambda i: (i, 0))
    vec = pl.BlockSpec((1, d), lambda i: (0, 0))
    in_specs = [pl.BlockSpec((tm, k), lambda i: (i, 0)), pl.BlockSpec((k, d), lambda i: (0, 0)), row]
    if target is None:
        outs = pl.pallas_call(
            body, name=name, grid=(m // tm,), in_specs=in_specs + [vec] * ng,
            out_specs=[row] * (1 + ng) + [pl.BlockSpec((tm, 1), lambda i: (i, 0))],
            out_shape=[jax.ShapeDtypeStruct((m, d), f32)] + [jax.ShapeDtypeStruct((m, d), bf16)] * ng
            + [jax.ShapeDtypeStruct((m, 1), f32)],
            compiler_params=_params("parallel"))(a, w, x, *gains)
        return outs[0], outs[1:1 + ng], outs[-1]
    return pl.pallas_call(
        body, name=name, grid=(m // tm,), in_specs=in_specs + [row], out_specs=[row, vec],
        out_shape=[jax.ShapeDtypeStruct((m, d), f32), jax.ShapeDtypeStruct((1, d), f32)],
        compiler_params=_params("arbitrary"))(a, w, x, target)


def rms_fwd(x, gains, *, name, tr=512):
    t, d = x.shape
    tr = min(tr, t)
    ng = len(gains)

    def body(*refs):
        x_ref = refs[0]
        g_refs = refs[1:1 + ng]
        h_refs = refs[1 + ng:1 + 2 * ng]
        r_ref = refs[-1]
        xv = x_ref[...]
        r = lax.rsqrt(jnp.mean(xv * xv, axis=1, keepdims=True) + EPS)
        xh = xv * r
        for g_ref, h_ref in zip(g_refs, h_refs):
            h_ref[...] = (xh * g_ref[...]).astype(bf16)
        r_ref[...] = r

    row = pl.BlockSpec((tr, d), lambda i: (i, 0))
    vec = pl.BlockSpec((1, d), lambda i: (0, 0))
    outs = pl.pallas_call(
        body, name=name, grid=(t // tr,), in_specs=[row] + [vec] * ng,
        out_specs=[row] * ng + [pl.BlockSpec((tr, 1), lambda i: (i, 0))],
        out_shape=[jax.ShapeDtypeStruct((t, d), bf16)] * ng + [jax.ShapeDtypeStruct((t, 1), f32)],
        compiler_params=_params("parallel"))(x, *gains)
    return outs[:ng], outs[ng]


def rms_bwd(dh_list, x, r, gains, dx_in, *, name, tr=512):
    t, d = x.shape
    tr = min(tr, t)
    ng = len(gains)

    def body(*refs):
        dh_refs = refs[:ng]
        x_ref, r_ref = refs[ng], refs[ng + 1]
        g_refs = refs[ng + 2:2 * ng + 2]
        dxin_ref = refs[2 * ng + 2]
        dx_ref = refs[2 * ng + 3]
        dg_refs = refs[2 * ng + 4:]
        i = pl.program_id(0)
        rv = r_ref[...]
        xh = x_ref[...] * rv
        acc = dxin_ref[...]
        for dh_ref, g_ref, dg_ref in zip(dh_refs, g_refs, dg_refs):
            dh = dh_ref[...]
            part = jnp.sum(dh * xh, axis=0, keepdims=True)

            @pl.when(i == 0)
            def _(dg_ref=dg_ref, part=part):
                dg_ref[...] = part

            @pl.when(i > 0)
            def _(dg_ref=dg_ref, part=part):
                dg_ref[...] += part

            tg = dh * g_ref[...]
            acc = acc + rv * (tg - xh * jnp.mean(tg * xh, axis=1, keepdims=True))
        dx_ref[...] = acc

    row = pl.BlockSpec((tr, d), lambda i: (i, 0))
    vec = pl.BlockSpec((1, d), lambda i: (0, 0))
    outs = pl.pallas_call(
        body, name=name, grid=(t // tr,),
        in_specs=[row] * ng + [row, pl.BlockSpec((tr, 1), lambda i: (i, 0))] + [vec] * ng + [row],
        out_specs=[row] + [vec] * ng,
        out_shape=[jax.ShapeDtypeStruct((t, d), f32)] + [jax.ShapeDtypeStruct((1, d), f32)] * ng,
        compiler_params=_params("arbitrary"))(*dh_list, x, r, *gains, dx_in)
    return outs[0], outs[1:]


def sgu_gate_fwd(zu, zv, gv, wc, bt, *, name, tr=512):
    t, w = zu.shape
    tr = min(tr, t)
    groups = w // LANES

    def body(zu_ref, zv_ref, gv_ref, wc_ref, bt_ref, y_ref):
        vp = _gelu(zv_ref[...])
        rv = lax.rsqrt(jnp.mean(vp * vp, axis=1, keepdims=True) + EPS)
        vb = (vp * rv * gv_ref[...]).astype(bf16)
        for c in range(tr // CHUNK):
            rows = slice(c * CHUNK, (c + 1) * CHUNK)
            for g in range(groups):
                cols = slice(g * LANES, (g + 1) * LANES)
                sv = _dot(wc_ref[g], vb[rows, cols]) + bt_ref[:, g:g + 1]
                y_ref[rows, cols] = (_gelu(zu_ref[rows, cols]) * sv).astype(bf16)

    row = pl.BlockSpec((tr, w), lambda i: (i, 0))
    return pl.pallas_call(
        body, name=name, grid=(t // tr,),
        in_specs=[row, row, pl.BlockSpec((1, w), lambda i: (0, 0)),
                  pl.BlockSpec((groups, CHUNK, CHUNK), lambda i: (0, 0, 0)),
                  pl.BlockSpec((CHUNK, groups), lambda i: (0, 0))],
        out_specs=row, out_shape=jax.ShapeDtypeStruct((t, w), bf16),
        compiler_params=_params("parallel"))(zu, zv, gv, wc, bt)


def sgu_gate_bwd(zu, zv, dy, gv, wc, bt, *, name, tr=512):
    t, w = zu.shape
    tr = min(tr, t)
    groups = w // LANES
    nsteps = t // tr

    def body(zu_ref, zv_ref, dy_ref, gv_ref, wc_ref, bt_ref,
             dzu_ref, dzv_ref, dgv_ref, dws_ref, dbt_ref, dv_ref, bacc_ref):
        i = pl.program_id(0)

        @pl.when(i == 0)
        def _():
            dgv_ref[...] = jnp.zeros_like(dgv_ref)
            dws_ref[...] = jnp.zeros_like(dws_ref)
            bacc_ref[...] = jnp.zeros_like(bacc_ref)

        vp, vp_grad = _gelu_and_grad(zv_ref[...])
        rv = lax.rsqrt(jnp.mean(vp * vp, axis=1, keepdims=True) + EPS)
        vhat = vp * rv
        vb = (vhat * gv_ref[...]).astype(bf16)
        for c in range(tr // CHUNK):
            rows = slice(c * CHUNK, (c + 1) * CHUNK)
            for g in range(groups):
                cols = slice(g * LANES, (g + 1) * LANES)
                vblk = vb[rows, cols]
                sv = _dot(wc_ref[g], vblk) + bt_ref[:, g:g + 1]
                zub = zu_ref[rows, cols]
                dyb = dy_ref[rows, cols]
                ub, ub_grad = _gelu_and_grad(zub)
                dzu_ref[rows, cols] = (dyb * sv * ub_grad).astype(bf16)
                dsv = dyb * ub
                bacc_ref[:, cols] += dsv
                dsvb = dsv.astype(bf16)
                dv_ref[rows, cols] = _dot_tn(wc_ref[g], dsvb)
                dws_ref[g] += _dot_nt(dsvb, vblk)
        dv = dv_ref[...]
        dgv_ref[...] += jnp.sum(dv * vhat, axis=0, keepdims=True)
        tg = dv * gv_ref[...]
        dvp = rv * (tg - vhat * jnp.mean(tg * vhat, axis=1, keepdims=True))
        dzv_ref[...] = (dvp * vp_grad).astype(bf16)

        @pl.when(i == nsteps - 1)
        def _():
            tt = lax.broadcasted_iota(jnp.int32, (CHUNK, CHUNK), 0)
            ss = lax.broadcasted_iota(jnp.int32, (CHUNK, CHUNK), 1)
            for g in range(groups):
                dws_ref[g] = jnp.where(ss <= tt, dws_ref[g], 0.0)
                dbt_ref[:, g:g + 1] = jnp.sum(bacc_ref[:, g * LANES:(g + 1) * LANES], axis=1, keepdims=True)

    row = pl.BlockSpec((tr, w), lambda i: (i, 0))
    full3 = pl.BlockSpec((groups, CHUNK, CHUNK), lambda i: (0, 0, 0))
    return pl.pallas_call(
        body, name=name, grid=(nsteps,),
        in_specs=[row, row, row, pl.BlockSpec((1, w), lambda i: (0, 0)), full3,
                  pl.BlockSpec((CHUNK, groups), lambda i: (0, 0))],
        out_specs=[row, row, pl.BlockSpec((1, w), lambda i: (0, 0)), full3,
                   pl.BlockSpec((CHUNK, groups), lambda i: (0, 0))],
        out_shape=[jax.ShapeDtypeStruct((t, w), bf16), jax.ShapeDtypeStruct((t, w), bf16),
                   jax.ShapeDtypeStruct((1, w), f32), jax.ShapeDtypeStruct((groups, CHUNK, CHUNK), f32),
                   jax.ShapeDtypeStruct((CHUNK, groups), f32)],
        scratch_shapes=[pltpu.VMEM((tr, w), f32), pltpu.VMEM((CHUNK, w), f32)],
        compiler_params=_params("arbitrary"))(zu, zv, dy, gv, wc, bt)


HALO = 8


def _shift_down(v, halo, k, first):
    r = pltpu.roll(v, k, 0)
    hh = jnp.where(first, 0.0, pltpu.roll(halo, k, 0))
    rid = lax.broadcasted_iota(jnp.int32, (HALO, v.shape[1]), 0)
    head = jnp.where(rid < k, hh, r[0:HALO])
    if v.shape[0] == HALO:
        return head
    return jnp.concatenate([head, r[HALO:]], axis=0)


def _shift_up(v, halo, k, last):
    n = v.shape[0]
    r = pltpu.roll(v, n - k, 0)
    hh = jnp.where(last, 0.0, pltpu.roll(halo, HALO - k, 0))
    rid = lax.broadcasted_iota(jnp.int32, (HALO, v.shape[1]), 0)
    tail = jnp.where(rid >= HALO - k, hh, r[n - HALO:])
    return jnp.concatenate([r[:n - HALO], tail], axis=0)


def _conv(p, halo, w_ref, b_ref, first):
    return (w_ref[2:3, :] * p + w_ref[1:2, :] * _shift_down(p, halo, 1, first)
            + w_ref[0:1, :] * _shift_down(p, halo, 2, first) + b_ref[...])


BF16_ROWS = 16


def ffn_in_fused(h, w_in4, wg, wu, bg, bu, *, name):
    t, k = h.shape
    s_all, _, n_s = w_in4.shape
    half = s_all // 2
    tm = _row_tile(t, 2 * k * n_s * 2, k * 2 + 4 * n_s * 4 + n_s * 2)

    def body(h_ref, hh_ref, wg_ref, wu_ref, cg_ref, cu_ref, bg_ref, bu_ref, pg_ref, pu_ref, gate_ref, up_ref, a_ref):
        first = pl.program_id(1) == 0
        hv, hh = h_ref[...], hh_ref[...]
        outs = []
        for w_ref, c_ref, b_ref, p_ref, o_ref in ((wg_ref, cg_ref, bg_ref, pg_ref, gate_ref),
                                                  (wu_ref, cu_ref, bu_ref, pu_ref, up_ref)):
            p = _dot(hv, w_ref[0])
            p_ref[...] = p
            hu = _conv(p, _dot(hh, w_ref[0])[BF16_ROWS - HALO:], c_ref, b_ref, first)
            o_ref[...] = hu
            outs.append(hu)
        gate, up = outs
        a_ref[...] = (gate * jax.nn.sigmoid(gate) * up).astype(bf16)

    tile = pl.BlockSpec((tm, n_s), lambda j, i: (i, j))
    cw = pl.BlockSpec((3, n_s), lambda j, i: (0, j))
    cb = pl.BlockSpec((1, n_s), lambda j, i: (0, j))
    f = half * n_s
    return pl.pallas_call(
        body, name=name, grid=(half, t // tm),
        in_specs=[pl.BlockSpec((tm, k), lambda j, i: (i, 0)),
                  pl.BlockSpec((BF16_ROWS, k), lambda j, i: (jnp.maximum(i * (tm // BF16_ROWS) - 1, 0), 0)),
                  pl.BlockSpec((1, k, n_s), lambda j, i: (j, 0, 0)),
                  pl.BlockSpec((1, k, n_s), lambda j, i: (j + half, 0, 0)), cw, cw, cb, cb],
        out_specs=[tile] * 5,
        out_shape=[jax.ShapeDtypeStruct((t, f), f32)] * 4 + [jax.ShapeDtypeStruct((t, f), bf16)],
        compiler_params=_params("parallel", "parallel"))(h, h, w_in4, w_in4, wg, wu, bg, bu)


def _gate_grads(gate, up, dav):
    sg = jax.nn.sigmoid(gate)
    return dav * up * (sg * (1.0 + gate * (1.0 - sg))), dav * gate * sg


GATE_BWD_ROWS = 512


def ffn_gate_bwd(dy, w_out, pg, pu, gate, up, wg, wu, *, name):
    t, f = pg.shape
    d = dy.shape[1]
    tr = min(GATE_BWD_ROWS, t)
    nsteps = t // tr
    tc = f // 2

    def body(dy_ref, dyn_ref, w_ref, pg_ref, pu_ref, gate_ref, gaten_ref, up_ref, upn_ref, wg_ref, wu_ref,
             dg_ref, du_ref, sg_ref, su_ref):
        i = pl.program_id(1)
        last = i == nsteps - 1
        w = w_ref[0]
        da = _dot_nt(dy_ref[...].astype(bf16), w)
        da_n = _dot_nt(dyn_ref[...].astype(bf16), w)
        dgate, dup = _gate_grads(gate_ref[...], up_ref[...], da)
        dgate_n, dup_n = _gate_grads(gaten_ref[...], upn_ref[...], da_n)
        rid = lax.broadcasted_iota(jnp.int32, (8, tc), 0)
        for dd, d_n, c_ref, p_ref, o_ref, s_ref in ((dgate, dgate_n, wg_ref, pg_ref, dg_ref, sg_ref),
                                                    (dup, dup_n, wu_ref, pu_ref, du_ref, su_ref)):
            d1, d2 = _shift_up(dd, d_n, 1, last), _shift_up(dd, d_n, 2, last)
            o_ref[...] = (c_ref[2:3, :] * dd + c_ref[1:2, :] * d1 + c_ref[0:1, :] * d2).astype(bf16)
            p = p_ref[...]
            sums = [jnp.sum(d2 * p, axis=0, keepdims=True), jnp.sum(d1 * p, axis=0, keepdims=True),
                    jnp.sum(dd * p, axis=0, keepdims=True), jnp.sum(dd, axis=0, keepdims=True)]
            part = jnp.zeros((8, tc), f32)
            for k, sk in enumerate(sums):
                part = jnp.where(rid == k, sk, part)

            @pl.when(i == 0)
            def _(s_ref=s_ref, part=part):
                s_ref[...] = part

            @pl.when(i > 0)
            def _(s_ref=s_ref, part=part):
                s_ref[...] += part

    def nxt_rows(j, i):
        return (jnp.minimum((i + 1) * (tr // HALO), t // HALO - 1), j)

    tile = pl.BlockSpec((tr, tc), lambda j, i: (i, j))
    nxt = pl.BlockSpec((HALO, tc), nxt_rows)
    wspec = pl.BlockSpec((3, tc), lambda j, i: (0, j))
    stat = pl.BlockSpec((8, tc), lambda j, i: (0, j))
    return pl.pallas_call(
        body, name=name, grid=(2, nsteps),
        in_specs=[pl.BlockSpec((tr, d), lambda j, i: (i, 0)),
                  pl.BlockSpec((HALO, d), lambda j, i: (nxt_rows(j, i)[0], 0)),
                  pl.BlockSpec((1, tc, d), lambda j, i: (j, 0, 0)),
                  tile, tile, tile, nxt, tile, nxt, wspec, wspec],
        out_specs=[tile, tile, stat, stat],
        out_shape=[jax.ShapeDtypeStruct((t, f), bf16), jax.ShapeDtypeStruct((t, f), bf16),
                   jax.ShapeDtypeStruct((8, f), f32), jax.ShapeDtypeStruct((8, f), f32)],
        compiler_params=_params("parallel", "arbitrary"))(
            dy, dy, w_out.reshape(2, tc, d), pg, pu, gate, gate, up, up, wg, wu)


def _head_mean_matrix():
    i = lax.broadcasted_iota(jnp.int32, (LANES, LANES), 0) // HEAD_DIM
    j = lax.broadcasted_iota(jnp.int32, (LANES, LANES), 1) // HEAD_DIM
    return jnp.where(i == j, 1.0 / HEAD_DIM, 0.0).astype(bf16)


def _lane_half(shape):
    return (lax.broadcasted_iota(jnp.int32, shape, 1) % LANES) // HEAD_DIM


def q_norm_fwd(qp, g2, *, name, scale, tr=512):
    t, w = qp.shape
    tr = min(tr, t)

    def body(x_ref, g_ref, o_ref):
        bd = _head_mean_matrix()
        for cb in range(w // LANES):
            cols = slice(cb * LANES, (cb + 1) * LANES)
            xc = x_ref[:, cols]
            rh = lax.rsqrt(_dot_split(xc * xc, bd) + EPS)
            o_ref[:, cols] = (xc * rh * g_ref[...] * scale).astype(bf16)

    row = pl.BlockSpec((tr, w), lambda i: (i, 0))
    return pl.pallas_call(
        body, name=name, grid=(t // tr,), in_specs=[row, pl.BlockSpec((1, LANES), lambda i: (0, 0))],
        out_specs=row, out_shape=jax.ShapeDtypeStruct((t, w), bf16),
        compiler_params=_params("parallel"))(qp, g2)


def q_norm_bwd(dq, qp, g2, *, name, scale, tr=512):
    t, w = qp.shape
    tr = min(tr, t)

    def body(dq_ref, x_ref, g_ref, o_ref, dg_ref):
        i = pl.program_id(0)
        bd = _head_mean_matrix()
        acc = jnp.zeros((1, LANES), f32)
        for cb in range(w // LANES):
            cols = slice(cb * LANES, (cb + 1) * LANES)
            xc = x_ref[:, cols]
            rh = lax.rsqrt(_dot_split(xc * xc, bd) + EPS)
            xh = xc * rh
            dy = dq_ref[:, cols] * scale
            acc = acc + jnp.sum(dy * xh, axis=0, keepdims=True)
            tg = dy * g_ref[...]
            o_ref[:, cols] = (rh * (tg - xh * _dot_split(tg * xh, bd))).astype(bf16)

        @pl.when(i == 0)
        def _():
            dg_ref[...] = acc

        @pl.when(i > 0)
        def _():
            dg_ref[...] += acc

    row = pl.BlockSpec((tr, w), lambda i: (i, 0))
    vec = pl.BlockSpec((1, LANES), lambda i: (0, 0))
    return pl.pallas_call(
        body, name=name, grid=(t // tr,), in_specs=[row, row, vec], out_specs=[row, vec],
        out_shape=[jax.ShapeDtypeStruct((t, w), bf16), jax.ShapeDtypeStruct((1, LANES), f32)],
        compiler_params=_params("arbitrary"))(dq, qp, g2)


def kv_post_fwd(kv, g2, *, name, tr=512):
    t, w = kv.shape
    tr = min(tr, t)
    kw = w // 2

    def body(x_ref, g_ref, k_ref, v_ref):
        bd = _head_mean_matrix()
        half = _lane_half((tr, LANES))
        for cb in range(kw // LANES):
            xc = x_ref[:, cb * LANES:(cb + 1) * LANES]
            rh = lax.rsqrt(_dot_split(xc * xc, bd) + EPS)
            kn = xc * rh * g_ref[...]
            vc = x_ref[:, kw + cb * LANES:kw + (cb + 1) * LANES]
            for src, dst in ((kn, k_ref), (vc, v_ref)):
                sw = pltpu.roll(src, HEAD_DIM, 1)
                for hf in range(2):
                    blk = 2 * cb + hf
                    dst[:, blk * LANES:(blk + 1) * LANES] = jnp.where(half == hf, src, sw).astype(bf16)

    return pl.pallas_call(
        body, name=name, grid=(t // tr,),
        in_specs=[pl.BlockSpec((tr, w), lambda i: (i, 0)), pl.BlockSpec((1, LANES), lambda i: (0, 0))],
        out_specs=[pl.BlockSpec((tr, 2 * kw), lambda i: (i, 0))] * 2,
        out_shape=[jax.ShapeDtypeStruct((t, 2 * kw), bf16)] * 2,
        compiler_params=_params("parallel"))(kv, g2)


def kv_post_bwd(dk2, dv2, kv, g2, *, name, tr=512):
    t, w = kv.shape
    tr = min(tr, t)
    kw = w // 2

    def body(dk_ref, dv_ref, x_ref, g_ref, o_ref, dg_ref):
        i = pl.program_id(0)
        bd = _head_mean_matrix()
        half = _lane_half((tr, LANES))
        acc = jnp.zeros((1, LANES), f32)

        def fold(ref, cb):
            a = ref[:, (2 * cb) * LANES:(2 * cb + 1) * LANES]
            b = ref[:, (2 * cb + 1) * LANES:(2 * cb + 2) * LANES]
            return jnp.where(half == 0, a + pltpu.roll(a, HEAD_DIM, 1), b + pltpu.roll(b, HEAD_DIM, 1))

        for cb in range(kw // LANES):
            cols = slice(cb * LANES, (cb + 1) * LANES)
            xc = x_ref[:, cols]
            rh = lax.rsqrt(_dot_split(xc * xc, bd) + EPS)
            xh = xc * rh
            dy = fold(dk_ref, cb)
            acc = acc + jnp.sum(dy * xh, axis=0, keepdims=True)
            tg = dy * g_ref[...]
            o_ref[:, cols] = (rh * (tg - xh * _dot_split(tg * xh, bd))).astype(bf16)
            o_ref[:, kw + cb * LANES:kw + (cb + 1) * LANES] = fold(dv_ref, cb).astype(bf16)

        @pl.when(i == 0)
        def _():
            dg_ref[...] = acc

        @pl.when(i > 0)
        def _():
            dg_ref[...] += acc

    dup = pl.BlockSpec((tr, 2 * kw), lambda i: (i, 0))
    row = pl.BlockSpec((tr, w), lambda i: (i, 0))
    vec = pl.BlockSpec((1, LANES), lambda i: (0, 0))
    return pl.pallas_call(
        body, name=name, grid=(t // tr,), in_specs=[dup, dup, row, vec], out_specs=[row, vec],
        out_shape=[jax.ShapeDtypeStruct((t, w), bf16), jax.ShapeDtypeStruct((1, LANES), f32)],
        compiler_params=_params("arbitrary"))(dk2, dv2, kv, g2)


def _slope(h):
    return 2.0 ** (-8.0 * (h + 1) / N_Q_HEADS)


GROUP_ROWS = Q_PER_KV * CHUNK


def _band_mask(n):
    tq = lax.broadcasted_iota(jnp.int32, (GROUP_ROWS, 2 * CHUNK), 0) % CHUNK
    jk = lax.broadcasted_iota(jnp.int32, (GROUP_ROWS, 2 * CHUNK), 1)
    dist = tq + CHUNK - jk
    ok = (dist >= 0) & (dist < CHUNK) & jnp.logical_not((n == 0) & (jk < CHUNK))
    return dist.astype(f32), ok


def _band(ref, n, kh):
    p0 = pl.multiple_of(jnp.maximum(n - 1, 0) * CHUNK, CHUNK)
    c0 = pl.multiple_of(n * CHUNK, CHUNK)
    cols = slice(kh * LANES, (kh + 1) * LANES)
    return jnp.concatenate([ref[pl.ds(p0, CHUNK), cols], ref[pl.ds(c0, CHUNK), cols]], axis=0)


def _stack_heads(ref, kh, half):
    parts = []
    for cb in (2 * kh, 2 * kh + 1):
        xc = ref[:, cb * LANES:(cb + 1) * LANES].astype(f32)
        parts += [jnp.where(half == hf, xc, 0.0).astype(bf16) for hf in range(2)]
    return jnp.concatenate(parts, axis=0)


def _unstack_heads(x4, half):
    return (jnp.where(half == 0, x4[0:CHUNK], x4[CHUNK:2 * CHUNK]),
            jnp.where(half == 0, x4[2 * CHUNK:3 * CHUNK], x4[3 * CHUNK:]))


def _per_head_column(kh, values):
    grp = lax.broadcasted_iota(jnp.int32, (GROUP_ROWS, 1), 0) // CHUNK
    col = jnp.full((GROUP_ROWS, 1), values[0], f32)
    for g in range(1, Q_PER_KV):
        col = jnp.where(grp == g, values[g], col)
    return col


def _softmax_band(q4, kband, dist, ok, slope, sink):
    s = _dot_nt(q4, kband)
    s = jnp.where(ok, s - slope * dist, -jnp.inf)
    m = jnp.maximum(jnp.max(s, axis=1, keepdims=True), sink)
    e = jnp.exp(s - m)
    es = jnp.exp(sink - m)
    den = jnp.sum(e, axis=1, keepdims=True) + es
    return e / den, es / den


def attn_fwd(q, k2, v2, sinks, *, name):
    t, w = q.shape
    nb = t // CHUNK

    def body(sink_ref, q_ref, k_ref, v_ref, o_ref):
        n = pl.program_id(0)
        dist, ok = _band_mask(n)
        half = _lane_half((CHUNK, LANES))
        for kh in range(N_KV_HEADS):
            heads = [Q_PER_KV * kh + g for g in range(Q_PER_KV)]
            slope = _per_head_column(kh, [_slope(h) for h in heads])
            sink = _per_head_column(kh, [sink_ref[h] for h in heads])
            q4 = _stack_heads(q_ref, kh, half)
            p, _ = _softmax_band(q4, _band(k_ref, n, kh), dist, ok, slope, sink)
            o4 = _dot(p.astype(bf16), _band(v_ref, n, kh))
            lo, hi = _unstack_heads(o4, half)
            o_ref[:, (2 * kh) * LANES:(2 * kh + 1) * LANES] = lo.astype(bf16)
            o_ref[:, (2 * kh + 1) * LANES:(2 * kh + 2) * LANES] = hi.astype(bf16)

    full = pl.BlockSpec((t, k2.shape[1]), lambda n: (0, 0))
    return pl.pallas_call(
        body, name=name, grid=(nb,),
        in_specs=[pl.BlockSpec(memory_space=pltpu.SMEM), pl.BlockSpec((CHUNK, w), lambda n: (n, 0)), full, full],
        out_specs=pl.BlockSpec((CHUNK, w), lambda n: (n, 0)),
        out_shape=jax.ShapeDtypeStruct((t, w), bf16),
        compiler_params=_params("parallel"))(sinks, q, k2, v2)


def attn_bwd(q, k2, v2, do, sinks, *, name):
    t, w = q.shape
    nb = t // CHUNK
    kw = k2.shape[1]

    def body(sink_ref, q_ref, k_ref, v_ref, do_ref, dq_ref, dk_ref, dv_ref, ds_ref, kc_ref, vc_ref):
        n = pl.program_id(0)

        @pl.when(n == 0)
        def _():
            ds_ref[...] = jnp.zeros_like(ds_ref)
            kc_ref[...] = jnp.zeros_like(kc_ref)
            vc_ref[...] = jnp.zeros_like(vc_ref)
            dk_ref[...] = jnp.zeros_like(dk_ref)
            dv_ref[...] = jnp.zeros_like(dv_ref)

        @pl.when(n == nb)
        def _():
            dk_ref[...] = kc_ref[...]
            dv_ref[...] = vc_ref[...]

        @pl.when(n < nb)
        def _():
            dist, ok = _band_mask(n)
            half = _lane_half((CHUNK, LANES))
            lane = lax.broadcasted_iota(jnp.int32, (1, LANES), 1)
            sink_acc = jnp.zeros((1, LANES), f32)
            for kh in range(N_KV_HEADS):
                heads = [Q_PER_KV * kh + g for g in range(Q_PER_KV)]
                slope = _per_head_column(kh, [_slope(h) for h in heads])
                sink = _per_head_column(kh, [sink_ref[h] for h in heads])
                q4 = _stack_heads(q_ref, kh, half)
                do4 = _stack_heads(do_ref, kh, half)
                kband = _band(k_ref, n, kh)
                vband = _band(v_ref, n, kh)
                p, ps = _softmax_band(q4, kband, dist, ok, slope, sink)
                dp = _dot_nt(do4, vband)
                delta = jnp.sum(p * dp, axis=1, keepdims=True)
                dsb = (p * (dp - delta)).astype(bf16)
                sd = ps * delta
                for g, h in enumerate(heads):
                    part = jnp.sum(sd[g * CHUNK:(g + 1) * CHUNK], axis=0, keepdims=True)
                    sink_acc = sink_acc + jnp.where(lane == h, -part, 0.0)
                lo, hi = _unstack_heads(_dot(dsb, kband), half)
                dq_ref[:, (2 * kh) * LANES:(2 * kh + 1) * LANES] = lo
                dq_ref[:, (2 * kh + 1) * LANES:(2 * kh + 2) * LANES] = hi
                dkb = _dot_tn(dsb, q4)
                dvb = _dot_tn(p.astype(bf16), do4)
                cols = slice(kh * LANES, (kh + 1) * LANES)
                dk_ref[:, cols] = kc_ref[:, cols] + dkb[0:CHUNK]
                dv_ref[:, cols] = vc_ref[:, cols] + dvb[0:CHUNK]
                kc_ref[:, cols] = dkb[CHUNK:]
                vc_ref[:, cols] = dvb[CHUNK:]
            ds_ref[...] += sink_acc

    full = pl.BlockSpec((t, kw), lambda n: (0, 0))
    qblk = pl.BlockSpec((CHUNK, w), lambda n: (jnp.minimum(n, nb - 1), 0))
    kblk = pl.BlockSpec((CHUNK, kw), lambda n: (jnp.maximum(n - 1, 0), 0))
    return pl.pallas_call(
        body, name=name, grid=(nb + 1,),
        in_specs=[pl.BlockSpec(memory_space=pltpu.SMEM), qblk, full, full, qblk],
        out_specs=[qblk, kblk, kblk, pl.BlockSpec((1, LANES), lambda n: (0, 0))],
        out_shape=[jax.ShapeDtypeStruct((t, w), f32), jax.ShapeDtypeStruct((t, kw), f32),
                   jax.ShapeDtypeStruct((t, kw), f32), jax.ShapeDtypeStruct((1, LANES), f32)],
        scratch_shapes=[pltpu.VMEM((CHUNK, kw), f32), pltpu.VMEM((CHUNK, kw), f32)],
        compiler_params=_params("arbitrary"))(sinks, q, k2, v2, do)


def loss_head(y, target, *, name, tr=512):
    t, d = y.shape
    tr = min(tr, t)

    def body(y_ref, t_ref, dy_ref, s_ref):
        i = pl.program_id(0)
        e = y_ref[...] - t_ref[...]
        dy_ref[...] = e * (1.0 / d)
        part = jnp.sum(e * e, axis=0, keepdims=True)

        @pl.when(i == 0)
        def _():
            s_ref[...] = part

        @pl.when(i > 0)
        def _():
            s_ref[...] += part

    row = pl.BlockSpec((tr, d), lambda i: (i, 0))
    vec = pl.BlockSpec((1, d), lambda i: (0, 0))
    return pl.pallas_call(
        body, name=name, grid=(t // tr,), in_specs=[row, row], out_specs=[row, vec],
        out_shape=[jax.ShapeDtypeStruct((t, d), f32), jax.ShapeDtypeStruct((1, d), f32)],
        compiler_params=_params("arbitrary"))(y, target)


N_STEPS = 8


def _row_blocks(shape):
    if len(shape) == 2:
        r, c = shape
        return (r // N_STEPS, c), (lambda s: (s, 0))
    l, r, c = shape
    per = N_STEPS // l
    return (1, r // per, c), (lambda s: (s // per, s % per, 0))


CAST_STEPS = 4


def cast_into_slot(arrays, k_arr, *, name):
    in_specs, out_specs, out_shape, layers = [], [], [], []
    for a in arrays:
        r, c = a.shape[-2:]
        rb = r // CAST_STEPS
        if a.ndim == 2:
            in_specs.append(pl.BlockSpec((rb, c), lambda s, k: (s, 0)))
            layers.append(None)
        else:
            for l in range(a.shape[0]):
                in_specs.append(pl.BlockSpec((1, rb, c), lambda s, k, l=l: (l, s, 0)))
                layers.append(l)
        for _ in range(1 if a.ndim == 2 else a.shape[0]):
            out_specs.append(pl.BlockSpec((1, rb, c), lambda s, k: (k[0], s, 0)))
            out_shape.append(jax.ShapeDtypeStruct((N_SHARDS, r, c), bf16))
    n = len(in_specs)

    def body(k_ref, *refs):
        for i_ref, o_ref, l in zip(refs[:n], refs[n:], layers):
            o_ref[0] = (i_ref[...] if l is None else i_ref[0]).astype(bf16)

    args = []
    for a in arrays:
        args += [a] * (1 if a.ndim == 2 else a.shape[0])
    return pl.pallas_call(
        body, name=name,
        grid_spec=pltpu.PrefetchScalarGridSpec(num_scalar_prefetch=1, grid=(CAST_STEPS,),
                                               in_specs=in_specs, out_specs=out_specs),
        out_shape=out_shape, compiler_params=_params("parallel"))(k_arr, *args)


def adamw(ws, gs, ms, vs, *, name):
    n = len(ws)
    specs, g_specs, g_count = [], [], []
    for w, g_list in zip(ws, gs):
        blk, index = _row_blocks(w.shape)
        specs.append(pl.BlockSpec(blk, index))
        layers = len(g_list)
        per = N_STEPS // layers
        g_count.append(layers)
        for l in range(layers):
            g_specs.append(pl.BlockSpec(blk[-2:], lambda s, l=l, per=per: (jnp.where(s // per == l, s % per, 0), 0)))
    ng = len(g_specs)

    def body(*refs):
        s = pl.program_id(0)
        g_refs = refs[3 * n:3 * n + ng]
        outs = refs[3 * n + ng:]
        off = 0
        for i in range(n):
            w_ref, m_ref, v_ref = refs[i], refs[n + i], refs[2 * n + i]
            go_ref, d_ref, nm_ref, nv_ref = (outs[k * n + i] for k in range(4))
            layers = g_count[i]
            g = g_refs[off][...]
            for l in range(1, layers):
                g = jnp.where(s // (N_STEPS // layers) == l, g_refs[off + l][...], g)
            off += layers
            g = g.reshape(w_ref.shape)
            m = ADAM_B1 * m_ref[...] + (1.0 - ADAM_B1) * g
            v = ADAM_B2 * v_ref[...] + (1.0 - ADAM_B2) * (g * g)
            m_hat = m / ADAM_C1
            v_hat = v / ADAM_C2
            go_ref[...] = g
            d_ref[...] = -ADAM_LR * (m_hat / (jnp.sqrt(v_hat) + ADAM_EPS) + ADAM_WD * w_ref[...])
            nm_ref[...] = m
            nv_ref[...] = v

    outs = pl.pallas_call(
        body, name=name, grid=(N_STEPS,), in_specs=specs * 3 + g_specs, out_specs=specs * 4,
        out_shape=[jax.ShapeDtypeStruct(a.shape, f32) for a in ws] * 4,
        compiler_params=_params("parallel"))(*ws, *ms, *vs, *[g for g_list in gs for g in g_list])
    return [outs[k * n:(k + 1) * n] for k in range(4)]


def _adamw_update(w, g, m, v):
    m = ADAM_B1 * m + (1.0 - ADAM_B1) * g
    v = ADAM_B2 * v + (1.0 - ADAM_B2) * (g * g)
    m_hat = m / ADAM_C1
    v_hat = v / ADAM_C2
    return -ADAM_LR * (m_hat / (jnp.sqrt(v_hat) + ADAM_EPS) + ADAM_WD * w), m, v


def adamw_small(ws, gs, ms, vs, *, name):
    n = len(ws)

    def body(*refs):
        for i in range(n):
            w_ref, g_ref, m_ref, v_ref = (refs[k * n + i] for k in range(4))
            d_ref, nm_ref, nv_ref = (refs[(4 + k) * n + i] for k in range(3))
            d_ref[...], nm_ref[...], nv_ref[...] = _adamw_update(w_ref[...], g_ref[...], m_ref[...], v_ref[...])

    outs = pl.pallas_call(
        body, name=name, out_shape=[jax.ShapeDtypeStruct(a.shape, f32) for a in ws] * 3)(*ws, *gs, *ms, *vs)
    return outs[:n], outs[n:2 * n], outs[2 * n:]


def _place():
    return lax.axis_index("x"), lax.axis_index("y"), lax.axis_index("c")


def gather_shards(bufs, *, name, split):
    n = len(bufs)

    def body(*refs):
        bufs_ = refs[:n]
        isend, irecv, dsend, drecv = refs[2 * n:]
        x, y, c = _place()
        k = 2 * x + y
        peers = [(1 - x, y, c), (x, 1 - y, c), (1 - x, 1 - y, c)]
        peer_k = [2 * (1 - x) + y, 2 * x + (1 - y), 2 * (1 - x) + (1 - y)]

        def slab(a, q, h):
            if not split[a]:
                return bufs_[a].at[q]
            half = bufs_[a].shape[1] // 2
            return bufs_[a].at[q, pl.ds(pl.multiple_of(h * half, 16), half)]

        def ici(a, j, q):
            return pltpu.make_async_remote_copy(
                src_ref=slab(a, q, c), dst_ref=slab(a, q, c), send_sem=isend.at[3 * a + j], recv_sem=irecv.at[3 * a + j],
                device_id=peers[j], device_id_type=MESH)

        def d2d(a, j, h):
            return pltpu.make_async_remote_copy(
                src_ref=slab(a, peer_k[j], h), dst_ref=slab(a, peer_k[j], h), send_sem=dsend.at[3 * a + j],
                recv_sem=drecv.at[3 * a + j], device_id=(x, y, 1 - c), device_id_type=MESH)

        for a in range(n):
            for j in range(3):
                ici(a, j, k).start()
        for a in range(n):
            for j in range(3):
                ici(a, j, peer_k[j]).wait_recv()
                if split[a]:
                    d2d(a, j, c).start()
        for a in range(n):
            for j in range(3):
                if split[a]:
                    d2d(a, j, 1 - c).wait_recv()
        for a in range(n):
            for j in range(3):
                ici(a, j, k).wait_send()
                if split[a]:
                    d2d(a, j, c).wait_send()

    return pl.pallas_call(
        body, name=name, in_specs=[ANY] * n, out_specs=[ANY] * n,
        out_shape=[jax.ShapeDtypeStruct(b.shape, b.dtype) for b in bufs],
        input_output_aliases={i: i for i in range(n)},
        scratch_shapes=[pltpu.SemaphoreType.DMA((3 * n,))] * 4)(*bufs)


HBM = pl.BlockSpec(memory_space=pltpu.HBM)
SEM = pl.BlockSpec(memory_space=pltpu.SEMAPHORE)
DATAFLOW = pltpu.SideEffectType.DATAFLOW_SIDE_EFFECTING


def _chip_peers():
    x, y, c = _place()
    return 2 * x + y, [(1 - x, y, c), (x, 1 - y, c), (1 - x, 1 - y, c)], [2 * (1 - x) + y, 2 * x + (1 - y), 2 * (1 - x) + (1 - y)]


def _half_slab(ref, q, h):
    half = ref.shape[1] // 2
    return ref.at[q, pl.ds(pl.multiple_of(h * half, BF16_ROWS), half)]


def gather_start(bufs, groups, after, *, name):
    n = len(bufs)
    ng = len(groups)

    def body(*refs):
        ins = refs[:n]
        sends, recvs = refs[2 * n + 1:2 * n + 1 + ng], refs[2 * n + 1 + ng:2 * n + 1 + 2 * ng]
        token = refs[-1]
        c = lax.axis_index("c")
        k, peers, _ = _chip_peers()
        for gi, grp in enumerate(groups):
            for pos, a in enumerate(grp):
                for j in range(3):
                    pltpu.make_async_remote_copy(
                        src_ref=_half_slab(ins[a], k, c), dst_ref=_half_slab(ins[a], k, c), send_sem=sends[gi].at[3 * pos + j],
                        recv_sem=recvs[gi].at[3 * pos + j], device_id=peers[j], device_id_type=MESH).start()
        token[...] = jnp.zeros_like(token)

    sems = [pltpu.SemaphoreType.DMA((3 * len(grp),)) for grp in groups]
    outs = pl.pallas_call(
        body, name=name, in_specs=[HBM] * n + [ANY],
        out_specs=[HBM] * n + [SEM] * (2 * ng) + [pl.BlockSpec(memory_space=pltpu.VMEM)],
        out_shape=[pltpu.HBM(b.shape, b.dtype) for b in bufs] + sems + sems + [jax.ShapeDtypeStruct((8, LANES), f32)],
        input_output_aliases={i: i for i in range(n)},
        compiler_params=pltpu.CompilerParams(has_side_effects=DATAFLOW))(
            *[pltpu.with_memory_space_constraint(b, pltpu.HBM) for b in bufs], after)
    return outs[:n], outs[n:n + ng], outs[n + ng:n + 2 * ng], outs[-1]


def gather_wait(bufs, send_sems, recv_sems, after, *, name):
    n = len(bufs)

    def body(*refs):
        ins = refs[:n]
        send, recv = refs[n], refs[n + 1]
        c = lax.axis_index("c")
        k, peers, peer_k = _chip_peers()
        for a in range(n):
            for j in range(3):
                copy = pltpu.make_async_remote_copy(
                    src_ref=_half_slab(ins[a], k, c), dst_ref=_half_slab(ins[a], peer_k[j], c), send_sem=send.at[3 * a + j],
                    recv_sem=recv.at[3 * a + j], device_id=peers[j], device_id_type=MESH)
                copy.wait_send()
                copy.wait_recv()

    return pl.pallas_call(
        body, name=name, in_specs=[HBM] * n + [SEM, SEM, ANY], out_specs=[HBM] * n,
        out_shape=[pltpu.HBM(b.shape, b.dtype) for b in bufs],
        input_output_aliases={i: i for i in range(n)},
        compiler_params=pltpu.CompilerParams(has_side_effects=DATAFLOW))(*bufs, send_sems, recv_sems, after)


def forward_halves(bufs, *, name):
    n = len(bufs)

    def body(*refs):
        bufs_ = refs[:n]
        send, recv = refs[2 * n:]
        x, y, c = _place()
        _, _, peer_k = _chip_peers()

        def copy(a, j, h):
            return pltpu.make_async_remote_copy(
                src_ref=_half_slab(bufs_[a], peer_k[j], h), dst_ref=_half_slab(bufs_[a], peer_k[j], h),
                send_sem=send.at[3 * a + j], recv_sem=recv.at[3 * a + j], device_id=(x, y, 1 - c), device_id_type=MESH)

        for a in range(n):
            for j in range(3):
                copy(a, j, c).start()
        for a in range(n):
            for j in range(3):
                copy(a, j, 1 - c).wait_recv()
        for a in range(n):
            for j in range(3):
                copy(a, j, c).wait_send()

    return pl.pallas_call(
        body, name=name, in_specs=[ANY] * n, out_specs=[ANY] * n,
        out_shape=[jax.ShapeDtypeStruct(b.shape, b.dtype) for b in bufs],
        input_output_aliases={i: i for i in range(n)},
        scratch_shapes=[pltpu.SemaphoreType.DMA((3 * n,))] * 2)(*bufs)


def _forward_copies(bufs_, send, recv, h):
    x, y, c = _place()
    _, _, peer_k = _chip_peers()
    return [pltpu.make_async_remote_copy(
        src_ref=_half_slab(bufs_[a], peer_k[j], h), dst_ref=_half_slab(bufs_[a], peer_k[j], h),
        send_sem=send.at[3 * a + j], recv_sem=recv.at[3 * a + j], device_id=(x, y, 1 - c), device_id_type=MESH)
        for a in range(len(bufs_)) for j in range(3)]


def forward_start(bufs, after, *, name):
    n = len(bufs)

    def body(*refs):
        for cp in _forward_copies(refs[:n], refs[2 * n + 1], refs[2 * n + 2], lax.axis_index("c")):
            cp.start()
        refs[-1][...] = jnp.zeros_like(refs[-1])

    sems = [pltpu.SemaphoreType.DMA((3 * n,))] * 2
    outs = pl.pallas_call(
        body, name=name, in_specs=[HBM] * n + [ANY],
        out_specs=[HBM] * n + [SEM] * 2 + [pl.BlockSpec(memory_space=pltpu.VMEM)],
        out_shape=[pltpu.HBM(b.shape, b.dtype) for b in bufs] + sems + [jax.ShapeDtypeStruct((8, LANES), f32)],
        input_output_aliases={i: i for i in range(n)},
        compiler_params=pltpu.CompilerParams(has_side_effects=DATAFLOW))(*bufs, after)
    return (n, outs[:-1]), outs[-1]


def forward_wait(state, after, *, name):
    n, held = state

    def body(*refs):
        c = lax.axis_index("c")
        for mine, theirs in zip(_forward_copies(refs[:n], refs[n], refs[n + 1], c),
                                _forward_copies(refs[:n], refs[n], refs[n + 1], 1 - c)):
            mine.wait_send()
            theirs.wait_recv()

    return pl.pallas_call(
        body, name=name, in_specs=[HBM] * n + [SEM] * 2 + [ANY], out_specs=[HBM] * n,
        out_shape=[pltpu.HBM(b.shape, b.dtype) for b in held[:n]],
        input_output_aliases={i: i for i in range(n)},
        compiler_params=pltpu.CompilerParams(has_side_effects=DATAFLOW))(*held, after)


def _sibling_copies(srcs, lands, send, recv):
    x, y, c = _place()
    return [pltpu.make_async_remote_copy(src_ref=srcs[a], dst_ref=lands[a], send_sem=send.at[a], recv_sem=recv.at[a],
                                         device_id=(x, y, 1 - c), device_id_type=MESH) for a in range(len(srcs))]


def sibling_start(arrays, after, *, name):
    n = len(arrays)
    lands = [pltpu.with_memory_space_constraint(lax.empty(a.shape, a.dtype), pltpu.HBM) for a in arrays]

    def body(*refs):
        for cp in _sibling_copies(refs[:n], refs[n:2 * n], refs[4 * n + 1], refs[4 * n + 2]):
            cp.start()
        refs[-1][...] = jnp.zeros_like(refs[-1])

    bufs = list(arrays) + lands
    sems = [pltpu.SemaphoreType.DMA((n,))] * 2
    outs = pl.pallas_call(
        body, name=name, in_specs=[HBM] * (2 * n) + [ANY],
        out_specs=[HBM] * (2 * n) + [SEM] * 2 + [pl.BlockSpec(memory_space=pltpu.VMEM)],
        out_shape=[pltpu.HBM(b.shape, b.dtype) for b in bufs] + sems + [jax.ShapeDtypeStruct((8, LANES), f32)],
        input_output_aliases={i: i for i in range(2 * n)},
        compiler_params=pltpu.CompilerParams(has_side_effects=DATAFLOW))(
            *[pltpu.with_memory_space_constraint(b, pltpu.HBM) for b in bufs], after)
    return (n, outs[:-1]), outs[-1]


def sibling_wait(state, after, *, name):
    n, held = state

    def body(*refs):
        for cp in _sibling_copies(refs[:n], refs[n:2 * n], refs[2 * n], refs[2 * n + 1]):
            cp.wait_send()
            cp.wait_recv()

    outs = pl.pallas_call(
        body, name=name, in_specs=[HBM] * (2 * n) + [SEM] * 2 + [ANY], out_specs=[HBM] * (2 * n),
        out_shape=[pltpu.HBM(b.shape, b.dtype) for b in held[:2 * n]],
        input_output_aliases={i: i for i in range(2 * n)},
        compiler_params=pltpu.CompilerParams(has_side_effects=DATAFLOW))(*held, after)
    return outs[n:]


def sibling_exchange(arrays, *, name):
    n = len(arrays)

    def body(*refs):
        ins, outs = refs[:n], refs[n:2 * n]
        send, recv = refs[2 * n:]
        x, y, c = _place()

        def copy(a):
            return pltpu.make_async_remote_copy(
                src_ref=ins[a], dst_ref=outs[a], send_sem=send.at[a], recv_sem=recv.at[a],
                device_id=(x, y, 1 - c), device_id_type=MESH)

        for a in range(n):
            copy(a).start()
        for a in range(n):
            copy(a).wait_recv()
        for a in range(n):
            copy(a).wait_send()

    return pl.pallas_call(
        body, name=name, in_specs=[ANY] * n, out_specs=[ANY] * n,
        out_shape=[jax.ShapeDtypeStruct(a.shape, a.dtype) for a in arrays],
        scratch_shapes=[pltpu.SemaphoreType.DMA((n,)), pltpu.SemaphoreType.DMA((n,))])(*arrays)


ALL_MASKS = [(mx, my, mc) for mx in (0, 1) for my in (0, 1) for mc in (0, 1)][1:]


def _scatter_copies(srcs, lands, ev, send, recv, esend, erecv):
    x, y, c = _place()
    me = 4 * x + 2 * y + c
    k, peers, peer_k = _chip_peers()
    out = []
    for a in range(len(srcs)):
        for j in range(3):
            out.append(pltpu.make_async_remote_copy(
                src_ref=srcs[a].at[peer_k[j]], dst_ref=lands[a].at[j], send_sem=send.at[3 * a + j],
                recv_sem=recv.at[3 * a + j], device_id=peers[j], device_id_type=MESH))
    start_ev, wait_ev = [], []
    if ev is not None:
        for j, (mx, my, mc) in enumerate(ALL_MASKS):
            peer = (x ^ mx, y ^ my, c ^ mc)
            start_ev.append(pltpu.make_async_remote_copy(
                src_ref=ev.at[me], dst_ref=ev.at[me], send_sem=esend.at[j], recv_sem=erecv.at[j],
                device_id=peer, device_id_type=MESH))
            wait_ev.append(pltpu.make_async_remote_copy(
                src_ref=ev.at[me], dst_ref=ev.at[me ^ (4 * mx + 2 * my + mc)], send_sem=esend.at[j],
                recv_sem=erecv.at[j], device_id=peer, device_id_type=MESH))
    return out, start_ev, wait_ev


def chip_scatter_start(arrays, everyone, after, *, name):
    n = len(arrays)
    ne = 0 if everyone is None else 1
    lands = [pltpu.with_memory_space_constraint(lax.empty((3,) + a.shape[1:], a.dtype), pltpu.HBM) for a in arrays]

    def body(*refs):
        srcs, lands_ = refs[:n], refs[n:2 * n]
        ev = refs[2 * n] if ne else None
        sems = refs[2 * n + ne + 1 + 2 * n + ne:-1]
        send, recv = sems[0], sems[1]
        esend, erecv = (sems[2], sems[3]) if ne else (None, None)
        copies, start_ev, _ = _scatter_copies(srcs, lands_, ev, send, recv, esend, erecv)
        for cp in start_ev + copies:
            cp.start()
        refs[-1][...] = jnp.zeros_like(refs[-1])

    sem_shapes = [pltpu.SemaphoreType.DMA((3 * n,))] * 2 + [pltpu.SemaphoreType.DMA((7,))] * (2 * ne)
    bufs = list(arrays) + lands + ([everyone] if ne else [])
    outs = pl.pallas_call(
        body, name=name, in_specs=[HBM] * len(bufs) + [ANY],
        out_specs=[HBM] * len(bufs) + [SEM] * len(sem_shapes) + [pl.BlockSpec(memory_space=pltpu.VMEM)],
        out_shape=[pltpu.HBM(b.shape, b.dtype) for b in bufs] + sem_shapes + [jax.ShapeDtypeStruct((8, LANES), f32)],
        input_output_aliases={i: i for i in range(len(bufs))},
        compiler_params=pltpu.CompilerParams(has_side_effects=DATAFLOW))(
            *[pltpu.with_memory_space_constraint(b, pltpu.HBM) for b in bufs], after)
    return (n, ne, outs[:-1]), outs[-1]


def chip_scatter_wait(state, after, *, name):
    n, ne, held = state
    nb = 2 * n + ne
    bufs, sems = held[:nb], held[nb:]

    def body(*refs):
        srcs, lands_ = refs[:n], refs[n:2 * n]
        ev = refs[2 * n] if ne else None
        sems_ = refs[nb:nb + len(sems)]
        esend, erecv = (sems_[2], sems_[3]) if ne else (None, None)
        copies, _, wait_ev = _scatter_copies(srcs, lands_, ev, sems_[0], sems_[1], esend, erecv)
        for cp in wait_ev + copies:
            cp.wait_send()
            cp.wait_recv()

    outs = pl.pallas_call(
        body, name=name, in_specs=[HBM] * nb + [SEM] * len(sems) + [ANY], out_specs=[HBM] * nb,
        out_shape=[pltpu.HBM(b.shape, b.dtype) for b in bufs],
        input_output_aliases={i: i for i in range(nb)},
        compiler_params=pltpu.CompilerParams(has_side_effects=DATAFLOW))(*bufs, *sems, after)
    return outs[n:2 * n], (outs[2 * n] if ne else None)


def sibling_merge(bufs, *, name):
    n = len(bufs)

    def body(*refs):
        bufs_ = refs[:n]
        send, recv = refs[2 * n:]
        x, y, c = _place()

        def copy(u, h):
            return pltpu.make_async_remote_copy(
                src_ref=bufs_[u].at[h], dst_ref=bufs_[u].at[h], send_sem=send.at[u], recv_sem=recv.at[u],
                device_id=(x, y, 1 - c), device_id_type=MESH)

        for u in range(n):
            copy(u, c).start()
        for u in range(n):
            copy(u, 1 - c).wait_recv()
        for u in range(n):
            copy(u, c).wait_send()

    return pl.pallas_call(
        body, name=name, in_specs=[ANY] * n, out_specs=[ANY] * n,
        out_shape=[jax.ShapeDtypeStruct(b.shape, b.dtype) for b in bufs],
        input_output_aliases={i: i for i in range(n)},
        scratch_shapes=[pltpu.SemaphoreType.DMA((n,)), pltpu.SemaphoreType.DMA((n,))])(*bufs)


def sum_leading(a, *, name):
    n, r, c = a.shape

    def body(a_ref, o_ref):
        acc = a_ref[0]
        for i in range(1, n):
            acc = acc + a_ref[i]
        o_ref[...] = acc

    rb = r // 2 if r % 16 == 0 else r
    return pl.pallas_call(
        body, name=name, grid=(r // rb,), in_specs=[pl.BlockSpec((n, rb, c), lambda i: (0, i, 0))],
        out_specs=pl.BlockSpec((rb, c), lambda i: (i, 0)), out_shape=jax.ShapeDtypeStruct((r, c), f32),
        compiler_params=_params("parallel"))(a)


def _half_rows(shape):
    return shape[1] // 2 // 2


def rs_cast_other_half(grads, c_arr, *, name):
    n = len(grads)

    def body(c_ref, *refs):
        for i_ref, o_ref in zip(refs[:n], refs[n:]):
            o_ref[...] = i_ref[...].astype(bf16)

    in_specs = [pl.BlockSpec((1, _half_rows(g.shape), g.shape[2]), lambda s, r, c_ref: (s, (1 - c_ref[0]) * 2 + r, 0))
                for g in grads]
    out_specs = [pl.BlockSpec((1, _half_rows(g.shape), g.shape[2]), lambda s, r, c_ref: (s, r, 0)) for g in grads]
    return pl.pallas_call(
        body, name=name,
        grid_spec=pltpu.PrefetchScalarGridSpec(num_scalar_prefetch=1, grid=(N_SHARDS, 2),
                                               in_specs=in_specs, out_specs=out_specs),
        out_shape=[jax.ShapeDtypeStruct((N_SHARDS, g.shape[1] // 2, g.shape[2]), bf16) for g in grads],
        compiler_params=_params("parallel", "parallel"))(c_arr, *grads)


def rs_add_sibling(grads, recvd, ck_arr, *, name):
    n = len(grads)

    def body(ck_ref, *refs):
        s = pl.program_id(1)
        for u in range(n):
            g_ref, r_ref = refs[u], refs[n + u]
            qb_ref, own_ref = refs[2 * n + u], refs[3 * n + u]
            q = g_ref[0] + r_ref[0].astype(f32)
            qb_ref[0] = q.astype(bf16)

            @pl.when(s == ck_ref[1])
            def _(own_ref=own_ref, q=q):
                own_ref[...] = q

    in_specs = [pl.BlockSpec((1, _half_rows(g.shape), g.shape[2]), lambda r, s, ck: (s, ck[0] * 2 + r, 0)) for g in grads]
    in_specs += [pl.BlockSpec((1, _half_rows(g.shape), g.shape[2]), lambda r, s, ck: (s, r, 0)) for g in grads]
    out_specs = [pl.BlockSpec((1, _half_rows(g.shape), g.shape[2]), lambda r, s, ck: (s, r, 0)) for g in grads]
    out_specs += [pl.BlockSpec((_half_rows(g.shape), g.shape[2]), lambda r, s, ck: (r, 0)) for g in grads]
    outs = pl.pallas_call(
        body, name=name,
        grid_spec=pltpu.PrefetchScalarGridSpec(num_scalar_prefetch=1, grid=(2, N_SHARDS),
                                               in_specs=in_specs, out_specs=out_specs),
        out_shape=[jax.ShapeDtypeStruct((N_SHARDS, g.shape[1] // 2, g.shape[2]), bf16) for g in grads]
        + [jax.ShapeDtypeStruct((g.shape[1] // 2, g.shape[2]), f32) for g in grads],
        compiler_params=_params("parallel", "arbitrary"))(ck_arr, *grads, *recvd)
    return outs[:n], outs[n:]


def rs_sum_chips(owns, recvd, ck_arr, *, name):
    n = len(owns)

    def body(ck_ref, *refs):
        for u in range(n):
            own_ref, r_ref, o_ref = refs[u], refs[n + u], refs[2 * n + u]
            o_ref[0] = ((own_ref[...] + r_ref[0].astype(f32)) + r_ref[1].astype(f32)) + r_ref[2].astype(f32)

    in_specs = [pl.BlockSpec((o.shape[0] // 2, o.shape[1]), lambda r, ck: (r, 0)) for o in owns]
    in_specs += [pl.BlockSpec((3, o.shape[0] // 2, o.shape[1]), lambda r, ck: (0, r, 0)) for o in owns]
    out_specs = [pl.BlockSpec((1, o.shape[0] // 2, o.shape[1]), lambda r, ck: (ck[0], r, 0)) for o in owns]
    return pl.pallas_call(
        body, name=name,
        grid_spec=pltpu.PrefetchScalarGridSpec(num_scalar_prefetch=1, grid=(2,), in_specs=in_specs, out_specs=out_specs),
        out_shape=[jax.ShapeDtypeStruct((2,) + o.shape, f32) for o in owns],
        compiler_params=_params("parallel"))(ck_arr, *owns, *recvd)


SMALL = ("a_norm", "a_v_norm", "a_w_s", "a_b_s", "f_norm", "f_conv_w", "f_conv_b", "kv_norm", "k_norm",
         "b_norm", "b_q_norm", "b_sinks")
BIG = ("a_w_in", "a_w_out", "f_w_in", "f_w_out", "w_kv", "b_w_q", "b_w_o")
PACK_COLS = 1024
PACK_ROWS = 8 * N_STEPS


def _pack(parts, rows=PACK_ROWS):
    flat = jnp.concatenate([p.reshape(-1).astype(f32) for p in parts])
    pad = (-flat.shape[0]) % (rows * PACK_COLS)
    return jnp.pad(flat, (0, pad)).reshape(-1, PACK_COLS)


def _unpack(packed, shapes):
    flat = packed.reshape(-1)
    out, off = [], 0
    for s in shapes:
        size = math.prod(s)
        out.append(flat[off:off + size].reshape(s))
        off += size
    return out


def _ffn_fwd(x, g, h, r, w_in4, conv_w, conv_b, f, tag):
    wg, wu = conv_w[:, :f], conv_w[:, f:]
    bg, bu = conv_b[None, :f], conv_b[None, f:]
    pg, pu, gate, up, a = ffn_in_fused(h, w_in4, wg, wu, bg, bu, name=f"ffn{tag}_in")
    return a, (x, g, h, r, pg, pu, gate, up, a, wg, wu)


def _ffn_bwd(dy, saved, w_in4, w_out, c_arr, tag, exchange=False):
    x, g, h, r, pg, pu, gate, up, a, wg, wu = saved
    f = w_out.shape[0]
    d_w_out = mm_tn(a, [dy], c_arr, name=f"ffn{tag}_dwout", n_s=w_out.shape[1], shard_rows=f // N_SHARDS, tki=f // 2)
    dpg, dpu, sg, su = ffn_gate_bwd(dy, w_out, pg, pu, gate, up, wg, wu, name=f"ffn{tag}_dgate")
    d_w_in = mm_tn(h, [dpg, dpu], c_arr, name=f"ffn{tag}_dwin", n_s=w_in4.shape[2], shard_rows=h.shape[1])
    state = None
    if exchange:
        state, token = sibling_start([d_w_in[1], d_w_out[1]], d_w_in[0], name=f"rs_sibling_start_ffn{tag}")
        g = g + token[0, 0]
    dx, dg = mm_nt_rms_bwd([dpg, dpu], w_in4, x, r, g, dy, name=f"ffn{tag}_dh")
    d_conv_w = jnp.concatenate([sg[0:3], su[0:3]], axis=1)
    d_conv_b = jnp.concatenate([sg[3], su[3]], axis=0)
    return dx, dg, d_w_in, d_conv_w, d_conv_b, d_w_out, state


def _rs_front(pairs, sibling_state, after, c_arr, tag):
    units = [full.reshape(N_SHARDS, -1, full.shape[-1]) for full, _ in pairs]
    from_sib = sibling_wait(sibling_state, after, name=f"rs_sibling_wait{tag}")
    return rs_add_sibling(units, from_sib, c_arr, name=f"rs_add{tag}")


def _rs_back(own, from_chips, c_arr, tag):
    halves = rs_sum_chips(list(own), list(from_chips), c_arr, name=f"rs_sum{tag}")
    return [m.reshape(-1, m.shape[2]) for m in sibling_merge(list(halves), name=f"rs_merge{tag}")]


def kernel(x, a_norm, a_w_in, a_v_norm, a_w_s, a_b_s, a_w_out, f_norm, f_w_in, f_conv_w, f_conv_b, f_w_out, kv_norm, w_kv, k_norm, b_norm, b_w_q, b_q_norm, b_sinks, b_w_o, loss_target, m_a_norm, m_a_w_in, m_a_v_norm, m_a_w_s, m_a_b_s, m_a_w_out, m_f_norm, m_f_w_in, m_f_conv_w, m_f_conv_b, m_f_w_out, m_kv_norm, m_w_kv, m_k_norm, m_b_norm, m_b_w_q, m_b_q_norm, m_b_sinks, m_b_w_o, v_a_norm, v_a_w_in, v_a_v_norm, v_a_w_s, v_a_b_s, v_a_w_out, v_f_norm, v_f_w_in, v_f_conv_w, v_f_conv_b, v_f_w_out, v_kv_norm, v_w_kv, v_k_norm, v_b_norm, v_b_w_q, v_b_q_norm, v_b_sinks, v_b_w_o):
    args = dict(locals())
    weights = {n: args[n] for n in SMALL + BIG}
    moms = {n: args["m_" + n] for n in SMALL + BIG}
    vars_ = {n: args["v_" + n] for n in SMALL + BIG}
    t, d = x.shape[1], x.shape[2]
    xi, yi, ci = _place()
    chip = 2 * xi + yi

    big_local = [a_w_in[0], a_w_out[0], f_w_in, f_w_out, w_kv, b_w_q[0], b_w_o[0]]
    c_arr = jnp.stack([ci, chip]).astype(jnp.int32)
    k_arr = jnp.stack([chip]).astype(jnp.int32)
    b_ain, b_aout, b_fin0, b_fin1, b_fout0, b_fout1, b_kv, b_q, b_o = cast_into_slot(big_local, k_arr, name="cast_weights")
    small_cols = _pack([a_norm, a_v_norm, f_conv_w], rows=8)
    b_small = lax.dynamic_update_slice(jnp.zeros((N_SHARDS,) + small_cols.shape, f32), small_cols[None], (chip, 0, 0))
    g_small, w_a_in, g_a_w_out = gather_shards([b_small, b_ain, b_aout], name="gather_first", split=[False, True, True])
    later, send_sems, recv_sems, token = gather_start([b_fin0, b_fout0, b_kv, b_q, b_o, b_fin1, b_fout1],
                                                      [[0], [1, 2, 3, 4], [5, 6]], g_small, name="gather_start")
    ns_cols = a_norm.shape[1]
    nf_cols = f_conv_w.shape[2]
    parts = [_unpack(g_small[k], [a_norm.shape, a_v_norm.shape, f_conv_w.shape]) for k in range(N_SHARDS)]
    a_norm_f = jnp.concatenate([p[0] for p in parts], axis=1) + token[0, 0]
    a_v_norm_f = jnp.concatenate([p[1] for p in parts], axis=1)
    conv_w_f = jnp.concatenate([p[2] for p in parts], axis=2)

    x0 = x[0]
    tril = jnp.tril(jnp.ones((CHUNK, CHUNK), dtype=bool))
    wc = jnp.where(tril[None], a_w_s[0], 0.0).astype(bf16)
    bt = a_b_s[0].T
    kg2 = jnp.tile(k_norm, 2)[None]
    qg2 = jnp.tile(b_q_norm[0], 2)[None]

    (h_a,), r_a = rms_fwd(x0, [a_norm_f], name="a_norm")
    zu = mm_nn(h_a, w_a_in, name="a_in_u", s0=0, ns=2)
    zv = mm_nn(h_a, w_a_in, name="a_in_v", s0=2, ns=2)
    y_a = sgu_gate_fwd(zu, zv, a_v_norm_f, wc, bt, name="a_gate")
    w_a_out = g_a_w_out.reshape(1, -1, d)
    f = f_w_out.shape[1] * N_SHARDS
    fwd0, tok0 = forward_start(gather_wait(later[0:1], send_sems[0], recv_sems[0], y_a, name="gather_wait_0"), y_a,
                               name="gather_forward_start_0")
    x1, (h_f0,), r_f0 = mm_residual(y_a, w_a_out[0], x0, name="a_out", gains=[f_norm[0:1] + tok0[0, 0]])
    (g_fin0,) = forward_wait(fwd0, x1, name="gather_forward_wait_0")
    w_f_in = [g_fin0, None]
    fwd1, tok1 = forward_start(gather_wait(later[1:5], send_sems[1], recv_sems[1], x1, name="gather_wait_1"), x1,
                               name="gather_forward_start_1")
    a0, ffn0 = _ffn_fwd(x1, f_norm[0:1], h_f0, r_f0, w_f_in[0], conv_w_f[0], f_conv_b[0] + tok1[0, 0], f, "0")
    g_fout0, g_w_kv, g_b_w_q, g_b_w_o = forward_wait(fwd1, a0, name="gather_forward_wait_1")
    w_f_out = [g_fout0.reshape(-1, d), None]
    w_kv_f = g_w_kv.reshape(1, d, -1)
    w_q_f = g_b_w_q.reshape(1, d, -1)
    w_o_f = g_b_w_o.reshape(1, -1, d)
    x2, (h_k, h_q), r_b = mm_residual(a0, w_f_out[0], x1, name="ffn0_out", gains=[kv_norm[None], b_norm])
    kv = mm_nn(h_k, w_kv_f, name="kv_proj")
    k2, v2 = kv_post_fwd(kv, kg2, name="kv_post")
    qp = mm_nn(h_q, w_q_f, name="q_proj")
    qn = q_norm_fwd(qp, qg2, name="q_norm", scale=HEAD_DIM ** -0.5)
    fwd2, tok2 = forward_start(gather_wait(later[5:7], send_sems[2], recv_sems[2], qn, name="gather_wait_2"), qn,
                               name="gather_forward_start_2")
    o = attn_fwd(qn, k2, v2, b_sinks[0] + tok2[0, 0], name="attn")
    x3, (h_f1,), r_f1 = mm_residual(o, w_o_f[0], x2, name="o_proj", gains=[f_norm[1:2]])
    g_fin1, g_fout1 = forward_wait(fwd2, x3, name="gather_forward_wait_2")
    w_f_in[1] = g_fin1
    w_f_out[1] = g_fout1.reshape(-1, d)
    a1, ffn1 = _ffn_fwd(x3, f_norm[1:2], h_f1, r_f1, w_f_in[1], conv_w_f[1], f_conv_b[1], f, "1")
    dx4, sq = mm_residual(a1, w_f_out[1], x3, name="ffn1_out", target=loss_target[0])
    loss_part = (0.5 * jnp.sum(sq) / d).reshape(1)

    proj_rows = d // N_SHARDS
    dx3, d_fn1, d_fwin1, d_cw1, d_cb1, d_fwout1, _ = _ffn_bwd(dx4, ffn1, w_f_in[1], w_f_out[1], c_arr, "1")
    do = mm_nt([dx3], w_o_f, name="o_proj_dx")
    d_w_o = mm_tn(o, [dx3], c_arr, name="o_proj_dw", n_s=d, shard_rows=o.shape[1] // N_SHARDS)
    dqn, dk2, dv2, dsink = attn_bwd(qn, k2, v2, do, b_sinks[0], name="attn_bwd")
    dqp, dqg = q_norm_bwd(dqn, qp, qg2, name="q_norm_bwd", scale=HEAD_DIM ** -0.5)
    dkv, dkg = kv_post_bwd(dk2, dv2, kv, kg2, name="kv_post_bwd")
    d_w_q = mm_tn(h_q, [dqp], c_arr, name="q_proj_dw", n_s=w_q_f.shape[2], shard_rows=proj_rows)
    d_w_kv = mm_tn(h_k, [dkv], c_arr, name="kv_proj_dw", n_s=w_kv_f.shape[2], shard_rows=proj_rows)
    group1 = [d_fwin1, d_fwout1, d_w_kv, d_w_q, d_w_o]
    sib1, token_s1 = sibling_start([half for _, half in group1], d_w_kv[0], name="rs_sibling_start1")
    dh_k = mm_nt([dkv], w_kv_f, name="kv_proj_dx")
    dx2, d_bn, d_kvn = mm_nt_rms_bwd([dqp], w_q_f, x2, r_b, b_norm + token_s1[0, 0], dx3, name="q_proj_dx",
                                     extra=(dh_k, kv_norm[None]))
    chip_bf1, own1 = _rs_front(group1, sib1, dx2, c_arr, "1")
    scatter1, token1 = chip_scatter_start(list(chip_bf1), None, dx2, name="rs_chips_start1")
    ffn0 = ffn0[:9] + (ffn0[9] + token1[0, 0],) + ffn0[10:]
    dx1, d_fn0, d_fwin0, d_cw0, d_cb0, d_fwout0, sib2 = _ffn_bwd(dx2, ffn0, w_f_in[0], w_f_out[0], c_arr, "0", exchange=True)
    chip_bf2, own2 = _rs_front([d_fwin0, d_fwout0], sib2, dx1, c_arr, "2")
    scatter2, token2 = chip_scatter_start(list(chip_bf2), None, dx1, name="rs_chips_start2")
    a_v_norm_f = a_v_norm_f + token2[0, 0]
    dy_a = mm_nt([dx1], w_a_out, name="a_out_dx")
    d_w_aout = mm_tn(y_a, [dx1], c_arr, name="a_out_dw", n_s=d, shard_rows=y_a.shape[1] // N_SHARDS)
    dzu, dzv, d_avn, d_ws, d_bt = sgu_gate_bwd(zu, zv, dy_a, a_v_norm_f, wc, bt, name="a_gate_bwd")
    d_w_ain = mm_tn(h_a, [dzu, dzv], c_arr, name="a_in_dw", n_s=w_a_in.shape[2], shard_rows=d)
    sib3, token_s3 = sibling_start([d_w_ain[1], d_w_aout[1]], d_w_ain[0], name="rs_sibling_start3")
    dx0, d_an = mm_nt_rms_bwd([dzu, dzv], w_a_in, x0, r_a, a_norm_f + token_s3[0, 0], dx1, name="a_in_dx")
    grad_x = dx0[None]

    chip_bf3, own3 = _rs_front([d_w_ain, d_w_aout], sib3, dx0, c_arr, "3")
    d_fn = jnp.concatenate([d_fn0, d_fn1], axis=0)
    d_cw = jnp.stack([d_cw0, d_cw1])
    d_cb = jnp.stack([d_cb0, d_cb1])
    d_kg = (dkg[0, :HEAD_DIM] + dkg[0, HEAD_DIM:])
    d_qg = (dqg[0, :HEAD_DIM] + dqg[0, HEAD_DIM:])[None]
    small_full = [d_an, d_avn, d_ws[None], d_bt.T[None], d_fn, d_cw, d_cb, d_kvn[0], d_kg, d_bn, d_qg,
                  dsink[:, :N_Q_HEADS], loss_part]
    packed = _pack(small_full)
    me = 4 * xi + 2 * yi + ci
    everyone = lax.dynamic_update_slice(lax.empty((N_DEV,) + packed.shape, f32), packed[None], (me, 0, 0))
    scatter3, token3 = chip_scatter_start(list(chip_bf3), everyone, own3[0], name="rs_chips_start3")
    from_chips1, _ = chip_scatter_wait(scatter1, token3, name="rs_chips_wait1")
    from_chips2, _ = chip_scatter_wait(scatter2, from_chips1[0], name="rs_chips_wait2")
    fin1, fout1, gkv, gq, go, fin0, fout0 = _rs_back(list(own1) + list(own2), list(from_chips1) + list(from_chips2),
                                                     c_arr, "12")
    late = ("f_w_in", "f_w_out", "w_kv", "b_w_q", "b_w_o")
    res_late = adamw([weights[n] for n in late], [[fin0, fin1], [fout0, fout1], [gkv], [gq], [go]],
                     [moms[n] for n in late], [vars_[n] for n in late], name="adamw_late")
    from_chips3, from_all = chip_scatter_wait(scatter3, res_late[1][2], name="rs_chips_wait3")
    ain, aout = _rs_back(own3, from_chips3, c_arr, "3")
    first = ("a_w_in", "a_w_out")
    res_first = adamw([weights[n] for n in first], [[ain], [aout]], [moms[n] for n in first],
                      [vars_[n] for n in first], name="adamw_first")
    big = {n: tuple(r[i] for r in res_late) for i, n in enumerate(late)}
    big.update({n: tuple(r[i] for r in res_first) for i, n in enumerate(first)})

    full_shapes = [g.shape for g in small_full]
    small_g = _unpack(sum_leading(from_all, name="small_sum"), full_shapes)
    loss = small_g.pop()[0]
    small_g[0] = lax.dynamic_slice_in_dim(small_g[0], chip * ns_cols, ns_cols, axis=1)
    small_g[1] = lax.dynamic_slice_in_dim(small_g[1], chip * ns_cols, ns_cols, axis=1)
    small_g[5] = lax.dynamic_slice_in_dim(small_g[5], chip * nf_cols, nf_cols, axis=2)
    small_shapes = [weights[n].shape for n in SMALL]
    small_g = [g.reshape(s) for g, s in zip(small_g, small_shapes)]
    flat2 = [(math.prod(s[:-1]), s[-1]) for s in small_shapes]
    small_d, small_m, small_v = adamw_small(
        *[[a.reshape(s2) for a, s2 in zip(group, flat2)]
          for group in ([weights[n] for n in SMALL], small_g, [moms[n] for n in SMALL], [vars_[n] for n in SMALL])],
        name="adamw_small")
    small_d, small_m, small_v = ([a.reshape(s) for a, s in zip(group, small_shapes)]
                                 for group in (small_d, small_m, small_v))

    out = {}
    for i, n in enumerate(SMALL):
        out[n] = (small_g[i], small_d[i], small_m[i], small_v[i])
    out.update(big)
    order = ["a_norm", "a_w_in", "a_v_norm", "a_w_s", "a_b_s", "a_w_out", "f_norm", "f_w_in", "f_conv_w", "f_conv_b",
             "f_w_out", "kv_norm", "w_kv", "k_norm", "b_norm", "b_w_q", "b_q_norm", "b_sinks", "b_w_o"]
    return (loss, grad_x, *[out[n][0] for n in order], *[out[n][1] for n in order],
            *[out[n][2] for n in order], *[out[n][3] for n in order])
```

```python
import functools
import math

import jax
import jax.numpy as jnp
from jax import lax
from jax.experimental import pallas as pl
from jax.experimental.pallas import tpu as pltpu

f32 = jnp.float32
bf16 = jnp.bfloat16
MESH = pl.DeviceIdType.MESH
ANY = pl.BlockSpec(memory_space=pl.ANY)

EPS = 1e-6
LANES = 128
CHUNK = 128
HEAD_DIM = 64
N_Q_HEADS = 16
N_KV_HEADS = 4
Q_PER_KV = N_Q_HEADS // N_KV_HEADS
N_SHARDS = 4
N_DEV = 8

ADAM_LR = 0.001
ADAM_B1 = 0.9
ADAM_B2 = 0.999
ADAM_EPS = 1e-08
ADAM_WD = 0.01
ADAM_STEP = 10
ADAM_C1 = 1.0 - ADAM_B1 ** ADAM_STEP
ADAM_C2 = 1.0 - ADAM_B2 ** ADAM_STEP

_INV_SQRT2 = 1.0 / math.sqrt(2.0)
_INV_SQRT2PI = 1.0 / math.sqrt(2.0 * math.pi)


def _params(*sem):
    return pltpu.CompilerParams(dimension_semantics=sem)


def _gelu(z):
    return 0.5 * z * (1.0 + lax.erf(z * _INV_SQRT2))


def _gelu_and_grad(z):
    cdf = 0.5 * (1.0 + lax.erf(z * _INV_SQRT2))
    return z * cdf, cdf + z * jnp.exp(-0.5 * z * z) * _INV_SQRT2PI


def _dot(a, b):
    return jnp.dot(a, b, preferred_element_type=f32)


def _dot_nt(a, b):
    return lax.dot_general(a, b, (((1,), (1,)), ((), ())), preferred_element_type=f32)


def _dot_tn(a, b):
    return lax.dot_general(a, b, (((0,), (0,)), ((), ())), preferred_element_type=f32)


def _dot_split(a, b):
    hi = a.astype(bf16)
    lo = (a - hi.astype(f32)).astype(bf16)
    return _dot(hi, b) + _dot(lo, b)


VMEM_TILE_BUDGET = 40 * 1024 * 1024
MAX_ROW_TILE = 2048


def _row_tile(m, fixed_bytes, row_bytes):
    tm = min(m, MAX_ROW_TILE)
    while tm > 256 and 2 * (fixed_bytes + tm * row_bytes) > VMEM_TILE_BUDGET:
        tm //= 2
    return tm


def _isz(a):
    return jnp.dtype(a.dtype).itemsize


def mm_nn(a, w3, *, name, s0=0, ns=None, add=None, out_dtype=f32):
    m, k = a.shape
    s_all, _, n_s = w3.shape
    ns = s_all if ns is None else ns
    tm = _row_tile(m, k * n_s * 2, k * _isz(a) + n_s * jnp.dtype(out_dtype).itemsize + (0 if add is None else n_s * 4))

    def body(*refs):
        if add is None:
            a_ref, w_ref, o_ref = refs
            acc = _dot(a_ref[...].astype(bf16), w_ref[0])
        else:
            a_ref, w_ref, add_ref, o_ref = refs
            acc = _dot(a_ref[...].astype(bf16), w_ref[0]) + add_ref[...]
        o_ref[...] = acc.astype(out_dtype)

    in_specs = [pl.BlockSpec((tm, k), lambda j, i: (i, 0)),
                pl.BlockSpec((1, k, n_s), lambda j, i: (s0 + j, 0, 0))]
    args = [a, w3]
    if add is not None:
        in_specs.append(pl.BlockSpec((tm, n_s), lambda j, i: (i, j)))
        args.append(add)
    return pl.pallas_call(
        body, name=name, grid=(ns, m // tm), in_specs=in_specs,
        out_specs=pl.BlockSpec((tm, n_s), lambda j, i: (i, j)),
        out_shape=jax.ShapeDtypeStruct((m, ns * n_s), out_dtype),
        compiler_params=_params("parallel", "parallel"))(*args)


def mm_nt(a_list, w3, *, name, tko=None, add=None, out_dtype=f32):
    s_all, k_out, n_s = w3.shape
    m = a_list[0].shape[0]
    na = len(a_list)
    spa = s_all // na
    tko = k_out if tko is None else tko
    tm = _row_tile(m, tko * n_s * 2, na * n_s * _isz(a_list[0]) + tko * 4 * (1 if add is None else 2))

    def body(*refs):
        a_refs = refs[:na]
        w_ref = refs[na]
        o_ref = refs[-1]
        s = pl.program_id(2)

        @pl.when(s == 0)
        def _():
            if add is None:
                o_ref[...] = jnp.zeros_like(o_ref)
            else:
                o_ref[...] = refs[na + 1][...]

        for idx in range(na):
            @pl.when(s // spa == idx)
            def _(idx=idx):
                o_ref[...] += _dot_nt(a_refs[idx][...].astype(bf16), w_ref[0])

    def a_map(idx):
        return lambda ko, i, s: (i, jnp.clip(s - idx * spa, 0, spa - 1))

    in_specs = [pl.BlockSpec((tm, n_s), a_map(idx)) for idx in range(na)]
    in_specs.append(pl.BlockSpec((1, tko, n_s), lambda ko, i, s: (s, ko, 0)))
    args = list(a_list) + [w3]
    if add is not None:
        in_specs.append(pl.BlockSpec((tm, tko), lambda ko, i, s: (i, ko)))
        args.append(add)
    return pl.pallas_call(
        body, name=name, grid=(k_out // tko, m // tm, s_all), in_specs=in_specs,
        out_specs=pl.BlockSpec((tm, tko), lambda ko, i, s: (i, ko)),
        out_shape=jax.ShapeDtypeStruct((m, k_out), out_dtype),
        compiler_params=_params("parallel", "parallel", "arbitrary"))(*args)


def mm_tn(a, b_list, c_arr, *, name, n_s, shard_rows, tki=None):
    m, k_in = a.shape
    na = len(b_list)
    s_all = sum(b.shape[1] for b in b_list) // n_s
    spa = s_all // na
    tki = k_in if tki is None else tki
    tm = _row_tile(m, tki * n_s * 4, tki * _isz(a) + na * n_s * _isz(b_list[0]))

    nsteps = m // tm
    per_blk = tki // shard_rows
    half = shard_rows // 2

    def body(c_ref, *refs):
        a_ref = refs[0]
        b_refs = refs[1:1 + na]
        o_ref, ob_ref = refs[-2], refs[-1]
        s = pl.program_id(0)
        r = pl.program_id(2)

        @pl.when(r == 0)
        def _():
            o_ref[...] = jnp.zeros_like(o_ref)

        for idx in range(na):
            @pl.when(s // spa == idx)
            def _(idx=idx):
                o_ref[0] += _dot_tn(a_ref[...].astype(bf16), b_refs[idx][...].astype(bf16))

        @pl.when(r == nsteps - 1)
        def _():
            for q in range(per_blk):
                start = pl.multiple_of(q * shard_rows + (1 - c_ref[0]) * half, 16)
                ob_ref[q] = o_ref[0, pl.ds(start, half), :].astype(bf16)

    def b_map(idx):
        def index(s, ki, r, c_ref):
            active = (s // spa) == idx
            return (jnp.where(active, r, 0), jnp.clip(s - idx * spa, 0, spa - 1))
        return index

    in_specs = [pl.BlockSpec((tm, tki), lambda s, ki, r, c_ref: (r, ki))]
    in_specs += [pl.BlockSpec((tm, n_s), b_map(idx)) for idx in range(na)]
    n_blk = k_in // tki
    return pl.pallas_call(
        body, name=name,
        grid_spec=pltpu.PrefetchScalarGridSpec(
            num_scalar_prefetch=1, grid=(s_all, n_blk, nsteps), in_specs=in_specs,
            out_specs=[pl.BlockSpec((1, tki, n_s), lambda s, ki, r, c_ref: (s, ki, 0)),
                       pl.BlockSpec((per_blk, half, n_s), lambda s, ki, r, c_ref: (s * n_blk + ki, 0, 0))]),
        out_shape=[jax.ShapeDtypeStruct((s_all, k_in, n_s), f32),
                   jax.ShapeDtypeStruct((s_all * k_in // shard_rows, half, n_s), bf16)],
        compiler_params=_params("parallel", "parallel", "arbitrary"))(c_arr, a, *b_list)


def mm_nt_rms_bwd(a_list, w3, x, r, g, dx_in, *, name, extra=None):
    s_all, d, n_s = w3.shape
    m = a_list[0].shape[0]
    na = len(a_list)
    spa = s_all // na
    ne = 0 if extra is None else 1
    tm = _row_tile(m, d * n_s * 2, na * n_s * _isz(a_list[0]) + d * 4 * (4 + ne))

    def body(*refs):
        a_refs, w_ref = refs[:na], refs[na]
        x_ref, r_ref, g_ref, dxin_ref = refs[na + 1:na + 5]
        dh2_ref, g2_ref = (refs[na + 5], refs[na + 6]) if ne else (None, None)
        outs = refs[na + 5 + 2 * ne:]
        dx_ref, dg_ref = outs[0], outs[1]
        dg2_ref = outs[2] if ne else None
        acc_ref = outs[-1]
        i, s = pl.program_id(0), pl.program_id(1)

        @pl.when(s == 0)
        def _():
            acc_ref[...] = jnp.zeros_like(acc_ref)

        for idx in range(na):
            @pl.when(s // spa == idx)
            def _(idx=idx):
                acc_ref[...] += _dot_nt(a_refs[idx][...].astype(bf16), w_ref[0])

        @pl.when(s == s_all - 1)
        def _():
            rv = r_ref[...]
            xh = x_ref[...] * rv
            total = dxin_ref[...]
            pairs = [(acc_ref[...], g_ref, dg_ref)] + ([(dh2_ref[...], g2_ref, dg2_ref)] if ne else [])
            for dh, gain_ref, dgain_ref in pairs:
                part = jnp.sum(dh * xh, axis=0, keepdims=True)

                @pl.when(i == 0)
                def _(dgain_ref=dgain_ref, part=part):
                    dgain_ref[...] = part

                @pl.when(i > 0)
                def _(dgain_ref=dgain_ref, part=part):
                    dgain_ref[...] += part

                tg = dh * gain_ref[...]
                total = total + rv * (tg - xh * jnp.mean(tg * xh, axis=1, keepdims=True))
            dx_ref[...] = total

    def a_map(idx):
        return lambda i, s: (i, jnp.clip(s - idx * spa, 0, spa - 1))

    row = pl.BlockSpec((tm, d), lambda i, s: (i, 0))
    vec = pl.BlockSpec((1, d), lambda i, s: (0, 0))
    in_specs = [pl.BlockSpec((tm, n_s), a_map(idx)) for idx in range(na)]
    in_specs += [pl.BlockSpec((1, d, n_s), lambda i, s: (s, 0, 0)), row, pl.BlockSpec((tm, 1), lambda i, s: (i, 0)), vec, row]
    args = list(a_list) + [w3, x, r, g, dx_in]
    if ne:
        in_specs += [row, vec]
        args += list(extra)
    outs = pl.pallas_call(
        body, name=name, grid=(m // tm, s_all), in_specs=in_specs, out_specs=[row] + [vec] * (1 + ne),
        out_shape=[jax.ShapeDtypeStruct((m, d), f32)] + [jax.ShapeDtypeStruct((1, d), f32)] * (1 + ne),
        scratch_shapes=[pltpu.VMEM((tm, d), f32)],
        compiler_params=_params("arbitrary", "arbitrary"))(*args)
    return outs


def mm_residual(a, w, x, *, name, gains=(), target=None):
    m, k = a.shape
    d = w.shape[1]
    ng = len(gains)
    tm = _row_tile(m, k * d * 2, k * _isz(a) + d * 4 * 3 + ng * d * 2)

    def body(*refs):
        a_ref, w_ref, x_ref = refs[:3]
        y = _dot(a_ref[...].astype(bf16), w_ref[...]) + x_ref[...]
        if target is None:
            g_refs = refs[3:3 + ng]
            y_ref = refs[3 + ng]
            h_refs = refs[4 + ng:4 + 2 * ng]
            r_ref = refs[-1]
            y_ref[...] = y
            r = lax.rsqrt(jnp.mean(y * y, axis=1, keepdims=True) + EPS)
            yh = y * r
            for g_ref, h_ref in zip(g_refs, h_refs):
                h_ref[...] = (yh * g_ref[...]).astype(bf16)
            r_ref[...] = r
        else:
            t_ref, dy_ref, s_ref = refs[3:]
            i = pl.program_id(0)
            e = y - t_ref[...]
            dy_ref[...] = e * (1.0 / d)
            part = jnp.sum(e * e, axis=0, keepdims=True)

            @pl.when(i == 0)
            def _():
                s_ref[...] = part

            @pl.when(i > 0)
            def _():
                s_ref[...] += part

    row = pl.BlockSpec((tm, d), lambda i: (i, 0))
    vec = pl.BlockSpec((1, d), lambda i: (0, 0))
    in_specs = [pl.BlockSpec((tm, k), lambda i: (i, 0)), pl.BlockSpec((k, d), lambda i: (0, 0)), row]
    if target is None:
        outs = pl.pallas_call(
            body, name=name, grid=(m // tm,), in_specs=in_specs + [vec] * ng,
            out_specs=[row] * (1 + ng) + [pl.BlockSpec((tm, 1), lambda i: (i, 0))],
            out_shape=[jax.ShapeDtypeStruct((m, d), f32)] + [jax.ShapeDtypeStruct((m, d), bf16)] * ng
            + [jax.ShapeDtypeStruct((m, 1), f32)],
            compiler_params=_params("parallel"))(a, w, x, *gains)
        return outs[0], outs[1:1 + ng], outs[-1]
    return pl.pallas_call(
        body, name=name, grid=(m // tm,), in_specs=in_specs + [row], out_specs=[row, vec],
        out_shape=[jax.ShapeDtypeStruct((m, d), f32), jax.ShapeDtypeStruct((1, d), f32)],
        compiler_params=_params("arbitrary"))(a, w, x, target)


def rms_fwd(x, gains, *, name, tr=512):
    t, d = x.shape
    tr = min(tr, t)
    ng = len(gains)

    def body(*refs):
        x_ref = refs[0]
        g_refs = refs[1:1 + ng]
        h_refs = refs[1 + ng:1 + 2 * ng]
        r_ref = refs[-1]
        xv = x_ref[...]
        r = lax.rsqrt(jnp.mean(xv * xv, axis=1, keepdims=True) + EPS)
        xh = xv * r
        for g_ref, h_ref in zip(g_refs, h_refs):
            h_ref[...] = (xh * g_ref[...]).astype(bf16)
        r_ref[...] = r

    row = pl.BlockSpec((tr, d), lambda i: (i, 0))
    vec = pl.BlockSpec((1, d), lambda i: (0, 0))
    outs = pl.pallas_call(
        body, name=name, grid=(t // tr,), in_specs=[row] + [vec] * ng,
        out_specs=[row] * ng + [pl.BlockSpec((tr, 1), lambda i: (i, 0))],
        out_shape=[jax.ShapeDtypeStruct((t, d), bf16)] * ng + [jax.ShapeDtypeStruct((t, 1), f32)],
        compiler_params=_params("parallel"))(x, *gains)
    return outs[:ng], outs[ng]


def rms_bwd(dh_list, x, r, gains, dx_in, *, name, tr=512):
    t, d = x.shape
    tr = min(tr, t)
    ng = len(gains)

    def body(*refs):
        dh_refs = refs[:ng]
        x_ref, r_ref = refs[ng], refs[ng + 1]
        g_refs = refs[ng + 2:2 * ng + 2]
        dxin_ref = refs[2 * ng + 2]
        dx_ref = refs[2 * ng + 3]
        dg_refs = refs[2 * ng + 4:]
        i = pl.program_id(0)
        rv = r_ref[...]
        xh = x_ref[...] * rv
        acc = dxin_ref[...]
        for dh_ref, g_ref, dg_ref in zip(dh_refs, g_refs, dg_refs):
            dh = dh_ref[...]
            part = jnp.sum(dh * xh, axis=0, keepdims=True)

            @pl.when(i == 0)
            def _(dg_ref=dg_ref, part=part):
                dg_ref[...] = part

            @pl.when(i > 0)
            def _(dg_ref=dg_ref, part=part):
                dg_ref[...] += part

            tg = dh * g_ref[...]
            acc = acc + rv * (tg - xh * jnp.mean(tg * xh, axis=1, keepdims=True))
        dx_ref[...] = acc

    row = pl.BlockSpec((tr, d), lambda i: (i, 0))
    vec = pl.BlockSpec((1, d), lambda i: (0, 0))
    outs = pl.pallas_call(
        body, name=name, grid=(t // tr,),
        in_specs=[row] * ng + [row, pl.BlockSpec((tr, 1), lambda i: (i, 0))] + [vec] * ng + [row],
        out_specs=[row] + [vec] * ng,
        out_shape=[jax.ShapeDtypeStruct((t, d), f32)] + [jax.ShapeDtypeStruct((1, d), f32)] * ng,
        compiler_params=_params("arbitrary"))(*dh_list, x, r, *gains, dx_in)
    return outs[0], outs[1:]


def sgu_gate_fwd(zu, zv, gv, wc, bt, *, name, tr=512):
    t, w = zu.shape
    tr = min(tr, t)
    groups = w // LANES

    def body(zu_ref, zv_ref, gv_ref, wc_ref, bt_ref, y_ref):
        vp = _gelu(zv_ref[...])
        rv = lax.rsqrt(jnp.mean(vp * vp, axis=1, keepdims=True) + EPS)
        vb = (vp * rv * gv_ref[...]).astype(bf16)
        for c in range(tr // CHUNK):
            rows = slice(c * CHUNK, (c + 1) * CHUNK)
            for g in range(groups):
                cols = slice(g * LANES, (g + 1) * LANES)
                sv = _dot(wc_ref[g], vb[rows, cols]) + bt_ref[:, g:g + 1]
                y_ref[rows, cols] = (_gelu(zu_ref[rows, cols]) * sv).astype(bf16)

    row = pl.BlockSpec((tr, w), lambda i: (i, 0))
    return pl.pallas_call(
        body, name=name, grid=(t // tr,),
        in_specs=[row, row, pl.BlockSpec((1, w), lambda i: (0, 0)),
                  pl.BlockSpec((groups, CHUNK, CHUNK), lambda i: (0, 0, 0)),
                  pl.BlockSpec((CHUNK, groups), lambda i: (0, 0))],
        out_specs=row, out_shape=jax.ShapeDtypeStruct((t, w), bf16),
        compiler_params=_params("parallel"))(zu, zv, gv, wc, bt)


def sgu_gate_bwd(zu, zv, dy, gv, wc, bt, *, name, tr=512):
    t, w = zu.shape
    tr = min(tr, t)
    groups = w // LANES
    nsteps = t // tr

    def body(zu_ref, zv_ref, dy_ref, gv_ref, wc_ref, bt_ref,
             dzu_ref, dzv_ref, dgv_ref, dws_ref, dbt_ref, dv_ref, bacc_ref):
        i = pl.program_id(0)

        @pl.when(i == 0)
        def _():
            dgv_ref[...] = jnp.zeros_like(dgv_ref)
            dws_ref[...] = jnp.zeros_like(dws_ref)
            bacc_ref[...] = jnp.zeros_like(bacc_ref)

        vp, vp_grad = _gelu_and_grad(zv_ref[...])
        rv = lax.rsqrt(jnp.mean(vp * vp, axis=1, keepdims=True) + EPS)
        vhat = vp * rv
        vb = (vhat * gv_ref[...]).astype(bf16)
        for c in range(tr // CHUNK):
            rows = slice(c * CHUNK, (c + 1) * CHUNK)
            for g in range(groups):
                cols = slice(g * LANES, (g + 1) * LANES)
                vblk = vb[rows, cols]
                sv = _dot(wc_ref[g], vblk) + bt_ref[:, g:g + 1]
                zub = zu_ref[rows, cols]
                dyb = dy_ref[rows, cols]
                ub, ub_grad = _gelu_and_grad(zub)
                dzu_ref[rows, cols] = (dyb * sv * ub_grad).astype(bf16)
                dsv = dyb * ub
                bacc_ref[:, cols] += dsv
                dsvb = dsv.astype(bf16)
                dv_ref[rows, cols] = _dot_tn(wc_ref[g], dsvb)
                dws_ref[g] += _dot_nt(dsvb, vblk)
        dv = dv_ref[...]
        dgv_ref[...] += jnp.sum(dv * vhat, axis=0, keepdims=True)
        tg = dv * gv_ref[...]
        dvp = rv * (tg - vhat * jnp.mean(tg * vhat, axis=1, keepdims=True))
        dzv_ref[...] = (dvp * vp_grad).astype(bf16)

        @pl.when(i == nsteps - 1)
        def _():
            tt = lax.broadcasted_iota(jnp.int32, (CHUNK, CHUNK), 0)
            ss = lax.broadcasted_iota(jnp.int32, (CHUNK, CHUNK), 1)
            for g in range(groups):
                dws_ref[g] = jnp.where(ss <= tt, dws_ref[g], 0.0)
                dbt_ref[:, g:g + 1] = jnp.sum(bacc_ref[:, g * LANES:(g + 1) * LANES], axis=1, keepdims=True)

    row = pl.BlockSpec((tr, w), lambda i: (i, 0))
    full3 = pl.BlockSpec((groups, CHUNK, CHUNK), lambda i: (0, 0, 0))
    return pl.pallas_call(
        body, name=name, grid=(nsteps,),
        in_specs=[row, row, row, pl.BlockSpec((1, w), lambda i: (0, 0)), full3,
                  pl.BlockSpec((CHUNK, groups), lambda i: (0, 0))],
        out_specs=[row, row, pl.BlockSpec((1, w), lambda i: (0, 0)), full3,
                   pl.BlockSpec((CHUNK, groups), lambda i: (0, 0))],
        out_shape=[jax.ShapeDtypeStruct((t, w), bf16), jax.ShapeDtypeStruct((t, w), bf16),
                   jax.ShapeDtypeStruct((1, w), f32), jax.ShapeDtypeStruct((groups, CHUNK, CHUNK), f32),
                   jax.ShapeDtypeStruct((CHUNK, groups), f32)],
        scratch_shapes=[pltpu.VMEM((tr, w), f32), pltpu.VMEM((CHUNK, w), f32)],
        compiler_params=_params("arbitrary"))(zu, zv, dy, gv, wc, bt)


HALO = 8


def _shift_down(v, halo, k, first):
    r = pltpu.roll(v, k, 0)
    hh = jnp.where(first, 0.0, pltpu.roll(halo, k, 0))
    rid = lax.broadcasted_iota(jnp.int32, (HALO, v.shape[1]), 0)
    head = jnp.where(rid < k, hh, r[0:HALO])
    if v.shape[0] == HALO:
        return head
    return jnp.concatenate([head, r[HALO:]], axis=0)


def _shift_up(v, halo, k, last):
    n = v.shape[0]
    r = pltpu.roll(v, n - k, 0)
    hh = jnp.where(last, 0.0, pltpu.roll(halo, HALO - k, 0))
    rid = lax.broadcasted_iota(jnp.int32, (HALO, v.shape[1]), 0)
    tail = jnp.where(rid >= HALO - k, hh, r[n - HALO:])
    return jnp.concatenate([r[:n - HALO], tail], axis=0)


def _conv(p, halo, w_ref, b_ref, first):
    return (w_ref[2:3, :] * p + w_ref[1:2, :] * _shift_down(p, halo, 1, first)
            + w_ref[0:1, :] * _shift_down(p, halo, 2, first) + b_ref[...])


BF16_ROWS = 16


def ffn_in_fused(h, w_in4, wg, wu, bg, bu, *, name):
    t, k = h.shape
    s_all, _, n_s = w_in4.shape
    half = s_all // 2
    tm = _row_tile(t, 2 * k * n_s * 2, k * 2 + 4 * n_s * 4 + n_s * 2)

    def body(h_ref, hh_ref, wg_ref, wu_ref, cg_ref, cu_ref, bg_ref, bu_ref, pg_ref, pu_ref, gate_ref, up_ref, a_ref):
        first = pl.program_id(1) == 0
        hv, hh = h_ref[...], hh_ref[...]
        outs = []
        for w_ref, c_ref, b_ref, p_ref, o_ref in ((wg_ref, cg_ref, bg_ref, pg_ref, gate_ref),
                                                  (wu_ref, cu_ref, bu_ref, pu_ref, up_ref)):
            p = _dot(hv, w_ref[0])
            p_ref[...] = p
            hu = _conv(p, _dot(hh, w_ref[0])[BF16_ROWS - HALO:], c_ref, b_ref, first)
            o_ref[...] = hu
            outs.append(hu)
        gate, up = outs
        a_ref[...] = (gate * jax.nn.sigmoid(gate) * up).astype(bf16)

    tile = pl.BlockSpec((tm, n_s), lambda j, i: (i, j))
    cw = pl.BlockSpec((3, n_s), lambda j, i: (0, j))
    cb = pl.BlockSpec((1, n_s), lambda j, i: (0, j))
    f = half * n_s
    return pl.pallas_call(
        body, name=name, grid=(half, t // tm),
        in_specs=[pl.BlockSpec((tm, k), lambda j, i: (i, 0)),
                  pl.BlockSpec((BF16_ROWS, k), lambda j, i: (jnp.maximum(i * (tm // BF16_ROWS) - 1, 0), 0)),
                  pl.BlockSpec((1, k, n_s), lambda j, i: (j, 0, 0)),
                  pl.BlockSpec((1, k, n_s), lambda j, i: (j + half, 0, 0)), cw, cw, cb, cb],
        out_specs=[tile] * 5,
        out_shape=[jax.ShapeDtypeStruct((t, f), f32)] * 4 + [jax.ShapeDtypeStruct((t, f), bf16)],
        compiler_params=_params("parallel", "parallel"))(h, h, w_in4, w_in4, wg, wu, bg, bu)


def _gate_grads(gate, up, dav):
    sg = jax.nn.sigmoid(gate)
    return dav * up * (sg * (1.0 + gate * (1.0 - sg))), dav * gate * sg


GATE_BWD_ROWS = 512


def ffn_gate_bwd(dy, w_out, pg, pu, gate, up, wg, wu, *, name):
    t, f = pg.shape
    d = dy.shape[1]
    tr = min(GATE_BWD_ROWS, t)
    nsteps = t // tr
    tc = f // 2

    def body(dy_ref, dyn_ref, w_ref, pg_ref, pu_ref, gate_ref, gaten_ref, up_ref, upn_ref, wg_ref, wu_ref,
             dg_ref, du_ref, sg_ref, su_ref):
        i = pl.program_id(1)
        last = i == nsteps - 1
        w = w_ref[0]
        da = _dot_nt(dy_ref[...].astype(bf16), w)
        da_n = _dot_nt(dyn_ref[...].astype(bf16), w)
        dgate, dup = _gate_grads(gate_ref[...], up_ref[...], da)
        dgate_n, dup_n = _gate_grads(gaten_ref[...], upn_ref[...], da_n)
        rid = lax.broadcasted_iota(jnp.int32, (8, tc), 0)
        for dd, d_n, c_ref, p_ref, o_ref, s_ref in ((dgate, dgate_n, wg_ref, pg_ref, dg_ref, sg_ref),
                                                    (dup, dup_n, wu_ref, pu_ref, du_ref, su_ref)):
            d1, d2 = _shift_up(dd, d_n, 1, last), _shift_up(dd, d_n, 2, last)
            o_ref[...] = (c_ref[2:3, :] * dd + c_ref[1:2, :] * d1 + c_ref[0:1, :] * d2).astype(bf16)
            p = p_ref[...]
            sums = [jnp.sum(d2 * p, axis=0, keepdims=True), jnp.sum(d1 * p, axis=0, keepdims=True),
                    jnp.sum(dd * p, axis=0, keepdims=True), jnp.sum(dd, axis=0, keepdims=True)]
            part = jnp.zeros((8, tc), f32)
            for k, sk in enumerate(sums):
                part = jnp.where(rid == k, sk, part)

            @pl.when(i == 0)
            def _(s_ref=s_ref, part=part):
                s_ref[...] = part

            @pl.when(i > 0)
            def _(s_ref=s_ref, part=part):
                s_ref[...] += part

    def nxt_rows(j, i):
        return (jnp.minimum((i + 1) * (tr // HALO), t // HALO - 1), j)

    tile = pl.BlockSpec((tr, tc), lambda j, i: (i, j))
    nxt = pl.BlockSpec((HALO, tc), nxt_rows)
    wspec = pl.BlockSpec((3, tc), lambda j, i: (0, j))
    stat = pl.BlockSpec((8, tc), lambda j, i: (0, j))
    return pl.pallas_call(
        body, name=name, grid=(2, nsteps),
        in_specs=[pl.BlockSpec((tr, d), lambda j, i: (i, 0)),
                  pl.BlockSpec((HALO, d), lambda j, i: (nxt_rows(j, i)[0], 0)),
                  pl.BlockSpec((1, tc, d), lambda j, i: (j, 0, 0)),
                  tile, tile, tile, nxt, tile, nxt, wspec, wspec],
        out_specs=[tile, tile, stat, stat],
        out_shape=[jax.ShapeDtypeStruct((t, f), bf16), jax.ShapeDtypeStruct((t, f), bf16),
                   jax.ShapeDtypeStruct((8, f), f32), jax.ShapeDtypeStruct((8, f), f32)],
        compiler_params=_params("parallel", "arbitrary"))(
            dy, dy, w_out.reshape(2, tc, d), pg, pu, gate, gate, up, up, wg, wu)


def _head_mean_matrix():
    i = lax.broadcasted_iota(jnp.int32, (LANES, LANES), 0) // HEAD_DIM
    j = lax.broadcasted_iota(jnp.int32, (LANES, LANES), 1) // HEAD_DIM
    return jnp.where(i == j, 1.0 / HEAD_DIM, 0.0).astype(bf16)


def _lane_half(shape):
    return (lax.broadcasted_iota(jnp.int32, shape, 1) % LANES) // HEAD_DIM


def q_norm_fwd(qp, g2, *, name, scale, tr=512):
    t, w = qp.shape
    tr = min(tr, t)

    def body(x_ref, g_ref, o_ref):
        bd = _head_mean_matrix()
        for cb in range(w // LANES):
            cols = slice(cb * LANES, (cb + 1) * LANES)
            xc = x_ref[:, cols]
            rh = lax.rsqrt(_dot_split(xc * xc, bd) + EPS)
            o_ref[:, cols] = (xc * rh * g_ref[...] * scale).astype(bf16)

    row = pl.BlockSpec((tr, w), lambda i: (i, 0))
    return pl.pallas_call(
        body, name=name, grid=(t // tr,), in_specs=[row, pl.BlockSpec((1, LANES), lambda i: (0, 0))],
        out_specs=row, out_shape=jax.ShapeDtypeStruct((t, w), bf16),
        compiler_params=_params("parallel"))(qp, g2)


def q_norm_bwd(dq, qp, g2, *, name, scale, tr=512):
    t, w = qp.shape
    tr = min(tr, t)

    def body(dq_ref, x_ref, g_ref, o_ref, dg_ref):
        i = pl.program_id(0)
        bd = _head_mean_matrix()
        acc = jnp.zeros((1, LANES), f32)
        for cb in range(w // LANES):
            cols = slice(cb * LANES, (cb + 1) * LANES)
            xc = x_ref[:, cols]
            rh = lax.rsqrt(_dot_split(xc * xc, bd) + EPS)
            xh = xc * rh
            dy = dq_ref[:, cols] * scale
            acc = acc + jnp.sum(dy * xh, axis=0, keepdims=True)
            tg = dy * g_ref[...]
            o_ref[:, cols] = (rh * (tg - xh * _dot_split(tg * xh, bd))).astype(bf16)

        @pl.when(i == 0)
        def _():
            dg_ref[...] = acc

        @pl.when(i > 0)
        def _():
            dg_ref[...] += acc

    row = pl.BlockSpec((tr, w), lambda i: (i, 0))
    vec = pl.BlockSpec((1, LANES), lambda i: (0, 0))
    return pl.pallas_call(
        body, name=name, grid=(t // tr,), in_specs=[row, row, vec], out_specs=[row, vec],
        out_shape=[jax.ShapeDtypeStruct((t, w), bf16), jax.ShapeDtypeStruct((1, LANES), f32)],
        compiler_params=_params("arbitrary"))(dq, qp, g2)


def kv_post_fwd(kv, g2, *, name, tr=512):
    t, w = kv.shape
    tr = min(tr, t)
    kw = w // 2

    def body(x_ref, g_ref, k_ref, v_ref):
        bd = _head_mean_matrix()
        half = _lane_half((tr, LANES))
        for cb in range(kw // LANES):
            xc = x_ref[:, cb * LANES:(cb + 1) * LANES]
            rh = lax.rsqrt(_dot_split(xc * xc, bd) + EPS)
            kn = xc * rh * g_ref[...]
            vc = x_ref[:, kw + cb * LANES:kw + (cb + 1) * LANES]
            for src, dst in ((kn, k_ref), (vc, v_ref)):
                sw = pltpu.roll(src, HEAD_DIM, 1)
                for hf in range(2):
                    blk = 2 * cb + hf
                    dst[:, blk * LANES:(blk + 1) * LANES] = jnp.where(half == hf, src, sw).astype(bf16)

    return pl.pallas_call(
        body, name=name, grid=(t // tr,),
        in_specs=[pl.BlockSpec((tr, w), lambda i: (i, 0)), pl.BlockSpec((1, LANES), lambda i: (0, 0))],
        out_specs=[pl.BlockSpec((tr, 2 * kw), lambda i: (i, 0))] * 2,
        out_shape=[jax.ShapeDtypeStruct((t, 2 * kw), bf16)] * 2,
        compiler_params=_params("parallel"))(kv, g2)


def kv_post_bwd(dk2, dv2, kv, g2, *, name, tr=512):
    t, w = kv.shape
    tr = min(tr, t)
    kw = w // 2

    def body(dk_ref, dv_ref, x_ref, g_ref, o_ref, dg_ref):
        i = pl.program_id(0)
        bd = _head_mean_matrix()
        half = _lane_half((tr, LANES))
        acc = jnp.zeros((1, LANES), f32)

        def fold(ref, cb):
            a = ref[:, (2 * cb) * LANES:(2 * cb + 1) * LANES]
            b = ref[:, (2 * cb + 1) * LANES:(2 * cb + 2) * LANES]
            return jnp.where(half == 0, a + pltpu.roll(a, HEAD_DIM, 1), b + pltpu.roll(b, HEAD_DIM, 1))

        for cb in range(kw // LANES):
            cols = slice(cb * LANES, (cb + 1) * LANES)
            xc = x_ref[:, cols]
            rh = lax.rsqrt(_dot_split(xc * xc, bd) + EPS)
            xh = xc * rh
            dy = fold(dk_ref, cb)
            acc = acc + jnp.sum(dy * xh, axis=0, keepdims=True)
            tg = dy * g_ref[...]
            o_ref[:, cols] = (rh * (tg - xh * _dot_split(tg * xh, bd))).astype(bf16)
            o_ref[:, kw + cb * LANES:kw + (cb + 1) * LANES] = fold(dv_ref, cb).astype(bf16)

        @pl.when(i == 0)
        def _():
            dg_ref[...] = acc

        @pl.when(i > 0)
        def _():
            dg_ref[...] += acc

    dup = pl.BlockSpec((tr, 2 * kw), lambda i: (i, 0))
    row = pl.BlockSpec((tr, w), lambda i: (i, 0))
    vec = pl.BlockSpec((1, LANES), lambda i: (0, 0))
    return pl.pallas_call(
        body, name=name, grid=(t // tr,), in_specs=[dup, dup, row, vec], out_specs=[row, vec],
        out_shape=[jax.ShapeDtypeStruct((t, w), bf16), jax.ShapeDtypeStruct((1, LANES), f32)],
        compiler_params=_params("arbitrary"))(dk2, dv2, kv, g2)


def _slope(h):
    return 2.0 ** (-8.0 * (h + 1) / N_Q_HEADS)


GROUP_ROWS = Q_PER_KV * CHUNK


def _band_mask(n):
    tq = lax.broadcasted_iota(jnp.int32, (GROUP_ROWS, 2 * CHUNK), 0) % CHUNK
    jk = lax.broadcasted_iota(jnp.int32, (GROUP_ROWS, 2 * CHUNK), 1)
    dist = tq + CHUNK - jk
    ok = (dist >= 0) & (dist < CHUNK) & jnp.logical_not((n == 0) & (jk < CHUNK))
    return dist.astype(f32), ok


def _band(ref, n, kh):
    p0 = pl.multiple_of(jnp.maximum(n - 1, 0) * CHUNK, CHUNK)
    c0 = pl.multiple_of(n * CHUNK, CHUNK)
    cols = slice(kh * LANES, (kh + 1) * LANES)
    return jnp.concatenate([ref[pl.ds(p0, CHUNK), cols], ref[pl.ds(c0, CHUNK), cols]], axis=0)


def _stack_heads(ref, kh, half):
    parts = []
    for cb in (2 * kh, 2 * kh + 1):
        xc = ref[:, cb * LANES:(cb + 1) * LANES].astype(f32)
        parts += [jnp.where(half == hf, xc, 0.0).astype(bf16) for hf in range(2)]
    return jnp.concatenate(parts, axis=0)


def _unstack_heads(x4, half):
    return (jnp.where(half == 0, x4[0:CHUNK], x4[CHUNK:2 * CHUNK]),
            jnp.where(half == 0, x4[2 * CHUNK:3 * CHUNK], x4[3 * CHUNK:]))


def _per_head_column(kh, values):
    grp = lax.broadcasted_iota(jnp.int32, (GROUP_ROWS, 1), 0) // CHUNK
    col = jnp.full((GROUP_ROWS, 1), values[0], f32)
    for g in range(1, Q_PER_KV):
        col = jnp.where(grp == g, values[g], col)
    return col


def _softmax_band(q4, kband, dist, ok, slope, sink):
    s = _dot_nt(q4, kband)
    s = jnp.where(ok, s - slope * dist, -jnp.inf)
    m = jnp.maximum(jnp.max(s, axis=1, keepdims=True), sink)
    e = jnp.exp(s - m)
    es = jnp.exp(sink - m)
    den = jnp.sum(e, axis=1, keepdims=True) + es
    return e / den, es / den


def attn_fwd(q, k2, v2, sinks, *, name):
    t, w = q.shape
    nb = t // CHUNK

    def body(sink_ref, q_ref, k_ref, v_ref, o_ref):
        n = pl.program_id(0)
        dist, ok = _band_mask(n)
        half = _lane_half((CHUNK, LANES))
        for kh in range(N_KV_HEADS):
            heads = [Q_PER_KV * kh + g for g in range(Q_PER_KV)]
            slope = _per_head_column(kh, [_slope(h) for h in heads])
            sink = _per_head_column(kh, [sink_ref[h] for h in heads])
            q4 = _stack_heads(q_ref, kh, half)
            p, _ = _softmax_band(q4, _band(k_ref, n, kh), dist, ok, slope, sink)
            o4 = _dot(p.astype(bf16), _band(v_ref, n, kh))
            lo, hi = _unstack_heads(o4, half)
            o_ref[:, (2 * kh) * LANES:(2 * kh + 1) * LANES] = lo.astype(bf16)
            o_ref[:, (2 * kh + 1) * LANES:(2 * kh + 2) * LANES] = hi.astype(bf16)

    full = pl.BlockSpec((t, k2.shape[1]), lambda n: (0, 0))
    return pl.pallas_call(
        body, name=name, grid=(nb,),
        in_specs=[pl.BlockSpec(memory_space=pltpu.SMEM), pl.BlockSpec((CHUNK, w), lambda n: (n, 0)), full, full],
        out_specs=pl.BlockSpec((CHUNK, w), lambda n: (n, 0)),
        out_shape=jax.ShapeDtypeStruct((t, w), bf16),
        compiler_params=_params("parallel"))(sinks, q, k2, v2)


def attn_bwd(q, k2, v2, do, sinks, *, name):
    t, w = q.shape
    nb = t // CHUNK
    kw = k2.shape[1]

    def body(sink_ref, q_ref, k_ref, v_ref, do_ref, dq_ref, dk_ref, dv_ref, ds_ref, kc_ref, vc_ref):
        n = pl.program_id(0)

        @pl.when(n == 0)
        def _():
            ds_ref[...] = jnp.zeros_like(ds_ref)
            kc_ref[...] = jnp.zeros_like(kc_ref)
            vc_ref[...] = jnp.zeros_like(vc_ref)
            dk_ref[...] = jnp.zeros_like(dk_ref)
            dv_ref[...] = jnp.zeros_like(dv_ref)

        @pl.when(n == nb)
        def _():
            dk_ref[...] = kc_ref[...]
            dv_ref[...] = vc_ref[...]

        @pl.when(n < nb)
        def _():
            dist, ok = _band_mask(n)
            half = _lane_half((CHUNK, LANES))
            lane = lax.broadcasted_iota(jnp.int32, (1, LANES), 1)
            sink_acc = jnp.zeros((1, LANES), f32)
            for kh in range(N_KV_HEADS):
                heads = [Q_PER_KV * kh + g for g in range(Q_PER_KV)]
                slope = _per_head_column(kh, [_slope(h) for h in heads])
                sink = _per_head_column(kh, [sink_ref[h] for h in heads])
                q4 = _stack_heads(q_ref, kh, half)
                do4 = _stack_heads(do_ref, kh, half)
                kband = _band(k_ref, n, kh)
                vband = _band(v_ref, n, kh)
                p, ps = _softmax_band(q4, kband, dist, ok, slope, sink)
                dp = _dot_nt(do4, vband)
                delta = jnp.sum(p * dp, axis=1, keepdims=True)
                dsb = (p * (dp - delta)).astype(bf16)
                sd = ps * delta
                for g, h in enumerate(heads):
                    part = jnp.sum(sd[g * CHUNK:(g + 1) * CHUNK], axis=0, keepdims=True)
                    sink_acc = sink_acc + jnp.where(lane == h, -part, 0.0)
                lo, hi = _unstack_heads(_dot(dsb, kband), half)
                dq_ref[:, (2 * kh) * LANES:(2 * kh + 1) * LANES] = lo
                dq_ref[:, (2 * kh + 1) * LANES:(2 * kh + 2) * LANES] = hi
                dkb = _dot_tn(dsb, q4)
                dvb = _dot_tn(p.astype(bf16), do4)
                cols = slice(kh * LANES, (kh + 1) * LANES)
                dk_ref[:, cols] = kc_ref[:, cols] + dkb[0:CHUNK]
                dv_ref[:, cols] = vc_ref[:, cols] + dvb[0:CHUNK]
                kc_ref[:, cols] = dkb[CHUNK:]
                vc_ref[:, cols] = dvb[CHUNK:]
            ds_ref[...] += sink_acc

    full = pl.BlockSpec((t, kw), lambda n: (0, 0))
    qblk = pl.BlockSpec((CHUNK, w), lambda n: (jnp.minimum(n, nb - 1), 0))
    kblk = pl.BlockSpec((CHUNK, kw), lambda n: (jnp.maximum(n - 1, 0), 0))
    return pl.pallas_call(
        body, name=name, grid=(nb + 1,),
        in_specs=[pl.BlockSpec(memory_space=pltpu.SMEM), qblk, full, full, qblk],
        out_specs=[qblk, kblk, kblk, pl.BlockSpec((1, LANES), lambda n: (0, 0))],
        out_shape=[jax.ShapeDtypeStruct((t, w), f32), jax.ShapeDtypeStruct((t, kw), f32),
                   jax.ShapeDtypeStruct((t, kw), f32), jax.ShapeDtypeStruct((1, LANES), f32)],
        scratch_shapes=[pltpu.VMEM((CHUNK, kw), f32), pltpu.VMEM((CHUNK, kw), f32)],
        compiler_params=_params("arbitrary"))(sinks, q, k2, v2, do)


def loss_head(y, target, *, name, tr=512):
    t, d = y.shape
    tr = min(tr, t)

    def body(y_ref, t_ref, dy_ref, s_ref):
        i = pl.program_id(0)
        e = y_ref[...] - t_ref[...]
        dy_ref[...] = e * (1.0 / d)
        part = jnp.sum(e * e, axis=0, keepdims=True)

        @pl.when(i == 0)
        def _():
            s_ref[...] = part

        @pl.when(i > 0)
        def _():
            s_ref[...] += part

    row = pl.BlockSpec((tr, d), lambda i: (i, 0))
    vec = pl.BlockSpec((1, d), lambda i: (0, 0))
    return pl.pallas_call(
        body, name=name, grid=(t // tr,), in_specs=[row, row], out_specs=[row, vec],
        out_shape=[jax.ShapeDtypeStruct((t, d), f32), jax.ShapeDtypeStruct((1, d), f32)],
        compiler_params=_params("arbitrary"))(y, target)


N_STEPS = 8


def _row_blocks(shape):
    if len(shape) == 2:
        r, c = shape
        return (r // N_STEPS, c), (lambda s: (s, 0))
    l, r, c = shape
    per = N_STEPS // l
    return (1, r // per, c), (lambda s: (s // per, s % per, 0))


CAST_STEPS = 4


def cast_into_slot(arrays, k_arr, *, name):
    in_specs, out_specs, out_shape, layers = [], [], [], []
    for a in arrays:
        r, c = a.shape[-2:]
        rb = r // CAST_STEPS
        if a.ndim == 2:
            in_specs.append(pl.BlockSpec((rb, c), lambda s, k: (s, 0)))
            layers.append(None)
        else:
            for l in range(a.shape[0]):
                in_specs.append(pl.BlockSpec((1, rb, c), lambda s, k, l=l: (l, s, 0)))
                layers.append(l)
        for _ in range(1 if a.ndim == 2 else a.shape[0]):
            out_specs.append(pl.BlockSpec((1, rb, c), lambda s, k: (k[0], s, 0)))
            out_shape.append(jax.ShapeDtypeStruct((N_SHARDS, r, c), bf16))
    n = len(in_specs)

    def body(k_ref, *refs):
        for i_ref, o_ref, l in zip(refs[:n], refs[n:], layers):
            o_ref[0] = (i_ref[...] if l is None else i_ref[0]).astype(bf16)

    args = []
    for a in arrays:
        args += [a] * (1 if a.ndim == 2 else a.shape[0])
    return pl.pallas_call(
        body, name=name,
        grid_spec=pltpu.PrefetchScalarGridSpec(num_scalar_prefetch=1, grid=(CAST_STEPS,),
                                               in_specs=in_specs, out_specs=out_specs),
        out_shape=out_shape, compiler_params=_params("parallel"))(k_arr, *args)


def adamw(ws, gs, ms, vs, *, name):
    n = len(ws)
    specs, g_specs, g_count = [], [], []
    for w, g_list in zip(ws, gs):
        blk, index = _row_blocks(w.shape)
        specs.append(pl.BlockSpec(blk, index))
        layers = len(g_list)
        per = N_STEPS // layers
        g_count.append(layers)
        for l in range(layers):
            g_specs.append(pl.BlockSpec(blk[-2:], lambda s, l=l, per=per: (jnp.where(s // per == l, s % per, 0), 0)))
    ng = len(g_specs)

    def body(*refs):
        s = pl.program_id(0)
        g_refs = refs[3 * n:3 * n + ng]
        outs = refs[3 * n + ng:]
        off = 0
        for i in range(n):
            w_ref, m_ref, v_ref = refs[i], refs[n + i], refs[2 * n + i]
            go_ref, d_ref, nm_ref, nv_ref = (outs[k * n + i] for k in range(4))
            layers = g_count[i]
            g = g_refs[off][...]
            for l in range(1, layers):
                g = jnp.where(s // (N_STEPS // layers) == l, g_refs[off + l][...], g)
            off += layers
            g = g.reshape(w_ref.shape)
            m = ADAM_B1 * m_ref[...] + (1.0 - ADAM_B1) * g
            v = ADAM_B2 * v_ref[...] + (1.0 - ADAM_B2) * (g * g)
            m_hat = m / ADAM_C1
            v_hat = v / ADAM_C2
            go_ref[...] = g
            d_ref[...] = -ADAM_LR * (m_hat / (jnp.sqrt(v_hat) + ADAM_EPS) + ADAM_WD * w_ref[...])
            nm_ref[...] = m
            nv_ref[...] = v

    outs = pl.pallas_call(
        body, name=name, grid=(N_STEPS,), in_specs=specs * 3 + g_specs, out_specs=specs * 4,
        out_shape=[jax.ShapeDtypeStruct(a.shape, f32) for a in ws] * 4,
        compiler_params=_params("parallel"))(*ws, *ms, *vs, *[g for g_list in gs for g in g_list])
    return [outs[k * n:(k + 1) * n] for k in range(4)]


def _adamw_update(w, g, m, v):
    m = ADAM_B1 * m + (1.0 - ADAM_B1) * g
    v = ADAM_B2 * v + (1.0 - ADAM_B2) * (g * g)
    m_hat = m / ADAM_C1
    v_hat = v / ADAM_C2
    return -ADAM_LR * (m_hat / (jnp.sqrt(v_hat) + ADAM_EPS) + ADAM_WD * w), m, v


def adamw_small(ws, gs, ms, vs, *, name):
    n = len(ws)

    def body(*refs):
        for i in range(n):
            w_ref, g_ref, m_ref, v_ref = (refs[k * n + i] for k in range(4))
            d_ref, nm_ref, nv_ref = (refs[(4 + k) * n + i] for k in range(3))
            d_ref[...], nm_ref[...], nv_ref[...] = _adamw_update(w_ref[...], g_ref[...], m_ref[...], v_ref[...])

    outs = pl.pallas_call(
        body, name=name, out_shape=[jax.ShapeDtypeStruct(a.shape, f32) for a in ws] * 3)(*ws, *gs, *ms, *vs)
    return outs[:n], outs[n:2 * n], outs[2 * n:]


def _place():
    return lax.axis_index("x"), lax.axis_index("y"), lax.axis_index("c")


def gather_shards(bufs, *, name, split):
    n = len(bufs)

    def body(*refs):
        bufs_ = refs[:n]
        isend, irecv, dsend, drecv = refs[2 * n:]
        x, y, c = _place()
        k = 2 * x + y
        peers = [(1 - x, y, c), (x, 1 - y, c), (1 - x, 1 - y, c)]
        peer_k = [2 * (1 - x) + y, 2 * x + (1 - y), 2 * (1 - x) + (1 - y)]

        def slab(a, q, h):
            if not split[a]:
                return bufs_[a].at[q]
            half = bufs_[a].shape[1] // 2
            return bufs_[a].at[q, pl.ds(pl.multiple_of(h * half, 16), half)]

        def ici(a, j, q):
            return pltpu.make_async_remote_copy(
                src_ref=slab(a, q, c), dst_ref=slab(a, q, c), send_sem=isend.at[3 * a + j], recv_sem=irecv.at[3 * a + j],
                device_id=peers[j], device_id_type=MESH)

        def d2d(a, j, h):
            return pltpu.make_async_remote_copy(
                src_ref=slab(a, peer_k[j], h), dst_ref=slab(a, peer_k[j], h), send_sem=dsend.at[3 * a + j],
                recv_sem=drecv.at[3 * a + j], device_id=(x, y, 1 - c), device_id_type=MESH)

        for a in range(n):
            for j in range(3):
                ici(a, j, k).start()
        for a in range(n):
            for j in range(3):
                ici(a, j, peer_k[j]).wait_recv()
                if split[a]:
                    d2d(a, j, c).start()
        for a in range(n):
            for j in range(3):
                if split[a]:
                    d2d(a, j, 1 - c).wait_recv()
        for a in range(n):
            for j in range(3):
                ici(a, j, k).wait_send()
                if split[a]:
                    d2d(a, j, c).wait_send()

    return pl.pallas_call(
        body, name=name, in_specs=[ANY] * n, out_specs=[ANY] * n,
        out_shape=[jax.ShapeDtypeStruct(b.shape, b.dtype) for b in bufs],
        input_output_aliases={i: i for i in range(n)},
        scratch_shapes=[pltpu.SemaphoreType.DMA((3 * n,))] * 4)(*bufs)


HBM = pl.BlockSpec(memory_space=pltpu.HBM)
SEM = pl.BlockSpec(memory_space=pltpu.SEMAPHORE)
DATAFLOW = pltpu.SideEffectType.DATAFLOW_SIDE_EFFECTING


def _chip_peers():
    x, y, c = _place()
    return 2 * x + y, [(1 - x, y, c), (x, 1 - y, c), (1 - x, 1 - y, c)], [2 * (1 - x) + y, 2 * x + (1 - y), 2 * (1 - x) + (1 - y)]


def _half_slab(ref, q, h):
    half = ref.shape[1] // 2
    return ref.at[q, pl.ds(pl.multiple_of(h * half, BF16_ROWS), half)]


def gather_start(bufs, groups, after, *, name):
    n = len(bufs)
    ng = len(groups)

    def body(*refs):
        ins = refs[:n]
        sends, recvs = refs[2 * n + 1:2 * n + 1 + ng], refs[2 * n + 1 + ng:2 * n + 1 + 2 * ng]
        token = refs[-1]
        c = lax.axis_index("c")
        k, peers, _ = _chip_peers()
        for gi, grp in enumerate(groups):
            for pos, a in enumerate(grp):
                for j in range(3):
                    pltpu.make_async_remote_copy(
                        src_ref=_half_slab(ins[a], k, c), dst_ref=_half_slab(ins[a], k, c), send_sem=sends[gi].at[3 * pos + j],
                        recv_sem=recvs[gi].at[3 * pos + j], device_id=peers[j], device_id_type=MESH).start()
        token[...] = jnp.zeros_like(token)

    sems = [pltpu.SemaphoreType.DMA((3 * len(grp),)) for grp in groups]
    outs = pl.pallas_call(
        body, name=name, in_specs=[HBM] * n + [ANY],
        out_specs=[HBM] * n + [SEM] * (2 * ng) + [pl.BlockSpec(memory_space=pltpu.VMEM)],
        out_shape=[pltpu.HBM(b.shape, b.dtype) for b in bufs] + sems + sems + [jax.ShapeDtypeStruct((8, LANES), f32)],
        input_output_aliases={i: i for i in range(n)},
        compiler_params=pltpu.CompilerParams(has_side_effects=DATAFLOW))(
            *[pltpu.with_memory_space_constraint(b, pltpu.HBM) for b in bufs], after)
    return outs[:n], outs[n:n + ng], outs[n + ng:n + 2 * ng], outs[-1]


def gather_wait(bufs, send_sems, recv_sems, after, *, name):
    n = len(bufs)

    def body(*refs):
        ins = refs[:n]
        send, recv = refs[n], refs[n + 1]
        c = lax.axis_index("c")
        k, peers, peer_k = _chip_peers()
        for a in range(n):
            for j in range(3):
                copy = pltpu.make_async_remote_copy(
                    src_ref=_half_slab(ins[a], k, c), dst_ref=_half_slab(ins[a], peer_k[j], c), send_sem=send.at[3 * a + j],
                    recv_sem=recv.at[3 * a + j], device_id=peers[j], device_id_type=MESH)
                copy.wait_send()
                copy.wait_recv()

    return pl.pallas_call(
        body, name=name, in_specs=[HBM] * n + [SEM, SEM, ANY], out_specs=[HBM] * n,
        out_shape=[pltpu.HBM(b.shape, b.dtype) for b in bufs],
        input_output_aliases={i: i for i in range(n)},
        compiler_params=pltpu.CompilerParams(has_side_effects=DATAFLOW))(*bufs, send_sems, recv_sems, after)


def forward_halves(bufs, *, name):
    n = len(bufs)

    def body(*refs):
        bufs_ = refs[:n]
        send, recv = refs[2 * n:]
        x, y, c = _place()
        _, _, peer_k = _chip_peers()

        def copy(a, j, h):
            return pltpu.make_async_remote_copy(
                src_ref=_half_slab(bufs_[a], peer_k[j], h), dst_ref=_half_slab(bufs_[a], peer_k[j], h),
                send_sem=send.at[3 * a + j], recv_sem=recv.at[3 * a + j], device_id=(x, y, 1 - c), device_id_type=MESH)

        for a in range(n):
            for j in range(3):
                copy(a, j, c).start()
        for a in range(n):
            for j in range(3):
                copy(a, j, 1 - c).wait_recv()
        for a in range(n):
            for j in range(3):
                copy(a, j, c).wait_send()

    return pl.pallas_call(
        body, name=name, in_specs=[ANY] * n, out_specs=[ANY] * n,
        out_shape=[jax.ShapeDtypeStruct(b.shape, b.dtype) for b in bufs],
        input_output_aliases={i: i for i in range(n)},
        scratch_shapes=[pltpu.SemaphoreType.DMA((3 * n,))] * 2)(*bufs)


def _forward_copies(bufs_, send, recv, h):
    x, y, c = _place()
    _, _, peer_k = _chip_peers()
    return [pltpu.make_async_remote_copy(
        src_ref=_half_slab(bufs_[a], peer_k[j], h), dst_ref=_half_slab(bufs_[a], peer_k[j], h),
        send_sem=send.at[3 * a + j], recv_sem=recv.at[3 * a + j], device_id=(x, y, 1 - c), device_id_type=MESH)
        for a in range(len(bufs_)) for j in range(3)]


def forward_start(bufs, after, *, name):
    n = len(bufs)

    def body(*refs):
        for cp in _forward_copies(refs[:n], refs[2 * n + 1], refs[2 * n + 2], lax.axis_index("c")):
            cp.start()
        refs[-1][...] = jnp.zeros_like(refs[-1])

    sems = [pltpu.SemaphoreType.DMA((3 * n,))] * 2
    outs = pl.pallas_call(
        body, name=name, in_specs=[HBM] * n + [ANY],
        out_specs=[HBM] * n + [SEM] * 2 + [pl.BlockSpec(memory_space=pltpu.VMEM)],
        out_shape=[pltpu.HBM(b.shape, b.dtype) for b in bufs] + sems + [jax.ShapeDtypeStruct((8, LANES), f32)],
        input_output_aliases={i: i for i in range(n)},
        compiler_params=pltpu.CompilerParams(has_side_effects=DATAFLOW))(*bufs, after)
    return (n, outs[:-1]), outs[-1]


def forward_wait(state, after, *, name):
    n, held = state

    def body(*refs):
        c = lax.axis_index("c")
        for mine, theirs in zip(_forward_copies(refs[:n], refs[n], refs[n + 1], c),
                                _forward_copies(refs[:n], refs[n], refs[n + 1], 1 - c)):
            mine.wait_send()
            theirs.wait_recv()

    return pl.pallas_call(
        body, name=name, in_specs=[HBM] * n + [SEM] * 2 + [ANY], out_specs=[HBM] * n,
        out_shape=[pltpu.HBM(b.shape, b.dtype) for b in held[:n]],
        input_output_aliases={i: i for i in range(n)},
        compiler_params=pltpu.CompilerParams(has_side_effects=DATAFLOW))(*held, after)


def _sibling_copies(srcs, lands, send, recv):
    x, y, c = _place()
    return [pltpu.make_async_remote_copy(src_ref=srcs[a], dst_ref=lands[a], send_sem=send.at[a], recv_sem=recv.at[a],
                                         device_id=(x, y, 1 - c), device_id_type=MESH) for a in range(len(srcs))]


def sibling_start(arrays, after, *, name):
    n = len(arrays)
    lands = [pltpu.with_memory_space_constraint(lax.empty(a.shape, a.dtype), pltpu.HBM) for a in arrays]

    def body(*refs):
        for cp in _sibling_copies(refs[:n], refs[n:2 * n], refs[4 * n + 1], refs[4 * n + 2]):
            cp.start()
        refs[-1][...] = jnp.zeros_like(refs[-1])

    bufs = list(arrays) + lands
    sems = [pltpu.SemaphoreType.DMA((n,))] * 2
    outs = pl.pallas_call(
        body, name=name, in_specs=[HBM] * (2 * n) + [ANY],
        out_specs=[HBM] * (2 * n) + [SEM] * 2 + [pl.BlockSpec(memory_space=pltpu.VMEM)],
        out_shape=[pltpu.HBM(b.shape, b.dtype) for b in bufs] + sems + [jax.ShapeDtypeStruct((8, LANES), f32)],
        input_output_aliases={i: i for i in range(2 * n)},
        compiler_params=pltpu.CompilerParams(has_side_effects=DATAFLOW))(
            *[pltpu.with_memory_space_constraint(b, pltpu.HBM) for b in bufs], after)
    return (n, outs[:-1]), outs[-1]


def sibling_wait(state, after, *, name):
    n, held = state

    def body(*refs):
        for cp in _sibling_copies(refs[:n], refs[n:2 * n], refs[2 * n], refs[2 * n + 1]):
            cp.wait_send()
            cp.wait_recv()

    outs = pl.pallas_call(
        body, name=name, in_specs=[HBM] * (2 * n) + [SEM] * 2 + [ANY], out_specs=[HBM] * (2 * n),
        out_shape=[pltpu.HBM(b.shape, b.dtype) for b in held[:2 * n]],
        input_output_aliases={i: i for i in range(2 * n)},
        compiler_params=pltpu.CompilerParams(has_side_effects=DATAFLOW))(*held, after)
    return outs[n:]


def sibling_exchange(arrays, *, name):
    n = len(arrays)

    def body(*refs):
        ins, outs = refs[:n], refs[n:2 * n]
        send, recv = refs[2 * n:]
        x, y, c = _place()

        def copy(a):
            return pltpu.make_async_remote_copy(
                src_ref=ins[a], dst_ref=outs[a], send_sem=send.at[a], recv_sem=recv.at[a],
                device_id=(x, y, 1 - c), device_id_type=MESH)

        for a in range(n):
            copy(a).start()
        for a in range(n):
            copy(a).wait_recv()
        for a in range(n):
            copy(a).wait_send()

    return pl.pallas_call(
        body, name=name, in_specs=[ANY] * n, out_specs=[ANY] * n,
        out_shape=[jax.ShapeDtypeStruct(a.shape, a.dtype) for a in arrays],
        scratch_shapes=[pltpu.SemaphoreType.DMA((n,)), pltpu.SemaphoreType.DMA((n,))])(*arrays)


ALL_MASKS = [(mx, my, mc) for mx in (0, 1) for my in (0, 1) for mc in (0, 1)][1:]


def _scatter_copies(srcs, lands, ev, send, recv, esend, erecv):
    x, y, c = _place()
    me = 4 * x + 2 * y + c
    k, peers, peer_k = _chip_peers()
    out = []
    for a in range(len(srcs)):
        for j in range(3):
            out.append(pltpu.make_async_remote_copy(
                src_ref=srcs[a].at[peer_k[j]], dst_ref=lands[a].at[j], send_sem=send.at[3 * a + j],
                recv_sem=recv.at[3 * a + j], device_id=peers[j], device_id_type=MESH))
    start_ev, wait_ev = [], []
    if ev is not None:
        for j, (mx, my, mc) in enumerate(ALL_MASKS):
            peer = (x ^ mx, y ^ my, c ^ mc)
            start_ev.append(pltpu.make_async_remote_copy(
                src_ref=ev.at[me], dst_ref=ev.at[me], send_sem=esend.at[j], recv_sem=erecv.at[j],
                device_id=peer, device_id_type=MESH))
            wait_ev.append(pltpu.make_async_remote_copy(
                src_ref=ev.at[me], dst_ref=ev.at[me ^ (4 * mx + 2 * my + mc)], send_sem=esend.at[j],
                recv_sem=erecv.at[j], device_id=peer, device_id_type=MESH))
    return out, start_ev, wait_ev


def chip_scatter_start(arrays, everyone, after, *, name):
    n = len(arrays)
    ne = 0 if everyone is None else 1
    lands = [pltpu.with_memory_space_constraint(lax.empty((3,) + a.shape[1:], a.dtype), pltpu.HBM) for a in arrays]

    def body(*refs):
        srcs, lands_ = refs[:n], refs[n:2 * n]
        ev = refs[2 * n] if ne else None
        sems = refs[2 * n + ne + 1 + 2 * n + ne:-1]
        send, recv = sems[0], sems[1]
        esend, erecv = (sems[2], sems[3]) if ne else (None, None)
        copies, start_ev, _ = _scatter_copies(srcs, lands_, ev, send, recv, esend, erecv)
        for cp in start_ev + copies:
            cp.start()
        refs[-1][...] = jnp.zeros_like(refs[-1])

    sem_shapes = [pltpu.SemaphoreType.DMA((3 * n,))] * 2 + [pltpu.SemaphoreType.DMA((7,))] * (2 * ne)
    bufs = list(arrays) + lands + ([everyone] if ne else [])
    outs = pl.pallas_call(
        body, name=name, in_specs=[HBM] * len(bufs) + [ANY],
        out_specs=[HBM] * len(bufs) + [SEM] * len(sem_shapes) + [pl.BlockSpec(memory_space=pltpu.VMEM)],
        out_shape=[pltpu.HBM(b.shape, b.dtype) for b in bufs] + sem_shapes + [jax.ShapeDtypeStruct((8, LANES), f32)],
        input_output_aliases={i: i for i in range(len(bufs))},
        compiler_params=pltpu.CompilerParams(has_side_effects=DATAFLOW))(
            *[pltpu.with_memory_space_constraint(b, pltpu.HBM) for b in bufs], after)
    return (n, ne, outs[:-1]), outs[-1]


def chip_scatter_wait(state, after, *, name):
    n, ne, held = state
    nb = 2 * n + ne
    bufs, sems = held[:nb], held[nb:]

    def body(*refs):
        srcs, lands_ = refs[:n], refs[n:2 * n]
        ev = refs[2 * n] if ne else None
        sems_ = refs[nb:nb + len(sems)]
        esend, erecv = (sems_[2], sems_[3]) if ne else (None, None)
        copies, _, wait_ev = _scatter_copies(srcs, lands_, ev, sems_[0], sems_[1], esend, erecv)
        for cp in wait_ev + copies:
            cp.wait_send()
            cp.wait_recv()

    outs = pl.pallas_call(
        body, name=name, in_specs=[HBM] * nb + [SEM] * len(sems) + [ANY], out_specs=[HBM] * nb,
        out_shape=[pltpu.HBM(b.shape, b.dtype) for b in bufs],
        input_output_aliases={i: i for i in range(nb)},
        compiler_params=pltpu.CompilerParams(has_side_effects=DATAFLOW))(*bufs, *sems, after)
    return outs[n:2 * n], (outs[2 * n] if ne else None)


def sibling_merge(bufs, *, name):
    n = len(bufs)

    def body(*refs):
        bufs_ = refs[:n]
        send, recv = refs[2 * n:]
        x, y, c = _place()

        def copy(u, h):
            return pltpu.make_async_remote_copy(
                src_ref=bufs_[u].at[h], dst_ref=bufs_[u].at[h], send_sem=send.at[u], recv_sem=recv.at[u],
                device_id=(x, y, 1 - c), device_id_type=MESH)

        for u in range(n):
            copy(u, c).start()
        for u in range(n):
            copy(u, 1 - c).wait_recv()
        for u in range(n):
            copy(u, c).wait_send()

    return pl.pallas_call(
        body, name=name, in_specs=[ANY] * n, out_specs=[ANY] * n,
        out_shape=[jax.ShapeDtypeStruct(b.shape, b.dtype) for b in bufs],
        input_output_aliases={i: i for i in range(n)},
        scratch_shapes=[pltpu.SemaphoreType.DMA((n,)), pltpu.SemaphoreType.DMA((n,))])(*bufs)


def sum_leading(a, *, name):
    n, r, c = a.shape

    def body(a_ref, o_ref):
        acc = a_ref[0]
        for i in range(1, n):
            acc = acc + a_ref[i]
        o_ref[...] = acc

    rb = r // 2 if r % 16 == 0 else r
    return pl.pallas_call(
        body, name=name, grid=(r // rb,), in_specs=[pl.BlockSpec((n, rb, c), lambda i: (0, i, 0))],
        out_specs=pl.BlockSpec((rb, c), lambda i: (i, 0)), out_shape=jax.ShapeDtypeStruct((r, c), f32),
        compiler_params=_params("parallel"))(a)


def _half_rows(shape):
    return shape[1] // 2 // 2


def rs_cast_other_half(grads, c_arr, *, name):
    n = len(grads)

    def body(c_ref, *refs):
        for i_ref, o_ref in zip(refs[:n], refs[n:]):
            o_ref[...] = i_ref[...].astype(bf16)

    in_specs = [pl.BlockSpec((1, _half_rows(g.shape), g.shape[2]), lambda s, r, c_ref: (s, (1 - c_ref[0]) * 2 + r, 0))
                for g in grads]
    out_specs = [pl.BlockSpec((1, _half_rows(g.shape), g.shape[2]), lambda s, r, c_ref: (s, r, 0)) for g in grads]
    return pl.pallas_call(
        body, name=name,
        grid_spec=pltpu.PrefetchScalarGridSpec(num_scalar_prefetch=1, grid=(N_SHARDS, 2),
                                               in_specs=in_specs, out_specs=out_specs),
        out_shape=[jax.ShapeDtypeStruct((N_SHARDS, g.shape[1] // 2, g.shape[2]), bf16) for g in grads],
        compiler_params=_params("parallel", "parallel"))(c_arr, *grads)


def rs_add_sibling(grads, recvd, ck_arr, *, name):
    n = len(grads)

    def body(ck_ref, *refs):
        s = pl.program_id(1)
        for u in range(n):
            g_ref, r_ref = refs[u], refs[n + u]
            qb_ref, own_ref = refs[2 * n + u], refs[3 * n + u]
            q = g_ref[0] + r_ref[0].astype(f32)
            qb_ref[0] = q.astype(bf16)

            @pl.when(s == ck_ref[1])
            def _(own_ref=own_ref, q=q):
                own_ref[...] = q

    in_specs = [pl.BlockSpec((1, _half_rows(g.shape), g.shape[2]), lambda r, s, ck: (s, ck[0] * 2 + r, 0)) for g in grads]
    in_specs += [pl.BlockSpec((1, _half_rows(g.shape), g.shape[2]), lambda r, s, ck: (s, r, 0)) for g in grads]
    out_specs = [pl.BlockSpec((1, _half_rows(g.shape), g.shape[2]), lambda r, s, ck: (s, r, 0)) for g in grads]
    out_specs += [pl.BlockSpec((_half_rows(g.shape), g.shape[2]), lambda r, s, ck: (r, 0)) for g in grads]
    outs = pl.pallas_call(
        body, name=name,
        grid_spec=pltpu.PrefetchScalarGridSpec(num_scalar_prefetch=1, grid=(2, N_SHARDS),
                                               in_specs=in_specs, out_specs=out_specs),
        out_shape=[jax.ShapeDtypeStruct((N_SHARDS, g.shape[1] // 2, g.shape[2]), bf16) for g in grads]
        + [jax.ShapeDtypeStruct((g.shape[1] // 2, g.shape[2]), f32) for g in grads],
        compiler_params=_params("parallel", "arbitrary"))(ck_arr, *grads, *recvd)
    return outs[:n], outs[n:]


def rs_sum_chips(owns, recvd, ck_arr, *, name):
    n = len(owns)

    def body(ck_ref, *refs):
        for u in range(n):
            own_ref, r_ref, o_ref = refs[u], refs[n + u], refs[2 * n + u]
            o_ref[0] = ((own_ref[...] + r_ref[0].astype(f32)) + r_ref[1].astype(f32)) + r_ref[2].astype(f32)

    in_specs = [pl.BlockSpec((o.shape[0] // 2, o.shape[1]), lambda r, ck: (r, 0)) for o in owns]
    in_specs += [pl.BlockSpec((3, o.shape[0] // 2, o.shape[1]), lambda r, ck: (0, r, 0)) for o in owns]
    out_specs = [pl.BlockSpec((1, o.shape[0] // 2, o.shape[1]), lambda r, ck: (ck[0], r, 0)) for o in owns]
    return pl.pallas_call(
        body, name=name,
        grid_spec=pltpu.PrefetchScalarGridSpec(num_scalar_prefetch=1, grid=(2,), in_specs=in_specs, out_specs=out_specs),
        out_shape=[jax.ShapeDtypeStruct((2,) + o.shape, f32) for o in owns],
        compiler_params=_params("parallel"))(ck_arr, *owns, *recvd)


SMALL = ("a_norm", "a_v_norm", "a_w_s", "a_b_s", "f_norm", "f_conv_w", "f_conv_b", "kv_norm", "k_norm",
         "b_norm", "b_q_norm", "b_sinks")
BIG = ("a_w_in", "a_w_out", "f_w_in", "f_w_out", "w_kv", "b_w_q", "b_w_o")
PACK_COLS = 1024
PACK_ROWS = 8 * N_STEPS


def _pack(parts, rows=PACK_ROWS):
    flat = jnp.concatenate([p.reshape(-1).astype(f32) for p in parts])
    pad = (-flat.shape[0]) % (rows * PACK_COLS)
    return jnp.pad(flat, (0, pad)).reshape(-1, PACK_COLS)


def _unpack(packed, shapes):
    flat = packed.reshape(-1)
    out, off = [], 0
    for s in shapes:
        size = math.prod(s)
        out.append(flat[off:off + size].reshape(s))
        off += size
    return out


def _ffn_fwd(x, g, h, r, w_in4, conv_w, conv_b, f, tag):
    wg, wu = conv_w[:, :f], conv_w[:, f:]
    bg, bu = conv_b[None, :f], conv_b[None, f:]
    pg, pu, gate, up, a = ffn_in_fused(h, w_in4, wg, wu, bg, bu, name=f"ffn{tag}_in")
    return a, (x, g, h, r, pg, pu, gate, up, a, wg, wu)


def _ffn_bwd(dy, saved, w_in4, w_out, c_arr, tag, exchange=False):
    x, g, h, r, pg, pu, gate, up, a, wg, wu = saved
    f = w_out.shape[0]
    d_w_out = mm_tn(a, [dy], c_arr, name=f"ffn{tag}_dwout", n_s=w_out.shape[1], shard_rows=f // N_SHARDS, tki=f // 2)
    dpg, dpu, sg, su = ffn_gate_bwd(dy, w_out, pg, pu, gate, up, wg, wu, name=f"ffn{tag}_dgate")
    d_w_in = mm_tn(h, [dpg, dpu], c_arr, name=f"ffn{tag}_dwin", n_s=w_in4.shape[2], shard_rows=h.shape[1])
    state = None
    if exchange:
        state, token = sibling_start([d_w_in[1], d_w_out[1]], d_w_in[0], name=f"rs_sibling_start_ffn{tag}")
        g = g + token[0, 0]
    dx, dg = mm_nt_rms_bwd([dpg, dpu], w_in4, x, r, g, dy, name=f"ffn{tag}_dh")
    d_conv_w = jnp.concatenate([sg[0:3], su[0:3]], axis=1)
    d_conv_b = jnp.concatenate([sg[3], su[3]], axis=0)
    return dx, dg, d_w_in, d_conv_w, d_conv_b, d_w_out, state


def _rs_front(pairs, sibling_state, after, c_arr, tag):
    units = [full.reshape(N_SHARDS, -1, full.shape[-1]) for full, _ in pairs]
    from_sib = sibling_wait(sibling_state, after, name=f"rs_sibling_wait{tag}")
    return rs_add_sibling(units, from_sib, c_arr, name=f"rs_add{tag}")


def _rs_back(own, from_chips, c_arr, tag):
    halves = rs_sum_chips(list(own), list(from_chips), c_arr, name=f"rs_sum{tag}")
    return [m.reshape(-1, m.shape[2]) for m in sibling_merge(list(halves), name=f"rs_merge{tag}")]


def kernel(x, a_norm, a_w_in, a_v_norm, a_w_s, a_b_s, a_w_out, f_norm, f_w_in, f_conv_w, f_conv_b, f_w_out, kv_norm, w_kv, k_norm, b_norm, b_w_q, b_q_norm, b_sinks, b_w_o, loss_target, m_a_norm, m_a_w_in, m_a_v_norm, m_a_w_s, m_a_b_s, m_a_w_out, m_f_norm, m_f_w_in, m_f_conv_w, m_f_conv_b, m_f_w_out, m_kv_norm, m_w_kv, m_k_norm, m_b_norm, m_b_w_q, m_b_q_norm, m_b_sinks, m_b_w_o, v_a_norm, v_a_w_in, v_a_v_norm, v_a_w_s, v_a_b_s, v_a_w_out, v_f_norm, v_f_w_in, v_f_conv_w, v_f_conv_b, v_f_w_out, v_kv_norm, v_w_kv, v_k_norm, v_b_norm, v_b_w_q, v_b_q_norm, v_b_sinks, v_b_w_o):
    args = dict(locals())
    weights = {n: args[n] for n in SMALL + BIG}
    moms = {n: args["m_" + n] for n in SMALL + BIG}
    vars_ = {n: args["v_" + n] for n in SMALL + BIG}
    t, d = x.shape[1], x.shape[2]
    xi, yi, ci = _place()
    chip = 2 * xi + yi

    big_local = [a_w_in[0], a_w_out[0], f_w_in, f_w_out, w_kv, b_w_q[0], b_w_o[0]]
    c_arr = jnp.stack([ci, chip]).astype(jnp.int32)
    k_arr = jnp.stack([chip]).astype(jnp.int32)
    b_ain, b_aout, b_fin0, b_fin1, b_fout0, b_fout1, b_kv, b_q, b_o = cast_into_slot(big_local, k_arr, name="cast_weights")
    small_cols = _pack([a_norm, a_v_norm, f_conv_w], rows=8)
    b_small = lax.dynamic_update_slice(jnp.zeros((N_SHARDS,) + small_cols.shape, f32), small_cols[None], (chip, 0, 0))
    g_small, w_a_in, g_a_w_out = gather_shards([b_small, b_ain, b_aout], name="gather_first", split=[False, True, True])
    later, send_sems, recv_sems, token = gather_start([b_fin0, b_fout0, b_kv, b_q, b_o, b_fin1, b_fout1],
                                                      [[0], [1], [2, 3, 4], [5, 6]], g_small, name="gather_start")
    ns_cols = a_norm.shape[1]
    nf_cols = f_conv_w.shape[2]
    parts = [_unpack(g_small[k], [a_norm.shape, a_v_norm.shape, f_conv_w.shape]) for k in range(N_SHARDS)]
    a_norm_f = jnp.concatenate([p[0] for p in parts], axis=1) + token[0, 0]
    a_v_norm_f = jnp.concatenate([p[1] for p in parts], axis=1)
    conv_w_f = jnp.concatenate([p[2] for p in parts], axis=2)

    x0 = x[0]
    tril = jnp.tril(jnp.ones((CHUNK, CHUNK), dtype=bool))
    wc = jnp.where(tril[None], a_w_s[0], 0.0).astype(bf16)
    bt = a_b_s[0].T
    kg2 = jnp.tile(k_norm, 2)[None]
    qg2 = jnp.tile(b_q_norm[0], 2)[None]

    (h_a,), r_a = rms_fwd(x0, [a_norm_f], name="a_norm")
    zu = mm_nn(h_a, w_a_in, name="a_in_u", s0=0, ns=2)
    zv = mm_nn(h_a, w_a_in, name="a_in_v", s0=2, ns=2)
    y_a = sgu_gate_fwd(zu, zv, a_v_norm_f, wc, bt, name="a_gate")
    w_a_out = g_a_w_out.reshape(1, -1, d)
    f = f_w_out.shape[1] * N_SHARDS
    fwd0, tok0 = forward_start(gather_wait(later[0:1], send_sems[0], recv_sems[0], y_a, name="gather_wait_0"), y_a,
                               name="gather_forward_start_0")
    x1, (h_f0,), r_f0 = mm_residual(y_a, w_a_out[0], x0, name="a_out", gains=[f_norm[0:1] + tok0[0, 0]])
    (g_fin0,) = forward_wait(fwd0, x1, name="gather_forward_wait_0")
    w_f_in = [g_fin0, None]
    a0, ffn0 = _ffn_fwd(x1, f_norm[0:1], h_f0, r_f0, w_f_in[0], conv_w_f[0], f_conv_b[0], f, "0")
    (g_fout0,) = forward_halves(gather_wait(later[1:2], send_sems[1], recv_sems[1], a0, name="gather_wait_1"),
                                name="gather_forward_1")
    w_f_out = [g_fout0.reshape(-1, d), None]
    fwd1, tok1 = forward_start(gather_wait(later[2:5], send_sems[2], recv_sems[2], g_fout0, name="gather_wait_1b"), a0,
                               name="gather_forward_start_1b")
    x2, (h_k, h_q), r_b = mm_residual(a0, w_f_out[0], x1, name="ffn0_out", gains=[kv_norm[None] + tok1[0, 0], b_norm])
    g_w_kv, g_b_w_q, g_b_w_o = forward_wait(fwd1, x2, name="gather_forward_wait_1b")
    w_kv_f = g_w_kv.reshape(1, d, -1)
    w_q_f = g_b_w_q.reshape(1, d, -1)
    w_o_f = g_b_w_o.reshape(1, -1, d)
    kv = mm_nn(h_k, w_kv_f, name="kv_proj")
    k2, v2 = kv_post_fwd(kv, kg2, name="kv_post")
    qp = mm_nn(h_q, w_q_f, name="q_proj")
    qn = q_norm_fwd(qp, qg2, name="q_norm", scale=HEAD_DIM ** -0.5)
    fwd2, tok2 = forward_start(gather_wait(later[5:7], send_sems[3], recv_sems[3], qn, name="gather_wait_2"), qn,
                               name="gather_forward_start_2")
    o = attn_fwd(qn, k2, v2, b_sinks[0] + tok2[0, 0], name="attn")
    x3, (h_f1,), r_f1 = mm_residual(o, w_o_f[0], x2, name="o_proj", gains=[f_norm[1:2]])
    g_fin1, g_fout1 = forward_wait(fwd2, x3, name="gather_forward_wait_2")
    w_f_in[1] = g_fin1
    w_f_out[1] = g_fout1.reshape(-1, d)
    a1, ffn1 = _ffn_fwd(x3, f_norm[1:2], h_f1, r_f1, w_f_in[1], conv_w_f[1], f_conv_b[1], f, "1")
    dx4, sq = mm_residual(a1, w_f_out[1], x3, name="ffn1_out", target=loss_target[0])
    loss_part = (0.5 * jnp.sum(sq) / d).reshape(1)

    proj_rows = d // N_SHARDS
    dx3, d_fn1, d_fwin1, d_cw1, d_cb1, d_fwout1, _ = _ffn_bwd(dx4, ffn1, w_f_in[1], w_f_out[1], c_arr, "1")
    do = mm_nt([dx3], w_o_f, name="o_proj_dx")
    d_w_o = mm_tn(o, [dx3], c_arr, name="o_proj_dw", n_s=d, shard_rows=o.shape[1] // N_SHARDS)
    dqn, dk2, dv2, dsink = attn_bwd(qn, k2, v2, do, b_sinks[0], name="attn_bwd")
    dqp, dqg = q_norm_bwd(dqn, qp, qg2, name="q_norm_bwd", scale=HEAD_DIM ** -0.5)
    dkv, dkg = kv_post_bwd(dk2, dv2, kv, kg2, name="kv_post_bwd")
    d_w_q = mm_tn(h_q, [dqp], c_arr, name="q_proj_dw", n_s=w_q_f.shape[2], shard_rows=proj_rows)
    d_w_kv = mm_tn(h_k, [dkv], c_arr, name="kv_proj_dw", n_s=w_kv_f.shape[2], shard_rows=proj_rows)
    group1 = [d_fwin1, d_fwout1, d_w_kv, d_w_q, d_w_o]
    sib1, token_s1 = sibling_start([half for _, half in group1], d_w_kv[0], name="rs_sibling_start1")
    dh_k = mm_nt([dkv], w_kv_f, name="kv_proj_dx")
    dx2, d_bn, d_kvn = mm_nt_rms_bwd([dqp], w_q_f, x2, r_b, b_norm + token_s1[0, 0], dx3, name="q_proj_dx",
                                     extra=(dh_k, kv_norm[None]))
    chip_bf1, own1 = _rs_front(group1, sib1, dx2, c_arr, "1")
    scatter1, token1 = chip_scatter_start(list(chip_bf1), None, dx2, name="rs_chips_start1")
    ffn0 = ffn0[:9] + (ffn0[9] + token1[0, 0],) + ffn0[10:]
    dx1, d_fn0, d_fwin0, d_cw0, d_cb0, d_fwout0, sib2 = _ffn_bwd(dx2, ffn0, w_f_in[0], w_f_out[0], c_arr, "0", exchange=True)
    chip_bf2, own2 = _rs_front([d_fwin0, d_fwout0], sib2, dx1, c_arr, "2")
    scatter2, token2 = chip_scatter_start(list(chip_bf2), None, dx1, name="rs_chips_start2")
    a_v_norm_f = a_v_norm_f + token2[0, 0]
    dy_a = mm_nt([dx1], w_a_out, name="a_out_dx")
    d_w_aout = mm_tn(y_a, [dx1], c_arr, name="a_out_dw", n_s=d, shard_rows=y_a.shape[1] // N_SHARDS)
    dzu, dzv, d_avn, d_ws, d_bt = sgu_gate_bwd(zu, zv, dy_a, a_v_norm_f, wc, bt, name="a_gate_bwd")
    d_w_ain = mm_tn(h_a, [dzu, dzv], c_arr, name="a_in_dw", n_s=w_a_in.shape[2], shard_rows=d)
    sib3, token_s3 = sibling_start([d_w_ain[1], d_w_aout[1]], d_w_ain[0], name="rs_sibling_start3")
    dx0, d_an = mm_nt_rms_bwd([dzu, dzv], w_a_in, x0, r_a, a_norm_f + token_s3[0, 0], dx1, name="a_in_dx")
    grad_x = dx0[None]

    chip_bf3, own3 = _rs_front([d_w_ain, d_w_aout], sib3, dx0, c_arr, "3")
    d_fn = jnp.concatenate([d_fn0, d_fn1], axis=0)
    d_cw = jnp.stack([d_cw0, d_cw1])
    d_cb = jnp.stack([d_cb0, d_cb1])
    d_kg = (dkg[0, :HEAD_DIM] + dkg[0, HEAD_DIM:])
    d_qg = (dqg[0, :HEAD_DIM] + dqg[0, HEAD_DIM:])[None]
    small_full = [d_an, d_avn, d_ws[None], d_bt.T[None], d_fn, d_cw, d_cb, d_kvn[0], d_kg, d_bn, d_qg,
                  dsink[:, :N_Q_HEADS], loss_part]
    packed = _pack(small_full)
    me = 4 * xi + 2 * yi + ci
    everyone = lax.dynamic_update_slice(lax.empty((N_DEV,) + packed.shape, f32), packed[None], (me, 0, 0))
    scatter3, token3 = chip_scatter_start(list(chip_bf3), everyone, own3[0], name="rs_chips_start3")
    from_chips1, _ = chip_scatter_wait(scatter1, token3, name="rs_chips_wait1")
    from_chips2, _ = chip_scatter_wait(scatter2, from_chips1[0], name="rs_chips_wait2")
    fin1, fout1, gkv, gq, go, fin0, fout0 = _rs_back(list(own1) + list(own2), list(from_chips1) + list(from_chips2),
                                                     c_arr, "12")
    late = ("f_w_in", "f_w_out", "w_kv", "b_w_q", "b_w_o")
    res_late = adamw([weights[n] for n in late], [[fin0, fin1], [fout0, fout1], [gkv], [gq], [go]],
                     [moms[n] for n in late], [vars_[n] for n in late], name="adamw_late")
    from_chips3, from_all = chip_scatter_wait(scatter3, res_late[1][2], name="rs_chips_wait3")
    ain, aout = _rs_back(own3, from_chips3, c_arr, "3")
    first = ("a_w_in", "a_w_out")
    res_first = adamw([weights[n] for n in first], [[ain], [aout]], [moms[n] for n in first],
                      [vars_[n] for n in first], name="adamw_first")
    big = {n: tuple(r[i] for r in res_late) for i, n in enumerate(late)}
    big.update({n: tuple(r[i] for r in res_first) for i, n in enumerate(first)})

    full_shapes = [g.shape for g in small_full]
    small_g = _unpack(sum_leading(from_all, name="small_sum"), full_shapes)
    loss = small_g.pop()[0]
    small_g[0] = lax.dynamic_slice_in_dim(small_g[0], chip * ns_cols, ns_cols, axis=1)
    small_g[1] = lax.dynamic_slice_in_dim(small_g[1], chip * ns_cols, ns_cols, axis=1)
    small_g[5] = lax.dynamic_slice_in_dim(small_g[5], chip * nf_cols, nf_cols, axis=2)
    small_shapes = [weights[n].shape for n in SMALL]
    small_g = [g.reshape(s) for g, s in zip(small_g, small_shapes)]
    flat2 = [(math.prod(s[:-1]), s[-1]) for s in small_shapes]
    small_d, small_m, small_v = adamw_small(
        *[[a.reshape(s2) for a, s2 in zip(group, flat2)]
          for group in ([weights[n] for n in SMALL], small_g, [moms[n] for n in SMALL], [vars_[n] for n in SMALL])],
        name="adamw_small")
    small_d, small_m, small_v = ([a.reshape(s) for a, s in zip(group, small_shapes)]
                                 for group in (small_d, small_m, small_v))

    out = {}
    for i, n in enumerate(SMALL):
        out[n] = (small_g[i], small_d[i], small_m[i], small_v[i])
    out.update(big)
    order = ["a_norm", "a_w_in", "a_v_norm", "a_w_s", "a_b_s", "a_w_out", "f_norm", "f_w_in", "f_conv_w", "f_conv_b",
             "f_w_out", "kv_norm", "w_kv", "k_norm", "b_norm", "b_w_q", "b_q_norm", "b_sinks", "b_w_o"]
    return (loss, grad_x, *[out[n][0] for n in order], *[out[n][1] for n in order],
            *[out[n][2] for n in order], *[out[n][3] for n in order])
```

```python
import functools
import math

import jax
import jax.numpy as jnp
from jax import lax
from jax.experimental import pallas as pl
from jax.experimental.pallas import tpu as pltpu

f32 = jnp.float32
bf16 = jnp.bfloat16
MESH = pl.DeviceIdType.MESH
ANY = pl.BlockSpec(memory_space=pl.ANY)

EPS = 1e-6
LANES = 128
CHUNK = 128
HEAD_DIM = 64
N_Q_HEADS = 16
N_KV_HEADS = 4
Q_PER_KV = N_Q_HEADS // N_KV_HEADS
N_SHARDS = 4
N_DEV = 8

ADAM_LR = 0.001
ADAM_B1 = 0.9
ADAM_B2 = 0.999
ADAM_EPS = 1e-08
ADAM_WD = 0.01
ADAM_STEP = 10
ADAM_C1 = 1.0 - ADAM_B1 ** ADAM_STEP
ADAM_C2 = 1.0 - ADAM_B2 ** ADAM_STEP

_INV_SQRT2 = 1.0 / math.sqrt(2.0)
_INV_SQRT2PI = 1.0 / math.sqrt(2.0 * math.pi)


def _params(*sem):
    return pltpu.CompilerParams(dimension_semantics=sem)


def _gelu(z):
    return 0.5 * z * (1.0 + lax.erf(z * _INV_SQRT2))


def _gelu_and_grad(z):
    cdf = 0.5 * (1.0 + lax.erf(z * _INV_SQRT2))
    return z * cdf, cdf + z * jnp.exp(-0.5 * z * z) * _INV_SQRT2PI


def _dot(a, b):
    return jnp.dot(a, b, preferred_element_type=f32)


def _dot_nt(a, b):
    return lax.dot_general(a, b, (((1,), (1,)), ((), ())), preferred_element_type=f32)


def _dot_tn(a, b):
    return lax.dot_general(a, b, (((0,), (0,)), ((), ())), preferred_element_type=f32)


def _dot_split(a, b):
    hi = a.astype(bf16)
    lo = (a - hi.astype(f32)).astype(bf16)
    return _dot(hi, b) + _dot(lo, b)


VMEM_TILE_BUDGET = 40 * 1024 * 1024
MAX_ROW_TILE = 2048


def _row_tile(m, fixed_bytes, row_bytes):
    tm = min(m, MAX_ROW_TILE)
    while tm > 256 and 2 * (fixed_bytes + tm * row_bytes) > VMEM_TILE_BUDGET:
        tm //= 2
    return tm


def _isz(a):
    return jnp.dtype(a.dtype).itemsize


def mm_nn(a, w3, *, name, s0=0, ns=None, add=None, out_dtype=f32):
    m, k = a.shape
    s_all, _, n_s = w3.shape
    ns = s_all if ns is None else ns
    tm = _row_tile(m, k * n_s * 2, k * _isz(a) + n_s * jnp.dtype(out_dtype).itemsize + (0 if add is None else n_s * 4))

    def body(*refs):
        if add is None:
            a_ref, w_ref, o_ref = refs
            acc = _dot(a_ref[...].astype(bf16), w_ref[0])
        else:
            a_ref, w_ref, add_ref, o_ref = refs
            acc = _dot(a_ref[...].astype(bf16), w_ref[0]) + add_ref[...]
        o_ref[...] = acc.astype(out_dtype)

    in_specs = [pl.BlockSpec((tm, k), lambda j, i: (i, 0)),
                pl.BlockSpec((1, k, n_s), lambda j, i: (s0 + j, 0, 0))]
    args = [a, w3]
    if add is not None:
        in_specs.append(pl.BlockSpec((tm, n_s), lambda j, i: (i, j)))
        args.append(add)
    return pl.pallas_call(
        body, name=name, grid=(ns, m // tm), in_specs=in_specs,
        out_specs=pl.BlockSpec((tm, n_s), lambda j, i: (i, j)),
        out_shape=jax.ShapeDtypeStruct((m, ns * n_s), out_dtype),
        compiler_params=_params("parallel", "parallel"))(*args)


def mm_nt(a_list, w3, *, name, tko=None, add=None, out_dtype=f32):
    s_all, k_out, n_s = w3.shape
    m = a_list[0].shape[0]
    na = len(a_list)
    spa = s_all // na
    tko = k_out if tko is None else tko
    tm = _row_tile(m, tko * n_s * 2, na * n_s * _isz(a_list[0]) + tko * 4 * (1 if add is None else 2))

    def body(*refs):
        a_refs = refs[:na]
        w_ref = refs[na]
        o_ref = refs[-1]
        s = pl.program_id(2)

        @pl.when(s == 0)
        def _():
            if add is None:
                o_ref[...] = jnp.zeros_like(o_ref)
            else:
                o_ref[...] = refs[na + 1][...]

        for idx in range(na):
            @pl.when(s // spa == idx)
            def _(idx=idx):
                o_ref[...] += _dot_nt(a_refs[idx][...].astype(bf16), w_ref[0])

    def a_map(idx):
        return lambda ko, i, s: (i, jnp.clip(s - idx * spa, 0, spa - 1))

    in_specs = [pl.BlockSpec((tm, n_s), a_map(idx)) for idx in range(na)]
    in_specs.append(pl.BlockSpec((1, tko, n_s), lambda ko, i, s: (s, ko, 0)))
    args = list(a_list) + [w3]
    if add is not None:
        in_specs.append(pl.BlockSpec((tm, tko), lambda ko, i, s: (i, ko)))
        args.append(add)
    return pl.pallas_call(
        body, name=name, grid=(k_out // tko, m // tm, s_all), in_specs=in_specs,
        out_specs=pl.BlockSpec((tm, tko), lambda ko, i, s: (i, ko)),
        out_shape=jax.ShapeDtypeStruct((m, k_out), out_dtype),
        compiler_params=_params("parallel", "parallel", "arbitrary"))(*args)


def mm_tn(a, b_list, c_arr, *, name, n_s, shard_rows, tki=None):
    m, k_in = a.shape
    na = len(b_list)
    s_all = sum(b.shape[1] for b in b_list) // n_s
    spa = s_all // na
    tki = k_in if tki is None else tki
    tm = _row_tile(m, tki * n_s * 4, tki * _isz(a) + na * n_s * _isz(b_list[0]))

    nsteps = m // tm
    per_blk = tki // shard_rows
    half = shard_rows // 2

    def body(c_ref, *refs):
        a_ref = refs[0]
        b_refs = refs[1:1 + na]
        o_ref, ob_ref = refs[-2], refs[-1]
        s = pl.program_id(0)
        r = pl.program_id(2)

        @pl.when(r == 0)
        def _():
            o_ref[...] = jnp.zeros_like(o_ref)

        for idx in range(na):
            @pl.when(s // spa == idx)
            def _(idx=idx):
                o_ref[0] += _dot_tn(a_ref[...].astype(bf16), b_refs[idx][...].astype(bf16))

        @pl.when(r == nsteps - 1)
        def _():
            for q in range(per_blk):
                start = pl.multiple_of(q * shard_rows + (1 - c_ref[0]) * half, 16)
                ob_ref[q] = o_ref[0, pl.ds(start, half), :].astype(bf16)

    def b_map(idx):
        def index(s, ki, r, c_ref):
            active = (s // spa) == idx
            return (jnp.where(active, r, 0), jnp.clip(s - idx * spa, 0, spa - 1))
        return index

    in_specs = [pl.BlockSpec((tm, tki), lambda s, ki, r, c_ref: (r, ki))]
    in_specs += [pl.BlockSpec((tm, n_s), b_map(idx)) for idx in range(na)]
    n_blk = k_in // tki
    return pl.pallas_call(
        body, name=name,
        grid_spec=pltpu.PrefetchScalarGridSpec(
            num_scalar_prefetch=1, grid=(s_all, n_blk, nsteps), in_specs=in_specs,
            out_specs=[pl.BlockSpec((1, tki, n_s), lambda s, ki, r, c_ref: (s, ki, 0)),
                       pl.BlockSpec((per_blk, half, n_s), lambda s, ki, r, c_ref: (s * n_blk + ki, 0, 0))]),
        out_shape=[jax.ShapeDtypeStruct((s_all, k_in, n_s), f32),
                   jax.ShapeDtypeStruct((s_all * k_in // shard_rows, half, n_s), bf16)],
        compiler_params=_params("parallel", "parallel", "arbitrary"))(c_arr, a, *b_list)


def mm_nt_rms_bwd(a_list, w3, x, r, g, dx_in, *, name, extra=None):
    s_all, d, n_s = w3.shape
    m = a_list[0].shape[0]
    na = len(a_list)
    spa = s_all // na
    ne = 0 if extra is None else 1
    tm = _row_tile(m, d * n_s * 2, na * n_s * _isz(a_list[0]) + d * 4 * (4 + ne))

    def body(*refs):
        a_refs, w_ref = refs[:na], refs[na]
        x_ref, r_ref, g_ref, dxin_ref = refs[na + 1:na + 5]
        dh2_ref, g2_ref = (refs[na + 5], refs[na + 6]) if ne else (None, None)
        outs = refs[na + 5 + 2 * ne:]
        dx_ref, dg_ref = outs[0], outs[1]
        dg2_ref = outs[2] if ne else None
        acc_ref = outs[-1]
        i, s = pl.program_id(0), pl.program_id(1)

        @pl.when(s == 0)
        def _():
            acc_ref[...] = jnp.zeros_like(acc_ref)

        for idx in range(na):
            @pl.when(s // spa == idx)
            def _(idx=idx):
                acc_ref[...] += _dot_nt(a_refs[idx][...].astype(bf16), w_ref[0])

        @pl.when(s == s_all - 1)
        def _():
            rv = r_ref[...]
            xh = x_ref[...] * rv
            total = dxin_ref[...]
            pairs = [(acc_ref[...], g_ref, dg_ref)] + ([(dh2_ref[...], g2_ref, dg2_ref)] if ne else [])
            for dh, gain_ref, dgain_ref in pairs:
                part = jnp.sum(dh * xh, axis=0, keepdims=True)

                @pl.when(i == 0)
                def _(dgain_ref=dgain_ref, part=part):
                    dgain_ref[...] = part

                @pl.when(i > 0)
                def _(dgain_ref=dgain_ref, part=part):
                    dgain_ref[...] += part

                tg = dh * gain_ref[...]
                total = total + rv * (tg - xh * jnp.mean(tg * xh, axis=1, keepdims=True))
            dx_ref[...] = total

    def a_map(idx):
        return lambda i, s: (i, jnp.clip(s - idx * spa, 0, spa - 1))

    row = pl.BlockSpec((tm, d), lambda i, s: (i, 0))
    vec = pl.BlockSpec((1, d), lambda i, s: (0, 0))
    in_specs = [pl.BlockSpec((tm, n_s), a_map(idx)) for idx in range(na)]
    in_specs += [pl.BlockSpec((1, d, n_s), lambda i, s: (s, 0, 0)), row, pl.BlockSpec((tm, 1), lambda i, s: (i, 0)), vec, row]
    args = list(a_list) + [w3, x, r, g, dx_in]
    if ne:
        in_specs += [row, vec]
        args += list(extra)
    outs = pl.pallas_call(
        body, name=name, grid=(m // tm, s_all), in_specs=in_specs, out_specs=[row] + [vec] * (1 + ne),
        out_shape=[jax.ShapeDtypeStruct((m, d), f32)] + [jax.ShapeDtypeStruct((1, d), f32)] * (1 + ne),
        scratch_shapes=[pltpu.VMEM((tm, d), f32)],
        compiler_params=_params("arbitrary", "arbitrary"))(*args)
    return outs


def mm_residual(a, w, x, *, name, gains=(), target=None):
    m, k = a.shape
    d = w.shape[1]
    ng = len(gains)
    tm = _row_tile(m, k * d * 2, k * _isz(a) + d * 4 * 3 + ng * d * 2)

    def body(*refs):
        a_ref, w_ref, x_ref = refs[:3]
        y = _dot(a_ref[...].astype(bf16), w_ref[...]) + x_ref[...]
        if target is None:
            g_refs = refs[3:3 + ng]
            y_ref = refs[3 + ng]
            h_refs = refs[4 + ng:4 + 2 * ng]
            r_ref = refs[-1]
            y_ref[...] = y
            r = lax.rsqrt(jnp.mean(y * y, axis=1, keepdims=True) + EPS)
            yh = y * r
            for g_ref, h_ref in zip(g_refs, h_refs):
                h_ref[...] = (yh * g_ref[...]).astype(bf16)
            r_ref[...] = r
        else:
            t_ref, dy_ref, s_ref = refs[3:]
            i = pl.program_id(0)
            e = y - t_ref[...]
            dy_ref[...] = e * (1.0 / d)
            part = jnp.sum(e * e, axis=0, keepdims=True)

            @pl.when(i == 0)
            def _():
                s_ref[...] = part

            @pl.when(i > 0)
            def _():
                s_ref[...] += part

    row = pl.BlockSpec((tm, d), lambda i: (i, 0))
    vec = pl.BlockSpec((1, d), lambda i: (0, 0))
    in_specs = [pl.BlockSpec((tm, k), lambda i: (i, 0)), pl.BlockSpec((k, d), lambda i: (0, 0)), row]
    if target is None:
        outs = pl.pallas_call(
            body, name=name, grid=(m // tm,), in_specs=in_specs + [vec] * ng,
            out_specs=[row] * (1 + ng) + [pl.BlockSpec((tm, 1), lambda i: (i, 0))],
            out_shape=[jax.ShapeDtypeStruct((m, d), f32)] + [jax.ShapeDtypeStruct((m, d), bf16)] * ng
            + [jax.ShapeDtypeStruct((m, 1), f32)],
            compiler_params=_params("parallel"))(a, w, x, *gains)
        return outs[0], outs[1:1 + ng], outs[-1]
    return pl.pallas_call(
        body, name=name, grid=(m // tm,), in_specs=in_specs + [row], out_specs=[row, vec],
        out_shape=[jax.ShapeDtypeStruct((m, d), f32), jax.ShapeDtypeStruct((1, d), f32)],
        compiler_params=_params("arbitrary"))(a, w, x, target)


def rms_fwd(x, gains, *, name, tr=512):
    t, d = x.shape
    tr = min(tr, t)
    ng = len(gains)

    def body(*refs):
        x_ref = refs[0]
        g_refs = refs[1:1 + ng]
        h_refs = refs[1 + ng:1 + 2 * ng]
        r_ref = refs[-1]
        xv = x_ref[...]
        r = lax.rsqrt(jnp.mean(xv * xv, axis=1, keepdims=True) + EPS)
        xh = xv * r
        for g_ref, h_ref in zip(g_refs, h_refs):
            h_ref[...] = (xh * g_ref[...]).astype(bf16)
        r_ref[...] = r

    row = pl.BlockSpec((tr, d), lambda i: (i, 0))
    vec = pl.BlockSpec((1, d), lambda i: (0, 0))
    outs = pl.pallas_call(
        body, name=name, grid=(t // tr,), in_specs=[row] + [vec] * ng,
        out_specs=[row] * ng + [pl.BlockSpec((tr, 1), lambda i: (i, 0))],
        out_shape=[jax.ShapeDtypeStruct((t, d), bf16)] * ng + [jax.ShapeDtypeStruct((t, 1), f32)],
        compiler_params=_params("parallel"))(x, *gains)
    return outs[:ng], outs[ng]


def rms_bwd(dh_list, x, r, gains, dx_in, *, name, tr=512):
    t, d = x.shape
    tr = min(tr, t)
    ng = len(gains)

    def body(*refs):
        dh_refs = refs[:ng]
        x_ref, r_ref = refs[ng], refs[ng + 1]
        g_refs = refs[ng + 2:2 * ng + 2]
        dxin_ref = refs[2 * ng + 2]
        dx_ref = refs[2 * ng + 3]
        dg_refs = refs[2 * ng + 4:]
        i = pl.program_id(0)
        rv = r_ref[...]
        xh = x_ref[...] * rv
        acc = dxin_ref[...]
        for dh_ref, g_ref, dg_ref in zip(dh_refs, g_refs, dg_refs):
            dh = dh_ref[...]
            part = jnp.sum(dh * xh, axis=0, keepdims=True)

            @pl.when(i == 0)
            def _(dg_ref=dg_ref, part=part):
                dg_ref[...] = part

            @pl.when(i > 0)
            def _(dg_ref=dg_ref, part=part):
                dg_ref[...] += part

            tg = dh * g_ref[...]
            acc = acc + rv * (tg - xh * jnp.mean(tg * xh, axis=1, keepdims=True))
        dx_ref[...] = acc

    row = pl.BlockSpec((tr, d), lambda i: (i, 0))
    vec = pl.BlockSpec((1, d), lambda i: (0, 0))
    outs = pl.pallas_call(
        body, name=name, grid=(t // tr,),
        in_specs=[row] * ng + [row, pl.BlockSpec((tr, 1), lambda i: (i, 0))] + [vec] * ng + [row],
        out_specs=[row] + [vec] * ng,
        out_shape=[jax.ShapeDtypeStruct((t, d), f32)] + [jax.ShapeDtypeStruct((1, d), f32)] * ng,
        compiler_params=_params("arbitrary"))(*dh_list, x, r, *gains, dx_in)
    return outs[0], outs[1:]


def sgu_gate_fwd(zu, zv, gv, wc, bt, *, name, tr=512):
    t, w = zu.shape
    tr = min(tr, t)
    groups = w // LANES

    def body(zu_ref, zv_ref, gv_ref, wc_ref, bt_ref, y_ref):
        vp = _gelu(zv_ref[...])
        rv = lax.rsqrt(jnp.mean(vp * vp, axis=1, keepdims=True) + EPS)
        vb = (vp * rv * gv_ref[...]).astype(bf16)
        for c in range(tr // CHUNK):
            rows = slice(c * CHUNK, (c + 1) * CHUNK)
            for g in range(groups):
                cols = slice(g * LANES, (g + 1) * LANES)
                sv = _dot(wc_ref[g], vb[rows, cols]) + bt_ref[:, g:g + 1]
                y_ref[rows, cols] = (_gelu(zu_ref[rows, cols]) * sv).astype(bf16)

    row = pl.BlockSpec((tr, w), lambda i: (i, 0))
    return pl.pallas_call(
        body, name=name, grid=(t // tr,),
        in_specs=[row, row, pl.BlockSpec((1, w), lambda i: (0, 0)),
                  pl.BlockSpec((groups, CHUNK, CHUNK), lambda i: (0, 0, 0)),
                  pl.BlockSpec((CHUNK, groups), lambda i: (0, 0))],
        out_specs=row, out_shape=jax.ShapeDtypeStruct((t, w), bf16),
        compiler_params=_params("parallel"))(zu, zv, gv, wc, bt)


def sgu_gate_bwd(zu, zv, dy, gv, wc, bt, *, name, tr=512):
    t, w = zu.shape
    tr = min(tr, t)
    groups = w // LANES
    nsteps = t // tr

    def body(zu_ref, zv_ref, dy_ref, gv_ref, wc_ref, bt_ref,
             dzu_ref, dzv_ref, dgv_ref, dws_ref, dbt_ref, dv_ref, bacc_ref):
        i = pl.program_id(0)

        @pl.when(i == 0)
        def _():
            dgv_ref[...] = jnp.zeros_like(dgv_ref)
            dws_ref[...] = jnp.zeros_like(dws_ref)
            bacc_ref[...] = jnp.zeros_like(bacc_ref)

        vp, vp_grad = _gelu_and_grad(zv_ref[...])
        rv = lax.rsqrt(jnp.mean(vp * vp, axis=1, keepdims=True) + EPS)
        vhat = vp * rv
        vb = (vhat * gv_ref[...]).astype(bf16)
        for c in range(tr // CHUNK):
            rows = slice(c * CHUNK, (c + 1) * CHUNK)
            for g in range(groups):
                cols = slice(g * LANES, (g + 1) * LANES)
                vblk = vb[rows, cols]
                sv = _dot(wc_ref[g], vblk) + bt_ref[:, g:g + 1]
                zub = zu_ref[rows, cols]
                dyb = dy_ref[rows, cols]
                ub, ub_grad = _gelu_and_grad(zub)
                dzu_ref[rows, cols] = (dyb * sv * ub_grad).astype(bf16)
                dsv = dyb * ub
                bacc_ref[:, cols] += dsv
                dsvb = dsv.astype(bf16)
                dv_ref[rows, cols] = _dot_tn(wc_ref[g], dsvb)
                dws_ref[g] += _dot_nt(dsvb, vblk)
        dv = dv_ref[...]
        dgv_ref[...] += jnp.sum(dv * vhat, axis=0, keepdims=True)
        tg = dv * gv_ref[...]
        dvp = rv * (tg - vhat * jnp.mean(tg * vhat, axis=1, keepdims=True))
        dzv_ref[...] = (dvp * vp_grad).astype(bf16)

        @pl.when(i == nsteps - 1)
        def _():
            tt = lax.broadcasted_iota(jnp.int32, (CHUNK, CHUNK), 0)
            ss = lax.broadcasted_iota(jnp.int32, (CHUNK, CHUNK), 1)
            for g in range(groups):
                dws_ref[g] = jnp.where(ss <= tt, dws_ref[g], 0.0)
                dbt_ref[:, g:g + 1] = jnp.sum(bacc_ref[:, g * LANES:(g + 1) * LANES], axis=1, keepdims=True)

    row = pl.BlockSpec((tr, w), lambda i: (i, 0))
    full3 = pl.BlockSpec((groups, CHUNK, CHUNK), lambda i: (0, 0, 0))
    return pl.pallas_call(
        body, name=name, grid=(nsteps,),
        in_specs=[row, row, row, pl.BlockSpec((1, w), lambda i: (0, 0)), full3,
                  pl.BlockSpec((CHUNK, groups), lambda i: (0, 0))],
        out_specs=[row, row, pl.BlockSpec((1, w), lambda i: (0, 0)), full3,
                   pl.BlockSpec((CHUNK, groups), lambda i: (0, 0))],
        out_shape=[jax.ShapeDtypeStruct((t, w), bf16), jax.ShapeDtypeStruct((t, w), bf16),
                   jax.ShapeDtypeStruct((1, w), f32), jax.ShapeDtypeStruct((groups, CHUNK, CHUNK), f32),
                   jax.ShapeDtypeStruct((CHUNK, groups), f32)],
        scratch_shapes=[pltpu.VMEM((tr, w), f32), pltpu.VMEM((CHUNK, w), f32)],
        compiler_params=_params("arbitrary"))(zu, zv, dy, gv, wc, bt)


HALO = 8


def _shift_down(v, halo, k, first):
    r = pltpu.roll(v, k, 0)
    hh = jnp.where(first, 0.0, pltpu.roll(halo, k, 0))
    rid = lax.broadcasted_iota(jnp.int32, (HALO, v.shape[1]), 0)
    head = jnp.where(rid < k, hh, r[0:HALO])
    if v.shape[0] == HALO:
        return head
    return jnp.concatenate([head, r[HALO:]], axis=0)


def _shift_up(v, halo, k, last):
    n = v.shape[0]
    r = pltpu.roll(v, n - k, 0)
    hh = jnp.where(last, 0.0, pltpu.roll(halo, HALO - k, 0))
    rid = lax.broadcasted_iota(jnp.int32, (HALO, v.shape[1]), 0)
    tail = jnp.where(rid >= HALO - k, hh, r[n - HALO:])
    return jnp.concatenate([r[:n - HALO], tail], axis=0)


def _conv(p, halo, w_ref, b_ref, first):
    return (w_ref[2:3, :] * p + w_ref[1:2, :] * _shift_down(p, halo, 1, first)
            + w_ref[0:1, :] * _shift_down(p, halo, 2, first) + b_ref[...])


BF16_ROWS = 16


def ffn_in_fused(h, w_in4, wg, wu, bg, bu, *, name):
    t, k = h.shape
    s_all, _, n_s = w_in4.shape
    half = s_all // 2
    tm = _row_tile(t, 2 * k * n_s * 2, k * 2 + 2 * n_s * 4 + n_s * 2)

    def body(h_ref, hh_ref, wg_ref, wu_ref, cg_ref, cu_ref, bg_ref, bu_ref, gate_ref, up_ref, a_ref):
        first = pl.program_id(1) == 0
        hv, hh = h_ref[...], hh_ref[...]
        outs = []
        for w_ref, c_ref, b_ref, o_ref in ((wg_ref, cg_ref, bg_ref, gate_ref), (wu_ref, cu_ref, bu_ref, up_ref)):
            hu = _conv(_dot(hv, w_ref[0]), _dot(hh, w_ref[0])[BF16_ROWS - HALO:], c_ref, b_ref, first)
            o_ref[...] = hu
            outs.append(hu)
        gate, up = outs
        a_ref[...] = (gate * jax.nn.sigmoid(gate) * up).astype(bf16)

    tile = pl.BlockSpec((tm, n_s), lambda j, i: (i, j))
    cw = pl.BlockSpec((3, n_s), lambda j, i: (0, j))
    cb = pl.BlockSpec((1, n_s), lambda j, i: (0, j))
    f = half * n_s
    return pl.pallas_call(
        body, name=name, grid=(half, t // tm),
        in_specs=[pl.BlockSpec((tm, k), lambda j, i: (i, 0)),
                  pl.BlockSpec((BF16_ROWS, k), lambda j, i: (jnp.maximum(i * (tm // BF16_ROWS) - 1, 0), 0)),
                  pl.BlockSpec((1, k, n_s), lambda j, i: (j, 0, 0)),
                  pl.BlockSpec((1, k, n_s), lambda j, i: (j + half, 0, 0)), cw, cw, cb, cb],
        out_specs=[tile] * 3,
        out_shape=[jax.ShapeDtypeStruct((t, f), f32)] * 2 + [jax.ShapeDtypeStruct((t, f), bf16)],
        compiler_params=_params("parallel", "parallel"))(h, h, w_in4, w_in4, wg, wu, bg, bu)


def _gate_grads(gate, up, dav):
    sg = jax.nn.sigmoid(gate)
    return dav * up * (sg * (1.0 + gate * (1.0 - sg))), dav * gate * sg


GATE_BWD_ROWS = 512


def ffn_gate_bwd(dy, w_out, h, w_in4, gate, up, wg, wu, *, name):
    t, f = gate.shape
    d = dy.shape[1]
    tr = min(GATE_BWD_ROWS, t)
    nsteps = t // tr
    tc = f // 2

    def body(dy_ref, dyn_ref, w_ref, h_ref, wig_ref, wiu_ref, gate_ref, gaten_ref, up_ref, upn_ref, wg_ref, wu_ref,
             dg_ref, du_ref, sg_ref, su_ref):
        i = pl.program_id(1)
        last = i == nsteps - 1
        w = w_ref[0]
        da = _dot_nt(dy_ref[...].astype(bf16), w)
        da_n = _dot_nt(dyn_ref[...].astype(bf16), w)
        dgate, dup = _gate_grads(gate_ref[...], up_ref[...], da)
        dgate_n, dup_n = _gate_grads(gaten_ref[...], upn_ref[...], da_n)
        rid = lax.broadcasted_iota(jnp.int32, (8, tc), 0)
        hv = h_ref[...]
        for dd, d_n, c_ref, wi_ref, o_ref, s_ref in ((dgate, dgate_n, wg_ref, wig_ref, dg_ref, sg_ref),
                                                     (dup, dup_n, wu_ref, wiu_ref, du_ref, su_ref)):
            d1, d2 = _shift_up(dd, d_n, 1, last), _shift_up(dd, d_n, 2, last)
            o_ref[...] = (c_ref[2:3, :] * dd + c_ref[1:2, :] * d1 + c_ref[0:1, :] * d2).astype(bf16)
            p = _dot(hv, wi_ref[0])
            sums = [jnp.sum(d2 * p, axis=0, keepdims=True), jnp.sum(d1 * p, axis=0, keepdims=True),
                    jnp.sum(dd * p, axis=0, keepdims=True), jnp.sum(dd, axis=0, keepdims=True)]
            part = jnp.zeros((8, tc), f32)
            for k, sk in enumerate(sums):
                part = jnp.where(rid == k, sk, part)

            @pl.when(i == 0)
            def _(s_ref=s_ref, part=part):
                s_ref[...] = part

            @pl.when(i > 0)
            def _(s_ref=s_ref, part=part):
                s_ref[...] += part

    def nxt_rows(j, i):
        return (jnp.minimum((i + 1) * (tr // HALO), t // HALO - 1), j)

    tile = pl.BlockSpec((tr, tc), lambda j, i: (i, j))
    nxt = pl.BlockSpec((HALO, tc), nxt_rows)
    wspec = pl.BlockSpec((3, tc), lambda j, i: (0, j))
    stat = pl.BlockSpec((8, tc), lambda j, i: (0, j))
    return pl.pallas_call(
        body, name=name, grid=(2, nsteps),
        in_specs=[pl.BlockSpec((tr, d), lambda j, i: (i, 0)),
                  pl.BlockSpec((HALO, d), lambda j, i: (nxt_rows(j, i)[0], 0)),
                  pl.BlockSpec((1, tc, d), lambda j, i: (j, 0, 0)),
                  pl.BlockSpec((tr, d), lambda j, i: (i, 0)),
                  pl.BlockSpec((1, d, tc), lambda j, i: (j, 0, 0)),
                  pl.BlockSpec((1, d, tc), lambda j, i: (j + 2, 0, 0)),
                  tile, nxt, tile, nxt, wspec, wspec],
        out_specs=[tile, tile, stat, stat],
        out_shape=[jax.ShapeDtypeStruct((t, f), bf16), jax.ShapeDtypeStruct((t, f), bf16),
                   jax.ShapeDtypeStruct((8, f), f32), jax.ShapeDtypeStruct((8, f), f32)],
        compiler_params=_params("parallel", "arbitrary"))(
            dy, dy, w_out.reshape(2, tc, d), h, w_in4, w_in4, gate, gate, up, up, wg, wu)


def _head_mean_matrix():
    i = lax.broadcasted_iota(jnp.int32, (LANES, LANES), 0) // HEAD_DIM
    j = lax.broadcasted_iota(jnp.int32, (LANES, LANES), 1) // HEAD_DIM
    return jnp.where(i == j, 1.0 / HEAD_DIM, 0.0).astype(bf16)


def _lane_half(shape):
    return (lax.broadcasted_iota(jnp.int32, shape, 1) % LANES) // HEAD_DIM


def q_norm_fwd(qp, g2, *, name, scale, tr=512):
    t, w = qp.shape
    tr = min(tr, t)

    def body(x_ref, g_ref, o_ref):
        bd = _head_mean_matrix()
        for cb in range(w // LANES):
            cols = slice(cb * LANES, (cb + 1) * LANES)
            xc = x_ref[:, cols]
            rh = lax.rsqrt(_dot_split(xc * xc, bd) + EPS)
            o_ref[:, cols] = (xc * rh * g_ref[...] * scale).astype(bf16)

    row = pl.BlockSpec((tr, w), lambda i: (i, 0))
    return pl.pallas_call(
        body, name=name, grid=(t // tr,), in_specs=[row, pl.BlockSpec((1, LANES), lambda i: (0, 0))],
        out_specs=row, out_shape=jax.ShapeDtypeStruct((t, w), bf16),
        compiler_params=_params("parallel"))(qp, g2)


def q_norm_bwd(dq, qp, g2, *, name, scale, tr=512):
    t, w = qp.shape
    tr = min(tr, t)

    def body(dq_ref, x_ref, g_ref, o_ref, dg_ref):
        i = pl.program_id(0)
        bd = _head_mean_matrix()
        acc = jnp.zeros((1, LANES), f32)
        for cb in range(w // LANES):
            cols = slice(cb * LANES, (cb + 1) * LANES)
            xc = x_ref[:, cols]
            rh = lax.rsqrt(_dot_split(xc * xc, bd) + EPS)
            xh = xc * rh
            dy = dq_ref[:, cols] * scale
            acc = acc + jnp.sum(dy * xh, axis=0, keepdims=True)
            tg = dy * g_ref[...]
            o_ref[:, cols] = (rh * (tg - xh * _dot_split(tg * xh, bd))).astype(bf16)

        @pl.when(i == 0)
        def _():
            dg_ref[...] = acc

        @pl.when(i > 0)
        def _():
            dg_ref[...] += acc

    row = pl.BlockSpec((tr, w), lambda i: (i, 0))
    vec = pl.BlockSpec((1, LANES), lambda i: (0, 0))
    return pl.pallas_call(
        body, name=name, grid=(t // tr,), in_specs=[row, row, vec], out_specs=[row, vec],
        out_shape=[jax.ShapeDtypeStruct((t, w), bf16), jax.ShapeDtypeStruct((1, LANES), f32)],
        compiler_params=_params("arbitrary"))(dq, qp, g2)


def kv_post_fwd(kv, g2, *, name, tr=512):
    t, w = kv.shape
    tr = min(tr, t)
    kw = w // 2

    def body(x_ref, g_ref, k_ref, v_ref):
        bd = _head_mean_matrix()
        half = _lane_half((tr, LANES))
        for cb in range(kw // LANES):
            xc = x_ref[:, cb * LANES:(cb + 1) * LANES]
            rh = lax.rsqrt(_dot_split(xc * xc, bd) + EPS)
            kn = xc * rh * g_ref[...]
            vc = x_ref[:, kw + cb * LANES:kw + (cb + 1) * LANES]
            for src, dst in ((kn, k_ref), (vc, v_ref)):
                sw = pltpu.roll(src, HEAD_DIM, 1)
                for hf in range(2):
                    blk = 2 * cb + hf
                    dst[:, blk * LANES:(blk + 1) * LANES] = jnp.where(half == hf, src, sw).astype(bf16)

    return pl.pallas_call(
        body, name=name, grid=(t // tr,),
        in_specs=[pl.BlockSpec((tr, w), lambda i: (i, 0)), pl.BlockSpec((1, LANES), lambda i: (0, 0))],
        out_specs=[pl.BlockSpec((tr, 2 * kw), lambda i: (i, 0))] * 2,
        out_shape=[jax.ShapeDtypeStruct((t, 2 * kw), bf16)] * 2,
        compiler_params=_params("parallel"))(kv, g2)


def kv_post_bwd(dk2, dv2, kv, g2, *, name, tr=512):
    t, w = kv.shape
    tr = min(tr, t)
    kw = w // 2

    def body(dk_ref, dv_ref, x_ref, g_ref, o_ref, dg_ref):
        i = pl.program_id(0)
        bd = _head_mean_matrix()
        half = _lane_half((tr, LANES))
        acc = jnp.zeros((1, LANES), f32)

        def fold(ref, cb):
            a = ref[:, (2 * cb) * LANES:(2 * cb + 1) * LANES]
            b = ref[:, (2 * cb + 1) * LANES:(2 * cb + 2) * LANES]
            return jnp.where(half == 0, a + pltpu.roll(a, HEAD_DIM, 1), b + pltpu.roll(b, HEAD_DIM, 1))

        for cb in range(kw // LANES):
            cols = slice(cb * LANES, (cb + 1) * LANES)
            xc = x_ref[:, cols]
            rh = lax.rsqrt(_dot_split(xc * xc, bd) + EPS)
            xh = xc * rh
            dy = fold(dk_ref, cb)
            acc = acc + jnp.sum(dy * xh, axis=0, keepdims=True)
            tg = dy * g_ref[...]
            o_ref[:, cols] = (rh * (tg - xh * _dot_split(tg * xh, bd))).astype(bf16)
            o_ref[:, kw + cb * LANES:kw + (cb + 1) * LANES] = fold(dv_ref, cb).astype(bf16)

        @pl.when(i == 0)
        def _():
            dg_ref[...] = acc

        @pl.when(i > 0)
        def _():
            dg_ref[...] += acc

    dup = pl.BlockSpec((tr, 2 * kw), lambda i: (i, 0))
    row = pl.BlockSpec((tr, w), lambda i: (i, 0))
    vec = pl.BlockSpec((1, LANES), lambda i: (0, 0))
    return pl.pallas_call(
        body, name=name, grid=(t // tr,), in_specs=[dup, dup, row, vec], out_specs=[row, vec],
        out_shape=[jax.ShapeDtypeStruct((t, w), bf16), jax.ShapeDtypeStruct((1, LANES), f32)],
        compiler_params=_params("arbitrary"))(dk2, dv2, kv, g2)


def _slope(h):
    return 2.0 ** (-8.0 * (h + 1) / N_Q_HEADS)


GROUP_ROWS = Q_PER_KV * CHUNK


def _band_mask(n):
    tq = lax.broadcasted_iota(jnp.int32, (GROUP_ROWS, 2 * CHUNK), 0) % CHUNK
    jk = lax.broadcasted_iota(jnp.int32, (GROUP_ROWS, 2 * CHUNK), 1)
    dist = tq + CHUNK - jk
    ok = (dist >= 0) & (dist < CHUNK) & jnp.logical_not((n == 0) & (jk < CHUNK))
    return dist.astype(f32), ok


def _band(ref, n, kh):
    p0 = pl.multiple_of(jnp.maximum(n - 1, 0) * CHUNK, CHUNK)
    c0 = pl.multiple_of(n * CHUNK, CHUNK)
    cols = slice(kh * LANES, (kh + 1) * LANES)
    return jnp.concatenate([ref[pl.ds(p0, CHUNK), cols], ref[pl.ds(c0, CHUNK), cols]], axis=0)


def _stack_heads(ref, kh, half):
    parts = []
    for cb in (2 * kh, 2 * kh + 1):
        xc = ref[:, cb * LANES:(cb + 1) * LANES].astype(f32)
        parts += [jnp.where(half == hf, xc, 0.0).astype(bf16) for hf in range(2)]
    return jnp.concatenate(parts, axis=0)


def _unstack_heads(x4, half):
    return (jnp.where(half == 0, x4[0:CHUNK], x4[CHUNK:2 * CHUNK]),
            jnp.where(half == 0, x4[2 * CHUNK:3 * CHUNK], x4[3 * CHUNK:]))


def _per_head_column(kh, values):
    grp = lax.broadcasted_iota(jnp.int32, (GROUP_ROWS, 1), 0) // CHUNK
    col = jnp.full((GROUP_ROWS, 1), values[0], f32)
    for g in range(1, Q_PER_KV):
        col = jnp.where(grp == g, values[g], col)
    return col


def _softmax_band(q4, kband, dist, ok, slope, sink):
    s = _dot_nt(q4, kband)
    s = jnp.where(ok, s - slope * dist, -jnp.inf)
    m = jnp.maximum(jnp.max(s, axis=1, keepdims=True), sink)
    e = jnp.exp(s - m)
    es = jnp.exp(sink - m)
    den = jnp.sum(e, axis=1, keepdims=True) + es
    return e / den, es / den


def attn_fwd(q, k2, v2, sinks, *, name):
    t, w = q.shape
    nb = t // CHUNK

    def body(sink_ref, q_ref, k_ref, v_ref, o_ref):
        n = pl.program_id(0)
        dist, ok = _band_mask(n)
        half = _lane_half((CHUNK, LANES))
        for kh in range(N_KV_HEADS):
            heads = [Q_PER_KV * kh + g for g in range(Q_PER_KV)]
            slope = _per_head_column(kh, [_slope(h) for h in heads])
            sink = _per_head_column(kh, [sink_ref[h] for h in heads])
            q4 = _stack_heads(q_ref, kh, half)
            p, _ = _softmax_band(q4, _band(k_ref, n, kh), dist, ok, slope, sink)
            o4 = _dot(p.astype(bf16), _band(v_ref, n, kh))
            lo, hi = _unstack_heads(o4, half)
            o_ref[:, (2 * kh) * LANES:(2 * kh + 1) * LANES] = lo.astype(bf16)
            o_ref[:, (2 * kh + 1) * LANES:(2 * kh + 2) * LANES] = hi.astype(bf16)

    full = pl.BlockSpec((t, k2.shape[1]), lambda n: (0, 0))
    return pl.pallas_call(
        body, name=name, grid=(nb,),
        in_specs=[pl.BlockSpec(memory_space=pltpu.SMEM), pl.BlockSpec((CHUNK, w), lambda n: (n, 0)), full, full],
        out_specs=pl.BlockSpec((CHUNK, w), lambda n: (n, 0)),
        out_shape=jax.ShapeDtypeStruct((t, w), bf16),
        compiler_params=_params("parallel"))(sinks, q, k2, v2)


def attn_bwd(q, k2, v2, do, sinks, *, name):
    t, w = q.shape
    nb = t // CHUNK
    kw = k2.shape[1]

    def body(sink_ref, q_ref, k_ref, v_ref, do_ref, dq_ref, dk_ref, dv_ref, ds_ref, kc_ref, vc_ref):
        n = pl.program_id(0)

        @pl.when(n == 0)
        def _():
            ds_ref[...] = jnp.zeros_like(ds_ref)
            kc_ref[...] = jnp.zeros_like(kc_ref)
            vc_ref[...] = jnp.zeros_like(vc_ref)
            dk_ref[...] = jnp.zeros_like(dk_ref)
            dv_ref[...] = jnp.zeros_like(dv_ref)

        @pl.when(n == nb)
        def _():
            dk_ref[...] = kc_ref[...]
            dv_ref[...] = vc_ref[...]

        @pl.when(n < nb)
        def _():
            dist, ok = _band_mask(n)
            half = _lane_half((CHUNK, LANES))
            lane = lax.broadcasted_iota(jnp.int32, (1, LANES), 1)
            sink_acc = jnp.zeros((1, LANES), f32)
            for kh in range(N_KV_HEADS):
                heads = [Q_PER_KV * kh + g for g in range(Q_PER_KV)]
                slope = _per_head_column(kh, [_slope(h) for h in heads])
                sink = _per_head_column(kh, [sink_ref[h] for h in heads])
                q4 = _stack_heads(q_ref, kh, half)
                do4 = _stack_heads(do_ref, kh, half)
                kband = _band(k_ref, n, kh)
                vband = _band(v_ref, n, kh)
                p, ps = _softmax_band(q4, kband, dist, ok, slope, sink)
                dp = _dot_nt(do4, vband)
                delta = jnp.sum(p * dp, axis=1, keepdims=True)
                dsb = (p * (dp - delta)).astype(bf16)
                sd = ps * delta
                for g, h in enumerate(heads):
                    part = jnp.sum(sd[g * CHUNK:(g + 1) * CHUNK], axis=0, keepdims=True)
                    sink_acc = sink_acc + jnp.where(lane == h, -part, 0.0)
                lo, hi = _unstack_heads(_dot(dsb, kband), half)
                dq_ref[:, (2 * kh) * LANES:(2 * kh + 1) * LANES] = lo
                dq_ref[:, (2 * kh + 1) * LANES:(2 * kh + 2) * LANES] = hi
                dkb = _dot_tn(dsb, q4)
                dvb = _dot_tn(p.astype(bf16), do4)
                cols = slice(kh * LANES, (kh + 1) * LANES)
                dk_ref[:, cols] = kc_ref[:, cols] + dkb[0:CHUNK]
                dv_ref[:, cols] = vc_ref[:, cols] + dvb[0:CHUNK]
                kc_ref[:, cols] = dkb[CHUNK:]
                vc_ref[:, cols] = dvb[CHUNK:]
            ds_ref[...] += sink_acc

    full = pl.BlockSpec((t, kw), lambda n: (0, 0))
    qblk = pl.BlockSpec((CHUNK, w), lambda n: (jnp.minimum(n, nb - 1), 0))
    kblk = pl.BlockSpec((CHUNK, kw), lambda n: (jnp.maximum(n - 1, 0), 0))
    return pl.pallas_call(
        body, name=name, grid=(nb + 1,),
        in_specs=[pl.BlockSpec(memory_space=pltpu.SMEM), qblk, full, full, qblk],
        out_specs=[qblk, kblk, kblk, pl.BlockSpec((1, LANES), lambda n: (0, 0))],
        out_shape=[jax.ShapeDtypeStruct((t, w), f32), jax.ShapeDtypeStruct((t, kw), f32),
                   jax.ShapeDtypeStruct((t, kw), f32), jax.ShapeDtypeStruct((1, LANES), f32)],
        scratch_shapes=[pltpu.VMEM((CHUNK, kw), f32), pltpu.VMEM((CHUNK, kw), f32)],
        compiler_params=_params("arbitrary"))(sinks, q, k2, v2, do)


def loss_head(y, target, *, name, tr=512):
    t, d = y.shape
    tr = min(tr, t)

    def body(y_ref, t_ref, dy_ref, s_ref):
        i = pl.program_id(0)
        e = y_ref[...] - t_ref[...]
        dy_ref[...] = e * (1.0 / d)
        part = jnp.sum(e * e, axis=0, keepdims=True)

        @pl.when(i == 0)
        def _():
            s_ref[...] = part

        @pl.when(i > 0)
        def _():
            s_ref[...] += part

    row = pl.BlockSpec((tr, d), lambda i: (i, 0))
    vec = pl.BlockSpec((1, d), lambda i: (0, 0))
    return pl.pallas_call(
        body, name=name, grid=(t // tr,), in_specs=[row, row], out_specs=[row, vec],
        out_shape=[jax.ShapeDtypeStruct((t, d), f32), jax.ShapeDtypeStruct((1, d), f32)],
        compiler_params=_params("arbitrary"))(y, target)


N_STEPS = 8


def _row_blocks(shape):
    if len(shape) == 2:
        r, c = shape
        return (r // N_STEPS, c), (lambda s: (s, 0))
    l, r, c = shape
    per = N_STEPS // l
    return (1, r // per, c), (lambda s: (s // per, s % per, 0))


CAST_STEPS = 4


def cast_into_slot(arrays, k_arr, *, name):
    in_specs, out_specs, out_shape, layers = [], [], [], []
    for a in arrays:
        r, c = a.shape[-2:]
        rb = r // CAST_STEPS
        if a.ndim == 2:
            in_specs.append(pl.BlockSpec((rb, c), lambda s, k: (s, 0)))
            layers.append(None)
        else:
            for l in range(a.shape[0]):
                in_specs.append(pl.BlockSpec((1, rb, c), lambda s, k, l=l: (l, s, 0)))
                layers.append(l)
        for _ in range(1 if a.ndim == 2 else a.shape[0]):
            out_specs.append(pl.BlockSpec((1, rb, c), lambda s, k: (k[0], s, 0)))
            out_shape.append(jax.ShapeDtypeStruct((N_SHARDS, r, c), bf16))
    n = len(in_specs)

    def body(k_ref, *refs):
        for i_ref, o_ref, l in zip(refs[:n], refs[n:], layers):
            o_ref[0] = (i_ref[...] if l is None else i_ref[0]).astype(bf16)

    args = []
    for a in arrays:
        args += [a] * (1 if a.ndim == 2 else a.shape[0])
    return pl.pallas_call(
        body, name=name,
        grid_spec=pltpu.PrefetchScalarGridSpec(num_scalar_prefetch=1, grid=(CAST_STEPS,),
                                               in_specs=in_specs, out_specs=out_specs),
        out_shape=out_shape, compiler_params=_params("parallel"))(k_arr, *args)


def adamw(ws, gs, ms, vs, *, name):
    n = len(ws)
    specs, g_specs, g_count = [], [], []
    for w, g_list in zip(ws, gs):
        blk, index = _row_blocks(w.shape)
        specs.append(pl.BlockSpec(blk, index))
        layers = len(g_list)
        per = N_STEPS // layers
        g_count.append(layers)
        for l in range(layers):
            g_specs.append(pl.BlockSpec(blk[-2:], lambda s, l=l, per=per: (jnp.where(s // per == l, s % per, 0), 0)))
    ng = len(g_specs)

    def body(*refs):
        s = pl.program_id(0)
        g_refs = refs[3 * n:3 * n + ng]
        outs = refs[3 * n + ng:]
        off = 0
        for i in range(n):
            w_ref, m_ref, v_ref = refs[i], refs[n + i], refs[2 * n + i]
            go_ref, d_ref, nm_ref, nv_ref = (outs[k * n + i] for k in range(4))
            layers = g_count[i]
            g = g_refs[off][...]
            for l in range(1, layers):
                g = jnp.where(s // (N_STEPS // layers) == l, g_refs[off + l][...], g)
            off += layers
            g = g.reshape(w_ref.shape)
            m = ADAM_B1 * m_ref[...] + (1.0 - ADAM_B1) * g
            v = ADAM_B2 * v_ref[...] + (1.0 - ADAM_B2) * (g * g)
            m_hat = m / ADAM_C1
            v_hat = v / ADAM_C2
            go_ref[...] = g
            d_ref[...] = -ADAM_LR * (m_hat / (jnp.sqrt(v_hat) + ADAM_EPS) + ADAM_WD * w_ref[...])
            nm_ref[...] = m
            nv_ref[...] = v

    outs = pl.pallas_call(
        body, name=name, grid=(N_STEPS,), in_specs=specs * 3 + g_specs, out_specs=specs * 4,
        out_shape=[jax.ShapeDtypeStruct(a.shape, f32) for a in ws] * 4,
        compiler_params=_params("parallel"))(*ws, *ms, *vs, *[g for g_list in gs for g in g_list])
    return [outs[k * n:(k + 1) * n] for k in range(4)]


def _adamw_update(w, g, m, v):
    m = ADAM_B1 * m + (1.0 - ADAM_B1) * g
    v = ADAM_B2 * v + (1.0 - ADAM_B2) * (g * g)
    m_hat = m / ADAM_C1
    v_hat = v / ADAM_C2
    return -ADAM_LR * (m_hat / (jnp.sqrt(v_hat) + ADAM_EPS) + ADAM_WD * w), m, v


def adamw_small(ws, gs, ms, vs, *, name):
    n = len(ws)

    def body(*refs):
        for i in range(n):
            w_ref, g_ref, m_ref, v_ref = (refs[k * n + i] for k in range(4))
            d_ref, nm_ref, nv_ref = (refs[(4 + k) * n + i] for k in range(3))
            d_ref[...], nm_ref[...], nv_ref[...] = _adamw_update(w_ref[...], g_ref[...], m_ref[...], v_ref[...])

    outs = pl.pallas_call(
        body, name=name, out_shape=[jax.ShapeDtypeStruct(a.shape, f32) for a in ws] * 3)(*ws, *gs, *ms, *vs)
    return outs[:n], outs[n:2 * n], outs[2 * n:]


def _place():
    return lax.axis_index("x"), lax.axis_index("y"), lax.axis_index("c")


def gather_shards(bufs, *, name, split):
    n = len(bufs)

    def body(*refs):
        bufs_ = refs[:n]
        isend, irecv, dsend, drecv = refs[2 * n:]
        x, y, c = _place()
        k = 2 * x + y
        peers = [(1 - x, y, c), (x, 1 - y, c), (1 - x, 1 - y, c)]
        peer_k = [2 * (1 - x) + y, 2 * x + (1 - y), 2 * (1 - x) + (1 - y)]

        def slab(a, q, h):
            if not split[a]:
                return bufs_[a].at[q]
            half = bufs_[a].shape[1] // 2
            return bufs_[a].at[q, pl.ds(pl.multiple_of(h * half, 16), half)]

        def ici(a, j, q):
            return pltpu.make_async_remote_copy(
                src_ref=slab(a, q, c), dst_ref=slab(a, q, c), send_sem=isend.at[3 * a + j], recv_sem=irecv.at[3 * a + j],
                device_id=peers[j], device_id_type=MESH)

        def d2d(a, j, h):
            return pltpu.make_async_remote_copy(
                src_ref=slab(a, peer_k[j], h), dst_ref=slab(a, peer_k[j], h), send_sem=dsend.at[3 * a + j],
                recv_sem=drecv.at[3 * a + j], device_id=(x, y, 1 - c), device_id_type=MESH)

        for a in range(n):
            for j in range(3):
                ici(a, j, k).start()
        for a in range(n):
            for j in range(3):
                ici(a, j, peer_k[j]).wait_recv()
                if split[a]:
                    d2d(a, j, c).start()
        for a in range(n):
            for j in range(3):
                if split[a]:
                    d2d(a, j, 1 - c).wait_recv()
        for a in range(n):
            for j in range(3):
                ici(a, j, k).wait_send()
                if split[a]:
                    d2d(a, j, c).wait_send()

    return pl.pallas_call(
        body, name=name, in_specs=[ANY] * n, out_specs=[ANY] * n,
        out_shape=[jax.ShapeDtypeStruct(b.shape, b.dtype) for b in bufs],
        input_output_aliases={i: i for i in range(n)},
        scratch_shapes=[pltpu.SemaphoreType.DMA((3 * n,))] * 4)(*bufs)


HBM = pl.BlockSpec(memory_space=pltpu.HBM)
SEM = pl.BlockSpec(memory_space=pltpu.SEMAPHORE)
DATAFLOW = pltpu.SideEffectType.DATAFLOW_SIDE_EFFECTING


def _chip_peers():
    x, y, c = _place()
    return 2 * x + y, [(1 - x, y, c), (x, 1 - y, c), (1 - x, 1 - y, c)], [2 * (1 - x) + y, 2 * x + (1 - y), 2 * (1 - x) + (1 - y)]


def _half_slab(ref, q, h):
    half = ref.shape[1] // 2
    return ref.at[q, pl.ds(pl.multiple_of(h * half, BF16_ROWS), half)]


def gather_start(bufs, groups, after, *, name):
    n = len(bufs)
    ng = len(groups)

    def body(*refs):
        ins = refs[:n]
        sends, recvs = refs[2 * n + 1:2 * n + 1 + ng], refs[2 * n + 1 + ng:2 * n + 1 + 2 * ng]
        token = refs[-1]
        c = lax.axis_index("c")
        k, peers, _ = _chip_peers()
        for gi, grp in enumerate(groups):
            for pos, a in enumerate(grp):
                for j in range(3):
                    pltpu.make_async_remote_copy(
                        src_ref=_half_slab(ins[a], k, c), dst_ref=_half_slab(ins[a], k, c), send_sem=sends[gi].at[3 * pos + j],
                        recv_sem=recvs[gi].at[3 * pos + j], device_id=peers[j], device_id_type=MESH).start()
        token[...] = jnp.zeros_like(token)

    sems = [pltpu.SemaphoreType.DMA((3 * len(grp),)) for grp in groups]
    outs = pl.pallas_call(
        body, name=name, in_specs=[HBM] * n + [ANY],
        out_specs=[HBM] * n + [SEM] * (2 * ng) + [pl.BlockSpec(memory_space=pltpu.VMEM)],
        out_shape=[pltpu.HBM(b.shape, b.dtype) for b in bufs] + sems + sems + [jax.ShapeDtypeStruct((8, LANES), f32)],
        input_output_aliases={i: i for i in range(n)},
        compiler_params=pltpu.CompilerParams(has_side_effects=DATAFLOW))(
            *[pltpu.with_memory_space_constraint(b, pltpu.HBM) for b in bufs], after)
    return outs[:n], outs[n:n + ng], outs[n + ng:n + 2 * ng], outs[-1]


def gather_wait(bufs, send_sems, recv_sems, after, *, name):
    n = len(bufs)

    def body(*refs):
        ins = refs[:n]
        send, recv = refs[n], refs[n + 1]
        c = lax.axis_index("c")
        k, peers, peer_k = _chip_peers()
        for a in range(n):
            for j in range(3):
                copy = pltpu.make_async_remote_copy(
                    src_ref=_half_slab(ins[a], k, c), dst_ref=_half_slab(ins[a], peer_k[j], c), send_sem=send.at[3 * a + j],
                    recv_sem=recv.at[3 * a + j], device_id=peers[j], device_id_type=MESH)
                copy.wait_send()
                copy.wait_recv()

    return pl.pallas_call(
        body, name=name, in_specs=[HBM] * n + [SEM, SEM, ANY], out_specs=[HBM] * n,
        out_shape=[pltpu.HBM(b.shape, b.dtype) for b in bufs],
        input_output_aliases={i: i for i in range(n)},
        compiler_params=pltpu.CompilerParams(has_side_effects=DATAFLOW))(*bufs, send_sems, recv_sems, after)


def forward_halves(bufs, *, name):
    n = len(bufs)

    def body(*refs):
        bufs_ = refs[:n]
        send, recv = refs[2 * n:]
        x, y, c = _place()
        _, _, peer_k = _chip_peers()

        def copy(a, j, h):
            return pltpu.make_async_remote_copy(
                src_ref=_half_slab(bufs_[a], peer_k[j], h), dst_ref=_half_slab(bufs_[a], peer_k[j], h),
                send_sem=send.at[3 * a + j], recv_sem=recv.at[3 * a + j], device_id=(x, y, 1 - c), device_id_type=MESH)

        for a in range(n):
            for j in range(3):
                copy(a, j, c).start()
        for a in range(n):
            for j in range(3):
                copy(a, j, 1 - c).wait_recv()
        for a in range(n):
            for j in range(3):
                copy(a, j, c).wait_send()

    return pl.pallas_call(
        body, name=name, in_specs=[ANY] * n, out_specs=[ANY] * n,
        out_shape=[jax.ShapeDtypeStruct(b.shape, b.dtype) for b in bufs],
        input_output_aliases={i: i for i in range(n)},
        scratch_shapes=[pltpu.SemaphoreType.DMA((3 * n,))] * 2)(*bufs)


def _forward_copies(bufs_, send, recv, h):
    x, y, c = _place()
    _, _, peer_k = _chip_peers()
    return [pltpu.make_async_remote_copy(
        src_ref=_half_slab(bufs_[a], peer_k[j], h), dst_ref=_half_slab(bufs_[a], peer_k[j], h),
        send_sem=send.at[3 * a + j], recv_sem=recv.at[3 * a + j], device_id=(x, y, 1 - c), device_id_type=MESH)
        for a in range(len(bufs_)) for j in range(3)]


def forward_start(bufs, after, *, name):
    n = len(bufs)

    def body(*refs):
        for cp in _forward_copies(refs[:n], refs[2 * n + 1], refs[2 * n + 2], lax.axis_index("c")):
            cp.start()
        refs[-1][...] = jnp.zeros_like(refs[-1])

    sems = [pltpu.SemaphoreType.DMA((3 * n,))] * 2
    outs = pl.pallas_call(
        body, name=name, in_specs=[HBM] * n + [ANY],
        out_specs=[HBM] * n + [SEM] * 2 + [pl.BlockSpec(memory_space=pltpu.VMEM)],
        out_shape=[pltpu.HBM(b.shape, b.dtype) for b in bufs] + sems + [jax.ShapeDtypeStruct((8, LANES), f32)],
        input_output_aliases={i: i for i in range(n)},
        compiler_params=pltpu.CompilerParams(has_side_effects=DATAFLOW))(*bufs, after)
    return (n, outs[:-1]), outs[-1]


def forward_wait(state, after, *, name):
    n, held = state

    def body(*refs):
        c = lax.axis_index("c")
        for mine, theirs in zip(_forward_copies(refs[:n], refs[n], refs[n + 1], c),
                                _forward_copies(refs[:n], refs[n], refs[n + 1], 1 - c)):
            mine.wait_send()
            theirs.wait_recv()

    return pl.pallas_call(
        body, name=name, in_specs=[HBM] * n + [SEM] * 2 + [ANY], out_specs=[HBM] * n,
        out_shape=[pltpu.HBM(b.shape, b.dtype) for b in held[:n]],
        input_output_aliases={i: i for i in range(n)},
        compiler_params=pltpu.CompilerParams(has_side_effects=DATAFLOW))(*held, after)


def _sibling_copies(srcs, lands, send, recv):
    x, y, c = _place()
    return [pltpu.make_async_remote_copy(src_ref=srcs[a], dst_ref=lands[a], send_sem=send.at[a], recv_sem=recv.at[a],
                                         device_id=(x, y, 1 - c), device_id_type=MESH) for a in range(len(srcs))]


def sibling_start(arrays, after, *, name):
    n = len(arrays)
    lands = [pltpu.with_memory_space_constraint(lax.empty(a.shape, a.dtype), pltpu.HBM) for a in arrays]

    def body(*refs):
        for cp in _sibling_copies(refs[:n], refs[n:2 * n], refs[4 * n + 1], refs[4 * n + 2]):
            cp.start()
        refs[-1][...] = jnp.zeros_like(refs[-1])

    bufs = list(arrays) + lands
    sems = [pltpu.SemaphoreType.DMA((n,))] * 2
    outs = pl.pallas_call(
        body, name=name, in_specs=[HBM] * (2 * n) + [ANY],
        out_specs=[HBM] * (2 * n) + [SEM] * 2 + [pl.BlockSpec(memory_space=pltpu.VMEM)],
        out_shape=[pltpu.HBM(b.shape, b.dtype) for b in bufs] + sems + [jax.ShapeDtypeStruct((8, LANES), f32)],
        input_output_aliases={i: i for i in range(2 * n)},
        compiler_params=pltpu.CompilerParams(has_side_effects=DATAFLOW))(
            *[pltpu.with_memory_space_constraint(b, pltpu.HBM) for b in bufs], after)
    return (n, outs[:-1]), outs[-1]


def sibling_wait(state, after, *, name):
    n, held = state

    def body(*refs):
        for cp in _sibling_copies(refs[:n], refs[n:2 * n], refs[2 * n], refs[2 * n + 1]):
            cp.wait_send()
            cp.wait_recv()

    outs = pl.pallas_call(
        body, name=name, in_specs=[HBM] * (2 * n) + [SEM] * 2 + [ANY], out_specs=[HBM] * (2 * n),
        out_shape=[pltpu.HBM(b.shape, b.dtype) for b in held[:2 * n]],
        input_output_aliases={i: i for i in range(2 * n)},
        compiler_params=pltpu.CompilerParams(has_side_effects=DATAFLOW))(*held, after)
    return outs[n:]


def sibling_exchange(arrays, *, name):
    n = len(arrays)

    def body(*refs):
        ins, outs = refs[:n], refs[n:2 * n]
        send, recv = refs[2 * n:]
        x, y, c = _place()

        def copy(a):
            return pltpu.make_async_remote_copy(
                src_ref=ins[a], dst_ref=outs[a], send_sem=send.at[a], recv_sem=recv.at[a],
                device_id=(x, y, 1 - c), device_id_type=MESH)

        for a in range(n):
            copy(a).start()
        for a in range(n):
            copy(a).wait_recv()
        for a in range(n):
            copy(a).wait_send()

    return pl.pallas_call(
        body, name=name, in_specs=[ANY] * n, out_specs=[ANY] * n,
        out_shape=[jax.ShapeDtypeStruct(a.shape, a.dtype) for a in arrays],
        scratch_shapes=[pltpu.SemaphoreType.DMA((n,)), pltpu.SemaphoreType.DMA((n,))])(*arrays)


ALL_MASKS = [(mx, my, mc) for mx in (0, 1) for my in (0, 1) for mc in (0, 1)][1:]


def _scatter_copies(srcs, lands, ev, send, recv, esend, erecv):
    x, y, c = _place()
    me = 4 * x + 2 * y + c
    k, peers, peer_k = _chip_peers()
    out = []
    for a in range(len(srcs)):
        for j in range(3):
            out.append(pltpu.make_async_remote_copy(
                src_ref=srcs[a].at[peer_k[j]], dst_ref=lands[a].at[j], send_sem=send.at[3 * a + j],
                recv_sem=recv.at[3 * a + j], device_id=peers[j], device_id_type=MESH))
    start_ev, wait_ev = [], []
    if ev is not None:
        for j, (mx, my, mc) in enumerate(ALL_MASKS):
            peer = (x ^ mx, y ^ my, c ^ mc)
            start_ev.append(pltpu.make_async_remote_copy(
                src_ref=ev.at[me], dst_ref=ev.at[me], send_sem=esend.at[j], recv_sem=erecv.at[j],
                device_id=peer, device_id_type=MESH))
            wait_ev.append(pltpu.make_async_remote_copy(
                src_ref=ev.at[me], dst_ref=ev.at[me ^ (4 * mx + 2 * my + mc)], send_sem=esend.at[j],
                recv_sem=erecv.at[j], device_id=peer, device_id_type=MESH))
    return out, start_ev, wait_ev


def chip_scatter_start(arrays, everyone, after, *, name):
    n = len(arrays)
    ne = 0 if everyone is None else 1
    lands = [pltpu.with_memory_space_constraint(lax.empty((3,) + a.shape[1:], a.dtype), pltpu.HBM) for a in arrays]

    def body(*refs):
        srcs, lands_ = refs[:n], refs[n:2 * n]
        ev = refs[2 * n] if ne else None
        sems = refs[2 * n + ne + 1 + 2 * n + ne:-1]
        send, recv = sems[0], sems[1]
        esend, erecv = (sems[2], sems[3]) if ne else (None, None)
        copies, start_ev, _ = _scatter_copies(srcs, lands_, ev, send, recv, esend, erecv)
        for cp in start_ev + copies:
            cp.start()
        refs[-1][...] = jnp.zeros_like(refs[-1])

    sem_shapes = [pltpu.SemaphoreType.DMA((3 * n,))] * 2 + [pltpu.SemaphoreType.DMA((7,))] * (2 * ne)
    bufs = list(arrays) + lands + ([everyone] if ne else [])
    outs = pl.pallas_call(
        body, name=name, in_specs=[HBM] * len(bufs) + [ANY],
        out_specs=[HBM] * len(bufs) + [SEM] * len(sem_shapes) + [pl.BlockSpec(memory_space=pltpu.VMEM)],
        out_shape=[pltpu.HBM(b.shape, b.dtype) for b in bufs] + sem_shapes + [jax.ShapeDtypeStruct((8, LANES), f32)],
        input_output_aliases={i: i for i in range(len(bufs))},
        compiler_params=pltpu.CompilerParams(has_side_effects=DATAFLOW))(
            *[pltpu.with_memory_space_constraint(b, pltpu.HBM) for b in bufs], after)
    return (n, ne, outs[:-1]), outs[-1]


def chip_scatter_wait(state, after, *, name):
    n, ne, held = state
    nb = 2 * n + ne
    bufs, sems = held[:nb], held[nb:]

    def body(*refs):
        srcs, lands_ = refs[:n], refs[n:2 * n]
        ev = refs[2 * n] if ne else None
        sems_ = refs[nb:nb + len(sems)]
        esend, erecv = (sems_[2], sems_[3]) if ne else (None, None)
        copies, _, wait_ev = _scatter_copies(srcs, lands_, ev, sems_[0], sems_[1], esend, erecv)
        for cp in wait_ev + copies:
            cp.wait_send()
            cp.wait_recv()

    outs = pl.pallas_call(
        body, name=name, in_specs=[HBM] * nb + [SEM] * len(sems) + [ANY], out_specs=[HBM] * nb,
        out_shape=[pltpu.HBM(b.shape, b.dtype) for b in bufs],
        input_output_aliases={i: i for i in range(nb)},
        compiler_params=pltpu.CompilerParams(has_side_effects=DATAFLOW))(*bufs, *sems, after)
    return outs[n:2 * n], (outs[2 * n] if ne else None)


def sibling_merge(bufs, *, name):
    n = len(bufs)

    def body(*refs):
        bufs_ = refs[:n]
        send, recv = refs[2 * n:]
        x, y, c = _place()

        def copy(u, h):
            return pltpu.make_async_remote_copy(
                src_ref=bufs_[u].at[h], dst_ref=bufs_[u].at[h], send_sem=send.at[u], recv_sem=recv.at[u],
                device_id=(x, y, 1 - c), device_id_type=MESH)

        for u in range(n):
            copy(u, c).start()
        for u in range(n):
            copy(u, 1 - c).wait_recv()
        for u in range(n):
            copy(u, c).wait_send()

    return pl.pallas_call(
        body, name=name, in_specs=[ANY] * n, out_specs=[ANY] * n,
        out_shape=[jax.ShapeDtypeStruct(b.shape, b.dtype) for b in bufs],
        input_output_aliases={i: i for i in range(n)},
        scratch_shapes=[pltpu.SemaphoreType.DMA((n,)), pltpu.SemaphoreType.DMA((n,))])(*bufs)


def sum_leading(a, *, name):
    n, r, c = a.shape

    def body(a_ref, o_ref):
        acc = a_ref[0]
        for i in range(1, n):
            acc = acc + a_ref[i]
        o_ref[...] = acc

    rb = r // 2 if r % 16 == 0 else r
    return pl.pallas_call(
        body, name=name, grid=(r // rb,), in_specs=[pl.BlockSpec((n, rb, c), lambda i: (0, i, 0))],
        out_specs=pl.BlockSpec((rb, c), lambda i: (i, 0)), out_shape=jax.ShapeDtypeStruct((r, c), f32),
        compiler_params=_params("parallel"))(a)


def _half_rows(shape):
    return shape[1] // 2 // 2


def rs_cast_other_half(grads, c_arr, *, name):
    n = len(grads)

    def body(c_ref, *refs):
        for i_ref, o_ref in zip(refs[:n], refs[n:]):
            o_ref[...] = i_ref[...].astype(bf16)

    in_specs = [pl.BlockSpec((1, _half_rows(g.shape), g.shape[2]), lambda s, r, c_ref: (s, (1 - c_ref[0]) * 2 + r, 0))
                for g in grads]
    out_specs = [pl.BlockSpec((1, _half_rows(g.shape), g.shape[2]), lambda s, r, c_ref: (s, r, 0)) for g in grads]
    return pl.pallas_call(
        body, name=name,
        grid_spec=pltpu.PrefetchScalarGridSpec(num_scalar_prefetch=1, grid=(N_SHARDS, 2),
                                               in_specs=in_specs, out_specs=out_specs),
        out_shape=[jax.ShapeDtypeStruct((N_SHARDS, g.shape[1] // 2, g.shape[2]), bf16) for g in grads],
        compiler_params=_params("parallel", "parallel"))(c_arr, *grads)


def rs_add_sibling(grads, recvd, ck_arr, *, name):
    n = len(grads)

    def body(ck_ref, *refs):
        s = pl.program_id(1)
        for u in range(n):
            g_ref, r_ref = refs[u], refs[n + u]
            qb_ref, own_ref = refs[2 * n + u], refs[3 * n + u]
            q = g_ref[0] + r_ref[0].astype(f32)
            qb_ref[0] = q.astype(bf16)

            @pl.when(s == ck_ref[1])
            def _(own_ref=own_ref, q=q):
                own_ref[...] = q

    in_specs = [pl.BlockSpec((1, _half_rows(g.shape), g.shape[2]), lambda r, s, ck: (s, ck[0] * 2 + r, 0)) for g in grads]
    in_specs += [pl.BlockSpec((1, _half_rows(g.shape), g.shape[2]), lambda r, s, ck: (s, r, 0)) for g in grads]
    out_specs = [pl.BlockSpec((1, _half_rows(g.shape), g.shape[2]), lambda r, s, ck: (s, r, 0)) for g in grads]
    out_specs += [pl.BlockSpec((_half_rows(g.shape), g.shape[2]), lambda r, s, ck: (r, 0)) for g in grads]
    outs = pl.pallas_call(
        body, name=name,
        grid_spec=pltpu.PrefetchScalarGridSpec(num_scalar_prefetch=1, grid=(2, N_SHARDS),
                                               in_specs=in_specs, out_specs=out_specs),
        out_shape=[jax.ShapeDtypeStruct((N_SHARDS, g.shape[1] // 2, g.shape[2]), bf16) for g in grads]
        + [jax.ShapeDtypeStruct((g.shape[1] // 2, g.shape[2]), f32) for g in grads],
        compiler_params=_params("parallel", "arbitrary"))(ck_arr, *grads, *recvd)
    return outs[:n], outs[n:]


def rs_sum_chips(owns, recvd, ck_arr, *, name):
    n = len(owns)

    def body(ck_ref, *refs):
        for u in range(n):
            own_ref, r_ref, o_ref = refs[u], refs[n + u], refs[2 * n + u]
            o_ref[0] = ((own_ref[...] + r_ref[0].astype(f32)) + r_ref[1].astype(f32)) + r_ref[2].astype(f32)

    in_specs = [pl.BlockSpec((o.shape[0] // 2, o.shape[1]), lambda r, ck: (r, 0)) for o in owns]
    in_specs += [pl.BlockSpec((3, o.shape[0] // 2, o.shape[1]), lambda r, ck: (0, r, 0)) for o in owns]
    out_specs = [pl.BlockSpec((1, o.shape[0] // 2, o.shape[1]), lambda r, ck: (ck[0], r, 0)) for o in owns]
    return pl.pallas_call(
        body, name=name,
        grid_spec=pltpu.PrefetchScalarGridSpec(num_scalar_prefetch=1, grid=(2,), in_specs=in_specs, out_specs=out_specs),
        out_shape=[jax.ShapeDtypeStruct((2,) + o.shape, f32) for o in owns],
        compiler_params=_params("parallel"))(ck_arr, *owns, *recvd)


SMALL = ("a_norm", "a_v_norm", "a_w_s", "a_b_s", "f_norm", "f_conv_w", "f_conv_b", "kv_norm", "k_norm",
         "b_norm", "b_q_norm", "b_sinks")
BIG = ("a_w_in", "a_w_out", "f_w_in", "f_w_out", "w_kv", "b_w_q", "b_w_o")
PACK_COLS = 1024
PACK_ROWS = 8 * N_STEPS


def _pack(parts, rows=PACK_ROWS):
    flat = jnp.concatenate([p.reshape(-1).astype(f32) for p in parts])
    pad = (-flat.shape[0]) % (rows * PACK_COLS)
    return jnp.pad(flat, (0, pad)).reshape(-1, PACK_COLS)


def _unpack(packed, shapes):
    flat = packed.reshape(-1)
    out, off = [], 0
    for s in shapes:
        size = math.prod(s)
        out.append(flat[off:off + size].reshape(s))
        off += size
    return out


def _ffn_fwd(x, g, h, r, w_in4, conv_w, conv_b, f, tag):
    wg, wu = conv_w[:, :f], conv_w[:, f:]
    bg, bu = conv_b[None, :f], conv_b[None, f:]
    gate, up, a = ffn_in_fused(h, w_in4, wg, wu, bg, bu, name=f"ffn{tag}_in")
    return a, (x, g, h, r, gate, up, a, wg, wu)


def _ffn_bwd(dy, saved, w_in4, w_out, c_arr, tag, exchange=False):
    x, g, h, r, gate, up, a, wg, wu = saved
    f = w_out.shape[0]
    d_w_out = mm_tn(a, [dy], c_arr, name=f"ffn{tag}_dwout", n_s=w_out.shape[1], shard_rows=f // N_SHARDS, tki=f // 2)
    dpg, dpu, sg, su = ffn_gate_bwd(dy, w_out, h, w_in4, gate, up, wg, wu, name=f"ffn{tag}_dgate")
    d_w_in = mm_tn(h, [dpg, dpu], c_arr, name=f"ffn{tag}_dwin", n_s=w_in4.shape[2], shard_rows=h.shape[1])
    state = None
    if exchange:
        state, token = sibling_start([d_w_in[1], d_w_out[1]], d_w_in[0], name=f"rs_sibling_start_ffn{tag}")
        g = g + token[0, 0]
    dx, dg = mm_nt_rms_bwd([dpg, dpu], w_in4, x, r, g, dy, name=f"ffn{tag}_dh")
    d_conv_w = jnp.concatenate([sg[0:3], su[0:3]], axis=1)
    d_conv_b = jnp.concatenate([sg[3], su[3]], axis=0)
    return dx, dg, d_w_in, d_conv_w, d_conv_b, d_w_out, state


def _rs_front(pairs, sibling_state, after, c_arr, tag):
    units = [full.reshape(N_SHARDS, -1, full.shape[-1]) for full, _ in pairs]
    from_sib = sibling_wait(sibling_state, after, name=f"rs_sibling_wait{tag}")
    return rs_add_sibling(units, from_sib, c_arr, name=f"rs_add{tag}")


def _rs_back(own, from_chips, c_arr, tag):
    halves = rs_sum_chips(list(own), list(from_chips), c_arr, name=f"rs_sum{tag}")
    return [m.reshape(-1, m.shape[2]) for m in sibling_merge(list(halves), name=f"rs_merge{tag}")]


def kernel(x, a_norm, a_w_in, a_v_norm, a_w_s, a_b_s, a_w_out, f_norm, f_w_in, f_conv_w, f_conv_b, f_w_out, kv_norm, w_kv, k_norm, b_norm, b_w_q, b_q_norm, b_sinks, b_w_o, loss_target, m_a_norm, m_a_w_in, m_a_v_norm, m_a_w_s, m_a_b_s, m_a_w_out, m_f_norm, m_f_w_in, m_f_conv_w, m_f_conv_b, m_f_w_out, m_kv_norm, m_w_kv, m_k_norm, m_b_norm, m_b_w_q, m_b_q_norm, m_b_sinks, m_b_w_o, v_a_norm, v_a_w_in, v_a_v_norm, v_a_w_s, v_a_b_s, v_a_w_out, v_f_norm, v_f_w_in, v_f_conv_w, v_f_conv_b, v_f_w_out, v_kv_norm, v_w_kv, v_k_norm, v_b_norm, v_b_w_q, v_b_q_norm, v_b_sinks, v_b_w_o):
    args = dict(locals())
    weights = {n: args[n] for n in SMALL + BIG}
    moms = {n: args["m_" + n] for n in SMALL + BIG}
    vars_ = {n: args["v_" + n] for n in SMALL + BIG}
    t, d = x.shape[1], x.shape[2]
    xi, yi, ci = _place()
    chip = 2 * xi + yi

    big_local = [a_w_in[0], a_w_out[0], f_w_in, f_w_out, w_kv, b_w_q[0], b_w_o[0]]
    c_arr = jnp.stack([ci, chip]).astype(jnp.int32)
    k_arr = jnp.stack([chip]).astype(jnp.int32)
    b_ain, b_aout, b_fin0, b_fin1, b_fout0, b_fout1, b_kv, b_q, b_o = cast_into_slot(big_local, k_arr, name="cast_weights")
    small_cols = _pack([a_norm, a_v_norm, f_conv_w], rows=8)
    b_small = lax.dynamic_update_slice(jnp.zeros((N_SHARDS,) + small_cols.shape, f32), small_cols[None], (chip, 0, 0))
    g_small, w_a_in, g_a_w_out = gather_shards([b_small, b_ain, b_aout], name="gather_first", split=[False, True, True])
    later, send_sems, recv_sems, token = gather_start([b_fin0, b_fout0, b_kv, b_q, b_o, b_fin1, b_fout1],
                                                      [[0], [1], [2, 3, 4], [5, 6]], g_small, name="gather_start")
    ns_cols = a_norm.shape[1]
    nf_cols = f_conv_w.shape[2]
    parts = [_unpack(g_small[k], [a_norm.shape, a_v_norm.shape, f_conv_w.shape]) for k in range(N_SHARDS)]
    a_norm_f = jnp.concatenate([p[0] for p in parts], axis=1) + token[0, 0]
    a_v_norm_f = jnp.concatenate([p[1] for p in parts], axis=1)
    conv_w_f = jnp.concatenate([p[2] for p in parts], axis=2)

    x0 = x[0]
    tril = jnp.tril(jnp.ones((CHUNK, CHUNK), dtype=bool))
    wc = jnp.where(tril[None], a_w_s[0], 0.0).astype(bf16)
    bt = a_b_s[0].T
    kg2 = jnp.tile(k_norm, 2)[None]
    qg2 = jnp.tile(b_q_norm[0], 2)[None]

    (h_a,), r_a = rms_fwd(x0, [a_norm_f], name="a_norm")
    zu = mm_nn(h_a, w_a_in, name="a_in_u", s0=0, ns=2)
    zv = mm_nn(h_a, w_a_in, name="a_in_v", s0=2, ns=2)
    y_a = sgu_gate_fwd(zu, zv, a_v_norm_f, wc, bt, name="a_gate")
    w_a_out = g_a_w_out.reshape(1, -1, d)
    f = f_w_out.shape[1] * N_SHARDS
    fwd0, tok0 = forward_start(gather_wait(later[0:1], send_sems[0], recv_sems[0], y_a, name="gather_wait_0"), y_a,
                               name="gather_forward_start_0")
    x1, (h_f0,), r_f0 = mm_residual(y_a, w_a_out[0], x0, name="a_out", gains=[f_norm[0:1] + tok0[0, 0]])
    (g_fin0,) = forward_wait(fwd0, x1, name="gather_forward_wait_0")
    w_f_in = [g_fin0, None]
    a0, ffn0 = _ffn_fwd(x1, f_norm[0:1], h_f0, r_f0, w_f_in[0], conv_w_f[0], f_conv_b[0], f, "0")
    (g_fout0,) = forward_halves(gather_wait(later[1:2], send_sems[1], recv_sems[1], a0, name="gather_wait_1"),
                                name="gather_forward_1")
    w_f_out = [g_fout0.reshape(-1, d), None]
    fwd1, tok1 = forward_start(gather_wait(later[2:5], send_sems[2], recv_sems[2], g_fout0, name="gather_wait_1b"), a0,
                               name="gather_forward_start_1b")
    x2, (h_k, h_q), r_b = mm_residual(a0, w_f_out[0], x1, name="ffn0_out", gains=[kv_norm[None] + tok1[0, 0], b_norm])
    g_w_kv, g_b_w_q, g_b_w_o = forward_wait(fwd1, x2, name="gather_forward_wait_1b")
    w_kv_f = g_w_kv.reshape(1, d, -1)
    w_q_f = g_b_w_q.reshape(1, d, -1)
    w_o_f = g_b_w_o.reshape(1, -1, d)
    kv = mm_nn(h_k, w_kv_f, name="kv_proj")
    k2, v2 = kv_post_fwd(kv, kg2, name="kv_post")
    qp = mm_nn(h_q, w_q_f, name="q_proj")
    qn = q_norm_fwd(qp, qg2, name="q_norm", scale=HEAD_DIM ** -0.5)
    fwd2, tok2 = forward_start(gather_wait(later[5:7], send_sems[3], recv_sems[3], qn, name="gather_wait_2"), qn,
                               name="gather_forward_start_2")
    o = attn_fwd(qn, k2, v2, b_sinks[0] + tok2[0, 0], name="attn")
    x3, (h_f1,), r_f1 = mm_residual(o, w_o_f[0], x2, name="o_proj", gains=[f_norm[1:2]])
    g_fin1, g_fout1 = forward_wait(fwd2, x3, name="gather_forward_wait_2")
    w_f_in[1] = g_fin1
    w_f_out[1] = g_fout1.reshape(-1, d)
    a1, ffn1 = _ffn_fwd(x3, f_norm[1:2], h_f1, r_f1, w_f_in[1], conv_w_f[1], f_conv_b[1], f, "1")
    dx4, sq = mm_residual(a1, w_f_out[1], x3, name="ffn1_out", target=loss_target[0])
    loss_part = (0.5 * jnp.sum(sq) / d).reshape(1)

    proj_rows = d // N_SHARDS
    dx3, d_fn1, d_fwin1, d_cw1, d_cb1, d_fwout1, _ = _ffn_bwd(dx4, ffn1, w_f_in[1], w_f_out[1], c_arr, "1")
    do = mm_nt([dx3], w_o_f, name="o_proj_dx")
    d_w_o = mm_tn(o, [dx3], c_arr, name="o_proj_dw", n_s=d, shard_rows=o.shape[1] // N_SHARDS)
    dqn, dk2, dv2, dsink = attn_bwd(qn, k2, v2, do, b_sinks[0], name="attn_bwd")
    dqp, dqg = q_norm_bwd(dqn, qp, qg2, name="q_norm_bwd", scale=HEAD_DIM ** -0.5)
    dkv, dkg = kv_post_bwd(dk2, dv2, kv, kg2, name="kv_post_bwd")
    d_w_q = mm_tn(h_q, [dqp], c_arr, name="q_proj_dw", n_s=w_q_f.shape[2], shard_rows=proj_rows)
    d_w_kv = mm_tn(h_k, [dkv], c_arr, name="kv_proj_dw", n_s=w_kv_f.shape[2], shard_rows=proj_rows)
    group1 = [d_fwin1, d_fwout1, d_w_kv, d_w_q, d_w_o]
    sib1, token_s1 = sibling_start([half for _, half in group1], d_w_kv[0], name="rs_sibling_start1")
    dh_k = mm_nt([dkv], w_kv_f, name="kv_proj_dx")
    dx2, d_bn, d_kvn = mm_nt_rms_bwd([dqp], w_q_f, x2, r_b, b_norm + token_s1[0, 0], dx3, name="q_proj_dx",
                                     extra=(dh_k, kv_norm[None]))
    chip_bf1, own1 = _rs_front(group1, sib1, dx2, c_arr, "1")
    scatter1, token1 = chip_scatter_start(list(chip_bf1), None, dx2, name="rs_chips_start1")
    ffn0 = ffn0[:7] + (ffn0[7] + token1[0, 0],) + ffn0[8:]
    dx1, d_fn0, d_fwin0, d_cw0, d_cb0, d_fwout0, sib2 = _ffn_bwd(dx2, ffn0, w_f_in[0], w_f_out[0], c_arr, "0", exchange=True)
    chip_bf2, own2 = _rs_front([d_fwin0, d_fwout0], sib2, dx1, c_arr, "2")
    scatter2, token2 = chip_scatter_start(list(chip_bf2), None, dx1, name="rs_chips_start2")
    a_v_norm_f = a_v_norm_f + token2[0, 0]
    dy_a = mm_nt([dx1], w_a_out, name="a_out_dx")
    d_w_aout = mm_tn(y_a, [dx1], c_arr, name="a_out_dw", n_s=d, shard_rows=y_a.shape[1] // N_SHARDS)
    dzu, dzv, d_avn, d_ws, d_bt = sgu_gate_bwd(zu, zv, dy_a, a_v_norm_f, wc, bt, name="a_gate_bwd")
    d_w_ain = mm_tn(h_a, [dzu, dzv], c_arr, name="a_in_dw", n_s=w_a_in.shape[2], shard_rows=d)
    sib3, token_s3 = sibling_start([d_w_ain[1], d_w_aout[1]], d_w_ain[0], name="rs_sibling_start3")
    dx0, d_an = mm_nt_rms_bwd([dzu, dzv], w_a_in, x0, r_a, a_norm_f + token_s3[0, 0], dx1, name="a_in_dx")
    grad_x = dx0[None]

    chip_bf3, own3 = _rs_front([d_w_ain, d_w_aout], sib3, dx0, c_arr, "3")
    d_fn = jnp.concatenate([d_fn0, d_fn1], axis=0)
    d_cw = jnp.stack([d_cw0, d_cw1])
    d_cb = jnp.stack([d_cb0, d_cb1])
    d_kg = (dkg[0, :HEAD_DIM] + dkg[0, HEAD_DIM:])
    d_qg = (dqg[0, :HEAD_DIM] + dqg[0, HEAD_DIM:])[None]
    small_full = [d_an, d_avn, d_ws[None], d_bt.T[None], d_fn, d_cw, d_cb, d_kvn[0], d_kg, d_bn, d_qg,
                  dsink[:, :N_Q_HEADS], loss_part]
    packed = _pack(small_full)
    me = 4 * xi + 2 * yi + ci
    everyone = lax.dynamic_update_slice(lax.empty((N_DEV,) + packed.shape, f32), packed[None], (me, 0, 0))
    scatter3, token3 = chip_scatter_start(list(chip_bf3), everyone, own3[0], name="rs_chips_start3")
    from_chips1, _ = chip_scatter_wait(scatter1, token3, name="rs_chips_wait1")
    from_chips2, _ = chip_scatter_wait(scatter2, from_chips1[0], name="rs_chips_wait2")
    fin1, fout1, gkv, gq, go, fin0, fout0 = _rs_back(list(own1) + list(own2), list(from_chips1) + list(from_chips2),
                                                     c_arr, "12")
    late = ("f_w_in", "f_w_out", "w_kv", "b_w_q", "b_w_o")
    res_late = adamw([weights[n] for n in late], [[fin0, fin1], [fout0, fout1], [gkv], [gq], [go]],
                     [moms[n] for n in late], [vars_[n] for n in late], name="adamw_late")
    from_chips3, from_all = chip_scatter_wait(scatter3, res_late[1][2], name="rs_chips_wait3")
    ain, aout = _rs_back(own3, from_chips3, c_arr, "3")
    first = ("a_w_in", "a_w_out")
    res_first = adamw([weights[n] for n in first], [[ain], [aout]], [moms[n] for n in first],
                      [vars_[n] for n in first], name="adamw_first")
    big = {n: tuple(r[i] for r in res_late) for i, n in enumerate(late)}
    big.update({n: tuple(r[i] for r in res_first) for i, n in enumerate(first)})

    full_shapes = [g.shape for g in small_full]
    small_g = _unpack(sum_leading(from_all, name="small_sum"), full_shapes)
    loss = small_g.pop()[0]
    small_g[0] = lax.dynamic_slice_in_dim(small_g[0], chip * ns_cols, ns_cols, axis=1)
    small_g[1] = lax.dynamic_slice_in_dim(small_g[1], chip * ns_cols, ns_cols, axis=1)
    small_g[5] = lax.dynamic_slice_in_dim(small_g[5], chip * nf_cols, nf_cols, axis=2)
    small_shapes = [weights[n].shape for n in SMALL]
    small_g = [g.reshape(s) for g, s in zip(small_g, small_shapes)]
    flat2 = [(math.prod(s[:-1]), s[-1]) for s in small_shapes]
    small_d, small_m, small_v = adamw_small(
        *[[a.reshape(s2) for a, s2 in zip(group, flat2)]
          for group in ([weights[n] for n in SMALL], small_g, [moms[n] for n in SMALL], [vars_[n] for n in SMALL])],
        name="adamw_small")
    small_d, small_m, small_v = ([a.reshape(s) for a, s in zip(group, small_shapes)]
                                 for group in (small_d, small_m, small_v))

    out = {}
    for i, n in enumerate(SMALL):
        out[n] = (small_g[i], small_d[i], small_m[i], small_v[i])
    out.update(big)
    order = ["a_norm", "a_w_in", "a_v_norm", "a_w_s", "a_b_s", "a_w_out", "f_norm", "f_w_in", "f_conv_w", "f_conv_b",
             "f_w_out", "kv_norm", "w_kv", "k_norm", "b_norm", "b_w_q", "b_q_norm", "b_sinks", "b_w_o"]
    return (loss, grad_x, *[out[n][0] for n in order], *[out[n][1] for n in order],
            *[out[n][2] for n in order], *[out[n][3] for n in order])
```

```python
import functools
import math

import jax
import jax.numpy as jnp
from jax import lax
from jax.experimental import pallas as pl
from jax.experimental.pallas import tpu as pltpu

f32 = jnp.float32
bf16 = jnp.bfloat16
MESH = pl.DeviceIdType.MESH
ANY = pl.BlockSpec(memory_space=pl.ANY)

EPS = 1e-6
LANES = 128
CHUNK = 128
HEAD_DIM = 64
N_Q_HEADS = 16
N_KV_HEADS = 4
Q_PER_KV = N_Q_HEADS // N_KV_HEADS
N_SHARDS = 4
N_DEV = 8

ADAM_LR = 0.001
ADAM_B1 = 0.9
ADAM_B2 = 0.999
ADAM_EPS = 1e-08
ADAM_WD = 0.01
ADAM_STEP = 10
ADAM_C1 = 1.0 - ADAM_B1 ** ADAM_STEP
ADAM_C2 = 1.0 - ADAM_B2 ** ADAM_STEP

_INV_SQRT2 = 1.0 / math.sqrt(2.0)
_INV_SQRT2PI = 1.0 / math.sqrt(2.0 * math.pi)


def _params(*sem):
    return pltpu.CompilerParams(dimension_semantics=sem)


def _gelu(z):
    return 0.5 * z * (1.0 + lax.erf(z * _INV_SQRT2))


def _gelu_and_grad(z):
    cdf = 0.5 * (1.0 + lax.erf(z * _INV_SQRT2))
    return z * cdf, cdf + z * jnp.exp(-0.5 * z * z) * _INV_SQRT2PI


def _dot(a, b):
    return jnp.dot(a, b, preferred_element_type=f32)


def _dot_nt(a, b):
    return lax.dot_general(a, b, (((1,), (1,)), ((), ())), preferred_element_type=f32)


def _dot_tn(a, b):
    return lax.dot_general(a, b, (((0,), (0,)), ((), ())), preferred_element_type=f32)


def _dot_split(a, b):
    hi = a.astype(bf16)
    lo = (a - hi.astype(f32)).astype(bf16)
    return _dot(hi, b) + _dot(lo, b)


VMEM_TILE_BUDGET = 40 * 1024 * 1024
MAX_ROW_TILE = 2048


def _row_tile(m, fixed_bytes, row_bytes):
    tm = min(m, MAX_ROW_TILE)
    while tm > 256 and 2 * (fixed_bytes + tm * row_bytes) > VMEM_TILE_BUDGET:
        tm //= 2
    return tm


def _isz(a):
    return jnp.dtype(a.dtype).itemsize


def mm_nn(a, w3, *, name, s0=0, ns=None, add=None, out_dtype=f32):
    m, k = a.shape
    s_all, _, n_s = w3.shape
    ns = s_all if ns is None else ns
    tm = _row_tile(m, k * n_s * 2, k * _isz(a) + n_s * jnp.dtype(out_dtype).itemsize + (0 if add is None else n_s * 4))

    def body(*refs):
        if add is None:
            a_ref, w_ref, o_ref = refs
            acc = _dot(a_ref[...].astype(bf16), w_ref[0])
        else:
            a_ref, w_ref, add_ref, o_ref = refs
            acc = _dot(a_ref[...].astype(bf16), w_ref[0]) + add_ref[...]
        o_ref[...] = acc.astype(out_dtype)

    in_specs = [pl.BlockSpec((tm, k), lambda j, i: (i, 0)),
                pl.BlockSpec((1, k, n_s), lambda j, i: (s0 + j, 0, 0))]
    args = [a, w3]
    if add is not None:
        in_specs.append(pl.BlockSpec((tm, n_s), lambda j, i: (i, j)))
        args.append(add)
    return pl.pallas_call(
        body, name=name, grid=(ns, m // tm), in_specs=in_specs,
        out_specs=pl.BlockSpec((tm, n_s), lambda j, i: (i, j)),
        out_shape=jax.ShapeDtypeStruct((m, ns * n_s), out_dtype),
        compiler_params=_params("parallel", "parallel"))(*args)


def mm_nt(a_list, w3, *, name, tko=None, add=None, out_dtype=f32):
    s_all, k_out, n_s = w3.shape
    m = a_list[0].shape[0]
    na = len(a_list)
    spa = s_all // na
    tko = k_out if tko is None else tko
    tm = _row_tile(m, tko * n_s * 2, na * n_s * _isz(a_list[0]) + tko * 4 * (1 if add is None else 2))

    def body(*refs):
        a_refs = refs[:na]
        w_ref = refs[na]
        o_ref = refs[-1]
        s = pl.program_id(2)

        @pl.when(s == 0)
        def _():
            if add is None:
                o_ref[...] = jnp.zeros_like(o_ref)
            else:
                o_ref[...] = refs[na + 1][...]

        for idx in range(na):
            @pl.when(s // spa == idx)
            def _(idx=idx):
                o_ref[...] += _dot_nt(a_refs[idx][...].astype(bf16), w_ref[0])

    def a_map(idx):
        return lambda ko, i, s: (i, jnp.clip(s - idx * spa, 0, spa - 1))

    in_specs = [pl.BlockSpec((tm, n_s), a_map(idx)) for idx in range(na)]
    in_specs.append(pl.BlockSpec((1, tko, n_s), lambda ko, i, s: (s, ko, 0)))
    args = list(a_list) + [w3]
    if add is not None:
        in_specs.append(pl.BlockSpec((tm, tko), lambda ko, i, s: (i, ko)))
        args.append(add)
    return pl.pallas_call(
        body, name=name, grid=(k_out // tko, m // tm, s_all), in_specs=in_specs,
        out_specs=pl.BlockSpec((tm, tko), lambda ko, i, s: (i, ko)),
        out_shape=jax.ShapeDtypeStruct((m, k_out), out_dtype),
        compiler_params=_params("parallel", "parallel", "arbitrary"))(*args)


def mm_tn(a, b_list, c_arr, *, name, n_s, shard_rows, tki=None):
    m, k_in = a.shape
    na = len(b_list)
    s_all = sum(b.shape[1] for b in b_list) // n_s
    spa = s_all // na
    tki = k_in if tki is None else tki
    tm = _row_tile(m, tki * n_s * 4, tki * _isz(a) + na * n_s * _isz(b_list[0]))

    nsteps = m // tm
    per_blk = tki // shard_rows
    half = shard_rows // 2

    def body(c_ref, *refs):
        a_ref = refs[0]
        b_refs = refs[1:1 + na]
        o_ref, ob_ref = refs[-2], refs[-1]
        s = pl.program_id(0)
        r = pl.program_id(2)

        @pl.when(r == 0)
        def _():
            o_ref[...] = jnp.zeros_like(o_ref)

        for idx in range(na):
            @pl.when(s // spa == idx)
            def _(idx=idx):
                o_ref[0] += _dot_tn(a_ref[...].astype(bf16), b_refs[idx][...].astype(bf16))

        @pl.when(r == nsteps - 1)
        def _():
            for q in range(per_blk):
                start = pl.multiple_of(q * shard_rows + (1 - c_ref[0]) * half, 16)
                ob_ref[q] = o_ref[0, pl.ds(start, half), :].astype(bf16)

    def b_map(idx):
        def index(s, ki, r, c_ref):
            active = (s // spa) == idx
            return (jnp.where(active, r, 0), jnp.clip(s - idx * spa, 0, spa - 1))
        return index

    in_specs = [pl.BlockSpec((tm, tki), lambda s, ki, r, c_ref: (r, ki))]
    in_specs += [pl.BlockSpec((tm, n_s), b_map(idx)) for idx in range(na)]
    n_blk = k_in // tki
    return pl.pallas_call(
        body, name=name,
        grid_spec=pltpu.PrefetchScalarGridSpec(
            num_scalar_prefetch=1, grid=(s_all, n_blk, nsteps), in_specs=in_specs,
            out_specs=[pl.BlockSpec((1, tki, n_s), lambda s, ki, r, c_ref: (s, ki, 0)),
                       pl.BlockSpec((per_blk, half, n_s), lambda s, ki, r, c_ref: (s * n_blk + ki, 0, 0))]),
        out_shape=[jax.ShapeDtypeStruct((s_all, k_in, n_s), f32),
                   jax.ShapeDtypeStruct((s_all * k_in // shard_rows, half, n_s), bf16)],
        compiler_params=_params("parallel", "parallel", "arbitrary"))(c_arr, a, *b_list)


def mm_nt_rms_bwd(a_list, w3, x, r, g, dx_in, *, name, extra=None):
    s_all, d, n_s = w3.shape
    m = a_list[0].shape[0]
    na = len(a_list)
    spa = s_all // na
    ne = 0 if extra is None else 1
    tm = _row_tile(m, d * n_s * 2, na * n_s * _isz(a_list[0]) + d * 4 * (4 + ne))

    def body(*refs):
        a_refs, w_ref = refs[:na], refs[na]
        x_ref, r_ref, g_ref, dxin_ref = refs[na + 1:na + 5]
        dh2_ref, g2_ref = (refs[na + 5], refs[na + 6]) if ne else (None, None)
        outs = refs[na + 5 + 2 * ne:]
        dx_ref, dg_ref = outs[0], outs[1]
        dg2_ref = outs[2] if ne else None
        acc_ref = outs[-1]
        i, s = pl.program_id(0), pl.program_id(1)

        @pl.when(s == 0)
        def _():
            acc_ref[...] = jnp.zeros_like(acc_ref)

        for idx in range(na):
            @pl.when(s // spa == idx)
            def _(idx=idx):
                acc_ref[...] += _dot_nt(a_refs[idx][...].astype(bf16), w_ref[0])

        @pl.when(s == s_all - 1)
        def _():
            rv = r_ref[...]
            xh = x_ref[...] * rv
            total = dxin_ref[...]
            pairs = [(acc_ref[...], g_ref, dg_ref)] + ([(dh2_ref[...], g2_ref, dg2_ref)] if ne else [])
            for dh, gain_ref, dgain_ref in pairs:
                part = jnp.sum(dh * xh, axis=0, keepdims=True)

                @pl.when(i == 0)
                def _(dgain_ref=dgain_ref, part=part):
                    dgain_ref[...] = part

                @pl.when(i > 0)
                def _(dgain_ref=dgain_ref, part=part):
                    dgain_ref[...] += part

                tg = dh * gain_ref[...]
                total = total + rv * (tg - xh * jnp.mean(tg * xh, axis=1, keepdims=True))
            dx_ref[...] = total

    def a_map(idx):
        return lambda i, s: (i, jnp.clip(s - idx * spa, 0, spa - 1))

    row = pl.BlockSpec((tm, d), lambda i, s: (i, 0))
    vec = pl.BlockSpec((1, d), lambda i, s: (0, 0))
    in_specs = [pl.BlockSpec((tm, n_s), a_map(idx)) for idx in range(na)]
    in_specs += [pl.BlockSpec((1, d, n_s), lambda i, s: (s, 0, 0)), row, pl.BlockSpec((tm, 1), lambda i, s: (i, 0)), vec, row]
    args = list(a_list) + [w3, x, r, g, dx_in]
    if ne:
        in_specs += [row, vec]
        args += list(extra)
    outs = pl.pallas_call(
        body, name=name, grid=(m // tm, s_all), in_specs=in_specs, out_specs=[row] + [vec] * (1 + ne),
        out_shape=[jax.ShapeDtypeStruct((m, d), f32)] + [jax.ShapeDtypeStruct((1, d), f32)] * (1 + ne),
        scratch_shapes=[pltpu.VMEM((tm, d), f32)],
        compiler_params=_params("arbitrary", "arbitrary"))(*args)
    return outs


def mm_residual(a, w, x, *, name, gains=(), target=None):
    m, k = a.shape
    d = w.shape[1]
    ng = len(gains)
    tm = _row_tile(m, k * d * 2, k * _isz(a) + d * 4 * 3 + ng * d * 2)

    def body(*refs):
        a_ref, w_ref, x_ref = refs[:3]
        y = _dot(a_ref[...].astype(bf16), w_ref[...]) + x_ref[...]
        if target is None:
            g_refs = refs[3:3 + ng]
            y_ref = refs[3 + ng]
            h_refs = refs[4 + ng:4 + 2 * ng]
            r_ref = refs[-1]
            y_ref[...] = y
            r = lax.rsqrt(jnp.mean(y * y, axis=1, keepdims=True) + EPS)
            yh = y * r
            for g_ref, h_ref in zip(g_refs, h_refs):
                h_ref[...] = (yh * g_ref[...]).astype(bf16)
            r_ref[...] = r
        else:
            t_ref, dy_ref, s_ref = refs[3:]
            i = pl.program_id(0)
            e = y - t_ref[...]
            dy_ref[...] = e * (1.0 / d)
            part = jnp.sum(e * e, axis=0, keepdims=True)

            @pl.when(i == 0)
            def _():
                s_ref[...] = part

            @pl.when(i > 0)
            def _():
                s_ref[...] += part

    row = pl.BlockSpec((tm, d), lambda i: (i, 0))
    vec = pl.BlockSpec((1, d), lambda i: (0, 0))
    in_specs = [pl.BlockSpec((tm, k), lambda i: (i, 0)), pl.BlockSpec((k, d), lambda i: (0, 0)), row]
    if target is None:
        outs = pl.pallas_call(
            body, name=name, grid=(m // tm,), in_specs=in_specs + [vec] * ng,
            out_specs=[row] * (1 + ng) + [pl.BlockSpec((tm, 1), lambda i: (i, 0))],
            out_shape=[jax.ShapeDtypeStruct((m, d), f32)] + [jax.ShapeDtypeStruct((m, d), bf16)] * ng
            + [jax.ShapeDtypeStruct((m, 1), f32)],
            compiler_params=_params("parallel"))(a, w, x, *gains)
        return outs[0], outs[1:1 + ng], outs[-1]
    return pl.pallas_call(
        body, name=name, grid=(m // tm,), in_specs=in_specs + [row], out_specs=[row, vec],
        out_shape=[jax.ShapeDtypeStruct((m, d), f32), jax.ShapeDtypeStruct((1, d), f32)],
        compiler_params=_params("arbitrary"))(a, w, x, target)


def rms_fwd(x, gains, *, name, tr=512):
    t, d = x.shape
    tr = min(tr, t)
    ng = len(gains)

    def body(*refs):
        x_ref = refs[0]
        g_refs = refs[1:1 + ng]
        h_refs = refs[1 + ng:1 + 2 * ng]
        r_ref = refs[-1]
        xv = x_ref[...]
        r = lax.rsqrt(jnp.mean(xv * xv, axis=1, keepdims=True) + EPS)
        xh = xv * r
        for g_ref, h_ref in zip(g_refs, h_refs):
            h_ref[...] = (xh * g_ref[...]).astype(bf16)
        r_ref[...] = r

    row = pl.BlockSpec((tr, d), lambda i: (i, 0))
    vec = pl.BlockSpec((1, d), lambda i: (0, 0))
    outs = pl.pallas_call(
        body, name=name, grid=(t // tr,), in_specs=[row] + [vec] * ng,
        out_specs=[row] * ng + [pl.BlockSpec((tr, 1), lambda i: (i, 0))],
        out_shape=[jax.ShapeDtypeStruct((t, d), bf16)] * ng + [jax.ShapeDtypeStruct((t, 1), f32)],
        compiler_params=_params("parallel"))(x, *gains)
    return outs[:ng], outs[ng]


def rms_bwd(dh_list, x, r, gains, dx_in, *, name, tr=512):
    t, d = x.shape
    tr = min(tr, t)
    ng = len(gains)

    def body(*refs):
        dh_refs = refs[:ng]
        x_ref, r_ref = refs[ng], refs[ng + 1]
        g_refs = refs[ng + 2:2 * ng + 2]
        dxin_ref = refs[2 * ng + 2]
        dx_ref = refs[2 * ng + 3]
        dg_refs = refs[2 * ng + 4:]
        i = pl.program_id(0)
        rv = r_ref[...]
        xh = x_ref[...] * rv
        acc = dxin_ref[...]
        for dh_ref, g_ref, dg_ref in zip(dh_refs, g_refs, dg_refs):
            dh = dh_ref[...]
            part = jnp.sum(dh * xh, axis=0, keepdims=True)

            @pl.when(i == 0)
            def _(dg_ref=dg_ref, part=part):
                dg_ref[...] = part

            @pl.when(i > 0)
            def _(dg_ref=dg_ref, part=part):
                dg_ref[...] += part

            tg = dh * g_ref[...]
            acc = acc + rv * (tg - xh * jnp.mean(tg * xh, axis=1, keepdims=True))
        dx_ref[...] = acc

    row = pl.BlockSpec((tr, d), lambda i: (i, 0))
    vec = pl.BlockSpec((1, d), lambda i: (0, 0))
    outs = pl.pallas_call(
        body, name=name, grid=(t // tr,),
        in_specs=[row] * ng + [row, pl.BlockSpec((tr, 1), lambda i: (i, 0))] + [vec] * ng + [row],
        out_specs=[row] + [vec] * ng,
        out_shape=[jax.ShapeDtypeStruct((t, d), f32)] + [jax.ShapeDtypeStruct((1, d), f32)] * ng,
        compiler_params=_params("arbitrary"))(*dh_list, x, r, *gains, dx_in)
    return outs[0], outs[1:]


def sgu_gate_fwd(zu, zv, gv, wc, bt, *, name, tr=512):
    t, w = zu.shape
    tr = min(tr, t)
    groups = w // LANES

    def body(zu_ref, zv_ref, gv_ref, wc_ref, bt_ref, y_ref):
        vp = _gelu(zv_ref[...])
        rv = lax.rsqrt(jnp.mean(vp * vp, axis=1, keepdims=True) + EPS)
        vb = (vp * rv * gv_ref[...]).astype(bf16)
        for c in range(tr // CHUNK):
            rows = slice(c * CHUNK, (c + 1) * CHUNK)
            for g in range(groups):
                cols = slice(g * LANES, (g + 1) * LANES)
                sv = _dot(wc_ref[g], vb[rows, cols]) + bt_ref[:, g:g + 1]
                y_ref[rows, cols] = (_gelu(zu_ref[rows, cols]) * sv).astype(bf16)

    row = pl.BlockSpec((tr, w), lambda i: (i, 0))
    return pl.pallas_call(
        body, name=name, grid=(t // tr,),
        in_specs=[row, row, pl.BlockSpec((1, w), lambda i: (0, 0)),
                  pl.BlockSpec((groups, CHUNK, CHUNK), lambda i: (0, 0, 0)),
                  pl.BlockSpec((CHUNK, groups), lambda i: (0, 0))],
        out_specs=row, out_shape=jax.ShapeDtypeStruct((t, w), bf16),
        compiler_params=_params("parallel"))(zu, zv, gv, wc, bt)


def sgu_gate_bwd(zu, zv, dy, gv, wc, bt, *, name, tr=512):
    t, w = zu.shape
    tr = min(tr, t)
    groups = w // LANES
    nsteps = t // tr

    def body(zu_ref, zv_ref, dy_ref, gv_ref, wc_ref, bt_ref,
             dzu_ref, dzv_ref, dgv_ref, dws_ref, dbt_ref, dv_ref, bacc_ref):
        i = pl.program_id(0)

        @pl.when(i == 0)
        def _():
            dgv_ref[...] = jnp.zeros_like(dgv_ref)
            dws_ref[...] = jnp.zeros_like(dws_ref)
            bacc_ref[...] = jnp.zeros_like(bacc_ref)

        vp, vp_grad = _gelu_and_grad(zv_ref[...])
        rv = lax.rsqrt(jnp.mean(vp * vp, axis=1, keepdims=True) + EPS)
        vhat = vp * rv
        vb = (vhat * gv_ref[...]).astype(bf16)
        for c in range(tr // CHUNK):
            rows = slice(c * CHUNK, (c + 1) * CHUNK)
            for g in range(groups):
                cols = slice(g * LANES, (g + 1) * LANES)
                vblk = vb[rows, cols]
                sv = _dot(wc_ref[g], vblk) + bt_ref[:, g:g + 1]
                zub = zu_ref[rows, cols]
                dyb = dy_ref[rows, cols]
                ub, ub_grad = _gelu_and_grad(zub)
                dzu_ref[rows, cols] = (dyb * sv * ub_grad).astype(bf16)
                dsv = dyb * ub
                bacc_ref[:, cols] += dsv
                dsvb = dsv.astype(bf16)
                dv_ref[rows, cols] = _dot_tn(wc_ref[g], dsvb)
                dws_ref[g] += _dot_nt(dsvb, vblk)
        dv = dv_ref[...]
        dgv_ref[...] += jnp.sum(dv * vhat, axis=0, keepdims=True)
        tg = dv * gv_ref[...]
        dvp = rv * (tg - vhat * jnp.mean(tg * vhat, axis=1, keepdims=True))
        dzv_ref[...] = (dvp * vp_grad).astype(bf16)

        @pl.when(i == nsteps - 1)
        def _():
            tt = lax.broadcasted_iota(jnp.int32, (CHUNK, CHUNK), 0)
            ss = lax.broadcasted_iota(jnp.int32, (CHUNK, CHUNK), 1)
            for g in range(groups):
                dws_ref[g] = jnp.where(ss <= tt, dws_ref[g], 0.0)
                dbt_ref[:, g:g + 1] = jnp.sum(bacc_ref[:, g * LANES:(g + 1) * LANES], axis=1, keepdims=True)

    row = pl.BlockSpec((tr, w), lambda i: (i, 0))
    full3 = pl.BlockSpec((groups, CHUNK, CHUNK), lambda i: (0, 0, 0))
    return pl.pallas_call(
        body, name=name, grid=(nsteps,),
        in_specs=[row, row, row, pl.BlockSpec((1, w), lambda i: (0, 0)), full3,
                  pl.BlockSpec((CHUNK, groups), lambda i: (0, 0))],
        out_specs=[row, row, pl.BlockSpec((1, w), lambda i: (0, 0)), full3,
                   pl.BlockSpec((CHUNK, groups), lambda i: (0, 0))],
        out_shape=[jax.ShapeDtypeStruct((t, w), bf16), jax.ShapeDtypeStruct((t, w), bf16),
                   jax.ShapeDtypeStruct((1, w), f32), jax.ShapeDtypeStruct((groups, CHUNK, CHUNK), f32),
                   jax.ShapeDtypeStruct((CHUNK, groups), f32)],
        scratch_shapes=[pltpu.VMEM((tr, w), f32), pltpu.VMEM((CHUNK, w), f32)],
        compiler_params=_params("arbitrary"))(zu, zv, dy, gv, wc, bt)


HALO = 8


def _shift_down(v, halo, k, first):
    r = pltpu.roll(v, k, 0)
    hh = jnp.where(first, 0.0, pltpu.roll(halo, k, 0))
    rid = lax.broadcasted_iota(jnp.int32, (HALO, v.shape[1]), 0)
    head = jnp.where(rid < k, hh, r[0:HALO])
    if v.shape[0] == HALO:
        return head
    return jnp.concatenate([head, r[HALO:]], axis=0)


def _shift_up(v, halo, k, last):
    n = v.shape[0]
    r = pltpu.roll(v, n - k, 0)
    hh = jnp.where(last, 0.0, pltpu.roll(halo, HALO - k, 0))
    rid = lax.broadcasted_iota(jnp.int32, (HALO, v.shape[1]), 0)
    tail = jnp.where(rid >= HALO - k, hh, r[n - HALO:])
    return jnp.concatenate([r[:n - HALO], tail], axis=0)


def _conv(p, halo, w_ref, b_ref, first):
    return (w_ref[2:3, :] * p + w_ref[1:2, :] * _shift_down(p, halo, 1, first)
            + w_ref[0:1, :] * _shift_down(p, halo, 2, first) + b_ref[...])


BF16_ROWS = 16


def ffn_in_fused(h, w_in4, wg, wu, bg, bu, *, name):
    t, k = h.shape
    s_all, _, n_s = w_in4.shape
    half = s_all // 2
    tm = _row_tile(t, 2 * k * n_s * 2, k * 2 + 4 * n_s * 4 + n_s * 2)

    def body(h_ref, hh_ref, wg_ref, wu_ref, cg_ref, cu_ref, bg_ref, bu_ref, pg_ref, pu_ref, gate_ref, up_ref, a_ref):
        first = pl.program_id(1) == 0
        hv, hh = h_ref[...], hh_ref[...]
        outs = []
        for w_ref, c_ref, b_ref, p_ref, o_ref in ((wg_ref, cg_ref, bg_ref, pg_ref, gate_ref),
                                                  (wu_ref, cu_ref, bu_ref, pu_ref, up_ref)):
            p = _dot(hv, w_ref[0])
            p_ref[...] = p
            hu = _conv(p, _dot(hh, w_ref[0])[BF16_ROWS - HALO:], c_ref, b_ref, first)
            o_ref[...] = hu
            outs.append(hu)
        gate, up = outs
        a_ref[...] = (gate * jax.nn.sigmoid(gate) * up).astype(bf16)

    tile = pl.BlockSpec((tm, n_s), lambda j, i: (i, j))
    cw = pl.BlockSpec((3, n_s), lambda j, i: (0, j))
    cb = pl.BlockSpec((1, n_s), lambda j, i: (0, j))
    f = half * n_s
    return pl.pallas_call(
        body, name=name, grid=(half, t // tm),
        in_specs=[pl.BlockSpec((tm, k), lambda j, i: (i, 0)),
                  pl.BlockSpec((BF16_ROWS, k), lambda j, i: (jnp.maximum(i * (tm // BF16_ROWS) - 1, 0), 0)),
                  pl.BlockSpec((1, k, n_s), lambda j, i: (j, 0, 0)),
                  pl.BlockSpec((1, k, n_s), lambda j, i: (j + half, 0, 0)), cw, cw, cb, cb],
        out_specs=[tile] * 5,
        out_shape=[jax.ShapeDtypeStruct((t, f), f32)] * 4 + [jax.ShapeDtypeStruct((t, f), bf16)],
        compiler_params=_params("parallel", "parallel"))(h, h, w_in4, w_in4, wg, wu, bg, bu)


def _gate_grads(gate, up, dav):
    sg = jax.nn.sigmoid(gate)
    return dav * up * (sg * (1.0 + gate * (1.0 - sg))), dav * gate * sg


GATE_BWD_ROWS = 512


def ffn_gate_bwd(dy, w_out, pg, pu, gate, up, wg, wu, *, name):
    t, f = pg.shape
    d = dy.shape[1]
    tr = min(GATE_BWD_ROWS, t)
    nsteps = t // tr
    tc = f // 2

    def body(dy_ref, dyn_ref, w_ref, pg_ref, pu_ref, gate_ref, gaten_ref, up_ref, upn_ref, wg_ref, wu_ref,
             dg_ref, du_ref, sg_ref, su_ref):
        i = pl.program_id(1)
        last = i == nsteps - 1
        w = w_ref[0]
        da = _dot_nt(dy_ref[...].astype(bf16), w)
        da_n = _dot_nt(dyn_ref[...].astype(bf16), w)
        dgate, dup = _gate_grads(gate_ref[...], up_ref[...], da)
        dgate_n, dup_n = _gate_grads(gaten_ref[...], upn_ref[...], da_n)
        rid = lax.broadcasted_iota(jnp.int32, (8, tc), 0)
        for dd, d_n, c_ref, p_ref, o_ref, s_ref in ((dgate, dgate_n, wg_ref, pg_ref, dg_ref, sg_ref),
                                                    (dup, dup_n, wu_ref, pu_ref, du_ref, su_ref)):
            d1, d2 = _shift_up(dd, d_n, 1, last), _shift_up(dd, d_n, 2, last)
            o_ref[...] = (c_ref[2:3, :] * dd + c_ref[1:2, :] * d1 + c_ref[0:1, :] * d2).astype(bf16)
            p = p_ref[...]
            sums = [jnp.sum(d2 * p, axis=0, keepdims=True), jnp.sum(d1 * p, axis=0, keepdims=True),
                    jnp.sum(dd * p, axis=0, keepdims=True), jnp.sum(dd, axis=0, keepdims=True)]
            part = jnp.zeros((8, tc), f32)
            for k, sk in enumerate(sums):
                part = jnp.where(rid == k, sk, part)

            @pl.when(i == 0)
            def _(s_ref=s_ref, part=part):
                s_ref[...] = part

            @pl.when(i > 0)
            def _(s_ref=s_ref, part=part):
                s_ref[...] += part

    def nxt_rows(j, i):
        return (jnp.minimum((i + 1) * (tr // HALO), t // HALO - 1), j)

    tile = pl.BlockSpec((tr, tc), lambda j, i: (i, j))
    nxt = pl.BlockSpec((HALO, tc), nxt_rows)
    wspec = pl.BlockSpec((3, tc), lambda j, i: (0, j))
    stat = pl.BlockSpec((8, tc), lambda j, i: (0, j))
    return pl.pallas_call(
        body, name=name, grid=(2, nsteps),
        in_specs=[pl.BlockSpec((tr, d), lambda j, i: (i, 0)),
                  pl.BlockSpec((HALO, d), lambda j, i: (nxt_rows(j, i)[0], 0)),
                  pl.BlockSpec((1, tc, d), lambda j, i: (j, 0, 0)),
                  tile, tile, tile, nxt, tile, nxt, wspec, wspec],
        out_specs=[tile, tile, stat, stat],
        out_shape=[jax.ShapeDtypeStruct((t, f), bf16), jax.ShapeDtypeStruct((t, f), bf16),
                   jax.ShapeDtypeStruct((8, f), f32), jax.ShapeDtypeStruct((8, f), f32)],
        compiler_params=_params("parallel", "arbitrary"))(
            dy, dy, w_out.reshape(2, tc, d), pg, pu, gate, gate, up, up, wg, wu)


def _head_mean_matrix():
    i = lax.broadcasted_iota(jnp.int32, (LANES, LANES), 0) // HEAD_DIM
    j = lax.broadcasted_iota(jnp.int32, (LANES, LANES), 1) // HEAD_DIM
    return jnp.where(i == j, 1.0 / HEAD_DIM, 0.0).astype(bf16)


def _lane_half(shape):
    return (lax.broadcasted_iota(jnp.int32, shape, 1) % LANES) // HEAD_DIM


def q_norm_fwd(qp, g2, *, name, scale, tr=512):
    t, w = qp.shape
    tr = min(tr, t)

    def body(x_ref, g_ref, o_ref):
        bd = _head_mean_matrix()
        for cb in range(w // LANES):
            cols = slice(cb * LANES, (cb + 1) * LANES)
            xc = x_ref[:, cols]
            rh = lax.rsqrt(_dot_split(xc * xc, bd) + EPS)
            o_ref[:, cols] = (xc * rh * g_ref[...] * scale).astype(bf16)

    row = pl.BlockSpec((tr, w), lambda i: (i, 0))
    return pl.pallas_call(
        body, name=name, grid=(t // tr,), in_specs=[row, pl.BlockSpec((1, LANES), lambda i: (0, 0))],
        out_specs=row, out_shape=jax.ShapeDtypeStruct((t, w), bf16),
        compiler_params=_params("parallel"))(qp, g2)


def q_norm_bwd(dq, qp, g2, *, name, scale, tr=512):
    t, w = qp.shape
    tr = min(tr, t)

    def body(dq_ref, x_ref, g_ref, o_ref, dg_ref):
        i = pl.program_id(0)
        bd = _head_mean_matrix()
        acc = jnp.zeros((1, LANES), f32)
        for cb in range(w // LANES):
            cols = slice(cb * LANES, (cb + 1) * LANES)
            xc = x_ref[:, cols]
            rh = lax.rsqrt(_dot_split(xc * xc, bd) + EPS)
            xh = xc * rh
            dy = dq_ref[:, cols] * scale
            acc = acc + jnp.sum(dy * xh, axis=0, keepdims=True)
            tg = dy * g_ref[...]
            o_ref[:, cols] = (rh * (tg - xh * _dot_split(tg * xh, bd))).astype(bf16)

        @pl.when(i == 0)
        def _():
            dg_ref[...] = acc

        @pl.when(i > 0)
        def _():
            dg_ref[...] += acc

    row = pl.BlockSpec((tr, w), lambda i: (i, 0))
    vec = pl.BlockSpec((1, LANES), lambda i: (0, 0))
    return pl.pallas_call(
        body, name=name, grid=(t // tr,), in_specs=[row, row, vec], out_specs=[row, vec],
        out_shape=[jax.ShapeDtypeStruct((t, w), bf16), jax.ShapeDtypeStruct((1, LANES), f32)],
        compiler_params=_params("arbitrary"))(dq, qp, g2)


def kv_post_fwd(kv, g2, *, name, tr=512):
    t, w = kv.shape
    tr = min(tr, t)
    kw = w // 2

    def body(x_ref, g_ref, k_ref, v_ref):
        bd = _head_mean_matrix()
        half = _lane_half((tr, LANES))
        for cb in range(kw // LANES):
            xc = x_ref[:, cb * LANES:(cb + 1) * LANES]
            rh = lax.rsqrt(_dot_split(xc * xc, bd) + EPS)
            kn = xc * rh * g_ref[...]
            vc = x_ref[:, kw + cb * LANES:kw + (cb + 1) * LANES]
            for src, dst in ((kn, k_ref), (vc, v_ref)):
                sw = pltpu.roll(src, HEAD_DIM, 1)
                for hf in range(2):
                    blk = 2 * cb + hf
                    dst[:, blk * LANES:(blk + 1) * LANES] = jnp.where(half == hf, src, sw).astype(bf16)

    return pl.pallas_call(
        body, name=name, grid=(t // tr,),
        in_specs=[pl.BlockSpec((tr, w), lambda i: (i, 0)), pl.BlockSpec((1, LANES), lambda i: (0, 0))],
        out_specs=[pl.BlockSpec((tr, 2 * kw), lambda i: (i, 0))] * 2,
        out_shape=[jax.ShapeDtypeStruct((t, 2 * kw), bf16)] * 2,
        compiler_params=_params("parallel"))(kv, g2)


def kv_post_bwd(dk2, dv2, kv, g2, *, name, tr=512):
    t, w = kv.shape
    tr = min(tr, t)
    kw = w // 2

    def body(dk_ref, dv_ref, x_ref, g_ref, o_ref, dg_ref):
        i = pl.program_id(0)
        bd = _head_mean_matrix()
        half = _lane_half((tr, LANES))
        acc = jnp.zeros((1, LANES), f32)

        def fold(ref, cb):
            a = ref[:, (2 * cb) * LANES:(2 * cb + 1) * LANES]
            b = ref[:, (2 * cb + 1) * LANES:(2 * cb + 2) * LANES]
            return jnp.where(half == 0, a + pltpu.roll(a, HEAD_DIM, 1), b + pltpu.roll(b, HEAD_DIM, 1))

        for cb in range(kw // LANES):
            cols = slice(cb * LANES, (cb + 1) * LANES)
            xc = x_ref[:, cols]
            rh = lax.rsqrt(_dot_split(xc * xc, bd) + EPS)
            xh = xc * rh
            dy = fold(dk_ref, cb)
            acc = acc + jnp.sum(dy * xh, axis=0, keepdims=True)
            tg = dy * g_ref[...]
            o_ref[:, cols] = (rh * (tg - xh * _dot_split(tg * xh, bd))).astype(bf16)
            o_ref[:, kw + cb * LANES:kw + (cb + 1) * LANES] = fold(dv_ref, cb).astype(bf16)

        @pl.when(i == 0)
        def _():
            dg_ref[...] = acc

        @pl.when(i > 0)
        def _():
            dg_ref[...] += acc

    dup = pl.BlockSpec((tr, 2 * kw), lambda i: (i, 0))
    row = pl.BlockSpec((tr, w), lambda i: (i, 0))
    vec = pl.BlockSpec((1, LANES), lambda i: (0, 0))
    return pl.pallas_call(
        body, name=name, grid=(t // tr,), in_specs=[dup, dup, row, vec], out_specs=[row, vec],
        out_shape=[jax.ShapeDtypeStruct((t, w), bf16), jax.ShapeDtypeStruct((1, LANES), f32)],
        compiler_params=_params("arbitrary"))(dk2, dv2, kv, g2)


def _slope(h):
    return 2.0 ** (-8.0 * (h + 1) / N_Q_HEADS)


GROUP_ROWS = Q_PER_KV * CHUNK


def _band_mask(n):
    tq = lax.broadcasted_iota(jnp.int32, (GROUP_ROWS, 2 * CHUNK), 0) % CHUNK
    jk = lax.broadcasted_iota(jnp.int32, (GROUP_ROWS, 2 * CHUNK), 1)
    dist = tq + CHUNK - jk
    ok = (dist >= 0) & (dist < CHUNK) & jnp.logical_not((n == 0) & (jk < CHUNK))
    return dist.astype(f32), ok


def _band(ref, n, kh):
    p0 = pl.multiple_of(jnp.maximum(n - 1, 0) * CHUNK, CHUNK)
    c0 = pl.multiple_of(n * CHUNK, CHUNK)
    cols = slice(kh * LANES, (kh + 1) * LANES)
    return jnp.concatenate([ref[pl.ds(p0, CHUNK), cols], ref[pl.ds(c0, CHUNK), cols]], axis=0)


def _stack_heads(ref, kh, half):
    parts = []
    for cb in (2 * kh, 2 * kh + 1):
        xc = ref[:, cb * LANES:(cb + 1) * LANES].astype(f32)
        parts += [jnp.where(half == hf, xc, 0.0).astype(bf16) for hf in range(2)]
    return jnp.concatenate(parts, axis=0)


def _unstack_heads(x4, half):
    return (jnp.where(half == 0, x4[0:CHUNK], x4[CHUNK:2 * CHUNK]),
            jnp.where(half == 0, x4[2 * CHUNK:3 * CHUNK], x4[3 * CHUNK:]))


def _per_head_column(kh, values):
    grp = lax.broadcasted_iota(jnp.int32, (GROUP_ROWS, 1), 0) // CHUNK
    col = jnp.full((GROUP_ROWS, 1), values[0], f32)
    for g in range(1, Q_PER_KV):
        col = jnp.where(grp == g, values[g], col)
    return col


def _softmax_band(q4, kband, dist, ok, slope, sink):
    s = _dot_nt(q4, kband)
    s = jnp.where(ok, s - slope * dist, -jnp.inf)
    m = jnp.maximum(jnp.max(s, axis=1, keepdims=True), sink)
    e = jnp.exp(s - m)
    es = jnp.exp(sink - m)
    den = jnp.sum(e, axis=1, keepdims=True) + es
    return e / den, es / den


def attn_fwd(q, k2, v2, sinks, *, name):
    t, w = q.shape
    nb = t // CHUNK

    def body(sink_ref, q_ref, k_ref, v_ref, o_ref):
        n = pl.program_id(0)
        dist, ok = _band_mask(n)
        half = _lane_half((CHUNK, LANES))
        khs = range(N_KV_HEADS)
        heads = [[Q_PER_KV * kh + g for g in range(Q_PER_KV)] for kh in khs]
        q4 = [_stack_heads(q_ref, kh, half) for kh in khs]
        soft = [_softmax_band(q4[kh], _band(k_ref, n, kh), dist, ok, _per_head_column(kh, [_slope(h) for h in heads[kh]]),
                              _per_head_column(kh, [sink_ref[h] for h in heads[kh]])) for kh in khs]
        o4 = [_dot(soft[kh][0].astype(bf16), _band(v_ref, n, kh)) for kh in khs]
        for kh in khs:
            lo, hi = _unstack_heads(o4[kh], half)
            o_ref[:, (2 * kh) * LANES:(2 * kh + 1) * LANES] = lo.astype(bf16)
            o_ref[:, (2 * kh + 1) * LANES:(2 * kh + 2) * LANES] = hi.astype(bf16)

    full = pl.BlockSpec((t, k2.shape[1]), lambda n: (0, 0))
    return pl.pallas_call(
        body, name=name, grid=(nb,),
        in_specs=[pl.BlockSpec(memory_space=pltpu.SMEM), pl.BlockSpec((CHUNK, w), lambda n: (n, 0)), full, full],
        out_specs=pl.BlockSpec((CHUNK, w), lambda n: (n, 0)),
        out_shape=jax.ShapeDtypeStruct((t, w), bf16),
        compiler_params=_params("parallel"))(sinks, q, k2, v2)


def attn_bwd(q, k2, v2, do, sinks, *, name):
    t, w = q.shape
    nb = t // CHUNK
    kw = k2.shape[1]

    def body(sink_ref, q_ref, k_ref, v_ref, do_ref, dq_ref, dk_ref, dv_ref, ds_ref, kc_ref, vc_ref):
        n = pl.program_id(0)

        @pl.when(n == 0)
        def _():
            ds_ref[...] = jnp.zeros_like(ds_ref)
            kc_ref[...] = jnp.zeros_like(kc_ref)
            vc_ref[...] = jnp.zeros_like(vc_ref)
            dk_ref[...] = jnp.zeros_like(dk_ref)
            dv_ref[...] = jnp.zeros_like(dv_ref)

        @pl.when(n == nb)
        def _():
            dk_ref[...] = kc_ref[...]
            dv_ref[...] = vc_ref[...]

        @pl.when(n < nb)
        def _():
            dist, ok = _band_mask(n)
            half = _lane_half((CHUNK, LANES))
            lane = lax.broadcasted_iota(jnp.int32, (1, LANES), 1)
            sink_acc = jnp.zeros((1, LANES), f32)
            khs = range(N_KV_HEADS)
            heads = [[Q_PER_KV * kh + g for g in range(Q_PER_KV)] for kh in khs]
            q4 = [_stack_heads(q_ref, kh, half) for kh in khs]
            do4 = [_stack_heads(do_ref, kh, half) for kh in khs]
            kband = [_band(k_ref, n, kh) for kh in khs]
            vband = [_band(v_ref, n, kh) for kh in khs]
            soft = [_softmax_band(q4[kh], kband[kh], dist, ok, _per_head_column(kh, [_slope(h) for h in heads[kh]]),
                                  _per_head_column(kh, [sink_ref[h] for h in heads[kh]])) for kh in khs]
            dp = [_dot_nt(do4[kh], vband[kh]) for kh in khs]
            delta = [jnp.sum(soft[kh][0] * dp[kh], axis=1, keepdims=True) for kh in khs]
            dsb = [(soft[kh][0] * (dp[kh] - delta[kh])).astype(bf16) for kh in khs]
            dq4 = [_dot(dsb[kh], kband[kh]) for kh in khs]
            dkb = [_dot_tn(dsb[kh], q4[kh]) for kh in khs]
            dvb = [_dot_tn(soft[kh][0].astype(bf16), do4[kh]) for kh in khs]
            for kh in khs:
                sd = soft[kh][1] * delta[kh]
                for g, h in enumerate(heads[kh]):
                    part = jnp.sum(sd[g * CHUNK:(g + 1) * CHUNK], axis=0, keepdims=True)
                    sink_acc = sink_acc + jnp.where(lane == h, -part, 0.0)
                lo, hi = _unstack_heads(dq4[kh], half)
                dq_ref[:, (2 * kh) * LANES:(2 * kh + 1) * LANES] = lo
                dq_ref[:, (2 * kh + 1) * LANES:(2 * kh + 2) * LANES] = hi
                cols = slice(kh * LANES, (kh + 1) * LANES)
                dk_ref[:, cols] = kc_ref[:, cols] + dkb[kh][0:CHUNK]
                dv_ref[:, cols] = vc_ref[:, cols] + dvb[kh][0:CHUNK]
                kc_ref[:, cols] = dkb[kh][CHUNK:]
                vc_ref[:, cols] = dvb[kh][CHUNK:]
            ds_ref[...] += sink_acc

    full = pl.BlockSpec((t, kw), lambda n: (0, 0))
    qblk = pl.BlockSpec((CHUNK, w), lambda n: (jnp.minimum(n, nb - 1), 0))
    kblk = pl.BlockSpec((CHUNK, kw), lambda n: (jnp.maximum(n - 1, 0), 0))
    return pl.pallas_call(
        body, name=name, grid=(nb + 1,),
        in_specs=[pl.BlockSpec(memory_space=pltpu.SMEM), qblk, full, full, qblk],
        out_specs=[qblk, kblk, kblk, pl.BlockSpec((1, LANES), lambda n: (0, 0))],
        out_shape=[jax.ShapeDtypeStruct((t, w), f32), jax.ShapeDtypeStruct((t, kw), f32),
                   jax.ShapeDtypeStruct((t, kw), f32), jax.ShapeDtypeStruct((1, LANES), f32)],
        scratch_shapes=[pltpu.VMEM((CHUNK, kw), f32), pltpu.VMEM((CHUNK, kw), f32)],
        compiler_params=_params("arbitrary"))(sinks, q, k2, v2, do)


def loss_head(y, target, *, name, tr=512):
    t, d = y.shape
    tr = min(tr, t)

    def body(y_ref, t_ref, dy_ref, s_ref):
        i = pl.program_id(0)
        e = y_ref[...] - t_ref[...]
        dy_ref[...] = e * (1.0 / d)
        part = jnp.sum(e * e, axis=0, keepdims=True)

        @pl.when(i == 0)
        def _():
            s_ref[...] = part

        @pl.when(i > 0)
        def _():
            s_ref[...] += part

    row = pl.BlockSpec((tr, d), lambda i: (i, 0))
    vec = pl.BlockSpec((1, d), lambda i: (0, 0))
    return pl.pallas_call(
        body, name=name, grid=(t // tr,), in_specs=[row, row], out_specs=[row, vec],
        out_shape=[jax.ShapeDtypeStruct((t, d), f32), jax.ShapeDtypeStruct((1, d), f32)],
        compiler_params=_params("arbitrary"))(y, target)


N_STEPS = 8


def _row_blocks(shape):
    if len(shape) == 2:
        r, c = shape
        return (r // N_STEPS, c), (lambda s: (s, 0))
    l, r, c = shape
    per = N_STEPS // l
    return (1, r // per, c), (lambda s: (s // per, s % per, 0))


CAST_STEPS = 4


def cast_into_slot(arrays, k_arr, *, name):
    in_specs, out_specs, out_shape, layers = [], [], [], []
    for a in arrays:
        r, c = a.shape[-2:]
        rb = r // CAST_STEPS
        if a.ndim == 2:
            in_specs.append(pl.BlockSpec((rb, c), lambda s, k: (s, 0)))
            layers.append(None)
        else:
            for l in range(a.shape[0]):
                in_specs.append(pl.BlockSpec((1, rb, c), lambda s, k, l=l: (l, s, 0)))
                layers.append(l)
        for _ in range(1 if a.ndim == 2 else a.shape[0]):
            out_specs.append(pl.BlockSpec((1, rb, c), lambda s, k: (k[0], s, 0)))
            out_shape.append(jax.ShapeDtypeStruct((N_SHARDS, r, c), bf16))
    n = len(in_specs)

    def body(k_ref, *refs):
        for i_ref, o_ref, l in zip(refs[:n], refs[n:], layers):
            o_ref[0] = (i_ref[...] if l is None else i_ref[0]).astype(bf16)

    args = []
    for a in arrays:
        args += [a] * (1 if a.ndim == 2 else a.shape[0])
    return pl.pallas_call(
        body, name=name,
        grid_spec=pltpu.PrefetchScalarGridSpec(num_scalar_prefetch=1, grid=(CAST_STEPS,),
                                               in_specs=in_specs, out_specs=out_specs),
        out_shape=out_shape, compiler_params=_params("parallel"))(k_arr, *args)


def adamw(ws, gs, ms, vs, *, name):
    n = len(ws)
    specs, g_specs, g_count = [], [], []
    for w, g_list in zip(ws, gs):
        blk, index = _row_blocks(w.shape)
        specs.append(pl.BlockSpec(blk, index))
        layers = len(g_list)
        per = N_STEPS // layers
        g_count.append(layers)
        for l in range(layers):
            g_specs.append(pl.BlockSpec(blk[-2:], lambda s, l=l, per=per: (jnp.where(s // per == l, s % per, 0), 0)))
    ng = len(g_specs)

    def body(*refs):
        s = pl.program_id(0)
        g_refs = refs[3 * n:3 * n + ng]
        outs = refs[3 * n + ng:]
        off = 0
        for i in range(n):
            w_ref, m_ref, v_ref = refs[i], refs[n + i], refs[2 * n + i]
            go_ref, d_ref, nm_ref, nv_ref = (outs[k * n + i] for k in range(4))
            layers = g_count[i]
            g = g_refs[off][...]
            for l in range(1, layers):
                g = jnp.where(s // (N_STEPS // layers) == l, g_refs[off + l][...], g)
            off += layers
            g = g.reshape(w_ref.shape)
            m = ADAM_B1 * m_ref[...] + (1.0 - ADAM_B1) * g
            v = ADAM_B2 * v_ref[...] + (1.0 - ADAM_B2) * (g * g)
            m_hat = m / ADAM_C1
            v_hat = v / ADAM_C2
            go_ref[...] = g
            d_ref[...] = -ADAM_LR * (m_hat / (jnp.sqrt(v_hat) + ADAM_EPS) + ADAM_WD * w_ref[...])
            nm_ref[...] = m
            nv_ref[...] = v

    outs = pl.pallas_call(
        body, name=name, grid=(N_STEPS,), in_specs=specs * 3 + g_specs, out_specs=specs * 4,
        out_shape=[jax.ShapeDtypeStruct(a.shape, f32) for a in ws] * 4,
        compiler_params=_params("parallel"))(*ws, *ms, *vs, *[g for g_list in gs for g in g_list])
    return [outs[k * n:(k + 1) * n] for k in range(4)]


def _adamw_update(w, g, m, v):
    m = ADAM_B1 * m + (1.0 - ADAM_B1) * g
    v = ADAM_B2 * v + (1.0 - ADAM_B2) * (g * g)
    m_hat = m / ADAM_C1
    v_hat = v / ADAM_C2
    return -ADAM_LR * (m_hat / (jnp.sqrt(v_hat) + ADAM_EPS) + ADAM_WD * w), m, v


def adamw_small(ws, gs, ms, vs, *, name):
    n = len(ws)

    def body(*refs):
        for i in range(n):
            w_ref, g_ref, m_ref, v_ref = (refs[k * n + i] for k in range(4))
            d_ref, nm_ref, nv_ref = (refs[(4 + k) * n + i] for k in range(3))
            d_ref[...], nm_ref[...], nv_ref[...] = _adamw_update(w_ref[...], g_ref[...], m_ref[...], v_ref[...])

    outs = pl.pallas_call(
        body, name=name, out_shape=[jax.ShapeDtypeStruct(a.shape, f32) for a in ws] * 3)(*ws, *gs, *ms, *vs)
    return outs[:n], outs[n:2 * n], outs[2 * n:]


def _place():
    return lax.axis_index("x"), lax.axis_index("y"), lax.axis_index("c")


def gather_shards(bufs, *, name, split):
    n = len(bufs)

    def body(*refs):
        bufs_ = refs[:n]
        isend, irecv, dsend, drecv = refs[2 * n:]
        x, y, c = _place()
        k = 2 * x + y
        peers = [(1 - x, y, c), (x, 1 - y, c), (1 - x, 1 - y, c)]
        peer_k = [2 * (1 - x) + y, 2 * x + (1 - y), 2 * (1 - x) + (1 - y)]

        def slab(a, q, h):
            if not split[a]:
                return bufs_[a].at[q]
            half = bufs_[a].shape[1] // 2
            return bufs_[a].at[q, pl.ds(pl.multiple_of(h * half, 16), half)]

        def ici(a, j, q):
            return pltpu.make_async_remote_copy(
                src_ref=slab(a, q, c), dst_ref=slab(a, q, c), send_sem=isend.at[3 * a + j], recv_sem=irecv.at[3 * a + j],
                device_id=peers[j], device_id_type=MESH)

        def d2d(a, j, h):
            return pltpu.make_async_remote_copy(
                src_ref=slab(a, peer_k[j], h), dst_ref=slab(a, peer_k[j], h), send_sem=dsend.at[3 * a + j],
                recv_sem=drecv.at[3 * a + j], device_id=(x, y, 1 - c), device_id_type=MESH)

        for a in range(n):
            for j in range(3):
                ici(a, j, k).start()
        for a in range(n):
            for j in range(3):
                ici(a, j, peer_k[j]).wait_recv()
                if split[a]:
                    d2d(a, j, c).start()
        for a in range(n):
            for j in range(3):
                if split[a]:
                    d2d(a, j, 1 - c).wait_recv()
        for a in range(n):
            for j in range(3):
                ici(a, j, k).wait_send()
                if split[a]:
                    d2d(a, j, c).wait_send()

    return pl.pallas_call(
        body, name=name, in_specs=[ANY] * n, out_specs=[ANY] * n,
        out_shape=[jax.ShapeDtypeStruct(b.shape, b.dtype) for b in bufs],
        input_output_aliases={i: i for i in range(n)},
        scratch_shapes=[pltpu.SemaphoreType.DMA((3 * n,))] * 4)(*bufs)


HBM = pl.BlockSpec(memory_space=pltpu.HBM)
SEM = pl.BlockSpec(memory_space=pltpu.SEMAPHORE)
DATAFLOW = pltpu.SideEffectType.DATAFLOW_SIDE_EFFECTING


def _chip_peers():
    x, y, c = _place()
    return 2 * x + y, [(1 - x, y, c), (x, 1 - y, c), (1 - x, 1 - y, c)], [2 * (1 - x) + y, 2 * x + (1 - y), 2 * (1 - x) + (1 - y)]


def _half_slab(ref, q, h):
    half = ref.shape[1] // 2
    return ref.at[q, pl.ds(pl.multiple_of(h * half, BF16_ROWS), half)]


def gather_start(bufs, groups, after, *, name):
    n = len(bufs)
    ng = len(groups)

    def body(*refs):
        ins = refs[:n]
        sends, recvs = refs[2 * n + 1:2 * n + 1 + ng], refs[2 * n + 1 + ng:2 * n + 1 + 2 * ng]
        token = refs[-1]
        c = lax.axis_index("c")
        k, peers, _ = _chip_peers()
        for gi, grp in enumerate(groups):
            for pos, a in enumerate(grp):
                for j in range(3):
                    pltpu.make_async_remote_copy(
                        src_ref=_half_slab(ins[a], k, c), dst_ref=_half_slab(ins[a], k, c), send_sem=sends[gi].at[3 * pos + j],
                        recv_sem=recvs[gi].at[3 * pos + j], device_id=peers[j], device_id_type=MESH).start()
        token[...] = jnp.zeros_like(token)

    sems = [pltpu.SemaphoreType.DMA((3 * len(grp),)) for grp in groups]
    outs = pl.pallas_call(
        body, name=name, in_specs=[HBM] * n + [ANY],
        out_specs=[HBM] * n + [SEM] * (2 * ng) + [pl.BlockSpec(memory_space=pltpu.VMEM)],
        out_shape=[pltpu.HBM(b.shape, b.dtype) for b in bufs] + sems + sems + [jax.ShapeDtypeStruct((8, LANES), f32)],
        input_output_aliases={i: i for i in range(n)},
        compiler_params=pltpu.CompilerParams(has_side_effects=DATAFLOW))(
            *[pltpu.with_memory_space_constraint(b, pltpu.HBM) for b in bufs], after)
    return outs[:n], outs[n:n + ng], outs[n + ng:n + 2 * ng], outs[-1]


def gather_wait(bufs, send_sems, recv_sems, after, *, name):
    n = len(bufs)

    def body(*refs):
        ins = refs[:n]
        send, recv = refs[n], refs[n + 1]
        c = lax.axis_index("c")
        k, peers, peer_k = _chip_peers()
        for a in range(n):
            for j in range(3):
                copy = pltpu.make_async_remote_copy(
                    src_ref=_half_slab(ins[a], k, c), dst_ref=_half_slab(ins[a], peer_k[j], c), send_sem=send.at[3 * a + j],
                    recv_sem=recv.at[3 * a + j], device_id=peers[j], device_id_type=MESH)
                copy.wait_send()
                copy.wait_recv()

    return pl.pallas_call(
        body, name=name, in_specs=[HBM] * n + [SEM, SEM, ANY], out_specs=[HBM] * n,
        out_shape=[pltpu.HBM(b.shape, b.dtype) for b in bufs],
        input_output_aliases={i: i for i in range(n)},
        compiler_params=pltpu.CompilerParams(has_side_effects=DATAFLOW))(*bufs, send_sems, recv_sems, after)


def forward_halves(bufs, *, name):
    n = len(bufs)

    def body(*refs):
        bufs_ = refs[:n]
        send, recv = refs[2 * n:]
        x, y, c = _place()
        _, _, peer_k = _chip_peers()

        def copy(a, j, h):
            return pltpu.make_async_remote_copy(
                src_ref=_half_slab(bufs_[a], peer_k[j], h), dst_ref=_half_slab(bufs_[a], peer_k[j], h),
                send_sem=send.at[3 * a + j], recv_sem=recv.at[3 * a + j], device_id=(x, y, 1 - c), device_id_type=MESH)

        for a in range(n):
            for j in range(3):
                copy(a, j, c).start()
        for a in range(n):
            for j in range(3):
                copy(a, j, 1 - c).wait_recv()
        for a in range(n):
            for j in range(3):
                copy(a, j, c).wait_send()

    return pl.pallas_call(
        body, name=name, in_specs=[ANY] * n, out_specs=[ANY] * n,
        out_shape=[jax.ShapeDtypeStruct(b.shape, b.dtype) for b in bufs],
        input_output_aliases={i: i for i in range(n)},
        scratch_shapes=[pltpu.SemaphoreType.DMA((3 * n,))] * 2)(*bufs)


def _forward_copies(bufs_, send, recv, h):
    x, y, c = _place()
    _, _, peer_k = _chip_peers()
    return [pltpu.make_async_remote_copy(
        src_ref=_half_slab(bufs_[a], peer_k[j], h), dst_ref=_half_slab(bufs_[a], peer_k[j], h),
        send_sem=send.at[3 * a + j], recv_sem=recv.at[3 * a + j], device_id=(x, y, 1 - c), device_id_type=MESH)
        for a in range(len(bufs_)) for j in range(3)]


def forward_start(bufs, after, *, name):
    n = len(bufs)

    def body(*refs):
        for cp in _forward_copies(refs[:n], refs[2 * n + 1], refs[2 * n + 2], lax.axis_index("c")):
            cp.start()
        refs[-1][...] = jnp.zeros_like(refs[-1])

    sems = [pltpu.SemaphoreType.DMA((3 * n,))] * 2
    outs = pl.pallas_call(
        body, name=name, in_specs=[HBM] * n + [ANY],
        out_specs=[HBM] * n + [SEM] * 2 + [pl.BlockSpec(memory_space=pltpu.VMEM)],
        out_shape=[pltpu.HBM(b.shape, b.dtype) for b in bufs] + sems + [jax.ShapeDtypeStruct((8, LANES), f32)],
        input_output_aliases={i: i for i in range(n)},
        compiler_params=pltpu.CompilerParams(has_side_effects=DATAFLOW))(*bufs, after)
    return (n, outs[:-1]), outs[-1]


def forward_wait(state, after, *, name):
    n, held = state

    def body(*refs):
        c = lax.axis_index("c")
        for mine, theirs in zip(_forward_copies(refs[:n], refs[n], refs[n + 1], c),
                                _forward_copies(refs[:n], refs[n], refs[n + 1], 1 - c)):
            mine.wait_send()
            theirs.wait_recv()

    return pl.pallas_call(
        body, name=name, in_specs=[HBM] * n + [SEM] * 2 + [ANY], out_specs=[HBM] * n,
        out_shape=[pltpu.HBM(b.shape, b.dtype) for b in held[:n]],
        input_output_aliases={i: i for i in range(n)},
        compiler_params=pltpu.CompilerParams(has_side_effects=DATAFLOW))(*held, after)


def _sibling_copies(srcs, lands, send, recv):
    x, y, c = _place()
    return [pltpu.make_async_remote_copy(src_ref=srcs[a], dst_ref=lands[a], send_sem=send.at[a], recv_sem=recv.at[a],
                                         device_id=(x, y, 1 - c), device_id_type=MESH) for a in range(len(srcs))]


def sibling_start(arrays, after, *, name):
    n = len(arrays)
    lands = [pltpu.with_memory_space_constraint(lax.empty(a.shape, a.dtype), pltpu.HBM) for a in arrays]

    def body(*refs):
        for cp in _sibling_copies(refs[:n], refs[n:2 * n], refs[4 * n + 1], refs[4 * n + 2]):
            cp.start()
        refs[-1][...] = jnp.zeros_like(refs[-1])

    bufs = list(arrays) + lands
    sems = [pltpu.SemaphoreType.DMA((n,))] * 2
    outs = pl.pallas_call(
        body, name=name, in_specs=[HBM] * (2 * n) + [ANY],
        out_specs=[HBM] * (2 * n) + [SEM] * 2 + [pl.BlockSpec(memory_space=pltpu.VMEM)],
        out_shape=[pltpu.HBM(b.shape, b.dtype) for b in bufs] + sems + [jax.ShapeDtypeStruct((8, LANES), f32)],
        input_output_aliases={i: i for i in range(2 * n)},
        compiler_params=pltpu.CompilerParams(has_side_effects=DATAFLOW))(
            *[pltpu.with_memory_space_constraint(b, pltpu.HBM) for b in bufs], after)
    return (n, outs[:-1]), outs[-1]


def sibling_wait(state, after, *, name):
    n, held = state

    def body(*refs):
        for cp in _sibling_copies(refs[:n], refs[n:2 * n], refs[2 * n], refs[2 * n + 1]):
            cp.wait_send()
            cp.wait_recv()

    outs = pl.pallas_call(
        body, name=name, in_specs=[HBM] * (2 * n) + [SEM] * 2 + [ANY], out_specs=[HBM] * (2 * n),
        out_shape=[pltpu.HBM(b.shape, b.dtype) for b in held[:2 * n]],
        input_output_aliases={i: i for i in range(2 * n)},
        compiler_params=pltpu.CompilerParams(has_side_effects=DATAFLOW))(*held, after)
    return outs[n:]


def sibling_exchange(arrays, *, name):
    n = len(arrays)

    def body(*refs):
        ins, outs = refs[:n], refs[n:2 * n]
        send, recv = refs[2 * n:]
        x, y, c = _place()

        def copy(a):
            return pltpu.make_async_remote_copy(
                src_ref=ins[a], dst_ref=outs[a], send_sem=send.at[a], recv_sem=recv.at[a],
                device_id=(x, y, 1 - c), device_id_type=MESH)

        for a in range(n):
            copy(a).start()
        for a in range(n):
            copy(a).wait_recv()
        for a in range(n):
            copy(a).wait_send()

    return pl.pallas_call(
        body, name=name, in_specs=[ANY] * n, out_specs=[ANY] * n,
        out_shape=[jax.ShapeDtypeStruct(a.shape, a.dtype) for a in arrays],
        scratch_shapes=[pltpu.SemaphoreType.DMA((n,)), pltpu.SemaphoreType.DMA((n,))])(*arrays)


ALL_MASKS = [(mx, my, mc) for mx in (0, 1) for my in (0, 1) for mc in (0, 1)][1:]


def _scatter_copies(srcs, lands, ev, send, recv, esend, erecv):
    x, y, c = _place()
    me = 4 * x + 2 * y + c
    k, peers, peer_k = _chip_peers()
    out = []
    for a in range(len(srcs)):
        for j in range(3):
            out.append(pltpu.make_async_remote_copy(
                src_ref=srcs[a].at[peer_k[j]], dst_ref=lands[a].at[j], send_sem=send.at[3 * a + j],
                recv_sem=recv.at[3 * a + j], device_id=peers[j], device_id_type=MESH))
    start_ev, wait_ev = [], []
    if ev is not None:
        for j, (mx, my, mc) in enumerate(ALL_MASKS):
            peer = (x ^ mx, y ^ my, c ^ mc)
            start_ev.append(pltpu.make_async_remote_copy(
                src_ref=ev.at[me], dst_ref=ev.at[me], send_sem=esend.at[j], recv_sem=erecv.at[j],
                device_id=peer, device_id_type=MESH))
            wait_ev.append(pltpu.make_async_remote_copy(
                src_ref=ev.at[me], dst_ref=ev.at[me ^ (4 * mx + 2 * my + mc)], send_sem=esend.at[j],
                recv_sem=erecv.at[j], device_id=peer, device_id_type=MESH))
    return out, start_ev, wait_ev


def chip_scatter_start(arrays, everyone, after, *, name):
    n = len(arrays)
    ne = 0 if everyone is None else 1
    lands = [pltpu.with_memory_space_constraint(lax.empty((3,) + a.shape[1:], a.dtype), pltpu.HBM) for a in arrays]

    def body(*refs):
        srcs, lands_ = refs[:n], refs[n:2 * n]
        ev = refs[2 * n] if ne else None
        sems = refs[2 * n + ne + 1 + 2 * n + ne:-1]
        send, recv = sems[0], sems[1]
        esend, erecv = (sems[2], sems[3]) if ne else (None, None)
        copies, start_ev, _ = _scatter_copies(srcs, lands_, ev, send, recv, esend, erecv)
        for cp in start_ev + copies:
            cp.start()
        refs[-1][...] = jnp.zeros_like(refs[-1])

    sem_shapes = [pltpu.SemaphoreType.DMA((3 * n,))] * 2 + [pltpu.SemaphoreType.DMA((7,))] * (2 * ne)
    bufs = list(arrays) + lands + ([everyone] if ne else [])
    outs = pl.pallas_call(
        body, name=name, in_specs=[HBM] * len(bufs) + [ANY],
        out_specs=[HBM] * len(bufs) + [SEM] * len(sem_shapes) + [pl.BlockSpec(memory_space=pltpu.VMEM)],
        out_shape=[pltpu.HBM(b.shape, b.dtype) for b in bufs] + sem_shapes + [jax.ShapeDtypeStruct((8, LANES), f32)],
        input_output_aliases={i: i for i in range(len(bufs))},
        compiler_params=pltpu.CompilerParams(has_side_effects=DATAFLOW))(
            *[pltpu.with_memory_space_constraint(b, pltpu.HBM) for b in bufs], after)
    return (n, ne, outs[:-1]), outs[-1]


def chip_scatter_wait(state, after, *, name):
    n, ne, held = state
    nb = 2 * n + ne
    bufs, sems = held[:nb], held[nb:]

    def body(*refs):
        srcs, lands_ = refs[:n], refs[n:2 * n]
        ev = refs[2 * n] if ne else None
        sems_ = refs[nb:nb + len(sems)]
        esend, erecv = (sems_[2], sems_[3]) if ne else (None, None)
        copies, _, wait_ev = _scatter_copies(srcs, lands_, ev, sems_[0], sems_[1], esend, erecv)
        for cp in wait_ev + copies:
            cp.wait_send()
            cp.wait_recv()

    outs = pl.pallas_call(
        body, name=name, in_specs=[HBM] * nb + [SEM] * len(sems) + [ANY], out_specs=[HBM] * nb,
        out_shape=[pltpu.HBM(b.shape, b.dtype) for b in bufs],
        input_output_aliases={i: i for i in range(nb)},
        compiler_params=pltpu.CompilerParams(has_side_effects=DATAFLOW))(*bufs, *sems, after)
    return outs[n:2 * n], (outs[2 * n] if ne else None)


def sibling_merge(bufs, *, name):
    n = len(bufs)

    def body(*refs):
        bufs_ = refs[:n]
        send, recv = refs[2 * n:]
        x, y, c = _place()

        def copy(u, h):
            return pltpu.make_async_remote_copy(
                src_ref=bufs_[u].at[h], dst_ref=bufs_[u].at[h], send_sem=send.at[u], recv_sem=recv.at[u],
                device_id=(x, y, 1 - c), device_id_type=MESH)

        for u in range(n):
            copy(u, c).start()
        for u in range(n):
            copy(u, 1 - c).wait_recv()
        for u in range(n):
            copy(u, c).wait_send()

    return pl.pallas_call(
        body, name=name, in_specs=[ANY] * n, out_specs=[ANY] * n,
        out_shape=[jax.ShapeDtypeStruct(b.shape, b.dtype) for b in bufs],
        input_output_aliases={i: i for i in range(n)},
        scratch_shapes=[pltpu.SemaphoreType.DMA((n,)), pltpu.SemaphoreType.DMA((n,))])(*bufs)


def sum_leading(a, *, name):
    n, r, c = a.shape

    def body(a_ref, o_ref):
        acc = a_ref[0]
        for i in range(1, n):
            acc = acc + a_ref[i]
        o_ref[...] = acc

    rb = r // 2 if r % 16 == 0 else r
    return pl.pallas_call(
        body, name=name, grid=(r // rb,), in_specs=[pl.BlockSpec((n, rb, c), lambda i: (0, i, 0))],
        out_specs=pl.BlockSpec((rb, c), lambda i: (i, 0)), out_shape=jax.ShapeDtypeStruct((r, c), f32),
        compiler_params=_params("parallel"))(a)


def _half_rows(shape):
    return shape[1] // 2 // 2


def rs_cast_other_half(grads, c_arr, *, name):
    n = len(grads)

    def body(c_ref, *refs):
        for i_ref, o_ref in zip(refs[:n], refs[n:]):
            o_ref[...] = i_ref[...].astype(bf16)

    in_specs = [pl.BlockSpec((1, _half_rows(g.shape), g.shape[2]), lambda s, r, c_ref: (s, (1 - c_ref[0]) * 2 + r, 0))
                for g in grads]
    out_specs = [pl.BlockSpec((1, _half_rows(g.shape), g.shape[2]), lambda s, r, c_ref: (s, r, 0)) for g in grads]
    return pl.pallas_call(
        body, name=name,
        grid_spec=pltpu.PrefetchScalarGridSpec(num_scalar_prefetch=1, grid=(N_SHARDS, 2),
                                               in_specs=in_specs, out_specs=out_specs),
        out_shape=[jax.ShapeDtypeStruct((N_SHARDS, g.shape[1] // 2, g.shape[2]), bf16) for g in grads],
        compiler_params=_params("parallel", "parallel"))(c_arr, *grads)


def rs_add_sibling(grads, recvd, ck_arr, *, name):
    n = len(grads)

    def body(ck_ref, *refs):
        s = pl.program_id(1)
        for u in range(n):
            g_ref, r_ref = refs[u], refs[n + u]
            qb_ref, own_ref = refs[2 * n + u], refs[3 * n + u]
            q = g_ref[0] + r_ref[0].astype(f32)
            qb_ref[0] = q.astype(bf16)

            @pl.when(s == ck_ref[1])
            def _(own_ref=own_ref, q=q):
                own_ref[...] = q

    in_specs = [pl.BlockSpec((1, _half_rows(g.shape), g.shape[2]), lambda r, s, ck: (s, ck[0] * 2 + r, 0)) for g in grads]
    in_specs += [pl.BlockSpec((1, _half_rows(g.shape), g.shape[2]), lambda r, s, ck: (s, r, 0)) for g in grads]
    out_specs = [pl.BlockSpec((1, _half_rows(g.shape), g.shape[2]), lambda r, s, ck: (s, r, 0)) for g in grads]
    out_specs += [pl.BlockSpec((_half_rows(g.shape), g.shape[2]), lambda r, s, ck: (r, 0)) for g in grads]
    outs = pl.pallas_call(
        body, name=name,
        grid_spec=pltpu.PrefetchScalarGridSpec(num_scalar_prefetch=1, grid=(2, N_SHARDS),
                                               in_specs=in_specs, out_specs=out_specs),
        out_shape=[jax.ShapeDtypeStruct((N_SHARDS, g.shape[1] // 2, g.shape[2]), bf16) for g in grads]
        + [jax.ShapeDtypeStruct((g.shape[1] // 2, g.shape[2]), f32) for g in grads],
        compiler_params=_params("parallel", "arbitrary"))(ck_arr, *grads, *recvd)
    return outs[:n], outs[n:]


def rs_sum_chips(owns, recvd, ck_arr, *, name):
    n = len(owns)

    def body(ck_ref, *refs):
        for u in range(n):
            own_ref, r_ref, o_ref = refs[u], refs[n + u], refs[2 * n + u]
            o_ref[0] = ((own_ref[...] + r_ref[0].astype(f32)) + r_ref[1].astype(f32)) + r_ref[2].astype(f32)

    in_specs = [pl.BlockSpec((o.shape[0] // 2, o.shape[1]), lambda r, ck: (r, 0)) for o in owns]
    in_specs += [pl.BlockSpec((3, o.shape[0] // 2, o.shape[1]), lambda r, ck: (0, r, 0)) for o in owns]
    out_specs = [pl.BlockSpec((1, o.shape[0] // 2, o.shape[1]), lambda r, ck: (ck[0], r, 0)) for o in owns]
    return pl.pallas_call(
        body, name=name,
        grid_spec=pltpu.PrefetchScalarGridSpec(num_scalar_prefetch=1, grid=(2,), in_specs=in_specs, out_specs=out_specs),
        out_shape=[jax.ShapeDtypeStruct((2,) + o.shape, f32) for o in owns],
        compiler_params=_params("parallel"))(ck_arr, *owns, *recvd)


SMALL = ("a_norm", "a_v_norm", "a_w_s", "a_b_s", "f_norm", "f_conv_w", "f_conv_b", "kv_norm", "k_norm",
         "b_norm", "b_q_norm", "b_sinks")
BIG = ("a_w_in", "a_w_out", "f_w_in", "f_w_out", "w_kv", "b_w_q", "b_w_o")
PACK_COLS = 1024
PACK_ROWS = 8 * N_STEPS


def _pack(parts, rows=PACK_ROWS):
    flat = jnp.concatenate([p.reshape(-1).astype(f32) for p in parts])
    pad = (-flat.shape[0]) % (rows * PACK_COLS)
    return jnp.pad(flat, (0, pad)).reshape(-1, PACK_COLS)


def _unpack(packed, shapes):
    flat = packed.reshape(-1)
    out, off = [], 0
    for s in shapes:
        size = math.prod(s)
        out.append(flat[off:off + size].reshape(s))
        off += size
    return out


def _ffn_fwd(x, g, h, r, w_in4, conv_w, conv_b, f, tag):
    wg, wu = conv_w[:, :f], conv_w[:, f:]
    bg, bu = conv_b[None, :f], conv_b[None, f:]
    pg, pu, gate, up, a = ffn_in_fused(h, w_in4, wg, wu, bg, bu, name=f"ffn{tag}_in")
    return a, (x, g, h, r, pg, pu, gate, up, a, wg, wu)


def _ffn_bwd(dy, saved, w_in4, w_out, c_arr, tag, exchange=False):
    x, g, h, r, pg, pu, gate, up, a, wg, wu = saved
    f = w_out.shape[0]
    d_w_out = mm_tn(a, [dy], c_arr, name=f"ffn{tag}_dwout", n_s=w_out.shape[1], shard_rows=f // N_SHARDS, tki=f // 2)
    dpg, dpu, sg, su = ffn_gate_bwd(dy, w_out, pg, pu, gate, up, wg, wu, name=f"ffn{tag}_dgate")
    d_w_in = mm_tn(h, [dpg, dpu], c_arr, name=f"ffn{tag}_dwin", n_s=w_in4.shape[2], shard_rows=h.shape[1])
    state = None
    if exchange:
        state, token = sibling_start([d_w_in[1], d_w_out[1]], d_w_in[0], name=f"rs_sibling_start_ffn{tag}")
        g = g + token[0, 0]
    dx, dg = mm_nt_rms_bwd([dpg, dpu], w_in4, x, r, g, dy, name=f"ffn{tag}_dh")
    d_conv_w = jnp.concatenate([sg[0:3], su[0:3]], axis=1)
    d_conv_b = jnp.concatenate([sg[3], su[3]], axis=0)
    return dx, dg, d_w_in, d_conv_w, d_conv_b, d_w_out, state


def _rs_front(pairs, sibling_state, after, c_arr, tag):
    units = [full.reshape(N_SHARDS, -1, full.shape[-1]) for full, _ in pairs]
    from_sib = sibling_wait(sibling_state, after, name=f"rs_sibling_wait{tag}")
    return rs_add_sibling(units, from_sib, c_arr, name=f"rs_add{tag}")


def _rs_back(own, from_chips, c_arr, tag):
    halves = rs_sum_chips(list(own), list(from_chips), c_arr, name=f"rs_sum{tag}")
    return [m.reshape(-1, m.shape[2]) for m in sibling_merge(list(halves), name=f"rs_merge{tag}")]


def kernel(x, a_norm, a_w_in, a_v_norm, a_w_s, a_b_s, a_w_out, f_norm, f_w_in, f_conv_w, f_conv_b, f_w_out, kv_norm, w_kv, k_norm, b_norm, b_w_q, b_q_norm, b_sinks, b_w_o, loss_target, m_a_norm, m_a_w_in, m_a_v_norm, m_a_w_s, m_a_b_s, m_a_w_out, m_f_norm, m_f_w_in, m_f_conv_w, m_f_conv_b, m_f_w_out, m_kv_norm, m_w_kv, m_k_norm, m_b_norm, m_b_w_q, m_b_q_norm, m_b_sinks, m_b_w_o, v_a_norm, v_a_w_in, v_a_v_norm, v_a_w_s, v_a_b_s, v_a_w_out, v_f_norm, v_f_w_in, v_f_conv_w, v_f_conv_b, v_f_w_out, v_kv_norm, v_w_kv, v_k_norm, v_b_norm, v_b_w_q, v_b_q_norm, v_b_sinks, v_b_w_o):
    args = dict(locals())
    weights = {n: args[n] for n in SMALL + BIG}
    moms = {n: args["m_" + n] for n in SMALL + BIG}
    vars_ = {n: args["v_" + n] for n in SMALL + BIG}
    t, d = x.shape[1], x.shape[2]
    xi, yi, ci = _place()
    chip = 2 * xi + yi

    big_local = [a_w_in[0], a_w_out[0], f_w_in, f_w_out, w_kv, b_w_q[0], b_w_o[0]]
    c_arr = jnp.stack([ci, chip]).astype(jnp.int32)
    k_arr = jnp.stack([chip]).astype(jnp.int32)
    b_ain, b_aout, b_fin0, b_fin1, b_fout0, b_fout1, b_kv, b_q, b_o = cast_into_slot(big_local, k_arr, name="cast_weights")
    small_cols = _pack([a_norm, a_v_norm, f_conv_w], rows=8)
    b_small = lax.dynamic_update_slice(jnp.zeros((N_SHARDS,) + small_cols.shape, f32), small_cols[None], (chip, 0, 0))
    g_small, w_a_in, g_a_w_out = gather_shards([b_small, b_ain, b_aout], name="gather_first", split=[False, True, True])
    later, send_sems, recv_sems, token = gather_start([b_fin0, b_fout0, b_kv, b_q, b_o, b_fin1, b_fout1],
                                                      [[0], [1], [2, 3, 4], [5, 6]], g_small, name="gather_start")
    ns_cols = a_norm.shape[1]
    nf_cols = f_conv_w.shape[2]
    parts = [_unpack(g_small[k], [a_norm.shape, a_v_norm.shape, f_conv_w.shape]) for k in range(N_SHARDS)]
    a_norm_f = jnp.concatenate([p[0] for p in parts], axis=1) + token[0, 0]
    a_v_norm_f = jnp.concatenate([p[1] for p in parts], axis=1)
    conv_w_f = jnp.concatenate([p[2] for p in parts], axis=2)

    x0 = x[0]
    tril = jnp.tril(jnp.ones((CHUNK, CHUNK), dtype=bool))
    wc = jnp.where(tril[None], a_w_s[0], 0.0).astype(bf16)
    bt = a_b_s[0].T
    kg2 = jnp.tile(k_norm, 2)[None]
    qg2 = jnp.tile(b_q_norm[0], 2)[None]

    (h_a,), r_a = rms_fwd(x0, [a_norm_f], name="a_norm")
    zu = mm_nn(h_a, w_a_in, name="a_in_u", s0=0, ns=2)
    zv = mm_nn(h_a, w_a_in, name="a_in_v", s0=2, ns=2)
    y_a = sgu_gate_fwd(zu, zv, a_v_norm_f, wc, bt, name="a_gate")
    w_a_out = g_a_w_out.reshape(1, -1, d)
    f = f_w_out.shape[1] * N_SHARDS
    fwd0, tok0 = forward_start(gather_wait(later[0:1], send_sems[0], recv_sems[0], y_a, name="gather_wait_0"), y_a,
                               name="gather_forward_start_0")
    x1, (h_f0,), r_f0 = mm_residual(y_a, w_a_out[0], x0, name="a_out", gains=[f_norm[0:1] + tok0[0, 0]])
    (g_fin0,) = forward_wait(fwd0, x1, name="gather_forward_wait_0")
    w_f_in = [g_fin0, None]
    a0, ffn0 = _ffn_fwd(x1, f_norm[0:1], h_f0, r_f0, w_f_in[0], conv_w_f[0], f_conv_b[0], f, "0")
    (g_fout0,) = forward_halves(gather_wait(later[1:2], send_sems[1], recv_sems[1], a0, name="gather_wait_1"),
                                name="gather_forward_1")
    w_f_out = [g_fout0.reshape(-1, d), None]
    fwd1, tok1 = forward_start(gather_wait(later[2:5], send_sems[2], recv_sems[2], g_fout0, name="gather_wait_1b"), a0,
                               name="gather_forward_start_1b")
    x2, (h_k, h_q), r_b = mm_residual(a0, w_f_out[0], x1, name="ffn0_out", gains=[kv_norm[None] + tok1[0, 0], b_norm])
    g_w_kv, g_b_w_q, g_b_w_o = forward_wait(fwd1, x2, name="gather_forward_wait_1b")
    w_kv_f = g_w_kv.reshape(1, d, -1)
    w_q_f = g_b_w_q.reshape(1, d, -1)
    w_o_f = g_b_w_o.reshape(1, -1, d)
    kv = mm_nn(h_k, w_kv_f, name="kv_proj")
    k2, v2 = kv_post_fwd(kv, kg2, name="kv_post")
    qp = mm_nn(h_q, w_q_f, name="q_proj")
    qn = q_norm_fwd(qp, qg2, name="q_norm", scale=HEAD_DIM ** -0.5)
    fwd2, tok2 = forward_start(gather_wait(later[5:7], send_sems[3], recv_sems[3], qn, name="gather_wait_2"), qn,
                               name="gather_forward_start_2")
    o = attn_fwd(qn, k2, v2, b_sinks[0] + tok2[0, 0], name="attn")
    x3, (h_f1,), r_f1 = mm_residual(o, w_o_f[0], x2, name="o_proj", gains=[f_norm[1:2]])
    g_fin1, g_fout1 = forward_wait(fwd2, x3, name="gather_forward_wait_2")
    w_f_in[1] = g_fin1
    w_f_out[1] = g_fout1.reshape(-1, d)
    a1, ffn1 = _ffn_fwd(x3, f_norm[1:2], h_f1, r_f1, w_f_in[1], conv_w_f[1], f_conv_b[1], f, "1")
    dx4, sq = mm_residual(a1, w_f_out[1], x3, name="ffn1_out", target=loss_target[0])
    loss_part = (0.5 * jnp.sum(sq) / d).reshape(1)

    proj_rows = d // N_SHARDS
    dx3, d_fn1, d_fwin1, d_cw1, d_cb1, d_fwout1, _ = _ffn_bwd(dx4, ffn1, w_f_in[1], w_f_out[1], c_arr, "1")
    do = mm_nt([dx3], w_o_f, name="o_proj_dx")
    d_w_o = mm_tn(o, [dx3], c_arr, name="o_proj_dw", n_s=d, shard_rows=o.shape[1] // N_SHARDS)
    dqn, dk2, dv2, dsink = attn_bwd(qn, k2, v2, do, b_sinks[0], name="attn_bwd")
    dqp, dqg = q_norm_bwd(dqn, qp, qg2, name="q_norm_bwd", scale=HEAD_DIM ** -0.5)
    dkv, dkg = kv_post_bwd(dk2, dv2, kv, kg2, name="kv_post_bwd")
    d_w_q = mm_tn(h_q, [dqp], c_arr, name="q_proj_dw", n_s=w_q_f.shape[2], shard_rows=proj_rows)
    d_w_kv = mm_tn(h_k, [dkv], c_arr, name="kv_proj_dw", n_s=w_kv_f.shape[2], shard_rows=proj_rows)
    group1 = [d_fwin1, d_fwout1, d_w_kv, d_w_q, d_w_o]
    sib1, token_s1 = sibling_start([half for _, half in group1], d_w_kv[0], name="rs_sibling_start1")
    dh_k = mm_nt([dkv], w_kv_f, name="kv_proj_dx")
    dx2, d_bn, d_kvn = mm_nt_rms_bwd([dqp], w_q_f, x2, r_b, b_norm + token_s1[0, 0], dx3, name="q_proj_dx",
                                     extra=(dh_k, kv_norm[None]))
    chip_bf1, own1 = _rs_front(group1, sib1, dx2, c_arr, "1")
    scatter1, token1 = chip_scatter_start(list(chip_bf1), None, dx2, name="rs_chips_start1")
    ffn0 = ffn0[:9] + (ffn0[9] + token1[0, 0],) + ffn0[10:]
    dx1, d_fn0, d_fwin0, d_cw0, d_cb0, d_fwout0, sib2 = _ffn_bwd(dx2, ffn0, w_f_in[0], w_f_out[0], c_arr, "0", exchange=True)
    chip_bf2, own2 = _rs_front([d_fwin0, d_fwout0], sib2, dx1, c_arr, "2")
    scatter2, token2 = chip_scatter_start(list(chip_bf2), None, dx1, name="rs_chips_start2")
    a_v_norm_f = a_v_norm_f + token2[0, 0]
    dy_a = mm_nt([dx1], w_a_out, name="a_out_dx")
    d_w_aout = mm_tn(y_a, [dx1], c_arr, name="a_out_dw", n_s=d, shard_rows=y_a.shape[1] // N_SHARDS)
    dzu, dzv, d_avn, d_ws, d_bt = sgu_gate_bwd(zu, zv, dy_a, a_v_norm_f, wc, bt, name="a_gate_bwd")
    d_w_ain = mm_tn(h_a, [dzu, dzv], c_arr, name="a_in_dw", n_s=w_a_in.shape[2], shard_rows=d)
    sib3, token_s3 = sibling_start([d_w_ain[1], d_w_aout[1]], d_w_ain[0], name="rs_sibling_start3")
    dx0, d_an = mm_nt_rms_bwd([dzu, dzv], w_a_in, x0, r_a, a_norm_f + token_s3[0, 0], dx1, name="a_in_dx")
    grad_x = dx0[None]

    chip_bf3, own3 = _rs_front([d_w_ain, d_w_aout], sib3, dx0, c_arr, "3")
    d_fn = jnp.concatenate([d_fn0, d_fn1], axis=0)
    d_cw = jnp.stack([d_cw0, d_cw1])
    d_cb = jnp.stack([d_cb0, d_cb1])
    d_kg = (dkg[0, :HEAD_DIM] + dkg[0, HEAD_DIM:])
    d_qg = (dqg[0, :HEAD_DIM] + dqg[0, HEAD_DIM:])[None]
    small_full = [d_an, d_avn, d_ws[None], d_bt.T[None], d_fn, d_cw, d_cb, d_kvn[0], d_kg, d_bn, d_qg,
                  dsink[:, :N_Q_HEADS], loss_part]
    packed = _pack(small_full)
    me = 4 * xi + 2 * yi + ci
    everyone = lax.dynamic_update_slice(lax.empty((N_DEV,) + packed.shape, f32), packed[None], (me, 0, 0))
    scatter3, token3 = chip_scatter_start(list(chip_bf3), everyone, own3[0], name="rs_chips_start3")
    from_chips1, _ = chip_scatter_wait(scatter1, token3, name="rs_chips_wait1")
    from_chips2, _ = chip_scatter_wait(scatter2, from_chips1[0], name="rs_chips_wait2")
    fin1, fout1, gkv, gq, go, fin0, fout0 = _rs_back(list(own1) + list(own2), list(from_chips1) + list(from_chips2),
                                                     c_arr, "12")
    late = ("f_w_in", "f_w_out", "w_kv", "b_w_q", "b_w_o")
    res_late = adamw([weights[n] for n in late], [[fin0, fin1], [fout0, fout1], [gkv], [gq], [go]],
                     [moms[n] for n in late], [vars_[n] for n in late], name="adamw_late")
    from_chips3, from_all = chip_scatter_wait(scatter3, res_late[1][2], name="rs_chips_wait3")
    ain, aout = _rs_back(own3, from_chips3, c_arr, "3")
    first = ("a_w_in", "a_w_out")
    res_first = adamw([weights[n] for n in first], [[ain], [aout]], [moms[n] for n in first],
                      [vars_[n] for n in first], name="adamw_first")
    big = {n: tuple(r[i] for r in res_late) for i, n in enumerate(late)}
    big.update({n: tuple(r[i] for r in res_first) for i, n in enumerate(first)})

    full_shapes = [g.shape for g in small_full]
    small_g = _unpack(sum_leading(from_all, name="small_sum"), full_shapes)
    loss = small_g.pop()[0]
    small_g[0] = lax.dynamic_slice_in_dim(small_g[0], chip * ns_cols, ns_cols, axis=1)
    small_g[1] = lax.dynamic_slice_in_dim(small_g[1], chip * ns_cols, ns_cols, axis=1)
    small_g[5] = lax.dynamic_slice_in_dim(small_g[5], chip * nf_cols, nf_cols, axis=2)
    small_shapes = [weights[n].shape for n in SMALL]
    small_g = [g.reshape(s) for g, s in zip(small_g, small_shapes)]
    flat2 = [(math.prod(s[:-1]), s[-1]) for s in small_shapes]
    small_d, small_m, small_v = adamw_small(
        *[[a.reshape(s2) for a, s2 in zip(group, flat2)]
          for group in ([weights[n] for n in SMALL], small_g, [moms[n] for n in SMALL], [vars_[n] for n in SMALL])],
        name="adamw_small")
    small_d, small_m, small_v = ([a.reshape(s) for a, s in zip(group, small_shapes)]
                                 for group in (small_d, small_m, small_v))

    out = {}
    for i, n in enumerate(SMALL):
        out[n] = (small_g[i], small_d[i], small_m[i], small_v[i])
    out.update(big)
    order = ["a_norm", "a_w_in", "a_v_norm", "a_w_s", "a_b_s", "a_w_out", "f_norm", "f_w_in", "f_conv_w", "f_conv_b",
             "f_w_out", "kv_norm", "w_kv", "k_norm", "b_norm", "b_w_q", "b_q_norm", "b_sinks", "b_w_o"]
    return (loss, grad_x, *[out[n][0] for n in order], *[out[n][1] for n in order],
            *[out[n][2] for n in order], *[out[n][3] for n in order])
```

```python
import functools
import math

import jax
import jax.numpy as jnp
from jax import lax
from jax.experimental import pallas as pl
from jax.experimental.pallas import tpu as pltpu

f32 = jnp.float32
bf16 = jnp.bfloat16
MESH = pl.DeviceIdType.MESH
ANY = pl.BlockSpec(memory_space=pl.ANY)

EPS = 1e-6
LANES = 128
CHUNK = 128
HEAD_DIM = 64
N_Q_HEADS = 16
N_KV_HEADS = 4
Q_PER_KV = N_Q_HEADS // N_KV_HEADS
N_SHARDS = 4
N_DEV = 8

ADAM_LR = 0.001
ADAM_B1 = 0.9
ADAM_B2 = 0.999
ADAM_EPS = 1e-08
ADAM_WD = 0.01
ADAM_STEP = 10
ADAM_C1 = 1.0 - ADAM_B1 ** ADAM_STEP
ADAM_C2 = 1.0 - ADAM_B2 ** ADAM_STEP

_INV_SQRT2 = 1.0 / math.sqrt(2.0)
_INV_SQRT2PI = 1.0 / math.sqrt(2.0 * math.pi)


def _params(*sem):
    return pltpu.CompilerParams(dimension_semantics=sem)


def _gelu(z):
    return 0.5 * z * (1.0 + lax.erf(z * _INV_SQRT2))


def _gelu_and_grad(z):
    cdf = 0.5 * (1.0 + lax.erf(z * _INV_SQRT2))
    return z * cdf, cdf + z * jnp.exp(-0.5 * z * z) * _INV_SQRT2PI


def _dot(a, b):
    return jnp.dot(a, b, preferred_element_type=f32)


def _dot_nt(a, b):
    return lax.dot_general(a, b, (((1,), (1,)), ((), ())), preferred_element_type=f32)


def _dot_tn(a, b):
    return lax.dot_general(a, b, (((0,), (0,)), ((), ())), preferred_element_type=f32)


def _dot_split(a, b):
    hi = a.astype(bf16)
    lo = (a - hi.astype(f32)).astype(bf16)
    return _dot(hi, b) + _dot(lo, b)


VMEM_TILE_BUDGET = 40 * 1024 * 1024
MAX_ROW_TILE = 2048


def _row_tile(m, fixed_bytes, row_bytes):
    tm = min(m, MAX_ROW_TILE)
    while tm > 256 and 2 * (fixed_bytes + tm * row_bytes) > VMEM_TILE_BUDGET:
        tm //= 2
    return tm


def _isz(a):
    return jnp.dtype(a.dtype).itemsize


def mm_nn(a, w3, *, name, s0=0, ns=None, add=None, out_dtype=f32):
    m, k = a.shape
    s_all, _, n_s = w3.shape
    ns = s_all if ns is None else ns
    tm = _row_tile(m, k * n_s * 2, k * _isz(a) + n_s * jnp.dtype(out_dtype).itemsize + (0 if add is None else n_s * 4))

    def body(*refs):
        if add is None:
            a_ref, w_ref, o_ref = refs
            acc = _dot(a_ref[...].astype(bf16), w_ref[0])
        else:
            a_ref, w_ref, add_ref, o_ref = refs
            acc = _dot(a_ref[...].astype(bf16), w_ref[0]) + add_ref[...]
        o_ref[...] = acc.astype(out_dtype)

    in_specs = [pl.BlockSpec((tm, k), lambda j, i: (i, 0)),
                pl.BlockSpec((1, k, n_s), lambda j, i: (s0 + j, 0, 0))]
    args = [a, w3]
    if add is not None:
        in_specs.append(pl.BlockSpec((tm, n_s), lambda j, i: (i, j)))
        args.append(add)
    return pl.pallas_call(
        body, name=name, grid=(ns, m // tm), in_specs=in_specs,
        out_specs=pl.BlockSpec((tm, n_s), lambda j, i: (i, j)),
        out_shape=jax.ShapeDtypeStruct((m, ns * n_s), out_dtype),
        compiler_params=_params("parallel", "parallel"))(*args)


def mm_nt(a_list, w3, *, name, tko=None, add=None, out_dtype=f32):
    s_all, k_out, n_s = w3.shape
    m = a_list[0].shape[0]
    na = len(a_list)
    spa = s_all // na
    tko = k_out if tko is None else tko
    tm = _row_tile(m, tko * n_s * 2, na * n_s * _isz(a_list[0]) + tko * 4 * (1 if add is None else 2))

    def body(*refs):
        a_refs = refs[:na]
        w_ref = refs[na]
        o_ref = refs[-1]
        s = pl.program_id(2)

        @pl.when(s == 0)
        def _():
            if add is None:
                o_ref[...] = jnp.zeros_like(o_ref)
            else:
                o_ref[...] = refs[na + 1][...]

        for idx in range(na):
            @pl.when(s // spa == idx)
            def _(idx=idx):
                o_ref[...] += _dot_nt(a_refs[idx][...].astype(bf16), w_ref[0])

    def a_map(idx):
        return lambda ko, i, s: (i, jnp.clip(s - idx * spa, 0, spa - 1))

    in_specs = [pl.BlockSpec((tm, n_s), a_map(idx)) for idx in range(na)]
    in_specs.append(pl.BlockSpec((1, tko, n_s), lambda ko, i, s: (s, ko, 0)))
    args = list(a_list) + [w3]
    if add is not None:
        in_specs.append(pl.BlockSpec((tm, tko), lambda ko, i, s: (i, ko)))
        args.append(add)
    return pl.pallas_call(
        body, name=name, grid=(k_out // tko, m // tm, s_all), in_specs=in_specs,
        out_specs=pl.BlockSpec((tm, tko), lambda ko, i, s: (i, ko)),
        out_shape=jax.ShapeDtypeStruct((m, k_out), out_dtype),
        compiler_params=_params("parallel", "parallel", "arbitrary"))(*args)


def mm_tn(a, b_list, c_arr, *, name, n_s, shard_rows, tki=None):
    m, k_in = a.shape
    na = len(b_list)
    s_all = sum(b.shape[1] for b in b_list) // n_s
    spa = s_all // na
    tki = k_in if tki is None else tki
    tm = _row_tile(m, tki * n_s * 4, tki * _isz(a) + na * n_s * _isz(b_list[0]))

    nsteps = m // tm
    per_blk = tki // shard_rows
    half = shard_rows // 2

    def body(c_ref, *refs):
        a_ref = refs[0]
        b_refs = refs[1:1 + na]
        o_ref, ob_ref = refs[-2], refs[-1]
        s = pl.program_id(0)
        r = pl.program_id(2)

        @pl.when(r == 0)
        def _():
            o_ref[...] = jnp.zeros_like(o_ref)

        for idx in range(na):
            @pl.when(s // spa == idx)
            def _(idx=idx):
                o_ref[0] += _dot_tn(a_ref[...].astype(bf16), b_refs[idx][...].astype(bf16))

        @pl.when(r == nsteps - 1)
        def _():
            for q in range(per_blk):
                start = pl.multiple_of(q * shard_rows + (1 - c_ref[0]) * half, 16)
                ob_ref[q] = o_ref[0, pl.ds(start, half), :].astype(bf16)

    def b_map(idx):
        def index(s, ki, r, c_ref):
            active = (s // spa) == idx
            return (jnp.where(active, r, 0), jnp.clip(s - idx * spa, 0, spa - 1))
        return index

    in_specs = [pl.BlockSpec((tm, tki), lambda s, ki, r, c_ref: (r, ki))]
    in_specs += [pl.BlockSpec((tm, n_s), b_map(idx)) for idx in range(na)]
    n_blk = k_in // tki
    return pl.pallas_call(
        body, name=name,
        grid_spec=pltpu.PrefetchScalarGridSpec(
            num_scalar_prefetch=1, grid=(s_all, n_blk, nsteps), in_specs=in_specs,
            out_specs=[pl.BlockSpec((1, tki, n_s), lambda s, ki, r, c_ref: (s, ki, 0)),
                       pl.BlockSpec((per_blk, half, n_s), lambda s, ki, r, c_ref: (s * n_blk + ki, 0, 0))]),
        out_shape=[jax.ShapeDtypeStruct((s_all, k_in, n_s), f32),
                   jax.ShapeDtypeStruct((s_all * k_in // shard_rows, half, n_s), bf16)],
        compiler_params=_params("parallel", "parallel", "arbitrary"))(c_arr, a, *b_list)


def mm_nt_rms_bwd(a_list, w3, x, r, g, dx_in, *, name, extra=None):
    s_all, d, n_s = w3.shape
    m = a_list[0].shape[0]
    na = len(a_list)
    spa = s_all // na
    ne = 0 if extra is None else 1
    tm = _row_tile(m, d * n_s * 2, na * n_s * _isz(a_list[0]) + d * 4 * (4 + ne))

    def body(*refs):
        a_refs, w_ref = refs[:na], refs[na]
        x_ref, r_ref, g_ref, dxin_ref = refs[na + 1:na + 5]
        dh2_ref, g2_ref = (refs[na + 5], refs[na + 6]) if ne else (None, None)
        outs = refs[na + 5 + 2 * ne:]
        dx_ref, dg_ref = outs[0], outs[1]
        dg2_ref = outs[2] if ne else None
        acc_ref = outs[-1]
        i, s = pl.program_id(0), pl.program_id(1)

        @pl.when(s == 0)
        def _():
            acc_ref[...] = jnp.zeros_like(acc_ref)

        for idx in range(na):
            @pl.when(s // spa == idx)
            def _(idx=idx):
                acc_ref[...] += _dot_nt(a_refs[idx][...].astype(bf16), w_ref[0])

        @pl.when(s == s_all - 1)
        def _():
            rv = r_ref[...]
            xh = x_ref[...] * rv
            total = dxin_ref[...]
            pairs = [(acc_ref[...], g_ref, dg_ref)] + ([(dh2_ref[...], g2_ref, dg2_ref)] if ne else [])
            for dh, gain_ref, dgain_ref in pairs:
                part = jnp.sum(dh * xh, axis=0, keepdims=True)

                @pl.when(i == 0)
                def _(dgain_ref=dgain_ref, part=part):
                    dgain_ref[...] = part

                @pl.when(i > 0)
                def _(dgain_ref=dgain_ref, part=part):
                    dgain_ref[...] += part

                tg = dh * gain_ref[...]
                total = total + rv * (tg - xh * jnp.mean(tg * xh, axis=1, keepdims=True))
            dx_ref[...] = total

    def a_map(idx):
        return lambda i, s: (i, jnp.clip(s - idx * spa, 0, spa - 1))

    row = pl.BlockSpec((tm, d), lambda i, s: (i, 0))
    vec = pl.BlockSpec((1, d), lambda i, s: (0, 0))
    in_specs = [pl.BlockSpec((tm, n_s), a_map(idx)) for idx in range(na)]
    in_specs += [pl.BlockSpec((1, d, n_s), lambda i, s: (s, 0, 0)), row, pl.BlockSpec((tm, 1), lambda i, s: (i, 0)), vec, row]
    args = list(a_list) + [w3, x, r, g, dx_in]
    if ne:
        in_specs += [row, vec]
        args += list(extra)
    outs = pl.pallas_call(
        body, name=name, grid=(m // tm, s_all), in_specs=in_specs, out_specs=[row] + [vec] * (1 + ne),
        out_shape=[jax.ShapeDtypeStruct((m, d), f32)] + [jax.ShapeDtypeStruct((1, d), f32)] * (1 + ne),
        scratch_shapes=[pltpu.VMEM((tm, d), f32)],
        compiler_params=_params("arbitrary", "arbitrary"))(*args)
    return outs


def mm_residual(a, w, x, *, name, gains=(), target=None):
    m, k = a.shape
    d = w.shape[1]
    ng = len(gains)
    tm = _row_tile(m, k * d * 2, k * _isz(a) + d * 4 * 3 + ng * d * 2)

    def body(*refs):
        a_ref, w_ref, x_ref = refs[:3]
        y = _dot(a_ref[...].astype(bf16), w_ref[...]) + x_ref[...]
        if target is None:
            g_refs = refs[3:3 + ng]
            y_ref = refs[3 + ng]
            h_refs = refs[4 + ng:4 + 2 * ng]
            r_ref = refs[-1]
            y_ref[...] = y
            r = lax.rsqrt(jnp.mean(y * y, axis=1, keepdims=True) + EPS)
            yh = y * r
            for g_ref, h_ref in zip(g_refs, h_refs):
                h_ref[...] = (yh * g_ref[...]).astype(bf16)
            r_ref[...] = r
        else:
            t_ref, dy_ref, s_ref = refs[3:]
            i = pl.program_id(0)
            e = y - t_ref[...]
            dy_ref[...] = e * (1.0 / d)
            part = jnp.sum(e * e, axis=0, keepdims=True)

            @pl.when(i == 0)
            def _():
                s_ref[...] = part

            @pl.when(i > 0)
            def _():
                s_ref[...] += part

    row = pl.BlockSpec((tm, d), lambda i: (i, 0))
    vec = pl.BlockSpec((1, d), lambda i: (0, 0))
    in_specs = [pl.BlockSpec((tm, k), lambda i: (i, 0)), pl.BlockSpec((k, d), lambda i: (0, 0)), row]
    if target is None:
        outs = pl.pallas_call(
            body, name=name, grid=(m // tm,), in_specs=in_specs + [vec] * ng,
            out_specs=[row] * (1 + ng) + [pl.BlockSpec((tm, 1), lambda i: (i, 0))],
            out_shape=[jax.ShapeDtypeStruct((m, d), f32)] + [jax.ShapeDtypeStruct((m, d), bf16)] * ng
            + [jax.ShapeDtypeStruct((m, 1), f32)],
            compiler_params=_params("parallel"))(a, w, x, *gains)
        return outs[0], outs[1:1 + ng], outs[-1]
    return pl.pallas_call(
        body, name=name, grid=(m // tm,), in_specs=in_specs + [row], out_specs=[row, vec],
        out_shape=[jax.ShapeDtypeStruct((m, d), f32), jax.ShapeDtypeStruct((1, d), f32)],
        compiler_params=_params("arbitrary"))(a, w, x, target)


def rms_fwd(x, gains, *, name, tr=512):
    t, d = x.shape
    tr = min(tr, t)
    ng = len(gains)

    def body(*refs):
        x_ref = refs[0]
        g_refs = refs[1:1 + ng]
        h_refs = refs[1 + ng:1 + 2 * ng]
        r_ref = refs[-1]
        xv = x_ref[...]
        r = lax.rsqrt(jnp.mean(xv * xv, axis=1, keepdims=True) + EPS)
        xh = xv * r
        for g_ref, h_ref in zip(g_refs, h_refs):
            h_ref[...] = (xh * g_ref[...]).astype(bf16)
        r_ref[...] = r

    row = pl.BlockSpec((tr, d), lambda i: (i, 0))
    vec = pl.BlockSpec((1, d), lambda i: (0, 0))
    outs = pl.pallas_call(
        body, name=name, grid=(t // tr,), in_specs=[row] + [vec] * ng,
        out_specs=[row] * ng + [pl.BlockSpec((tr, 1), lambda i: (i, 0))],
        out_shape=[jax.ShapeDtypeStruct((t, d), bf16)] * ng + [jax.ShapeDtypeStruct((t, 1), f32)],
        compiler_params=_params("parallel"))(x, *gains)
    return outs[:ng], outs[ng]


def rms_bwd(dh_list, x, r, gains, dx_in, *, name, tr=512):
    t, d = x.shape
    tr = min(tr, t)
    ng = len(gains)

    def body(*refs):
        dh_refs = refs[:ng]
        x_ref, r_ref = refs[ng], refs[ng + 1]
        g_refs = refs[ng + 2:2 * ng + 2]
        dxin_ref = refs[2 * ng + 2]
        dx_ref = refs[2 * ng + 3]
        dg_refs = refs[2 * ng + 4:]
        i = pl.program_id(0)
        rv = r_ref[...]
        xh = x_ref[...] * rv
        acc = dxin_ref[...]
        for dh_ref, g_ref, dg_ref in zip(dh_refs, g_refs, dg_refs):
            dh = dh_ref[...]
            part = jnp.sum(dh * xh, axis=0, keepdims=True)

            @pl.when(i == 0)
            def _(dg_ref=dg_ref, part=part):
                dg_ref[...] = part

            @pl.when(i > 0)
            def _(dg_ref=dg_ref, part=part):
                dg_ref[...] += part

            tg = dh * g_ref[...]
            acc = acc + rv * (tg - xh * jnp.mean(tg * xh, axis=1, keepdims=True))
        dx_ref[...] = acc

    row = pl.BlockSpec((tr, d), lambda i: (i, 0))
    vec = pl.BlockSpec((1, d), lambda i: (0, 0))
    outs = pl.pallas_call(
        body, name=name, grid=(t // tr,),
        in_specs=[row] * ng + [row, pl.BlockSpec((tr, 1), lambda i: (i, 0))] + [vec] * ng + [row],
        out_specs=[row] + [vec] * ng,
        out_shape=[jax.ShapeDtypeStruct((t, d), f32)] + [jax.ShapeDtypeStruct((1, d), f32)] * ng,
        compiler_params=_params("arbitrary"))(*dh_list, x, r, *gains, dx_in)
    return outs[0], outs[1:]


def sgu_gate_fwd(zuv, gv, wc, bt, *, name, tr=512):
    t, w = zuv.shape[0], zuv.shape[1] // 2
    tr = min(tr, t)
    groups = w // LANES

    def body(zu_ref, zv_ref, gv_ref, wc_ref, bt_ref, y_ref):
        vp = _gelu(zv_ref[...])
        rv = lax.rsqrt(jnp.mean(vp * vp, axis=1, keepdims=True) + EPS)
        vb = (vp * rv * gv_ref[...]).astype(bf16)
        for c in range(tr // CHUNK):
            rows = slice(c * CHUNK, (c + 1) * CHUNK)
            for g in range(groups):
                cols = slice(g * LANES, (g + 1) * LANES)
                sv = _dot(wc_ref[g], vb[rows, cols]) + bt_ref[:, g:g + 1]
                y_ref[rows, cols] = (_gelu(zu_ref[rows, cols]) * sv).astype(bf16)

    row = pl.BlockSpec((tr, w), lambda i: (i, 0))
    return pl.pallas_call(
        body, name=name, grid=(t // tr,),
        in_specs=[row, pl.BlockSpec((tr, w), lambda i: (i, 1)), pl.BlockSpec((1, w), lambda i: (0, 0)),
                  pl.BlockSpec((groups, CHUNK, CHUNK), lambda i: (0, 0, 0)),
                  pl.BlockSpec((CHUNK, groups), lambda i: (0, 0))],
        out_specs=row, out_shape=jax.ShapeDtypeStruct((t, w), bf16),
        compiler_params=_params("parallel"))(zuv, zuv, gv, wc, bt)


def sgu_gate_bwd(zuv, dy, gv, wc, bt, *, name, tr=512):
    t, w = zuv.shape[0], zuv.shape[1] // 2
    tr = min(tr, t)
    groups = w // LANES
    nsteps = t // tr

    def body(zu_ref, zv_ref, dy_ref, gv_ref, wc_ref, bt_ref,
             dzu_ref, dzv_ref, dgv_ref, dws_ref, dbt_ref, dv_ref, bacc_ref):
        i = pl.program_id(0)

        @pl.when(i == 0)
        def _():
            dgv_ref[...] = jnp.zeros_like(dgv_ref)
            dws_ref[...] = jnp.zeros_like(dws_ref)
            bacc_ref[...] = jnp.zeros_like(bacc_ref)

        vp, vp_grad = _gelu_and_grad(zv_ref[...])
        rv = lax.rsqrt(jnp.mean(vp * vp, axis=1, keepdims=True) + EPS)
        vhat = vp * rv
        vb = (vhat * gv_ref[...]).astype(bf16)
        for c in range(tr // CHUNK):
            rows = slice(c * CHUNK, (c + 1) * CHUNK)
            for g in range(groups):
                cols = slice(g * LANES, (g + 1) * LANES)
                vblk = vb[rows, cols]
                sv = _dot(wc_ref[g], vblk) + bt_ref[:, g:g + 1]
                zub = zu_ref[rows, cols]
                dyb = dy_ref[rows, cols]
                ub, ub_grad = _gelu_and_grad(zub)
                dzu_ref[rows, cols] = (dyb * sv * ub_grad).astype(bf16)
                dsv = dyb * ub
                bacc_ref[:, cols] += dsv
                dsvb = dsv.astype(bf16)
                dv_ref[rows, cols] = _dot_tn(wc_ref[g], dsvb)
                dws_ref[g] += _dot_nt(dsvb, vblk)
        dv = dv_ref[...]
        dgv_ref[...] += jnp.sum(dv * vhat, axis=0, keepdims=True)
        tg = dv * gv_ref[...]
        dvp = rv * (tg - vhat * jnp.mean(tg * vhat, axis=1, keepdims=True))
        dzv_ref[...] = (dvp * vp_grad).astype(bf16)

        @pl.when(i == nsteps - 1)
        def _():
            tt = lax.broadcasted_iota(jnp.int32, (CHUNK, CHUNK), 0)
            ss = lax.broadcasted_iota(jnp.int32, (CHUNK, CHUNK), 1)
            for g in range(groups):
                dws_ref[g] = jnp.where(ss <= tt, dws_ref[g], 0.0)
                dbt_ref[:, g:g + 1] = jnp.sum(bacc_ref[:, g * LANES:(g + 1) * LANES], axis=1, keepdims=True)

    row = pl.BlockSpec((tr, w), lambda i: (i, 0))
    full3 = pl.BlockSpec((groups, CHUNK, CHUNK), lambda i: (0, 0, 0))
    return pl.pallas_call(
        body, name=name, grid=(nsteps,),
        in_specs=[row, pl.BlockSpec((tr, w), lambda i: (i, 1)), row, pl.BlockSpec((1, w), lambda i: (0, 0)), full3,
                  pl.BlockSpec((CHUNK, groups), lambda i: (0, 0))],
        out_specs=[row, row, pl.BlockSpec((1, w), lambda i: (0, 0)), full3,
                   pl.BlockSpec((CHUNK, groups), lambda i: (0, 0))],
        out_shape=[jax.ShapeDtypeStruct((t, w), bf16), jax.ShapeDtypeStruct((t, w), bf16),
                   jax.ShapeDtypeStruct((1, w), f32), jax.ShapeDtypeStruct((groups, CHUNK, CHUNK), f32),
                   jax.ShapeDtypeStruct((CHUNK, groups), f32)],
        scratch_shapes=[pltpu.VMEM((tr, w), f32), pltpu.VMEM((CHUNK, w), f32)],
        compiler_params=_params("arbitrary"))(zuv, zuv, dy, gv, wc, bt)


HALO = 8


def _shift_down(v, halo, k, first):
    r = pltpu.roll(v, k, 0)
    hh = jnp.where(first, 0.0, pltpu.roll(halo, k, 0))
    rid = lax.broadcasted_iota(jnp.int32, (HALO, v.shape[1]), 0)
    head = jnp.where(rid < k, hh, r[0:HALO])
    if v.shape[0] == HALO:
        return head
    return jnp.concatenate([head, r[HALO:]], axis=0)


def _shift_up(v, halo, k, last):
    n = v.shape[0]
    r = pltpu.roll(v, n - k, 0)
    hh = jnp.where(last, 0.0, pltpu.roll(halo, HALO - k, 0))
    rid = lax.broadcasted_iota(jnp.int32, (HALO, v.shape[1]), 0)
    tail = jnp.where(rid >= HALO - k, hh, r[n - HALO:])
    return jnp.concatenate([r[:n - HALO], tail], axis=0)


def _conv(p, halo, w_ref, b_ref, first):
    return (w_ref[2:3, :] * p + w_ref[1:2, :] * _shift_down(p, halo, 1, first)
            + w_ref[0:1, :] * _shift_down(p, halo, 2, first) + b_ref[...])


BF16_ROWS = 16


def ffn_in_fused(h, w_in4, wg, wu, bg, bu, *, name):
    t, k = h.shape
    s_all, _, n_s = w_in4.shape
    half = s_all // 2
    tm = _row_tile(t, 2 * k * n_s * 2, k * 2 + 4 * n_s * 4 + n_s * 2)

    def body(h_ref, hh_ref, wg_ref, wu_ref, cg_ref, cu_ref, bg_ref, bu_ref, pg_ref, pu_ref, gate_ref, up_ref, a_ref):
        first = pl.program_id(1) == 0
        hv, hh = h_ref[...], hh_ref[...]
        outs = []
        for w_ref, c_ref, b_ref, p_ref, o_ref in ((wg_ref, cg_ref, bg_ref, pg_ref, gate_ref),
                                                  (wu_ref, cu_ref, bu_ref, pu_ref, up_ref)):
            p = _dot(hv, w_ref[0])
            p_ref[...] = p
            hu = _conv(p, _dot(hh, w_ref[0])[BF16_ROWS - HALO:], c_ref, b_ref, first)
            o_ref[...] = hu
            outs.append(hu)
        gate, up = outs
        a_ref[...] = (gate * jax.nn.sigmoid(gate) * up).astype(bf16)

    tile = pl.BlockSpec((tm, n_s), lambda j, i: (i, j))
    cw = pl.BlockSpec((3, n_s), lambda j, i: (0, j))
    cb = pl.BlockSpec((1, n_s), lambda j, i: (0, j))
    f = half * n_s
    return pl.pallas_call(
        body, name=name, grid=(half, t // tm),
        in_specs=[pl.BlockSpec((tm, k), lambda j, i: (i, 0)),
                  pl.BlockSpec((BF16_ROWS, k), lambda j, i: (jnp.maximum(i * (tm // BF16_ROWS) - 1, 0), 0)),
                  pl.BlockSpec((1, k, n_s), lambda j, i: (j, 0, 0)),
                  pl.BlockSpec((1, k, n_s), lambda j, i: (j + half, 0, 0)), cw, cw, cb, cb],
        out_specs=[tile] * 5,
        out_shape=[jax.ShapeDtypeStruct((t, f), f32)] * 4 + [jax.ShapeDtypeStruct((t, f), bf16)],
        compiler_params=_params("parallel", "parallel"))(h, h, w_in4, w_in4, wg, wu, bg, bu)


def _gate_grads(gate, up, dav):
    sg = jax.nn.sigmoid(gate)
    return dav * up * (sg * (1.0 + gate * (1.0 - sg))), dav * gate * sg


GATE_BWD_ROWS = 512


def ffn_gate_bwd(dy, w_out, pg, pu, gate, up, wg, wu, *, name):
    t, f = pg.shape
    d = dy.shape[1]
    tr = min(GATE_BWD_ROWS, t)
    nsteps = t // tr
    tc = f // 2

    def body(dy_ref, dyn_ref, w_ref, pg_ref, pu_ref, gate_ref, gaten_ref, up_ref, upn_ref, wg_ref, wu_ref,
             dg_ref, du_ref, sg_ref, su_ref):
        i = pl.program_id(1)
        last = i == nsteps - 1
        w = w_ref[0]
        da = _dot_nt(dy_ref[...].astype(bf16), w)
        da_n = _dot_nt(dyn_ref[...].astype(bf16), w)
        dgate, dup = _gate_grads(gate_ref[...], up_ref[...], da)
        dgate_n, dup_n = _gate_grads(gaten_ref[...], upn_ref[...], da_n)
        rid = lax.broadcasted_iota(jnp.int32, (8, tc), 0)
        for dd, d_n, c_ref, p_ref, o_ref, s_ref in ((dgate, dgate_n, wg_ref, pg_ref, dg_ref, sg_ref),
                                                    (dup, dup_n, wu_ref, pu_ref, du_ref, su_ref)):
            d1, d2 = _shift_up(dd, d_n, 1, last), _shift_up(dd, d_n, 2, last)
            o_ref[...] = (c_ref[2:3, :] * dd + c_ref[1:2, :] * d1 + c_ref[0:1, :] * d2).astype(bf16)
            p = p_ref[...]
            sums = [jnp.sum(d2 * p, axis=0, keepdims=True), jnp.sum(d1 * p, axis=0, keepdims=True),
                    jnp.sum(dd * p, axis=0, keepdims=True), jnp.sum(dd, axis=0, keepdims=True)]
            part = jnp.zeros((8, tc), f32)
            for k, sk in enumerate(sums):
                part = jnp.where(rid == k, sk, part)

            @pl.when(i == 0)
            def _(s_ref=s_ref, part=part):
                s_ref[...] = part

            @pl.when(i > 0)
            def _(s_ref=s_ref, part=part):
                s_ref[...] += part

    def nxt_rows(j, i):
        return (jnp.minimum((i + 1) * (tr // HALO), t // HALO - 1), j)

    tile = pl.BlockSpec((tr, tc), lambda j, i: (i, j))
    nxt = pl.BlockSpec((HALO, tc), nxt_rows)
    wspec = pl.BlockSpec((3, tc), lambda j, i: (0, j))
    stat = pl.BlockSpec((8, tc), lambda j, i: (0, j))
    return pl.pallas_call(
        body, name=name, grid=(2, nsteps),
        in_specs=[pl.BlockSpec((tr, d), lambda j, i: (i, 0)),
                  pl.BlockSpec((HALO, d), lambda j, i: (nxt_rows(j, i)[0], 0)),
                  pl.BlockSpec((1, tc, d), lambda j, i: (j, 0, 0)),
                  tile, tile, tile, nxt, tile, nxt, wspec, wspec],
        out_specs=[tile, tile, stat, stat],
        out_shape=[jax.ShapeDtypeStruct((t, f), bf16), jax.ShapeDtypeStruct((t, f), bf16),
                   jax.ShapeDtypeStruct((8, f), f32), jax.ShapeDtypeStruct((8, f), f32)],
        compiler_params=_params("parallel", "arbitrary"))(
            dy, dy, w_out.reshape(2, tc, d), pg, pu, gate, gate, up, up, wg, wu)


def _head_mean_matrix():
    i = lax.broadcasted_iota(jnp.int32, (LANES, LANES), 0) // HEAD_DIM
    j = lax.broadcasted_iota(jnp.int32, (LANES, LANES), 1) // HEAD_DIM
    return jnp.where(i == j, 1.0 / HEAD_DIM, 0.0).astype(bf16)


def _lane_half(shape):
    return (lax.broadcasted_iota(jnp.int32, shape, 1) % LANES) // HEAD_DIM


def q_norm_fwd(qp, g2, *, name, scale, tr=512):
    t, w = qp.shape
    tr = min(tr, t)

    def body(x_ref, g_ref, o_ref):
        bd = _head_mean_matrix()
        for cb in range(w // LANES):
            cols = slice(cb * LANES, (cb + 1) * LANES)
            xc = x_ref[:, cols]
            rh = lax.rsqrt(_dot_split(xc * xc, bd) + EPS)
            o_ref[:, cols] = (xc * rh * g_ref[...] * scale).astype(bf16)

    row = pl.BlockSpec((tr, w), lambda i: (i, 0))
    return pl.pallas_call(
        body, name=name, grid=(t // tr,), in_specs=[row, pl.BlockSpec((1, LANES), lambda i: (0, 0))],
        out_specs=row, out_shape=jax.ShapeDtypeStruct((t, w), bf16),
        compiler_params=_params("parallel"))(qp, g2)


def q_proj_norm(h, w, g2, *, name, scale):
    t, k = h.shape
    n = w.shape[1]
    tm = _row_tile(t, k * n * 2, k * 2 + n * 4 + n * 2)

    def body(h_ref, w_ref, g_ref, qp_ref, o_ref):
        qp_ref[...] = _dot(h_ref[...], w_ref[...])
        bd = _head_mean_matrix()
        for cb in range(n // LANES):
            cols = slice(cb * LANES, (cb + 1) * LANES)
            xc = qp_ref[:, cols]
            rh = lax.rsqrt(_dot_split(xc * xc, bd) + EPS)
            o_ref[:, cols] = (xc * rh * g_ref[...] * scale).astype(bf16)

    row = pl.BlockSpec((tm, n), lambda i: (i, 0))
    return pl.pallas_call(
        body, name=name, grid=(t // tm,),
        in_specs=[pl.BlockSpec((tm, k), lambda i: (i, 0)), pl.BlockSpec((k, n), lambda i: (0, 0)),
                  pl.BlockSpec((1, LANES), lambda i: (0, 0))],
        out_specs=[row, row], out_shape=[jax.ShapeDtypeStruct((t, n), f32), jax.ShapeDtypeStruct((t, n), bf16)],
        compiler_params=_params("parallel"))(h, w, g2)


def kv_proj_post(h, w, g2, *, name):
    t, k = h.shape
    n = w.shape[1]
    kw = n // 2
    tm = _row_tile(t, k * n * 2, k * 2 + n * 4 + 2 * n * 2)

    def body(h_ref, w_ref, g_ref, kv_ref, k_ref, v_ref):
        kv_ref[...] = _dot(h_ref[...], w_ref[...])
        bd = _head_mean_matrix()
        half = _lane_half((tm, LANES))
        for cb in range(kw // LANES):
            xc = kv_ref[:, cb * LANES:(cb + 1) * LANES]
            rh = lax.rsqrt(_dot_split(xc * xc, bd) + EPS)
            kn = xc * rh * g_ref[...]
            vc = kv_ref[:, kw + cb * LANES:kw + (cb + 1) * LANES]
            for src, dst in ((kn, k_ref), (vc, v_ref)):
                sw = pltpu.roll(src, HEAD_DIM, 1)
                for hf in range(2):
                    blk = 2 * cb + hf
                    dst[:, blk * LANES:(blk + 1) * LANES] = jnp.where(half == hf, src, sw).astype(bf16)

    row = pl.BlockSpec((tm, n), lambda i: (i, 0))
    return pl.pallas_call(
        body, name=name, grid=(t // tm,),
        in_specs=[pl.BlockSpec((tm, k), lambda i: (i, 0)), pl.BlockSpec((k, n), lambda i: (0, 0)),
                  pl.BlockSpec((1, LANES), lambda i: (0, 0))],
        out_specs=[row, row, row],
        out_shape=[jax.ShapeDtypeStruct((t, n), f32), jax.ShapeDtypeStruct((t, n), bf16), jax.ShapeDtypeStruct((t, n), bf16)],
        compiler_params=_params("parallel"))(h, w, g2)


def q_norm_bwd(dq, qp, g2, *, name, scale, tr=512):
    t, w = qp.shape
    tr = min(tr, t)

    def body(dq_ref, x_ref, g_ref, o_ref, dg_ref):
        i = pl.program_id(0)
        bd = _head_mean_matrix()
        acc = jnp.zeros((1, LANES), f32)
        for cb in range(w // LANES):
            cols = slice(cb * LANES, (cb + 1) * LANES)
            xc = x_ref[:, cols]
            rh = lax.rsqrt(_dot_split(xc * xc, bd) + EPS)
            xh = xc * rh
            dy = dq_ref[:, cols] * scale
            acc = acc + jnp.sum(dy * xh, axis=0, keepdims=True)
            tg = dy * g_ref[...]
            o_ref[:, cols] = (rh * (tg - xh * _dot_split(tg * xh, bd))).astype(bf16)

        @pl.when(i == 0)
        def _():
            dg_ref[...] = acc

        @pl.when(i > 0)
        def _():
            dg_ref[...] += acc

    row = pl.BlockSpec((tr, w), lambda i: (i, 0))
    vec = pl.BlockSpec((1, LANES), lambda i: (0, 0))
    return pl.pallas_call(
        body, name=name, grid=(t // tr,), in_specs=[row, row, vec], out_specs=[row, vec],
        out_shape=[jax.ShapeDtypeStruct((t, w), bf16), jax.ShapeDtypeStruct((1, LANES), f32)],
        compiler_params=_params("arbitrary"))(dq, qp, g2)


def kv_post_fwd(kv, g2, *, name, tr=512):
    t, w = kv.shape
    tr = min(tr, t)
    kw = w // 2

    def body(x_ref, g_ref, k_ref, v_ref):
        bd = _head_mean_matrix()
        half = _lane_half((tr, LANES))
        for cb in range(kw // LANES):
            xc = x_ref[:, cb * LANES:(cb + 1) * LANES]
            rh = lax.rsqrt(_dot_split(xc * xc, bd) + EPS)
            kn = xc * rh * g_ref[...]
            vc = x_ref[:, kw + cb * LANES:kw + (cb + 1) * LANES]
            for src, dst in ((kn, k_ref), (vc, v_ref)):
                sw = pltpu.roll(src, HEAD_DIM, 1)
                for hf in range(2):
                    blk = 2 * cb + hf
                    dst[:, blk * LANES:(blk + 1) * LANES] = jnp.where(half == hf, src, sw).astype(bf16)

    return pl.pallas_call(
        body, name=name, grid=(t // tr,),
        in_specs=[pl.BlockSpec((tr, w), lambda i: (i, 0)), pl.BlockSpec((1, LANES), lambda i: (0, 0))],
        out_specs=[pl.BlockSpec((tr, 2 * kw), lambda i: (i, 0))] * 2,
        out_shape=[jax.ShapeDtypeStruct((t, 2 * kw), bf16)] * 2,
        compiler_params=_params("parallel"))(kv, g2)


def kv_post_bwd(dk2, dv2, kv, g2, *, name, tr=512):
    t, w = kv.shape
    tr = min(tr, t)
    kw = w // 2

    def body(dk_ref, dv_ref, x_ref, g_ref, o_ref, dg_ref):
        i = pl.program_id(0)
        bd = _head_mean_matrix()
        half = _lane_half((tr, LANES))
        acc = jnp.zeros((1, LANES), f32)

        def fold(ref, cb):
            a = ref[:, (2 * cb) * LANES:(2 * cb + 1) * LANES]
            b = ref[:, (2 * cb + 1) * LANES:(2 * cb + 2) * LANES]
            return jnp.where(half == 0, a + pltpu.roll(a, HEAD_DIM, 1), b + pltpu.roll(b, HEAD_DIM, 1))

        for cb in range(kw // LANES):
            cols = slice(cb * LANES, (cb + 1) * LANES)
            xc = x_ref[:, cols]
            rh = lax.rsqrt(_dot_split(xc * xc, bd) + EPS)
            xh = xc * rh
            dy = fold(dk_ref, cb)
            acc = acc + jnp.sum(dy * xh, axis=0, keepdims=True)
            tg = dy * g_ref[...]
            o_ref[:, cols] = (rh * (tg - xh * _dot_split(tg * xh, bd))).astype(bf16)
            o_ref[:, kw + cb * LANES:kw + (cb + 1) * LANES] = fold(dv_ref, cb).astype(bf16)

        @pl.when(i == 0)
        def _():
            dg_ref[...] = acc

        @pl.when(i > 0)
        def _():
            dg_ref[...] += acc

    dup = pl.BlockSpec((tr, 2 * kw), lambda i: (i, 0))
    row = pl.BlockSpec((tr, w), lambda i: (i, 0))
    vec = pl.BlockSpec((1, LANES), lambda i: (0, 0))
    return pl.pallas_call(
        body, name=name, grid=(t // tr,), in_specs=[dup, dup, row, vec], out_specs=[row, vec],
        out_shape=[jax.ShapeDtypeStruct((t, w), bf16), jax.ShapeDtypeStruct((1, LANES), f32)],
        compiler_params=_params("arbitrary"))(dk2, dv2, kv, g2)


def _slope(h):
    return 2.0 ** (-8.0 * (h + 1) / N_Q_HEADS)


GROUP_ROWS = Q_PER_KV * CHUNK


def _band_mask(n):
    tq = lax.broadcasted_iota(jnp.int32, (GROUP_ROWS, 2 * CHUNK), 0) % CHUNK
    jk = lax.broadcasted_iota(jnp.int32, (GROUP_ROWS, 2 * CHUNK), 1)
    dist = tq + CHUNK - jk
    ok = (dist >= 0) & (dist < CHUNK) & jnp.logical_not((n == 0) & (jk < CHUNK))
    return dist.astype(f32), ok


def _band(ref, n, kh):
    p0 = pl.multiple_of(jnp.maximum(n - 1, 0) * CHUNK, CHUNK)
    c0 = pl.multiple_of(n * CHUNK, CHUNK)
    cols = slice(kh * LANES, (kh + 1) * LANES)
    return jnp.concatenate([ref[pl.ds(p0, CHUNK), cols], ref[pl.ds(c0, CHUNK), cols]], axis=0)


def _stack_heads(ref, kh, half):
    parts = []
    for cb in (2 * kh, 2 * kh + 1):
        xc = ref[:, cb * LANES:(cb + 1) * LANES].astype(f32)
        parts += [jnp.where(half == hf, xc, 0.0).astype(bf16) for hf in range(2)]
    return jnp.concatenate(parts, axis=0)


def _unstack_heads(x4, half):
    return (jnp.where(half == 0, x4[0:CHUNK], x4[CHUNK:2 * CHUNK]),
            jnp.where(half == 0, x4[2 * CHUNK:3 * CHUNK], x4[3 * CHUNK:]))


def _per_head_column(kh, values):
    grp = lax.broadcasted_iota(jnp.int32, (GROUP_ROWS, 1), 0) // CHUNK
    col = jnp.full((GROUP_ROWS, 1), values[0], f32)
    for g in range(1, Q_PER_KV):
        col = jnp.where(grp == g, values[g], col)
    return col


def _softmax_band(q4, kband, dist, ok, slope, sink):
    s = _dot_nt(q4, kband)
    s = jnp.where(ok, s - slope * dist, -jnp.inf)
    m = jnp.maximum(jnp.max(s, axis=1, keepdims=True), sink)
    e = jnp.exp(s - m)
    es = jnp.exp(sink - m)
    den = jnp.sum(e, axis=1, keepdims=True) + es
    return e / den, es / den


def attn_fwd(q, k2, v2, sinks, *, name):
    t, w = q.shape
    nb = t // CHUNK

    def body(sink_ref, q_ref, k_ref, v_ref, o_ref):
        n = pl.program_id(0)
        dist, ok = _band_mask(n)
        half = _lane_half((CHUNK, LANES))
        khs = range(N_KV_HEADS)
        heads = [[Q_PER_KV * kh + g for g in range(Q_PER_KV)] for kh in khs]
        q4 = [_stack_heads(q_ref, kh, half) for kh in khs]
        soft = [_softmax_band(q4[kh], _band(k_ref, n, kh), dist, ok, _per_head_column(kh, [_slope(h) for h in heads[kh]]),
                              _per_head_column(kh, [sink_ref[h] for h in heads[kh]])) for kh in khs]
        o4 = [_dot(soft[kh][0].astype(bf16), _band(v_ref, n, kh)) for kh in khs]
        for kh in khs:
            lo, hi = _unstack_heads(o4[kh], half)
            o_ref[:, (2 * kh) * LANES:(2 * kh + 1) * LANES] = lo.astype(bf16)
            o_ref[:, (2 * kh + 1) * LANES:(2 * kh + 2) * LANES] = hi.astype(bf16)

    full = pl.BlockSpec((t, k2.shape[1]), lambda n: (0, 0))
    return pl.pallas_call(
        body, name=name, grid=(nb,),
        in_specs=[pl.BlockSpec(memory_space=pltpu.SMEM), pl.BlockSpec((CHUNK, w), lambda n: (n, 0)), full, full],
        out_specs=pl.BlockSpec((CHUNK, w), lambda n: (n, 0)),
        out_shape=jax.ShapeDtypeStruct((t, w), bf16),
        compiler_params=_params("parallel"))(sinks, q, k2, v2)


def attn_bwd(q, k2, v2, do, sinks, *, name):
    t, w = q.shape
    nb = t // CHUNK
    kw = k2.shape[1]

    def body(sink_ref, q_ref, k_ref, v_ref, do_ref, dq_ref, dk_ref, dv_ref, ds_ref, kc_ref, vc_ref):
        n = pl.program_id(0)

        @pl.when(n == 0)
        def _():
            ds_ref[...] = jnp.zeros_like(ds_ref)
            kc_ref[...] = jnp.zeros_like(kc_ref)
            vc_ref[...] = jnp.zeros_like(vc_ref)
            dk_ref[...] = jnp.zeros_like(dk_ref)
            dv_ref[...] = jnp.zeros_like(dv_ref)

        @pl.when(n == nb)
        def _():
            dk_ref[...] = kc_ref[...]
            dv_ref[...] = vc_ref[...]

        @pl.when(n < nb)
        def _():
            dist, ok = _band_mask(n)
            half = _lane_half((CHUNK, LANES))
            lane = lax.broadcasted_iota(jnp.int32, (1, LANES), 1)
            sink_acc = jnp.zeros((1, LANES), f32)
            khs = range(N_KV_HEADS)
            heads = [[Q_PER_KV * kh + g for g in range(Q_PER_KV)] for kh in khs]
            q4 = [_stack_heads(q_ref, kh, half) for kh in khs]
            do4 = [_stack_heads(do_ref, kh, half) for kh in khs]
            kband = [_band(k_ref, n, kh) for kh in khs]
            vband = [_band(v_ref, n, kh) for kh in khs]
            soft = [_softmax_band(q4[kh], kband[kh], dist, ok, _per_head_column(kh, [_slope(h) for h in heads[kh]]),
                                  _per_head_column(kh, [sink_ref[h] for h in heads[kh]])) for kh in khs]
            dp = [_dot_nt(do4[kh], vband[kh]) for kh in khs]
            delta = [jnp.sum(soft[kh][0] * dp[kh], axis=1, keepdims=True) for kh in khs]
            dsb = [(soft[kh][0] * (dp[kh] - delta[kh])).astype(bf16) for kh in khs]
            dq4 = [_dot(dsb[kh], kband[kh]) for kh in khs]
            dkb = [_dot_tn(dsb[kh], q4[kh]) for kh in khs]
            dvb = [_dot_tn(soft[kh][0].astype(bf16), do4[kh]) for kh in khs]
            for kh in khs:
                sd = soft[kh][1] * delta[kh]
                for g, h in enumerate(heads[kh]):
                    part = jnp.sum(sd[g * CHUNK:(g + 1) * CHUNK], axis=0, keepdims=True)
                    sink_acc = sink_acc + jnp.where(lane == h, -part, 0.0)
                lo, hi = _unstack_heads(dq4[kh], half)
                dq_ref[:, (2 * kh) * LANES:(2 * kh + 1) * LANES] = lo
                dq_ref[:, (2 * kh + 1) * LANES:(2 * kh + 2) * LANES] = hi
                cols = slice(kh * LANES, (kh + 1) * LANES)
                dk_ref[:, cols] = kc_ref[:, cols] + dkb[kh][0:CHUNK]
                dv_ref[:, cols] = vc_ref[:, cols] + dvb[kh][0:CHUNK]
                kc_ref[:, cols] = dkb[kh][CHUNK:]
                vc_ref[:, cols] = dvb[kh][CHUNK:]
            ds_ref[...] += sink_acc

    full = pl.BlockSpec((t, kw), lambda n: (0, 0))
    qblk = pl.BlockSpec((CHUNK, w), lambda n: (jnp.minimum(n, nb - 1), 0))
    kblk = pl.BlockSpec((CHUNK, kw), lambda n: (jnp.maximum(n - 1, 0), 0))
    return pl.pallas_call(
        body, name=name, grid=(nb + 1,),
        in_specs=[pl.BlockSpec(memory_space=pltpu.SMEM), qblk, full, full, qblk],
        out_specs=[qblk, kblk, kblk, pl.BlockSpec((1, LANES), lambda n: (0, 0))],
        out_shape=[jax.ShapeDtypeStruct((t, w), f32), jax.ShapeDtypeStruct((t, kw), f32),
                   jax.ShapeDtypeStruct((t, kw), f32), jax.ShapeDtypeStruct((1, LANES), f32)],
        scratch_shapes=[pltpu.VMEM((CHUNK, kw), f32), pltpu.VMEM((CHUNK, kw), f32)],
        compiler_params=_params("arbitrary"))(sinks, q, k2, v2, do)


def loss_head(y, target, *, name, tr=512):
    t, d = y.shape
    tr = min(tr, t)

    def body(y_ref, t_ref, dy_ref, s_ref):
        i = pl.program_id(0)
        e = y_ref[...] - t_ref[...]
        dy_ref[...] = e * (1.0 / d)
        part = jnp.sum(e * e, axis=0, keepdims=True)

        @pl.when(i == 0)
        def _():
            s_ref[...] = part

        @pl.when(i > 0)
        def _():
            s_ref[...] += part

    row = pl.BlockSpec((tr, d), lambda i: (i, 0))
    vec = pl.BlockSpec((1, d), lambda i: (0, 0))
    return pl.pallas_call(
        body, name=name, grid=(t // tr,), in_specs=[row, row], out_specs=[row, vec],
        out_shape=[jax.ShapeDtypeStruct((t, d), f32), jax.ShapeDtypeStruct((1, d), f32)],
        compiler_params=_params("arbitrary"))(y, target)


N_STEPS = 8


def _row_blocks(shape):
    if len(shape) == 2:
        r, c = shape
        return (r // N_STEPS, c), (lambda s: (s, 0))
    l, r, c = shape
    per = N_STEPS // l
    return (1, r // per, c), (lambda s: (s // per, s % per, 0))


CAST_STEPS = 4


def cast_into_slot(arrays, k_arr, *, name):
    in_specs, out_specs, out_shape, layers = [], [], [], []
    for a in arrays:
        r, c = a.shape[-2:]
        rb = r // CAST_STEPS
        if a.ndim == 2:
            in_specs.append(pl.BlockSpec((rb, c), lambda s, k: (s, 0)))
            layers.append(None)
        else:
            for l in range(a.shape[0]):
                in_specs.append(pl.BlockSpec((1, rb, c), lambda s, k, l=l: (l, s, 0)))
                layers.append(l)
        for _ in range(1 if a.ndim == 2 else a.shape[0]):
            out_specs.append(pl.BlockSpec((1, rb, c), lambda s, k: (k[0], s, 0)))
            out_shape.append(jax.ShapeDtypeStruct((N_SHARDS, r, c), bf16))
    n = len(in_specs)

    def body(k_ref, *refs):
        for i_ref, o_ref, l in zip(refs[:n], refs[n:], layers):
            o_ref[0] = (i_ref[...] if l is None else i_ref[0]).astype(bf16)

    args = []
    for a in arrays:
        args += [a] * (1 if a.ndim == 2 else a.shape[0])
    return pl.pallas_call(
        body, name=name,
        grid_spec=pltpu.PrefetchScalarGridSpec(num_scalar_prefetch=1, grid=(CAST_STEPS,),
                                               in_specs=in_specs, out_specs=out_specs),
        out_shape=out_shape, compiler_params=_params("parallel"))(k_arr, *args)


def adamw(ws, gs, ms, vs, *, name):
    n = len(ws)
    specs, g_specs, g_count = [], [], []
    for w, g_list in zip(ws, gs):
        blk, index = _row_blocks(w.shape)
        specs.append(pl.BlockSpec(blk, index))
        layers = len(g_list)
        per = N_STEPS // layers
        g_count.append(layers)
        for l in range(layers):
            g_specs.append(pl.BlockSpec(blk[-2:], lambda s, l=l, per=per: (jnp.where(s // per == l, s % per, 0), 0)))
    ng = len(g_specs)

    def body(*refs):
        s = pl.program_id(0)
        g_refs = refs[3 * n:3 * n + ng]
        outs = refs[3 * n + ng:]
        off = 0
        for i in range(n):
            w_ref, m_ref, v_ref = refs[i], refs[n + i], refs[2 * n + i]
            go_ref, d_ref, nm_ref, nv_ref = (outs[k * n + i] for k in range(4))
            layers = g_count[i]
            g = g_refs[off][...]
            for l in range(1, layers):
                g = jnp.where(s // (N_STEPS // layers) == l, g_refs[off + l][...], g)
            off += layers
            g = g.reshape(w_ref.shape)
            m = ADAM_B1 * m_ref[...] + (1.0 - ADAM_B1) * g
            v = ADAM_B2 * v_ref[...] + (1.0 - ADAM_B2) * (g * g)
            m_hat = m / ADAM_C1
            v_hat = v / ADAM_C2
            go_ref[...] = g
            d_ref[...] = -ADAM_LR * (m_hat / (jnp.sqrt(v_hat) + ADAM_EPS) + ADAM_WD * w_ref[...])
            nm_ref[...] = m
            nv_ref[...] = v

    outs = pl.pallas_call(
        body, name=name, grid=(N_STEPS,), in_specs=specs * 3 + g_specs, out_specs=specs * 4,
        out_shape=[jax.ShapeDtypeStruct(a.shape, f32) for a in ws] * 4,
        compiler_params=_params("parallel"))(*ws, *ms, *vs, *[g for g_list in gs for g in g_list])
    return [outs[k * n:(k + 1) * n] for k in range(4)]


def _adamw_update(w, g, m, v):
    m = ADAM_B1 * m + (1.0 - ADAM_B1) * g
    v = ADAM_B2 * v + (1.0 - ADAM_B2) * (g * g)
    m_hat = m / ADAM_C1
    v_hat = v / ADAM_C2
    return -ADAM_LR * (m_hat / (jnp.sqrt(v_hat) + ADAM_EPS) + ADAM_WD * w), m, v


def adamw_small(ws, gs, ms, vs, *, name):
    n = len(ws)

    def body(*refs):
        for i in range(n):
            w_ref, g_ref, m_ref, v_ref = (refs[k * n + i] for k in range(4))
            d_ref, nm_ref, nv_ref = (refs[(4 + k) * n + i] for k in range(3))
            d_ref[...], nm_ref[...], nv_ref[...] = _adamw_update(w_ref[...], g_ref[...], m_ref[...], v_ref[...])

    outs = pl.pallas_call(
        body, name=name, out_shape=[jax.ShapeDtypeStruct(a.shape, f32) for a in ws] * 3)(*ws, *gs, *ms, *vs)
    return outs[:n], outs[n:2 * n], outs[2 * n:]


def _place():
    return lax.axis_index("x"), lax.axis_index("y"), lax.axis_index("c")


def gather_shards(bufs, *, name, split):
    n = len(bufs)

    def body(*refs):
        bufs_ = refs[:n]
        isend, irecv, dsend, drecv = refs[2 * n:]
        x, y, c = _place()
        k = 2 * x + y
        peers = [(1 - x, y, c), (x, 1 - y, c), (1 - x, 1 - y, c)]
        peer_k = [2 * (1 - x) + y, 2 * x + (1 - y), 2 * (1 - x) + (1 - y)]

        def slab(a, q, h):
            if not split[a]:
                return bufs_[a].at[q]
            half = bufs_[a].shape[1] // 2
            return bufs_[a].at[q, pl.ds(pl.multiple_of(h * half, 16), half)]

        def ici(a, j, q):
            return pltpu.make_async_remote_copy(
                src_ref=slab(a, q, c), dst_ref=slab(a, q, c), send_sem=isend.at[3 * a + j], recv_sem=irecv.at[3 * a + j],
                device_id=peers[j], device_id_type=MESH)

        def d2d(a, j, h):
            return pltpu.make_async_remote_copy(
                src_ref=slab(a, peer_k[j], h), dst_ref=slab(a, peer_k[j], h), send_sem=dsend.at[3 * a + j],
                recv_sem=drecv.at[3 * a + j], device_id=(x, y, 1 - c), device_id_type=MESH)

        for a in range(n):
            for j in range(3):
                ici(a, j, k).start()
        for a in range(n):
            for j in range(3):
                ici(a, j, peer_k[j]).wait_recv()
                if split[a]:
                    d2d(a, j, c).start()
        for a in range(n):
            for j in range(3):
                if split[a]:
                    d2d(a, j, 1 - c).wait_recv()
        for a in range(n):
            for j in range(3):
                ici(a, j, k).wait_send()
                if split[a]:
                    d2d(a, j, c).wait_send()

    return pl.pallas_call(
        body, name=name, in_specs=[ANY] * n, out_specs=[ANY] * n,
        out_shape=[jax.ShapeDtypeStruct(b.shape, b.dtype) for b in bufs],
        input_output_aliases={i: i for i in range(n)},
        scratch_shapes=[pltpu.SemaphoreType.DMA((3 * n,))] * 4)(*bufs)


HBM = pl.BlockSpec(memory_space=pltpu.HBM)
SEM = pl.BlockSpec(memory_space=pltpu.SEMAPHORE)
DATAFLOW = pltpu.SideEffectType.DATAFLOW_SIDE_EFFECTING


def _chip_peers():
    x, y, c = _place()
    return 2 * x + y, [(1 - x, y, c), (x, 1 - y, c), (1 - x, 1 - y, c)], [2 * (1 - x) + y, 2 * x + (1 - y), 2 * (1 - x) + (1 - y)]


def _half_slab(ref, q, h):
    half = ref.shape[1] // 2
    return ref.at[q, pl.ds(pl.multiple_of(h * half, BF16_ROWS), half)]


def gather_start(bufs, groups, after, *, name):
    n = len(bufs)
    ng = len(groups)

    def body(*refs):
        ins = refs[:n]
        sends, recvs = refs[2 * n + 1:2 * n + 1 + ng], refs[2 * n + 1 + ng:2 * n + 1 + 2 * ng]
        token = refs[-1]
        c = lax.axis_index("c")
        k, peers, _ = _chip_peers()
        for gi, grp in enumerate(groups):
            for pos, a in enumerate(grp):
                for j in range(3):
                    pltpu.make_async_remote_copy(
                        src_ref=_half_slab(ins[a], k, c), dst_ref=_half_slab(ins[a], k, c), send_sem=sends[gi].at[3 * pos + j],
                        recv_sem=recvs[gi].at[3 * pos + j], device_id=peers[j], device_id_type=MESH).start()
        token[...] = jnp.zeros_like(token)

    sems = [pltpu.SemaphoreType.DMA((3 * len(grp),)) for grp in groups]
    outs = pl.pallas_call(
        body, name=name, in_specs=[HBM] * n + [ANY],
        out_specs=[HBM] * n + [SEM] * (2 * ng) + [pl.BlockSpec(memory_space=pltpu.VMEM)],
        out_shape=[pltpu.HBM(b.shape, b.dtype) for b in bufs] + sems + sems + [jax.ShapeDtypeStruct((8, LANES), f32)],
        input_output_aliases={i: i for i in range(n)},
        compiler_params=pltpu.CompilerParams(has_side_effects=DATAFLOW))(
            *[pltpu.with_memory_space_constraint(b, pltpu.HBM) for b in bufs], after)
    return outs[:n], outs[n:n + ng], outs[n + ng:n + 2 * ng], outs[-1]


def gather_wait(bufs, send_sems, recv_sems, after, *, name):
    n = len(bufs)

    def body(*refs):
        ins = refs[:n]
        send, recv = refs[n], refs[n + 1]
        c = lax.axis_index("c")
        k, peers, peer_k = _chip_peers()
        for a in range(n):
            for j in range(3):
                copy = pltpu.make_async_remote_copy(
                    src_ref=_half_slab(ins[a], k, c), dst_ref=_half_slab(ins[a], peer_k[j], c), send_sem=send.at[3 * a + j],
                    recv_sem=recv.at[3 * a + j], device_id=peers[j], device_id_type=MESH)
                copy.wait_send()
                copy.wait_recv()

    return pl.pallas_call(
        body, name=name, in_specs=[HBM] * n + [SEM, SEM, ANY], out_specs=[HBM] * n,
        out_shape=[pltpu.HBM(b.shape, b.dtype) for b in bufs],
        input_output_aliases={i: i for i in range(n)},
        compiler_params=pltpu.CompilerParams(has_side_effects=DATAFLOW))(*bufs, send_sems, recv_sems, after)


def forward_halves(bufs, *, name):
    n = len(bufs)

    def body(*refs):
        bufs_ = refs[:n]
        send, recv = refs[2 * n:]
        x, y, c = _place()
        _, _, peer_k = _chip_peers()

        def copy(a, j, h):
            return pltpu.make_async_remote_copy(
                src_ref=_half_slab(bufs_[a], peer_k[j], h), dst_ref=_half_slab(bufs_[a], peer_k[j], h),
                send_sem=send.at[3 * a + j], recv_sem=recv.at[3 * a + j], device_id=(x, y, 1 - c), device_id_type=MESH)

        for a in range(n):
            for j in range(3):
                copy(a, j, c).start()
        for a in range(n):
            for j in range(3):
                copy(a, j, 1 - c).wait_recv()
        for a in range(n):
            for j in range(3):
                copy(a, j, c).wait_send()

    return pl.pallas_call(
        body, name=name, in_specs=[ANY] * n, out_specs=[ANY] * n,
        out_shape=[jax.ShapeDtypeStruct(b.shape, b.dtype) for b in bufs],
        input_output_aliases={i: i for i in range(n)},
        scratch_shapes=[pltpu.SemaphoreType.DMA((3 * n,))] * 2)(*bufs)


def _forward_copies(bufs_, send, recv, h):
    x, y, c = _place()
    _, _, peer_k = _chip_peers()
    return [pltpu.make_async_remote_copy(
        src_ref=_half_slab(bufs_[a], peer_k[j], h), dst_ref=_half_slab(bufs_[a], peer_k[j], h),
        send_sem=send.at[3 * a + j], recv_sem=recv.at[3 * a + j], device_id=(x, y, 1 - c), device_id_type=MESH)
        for a in range(len(bufs_)) for j in range(3)]


def forward_start(bufs, after, *, name):
    n = len(bufs)

    def body(*refs):
        for cp in _forward_copies(refs[:n], refs[2 * n + 1], refs[2 * n + 2], lax.axis_index("c")):
            cp.start()
        refs[-1][...] = jnp.zeros_like(refs[-1])

    sems = [pltpu.SemaphoreType.DMA((3 * n,))] * 2
    outs = pl.pallas_call(
        body, name=name, in_specs=[HBM] * n + [ANY],
        out_specs=[HBM] * n + [SEM] * 2 + [pl.BlockSpec(memory_space=pltpu.VMEM)],
        out_shape=[pltpu.HBM(b.shape, b.dtype) for b in bufs] + sems + [jax.ShapeDtypeStruct((8, LANES), f32)],
        input_output_aliases={i: i for i in range(n)},
        compiler_params=pltpu.CompilerParams(has_side_effects=DATAFLOW))(*bufs, after)
    return (n, outs[:-1]), outs[-1]


def forward_wait(state, after, *, name):
    n, held = state

    def body(*refs):
        c = lax.axis_index("c")
        for mine, theirs in zip(_forward_copies(refs[:n], refs[n], refs[n + 1], c),
                                _forward_copies(refs[:n], refs[n], refs[n + 1], 1 - c)):
            mine.wait_send()
            theirs.wait_recv()

    return pl.pallas_call(
        body, name=name, in_specs=[HBM] * n + [SEM] * 2 + [ANY], out_specs=[HBM] * n,
        out_shape=[pltpu.HBM(b.shape, b.dtype) for b in held[:n]],
        input_output_aliases={i: i for i in range(n)},
        compiler_params=pltpu.CompilerParams(has_side_effects=DATAFLOW))(*held, after)


def _sibling_copies(srcs, lands, send, recv):
    x, y, c = _place()
    return [pltpu.make_async_remote_copy(src_ref=srcs[a], dst_ref=lands[a], send_sem=send.at[a], recv_sem=recv.at[a],
                                         device_id=(x, y, 1 - c), device_id_type=MESH) for a in range(len(srcs))]


def sibling_start(arrays, after, *, name):
    n = len(arrays)
    lands = [pltpu.with_memory_space_constraint(lax.empty(a.shape, a.dtype), pltpu.HBM) for a in arrays]

    def body(*refs):
        for cp in _sibling_copies(refs[:n], refs[n:2 * n], refs[4 * n + 1], refs[4 * n + 2]):
            cp.start()
        refs[-1][...] = jnp.zeros_like(refs[-1])

    bufs = list(arrays) + lands
    sems = [pltpu.SemaphoreType.DMA((n,))] * 2
    outs = pl.pallas_call(
        body, name=name, in_specs=[HBM] * (2 * n) + [ANY],
        out_specs=[HBM] * (2 * n) + [SEM] * 2 + [pl.BlockSpec(memory_space=pltpu.VMEM)],
        out_shape=[pltpu.HBM(b.shape, b.dtype) for b in bufs] + sems + [jax.ShapeDtypeStruct((8, LANES), f32)],
        input_output_aliases={i: i for i in range(2 * n)},
        compiler_params=pltpu.CompilerParams(has_side_effects=DATAFLOW))(
            *[pltpu.with_memory_space_constraint(b, pltpu.HBM) for b in bufs], after)
    return (n, outs[:-1]), outs[-1]


def sibling_wait(state, after, *, name):
    n, held = state

    def body(*refs):
        for cp in _sibling_copies(refs[:n], refs[n:2 * n], refs[2 * n], refs[2 * n + 1]):
            cp.wait_send()
            cp.wait_recv()

    outs = pl.pallas_call(
        body, name=name, in_specs=[HBM] * (2 * n) + [SEM] * 2 + [ANY], out_specs=[HBM] * (2 * n),
        out_shape=[pltpu.HBM(b.shape, b.dtype) for b in held[:2 * n]],
        input_output_aliases={i: i for i in range(2 * n)},
        compiler_params=pltpu.CompilerParams(has_side_effects=DATAFLOW))(*held, after)
    return outs[n:]


def sibling_exchange(arrays, *, name):
    n = len(arrays)

    def body(*refs):
        ins, outs = refs[:n], refs[n:2 * n]
        send, recv = refs[2 * n:]
        x, y, c = _place()

        def copy(a):
            return pltpu.make_async_remote_copy(
                src_ref=ins[a], dst_ref=outs[a], send_sem=send.at[a], recv_sem=recv.at[a],
                device_id=(x, y, 1 - c), device_id_type=MESH)

        for a in range(n):
            copy(a).start()
        for a in range(n):
            copy(a).wait_recv()
        for a in range(n):
            copy(a).wait_send()

    return pl.pallas_call(
        body, name=name, in_specs=[ANY] * n, out_specs=[ANY] * n,
        out_shape=[jax.ShapeDtypeStruct(a.shape, a.dtype) for a in arrays],
        scratch_shapes=[pltpu.SemaphoreType.DMA((n,)), pltpu.SemaphoreType.DMA((n,))])(*arrays)


ALL_MASKS = [(mx, my, mc) for mx in (0, 1) for my in (0, 1) for mc in (0, 1)][1:]


def _scatter_copies(srcs, lands, ev, send, recv, esend, erecv):
    x, y, c = _place()
    me = 4 * x + 2 * y + c
    k, peers, peer_k = _chip_peers()
    out = []
    for a in range(len(srcs)):
        for j in range(3):
            out.append(pltpu.make_async_remote_copy(
                src_ref=srcs[a].at[peer_k[j]], dst_ref=lands[a].at[j], send_sem=send.at[3 * a + j],
                recv_sem=recv.at[3 * a + j], device_id=peers[j], device_id_type=MESH))
    start_ev, wait_ev = [], []
    if ev is not None:
        for j, (mx, my, mc) in enumerate(ALL_MASKS):
            peer = (x ^ mx, y ^ my, c ^ mc)
            start_ev.append(pltpu.make_async_remote_copy(
                src_ref=ev.at[me], dst_ref=ev.at[me], send_sem=esend.at[j], recv_sem=erecv.at[j],
                device_id=peer, device_id_type=MESH))
            wait_ev.append(pltpu.make_async_remote_copy(
                src_ref=ev.at[me], dst_ref=ev.at[me ^ (4 * mx + 2 * my + mc)], send_sem=esend.at[j],
                recv_sem=erecv.at[j], device_id=peer, device_id_type=MESH))
    return out, start_ev, wait_ev


def chip_scatter_start(arrays, everyone, after, *, name):
    n = len(arrays)
    ne = 0 if everyone is None else 1
    lands = [pltpu.with_memory_space_constraint(lax.empty((3,) + a.shape[1:], a.dtype), pltpu.HBM) for a in arrays]

    def body(*refs):
        srcs, lands_ = refs[:n], refs[n:2 * n]
        ev = refs[2 * n] if ne else None
        sems = refs[2 * n + ne + 1 + 2 * n + ne:-1]
        send, recv = sems[0], sems[1]
        esend, erecv = (sems[2], sems[3]) if ne else (None, None)
        copies, start_ev, _ = _scatter_copies(srcs, lands_, ev, send, recv, esend, erecv)
        for cp in start_ev + copies:
            cp.start()
        refs[-1][...] = jnp.zeros_like(refs[-1])

    sem_shapes = [pltpu.SemaphoreType.DMA((3 * n,))] * 2 + [pltpu.SemaphoreType.DMA((7,))] * (2 * ne)
    bufs = list(arrays) + lands + ([everyone] if ne else [])
    outs = pl.pallas_call(
        body, name=name, in_specs=[HBM] * len(bufs) + [ANY],
        out_specs=[HBM] * len(bufs) + [SEM] * len(sem_shapes) + [pl.BlockSpec(memory_space=pltpu.VMEM)],
        out_shape=[pltpu.HBM(b.shape, b.dtype) for b in bufs] + sem_shapes + [jax.ShapeDtypeStruct((8, LANES), f32)],
        input_output_aliases={i: i for i in range(len(bufs))},
        compiler_params=pltpu.CompilerParams(has_side_effects=DATAFLOW))(
            *[pltpu.with_memory_space_constraint(b, pltpu.HBM) for b in bufs], after)
    return (n, ne, outs[:-1]), outs[-1]


def chip_scatter_wait(state, after, *, name):
    n, ne, held = state
    nb = 2 * n + ne
    bufs, sems = held[:nb], held[nb:]

    def body(*refs):
        srcs, lands_ = refs[:n], refs[n:2 * n]
        ev = refs[2 * n] if ne else None
        sems_ = refs[nb:nb + len(sems)]
        esend, erecv = (sems_[2], sems_[3]) if ne else (None, None)
        copies, _, wait_ev = _scatter_copies(srcs, lands_, ev, sems_[0], sems_[1], esend, erecv)
        for cp in wait_ev + copies:
            cp.wait_send()
            cp.wait_recv()

    outs = pl.pallas_call(
        body, name=name, in_specs=[HBM] * nb + [SEM] * len(sems) + [ANY], out_specs=[HBM] * nb,
        out_shape=[pltpu.HBM(b.shape, b.dtype) for b in bufs],
        input_output_aliases={i: i for i in range(nb)},
        compiler_params=pltpu.CompilerParams(has_side_effects=DATAFLOW))(*bufs, *sems, after)
    return outs[n:2 * n], (outs[2 * n] if ne else None)


def sibling_merge(bufs, *, name):
    n = len(bufs)

    def body(*refs):
        bufs_ = refs[:n]
        send, recv = refs[2 * n:]
        x, y, c = _place()

        def copy(u, h):
            return pltpu.make_async_remote_copy(
                src_ref=bufs_[u].at[h], dst_ref=bufs_[u].at[h], send_sem=send.at[u], recv_sem=recv.at[u],
                device_id=(x, y, 1 - c), device_id_type=MESH)

        for u in range(n):
            copy(u, c).start()
        for u in range(n):
            copy(u, 1 - c).wait_recv()
        for u in range(n):
            copy(u, c).wait_send()

    return pl.pallas_call(
        body, name=name, in_specs=[ANY] * n, out_specs=[ANY] * n,
        out_shape=[jax.ShapeDtypeStruct(b.shape, b.dtype) for b in bufs],
        input_output_aliases={i: i for i in range(n)},
        scratch_shapes=[pltpu.SemaphoreType.DMA((n,)), pltpu.SemaphoreType.DMA((n,))])(*bufs)


def sum_leading(a, *, name):
    n, r, c = a.shape

    def body(a_ref, o_ref):
        acc = a_ref[0]
        for i in range(1, n):
            acc = acc + a_ref[i]
        o_ref[...] = acc

    rb = r // 2 if r % 16 == 0 else r
    return pl.pallas_call(
        body, name=name, grid=(r // rb,), in_specs=[pl.BlockSpec((n, rb, c), lambda i: (0, i, 0))],
        out_specs=pl.BlockSpec((rb, c), lambda i: (i, 0)), out_shape=jax.ShapeDtypeStruct((r, c), f32),
        compiler_params=_params("parallel"))(a)


def _half_rows(shape):
    return shape[1] // 2 // 2


def rs_cast_other_half(grads, c_arr, *, name):
    n = len(grads)

    def body(c_ref, *refs):
        for i_ref, o_ref in zip(refs[:n], refs[n:]):
            o_ref[...] = i_ref[...].astype(bf16)

    in_specs = [pl.BlockSpec((1, _half_rows(g.shape), g.shape[2]), lambda s, r, c_ref: (s, (1 - c_ref[0]) * 2 + r, 0))
                for g in grads]
    out_specs = [pl.BlockSpec((1, _half_rows(g.shape), g.shape[2]), lambda s, r, c_ref: (s, r, 0)) for g in grads]
    return pl.pallas_call(
        body, name=name,
        grid_spec=pltpu.PrefetchScalarGridSpec(num_scalar_prefetch=1, grid=(N_SHARDS, 2),
                                               in_specs=in_specs, out_specs=out_specs),
        out_shape=[jax.ShapeDtypeStruct((N_SHARDS, g.shape[1] // 2, g.shape[2]), bf16) for g in grads],
        compiler_params=_params("parallel", "parallel"))(c_arr, *grads)


def rs_add_sibling(grads, recvd, ck_arr, *, name):
    n = len(grads)

    def body(ck_ref, *refs):
        s = pl.program_id(1)
        for u in range(n):
            g_ref, r_ref = refs[u], refs[n + u]
            qb_ref, own_ref = refs[2 * n + u], refs[3 * n + u]
            q = g_ref[0] + r_ref[0].astype(f32)
            qb_ref[0] = q.astype(bf16)

            @pl.when(s == ck_ref[1])
            def _(own_ref=own_ref, q=q):
                own_ref[...] = q

    in_specs = [pl.BlockSpec((1, _half_rows(g.shape), g.shape[2]), lambda r, s, ck: (s, ck[0] * 2 + r, 0)) for g in grads]
    in_specs += [pl.BlockSpec((1, _half_rows(g.shape), g.shape[2]), lambda r, s, ck: (s, r, 0)) for g in grads]
    out_specs = [pl.BlockSpec((1, _half_rows(g.shape), g.shape[2]), lambda r, s, ck: (s, r, 0)) for g in grads]
    out_specs += [pl.BlockSpec((_half_rows(g.shape), g.shape[2]), lambda r, s, ck: (r, 0)) for g in grads]
    outs = pl.pallas_call(
        body, name=name,
        grid_spec=pltpu.PrefetchScalarGridSpec(num_scalar_prefetch=1, grid=(2, N_SHARDS),
                                               in_specs=in_specs, out_specs=out_specs),
        out_shape=[jax.ShapeDtypeStruct((N_SHARDS, g.shape[1] // 2, g.shape[2]), bf16) for g in grads]
        + [jax.ShapeDtypeStruct((g.shape[1] // 2, g.shape[2]), f32) for g in grads],
        compiler_params=_params("parallel", "arbitrary"))(ck_arr, *grads, *recvd)
    return outs[:n], outs[n:]


def rs_sum_chips(owns, recvd, ck_arr, *, name):
    n = len(owns)

    def body(ck_ref, *refs):
        for u in range(n):
            own_ref, r_ref, o_ref = refs[u], refs[n + u], refs[2 * n + u]
            o_ref[0] = ((own_ref[...] + r_ref[0].astype(f32)) + r_ref[1].astype(f32)) + r_ref[2].astype(f32)

    in_specs = [pl.BlockSpec((o.shape[0] // 2, o.shape[1]), lambda r, ck: (r, 0)) for o in owns]
    in_specs += [pl.BlockSpec((3, o.shape[0] // 2, o.shape[1]), lambda r, ck: (0, r, 0)) for o in owns]
    out_specs = [pl.BlockSpec((1, o.shape[0] // 2, o.shape[1]), lambda r, ck: (ck[0], r, 0)) for o in owns]
    return pl.pallas_call(
        body, name=name,
        grid_spec=pltpu.PrefetchScalarGridSpec(num_scalar_prefetch=1, grid=(2,), in_specs=in_specs, out_specs=out_specs),
        out_shape=[jax.ShapeDtypeStruct((2,) + o.shape, f32) for o in owns],
        compiler_params=_params("parallel"))(ck_arr, *owns, *recvd)


SMALL = ("a_norm", "a_v_norm", "a_w_s", "a_b_s", "f_norm", "f_conv_w", "f_conv_b", "kv_norm", "k_norm",
         "b_norm", "b_q_norm", "b_sinks")
BIG = ("a_w_in", "a_w_out", "f_w_in", "f_w_out", "w_kv", "b_w_q", "b_w_o")
PACK_COLS = 1024
PACK_ROWS = 8 * N_STEPS


def _pack(parts, rows=PACK_ROWS):
    flat = jnp.concatenate([p.reshape(-1).astype(f32) for p in parts])
    pad = (-flat.shape[0]) % (rows * PACK_COLS)
    return jnp.pad(flat, (0, pad)).reshape(-1, PACK_COLS)


def _unpack(packed, shapes):
    flat = packed.reshape(-1)
    out, off = [], 0
    for s in shapes:
        size = math.prod(s)
        out.append(flat[off:off + size].reshape(s))
        off += size
    return out


def _ffn_fwd(x, g, h, r, w_in4, conv_w, conv_b, f, tag):
    wg, wu = conv_w[:, :f], conv_w[:, f:]
    bg, bu = conv_b[None, :f], conv_b[None, f:]
    pg, pu, gate, up, a = ffn_in_fused(h, w_in4, wg, wu, bg, bu, name=f"ffn{tag}_in")
    return a, (x, g, h, r, pg, pu, gate, up, a, wg, wu)


def _ffn_bwd(dy, saved, w_in4, w_out, c_arr, tag, exchange=False):
    x, g, h, r, pg, pu, gate, up, a, wg, wu = saved
    f = w_out.shape[0]
    d_w_out = mm_tn(a, [dy], c_arr, name=f"ffn{tag}_dwout", n_s=w_out.shape[1], shard_rows=f // N_SHARDS, tki=f // 2)
    dpg, dpu, sg, su = ffn_gate_bwd(dy, w_out, pg, pu, gate, up, wg, wu, name=f"ffn{tag}_dgate")
    d_w_in = mm_tn(h, [dpg, dpu], c_arr, name=f"ffn{tag}_dwin", n_s=w_in4.shape[2], shard_rows=h.shape[1])
    state = None
    if exchange:
        state, token = sibling_start([d_w_in[1], d_w_out[1]], d_w_in[0], name=f"rs_sibling_start_ffn{tag}")
        g = g + token[0, 0]
    dx, dg = mm_nt_rms_bwd([dpg, dpu], w_in4, x, r, g, dy, name=f"ffn{tag}_dh")
    d_conv_w = jnp.concatenate([sg[0:3], su[0:3]], axis=1)
    d_conv_b = jnp.concatenate([sg[3], su[3]], axis=0)
    return dx, dg, d_w_in, d_conv_w, d_conv_b, d_w_out, state


def _rs_front(pairs, sibling_state, after, c_arr, tag):
    units = [full.reshape(N_SHARDS, -1, full.shape[-1]) for full, _ in pairs]
    from_sib = sibling_wait(sibling_state, after, name=f"rs_sibling_wait{tag}")
    return rs_add_sibling(units, from_sib, c_arr, name=f"rs_add{tag}")


def _rs_back(own, from_chips, c_arr, tag):
    halves = rs_sum_chips(list(own), list(from_chips), c_arr, name=f"rs_sum{tag}")
    return [m.reshape(-1, m.shape[2]) for m in sibling_merge(list(halves), name=f"rs_merge{tag}")]


def kernel(x, a_norm, a_w_in, a_v_norm, a_w_s, a_b_s, a_w_out, f_norm, f_w_in, f_conv_w, f_conv_b, f_w_out, kv_norm, w_kv, k_norm, b_norm, b_w_q, b_q_norm, b_sinks, b_w_o, loss_target, m_a_norm, m_a_w_in, m_a_v_norm, m_a_w_s, m_a_b_s, m_a_w_out, m_f_norm, m_f_w_in, m_f_conv_w, m_f_conv_b, m_f_w_out, m_kv_norm, m_w_kv, m_k_norm, m_b_norm, m_b_w_q, m_b_q_norm, m_b_sinks, m_b_w_o, v_a_norm, v_a_w_in, v_a_v_norm, v_a_w_s, v_a_b_s, v_a_w_out, v_f_norm, v_f_w_in, v_f_conv_w, v_f_conv_b, v_f_w_out, v_kv_norm, v_w_kv, v_k_norm, v_b_norm, v_b_w_q, v_b_q_norm, v_b_sinks, v_b_w_o):
    args = dict(locals())
    weights = {n: args[n] for n in SMALL + BIG}
    moms = {n: args["m_" + n] for n in SMALL + BIG}
    vars_ = {n: args["v_" + n] for n in SMALL + BIG}
    t, d = x.shape[1], x.shape[2]
    xi, yi, ci = _place()
    chip = 2 * xi + yi

    big_local = [a_w_in[0], a_w_out[0], f_w_in, f_w_out, w_kv, b_w_q[0], b_w_o[0]]
    c_arr = jnp.stack([ci, chip]).astype(jnp.int32)
    k_arr = jnp.stack([chip]).astype(jnp.int32)
    b_ain, b_aout, b_fin0, b_fin1, b_fout0, b_fout1, b_kv, b_q, b_o = cast_into_slot(big_local, k_arr, name="cast_weights")
    small_cols = _pack([a_norm, a_v_norm, f_conv_w], rows=8)
    b_small = lax.dynamic_update_slice(jnp.zeros((N_SHARDS,) + small_cols.shape, f32), small_cols[None], (chip, 0, 0))
    g_small, w_a_in, g_a_w_out = gather_shards([b_small, b_ain, b_aout], name="gather_first", split=[False, True, True])
    later, send_sems, recv_sems, token = gather_start([b_fin0, b_fout0, b_kv, b_q, b_o, b_fin1, b_fout1],
                                                      [[0], [1], [2, 3, 4], [5, 6]], g_small, name="gather_start")
    ns_cols = a_norm.shape[1]
    nf_cols = f_conv_w.shape[2]
    parts = [_unpack(g_small[k], [a_norm.shape, a_v_norm.shape, f_conv_w.shape]) for k in range(N_SHARDS)]
    a_norm_f = jnp.concatenate([p[0] for p in parts], axis=1) + token[0, 0]
    a_v_norm_f = jnp.concatenate([p[1] for p in parts], axis=1)
    conv_w_f = jnp.concatenate([p[2] for p in parts], axis=2)

    x0 = x[0]
    tril = jnp.tril(jnp.ones((CHUNK, CHUNK), dtype=bool))
    wc = jnp.where(tril[None], a_w_s[0], 0.0).astype(bf16)
    bt = a_b_s[0].T
    kg2 = jnp.tile(k_norm, 2)[None]
    qg2 = jnp.tile(b_q_norm[0], 2)[None]

    (h_a,), r_a = rms_fwd(x0, [a_norm_f], name="a_norm")
    zuv = mm_nn(h_a, w_a_in, name="a_in")
    y_a = sgu_gate_fwd(zuv, a_v_norm_f, wc, bt, name="a_gate")
    w_a_out = g_a_w_out.reshape(1, -1, d)
    f = f_w_out.shape[1] * N_SHARDS
    fwd0, tok0 = forward_start(gather_wait(later[0:1], send_sems[0], recv_sems[0], y_a, name="gather_wait_0"), y_a,
                               name="gather_forward_start_0")
    x1, (h_f0,), r_f0 = mm_residual(y_a, w_a_out[0], x0, name="a_out", gains=[f_norm[0:1] + tok0[0, 0]])
    (g_fin0,) = forward_wait(fwd0, x1, name="gather_forward_wait_0")
    w_f_in = [g_fin0, None]
    a0, ffn0 = _ffn_fwd(x1, f_norm[0:1], h_f0, r_f0, w_f_in[0], conv_w_f[0], f_conv_b[0], f, "0")
    (g_fout0,) = forward_halves(gather_wait(later[1:2], send_sems[1], recv_sems[1], a0, name="gather_wait_1"),
                                name="gather_forward_1")
    w_f_out = [g_fout0.reshape(-1, d), None]
    fwd1, tok1 = forward_start(gather_wait(later[2:5], send_sems[2], recv_sems[2], g_fout0, name="gather_wait_1b"), a0,
                               name="gather_forward_start_1b")
    x2, (h_k, h_q), r_b = mm_residual(a0, w_f_out[0], x1, name="ffn0_out", gains=[kv_norm[None] + tok1[0, 0], b_norm])
    g_w_kv, g_b_w_q, g_b_w_o = forward_wait(fwd1, x2, name="gather_forward_wait_1b")
    w_kv_f = g_w_kv.reshape(1, d, -1)
    w_q_f = g_b_w_q.reshape(1, d, -1)
    w_o_f = g_b_w_o.reshape(1, -1, d)
    kv, k2, v2 = kv_proj_post(h_k, w_kv_f[0], kg2, name="kv_proj")
    qp, qn = q_proj_norm(h_q, w_q_f[0], qg2, name="q_proj", scale=HEAD_DIM ** -0.5)
    fwd2, tok2 = forward_start(gather_wait(later[5:7], send_sems[3], recv_sems[3], qn, name="gather_wait_2"), qn,
                               name="gather_forward_start_2")
    o = attn_fwd(qn, k2, v2, b_sinks[0] + tok2[0, 0], name="attn")
    x3, (h_f1,), r_f1 = mm_residual(o, w_o_f[0], x2, name="o_proj", gains=[f_norm[1:2]])
    g_fin1, g_fout1 = forward_wait(fwd2, x3, name="gather_forward_wait_2")
    w_f_in[1] = g_fin1
    w_f_out[1] = g_fout1.reshape(-1, d)
    a1, ffn1 = _ffn_fwd(x3, f_norm[1:2], h_f1, r_f1, w_f_in[1], conv_w_f[1], f_conv_b[1], f, "1")
    dx4, sq = mm_residual(a1, w_f_out[1], x3, name="ffn1_out", target=loss_target[0])
    loss_part = (0.5 * jnp.sum(sq) / d).reshape(1)

    proj_rows = d // N_SHARDS
    dx3, d_fn1, d_fwin1, d_cw1, d_cb1, d_fwout1, _ = _ffn_bwd(dx4, ffn1, w_f_in[1], w_f_out[1], c_arr, "1")
    do = mm_nt([dx3], w_o_f, name="o_proj_dx")
    d_w_o = mm_tn(o, [dx3], c_arr, name="o_proj_dw", n_s=d, shard_rows=o.shape[1] // N_SHARDS)
    dqn, dk2, dv2, dsink = attn_bwd(qn, k2, v2, do, b_sinks[0], name="attn_bwd")
    dqp, dqg = q_norm_bwd(dqn, qp, qg2, name="q_norm_bwd", scale=HEAD_DIM ** -0.5)
    dkv, dkg = kv_post_bwd(dk2, dv2, kv, kg2, name="kv_post_bwd")
    d_w_q = mm_tn(h_q, [dqp], c_arr, name="q_proj_dw", n_s=w_q_f.shape[2], shard_rows=proj_rows)
    d_w_kv = mm_tn(h_k, [dkv], c_arr, name="kv_proj_dw", n_s=w_kv_f.shape[2], shard_rows=proj_rows)
    group1 = [d_fwin1, d_fwout1, d_w_kv, d_w_q, d_w_o]
    sib1, token_s1 = sibling_start([half for _, half in group1], d_w_kv[0], name="rs_sibling_start1")
    dh_k = mm_nt([dkv], w_kv_f, name="kv_proj_dx")
    dx2, d_bn, d_kvn = mm_nt_rms_bwd([dqp], w_q_f, x2, r_b, b_norm + token_s1[0, 0], dx3, name="q_proj_dx",
                                     extra=(dh_k, kv_norm[None]))
    chip_bf1, own1 = _rs_front(group1, sib1, dx2, c_arr, "1")
    scatter1, token1 = chip_scatter_start(list(chip_bf1), None, dx2, name="rs_chips_start1")
    ffn0 = ffn0[:9] + (ffn0[9] + token1[0, 0],) + ffn0[10:]
    dx1, d_fn0, d_fwin0, d_cw0, d_cb0, d_fwout0, sib2 = _ffn_bwd(dx2, ffn0, w_f_in[0], w_f_out[0], c_arr, "0", exchange=True)
    chip_bf2, own2 = _rs_front([d_fwin0, d_fwout0], sib2, dx1, c_arr, "2")
    scatter2, token2 = chip_scatter_start(list(chip_bf2), None, dx1, name="rs_chips_start2")
    a_v_norm_f = a_v_norm_f + token2[0, 0]
    dy_a = mm_nt([dx1], w_a_out, name="a_out_dx")
    d_w_aout = mm_tn(y_a, [dx1], c_arr, name="a_out_dw", n_s=d, shard_rows=y_a.shape[1] // N_SHARDS)
    dzu, dzv, d_avn, d_ws, d_bt = sgu_gate_bwd(zuv, dy_a, a_v_norm_f, wc, bt, name="a_gate_bwd")
    d_w_ain = mm_tn(h_a, [dzu, dzv], c_arr, name="a_in_dw", n_s=w_a_in.shape[2], shard_rows=d)
    sib3, token_s3 = sibling_start([d_w_ain[1], d_w_aout[1]], d_w_ain[0], name="rs_sibling_start3")
    dx0, d_an = mm_nt_rms_bwd([dzu, dzv], w_a_in, x0, r_a, a_norm_f + token_s3[0, 0], dx1, name="a_in_dx")
    grad_x = dx0[None]

    chip_bf3, own3 = _rs_front([d_w_ain, d_w_aout], sib3, dx0, c_arr, "3")
    d_fn = jnp.concatenate([d_fn0, d_fn1], axis=0)
    d_cw = jnp.stack([d_cw0, d_cw1])
    d_cb = jnp.stack([d_cb0, d_cb1])
    d_kg = (dkg[0, :HEAD_DIM] + dkg[0, HEAD_DIM:])
    d_qg = (dqg[0, :HEAD_DIM] + dqg[0, HEAD_DIM:])[None]
    small_full = [d_an, d_avn, d_ws[None], d_bt.T[None], d_fn, d_cw, d_cb, d_kvn[0], d_kg, d_bn, d_qg,
                  dsink[:, :N_Q_HEADS], loss_part]
    packed = _pack(small_full)
    me = 4 * xi + 2 * yi + ci
    everyone = lax.dynamic_update_slice(lax.empty((N_DEV,) + packed.shape, f32), packed[None], (me, 0, 0))
    scatter3, token3 = chip_scatter_start(list(chip_bf3), everyone, own3[0], name="rs_chips_start3")
    from_chips1, _ = chip_scatter_wait(scatter1, token3, name="rs_chips_wait1")
    from_chips2, _ = chip_scatter_wait(scatter2, from_chips1[0], name="rs_chips_wait2")
    fin1, fout1, gkv, gq, go, fin0, fout0 = _rs_back(list(own1) + list(own2), list(from_chips1) + list(from_chips2),
                                                     c_arr, "12")
    late = ("f_w_in", "f_w_out", "w_kv", "b_w_q", "b_w_o")
    res_late = adamw([weights[n] for n in late], [[fin0, fin1], [fout0, fout1], [gkv], [gq], [go]],
                     [moms[n] for n in late], [vars_[n] for n in late], name="adamw_late")
    from_chips3, from_all = chip_scatter_wait(scatter3, res_late[1][2], name="rs_chips_wait3")
    ain, aout = _rs_back(own3, from_chips3, c_arr, "3")
    first = ("a_w_in", "a_w_out")
    res_first = adamw([weights[n] for n in first], [[ain], [aout]], [moms[n] for n in first],
                      [vars_[n] for n in first], name="adamw_first")
    big = {n: tuple(r[i] for r in res_late) for i, n in enumerate(late)}
    big.update({n: tuple(r[i] for r in res_first) for i, n in enumerate(first)})

    full_shapes = [g.shape for g in small_full]
    small_g = _unpack(sum_leading(from_all, name="small_sum"), full_shapes)
    loss = small_g.pop()[0]
    small_g[0] = lax.dynamic_slice_in_dim(small_g[0], chip * ns_cols, ns_cols, axis=1)
    small_g[1] = lax.dynamic_slice_in_dim(small_g[1], chip * ns_cols, ns_cols, axis=1)
    small_g[5] = lax.dynamic_slice_in_dim(small_g[5], chip * nf_cols, nf_cols, axis=2)
    small_shapes = [weights[n].shape for n in SMALL]
    small_g = [g.reshape(s) for g, s in zip(small_g, small_shapes)]
    flat2 = [(math.prod(s[:-1]), s[-1]) for s in small_shapes]
    small_d, small_m, small_v = adamw_small(
        *[[a.reshape(s2) for a, s2 in zip(group, flat2)]
          for group in ([weights[n] for n in SMALL], small_g, [moms[n] for n in SMALL], [vars_[n] for n in SMALL])],
        name="adamw_small")
    small_d, small_m, small_v = ([a.reshape(s) for a, s in zip(group, small_shapes)]
                                 for group in (small_d, small_m, small_v))

    out = {}
    for i, n in enumerate(SMALL):
        out[n] = (small_g[i], small_d[i], small_m[i], small_v[i])
    out.update(big)
    order = ["a_norm", "a_w_in", "a_v_norm", "a_w_s", "a_b_s", "a_w_out", "f_norm", "f_w_in", "f_conv_w", "f_conv_b",
             "f_w_out", "kv_norm", "w_kv", "k_norm", "b_norm", "b_w_q", "b_q_norm", "b_sinks", "b_w_o"]
    return (loss, grad_x, *[out[n][0] for n in order], *[out[n][1] for n in order],
            *[out[n][2] for n in order], *[out[n][3] for n in order])
```

```python
import functools
import math

import jax
import jax.numpy as jnp
from jax import lax
from jax.experimental import pallas as pl
from jax.experimental.pallas import tpu as pltpu

f32 = jnp.float32
bf16 = jnp.bfloat16
MESH = pl.DeviceIdType.MESH
ANY = pl.BlockSpec(memory_space=pl.ANY)

EPS = 1e-6
LANES = 128
CHUNK = 128
HEAD_DIM = 64
N_Q_HEADS = 16
N_KV_HEADS = 4
Q_PER_KV = N_Q_HEADS // N_KV_HEADS
N_SHARDS = 4
N_DEV = 8

ADAM_LR = 0.001
ADAM_B1 = 0.9
ADAM_B2 = 0.999
ADAM_EPS = 1e-08
ADAM_WD = 0.01
ADAM_STEP = 10
ADAM_C1 = 1.0 - ADAM_B1 ** ADAM_STEP
ADAM_C2 = 1.0 - ADAM_B2 ** ADAM_STEP

_INV_SQRT2 = 1.0 / math.sqrt(2.0)
_INV_SQRT2PI = 1.0 / math.sqrt(2.0 * math.pi)


def _params(*sem):
    return pltpu.CompilerParams(dimension_semantics=sem)


def _gelu(z):
    return 0.5 * z * (1.0 + lax.erf(z * _INV_SQRT2))


def _gelu_and_grad(z):
    cdf = 0.5 * (1.0 + lax.erf(z * _INV_SQRT2))
    return z * cdf, cdf + z * jnp.exp(-0.5 * z * z) * _INV_SQRT2PI


def _dot(a, b):
    return jnp.dot(a, b, preferred_element_type=f32)


def _dot_nt(a, b):
    return lax.dot_general(a, b, (((1,), (1,)), ((), ())), preferred_element_type=f32)


def _dot_tn(a, b):
    return lax.dot_general(a, b, (((0,), (0,)), ((), ())), preferred_element_type=f32)


def _dot_split(a, b):
    hi = a.astype(bf16)
    lo = (a - hi.astype(f32)).astype(bf16)
    return _dot(hi, b) + _dot(lo, b)


VMEM_TILE_BUDGET = 40 * 1024 * 1024
MAX_ROW_TILE = 2048


def _row_tile(m, fixed_bytes, row_bytes):
    tm = min(m, MAX_ROW_TILE)
    while tm > 256 and 2 * (fixed_bytes + tm * row_bytes) > VMEM_TILE_BUDGET:
        tm //= 2
    return tm


def _isz(a):
    return jnp.dtype(a.dtype).itemsize


def mm_nn(a, w3, *, name, s0=0, ns=None, add=None, out_dtype=f32):
    m, k = a.shape
    s_all, _, n_s = w3.shape
    ns = s_all if ns is None else ns
    tm = _row_tile(m, k * n_s * 2, k * _isz(a) + n_s * jnp.dtype(out_dtype).itemsize + (0 if add is None else n_s * 4))

    def body(*refs):
        if add is None:
            a_ref, w_ref, o_ref = refs
            acc = _dot(a_ref[...].astype(bf16), w_ref[0])
        else:
            a_ref, w_ref, add_ref, o_ref = refs
            acc = _dot(a_ref[...].astype(bf16), w_ref[0]) + add_ref[...]
        o_ref[...] = acc.astype(out_dtype)

    in_specs = [pl.BlockSpec((tm, k), lambda j, i: (i, 0)),
                pl.BlockSpec((1, k, n_s), lambda j, i: (s0 + j, 0, 0))]
    args = [a, w3]
    if add is not None:
        in_specs.append(pl.BlockSpec((tm, n_s), lambda j, i: (i, j)))
        args.append(add)
    return pl.pallas_call(
        body, name=name, grid=(ns, m // tm), in_specs=in_specs,
        out_specs=pl.BlockSpec((tm, n_s), lambda j, i: (i, j)),
        out_shape=jax.ShapeDtypeStruct((m, ns * n_s), out_dtype),
        compiler_params=_params("parallel", "parallel"))(*args)


def mm_nt(a_list, w3, *, name, tko=None, add=None, out_dtype=f32):
    s_all, k_out, n_s = w3.shape
    m = a_list[0].shape[0]
    na = len(a_list)
    spa = s_all // na
    tko = k_out if tko is None else tko
    tm = _row_tile(m, tko * n_s * 2, na * n_s * _isz(a_list[0]) + tko * 4 * (1 if add is None else 2))

    def body(*refs):
        a_refs = refs[:na]
        w_ref = refs[na]
        o_ref = refs[-1]
        s = pl.program_id(2)

        @pl.when(s == 0)
        def _():
            if add is None:
                o_ref[...] = jnp.zeros_like(o_ref)
            else:
                o_ref[...] = refs[na + 1][...]

        for idx in range(na):
            @pl.when(s // spa == idx)
            def _(idx=idx):
                o_ref[...] += _dot_nt(a_refs[idx][...].astype(bf16), w_ref[0])

    def a_map(idx):
        return lambda ko, i, s: (i, jnp.clip(s - idx * spa, 0, spa - 1))

    in_specs = [pl.BlockSpec((tm, n_s), a_map(idx)) for idx in range(na)]
    in_specs.append(pl.BlockSpec((1, tko, n_s), lambda ko, i, s: (s, ko, 0)))
    args = list(a_list) + [w3]
    if add is not None:
        in_specs.append(pl.BlockSpec((tm, tko), lambda ko, i, s: (i, ko)))
        args.append(add)
    return pl.pallas_call(
        body, name=name, grid=(k_out // tko, m // tm, s_all), in_specs=in_specs,
        out_specs=pl.BlockSpec((tm, tko), lambda ko, i, s: (i, ko)),
        out_shape=jax.ShapeDtypeStruct((m, k_out), out_dtype),
        compiler_params=_params("parallel", "parallel", "arbitrary"))(*args)


def mm_tn(a, b_list, c_arr, *, name, n_s, shard_rows, tki=None):
    m, k_in = a.shape
    na = len(b_list)
    s_all = sum(b.shape[1] for b in b_list) // n_s
    spa = s_all // na
    tki = k_in if tki is None else tki
    tm = _row_tile(m, tki * n_s * 4, tki * _isz(a) + na * n_s * _isz(b_list[0]))

    nsteps = m // tm
    per_blk = tki // shard_rows
    half = shard_rows // 2

    def body(c_ref, *refs):
        a_ref = refs[0]
        b_refs = refs[1:1 + na]
        o_ref, ob_ref = refs[-2], refs[-1]
        s = pl.program_id(0)
        r = pl.program_id(2)

        @pl.when(r == 0)
        def _():
            o_ref[...] = jnp.zeros_like(o_ref)

        for idx in range(na):
            @pl.when(s // spa == idx)
            def _(idx=idx):
                o_ref[0] += _dot_tn(a_ref[...].astype(bf16), b_refs[idx][...].astype(bf16))

        @pl.when(r == nsteps - 1)
        def _():
            for q in range(per_blk):
                start = pl.multiple_of(q * shard_rows + (1 - c_ref[0]) * half, 16)
                ob_ref[q] = o_ref[0, pl.ds(start, half), :].astype(bf16)

    def b_map(idx):
        def index(s, ki, r, c_ref):
            active = (s // spa) == idx
            return (jnp.where(active, r, 0), jnp.clip(s - idx * spa, 0, spa - 1))
        return index

    in_specs = [pl.BlockSpec((tm, tki), lambda s, ki, r, c_ref: (r, ki))]
    in_specs += [pl.BlockSpec((tm, n_s), b_map(idx)) for idx in range(na)]
    n_blk = k_in // tki
    return pl.pallas_call(
        body, name=name,
        grid_spec=pltpu.PrefetchScalarGridSpec(
            num_scalar_prefetch=1, grid=(s_all, n_blk, nsteps), in_specs=in_specs,
            out_specs=[pl.BlockSpec((1, tki, n_s), lambda s, ki, r, c_ref: (s, ki, 0)),
                       pl.BlockSpec((per_blk, half, n_s), lambda s, ki, r, c_ref: (s * n_blk + ki, 0, 0))]),
        out_shape=[jax.ShapeDtypeStruct((s_all, k_in, n_s), f32),
                   jax.ShapeDtypeStruct((s_all * k_in // shard_rows, half, n_s), bf16)],
        compiler_params=_params("parallel", "parallel", "arbitrary"))(c_arr, a, *b_list)


def mm_nt_rms_bwd(a_list, w3, x, r, g, dx_in, *, name, extra=None):
    s_all, d, n_s = w3.shape
    m = a_list[0].shape[0]
    na = len(a_list)
    spa = s_all // na
    ne = 0 if extra is None else 1
    tm = _row_tile(m, d * n_s * 2, na * n_s * _isz(a_list[0]) + d * 4 * (4 + ne))

    def body(*refs):
        a_refs, w_ref = refs[:na], refs[na]
        x_ref, r_ref, g_ref, dxin_ref = refs[na + 1:na + 5]
        dh2_ref, g2_ref = (refs[na + 5], refs[na + 6]) if ne else (None, None)
        outs = refs[na + 5 + 2 * ne:]
        dx_ref, dg_ref = outs[0], outs[1]
        dg2_ref = outs[2] if ne else None
        acc_ref = outs[-1]
        i, s = pl.program_id(0), pl.program_id(1)

        @pl.when(s == 0)
        def _():
            acc_ref[...] = jnp.zeros_like(acc_ref)

        for idx in range(na):
            @pl.when(s // spa == idx)
            def _(idx=idx):
                acc_ref[...] += _dot_nt(a_refs[idx][...].astype(bf16), w_ref[0])

        @pl.when(s == s_all - 1)
        def _():
            rv = r_ref[...]
            xh = x_ref[...] * rv
            total = dxin_ref[...]
            pairs = [(acc_ref[...], g_ref, dg_ref)] + ([(dh2_ref[...], g2_ref, dg2_ref)] if ne else [])
            for dh, gain_ref, dgain_ref in pairs:
                part = jnp.sum(dh * xh, axis=0, keepdims=True)

                @pl.when(i == 0)
                def _(dgain_ref=dgain_ref, part=part):
                    dgain_ref[...] = part

                @pl.when(i > 0)
                def _(dgain_ref=dgain_ref, part=part):
                    dgain_ref[...] += part

                tg = dh * gain_ref[...]
                total = total + rv * (tg - xh * jnp.mean(tg * xh, axis=1, keepdims=True))
            dx_ref[...] = total

    def a_map(idx):
        return lambda i, s: (i, jnp.clip(s - idx * spa, 0, spa - 1))

    row = pl.BlockSpec((tm, d), lambda i, s: (i, 0))
    vec = pl.BlockSpec((1, d), lambda i, s: (0, 0))
    in_specs = [pl.BlockSpec((tm, n_s), a_map(idx)) for idx in range(na)]
    in_specs += [pl.BlockSpec((1, d, n_s), lambda i, s: (s, 0, 0)), row, pl.BlockSpec((tm, 1), lambda i, s: (i, 0)), vec, row]
    args = list(a_list) + [w3, x, r, g, dx_in]
    if ne:
        in_specs += [row, vec]
        args += list(extra)
    outs = pl.pallas_call(
        body, name=name, grid=(m // tm, s_all), in_specs=in_specs, out_specs=[row] + [vec] * (1 + ne),
        out_shape=[jax.ShapeDtypeStruct((m, d), f32)] + [jax.ShapeDtypeStruct((1, d), f32)] * (1 + ne),
        scratch_shapes=[pltpu.VMEM((tm, d), f32)],
        compiler_params=_params("arbitrary", "arbitrary"))(*args)
    return outs


def mm_residual(a, w, x, *, name, gains=(), target=None):
    m, k = a.shape
    d = w.shape[1]
    ng = len(gains)
    tm = _row_tile(m, k * d * 2, k * _isz(a) + d * 4 * 3 + ng * d * 2)

    def body(*refs):
        a_ref, w_ref, x_ref = refs[:3]
        y = _dot(a_ref[...].astype(bf16), w_ref[...]) + x_ref[...]
        if target is None:
            g_refs = refs[3:3 + ng]
            y_ref = refs[3 + ng]
            h_refs = refs[4 + ng:4 + 2 * ng]
            r_ref = refs[-1]
            y_ref[...] = y
            r = lax.rsqrt(jnp.mean(y * y, axis=1, keepdims=True) + EPS)
            yh = y * r
            for g_ref, h_ref in zip(g_refs, h_refs):
                h_ref[...] = (yh * g_ref[...]).astype(bf16)
            r_ref[...] = r
        else:
            t_ref, dy_ref, s_ref = refs[3:]
            i = pl.program_id(0)
            e = y - t_ref[...]
            dy_ref[...] = e * (1.0 / d)
            part = jnp.sum(e * e, axis=0, keepdims=True)

            @pl.when(i == 0)
            def _():
                s_ref[...] = part

            @pl.when(i > 0)
            def _():
                s_ref[...] += part

    row = pl.BlockSpec((tm, d), lambda i: (i, 0))
    vec = pl.BlockSpec((1, d), lambda i: (0, 0))
    in_specs = [pl.BlockSpec((tm, k), lambda i: (i, 0)), pl.BlockSpec((k, d), lambda i: (0, 0)), row]
    if target is None:
        outs = pl.pallas_call(
            body, name=name, grid=(m // tm,), in_specs=in_specs + [vec] * ng,
            out_specs=[row] * (1 + ng) + [pl.BlockSpec((tm, 1), lambda i: (i, 0))],
            out_shape=[jax.ShapeDtypeStruct((m, d), f32)] + [jax.ShapeDtypeStruct((m, d), bf16)] * ng
            + [jax.ShapeDtypeStruct((m, 1), f32)],
            compiler_params=_params("parallel"))(a, w, x, *gains)
        return outs[0], outs[1:1 + ng], outs[-1]
    return pl.pallas_call(
        body, name=name, grid=(m // tm,), in_specs=in_specs + [row], out_specs=[row, vec],
        out_shape=[jax.ShapeDtypeStruct((m, d), f32), jax.ShapeDtypeStruct((1, d), f32)],
        compiler_params=_params("arbitrary"))(a, w, x, target)


def rms_fwd(x, gains, *, name, tr=512):
    t, d = x.shape
    tr = min(tr, t)
    ng = len(gains)

    def body(*refs):
        x_ref = refs[0]
        g_refs = refs[1:1 + ng]
        h_refs = refs[1 + ng:1 + 2 * ng]
        r_ref = refs[-1]
        xv = x_ref[...]
        r = lax.rsqrt(jnp.mean(xv * xv, axis=1, keepdims=True) + EPS)
        xh = xv * r
        for g_ref, h_ref in zip(g_refs, h_refs):
            h_ref[...] = (xh * g_ref[...]).astype(bf16)
        r_ref[...] = r

    row = pl.BlockSpec((tr, d), lambda i: (i, 0))
    vec = pl.BlockSpec((1, d), lambda i: (0, 0))
    outs = pl.pallas_call(
        body, name=name, grid=(t // tr,), in_specs=[row] + [vec] * ng,
        out_specs=[row] * ng + [pl.BlockSpec((tr, 1), lambda i: (i, 0))],
        out_shape=[jax.ShapeDtypeStruct((t, d), bf16)] * ng + [jax.ShapeDtypeStruct((t, 1), f32)],
        compiler_params=_params("parallel"))(x, *gains)
    return outs[:ng], outs[ng]


def sgu_gate_fwd(zuv, gv, wc, bt, *, name, tr=512):
    t, w = zuv.shape[0], zuv.shape[1] // 2
    tr = min(tr, t)
    groups = w // LANES

    def body(zu_ref, zv_ref, gv_ref, wc_ref, bt_ref, y_ref):
        vp = _gelu(zv_ref[...])
        rv = lax.rsqrt(jnp.mean(vp * vp, axis=1, keepdims=True) + EPS)
        vb = (vp * rv * gv_ref[...]).astype(bf16)
        for c in range(tr // CHUNK):
            rows = slice(c * CHUNK, (c + 1) * CHUNK)
            for g in range(groups):
                cols = slice(g * LANES, (g + 1) * LANES)
                sv = _dot(wc_ref[g], vb[rows, cols]) + bt_ref[:, g:g + 1]
                y_ref[rows, cols] = (_gelu(zu_ref[rows, cols]) * sv).astype(bf16)

    row = pl.BlockSpec((tr, w), lambda i: (i, 0))
    return pl.pallas_call(
        body, name=name, grid=(t // tr,),
        in_specs=[row, pl.BlockSpec((tr, w), lambda i: (i, 1)), pl.BlockSpec((1, w), lambda i: (0, 0)),
                  pl.BlockSpec((groups, CHUNK, CHUNK), lambda i: (0, 0, 0)),
                  pl.BlockSpec((CHUNK, groups), lambda i: (0, 0))],
        out_specs=row, out_shape=jax.ShapeDtypeStruct((t, w), bf16),
        compiler_params=_params("parallel"))(zuv, zuv, gv, wc, bt)


def sgu_gate_bwd(zuv, dy, gv, wc, bt, *, name, tr=512):
    t, w = zuv.shape[0], zuv.shape[1] // 2
    tr = min(tr, t)
    groups = w // LANES
    nsteps = t // tr

    def body(zu_ref, zv_ref, dy_ref, gv_ref, wc_ref, bt_ref,
             dzu_ref, dzv_ref, dgv_ref, dws_ref, dbt_ref, dv_ref, bacc_ref):
        i = pl.program_id(0)

        @pl.when(i == 0)
        def _():
            dgv_ref[...] = jnp.zeros_like(dgv_ref)
            dws_ref[...] = jnp.zeros_like(dws_ref)
            bacc_ref[...] = jnp.zeros_like(bacc_ref)

        vp, vp_grad = _gelu_and_grad(zv_ref[...])
        rv = lax.rsqrt(jnp.mean(vp * vp, axis=1, keepdims=True) + EPS)
        vhat = vp * rv
        vb = (vhat * gv_ref[...]).astype(bf16)
        for c in range(tr // CHUNK):
            rows = slice(c * CHUNK, (c + 1) * CHUNK)
            for g in range(groups):
                cols = slice(g * LANES, (g + 1) * LANES)
                vblk = vb[rows, cols]
                sv = _dot(wc_ref[g], vblk) + bt_ref[:, g:g + 1]
                zub = zu_ref[rows, cols]
                dyb = dy_ref[rows, cols]
                ub, ub_grad = _gelu_and_grad(zub)
                dzu_ref[rows, cols] = (dyb * sv * ub_grad).astype(bf16)
                dsv = dyb * ub
                bacc_ref[:, cols] += dsv
                dsvb = dsv.astype(bf16)
                dv_ref[rows, cols] = _dot_tn(wc_ref[g], dsvb)
                dws_ref[g] += _dot_nt(dsvb, vblk)
        dv = dv_ref[...]
        dgv_ref[...] += jnp.sum(dv * vhat, axis=0, keepdims=True)
        tg = dv * gv_ref[...]
        dvp = rv * (tg - vhat * jnp.mean(tg * vhat, axis=1, keepdims=True))
        dzv_ref[...] = (dvp * vp_grad).astype(bf16)

        @pl.when(i == nsteps - 1)
        def _():
            tt = lax.broadcasted_iota(jnp.int32, (CHUNK, CHUNK), 0)
            ss = lax.broadcasted_iota(jnp.int32, (CHUNK, CHUNK), 1)
            for g in range(groups):
                dws_ref[g] = jnp.where(ss <= tt, dws_ref[g], 0.0)
                dbt_ref[:, g:g + 1] = jnp.sum(bacc_ref[:, g * LANES:(g + 1) * LANES], axis=1, keepdims=True)

    row = pl.BlockSpec((tr, w), lambda i: (i, 0))
    full3 = pl.BlockSpec((groups, CHUNK, CHUNK), lambda i: (0, 0, 0))
    return pl.pallas_call(
        body, name=name, grid=(nsteps,),
        in_specs=[row, pl.BlockSpec((tr, w), lambda i: (i, 1)), row, pl.BlockSpec((1, w), lambda i: (0, 0)), full3,
                  pl.BlockSpec((CHUNK, groups), lambda i: (0, 0))],
        out_specs=[row, row, pl.BlockSpec((1, w), lambda i: (0, 0)), full3,
                   pl.BlockSpec((CHUNK, groups), lambda i: (0, 0))],
        out_shape=[jax.ShapeDtypeStruct((t, w), bf16), jax.ShapeDtypeStruct((t, w), bf16),
                   jax.ShapeDtypeStruct((1, w), f32), jax.ShapeDtypeStruct((groups, CHUNK, CHUNK), f32),
                   jax.ShapeDtypeStruct((CHUNK, groups), f32)],
        scratch_shapes=[pltpu.VMEM((tr, w), f32), pltpu.VMEM((CHUNK, w), f32)],
        compiler_params=_params("arbitrary"))(zuv, zuv, dy, gv, wc, bt)


HALO = 8


def _shift_down(v, halo, k, first):
    r = pltpu.roll(v, k, 0)
    hh = jnp.where(first, 0.0, pltpu.roll(halo, k, 0))
    rid = lax.broadcasted_iota(jnp.int32, (HALO, v.shape[1]), 0)
    head = jnp.where(rid < k, hh, r[0:HALO])
    if v.shape[0] == HALO:
        return head
    return jnp.concatenate([head, r[HALO:]], axis=0)


def _shift_up(v, halo, k, last):
    n = v.shape[0]
    r = pltpu.roll(v, n - k, 0)
    hh = jnp.where(last, 0.0, pltpu.roll(halo, HALO - k, 0))
    rid = lax.broadcasted_iota(jnp.int32, (HALO, v.shape[1]), 0)
    tail = jnp.where(rid >= HALO - k, hh, r[n - HALO:])
    return jnp.concatenate([r[:n - HALO], tail], axis=0)


def _conv(p, halo, w_ref, b_ref, first):
    return (w_ref[2:3, :] * p + w_ref[1:2, :] * _shift_down(p, halo, 1, first)
            + w_ref[0:1, :] * _shift_down(p, halo, 2, first) + b_ref[...])


BF16_ROWS = 16


def ffn_in_fused(h, w_in4, wg, wu, bg, bu, *, name):
    t, k = h.shape
    s_all, _, n_s = w_in4.shape
    half = s_all // 2
    tm = _row_tile(t, 2 * k * n_s * 2, k * 2 + 4 * n_s * 4 + n_s * 2)

    def body(h_ref, hh_ref, wg_ref, wu_ref, cg_ref, cu_ref, bg_ref, bu_ref, pg_ref, pu_ref, gate_ref, up_ref, a_ref):
        first = pl.program_id(1) == 0
        hv, hh = h_ref[...], hh_ref[...]
        outs = []
        for w_ref, c_ref, b_ref, p_ref, o_ref in ((wg_ref, cg_ref, bg_ref, pg_ref, gate_ref),
                                                  (wu_ref, cu_ref, bu_ref, pu_ref, up_ref)):
            p = _dot(hv, w_ref[0])
            p_ref[...] = p
            hu = _conv(p, _dot(hh, w_ref[0])[BF16_ROWS - HALO:], c_ref, b_ref, first)
            o_ref[...] = hu
            outs.append(hu)
        gate, up = outs
        a_ref[...] = (gate * jax.nn.sigmoid(gate) * up).astype(bf16)

    tile = pl.BlockSpec((tm, n_s), lambda j, i: (i, j))
    cw = pl.BlockSpec((3, n_s), lambda j, i: (0, j))
    cb = pl.BlockSpec((1, n_s), lambda j, i: (0, j))
    f = half * n_s
    return pl.pallas_call(
        body, name=name, grid=(half, t // tm),
        in_specs=[pl.BlockSpec((tm, k), lambda j, i: (i, 0)),
                  pl.BlockSpec((BF16_ROWS, k), lambda j, i: (jnp.maximum(i * (tm // BF16_ROWS) - 1, 0), 0)),
                  pl.BlockSpec((1, k, n_s), lambda j, i: (j, 0, 0)),
                  pl.BlockSpec((1, k, n_s), lambda j, i: (j + half, 0, 0)), cw, cw, cb, cb],
        out_specs=[tile] * 5,
        out_shape=[jax.ShapeDtypeStruct((t, f), f32)] * 4 + [jax.ShapeDtypeStruct((t, f), bf16)],
        compiler_params=_params("parallel", "parallel"))(h, h, w_in4, w_in4, wg, wu, bg, bu)


def _gate_grads(gate, up, dav):
    sg = jax.nn.sigmoid(gate)
    return dav * up * (sg * (1.0 + gate * (1.0 - sg))), dav * gate * sg


GATE_BWD_ROWS = 512


def ffn_gate_bwd(dy, w_out, pg, pu, gate, up, wg, wu, *, name):
    t, f = pg.shape
    d = dy.shape[1]
    tr = min(GATE_BWD_ROWS, t)
    nsteps = t // tr
    tc = f // 2

    def body(dy_ref, dyn_ref, w_ref, pg_ref, pu_ref, gate_ref, gaten_ref, up_ref, upn_ref, wg_ref, wu_ref,
             dg_ref, du_ref, sg_ref, su_ref):
        i = pl.program_id(1)
        last = i == nsteps - 1
        w = w_ref[0]
        da = _dot_nt(dy_ref[...].astype(bf16), w)
        da_n = _dot_nt(dyn_ref[...].astype(bf16), w)
        dgate, dup = _gate_grads(gate_ref[...], up_ref[...], da)
        dgate_n, dup_n = _gate_grads(gaten_ref[...], upn_ref[...], da_n)
        rid = lax.broadcasted_iota(jnp.int32, (8, tc), 0)
        for dd, d_n, c_ref, p_ref, o_ref, s_ref in ((dgate, dgate_n, wg_ref, pg_ref, dg_ref, sg_ref),
                                                    (dup, dup_n, wu_ref, pu_ref, du_ref, su_ref)):
            d1, d2 = _shift_up(dd, d_n, 1, last), _shift_up(dd, d_n, 2, last)
            o_ref[...] = (c_ref[2:3, :] * dd + c_ref[1:2, :] * d1 + c_ref[0:1, :] * d2).astype(bf16)
            p = p_ref[...]
            sums = [jnp.sum(d2 * p, axis=0, keepdims=True), jnp.sum(d1 * p, axis=0, keepdims=True),
                    jnp.sum(dd * p, axis=0, keepdims=True), jnp.sum(dd, axis=0, keepdims=True)]
            part = jnp.zeros((8, tc), f32)
            for k, sk in enumerate(sums):
                part = jnp.where(rid == k, sk, part)

            @pl.when(i == 0)
            def _(s_ref=s_ref, part=part):
                s_ref[...] = part

            @pl.when(i > 0)
            def _(s_ref=s_ref, part=part):
                s_ref[...] += part

    def nxt_rows(j, i):
        return (jnp.minimum((i + 1) * (tr // HALO), t // HALO - 1), j)

    tile = pl.BlockSpec((tr, tc), lambda j, i: (i, j))
    nxt = pl.BlockSpec((HALO, tc), nxt_rows)
    wspec = pl.BlockSpec((3, tc), lambda j, i: (0, j))
    stat = pl.BlockSpec((8, tc), lambda j, i: (0, j))
    return pl.pallas_call(
        body, name=name, grid=(2, nsteps),
        in_specs=[pl.BlockSpec((tr, d), lambda j, i: (i, 0)),
                  pl.BlockSpec((HALO, d), lambda j, i: (nxt_rows(j, i)[0], 0)),
                  pl.BlockSpec((1, tc, d), lambda j, i: (j, 0, 0)),
                  tile, tile, tile, nxt, tile, nxt, wspec, wspec],
        out_specs=[tile, tile, stat, stat],
        out_shape=[jax.ShapeDtypeStruct((t, f), bf16), jax.ShapeDtypeStruct((t, f), bf16),
                   jax.ShapeDtypeStruct((8, f), f32), jax.ShapeDtypeStruct((8, f), f32)],
        compiler_params=_params("parallel", "arbitrary"))(
            dy, dy, w_out.reshape(2, tc, d), pg, pu, gate, gate, up, up, wg, wu)


def _head_mean_matrix():
    i = lax.broadcasted_iota(jnp.int32, (LANES, LANES), 0) // HEAD_DIM
    j = lax.broadcasted_iota(jnp.int32, (LANES, LANES), 1) // HEAD_DIM
    return jnp.where(i == j, 1.0 / HEAD_DIM, 0.0).astype(bf16)


def _lane_half(shape):
    return (lax.broadcasted_iota(jnp.int32, shape, 1) % LANES) // HEAD_DIM


def q_proj_norm(h, w, g2, *, name, scale):
    t, k = h.shape
    n = w.shape[1]
    tm = _row_tile(t, k * n * 2, k * 2 + n * 4 + n * 2)

    def body(h_ref, w_ref, g_ref, qp_ref, o_ref):
        qp_ref[...] = _dot(h_ref[...], w_ref[...])
        bd = _head_mean_matrix()
        for cb in range(n // LANES):
            cols = slice(cb * LANES, (cb + 1) * LANES)
            xc = qp_ref[:, cols]
            rh = lax.rsqrt(_dot_split(xc * xc, bd) + EPS)
            o_ref[:, cols] = (xc * rh * g_ref[...] * scale).astype(bf16)

    row = pl.BlockSpec((tm, n), lambda i: (i, 0))
    return pl.pallas_call(
        body, name=name, grid=(t // tm,),
        in_specs=[pl.BlockSpec((tm, k), lambda i: (i, 0)), pl.BlockSpec((k, n), lambda i: (0, 0)),
                  pl.BlockSpec((1, LANES), lambda i: (0, 0))],
        out_specs=[row, row], out_shape=[jax.ShapeDtypeStruct((t, n), f32), jax.ShapeDtypeStruct((t, n), bf16)],
        compiler_params=_params("parallel"))(h, w, g2)


def kv_proj_post(h, w, g2, *, name):
    t, k = h.shape
    n = w.shape[1]
    kw = n // 2
    tm = _row_tile(t, k * n * 2, k * 2 + n * 4 + 2 * n * 2)

    def body(h_ref, w_ref, g_ref, kv_ref, k_ref, v_ref):
        kv_ref[...] = _dot(h_ref[...], w_ref[...])
        bd = _head_mean_matrix()
        half = _lane_half((tm, LANES))
        for cb in range(kw // LANES):
            xc = kv_ref[:, cb * LANES:(cb + 1) * LANES]
            rh = lax.rsqrt(_dot_split(xc * xc, bd) + EPS)
            kn = xc * rh * g_ref[...]
            vc = kv_ref[:, kw + cb * LANES:kw + (cb + 1) * LANES]
            for src, dst in ((kn, k_ref), (vc, v_ref)):
                sw = pltpu.roll(src, HEAD_DIM, 1)
                for hf in range(2):
                    blk = 2 * cb + hf
                    dst[:, blk * LANES:(blk + 1) * LANES] = jnp.where(half == hf, src, sw).astype(bf16)

    row = pl.BlockSpec((tm, n), lambda i: (i, 0))
    return pl.pallas_call(
        body, name=name, grid=(t // tm,),
        in_specs=[pl.BlockSpec((tm, k), lambda i: (i, 0)), pl.BlockSpec((k, n), lambda i: (0, 0)),
                  pl.BlockSpec((1, LANES), lambda i: (0, 0))],
        out_specs=[row, row, row],
        out_shape=[jax.ShapeDtypeStruct((t, n), f32), jax.ShapeDtypeStruct((t, n), bf16), jax.ShapeDtypeStruct((t, n), bf16)],
        compiler_params=_params("parallel"))(h, w, g2)


def q_norm_bwd(dq, qp, g2, *, name, scale, tr=512):
    t, w = qp.shape
    tr = min(tr, t)

    def body(dq_ref, x_ref, g_ref, o_ref, dg_ref):
        i = pl.program_id(0)
        bd = _head_mean_matrix()
        acc = jnp.zeros((1, LANES), f32)
        for cb in range(w // LANES):
            cols = slice(cb * LANES, (cb + 1) * LANES)
            xc = x_ref[:, cols]
            rh = lax.rsqrt(_dot_split(xc * xc, bd) + EPS)
            xh = xc * rh
            dy = dq_ref[:, cols] * scale
            acc = acc + jnp.sum(dy * xh, axis=0, keepdims=True)
            tg = dy * g_ref[...]
            o_ref[:, cols] = (rh * (tg - xh * _dot_split(tg * xh, bd))).astype(bf16)

        @pl.when(i == 0)
        def _():
            dg_ref[...] = acc

        @pl.when(i > 0)
        def _():
            dg_ref[...] += acc

    row = pl.BlockSpec((tr, w), lambda i: (i, 0))
    vec = pl.BlockSpec((1, LANES), lambda i: (0, 0))
    return pl.pallas_call(
        body, name=name, grid=(t // tr,), in_specs=[row, row, vec], out_specs=[row, vec],
        out_shape=[jax.ShapeDtypeStruct((t, w), bf16), jax.ShapeDtypeStruct((1, LANES), f32)],
        compiler_params=_params("arbitrary"))(dq, qp, g2)


def kv_post_bwd(dk2, dv2, kv, g2, *, name, tr=512):
    t, w = kv.shape
    tr = min(tr, t)
    kw = w // 2

    def body(dk_ref, dv_ref, x_ref, g_ref, o_ref, dg_ref):
        i = pl.program_id(0)
        bd = _head_mean_matrix()
        half = _lane_half((tr, LANES))
        acc = jnp.zeros((1, LANES), f32)

        def fold(ref, cb):
            a = ref[:, (2 * cb) * LANES:(2 * cb + 1) * LANES]
            b = ref[:, (2 * cb + 1) * LANES:(2 * cb + 2) * LANES]
            return jnp.where(half == 0, a + pltpu.roll(a, HEAD_DIM, 1), b + pltpu.roll(b, HEAD_DIM, 1))

        for cb in range(kw // LANES):
            cols = slice(cb * LANES, (cb + 1) * LANES)
            xc = x_ref[:, cols]
            rh = lax.rsqrt(_dot_split(xc * xc, bd) + EPS)
            xh = xc * rh
            dy = fold(dk_ref, cb)
            acc = acc + jnp.sum(dy * xh, axis=0, keepdims=True)
            tg = dy * g_ref[...]
            o_ref[:, cols] = (rh * (tg - xh * _dot_split(tg * xh, bd))).astype(bf16)
            o_ref[:, kw + cb * LANES:kw + (cb + 1) * LANES] = fold(dv_ref, cb).astype(bf16)

        @pl.when(i == 0)
        def _():
            dg_ref[...] = acc

        @pl.when(i > 0)
        def _():
            dg_ref[...] += acc

    dup = pl.BlockSpec((tr, 2 * kw), lambda i: (i, 0))
    row = pl.BlockSpec((tr, w), lambda i: (i, 0))
    vec = pl.BlockSpec((1, LANES), lambda i: (0, 0))
    return pl.pallas_call(
        body, name=name, grid=(t // tr,), in_specs=[dup, dup, row, vec], out_specs=[row, vec],
        out_shape=[jax.ShapeDtypeStruct((t, w), bf16), jax.ShapeDtypeStruct((1, LANES), f32)],
        compiler_params=_params("arbitrary"))(dk2, dv2, kv, g2)


def _slope(h):
    return 2.0 ** (-8.0 * (h + 1) / N_Q_HEADS)


GROUP_ROWS = Q_PER_KV * CHUNK


def _band_mask(n):
    tq = lax.broadcasted_iota(jnp.int32, (GROUP_ROWS, 2 * CHUNK), 0) % CHUNK
    jk = lax.broadcasted_iota(jnp.int32, (GROUP_ROWS, 2 * CHUNK), 1)
    dist = tq + CHUNK - jk
    ok = (dist >= 0) & (dist < CHUNK) & jnp.logical_not((n == 0) & (jk < CHUNK))
    return dist.astype(f32), ok


def _band(ref, n, kh):
    p0 = pl.multiple_of(jnp.maximum(n - 1, 0) * CHUNK, CHUNK)
    c0 = pl.multiple_of(n * CHUNK, CHUNK)
    cols = slice(kh * LANES, (kh + 1) * LANES)
    return jnp.concatenate([ref[pl.ds(p0, CHUNK), cols], ref[pl.ds(c0, CHUNK), cols]], axis=0)


def _stack_heads(ref, kh, half):
    parts = []
    for cb in (2 * kh, 2 * kh + 1):
        xc = ref[:, cb * LANES:(cb + 1) * LANES].astype(f32)
        parts += [jnp.where(half == hf, xc, 0.0).astype(bf16) for hf in range(2)]
    return jnp.concatenate(parts, axis=0)


def _unstack_heads(x4, half):
    return (jnp.where(half == 0, x4[0:CHUNK], x4[CHUNK:2 * CHUNK]),
            jnp.where(half == 0, x4[2 * CHUNK:3 * CHUNK], x4[3 * CHUNK:]))


def _per_head_column(kh, values):
    grp = lax.broadcasted_iota(jnp.int32, (GROUP_ROWS, 1), 0) // CHUNK
    col = jnp.full((GROUP_ROWS, 1), values[0], f32)
    for g in range(1, Q_PER_KV):
        col = jnp.where(grp == g, values[g], col)
    return col


def _softmax_band(q4, kband, dist, ok, slope, sink):
    s = _dot_nt(q4, kband)
    s = jnp.where(ok, s - slope * dist, -jnp.inf)
    m = jnp.maximum(jnp.max(s, axis=1, keepdims=True), sink)
    e = jnp.exp(s - m)
    es = jnp.exp(sink - m)
    den = jnp.sum(e, axis=1, keepdims=True) + es
    return e / den, es / den


def attn_fwd(q, k2, v2, sinks, *, name):
    t, w = q.shape
    nb = t // CHUNK

    def body(sink_ref, q_ref, k_ref, v_ref, o_ref):
        n = pl.program_id(0)
        dist, ok = _band_mask(n)
        half = _lane_half((CHUNK, LANES))
        khs = range(N_KV_HEADS)
        heads = [[Q_PER_KV * kh + g for g in range(Q_PER_KV)] for kh in khs]
        q4 = [_stack_heads(q_ref, kh, half) for kh in khs]
        soft = [_softmax_band(q4[kh], _band(k_ref, n, kh), dist, ok, _per_head_column(kh, [_slope(h) for h in heads[kh]]),
                              _per_head_column(kh, [sink_ref[h] for h in heads[kh]])) for kh in khs]
        o4 = [_dot(soft[kh][0].astype(bf16), _band(v_ref, n, kh)) for kh in khs]
        for kh in khs:
            lo, hi = _unstack_heads(o4[kh], half)
            o_ref[:, (2 * kh) * LANES:(2 * kh + 1) * LANES] = lo.astype(bf16)
            o_ref[:, (2 * kh + 1) * LANES:(2 * kh + 2) * LANES] = hi.astype(bf16)

    full = pl.BlockSpec((t, k2.shape[1]), lambda n: (0, 0))
    return pl.pallas_call(
        body, name=name, grid=(nb,),
        in_specs=[pl.BlockSpec(memory_space=pltpu.SMEM), pl.BlockSpec((CHUNK, w), lambda n: (n, 0)), full, full],
        out_specs=pl.BlockSpec((CHUNK, w), lambda n: (n, 0)),
        out_shape=jax.ShapeDtypeStruct((t, w), bf16),
        compiler_params=_params("parallel"))(sinks, q, k2, v2)


def attn_bwd(q, k2, v2, do, sinks, *, name):
    t, w = q.shape
    nb = t // CHUNK
    kw = k2.shape[1]

    def body(sink_ref, q_ref, k_ref, v_ref, do_ref, dq_ref, dk_ref, dv_ref, ds_ref, kc_ref, vc_ref):
        n = pl.program_id(0)

        @pl.when(n == 0)
        def _():
            ds_ref[...] = jnp.zeros_like(ds_ref)
            kc_ref[...] = jnp.zeros_like(kc_ref)
            vc_ref[...] = jnp.zeros_like(vc_ref)
            dk_ref[...] = jnp.zeros_like(dk_ref)
            dv_ref[...] = jnp.zeros_like(dv_ref)

        @pl.when(n == nb)
        def _():
            dk_ref[...] = kc_ref[...]
            dv_ref[...] = vc_ref[...]

        @pl.when(n < nb)
        def _():
            dist, ok = _band_mask(n)
            half = _lane_half((CHUNK, LANES))
            lane = lax.broadcasted_iota(jnp.int32, (1, LANES), 1)
            sink_acc = jnp.zeros((1, LANES), f32)
            khs = range(N_KV_HEADS)
            heads = [[Q_PER_KV * kh + g for g in range(Q_PER_KV)] for kh in khs]
            q4 = [_stack_heads(q_ref, kh, half) for kh in khs]
            do4 = [_stack_heads(do_ref, kh, half) for kh in khs]
            kband = [_band(k_ref, n, kh) for kh in khs]
            vband = [_band(v_ref, n, kh) for kh in khs]
            soft = [_softmax_band(q4[kh], kband[kh], dist, ok, _per_head_column(kh, [_slope(h) for h in heads[kh]]),
                                  _per_head_column(kh, [sink_ref[h] for h in heads[kh]])) for kh in khs]
            dp = [_dot_nt(do4[kh], vband[kh]) for kh in khs]
            delta = [jnp.sum(soft[kh][0] * dp[kh], axis=1, keepdims=True) for kh in khs]
            dsb = [(soft[kh][0] * (dp[kh] - delta[kh])).astype(bf16) for kh in khs]
            dq4 = [_dot(dsb[kh], kband[kh]) for kh in khs]
            dkb = [_dot_tn(dsb[kh], q4[kh]) for kh in khs]
            dvb = [_dot_tn(soft[kh][0].astype(bf16), do4[kh]) for kh in khs]
            for kh in khs:
                sd = soft[kh][1] * delta[kh]
                for g, h in enumerate(heads[kh]):
                    part = jnp.sum(sd[g * CHUNK:(g + 1) * CHUNK], axis=0, keepdims=True)
                    sink_acc = sink_acc + jnp.where(lane == h, -part, 0.0)
                lo, hi = _unstack_heads(dq4[kh], half)
                dq_ref[:, (2 * kh) * LANES:(2 * kh + 1) * LANES] = lo
                dq_ref[:, (2 * kh + 1) * LANES:(2 * kh + 2) * LANES] = hi
                cols = slice(kh * LANES, (kh + 1) * LANES)
                dk_ref[:, cols] = kc_ref[:, cols] + dkb[kh][0:CHUNK]
                dv_ref[:, cols] = vc_ref[:, cols] + dvb[kh][0:CHUNK]
                kc_ref[:, cols] = dkb[kh][CHUNK:]
                vc_ref[:, cols] = dvb[kh][CHUNK:]
            ds_ref[...] += sink_acc

    full = pl.BlockSpec((t, kw), lambda n: (0, 0))
    qblk = pl.BlockSpec((CHUNK, w), lambda n: (jnp.minimum(n, nb - 1), 0))
    kblk = pl.BlockSpec((CHUNK, kw), lambda n: (jnp.maximum(n - 1, 0), 0))
    return pl.pallas_call(
        body, name=name, grid=(nb + 1,),
        in_specs=[pl.BlockSpec(memory_space=pltpu.SMEM), qblk, full, full, qblk],
        out_specs=[qblk, kblk, kblk, pl.BlockSpec((1, LANES), lambda n: (0, 0))],
        out_shape=[jax.ShapeDtypeStruct((t, w), f32), jax.ShapeDtypeStruct((t, kw), f32),
                   jax.ShapeDtypeStruct((t, kw), f32), jax.ShapeDtypeStruct((1, LANES), f32)],
        scratch_shapes=[pltpu.VMEM((CHUNK, kw), f32), pltpu.VMEM((CHUNK, kw), f32)],
        compiler_params=_params("arbitrary"))(sinks, q, k2, v2, do)


N_STEPS = 8


def _row_blocks(shape):
    if len(shape) == 2:
        r, c = shape
        return (r // N_STEPS, c), (lambda s: (s, 0))
    l, r, c = shape
    per = N_STEPS // l
    return (1, r // per, c), (lambda s: (s // per, s % per, 0))


CAST_STEPS = 4


def cast_into_slot(arrays, k_arr, *, name):
    in_specs, out_specs, out_shape, layers = [], [], [], []
    for a in arrays:
        r, c = a.shape[-2:]
        rb = r // CAST_STEPS
        if a.ndim == 2:
            in_specs.append(pl.BlockSpec((rb, c), lambda s, k: (s, 0)))
            layers.append(None)
        else:
            for l in range(a.shape[0]):
                in_specs.append(pl.BlockSpec((1, rb, c), lambda s, k, l=l: (l, s, 0)))
                layers.append(l)
        for _ in range(1 if a.ndim == 2 else a.shape[0]):
            out_specs.append(pl.BlockSpec((1, rb, c), lambda s, k: (k[0], s, 0)))
            out_shape.append(jax.ShapeDtypeStruct((N_SHARDS, r, c), bf16))
    n = len(in_specs)

    def body(k_ref, *refs):
        for i_ref, o_ref, l in zip(refs[:n], refs[n:], layers):
            o_ref[0] = (i_ref[...] if l is None else i_ref[0]).astype(bf16)

    args = []
    for a in arrays:
        args += [a] * (1 if a.ndim == 2 else a.shape[0])
    return pl.pallas_call(
        body, name=name,
        grid_spec=pltpu.PrefetchScalarGridSpec(num_scalar_prefetch=1, grid=(CAST_STEPS,),
                                               in_specs=in_specs, out_specs=out_specs),
        out_shape=out_shape, compiler_params=_params("parallel"))(k_arr, *args)


def adamw(ws, gs, ms, vs, *, name):
    n = len(ws)
    specs, g_specs, g_count = [], [], []
    for w, g_list in zip(ws, gs):
        blk, index = _row_blocks(w.shape)
        specs.append(pl.BlockSpec(blk, index))
        layers = len(g_list)
        per = N_STEPS // layers
        g_count.append(layers)
        for l in range(layers):
            g_specs.append(pl.BlockSpec(blk[-2:], lambda s, l=l, per=per: (jnp.where(s // per == l, s % per, 0), 0)))
    ng = len(g_specs)

    def body(*refs):
        s = pl.program_id(0)
        g_refs = refs[3 * n:3 * n + ng]
        outs = refs[3 * n + ng:]
        off = 0
        for i in range(n):
            w_ref, m_ref, v_ref = refs[i], refs[n + i], refs[2 * n + i]
            go_ref, d_ref, nm_ref, nv_ref = (outs[k * n + i] for k in range(4))
            layers = g_count[i]
            g = g_refs[off][...]
            for l in range(1, layers):
                g = jnp.where(s // (N_STEPS // layers) == l, g_refs[off + l][...], g)
            off += layers
            g = g.reshape(w_ref.shape)
            m = ADAM_B1 * m_ref[...] + (1.0 - ADAM_B1) * g
            v = ADAM_B2 * v_ref[...] + (1.0 - ADAM_B2) * (g * g)
            m_hat = m / ADAM_C1
            v_hat = v / ADAM_C2
            go_ref[...] = g
            d_ref[...] = -ADAM_LR * (m_hat / (jnp.sqrt(v_hat) + ADAM_EPS) + ADAM_WD * w_ref[...])
            nm_ref[...] = m
            nv_ref[...] = v

    outs = pl.pallas_call(
        body, name=name, grid=(N_STEPS,), in_specs=specs * 3 + g_specs, out_specs=specs * 4,
        out_shape=[jax.ShapeDtypeStruct(a.shape, f32) for a in ws] * 4,
        compiler_params=_params("parallel"))(*ws, *ms, *vs, *[g for g_list in gs for g in g_list])
    return [outs[k * n:(k + 1) * n] for k in range(4)]


def _adamw_update(w, g, m, v):
    m = ADAM_B1 * m + (1.0 - ADAM_B1) * g
    v = ADAM_B2 * v + (1.0 - ADAM_B2) * (g * g)
    m_hat = m / ADAM_C1
    v_hat = v / ADAM_C2
    return -ADAM_LR * (m_hat / (jnp.sqrt(v_hat) + ADAM_EPS) + ADAM_WD * w), m, v


def adamw_small(ws, gs, ms, vs, *, name):
    n = len(ws)

    def body(*refs):
        for i in range(n):
            w_ref, g_ref, m_ref, v_ref = (refs[k * n + i] for k in range(4))
            d_ref, nm_ref, nv_ref = (refs[(4 + k) * n + i] for k in range(3))
            d_ref[...], nm_ref[...], nv_ref[...] = _adamw_update(w_ref[...], g_ref[...], m_ref[...], v_ref[...])

    outs = pl.pallas_call(
        body, name=name, out_shape=[jax.ShapeDtypeStruct(a.shape, f32) for a in ws] * 3)(*ws, *gs, *ms, *vs)
    return outs[:n], outs[n:2 * n], outs[2 * n:]


def _place():
    return lax.axis_index("x"), lax.axis_index("y"), lax.axis_index("c")


def gather_shards(bufs, *, name, split):
    n = len(bufs)

    def body(*refs):
        bufs_ = refs[:n]
        isend, irecv, dsend, drecv = refs[2 * n:]
        x, y, c = _place()
        k = 2 * x + y
        peers = [(1 - x, y, c), (x, 1 - y, c), (1 - x, 1 - y, c)]
        peer_k = [2 * (1 - x) + y, 2 * x + (1 - y), 2 * (1 - x) + (1 - y)]

        def slab(a, q, h):
            if not split[a]:
                return bufs_[a].at[q]
            half = bufs_[a].shape[1] // 2
            return bufs_[a].at[q, pl.ds(pl.multiple_of(h * half, 16), half)]

        def ici(a, j, q):
            return pltpu.make_async_remote_copy(
                src_ref=slab(a, q, c), dst_ref=slab(a, q, c), send_sem=isend.at[3 * a + j], recv_sem=irecv.at[3 * a + j],
                device_id=peers[j], device_id_type=MESH)

        def d2d(a, j, h):
            return pltpu.make_async_remote_copy(
                src_ref=slab(a, peer_k[j], h), dst_ref=slab(a, peer_k[j], h), send_sem=dsend.at[3 * a + j],
                recv_sem=drecv.at[3 * a + j], device_id=(x, y, 1 - c), device_id_type=MESH)

        for a in range(n):
            for j in range(3):
                ici(a, j, k).start()
        for a in range(n):
            for j in range(3):
                ici(a, j, peer_k[j]).wait_recv()
                if split[a]:
                    d2d(a, j, c).start()
        for a in range(n):
            for j in range(3):
                if split[a]:
                    d2d(a, j, 1 - c).wait_recv()
        for a in range(n):
            for j in range(3):
                ici(a, j, k).wait_send()
                if split[a]:
                    d2d(a, j, c).wait_send()

    return pl.pallas_call(
        body, name=name, in_specs=[ANY] * n, out_specs=[ANY] * n,
        out_shape=[jax.ShapeDtypeStruct(b.shape, b.dtype) for b in bufs],
        input_output_aliases={i: i for i in range(n)},
        scratch_shapes=[pltpu.SemaphoreType.DMA((3 * n,))] * 4)(*bufs)


HBM = pl.BlockSpec(memory_space=pltpu.HBM)
SEM = pl.BlockSpec(memory_space=pltpu.SEMAPHORE)
DATAFLOW = pltpu.SideEffectType.DATAFLOW_SIDE_EFFECTING


def _chip_peers():
    x, y, c = _place()
    return 2 * x + y, [(1 - x, y, c), (x, 1 - y, c), (1 - x, 1 - y, c)], [2 * (1 - x) + y, 2 * x + (1 - y), 2 * (1 - x) + (1 - y)]


def _half_slab(ref, q, h):
    half = ref.shape[1] // 2
    return ref.at[q, pl.ds(pl.multiple_of(h * half, BF16_ROWS), half)]


def gather_start(bufs, groups, after, *, name):
    n = len(bufs)
    ng = len(groups)

    def body(*refs):
        ins = refs[:n]
        sends, recvs = refs[2 * n + 1:2 * n + 1 + ng], refs[2 * n + 1 + ng:2 * n + 1 + 2 * ng]
        token = refs[-1]
        c = lax.axis_index("c")
        k, peers, _ = _chip_peers()
        for gi, grp in enumerate(groups):
            for pos, a in enumerate(grp):
                for j in range(3):
                    pltpu.make_async_remote_copy(
                        src_ref=_half_slab(ins[a], k, c), dst_ref=_half_slab(ins[a], k, c), send_sem=sends[gi].at[3 * pos + j],
                        recv_sem=recvs[gi].at[3 * pos + j], device_id=peers[j], device_id_type=MESH).start()
        token[...] = jnp.zeros_like(token)

    sems = [pltpu.SemaphoreType.DMA((3 * len(grp),)) for grp in groups]
    outs = pl.pallas_call(
        body, name=name, in_specs=[HBM] * n + [ANY],
        out_specs=[HBM] * n + [SEM] * (2 * ng) + [pl.BlockSpec(memory_space=pltpu.VMEM)],
        out_shape=[pltpu.HBM(b.shape, b.dtype) for b in bufs] + sems + sems + [jax.ShapeDtypeStruct((8, LANES), f32)],
        input_output_aliases={i: i for i in range(n)},
        compiler_params=pltpu.CompilerParams(has_side_effects=DATAFLOW))(
            *[pltpu.with_memory_space_constraint(b, pltpu.HBM) for b in bufs], after)
    return outs[:n], outs[n:n + ng], outs[n + ng:n + 2 * ng], outs[-1]


def gather_wait(bufs, send_sems, recv_sems, after, *, name):
    n = len(bufs)

    def body(*refs):
        ins = refs[:n]
        send, recv = refs[n], refs[n + 1]
        c = lax.axis_index("c")
        k, peers, peer_k = _chip_peers()
        for a in range(n):
            for j in range(3):
                copy = pltpu.make_async_remote_copy(
                    src_ref=_half_slab(ins[a], k, c), dst_ref=_half_slab(ins[a], peer_k[j], c), send_sem=send.at[3 * a + j],
                    recv_sem=recv.at[3 * a + j], device_id=peers[j], device_id_type=MESH)
                copy.wait_send()
                copy.wait_recv()

    return pl.pallas_call(
        body, name=name, in_specs=[HBM] * n + [SEM, SEM, ANY], out_specs=[HBM] * n,
        out_shape=[pltpu.HBM(b.shape, b.dtype) for b in bufs],
        input_output_aliases={i: i for i in range(n)},
        compiler_params=pltpu.CompilerParams(has_side_effects=DATAFLOW))(*bufs, send_sems, recv_sems, after)


def forward_halves(bufs, *, name):
    n = len(bufs)

    def body(*refs):
        bufs_ = refs[:n]
        send, recv = refs[2 * n:]
        x, y, c = _place()
        _, _, peer_k = _chip_peers()

        def copy(a, j, h):
            return pltpu.make_async_remote_copy(
                src_ref=_half_slab(bufs_[a], peer_k[j], h), dst_ref=_half_slab(bufs_[a], peer_k[j], h),
                send_sem=send.at[3 * a + j], recv_sem=recv.at[3 * a + j], device_id=(x, y, 1 - c), device_id_type=MESH)

        for a in range(n):
            for j in range(3):
                copy(a, j, c).start()
        for a in range(n):
            for j in range(3):
                copy(a, j, 1 - c).wait_recv()
        for a in range(n):
            for j in range(3):
                copy(a, j, c).wait_send()

    return pl.pallas_call(
        body, name=name, in_specs=[ANY] * n, out_specs=[ANY] * n,
        out_shape=[jax.ShapeDtypeStruct(b.shape, b.dtype) for b in bufs],
        input_output_aliases={i: i for i in range(n)},
        scratch_shapes=[pltpu.SemaphoreType.DMA((3 * n,))] * 2)(*bufs)


def _forward_copies(bufs_, send, recv, h):
    x, y, c = _place()
    _, _, peer_k = _chip_peers()
    return [pltpu.make_async_remote_copy(
        src_ref=_half_slab(bufs_[a], peer_k[j], h), dst_ref=_half_slab(bufs_[a], peer_k[j], h),
        send_sem=send.at[3 * a + j], recv_sem=recv.at[3 * a + j], device_id=(x, y, 1 - c), device_id_type=MESH)
        for a in range(len(bufs_)) for j in range(3)]


def forward_start(bufs, after, *, name):
    n = len(bufs)

    def body(*refs):
        for cp in _forward_copies(refs[:n], refs[2 * n + 1], refs[2 * n + 2], lax.axis_index("c")):
            cp.start()
        refs[-1][...] = jnp.zeros_like(refs[-1])

    sems = [pltpu.SemaphoreType.DMA((3 * n,))] * 2
    outs = pl.pallas_call(
        body, name=name, in_specs=[HBM] * n + [ANY],
        out_specs=[HBM] * n + [SEM] * 2 + [pl.BlockSpec(memory_space=pltpu.VMEM)],
        out_shape=[pltpu.HBM(b.shape, b.dtype) for b in bufs] + sems + [jax.ShapeDtypeStruct((8, LANES), f32)],
        input_output_aliases={i: i for i in range(n)},
        compiler_params=pltpu.CompilerParams(has_side_effects=DATAFLOW))(*bufs, after)
    return (n, outs[:-1]), outs[-1]


def forward_wait(state, after, *, name):
    n, held = state

    def body(*refs):
        c = lax.axis_index("c")
        for mine, theirs in zip(_forward_copies(refs[:n], refs[n], refs[n + 1], c),
                                _forward_copies(refs[:n], refs[n], refs[n + 1], 1 - c)):
            mine.wait_send()
            theirs.wait_recv()

    return pl.pallas_call(
        body, name=name, in_specs=[HBM] * n + [SEM] * 2 + [ANY], out_specs=[HBM] * n,
        out_shape=[pltpu.HBM(b.shape, b.dtype) for b in held[:n]],
        input_output_aliases={i: i for i in range(n)},
        compiler_params=pltpu.CompilerParams(has_side_effects=DATAFLOW))(*held, after)


def _sibling_copies(srcs, lands, send, recv):
    x, y, c = _place()
    return [pltpu.make_async_remote_copy(src_ref=srcs[a], dst_ref=lands[a], send_sem=send.at[a], recv_sem=recv.at[a],
                                         device_id=(x, y, 1 - c), device_id_type=MESH) for a in range(len(srcs))]


def sibling_start(arrays, after, *, name):
    n = len(arrays)
    lands = [pltpu.with_memory_space_constraint(lax.empty(a.shape, a.dtype), pltpu.HBM) for a in arrays]

    def body(*refs):
        for cp in _sibling_copies(refs[:n], refs[n:2 * n], refs[4 * n + 1], refs[4 * n + 2]):
            cp.start()
        refs[-1][...] = jnp.zeros_like(refs[-1])

    bufs = list(arrays) + lands
    sems = [pltpu.SemaphoreType.DMA((n,))] * 2
    outs = pl.pallas_call(
        body, name=name, in_specs=[HBM] * (2 * n) + [ANY],
        out_specs=[HBM] * (2 * n) + [SEM] * 2 + [pl.BlockSpec(memory_space=pltpu.VMEM)],
        out_shape=[pltpu.HBM(b.shape, b.dtype) for b in bufs] + sems + [jax.ShapeDtypeStruct((8, LANES), f32)],
        input_output_aliases={i: i for i in range(2 * n)},
        compiler_params=pltpu.CompilerParams(has_side_effects=DATAFLOW))(
            *[pltpu.with_memory_space_constraint(b, pltpu.HBM) for b in bufs], after)
    return (n, outs[:-1]), outs[-1]


def sibling_wait(state, after, *, name):
    n, held = state

    def body(*refs):
        for cp in _sibling_copies(refs[:n], refs[n:2 * n], refs[2 * n], refs[2 * n + 1]):
            cp.wait_send()
            cp.wait_recv()

    outs = pl.pallas_call(
        body, name=name, in_specs=[HBM] * (2 * n) + [SEM] * 2 + [ANY], out_specs=[HBM] * (2 * n),
        out_shape=[pltpu.HBM(b.shape, b.dtype) for b in held[:2 * n]],
        input_output_aliases={i: i for i in range(2 * n)},
        compiler_params=pltpu.CompilerParams(has_side_effects=DATAFLOW))(*held, after)
    return outs[n:]


ALL_MASKS = [(mx, my, mc) for mx in (0, 1) for my in (0, 1) for mc in (0, 1)][1:]


def _scatter_copies(srcs, lands, ev, send, recv, esend, erecv):
    x, y, c = _place()
    me = 4 * x + 2 * y + c
    k, peers, peer_k = _chip_peers()
    out = []
    for a in range(len(srcs)):
        for j in range(3):
            out.append(pltpu.make_async_remote_copy(
                src_ref=srcs[a].at[peer_k[j]], dst_ref=lands[a].at[j], send_sem=send.at[3 * a + j],
                recv_sem=recv.at[3 * a + j], device_id=peers[j], device_id_type=MESH))
    start_ev, wait_ev = [], []
    if ev is not None:
        for j, (mx, my, mc) in enumerate(ALL_MASKS):
            peer = (x ^ mx, y ^ my, c ^ mc)
            start_ev.append(pltpu.make_async_remote_copy(
                src_ref=ev.at[me], dst_ref=ev.at[me], send_sem=esend.at[j], recv_sem=erecv.at[j],
                device_id=peer, device_id_type=MESH))
            wait_ev.append(pltpu.make_async_remote_copy(
                src_ref=ev.at[me], dst_ref=ev.at[me ^ (4 * mx + 2 * my + mc)], send_sem=esend.at[j],
                recv_sem=erecv.at[j], device_id=peer, device_id_type=MESH))
    return out, start_ev, wait_ev


def chip_scatter_start(arrays, everyone, after, *, name):
    n = len(arrays)
    ne = 0 if everyone is None else 1
    lands = [pltpu.with_memory_space_constraint(lax.empty((3,) + a.shape[1:], a.dtype), pltpu.HBM) for a in arrays]

    def body(*refs):
        srcs, lands_ = refs[:n], refs[n:2 * n]
        ev = refs[2 * n] if ne else None
        sems = refs[2 * n + ne + 1 + 2 * n + ne:-1]
        send, recv = sems[0], sems[1]
        esend, erecv = (sems[2], sems[3]) if ne else (None, None)
        copies, start_ev, _ = _scatter_copies(srcs, lands_, ev, send, recv, esend, erecv)
        for cp in start_ev + copies:
            cp.start()
        refs[-1][...] = jnp.zeros_like(refs[-1])

    sem_shapes = [pltpu.SemaphoreType.DMA((3 * n,))] * 2 + [pltpu.SemaphoreType.DMA((7,))] * (2 * ne)
    bufs = list(arrays) + lands + ([everyone] if ne else [])
    outs = pl.pallas_call(
        body, name=name, in_specs=[HBM] * len(bufs) + [ANY],
        out_specs=[HBM] * len(bufs) + [SEM] * len(sem_shapes) + [pl.BlockSpec(memory_space=pltpu.VMEM)],
        out_shape=[pltpu.HBM(b.shape, b.dtype) for b in bufs] + sem_shapes + [jax.ShapeDtypeStruct((8, LANES), f32)],
        input_output_aliases={i: i for i in range(len(bufs))},
        compiler_params=pltpu.CompilerParams(has_side_effects=DATAFLOW))(
            *[pltpu.with_memory_space_constraint(b, pltpu.HBM) for b in bufs], after)
    return (n, ne, outs[:-1]), outs[-1]


def chip_scatter_wait(state, after, *, name):
    n, ne, held = state
    nb = 2 * n + ne
    bufs, sems = held[:nb], held[nb:]

    def body(*refs):
        srcs, lands_ = refs[:n], refs[n:2 * n]
        ev = refs[2 * n] if ne else None
        sems_ = refs[nb:nb + len(sems)]
        esend, erecv = (sems_[2], sems_[3]) if ne else (None, None)
        copies, _, wait_ev = _scatter_copies(srcs, lands_, ev, sems_[0], sems_[1], esend, erecv)
        for cp in wait_ev + copies:
            cp.wait_send()
            cp.wait_recv()

    outs = pl.pallas_call(
        body, name=name, in_specs=[HBM] * nb + [SEM] * len(sems) + [ANY], out_specs=[HBM] * nb,
        out_shape=[pltpu.HBM(b.shape, b.dtype) for b in bufs],
        input_output_aliases={i: i for i in range(nb)},
        compiler_params=pltpu.CompilerParams(has_side_effects=DATAFLOW))(*bufs, *sems, after)
    return outs[n:2 * n], (outs[2 * n] if ne else None)


def sibling_merge(bufs, *, name):
    n = len(bufs)

    def body(*refs):
        bufs_ = refs[:n]
        send, recv = refs[2 * n:]
        x, y, c = _place()

        def copy(u, h):
            return pltpu.make_async_remote_copy(
                src_ref=bufs_[u].at[h], dst_ref=bufs_[u].at[h], send_sem=send.at[u], recv_sem=recv.at[u],
                device_id=(x, y, 1 - c), device_id_type=MESH)

        for u in range(n):
            copy(u, c).start()
        for u in range(n):
            copy(u, 1 - c).wait_recv()
        for u in range(n):
            copy(u, c).wait_send()

    return pl.pallas_call(
        body, name=name, in_specs=[ANY] * n, out_specs=[ANY] * n,
        out_shape=[jax.ShapeDtypeStruct(b.shape, b.dtype) for b in bufs],
        input_output_aliases={i: i for i in range(n)},
        scratch_shapes=[pltpu.SemaphoreType.DMA((n,)), pltpu.SemaphoreType.DMA((n,))])(*bufs)


def sum_leading(a, *, name):
    n, r, c = a.shape

    def body(a_ref, o_ref):
        acc = a_ref[0]
        for i in range(1, n):
            acc = acc + a_ref[i]
        o_ref[...] = acc

    rb = r // 2 if r % 16 == 0 else r
    return pl.pallas_call(
        body, name=name, grid=(r // rb,), in_specs=[pl.BlockSpec((n, rb, c), lambda i: (0, i, 0))],
        out_specs=pl.BlockSpec((rb, c), lambda i: (i, 0)), out_shape=jax.ShapeDtypeStruct((r, c), f32),
        compiler_params=_params("parallel"))(a)


def _half_rows(shape):
    return shape[1] // 2 // 2


def rs_add_sibling(grads, recvd, ck_arr, *, name):
    n = len(grads)

    def body(ck_ref, *refs):
        s = pl.program_id(1)
        for u in range(n):
            g_ref, r_ref = refs[u], refs[n + u]
            qb_ref, own_ref = refs[2 * n + u], refs[3 * n + u]
            q = g_ref[0] + r_ref[0].astype(f32)
            qb_ref[0] = q.astype(bf16)

            @pl.when(s == ck_ref[1])
            def _(own_ref=own_ref, q=q):
                own_ref[...] = q

    in_specs = [pl.BlockSpec((1, _half_rows(g.shape), g.shape[2]), lambda r, s, ck: (s, ck[0] * 2 + r, 0)) for g in grads]
    in_specs += [pl.BlockSpec((1, _half_rows(g.shape), g.shape[2]), lambda r, s, ck: (s, r, 0)) for g in grads]
    out_specs = [pl.BlockSpec((1, _half_rows(g.shape), g.shape[2]), lambda r, s, ck: (s, r, 0)) for g in grads]
    out_specs += [pl.BlockSpec((_half_rows(g.shape), g.shape[2]), lambda r, s, ck: (r, 0)) for g in grads]
    outs = pl.pallas_call(
        body, name=name,
        grid_spec=pltpu.PrefetchScalarGridSpec(num_scalar_prefetch=1, grid=(2, N_SHARDS),
                                               in_specs=in_specs, out_specs=out_specs),
        out_shape=[jax.ShapeDtypeStruct((N_SHARDS, g.shape[1] // 2, g.shape[2]), bf16) for g in grads]
        + [jax.ShapeDtypeStruct((g.shape[1] // 2, g.shape[2]), f32) for g in grads],
        compiler_params=_params("parallel", "arbitrary"))(ck_arr, *grads, *recvd)
    return outs[:n], outs[n:]


def rs_sum_chips(owns, recvd, ck_arr, *, name):
    n = len(owns)

    def body(ck_ref, *refs):
        for u in range(n):
            own_ref, r_ref, o_ref = refs[u], refs[n + u], refs[2 * n + u]
            o_ref[0] = ((own_ref[...] + r_ref[0].astype(f32)) + r_ref[1].astype(f32)) + r_ref[2].astype(f32)

    in_specs = [pl.BlockSpec((o.shape[0] // 2, o.shape[1]), lambda r, ck: (r, 0)) for o in owns]
    in_specs += [pl.BlockSpec((3, o.shape[0] // 2, o.shape[1]), lambda r, ck: (0, r, 0)) for o in owns]
    out_specs = [pl.BlockSpec((1, o.shape[0] // 2, o.shape[1]), lambda r, ck: (ck[0], r, 0)) for o in owns]
    return pl.pallas_call(
        body, name=name,
        grid_spec=pltpu.PrefetchScalarGridSpec(num_scalar_prefetch=1, grid=(2,), in_specs=in_specs, out_specs=out_specs),
        out_shape=[jax.ShapeDtypeStruct((2,) + o.shape, f32) for o in owns],
        compiler_params=_params("parallel"))(ck_arr, *owns, *recvd)


SMALL = ("a_norm", "a_v_norm", "a_w_s", "a_b_s", "f_norm", "f_conv_w", "f_conv_b", "kv_norm", "k_norm",
         "b_norm", "b_q_norm", "b_sinks")
BIG = ("a_w_in", "a_w_out", "f_w_in", "f_w_out", "w_kv", "b_w_q", "b_w_o")
PACK_COLS = 1024
PACK_ROWS = 8 * N_STEPS


def _pack(parts, rows=PACK_ROWS):
    flat = jnp.concatenate([p.reshape(-1).astype(f32) for p in parts])
    pad = (-flat.shape[0]) % (rows * PACK_COLS)
    return jnp.pad(flat, (0, pad)).reshape(-1, PACK_COLS)


def _unpack(packed, shapes):
    flat = packed.reshape(-1)
    out, off = [], 0
    for s in shapes:
        size = math.prod(s)
        out.append(flat[off:off + size].reshape(s))
        off += size
    return out


def _behind(value, token):
    return lax.optimization_barrier((value, token))[0]


def _ffn_fwd(x, g, h, r, w_in4, conv_w, conv_b, f, tag):
    wg, wu = conv_w[:, :f], conv_w[:, f:]
    bg, bu = conv_b[None, :f], conv_b[None, f:]
    pg, pu, gate, up, a = ffn_in_fused(h, w_in4, wg, wu, bg, bu, name=f"ffn{tag}_in")
    return a, (x, g, h, r, pg, pu, gate, up, a, wg, wu)


def _ffn_bwd(dy, saved, w_in4, w_out, c_arr, tag, exchange=False):
    x, g, h, r, pg, pu, gate, up, a, wg, wu = saved
    f = w_out.shape[0]
    d_w_out = mm_tn(a, [dy], c_arr, name=f"ffn{tag}_dwout", n_s=w_out.shape[1], shard_rows=f // N_SHARDS, tki=f // 2)
    dpg, dpu, sg, su = ffn_gate_bwd(dy, w_out, pg, pu, gate, up, wg, wu, name=f"ffn{tag}_dgate")
    d_w_in = mm_tn(h, [dpg, dpu], c_arr, name=f"ffn{tag}_dwin", n_s=w_in4.shape[2], shard_rows=h.shape[1])
    state = None
    if exchange:
        state, token = sibling_start([d_w_in[1], d_w_out[1]], d_w_in[0], name=f"rs_sibling_start_ffn{tag}")
        g = _behind(g, token)
    dx, dg = mm_nt_rms_bwd([dpg, dpu], w_in4, x, r, g, dy, name=f"ffn{tag}_dh")
    d_conv_w = jnp.concatenate([sg[0:3], su[0:3]], axis=1)
    d_conv_b = jnp.concatenate([sg[3], su[3]], axis=0)
    return dx, dg, d_w_in, d_conv_w, d_conv_b, d_w_out, state


def _rs_front(pairs, sibling_state, after, c_arr, tag):
    units = [full.reshape(N_SHARDS, -1, full.shape[-1]) for full, _ in pairs]
    from_sib = sibling_wait(sibling_state, after, name=f"rs_sibling_wait{tag}")
    return rs_add_sibling(units, from_sib, c_arr, name=f"rs_add{tag}")


def _rs_back(own, from_chips, c_arr, tag):
    halves = rs_sum_chips(list(own), list(from_chips), c_arr, name=f"rs_sum{tag}")
    return [m.reshape(-1, m.shape[2]) for m in sibling_merge(list(halves), name=f"rs_merge{tag}")]


def kernel(x, a_norm, a_w_in, a_v_norm, a_w_s, a_b_s, a_w_out, f_norm, f_w_in, f_conv_w, f_conv_b, f_w_out, kv_norm, w_kv, k_norm, b_norm, b_w_q, b_q_norm, b_sinks, b_w_o, loss_target, m_a_norm, m_a_w_in, m_a_v_norm, m_a_w_s, m_a_b_s, m_a_w_out, m_f_norm, m_f_w_in, m_f_conv_w, m_f_conv_b, m_f_w_out, m_kv_norm, m_w_kv, m_k_norm, m_b_norm, m_b_w_q, m_b_q_norm, m_b_sinks, m_b_w_o, v_a_norm, v_a_w_in, v_a_v_norm, v_a_w_s, v_a_b_s, v_a_w_out, v_f_norm, v_f_w_in, v_f_conv_w, v_f_conv_b, v_f_w_out, v_kv_norm, v_w_kv, v_k_norm, v_b_norm, v_b_w_q, v_b_q_norm, v_b_sinks, v_b_w_o):
    args = dict(locals())
    weights = {n: args[n] for n in SMALL + BIG}
    moms = {n: args["m_" + n] for n in SMALL + BIG}
    vars_ = {n: args["v_" + n] for n in SMALL + BIG}
    t, d = x.shape[1], x.shape[2]
    xi, yi, ci = _place()
    chip = 2 * xi + yi

    big_local = [a_w_in[0], a_w_out[0], f_w_in, f_w_out, w_kv, b_w_q[0], b_w_o[0]]
    c_arr = jnp.stack([ci, chip]).astype(jnp.int32)
    k_arr = jnp.stack([chip]).astype(jnp.int32)
    b_ain, b_aout, b_fin0, b_fin1, b_fout0, b_fout1, b_kv, b_q, b_o = cast_into_slot(big_local, k_arr, name="cast_weights")
    small_cols = _pack([a_norm, a_v_norm, f_conv_w], rows=8)
    b_small = lax.dynamic_update_slice(jnp.zeros((N_SHARDS,) + small_cols.shape, f32), small_cols[None], (chip, 0, 0))
    g_small, w_a_in, g_a_w_out = gather_shards([b_small, b_ain, b_aout], name="gather_first", split=[False, True, True])
    later, send_sems, recv_sems, token = gather_start([b_fin0, b_fout0, b_kv, b_q, b_o, b_fin1, b_fout1],
                                                      [[0], [1], [2, 3, 4], [5, 6]], g_small, name="gather_start")
    ns_cols = a_norm.shape[1]
    nf_cols = f_conv_w.shape[2]
    parts = [_unpack(g_small[k], [a_norm.shape, a_v_norm.shape, f_conv_w.shape]) for k in range(N_SHARDS)]
    a_norm_f = jnp.concatenate([p[0] for p in parts], axis=1) + token[0, 0]
    a_v_norm_f = jnp.concatenate([p[1] for p in parts], axis=1)
    conv_w_f = jnp.concatenate([p[2] for p in parts], axis=2)

    x0 = x[0]
    tril = jnp.tril(jnp.ones((CHUNK, CHUNK), dtype=bool))
    wc = jnp.where(tril[None], a_w_s[0], 0.0).astype(bf16)
    bt = a_b_s[0].T
    kg2 = jnp.tile(k_norm, 2)[None]
    qg2 = jnp.tile(b_q_norm[0], 2)[None]

    (h_a,), r_a = rms_fwd(x0, [a_norm_f], name="a_norm")
    zuv = mm_nn(h_a, w_a_in, name="a_in")
    y_a = sgu_gate_fwd(zuv, a_v_norm_f, wc, bt, name="a_gate")
    w_a_out = g_a_w_out.reshape(1, -1, d)
    f = f_w_out.shape[1] * N_SHARDS
    fwd0, tok0 = forward_start(gather_wait(later[0:1], send_sems[0], recv_sems[0], y_a, name="gather_wait_0"), y_a,
                               name="gather_forward_start_0")
    x1, (h_f0,), r_f0 = mm_residual(y_a, w_a_out[0], x0, name="a_out", gains=[_behind(f_norm[0:1], tok0)])
    (g_fin0,) = forward_wait(fwd0, x1, name="gather_forward_wait_0")
    w_f_in = [g_fin0, None]
    a0, ffn0 = _ffn_fwd(x1, f_norm[0:1], h_f0, r_f0, w_f_in[0], conv_w_f[0], f_conv_b[0], f, "0")
    (g_fout0,) = forward_halves(gather_wait(later[1:2], send_sems[1], recv_sems[1], a0, name="gather_wait_1"),
                                name="gather_forward_1")
    w_f_out = [g_fout0.reshape(-1, d), None]
    fwd1, tok1 = forward_start(gather_wait(later[2:5], send_sems[2], recv_sems[2], g_fout0, name="gather_wait_1b"), a0,
                               name="gather_forward_start_1b")
    x2, (h_k, h_q), r_b = mm_residual(a0, w_f_out[0], x1, name="ffn0_out", gains=[_behind(kv_norm[None], tok1), b_norm])
    g_w_kv, g_b_w_q, g_b_w_o = forward_wait(fwd1, x2, name="gather_forward_wait_1b")
    w_kv_f = g_w_kv.reshape(1, d, -1)
    w_q_f = g_b_w_q.reshape(1, d, -1)
    w_o_f = g_b_w_o.reshape(1, -1, d)
    kv, k2, v2 = kv_proj_post(h_k, w_kv_f[0], kg2, name="kv_proj")
    qp, qn = q_proj_norm(h_q, w_q_f[0], qg2, name="q_proj", scale=HEAD_DIM ** -0.5)
    fwd2, tok2 = forward_start(gather_wait(later[5:7], send_sems[3], recv_sems[3], qn, name="gather_wait_2"), qn,
                               name="gather_forward_start_2")
    o = attn_fwd(qn, k2, v2, _behind(b_sinks[0], tok2), name="attn")
    x3, (h_f1,), r_f1 = mm_residual(o, w_o_f[0], x2, name="o_proj", gains=[f_norm[1:2]])
    g_fin1, g_fout1 = forward_wait(fwd2, x3, name="gather_forward_wait_2")
    w_f_in[1] = g_fin1
    w_f_out[1] = g_fout1.reshape(-1, d)
    a1, ffn1 = _ffn_fwd(x3, f_norm[1:2], h_f1, r_f1, w_f_in[1], conv_w_f[1], f_conv_b[1], f, "1")
    dx4, sq = mm_residual(a1, w_f_out[1], x3, name="ffn1_out", target=loss_target[0])
    loss_part = (0.5 * jnp.sum(sq) / d).reshape(1)

    proj_rows = d // N_SHARDS
    dx3, d_fn1, d_fwin1, d_cw1, d_cb1, d_fwout1, _ = _ffn_bwd(dx4, ffn1, w_f_in[1], w_f_out[1], c_arr, "1")
    do = mm_nt([dx3], w_o_f, name="o_proj_dx")
    d_w_o = mm_tn(o, [dx3], c_arr, name="o_proj_dw", n_s=d, shard_rows=o.shape[1] // N_SHARDS)
    dqn, dk2, dv2, dsink = attn_bwd(qn, k2, v2, do, b_sinks[0], name="attn_bwd")
    dqp, dqg = q_norm_bwd(dqn, qp, qg2, name="q_norm_bwd", scale=HEAD_DIM ** -0.5)
    dkv, dkg = kv_post_bwd(dk2, dv2, kv, kg2, name="kv_post_bwd")
    d_w_q = mm_tn(h_q, [dqp], c_arr, name="q_proj_dw", n_s=w_q_f.shape[2], shard_rows=proj_rows)
    d_w_kv = mm_tn(h_k, [dkv], c_arr, name="kv_proj_dw", n_s=w_kv_f.shape[2], shard_rows=proj_rows)
    group1 = [d_fwin1, d_fwout1, d_w_kv, d_w_q, d_w_o]
    sib1, token_s1 = sibling_start([half for _, half in group1], d_w_kv[0], name="rs_sibling_start1")
    dh_k = mm_nt([dkv], w_kv_f, name="kv_proj_dx")
    dx2, d_bn, d_kvn = mm_nt_rms_bwd([dqp], w_q_f, x2, r_b, _behind(b_norm, token_s1), dx3, name="q_proj_dx",
                                     extra=(dh_k, kv_norm[None]))
    chip_bf1, own1 = _rs_front(group1, sib1, dx2, c_arr, "1")
    scatter1, token1 = chip_scatter_start(list(chip_bf1), None, dx2, name="rs_chips_start1")
    ffn0 = ffn0[:9] + (_behind(ffn0[9], token1),) + ffn0[10:]
    dx1, d_fn0, d_fwin0, d_cw0, d_cb0, d_fwout0, sib2 = _ffn_bwd(dx2, ffn0, w_f_in[0], w_f_out[0], c_arr, "0", exchange=True)
    chip_bf2, own2 = _rs_front([d_fwin0, d_fwout0], sib2, dx1, c_arr, "2")
    scatter2, token2 = chip_scatter_start(list(chip_bf2), None, dx1, name="rs_chips_start2")
    a_v_norm_f = _behind(a_v_norm_f, token2)
    dy_a = mm_nt([dx1], w_a_out, name="a_out_dx")
    d_w_aout = mm_tn(y_a, [dx1], c_arr, name="a_out_dw", n_s=d, shard_rows=y_a.shape[1] // N_SHARDS)
    dzu, dzv, d_avn, d_ws, d_bt = sgu_gate_bwd(zuv, dy_a, a_v_norm_f, wc, bt, name="a_gate_bwd")
    d_w_ain = mm_tn(h_a, [dzu, dzv], c_arr, name="a_in_dw", n_s=w_a_in.shape[2], shard_rows=d)
    sib3, token_s3 = sibling_start([d_w_ain[1], d_w_aout[1]], d_w_ain[0], name="rs_sibling_start3")
    dx0, d_an = mm_nt_rms_bwd([dzu, dzv], w_a_in, x0, r_a, _behind(a_norm_f, token_s3), dx1, name="a_in_dx")
    grad_x = dx0[None]

    chip_bf3, own3 = _rs_front([d_w_ain, d_w_aout], sib3, dx0, c_arr, "3")
    d_fn = jnp.concatenate([d_fn0, d_fn1], axis=0)
    d_cw = jnp.stack([d_cw0, d_cw1])
    d_cb = jnp.stack([d_cb0, d_cb1])
    d_kg = (dkg[0, :HEAD_DIM] + dkg[0, HEAD_DIM:])
    d_qg = (dqg[0, :HEAD_DIM] + dqg[0, HEAD_DIM:])[None]
    small_full = [d_an, d_avn, d_ws[None], d_bt.T[None], d_fn, d_cw, d_cb, d_kvn[0], d_kg, d_bn, d_qg,
                  dsink[:, :N_Q_HEADS], loss_part]
    packed = _pack(small_full)
    me = 4 * xi + 2 * yi + ci
    everyone = lax.dynamic_update_slice(lax.empty((N_DEV,) + packed.shape, f32), packed[None], (me, 0, 0))
    scatter3, token3 = chip_scatter_start(list(chip_bf3), everyone, own3[0], name="rs_chips_start3")
    from_chips1, _ = chip_scatter_wait(scatter1, token3, name="rs_chips_wait1")
    from_chips2, _ = chip_scatter_wait(scatter2, from_chips1[0], name="rs_chips_wait2")
    fin1, fout1, gkv, gq, go, fin0, fout0 = _rs_back(list(own1) + list(own2), list(from_chips1) + list(from_chips2),
                                                     c_arr, "12")
    late = ("f_w_in", "f_w_out", "w_kv", "b_w_q", "b_w_o")
    res_late = adamw([weights[n] for n in late], [[fin0, fin1], [fout0, fout1], [gkv], [gq], [go]],
                     [moms[n] for n in late], [vars_[n] for n in late], name="adamw_late")
    from_chips3, from_all = chip_scatter_wait(scatter3, res_late[1][2], name="rs_chips_wait3")
    ain, aout = _rs_back(own3, from_chips3, c_arr, "3")
    first = ("a_w_in", "a_w_out")
    res_first = adamw([weights[n] for n in first], [[ain], [aout]], [moms[n] for n in first],
                      [vars_[n] for n in first], name="adamw_first")
    big = {n: tuple(r[i] for r in res_late) for i, n in enumerate(late)}
    big.update({n: tuple(r[i] for r in res_first) for i, n in enumerate(first)})

    full_shapes = [g.shape for g in small_full]
    small_g = _unpack(sum_leading(from_all, name="small_sum"), full_shapes)
    loss = small_g.pop()[0]
    small_g[0] = lax.dynamic_slice_in_dim(small_g[0], chip * ns_cols, ns_cols, axis=1)
    small_g[1] = lax.dynamic_slice_in_dim(small_g[1], chip * ns_cols, ns_cols, axis=1)
    small_g[5] = lax.dynamic_slice_in_dim(small_g[5], chip * nf_cols, nf_cols, axis=2)
    small_shapes = [weights[n].shape for n in SMALL]
    small_g = [g.reshape(s) for g, s in zip(small_g, small_shapes)]
    flat2 = [(math.prod(s[:-1]), s[-1]) for s in small_shapes]
    small_d, small_m, small_v = adamw_small(
        *[[a.reshape(s2) for a, s2 in zip(group, flat2)]
          for group in ([weights[n] for n in SMALL], small_g, [moms[n] for n in SMALL], [vars_[n] for n in SMALL])],
        name="adamw_small")
    small_d, small_m, small_v = ([a.reshape(s) for a, s in zip(group, small_shapes)]
                                 for group in (small_d, small_m, small_v))

    out = {}
    for i, n in enumerate(SMALL):
        out[n] = (small_g[i], small_d[i], small_m[i], small_v[i])
    out.update(big)
    order = ["a_norm", "a_w_in", "a_v_norm", "a_w_s", "a_b_s", "a_w_out", "f_norm", "f_w_in", "f_conv_w", "f_conv_b",
             "f_w_out", "kv_norm", "w_kv", "k_norm", "b_norm", "b_w_q", "b_q_norm", "b_sinks", "b_w_o"]
    return (loss, grad_x, *[out[n][0] for n in order], *[out[n][1] for n in order],
            *[out[n][2] for n in order], *[out[n][3] for n in order])
```

```python
import functools
import math

import jax
import jax.numpy as jnp
from jax import lax
from jax.experimental import pallas as pl
from jax.experimental.pallas import tpu as pltpu

f32 = jnp.float32
bf16 = jnp.bfloat16
MESH = pl.DeviceIdType.MESH
ANY = pl.BlockSpec(memory_space=pl.ANY)

EPS = 1e-6
LANES = 128
CHUNK = 128
HEAD_DIM = 64
N_Q_HEADS = 16
N_KV_HEADS = 4
Q_PER_KV = N_Q_HEADS // N_KV_HEADS
N_SHARDS = 4
N_DEV = 8

ADAM_LR = 0.001
ADAM_B1 = 0.9
ADAM_B2 = 0.999
ADAM_EPS = 1e-08
ADAM_WD = 0.01
ADAM_STEP = 10
ADAM_C1 = 1.0 - ADAM_B1 ** ADAM_STEP
ADAM_C2 = 1.0 - ADAM_B2 ** ADAM_STEP

_INV_SQRT2 = 1.0 / math.sqrt(2.0)
_INV_SQRT2PI = 1.0 / math.sqrt(2.0 * math.pi)


def _params(*sem):
    return pltpu.CompilerParams(dimension_semantics=sem)


def _gelu(z):
    return 0.5 * z * (1.0 + lax.erf(z * _INV_SQRT2))


def _gelu_and_grad(z):
    cdf = 0.5 * (1.0 + lax.erf(z * _INV_SQRT2))
    return z * cdf, cdf + z * jnp.exp(-0.5 * z * z) * _INV_SQRT2PI


def _dot(a, b):
    return jnp.dot(a, b, preferred_element_type=f32)


def _dot_nt(a, b):
    return lax.dot_general(a, b, (((1,), (1,)), ((), ())), preferred_element_type=f32)


def _dot_tn(a, b):
    return lax.dot_general(a, b, (((0,), (0,)), ((), ())), preferred_element_type=f32)


def _dot_split(a, b):
    hi = a.astype(bf16)
    lo = (a - hi.astype(f32)).astype(bf16)
    return _dot(hi, b) + _dot(lo, b)


VMEM_TILE_BUDGET = 52 * 1024 * 1024
MAX_ROW_TILE = 2048


def _row_tile(m, fixed_bytes, row_bytes):
    tm = min(m, MAX_ROW_TILE)
    while tm > 256 and 2 * (fixed_bytes + tm * row_bytes) > VMEM_TILE_BUDGET:
        tm //= 2
    return tm


def _isz(a):
    return jnp.dtype(a.dtype).itemsize


def mm_nn(a, w3, *, name, s0=0, ns=None, add=None, out_dtype=f32):
    m, k = a.shape
    s_all, _, n_s = w3.shape
    ns = s_all if ns is None else ns
    tm = _row_tile(m, k * n_s * 2, k * _isz(a) + n_s * jnp.dtype(out_dtype).itemsize + (0 if add is None else n_s * 4))

    def body(*refs):
        if add is None:
            a_ref, w_ref, o_ref = refs
            acc = _dot(a_ref[...].astype(bf16), w_ref[0])
        else:
            a_ref, w_ref, add_ref, o_ref = refs
            acc = _dot(a_ref[...].astype(bf16), w_ref[0]) + add_ref[...]
        o_ref[...] = acc.astype(out_dtype)

    in_specs = [pl.BlockSpec((tm, k), lambda j, i: (i, 0)),
                pl.BlockSpec((1, k, n_s), lambda j, i: (s0 + j, 0, 0))]
    args = [a, w3]
    if add is not None:
        in_specs.append(pl.BlockSpec((tm, n_s), lambda j, i: (i, j)))
        args.append(add)
    return pl.pallas_call(
        body, name=name, grid=(ns, m // tm), in_specs=in_specs,
        out_specs=pl.BlockSpec((tm, n_s), lambda j, i: (i, j)),
        out_shape=jax.ShapeDtypeStruct((m, ns * n_s), out_dtype),
        compiler_params=_params("parallel", "parallel"))(*args)


def mm_nt(a_list, w3, *, name, tko=None, add=None, out_dtype=f32):
    s_all, k_out, n_s = w3.shape
    m = a_list[0].shape[0]
    na = len(a_list)
    spa = s_all // na
    tko = k_out if tko is None else tko
    tm = _row_tile(m, tko * n_s * 2, na * n_s * _isz(a_list[0]) + tko * 4 * (1 if add is None else 2))

    def body(*refs):
        a_refs = refs[:na]
        w_ref = refs[na]
        o_ref = refs[-1]
        s = pl.program_id(2)

        @pl.when(s == 0)
        def _():
            if add is None:
                o_ref[...] = jnp.zeros_like(o_ref)
            else:
                o_ref[...] = refs[na + 1][...]

        for idx in range(na):
            @pl.when(s // spa == idx)
            def _(idx=idx):
                o_ref[...] += _dot_nt(a_refs[idx][...].astype(bf16), w_ref[0])

    def a_map(idx):
        return lambda ko, i, s: (i, jnp.clip(s - idx * spa, 0, spa - 1))

    in_specs = [pl.BlockSpec((tm, n_s), a_map(idx)) for idx in range(na)]
    in_specs.append(pl.BlockSpec((1, tko, n_s), lambda ko, i, s: (s, ko, 0)))
    args = list(a_list) + [w3]
    if add is not None:
        in_specs.append(pl.BlockSpec((tm, tko), lambda ko, i, s: (i, ko)))
        args.append(add)
    return pl.pallas_call(
        body, name=name, grid=(k_out // tko, m // tm, s_all), in_specs=in_specs,
        out_specs=pl.BlockSpec((tm, tko), lambda ko, i, s: (i, ko)),
        out_shape=jax.ShapeDtypeStruct((m, k_out), out_dtype),
        compiler_params=_params("parallel", "parallel", "arbitrary"))(*args)


def mm_tn(a, b_list, c_arr, *, name, n_s, shard_rows, tki=None):
    m, k_in = a.shape
    na = len(b_list)
    s_all = sum(b.shape[1] for b in b_list) // n_s
    spa = s_all // na
    tki = k_in if tki is None else tki
    tm = _row_tile(m, tki * n_s * 4, tki * _isz(a) + na * n_s * _isz(b_list[0]))

    nsteps = m // tm
    per_blk = tki // shard_rows
    half = shard_rows // 2

    def body(c_ref, *refs):
        a_ref = refs[0]
        b_refs = refs[1:1 + na]
        o_ref, ob_ref = refs[-2], refs[-1]
        s = pl.program_id(0)
        r = pl.program_id(2)

        @pl.when(r == 0)
        def _():
            o_ref[...] = jnp.zeros_like(o_ref)

        for idx in range(na):
            @pl.when(s // spa == idx)
            def _(idx=idx):
                o_ref[0] += _dot_tn(a_ref[...].astype(bf16), b_refs[idx][...].astype(bf16))

        @pl.when(r == nsteps - 1)
        def _():
            for q in range(per_blk):
                start = pl.multiple_of(q * shard_rows + (1 - c_ref[0]) * half, 16)
                ob_ref[q] = o_ref[0, pl.ds(start, half), :].astype(bf16)

    def b_map(idx):
        def index(s, ki, r, c_ref):
            active = (s // spa) == idx
            return (jnp.where(active, r, 0), jnp.clip(s - idx * spa, 0, spa - 1))
        return index

    in_specs = [pl.BlockSpec((tm, tki), lambda s, ki, r, c_ref: (r, ki))]
    in_specs += [pl.BlockSpec((tm, n_s), b_map(idx)) for idx in range(na)]
    n_blk = k_in // tki
    return pl.pallas_call(
        body, name=name,
        grid_spec=pltpu.PrefetchScalarGridSpec(
            num_scalar_prefetch=1, grid=(s_all, n_blk, nsteps), in_specs=in_specs,
            out_specs=[pl.BlockSpec((1, tki, n_s), lambda s, ki, r, c_ref: (s, ki, 0)),
                       pl.BlockSpec((per_blk, half, n_s), lambda s, ki, r, c_ref: (s * n_blk + ki, 0, 0))]),
        out_shape=[jax.ShapeDtypeStruct((s_all, k_in, n_s), f32),
                   jax.ShapeDtypeStruct((s_all * k_in // shard_rows, half, n_s), bf16)],
        compiler_params=_params("parallel", "parallel", "arbitrary"))(c_arr, a, *b_list)


def mm_nt_rms_bwd(a_list, w3, x, r, g, dx_in, *, name, extra=None):
    s_all, d, n_s = w3.shape
    m = a_list[0].shape[0]
    na = len(a_list)
    spa = s_all // na
    ne = 0 if extra is None else 1
    tm = _row_tile(m, d * n_s * 2, na * n_s * _isz(a_list[0]) + d * 4 * (4 + ne))

    def body(*refs):
        a_refs, w_ref = refs[:na], refs[na]
        x_ref, r_ref, g_ref, dxin_ref = refs[na + 1:na + 5]
        dh2_ref, g2_ref = (refs[na + 5], refs[na + 6]) if ne else (None, None)
        outs = refs[na + 5 + 2 * ne:]
        dx_ref, dg_ref = outs[0], outs[1]
        dg2_ref = outs[2] if ne else None
        acc_ref = outs[-1]
        i, s = pl.program_id(0), pl.program_id(1)

        @pl.when(s == 0)
        def _():
            acc_ref[...] = jnp.zeros_like(acc_ref)

        for idx in range(na):
            @pl.when(s // spa == idx)
            def _(idx=idx):
                acc_ref[...] += _dot_nt(a_refs[idx][...].astype(bf16), w_ref[0])

        @pl.when(s == s_all - 1)
        def _():
            rv = r_ref[...]
            xh = x_ref[...] * rv
            total = dxin_ref[...]
            pairs = [(acc_ref[...], g_ref, dg_ref)] + ([(dh2_ref[...], g2_ref, dg2_ref)] if ne else [])
            for dh, gain_ref, dgain_ref in pairs:
                part = jnp.sum(dh * xh, axis=0, keepdims=True)

                @pl.when(i == 0)
                def _(dgain_ref=dgain_ref, part=part):
                    dgain_ref[...] = part

                @pl.when(i > 0)
                def _(dgain_ref=dgain_ref, part=part):
                    dgain_ref[...] += part

                tg = dh * gain_ref[...]
                total = total + rv * (tg - xh * jnp.mean(tg * xh, axis=1, keepdims=True))
            dx_ref[...] = total

    def a_map(idx):
        return lambda i, s: (i, jnp.clip(s - idx * spa, 0, spa - 1))

    row = pl.BlockSpec((tm, d), lambda i, s: (i, 0))
    vec = pl.BlockSpec((1, d), lambda i, s: (0, 0))
    in_specs = [pl.BlockSpec((tm, n_s), a_map(idx)) for idx in range(na)]
    in_specs += [pl.BlockSpec((1, d, n_s), lambda i, s: (s, 0, 0)), row, pl.BlockSpec((tm, 1), lambda i, s: (i, 0)), vec, row]
    args = list(a_list) + [w3, x, r, g, dx_in]
    if ne:
        in_specs += [row, vec]
        args += list(extra)
    outs = pl.pallas_call(
        body, name=name, grid=(m // tm, s_all), in_specs=in_specs, out_specs=[row] + [vec] * (1 + ne),
        out_shape=[jax.ShapeDtypeStruct((m, d), f32)] + [jax.ShapeDtypeStruct((1, d), f32)] * (1 + ne),
        scratch_shapes=[pltpu.VMEM((tm, d), f32)],
        compiler_params=_params("arbitrary", "arbitrary"))(*args)
    return outs


def mm_residual(a, w, x, *, name, gains=(), target=None):
    m, k = a.shape
    d = w.shape[1]
    ng = len(gains)
    tm = _row_tile(m, k * d * 2, k * _isz(a) + d * 4 * 3 + ng * d * 2)

    def body(*refs):
        a_ref, w_ref, x_ref = refs[:3]
        y = _dot(a_ref[...].astype(bf16), w_ref[...]) + x_ref[...]
        if target is None:
            g_refs = refs[3:3 + ng]
            y_ref = refs[3 + ng]
            h_refs = refs[4 + ng:4 + 2 * ng]
            r_ref = refs[-1]
            y_ref[...] = y
            r = lax.rsqrt(jnp.mean(y * y, axis=1, keepdims=True) + EPS)
            yh = y * r
            for g_ref, h_ref in zip(g_refs, h_refs):
                h_ref[...] = (yh * g_ref[...]).astype(bf16)
            r_ref[...] = r
        else:
            t_ref, dy_ref, s_ref = refs[3:]
            i = pl.program_id(0)
            e = y - t_ref[...]
            dy_ref[...] = e * (1.0 / d)
            part = jnp.sum(e * e, axis=0, keepdims=True)

            @pl.when(i == 0)
            def _():
                s_ref[...] = part

            @pl.when(i > 0)
            def _():
                s_ref[...] += part

    row = pl.BlockSpec((tm, d), lambda i: (i, 0))
    vec = pl.BlockSpec((1, d), lambda i: (0, 0))
    in_specs = [pl.BlockSpec((tm, k), lambda i: (i, 0)), pl.BlockSpec((k, d), lambda i: (0, 0)), row]
    if target is None:
        outs = pl.pallas_call(
            body, name=name, grid=(m // tm,), in_specs=in_specs + [vec] * ng,
            out_specs=[row] * (1 + ng) + [pl.BlockSpec((tm, 1), lambda i: (i, 0))],
            out_shape=[jax.ShapeDtypeStruct((m, d), f32)] + [jax.ShapeDtypeStruct((m, d), bf16)] * ng
            + [jax.ShapeDtypeStruct((m, 1), f32)],
            compiler_params=_params("parallel"))(a, w, x, *gains)
        return outs[0], outs[1:1 + ng], outs[-1]
    return pl.pallas_call(
        body, name=name, grid=(m // tm,), in_specs=in_specs + [row], out_specs=[row, vec],
        out_shape=[jax.ShapeDtypeStruct((m, d), f32), jax.ShapeDtypeStruct((1, d), f32)],
        compiler_params=_params("arbitrary"))(a, w, x, target)


def rms_fwd(x, gains, *, name, tr=512):
    t, d = x.shape
    tr = min(tr, t)
    ng = len(gains)

    def body(*refs):
        x_ref = refs[0]
        g_refs = refs[1:1 + ng]
        h_refs = refs[1 + ng:1 + 2 * ng]
        r_ref = refs[-1]
        xv = x_ref[...]
        r = lax.rsqrt(jnp.mean(xv * xv, axis=1, keepdims=True) + EPS)
        xh = xv * r
        for g_ref, h_ref in zip(g_refs, h_refs):
            h_ref[...] = (xh * g_ref[...]).astype(bf16)
        r_ref[...] = r

    row = pl.BlockSpec((tr, d), lambda i: (i, 0))
    vec = pl.BlockSpec((1, d), lambda i: (0, 0))
    outs = pl.pallas_call(
        body, name=name, grid=(t // tr,), in_specs=[row] + [vec] * ng,
        out_specs=[row] * ng + [pl.BlockSpec((tr, 1), lambda i: (i, 0))],
        out_shape=[jax.ShapeDtypeStruct((t, d), bf16)] * ng + [jax.ShapeDtypeStruct((t, 1), f32)],
        compiler_params=_params("parallel"))(x, *gains)
    return outs[:ng], outs[ng]


def sgu_gate_fwd(zuv, gv, wc, bt, *, name, tr=512):
    t, w = zuv.shape[0], zuv.shape[1] // 2
    tr = min(tr, t)
    groups = w // LANES

    def body(zu_ref, zv_ref, gv_ref, wc_ref, bt_ref, y_ref):
        vp = _gelu(zv_ref[...])
        rv = lax.rsqrt(jnp.mean(vp * vp, axis=1, keepdims=True) + EPS)
        vb = (vp * rv * gv_ref[...]).astype(bf16)
        for c in range(tr // CHUNK):
            rows = slice(c * CHUNK, (c + 1) * CHUNK)
            for g in range(groups):
                cols = slice(g * LANES, (g + 1) * LANES)
                sv = _dot(wc_ref[g], vb[rows, cols]) + bt_ref[:, g:g + 1]
                y_ref[rows, cols] = (_gelu(zu_ref[rows, cols]) * sv).astype(bf16)

    row = pl.BlockSpec((tr, w), lambda i: (i, 0))
    return pl.pallas_call(
        body, name=name, grid=(t // tr,),
        in_specs=[row, pl.BlockSpec((tr, w), lambda i: (i, 1)), pl.BlockSpec((1, w), lambda i: (0, 0)),
                  pl.BlockSpec((groups, CHUNK, CHUNK), lambda i: (0, 0, 0)),
                  pl.BlockSpec((CHUNK, groups), lambda i: (0, 0))],
        out_specs=row, out_shape=jax.ShapeDtypeStruct((t, w), bf16),
        compiler_params=_params("parallel"))(zuv, zuv, gv, wc, bt)


def sgu_gate_bwd(zuv, dy, gv, wc, bt, *, name, tr=512):
    t, w = zuv.shape[0], zuv.shape[1] // 2
    tr = min(tr, t)
    groups = w // LANES
    nsteps = t // tr

    def body(zu_ref, zv_ref, dy_ref, gv_ref, wc_ref, bt_ref,
             dzu_ref, dzv_ref, dgv_ref, dws_ref, dbt_ref, dv_ref, bacc_ref):
        i = pl.program_id(0)

        @pl.when(i == 0)
        def _():
            dgv_ref[...] = jnp.zeros_like(dgv_ref)
            dws_ref[...] = jnp.zeros_like(dws_ref)
            bacc_ref[...] = jnp.zeros_like(bacc_ref)

        vp, vp_grad = _gelu_and_grad(zv_ref[...])
        rv = lax.rsqrt(jnp.mean(vp * vp, axis=1, keepdims=True) + EPS)
        vhat = vp * rv
        vb = (vhat * gv_ref[...]).astype(bf16)
        for c in range(tr // CHUNK):
            rows = slice(c * CHUNK, (c + 1) * CHUNK)
            for g in range(groups):
                cols = slice(g * LANES, (g + 1) * LANES)
                vblk = vb[rows, cols]
                sv = _dot(wc_ref[g], vblk) + bt_ref[:, g:g + 1]
                zub = zu_ref[rows, cols]
                dyb = dy_ref[rows, cols]
                ub, ub_grad = _gelu_and_grad(zub)
                dzu_ref[rows, cols] = (dyb * sv * ub_grad).astype(bf16)
                dsv = dyb * ub
                bacc_ref[:, cols] += dsv
                dsvb = dsv.astype(bf16)
                dv_ref[rows, cols] = _dot_tn(wc_ref[g], dsvb)
                dws_ref[g] += _dot_nt(dsvb, vblk)
        dv = dv_ref[...]
        dgv_ref[...] += jnp.sum(dv * vhat, axis=0, keepdims=True)
        tg = dv * gv_ref[...]
        dvp = rv * (tg - vhat * jnp.mean(tg * vhat, axis=1, keepdims=True))
        dzv_ref[...] = (dvp * vp_grad).astype(bf16)

        @pl.when(i == nsteps - 1)
        def _():
            tt = lax.broadcasted_iota(jnp.int32, (CHUNK, CHUNK), 0)
            ss = lax.broadcasted_iota(jnp.int32, (CHUNK, CHUNK), 1)
            for g in range(groups):
                dws_ref[g] = jnp.where(ss <= tt, dws_ref[g], 0.0)
                dbt_ref[:, g:g + 1] = jnp.sum(bacc_ref[:, g * LANES:(g + 1) * LANES], axis=1, keepdims=True)

    row = pl.BlockSpec((tr, w), lambda i: (i, 0))
    full3 = pl.BlockSpec((groups, CHUNK, CHUNK), lambda i: (0, 0, 0))
    return pl.pallas_call(
        body, name=name, grid=(nsteps,),
        in_specs=[row, pl.BlockSpec((tr, w), lambda i: (i, 1)), row, pl.BlockSpec((1, w), lambda i: (0, 0)), full3,
                  pl.BlockSpec((CHUNK, groups), lambda i: (0, 0))],
        out_specs=[row, row, pl.BlockSpec((1, w), lambda i: (0, 0)), full3,
                   pl.BlockSpec((CHUNK, groups), lambda i: (0, 0))],
        out_shape=[jax.ShapeDtypeStruct((t, w), bf16), jax.ShapeDtypeStruct((t, w), bf16),
                   jax.ShapeDtypeStruct((1, w), f32), jax.ShapeDtypeStruct((groups, CHUNK, CHUNK), f32),
                   jax.ShapeDtypeStruct((CHUNK, groups), f32)],
        scratch_shapes=[pltpu.VMEM((tr, w), f32), pltpu.VMEM((CHUNK, w), f32)],
        compiler_params=_params("arbitrary"))(zuv, zuv, dy, gv, wc, bt)


HALO = 8


def _shift_down(v, halo, k, first):
    r = pltpu.roll(v, k, 0)
    hh = jnp.where(first, 0.0, pltpu.roll(halo, k, 0))
    rid = lax.broadcasted_iota(jnp.int32, (HALO, v.shape[1]), 0)
    head = jnp.where(rid < k, hh, r[0:HALO])
    if v.shape[0] == HALO:
        return head
    return jnp.concatenate([head, r[HALO:]], axis=0)


def _shift_up(v, halo, k, last):
    n = v.shape[0]
    r = pltpu.roll(v, n - k, 0)
    hh = jnp.where(last, 0.0, pltpu.roll(halo, HALO - k, 0))
    rid = lax.broadcasted_iota(jnp.int32, (HALO, v.shape[1]), 0)
    tail = jnp.where(rid >= HALO - k, hh, r[n - HALO:])
    return jnp.concatenate([r[:n - HALO], tail], axis=0)


def _conv(p, halo, w_ref, b_ref, first):
    return (w_ref[2:3, :] * p + w_ref[1:2, :] * _shift_down(p, halo, 1, first)
            + w_ref[0:1, :] * _shift_down(p, halo, 2, first) + b_ref[...])


BF16_ROWS = 16


def ffn_in_fused(h, w_in4, wg, wu, bg, bu, *, name):
    t, k = h.shape
    s_all, _, n_s = w_in4.shape
    half = s_all // 2
    tm = _row_tile(t, 2 * k * n_s * 2, k * 2 + 4 * n_s * 4 + n_s * 2)

    def body(h_ref, hh_ref, wg_ref, wu_ref, cg_ref, cu_ref, bg_ref, bu_ref, pg_ref, pu_ref, gate_ref, up_ref, a_ref):
        first = pl.program_id(1) == 0
        hv, hh = h_ref[...], hh_ref[...]
        outs = []
        for w_ref, c_ref, b_ref, p_ref, o_ref in ((wg_ref, cg_ref, bg_ref, pg_ref, gate_ref),
                                                  (wu_ref, cu_ref, bu_ref, pu_ref, up_ref)):
            p = _dot(hv, w_ref[0])
            p_ref[...] = p
            hu = _conv(p, _dot(hh, w_ref[0])[BF16_ROWS - HALO:], c_ref, b_ref, first)
            o_ref[...] = hu
            outs.append(hu)
        gate, up = outs
        a_ref[...] = (gate * jax.nn.sigmoid(gate) * up).astype(bf16)

    tile = pl.BlockSpec((tm, n_s), lambda j, i: (i, j))
    cw = pl.BlockSpec((3, n_s), lambda j, i: (0, j))
    cb = pl.BlockSpec((1, n_s), lambda j, i: (0, j))
    f = half * n_s
    return pl.pallas_call(
        body, name=name, grid=(half, t // tm),
        in_specs=[pl.BlockSpec((tm, k), lambda j, i: (i, 0)),
                  pl.BlockSpec((BF16_ROWS, k), lambda j, i: (jnp.maximum(i * (tm // BF16_ROWS) - 1, 0), 0)),
                  pl.BlockSpec((1, k, n_s), lambda j, i: (j, 0, 0)),
                  pl.BlockSpec((1, k, n_s), lambda j, i: (j + half, 0, 0)), cw, cw, cb, cb],
        out_specs=[tile] * 5,
        out_shape=[jax.ShapeDtypeStruct((t, f), f32)] * 4 + [jax.ShapeDtypeStruct((t, f), bf16)],
        compiler_params=_params("parallel", "parallel"))(h, h, w_in4, w_in4, wg, wu, bg, bu)


def _gate_grads(gate, up, dav):
    sg = jax.nn.sigmoid(gate)
    return dav * up * (sg * (1.0 + gate * (1.0 - sg))), dav * gate * sg


GATE_BWD_ROWS = 512


def ffn_gate_bwd(dy, w_out, pg, pu, gate, up, wg, wu, *, name):
    t, f = pg.shape
    d = dy.shape[1]
    tr = min(GATE_BWD_ROWS, t)
    nsteps = t // tr
    tc = f // 2

    def body(dy_ref, dyn_ref, w_ref, pg_ref, pu_ref, gate_ref, gaten_ref, up_ref, upn_ref, wg_ref, wu_ref,
             dg_ref, du_ref, sg_ref, su_ref):
        i = pl.program_id(1)
        last = i == nsteps - 1
        w = w_ref[0]
        da = _dot_nt(dy_ref[...].astype(bf16), w)
        da_n = _dot_nt(dyn_ref[...].astype(bf16), w)
        dgate, dup = _gate_grads(gate_ref[...], up_ref[...], da)
        dgate_n, dup_n = _gate_grads(gaten_ref[...], upn_ref[...], da_n)
        rid = lax.broadcasted_iota(jnp.int32, (8, tc), 0)
        for dd, d_n, c_ref, p_ref, o_ref, s_ref in ((dgate, dgate_n, wg_ref, pg_ref, dg_ref, sg_ref),
                                                    (dup, dup_n, wu_ref, pu_ref, du_ref, su_ref)):
            d1, d2 = _shift_up(dd, d_n, 1, last), _shift_up(dd, d_n, 2, last)
            o_ref[...] = (c_ref[2:3, :] * dd + c_ref[1:2, :] * d1 + c_ref[0:1, :] * d2).astype(bf16)
            p = p_ref[...]
            sums = [jnp.sum(d2 * p, axis=0, keepdims=True), jnp.sum(d1 * p, axis=0, keepdims=True),
                    jnp.sum(dd * p, axis=0, keepdims=True), jnp.sum(dd, axis=0, keepdims=True)]
            part = jnp.zeros((8, tc), f32)
            for k, sk in enumerate(sums):
                part = jnp.where(rid == k, sk, part)

            @pl.when(i == 0)
            def _(s_ref=s_ref, part=part):
                s_ref[...] = part

            @pl.when(i > 0)
            def _(s_ref=s_ref, part=part):
                s_ref[...] += part

    def nxt_rows(j, i):
        return (jnp.minimum((i + 1) * (tr // HALO), t // HALO - 1), j)

    tile = pl.BlockSpec((tr, tc), lambda j, i: (i, j))
    nxt = pl.BlockSpec((HALO, tc), nxt_rows)
    wspec = pl.BlockSpec((3, tc), lambda j, i: (0, j))
    stat = pl.BlockSpec((8, tc), lambda j, i: (0, j))
    return pl.pallas_call(
        body, name=name, grid=(2, nsteps),
        in_specs=[pl.BlockSpec((tr, d), lambda j, i: (i, 0)),
                  pl.BlockSpec((HALO, d), lambda j, i: (nxt_rows(j, i)[0], 0)),
                  pl.BlockSpec((1, tc, d), lambda j, i: (j, 0, 0)),
                  tile, tile, tile, nxt, tile, nxt, wspec, wspec],
        out_specs=[tile, tile, stat, stat],
        out_shape=[jax.ShapeDtypeStruct((t, f), bf16), jax.ShapeDtypeStruct((t, f), bf16),
                   jax.ShapeDtypeStruct((8, f), f32), jax.ShapeDtypeStruct((8, f), f32)],
        compiler_params=_params("parallel", "arbitrary"))(
            dy, dy, w_out.reshape(2, tc, d), pg, pu, gate, gate, up, up, wg, wu)


def _head_mean_matrix():
    i = lax.broadcasted_iota(jnp.int32, (LANES, LANES), 0) // HEAD_DIM
    j = lax.broadcasted_iota(jnp.int32, (LANES, LANES), 1) // HEAD_DIM
    return jnp.where(i == j, 1.0 / HEAD_DIM, 0.0).astype(bf16)


def _lane_half(shape):
    return (lax.broadcasted_iota(jnp.int32, shape, 1) % LANES) // HEAD_DIM


def q_proj_norm(h, w, g2, *, name, scale):
    t, k = h.shape
    n = w.shape[1]
    tm = _row_tile(t, k * n * 2, k * 2 + n * 4 + n * 2)

    def body(h_ref, w_ref, g_ref, qp_ref, o_ref):
        qp_ref[...] = _dot(h_ref[...], w_ref[...])
        bd = _head_mean_matrix()
        for cb in range(n // LANES):
            cols = slice(cb * LANES, (cb + 1) * LANES)
            xc = qp_ref[:, cols]
            rh = lax.rsqrt(_dot_split(xc * xc, bd) + EPS)
            o_ref[:, cols] = (xc * rh * g_ref[...] * scale).astype(bf16)

    row = pl.BlockSpec((tm, n), lambda i: (i, 0))
    return pl.pallas_call(
        body, name=name, grid=(t // tm,),
        in_specs=[pl.BlockSpec((tm, k), lambda i: (i, 0)), pl.BlockSpec((k, n), lambda i: (0, 0)),
                  pl.BlockSpec((1, LANES), lambda i: (0, 0))],
        out_specs=[row, row], out_shape=[jax.ShapeDtypeStruct((t, n), f32), jax.ShapeDtypeStruct((t, n), bf16)],
        compiler_params=_params("parallel"))(h, w, g2)


def kv_proj_post(h, w, g2, *, name):
    t, k = h.shape
    n = w.shape[1]
    kw = n // 2
    tm = _row_tile(t, k * n * 2, k * 2 + n * 4 + 2 * n * 2)

    def body(h_ref, w_ref, g_ref, kv_ref, k_ref, v_ref):
        kv_ref[...] = _dot(h_ref[...], w_ref[...])
        bd = _head_mean_matrix()
        half = _lane_half((tm, LANES))
        for cb in range(kw // LANES):
            xc = kv_ref[:, cb * LANES:(cb + 1) * LANES]
            rh = lax.rsqrt(_dot_split(xc * xc, bd) + EPS)
            kn = xc * rh * g_ref[...]
            vc = kv_ref[:, kw + cb * LANES:kw + (cb + 1) * LANES]
            for src, dst in ((kn, k_ref), (vc, v_ref)):
                sw = pltpu.roll(src, HEAD_DIM, 1)
                for hf in range(2):
                    blk = 2 * cb + hf
                    dst[:, blk * LANES:(blk + 1) * LANES] = jnp.where(half == hf, src, sw).astype(bf16)

    row = pl.BlockSpec((tm, n), lambda i: (i, 0))
    return pl.pallas_call(
        body, name=name, grid=(t // tm,),
        in_specs=[pl.BlockSpec((tm, k), lambda i: (i, 0)), pl.BlockSpec((k, n), lambda i: (0, 0)),
                  pl.BlockSpec((1, LANES), lambda i: (0, 0))],
        out_specs=[row, row, row],
        out_shape=[jax.ShapeDtypeStruct((t, n), f32), jax.ShapeDtypeStruct((t, n), bf16), jax.ShapeDtypeStruct((t, n), bf16)],
        compiler_params=_params("parallel"))(h, w, g2)


def q_norm_bwd(dq, qp, g2, *, name, scale, tr=512):
    t, w = qp.shape
    tr = min(tr, t)

    def body(dq_ref, x_ref, g_ref, o_ref, dg_ref):
        i = pl.program_id(0)
        bd = _head_mean_matrix()
        acc = jnp.zeros((1, LANES), f32)
        for cb in range(w // LANES):
            cols = slice(cb * LANES, (cb + 1) * LANES)
            xc = x_ref[:, cols]
            rh = lax.rsqrt(_dot_split(xc * xc, bd) + EPS)
            xh = xc * rh
            dy = dq_ref[:, cols] * scale
            acc = acc + jnp.sum(dy * xh, axis=0, keepdims=True)
            tg = dy * g_ref[...]
            o_ref[:, cols] = (rh * (tg - xh * _dot_split(tg * xh, bd))).astype(bf16)

        @pl.when(i == 0)
        def _():
            dg_ref[...] = acc

        @pl.when(i > 0)
        def _():
            dg_ref[...] += acc

    row = pl.BlockSpec((tr, w), lambda i: (i, 0))
    vec = pl.BlockSpec((1, LANES), lambda i: (0, 0))
    return pl.pallas_call(
        body, name=name, grid=(t // tr,), in_specs=[row, row, vec], out_specs=[row, vec],
        out_shape=[jax.ShapeDtypeStruct((t, w), bf16), jax.ShapeDtypeStruct((1, LANES), f32)],
        compiler_params=_params("arbitrary"))(dq, qp, g2)


def kv_post_bwd(dk2, dv2, kv, g2, *, name, tr=512):
    t, w = kv.shape
    tr = min(tr, t)
    kw = w // 2

    def body(dk_ref, dv_ref, x_ref, g_ref, o_ref, dg_ref):
        i = pl.program_id(0)
        bd = _head_mean_matrix()
        half = _lane_half((tr, LANES))
        acc = jnp.zeros((1, LANES), f32)

        def fold(ref, cb):
            a = ref[:, (2 * cb) * LANES:(2 * cb + 1) * LANES]
            b = ref[:, (2 * cb + 1) * LANES:(2 * cb + 2) * LANES]
            return jnp.where(half == 0, a + pltpu.roll(a, HEAD_DIM, 1), b + pltpu.roll(b, HEAD_DIM, 1))

        for cb in range(kw // LANES):
            cols = slice(cb * LANES, (cb + 1) * LANES)
            xc = x_ref[:, cols]
            rh = lax.rsqrt(_dot_split(xc * xc, bd) + EPS)
            xh = xc * rh
            dy = fold(dk_ref, cb)
            acc = acc + jnp.sum(dy * xh, axis=0, keepdims=True)
            tg = dy * g_ref[...]
            o_ref[:, cols] = (rh * (tg - xh * _dot_split(tg * xh, bd))).astype(bf16)
            o_ref[:, kw + cb * LANES:kw + (cb + 1) * LANES] = fold(dv_ref, cb).astype(bf16)

        @pl.when(i == 0)
        def _():
            dg_ref[...] = acc

        @pl.when(i > 0)
        def _():
            dg_ref[...] += acc

    dup = pl.BlockSpec((tr, 2 * kw), lambda i: (i, 0))
    row = pl.BlockSpec((tr, w), lambda i: (i, 0))
    vec = pl.BlockSpec((1, LANES), lambda i: (0, 0))
    return pl.pallas_call(
        body, name=name, grid=(t // tr,), in_specs=[dup, dup, row, vec], out_specs=[row, vec],
        out_shape=[jax.ShapeDtypeStruct((t, w), bf16), jax.ShapeDtypeStruct((1, LANES), f32)],
        compiler_params=_params("arbitrary"))(dk2, dv2, kv, g2)


def _slope(h):
    return 2.0 ** (-8.0 * (h + 1) / N_Q_HEADS)


GROUP_ROWS = Q_PER_KV * CHUNK


def _band_mask(n):
    tq = lax.broadcasted_iota(jnp.int32, (GROUP_ROWS, 2 * CHUNK), 0) % CHUNK
    jk = lax.broadcasted_iota(jnp.int32, (GROUP_ROWS, 2 * CHUNK), 1)
    dist = tq + CHUNK - jk
    ok = (dist >= 0) & (dist < CHUNK) & jnp.logical_not((n == 0) & (jk < CHUNK))
    return dist.astype(f32), ok


def _band(ref, n, kh):
    p0 = pl.multiple_of(jnp.maximum(n - 1, 0) * CHUNK, CHUNK)
    c0 = pl.multiple_of(n * CHUNK, CHUNK)
    cols = slice(kh * LANES, (kh + 1) * LANES)
    return jnp.concatenate([ref[pl.ds(p0, CHUNK), cols], ref[pl.ds(c0, CHUNK), cols]], axis=0)


def _stack_heads(ref, kh, half):
    parts = []
    for cb in (2 * kh, 2 * kh + 1):
        xc = ref[:, cb * LANES:(cb + 1) * LANES].astype(f32)
        parts += [jnp.where(half == hf, xc, 0.0).astype(bf16) for hf in range(2)]
    return jnp.concatenate(parts, axis=0)


def _unstack_heads(x4, half):
    return (jnp.where(half == 0, x4[0:CHUNK], x4[CHUNK:2 * CHUNK]),
            jnp.where(half == 0, x4[2 * CHUNK:3 * CHUNK], x4[3 * CHUNK:]))


def _per_head_column(kh, values):
    grp = lax.broadcasted_iota(jnp.int32, (GROUP_ROWS, 1), 0) // CHUNK
    col = jnp.full((GROUP_ROWS, 1), values[0], f32)
    for g in range(1, Q_PER_KV):
        col = jnp.where(grp == g, values[g], col)
    return col


def _softmax_band(q4, kband, dist, ok, slope, sink):
    s = _dot_nt(q4, kband)
    s = jnp.where(ok, s - slope * dist, -jnp.inf)
    m = jnp.maximum(jnp.max(s, axis=1, keepdims=True), sink)
    e = jnp.exp(s - m)
    es = jnp.exp(sink - m)
    den = jnp.sum(e, axis=1, keepdims=True) + es
    return e / den, es / den


def attn_fwd(q, k2, v2, sinks, *, name):
    t, w = q.shape
    nb = t // CHUNK

    def body(sink_ref, q_ref, k_ref, v_ref, o_ref):
        n = pl.program_id(0)
        dist, ok = _band_mask(n)
        half = _lane_half((CHUNK, LANES))
        khs = range(N_KV_HEADS)
        heads = [[Q_PER_KV * kh + g for g in range(Q_PER_KV)] for kh in khs]
        q4 = [_stack_heads(q_ref, kh, half) for kh in khs]
        soft = [_softmax_band(q4[kh], _band(k_ref, n, kh), dist, ok, _per_head_column(kh, [_slope(h) for h in heads[kh]]),
                              _per_head_column(kh, [sink_ref[h] for h in heads[kh]])) for kh in khs]
        o4 = [_dot(soft[kh][0].astype(bf16), _band(v_ref, n, kh)) for kh in khs]
        for kh in khs:
            lo, hi = _unstack_heads(o4[kh], half)
            o_ref[:, (2 * kh) * LANES:(2 * kh + 1) * LANES] = lo.astype(bf16)
            o_ref[:, (2 * kh + 1) * LANES:(2 * kh + 2) * LANES] = hi.astype(bf16)

    full = pl.BlockSpec((t, k2.shape[1]), lambda n: (0, 0))
    return pl.pallas_call(
        body, name=name, grid=(nb,),
        in_specs=[pl.BlockSpec(memory_space=pltpu.SMEM), pl.BlockSpec((CHUNK, w), lambda n: (n, 0)), full, full],
        out_specs=pl.BlockSpec((CHUNK, w), lambda n: (n, 0)),
        out_shape=jax.ShapeDtypeStruct((t, w), bf16),
        compiler_params=_params("parallel"))(sinks, q, k2, v2)


def attn_bwd(q, k2, v2, do, sinks, *, name):
    t, w = q.shape
    nb = t // CHUNK
    kw = k2.shape[1]

    def body(sink_ref, q_ref, k_ref, v_ref, do_ref, dq_ref, dk_ref, dv_ref, ds_ref, kc_ref, vc_ref):
        n = pl.program_id(0)

        @pl.when(n == 0)
        def _():
            ds_ref[...] = jnp.zeros_like(ds_ref)
            kc_ref[...] = jnp.zeros_like(kc_ref)
            vc_ref[...] = jnp.zeros_like(vc_ref)
            dk_ref[...] = jnp.zeros_like(dk_ref)
            dv_ref[...] = jnp.zeros_like(dv_ref)

        @pl.when(n == nb)
        def _():
            dk_ref[...] = kc_ref[...]
            dv_ref[...] = vc_ref[...]

        @pl.when(n < nb)
        def _():
            dist, ok = _band_mask(n)
            half = _lane_half((CHUNK, LANES))
            lane = lax.broadcasted_iota(jnp.int32, (1, LANES), 1)
            sink_acc = jnp.zeros((1, LANES), f32)
            khs = range(N_KV_HEADS)
            heads = [[Q_PER_KV * kh + g for g in range(Q_PER_KV)] for kh in khs]
            q4 = [_stack_heads(q_ref, kh, half) for kh in khs]
            do4 = [_stack_heads(do_ref, kh, half) for kh in khs]
            kband = [_band(k_ref, n, kh) for kh in khs]
            vband = [_band(v_ref, n, kh) for kh in khs]
            soft = [_softmax_band(q4[kh], kband[kh], dist, ok, _per_head_column(kh, [_slope(h) for h in heads[kh]]),
                                  _per_head_column(kh, [sink_ref[h] for h in heads[kh]])) for kh in khs]
            dp = [_dot_nt(do4[kh], vband[kh]) for kh in khs]
            delta = [jnp.sum(soft[kh][0] * dp[kh], axis=1, keepdims=True) for kh in khs]
            dsb = [(soft[kh][0] * (dp[kh] - delta[kh])).astype(bf16) for kh in khs]
            dq4 = [_dot(dsb[kh], kband[kh]) for kh in khs]
            dkb = [_dot_tn(dsb[kh], q4[kh]) for kh in khs]
            dvb = [_dot_tn(soft[kh][0].astype(bf16), do4[kh]) for kh in khs]
            for kh in khs:
                sd = soft[kh][1] * delta[kh]
                for g, h in enumerate(heads[kh]):
                    part = jnp.sum(sd[g * CHUNK:(g + 1) * CHUNK], axis=0, keepdims=True)
                    sink_acc = sink_acc + jnp.where(lane == h, -part, 0.0)
                lo, hi = _unstack_heads(dq4[kh], half)
                dq_ref[:, (2 * kh) * LANES:(2 * kh + 1) * LANES] = lo
                dq_ref[:, (2 * kh + 1) * LANES:(2 * kh + 2) * LANES] = hi
                cols = slice(kh * LANES, (kh + 1) * LANES)
                dk_ref[:, cols] = kc_ref[:, cols] + dkb[kh][0:CHUNK]
                dv_ref[:, cols] = vc_ref[:, cols] + dvb[kh][0:CHUNK]
                kc_ref[:, cols] = dkb[kh][CHUNK:]
                vc_ref[:, cols] = dvb[kh][CHUNK:]
            ds_ref[...] += sink_acc

    full = pl.BlockSpec((t, kw), lambda n: (0, 0))
    qblk = pl.BlockSpec((CHUNK, w), lambda n: (jnp.minimum(n, nb - 1), 0))
    kblk = pl.BlockSpec((CHUNK, kw), lambda n: (jnp.maximum(n - 1, 0), 0))
    return pl.pallas_call(
        body, name=name, grid=(nb + 1,),
        in_specs=[pl.BlockSpec(memory_space=pltpu.SMEM), qblk, full, full, qblk],
        out_specs=[qblk, kblk, kblk, pl.BlockSpec((1, LANES), lambda n: (0, 0))],
        out_shape=[jax.ShapeDtypeStruct((t, w), f32), jax.ShapeDtypeStruct((t, kw), f32),
                   jax.ShapeDtypeStruct((t, kw), f32), jax.ShapeDtypeStruct((1, LANES), f32)],
        scratch_shapes=[pltpu.VMEM((CHUNK, kw), f32), pltpu.VMEM((CHUNK, kw), f32)],
        compiler_params=_params("arbitrary"))(sinks, q, k2, v2, do)


N_STEPS = 8


def _row_blocks(shape):
    if len(shape) == 2:
        r, c = shape
        return (r // N_STEPS, c), (lambda s: (s, 0))
    l, r, c = shape
    per = N_STEPS // l
    return (1, r // per, c), (lambda s: (s // per, s % per, 0))


CAST_STEPS = 4


def cast_into_slot(arrays, k_arr, *, name):
    in_specs, out_specs, out_shape, layers = [], [], [], []
    for a in arrays:
        r, c = a.shape[-2:]
        rb = r // CAST_STEPS
        if a.ndim == 2:
            in_specs.append(pl.BlockSpec((rb, c), lambda s, k: (s, 0)))
            layers.append(None)
        else:
            for l in range(a.shape[0]):
                in_specs.append(pl.BlockSpec((1, rb, c), lambda s, k, l=l: (l, s, 0)))
                layers.append(l)
        for _ in range(1 if a.ndim == 2 else a.shape[0]):
            out_specs.append(pl.BlockSpec((1, rb, c), lambda s, k: (k[0], s, 0)))
            out_shape.append(jax.ShapeDtypeStruct((N_SHARDS, r, c), bf16))
    n = len(in_specs)

    def body(k_ref, *refs):
        for i_ref, o_ref, l in zip(refs[:n], refs[n:], layers):
            o_ref[0] = (i_ref[...] if l is None else i_ref[0]).astype(bf16)

    args = []
    for a in arrays:
        args += [a] * (1 if a.ndim == 2 else a.shape[0])
    return pl.pallas_call(
        body, name=name,
        grid_spec=pltpu.PrefetchScalarGridSpec(num_scalar_prefetch=1, grid=(CAST_STEPS,),
                                               in_specs=in_specs, out_specs=out_specs),
        out_shape=out_shape, compiler_params=_params("parallel"))(k_arr, *args)


def adamw(ws, gs, ms, vs, *, name):
    n = len(ws)
    specs, g_specs, g_count = [], [], []
    for w, g_list in zip(ws, gs):
        blk, index = _row_blocks(w.shape)
        specs.append(pl.BlockSpec(blk, index))
        layers = len(g_list)
        per = N_STEPS // layers
        g_count.append(layers)
        for l in range(layers):
            g_specs.append(pl.BlockSpec(blk[-2:], lambda s, l=l, per=per: (jnp.where(s // per == l, s % per, 0), 0)))
    ng = len(g_specs)

    def body(*refs):
        s = pl.program_id(0)
        g_refs = refs[3 * n:3 * n + ng]
        outs = refs[3 * n + ng:]
        off = 0
        for i in range(n):
            w_ref, m_ref, v_ref = refs[i], refs[n + i], refs[2 * n + i]
            go_ref, d_ref, nm_ref, nv_ref = (outs[k * n + i] for k in range(4))
            layers = g_count[i]
            g = g_refs[off][...]
            for l in range(1, layers):
                g = jnp.where(s // (N_STEPS // layers) == l, g_refs[off + l][...], g)
            off += layers
            g = g.reshape(w_ref.shape)
            m = ADAM_B1 * m_ref[...] + (1.0 - ADAM_B1) * g
            v = ADAM_B2 * v_ref[...] + (1.0 - ADAM_B2) * (g * g)
            m_hat = m / ADAM_C1
            v_hat = v / ADAM_C2
            go_ref[...] = g
            d_ref[...] = -ADAM_LR * (m_hat / (jnp.sqrt(v_hat) + ADAM_EPS) + ADAM_WD * w_ref[...])
            nm_ref[...] = m
            nv_ref[...] = v

    outs = pl.pallas_call(
        body, name=name, grid=(N_STEPS,), in_specs=specs * 3 + g_specs, out_specs=specs * 4,
        out_shape=[jax.ShapeDtypeStruct(a.shape, f32) for a in ws] * 4,
        compiler_params=_params("parallel"))(*ws, *ms, *vs, *[g for g_list in gs for g in g_list])
    return [outs[k * n:(k + 1) * n] for k in range(4)]


def _adamw_update(w, g, m, v):
    m = ADAM_B1 * m + (1.0 - ADAM_B1) * g
    v = ADAM_B2 * v + (1.0 - ADAM_B2) * (g * g)
    m_hat = m / ADAM_C1
    v_hat = v / ADAM_C2
    return -ADAM_LR * (m_hat / (jnp.sqrt(v_hat) + ADAM_EPS) + ADAM_WD * w), m, v


def adamw_small(ws, gs, ms, vs, *, name):
    n = len(ws)

    def body(*refs):
        for i in range(n):
            w_ref, g_ref, m_ref, v_ref = (refs[k * n + i] for k in range(4))
            d_ref, nm_ref, nv_ref = (refs[(4 + k) * n + i] for k in range(3))
            d_ref[...], nm_ref[...], nv_ref[...] = _adamw_update(w_ref[...], g_ref[...], m_ref[...], v_ref[...])

    outs = pl.pallas_call(
        body, name=name, out_shape=[jax.ShapeDtypeStruct(a.shape, f32) for a in ws] * 3)(*ws, *gs, *ms, *vs)
    return outs[:n], outs[n:2 * n], outs[2 * n:]


def _place():
    return lax.axis_index("x"), lax.axis_index("y"), lax.axis_index("c")


def gather_shards(bufs, *, name, split):
    n = len(bufs)

    def body(*refs):
        bufs_ = refs[:n]
        isend, irecv, dsend, drecv = refs[2 * n:]
        x, y, c = _place()
        k = 2 * x + y
        peers = [(1 - x, y, c), (x, 1 - y, c), (1 - x, 1 - y, c)]
        peer_k = [2 * (1 - x) + y, 2 * x + (1 - y), 2 * (1 - x) + (1 - y)]

        def slab(a, q, h):
            if not split[a]:
                return bufs_[a].at[q]
            half = bufs_[a].shape[1] // 2
            return bufs_[a].at[q, pl.ds(pl.multiple_of(h * half, 16), half)]

        def ici(a, j, q):
            return pltpu.make_async_remote_copy(
                src_ref=slab(a, q, c), dst_ref=slab(a, q, c), send_sem=isend.at[3 * a + j], recv_sem=irecv.at[3 * a + j],
                device_id=peers[j], device_id_type=MESH)

        def d2d(a, j, h):
            return pltpu.make_async_remote_copy(
                src_ref=slab(a, peer_k[j], h), dst_ref=slab(a, peer_k[j], h), send_sem=dsend.at[3 * a + j],
                recv_sem=drecv.at[3 * a + j], device_id=(x, y, 1 - c), device_id_type=MESH)

        for a in range(n):
            for j in range(3):
                ici(a, j, k).start()
        for a in range(n):
            for j in range(3):
                ici(a, j, peer_k[j]).wait_recv()
                if split[a]:
                    d2d(a, j, c).start()
        for a in range(n):
            for j in range(3):
                if split[a]:
                    d2d(a, j, 1 - c).wait_recv()
        for a in range(n):
            for j in range(3):
                ici(a, j, k).wait_send()
                if split[a]:
                    d2d(a, j, c).wait_send()

    return pl.pallas_call(
        body, name=name, in_specs=[ANY] * n, out_specs=[ANY] * n,
        out_shape=[jax.ShapeDtypeStruct(b.shape, b.dtype) for b in bufs],
        input_output_aliases={i: i for i in range(n)},
        scratch_shapes=[pltpu.SemaphoreType.DMA((3 * n,))] * 4)(*bufs)


HBM = pl.BlockSpec(memory_space=pltpu.HBM)
SEM = pl.BlockSpec(memory_space=pltpu.SEMAPHORE)
DATAFLOW = pltpu.SideEffectType.DATAFLOW_SIDE_EFFECTING


def _chip_peers():
    x, y, c = _place()
    return 2 * x + y, [(1 - x, y, c), (x, 1 - y, c), (1 - x, 1 - y, c)], [2 * (1 - x) + y, 2 * x + (1 - y), 2 * (1 - x) + (1 - y)]


def _half_slab(ref, q, h):
    half = ref.shape[1] // 2
    return ref.at[q, pl.ds(pl.multiple_of(h * half, BF16_ROWS), half)]


def gather_start(bufs, groups, after, *, name):
    n = len(bufs)
    ng = len(groups)

    def body(*refs):
        ins = refs[:n]
        sends, recvs = refs[2 * n + 1:2 * n + 1 + ng], refs[2 * n + 1 + ng:2 * n + 1 + 2 * ng]
        token = refs[-1]
        c = lax.axis_index("c")
        k, peers, _ = _chip_peers()
        for gi, grp in enumerate(groups):
            for pos, a in enumerate(grp):
                for j in range(3):
                    pltpu.make_async_remote_copy(
                        src_ref=_half_slab(ins[a], k, c), dst_ref=_half_slab(ins[a], k, c), send_sem=sends[gi].at[3 * pos + j],
                        recv_sem=recvs[gi].at[3 * pos + j], device_id=peers[j], device_id_type=MESH).start()
        token[...] = jnp.zeros_like(token)

    sems = [pltpu.SemaphoreType.DMA((3 * len(grp),)) for grp in groups]
    outs = pl.pallas_call(
        body, name=name, in_specs=[HBM] * n + [ANY],
        out_specs=[HBM] * n + [SEM] * (2 * ng) + [pl.BlockSpec(memory_space=pltpu.VMEM)],
        out_shape=[pltpu.HBM(b.shape, b.dtype) for b in bufs] + sems + sems + [jax.ShapeDtypeStruct((8, LANES), f32)],
        input_output_aliases={i: i for i in range(n)},
        compiler_params=pltpu.CompilerParams(has_side_effects=DATAFLOW))(
            *[pltpu.with_memory_space_constraint(b, pltpu.HBM) for b in bufs], after)
    return outs[:n], outs[n:n + ng], outs[n + ng:n + 2 * ng], outs[-1]


def gather_wait(bufs, send_sems, recv_sems, after, *, name):
    n = len(bufs)

    def body(*refs):
        ins = refs[:n]
        send, recv = refs[n], refs[n + 1]
        c = lax.axis_index("c")
        k, peers, peer_k = _chip_peers()
        for a in range(n):
            for j in range(3):
                copy = pltpu.make_async_remote_copy(
                    src_ref=_half_slab(ins[a], k, c), dst_ref=_half_slab(ins[a], peer_k[j], c), send_sem=send.at[3 * a + j],
                    recv_sem=recv.at[3 * a + j], device_id=peers[j], device_id_type=MESH)
                copy.wait_send()
                copy.wait_recv()

    return pl.pallas_call(
        body, name=name, in_specs=[HBM] * n + [SEM, SEM, ANY], out_specs=[HBM] * n,
        out_shape=[pltpu.HBM(b.shape, b.dtype) for b in bufs],
        input_output_aliases={i: i for i in range(n)},
        compiler_params=pltpu.CompilerParams(has_side_effects=DATAFLOW))(*bufs, send_sems, recv_sems, after)


def forward_halves(bufs, *, name):
    n = len(bufs)

    def body(*refs):
        bufs_ = refs[:n]
        send, recv = refs[2 * n:]
        x, y, c = _place()
        _, _, peer_k = _chip_peers()

        def copy(a, j, h):
            return pltpu.make_async_remote_copy(
                src_ref=_half_slab(bufs_[a], peer_k[j], h), dst_ref=_half_slab(bufs_[a], peer_k[j], h),
                send_sem=send.at[3 * a + j], recv_sem=recv.at[3 * a + j], device_id=(x, y, 1 - c), device_id_type=MESH)

        for a in range(n):
            for j in range(3):
                copy(a, j, c).start()
        for a in range(n):
            for j in range(3):
                copy(a, j, 1 - c).wait_recv()
        for a in range(n):
            for j in range(3):
                copy(a, j, c).wait_send()

    return pl.pallas_call(
        body, name=name, in_specs=[ANY] * n, out_specs=[ANY] * n,
        out_shape=[jax.ShapeDtypeStruct(b.shape, b.dtype) for b in bufs],
        input_output_aliases={i: i for i in range(n)},
        scratch_shapes=[pltpu.SemaphoreType.DMA((3 * n,))] * 2)(*bufs)


def _forward_copies(bufs_, send, recv, h):
    x, y, c = _place()
    _, _, peer_k = _chip_peers()
    return [pltpu.make_async_remote_copy(
        src_ref=_half_slab(bufs_[a], peer_k[j], h), dst_ref=_half_slab(bufs_[a], peer_k[j], h),
        send_sem=send.at[3 * a + j], recv_sem=recv.at[3 * a + j], device_id=(x, y, 1 - c), device_id_type=MESH)
        for a in range(len(bufs_)) for j in range(3)]


def forward_start(bufs, after, *, name):
    n = len(bufs)

    def body(*refs):
        for cp in _forward_copies(refs[:n], refs[2 * n + 1], refs[2 * n + 2], lax.axis_index("c")):
            cp.start()
        refs[-1][...] = jnp.zeros_like(refs[-1])

    sems = [pltpu.SemaphoreType.DMA((3 * n,))] * 2
    outs = pl.pallas_call(
        body, name=name, in_specs=[HBM] * n + [ANY],
        out_specs=[HBM] * n + [SEM] * 2 + [pl.BlockSpec(memory_space=pltpu.VMEM)],
        out_shape=[pltpu.HBM(b.shape, b.dtype) for b in bufs] + sems + [jax.ShapeDtypeStruct((8, LANES), f32)],
        input_output_aliases={i: i for i in range(n)},
        compiler_params=pltpu.CompilerParams(has_side_effects=DATAFLOW))(*bufs, after)
    return (n, outs[:-1]), outs[-1]


def forward_wait(state, after, *, name):
    n, held = state

    def body(*refs):
        c = lax.axis_index("c")
        for mine, theirs in zip(_forward_copies(refs[:n], refs[n], refs[n + 1], c),
                                _forward_copies(refs[:n], refs[n], refs[n + 1], 1 - c)):
            mine.wait_send()
            theirs.wait_recv()

    return pl.pallas_call(
        body, name=name, in_specs=[HBM] * n + [SEM] * 2 + [ANY], out_specs=[HBM] * n,
        out_shape=[pltpu.HBM(b.shape, b.dtype) for b in held[:n]],
        input_output_aliases={i: i for i in range(n)},
        compiler_params=pltpu.CompilerParams(has_side_effects=DATAFLOW))(*held, after)


def _sibling_copies(srcs, lands, send, recv):
    x, y, c = _place()
    return [pltpu.make_async_remote_copy(src_ref=srcs[a], dst_ref=lands[a], send_sem=send.at[a], recv_sem=recv.at[a],
                                         device_id=(x, y, 1 - c), device_id_type=MESH) for a in range(len(srcs))]


def sibling_start(arrays, after, *, name):
    n = len(arrays)
    lands = [pltpu.with_memory_space_constraint(lax.empty(a.shape, a.dtype), pltpu.HBM) for a in arrays]

    def body(*refs):
        for cp in _sibling_copies(refs[:n], refs[n:2 * n], refs[4 * n + 1], refs[4 * n + 2]):
            cp.start()
        refs[-1][...] = jnp.zeros_like(refs[-1])

    bufs = list(arrays) + lands
    sems = [pltpu.SemaphoreType.DMA((n,))] * 2
    outs = pl.pallas_call(
        body, name=name, in_specs=[HBM] * (2 * n) + [ANY],
        out_specs=[HBM] * (2 * n) + [SEM] * 2 + [pl.BlockSpec(memory_space=pltpu.VMEM)],
        out_shape=[pltpu.HBM(b.shape, b.dtype) for b in bufs] + sems + [jax.ShapeDtypeStruct((8, LANES), f32)],
        input_output_aliases={i: i for i in range(2 * n)},
        compiler_params=pltpu.CompilerParams(has_side_effects=DATAFLOW))(
            *[pltpu.with_memory_space_constraint(b, pltpu.HBM) for b in bufs], after)
    return (n, outs[:-1]), outs[-1]


def sibling_wait(state, after, *, name):
    n, held = state

    def body(*refs):
        for cp in _sibling_copies(refs[:n], refs[n:2 * n], refs[2 * n], refs[2 * n + 1]):
            cp.wait_send()
            cp.wait_recv()

    outs = pl.pallas_call(
        body, name=name, in_specs=[HBM] * (2 * n) + [SEM] * 2 + [ANY], out_specs=[HBM] * (2 * n),
        out_shape=[pltpu.HBM(b.shape, b.dtype) for b in held[:2 * n]],
        input_output_aliases={i: i for i in range(2 * n)},
        compiler_params=pltpu.CompilerParams(has_side_effects=DATAFLOW))(*held, after)
    return outs[n:]


ALL_MASKS = [(mx, my, mc) for mx in (0, 1) for my in (0, 1) for mc in (0, 1)][1:]


def _scatter_copies(srcs, lands, ev, send, recv, esend, erecv):
    x, y, c = _place()
    me = 4 * x + 2 * y + c
    k, peers, peer_k = _chip_peers()
    out = []
    for a in range(len(srcs)):
        for j in range(3):
            out.append(pltpu.make_async_remote_copy(
                src_ref=srcs[a].at[peer_k[j]], dst_ref=lands[a].at[j], send_sem=send.at[3 * a + j],
                recv_sem=recv.at[3 * a + j], device_id=peers[j], device_id_type=MESH))
    start_ev, wait_ev = [], []
    if ev is not None:
        for j, (mx, my, mc) in enumerate(ALL_MASKS):
            peer = (x ^ mx, y ^ my, c ^ mc)
            start_ev.append(pltpu.make_async_remote_copy(
                src_ref=ev.at[me], dst_ref=ev.at[me], send_sem=esend.at[j], recv_sem=erecv.at[j],
                device_id=peer, device_id_type=MESH))
            wait_ev.append(pltpu.make_async_remote_copy(
                src_ref=ev.at[me], dst_ref=ev.at[me ^ (4 * mx + 2 * my + mc)], send_sem=esend.at[j],
                recv_sem=erecv.at[j], device_id=peer, device_id_type=MESH))
    return out, start_ev, wait_ev


def chip_scatter_start(arrays, everyone, after, *, name):
    n = len(arrays)
    ne = 0 if everyone is None else 1
    lands = [pltpu.with_memory_space_constraint(lax.empty((3,) + a.shape[1:], a.dtype), pltpu.HBM) for a in arrays]

    def body(*refs):
        srcs, lands_ = refs[:n], refs[n:2 * n]
        ev = refs[2 * n] if ne else None
        sems = refs[2 * n + ne + 1 + 2 * n + ne:-1]
        send, recv = sems[0], sems[1]
        esend, erecv = (sems[2], sems[3]) if ne else (None, None)
        copies, start_ev, _ = _scatter_copies(srcs, lands_, ev, send, recv, esend, erecv)
        for cp in start_ev + copies:
            cp.start()
        refs[-1][...] = jnp.zeros_like(refs[-1])

    sem_shapes = [pltpu.SemaphoreType.DMA((3 * n,))] * 2 + [pltpu.SemaphoreType.DMA((7,))] * (2 * ne)
    bufs = list(arrays) + lands + ([everyone] if ne else [])
    outs = pl.pallas_call(
        body, name=name, in_specs=[HBM] * len(bufs) + [ANY],
        out_specs=[HBM] * len(bufs) + [SEM] * len(sem_shapes) + [pl.BlockSpec(memory_space=pltpu.VMEM)],
        out_shape=[pltpu.HBM(b.shape, b.dtype) for b in bufs] + sem_shapes + [jax.ShapeDtypeStruct((8, LANES), f32)],
        input_output_aliases={i: i for i in range(len(bufs))},
        compiler_params=pltpu.CompilerParams(has_side_effects=DATAFLOW))(
            *[pltpu.with_memory_space_constraint(b, pltpu.HBM) for b in bufs], after)
    return (n, ne, outs[:-1]), outs[-1]


def chip_scatter_wait(state, after, *, name):
    n, ne, held = state
    nb = 2 * n + ne
    bufs, sems = held[:nb], held[nb:]

    def body(*refs):
        srcs, lands_ = refs[:n], refs[n:2 * n]
        ev = refs[2 * n] if ne else None
        sems_ = refs[nb:nb + len(sems)]
        esend, erecv = (sems_[2], sems_[3]) if ne else (None, None)
        copies, _, wait_ev = _scatter_copies(srcs, lands_, ev, sems_[0], sems_[1], esend, erecv)
        for cp in wait_ev + copies:
            cp.wait_send()
            cp.wait_recv()

    outs = pl.pallas_call(
        body, name=name, in_specs=[HBM] * nb + [SEM] * len(sems) + [ANY], out_specs=[HBM] * nb,
        out_shape=[pltpu.HBM(b.shape, b.dtype) for b in bufs],
        input_output_aliases={i: i for i in range(nb)},
        compiler_params=pltpu.CompilerParams(has_side_effects=DATAFLOW))(*bufs, *sems, after)
    return outs[n:2 * n], (outs[2 * n] if ne else None)


def sibling_merge(bufs, *, name):
    n = len(bufs)

    def body(*refs):
        bufs_ = refs[:n]
        send, recv = refs[2 * n:]
        x, y, c = _place()

        def copy(u, h):
            return pltpu.make_async_remote_copy(
                src_ref=bufs_[u].at[h], dst_ref=bufs_[u].at[h], send_sem=send.at[u], recv_sem=recv.at[u],
                device_id=(x, y, 1 - c), device_id_type=MESH)

        for u in range(n):
            copy(u, c).start()
        for u in range(n):
            copy(u, 1 - c).wait_recv()
        for u in range(n):
            copy(u, c).wait_send()

    return pl.pallas_call(
        body, name=name, in_specs=[ANY] * n, out_specs=[ANY] * n,
        out_shape=[jax.ShapeDtypeStruct(b.shape, b.dtype) for b in bufs],
        input_output_aliases={i: i for i in range(n)},
        scratch_shapes=[pltpu.SemaphoreType.DMA((n,)), pltpu.SemaphoreType.DMA((n,))])(*bufs)


def sum_leading(a, *, name):
    n, r, c = a.shape

    def body(a_ref, o_ref):
        acc = a_ref[0]
        for i in range(1, n):
            acc = acc + a_ref[i]
        o_ref[...] = acc

    rb = r // 2 if r % 16 == 0 else r
    return pl.pallas_call(
        body, name=name, grid=(r // rb,), in_specs=[pl.BlockSpec((n, rb, c), lambda i: (0, i, 0))],
        out_specs=pl.BlockSpec((rb, c), lambda i: (i, 0)), out_shape=jax.ShapeDtypeStruct((r, c), f32),
        compiler_params=_params("parallel"))(a)


def _half_rows(shape):
    return shape[1] // 2 // 2


def rs_add_sibling(grads, recvd, ck_arr, *, name):
    n = len(grads)

    def body(ck_ref, *refs):
        s = pl.program_id(1)
        for u in range(n):
            g_ref, r_ref = refs[u], refs[n + u]
            qb_ref, own_ref = refs[2 * n + u], refs[3 * n + u]
            q = g_ref[0] + r_ref[0].astype(f32)
            qb_ref[0] = q.astype(bf16)

            @pl.when(s == ck_ref[1])
            def _(own_ref=own_ref, q=q):
                own_ref[...] = q

    in_specs = [pl.BlockSpec((1, _half_rows(g.shape), g.shape[2]), lambda r, s, ck: (s, ck[0] * 2 + r, 0)) for g in grads]
    in_specs += [pl.BlockSpec((1, _half_rows(g.shape), g.shape[2]), lambda r, s, ck: (s, r, 0)) for g in grads]
    out_specs = [pl.BlockSpec((1, _half_rows(g.shape), g.shape[2]), lambda r, s, ck: (s, r, 0)) for g in grads]
    out_specs += [pl.BlockSpec((_half_rows(g.shape), g.shape[2]), lambda r, s, ck: (r, 0)) for g in grads]
    outs = pl.pallas_call(
        body, name=name,
        grid_spec=pltpu.PrefetchScalarGridSpec(num_scalar_prefetch=1, grid=(2, N_SHARDS),
                                               in_specs=in_specs, out_specs=out_specs),
        out_shape=[jax.ShapeDtypeStruct((N_SHARDS, g.shape[1] // 2, g.shape[2]), bf16) for g in grads]
        + [jax.ShapeDtypeStruct((g.shape[1] // 2, g.shape[2]), f32) for g in grads],
        compiler_params=_params("parallel", "arbitrary"))(ck_arr, *grads, *recvd)
    return outs[:n], outs[n:]


def rs_sum_chips(owns, recvd, ck_arr, *, name):
    n = len(owns)

    def body(ck_ref, *refs):
        for u in range(n):
            own_ref, r_ref, o_ref = refs[u], refs[n + u], refs[2 * n + u]
            o_ref[0] = ((own_ref[...] + r_ref[0].astype(f32)) + r_ref[1].astype(f32)) + r_ref[2].astype(f32)

    in_specs = [pl.BlockSpec((o.shape[0] // 2, o.shape[1]), lambda r, ck: (r, 0)) for o in owns]
    in_specs += [pl.BlockSpec((3, o.shape[0] // 2, o.shape[1]), lambda r, ck: (0, r, 0)) for o in owns]
    out_specs = [pl.BlockSpec((1, o.shape[0] // 2, o.shape[1]), lambda r, ck: (ck[0], r, 0)) for o in owns]
    return pl.pallas_call(
        body, name=name,
        grid_spec=pltpu.PrefetchScalarGridSpec(num_scalar_prefetch=1, grid=(2,), in_specs=in_specs, out_specs=out_specs),
        out_shape=[jax.ShapeDtypeStruct((2,) + o.shape, f32) for o in owns],
        compiler_params=_params("parallel"))(ck_arr, *owns, *recvd)


SMALL = ("a_norm", "a_v_norm", "a_w_s", "a_b_s", "f_norm", "f_conv_w", "f_conv_b", "kv_norm", "k_norm",
         "b_norm", "b_q_norm", "b_sinks")
BIG = ("a_w_in", "a_w_out", "f_w_in", "f_w_out", "w_kv", "b_w_q", "b_w_o")
PACK_COLS = 1024
PACK_ROWS = 8 * N_STEPS


def _pack(parts, rows=PACK_ROWS):
    flat = jnp.concatenate([p.reshape(-1).astype(f32) for p in parts])
    pad = (-flat.shape[0]) % (rows * PACK_COLS)
    return jnp.pad(flat, (0, pad)).reshape(-1, PACK_COLS)


def _unpack(packed, shapes):
    flat = packed.reshape(-1)
    out, off = [], 0
    for s in shapes:
        size = math.prod(s)
        out.append(flat[off:off + size].reshape(s))
        off += size
    return out


def _behind(value, token):
    return lax.optimization_barrier((value, token))[0]


def _ffn_fwd(x, g, h, r, w_in4, conv_w, conv_b, f, tag):
    wg, wu = conv_w[:, :f], conv_w[:, f:]
    bg, bu = conv_b[None, :f], conv_b[None, f:]
    pg, pu, gate, up, a = ffn_in_fused(h, w_in4, wg, wu, bg, bu, name=f"ffn{tag}_in")
    return a, (x, g, h, r, pg, pu, gate, up, a, wg, wu)


def _ffn_bwd(dy, saved, w_in4, w_out, c_arr, tag, exchange=False):
    x, g, h, r, pg, pu, gate, up, a, wg, wu = saved
    f = w_out.shape[0]
    d_w_out = mm_tn(a, [dy], c_arr, name=f"ffn{tag}_dwout", n_s=w_out.shape[1], shard_rows=f // N_SHARDS, tki=f // 2)
    dpg, dpu, sg, su = ffn_gate_bwd(dy, w_out, pg, pu, gate, up, wg, wu, name=f"ffn{tag}_dgate")
    d_w_in = mm_tn(h, [dpg, dpu], c_arr, name=f"ffn{tag}_dwin", n_s=w_in4.shape[2], shard_rows=h.shape[1])
    state = None
    if exchange:
        state, token = sibling_start([d_w_in[1], d_w_out[1]], d_w_in[0], name=f"rs_sibling_start_ffn{tag}")
        g = _behind(g, token)
    dx, dg = mm_nt_rms_bwd([dpg, dpu], w_in4, x, r, g, dy, name=f"ffn{tag}_dh")
    d_conv_w = jnp.concatenate([sg[0:3], su[0:3]], axis=1)
    d_conv_b = jnp.concatenate([sg[3], su[3]], axis=0)
    return dx, dg, d_w_in, d_conv_w, d_conv_b, d_w_out, state


def _rs_front(pairs, sibling_state, after, c_arr, tag):
    units = [full.reshape(N_SHARDS, -1, full.shape[-1]) for full, _ in pairs]
    from_sib = sibling_wait(sibling_state, after, name=f"rs_sibling_wait{tag}")
    return rs_add_sibling(units, from_sib, c_arr, name=f"rs_add{tag}")


def _rs_back(own, from_chips, c_arr, tag):
    halves = rs_sum_chips(list(own), list(from_chips), c_arr, name=f"rs_sum{tag}")
    return [m.reshape(-1, m.shape[2]) for m in sibling_merge(list(halves), name=f"rs_merge{tag}")]


def kernel(x, a_norm, a_w_in, a_v_norm, a_w_s, a_b_s, a_w_out, f_norm, f_w_in, f_conv_w, f_conv_b, f_w_out, kv_norm, w_kv, k_norm, b_norm, b_w_q, b_q_norm, b_sinks, b_w_o, loss_target, m_a_norm, m_a_w_in, m_a_v_norm, m_a_w_s, m_a_b_s, m_a_w_out, m_f_norm, m_f_w_in, m_f_conv_w, m_f_conv_b, m_f_w_out, m_kv_norm, m_w_kv, m_k_norm, m_b_norm, m_b_w_q, m_b_q_norm, m_b_sinks, m_b_w_o, v_a_norm, v_a_w_in, v_a_v_norm, v_a_w_s, v_a_b_s, v_a_w_out, v_f_norm, v_f_w_in, v_f_conv_w, v_f_conv_b, v_f_w_out, v_kv_norm, v_w_kv, v_k_norm, v_b_norm, v_b_w_q, v_b_q_norm, v_b_sinks, v_b_w_o):
    args = dict(locals())
    weights = {n: args[n] for n in SMALL + BIG}
    moms = {n: args["m_" + n] for n in SMALL + BIG}
    vars_ = {n: args["v_" + n] for n in SMALL + BIG}
    t, d = x.shape[1], x.shape[2]
    xi, yi, ci = _place()
    chip = 2 * xi + yi

    big_local = [a_w_in[0], a_w_out[0], f_w_in, f_w_out, w_kv, b_w_q[0], b_w_o[0]]
    c_arr = jnp.stack([ci, chip]).astype(jnp.int32)
    k_arr = jnp.stack([chip]).astype(jnp.int32)
    b_ain, b_aout, b_fin0, b_fin1, b_fout0, b_fout1, b_kv, b_q, b_o = cast_into_slot(big_local, k_arr, name="cast_weights")
    small_cols = _pack([a_norm, a_v_norm, f_conv_w], rows=8)
    b_small = lax.dynamic_update_slice(jnp.zeros((N_SHARDS,) + small_cols.shape, f32), small_cols[None], (chip, 0, 0))
    g_small, w_a_in, g_a_w_out = gather_shards([b_small, b_ain, b_aout], name="gather_first", split=[False, True, True])
    later, send_sems, recv_sems, token = gather_start([b_fin0, b_fout0, b_kv, b_q, b_o, b_fin1, b_fout1],
                                                      [[0], [1], [2, 3, 4], [5, 6]], g_small, name="gather_start")
    ns_cols = a_norm.shape[1]
    nf_cols = f_conv_w.shape[2]
    parts = [_unpack(g_small[k], [a_norm.shape, a_v_norm.shape, f_conv_w.shape]) for k in range(N_SHARDS)]
    a_norm_f = jnp.concatenate([p[0] for p in parts], axis=1) + token[0, 0]
    a_v_norm_f = jnp.concatenate([p[1] for p in parts], axis=1)
    conv_w_f = jnp.concatenate([p[2] for p in parts], axis=2)

    x0 = x[0]
    tril = jnp.tril(jnp.ones((CHUNK, CHUNK), dtype=bool))
    wc = jnp.where(tril[None], a_w_s[0], 0.0).astype(bf16)
    bt = a_b_s[0].T
    kg2 = jnp.tile(k_norm, 2)[None]
    qg2 = jnp.tile(b_q_norm[0], 2)[None]

    (h_a,), r_a = rms_fwd(x0, [a_norm_f], name="a_norm")
    zuv = mm_nn(h_a, w_a_in, name="a_in")
    y_a = sgu_gate_fwd(zuv, a_v_norm_f, wc, bt, name="a_gate")
    w_a_out = g_a_w_out.reshape(1, -1, d)
    f = f_w_out.shape[1] * N_SHARDS
    fwd0, tok0 = forward_start(gather_wait(later[0:1], send_sems[0], recv_sems[0], y_a, name="gather_wait_0"), y_a,
                               name="gather_forward_start_0")
    x1, (h_f0,), r_f0 = mm_residual(y_a, w_a_out[0], x0, name="a_out", gains=[_behind(f_norm[0:1], tok0)])
    (g_fin0,) = forward_wait(fwd0, x1, name="gather_forward_wait_0")
    w_f_in = [g_fin0, None]
    a0, ffn0 = _ffn_fwd(x1, f_norm[0:1], h_f0, r_f0, w_f_in[0], conv_w_f[0], f_conv_b[0], f, "0")
    (g_fout0,) = forward_halves(gather_wait(later[1:2], send_sems[1], recv_sems[1], a0, name="gather_wait_1"),
                                name="gather_forward_1")
    w_f_out = [g_fout0.reshape(-1, d), None]
    fwd1, tok1 = forward_start(gather_wait(later[2:5], send_sems[2], recv_sems[2], g_fout0, name="gather_wait_1b"), a0,
                               name="gather_forward_start_1b")
    x2, (h_k, h_q), r_b = mm_residual(a0, w_f_out[0], x1, name="ffn0_out", gains=[_behind(kv_norm[None], tok1), b_norm])
    g_w_kv, g_b_w_q, g_b_w_o = forward_wait(fwd1, x2, name="gather_forward_wait_1b")
    w_kv_f = g_w_kv.reshape(1, d, -1)
    w_q_f = g_b_w_q.reshape(1, d, -1)
    w_o_f = g_b_w_o.reshape(1, -1, d)
    kv, k2, v2 = kv_proj_post(h_k, w_kv_f[0], kg2, name="kv_proj")
    qp, qn = q_proj_norm(h_q, w_q_f[0], qg2, name="q_proj", scale=HEAD_DIM ** -0.5)
    fwd2, tok2 = forward_start(gather_wait(later[5:7], send_sems[3], recv_sems[3], qn, name="gather_wait_2"), qn,
                               name="gather_forward_start_2")
    o = attn_fwd(qn, k2, v2, _behind(b_sinks[0], tok2), name="attn")
    x3, (h_f1,), r_f1 = mm_residual(o, w_o_f[0], x2, name="o_proj", gains=[f_norm[1:2]])
    g_fin1, g_fout1 = forward_wait(fwd2, x3, name="gather_forward_wait_2")
    w_f_in[1] = g_fin1
    w_f_out[1] = g_fout1.reshape(-1, d)
    a1, ffn1 = _ffn_fwd(x3, f_norm[1:2], h_f1, r_f1, w_f_in[1], conv_w_f[1], f_conv_b[1], f, "1")
    dx4, sq = mm_residual(a1, w_f_out[1], x3, name="ffn1_out", target=loss_target[0])
    loss_part = (0.5 * jnp.sum(sq) / d).reshape(1)

    proj_rows = d // N_SHARDS
    dx3, d_fn1, d_fwin1, d_cw1, d_cb1, d_fwout1, _ = _ffn_bwd(dx4, ffn1, w_f_in[1], w_f_out[1], c_arr, "1")
    do = mm_nt([dx3], w_o_f, name="o_proj_dx")
    d_w_o = mm_tn(o, [dx3], c_arr, name="o_proj_dw", n_s=d, shard_rows=o.shape[1] // N_SHARDS)
    dqn, dk2, dv2, dsink = attn_bwd(qn, k2, v2, do, b_sinks[0], name="attn_bwd")
    dqp, dqg = q_norm_bwd(dqn, qp, qg2, name="q_norm_bwd", scale=HEAD_DIM ** -0.5)
    dkv, dkg = kv_post_bwd(dk2, dv2, kv, kg2, name="kv_post_bwd")
    d_w_q = mm_tn(h_q, [dqp], c_arr, name="q_proj_dw", n_s=w_q_f.shape[2], shard_rows=proj_rows)
    d_w_kv = mm_tn(h_k, [dkv], c_arr, name="kv_proj_dw", n_s=w_kv_f.shape[2], shard_rows=proj_rows)
    group1 = [d_fwin1, d_fwout1, d_w_kv, d_w_q, d_w_o]
    sib1, token_s1 = sibling_start([half for _, half in group1], d_w_kv[0], name="rs_sibling_start1")
    dh_k = mm_nt([dkv], w_kv_f, name="kv_proj_dx")
    dx2, d_bn, d_kvn = mm_nt_rms_bwd([dqp], w_q_f, x2, r_b, _behind(b_norm, token_s1), dx3, name="q_proj_dx",
                                     extra=(dh_k, kv_norm[None]))
    chip_bf1, own1 = _rs_front(group1, sib1, dx2, c_arr, "1")
    scatter1, token1 = chip_scatter_start(list(chip_bf1), None, dx2, name="rs_chips_start1")
    ffn0 = ffn0[:9] + (_behind(ffn0[9], token1),) + ffn0[10:]
    dx1, d_fn0, d_fwin0, d_cw0, d_cb0, d_fwout0, sib2 = _ffn_bwd(dx2, ffn0, w_f_in[0], w_f_out[0], c_arr, "0", exchange=True)
    chip_bf2, own2 = _rs_front([d_fwin0, d_fwout0], sib2, dx1, c_arr, "2")
    scatter2, token2 = chip_scatter_start(list(chip_bf2), None, dx1, name="rs_chips_start2")
    a_v_norm_f = _behind(a_v_norm_f, token2)
    dy_a = mm_nt([dx1], w_a_out, name="a_out_dx")
    d_w_aout = mm_tn(y_a, [dx1], c_arr, name="a_out_dw", n_s=d, shard_rows=y_a.shape[1] // N_SHARDS)
    dzu, dzv, d_avn, d_ws, d_bt = sgu_gate_bwd(zuv, dy_a, a_v_norm_f, wc, bt, name="a_gate_bwd")
    d_w_ain = mm_tn(h_a, [dzu, dzv], c_arr, name="a_in_dw", n_s=w_a_in.shape[2], shard_rows=d)
    sib3, token_s3 = sibling_start([d_w_ain[1], d_w_aout[1]], d_w_ain[0], name="rs_sibling_start3")
    dx0, d_an = mm_nt_rms_bwd([dzu, dzv], w_a_in, x0, r_a, _behind(a_norm_f, token_s3), dx1, name="a_in_dx")
    grad_x = dx0[None]

    chip_bf3, own3 = _rs_front([d_w_ain, d_w_aout], sib3, dx0, c_arr, "3")
    d_fn = jnp.concatenate([d_fn0, d_fn1], axis=0)
    d_cw = jnp.stack([d_cw0, d_cw1])
    d_cb = jnp.stack([d_cb0, d_cb1])
    d_kg = (dkg[0, :HEAD_DIM] + dkg[0, HEAD_DIM:])
    d_qg = (dqg[0, :HEAD_DIM] + dqg[0, HEAD_DIM:])[None]
    small_full = [d_an, d_avn, d_ws[None], d_bt.T[None], d_fn, d_cw, d_cb, d_kvn[0], d_kg, d_bn, d_qg,
                  dsink[:, :N_Q_HEADS], loss_part]
    packed = _pack(small_full)
    me = 4 * xi + 2 * yi + ci
    everyone = lax.dynamic_update_slice(lax.empty((N_DEV,) + packed.shape, f32), packed[None], (me, 0, 0))
    scatter3, token3 = chip_scatter_start(list(chip_bf3), everyone, own3[0], name="rs_chips_start3")
    from_chips1, _ = chip_scatter_wait(scatter1, token3, name="rs_chips_wait1")
    from_chips2, _ = chip_scatter_wait(scatter2, from_chips1[0], name="rs_chips_wait2")
    fin1, fout1, gkv, gq, go, fin0, fout0 = _rs_back(list(own1) + list(own2), list(from_chips1) + list(from_chips2),
                                                     c_arr, "12")
    late = ("f_w_in", "f_w_out", "w_kv", "b_w_q", "b_w_o")
    res_late = adamw([weights[n] for n in late], [[fin0, fin1], [fout0, fout1], [gkv], [gq], [go]],
                     [moms[n] for n in late], [vars_[n] for n in late], name="adamw_late")
    from_chips3, from_all = chip_scatter_wait(scatter3, res_late[1][2], name="rs_chips_wait3")
    ain, aout = _rs_back(own3, from_chips3, c_arr, "3")
    first = ("a_w_in", "a_w_out")
    res_first = adamw([weights[n] for n in first], [[ain], [aout]], [moms[n] for n in first],
                      [vars_[n] for n in first], name="adamw_first")
    big = {n: tuple(r[i] for r in res_late) for i, n in enumerate(late)}
    big.update({n: tuple(r[i] for r in res_first) for i, n in enumerate(first)})

    full_shapes = [g.shape for g in small_full]
    small_g = _unpack(sum_leading(from_all, name="small_sum"), full_shapes)
    loss = small_g.pop()[0]
    small_g[0] = lax.dynamic_slice_in_dim(small_g[0], chip * ns_cols, ns_cols, axis=1)
    small_g[1] = lax.dynamic_slice_in_dim(small_g[1], chip * ns_cols, ns_cols, axis=1)
    small_g[5] = lax.dynamic_slice_in_dim(small_g[5], chip * nf_cols, nf_cols, axis=2)
    small_shapes = [weights[n].shape for n in SMALL]
    small_g = [g.reshape(s) for g, s in zip(small_g, small_shapes)]
    flat2 = [(math.prod(s[:-1]), s[-1]) for s in small_shapes]
    small_d, small_m, small_v = adamw_small(
        *[[a.reshape(s2) for a, s2 in zip(group, flat2)]
          for group in ([weights[n] for n in SMALL], small_g, [moms[n] for n in SMALL], [vars_[n] for n in SMALL])],
        name="adamw_small")
    small_d, small_m, small_v = ([a.reshape(s) for a, s in zip(group, small_shapes)]
                                 for group in (small_d, small_m, small_v))

    out = {}
    for i, n in enumerate(SMALL):
        out[n] = (small_g[i], small_d[i], small_m[i], small_v[i])
    out.update(big)
    order = ["a_norm", "a_w_in", "a_v_norm", "a_w_s", "a_b_s", "a_w_out", "f_norm", "f_w_in", "f_conv_w", "f_conv_b",
             "f_w_out", "kv_norm", "w_kv", "k_norm", "b_norm", "b_w_q", "b_q_norm", "b_sinks", "b_w_o"]
    return (loss, grad_x, *[out[n][0] for n in order], *[out[n][1] for n in order],
            *[out[n][2] for n in order], *[out[n][3] for n in order])
```

```python
import functools
import math

import jax
import jax.numpy as jnp
from jax import lax
from jax.experimental import pallas as pl
from jax.experimental.pallas import tpu as pltpu

f32 = jnp.float32
bf16 = jnp.bfloat16
MESH = pl.DeviceIdType.MESH
ANY = pl.BlockSpec(memory_space=pl.ANY)

EPS = 1e-6
LANES = 128
CHUNK = 128
HEAD_DIM = 64
N_Q_HEADS = 16
N_KV_HEADS = 4
Q_PER_KV = N_Q_HEADS // N_KV_HEADS
N_SHARDS = 4
N_DEV = 8

ADAM_LR = 0.001
ADAM_B1 = 0.9
ADAM_B2 = 0.999
ADAM_EPS = 1e-08
ADAM_WD = 0.01
ADAM_STEP = 10
ADAM_C1 = 1.0 - ADAM_B1 ** ADAM_STEP
ADAM_C2 = 1.0 - ADAM_B2 ** ADAM_STEP

_INV_SQRT2 = 1.0 / math.sqrt(2.0)
_INV_SQRT2PI = 1.0 / math.sqrt(2.0 * math.pi)


def _params(*sem):
    return pltpu.CompilerParams(dimension_semantics=sem)


def _gelu(z):
    return 0.5 * z * (1.0 + lax.erf(z * _INV_SQRT2))


def _gelu_and_grad(z):
    cdf = 0.5 * (1.0 + lax.erf(z * _INV_SQRT2))
    return z * cdf, cdf + z * jnp.exp(-0.5 * z * z) * _INV_SQRT2PI


def _dot(a, b):
    return jnp.dot(a, b, preferred_element_type=f32)


def _dot_nt(a, b):
    return lax.dot_general(a, b, (((1,), (1,)), ((), ())), preferred_element_type=f32)


def _dot_tn(a, b):
    return lax.dot_general(a, b, (((0,), (0,)), ((), ())), preferred_element_type=f32)


def _dot_split(a, b):
    hi = a.astype(bf16)
    lo = (a - hi.astype(f32)).astype(bf16)
    return _dot(hi, b) + _dot(lo, b)


VMEM_TILE_BUDGET = 57 * 1024 * 1024
MAX_ROW_TILE = 2048


def _row_tile(m, fixed_bytes, row_bytes):
    tm = min(m, MAX_ROW_TILE)
    while tm > 256 and 2 * (fixed_bytes + tm * row_bytes) > VMEM_TILE_BUDGET:
        tm //= 2
    return tm


def _isz(a):
    return jnp.dtype(a.dtype).itemsize


def mm_nn(a, w3, *, name, s0=0, ns=None, add=None, out_dtype=f32):
    m, k = a.shape
    s_all, _, n_s = w3.shape
    ns = s_all if ns is None else ns
    tm = _row_tile(m, k * n_s * 2, k * _isz(a) + n_s * jnp.dtype(out_dtype).itemsize + (0 if add is None else n_s * 4))

    def body(*refs):
        if add is None:
            a_ref, w_ref, o_ref = refs
            acc = _dot(a_ref[...].astype(bf16), w_ref[0])
        else:
            a_ref, w_ref, add_ref, o_ref = refs
            acc = _dot(a_ref[...].astype(bf16), w_ref[0]) + add_ref[...]
        o_ref[...] = acc.astype(out_dtype)

    in_specs = [pl.BlockSpec((tm, k), lambda j, i: (i, 0)),
                pl.BlockSpec((1, k, n_s), lambda j, i: (s0 + j, 0, 0))]
    args = [a, w3]
    if add is not None:
        in_specs.append(pl.BlockSpec((tm, n_s), lambda j, i: (i, j)))
        args.append(add)
    return pl.pallas_call(
        body, name=name, grid=(ns, m // tm), in_specs=in_specs,
        out_specs=pl.BlockSpec((tm, n_s), lambda j, i: (i, j)),
        out_shape=jax.ShapeDtypeStruct((m, ns * n_s), out_dtype),
        compiler_params=_params("parallel", "parallel"))(*args)


def mm_nt(a_list, w3, *, name, tko=None, add=None, out_dtype=f32):
    s_all, k_out, n_s = w3.shape
    m = a_list[0].shape[0]
    na = len(a_list)
    spa = s_all // na
    tko = k_out if tko is None else tko
    tm = _row_tile(m, tko * n_s * 2, na * n_s * _isz(a_list[0]) + tko * 4 * (1 if add is None else 2))

    def body(*refs):
        a_refs = refs[:na]
        w_ref = refs[na]
        o_ref = refs[-1]
        s = pl.program_id(2)

        @pl.when(s == 0)
        def _():
            if add is None:
                o_ref[...] = jnp.zeros_like(o_ref)
            else:
                o_ref[...] = refs[na + 1][...]

        for idx in range(na):
            @pl.when(s // spa == idx)
            def _(idx=idx):
                o_ref[...] += _dot_nt(a_refs[idx][...].astype(bf16), w_ref[0])

    def a_map(idx):
        return lambda ko, i, s: (i, jnp.clip(s - idx * spa, 0, spa - 1))

    in_specs = [pl.BlockSpec((tm, n_s), a_map(idx)) for idx in range(na)]
    in_specs.append(pl.BlockSpec((1, tko, n_s), lambda ko, i, s: (s, ko, 0)))
    args = list(a_list) + [w3]
    if add is not None:
        in_specs.append(pl.BlockSpec((tm, tko), lambda ko, i, s: (i, ko)))
        args.append(add)
    return pl.pallas_call(
        body, name=name, grid=(k_out // tko, m // tm, s_all), in_specs=in_specs,
        out_specs=pl.BlockSpec((tm, tko), lambda ko, i, s: (i, ko)),
        out_shape=jax.ShapeDtypeStruct((m, k_out), out_dtype),
        compiler_params=_params("parallel", "parallel", "arbitrary"))(*args)


def mm_tn(a, b_list, c_arr, *, name, n_s, shard_rows, tki=None):
    m, k_in = a.shape
    na = len(b_list)
    s_all = sum(b.shape[1] for b in b_list) // n_s
    spa = s_all // na
    tki = k_in if tki is None else tki
    tm = _row_tile(m, tki * n_s * 4, tki * _isz(a) + na * n_s * _isz(b_list[0]))

    nsteps = m // tm
    per_blk = tki // shard_rows
    half = shard_rows // 2

    def body(c_ref, *refs):
        a_ref = refs[0]
        b_refs = refs[1:1 + na]
        o_ref, ob_ref = refs[-2], refs[-1]
        s = pl.program_id(0)
        r = pl.program_id(2)

        @pl.when(r == 0)
        def _():
            o_ref[...] = jnp.zeros_like(o_ref)

        for idx in range(na):
            @pl.when(s // spa == idx)
            def _(idx=idx):
                o_ref[0] += _dot_tn(a_ref[...].astype(bf16), b_refs[idx][...].astype(bf16))

        @pl.when(r == nsteps - 1)
        def _():
            for q in range(per_blk):
                start = pl.multiple_of(q * shard_rows + (1 - c_ref[0]) * half, 16)
                ob_ref[q] = o_ref[0, pl.ds(start, half), :].astype(bf16)

    def b_map(idx):
        def index(s, ki, r, c_ref):
            active = (s // spa) == idx
            return (jnp.where(active, r, 0), jnp.clip(s - idx * spa, 0, spa - 1))
        return index

    in_specs = [pl.BlockSpec((tm, tki), lambda s, ki, r, c_ref: (r, ki))]
    in_specs += [pl.BlockSpec((tm, n_s), b_map(idx)) for idx in range(na)]
    n_blk = k_in // tki
    return pl.pallas_call(
        body, name=name,
        grid_spec=pltpu.PrefetchScalarGridSpec(
            num_scalar_prefetch=1, grid=(s_all, n_blk, nsteps), in_specs=in_specs,
            out_specs=[pl.BlockSpec((1, tki, n_s), lambda s, ki, r, c_ref: (s, ki, 0)),
                       pl.BlockSpec((per_blk, half, n_s), lambda s, ki, r, c_ref: (s * n_blk + ki, 0, 0))]),
        out_shape=[jax.ShapeDtypeStruct((s_all, k_in, n_s), f32),
                   jax.ShapeDtypeStruct((s_all * k_in // shard_rows, half, n_s), bf16)],
        compiler_params=_params("parallel", "parallel", "arbitrary"))(c_arr, a, *b_list)


def mm_nt_rms_bwd(a_list, w3, x, r, g, dx_in, *, name, extra=None):
    s_all, d, n_s = w3.shape
    m = a_list[0].shape[0]
    na = len(a_list)
    spa = s_all // na
    ne = 0 if extra is None else 1
    tm = _row_tile(m, d * n_s * 2, na * n_s * _isz(a_list[0]) + d * 4 * (4 + ne))

    def body(*refs):
        a_refs, w_ref = refs[:na], refs[na]
        x_ref, r_ref, g_ref, dxin_ref = refs[na + 1:na + 5]
        dh2_ref, g2_ref = (refs[na + 5], refs[na + 6]) if ne else (None, None)
        outs = refs[na + 5 + 2 * ne:]
        dx_ref, dg_ref = outs[0], outs[1]
        dg2_ref = outs[2] if ne else None
        acc_ref = outs[-1]
        i, s = pl.program_id(0), pl.program_id(1)

        @pl.when(s == 0)
        def _():
            acc_ref[...] = jnp.zeros_like(acc_ref)

        for idx in range(na):
            @pl.when(s // spa == idx)
            def _(idx=idx):
                acc_ref[...] += _dot_nt(a_refs[idx][...].astype(bf16), w_ref[0])

        @pl.when(s == s_all - 1)
        def _():
            rv = r_ref[...]
            xh = x_ref[...] * rv
            total = dxin_ref[...]
            pairs = [(acc_ref[...], g_ref, dg_ref)] + ([(dh2_ref[...], g2_ref, dg2_ref)] if ne else [])
            for dh, gain_ref, dgain_ref in pairs:
                part = jnp.sum(dh * xh, axis=0, keepdims=True)

                @pl.when(i == 0)
                def _(dgain_ref=dgain_ref, part=part):
                    dgain_ref[...] = part

                @pl.when(i > 0)
                def _(dgain_ref=dgain_ref, part=part):
                    dgain_ref[...] += part

                tg = dh * gain_ref[...]
                total = total + rv * (tg - xh * jnp.mean(tg * xh, axis=1, keepdims=True))
            dx_ref[...] = total

    def a_map(idx):
        return lambda i, s: (i, jnp.clip(s - idx * spa, 0, spa - 1))

    row = pl.BlockSpec((tm, d), lambda i, s: (i, 0))
    vec = pl.BlockSpec((1, d), lambda i, s: (0, 0))
    in_specs = [pl.BlockSpec((tm, n_s), a_map(idx)) for idx in range(na)]
    in_specs += [pl.BlockSpec((1, d, n_s), lambda i, s: (s, 0, 0)), row, pl.BlockSpec((tm, 1), lambda i, s: (i, 0)), vec, row]
    args = list(a_list) + [w3, x, r, g, dx_in]
    if ne:
        in_specs += [row, vec]
        args += list(extra)
    outs = pl.pallas_call(
        body, name=name, grid=(m // tm, s_all), in_specs=in_specs, out_specs=[row] + [vec] * (1 + ne),
        out_shape=[jax.ShapeDtypeStruct((m, d), f32)] + [jax.ShapeDtypeStruct((1, d), f32)] * (1 + ne),
        scratch_shapes=[pltpu.VMEM((tm, d), f32)],
        compiler_params=_params("arbitrary", "arbitrary"))(*args)
    return outs


def mm_residual(a, w, x, *, name, gains=(), target=None):
    m, k = a.shape
    d = w.shape[1]
    ng = len(gains)
    tm = _row_tile(m, k * d * 2, k * _isz(a) + d * 4 * 3 + ng * d * 2)

    def body(*refs):
        a_ref, w_ref, x_ref = refs[:3]
        y = _dot(a_ref[...].astype(bf16), w_ref[...]) + x_ref[...]
        if target is None:
            g_refs = refs[3:3 + ng]
            y_ref = refs[3 + ng]
            h_refs = refs[4 + ng:4 + 2 * ng]
            r_ref = refs[-1]
            y_ref[...] = y
            r = lax.rsqrt(jnp.mean(y * y, axis=1, keepdims=True) + EPS)
            yh = y * r
            for g_ref, h_ref in zip(g_refs, h_refs):
                h_ref[...] = (yh * g_ref[...]).astype(bf16)
            r_ref[...] = r
        else:
            t_ref, dy_ref, s_ref = refs[3:]
            i = pl.program_id(0)
            e = y - t_ref[...]
            dy_ref[...] = e * (1.0 / d)
            part = jnp.sum(e * e, axis=0, keepdims=True)

            @pl.when(i == 0)
            def _():
                s_ref[...] = part

            @pl.when(i > 0)
            def _():
                s_ref[...] += part

    row = pl.BlockSpec((tm, d), lambda i: (i, 0))
    vec = pl.BlockSpec((1, d), lambda i: (0, 0))
    in_specs = [pl.BlockSpec((tm, k), lambda i: (i, 0)), pl.BlockSpec((k, d), lambda i: (0, 0)), row]
    if target is None:
        outs = pl.pallas_call(
            body, name=name, grid=(m // tm,), in_specs=in_specs + [vec] * ng,
            out_specs=[row] * (1 + ng) + [pl.BlockSpec((tm, 1), lambda i: (i, 0))],
            out_shape=[jax.ShapeDtypeStruct((m, d), f32)] + [jax.ShapeDtypeStruct((m, d), bf16)] * ng
            + [jax.ShapeDtypeStruct((m, 1), f32)],
            compiler_params=_params("parallel"))(a, w, x, *gains)
        return outs[0], outs[1:1 + ng], outs[-1]
    return pl.pallas_call(
        body, name=name, grid=(m // tm,), in_specs=in_specs + [row], out_specs=[row, vec],
        out_shape=[jax.ShapeDtypeStruct((m, d), f32), jax.ShapeDtypeStruct((1, d), f32)],
        compiler_params=_params("arbitrary"))(a, w, x, target)


def rms_fwd(x, gains, *, name, tr=512):
    t, d = x.shape
    tr = min(tr, t)
    ng = len(gains)

    def body(*refs):
        x_ref = refs[0]
        g_refs = refs[1:1 + ng]
        h_refs = refs[1 + ng:1 + 2 * ng]
        r_ref = refs[-1]
        xv = x_ref[...]
        r = lax.rsqrt(jnp.mean(xv * xv, axis=1, keepdims=True) + EPS)
        xh = xv * r
        for g_ref, h_ref in zip(g_refs, h_refs):
            h_ref[...] = (xh * g_ref[...]).astype(bf16)
        r_ref[...] = r

    row = pl.BlockSpec((tr, d), lambda i: (i, 0))
    vec = pl.BlockSpec((1, d), lambda i: (0, 0))
    outs = pl.pallas_call(
        body, name=name, grid=(t // tr,), in_specs=[row] + [vec] * ng,
        out_specs=[row] * ng + [pl.BlockSpec((tr, 1), lambda i: (i, 0))],
        out_shape=[jax.ShapeDtypeStruct((t, d), bf16)] * ng + [jax.ShapeDtypeStruct((t, 1), f32)],
        compiler_params=_params("parallel"))(x, *gains)
    return outs[:ng], outs[ng]


def sgu_gate_fwd(zuv, gv, wc, bt, *, name, tr=512):
    t, w = zuv.shape[0], zuv.shape[1] // 2
    tr = min(tr, t)
    groups = w // LANES

    def body(zu_ref, zv_ref, gv_ref, wc_ref, bt_ref, y_ref):
        vp = _gelu(zv_ref[...])
        rv = lax.rsqrt(jnp.mean(vp * vp, axis=1, keepdims=True) + EPS)
        vb = (vp * rv * gv_ref[...]).astype(bf16)
        for c in range(tr // CHUNK):
            rows = slice(c * CHUNK, (c + 1) * CHUNK)
            for g in range(groups):
                cols = slice(g * LANES, (g + 1) * LANES)
                sv = _dot(wc_ref[g], vb[rows, cols]) + bt_ref[:, g:g + 1]
                y_ref[rows, cols] = (_gelu(zu_ref[rows, cols]) * sv).astype(bf16)

    row = pl.BlockSpec((tr, w), lambda i: (i, 0))
    return pl.pallas_call(
        body, name=name, grid=(t // tr,),
        in_specs=[row, pl.BlockSpec((tr, w), lambda i: (i, 1)), pl.BlockSpec((1, w), lambda i: (0, 0)),
                  pl.BlockSpec((groups, CHUNK, CHUNK), lambda i: (0, 0, 0)),
                  pl.BlockSpec((CHUNK, groups), lambda i: (0, 0))],
        out_specs=row, out_shape=jax.ShapeDtypeStruct((t, w), bf16),
        compiler_params=_params("parallel"))(zuv, zuv, gv, wc, bt)


def sgu_gate_bwd(zuv, dy, gv, wc, bt, *, name, tr=512):
    t, w = zuv.shape[0], zuv.shape[1] // 2
    tr = min(tr, t)
    groups = w // LANES
    nsteps = t // tr

    def body(zu_ref, zv_ref, dy_ref, gv_ref, wc_ref, bt_ref,
             dzu_ref, dzv_ref, dgv_ref, dws_ref, dbt_ref, dv_ref, bacc_ref):
        i = pl.program_id(0)

        @pl.when(i == 0)
        def _():
            dgv_ref[...] = jnp.zeros_like(dgv_ref)
            dws_ref[...] = jnp.zeros_like(dws_ref)
            bacc_ref[...] = jnp.zeros_like(bacc_ref)

        vp, vp_grad = _gelu_and_grad(zv_ref[...])
        rv = lax.rsqrt(jnp.mean(vp * vp, axis=1, keepdims=True) + EPS)
        vhat = vp * rv
        vb = (vhat * gv_ref[...]).astype(bf16)
        for c in range(tr // CHUNK):
            rows = slice(c * CHUNK, (c + 1) * CHUNK)
            for g in range(groups):
                cols = slice(g * LANES, (g + 1) * LANES)
                vblk = vb[rows, cols]
                sv = _dot(wc_ref[g], vblk) + bt_ref[:, g:g + 1]
                zub = zu_ref[rows, cols]
                dyb = dy_ref[rows, cols]
                ub, ub_grad = _gelu_and_grad(zub)
                dzu_ref[rows, cols] = (dyb * sv * ub_grad).astype(bf16)
                dsv = dyb * ub
                bacc_ref[:, cols] += dsv
                dsvb = dsv.astype(bf16)
                dv_ref[rows, cols] = _dot_tn(wc_ref[g], dsvb)
                dws_ref[g] += _dot_nt(dsvb, vblk)
        dv = dv_ref[...]
        dgv_ref[...] += jnp.sum(dv * vhat, axis=0, keepdims=True)
        tg = dv * gv_ref[...]
        dvp = rv * (tg - vhat * jnp.mean(tg * vhat, axis=1, keepdims=True))
        dzv_ref[...] = (dvp * vp_grad).astype(bf16)

        @pl.when(i == nsteps - 1)
        def _():
            tt = lax.broadcasted_iota(jnp.int32, (CHUNK, CHUNK), 0)
            ss = lax.broadcasted_iota(jnp.int32, (CHUNK, CHUNK), 1)
            for g in range(groups):
                dws_ref[g] = jnp.where(ss <= tt, dws_ref[g], 0.0)
                dbt_ref[:, g:g + 1] = jnp.sum(bacc_ref[:, g * LANES:(g + 1) * LANES], axis=1, keepdims=True)

    row = pl.BlockSpec((tr, w), lambda i: (i, 0))
    full3 = pl.BlockSpec((groups, CHUNK, CHUNK), lambda i: (0, 0, 0))
    return pl.pallas_call(
        body, name=name, grid=(nsteps,),
        in_specs=[row, pl.BlockSpec((tr, w), lambda i: (i, 1)), row, pl.BlockSpec((1, w), lambda i: (0, 0)), full3,
                  pl.BlockSpec((CHUNK, groups), lambda i: (0, 0))],
        out_specs=[row, row, pl.BlockSpec((1, w), lambda i: (0, 0)), full3,
                   pl.BlockSpec((CHUNK, groups), lambda i: (0, 0))],
        out_shape=[jax.ShapeDtypeStruct((t, w), bf16), jax.ShapeDtypeStruct((t, w), bf16),
                   jax.ShapeDtypeStruct((1, w), f32), jax.ShapeDtypeStruct((groups, CHUNK, CHUNK), f32),
                   jax.ShapeDtypeStruct((CHUNK, groups), f32)],
        scratch_shapes=[pltpu.VMEM((tr, w), f32), pltpu.VMEM((CHUNK, w), f32)],
        compiler_params=_params("arbitrary"))(zuv, zuv, dy, gv, wc, bt)


HALO = 8


def _shift_down(v, halo, k, first):
    r = pltpu.roll(v, k, 0)
    hh = jnp.where(first, 0.0, pltpu.roll(halo, k, 0))
    rid = lax.broadcasted_iota(jnp.int32, (HALO, v.shape[1]), 0)
    head = jnp.where(rid < k, hh, r[0:HALO])
    if v.shape[0] == HALO:
        return head
    return jnp.concatenate([head, r[HALO:]], axis=0)


def _shift_up(v, halo, k, last):
    n = v.shape[0]
    r = pltpu.roll(v, n - k, 0)
    hh = jnp.where(last, 0.0, pltpu.roll(halo, HALO - k, 0))
    rid = lax.broadcasted_iota(jnp.int32, (HALO, v.shape[1]), 0)
    tail = jnp.where(rid >= HALO - k, hh, r[n - HALO:])
    return jnp.concatenate([r[:n - HALO], tail], axis=0)


def _conv(p, halo, w_ref, b_ref, first):
    return (w_ref[2:3, :] * p + w_ref[1:2, :] * _shift_down(p, halo, 1, first)
            + w_ref[0:1, :] * _shift_down(p, halo, 2, first) + b_ref[...])


BF16_ROWS = 16


def ffn_in_fused(h, w_in4, wg, wu, bg, bu, *, name):
    t, k = h.shape
    s_all, _, n_s = w_in4.shape
    half = s_all // 2
    tm = _row_tile(t, 2 * k * n_s * 2, k * 2 + 4 * n_s * 4 + n_s * 2)

    def body(h_ref, hh_ref, wg_ref, wu_ref, cg_ref, cu_ref, bg_ref, bu_ref, pg_ref, pu_ref, gate_ref, up_ref, a_ref):
        first = pl.program_id(1) == 0
        hv, hh = h_ref[...], hh_ref[...]
        outs = []
        for w_ref, c_ref, b_ref, p_ref, o_ref in ((wg_ref, cg_ref, bg_ref, pg_ref, gate_ref),
                                                  (wu_ref, cu_ref, bu_ref, pu_ref, up_ref)):
            p = _dot(hv, w_ref[0])
            p_ref[...] = p
            hu = _conv(p, _dot(hh, w_ref[0])[BF16_ROWS - HALO:], c_ref, b_ref, first)
            o_ref[...] = hu
            outs.append(hu)
        gate, up = outs
        a_ref[...] = (gate * jax.nn.sigmoid(gate) * up).astype(bf16)

    tile = pl.BlockSpec((tm, n_s), lambda j, i: (i, j))
    cw = pl.BlockSpec((3, n_s), lambda j, i: (0, j))
    cb = pl.BlockSpec((1, n_s), lambda j, i: (0, j))
    f = half * n_s
    return pl.pallas_call(
        body, name=name, grid=(half, t // tm),
        in_specs=[pl.BlockSpec((tm, k), lambda j, i: (i, 0)),
                  pl.BlockSpec((BF16_ROWS, k), lambda j, i: (jnp.maximum(i * (tm // BF16_ROWS) - 1, 0), 0)),
                  pl.BlockSpec((1, k, n_s), lambda j, i: (j, 0, 0)),
                  pl.BlockSpec((1, k, n_s), lambda j, i: (j + half, 0, 0)), cw, cw, cb, cb],
        out_specs=[tile] * 5,
        out_shape=[jax.ShapeDtypeStruct((t, f), f32)] * 4 + [jax.ShapeDtypeStruct((t, f), bf16)],
        compiler_params=_params("parallel", "parallel"))(h, h, w_in4, w_in4, wg, wu, bg, bu)


def _gate_grads(gate, up, dav):
    sg = jax.nn.sigmoid(gate)
    return dav * up * (sg * (1.0 + gate * (1.0 - sg))), dav * gate * sg


GATE_BWD_ROWS = 512


def ffn_gate_bwd(dy, w_out, pg, pu, gate, up, wg, wu, *, name):
    t, f = pg.shape
    d = dy.shape[1]
    tr = min(GATE_BWD_ROWS, t)
    nsteps = t // tr
    tc = f // 2

    def body(dy_ref, dyn_ref, w_ref, pg_ref, pu_ref, gate_ref, gaten_ref, up_ref, upn_ref, wg_ref, wu_ref,
             dg_ref, du_ref, sg_ref, su_ref):
        i = pl.program_id(1)
        last = i == nsteps - 1
        w = w_ref[0]
        da = _dot_nt(dy_ref[...].astype(bf16), w)
        da_n = _dot_nt(dyn_ref[...].astype(bf16), w)
        dgate, dup = _gate_grads(gate_ref[...], up_ref[...], da)
        dgate_n, dup_n = _gate_grads(gaten_ref[...], upn_ref[...], da_n)
        rid = lax.broadcasted_iota(jnp.int32, (8, tc), 0)
        for dd, d_n, c_ref, p_ref, o_ref, s_ref in ((dgate, dgate_n, wg_ref, pg_ref, dg_ref, sg_ref),
                                                    (dup, dup_n, wu_ref, pu_ref, du_ref, su_ref)):
            d1, d2 = _shift_up(dd, d_n, 1, last), _shift_up(dd, d_n, 2, last)
            o_ref[...] = (c_ref[2:3, :] * dd + c_ref[1:2, :] * d1 + c_ref[0:1, :] * d2).astype(bf16)
            p = p_ref[...]
            sums = [jnp.sum(d2 * p, axis=0, keepdims=True), jnp.sum(d1 * p, axis=0, keepdims=True),
                    jnp.sum(dd * p, axis=0, keepdims=True), jnp.sum(dd, axis=0, keepdims=True)]
            part = jnp.zeros((8, tc), f32)
            for k, sk in enumerate(sums):
                part = jnp.where(rid == k, sk, part)

            @pl.when(i == 0)
            def _(s_ref=s_ref, part=part):
                s_ref[...] = part

            @pl.when(i > 0)
            def _(s_ref=s_ref, part=part):
                s_ref[...] += part

    def nxt_rows(j, i):
        return (jnp.minimum((i + 1) * (tr // HALO), t // HALO - 1), j)

    tile = pl.BlockSpec((tr, tc), lambda j, i: (i, j))
    nxt = pl.BlockSpec((HALO, tc), nxt_rows)
    wspec = pl.BlockSpec((3, tc), lambda j, i: (0, j))
    stat = pl.BlockSpec((8, tc), lambda j, i: (0, j))
    return pl.pallas_call(
        body, name=name, grid=(2, nsteps),
        in_specs=[pl.BlockSpec((tr, d), lambda j, i: (i, 0)),
                  pl.BlockSpec((HALO, d), lambda j, i: (nxt_rows(j, i)[0], 0)),
                  pl.BlockSpec((1, tc, d), lambda j, i: (j, 0, 0)),
                  tile, tile, tile, nxt, tile, nxt, wspec, wspec],
        out_specs=[tile, tile, stat, stat],
        out_shape=[jax.ShapeDtypeStruct((t, f), bf16), jax.ShapeDtypeStruct((t, f), bf16),
                   jax.ShapeDtypeStruct((8, f), f32), jax.ShapeDtypeStruct((8, f), f32)],
        compiler_params=_params("parallel", "arbitrary"))(
            dy, dy, w_out.reshape(2, tc, d), pg, pu, gate, gate, up, up, wg, wu)


def _head_mean_matrix():
    i = lax.broadcasted_iota(jnp.int32, (LANES, LANES), 0) // HEAD_DIM
    j = lax.broadcasted_iota(jnp.int32, (LANES, LANES), 1) // HEAD_DIM
    return jnp.where(i == j, 1.0 / HEAD_DIM, 0.0).astype(bf16)


def _lane_half(shape):
    return (lax.broadcasted_iota(jnp.int32, shape, 1) % LANES) // HEAD_DIM


def q_proj_norm(h, w, g2, *, name, scale):
    t, k = h.shape
    n = w.shape[1]
    tm = _row_tile(t, k * n * 2, k * 2 + n * 4 + n * 2)

    def body(h_ref, w_ref, g_ref, qp_ref, o_ref):
        qp_ref[...] = _dot(h_ref[...], w_ref[...])
        bd = _head_mean_matrix()
        for cb in range(n // LANES):
            cols = slice(cb * LANES, (cb + 1) * LANES)
            xc = qp_ref[:, cols]
            rh = lax.rsqrt(_dot_split(xc * xc, bd) + EPS)
            o_ref[:, cols] = (xc * rh * g_ref[...] * scale).astype(bf16)

    row = pl.BlockSpec((tm, n), lambda i: (i, 0))
    return pl.pallas_call(
        body, name=name, grid=(t // tm,),
        in_specs=[pl.BlockSpec((tm, k), lambda i: (i, 0)), pl.BlockSpec((k, n), lambda i: (0, 0)),
                  pl.BlockSpec((1, LANES), lambda i: (0, 0))],
        out_specs=[row, row], out_shape=[jax.ShapeDtypeStruct((t, n), f32), jax.ShapeDtypeStruct((t, n), bf16)],
        compiler_params=_params("parallel"))(h, w, g2)


def kv_proj_post(h, w, g2, *, name):
    t, k = h.shape
    n = w.shape[1]
    kw = n // 2
    tm = _row_tile(t, k * n * 2, k * 2 + n * 4 + 2 * n * 2)

    def body(h_ref, w_ref, g_ref, kv_ref, k_ref, v_ref):
        kv_ref[...] = _dot(h_ref[...], w_ref[...])
        bd = _head_mean_matrix()
        half = _lane_half((tm, LANES))
        for cb in range(kw // LANES):
            xc = kv_ref[:, cb * LANES:(cb + 1) * LANES]
            rh = lax.rsqrt(_dot_split(xc * xc, bd) + EPS)
            kn = xc * rh * g_ref[...]
            vc = kv_ref[:, kw + cb * LANES:kw + (cb + 1) * LANES]
            for src, dst in ((kn, k_ref), (vc, v_ref)):
                sw = pltpu.roll(src, HEAD_DIM, 1)
                for hf in range(2):
                    blk = 2 * cb + hf
                    dst[:, blk * LANES:(blk + 1) * LANES] = jnp.where(half == hf, src, sw).astype(bf16)

    row = pl.BlockSpec((tm, n), lambda i: (i, 0))
    return pl.pallas_call(
        body, name=name, grid=(t // tm,),
        in_specs=[pl.BlockSpec((tm, k), lambda i: (i, 0)), pl.BlockSpec((k, n), lambda i: (0, 0)),
                  pl.BlockSpec((1, LANES), lambda i: (0, 0))],
        out_specs=[row, row, row],
        out_shape=[jax.ShapeDtypeStruct((t, n), f32), jax.ShapeDtypeStruct((t, n), bf16), jax.ShapeDtypeStruct((t, n), bf16)],
        compiler_params=_params("parallel"))(h, w, g2)


def q_norm_bwd(dq, qp, g2, *, name, scale, tr=512):
    t, w = qp.shape
    tr = min(tr, t)

    def body(dq_ref, x_ref, g_ref, o_ref, dg_ref):
        i = pl.program_id(0)
        bd = _head_mean_matrix()
        acc = jnp.zeros((1, LANES), f32)
        for cb in range(w // LANES):
            cols = slice(cb * LANES, (cb + 1) * LANES)
            xc = x_ref[:, cols]
            rh = lax.rsqrt(_dot_split(xc * xc, bd) + EPS)
            xh = xc * rh
            dy = dq_ref[:, cols] * scale
            acc = acc + jnp.sum(dy * xh, axis=0, keepdims=True)
            tg = dy * g_ref[...]
            o_ref[:, cols] = (rh * (tg - xh * _dot_split(tg * xh, bd))).astype(bf16)

        @pl.when(i == 0)
        def _():
            dg_ref[...] = acc

        @pl.when(i > 0)
        def _():
            dg_ref[...] += acc

    row = pl.BlockSpec((tr, w), lambda i: (i, 0))
    vec = pl.BlockSpec((1, LANES), lambda i: (0, 0))
    return pl.pallas_call(
        body, name=name, grid=(t // tr,), in_specs=[row, row, vec], out_specs=[row, vec],
        out_shape=[jax.ShapeDtypeStruct((t, w), bf16), jax.ShapeDtypeStruct((1, LANES), f32)],
        compiler_params=_params("arbitrary"))(dq, qp, g2)


def kv_post_bwd(dk2, dv2, kv, g2, *, name, tr=512):
    t, w = kv.shape
    tr = min(tr, t)
    kw = w // 2

    def body(dk_ref, dv_ref, x_ref, g_ref, o_ref, dg_ref):
        i = pl.program_id(0)
        bd = _head_mean_matrix()
        half = _lane_half((tr, LANES))
        acc = jnp.zeros((1, LANES), f32)

        def fold(ref, cb):
            a = ref[:, (2 * cb) * LANES:(2 * cb + 1) * LANES]
            b = ref[:, (2 * cb + 1) * LANES:(2 * cb + 2) * LANES]
            return jnp.where(half == 0, a + pltpu.roll(a, HEAD_DIM, 1), b + pltpu.roll(b, HEAD_DIM, 1))

        for cb in range(kw // LANES):
            cols = slice(cb * LANES, (cb + 1) * LANES)
            xc = x_ref[:, cols]
            rh = lax.rsqrt(_dot_split(xc * xc, bd) + EPS)
            xh = xc * rh
            dy = fold(dk_ref, cb)
            acc = acc + jnp.sum(dy * xh, axis=0, keepdims=True)
            tg = dy * g_ref[...]
            o_ref[:, cols] = (rh * (tg - xh * _dot_split(tg * xh, bd))).astype(bf16)
            o_ref[:, kw + cb * LANES:kw + (cb + 1) * LANES] = fold(dv_ref, cb).astype(bf16)

        @pl.when(i == 0)
        def _():
            dg_ref[...] = acc

        @pl.when(i > 0)
        def _():
            dg_ref[...] += acc

    dup = pl.BlockSpec((tr, 2 * kw), lambda i: (i, 0))
    row = pl.BlockSpec((tr, w), lambda i: (i, 0))
    vec = pl.BlockSpec((1, LANES), lambda i: (0, 0))
    return pl.pallas_call(
        body, name=name, grid=(t // tr,), in_specs=[dup, dup, row, vec], out_specs=[row, vec],
        out_shape=[jax.ShapeDtypeStruct((t, w), bf16), jax.ShapeDtypeStruct((1, LANES), f32)],
        compiler_params=_params("arbitrary"))(dk2, dv2, kv, g2)


def _slope(h):
    return 2.0 ** (-8.0 * (h + 1) / N_Q_HEADS)


GROUP_ROWS = Q_PER_KV * CHUNK


def _band_mask(n):
    tq = lax.broadcasted_iota(jnp.int32, (GROUP_ROWS, 2 * CHUNK), 0) % CHUNK
    jk = lax.broadcasted_iota(jnp.int32, (GROUP_ROWS, 2 * CHUNK), 1)
    dist = tq + CHUNK - jk
    ok = (dist >= 0) & (dist < CHUNK) & jnp.logical_not((n == 0) & (jk < CHUNK))
    return dist.astype(f32), ok


def _band(ref, n, kh):
    p0 = pl.multiple_of(jnp.maximum(n - 1, 0) * CHUNK, CHUNK)
    c0 = pl.multiple_of(n * CHUNK, CHUNK)
    cols = slice(kh * LANES, (kh + 1) * LANES)
    return jnp.concatenate([ref[pl.ds(p0, CHUNK), cols], ref[pl.ds(c0, CHUNK), cols]], axis=0)


def _stack_heads(ref, kh, half):
    parts = []
    for cb in (2 * kh, 2 * kh + 1):
        xc = ref[:, cb * LANES:(cb + 1) * LANES].astype(f32)
        parts += [jnp.where(half == hf, xc, 0.0).astype(bf16) for hf in range(2)]
    return jnp.concatenate(parts, axis=0)


def _unstack_heads(x4, half):
    return (jnp.where(half == 0, x4[0:CHUNK], x4[CHUNK:2 * CHUNK]),
            jnp.where(half == 0, x4[2 * CHUNK:3 * CHUNK], x4[3 * CHUNK:]))


def _per_head_column(kh, values):
    grp = lax.broadcasted_iota(jnp.int32, (GROUP_ROWS, 1), 0) // CHUNK
    col = jnp.full((GROUP_ROWS, 1), values[0], f32)
    for g in range(1, Q_PER_KV):
        col = jnp.where(grp == g, values[g], col)
    return col


def _softmax_band(q4, kband, dist, ok, slope, sink):
    s = _dot_nt(q4, kband)
    s = jnp.where(ok, s - slope * dist, -jnp.inf)
    m = jnp.maximum(jnp.max(s, axis=1, keepdims=True), sink)
    e = jnp.exp(s - m)
    es = jnp.exp(sink - m)
    den = jnp.sum(e, axis=1, keepdims=True) + es
    return e / den, es / den


def attn_fwd(q, k2, v2, sinks, *, name):
    t, w = q.shape
    nb = t // CHUNK

    def body(sink_ref, q_ref, k_ref, v_ref, o_ref):
        n = pl.program_id(0)
        dist, ok = _band_mask(n)
        half = _lane_half((CHUNK, LANES))
        khs = range(N_KV_HEADS)
        heads = [[Q_PER_KV * kh + g for g in range(Q_PER_KV)] for kh in khs]
        q4 = [_stack_heads(q_ref, kh, half) for kh in khs]
        soft = [_softmax_band(q4[kh], _band(k_ref, n, kh), dist, ok, _per_head_column(kh, [_slope(h) for h in heads[kh]]),
                              _per_head_column(kh, [sink_ref[h] for h in heads[kh]])) for kh in khs]
        o4 = [_dot(soft[kh][0].astype(bf16), _band(v_ref, n, kh)) for kh in khs]
        for kh in khs:
            lo, hi = _unstack_heads(o4[kh], half)
            o_ref[:, (2 * kh) * LANES:(2 * kh + 1) * LANES] = lo.astype(bf16)
            o_ref[:, (2 * kh + 1) * LANES:(2 * kh + 2) * LANES] = hi.astype(bf16)

    full = pl.BlockSpec((t, k2.shape[1]), lambda n: (0, 0))
    return pl.pallas_call(
        body, name=name, grid=(nb,),
        in_specs=[pl.BlockSpec(memory_space=pltpu.SMEM), pl.BlockSpec((CHUNK, w), lambda n: (n, 0)), full, full],
        out_specs=pl.BlockSpec((CHUNK, w), lambda n: (n, 0)),
        out_shape=jax.ShapeDtypeStruct((t, w), bf16),
        compiler_params=_params("parallel"))(sinks, q, k2, v2)


def attn_bwd(q, k2, v2, do, sinks, *, name):
    t, w = q.shape
    nb = t // CHUNK
    kw = k2.shape[1]

    def body(sink_ref, q_ref, k_ref, v_ref, do_ref, dq_ref, dk_ref, dv_ref, ds_ref, kc_ref, vc_ref):
        n = pl.program_id(0)

        @pl.when(n == 0)
        def _():
            ds_ref[...] = jnp.zeros_like(ds_ref)
            kc_ref[...] = jnp.zeros_like(kc_ref)
            vc_ref[...] = jnp.zeros_like(vc_ref)
            dk_ref[...] = jnp.zeros_like(dk_ref)
            dv_ref[...] = jnp.zeros_like(dv_ref)

        @pl.when(n == nb)
        def _():
            dk_ref[...] = kc_ref[...]
            dv_ref[...] = vc_ref[...]

        @pl.when(n < nb)
        def _():
            dist, ok = _band_mask(n)
            half = _lane_half((CHUNK, LANES))
            lane = lax.broadcasted_iota(jnp.int32, (1, LANES), 1)
            sink_acc = jnp.zeros((1, LANES), f32)
            khs = range(N_KV_HEADS)
            heads = [[Q_PER_KV * kh + g for g in range(Q_PER_KV)] for kh in khs]
            q4 = [_stack_heads(q_ref, kh, half) for kh in khs]
            do4 = [_stack_heads(do_ref, kh, half) for kh in khs]
            kband = [_band(k_ref, n, kh) for kh in khs]
            vband = [_band(v_ref, n, kh) for kh in khs]
            soft = [_softmax_band(q4[kh], kband[kh], dist, ok, _per_head_column(kh, [_slope(h) for h in heads[kh]]),
                                  _per_head_column(kh, [sink_ref[h] for h in heads[kh]])) for kh in khs]
            dp = [_dot_nt(do4[kh], vband[kh]) for kh in khs]
            delta = [jnp.sum(soft[kh][0] * dp[kh], axis=1, keepdims=True) for kh in khs]
            dsb = [(soft[kh][0] * (dp[kh] - delta[kh])).astype(bf16) for kh in khs]
            dq4 = [_dot(dsb[kh], kband[kh]) for kh in khs]
            dkb = [_dot_tn(dsb[kh], q4[kh]) for kh in khs]
            dvb = [_dot_tn(soft[kh][0].astype(bf16), do4[kh]) for kh in khs]
            for kh in khs:
                sd = soft[kh][1] * delta[kh]
                for g, h in enumerate(heads[kh]):
                    part = jnp.sum(sd[g * CHUNK:(g + 1) * CHUNK], axis=0, keepdims=True)
                    sink_acc = sink_acc + jnp.where(lane == h, -part, 0.0)
                lo, hi = _unstack_heads(dq4[kh], half)
                dq_ref[:, (2 * kh) * LANES:(2 * kh + 1) * LANES] = lo
                dq_ref[:, (2 * kh + 1) * LANES:(2 * kh + 2) * LANES] = hi
                cols = slice(kh * LANES, (kh + 1) * LANES)
                dk_ref[:, cols] = kc_ref[:, cols] + dkb[kh][0:CHUNK]
                dv_ref[:, cols] = vc_ref[:, cols] + dvb[kh][0:CHUNK]
                kc_ref[:, cols] = dkb[kh][CHUNK:]
                vc_ref[:, cols] = dvb[kh][CHUNK:]
            ds_ref[...] += sink_acc

    full = pl.BlockSpec((t, kw), lambda n: (0, 0))
    qblk = pl.BlockSpec((CHUNK, w), lambda n: (jnp.minimum(n, nb - 1), 0))
    kblk = pl.BlockSpec((CHUNK, kw), lambda n: (jnp.maximum(n - 1, 0), 0))
    return pl.pallas_call(
        body, name=name, grid=(nb + 1,),
        in_specs=[pl.BlockSpec(memory_space=pltpu.SMEM), qblk, full, full, qblk],
        out_specs=[qblk, kblk, kblk, pl.BlockSpec((1, LANES), lambda n: (0, 0))],
        out_shape=[jax.ShapeDtypeStruct((t, w), f32), jax.ShapeDtypeStruct((t, kw), f32),
                   jax.ShapeDtypeStruct((t, kw), f32), jax.ShapeDtypeStruct((1, LANES), f32)],
        scratch_shapes=[pltpu.VMEM((CHUNK, kw), f32), pltpu.VMEM((CHUNK, kw), f32)],
        compiler_params=_params("arbitrary"))(sinks, q, k2, v2, do)


N_STEPS = 8


def _row_blocks(shape):
    if len(shape) == 2:
        r, c = shape
        return (r // N_STEPS, c), (lambda s: (s, 0))
    l, r, c = shape
    per = N_STEPS // l
    return (1, r // per, c), (lambda s: (s // per, s % per, 0))


CAST_STEPS = 4


def cast_into_slot(arrays, k_arr, *, name):
    in_specs, out_specs, out_shape, layers = [], [], [], []
    for a in arrays:
        r, c = a.shape[-2:]
        rb = r // CAST_STEPS
        if a.ndim == 2:
            in_specs.append(pl.BlockSpec((rb, c), lambda s, k: (s, 0)))
            layers.append(None)
        else:
            for l in range(a.shape[0]):
                in_specs.append(pl.BlockSpec((1, rb, c), lambda s, k, l=l: (l, s, 0)))
                layers.append(l)
        for _ in range(1 if a.ndim == 2 else a.shape[0]):
            out_specs.append(pl.BlockSpec((1, rb, c), lambda s, k: (k[0], s, 0)))
            out_shape.append(jax.ShapeDtypeStruct((N_SHARDS, r, c), bf16))
    n = len(in_specs)

    def body(k_ref, *refs):
        for i_ref, o_ref, l in zip(refs[:n], refs[n:], layers):
            o_ref[0] = (i_ref[...] if l is None else i_ref[0]).astype(bf16)

    args = []
    for a in arrays:
        args += [a] * (1 if a.ndim == 2 else a.shape[0])
    return pl.pallas_call(
        body, name=name,
        grid_spec=pltpu.PrefetchScalarGridSpec(num_scalar_prefetch=1, grid=(CAST_STEPS,),
                                               in_specs=in_specs, out_specs=out_specs),
        out_shape=out_shape, compiler_params=_params("parallel"))(k_arr, *args)


def adamw(ws, gs, ms, vs, *, name):
    n = len(ws)
    specs, g_specs, g_count = [], [], []
    for w, g_list in zip(ws, gs):
        blk, index = _row_blocks(w.shape)
        specs.append(pl.BlockSpec(blk, index))
        layers = len(g_list)
        per = N_STEPS // layers
        g_count.append(layers)
        for l in range(layers):
            g_specs.append(pl.BlockSpec(blk[-2:], lambda s, l=l, per=per: (jnp.where(s // per == l, s % per, 0), 0)))
    ng = len(g_specs)

    def body(*refs):
        s = pl.program_id(0)
        g_refs = refs[3 * n:3 * n + ng]
        outs = refs[3 * n + ng:]
        off = 0
        for i in range(n):
            w_ref, m_ref, v_ref = refs[i], refs[n + i], refs[2 * n + i]
            go_ref, d_ref, nm_ref, nv_ref = (outs[k * n + i] for k in range(4))
            layers = g_count[i]
            g = g_refs[off][...]
            for l in range(1, layers):
                g = jnp.where(s // (N_STEPS // layers) == l, g_refs[off + l][...], g)
            off += layers
            g = g.reshape(w_ref.shape)
            m = ADAM_B1 * m_ref[...] + (1.0 - ADAM_B1) * g
            v = ADAM_B2 * v_ref[...] + (1.0 - ADAM_B2) * (g * g)
            m_hat = m / ADAM_C1
            v_hat = v / ADAM_C2
            go_ref[...] = g
            d_ref[...] = -ADAM_LR * (m_hat / (jnp.sqrt(v_hat) + ADAM_EPS) + ADAM_WD * w_ref[...])
            nm_ref[...] = m
            nv_ref[...] = v

    outs = pl.pallas_call(
        body, name=name, grid=(N_STEPS,), in_specs=specs * 3 + g_specs, out_specs=specs * 4,
        out_shape=[jax.ShapeDtypeStruct(a.shape, f32) for a in ws] * 4,
        compiler_params=_params("parallel"))(*ws, *ms, *vs, *[g for g_list in gs for g in g_list])
    return [outs[k * n:(k + 1) * n] for k in range(4)]


def _adamw_update(w, g, m, v):
    m = ADAM_B1 * m + (1.0 - ADAM_B1) * g
    v = ADAM_B2 * v + (1.0 - ADAM_B2) * (g * g)
    m_hat = m / ADAM_C1
    v_hat = v / ADAM_C2
    return -ADAM_LR * (m_hat / (jnp.sqrt(v_hat) + ADAM_EPS) + ADAM_WD * w), m, v


def adamw_small(ws, gs, ms, vs, *, name):
    n = len(ws)

    def body(*refs):
        for i in range(n):
            w_ref, g_ref, m_ref, v_ref = (refs[k * n + i] for k in range(4))
            d_ref, nm_ref, nv_ref = (refs[(4 + k) * n + i] for k in range(3))
            d_ref[...], nm_ref[...], nv_ref[...] = _adamw_update(w_ref[...], g_ref[...], m_ref[...], v_ref[...])

    outs = pl.pallas_call(
        body, name=name, out_shape=[jax.ShapeDtypeStruct(a.shape, f32) for a in ws] * 3)(*ws, *gs, *ms, *vs)
    return outs[:n], outs[n:2 * n], outs[2 * n:]


def _place():
    return lax.axis_index("x"), lax.axis_index("y"), lax.axis_index("c")


def gather_shards(bufs, *, name, split):
    n = len(bufs)

    def body(*refs):
        bufs_ = refs[:n]
        isend, irecv, dsend, drecv = refs[2 * n:]
        x, y, c = _place()
        k = 2 * x + y
        peers = [(1 - x, y, c), (x, 1 - y, c), (1 - x, 1 - y, c)]
        peer_k = [2 * (1 - x) + y, 2 * x + (1 - y), 2 * (1 - x) + (1 - y)]

        def slab(a, q, h):
            if not split[a]:
                return bufs_[a].at[q]
            half = bufs_[a].shape[1] // 2
            return bufs_[a].at[q, pl.ds(pl.multiple_of(h * half, 16), half)]

        def ici(a, j, q):
            return pltpu.make_async_remote_copy(
                src_ref=slab(a, q, c), dst_ref=slab(a, q, c), send_sem=isend.at[3 * a + j], recv_sem=irecv.at[3 * a + j],
                device_id=peers[j], device_id_type=MESH)

        def d2d(a, j, h):
            return pltpu.make_async_remote_copy(
                src_ref=slab(a, peer_k[j], h), dst_ref=slab(a, peer_k[j], h), send_sem=dsend.at[3 * a + j],
                recv_sem=drecv.at[3 * a + j], device_id=(x, y, 1 - c), device_id_type=MESH)

        for a in range(n):
            for j in range(3):
                ici(a, j, k).start()
        for a in range(n):
            for j in range(3):
                ici(a, j, peer_k[j]).wait_recv()
                if split[a]:
                    d2d(a, j, c).start()
        for a in range(n):
            for j in range(3):
                if split[a]:
                    d2d(a, j, 1 - c).wait_recv()
        for a in range(n):
            for j in range(3):
                ici(a, j, k).wait_send()
                if split[a]:
                    d2d(a, j, c).wait_send()

    return pl.pallas_call(
        body, name=name, in_specs=[ANY] * n, out_specs=[ANY] * n,
        out_shape=[jax.ShapeDtypeStruct(b.shape, b.dtype) for b in bufs],
        input_output_aliases={i: i for i in range(n)},
        scratch_shapes=[pltpu.SemaphoreType.DMA((3 * n,))] * 4)(*bufs)


HBM = pl.BlockSpec(memory_space=pltpu.HBM)
SEM = pl.BlockSpec(memory_space=pltpu.SEMAPHORE)
DATAFLOW = pltpu.SideEffectType.DATAFLOW_SIDE_EFFECTING


def _chip_peers():
    x, y, c = _place()
    return 2 * x + y, [(1 - x, y, c), (x, 1 - y, c), (1 - x, 1 - y, c)], [2 * (1 - x) + y, 2 * x + (1 - y), 2 * (1 - x) + (1 - y)]


def _half_slab(ref, q, h):
    half = ref.shape[1] // 2
    return ref.at[q, pl.ds(pl.multiple_of(h * half, BF16_ROWS), half)]


def gather_start(bufs, groups, after, *, name):
    n = len(bufs)
    ng = len(groups)

    def body(*refs):
        ins = refs[:n]
        sends, recvs = refs[2 * n + 1:2 * n + 1 + ng], refs[2 * n + 1 + ng:2 * n + 1 + 2 * ng]
        token = refs[-1]
        c = lax.axis_index("c")
        k, peers, _ = _chip_peers()
        for gi, grp in enumerate(groups):
            for pos, a in enumerate(grp):
                for j in range(3):
                    pltpu.make_async_remote_copy(
                        src_ref=_half_slab(ins[a], k, c), dst_ref=_half_slab(ins[a], k, c), send_sem=sends[gi].at[3 * pos + j],
                        recv_sem=recvs[gi].at[3 * pos + j], device_id=peers[j], device_id_type=MESH).start()
        token[...] = jnp.zeros_like(token)

    sems = [pltpu.SemaphoreType.DMA((3 * len(grp),)) for grp in groups]
    outs = pl.pallas_call(
        body, name=name, in_specs=[HBM] * n + [ANY],
        out_specs=[HBM] * n + [SEM] * (2 * ng) + [pl.BlockSpec(memory_space=pltpu.VMEM)],
        out_shape=[pltpu.HBM(b.shape, b.dtype) for b in bufs] + sems + sems + [jax.ShapeDtypeStruct((8, LANES), f32)],
        input_output_aliases={i: i for i in range(n)},
        compiler_params=pltpu.CompilerParams(has_side_effects=DATAFLOW))(
            *[pltpu.with_memory_space_constraint(b, pltpu.HBM) for b in bufs], after)
    return outs[:n], outs[n:n + ng], outs[n + ng:n + 2 * ng], outs[-1]


def gather_wait(bufs, send_sems, recv_sems, after, *, name):
    n = len(bufs)

    def body(*refs):
        ins = refs[:n]
        send, recv = refs[n], refs[n + 1]
        c = lax.axis_index("c")
        k, peers, peer_k = _chip_peers()
        for a in range(n):
            for j in range(3):
                copy = pltpu.make_async_remote_copy(
                    src_ref=_half_slab(ins[a], k, c), dst_ref=_half_slab(ins[a], peer_k[j], c), send_sem=send.at[3 * a + j],
                    recv_sem=recv.at[3 * a + j], device_id=peers[j], device_id_type=MESH)
                copy.wait_send()
                copy.wait_recv()

    return pl.pallas_call(
        body, name=name, in_specs=[HBM] * n + [SEM, SEM, ANY], out_specs=[HBM] * n,
        out_shape=[pltpu.HBM(b.shape, b.dtype) for b in bufs],
        input_output_aliases={i: i for i in range(n)},
        compiler_params=pltpu.CompilerParams(has_side_effects=DATAFLOW))(*bufs, send_sems, recv_sems, after)


def forward_halves(bufs, *, name):
    n = len(bufs)

    def body(*refs):
        bufs_ = refs[:n]
        send, recv = refs[2 * n:]
        x, y, c = _place()
        _, _, peer_k = _chip_peers()

        def copy(a, j, h):
            return pltpu.make_async_remote_copy(
                src_ref=_half_slab(bufs_[a], peer_k[j], h), dst_ref=_half_slab(bufs_[a], peer_k[j], h),
                send_sem=send.at[3 * a + j], recv_sem=recv.at[3 * a + j], device_id=(x, y, 1 - c), device_id_type=MESH)

        for a in range(n):
            for j in range(3):
                copy(a, j, c).start()
        for a in range(n):
            for j in range(3):
                copy(a, j, 1 - c).wait_recv()
        for a in range(n):
            for j in range(3):
                copy(a, j, c).wait_send()

    return pl.pallas_call(
        body, name=name, in_specs=[ANY] * n, out_specs=[ANY] * n,
        out_shape=[jax.ShapeDtypeStruct(b.shape, b.dtype) for b in bufs],
        input_output_aliases={i: i for i in range(n)},
        scratch_shapes=[pltpu.SemaphoreType.DMA((3 * n,))] * 2)(*bufs)


def _forward_copies(bufs_, send, recv, h):
    x, y, c = _place()
    _, _, peer_k = _chip_peers()
    return [pltpu.make_async_remote_copy(
        src_ref=_half_slab(bufs_[a], peer_k[j], h), dst_ref=_half_slab(bufs_[a], peer_k[j], h),
        send_sem=send.at[3 * a + j], recv_sem=recv.at[3 * a + j], device_id=(x, y, 1 - c), device_id_type=MESH)
        for a in range(len(bufs_)) for j in range(3)]


def forward_start(bufs, after, *, name):
    n = len(bufs)

    def body(*refs):
        for cp in _forward_copies(refs[:n], refs[2 * n + 1], refs[2 * n + 2], lax.axis_index("c")):
            cp.start()
        refs[-1][...] = jnp.zeros_like(refs[-1])

    sems = [pltpu.SemaphoreType.DMA((3 * n,))] * 2
    outs = pl.pallas_call(
        body, name=name, in_specs=[HBM] * n + [ANY],
        out_specs=[HBM] * n + [SEM] * 2 + [pl.BlockSpec(memory_space=pltpu.VMEM)],
        out_shape=[pltpu.HBM(b.shape, b.dtype) for b in bufs] + sems + [jax.ShapeDtypeStruct((8, LANES), f32)],
        input_output_aliases={i: i for i in range(n)},
        compiler_params=pltpu.CompilerParams(has_side_effects=DATAFLOW))(*bufs, after)
    return (n, outs[:-1]), outs[-1]


def forward_wait(state, after, *, name):
    n, held = state

    def body(*refs):
        c = lax.axis_index("c")
        for mine, theirs in zip(_forward_copies(refs[:n], refs[n], refs[n + 1], c),
                                _forward_copies(refs[:n], refs[n], refs[n + 1], 1 - c)):
            mine.wait_send()
            theirs.wait_recv()

    return pl.pallas_call(
        body, name=name, in_specs=[HBM] * n + [SEM] * 2 + [ANY], out_specs=[HBM] * n,
        out_shape=[pltpu.HBM(b.shape, b.dtype) for b in held[:n]],
        input_output_aliases={i: i for i in range(n)},
        compiler_params=pltpu.CompilerParams(has_side_effects=DATAFLOW))(*held, after)


def _sibling_copies(srcs, lands, send, recv):
    x, y, c = _place()
    return [pltpu.make_async_remote_copy(src_ref=srcs[a], dst_ref=lands[a], send_sem=send.at[a], recv_sem=recv.at[a],
                                         device_id=(x, y, 1 - c), device_id_type=MESH) for a in range(len(srcs))]


def sibling_start(arrays, after, *, name):
    n = len(arrays)
    lands = [pltpu.with_memory_space_constraint(lax.empty(a.shape, a.dtype), pltpu.HBM) for a in arrays]

    def body(*refs):
        for cp in _sibling_copies(refs[:n], refs[n:2 * n], refs[4 * n + 1], refs[4 * n + 2]):
            cp.start()
        refs[-1][...] = jnp.zeros_like(refs[-1])

    bufs = list(arrays) + lands
    sems = [pltpu.SemaphoreType.DMA((n,))] * 2
    outs = pl.pallas_call(
        body, name=name, in_specs=[HBM] * (2 * n) + [ANY],
        out_specs=[HBM] * (2 * n) + [SEM] * 2 + [pl.BlockSpec(memory_space=pltpu.VMEM)],
        out_shape=[pltpu.HBM(b.shape, b.dtype) for b in bufs] + sems + [jax.ShapeDtypeStruct((8, LANES), f32)],
        input_output_aliases={i: i for i in range(2 * n)},
        compiler_params=pltpu.CompilerParams(has_side_effects=DATAFLOW))(
            *[pltpu.with_memory_space_constraint(b, pltpu.HBM) for b in bufs], after)
    return (n, outs[:-1]), outs[-1]


def sibling_wait(state, after, *, name):
    n, held = state

    def body(*refs):
        for cp in _sibling_copies(refs[:n], refs[n:2 * n], refs[2 * n], refs[2 * n + 1]):
            cp.wait_send()
            cp.wait_recv()

    outs = pl.pallas_call(
        body, name=name, in_specs=[HBM] * (2 * n) + [SEM] * 2 + [ANY], out_specs=[HBM] * (2 * n),
        out_shape=[pltpu.HBM(b.shape, b.dtype) for b in held[:2 * n]],
        input_output_aliases={i: i for i in range(2 * n)},
        compiler_params=pltpu.CompilerParams(has_side_effects=DATAFLOW))(*held, after)
    return outs[n:]


ALL_MASKS = [(mx, my, mc) for mx in (0, 1) for my in (0, 1) for mc in (0, 1)][1:]


def _scatter_copies(srcs, lands, ev, send, recv, esend, erecv):
    x, y, c = _place()
    me = 4 * x + 2 * y + c
    k, peers, peer_k = _chip_peers()
    out = []
    for a in range(len(srcs)):
        for j in range(3):
            out.append(pltpu.make_async_remote_copy(
                src_ref=srcs[a].at[peer_k[j]], dst_ref=lands[a].at[j], send_sem=send.at[3 * a + j],
                recv_sem=recv.at[3 * a + j], device_id=peers[j], device_id_type=MESH))
    start_ev, wait_ev = [], []
    if ev is not None:
        for j, (mx, my, mc) in enumerate(ALL_MASKS):
            peer = (x ^ mx, y ^ my, c ^ mc)
            start_ev.append(pltpu.make_async_remote_copy(
                src_ref=ev.at[me], dst_ref=ev.at[me], send_sem=esend.at[j], recv_sem=erecv.at[j],
                device_id=peer, device_id_type=MESH))
            wait_ev.append(pltpu.make_async_remote_copy(
                src_ref=ev.at[me], dst_ref=ev.at[me ^ (4 * mx + 2 * my + mc)], send_sem=esend.at[j],
                recv_sem=erecv.at[j], device_id=peer, device_id_type=MESH))
    return out, start_ev, wait_ev


def chip_scatter_start(arrays, everyone, after, *, name):
    n = len(arrays)
    ne = 0 if everyone is None else 1
    lands = [pltpu.with_memory_space_constraint(lax.empty((3,) + a.shape[1:], a.dtype), pltpu.HBM) for a in arrays]

    def body(*refs):
        srcs, lands_ = refs[:n], refs[n:2 * n]
        ev = refs[2 * n] if ne else None
        sems = refs[2 * n + ne + 1 + 2 * n + ne:-1]
        send, recv = sems[0], sems[1]
        esend, erecv = (sems[2], sems[3]) if ne else (None, None)
        copies, start_ev, _ = _scatter_copies(srcs, lands_, ev, send, recv, esend, erecv)
        for cp in start_ev + copies:
            cp.start()
        refs[-1][...] = jnp.zeros_like(refs[-1])

    sem_shapes = [pltpu.SemaphoreType.DMA((3 * n,))] * 2 + [pltpu.SemaphoreType.DMA((7,))] * (2 * ne)
    bufs = list(arrays) + lands + ([everyone] if ne else [])
    outs = pl.pallas_call(
        body, name=name, in_specs=[HBM] * len(bufs) + [ANY],
        out_specs=[HBM] * len(bufs) + [SEM] * len(sem_shapes) + [pl.BlockSpec(memory_space=pltpu.VMEM)],
        out_shape=[pltpu.HBM(b.shape, b.dtype) for b in bufs] + sem_shapes + [jax.ShapeDtypeStruct((8, LANES), f32)],
        input_output_aliases={i: i for i in range(len(bufs))},
        compiler_params=pltpu.CompilerParams(has_side_effects=DATAFLOW))(
            *[pltpu.with_memory_space_constraint(b, pltpu.HBM) for b in bufs], after)
    return (n, ne, outs[:-1]), outs[-1]


def chip_scatter_wait(state, after, *, name):
    n, ne, held = state
    nb = 2 * n + ne
    bufs, sems = held[:nb], held[nb:]

    def body(*refs):
        srcs, lands_ = refs[:n], refs[n:2 * n]
        ev = refs[2 * n] if ne else None
        sems_ = refs[nb:nb + len(sems)]
        esend, erecv = (sems_[2], sems_[3]) if ne else (None, None)
        copies, _, wait_ev = _scatter_copies(srcs, lands_, ev, sems_[0], sems_[1], esend, erecv)
        for cp in wait_ev + copies:
            cp.wait_send()
            cp.wait_recv()

    outs = pl.pallas_call(
        body, name=name, in_specs=[HBM] * nb + [SEM] * len(sems) + [ANY], out_specs=[HBM] * nb,
        out_shape=[pltpu.HBM(b.shape, b.dtype) for b in bufs],
        input_output_aliases={i: i for i in range(nb)},
        compiler_params=pltpu.CompilerParams(has_side_effects=DATAFLOW))(*bufs, *sems, after)
    return outs[n:2 * n], (outs[2 * n] if ne else None)


def sibling_merge(bufs, *, name):
    n = len(bufs)

    def body(*refs):
        bufs_ = refs[:n]
        send, recv = refs[2 * n:]
        x, y, c = _place()

        def copy(u, h):
            return pltpu.make_async_remote_copy(
                src_ref=bufs_[u].at[h], dst_ref=bufs_[u].at[h], send_sem=send.at[u], recv_sem=recv.at[u],
                device_id=(x, y, 1 - c), device_id_type=MESH)

        for u in range(n):
            copy(u, c).start()
        for u in range(n):
            copy(u, 1 - c).wait_recv()
        for u in range(n):
            copy(u, c).wait_send()

    return pl.pallas_call(
        body, name=name, in_specs=[ANY] * n, out_specs=[ANY] * n,
        out_shape=[jax.ShapeDtypeStruct(b.shape, b.dtype) for b in bufs],
        input_output_aliases={i: i for i in range(n)},
        scratch_shapes=[pltpu.SemaphoreType.DMA((n,)), pltpu.SemaphoreType.DMA((n,))])(*bufs)


def sum_leading(a, *, name):
    n, r, c = a.shape

    def body(a_ref, o_ref):
        acc = a_ref[0]
        for i in range(1, n):
            acc = acc + a_ref[i]
        o_ref[...] = acc

    rb = r // 2 if r % 16 == 0 else r
    return pl.pallas_call(
        body, name=name, grid=(r // rb,), in_specs=[pl.BlockSpec((n, rb, c), lambda i: (0, i, 0))],
        out_specs=pl.BlockSpec((rb, c), lambda i: (i, 0)), out_shape=jax.ShapeDtypeStruct((r, c), f32),
        compiler_params=_params("parallel"))(a)


def _half_rows(shape):
    return shape[1] // 2 // 2


def rs_add_sibling(grads, recvd, ck_arr, *, name):
    n = len(grads)

    def body(ck_ref, *refs):
        s = pl.program_id(1)
        for u in range(n):
            g_ref, r_ref = refs[u], refs[n + u]
            qb_ref, own_ref = refs[2 * n + u], refs[3 * n + u]
            q = g_ref[0] + r_ref[0].astype(f32)
            qb_ref[0] = q.astype(bf16)

            @pl.when(s == ck_ref[1])
            def _(own_ref=own_ref, q=q):
                own_ref[...] = q

    in_specs = [pl.BlockSpec((1, _half_rows(g.shape), g.shape[2]), lambda r, s, ck: (s, ck[0] * 2 + r, 0)) for g in grads]
    in_specs += [pl.BlockSpec((1, _half_rows(g.shape), g.shape[2]), lambda r, s, ck: (s, r, 0)) for g in grads]
    out_specs = [pl.BlockSpec((1, _half_rows(g.shape), g.shape[2]), lambda r, s, ck: (s, r, 0)) for g in grads]
    out_specs += [pl.BlockSpec((_half_rows(g.shape), g.shape[2]), lambda r, s, ck: (r, 0)) for g in grads]
    outs = pl.pallas_call(
        body, name=name,
        grid_spec=pltpu.PrefetchScalarGridSpec(num_scalar_prefetch=1, grid=(2, N_SHARDS),
                                               in_specs=in_specs, out_specs=out_specs),
        out_shape=[jax.ShapeDtypeStruct((N_SHARDS, g.shape[1] // 2, g.shape[2]), bf16) for g in grads]
        + [jax.ShapeDtypeStruct((g.shape[1] // 2, g.shape[2]), f32) for g in grads],
        compiler_params=_params("parallel", "arbitrary"))(ck_arr, *grads, *recvd)
    return outs[:n], outs[n:]


def rs_sum_chips(owns, recvd, ck_arr, *, name):
    n = len(owns)

    def body(ck_ref, *refs):
        for u in range(n):
            own_ref, r_ref, o_ref = refs[u], refs[n + u], refs[2 * n + u]
            o_ref[0] = ((own_ref[...] + r_ref[0].astype(f32)) + r_ref[1].astype(f32)) + r_ref[2].astype(f32)

    in_specs = [pl.BlockSpec((o.shape[0] // 2, o.shape[1]), lambda r, ck: (r, 0)) for o in owns]
    in_specs += [pl.BlockSpec((3, o.shape[0] // 2, o.shape[1]), lambda r, ck: (0, r, 0)) for o in owns]
    out_specs = [pl.BlockSpec((1, o.shape[0] // 2, o.shape[1]), lambda r, ck: (ck[0], r, 0)) for o in owns]
    return pl.pallas_call(
        body, name=name,
        grid_spec=pltpu.PrefetchScalarGridSpec(num_scalar_prefetch=1, grid=(2,), in_specs=in_specs, out_specs=out_specs),
        out_shape=[jax.ShapeDtypeStruct((2,) + o.shape, f32) for o in owns],
        compiler_params=_params("parallel"))(ck_arr, *owns, *recvd)


SMALL = ("a_norm", "a_v_norm", "a_w_s", "a_b_s", "f_norm", "f_conv_w", "f_conv_b", "kv_norm", "k_norm",
         "b_norm", "b_q_norm", "b_sinks")
BIG = ("a_w_in", "a_w_out", "f_w_in", "f_w_out", "w_kv", "b_w_q", "b_w_o")
PACK_COLS = 1024
PACK_ROWS = 8 * N_STEPS


def _pack(parts, rows=PACK_ROWS):
    flat = jnp.concatenate([p.reshape(-1).astype(f32) for p in parts])
    pad = (-flat.shape[0]) % (rows * PACK_COLS)
    return jnp.pad(flat, (0, pad)).reshape(-1, PACK_COLS)


def _unpack(packed, shapes):
    flat = packed.reshape(-1)
    out, off = [], 0
    for s in shapes:
        size = math.prod(s)
        out.append(flat[off:off + size].reshape(s))
        off += size
    return out


def _behind(value, token):
    return lax.optimization_barrier((value, token))[0]


def _ffn_fwd(x, g, h, r, w_in4, conv_w, conv_b, f, tag):
    wg, wu = conv_w[:, :f], conv_w[:, f:]
    bg, bu = conv_b[None, :f], conv_b[None, f:]
    pg, pu, gate, up, a = ffn_in_fused(h, w_in4, wg, wu, bg, bu, name=f"ffn{tag}_in")
    return a, (x, g, h, r, pg, pu, gate, up, a, wg, wu)


def _ffn_bwd(dy, saved, w_in4, w_out, c_arr, tag, exchange=False):
    x, g, h, r, pg, pu, gate, up, a, wg, wu = saved
    f = w_out.shape[0]
    d_w_out = mm_tn(a, [dy], c_arr, name=f"ffn{tag}_dwout", n_s=w_out.shape[1], shard_rows=f // N_SHARDS, tki=f // 2)
    dpg, dpu, sg, su = ffn_gate_bwd(dy, w_out, pg, pu, gate, up, wg, wu, name=f"ffn{tag}_dgate")
    d_w_in = mm_tn(h, [dpg, dpu], c_arr, name=f"ffn{tag}_dwin", n_s=w_in4.shape[2], shard_rows=h.shape[1])
    state = None
    if exchange:
        state, token = sibling_start([d_w_in[1], d_w_out[1]], d_w_in[0], name=f"rs_sibling_start_ffn{tag}")
        g = _behind(g, token)
    dx, dg = mm_nt_rms_bwd([dpg, dpu], w_in4, x, r, g, dy, name=f"ffn{tag}_dh")
    d_conv_w = jnp.concatenate([sg[0:3], su[0:3]], axis=1)
    d_conv_b = jnp.concatenate([sg[3], su[3]], axis=0)
    return dx, dg, d_w_in, d_conv_w, d_conv_b, d_w_out, state


def _rs_front(pairs, sibling_state, after, c_arr, tag):
    units = [full.reshape(N_SHARDS, -1, full.shape[-1]) for full, _ in pairs]
    from_sib = sibling_wait(sibling_state, after, name=f"rs_sibling_wait{tag}")
    return rs_add_sibling(units, from_sib, c_arr, name=f"rs_add{tag}")


def _rs_back(own, from_chips, c_arr, tag):
    halves = rs_sum_chips(list(own), list(from_chips), c_arr, name=f"rs_sum{tag}")
    return [m.reshape(-1, m.shape[2]) for m in sibling_merge(list(halves), name=f"rs_merge{tag}")]


def kernel(x, a_norm, a_w_in, a_v_norm, a_w_s, a_b_s, a_w_out, f_norm, f_w_in, f_conv_w, f_conv_b, f_w_out, kv_norm, w_kv, k_norm, b_norm, b_w_q, b_q_norm, b_sinks, b_w_o, loss_target, m_a_norm, m_a_w_in, m_a_v_norm, m_a_w_s, m_a_b_s, m_a_w_out, m_f_norm, m_f_w_in, m_f_conv_w, m_f_conv_b, m_f_w_out, m_kv_norm, m_w_kv, m_k_norm, m_b_norm, m_b_w_q, m_b_q_norm, m_b_sinks, m_b_w_o, v_a_norm, v_a_w_in, v_a_v_norm, v_a_w_s, v_a_b_s, v_a_w_out, v_f_norm, v_f_w_in, v_f_conv_w, v_f_conv_b, v_f_w_out, v_kv_norm, v_w_kv, v_k_norm, v_b_norm, v_b_w_q, v_b_q_norm, v_b_sinks, v_b_w_o):
    args = dict(locals())
    weights = {n: args[n] for n in SMALL + BIG}
    moms = {n: args["m_" + n] for n in SMALL + BIG}
    vars_ = {n: args["v_" + n] for n in SMALL + BIG}
    t, d = x.shape[1], x.shape[2]
    xi, yi, ci = _place()
    chip = 2 * xi + yi

    big_local = [a_w_in[0], a_w_out[0], f_w_in, f_w_out, w_kv, b_w_q[0], b_w_o[0]]
    c_arr = jnp.stack([ci, chip]).astype(jnp.int32)
    k_arr = jnp.stack([chip]).astype(jnp.int32)
    b_ain, b_aout, b_fin0, b_fin1, b_fout0, b_fout1, b_kv, b_q, b_o = cast_into_slot(big_local, k_arr, name="cast_weights")
    small_cols = _pack([a_norm, a_v_norm, f_conv_w], rows=8)
    b_small = lax.dynamic_update_slice(jnp.zeros((N_SHARDS,) + small_cols.shape, f32), small_cols[None], (chip, 0, 0))
    g_small, w_a_in, g_a_w_out = gather_shards([b_small, b_ain, b_aout], name="gather_first", split=[False, True, True])
    later, send_sems, recv_sems, token = gather_start([b_fin0, b_fout0, b_kv, b_q, b_o, b_fin1, b_fout1],
                                                      [[0], [1], [2, 3, 4], [5, 6]], g_small, name="gather_start")
    ns_cols = a_norm.shape[1]
    nf_cols = f_conv_w.shape[2]
    parts = [_unpack(g_small[k], [a_norm.shape, a_v_norm.shape, f_conv_w.shape]) for k in range(N_SHARDS)]
    a_norm_f = jnp.concatenate([p[0] for p in parts], axis=1) + token[0, 0]
    a_v_norm_f = jnp.concatenate([p[1] for p in parts], axis=1)
    conv_w_f = jnp.concatenate([p[2] for p in parts], axis=2)

    x0 = x[0]
    tril = jnp.tril(jnp.ones((CHUNK, CHUNK), dtype=bool))
    wc = jnp.where(tril[None], a_w_s[0], 0.0).astype(bf16)
    bt = a_b_s[0].T
    kg2 = jnp.tile(k_norm, 2)[None]
    qg2 = jnp.tile(b_q_norm[0], 2)[None]

    (h_a,), r_a = rms_fwd(x0, [a_norm_f], name="a_norm")
    zuv = mm_nn(h_a, w_a_in, name="a_in")
    y_a = sgu_gate_fwd(zuv, a_v_norm_f, wc, bt, name="a_gate")
    w_a_out = g_a_w_out.reshape(1, -1, d)
    f = f_w_out.shape[1] * N_SHARDS
    fwd0, tok0 = forward_start(gather_wait(later[0:1], send_sems[0], recv_sems[0], y_a, name="gather_wait_0"), y_a,
                               name="gather_forward_start_0")
    x1, (h_f0,), r_f0 = mm_residual(y_a, w_a_out[0], x0, name="a_out", gains=[_behind(f_norm[0:1], tok0)])
    (g_fin0,) = forward_wait(fwd0, x1, name="gather_forward_wait_0")
    w_f_in = [g_fin0, None]
    a0, ffn0 = _ffn_fwd(x1, f_norm[0:1], h_f0, r_f0, w_f_in[0], conv_w_f[0], f_conv_b[0], f, "0")
    (g_fout0,) = forward_halves(gather_wait(later[1:2], send_sems[1], recv_sems[1], a0, name="gather_wait_1"),
                                name="gather_forward_1")
    w_f_out = [g_fout0.reshape(-1, d), None]
    fwd1, tok1 = forward_start(gather_wait(later[2:5], send_sems[2], recv_sems[2], g_fout0, name="gather_wait_1b"), a0,
                               name="gather_forward_start_1b")
    x2, (h_k, h_q), r_b = mm_residual(a0, w_f_out[0], x1, name="ffn0_out", gains=[_behind(kv_norm[None], tok1), b_norm])
    g_w_kv, g_b_w_q, g_b_w_o = forward_wait(fwd1, x2, name="gather_forward_wait_1b")
    w_kv_f = g_w_kv.reshape(1, d, -1)
    w_q_f = g_b_w_q.reshape(1, d, -1)
    w_o_f = g_b_w_o.reshape(1, -1, d)
    kv, k2, v2 = kv_proj_post(h_k, w_kv_f[0], kg2, name="kv_proj")
    qp, qn = q_proj_norm(h_q, w_q_f[0], qg2, name="q_proj", scale=HEAD_DIM ** -0.5)
    fwd2, tok2 = forward_start(gather_wait(later[5:7], send_sems[3], recv_sems[3], qn, name="gather_wait_2"), qn,
                               name="gather_forward_start_2")
    o = attn_fwd(qn, k2, v2, _behind(b_sinks[0], tok2), name="attn")
    x3, (h_f1,), r_f1 = mm_residual(o, w_o_f[0], x2, name="o_proj", gains=[f_norm[1:2]])
    g_fin1, g_fout1 = forward_wait(fwd2, x3, name="gather_forward_wait_2")
    w_f_in[1] = g_fin1
    w_f_out[1] = g_fout1.reshape(-1, d)
    a1, ffn1 = _ffn_fwd(x3, f_norm[1:2], h_f1, r_f1, w_f_in[1], conv_w_f[1], f_conv_b[1], f, "1")
    dx4, sq = mm_residual(a1, w_f_out[1], x3, name="ffn1_out", target=loss_target[0])
    loss_part = (0.5 * jnp.sum(sq) / d).reshape(1)

    proj_rows = d // N_SHARDS
    dx3, d_fn1, d_fwin1, d_cw1, d_cb1, d_fwout1, _ = _ffn_bwd(dx4, ffn1, w_f_in[1], w_f_out[1], c_arr, "1")
    do = mm_nt([dx3], w_o_f, name="o_proj_dx")
    d_w_o = mm_tn(o, [dx3], c_arr, name="o_proj_dw", n_s=d, shard_rows=o.shape[1] // N_SHARDS)
    dqn, dk2, dv2, dsink = attn_bwd(qn, k2, v2, do, b_sinks[0], name="attn_bwd")
    dqp, dqg = q_norm_bwd(dqn, qp, qg2, name="q_norm_bwd", scale=HEAD_DIM ** -0.5)
    dkv, dkg = kv_post_bwd(dk2, dv2, kv, kg2, name="kv_post_bwd")
    d_w_q = mm_tn(h_q, [dqp], c_arr, name="q_proj_dw", n_s=w_q_f.shape[2], shard_rows=proj_rows)
    d_w_kv = mm_tn(h_k, [dkv], c_arr, name="kv_proj_dw", n_s=w_kv_f.shape[2], shard_rows=proj_rows)
    group1 = [d_fwin1, d_fwout1, d_w_kv, d_w_q, d_w_o]
    sib1, token_s1 = sibling_start([half for _, half in group1], d_w_kv[0], name="rs_sibling_start1")
    dh_k = mm_nt([dkv], w_kv_f, name="kv_proj_dx")
    dx2, d_bn, d_kvn = mm_nt_rms_bwd([dqp], w_q_f, x2, r_b, _behind(b_norm, token_s1), dx3, name="q_proj_dx",
                                     extra=(dh_k, kv_norm[None]))
    chip_bf1, own1 = _rs_front(group1, sib1, dx2, c_arr, "1")
    scatter1, token1 = chip_scatter_start(list(chip_bf1), None, dx2, name="rs_chips_start1")
    ffn0 = ffn0[:9] + (_behind(ffn0[9], token1),) + ffn0[10:]
    dx1, d_fn0, d_fwin0, d_cw0, d_cb0, d_fwout0, sib2 = _ffn_bwd(dx2, ffn0, w_f_in[0], w_f_out[0], c_arr, "0", exchange=True)
    chip_bf2, own2 = _rs_front([d_fwin0, d_fwout0], sib2, dx1, c_arr, "2")
    scatter2, token2 = chip_scatter_start(list(chip_bf2), None, dx1, name="rs_chips_start2")
    a_v_norm_f = _behind(a_v_norm_f, token2)
    dy_a = mm_nt([dx1], w_a_out, name="a_out_dx")
    d_w_aout = mm_tn(y_a, [dx1], c_arr, name="a_out_dw", n_s=d, shard_rows=y_a.shape[1] // N_SHARDS)
    dzu, dzv, d_avn, d_ws, d_bt = sgu_gate_bwd(zuv, dy_a, a_v_norm_f, wc, bt, name="a_gate_bwd")
    d_w_ain = mm_tn(h_a, [dzu, dzv], c_arr, name="a_in_dw", n_s=w_a_in.shape[2], shard_rows=d)
    sib3, token_s3 = sibling_start([d_w_ain[1], d_w_aout[1]], d_w_ain[0], name="rs_sibling_start3")
    dx0, d_an = mm_nt_rms_bwd([dzu, dzv], w_a_in, x0, r_a, _behind(a_norm_f, token_s3), dx1, name="a_in_dx")
    grad_x = dx0[None]

    chip_bf3, own3 = _rs_front([d_w_ain, d_w_aout], sib3, dx0, c_arr, "3")
    d_fn = jnp.concatenate([d_fn0, d_fn1], axis=0)
    d_cw = jnp.stack([d_cw0, d_cw1])
    d_cb = jnp.stack([d_cb0, d_cb1])
    d_kg = (dkg[0, :HEAD_DIM] + dkg[0, HEAD_DIM:])
    d_qg = (dqg[0, :HEAD_DIM] + dqg[0, HEAD_DIM:])[None]
    small_full = [d_an, d_avn, d_ws[None], d_bt.T[None], d_fn, d_cw, d_cb, d_kvn[0], d_kg, d_bn, d_qg,
                  dsink[:, :N_Q_HEADS], loss_part]
    packed = _pack(small_full)
    me = 4 * xi + 2 * yi + ci
    everyone = lax.dynamic_update_slice(lax.empty((N_DEV,) + packed.shape, f32), packed[None], (me, 0, 0))
    scatter3, token3 = chip_scatter_start(list(chip_bf3), everyone, own3[0], name="rs_chips_start3")
    from_chips1, _ = chip_scatter_wait(scatter1, token3, name="rs_chips_wait1")
    from_chips2, _ = chip_scatter_wait(scatter2, from_chips1[0], name="rs_chips_wait2")
    fin1, fout1, gkv, gq, go, fin0, fout0 = _rs_back(list(own1) + list(own2), list(from_chips1) + list(from_chips2),
                                                     c_arr, "12")
    late = ("f_w_in", "f_w_out", "w_kv", "b_w_q", "b_w_o")
    res_late = adamw([weights[n] for n in late], [[fin0, fin1], [fout0, fout1], [gkv], [gq], [go]],
                     [moms[n] for n in late], [vars_[n] for n in late], name="adamw_late")
    from_chips3, from_all = chip_scatter_wait(scatter3, res_late[1][2], name="rs_chips_wait3")
    ain, aout = _rs_back(own3, from_chips3, c_arr, "3")
    first = ("a_w_in", "a_w_out")
    res_first = adamw([weights[n] for n in first], [[ain], [aout]], [moms[n] for n in first],
                      [vars_[n] for n in first], name="adamw_first")
    big = {n: tuple(r[i] for r in res_late) for i, n in enumerate(late)}
    big.update({n: tuple(r[i] for r in res_first) for i, n in enumerate(first)})

    full_shapes = [g.shape for g in small_full]
    small_g = _unpack(sum_leading(from_all, name="small_sum"), full_shapes)
    loss = small_g.pop()[0]
    small_g[0] = lax.dynamic_slice_in_dim(small_g[0], chip * ns_cols, ns_cols, axis=1)
    small_g[1] = lax.dynamic_slice_in_dim(small_g[1], chip * ns_cols, ns_cols, axis=1)
    small_g[5] = lax.dynamic_slice_in_dim(small_g[5], chip * nf_cols, nf_cols, axis=2)
    small_shapes = [weights[n].shape for n in SMALL]
    small_g = [g.reshape(s) for g, s in zip(small_g, small_shapes)]
    flat2 = [(math.prod(s[:-1]), s[-1]) for s in small_shapes]
    small_d, small_m, small_v = adamw_small(
        *[[a.reshape(s2) for a, s2 in zip(group, flat2)]
          for group in ([weights[n] for n in SMALL], small_g, [moms[n] for n in SMALL], [vars_[n] for n in SMALL])],
        name="adamw_small")
    small_d, small_m, small_v = ([a.reshape(s) for a, s in zip(group, small_shapes)]
                                 for group in (small_d, small_m, small_v))

    out = {}
    for i, n in enumerate(SMALL):
        out[n] = (small_g[i], small_d[i], small_m[i], small_v[i])
    out.update(big)
    order = ["a_norm", "a_w_in", "a_v_norm", "a_w_s", "a_b_s", "a_w_out", "f_norm", "f_w_in", "f_conv_w", "f_conv_b",
             "f_w_out", "kv_norm", "w_kv", "k_norm", "b_norm", "b_w_q", "b_q_norm", "b_sinks", "b_w_o"]
    return (loss, grad_x, *[out[n][0] for n in order], *[out[n][1] for n in order],
            *[out[n][2] for n in order], *[out[n][3] for n in order])
```

```python
import functools
import math

import jax
import jax.numpy as jnp
from jax import lax
from jax.experimental import pallas as pl
from jax.experimental.pallas import tpu as pltpu

f32 = jnp.float32
bf16 = jnp.bfloat16
MESH = pl.DeviceIdType.MESH
ANY = pl.BlockSpec(memory_space=pl.ANY)

EPS = 1e-6
LANES = 128
CHUNK = 128
HEAD_DIM = 64
N_Q_HEADS = 16
N_KV_HEADS = 4
Q_PER_KV = N_Q_HEADS // N_KV_HEADS
N_SHARDS = 4
N_DEV = 8

ADAM_LR = 0.001
ADAM_B1 = 0.9
ADAM_B2 = 0.999
ADAM_EPS = 1e-08
ADAM_WD = 0.01
ADAM_STEP = 10
ADAM_C1 = 1.0 - ADAM_B1 ** ADAM_STEP
ADAM_C2 = 1.0 - ADAM_B2 ** ADAM_STEP

_INV_SQRT2 = 1.0 / math.sqrt(2.0)
_INV_SQRT2PI = 1.0 / math.sqrt(2.0 * math.pi)


def _params(*sem):
    return pltpu.CompilerParams(dimension_semantics=sem)


def _gelu(z):
    return 0.5 * z * (1.0 + lax.erf(z * _INV_SQRT2))


def _gelu_and_grad(z):
    cdf = 0.5 * (1.0 + lax.erf(z * _INV_SQRT2))
    return z * cdf, cdf + z * jnp.exp(-0.5 * z * z) * _INV_SQRT2PI


def _dot(a, b):
    return jnp.dot(a, b, preferred_element_type=f32)


def _dot_nt(a, b):
    return lax.dot_general(a, b, (((1,), (1,)), ((), ())), preferred_element_type=f32)


def _dot_tn(a, b):
    return lax.dot_general(a, b, (((0,), (0,)), ((), ())), preferred_element_type=f32)


def _dot_split(a, b):
    hi = a.astype(bf16)
    lo = (a - hi.astype(f32)).astype(bf16)
    return _dot(hi, b) + _dot(lo, b)


VMEM_TILE_BUDGET = 52 * 1024 * 1024
MAX_ROW_TILE = 2048


def _row_tile(m, fixed_bytes, row_bytes):
    tm = min(m, MAX_ROW_TILE)
    while tm > 256 and 2 * (fixed_bytes + tm * row_bytes) > VMEM_TILE_BUDGET:
        tm //= 2
    return tm


def _isz(a):
    return jnp.dtype(a.dtype).itemsize


def mm_nn(a, w3, *, name, s0=0, ns=None, add=None, out_dtype=f32):
    m, k = a.shape
    s_all, _, n_s = w3.shape
    ns = s_all if ns is None else ns
    tm = _row_tile(m, k * n_s * 2, k * _isz(a) + n_s * jnp.dtype(out_dtype).itemsize + (0 if add is None else n_s * 4))

    def body(*refs):
        if add is None:
            a_ref, w_ref, o_ref = refs
            acc = _dot(a_ref[...].astype(bf16), w_ref[0])
        else:
            a_ref, w_ref, add_ref, o_ref = refs
            acc = _dot(a_ref[...].astype(bf16), w_ref[0]) + add_ref[...]
        o_ref[...] = acc.astype(out_dtype)

    in_specs = [pl.BlockSpec((tm, k), lambda j, i: (i, 0)),
                pl.BlockSpec((1, k, n_s), lambda j, i: (s0 + j, 0, 0))]
    args = [a, w3]
    if add is not None:
        in_specs.append(pl.BlockSpec((tm, n_s), lambda j, i: (i, j)))
        args.append(add)
    return pl.pallas_call(
        body, name=name, grid=(ns, m // tm), in_specs=in_specs,
        out_specs=pl.BlockSpec((tm, n_s), lambda j, i: (i, j)),
        out_shape=jax.ShapeDtypeStruct((m, ns * n_s), out_dtype),
        compiler_params=_params("parallel", "parallel"))(*args)


def mm_nt(a_list, w3, *, name, tko=None, add=None, out_dtype=f32):
    s_all, k_out, n_s = w3.shape
    m = a_list[0].shape[0]
    na = len(a_list)
    spa = s_all // na
    tko = k_out if tko is None else tko
    tm = _row_tile(m, tko * n_s * 2, na * n_s * _isz(a_list[0]) + tko * 4 * (1 if add is None else 2))

    def body(*refs):
        a_refs = refs[:na]
        w_ref = refs[na]
        o_ref = refs[-1]
        s = pl.program_id(2)

        @pl.when(s == 0)
        def _():
            if add is None:
                o_ref[...] = jnp.zeros_like(o_ref)
            else:
                o_ref[...] = refs[na + 1][...]

        for idx in range(na):
            @pl.when(s // spa == idx)
            def _(idx=idx):
                o_ref[...] += _dot_nt(a_refs[idx][...].astype(bf16), w_ref[0])

    def a_map(idx):
        return lambda ko, i, s: (i, jnp.clip(s - idx * spa, 0, spa - 1))

    in_specs = [pl.BlockSpec((tm, n_s), a_map(idx)) for idx in range(na)]
    in_specs.append(pl.BlockSpec((1, tko, n_s), lambda ko, i, s: (s, ko, 0)))
    args = list(a_list) + [w3]
    if add is not None:
        in_specs.append(pl.BlockSpec((tm, tko), lambda ko, i, s: (i, ko)))
        args.append(add)
    return pl.pallas_call(
        body, name=name, grid=(k_out // tko, m // tm, s_all), in_specs=in_specs,
        out_specs=pl.BlockSpec((tm, tko), lambda ko, i, s: (i, ko)),
        out_shape=jax.ShapeDtypeStruct((m, k_out), out_dtype),
        compiler_params=_params("parallel", "parallel", "arbitrary"))(*args)


def mm_tn(a, b_list, c_arr, *, name, n_s, shard_rows, tki=None):
    m, k_in = a.shape
    na = len(b_list)
    s_all = sum(b.shape[1] for b in b_list) // n_s
    spa = s_all // na
    tki = k_in if tki is None else tki
    tm = _row_tile(m, tki * n_s * 4, tki * _isz(a) + na * n_s * _isz(b_list[0]))

    nsteps = m // tm
    per_blk = tki // shard_rows
    half = shard_rows // 2

    def body(c_ref, *refs):
        a_ref = refs[0]
        b_refs = refs[1:1 + na]
        o_ref, ob_ref = refs[-2], refs[-1]
        s = pl.program_id(0)
        r = pl.program_id(2)

        @pl.when(r == 0)
        def _():
            o_ref[...] = jnp.zeros_like(o_ref)

        for idx in range(na):
            @pl.when(s // spa == idx)
            def _(idx=idx):
                o_ref[0] += _dot_tn(a_ref[...].astype(bf16), b_refs[idx][...].astype(bf16))

        @pl.when(r == nsteps - 1)
        def _():
            for q in range(per_blk):
                start = pl.multiple_of(q * shard_rows + (1 - c_ref[0]) * half, 16)
                ob_ref[q] = o_ref[0, pl.ds(start, half), :].astype(bf16)

    def b_map(idx):
        def index(s, ki, r, c_ref):
            active = (s // spa) == idx
            return (jnp.where(active, r, 0), jnp.clip(s - idx * spa, 0, spa - 1))
        return index

    in_specs = [pl.BlockSpec((tm, tki), lambda s, ki, r, c_ref: (r, ki))]
    in_specs += [pl.BlockSpec((tm, n_s), b_map(idx)) for idx in range(na)]
    n_blk = k_in // tki
    return pl.pallas_call(
        body, name=name,
        grid_spec=pltpu.PrefetchScalarGridSpec(
            num_scalar_prefetch=1, grid=(s_all, n_blk, nsteps), in_specs=in_specs,
            out_specs=[pl.BlockSpec((1, tki, n_s), lambda s, ki, r, c_ref: (s, ki, 0)),
                       pl.BlockSpec((per_blk, half, n_s), lambda s, ki, r, c_ref: (s * n_blk + ki, 0, 0))]),
        out_shape=[jax.ShapeDtypeStruct((s_all, k_in, n_s), f32),
                   jax.ShapeDtypeStruct((s_all * k_in // shard_rows, half, n_s), bf16)],
        compiler_params=_params("parallel", "parallel", "arbitrary"))(c_arr, a, *b_list)


def mm_nt_rms_bwd(a_list, w3, x, r, g, dx_in, *, name, extra=None):
    s_all, d, n_s = w3.shape
    m = a_list[0].shape[0]
    na = len(a_list)
    spa = s_all // na
    ne = 0 if extra is None else 1
    tm = _row_tile(m, d * n_s * 2, na * n_s * _isz(a_list[0]) + d * 4 * (4 + ne))

    def body(*refs):
        a_refs, w_ref = refs[:na], refs[na]
        x_ref, r_ref, g_ref, dxin_ref = refs[na + 1:na + 5]
        dh2_ref, g2_ref = (refs[na + 5], refs[na + 6]) if ne else (None, None)
        outs = refs[na + 5 + 2 * ne:]
        dx_ref, dg_ref = outs[0], outs[1]
        dg2_ref = outs[2] if ne else None
        acc_ref = outs[-1]
        i, s = pl.program_id(0), pl.program_id(1)

        @pl.when(s == 0)
        def _():
            acc_ref[...] = jnp.zeros_like(acc_ref)

        for idx in range(na):
            @pl.when(s // spa == idx)
            def _(idx=idx):
                acc_ref[...] += _dot_nt(a_refs[idx][...].astype(bf16), w_ref[0])

        @pl.when(s == s_all - 1)
        def _():
            rv = r_ref[...]
            xh = x_ref[...] * rv
            total = dxin_ref[...]
            pairs = [(acc_ref[...], g_ref, dg_ref)] + ([(dh2_ref[...], g2_ref, dg2_ref)] if ne else [])
            for dh, gain_ref, dgain_ref in pairs:
                part = jnp.sum(dh * xh, axis=0, keepdims=True)

                @pl.when(i == 0)
                def _(dgain_ref=dgain_ref, part=part):
                    dgain_ref[...] = part

                @pl.when(i > 0)
                def _(dgain_ref=dgain_ref, part=part):
                    dgain_ref[...] += part

                tg = dh * gain_ref[...]
                total = total + rv * (tg - xh * jnp.mean(tg * xh, axis=1, keepdims=True))
            dx_ref[...] = total

    def a_map(idx):
        return lambda i, s: (i, jnp.clip(s - idx * spa, 0, spa - 1))

    row = pl.BlockSpec((tm, d), lambda i, s: (i, 0))
    vec = pl.BlockSpec((1, d), lambda i, s: (0, 0))
    in_specs = [pl.BlockSpec((tm, n_s), a_map(idx)) for idx in range(na)]
    in_specs += [pl.BlockSpec((1, d, n_s), lambda i, s: (s, 0, 0)), row, pl.BlockSpec((tm, 1), lambda i, s: (i, 0)), vec, row]
    args = list(a_list) + [w3, x, r, g, dx_in]
    if ne:
        in_specs += [row, vec]
        args += list(extra)
    outs = pl.pallas_call(
        body, name=name, grid=(m // tm, s_all), in_specs=in_specs, out_specs=[row] + [vec] * (1 + ne),
        out_shape=[jax.ShapeDtypeStruct((m, d), f32)] + [jax.ShapeDtypeStruct((1, d), f32)] * (1 + ne),
        scratch_shapes=[pltpu.VMEM((tm, d), f32)],
        compiler_params=_params("arbitrary", "arbitrary"))(*args)
    return outs


def mm_residual(a, w, x, *, name, gains=(), target=None):
    m, k = a.shape
    d = w.shape[1]
    ng = len(gains)
    tm = _row_tile(m, k * d * 2, k * _isz(a) + d * 4 * 3 + ng * d * 2)

    def body(*refs):
        a_ref, w_ref, x_ref = refs[:3]
        y = _dot(a_ref[...].astype(bf16), w_ref[...]) + x_ref[...]
        if target is None:
            g_refs = refs[3:3 + ng]
            y_ref = refs[3 + ng]
            h_refs = refs[4 + ng:4 + 2 * ng]
            r_ref = refs[-1]
            y_ref[...] = y
            r = lax.rsqrt(jnp.mean(y * y, axis=1, keepdims=True) + EPS)
            yh = y * r
            for g_ref, h_ref in zip(g_refs, h_refs):
                h_ref[...] = (yh * g_ref[...]).astype(bf16)
            r_ref[...] = r
        else:
            t_ref, dy_ref, s_ref = refs[3:]
            i = pl.program_id(0)
            e = y - t_ref[...]
            dy_ref[...] = e * (1.0 / d)
            part = jnp.sum(e * e, axis=0, keepdims=True)

            @pl.when(i == 0)
            def _():
                s_ref[...] = part

            @pl.when(i > 0)
            def _():
                s_ref[...] += part

    row = pl.BlockSpec((tm, d), lambda i: (i, 0))
    vec = pl.BlockSpec((1, d), lambda i: (0, 0))
    in_specs = [pl.BlockSpec((tm, k), lambda i: (i, 0)), pl.BlockSpec((k, d), lambda i: (0, 0)), row]
    if target is None:
        outs = pl.pallas_call(
            body, name=name, grid=(m // tm,), in_specs=in_specs + [vec] * ng,
            out_specs=[row] * (1 + ng) + [pl.BlockSpec((tm, 1), lambda i: (i, 0))],
            out_shape=[jax.ShapeDtypeStruct((m, d), f32)] + [jax.ShapeDtypeStruct((m, d), bf16)] * ng
            + [jax.ShapeDtypeStruct((m, 1), f32)],
            compiler_params=_params("parallel"))(a, w, x, *gains)
        return outs[0], outs[1:1 + ng], outs[-1]
    return pl.pallas_call(
        body, name=name, grid=(m // tm,), in_specs=in_specs + [row], out_specs=[row, vec],
        out_shape=[jax.ShapeDtypeStruct((m, d), f32), jax.ShapeDtypeStruct((1, d), f32)],
        compiler_params=_params("arbitrary"))(a, w, x, target)


def rms_fwd(x, gains, *, name, tr=512):
    t, d = x.shape
    tr = min(tr, t)
    ng = len(gains)

    def body(*refs):
        x_ref = refs[0]
        g_refs = refs[1:1 + ng]
        h_refs = refs[1 + ng:1 + 2 * ng]
        r_ref = refs[-1]
        xv = x_ref[...]
        r = lax.rsqrt(jnp.mean(xv * xv, axis=1, keepdims=True) + EPS)
        xh = xv * r
        for g_ref, h_ref in zip(g_refs, h_refs):
            h_ref[...] = (xh * g_ref[...]).astype(bf16)
        r_ref[...] = r

    row = pl.BlockSpec((tr, d), lambda i: (i, 0))
    vec = pl.BlockSpec((1, d), lambda i: (0, 0))
    outs = pl.pallas_call(
        body, name=name, grid=(t // tr,), in_specs=[row] + [vec] * ng,
        out_specs=[row] * ng + [pl.BlockSpec((tr, 1), lambda i: (i, 0))],
        out_shape=[jax.ShapeDtypeStruct((t, d), bf16)] * ng + [jax.ShapeDtypeStruct((t, 1), f32)],
        compiler_params=_params("parallel"))(x, *gains)
    return outs[:ng], outs[ng]


def sgu_gate_fwd(zuv, gv, wc, bt, *, name, tr=512):
    t, w = zuv.shape[0], zuv.shape[1] // 2
    tr = min(tr, t)
    groups = w // LANES

    def body(zu_ref, zv_ref, gv_ref, wc_ref, bt_ref, y_ref):
        vp = _gelu(zv_ref[...])
        rv = lax.rsqrt(jnp.mean(vp * vp, axis=1, keepdims=True) + EPS)
        vb = (vp * rv * gv_ref[...]).astype(bf16)
        for c in range(tr // CHUNK):
            rows = slice(c * CHUNK, (c + 1) * CHUNK)
            for g in range(groups):
                cols = slice(g * LANES, (g + 1) * LANES)
                sv = _dot(wc_ref[g], vb[rows, cols]) + bt_ref[:, g:g + 1]
                y_ref[rows, cols] = (_gelu(zu_ref[rows, cols]) * sv).astype(bf16)

    row = pl.BlockSpec((tr, w), lambda i: (i, 0))
    return pl.pallas_call(
        body, name=name, grid=(t // tr,),
        in_specs=[row, pl.BlockSpec((tr, w), lambda i: (i, 1)), pl.BlockSpec((1, w), lambda i: (0, 0)),
                  pl.BlockSpec((groups, CHUNK, CHUNK), lambda i: (0, 0, 0)),
                  pl.BlockSpec((CHUNK, groups), lambda i: (0, 0))],
        out_specs=row, out_shape=jax.ShapeDtypeStruct((t, w), bf16),
        compiler_params=_params("parallel"))(zuv, zuv, gv, wc, bt)


def sgu_gate_bwd(zuv, dy, gv, wc, bt, *, name, tr=512):
    t, w = zuv.shape[0], zuv.shape[1] // 2
    tr = min(tr, t)
    groups = w // LANES
    nsteps = t // tr

    def body(zu_ref, zv_ref, dy_ref, gv_ref, wc_ref, bt_ref,
             dzu_ref, dzv_ref, dgv_ref, dws_ref, dbt_ref, dv_ref, bacc_ref):
        i = pl.program_id(0)

        @pl.when(i == 0)
        def _():
            dgv_ref[...] = jnp.zeros_like(dgv_ref)
            dws_ref[...] = jnp.zeros_like(dws_ref)
            bacc_ref[...] = jnp.zeros_like(bacc_ref)

        vp, vp_grad = _gelu_and_grad(zv_ref[...])
        rv = lax.rsqrt(jnp.mean(vp * vp, axis=1, keepdims=True) + EPS)
        vhat = vp * rv
        vb = (vhat * gv_ref[...]).astype(bf16)
        for c in range(tr // CHUNK):
            rows = slice(c * CHUNK, (c + 1) * CHUNK)
            for g in range(groups):
                cols = slice(g * LANES, (g + 1) * LANES)
                vblk = vb[rows, cols]
                sv = _dot(wc_ref[g], vblk) + bt_ref[:, g:g + 1]
                zub = zu_ref[rows, cols]
                dyb = dy_ref[rows, cols]
                ub, ub_grad = _gelu_and_grad(zub)
                dzu_ref[rows, cols] = (dyb * sv * ub_grad).astype(bf16)
                dsv = dyb * ub
                bacc_ref[:, cols] += dsv
                dsvb = dsv.astype(bf16)
                dv_ref[rows, cols] = _dot_tn(wc_ref[g], dsvb)
                dws_ref[g] += _dot_nt(dsvb, vblk)
        dv = dv_ref[...]
        dgv_ref[...] += jnp.sum(dv * vhat, axis=0, keepdims=True)
        tg = dv * gv_ref[...]
        dvp = rv * (tg - vhat * jnp.mean(tg * vhat, axis=1, keepdims=True))
        dzv_ref[...] = (dvp * vp_grad).astype(bf16)

        @pl.when(i == nsteps - 1)
        def _():
            tt = lax.broadcasted_iota(jnp.int32, (CHUNK, CHUNK), 0)
            ss = lax.broadcasted_iota(jnp.int32, (CHUNK, CHUNK), 1)
            for g in range(groups):
                dws_ref[g] = jnp.where(ss <= tt, dws_ref[g], 0.0)
                dbt_ref[:, g:g + 1] = jnp.sum(bacc_ref[:, g * LANES:(g + 1) * LANES], axis=1, keepdims=True)

    row = pl.BlockSpec((tr, w), lambda i: (i, 0))
    full3 = pl.BlockSpec((groups, CHUNK, CHUNK), lambda i: (0, 0, 0))
    return pl.pallas_call(
        body, name=name, grid=(nsteps,),
        in_specs=[row, pl.BlockSpec((tr, w), lambda i: (i, 1)), row, pl.BlockSpec((1, w), lambda i: (0, 0)), full3,
                  pl.BlockSpec((CHUNK, groups), lambda i: (0, 0))],
        out_specs=[row, row, pl.BlockSpec((1, w), lambda i: (0, 0)), full3,
                   pl.BlockSpec((CHUNK, groups), lambda i: (0, 0))],
        out_shape=[jax.ShapeDtypeStruct((t, w), bf16), jax.ShapeDtypeStruct((t, w), bf16),
                   jax.ShapeDtypeStruct((1, w), f32), jax.ShapeDtypeStruct((groups, CHUNK, CHUNK), f32),
                   jax.ShapeDtypeStruct((CHUNK, groups), f32)],
        scratch_shapes=[pltpu.VMEM((tr, w), f32), pltpu.VMEM((CHUNK, w), f32)],
        compiler_params=_params("arbitrary"))(zuv, zuv, dy, gv, wc, bt)


HALO = 8


def _shift_down(v, halo, k, first):
    r = pltpu.roll(v, k, 0)
    hh = jnp.where(first, 0.0, pltpu.roll(halo, k, 0))
    rid = lax.broadcasted_iota(jnp.int32, (HALO, v.shape[1]), 0)
    head = jnp.where(rid < k, hh, r[0:HALO])
    if v.shape[0] == HALO:
        return head
    return jnp.concatenate([head, r[HALO:]], axis=0)


def _shift_up(v, halo, k, last):
    n = v.shape[0]
    r = pltpu.roll(v, n - k, 0)
    hh = jnp.where(last, 0.0, pltpu.roll(halo, HALO - k, 0))
    rid = lax.broadcasted_iota(jnp.int32, (HALO, v.shape[1]), 0)
    tail = jnp.where(rid >= HALO - k, hh, r[n - HALO:])
    return jnp.concatenate([r[:n - HALO], tail], axis=0)


def _conv(p, halo, w_ref, b_ref, first):
    return (w_ref[2:3, :] * p + w_ref[1:2, :] * _shift_down(p, halo, 1, first)
            + w_ref[0:1, :] * _shift_down(p, halo, 2, first) + b_ref[...])


BF16_ROWS = 16


def ffn_in_fused(h, w_in4, wg, wu, bg, bu, *, name):
    t, k = h.shape
    s_all, _, n_s = w_in4.shape
    half = s_all // 2
    tm = _row_tile(t, 2 * k * n_s * 2, k * 2 + 4 * n_s * 4 + n_s * 2)

    def body(h_ref, hh_ref, wg_ref, wu_ref, cg_ref, cu_ref, bg_ref, bu_ref, pg_ref, pu_ref, gate_ref, up_ref, a_ref):
        first = pl.program_id(1) == 0
        hv, hh = h_ref[...], hh_ref[...]
        outs = []
        for w_ref, c_ref, b_ref, p_ref, o_ref in ((wg_ref, cg_ref, bg_ref, pg_ref, gate_ref),
                                                  (wu_ref, cu_ref, bu_ref, pu_ref, up_ref)):
            p = _dot(hv, w_ref[0])
            p_ref[...] = p
            hu = _conv(p, _dot(hh, w_ref[0])[BF16_ROWS - HALO:], c_ref, b_ref, first)
            o_ref[...] = hu
            outs.append(hu)
        gate, up = outs
        a_ref[...] = (gate * jax.nn.sigmoid(gate) * up).astype(bf16)

    tile = pl.BlockSpec((tm, n_s), lambda j, i: (i, j))
    cw = pl.BlockSpec((3, n_s), lambda j, i: (0, j))
    cb = pl.BlockSpec((1, n_s), lambda j, i: (0, j))
    f = half * n_s
    return pl.pallas_call(
        body, name=name, grid=(half, t // tm),
        in_specs=[pl.BlockSpec((tm, k), lambda j, i: (i, 0)),
                  pl.BlockSpec((BF16_ROWS, k), lambda j, i: (jnp.maximum(i * (tm // BF16_ROWS) - 1, 0), 0)),
                  pl.BlockSpec((1, k, n_s), lambda j, i: (j, 0, 0)),
                  pl.BlockSpec((1, k, n_s), lambda j, i: (j + half, 0, 0)), cw, cw, cb, cb],
        out_specs=[tile] * 5,
        out_shape=[jax.ShapeDtypeStruct((t, f), f32)] * 4 + [jax.ShapeDtypeStruct((t, f), bf16)],
        compiler_params=_params("parallel", "parallel"))(h, h, w_in4, w_in4, wg, wu, bg, bu)


def _gate_grads(gate, up, dav):
    sg = jax.nn.sigmoid(gate)
    return dav * up * (sg * (1.0 + gate * (1.0 - sg))), dav * gate * sg


GATE_BWD_ROWS = 512


def ffn_gate_bwd(dy, w_out, pg, pu, gate, up, wg, wu, *, name):
    t, f = pg.shape
    d = dy.shape[1]
    tr = min(GATE_BWD_ROWS, t)
    nsteps = t // tr
    tc = f // 2

    def body(dy_ref, dyn_ref, w_ref, pg_ref, pu_ref, gate_ref, gaten_ref, up_ref, upn_ref, wg_ref, wu_ref,
             dg_ref, du_ref, sg_ref, su_ref):
        i = pl.program_id(1)
        last = i == nsteps - 1
        w = w_ref[0]
        da = _dot_nt(dy_ref[...].astype(bf16), w)
        da_n = _dot_nt(dyn_ref[...].astype(bf16), w)
        dgate, dup = _gate_grads(gate_ref[...], up_ref[...], da)
        dgate_n, dup_n = _gate_grads(gaten_ref[...], upn_ref[...], da_n)
        rid = lax.broadcasted_iota(jnp.int32, (8, tc), 0)
        for dd, d_n, c_ref, p_ref, o_ref, s_ref in ((dgate, dgate_n, wg_ref, pg_ref, dg_ref, sg_ref),
                                                    (dup, dup_n, wu_ref, pu_ref, du_ref, su_ref)):
            d1, d2 = _shift_up(dd, d_n, 1, last), _shift_up(dd, d_n, 2, last)
            o_ref[...] = (c_ref[2:3, :] * dd + c_ref[1:2, :] * d1 + c_ref[0:1, :] * d2).astype(bf16)
            p = p_ref[...]
            sums = [jnp.sum(d2 * p, axis=0, keepdims=True), jnp.sum(d1 * p, axis=0, keepdims=True),
                    jnp.sum(dd * p, axis=0, keepdims=True), jnp.sum(dd, axis=0, keepdims=True)]
            part = jnp.zeros((8, tc), f32)
            for k, sk in enumerate(sums):
                part = jnp.where(rid == k, sk, part)

            @pl.when(i == 0)
            def _(s_ref=s_ref, part=part):
                s_ref[...] = part

            @pl.when(i > 0)
            def _(s_ref=s_ref, part=part):
                s_ref[...] += part

    def nxt_rows(j, i):
        return (jnp.minimum((i + 1) * (tr // HALO), t // HALO - 1), j)

    tile = pl.BlockSpec((tr, tc), lambda j, i: (i, j))
    nxt = pl.BlockSpec((HALO, tc), nxt_rows)
    wspec = pl.BlockSpec((3, tc), lambda j, i: (0, j))
    stat = pl.BlockSpec((8, tc), lambda j, i: (0, j))
    return pl.pallas_call(
        body, name=name, grid=(2, nsteps),
        in_specs=[pl.BlockSpec((tr, d), lambda j, i: (i, 0)),
                  pl.BlockSpec((HALO, d), lambda j, i: (nxt_rows(j, i)[0], 0)),
                  pl.BlockSpec((1, tc, d), lambda j, i: (j, 0, 0)),
                  tile, tile, tile, nxt, tile, nxt, wspec, wspec],
        out_specs=[tile, tile, stat, stat],
        out_shape=[jax.ShapeDtypeStruct((t, f), bf16), jax.ShapeDtypeStruct((t, f), bf16),
                   jax.ShapeDtypeStruct((8, f), f32), jax.ShapeDtypeStruct((8, f), f32)],
        compiler_params=_params("parallel", "arbitrary"))(
            dy, dy, w_out.reshape(2, tc, d), pg, pu, gate, gate, up, up, wg, wu)


def _head_mean_matrix():
    i = lax.broadcasted_iota(jnp.int32, (LANES, LANES), 0) // HEAD_DIM
    j = lax.broadcasted_iota(jnp.int32, (LANES, LANES), 1) // HEAD_DIM
    return jnp.where(i == j, 1.0 / HEAD_DIM, 0.0).astype(bf16)


def _lane_half(shape):
    return (lax.broadcasted_iota(jnp.int32, shape, 1) % LANES) // HEAD_DIM


def q_proj_norm(h, w, g2, *, name, scale):
    t, k = h.shape
    n = w.shape[1]
    tm = _row_tile(t, k * n * 2, k * 2 + n * 4 + n * 2)

    def body(h_ref, w_ref, g_ref, qp_ref, o_ref):
        qp_ref[...] = _dot(h_ref[...], w_ref[...])
        bd = _head_mean_matrix()
        for cb in range(n // LANES):
            cols = slice(cb * LANES, (cb + 1) * LANES)
            xc = qp_ref[:, cols]
            rh = lax.rsqrt(_dot_split(xc * xc, bd) + EPS)
            o_ref[:, cols] = (xc * rh * g_ref[...] * scale).astype(bf16)

    row = pl.BlockSpec((tm, n), lambda i: (i, 0))
    return pl.pallas_call(
        body, name=name, grid=(t // tm,),
        in_specs=[pl.BlockSpec((tm, k), lambda i: (i, 0)), pl.BlockSpec((k, n), lambda i: (0, 0)),
                  pl.BlockSpec((1, LANES), lambda i: (0, 0))],
        out_specs=[row, row], out_shape=[jax.ShapeDtypeStruct((t, n), f32), jax.ShapeDtypeStruct((t, n), bf16)],
        compiler_params=_params("parallel"))(h, w, g2)


def kv_proj_post(h, w, g2, *, name):
    t, k = h.shape
    n = w.shape[1]
    kw = n // 2
    tm = _row_tile(t, k * n * 2, k * 2 + n * 4 + 2 * n * 2)

    def body(h_ref, w_ref, g_ref, kv_ref, k_ref, v_ref):
        kv_ref[...] = _dot(h_ref[...], w_ref[...])
        bd = _head_mean_matrix()
        half = _lane_half((tm, LANES))
        for cb in range(kw // LANES):
            xc = kv_ref[:, cb * LANES:(cb + 1) * LANES]
            rh = lax.rsqrt(_dot_split(xc * xc, bd) + EPS)
            kn = xc * rh * g_ref[...]
            vc = kv_ref[:, kw + cb * LANES:kw + (cb + 1) * LANES]
            for src, dst in ((kn, k_ref), (vc, v_ref)):
                sw = pltpu.roll(src, HEAD_DIM, 1)
                for hf in range(2):
                    blk = 2 * cb + hf
                    dst[:, blk * LANES:(blk + 1) * LANES] = jnp.where(half == hf, src, sw).astype(bf16)

    row = pl.BlockSpec((tm, n), lambda i: (i, 0))
    return pl.pallas_call(
        body, name=name, grid=(t // tm,),
        in_specs=[pl.BlockSpec((tm, k), lambda i: (i, 0)), pl.BlockSpec((k, n), lambda i: (0, 0)),
                  pl.BlockSpec((1, LANES), lambda i: (0, 0))],
        out_specs=[row, row, row],
        out_shape=[jax.ShapeDtypeStruct((t, n), f32), jax.ShapeDtypeStruct((t, n), bf16), jax.ShapeDtypeStruct((t, n), bf16)],
        compiler_params=_params("parallel"))(h, w, g2)


def q_norm_bwd(dq, qp, g2, *, name, scale, tr=512):
    t, w = qp.shape
    tr = min(tr, t)

    def body(dq_ref, x_ref, g_ref, o_ref, dg_ref):
        i = pl.program_id(0)
        bd = _head_mean_matrix()
        acc = jnp.zeros((1, LANES), f32)
        for cb in range(w // LANES):
            cols = slice(cb * LANES, (cb + 1) * LANES)
            xc = x_ref[:, cols]
            rh = lax.rsqrt(_dot_split(xc * xc, bd) + EPS)
            xh = xc * rh
            dy = dq_ref[:, cols] * scale
            acc = acc + jnp.sum(dy * xh, axis=0, keepdims=True)
            tg = dy * g_ref[...]
            o_ref[:, cols] = (rh * (tg - xh * _dot_split(tg * xh, bd))).astype(bf16)

        @pl.when(i == 0)
        def _():
            dg_ref[...] = acc

        @pl.when(i > 0)
        def _():
            dg_ref[...] += acc

    row = pl.BlockSpec((tr, w), lambda i: (i, 0))
    vec = pl.BlockSpec((1, LANES), lambda i: (0, 0))
    return pl.pallas_call(
        body, name=name, grid=(t // tr,), in_specs=[row, row, vec], out_specs=[row, vec],
        out_shape=[jax.ShapeDtypeStruct((t, w), bf16), jax.ShapeDtypeStruct((1, LANES), f32)],
        compiler_params=_params("arbitrary"))(dq, qp, g2)


def kv_post_bwd(dk2, dv2, kv, g2, *, name, tr=512):
    t, w = kv.shape
    tr = min(tr, t)
    kw = w // 2

    def body(dk_ref, dv_ref, x_ref, g_ref, o_ref, dg_ref):
        i = pl.program_id(0)
        bd = _head_mean_matrix()
        half = _lane_half((tr, LANES))
        acc = jnp.zeros((1, LANES), f32)

        def fold(ref, cb):
            a = ref[:, (2 * cb) * LANES:(2 * cb + 1) * LANES]
            b = ref[:, (2 * cb + 1) * LANES:(2 * cb + 2) * LANES]
            return jnp.where(half == 0, a + pltpu.roll(a, HEAD_DIM, 1), b + pltpu.roll(b, HEAD_DIM, 1))

        for cb in range(kw // LANES):
            cols = slice(cb * LANES, (cb + 1) * LANES)
            xc = x_ref[:, cols]
            rh = lax.rsqrt(_dot_split(xc * xc, bd) + EPS)
            xh = xc * rh
            dy = fold(dk_ref, cb)
            acc = acc + jnp.sum(dy * xh, axis=0, keepdims=True)
            tg = dy * g_ref[...]
            o_ref[:, cols] = (rh * (tg - xh * _dot_split(tg * xh, bd))).astype(bf16)
            o_ref[:, kw + cb * LANES:kw + (cb + 1) * LANES] = fold(dv_ref, cb).astype(bf16)

        @pl.when(i == 0)
        def _():
            dg_ref[...] = acc

        @pl.when(i > 0)
        def _():
            dg_ref[...] += acc

    dup = pl.BlockSpec((tr, 2 * kw), lambda i: (i, 0))
    row = pl.BlockSpec((tr, w), lambda i: (i, 0))
    vec = pl.BlockSpec((1, LANES), lambda i: (0, 0))
    return pl.pallas_call(
        body, name=name, grid=(t // tr,), in_specs=[dup, dup, row, vec], out_specs=[row, vec],
        out_shape=[jax.ShapeDtypeStruct((t, w), bf16), jax.ShapeDtypeStruct((1, LANES), f32)],
        compiler_params=_params("arbitrary"))(dk2, dv2, kv, g2)


def _slope(h):
    return 2.0 ** (-8.0 * (h + 1) / N_Q_HEADS)


GROUP_ROWS = Q_PER_KV * CHUNK


def _band_mask(n):
    tq = lax.broadcasted_iota(jnp.int32, (GROUP_ROWS, 2 * CHUNK), 0) % CHUNK
    jk = lax.broadcasted_iota(jnp.int32, (GROUP_ROWS, 2 * CHUNK), 1)
    dist = tq + CHUNK - jk
    ok = (dist >= 0) & (dist < CHUNK) & jnp.logical_not((n == 0) & (jk < CHUNK))
    return dist.astype(f32), ok


def _band(ref, n, kh):
    p0 = pl.multiple_of(jnp.maximum(n - 1, 0) * CHUNK, CHUNK)
    c0 = pl.multiple_of(n * CHUNK, CHUNK)
    cols = slice(kh * LANES, (kh + 1) * LANES)
    return jnp.concatenate([ref[pl.ds(p0, CHUNK), cols], ref[pl.ds(c0, CHUNK), cols]], axis=0)


def _stack_heads(ref, kh, half):
    parts = []
    for cb in (2 * kh, 2 * kh + 1):
        xc = ref[:, cb * LANES:(cb + 1) * LANES].astype(f32)
        parts += [jnp.where(half == hf, xc, 0.0).astype(bf16) for hf in range(2)]
    return jnp.concatenate(parts, axis=0)


def _unstack_heads(x4, half):
    return (jnp.where(half == 0, x4[0:CHUNK], x4[CHUNK:2 * CHUNK]),
            jnp.where(half == 0, x4[2 * CHUNK:3 * CHUNK], x4[3 * CHUNK:]))


def _per_head_column(kh, values):
    grp = lax.broadcasted_iota(jnp.int32, (GROUP_ROWS, 1), 0) // CHUNK
    col = jnp.full((GROUP_ROWS, 1), values[0], f32)
    for g in range(1, Q_PER_KV):
        col = jnp.where(grp == g, values[g], col)
    return col


def _softmax_band(q4, kband, dist, ok, slope, sink):
    s = _dot_nt(q4, kband)
    s = jnp.where(ok, s - slope * dist, -jnp.inf)
    m = jnp.maximum(jnp.max(s, axis=1, keepdims=True), sink)
    e = jnp.exp(s - m)
    es = jnp.exp(sink - m)
    den = jnp.sum(e, axis=1, keepdims=True) + es
    return e / den, es / den


def attn_fwd(q, k2, v2, sinks, *, name):
    t, w = q.shape
    nb = t // CHUNK

    def body(sink_ref, q_ref, k_ref, v_ref, o_ref):
        n = pl.program_id(0)
        dist, ok = _band_mask(n)
        half = _lane_half((CHUNK, LANES))
        khs = range(N_KV_HEADS)
        heads = [[Q_PER_KV * kh + g for g in range(Q_PER_KV)] for kh in khs]
        q4 = [_stack_heads(q_ref, kh, half) for kh in khs]
        soft = [_softmax_band(q4[kh], _band(k_ref, n, kh), dist, ok, _per_head_column(kh, [_slope(h) for h in heads[kh]]),
                              _per_head_column(kh, [sink_ref[h] for h in heads[kh]])) for kh in khs]
        o4 = [_dot(soft[kh][0].astype(bf16), _band(v_ref, n, kh)) for kh in khs]
        for kh in khs:
            lo, hi = _unstack_heads(o4[kh], half)
            o_ref[:, (2 * kh) * LANES:(2 * kh + 1) * LANES] = lo.astype(bf16)
            o_ref[:, (2 * kh + 1) * LANES:(2 * kh + 2) * LANES] = hi.astype(bf16)

    full = pl.BlockSpec((t, k2.shape[1]), lambda n: (0, 0))
    return pl.pallas_call(
        body, name=name, grid=(nb,),
        in_specs=[pl.BlockSpec(memory_space=pltpu.SMEM), pl.BlockSpec((CHUNK, w), lambda n: (n, 0)), full, full],
        out_specs=pl.BlockSpec((CHUNK, w), lambda n: (n, 0)),
        out_shape=jax.ShapeDtypeStruct((t, w), bf16),
        compiler_params=_params("parallel"))(sinks, q, k2, v2)


def attn_bwd(q, k2, v2, dy, w_o, sinks, *, name):
    t, w = q.shape
    nb = t // CHUNK
    kw = k2.shape[1]
    d = dy.shape[1]

    def body(sink_ref, q_ref, k_ref, v_ref, dy_ref, wo_ref, dq_ref, dk_ref, dv_ref, ds_ref, kc_ref, vc_ref, do_ref):
        n = pl.program_id(0)

        @pl.when(n == 0)
        def _():
            ds_ref[...] = jnp.zeros_like(ds_ref)
            kc_ref[...] = jnp.zeros_like(kc_ref)
            vc_ref[...] = jnp.zeros_like(vc_ref)
            dk_ref[...] = jnp.zeros_like(dk_ref)
            dv_ref[...] = jnp.zeros_like(dv_ref)

        @pl.when(n == nb)
        def _():
            dk_ref[...] = kc_ref[...]
            dv_ref[...] = vc_ref[...]

        @pl.when(n < nb)
        def _():
            dist, ok = _band_mask(n)
            half = _lane_half((CHUNK, LANES))
            lane = lax.broadcasted_iota(jnp.int32, (1, LANES), 1)
            sink_acc = jnp.zeros((1, LANES), f32)
            khs = range(N_KV_HEADS)
            heads = [[Q_PER_KV * kh + g for g in range(Q_PER_KV)] for kh in khs]
            q4 = [_stack_heads(q_ref, kh, half) for kh in khs]
            do_ref[...] = _dot_nt(dy_ref[...].astype(bf16), wo_ref[...])
            do4 = [_stack_heads(do_ref, kh, half) for kh in khs]
            kband = [_band(k_ref, n, kh) for kh in khs]
            vband = [_band(v_ref, n, kh) for kh in khs]
            soft = [_softmax_band(q4[kh], kband[kh], dist, ok, _per_head_column(kh, [_slope(h) for h in heads[kh]]),
                                  _per_head_column(kh, [sink_ref[h] for h in heads[kh]])) for kh in khs]
            dp = [_dot_nt(do4[kh], vband[kh]) for kh in khs]
            delta = [jnp.sum(soft[kh][0] * dp[kh], axis=1, keepdims=True) for kh in khs]
            dsb = [(soft[kh][0] * (dp[kh] - delta[kh])).astype(bf16) for kh in khs]
            dq4 = [_dot(dsb[kh], kband[kh]) for kh in khs]
            dkb = [_dot_tn(dsb[kh], q4[kh]) for kh in khs]
            dvb = [_dot_tn(soft[kh][0].astype(bf16), do4[kh]) for kh in khs]
            for kh in khs:
                sd = soft[kh][1] * delta[kh]
                for g, h in enumerate(heads[kh]):
                    part = jnp.sum(sd[g * CHUNK:(g + 1) * CHUNK], axis=0, keepdims=True)
                    sink_acc = sink_acc + jnp.where(lane == h, -part, 0.0)
                lo, hi = _unstack_heads(dq4[kh], half)
                dq_ref[:, (2 * kh) * LANES:(2 * kh + 1) * LANES] = lo
                dq_ref[:, (2 * kh + 1) * LANES:(2 * kh + 2) * LANES] = hi
                cols = slice(kh * LANES, (kh + 1) * LANES)
                dk_ref[:, cols] = kc_ref[:, cols] + dkb[kh][0:CHUNK]
                dv_ref[:, cols] = vc_ref[:, cols] + dvb[kh][0:CHUNK]
                kc_ref[:, cols] = dkb[kh][CHUNK:]
                vc_ref[:, cols] = dvb[kh][CHUNK:]
            ds_ref[...] += sink_acc

    full = pl.BlockSpec((t, kw), lambda n: (0, 0))
    qblk = pl.BlockSpec((CHUNK, w), lambda n: (jnp.minimum(n, nb - 1), 0))
    kblk = pl.BlockSpec((CHUNK, kw), lambda n: (jnp.maximum(n - 1, 0), 0))
    return pl.pallas_call(
        body, name=name, grid=(nb + 1,),
        in_specs=[pl.BlockSpec(memory_space=pltpu.SMEM), qblk, full, full,
                  pl.BlockSpec((CHUNK, d), lambda n: (jnp.minimum(n, nb - 1), 0)),
                  pl.BlockSpec((w, d), lambda n: (0, 0))],
        out_specs=[qblk, kblk, kblk, pl.BlockSpec((1, LANES), lambda n: (0, 0))],
        out_shape=[jax.ShapeDtypeStruct((t, w), f32), jax.ShapeDtypeStruct((t, kw), f32),
                   jax.ShapeDtypeStruct((t, kw), f32), jax.ShapeDtypeStruct((1, LANES), f32)],
        scratch_shapes=[pltpu.VMEM((CHUNK, kw), f32), pltpu.VMEM((CHUNK, kw), f32), pltpu.VMEM((CHUNK, w), f32)],
        compiler_params=_params("arbitrary"))(sinks, q, k2, v2, dy, w_o)


N_STEPS = 8


def _row_blocks(shape):
    if len(shape) == 2:
        r, c = shape
        return (r // N_STEPS, c), (lambda s: (s, 0))
    l, r, c = shape
    per = N_STEPS // l
    return (1, r // per, c), (lambda s: (s // per, s % per, 0))


CAST_STEPS = 4


def cast_into_slot(arrays, k_arr, *, name):
    in_specs, out_specs, out_shape, layers = [], [], [], []
    for a in arrays:
        r, c = a.shape[-2:]
        rb = r // CAST_STEPS
        if a.ndim == 2:
            in_specs.append(pl.BlockSpec((rb, c), lambda s, k: (s, 0)))
            layers.append(None)
        else:
            for l in range(a.shape[0]):
                in_specs.append(pl.BlockSpec((1, rb, c), lambda s, k, l=l: (l, s, 0)))
                layers.append(l)
        for _ in range(1 if a.ndim == 2 else a.shape[0]):
            out_specs.append(pl.BlockSpec((1, rb, c), lambda s, k: (k[0], s, 0)))
            out_shape.append(jax.ShapeDtypeStruct((N_SHARDS, r, c), bf16))
    n = len(in_specs)

    def body(k_ref, *refs):
        for i_ref, o_ref, l in zip(refs[:n], refs[n:], layers):
            o_ref[0] = (i_ref[...] if l is None else i_ref[0]).astype(bf16)

    args = []
    for a in arrays:
        args += [a] * (1 if a.ndim == 2 else a.shape[0])
    return pl.pallas_call(
        body, name=name,
        grid_spec=pltpu.PrefetchScalarGridSpec(num_scalar_prefetch=1, grid=(CAST_STEPS,),
                                               in_specs=in_specs, out_specs=out_specs),
        out_shape=out_shape, compiler_params=_params("parallel"))(k_arr, *args)


def adamw(ws, gs, ms, vs, *, name):
    n = len(ws)
    specs, g_specs, g_count = [], [], []
    for w, g_list in zip(ws, gs):
        blk, index = _row_blocks(w.shape)
        specs.append(pl.BlockSpec(blk, index))
        layers = len(g_list)
        per = N_STEPS // layers
        g_count.append(layers)
        for l in range(layers):
            g_specs.append(pl.BlockSpec(blk[-2:], lambda s, l=l, per=per: (jnp.where(s // per == l, s % per, 0), 0)))
    ng = len(g_specs)

    def body(*refs):
        s = pl.program_id(0)
        g_refs = refs[3 * n:3 * n + ng]
        outs = refs[3 * n + ng:]
        off = 0
        for i in range(n):
            w_ref, m_ref, v_ref = refs[i], refs[n + i], refs[2 * n + i]
            go_ref, d_ref, nm_ref, nv_ref = (outs[k * n + i] for k in range(4))
            layers = g_count[i]
            g = g_refs[off][...]
            for l in range(1, layers):
                g = jnp.where(s // (N_STEPS // layers) == l, g_refs[off + l][...], g)
            off += layers
            g = g.reshape(w_ref.shape)
            m = ADAM_B1 * m_ref[...] + (1.0 - ADAM_B1) * g
            v = ADAM_B2 * v_ref[...] + (1.0 - ADAM_B2) * (g * g)
            m_hat = m / ADAM_C1
            v_hat = v / ADAM_C2
            go_ref[...] = g
            d_ref[...] = -ADAM_LR * (m_hat / (jnp.sqrt(v_hat) + ADAM_EPS) + ADAM_WD * w_ref[...])
            nm_ref[...] = m
            nv_ref[...] = v

    outs = pl.pallas_call(
        body, name=name, grid=(N_STEPS,), in_specs=specs * 3 + g_specs, out_specs=specs * 4,
        out_shape=[jax.ShapeDtypeStruct(a.shape, f32) for a in ws] * 4,
        compiler_params=_params("parallel"))(*ws, *ms, *vs, *[g for g_list in gs for g in g_list])
    return [outs[k * n:(k + 1) * n] for k in range(4)]


def _adamw_update(w, g, m, v):
    m = ADAM_B1 * m + (1.0 - ADAM_B1) * g
    v = ADAM_B2 * v + (1.0 - ADAM_B2) * (g * g)
    m_hat = m / ADAM_C1
    v_hat = v / ADAM_C2
    return -ADAM_LR * (m_hat / (jnp.sqrt(v_hat) + ADAM_EPS) + ADAM_WD * w), m, v


def adamw_small(ws, gs, ms, vs, *, name):
    n = len(ws)

    def body(*refs):
        for i in range(n):
            w_ref, g_ref, m_ref, v_ref = (refs[k * n + i] for k in range(4))
            d_ref, nm_ref, nv_ref = (refs[(4 + k) * n + i] for k in range(3))
            d_ref[...], nm_ref[...], nv_ref[...] = _adamw_update(w_ref[...], g_ref[...], m_ref[...], v_ref[...])

    outs = pl.pallas_call(
        body, name=name, out_shape=[jax.ShapeDtypeStruct(a.shape, f32) for a in ws] * 3)(*ws, *gs, *ms, *vs)
    return outs[:n], outs[n:2 * n], outs[2 * n:]


def _place():
    return lax.axis_index("x"), lax.axis_index("y"), lax.axis_index("c")


def gather_shards(bufs, *, name, split):
    n = len(bufs)

    def body(*refs):
        bufs_ = refs[:n]
        isend, irecv, dsend, drecv = refs[2 * n:]
        x, y, c = _place()
        k = 2 * x + y
        peers = [(1 - x, y, c), (x, 1 - y, c), (1 - x, 1 - y, c)]
        peer_k = [2 * (1 - x) + y, 2 * x + (1 - y), 2 * (1 - x) + (1 - y)]

        def slab(a, q, h):
            if not split[a]:
                return bufs_[a].at[q]
            half = bufs_[a].shape[1] // 2
            return bufs_[a].at[q, pl.ds(pl.multiple_of(h * half, 16), half)]

        def ici(a, j, q):
            return pltpu.make_async_remote_copy(
                src_ref=slab(a, q, c), dst_ref=slab(a, q, c), send_sem=isend.at[3 * a + j], recv_sem=irecv.at[3 * a + j],
                device_id=peers[j], device_id_type=MESH)

        def d2d(a, j, h):
            return pltpu.make_async_remote_copy(
                src_ref=slab(a, peer_k[j], h), dst_ref=slab(a, peer_k[j], h), send_sem=dsend.at[3 * a + j],
                recv_sem=drecv.at[3 * a + j], device_id=(x, y, 1 - c), device_id_type=MESH)

        for a in range(n):
            for j in range(3):
                ici(a, j, k).start()
        for a in range(n):
            for j in range(3):
                ici(a, j, peer_k[j]).wait_recv()
                if split[a]:
                    d2d(a, j, c).start()
        for a in range(n):
            for j in range(3):
                if split[a]:
                    d2d(a, j, 1 - c).wait_recv()
        for a in range(n):
            for j in range(3):
                ici(a, j, k).wait_send()
                if split[a]:
                    d2d(a, j, c).wait_send()

    return pl.pallas_call(
        body, name=name, in_specs=[ANY] * n, out_specs=[ANY] * n,
        out_shape=[jax.ShapeDtypeStruct(b.shape, b.dtype) for b in bufs],
        input_output_aliases={i: i for i in range(n)},
        scratch_shapes=[pltpu.SemaphoreType.DMA((3 * n,))] * 4)(*bufs)


HBM = pl.BlockSpec(memory_space=pltpu.HBM)
SEM = pl.BlockSpec(memory_space=pltpu.SEMAPHORE)
DATAFLOW = pltpu.SideEffectType.DATAFLOW_SIDE_EFFECTING


def _chip_peers():
    x, y, c = _place()
    return 2 * x + y, [(1 - x, y, c), (x, 1 - y, c), (1 - x, 1 - y, c)], [2 * (1 - x) + y, 2 * x + (1 - y), 2 * (1 - x) + (1 - y)]


def _half_slab(ref, q, h):
    half = ref.shape[1] // 2
    return ref.at[q, pl.ds(pl.multiple_of(h * half, BF16_ROWS), half)]


def gather_start(bufs, groups, after, *, name):
    n = len(bufs)
    ng = len(groups)

    def body(*refs):
        ins = refs[:n]
        sends, recvs = refs[2 * n + 1:2 * n + 1 + ng], refs[2 * n + 1 + ng:2 * n + 1 + 2 * ng]
        token = refs[-1]
        c = lax.axis_index("c")
        k, peers, _ = _chip_peers()
        for gi, grp in enumerate(groups):
            for pos, a in enumerate(grp):
                for j in range(3):
                    pltpu.make_async_remote_copy(
                        src_ref=_half_slab(ins[a], k, c), dst_ref=_half_slab(ins[a], k, c), send_sem=sends[gi].at[3 * pos + j],
                        recv_sem=recvs[gi].at[3 * pos + j], device_id=peers[j], device_id_type=MESH).start()
        token[...] = jnp.zeros_like(token)

    sems = [pltpu.SemaphoreType.DMA((3 * len(grp),)) for grp in groups]
    outs = pl.pallas_call(
        body, name=name, in_specs=[HBM] * n + [ANY],
        out_specs=[HBM] * n + [SEM] * (2 * ng) + [pl.BlockSpec(memory_space=pltpu.VMEM)],
        out_shape=[pltpu.HBM(b.shape, b.dtype) for b in bufs] + sems + sems + [jax.ShapeDtypeStruct((8, LANES), f32)],
        input_output_aliases={i: i for i in range(n)},
        compiler_params=pltpu.CompilerParams(has_side_effects=DATAFLOW))(
            *[pltpu.with_memory_space_constraint(b, pltpu.HBM) for b in bufs], after)
    return outs[:n], outs[n:n + ng], outs[n + ng:n + 2 * ng], outs[-1]


def gather_wait(bufs, send_sems, recv_sems, after, *, name):
    n = len(bufs)

    def body(*refs):
        ins = refs[:n]
        send, recv = refs[n], refs[n + 1]
        c = lax.axis_index("c")
        k, peers, peer_k = _chip_peers()
        for a in range(n):
            for j in range(3):
                copy = pltpu.make_async_remote_copy(
                    src_ref=_half_slab(ins[a], k, c), dst_ref=_half_slab(ins[a], peer_k[j], c), send_sem=send.at[3 * a + j],
                    recv_sem=recv.at[3 * a + j], device_id=peers[j], device_id_type=MESH)
                copy.wait_send()
                copy.wait_recv()

    return pl.pallas_call(
        body, name=name, in_specs=[HBM] * n + [SEM, SEM, ANY], out_specs=[HBM] * n,
        out_shape=[pltpu.HBM(b.shape, b.dtype) for b in bufs],
        input_output_aliases={i: i for i in range(n)},
        compiler_params=pltpu.CompilerParams(has_side_effects=DATAFLOW))(*bufs, send_sems, recv_sems, after)


def forward_halves(bufs, *, name):
    n = len(bufs)

    def body(*refs):
        bufs_ = refs[:n]
        send, recv = refs[2 * n:]
        x, y, c = _place()
        _, _, peer_k = _chip_peers()

        def copy(a, j, h):
            return pltpu.make_async_remote_copy(
                src_ref=_half_slab(bufs_[a], peer_k[j], h), dst_ref=_half_slab(bufs_[a], peer_k[j], h),
                send_sem=send.at[3 * a + j], recv_sem=recv.at[3 * a + j], device_id=(x, y, 1 - c), device_id_type=MESH)

        for a in range(n):
            for j in range(3):
                copy(a, j, c).start()
        for a in range(n):
            for j in range(3):
                copy(a, j, 1 - c).wait_recv()
        for a in range(n):
            for j in range(3):
                copy(a, j, c).wait_send()

    return pl.pallas_call(
        body, name=name, in_specs=[ANY] * n, out_specs=[ANY] * n,
        out_shape=[jax.ShapeDtypeStruct(b.shape, b.dtype) for b in bufs],
        input_output_aliases={i: i for i in range(n)},
        scratch_shapes=[pltpu.SemaphoreType.DMA((3 * n,))] * 2)(*bufs)


def _forward_copies(bufs_, send, recv, h):
    x, y, c = _place()
    _, _, peer_k = _chip_peers()
    return [pltpu.make_async_remote_copy(
        src_ref=_half_slab(bufs_[a], peer_k[j], h), dst_ref=_half_slab(bufs_[a], peer_k[j], h),
        send_sem=send.at[3 * a + j], recv_sem=recv.at[3 * a + j], device_id=(x, y, 1 - c), device_id_type=MESH)
        for a in range(len(bufs_)) for j in range(3)]


def forward_start(bufs, after, *, name):
    n = len(bufs)

    def body(*refs):
        for cp in _forward_copies(refs[:n], refs[2 * n + 1], refs[2 * n + 2], lax.axis_index("c")):
            cp.start()
        refs[-1][...] = jnp.zeros_like(refs[-1])

    sems = [pltpu.SemaphoreType.DMA((3 * n,))] * 2
    outs = pl.pallas_call(
        body, name=name, in_specs=[HBM] * n + [ANY],
        out_specs=[HBM] * n + [SEM] * 2 + [pl.BlockSpec(memory_space=pltpu.VMEM)],
        out_shape=[pltpu.HBM(b.shape, b.dtype) for b in bufs] + sems + [jax.ShapeDtypeStruct((8, LANES), f32)],
        input_output_aliases={i: i for i in range(n)},
        compiler_params=pltpu.CompilerParams(has_side_effects=DATAFLOW))(*bufs, after)
    return (n, outs[:-1]), outs[-1]


def forward_wait(state, after, *, name):
    n, held = state

    def body(*refs):
        c = lax.axis_index("c")
        for mine, theirs in zip(_forward_copies(refs[:n], refs[n], refs[n + 1], c),
                                _forward_copies(refs[:n], refs[n], refs[n + 1], 1 - c)):
            mine.wait_send()
            theirs.wait_recv()

    return pl.pallas_call(
        body, name=name, in_specs=[HBM] * n + [SEM] * 2 + [ANY], out_specs=[HBM] * n,
        out_shape=[pltpu.HBM(b.shape, b.dtype) for b in held[:n]],
        input_output_aliases={i: i for i in range(n)},
        compiler_params=pltpu.CompilerParams(has_side_effects=DATAFLOW))(*held, after)


def _sibling_copies(srcs, lands, send, recv):
    x, y, c = _place()
    return [pltpu.make_async_remote_copy(src_ref=srcs[a], dst_ref=lands[a], send_sem=send.at[a], recv_sem=recv.at[a],
                                         device_id=(x, y, 1 - c), device_id_type=MESH) for a in range(len(srcs))]


def sibling_start(arrays, after, *, name):
    n = len(arrays)
    lands = [pltpu.with_memory_space_constraint(lax.empty(a.shape, a.dtype), pltpu.HBM) for a in arrays]

    def body(*refs):
        for cp in _sibling_copies(refs[:n], refs[n:2 * n], refs[4 * n + 1], refs[4 * n + 2]):
            cp.start()
        refs[-1][...] = jnp.zeros_like(refs[-1])

    bufs = list(arrays) + lands
    sems = [pltpu.SemaphoreType.DMA((n,))] * 2
    outs = pl.pallas_call(
        body, name=name, in_specs=[HBM] * (2 * n) + [ANY],
        out_specs=[HBM] * (2 * n) + [SEM] * 2 + [pl.BlockSpec(memory_space=pltpu.VMEM)],
        out_shape=[pltpu.HBM(b.shape, b.dtype) for b in bufs] + sems + [jax.ShapeDtypeStruct((8, LANES), f32)],
        input_output_aliases={i: i for i in range(2 * n)},
        compiler_params=pltpu.CompilerParams(has_side_effects=DATAFLOW))(
            *[pltpu.with_memory_space_constraint(b, pltpu.HBM) for b in bufs], after)
    return (n, outs[:-1]), outs[-1]


def sibling_wait(state, after, *, name):
    n, held = state

    def body(*refs):
        for cp in _sibling_copies(refs[:n], refs[n:2 * n], refs[2 * n], refs[2 * n + 1]):
            cp.wait_send()
            cp.wait_recv()

    outs = pl.pallas_call(
        body, name=name, in_specs=[HBM] * (2 * n) + [SEM] * 2 + [ANY], out_specs=[HBM] * (2 * n),
        out_shape=[pltpu.HBM(b.shape, b.dtype) for b in held[:2 * n]],
        input_output_aliases={i: i for i in range(2 * n)},
        compiler_params=pltpu.CompilerParams(has_side_effects=DATAFLOW))(*held, after)
    return outs[n:]


ALL_MASKS = [(mx, my, mc) for mx in (0, 1) for my in (0, 1) for mc in (0, 1)][1:]


def _scatter_copies(srcs, lands, ev, send, recv, esend, erecv):
    x, y, c = _place()
    me = 4 * x + 2 * y + c
    k, peers, peer_k = _chip_peers()
    out = []
    for a in range(len(srcs)):
        for j in range(3):
            out.append(pltpu.make_async_remote_copy(
                src_ref=srcs[a].at[peer_k[j]], dst_ref=lands[a].at[j], send_sem=send.at[3 * a + j],
                recv_sem=recv.at[3 * a + j], device_id=peers[j], device_id_type=MESH))
    start_ev, wait_ev = [], []
    if ev is not None:
        for j, (mx, my, mc) in enumerate(ALL_MASKS):
            peer = (x ^ mx, y ^ my, c ^ mc)
            start_ev.append(pltpu.make_async_remote_copy(
                src_ref=ev.at[me], dst_ref=ev.at[me], send_sem=esend.at[j], recv_sem=erecv.at[j],
                device_id=peer, device_id_type=MESH))
            wait_ev.append(pltpu.make_async_remote_copy(
                src_ref=ev.at[me], dst_ref=ev.at[me ^ (4 * mx + 2 * my + mc)], send_sem=esend.at[j],
                recv_sem=erecv.at[j], device_id=peer, device_id_type=MESH))
    return out, start_ev, wait_ev


def chip_scatter_start(arrays, everyone, after, *, name):
    n = len(arrays)
    ne = 0 if everyone is None else 1
    lands = [pltpu.with_memory_space_constraint(lax.empty((3,) + a.shape[1:], a.dtype), pltpu.HBM) for a in arrays]

    def body(*refs):
        srcs, lands_ = refs[:n], refs[n:2 * n]
        ev = refs[2 * n] if ne else None
        sems = refs[2 * n + ne + 1 + 2 * n + ne:-1]
        send, recv = sems[0], sems[1]
        esend, erecv = (sems[2], sems[3]) if ne else (None, None)
        copies, start_ev, _ = _scatter_copies(srcs, lands_, ev, send, recv, esend, erecv)
        for cp in start_ev + copies:
            cp.start()
        refs[-1][...] = jnp.zeros_like(refs[-1])

    sem_shapes = [pltpu.SemaphoreType.DMA((3 * n,))] * 2 + [pltpu.SemaphoreType.DMA((7,))] * (2 * ne)
    bufs = list(arrays) + lands + ([everyone] if ne else [])
    outs = pl.pallas_call(
        body, name=name, in_specs=[HBM] * len(bufs) + [ANY],
        out_specs=[HBM] * len(bufs) + [SEM] * len(sem_shapes) + [pl.BlockSpec(memory_space=pltpu.VMEM)],
        out_shape=[pltpu.HBM(b.shape, b.dtype) for b in bufs] + sem_shapes + [jax.ShapeDtypeStruct((8, LANES), f32)],
        input_output_aliases={i: i for i in range(len(bufs))},
        compiler_params=pltpu.CompilerParams(has_side_effects=DATAFLOW))(
            *[pltpu.with_memory_space_constraint(b, pltpu.HBM) for b in bufs], after)
    return (n, ne, outs[:-1]), outs[-1]


def chip_scatter_wait(state, after, *, name):
    n, ne, held = state
    nb = 2 * n + ne
    bufs, sems = held[:nb], held[nb:]

    def body(*refs):
        srcs, lands_ = refs[:n], refs[n:2 * n]
        ev = refs[2 * n] if ne else None
        sems_ = refs[nb:nb + len(sems)]
        esend, erecv = (sems_[2], sems_[3]) if ne else (None, None)
        copies, _, wait_ev = _scatter_copies(srcs, lands_, ev, sems_[0], sems_[1], esend, erecv)
        for cp in wait_ev + copies:
            cp.wait_send()
            cp.wait_recv()

    outs = pl.pallas_call(
        body, name=name, in_specs=[HBM] * nb + [SEM] * len(sems) + [ANY], out_specs=[HBM] * nb,
        out_shape=[pltpu.HBM(b.shape, b.dtype) for b in bufs],
        input_output_aliases={i: i for i in range(nb)},
        compiler_params=pltpu.CompilerParams(has_side_effects=DATAFLOW))(*bufs, *sems, after)
    return outs[n:2 * n], (outs[2 * n] if ne else None)


def sibling_merge(bufs, *, name):
    n = len(bufs)

    def body(*refs):
        bufs_ = refs[:n]
        send, recv = refs[2 * n:]
        x, y, c = _place()

        def copy(u, h):
            return pltpu.make_async_remote_copy(
                src_ref=bufs_[u].at[h], dst_ref=bufs_[u].at[h], send_sem=send.at[u], recv_sem=recv.at[u],
                device_id=(x, y, 1 - c), device_id_type=MESH)

        for u in range(n):
            copy(u, c).start()
        for u in range(n):
            copy(u, 1 - c).wait_recv()
        for u in range(n):
            copy(u, c).wait_send()

    return pl.pallas_call(
        body, name=name, in_specs=[ANY] * n, out_specs=[ANY] * n,
        out_shape=[jax.ShapeDtypeStruct(b.shape, b.dtype) for b in bufs],
        input_output_aliases={i: i for i in range(n)},
        scratch_shapes=[pltpu.SemaphoreType.DMA((n,)), pltpu.SemaphoreType.DMA((n,))])(*bufs)


def sum_leading(a, *, name):
    n, r, c = a.shape

    def body(a_ref, o_ref):
        acc = a_ref[0]
        for i in range(1, n):
            acc = acc + a_ref[i]
        o_ref[...] = acc

    rb = r // 2 if r % 16 == 0 else r
    return pl.pallas_call(
        body, name=name, grid=(r // rb,), in_specs=[pl.BlockSpec((n, rb, c), lambda i: (0, i, 0))],
        out_specs=pl.BlockSpec((rb, c), lambda i: (i, 0)), out_shape=jax.ShapeDtypeStruct((r, c), f32),
        compiler_params=_params("parallel"))(a)


def _half_rows(shape):
    return shape[1] // 2 // 2


def rs_add_sibling(grads, recvd, ck_arr, *, name):
    n = len(grads)

    def body(ck_ref, *refs):
        s = pl.program_id(1)
        for u in range(n):
            g_ref, r_ref = refs[u], refs[n + u]
            qb_ref, own_ref = refs[2 * n + u], refs[3 * n + u]
            q = g_ref[0] + r_ref[0].astype(f32)
            qb_ref[0] = q.astype(bf16)

            @pl.when(s == ck_ref[1])
            def _(own_ref=own_ref, q=q):
                own_ref[...] = q

    in_specs = [pl.BlockSpec((1, _half_rows(g.shape), g.shape[2]), lambda r, s, ck: (s, ck[0] * 2 + r, 0)) for g in grads]
    in_specs += [pl.BlockSpec((1, _half_rows(g.shape), g.shape[2]), lambda r, s, ck: (s, r, 0)) for g in grads]
    out_specs = [pl.BlockSpec((1, _half_rows(g.shape), g.shape[2]), lambda r, s, ck: (s, r, 0)) for g in grads]
    out_specs += [pl.BlockSpec((_half_rows(g.shape), g.shape[2]), lambda r, s, ck: (r, 0)) for g in grads]
    outs = pl.pallas_call(
        body, name=name,
        grid_spec=pltpu.PrefetchScalarGridSpec(num_scalar_prefetch=1, grid=(2, N_SHARDS),
                                               in_specs=in_specs, out_specs=out_specs),
        out_shape=[jax.ShapeDtypeStruct((N_SHARDS, g.shape[1] // 2, g.shape[2]), bf16) for g in grads]
        + [jax.ShapeDtypeStruct((g.shape[1] // 2, g.shape[2]), f32) for g in grads],
        compiler_params=_params("parallel", "arbitrary"))(ck_arr, *grads, *recvd)
    return outs[:n], outs[n:]


def rs_sum_chips(owns, recvd, ck_arr, *, name):
    n = len(owns)

    def body(ck_ref, *refs):
        for u in range(n):
            own_ref, r_ref, o_ref = refs[u], refs[n + u], refs[2 * n + u]
            o_ref[0] = ((own_ref[...] + r_ref[0].astype(f32)) + r_ref[1].astype(f32)) + r_ref[2].astype(f32)

    in_specs = [pl.BlockSpec((o.shape[0] // 2, o.shape[1]), lambda r, ck: (r, 0)) for o in owns]
    in_specs += [pl.BlockSpec((3, o.shape[0] // 2, o.shape[1]), lambda r, ck: (0, r, 0)) for o in owns]
    out_specs = [pl.BlockSpec((1, o.shape[0] // 2, o.shape[1]), lambda r, ck: (ck[0], r, 0)) for o in owns]
    return pl.pallas_call(
        body, name=name,
        grid_spec=pltpu.PrefetchScalarGridSpec(num_scalar_prefetch=1, grid=(2,), in_specs=in_specs, out_specs=out_specs),
        out_shape=[jax.ShapeDtypeStruct((2,) + o.shape, f32) for o in owns],
        compiler_params=_params("parallel"))(ck_arr, *owns, *recvd)


SMALL = ("a_norm", "a_v_norm", "a_w_s", "a_b_s", "f_norm", "f_conv_w", "f_conv_b", "kv_norm", "k_norm",
         "b_norm", "b_q_norm", "b_sinks")
BIG = ("a_w_in", "a_w_out", "f_w_in", "f_w_out", "w_kv", "b_w_q", "b_w_o")
PACK_COLS = 1024
PACK_ROWS = 8 * N_STEPS


def _pack(parts, rows=PACK_ROWS):
    flat = jnp.concatenate([p.reshape(-1).astype(f32) for p in parts])
    pad = (-flat.shape[0]) % (rows * PACK_COLS)
    return jnp.pad(flat, (0, pad)).reshape(-1, PACK_COLS)


def _unpack(packed, shapes):
    flat = packed.reshape(-1)
    out, off = [], 0
    for s in shapes:
        size = math.prod(s)
        out.append(flat[off:off + size].reshape(s))
        off += size
    return out


def _behind(value, token):
    return lax.optimization_barrier((value, token))[0]


def _ffn_fwd(x, g, h, r, w_in4, conv_w, conv_b, f, tag):
    wg, wu = conv_w[:, :f], conv_w[:, f:]
    bg, bu = conv_b[None, :f], conv_b[None, f:]
    pg, pu, gate, up, a = ffn_in_fused(h, w_in4, wg, wu, bg, bu, name=f"ffn{tag}_in")
    return a, (x, g, h, r, pg, pu, gate, up, a, wg, wu)


def _ffn_bwd(dy, saved, w_in4, w_out, c_arr, tag, exchange=False):
    x, g, h, r, pg, pu, gate, up, a, wg, wu = saved
    f = w_out.shape[0]
    d_w_out = mm_tn(a, [dy], c_arr, name=f"ffn{tag}_dwout", n_s=w_out.shape[1], shard_rows=f // N_SHARDS, tki=f // 2)
    dpg, dpu, sg, su = ffn_gate_bwd(dy, w_out, pg, pu, gate, up, wg, wu, name=f"ffn{tag}_dgate")
    d_w_in = mm_tn(h, [dpg, dpu], c_arr, name=f"ffn{tag}_dwin", n_s=w_in4.shape[2], shard_rows=h.shape[1])
    state = None
    if exchange:
        state, token = sibling_start([d_w_in[1], d_w_out[1]], d_w_in[0], name=f"rs_sibling_start_ffn{tag}")
        g = _behind(g, token)
    dx, dg = mm_nt_rms_bwd([dpg, dpu], w_in4, x, r, g, dy, name=f"ffn{tag}_dh")
    d_conv_w = jnp.concatenate([sg[0:3], su[0:3]], axis=1)
    d_conv_b = jnp.concatenate([sg[3], su[3]], axis=0)
    return dx, dg, d_w_in, d_conv_w, d_conv_b, d_w_out, state


def _rs_front(pairs, sibling_state, after, c_arr, tag):
    units = [full.reshape(N_SHARDS, -1, full.shape[-1]) for full, _ in pairs]
    from_sib = sibling_wait(sibling_state, after, name=f"rs_sibling_wait{tag}")
    return rs_add_sibling(units, from_sib, c_arr, name=f"rs_add{tag}")


def _rs_back(own, from_chips, c_arr, tag):
    halves = rs_sum_chips(list(own), list(from_chips), c_arr, name=f"rs_sum{tag}")
    return [m.reshape(-1, m.shape[2]) for m in sibling_merge(list(halves), name=f"rs_merge{tag}")]


def kernel(x, a_norm, a_w_in, a_v_norm, a_w_s, a_b_s, a_w_out, f_norm, f_w_in, f_conv_w, f_conv_b, f_w_out, kv_norm, w_kv, k_norm, b_norm, b_w_q, b_q_norm, b_sinks, b_w_o, loss_target, m_a_norm, m_a_w_in, m_a_v_norm, m_a_w_s, m_a_b_s, m_a_w_out, m_f_norm, m_f_w_in, m_f_conv_w, m_f_conv_b, m_f_w_out, m_kv_norm, m_w_kv, m_k_norm, m_b_norm, m_b_w_q, m_b_q_norm, m_b_sinks, m_b_w_o, v_a_norm, v_a_w_in, v_a_v_norm, v_a_w_s, v_a_b_s, v_a_w_out, v_f_norm, v_f_w_in, v_f_conv_w, v_f_conv_b, v_f_w_out, v_kv_norm, v_w_kv, v_k_norm, v_b_norm, v_b_w_q, v_b_q_norm, v_b_sinks, v_b_w_o):
    args = dict(locals())
    weights = {n: args[n] for n in SMALL + BIG}
    moms = {n: args["m_" + n] for n in SMALL + BIG}
    vars_ = {n: args["v_" + n] for n in SMALL + BIG}
    t, d = x.shape[1], x.shape[2]
    xi, yi, ci = _place()
    chip = 2 * xi + yi

    big_local = [a_w_in[0], a_w_out[0], f_w_in, f_w_out, w_kv, b_w_q[0], b_w_o[0]]
    c_arr = jnp.stack([ci, chip]).astype(jnp.int32)
    k_arr = jnp.stack([chip]).astype(jnp.int32)
    b_ain, b_aout, b_fin0, b_fin1, b_fout0, b_fout1, b_kv, b_q, b_o = cast_into_slot(big_local, k_arr, name="cast_weights")
    small_cols = _pack([a_norm, a_v_norm, f_conv_w], rows=8)
    b_small = lax.dynamic_update_slice(jnp.zeros((N_SHARDS,) + small_cols.shape, f32), small_cols[None], (chip, 0, 0))
    g_small, w_a_in, g_a_w_out = gather_shards([b_small, b_ain, b_aout], name="gather_first", split=[False, True, True])
    later, send_sems, recv_sems, token = gather_start([b_fin0, b_fout0, b_kv, b_q, b_o, b_fin1, b_fout1],
                                                      [[0], [1], [2, 3, 4], [5, 6]], g_small, name="gather_start")
    ns_cols = a_norm.shape[1]
    nf_cols = f_conv_w.shape[2]
    parts = [_unpack(g_small[k], [a_norm.shape, a_v_norm.shape, f_conv_w.shape]) for k in range(N_SHARDS)]
    a_norm_f = jnp.concatenate([p[0] for p in parts], axis=1) + token[0, 0]
    a_v_norm_f = jnp.concatenate([p[1] for p in parts], axis=1)
    conv_w_f = jnp.concatenate([p[2] for p in parts], axis=2)

    x0 = x[0]
    tril = jnp.tril(jnp.ones((CHUNK, CHUNK), dtype=bool))
    wc = jnp.where(tril[None], a_w_s[0], 0.0).astype(bf16)
    bt = a_b_s[0].T
    kg2 = jnp.tile(k_norm, 2)[None]
    qg2 = jnp.tile(b_q_norm[0], 2)[None]

    (h_a,), r_a = rms_fwd(x0, [a_norm_f], name="a_norm")
    zuv = mm_nn(h_a, w_a_in, name="a_in")
    y_a = sgu_gate_fwd(zuv, a_v_norm_f, wc, bt, name="a_gate")
    w_a_out = g_a_w_out.reshape(1, -1, d)
    f = f_w_out.shape[1] * N_SHARDS
    fwd0, tok0 = forward_start(gather_wait(later[0:1], send_sems[0], recv_sems[0], y_a, name="gather_wait_0"), y_a,
                               name="gather_forward_start_0")
    x1, (h_f0,), r_f0 = mm_residual(y_a, w_a_out[0], x0, name="a_out", gains=[_behind(f_norm[0:1], tok0)])
    (g_fin0,) = forward_wait(fwd0, x1, name="gather_forward_wait_0")
    w_f_in = [g_fin0, None]
    a0, ffn0 = _ffn_fwd(x1, f_norm[0:1], h_f0, r_f0, w_f_in[0], conv_w_f[0], f_conv_b[0], f, "0")
    (g_fout0,) = forward_halves(gather_wait(later[1:2], send_sems[1], recv_sems[1], a0, name="gather_wait_1"),
                                name="gather_forward_1")
    w_f_out = [g_fout0.reshape(-1, d), None]
    fwd1, tok1 = forward_start(gather_wait(later[2:5], send_sems[2], recv_sems[2], g_fout0, name="gather_wait_1b"), a0,
                               name="gather_forward_start_1b")
    x2, (h_k, h_q), r_b = mm_residual(a0, w_f_out[0], x1, name="ffn0_out", gains=[_behind(kv_norm[None], tok1), b_norm])
    g_w_kv, g_b_w_q, g_b_w_o = forward_wait(fwd1, x2, name="gather_forward_wait_1b")
    w_kv_f = g_w_kv.reshape(1, d, -1)
    w_q_f = g_b_w_q.reshape(1, d, -1)
    w_o_f = g_b_w_o.reshape(1, -1, d)
    kv, k2, v2 = kv_proj_post(h_k, w_kv_f[0], kg2, name="kv_proj")
    qp, qn = q_proj_norm(h_q, w_q_f[0], qg2, name="q_proj", scale=HEAD_DIM ** -0.5)
    fwd2, tok2 = forward_start(gather_wait(later[5:7], send_sems[3], recv_sems[3], qn, name="gather_wait_2"), qn,
                               name="gather_forward_start_2")
    o = attn_fwd(qn, k2, v2, _behind(b_sinks[0], tok2), name="attn")
    x3, (h_f1,), r_f1 = mm_residual(o, w_o_f[0], x2, name="o_proj", gains=[f_norm[1:2]])
    g_fin1, g_fout1 = forward_wait(fwd2, x3, name="gather_forward_wait_2")
    w_f_in[1] = g_fin1
    w_f_out[1] = g_fout1.reshape(-1, d)
    a1, ffn1 = _ffn_fwd(x3, f_norm[1:2], h_f1, r_f1, w_f_in[1], conv_w_f[1], f_conv_b[1], f, "1")
    dx4, sq = mm_residual(a1, w_f_out[1], x3, name="ffn1_out", target=loss_target[0])
    loss_part = (0.5 * jnp.sum(sq) / d).reshape(1)

    proj_rows = d // N_SHARDS
    dx3, d_fn1, d_fwin1, d_cw1, d_cb1, d_fwout1, _ = _ffn_bwd(dx4, ffn1, w_f_in[1], w_f_out[1], c_arr, "1")
    d_w_o = mm_tn(o, [dx3], c_arr, name="o_proj_dw", n_s=d, shard_rows=o.shape[1] // N_SHARDS)
    dqn, dk2, dv2, dsink = attn_bwd(qn, k2, v2, dx3, w_o_f[0], b_sinks[0], name="attn_bwd")
    dqp, dqg = q_norm_bwd(dqn, qp, qg2, name="q_norm_bwd", scale=HEAD_DIM ** -0.5)
    dkv, dkg = kv_post_bwd(dk2, dv2, kv, kg2, name="kv_post_bwd")
    d_w_q = mm_tn(h_q, [dqp], c_arr, name="q_proj_dw", n_s=w_q_f.shape[2], shard_rows=proj_rows)
    d_w_kv = mm_tn(h_k, [dkv], c_arr, name="kv_proj_dw", n_s=w_kv_f.shape[2], shard_rows=proj_rows)
    group1 = [d_fwin1, d_fwout1, d_w_kv, d_w_q, d_w_o]
    sib1, token_s1 = sibling_start([half for _, half in group1], d_w_kv[0], name="rs_sibling_start1")
    dh_k = mm_nt([dkv], w_kv_f, name="kv_proj_dx")
    dx2, d_bn, d_kvn = mm_nt_rms_bwd([dqp], w_q_f, x2, r_b, _behind(b_norm, token_s1), dx3, name="q_proj_dx",
                                     extra=(dh_k, kv_norm[None]))
    chip_bf1, own1 = _rs_front(group1, sib1, dx2, c_arr, "1")
    scatter1, token1 = chip_scatter_start(list(chip_bf1), None, dx2, name="rs_chips_start1")
    ffn0 = ffn0[:9] + (_behind(ffn0[9], token1),) + ffn0[10:]
    dx1, d_fn0, d_fwin0, d_cw0, d_cb0, d_fwout0, sib2 = _ffn_bwd(dx2, ffn0, w_f_in[0], w_f_out[0], c_arr, "0", exchange=True)
    chip_bf2, own2 = _rs_front([d_fwin0, d_fwout0], sib2, dx1, c_arr, "2")
    scatter2, token2 = chip_scatter_start(list(chip_bf2), None, dx1, name="rs_chips_start2")
    a_v_norm_f = _behind(a_v_norm_f, token2)
    dy_a = mm_nt([dx1], w_a_out, name="a_out_dx")
    d_w_aout = mm_tn(y_a, [dx1], c_arr, name="a_out_dw", n_s=d, shard_rows=y_a.shape[1] // N_SHARDS)
    dzu, dzv, d_avn, d_ws, d_bt = sgu_gate_bwd(zuv, dy_a, a_v_norm_f, wc, bt, name="a_gate_bwd")
    d_w_ain = mm_tn(h_a, [dzu, dzv], c_arr, name="a_in_dw", n_s=w_a_in.shape[2], shard_rows=d)
    sib3, token_s3 = sibling_start([d_w_ain[1], d_w_aout[1]], d_w_ain[0], name="rs_sibling_start3")
    dx0, d_an = mm_nt_rms_bwd([dzu, dzv], w_a_in, x0, r_a, _behind(a_norm_f, token_s3), dx1, name="a_in_dx")
    grad_x = dx0[None]

    chip_bf3, own3 = _rs_front([d_w_ain, d_w_aout], sib3, dx0, c_arr, "3")
    d_fn = jnp.concatenate([d_fn0, d_fn1], axis=0)
    d_cw = jnp.stack([d_cw0, d_cw1])
    d_cb = jnp.stack([d_cb0, d_cb1])
    d_kg = (dkg[0, :HEAD_DIM] + dkg[0, HEAD_DIM:])
    d_qg = (dqg[0, :HEAD_DIM] + dqg[0, HEAD_DIM:])[None]
    small_full = [d_an, d_avn, d_ws[None], d_bt.T[None], d_fn, d_cw, d_cb, d_kvn[0], d_kg, d_bn, d_qg,
                  dsink[:, :N_Q_HEADS], loss_part]
    packed = _pack(small_full)
    me = 4 * xi + 2 * yi + ci
    everyone = lax.dynamic_update_slice(lax.empty((N_DEV,) + packed.shape, f32), packed[None], (me, 0, 0))
    scatter3, token3 = chip_scatter_start(list(chip_bf3), everyone, own3[0], name="rs_chips_start3")
    from_chips1, _ = chip_scatter_wait(scatter1, token3, name="rs_chips_wait1")
    from_chips2, _ = chip_scatter_wait(scatter2, from_chips1[0], name="rs_chips_wait2")
    fin1, fout1, gkv, gq, go, fin0, fout0 = _rs_back(list(own1) + list(own2), list(from_chips1) + list(from_chips2),
                                                     c_arr, "12")
    late = ("f_w_in", "f_w_out", "w_kv", "b_w_q", "b_w_o")
    res_late = adamw([weights[n] for n in late], [[fin0, fin1], [fout0, fout1], [gkv], [gq], [go]],
                     [moms[n] for n in late], [vars_[n] for n in late], name="adamw_late")
    from_chips3, from_all = chip_scatter_wait(scatter3, res_late[1][2], name="rs_chips_wait3")
    ain, aout = _rs_back(own3, from_chips3, c_arr, "3")
    first = ("a_w_in", "a_w_out")
    res_first = adamw([weights[n] for n in first], [[ain], [aout]], [moms[n] for n in first],
                      [vars_[n] for n in first], name="adamw_first")
    big = {n: tuple(r[i] for r in res_late) for i, n in enumerate(late)}
    big.update({n: tuple(r[i] for r in res_first) for i, n in enumerate(first)})

    full_shapes = [g.shape for g in small_full]
    small_g = _unpack(sum_leading(from_all, name="small_sum"), full_shapes)
    loss = small_g.pop()[0]
    small_g[0] = lax.dynamic_slice_in_dim(small_g[0], chip * ns_cols, ns_cols, axis=1)
    small_g[1] = lax.dynamic_slice_in_dim(small_g[1], chip * ns_cols, ns_cols, axis=1)
    small_g[5] = lax.dynamic_slice_in_dim(small_g[5], chip * nf_cols, nf_cols, axis=2)
    small_shapes = [weights[n].shape for n in SMALL]
    small_g = [g.reshape(s) for g, s in zip(small_g, small_shapes)]
    flat2 = [(math.prod(s[:-1]), s[-1]) for s in small_shapes]
    small_d, small_m, small_v = adamw_small(
        *[[a.reshape(s2) for a, s2 in zip(group, flat2)]
          for group in ([weights[n] for n in SMALL], small_g, [moms[n] for n in SMALL], [vars_[n] for n in SMALL])],
        name="adamw_small")
    small_d, small_m, small_v = ([a.reshape(s) for a, s in zip(group, small_shapes)]
                                 for group in (small_d, small_m, small_v))

    out = {}
    for i, n in enumerate(SMALL):
        out[n] = (small_g[i], small_d[i], small_m[i], small_v[i])
    out.update(big)
    order = ["a_norm", "a_w_in", "a_v_norm", "a_w_s", "a_b_s", "a_w_out", "f_norm", "f_w_in", "f_conv_w", "f_conv_b",
             "f_w_out", "kv_norm", "w_kv", "k_norm", "b_norm", "b_w_q", "b_q_norm", "b_sinks", "b_w_o"]
    return (loss, grad_x, *[out[n][0] for n in order], *[out[n][1] for n in order],
            *[out[n][2] for n in order], *[out[n][3] for n in order])
```

```python
import functools
import math

import jax
import jax.numpy as jnp
from jax import lax
from jax.experimental import pallas as pl
from jax.experimental.pallas import tpu as pltpu

f32 = jnp.float32
bf16 = jnp.bfloat16
MESH = pl.DeviceIdType.MESH
ANY = pl.BlockSpec(memory_space=pl.ANY)

EPS = 1e-6
LANES = 128
CHUNK = 128
HEAD_DIM = 64
N_Q_HEADS = 16
N_KV_HEADS = 4
Q_PER_KV = N_Q_HEADS // N_KV_HEADS
N_SHARDS = 4
N_DEV = 8

ADAM_LR = 0.001
ADAM_B1 = 0.9
ADAM_B2 = 0.999
ADAM_EPS = 1e-08
ADAM_WD = 0.01
ADAM_STEP = 10
ADAM_C1 = 1.0 - ADAM_B1 ** ADAM_STEP
ADAM_C2 = 1.0 - ADAM_B2 ** ADAM_STEP

_INV_SQRT2 = 1.0 / math.sqrt(2.0)
_INV_SQRT2PI = 1.0 / math.sqrt(2.0 * math.pi)


def _params(*sem):
    return pltpu.CompilerParams(dimension_semantics=sem)


def _gelu(z):
    return 0.5 * z * (1.0 + lax.erf(z * _INV_SQRT2))


def _gelu_and_grad(z):
    cdf = 0.5 * (1.0 + lax.erf(z * _INV_SQRT2))
    return z * cdf, cdf + z * jnp.exp(-0.5 * z * z) * _INV_SQRT2PI


def _dot(a, b):
    return jnp.dot(a, b, preferred_element_type=f32)


def _dot_nt(a, b):
    return lax.dot_general(a, b, (((1,), (1,)), ((), ())), preferred_element_type=f32)


def _dot_tn(a, b):
    return lax.dot_general(a, b, (((0,), (0,)), ((), ())), preferred_element_type=f32)


def _dot_split(a, b):
    hi = a.astype(bf16)
    lo = (a - hi.astype(f32)).astype(bf16)
    return _dot(hi, b) + _dot(lo, b)


VMEM_TILE_BUDGET = 52 * 1024 * 1024
MAX_ROW_TILE = 2048


def _row_tile(m, fixed_bytes, row_bytes):
    tm = min(m, MAX_ROW_TILE)
    while tm > 256 and 2 * (fixed_bytes + tm * row_bytes) > VMEM_TILE_BUDGET:
        tm //= 2
    return tm


def _isz(a):
    return jnp.dtype(a.dtype).itemsize


def mm_nn(a, w3, *, name, s0=0, ns=None, add=None, out_dtype=f32):
    m, k = a.shape
    s_all, _, n_s = w3.shape
    ns = s_all if ns is None else ns
    tm = _row_tile(m, k * n_s * 2, k * _isz(a) + n_s * jnp.dtype(out_dtype).itemsize + (0 if add is None else n_s * 4))

    def body(*refs):
        if add is None:
            a_ref, w_ref, o_ref = refs
            acc = _dot(a_ref[...].astype(bf16), w_ref[0])
        else:
            a_ref, w_ref, add_ref, o_ref = refs
            acc = _dot(a_ref[...].astype(bf16), w_ref[0]) + add_ref[...]
        o_ref[...] = acc.astype(out_dtype)

    in_specs = [pl.BlockSpec((tm, k), lambda j, i: (i, 0)),
                pl.BlockSpec((1, k, n_s), lambda j, i: (s0 + j, 0, 0))]
    args = [a, w3]
    if add is not None:
        in_specs.append(pl.BlockSpec((tm, n_s), lambda j, i: (i, j)))
        args.append(add)
    return pl.pallas_call(
        body, name=name, grid=(ns, m // tm), in_specs=in_specs,
        out_specs=pl.BlockSpec((tm, n_s), lambda j, i: (i, j)),
        out_shape=jax.ShapeDtypeStruct((m, ns * n_s), out_dtype),
        compiler_params=_params("parallel", "parallel"))(*args)


def mm_nt(a_list, w3, *, name, tko=None, add=None, out_dtype=f32):
    s_all, k_out, n_s = w3.shape
    m = a_list[0].shape[0]
    na = len(a_list)
    spa = s_all // na
    tko = k_out if tko is None else tko
    tm = _row_tile(m, tko * n_s * 2, na * n_s * _isz(a_list[0]) + tko * 4 * (1 if add is None else 2))

    def body(*refs):
        a_refs = refs[:na]
        w_ref = refs[na]
        o_ref = refs[-1]
        s = pl.program_id(2)

        @pl.when(s == 0)
        def _():
            if add is None:
                o_ref[...] = jnp.zeros_like(o_ref)
            else:
                o_ref[...] = refs[na + 1][...]

        for idx in range(na):
            @pl.when(s // spa == idx)
            def _(idx=idx):
                o_ref[...] += _dot_nt(a_refs[idx][...].astype(bf16), w_ref[0])

    def a_map(idx):
        return lambda ko, i, s: (i, jnp.clip(s - idx * spa, 0, spa - 1))

    in_specs = [pl.BlockSpec((tm, n_s), a_map(idx)) for idx in range(na)]
    in_specs.append(pl.BlockSpec((1, tko, n_s), lambda ko, i, s: (s, ko, 0)))
    args = list(a_list) + [w3]
    if add is not None:
        in_specs.append(pl.BlockSpec((tm, tko), lambda ko, i, s: (i, ko)))
        args.append(add)
    return pl.pallas_call(
        body, name=name, grid=(k_out // tko, m // tm, s_all), in_specs=in_specs,
        out_specs=pl.BlockSpec((tm, tko), lambda ko, i, s: (i, ko)),
        out_shape=jax.ShapeDtypeStruct((m, k_out), out_dtype),
        compiler_params=_params("parallel", "parallel", "arbitrary"))(*args)


def mm_tn(a, b_list, c_arr, *, name, n_s, shard_rows, tki=None):
    m, k_in = a.shape
    na = len(b_list)
    s_all = sum(b.shape[1] for b in b_list) // n_s
    spa = s_all // na
    tki = k_in if tki is None else tki
    tm = _row_tile(m, tki * n_s * 4, tki * _isz(a) + na * n_s * _isz(b_list[0]))

    nsteps = m // tm
    per_blk = tki // shard_rows
    half = shard_rows // 2

    def body(c_ref, *refs):
        a_ref = refs[0]
        b_refs = refs[1:1 + na]
        o_ref, ob_ref = refs[-2], refs[-1]
        s = pl.program_id(0)
        r = pl.program_id(2)

        @pl.when(r == 0)
        def _():
            o_ref[...] = jnp.zeros_like(o_ref)

        for idx in range(na):
            @pl.when(s // spa == idx)
            def _(idx=idx):
                o_ref[0] += _dot_tn(a_ref[...].astype(bf16), b_refs[idx][...].astype(bf16))

        @pl.when(r == nsteps - 1)
        def _():
            for q in range(per_blk):
                start = pl.multiple_of(q * shard_rows + (1 - c_ref[0]) * half, 16)
                ob_ref[q] = o_ref[0, pl.ds(start, half), :].astype(bf16)

    def b_map(idx):
        def index(s, ki, r, c_ref):
            active = (s // spa) == idx
            return (jnp.where(active, r, 0), jnp.clip(s - idx * spa, 0, spa - 1))
        return index

    in_specs = [pl.BlockSpec((tm, tki), lambda s, ki, r, c_ref: (r, ki))]
    in_specs += [pl.BlockSpec((tm, n_s), b_map(idx)) for idx in range(na)]
    n_blk = k_in // tki
    return pl.pallas_call(
        body, name=name,
        grid_spec=pltpu.PrefetchScalarGridSpec(
            num_scalar_prefetch=1, grid=(s_all, n_blk, nsteps), in_specs=in_specs,
            out_specs=[pl.BlockSpec((1, tki, n_s), lambda s, ki, r, c_ref: (s, ki, 0)),
                       pl.BlockSpec((per_blk, half, n_s), lambda s, ki, r, c_ref: (s * n_blk + ki, 0, 0))]),
        out_shape=[jax.ShapeDtypeStruct((s_all, k_in, n_s), f32),
                   jax.ShapeDtypeStruct((s_all * k_in // shard_rows, half, n_s), bf16)],
        compiler_params=_params("parallel", "parallel", "arbitrary"))(c_arr, a, *b_list)


def mm_nt_rms_bwd(a_list, w3, x, r, g, dx_in, *, name, extra=None):
    s_all, d, n_s = w3.shape
    m = a_list[0].shape[0]
    na = len(a_list)
    spa = s_all // na
    ne = 0 if extra is None else 1
    tm = _row_tile(m, d * n_s * 2, na * n_s * _isz(a_list[0]) + d * 4 * (4 + ne))

    def body(*refs):
        a_refs, w_ref = refs[:na], refs[na]
        x_ref, r_ref, g_ref, dxin_ref = refs[na + 1:na + 5]
        dh2_ref, g2_ref = (refs[na + 5], refs[na + 6]) if ne else (None, None)
        outs = refs[na + 5 + 2 * ne:]
        dx_ref, dg_ref = outs[0], outs[1]
        dg2_ref = outs[2] if ne else None
        acc_ref = outs[-1]
        i, s = pl.program_id(0), pl.program_id(1)

        @pl.when(s == 0)
        def _():
            acc_ref[...] = jnp.zeros_like(acc_ref)

        for idx in range(na):
            @pl.when(s // spa == idx)
            def _(idx=idx):
                acc_ref[...] += _dot_nt(a_refs[idx][...].astype(bf16), w_ref[0])

        @pl.when(s == s_all - 1)
        def _():
            rv = r_ref[...]
            xh = x_ref[...] * rv
            total = dxin_ref[...]
            pairs = [(acc_ref[...], g_ref, dg_ref)] + ([(dh2_ref[...], g2_ref, dg2_ref)] if ne else [])
            for dh, gain_ref, dgain_ref in pairs:
                part = jnp.sum(dh * xh, axis=0, keepdims=True)

                @pl.when(i == 0)
                def _(dgain_ref=dgain_ref, part=part):
                    dgain_ref[...] = part

                @pl.when(i > 0)
                def _(dgain_ref=dgain_ref, part=part):
                    dgain_ref[...] += part

                tg = dh * gain_ref[...]
                total = total + rv * (tg - xh * jnp.mean(tg * xh, axis=1, keepdims=True))
            dx_ref[...] = total

    def a_map(idx):
        return lambda i, s: (i, jnp.clip(s - idx * spa, 0, spa - 1))

    row = pl.BlockSpec((tm, d), lambda i, s: (i, 0))
    vec = pl.BlockSpec((1, d), lambda i, s: (0, 0))
    in_specs = [pl.BlockSpec((tm, n_s), a_map(idx)) for idx in range(na)]
    in_specs += [pl.BlockSpec((1, d, n_s), lambda i, s: (s, 0, 0)), row, pl.BlockSpec((tm, 1), lambda i, s: (i, 0)), vec, row]
    args = list(a_list) + [w3, x, r, g, dx_in]
    if ne:
        in_specs += [row, vec]
        args += list(extra)
    outs = pl.pallas_call(
        body, name=name, grid=(m // tm, s_all), in_specs=in_specs, out_specs=[row] + [vec] * (1 + ne),
        out_shape=[jax.ShapeDtypeStruct((m, d), f32)] + [jax.ShapeDtypeStruct((1, d), f32)] * (1 + ne),
        scratch_shapes=[pltpu.VMEM((tm, d), f32)],
        compiler_params=_params("arbitrary", "arbitrary"))(*args)
    return outs


def mm_residual(a, w, x, *, name, gains=(), target=None):
    m, k = a.shape
    d = w.shape[1]
    ng = len(gains)
    tm = _row_tile(m, k * d * 2, k * _isz(a) + d * 4 * 3 + ng * d * 2)

    def body(*refs):
        a_ref, w_ref, x_ref = refs[:3]
        y = _dot(a_ref[...].astype(bf16), w_ref[...]) + x_ref[...]
        if target is None:
            g_refs = refs[3:3 + ng]
            y_ref = refs[3 + ng]
            h_refs = refs[4 + ng:4 + 2 * ng]
            r_ref = refs[-1]
            y_ref[...] = y
            r = lax.rsqrt(jnp.mean(y * y, axis=1, keepdims=True) + EPS)
            yh = y * r
            for g_ref, h_ref in zip(g_refs, h_refs):
                h_ref[...] = (yh * g_ref[...]).astype(bf16)
            r_ref[...] = r
        else:
            t_ref, dy_ref, s_ref = refs[3:]
            i = pl.program_id(0)
            e = y - t_ref[...]
            dy_ref[...] = e * (1.0 / d)
            part = jnp.sum(e * e, axis=0, keepdims=True)

            @pl.when(i == 0)
            def _():
                s_ref[...] = part

            @pl.when(i > 0)
            def _():
                s_ref[...] += part

    row = pl.BlockSpec((tm, d), lambda i: (i, 0))
    vec = pl.BlockSpec((1, d), lambda i: (0, 0))
    in_specs = [pl.BlockSpec((tm, k), lambda i: (i, 0)), pl.BlockSpec((k, d), lambda i: (0, 0)), row]
    if target is None:
        outs = pl.pallas_call(
            body, name=name, grid=(m // tm,), in_specs=in_specs + [vec] * ng,
            out_specs=[row] * (1 + ng) + [pl.BlockSpec((tm, 1), lambda i: (i, 0))],
            out_shape=[jax.ShapeDtypeStruct((m, d), f32)] + [jax.ShapeDtypeStruct((m, d), bf16)] * ng
            + [jax.ShapeDtypeStruct((m, 1), f32)],
            compiler_params=_params("parallel"))(a, w, x, *gains)
        return outs[0], outs[1:1 + ng], outs[-1]
    return pl.pallas_call(
        body, name=name, grid=(m // tm,), in_specs=in_specs + [row], out_specs=[row, vec],
        out_shape=[jax.ShapeDtypeStruct((m, d), f32), jax.ShapeDtypeStruct((1, d), f32)],
        compiler_params=_params("arbitrary"))(a, w, x, target)


def rms_fwd(x, gains, *, name, tr=512):
    t, d = x.shape
    tr = min(tr, t)
    ng = len(gains)

    def body(*refs):
        x_ref = refs[0]
        g_refs = refs[1:1 + ng]
        h_refs = refs[1 + ng:1 + 2 * ng]
        r_ref = refs[-1]
        xv = x_ref[...]
        r = lax.rsqrt(jnp.mean(xv * xv, axis=1, keepdims=True) + EPS)
        xh = xv * r
        for g_ref, h_ref in zip(g_refs, h_refs):
            h_ref[...] = (xh * g_ref[...]).astype(bf16)
        r_ref[...] = r

    row = pl.BlockSpec((tr, d), lambda i: (i, 0))
    vec = pl.BlockSpec((1, d), lambda i: (0, 0))
    outs = pl.pallas_call(
        body, name=name, grid=(t // tr,), in_specs=[row] + [vec] * ng,
        out_specs=[row] * ng + [pl.BlockSpec((tr, 1), lambda i: (i, 0))],
        out_shape=[jax.ShapeDtypeStruct((t, d), bf16)] * ng + [jax.ShapeDtypeStruct((t, 1), f32)],
        compiler_params=_params("parallel"))(x, *gains)
    return outs[:ng], outs[ng]


def sgu_gate_fwd(zuv, gv, wc, bt, *, name, tr=512):
    t, w = zuv.shape[0], zuv.shape[1] // 2
    tr = min(tr, t)
    groups = w // LANES

    def body(zu_ref, zv_ref, gv_ref, wc_ref, bt_ref, y_ref):
        vp = _gelu(zv_ref[...])
        rv = lax.rsqrt(jnp.mean(vp * vp, axis=1, keepdims=True) + EPS)
        vb = (vp * rv * gv_ref[...]).astype(bf16)
        for c in range(tr // CHUNK):
            rows = slice(c * CHUNK, (c + 1) * CHUNK)
            for g in range(groups):
                cols = slice(g * LANES, (g + 1) * LANES)
                sv = _dot(wc_ref[g], vb[rows, cols]) + bt_ref[:, g:g + 1]
                y_ref[rows, cols] = (_gelu(zu_ref[rows, cols]) * sv).astype(bf16)

    row = pl.BlockSpec((tr, w), lambda i: (i, 0))
    return pl.pallas_call(
        body, name=name, grid=(t // tr,),
        in_specs=[row, pl.BlockSpec((tr, w), lambda i: (i, 1)), pl.BlockSpec((1, w), lambda i: (0, 0)),
                  pl.BlockSpec((groups, CHUNK, CHUNK), lambda i: (0, 0, 0)),
                  pl.BlockSpec((CHUNK, groups), lambda i: (0, 0))],
        out_specs=row, out_shape=jax.ShapeDtypeStruct((t, w), bf16),
        compiler_params=_params("parallel"))(zuv, zuv, gv, wc, bt)


def sgu_gate_bwd(zuv, dy, gv, wc, bt, *, name, tr=512):
    t, w = zuv.shape[0], zuv.shape[1] // 2
    tr = min(tr, t)
    groups = w // LANES
    nsteps = t // tr

    def body(zu_ref, zv_ref, dy_ref, gv_ref, wc_ref, bt_ref,
             dzu_ref, dzv_ref, dgv_ref, dws_ref, dbt_ref, dv_ref, bacc_ref):
        i = pl.program_id(0)

        @pl.when(i == 0)
        def _():
            dgv_ref[...] = jnp.zeros_like(dgv_ref)
            dws_ref[...] = jnp.zeros_like(dws_ref)
            bacc_ref[...] = jnp.zeros_like(bacc_ref)

        vp, vp_grad = _gelu_and_grad(zv_ref[...])
        rv = lax.rsqrt(jnp.mean(vp * vp, axis=1, keepdims=True) + EPS)
        vhat = vp * rv
        vb = (vhat * gv_ref[...]).astype(bf16)
        for c in range(tr // CHUNK):
            rows = slice(c * CHUNK, (c + 1) * CHUNK)
            for g in range(groups):
                cols = slice(g * LANES, (g + 1) * LANES)
                vblk = vb[rows, cols]
                sv = _dot(wc_ref[g], vblk) + bt_ref[:, g:g + 1]
                zub = zu_ref[rows, cols]
                dyb = dy_ref[rows, cols]
                ub, ub_grad = _gelu_and_grad(zub)
                dzu_ref[rows, cols] = (dyb * sv * ub_grad).astype(bf16)
                dsv = dyb * ub
                bacc_ref[:, cols] += dsv
                dsvb = dsv.astype(bf16)
                dv_ref[rows, cols] = _dot_tn(wc_ref[g], dsvb)
                dws_ref[g] += _dot_nt(dsvb, vblk)
        dv = dv_ref[...]
        dgv_ref[...] += jnp.sum(dv * vhat, axis=0, keepdims=True)
        tg = dv * gv_ref[...]
        dvp = rv * (tg - vhat * jnp.mean(tg * vhat, axis=1, keepdims=True))
        dzv_ref[...] = (dvp * vp_grad).astype(bf16)

        @pl.when(i == nsteps - 1)
        def _():
            tt = lax.broadcasted_iota(jnp.int32, (CHUNK, CHUNK), 0)
            ss = lax.broadcasted_iota(jnp.int32, (CHUNK, CHUNK), 1)
            for g in range(groups):
                dws_ref[g] = jnp.where(ss <= tt, dws_ref[g], 0.0)
                dbt_ref[:, g:g + 1] = jnp.sum(bacc_ref[:, g * LANES:(g + 1) * LANES], axis=1, keepdims=True)

    row = pl.BlockSpec((tr, w), lambda i: (i, 0))
    full3 = pl.BlockSpec((groups, CHUNK, CHUNK), lambda i: (0, 0, 0))
    return pl.pallas_call(
        body, name=name, grid=(nsteps,),
        in_specs=[row, pl.BlockSpec((tr, w), lambda i: (i, 1)), row, pl.BlockSpec((1, w), lambda i: (0, 0)), full3,
                  pl.BlockSpec((CHUNK, groups), lambda i: (0, 0))],
        out_specs=[row, row, pl.BlockSpec((1, w), lambda i: (0, 0)), full3,
                   pl.BlockSpec((CHUNK, groups), lambda i: (0, 0))],
        out_shape=[jax.ShapeDtypeStruct((t, w), bf16), jax.ShapeDtypeStruct((t, w), bf16),
                   jax.ShapeDtypeStruct((1, w), f32), jax.ShapeDtypeStruct((groups, CHUNK, CHUNK), f32),
                   jax.ShapeDtypeStruct((CHUNK, groups), f32)],
        scratch_shapes=[pltpu.VMEM((tr, w), f32), pltpu.VMEM((CHUNK, w), f32)],
        compiler_params=_params("arbitrary"))(zuv, zuv, dy, gv, wc, bt)


HALO = 8


def _shift_down(v, halo, k, first):
    r = pltpu.roll(v, k, 0)
    hh = jnp.where(first, 0.0, pltpu.roll(halo, k, 0))
    rid = lax.broadcasted_iota(jnp.int32, (HALO, v.shape[1]), 0)
    head = jnp.where(rid < k, hh, r[0:HALO])
    if v.shape[0] == HALO:
        return head
    return jnp.concatenate([head, r[HALO:]], axis=0)


def _shift_up(v, halo, k, last):
    n = v.shape[0]
    r = pltpu.roll(v, n - k, 0)
    hh = jnp.where(last, 0.0, pltpu.roll(halo, HALO - k, 0))
    rid = lax.broadcasted_iota(jnp.int32, (HALO, v.shape[1]), 0)
    tail = jnp.where(rid >= HALO - k, hh, r[n - HALO:])
    return jnp.concatenate([r[:n - HALO], tail], axis=0)


def _conv(p, halo, w_ref, b_ref, first):
    return (w_ref[2:3, :] * p + w_ref[1:2, :] * _shift_down(p, halo, 1, first)
            + w_ref[0:1, :] * _shift_down(p, halo, 2, first) + b_ref[...])


BF16_ROWS = 16


def ffn_in_fused(h, w_in4, wg, wu, bg, bu, *, name):
    t, k = h.shape
    s_all, _, n_s = w_in4.shape
    half = s_all // 2
    tm = _row_tile(t, 2 * k * n_s * 2, k * 2 + 4 * n_s * 4 + n_s * 2)

    def body(h_ref, hh_ref, wg_ref, wu_ref, cg_ref, cu_ref, bg_ref, bu_ref, pg_ref, pu_ref, gate_ref, up_ref, a_ref):
        first = pl.program_id(1) == 0
        hv, hh = h_ref[...], hh_ref[...]
        outs = []
        for w_ref, c_ref, b_ref, p_ref, o_ref in ((wg_ref, cg_ref, bg_ref, pg_ref, gate_ref),
                                                  (wu_ref, cu_ref, bu_ref, pu_ref, up_ref)):
            p = _dot(hv, w_ref[0])
            p_ref[...] = p
            hu = _conv(p, _dot(hh, w_ref[0])[BF16_ROWS - HALO:], c_ref, b_ref, first)
            o_ref[...] = hu
            outs.append(hu)
        gate, up = outs
        a_ref[...] = (gate * jax.nn.sigmoid(gate) * up).astype(bf16)

    tile = pl.BlockSpec((tm, n_s), lambda j, i: (i, j))
    cw = pl.BlockSpec((3, n_s), lambda j, i: (0, j))
    cb = pl.BlockSpec((1, n_s), lambda j, i: (0, j))
    f = half * n_s
    return pl.pallas_call(
        body, name=name, grid=(half, t // tm),
        in_specs=[pl.BlockSpec((tm, k), lambda j, i: (i, 0)),
                  pl.BlockSpec((BF16_ROWS, k), lambda j, i: (jnp.maximum(i * (tm // BF16_ROWS) - 1, 0), 0)),
                  pl.BlockSpec((1, k, n_s), lambda j, i: (j, 0, 0)),
                  pl.BlockSpec((1, k, n_s), lambda j, i: (j + half, 0, 0)), cw, cw, cb, cb],
        out_specs=[tile] * 5,
        out_shape=[jax.ShapeDtypeStruct((t, f), f32)] * 4 + [jax.ShapeDtypeStruct((t, f), bf16)],
        compiler_params=_params("parallel", "parallel"))(h, h, w_in4, w_in4, wg, wu, bg, bu)


def _gate_grads(gate, up, dav):
    sg = jax.nn.sigmoid(gate)
    return dav * up * (sg * (1.0 + gate * (1.0 - sg))), dav * gate * sg


GATE_BWD_ROWS = 512


def ffn_gate_bwd(dy, w_out, pg, pu, gate, up, wg, wu, *, name):
    t, f = pg.shape
    d = dy.shape[1]
    tr = min(GATE_BWD_ROWS, t)
    nsteps = t // tr
    tc = f // 2

    def body(dy_ref, dyn_ref, w_ref, pg_ref, pu_ref, gate_ref, gaten_ref, up_ref, upn_ref, wg_ref, wu_ref,
             dg_ref, du_ref, sg_ref, su_ref):
        i = pl.program_id(1)
        last = i == nsteps - 1
        w = w_ref[0]
        da = _dot_nt(dy_ref[...].astype(bf16), w)
        da_n = _dot_nt(dyn_ref[...].astype(bf16), w)
        dgate, dup = _gate_grads(gate_ref[...], up_ref[...], da)
        dgate_n, dup_n = _gate_grads(gaten_ref[...], upn_ref[...], da_n)
        rid = lax.broadcasted_iota(jnp.int32, (8, tc), 0)
        for dd, d_n, c_ref, p_ref, o_ref, s_ref in ((dgate, dgate_n, wg_ref, pg_ref, dg_ref, sg_ref),
                                                    (dup, dup_n, wu_ref, pu_ref, du_ref, su_ref)):
            d1, d2 = _shift_up(dd, d_n, 1, last), _shift_up(dd, d_n, 2, last)
            o_ref[...] = (c_ref[2:3, :] * dd + c_ref[1:2, :] * d1 + c_ref[0:1, :] * d2).astype(bf16)
            p = p_ref[...]
            sums = [jnp.sum(d2 * p, axis=0, keepdims=True), jnp.sum(d1 * p, axis=0, keepdims=True),
                    jnp.sum(dd * p, axis=0, keepdims=True), jnp.sum(dd, axis=0, keepdims=True)]
            part = jnp.zeros((8, tc), f32)
            for k, sk in enumerate(sums):
                part = jnp.where(rid == k, sk, part)

            @pl.when(i == 0)
            def _(s_ref=s_ref, part=part):
                s_ref[...] = part

            @pl.when(i > 0)
            def _(s_ref=s_ref, part=part):
                s_ref[...] += part

    def nxt_rows(j, i):
        return (jnp.minimum((i + 1) * (tr // HALO), t // HALO - 1), j)

    tile = pl.BlockSpec((tr, tc), lambda j, i: (i, j))
    nxt = pl.BlockSpec((HALO, tc), nxt_rows)
    wspec = pl.BlockSpec((3, tc), lambda j, i: (0, j))
    stat = pl.BlockSpec((8, tc), lambda j, i: (0, j))
    return pl.pallas_call(
        body, name=name, grid=(2, nsteps),
        in_specs=[pl.BlockSpec((tr, d), lambda j, i: (i, 0)),
                  pl.BlockSpec((HALO, d), lambda j, i: (nxt_rows(j, i)[0], 0)),
                  pl.BlockSpec((1, tc, d), lambda j, i: (j, 0, 0)),
                  tile, tile, tile, nxt, tile, nxt, wspec, wspec],
        out_specs=[tile, tile, stat, stat],
        out_shape=[jax.ShapeDtypeStruct((t, f), bf16), jax.ShapeDtypeStruct((t, f), bf16),
                   jax.ShapeDtypeStruct((8, f), f32), jax.ShapeDtypeStruct((8, f), f32)],
        compiler_params=_params("parallel", "arbitrary"))(
            dy, dy, w_out.reshape(2, tc, d), pg, pu, gate, gate, up, up, wg, wu)


def _head_mean_matrix():
    i = lax.broadcasted_iota(jnp.int32, (LANES, LANES), 0) // HEAD_DIM
    j = lax.broadcasted_iota(jnp.int32, (LANES, LANES), 1) // HEAD_DIM
    return jnp.where(i == j, 1.0 / HEAD_DIM, 0.0).astype(bf16)


def _lane_half(shape):
    return (lax.broadcasted_iota(jnp.int32, shape, 1) % LANES) // HEAD_DIM


def q_proj_norm(h, w, g2, *, name, scale):
    t, k = h.shape
    n = w.shape[1]
    tm = _row_tile(t, k * n * 2, k * 2 + n * 4 + n * 2)

    def body(h_ref, w_ref, g_ref, qp_ref, o_ref):
        qp_ref[...] = _dot(h_ref[...], w_ref[...])
        bd = _head_mean_matrix()
        for cb in range(n // LANES):
            cols = slice(cb * LANES, (cb + 1) * LANES)
            xc = qp_ref[:, cols]
            rh = lax.rsqrt(_dot_split(xc * xc, bd) + EPS)
            o_ref[:, cols] = (xc * rh * g_ref[...] * scale).astype(bf16)

    row = pl.BlockSpec((tm, n), lambda i: (i, 0))
    return pl.pallas_call(
        body, name=name, grid=(t // tm,),
        in_specs=[pl.BlockSpec((tm, k), lambda i: (i, 0)), pl.BlockSpec((k, n), lambda i: (0, 0)),
                  pl.BlockSpec((1, LANES), lambda i: (0, 0))],
        out_specs=[row, row], out_shape=[jax.ShapeDtypeStruct((t, n), f32), jax.ShapeDtypeStruct((t, n), bf16)],
        compiler_params=_params("parallel"))(h, w, g2)


def kv_proj_post(h, w, g2, *, name):
    t, k = h.shape
    n = w.shape[1]
    kw = n // 2
    tm = _row_tile(t, k * n * 2, k * 2 + n * 4 + 2 * n * 2)

    def body(h_ref, w_ref, g_ref, kv_ref, k_ref, v_ref):
        kv_ref[...] = _dot(h_ref[...], w_ref[...])
        bd = _head_mean_matrix()
        half = _lane_half((tm, LANES))
        for cb in range(kw // LANES):
            xc = kv_ref[:, cb * LANES:(cb + 1) * LANES]
            rh = lax.rsqrt(_dot_split(xc * xc, bd) + EPS)
            kn = xc * rh * g_ref[...]
            vc = kv_ref[:, kw + cb * LANES:kw + (cb + 1) * LANES]
            for src, dst in ((kn, k_ref), (vc, v_ref)):
                sw = pltpu.roll(src, HEAD_DIM, 1)
                for hf in range(2):
                    blk = 2 * cb + hf
                    dst[:, blk * LANES:(blk + 1) * LANES] = jnp.where(half == hf, src, sw).astype(bf16)

    row = pl.BlockSpec((tm, n), lambda i: (i, 0))
    return pl.pallas_call(
        body, name=name, grid=(t // tm,),
        in_specs=[pl.BlockSpec((tm, k), lambda i: (i, 0)), pl.BlockSpec((k, n), lambda i: (0, 0)),
                  pl.BlockSpec((1, LANES), lambda i: (0, 0))],
        out_specs=[row, row, row],
        out_shape=[jax.ShapeDtypeStruct((t, n), f32), jax.ShapeDtypeStruct((t, n), bf16), jax.ShapeDtypeStruct((t, n), bf16)],
        compiler_params=_params("parallel"))(h, w, g2)


def q_norm_bwd(dq, qp, g2, *, name, scale, tr=512):
    t, w = qp.shape
    tr = min(tr, t)

    def body(dq_ref, x_ref, g_ref, o_ref, dg_ref):
        i = pl.program_id(0)
        bd = _head_mean_matrix()
        acc = jnp.zeros((1, LANES), f32)
        for cb in range(w // LANES):
            cols = slice(cb * LANES, (cb + 1) * LANES)
            xc = x_ref[:, cols]
            rh = lax.rsqrt(_dot_split(xc * xc, bd) + EPS)
            xh = xc * rh
            dy = dq_ref[:, cols] * scale
            acc = acc + jnp.sum(dy * xh, axis=0, keepdims=True)
            tg = dy * g_ref[...]
            o_ref[:, cols] = (rh * (tg - xh * _dot_split(tg * xh, bd))).astype(bf16)

        @pl.when(i == 0)
        def _():
            dg_ref[...] = acc

        @pl.when(i > 0)
        def _():
            dg_ref[...] += acc

    row = pl.BlockSpec((tr, w), lambda i: (i, 0))
    vec = pl.BlockSpec((1, LANES), lambda i: (0, 0))
    return pl.pallas_call(
        body, name=name, grid=(t // tr,), in_specs=[row, row, vec], out_specs=[row, vec],
        out_shape=[jax.ShapeDtypeStruct((t, w), bf16), jax.ShapeDtypeStruct((1, LANES), f32)],
        compiler_params=_params("arbitrary"))(dq, qp, g2)


def kv_post_bwd(dk2, dv2, kv, g2, *, name, tr=512):
    t, w = kv.shape
    tr = min(tr, t)
    kw = w // 2

    def body(dk_ref, dv_ref, x_ref, g_ref, o_ref, dg_ref):
        i = pl.program_id(0)
        bd = _head_mean_matrix()
        half = _lane_half((tr, LANES))
        acc = jnp.zeros((1, LANES), f32)

        def fold(ref, cb):
            a = ref[:, (2 * cb) * LANES:(2 * cb + 1) * LANES]
            b = ref[:, (2 * cb + 1) * LANES:(2 * cb + 2) * LANES]
            return jnp.where(half == 0, a + pltpu.roll(a, HEAD_DIM, 1), b + pltpu.roll(b, HEAD_DIM, 1))

        for cb in range(kw // LANES):
            cols = slice(cb * LANES, (cb + 1) * LANES)
            xc = x_ref[:, cols]
            rh = lax.rsqrt(_dot_split(xc * xc, bd) + EPS)
            xh = xc * rh
            dy = fold(dk_ref, cb)
            acc = acc + jnp.sum(dy * xh, axis=0, keepdims=True)
            tg = dy * g_ref[...]
            o_ref[:, cols] = (rh * (tg - xh * _dot_split(tg * xh, bd))).astype(bf16)
            o_ref[:, kw + cb * LANES:kw + (cb + 1) * LANES] = fold(dv_ref, cb).astype(bf16)

        @pl.when(i == 0)
        def _():
            dg_ref[...] = acc

        @pl.when(i > 0)
        def _():
            dg_ref[...] += acc

    dup = pl.BlockSpec((tr, 2 * kw), lambda i: (i, 0))
    row = pl.BlockSpec((tr, w), lambda i: (i, 0))
    vec = pl.BlockSpec((1, LANES), lambda i: (0, 0))
    return pl.pallas_call(
        body, name=name, grid=(t // tr,), in_specs=[dup, dup, row, vec], out_specs=[row, vec],
        out_shape=[jax.ShapeDtypeStruct((t, w), bf16), jax.ShapeDtypeStruct((1, LANES), f32)],
        compiler_params=_params("arbitrary"))(dk2, dv2, kv, g2)


def _slope(h):
    return 2.0 ** (-8.0 * (h + 1) / N_Q_HEADS)


GROUP_ROWS = Q_PER_KV * CHUNK


def _band_mask(n):
    tq = lax.broadcasted_iota(jnp.int32, (GROUP_ROWS, 2 * CHUNK), 0) % CHUNK
    jk = lax.broadcasted_iota(jnp.int32, (GROUP_ROWS, 2 * CHUNK), 1)
    dist = tq + CHUNK - jk
    ok = (dist >= 0) & (dist < CHUNK) & jnp.logical_not((n == 0) & (jk < CHUNK))
    return dist.astype(f32), ok


def _band(ref, n, kh):
    p0 = pl.multiple_of(jnp.maximum(n - 1, 0) * CHUNK, CHUNK)
    c0 = pl.multiple_of(n * CHUNK, CHUNK)
    cols = slice(kh * LANES, (kh + 1) * LANES)
    return jnp.concatenate([ref[pl.ds(p0, CHUNK), cols], ref[pl.ds(c0, CHUNK), cols]], axis=0)


def _stack_heads(ref, kh, half):
    parts = []
    for cb in (2 * kh, 2 * kh + 1):
        xc = ref[:, cb * LANES:(cb + 1) * LANES].astype(f32)
        parts += [jnp.where(half == hf, xc, 0.0).astype(bf16) for hf in range(2)]
    return jnp.concatenate(parts, axis=0)


def _unstack_heads(x4, half):
    return (jnp.where(half == 0, x4[0:CHUNK], x4[CHUNK:2 * CHUNK]),
            jnp.where(half == 0, x4[2 * CHUNK:3 * CHUNK], x4[3 * CHUNK:]))


def _per_head_column(kh, values):
    grp = lax.broadcasted_iota(jnp.int32, (GROUP_ROWS, 1), 0) // CHUNK
    col = jnp.full((GROUP_ROWS, 1), values[0], f32)
    for g in range(1, Q_PER_KV):
        col = jnp.where(grp == g, values[g], col)
    return col


def _softmax_band(q4, kband, dist, ok, slope, sink):
    s = _dot_nt(q4, kband)
    s = jnp.where(ok, s - slope * dist, -jnp.inf)
    m = jnp.maximum(jnp.max(s, axis=1, keepdims=True), sink)
    e = jnp.exp(s - m)
    es = jnp.exp(sink - m)
    inv = 1.0 / (jnp.sum(e, axis=1, keepdims=True) + es)
    return e * inv, es * inv


def attn_fwd(q, k2, v2, sinks, *, name):
    t, w = q.shape
    nb = t // CHUNK

    def body(sink_ref, q_ref, k_ref, v_ref, o_ref):
        n = pl.program_id(0)
        dist, ok = _band_mask(n)
        half = _lane_half((CHUNK, LANES))
        khs = range(N_KV_HEADS)
        heads = [[Q_PER_KV * kh + g for g in range(Q_PER_KV)] for kh in khs]
        q4 = [_stack_heads(q_ref, kh, half) for kh in khs]
        soft = [_softmax_band(q4[kh], _band(k_ref, n, kh), dist, ok, _per_head_column(kh, [_slope(h) for h in heads[kh]]),
                              _per_head_column(kh, [sink_ref[h] for h in heads[kh]])) for kh in khs]
        o4 = [_dot(soft[kh][0].astype(bf16), _band(v_ref, n, kh)) for kh in khs]
        for kh in khs:
            lo, hi = _unstack_heads(o4[kh], half)
            o_ref[:, (2 * kh) * LANES:(2 * kh + 1) * LANES] = lo.astype(bf16)
            o_ref[:, (2 * kh + 1) * LANES:(2 * kh + 2) * LANES] = hi.astype(bf16)

    full = pl.BlockSpec((t, k2.shape[1]), lambda n: (0, 0))
    return pl.pallas_call(
        body, name=name, grid=(nb,),
        in_specs=[pl.BlockSpec(memory_space=pltpu.SMEM), pl.BlockSpec((CHUNK, w), lambda n: (n, 0)), full, full],
        out_specs=pl.BlockSpec((CHUNK, w), lambda n: (n, 0)),
        out_shape=jax.ShapeDtypeStruct((t, w), bf16),
        compiler_params=_params("parallel"))(sinks, q, k2, v2)


def attn_bwd(q, k2, v2, do, sinks, *, name):
    t, w = q.shape
    nb = t // CHUNK
    kw = k2.shape[1]

    def body(sink_ref, q_ref, k_ref, v_ref, do_ref, dq_ref, dk_ref, dv_ref, ds_ref, kc_ref, vc_ref):
        n = pl.program_id(0)

        @pl.when(n == 0)
        def _():
            ds_ref[...] = jnp.zeros_like(ds_ref)
            kc_ref[...] = jnp.zeros_like(kc_ref)
            vc_ref[...] = jnp.zeros_like(vc_ref)
            dk_ref[...] = jnp.zeros_like(dk_ref)
            dv_ref[...] = jnp.zeros_like(dv_ref)

        @pl.when(n == nb)
        def _():
            dk_ref[...] = kc_ref[...]
            dv_ref[...] = vc_ref[...]

        @pl.when(n < nb)
        def _():
            dist, ok = _band_mask(n)
            half = _lane_half((CHUNK, LANES))
            lane = lax.broadcasted_iota(jnp.int32, (1, LANES), 1)
            sink_acc = jnp.zeros((1, LANES), f32)
            khs = range(N_KV_HEADS)
            heads = [[Q_PER_KV * kh + g for g in range(Q_PER_KV)] for kh in khs]
            q4 = [_stack_heads(q_ref, kh, half) for kh in khs]
            do4 = [_stack_heads(do_ref, kh, half) for kh in khs]
            kband = [_band(k_ref, n, kh) for kh in khs]
            vband = [_band(v_ref, n, kh) for kh in khs]
            soft = [_softmax_band(q4[kh], kband[kh], dist, ok, _per_head_column(kh, [_slope(h) for h in heads[kh]]),
                                  _per_head_column(kh, [sink_ref[h] for h in heads[kh]])) for kh in khs]
            dp = [_dot_nt(do4[kh], vband[kh]) for kh in khs]
            delta = [jnp.sum(soft[kh][0] * dp[kh], axis=1, keepdims=True) for kh in khs]
            dsb = [(soft[kh][0] * (dp[kh] - delta[kh])).astype(bf16) for kh in khs]
            dq4 = [_dot(dsb[kh], kband[kh]) for kh in khs]
            dkb = [_dot_tn(dsb[kh], q4[kh]) for kh in khs]
            dvb = [_dot_tn(soft[kh][0].astype(bf16), do4[kh]) for kh in khs]
            for kh in khs:
                sd = soft[kh][1] * delta[kh]
                for g, h in enumerate(heads[kh]):
                    part = jnp.sum(sd[g * CHUNK:(g + 1) * CHUNK], axis=0, keepdims=True)
                    sink_acc = sink_acc + jnp.where(lane == h, -part, 0.0)
                lo, hi = _unstack_heads(dq4[kh], half)
                dq_ref[:, (2 * kh) * LANES:(2 * kh + 1) * LANES] = lo
                dq_ref[:, (2 * kh + 1) * LANES:(2 * kh + 2) * LANES] = hi
                cols = slice(kh * LANES, (kh + 1) * LANES)
                dk_ref[:, cols] = kc_ref[:, cols] + dkb[kh][0:CHUNK]
                dv_ref[:, cols] = vc_ref[:, cols] + dvb[kh][0:CHUNK]
                kc_ref[:, cols] = dkb[kh][CHUNK:]
                vc_ref[:, cols] = dvb[kh][CHUNK:]
            ds_ref[...] += sink_acc

    full = pl.BlockSpec((t, kw), lambda n: (0, 0))
    qblk = pl.BlockSpec((CHUNK, w), lambda n: (jnp.minimum(n, nb - 1), 0))
    kblk = pl.BlockSpec((CHUNK, kw), lambda n: (jnp.maximum(n - 1, 0), 0))
    return pl.pallas_call(
        body, name=name, grid=(nb + 1,),
        in_specs=[pl.BlockSpec(memory_space=pltpu.SMEM), qblk, full, full, qblk],
        out_specs=[qblk, kblk, kblk, pl.BlockSpec((1, LANES), lambda n: (0, 0))],
        out_shape=[jax.ShapeDtypeStruct((t, w), f32), jax.ShapeDtypeStruct((t, kw), f32),
                   jax.ShapeDtypeStruct((t, kw), f32), jax.ShapeDtypeStruct((1, LANES), f32)],
        scratch_shapes=[pltpu.VMEM((CHUNK, kw), f32), pltpu.VMEM((CHUNK, kw), f32)],
        compiler_params=_params("arbitrary"))(sinks, q, k2, v2, do)


N_STEPS = 8


def _row_blocks(shape):
    if len(shape) == 2:
        r, c = shape
        return (r // N_STEPS, c), (lambda s: (s, 0))
    l, r, c = shape
    per = N_STEPS // l
    return (1, r // per, c), (lambda s: (s // per, s % per, 0))


CAST_STEPS = 4


def cast_into_slot(arrays, k_arr, *, name):
    in_specs, out_specs, out_shape, layers = [], [], [], []
    for a in arrays:
        r, c = a.shape[-2:]
        rb = r // CAST_STEPS
        if a.ndim == 2:
            in_specs.append(pl.BlockSpec((rb, c), lambda s, k: (s, 0)))
            layers.append(None)
        else:
            for l in range(a.shape[0]):
                in_specs.append(pl.BlockSpec((1, rb, c), lambda s, k, l=l: (l, s, 0)))
                layers.append(l)
        for _ in range(1 if a.ndim == 2 else a.shape[0]):
            out_specs.append(pl.BlockSpec((1, rb, c), lambda s, k: (k[0], s, 0)))
            out_shape.append(jax.ShapeDtypeStruct((N_SHARDS, r, c), bf16))
    n = len(in_specs)

    def body(k_ref, *refs):
        for i_ref, o_ref, l in zip(refs[:n], refs[n:], layers):
            o_ref[0] = (i_ref[...] if l is None else i_ref[0]).astype(bf16)

    args = []
    for a in arrays:
        args += [a] * (1 if a.ndim == 2 else a.shape[0])
    return pl.pallas_call(
        body, name=name,
        grid_spec=pltpu.PrefetchScalarGridSpec(num_scalar_prefetch=1, grid=(CAST_STEPS,),
                                               in_specs=in_specs, out_specs=out_specs),
        out_shape=out_shape, compiler_params=_params("parallel"))(k_arr, *args)


def adamw(ws, gs, ms, vs, *, name):
    n = len(ws)
    specs, g_specs, g_count = [], [], []
    for w, g_list in zip(ws, gs):
        blk, index = _row_blocks(w.shape)
        specs.append(pl.BlockSpec(blk, index))
        layers = len(g_list)
        per = N_STEPS // layers
        g_count.append(layers)
        for l in range(layers):
            g_specs.append(pl.BlockSpec(blk[-2:], lambda s, l=l, per=per: (jnp.where(s // per == l, s % per, 0), 0)))
    ng = len(g_specs)

    def body(*refs):
        s = pl.program_id(0)
        g_refs = refs[3 * n:3 * n + ng]
        outs = refs[3 * n + ng:]
        off = 0
        for i in range(n):
            w_ref, m_ref, v_ref = refs[i], refs[n + i], refs[2 * n + i]
            go_ref, d_ref, nm_ref, nv_ref = (outs[k * n + i] for k in range(4))
            layers = g_count[i]
            g = g_refs[off][...]
            for l in range(1, layers):
                g = jnp.where(s // (N_STEPS // layers) == l, g_refs[off + l][...], g)
            off += layers
            g = g.reshape(w_ref.shape)
            m = ADAM_B1 * m_ref[...] + (1.0 - ADAM_B1) * g
            v = ADAM_B2 * v_ref[...] + (1.0 - ADAM_B2) * (g * g)
            m_hat = m / ADAM_C1
            v_hat = v / ADAM_C2
            go_ref[...] = g
            d_ref[...] = -ADAM_LR * (m_hat / (jnp.sqrt(v_hat) + ADAM_EPS) + ADAM_WD * w_ref[...])
            nm_ref[...] = m
            nv_ref[...] = v

    outs = pl.pallas_call(
        body, name=name, grid=(N_STEPS,), in_specs=specs * 3 + g_specs, out_specs=specs * 4,
        out_shape=[jax.ShapeDtypeStruct(a.shape, f32) for a in ws] * 4,
        compiler_params=_params("parallel"))(*ws, *ms, *vs, *[g for g_list in gs for g in g_list])
    return [outs[k * n:(k + 1) * n] for k in range(4)]


def _adamw_update(w, g, m, v):
    m = ADAM_B1 * m + (1.0 - ADAM_B1) * g
    v = ADAM_B2 * v + (1.0 - ADAM_B2) * (g * g)
    m_hat = m / ADAM_C1
    v_hat = v / ADAM_C2
    return -ADAM_LR * (m_hat / (jnp.sqrt(v_hat) + ADAM_EPS) + ADAM_WD * w), m, v


def adamw_small(ws, gs, ms, vs, *, name):
    n = len(ws)

    def body(*refs):
        for i in range(n):
            w_ref, g_ref, m_ref, v_ref = (refs[k * n + i] for k in range(4))
            d_ref, nm_ref, nv_ref = (refs[(4 + k) * n + i] for k in range(3))
            d_ref[...], nm_ref[...], nv_ref[...] = _adamw_update(w_ref[...], g_ref[...], m_ref[...], v_ref[...])

    outs = pl.pallas_call(
        body, name=name, out_shape=[jax.ShapeDtypeStruct(a.shape, f32) for a in ws] * 3)(*ws, *gs, *ms, *vs)
    return outs[:n], outs[n:2 * n], outs[2 * n:]


def _place():
    return lax.axis_index("x"), lax.axis_index("y"), lax.axis_index("c")


def gather_shards(bufs, *, name, split):
    n = len(bufs)

    def body(*refs):
        bufs_ = refs[:n]
        isend, irecv, dsend, drecv = refs[2 * n:]
        x, y, c = _place()
        k = 2 * x + y
        peers = [(1 - x, y, c), (x, 1 - y, c), (1 - x, 1 - y, c)]
        peer_k = [2 * (1 - x) + y, 2 * x + (1 - y), 2 * (1 - x) + (1 - y)]

        def slab(a, q, h):
            if not split[a]:
                return bufs_[a].at[q]
            half = bufs_[a].shape[1] // 2
            return bufs_[a].at[q, pl.ds(pl.multiple_of(h * half, 16), half)]

        def ici(a, j, q):
            return pltpu.make_async_remote_copy(
                src_ref=slab(a, q, c), dst_ref=slab(a, q, c), send_sem=isend.at[3 * a + j], recv_sem=irecv.at[3 * a + j],
                device_id=peers[j], device_id_type=MESH)

        def d2d(a, j, h):
            return pltpu.make_async_remote_copy(
                src_ref=slab(a, peer_k[j], h), dst_ref=slab(a, peer_k[j], h), send_sem=dsend.at[3 * a + j],
                recv_sem=drecv.at[3 * a + j], device_id=(x, y, 1 - c), device_id_type=MESH)

        for a in range(n):
            for j in range(3):
                ici(a, j, k).start()
        for a in range(n):
            for j in range(3):
                ici(a, j, peer_k[j]).wait_recv()
                if split[a]:
                    d2d(a, j, c).start()
        for a in range(n):
            for j in range(3):
                if split[a]:
                    d2d(a, j, 1 - c).wait_recv()
        for a in range(n):
            for j in range(3):
                ici(a, j, k).wait_send()
                if split[a]:
                    d2d(a, j, c).wait_send()

    return pl.pallas_call(
        body, name=name, in_specs=[ANY] * n, out_specs=[ANY] * n,
        out_shape=[jax.ShapeDtypeStruct(b.shape, b.dtype) for b in bufs],
        input_output_aliases={i: i for i in range(n)},
        scratch_shapes=[pltpu.SemaphoreType.DMA((3 * n,))] * 4)(*bufs)


HBM = pl.BlockSpec(memory_space=pltpu.HBM)
SEM = pl.BlockSpec(memory_space=pltpu.SEMAPHORE)
DATAFLOW = pltpu.SideEffectType.DATAFLOW_SIDE_EFFECTING


def _chip_peers():
    x, y, c = _place()
    return 2 * x + y, [(1 - x, y, c), (x, 1 - y, c), (1 - x, 1 - y, c)], [2 * (1 - x) + y, 2 * x + (1 - y), 2 * (1 - x) + (1 - y)]


def _half_slab(ref, q, h):
    half = ref.shape[1] // 2
    return ref.at[q, pl.ds(pl.multiple_of(h * half, BF16_ROWS), half)]


def gather_start(bufs, groups, after, *, name):
    n = len(bufs)
    ng = len(groups)

    def body(*refs):
        ins = refs[:n]
        sends, recvs = refs[2 * n + 1:2 * n + 1 + ng], refs[2 * n + 1 + ng:2 * n + 1 + 2 * ng]
        token = refs[-1]
        c = lax.axis_index("c")
        k, peers, _ = _chip_peers()
        for gi, grp in enumerate(groups):
            for pos, a in enumerate(grp):
                for j in range(3):
                    pltpu.make_async_remote_copy(
                        src_ref=_half_slab(ins[a], k, c), dst_ref=_half_slab(ins[a], k, c), send_sem=sends[gi].at[3 * pos + j],
                        recv_sem=recvs[gi].at[3 * pos + j], device_id=peers[j], device_id_type=MESH).start()
        token[...] = jnp.zeros_like(token)

    sems = [pltpu.SemaphoreType.DMA((3 * len(grp),)) for grp in groups]
    outs = pl.pallas_call(
        body, name=name, in_specs=[HBM] * n + [ANY],
        out_specs=[HBM] * n + [SEM] * (2 * ng) + [pl.BlockSpec(memory_space=pltpu.VMEM)],
        out_shape=[pltpu.HBM(b.shape, b.dtype) for b in bufs] + sems + sems + [jax.ShapeDtypeStruct((8, LANES), f32)],
        input_output_aliases={i: i for i in range(n)},
        compiler_params=pltpu.CompilerParams(has_side_effects=DATAFLOW))(
            *[pltpu.with_memory_space_constraint(b, pltpu.HBM) for b in bufs], after)
    return outs[:n], outs[n:n + ng], outs[n + ng:n + 2 * ng], outs[-1]


def gather_wait(bufs, send_sems, recv_sems, after, *, name):
    n = len(bufs)

    def body(*refs):
        ins = refs[:n]
        send, recv = refs[n], refs[n + 1]
        c = lax.axis_index("c")
        k, peers, peer_k = _chip_peers()
        for a in range(n):
            for j in range(3):
                copy = pltpu.make_async_remote_copy(
                    src_ref=_half_slab(ins[a], k, c), dst_ref=_half_slab(ins[a], peer_k[j], c), send_sem=send.at[3 * a + j],
                    recv_sem=recv.at[3 * a + j], device_id=peers[j], device_id_type=MESH)
                copy.wait_send()
                copy.wait_recv()

    return pl.pallas_call(
        body, name=name, in_specs=[HBM] * n + [SEM, SEM, ANY], out_specs=[HBM] * n,
        out_shape=[pltpu.HBM(b.shape, b.dtype) for b in bufs],
        input_output_aliases={i: i for i in range(n)},
        compiler_params=pltpu.CompilerParams(has_side_effects=DATAFLOW))(*bufs, send_sems, recv_sems, after)


def forward_halves(bufs, *, name):
    n = len(bufs)

    def body(*refs):
        bufs_ = refs[:n]
        send, recv = refs[2 * n:]
        x, y, c = _place()
        _, _, peer_k = _chip_peers()

        def copy(a, j, h):
            return pltpu.make_async_remote_copy(
                src_ref=_half_slab(bufs_[a], peer_k[j], h), dst_ref=_half_slab(bufs_[a], peer_k[j], h),
                send_sem=send.at[3 * a + j], recv_sem=recv.at[3 * a + j], device_id=(x, y, 1 - c), device_id_type=MESH)

        for a in range(n):
            for j in range(3):
                copy(a, j, c).start()
        for a in range(n):
            for j in range(3):
                copy(a, j, 1 - c).wait_recv()
        for a in range(n):
            for j in range(3):
                copy(a, j, c).wait_send()

    return pl.pallas_call(
        body, name=name, in_specs=[ANY] * n, out_specs=[ANY] * n,
        out_shape=[jax.ShapeDtypeStruct(b.shape, b.dtype) for b in bufs],
        input_output_aliases={i: i for i in range(n)},
        scratch_shapes=[pltpu.SemaphoreType.DMA((3 * n,))] * 2)(*bufs)


def _forward_copies(bufs_, send, recv, h):
    x, y, c = _place()
    _, _, peer_k = _chip_peers()
    return [pltpu.make_async_remote_copy(
        src_ref=_half_slab(bufs_[a], peer_k[j], h), dst_ref=_half_slab(bufs_[a], peer_k[j], h),
        send_sem=send.at[3 * a + j], recv_sem=recv.at[3 * a + j], device_id=(x, y, 1 - c), device_id_type=MESH)
        for a in range(len(bufs_)) for j in range(3)]


def forward_start(bufs, after, *, name):
    n = len(bufs)

    def body(*refs):
        for cp in _forward_copies(refs[:n], refs[2 * n + 1], refs[2 * n + 2], lax.axis_index("c")):
            cp.start()
        refs[-1][...] = jnp.zeros_like(refs[-1])

    sems = [pltpu.SemaphoreType.DMA((3 * n,))] * 2
    outs = pl.pallas_call(
        body, name=name, in_specs=[HBM] * n + [ANY],
        out_specs=[HBM] * n + [SEM] * 2 + [pl.BlockSpec(memory_space=pltpu.VMEM)],
        out_shape=[pltpu.HBM(b.shape, b.dtype) for b in bufs] + sems + [jax.ShapeDtypeStruct((8, LANES), f32)],
        input_output_aliases={i: i for i in range(n)},
        compiler_params=pltpu.CompilerParams(has_side_effects=DATAFLOW))(*bufs, after)
    return (n, outs[:-1]), outs[-1]


def forward_wait(state, after, *, name):
    n, held = state

    def body(*refs):
        c = lax.axis_index("c")
        for mine, theirs in zip(_forward_copies(refs[:n], refs[n], refs[n + 1], c),
                                _forward_copies(refs[:n], refs[n], refs[n + 1], 1 - c)):
            mine.wait_send()
            theirs.wait_recv()

    return pl.pallas_call(
        body, name=name, in_specs=[HBM] * n + [SEM] * 2 + [ANY], out_specs=[HBM] * n,
        out_shape=[pltpu.HBM(b.shape, b.dtype) for b in held[:n]],
        input_output_aliases={i: i for i in range(n)},
        compiler_params=pltpu.CompilerParams(has_side_effects=DATAFLOW))(*held, after)


def _sibling_copies(srcs, lands, send, recv):
    x, y, c = _place()
    return [pltpu.make_async_remote_copy(src_ref=srcs[a], dst_ref=lands[a], send_sem=send.at[a], recv_sem=recv.at[a],
                                         device_id=(x, y, 1 - c), device_id_type=MESH) for a in range(len(srcs))]


def sibling_start(arrays, after, *, name):
    n = len(arrays)
    lands = [pltpu.with_memory_space_constraint(lax.empty(a.shape, a.dtype), pltpu.HBM) for a in arrays]

    def body(*refs):
        for cp in _sibling_copies(refs[:n], refs[n:2 * n], refs[4 * n + 1], refs[4 * n + 2]):
            cp.start()
        refs[-1][...] = jnp.zeros_like(refs[-1])

    bufs = list(arrays) + lands
    sems = [pltpu.SemaphoreType.DMA((n,))] * 2
    outs = pl.pallas_call(
        body, name=name, in_specs=[HBM] * (2 * n) + [ANY],
        out_specs=[HBM] * (2 * n) + [SEM] * 2 + [pl.BlockSpec(memory_space=pltpu.VMEM)],
        out_shape=[pltpu.HBM(b.shape, b.dtype) for b in bufs] + sems + [jax.ShapeDtypeStruct((8, LANES), f32)],
        input_output_aliases={i: i for i in range(2 * n)},
        compiler_params=pltpu.CompilerParams(has_side_effects=DATAFLOW))(
            *[pltpu.with_memory_space_constraint(b, pltpu.HBM) for b in bufs], after)
    return (n, outs[:-1]), outs[-1]


def sibling_wait(state, after, *, name):
    n, held = state

    def body(*refs):
        for cp in _sibling_copies(refs[:n], refs[n:2 * n], refs[2 * n], refs[2 * n + 1]):
            cp.wait_send()
            cp.wait_recv()

    outs = pl.pallas_call(
        body, name=name, in_specs=[HBM] * (2 * n) + [SEM] * 2 + [ANY], out_specs=[HBM] * (2 * n),
        out_shape=[pltpu.HBM(b.shape, b.dtype) for b in held[:2 * n]],
        input_output_aliases={i: i for i in range(2 * n)},
        compiler_params=pltpu.CompilerParams(has_side_effects=DATAFLOW))(*held, after)
    return outs[n:]


ALL_MASKS = [(mx, my, mc) for mx in (0, 1) for my in (0, 1) for mc in (0, 1)][1:]


def _scatter_copies(srcs, lands, ev, send, recv, esend, erecv):
    x, y, c = _place()
    me = 4 * x + 2 * y + c
    k, peers, peer_k = _chip_peers()
    out = []
    for a in range(len(srcs)):
        for j in range(3):
            out.append(pltpu.make_async_remote_copy(
                src_ref=srcs[a].at[peer_k[j]], dst_ref=lands[a].at[j], send_sem=send.at[3 * a + j],
                recv_sem=recv.at[3 * a + j], device_id=peers[j], device_id_type=MESH))
    start_ev, wait_ev = [], []
    if ev is not None:
        for j, (mx, my, mc) in enumerate(ALL_MASKS):
            peer = (x ^ mx, y ^ my, c ^ mc)
            start_ev.append(pltpu.make_async_remote_copy(
                src_ref=ev.at[me], dst_ref=ev.at[me], send_sem=esend.at[j], recv_sem=erecv.at[j],
                device_id=peer, device_id_type=MESH))
            wait_ev.append(pltpu.make_async_remote_copy(
                src_ref=ev.at[me], dst_ref=ev.at[me ^ (4 * mx + 2 * my + mc)], send_sem=esend.at[j],
                recv_sem=erecv.at[j], device_id=peer, device_id_type=MESH))
    return out, start_ev, wait_ev


def chip_scatter_start(arrays, everyone, after, *, name):
    n = len(arrays)
    ne = 0 if everyone is None else 1
    lands = [pltpu.with_memory_space_constraint(lax.empty((3,) + a.shape[1:], a.dtype), pltpu.HBM) for a in arrays]

    def body(*refs):
        srcs, lands_ = refs[:n], refs[n:2 * n]
        ev = refs[2 * n] if ne else None
        sems = refs[2 * n + ne + 1 + 2 * n + ne:-1]
        send, recv = sems[0], sems[1]
        esend, erecv = (sems[2], sems[3]) if ne else (None, None)
        copies, start_ev, _ = _scatter_copies(srcs, lands_, ev, send, recv, esend, erecv)
        for cp in start_ev + copies:
            cp.start()
        refs[-1][...] = jnp.zeros_like(refs[-1])

    sem_shapes = [pltpu.SemaphoreType.DMA((3 * n,))] * 2 + [pltpu.SemaphoreType.DMA((7,))] * (2 * ne)
    bufs = list(arrays) + lands + ([everyone] if ne else [])
    outs = pl.pallas_call(
        body, name=name, in_specs=[HBM] * len(bufs) + [ANY],
        out_specs=[HBM] * len(bufs) + [SEM] * len(sem_shapes) + [pl.BlockSpec(memory_space=pltpu.VMEM)],
        out_shape=[pltpu.HBM(b.shape, b.dtype) for b in bufs] + sem_shapes + [jax.ShapeDtypeStruct((8, LANES), f32)],
        input_output_aliases={i: i for i in range(len(bufs))},
        compiler_params=pltpu.CompilerParams(has_side_effects=DATAFLOW))(
            *[pltpu.with_memory_space_constraint(b, pltpu.HBM) for b in bufs], after)
    return (n, ne, outs[:-1]), outs[-1]


def chip_scatter_wait(state, after, *, name):
    n, ne, held = state
    nb = 2 * n + ne
    bufs, sems = held[:nb], held[nb:]

    def body(*refs):
        srcs, lands_ = refs[:n], refs[n:2 * n]
        ev = refs[2 * n] if ne else None
        sems_ = refs[nb:nb + len(sems)]
        esend, erecv = (sems_[2], sems_[3]) if ne else (None, None)
        copies, _, wait_ev = _scatter_copies(srcs, lands_, ev, sems_[0], sems_[1], esend, erecv)
        for cp in wait_ev + copies:
            cp.wait_send()
            cp.wait_recv()

    outs = pl.pallas_call(
        body, name=name, in_specs=[HBM] * nb + [SEM] * len(sems) + [ANY], out_specs=[HBM] * nb,
        out_shape=[pltpu.HBM(b.shape, b.dtype) for b in bufs],
        input_output_aliases={i: i for i in range(nb)},
        compiler_params=pltpu.CompilerParams(has_side_effects=DATAFLOW))(*bufs, *sems, after)
    return outs[n:2 * n], (outs[2 * n] if ne else None)


def sibling_merge(bufs, *, name):
    n = len(bufs)

    def body(*refs):
        bufs_ = refs[:n]
        send, recv = refs[2 * n:]
        x, y, c = _place()

        def copy(u, h):
            return pltpu.make_async_remote_copy(
                src_ref=bufs_[u].at[h], dst_ref=bufs_[u].at[h], send_sem=send.at[u], recv_sem=recv.at[u],
                device_id=(x, y, 1 - c), device_id_type=MESH)

        for u in range(n):
            copy(u, c).start()
        for u in range(n):
            copy(u, 1 - c).wait_recv()
        for u in range(n):
            copy(u, c).wait_send()

    return pl.pallas_call(
        body, name=name, in_specs=[ANY] * n, out_specs=[ANY] * n,
        out_shape=[jax.ShapeDtypeStruct(b.shape, b.dtype) for b in bufs],
        input_output_aliases={i: i for i in range(n)},
        scratch_shapes=[pltpu.SemaphoreType.DMA((n,)), pltpu.SemaphoreType.DMA((n,))])(*bufs)


def sum_leading(a, *, name):
    n, r, c = a.shape

    def body(a_ref, o_ref):
        acc = a_ref[0]
        for i in range(1, n):
            acc = acc + a_ref[i]
        o_ref[...] = acc

    rb = r // 2 if r % 16 == 0 else r
    return pl.pallas_call(
        body, name=name, grid=(r // rb,), in_specs=[pl.BlockSpec((n, rb, c), lambda i: (0, i, 0))],
        out_specs=pl.BlockSpec((rb, c), lambda i: (i, 0)), out_shape=jax.ShapeDtypeStruct((r, c), f32),
        compiler_params=_params("parallel"))(a)


def _half_rows(shape):
    return shape[1] // 2 // 2


def rs_add_sibling(grads, recvd, ck_arr, *, name):
    n = len(grads)

    def body(ck_ref, *refs):
        s = pl.program_id(1)
        for u in range(n):
            g_ref, r_ref = refs[u], refs[n + u]
            qb_ref, own_ref = refs[2 * n + u], refs[3 * n + u]
            q = g_ref[0] + r_ref[0].astype(f32)
            qb_ref[0] = q.astype(bf16)

            @pl.when(s == ck_ref[1])
            def _(own_ref=own_ref, q=q):
                own_ref[...] = q

    in_specs = [pl.BlockSpec((1, _half_rows(g.shape), g.shape[2]), lambda r, s, ck: (s, ck[0] * 2 + r, 0)) for g in grads]
    in_specs += [pl.BlockSpec((1, _half_rows(g.shape), g.shape[2]), lambda r, s, ck: (s, r, 0)) for g in grads]
    out_specs = [pl.BlockSpec((1, _half_rows(g.shape), g.shape[2]), lambda r, s, ck: (s, r, 0)) for g in grads]
    out_specs += [pl.BlockSpec((_half_rows(g.shape), g.shape[2]), lambda r, s, ck: (r, 0)) for g in grads]
    outs = pl.pallas_call(
        body, name=name,
        grid_spec=pltpu.PrefetchScalarGridSpec(num_scalar_prefetch=1, grid=(2, N_SHARDS),
                                               in_specs=in_specs, out_specs=out_specs),
        out_shape=[jax.ShapeDtypeStruct((N_SHARDS, g.shape[1] // 2, g.shape[2]), bf16) for g in grads]
        + [jax.ShapeDtypeStruct((g.shape[1] // 2, g.shape[2]), f32) for g in grads],
        compiler_params=_params("parallel", "arbitrary"))(ck_arr, *grads, *recvd)
    return outs[:n], outs[n:]


def rs_sum_chips(owns, recvd, ck_arr, *, name):
    n = len(owns)

    def body(ck_ref, *refs):
        for u in range(n):
            own_ref, r_ref, o_ref = refs[u], refs[n + u], refs[2 * n + u]
            o_ref[0] = ((own_ref[...] + r_ref[0].astype(f32)) + r_ref[1].astype(f32)) + r_ref[2].astype(f32)

    in_specs = [pl.BlockSpec((o.shape[0] // 2, o.shape[1]), lambda r, ck: (r, 0)) for o in owns]
    in_specs += [pl.BlockSpec((3, o.shape[0] // 2, o.shape[1]), lambda r, ck: (0, r, 0)) for o in owns]
    out_specs = [pl.BlockSpec((1, o.shape[0] // 2, o.shape[1]), lambda r, ck: (ck[0], r, 0)) for o in owns]
    return pl.pallas_call(
        body, name=name,
        grid_spec=pltpu.PrefetchScalarGridSpec(num_scalar_prefetch=1, grid=(2,), in_specs=in_specs, out_specs=out_specs),
        out_shape=[jax.ShapeDtypeStruct((2,) + o.shape, f32) for o in owns],
        compiler_params=_params("parallel"))(ck_arr, *owns, *recvd)


SMALL = ("a_norm", "a_v_norm", "a_w_s", "a_b_s", "f_norm", "f_conv_w", "f_conv_b", "kv_norm", "k_norm",
         "b_norm", "b_q_norm", "b_sinks")
BIG = ("a_w_in", "a_w_out", "f_w_in", "f_w_out", "w_kv", "b_w_q", "b_w_o")
PACK_COLS = 1024
PACK_ROWS = 8 * N_STEPS


def _pack(parts, rows=PACK_ROWS):
    flat = jnp.concatenate([p.reshape(-1).astype(f32) for p in parts])
    pad = (-flat.shape[0]) % (rows * PACK_COLS)
    return jnp.pad(flat, (0, pad)).reshape(-1, PACK_COLS)


def _unpack(packed, shapes):
    flat = packed.reshape(-1)
    out, off = [], 0
    for s in shapes:
        size = math.prod(s)
        out.append(flat[off:off + size].reshape(s))
        off += size
    return out


def _behind(value, token):
    return lax.optimization_barrier((value, token))[0]


def _ffn_fwd(x, g, h, r, w_in4, conv_w, conv_b, f, tag):
    wg, wu = conv_w[:, :f], conv_w[:, f:]
    bg, bu = conv_b[None, :f], conv_b[None, f:]
    pg, pu, gate, up, a = ffn_in_fused(h, w_in4, wg, wu, bg, bu, name=f"ffn{tag}_in")
    return a, (x, g, h, r, pg, pu, gate, up, a, wg, wu)


def _ffn_bwd(dy, saved, w_in4, w_out, c_arr, tag, exchange=False):
    x, g, h, r, pg, pu, gate, up, a, wg, wu = saved
    f = w_out.shape[0]
    d_w_out = mm_tn(a, [dy], c_arr, name=f"ffn{tag}_dwout", n_s=w_out.shape[1], shard_rows=f // N_SHARDS, tki=f // 2)
    dpg, dpu, sg, su = ffn_gate_bwd(dy, w_out, pg, pu, gate, up, wg, wu, name=f"ffn{tag}_dgate")
    d_w_in = mm_tn(h, [dpg, dpu], c_arr, name=f"ffn{tag}_dwin", n_s=w_in4.shape[2], shard_rows=h.shape[1])
    state = None
    if exchange:
        state, token = sibling_start([d_w_in[1], d_w_out[1]], d_w_in[0], name=f"rs_sibling_start_ffn{tag}")
        g = _behind(g, token)
    dx, dg = mm_nt_rms_bwd([dpg, dpu], w_in4, x, r, g, dy, name=f"ffn{tag}_dh")
    d_conv_w = jnp.concatenate([sg[0:3], su[0:3]], axis=1)
    d_conv_b = jnp.concatenate([sg[3], su[3]], axis=0)
    return dx, dg, d_w_in, d_conv_w, d_conv_b, d_w_out, state


def _rs_front(pairs, sibling_state, after, c_arr, tag):
    units = [full.reshape(N_SHARDS, -1, full.shape[-1]) for full, _ in pairs]
    from_sib = sibling_wait(sibling_state, after, name=f"rs_sibling_wait{tag}")
    return rs_add_sibling(units, from_sib, c_arr, name=f"rs_add{tag}")


def _rs_back(own, from_chips, c_arr, tag):
    halves = rs_sum_chips(list(own), list(from_chips), c_arr, name=f"rs_sum{tag}")
    return [m.reshape(-1, m.shape[2]) for m in sibling_merge(list(halves), name=f"rs_merge{tag}")]


def kernel(x, a_norm, a_w_in, a_v_norm, a_w_s, a_b_s, a_w_out, f_norm, f_w_in, f_conv_w, f_conv_b, f_w_out, kv_norm, w_kv, k_norm, b_norm, b_w_q, b_q_norm, b_sinks, b_w_o, loss_target, m_a_norm, m_a_w_in, m_a_v_norm, m_a_w_s, m_a_b_s, m_a_w_out, m_f_norm, m_f_w_in, m_f_conv_w, m_f_conv_b, m_f_w_out, m_kv_norm, m_w_kv, m_k_norm, m_b_norm, m_b_w_q, m_b_q_norm, m_b_sinks, m_b_w_o, v_a_norm, v_a_w_in, v_a_v_norm, v_a_w_s, v_a_b_s, v_a_w_out, v_f_norm, v_f_w_in, v_f_conv_w, v_f_conv_b, v_f_w_out, v_kv_norm, v_w_kv, v_k_norm, v_b_norm, v_b_w_q, v_b_q_norm, v_b_sinks, v_b_w_o):
    args = dict(locals())
    weights = {n: args[n] for n in SMALL + BIG}
    moms = {n: args["m_" + n] for n in SMALL + BIG}
    vars_ = {n: args["v_" + n] for n in SMALL + BIG}
    t, d = x.shape[1], x.shape[2]
    xi, yi, ci = _place()
    chip = 2 * xi + yi

    big_local = [a_w_in[0], a_w_out[0], f_w_in, f_w_out, w_kv, b_w_q[0], b_w_o[0]]
    c_arr = jnp.stack([ci, chip]).astype(jnp.int32)
    k_arr = jnp.stack([chip]).astype(jnp.int32)
    b_ain, b_aout, b_fin0, b_fin1, b_fout0, b_fout1, b_kv, b_q, b_o = cast_into_slot(big_local, k_arr, name="cast_weights")
    small_cols = _pack([a_norm, a_v_norm, f_conv_w], rows=8)
    b_small = lax.dynamic_update_slice(jnp.zeros((N_SHARDS,) + small_cols.shape, f32), small_cols[None], (chip, 0, 0))
    g_small, w_a_in, g_a_w_out = gather_shards([b_small, b_ain, b_aout], name="gather_first", split=[False, True, True])
    later, send_sems, recv_sems, token = gather_start([b_fin0, b_fout0, b_kv, b_q, b_o, b_fin1, b_fout1],
                                                      [[0], [1], [2, 3, 4], [5, 6]], g_small, name="gather_start")
    ns_cols = a_norm.shape[1]
    nf_cols = f_conv_w.shape[2]
    parts = [_unpack(g_small[k], [a_norm.shape, a_v_norm.shape, f_conv_w.shape]) for k in range(N_SHARDS)]
    a_norm_f = jnp.concatenate([p[0] for p in parts], axis=1) + token[0, 0]
    a_v_norm_f = jnp.concatenate([p[1] for p in parts], axis=1)
    conv_w_f = jnp.concatenate([p[2] for p in parts], axis=2)

    x0 = x[0]
    tril = jnp.tril(jnp.ones((CHUNK, CHUNK), dtype=bool))
    wc = jnp.where(tril[None], a_w_s[0], 0.0).astype(bf16)
    bt = a_b_s[0].T
    kg2 = jnp.tile(k_norm, 2)[None]
    qg2 = jnp.tile(b_q_norm[0], 2)[None]

    (h_a,), r_a = rms_fwd(x0, [a_norm_f], name="a_norm")
    zuv = mm_nn(h_a, w_a_in, name="a_in")
    y_a = sgu_gate_fwd(zuv, a_v_norm_f, wc, bt, name="a_gate")
    w_a_out = g_a_w_out.reshape(1, -1, d)
    f = f_w_out.shape[1] * N_SHARDS
    fwd0, tok0 = forward_start(gather_wait(later[0:1], send_sems[0], recv_sems[0], y_a, name="gather_wait_0"), y_a,
                               name="gather_forward_start_0")
    x1, (h_f0,), r_f0 = mm_residual(y_a, w_a_out[0], x0, name="a_out", gains=[_behind(f_norm[0:1], tok0)])
    (g_fin0,) = forward_wait(fwd0, x1, name="gather_forward_wait_0")
    w_f_in = [g_fin0, None]
    a0, ffn0 = _ffn_fwd(x1, f_norm[0:1], h_f0, r_f0, w_f_in[0], conv_w_f[0], f_conv_b[0], f, "0")
    (g_fout0,) = forward_halves(gather_wait(later[1:2], send_sems[1], recv_sems[1], a0, name="gather_wait_1"),
                                name="gather_forward_1")
    w_f_out = [g_fout0.reshape(-1, d), None]
    fwd1, tok1 = forward_start(gather_wait(later[2:5], send_sems[2], recv_sems[2], g_fout0, name="gather_wait_1b"), a0,
                               name="gather_forward_start_1b")
    x2, (h_k, h_q), r_b = mm_residual(a0, w_f_out[0], x1, name="ffn0_out", gains=[_behind(kv_norm[None], tok1), b_norm])
    g_w_kv, g_b_w_q, g_b_w_o = forward_wait(fwd1, x2, name="gather_forward_wait_1b")
    w_kv_f = g_w_kv.reshape(1, d, -1)
    w_q_f = g_b_w_q.reshape(1, d, -1)
    w_o_f = g_b_w_o.reshape(1, -1, d)
    kv, k2, v2 = kv_proj_post(h_k, w_kv_f[0], kg2, name="kv_proj")
    qp, qn = q_proj_norm(h_q, w_q_f[0], qg2, name="q_proj", scale=HEAD_DIM ** -0.5)
    fwd2, tok2 = forward_start(gather_wait(later[5:7], send_sems[3], recv_sems[3], qn, name="gather_wait_2"), qn,
                               name="gather_forward_start_2")
    o = attn_fwd(qn, k2, v2, _behind(b_sinks[0], tok2), name="attn")
    x3, (h_f1,), r_f1 = mm_residual(o, w_o_f[0], x2, name="o_proj", gains=[f_norm[1:2]])
    g_fin1, g_fout1 = forward_wait(fwd2, x3, name="gather_forward_wait_2")
    w_f_in[1] = g_fin1
    w_f_out[1] = g_fout1.reshape(-1, d)
    a1, ffn1 = _ffn_fwd(x3, f_norm[1:2], h_f1, r_f1, w_f_in[1], conv_w_f[1], f_conv_b[1], f, "1")
    dx4, sq = mm_residual(a1, w_f_out[1], x3, name="ffn1_out", target=loss_target[0])
    loss_part = (0.5 * jnp.sum(sq) / d).reshape(1)

    proj_rows = d // N_SHARDS
    dx3, d_fn1, d_fwin1, d_cw1, d_cb1, d_fwout1, _ = _ffn_bwd(dx4, ffn1, w_f_in[1], w_f_out[1], c_arr, "1")
    do = mm_nt([dx3], w_o_f, name="o_proj_dx")
    d_w_o = mm_tn(o, [dx3], c_arr, name="o_proj_dw", n_s=d, shard_rows=o.shape[1] // N_SHARDS)
    dqn, dk2, dv2, dsink = attn_bwd(qn, k2, v2, do, b_sinks[0], name="attn_bwd")
    dqp, dqg = q_norm_bwd(dqn, qp, qg2, name="q_norm_bwd", scale=HEAD_DIM ** -0.5)
    dkv, dkg = kv_post_bwd(dk2, dv2, kv, kg2, name="kv_post_bwd")
    d_w_q = mm_tn(h_q, [dqp], c_arr, name="q_proj_dw", n_s=w_q_f.shape[2], shard_rows=proj_rows)
    d_w_kv = mm_tn(h_k, [dkv], c_arr, name="kv_proj_dw", n_s=w_kv_f.shape[2], shard_rows=proj_rows)
    group1 = [d_fwin1, d_fwout1, d_w_kv, d_w_q, d_w_o]
    sib1, token_s1 = sibling_start([half for _, half in group1], d_w_kv[0], name="rs_sibling_start1")
    dh_k = mm_nt([dkv], w_kv_f, name="kv_proj_dx")
    dx2, d_bn, d_kvn = mm_nt_rms_bwd([dqp], w_q_f, x2, r_b, _behind(b_norm, token_s1), dx3, name="q_proj_dx",
                                     extra=(dh_k, kv_norm[None]))
    chip_bf1, own1 = _rs_front(group1, sib1, dx2, c_arr, "1")
    scatter1, token1 = chip_scatter_start(list(chip_bf1), None, dx2, name="rs_chips_start1")
    ffn0 = ffn0[:9] + (_behind(ffn0[9], token1),) + ffn0[10:]
    dx1, d_fn0, d_fwin0, d_cw0, d_cb0, d_fwout0, sib2 = _ffn_bwd(dx2, ffn0, w_f_in[0], w_f_out[0], c_arr, "0", exchange=True)
    chip_bf2, own2 = _rs_front([d_fwin0, d_fwout0], sib2, dx1, c_arr, "2")
    scatter2, token2 = chip_scatter_start(list(chip_bf2), None, dx1, name="rs_chips_start2")
    a_v_norm_f = _behind(a_v_norm_f, token2)
    dy_a = mm_nt([dx1], w_a_out, name="a_out_dx")
    d_w_aout = mm_tn(y_a, [dx1], c_arr, name="a_out_dw", n_s=d, shard_rows=y_a.shape[1] // N_SHARDS)
    dzu, dzv, d_avn, d_ws, d_bt = sgu_gate_bwd(zuv, dy_a, a_v_norm_f, wc, bt, name="a_gate_bwd")
    d_w_ain = mm_tn(h_a, [dzu, dzv], c_arr, name="a_in_dw", n_s=w_a_in.shape[2], shard_rows=d)
    sib3, token_s3 = sibling_start([d_w_ain[1], d_w_aout[1]], d_w_ain[0], name="rs_sibling_start3")
    dx0, d_an = mm_nt_rms_bwd([dzu, dzv], w_a_in, x0, r_a, _behind(a_norm_f, token_s3), dx1, name="a_in_dx")
    grad_x = dx0[None]

    chip_bf3, own3 = _rs_front([d_w_ain, d_w_aout], sib3, dx0, c_arr, "3")
    d_fn = jnp.concatenate([d_fn0, d_fn1], axis=0)
    d_cw = jnp.stack([d_cw0, d_cw1])
    d_cb = jnp.stack([d_cb0, d_cb1])
    d_kg = (dkg[0, :HEAD_DIM] + dkg[0, HEAD_DIM:])
    d_qg = (dqg[0, :HEAD_DIM] + dqg[0, HEAD_DIM:])[None]
    small_full = [d_an, d_avn, d_ws[None], d_bt.T[None], d_fn, d_cw, d_cb, d_kvn[0], d_kg, d_bn, d_qg,
                  dsink[:, :N_Q_HEADS], loss_part]
    packed = _pack(small_full)
    me = 4 * xi + 2 * yi + ci
    everyone = lax.dynamic_update_slice(lax.empty((N_DEV,) + packed.shape, f32), packed[None], (me, 0, 0))
    scatter3, token3 = chip_scatter_start(list(chip_bf3), everyone, own3[0], name="rs_chips_start3")
    from_chips1, _ = chip_scatter_wait(scatter1, token3, name="rs_chips_wait1")
    from_chips2, _ = chip_scatter_wait(scatter2, from_chips1[0], name="rs_chips_wait2")
    fin1, fout1, gkv, gq, go, fin0, fout0 = _rs_back(list(own1) + list(own2), list(from_chips1) + list(from_chips2),
                                                     c_arr, "12")
    late = ("f_w_in", "f_w_out", "w_kv", "b_w_q", "b_w_o")
    res_late = adamw([weights[n] for n in late], [[fin0, fin1], [fout0, fout1], [gkv], [gq], [go]],
                     [moms[n] for n in late], [vars_[n] for n in late], name="adamw_late")
    from_chips3, from_all = chip_scatter_wait(scatter3, res_late[1][2], name="rs_chips_wait3")
    ain, aout = _rs_back(own3, from_chips3, c_arr, "3")
    first = ("a_w_in", "a_w_out")
    res_first = adamw([weights[n] for n in first], [[ain], [aout]], [moms[n] for n in first],
                      [vars_[n] for n in first], name="adamw_first")
    big = {n: tuple(r[i] for r in res_late) for i, n in enumerate(late)}
    big.update({n: tuple(r[i] for r in res_first) for i, n in enumerate(first)})

    full_shapes = [g.shape for g in small_full]
    small_g = _unpack(sum_leading(from_all, name="small_sum"), full_shapes)
    loss = small_g.pop()[0]
    small_g[0] = lax.dynamic_slice_in_dim(small_g[0], chip * ns_cols, ns_cols, axis=1)
    small_g[1] = lax.dynamic_slice_in_dim(small_g[1], chip * ns_cols, ns_cols, axis=1)
    small_g[5] = lax.dynamic_slice_in_dim(small_g[5], chip * nf_cols, nf_cols, axis=2)
    small_shapes = [weights[n].shape for n in SMALL]
    small_g = [g.reshape(s) for g, s in zip(small_g, small_shapes)]
    flat2 = [(math.prod(s[:-1]), s[-1]) for s in small_shapes]
    small_d, small_m, small_v = adamw_small(
        *[[a.reshape(s2) for a, s2 in zip(group, flat2)]
          for group in ([weights[n] for n in SMALL], small_g, [moms[n] for n in SMALL], [vars_[n] for n in SMALL])],
        name="adamw_small")
    small_d, small_m, small_v = ([a.reshape(s) for a, s in zip(group, small_shapes)]
                                 for group in (small_d, small_m, small_v))

    out = {}
    for i, n in enumerate(SMALL):
        out[n] = (small_g[i], small_d[i], small_m[i], small_v[i])
    out.update(big)
    order = ["a_norm", "a_w_in", "a_v_norm", "a_w_s", "a_b_s", "a_w_out", "f_norm", "f_w_in", "f_conv_w", "f_conv_b",
             "f_w_out", "kv_norm", "w_kv", "k_norm", "b_norm", "b_w_q", "b_q_norm", "b_sinks", "b_w_o"]
    return (loss, grad_x, *[out[n][0] for n in order], *[out[n][1] for n in order],
            *[out[n][2] for n in order], *[out[n][3] for n in order])
```

```python
import functools
import math

import jax
import jax.numpy as jnp
from jax import lax
from jax.experimental import pallas as pl
from jax.experimental.pallas import tpu as pltpu

f32 = jnp.float32
bf16 = jnp.bfloat16
MESH = pl.DeviceIdType.MESH
ANY = pl.BlockSpec(memory_space=pl.ANY)

EPS = 1e-6
LANES = 128
CHUNK = 128
HEAD_DIM = 64
N_Q_HEADS = 16
N_KV_HEADS = 4
Q_PER_KV = N_Q_HEADS // N_KV_HEADS
N_SHARDS = 4
N_DEV = 8

ADAM_LR = 0.001
ADAM_B1 = 0.9
ADAM_B2 = 0.999
ADAM_EPS = 1e-08
ADAM_WD = 0.01
ADAM_STEP = 10
ADAM_C1 = 1.0 - ADAM_B1 ** ADAM_STEP
ADAM_C2 = 1.0 - ADAM_B2 ** ADAM_STEP

_INV_SQRT2 = 1.0 / math.sqrt(2.0)
_INV_SQRT2PI = 1.0 / math.sqrt(2.0 * math.pi)


def _params(*sem):
    return pltpu.CompilerParams(dimension_semantics=sem)


def _gelu(z):
    return 0.5 * z * (1.0 + lax.erf(z * _INV_SQRT2))


def _gelu_and_grad(z):
    cdf = 0.5 * (1.0 + lax.erf(z * _INV_SQRT2))
    return z * cdf, cdf + z * jnp.exp(-0.5 * z * z) * _INV_SQRT2PI


def _dot(a, b):
    return jnp.dot(a, b, preferred_element_type=f32)


def _dot_nt(a, b):
    return lax.dot_general(a, b, (((1,), (1,)), ((), ())), preferred_element_type=f32)


def _dot_tn(a, b):
    return lax.dot_general(a, b, (((0,), (0,)), ((), ())), preferred_element_type=f32)


def _dot_split(a, b):
    hi = a.astype(bf16)
    lo = (a - hi.astype(f32)).astype(bf16)
    return _dot(hi, b) + _dot(lo, b)


VMEM_TILE_BUDGET = 52 * 1024 * 1024
MAX_ROW_TILE = 2048


def _row_tile(m, fixed_bytes, row_bytes):
    tm = min(m, MAX_ROW_TILE)
    while tm > 256 and 2 * (fixed_bytes + tm * row_bytes) > VMEM_TILE_BUDGET:
        tm //= 2
    return tm


def _isz(a):
    return jnp.dtype(a.dtype).itemsize


def mm_nn(a, w3, *, name, s0=0, ns=None, add=None, out_dtype=f32):
    m, k = a.shape
    s_all, _, n_s = w3.shape
    ns = s_all if ns is None else ns
    tm = _row_tile(m, k * n_s * 2, k * _isz(a) + n_s * jnp.dtype(out_dtype).itemsize + (0 if add is None else n_s * 4))

    def body(*refs):
        if add is None:
            a_ref, w_ref, o_ref = refs
            acc = _dot(a_ref[...].astype(bf16), w_ref[0])
        else:
            a_ref, w_ref, add_ref, o_ref = refs
            acc = _dot(a_ref[...].astype(bf16), w_ref[0]) + add_ref[...]
        o_ref[...] = acc.astype(out_dtype)

    in_specs = [pl.BlockSpec((tm, k), lambda j, i: (i, 0)),
                pl.BlockSpec((1, k, n_s), lambda j, i: (s0 + j, 0, 0))]
    args = [a, w3]
    if add is not None:
        in_specs.append(pl.BlockSpec((tm, n_s), lambda j, i: (i, j)))
        args.append(add)
    return pl.pallas_call(
        body, name=name, grid=(ns, m // tm), in_specs=in_specs,
        out_specs=pl.BlockSpec((tm, n_s), lambda j, i: (i, j)),
        out_shape=jax.ShapeDtypeStruct((m, ns * n_s), out_dtype),
        compiler_params=_params("parallel", "parallel"))(*args)


def mm_nt(a_list, w3, *, name, tko=None, add=None, out_dtype=f32):
    s_all, k_out, n_s = w3.shape
    m = a_list[0].shape[0]
    na = len(a_list)
    spa = s_all // na
    tko = k_out if tko is None else tko
    tm = _row_tile(m, tko * n_s * 2, na * n_s * _isz(a_list[0]) + tko * 4 * (1 if add is None else 2))

    def body(*refs):
        a_refs = refs[:na]
        w_ref = refs[na]
        o_ref = refs[-1]
        s = pl.program_id(2)

        @pl.when(s == 0)
        def _():
            if add is None:
                o_ref[...] = jnp.zeros_like(o_ref)
            else:
                o_ref[...] = refs[na + 1][...]

        for idx in range(na):
            @pl.when(s // spa == idx)
            def _(idx=idx):
                o_ref[...] += _dot_nt(a_refs[idx][...].astype(bf16), w_ref[0])

    def a_map(idx):
        return lambda ko, i, s: (i, jnp.clip(s - idx * spa, 0, spa - 1))

    in_specs = [pl.BlockSpec((tm, n_s), a_map(idx)) for idx in range(na)]
    in_specs.append(pl.BlockSpec((1, tko, n_s), lambda ko, i, s: (s, ko, 0)))
    args = list(a_list) + [w3]
    if add is not None:
        in_specs.append(pl.BlockSpec((tm, tko), lambda ko, i, s: (i, ko)))
        args.append(add)
    return pl.pallas_call(
        body, name=name, grid=(k_out // tko, m // tm, s_all), in_specs=in_specs,
        out_specs=pl.BlockSpec((tm, tko), lambda ko, i, s: (i, ko)),
        out_shape=jax.ShapeDtypeStruct((m, k_out), out_dtype),
        compiler_params=_params("parallel", "parallel", "arbitrary"))(*args)


def mm_tn(a, b_list, c_arr, *, name, n_s, shard_rows, tki=None):
    m, k_in = a.shape
    na = len(b_list)
    s_all = sum(b.shape[1] for b in b_list) // n_s
    spa = s_all // na
    tki = k_in if tki is None else tki
    tm = _row_tile(m, tki * n_s * 4, tki * _isz(a) + na * n_s * _isz(b_list[0]))

    nsteps = m // tm
    per_blk = tki // shard_rows
    half = shard_rows // 2

    def body(c_ref, *refs):
        a_ref = refs[0]
        b_refs = refs[1:1 + na]
        o_ref, ob_ref = refs[-2], refs[-1]
        s = pl.program_id(0)
        r = pl.program_id(2)

        @pl.when(r == 0)
        def _():
            o_ref[...] = jnp.zeros_like(o_ref)

        for idx in range(na):
            @pl.when(s // spa == idx)
            def _(idx=idx):
                o_ref[0] += _dot_tn(a_ref[...].astype(bf16), b_refs[idx][...].astype(bf16))

        @pl.when(r == nsteps - 1)
        def _():
            for q in range(per_blk):
                start = pl.multiple_of(q * shard_rows + (1 - c_ref[0]) * half, 16)
                ob_ref[q] = o_ref[0, pl.ds(start, half), :].astype(bf16)

    def b_map(idx):
        def index(s, ki, r, c_ref):
            active = (s // spa) == idx
            return (jnp.where(active, r, 0), jnp.clip(s - idx * spa, 0, spa - 1))
        return index

    in_specs = [pl.BlockSpec((tm, tki), lambda s, ki, r, c_ref: (r, ki))]
    in_specs += [pl.BlockSpec((tm, n_s), b_map(idx)) for idx in range(na)]
    n_blk = k_in // tki
    return pl.pallas_call(
        body, name=name,
        grid_spec=pltpu.PrefetchScalarGridSpec(
            num_scalar_prefetch=1, grid=(s_all, n_blk, nsteps), in_specs=in_specs,
            out_specs=[pl.BlockSpec((1, tki, n_s), lambda s, ki, r, c_ref: (s, ki, 0)),
                       pl.BlockSpec((per_blk, half, n_s), lambda s, ki, r, c_ref: (s * n_blk + ki, 0, 0))]),
        out_shape=[jax.ShapeDtypeStruct((s_all, k_in, n_s), f32),
                   jax.ShapeDtypeStruct((s_all * k_in // shard_rows, half, n_s), bf16)],
        compiler_params=_params("parallel", "parallel", "arbitrary"))(c_arr, a, *b_list)


def mm_nt_rms_bwd(a_list, w3, x, r, g, dx_in, *, name, extra=None):
    s_all, d, n_s = w3.shape
    m = a_list[0].shape[0]
    na = len(a_list)
    spa = s_all // na
    ne = 0 if extra is None else 1
    tm = _row_tile(m, d * n_s * 2, na * n_s * _isz(a_list[0]) + d * 4 * (4 + ne))

    def body(*refs):
        a_refs, w_ref = refs[:na], refs[na]
        x_ref, r_ref, g_ref, dxin_ref = refs[na + 1:na + 5]
        dh2_ref, g2_ref = (refs[na + 5], refs[na + 6]) if ne else (None, None)
        outs = refs[na + 5 + 2 * ne:]
        dx_ref, dg_ref = outs[0], outs[1]
        dg2_ref = outs[2] if ne else None
        acc_ref = outs[-1]
        i, s = pl.program_id(0), pl.program_id(1)

        @pl.when(s == 0)
        def _():
            acc_ref[...] = jnp.zeros_like(acc_ref)

        for idx in range(na):
            @pl.when(s // spa == idx)
            def _(idx=idx):
                acc_ref[...] += _dot_nt(a_refs[idx][...].astype(bf16), w_ref[0])

        @pl.when(s == s_all - 1)
        def _():
            rv = r_ref[...]
            xh = x_ref[...] * rv
            total = dxin_ref[...]
            pairs = [(acc_ref[...], g_ref, dg_ref)] + ([(dh2_ref[...], g2_ref, dg2_ref)] if ne else [])
            for dh, gain_ref, dgain_ref in pairs:
                part = jnp.sum(dh * xh, axis=0, keepdims=True)

                @pl.when(i == 0)
                def _(dgain_ref=dgain_ref, part=part):
                    dgain_ref[...] = part

                @pl.when(i > 0)
                def _(dgain_ref=dgain_ref, part=part):
                    dgain_ref[...] += part

                tg = dh * gain_ref[...]
                total = total + rv * (tg - xh * jnp.mean(tg * xh, axis=1, keepdims=True))
            dx_ref[...] = total

    def a_map(idx):
        return lambda i, s: (i, jnp.clip(s - idx * spa, 0, spa - 1))

    row = pl.BlockSpec((tm, d), lambda i, s: (i, 0))
    vec = pl.BlockSpec((1, d), lambda i, s: (0, 0))
    in_specs = [pl.BlockSpec((tm, n_s), a_map(idx)) for idx in range(na)]
    in_specs += [pl.BlockSpec((1, d, n_s), lambda i, s: (s, 0, 0)), row, pl.BlockSpec((tm, 1), lambda i, s: (i, 0)), vec, row]
    args = list(a_list) + [w3, x, r, g, dx_in]
    if ne:
        in_specs += [row, vec]
        args += list(extra)
    outs = pl.pallas_call(
        body, name=name, grid=(m // tm, s_all), in_specs=in_specs, out_specs=[row] + [vec] * (1 + ne),
        out_shape=[jax.ShapeDtypeStruct((m, d), f32)] + [jax.ShapeDtypeStruct((1, d), f32)] * (1 + ne),
        scratch_shapes=[pltpu.VMEM((tm, d), f32)],
        compiler_params=_params("arbitrary", "arbitrary"))(*args)
    return outs


def mm_residual(a, w, x, *, name, gains=(), target=None):
    m, k = a.shape
    d = w.shape[1]
    ng = len(gains)
    tm = _row_tile(m, k * d * 2, k * _isz(a) + d * 4 * 3 + ng * d * 2)

    def body(*refs):
        a_ref, w_ref, x_ref = refs[:3]
        y = _dot(a_ref[...].astype(bf16), w_ref[...]) + x_ref[...]
        if target is None:
            g_refs = refs[3:3 + ng]
            y_ref = refs[3 + ng]
            h_refs = refs[4 + ng:4 + 2 * ng]
            r_ref = refs[-1]
            y_ref[...] = y
            r = lax.rsqrt(jnp.mean(y * y, axis=1, keepdims=True) + EPS)
            yh = y * r
            for g_ref, h_ref in zip(g_refs, h_refs):
                h_ref[...] = (yh * g_ref[...]).astype(bf16)
            r_ref[...] = r
        else:
            t_ref, dy_ref, s_ref = refs[3:]
            i = pl.program_id(0)
            e = y - t_ref[...]
            dy_ref[...] = e * (1.0 / d)
            part = jnp.sum(e * e, axis=0, keepdims=True)

            @pl.when(i == 0)
            def _():
                s_ref[...] = part

            @pl.when(i > 0)
            def _():
                s_ref[...] += part

    row = pl.BlockSpec((tm, d), lambda i: (i, 0))
    vec = pl.BlockSpec((1, d), lambda i: (0, 0))
    in_specs = [pl.BlockSpec((tm, k), lambda i: (i, 0)), pl.BlockSpec((k, d), lambda i: (0, 0)), row]
    if target is None:
        outs = pl.pallas_call(
            body, name=name, grid=(m // tm,), in_specs=in_specs + [vec] * ng,
            out_specs=[row] * (1 + ng) + [pl.BlockSpec((tm, 1), lambda i: (i, 0))],
            out_shape=[jax.ShapeDtypeStruct((m, d), f32)] + [jax.ShapeDtypeStruct((m, d), bf16)] * ng
            + [jax.ShapeDtypeStruct((m, 1), f32)],
            compiler_params=_params("parallel"))(a, w, x, *gains)
        return outs[0], outs[1:1 + ng], outs[-1]
    return pl.pallas_call(
        body, name=name, grid=(m // tm,), in_specs=in_specs + [row], out_specs=[row, vec],
        out_shape=[jax.ShapeDtypeStruct((m, d), f32), jax.ShapeDtypeStruct((1, d), f32)],
        compiler_params=_params("arbitrary"))(a, w, x, target)


def rms_fwd(x, gains, *, name, tr=512):
    t, d = x.shape
    tr = min(tr, t)
    ng = len(gains)

    def body(*refs):
        x_ref = refs[0]
        g_refs = refs[1:1 + ng]
        h_refs = refs[1 + ng:1 + 2 * ng]
        r_ref = refs[-1]
        xv = x_ref[...]
        r = lax.rsqrt(jnp.mean(xv * xv, axis=1, keepdims=True) + EPS)
        xh = xv * r
        for g_ref, h_ref in zip(g_refs, h_refs):
            h_ref[...] = (xh * g_ref[...]).astype(bf16)
        r_ref[...] = r

    row = pl.BlockSpec((tr, d), lambda i: (i, 0))
    vec = pl.BlockSpec((1, d), lambda i: (0, 0))
    outs = pl.pallas_call(
        body, name=name, grid=(t // tr,), in_specs=[row] + [vec] * ng,
        out_specs=[row] * ng + [pl.BlockSpec((tr, 1), lambda i: (i, 0))],
        out_shape=[jax.ShapeDtypeStruct((t, d), bf16)] * ng + [jax.ShapeDtypeStruct((t, 1), f32)],
        compiler_params=_params("parallel"))(x, *gains)
    return outs[:ng], outs[ng]


def sgu_gate_fwd(zuv, gv, wc, bt, *, name, tr=512):
    t, w = zuv.shape[0], zuv.shape[1] // 2
    tr = min(tr, t)
    groups = w // LANES

    def body(zu_ref, zv_ref, gv_ref, wc_ref, bt_ref, y_ref):
        vp = _gelu(zv_ref[...])
        rv = lax.rsqrt(jnp.mean(vp * vp, axis=1, keepdims=True) + EPS)
        vb = (vp * rv * gv_ref[...]).astype(bf16)
        for c in range(tr // CHUNK):
            rows = slice(c * CHUNK, (c + 1) * CHUNK)
            for g in range(groups):
                cols = slice(g * LANES, (g + 1) * LANES)
                sv = _dot(wc_ref[g], vb[rows, cols]) + bt_ref[:, g:g + 1]
                y_ref[rows, cols] = (_gelu(zu_ref[rows, cols]) * sv).astype(bf16)

    row = pl.BlockSpec((tr, w), lambda i: (i, 0))
    return pl.pallas_call(
        body, name=name, grid=(t // tr,),
        in_specs=[row, pl.BlockSpec((tr, w), lambda i: (i, 1)), pl.BlockSpec((1, w), lambda i: (0, 0)),
                  pl.BlockSpec((groups, CHUNK, CHUNK), lambda i: (0, 0, 0)),
                  pl.BlockSpec((CHUNK, groups), lambda i: (0, 0))],
        out_specs=row, out_shape=jax.ShapeDtypeStruct((t, w), bf16),
        compiler_params=_params("parallel"))(zuv, zuv, gv, wc, bt)


def sgu_gate_bwd(zuv, dy, gv, wc, bt, *, name, tr=512):
    t, w = zuv.shape[0], zuv.shape[1] // 2
    tr = min(tr, t)
    groups = w // LANES
    nsteps = t // tr

    def body(zu_ref, zv_ref, dy_ref, gv_ref, wc_ref, bt_ref,
             dzu_ref, dzv_ref, dgv_ref, dws_ref, dbt_ref, dv_ref, bacc_ref):
        i = pl.program_id(0)

        @pl.when(i == 0)
        def _():
            dgv_ref[...] = jnp.zeros_like(dgv_ref)
            dws_ref[...] = jnp.zeros_like(dws_ref)
            bacc_ref[...] = jnp.zeros_like(bacc_ref)

        vp, vp_grad = _gelu_and_grad(zv_ref[...])
        rv = lax.rsqrt(jnp.mean(vp * vp, axis=1, keepdims=True) + EPS)
        vhat = vp * rv
        vb = (vhat * gv_ref[...]).astype(bf16)
        for c in range(tr // CHUNK):
            rows = slice(c * CHUNK, (c + 1) * CHUNK)
            for g in range(groups):
                cols = slice(g * LANES, (g + 1) * LANES)
                vblk = vb[rows, cols]
                sv = _dot(wc_ref[g], vblk) + bt_ref[:, g:g + 1]
                zub = zu_ref[rows, cols]
                dyb = dy_ref[rows, cols]
                ub, ub_grad = _gelu_and_grad(zub)
                dzu_ref[rows, cols] = (dyb * sv * ub_grad).astype(bf16)
                dsv = dyb * ub
                bacc_ref[:, cols] += dsv
                dsvb = dsv.astype(bf16)
                dv_ref[rows, cols] = _dot_tn(wc_ref[g], dsvb)
                dws_ref[g] += _dot_nt(dsvb, vblk)
        dv = dv_ref[...]
        dgv_ref[...] += jnp.sum(dv * vhat, axis=0, keepdims=True)
        tg = dv * gv_ref[...]
        dvp = rv * (tg - vhat * jnp.mean(tg * vhat, axis=1, keepdims=True))
        dzv_ref[...] = (dvp * vp_grad).astype(bf16)

        @pl.when(i == nsteps - 1)
        def _():
            tt = lax.broadcasted_iota(jnp.int32, (CHUNK, CHUNK), 0)
            ss = lax.broadcasted_iota(jnp.int32, (CHUNK, CHUNK), 1)
            for g in range(groups):
                dws_ref[g] = jnp.where(ss <= tt, dws_ref[g], 0.0)
                dbt_ref[:, g:g + 1] = jnp.sum(bacc_ref[:, g * LANES:(g + 1) * LANES], axis=1, keepdims=True)

    row = pl.BlockSpec((tr, w), lambda i: (i, 0))
    full3 = pl.BlockSpec((groups, CHUNK, CHUNK), lambda i: (0, 0, 0))
    return pl.pallas_call(
        body, name=name, grid=(nsteps,),
        in_specs=[row, pl.BlockSpec((tr, w), lambda i: (i, 1)), row, pl.BlockSpec((1, w), lambda i: (0, 0)), full3,
                  pl.BlockSpec((CHUNK, groups), lambda i: (0, 0))],
        out_specs=[row, row, pl.BlockSpec((1, w), lambda i: (0, 0)), full3,
                   pl.BlockSpec((CHUNK, groups), lambda i: (0, 0))],
        out_shape=[jax.ShapeDtypeStruct((t, w), bf16), jax.ShapeDtypeStruct((t, w), bf16),
                   jax.ShapeDtypeStruct((1, w), f32), jax.ShapeDtypeStruct((groups, CHUNK, CHUNK), f32),
                   jax.ShapeDtypeStruct((CHUNK, groups), f32)],
        scratch_shapes=[pltpu.VMEM((tr, w), f32), pltpu.VMEM((CHUNK, w), f32)],
        compiler_params=_params("arbitrary"))(zuv, zuv, dy, gv, wc, bt)


HALO = 8


def _shift_down(v, halo, k, first):
    r = pltpu.roll(v, k, 0)
    hh = jnp.where(first, 0.0, pltpu.roll(halo, k, 0))
    rid = lax.broadcasted_iota(jnp.int32, (HALO, v.shape[1]), 0)
    head = jnp.where(rid < k, hh, r[0:HALO])
    if v.shape[0] == HALO:
        return head
    return jnp.concatenate([head, r[HALO:]], axis=0)


def _shift_up(v, halo, k, last):
    n = v.shape[0]
    r = pltpu.roll(v, n - k, 0)
    hh = jnp.where(last, 0.0, pltpu.roll(halo, HALO - k, 0))
    rid = lax.broadcasted_iota(jnp.int32, (HALO, v.shape[1]), 0)
    tail = jnp.where(rid >= HALO - k, hh, r[n - HALO:])
    return jnp.concatenate([r[:n - HALO], tail], axis=0)


def _conv(p, halo, w_ref, b_ref, first):
    return (w_ref[2:3, :] * p + w_ref[1:2, :] * _shift_down(p, halo, 1, first)
            + w_ref[0:1, :] * _shift_down(p, halo, 2, first) + b_ref[...])


BF16_ROWS = 16


def ffn_in_fused(h, w_in4, wg, wu, bg, bu, *, name):
    t, k = h.shape
    s_all, _, n_s = w_in4.shape
    half = s_all // 2
    tm = _row_tile(t, 2 * k * n_s * 2, k * 2 + 4 * n_s * 4 + n_s * 2)

    def body(h_ref, hh_ref, wg_ref, wu_ref, cg_ref, cu_ref, bg_ref, bu_ref, pg_ref, pu_ref, gate_ref, up_ref, a_ref):
        first = pl.program_id(1) == 0
        hv, hh = h_ref[...], hh_ref[...]
        outs = []
        for w_ref, c_ref, b_ref, p_ref, o_ref in ((wg_ref, cg_ref, bg_ref, pg_ref, gate_ref),
                                                  (wu_ref, cu_ref, bu_ref, pu_ref, up_ref)):
            p = _dot(hv, w_ref[0])
            p_ref[...] = p
            hu = _conv(p, _dot(hh, w_ref[0])[BF16_ROWS - HALO:], c_ref, b_ref, first)
            o_ref[...] = hu
            outs.append(hu)
        gate, up = outs
        a_ref[...] = (gate * jax.nn.sigmoid(gate) * up).astype(bf16)

    tile = pl.BlockSpec((tm, n_s), lambda j, i: (i, j))
    cw = pl.BlockSpec((3, n_s), lambda j, i: (0, j))
    cb = pl.BlockSpec((1, n_s), lambda j, i: (0, j))
    f = half * n_s
    return pl.pallas_call(
        body, name=name, grid=(half, t // tm),
        in_specs=[pl.BlockSpec((tm, k), lambda j, i: (i, 0)),
                  pl.BlockSpec((BF16_ROWS, k), lambda j, i: (jnp.maximum(i * (tm // BF16_ROWS) - 1, 0), 0)),
                  pl.BlockSpec((1, k, n_s), lambda j, i: (j, 0, 0)),
                  pl.BlockSpec((1, k, n_s), lambda j, i: (j + half, 0, 0)), cw, cw, cb, cb],
        out_specs=[tile] * 5,
        out_shape=[jax.ShapeDtypeStruct((t, f), f32)] * 4 + [jax.ShapeDtypeStruct((t, f), bf16)],
        compiler_params=_params("parallel", "parallel"))(h, h, w_in4, w_in4, wg, wu, bg, bu)


def _gate_grads(gate, up, dav):
    sg = jax.nn.sigmoid(gate)
    return dav * up * (sg * (1.0 + gate * (1.0 - sg))), dav * gate * sg


GATE_BWD_ROWS = 512


def ffn_gate_bwd(dy, w_out, pg, pu, gate, up, wg, wu, *, name):
    t, f = pg.shape
    d = dy.shape[1]
    tr = min(GATE_BWD_ROWS, t)
    nsteps = t // tr
    tc = f // 2

    def body(dy_ref, dyn_ref, w_ref, pg_ref, pu_ref, gate_ref, gaten_ref, up_ref, upn_ref, wg_ref, wu_ref,
             dg_ref, du_ref, sg_ref, su_ref):
        i = pl.program_id(1)
        last = i == nsteps - 1
        w = w_ref[0]
        da = _dot_nt(dy_ref[...].astype(bf16), w)
        da_n = _dot_nt(dyn_ref[...].astype(bf16), w)
        dgate, dup = _gate_grads(gate_ref[...], up_ref[...], da)
        dgate_n, dup_n = _gate_grads(gaten_ref[...], upn_ref[...], da_n)
        rid = lax.broadcasted_iota(jnp.int32, (8, tc), 0)
        for dd, d_n, c_ref, p_ref, o_ref, s_ref in ((dgate, dgate_n, wg_ref, pg_ref, dg_ref, sg_ref),
                                                    (dup, dup_n, wu_ref, pu_ref, du_ref, su_ref)):
            d1, d2 = _shift_up(dd, d_n, 1, last), _shift_up(dd, d_n, 2, last)
            o_ref[...] = (c_ref[2:3, :] * dd + c_ref[1:2, :] * d1 + c_ref[0:1, :] * d2).astype(bf16)
            p = p_ref[...]
            sums = [jnp.sum(d2 * p, axis=0, keepdims=True), jnp.sum(d1 * p, axis=0, keepdims=True),
                    jnp.sum(dd * p, axis=0, keepdims=True), jnp.sum(dd, axis=0, keepdims=True)]
            part = jnp.zeros((8, tc), f32)
            for k, sk in enumerate(sums):
                part = jnp.where(rid == k, sk, part)

            @pl.when(i == 0)
            def _(s_ref=s_ref, part=part):
                s_ref[...] = part

            @pl.when(i > 0)
            def _(s_ref=s_ref, part=part):
                s_ref[...] += part

    def nxt_rows(j, i):
        return (jnp.minimum((i + 1) * (tr // HALO), t // HALO - 1), j)

    tile = pl.BlockSpec((tr, tc), lambda j, i: (i, j))
    nxt = pl.BlockSpec((HALO, tc), nxt_rows)
    wspec = pl.BlockSpec((3, tc), lambda j, i: (0, j))
    stat = pl.BlockSpec((8, tc), lambda j, i: (0, j))
    return pl.pallas_call(
        body, name=name, grid=(2, nsteps),
        in_specs=[pl.BlockSpec((tr, d), lambda j, i: (i, 0)),
                  pl.BlockSpec((HALO, d), lambda j, i: (nxt_rows(j, i)[0], 0)),
                  pl.BlockSpec((1, tc, d), lambda j, i: (j, 0, 0)),
                  tile, tile, tile, nxt, tile, nxt, wspec, wspec],
        out_specs=[tile, tile, stat, stat],
        out_shape=[jax.ShapeDtypeStruct((t, f), bf16), jax.ShapeDtypeStruct((t, f), bf16),
                   jax.ShapeDtypeStruct((8, f), f32), jax.ShapeDtypeStruct((8, f), f32)],
        compiler_params=_params("parallel", "arbitrary"))(
            dy, dy, w_out.reshape(2, tc, d), pg, pu, gate, gate, up, up, wg, wu)


def _head_mean_matrix():
    i = lax.broadcasted_iota(jnp.int32, (LANES, LANES), 0) // HEAD_DIM
    j = lax.broadcasted_iota(jnp.int32, (LANES, LANES), 1) // HEAD_DIM
    return jnp.where(i == j, 1.0 / HEAD_DIM, 0.0).astype(bf16)


def _lane_half(shape):
    return (lax.broadcasted_iota(jnp.int32, shape, 1) % LANES) // HEAD_DIM


def q_proj_norm(h, w, g2, *, name, scale):
    t, k = h.shape
    n = w.shape[1]
    tm = _row_tile(t, k * n * 2, k * 2 + n * 4 + n * 2)

    def body(h_ref, w_ref, g_ref, qp_ref, o_ref):
        qp_ref[...] = _dot(h_ref[...], w_ref[...])
        bd = _head_mean_matrix()
        for cb in range(n // LANES):
            cols = slice(cb * LANES, (cb + 1) * LANES)
            xc = qp_ref[:, cols]
            rh = lax.rsqrt(_dot_split(xc * xc, bd) + EPS)
            o_ref[:, cols] = (xc * rh * g_ref[...] * scale).astype(bf16)

    row = pl.BlockSpec((tm, n), lambda i: (i, 0))
    return pl.pallas_call(
        body, name=name, grid=(t // tm,),
        in_specs=[pl.BlockSpec((tm, k), lambda i: (i, 0)), pl.BlockSpec((k, n), lambda i: (0, 0)),
                  pl.BlockSpec((1, LANES), lambda i: (0, 0))],
        out_specs=[row, row], out_shape=[jax.ShapeDtypeStruct((t, n), f32), jax.ShapeDtypeStruct((t, n), bf16)],
        compiler_params=_params("parallel"))(h, w, g2)


def kv_proj_post(h, w, g2, *, name):
    t, k = h.shape
    n = w.shape[1]
    kw = n // 2
    tm = _row_tile(t, k * n * 2, k * 2 + n * 4 + 2 * n * 2)

    def body(h_ref, w_ref, g_ref, kv_ref, k_ref, v_ref):
        kv_ref[...] = _dot(h_ref[...], w_ref[...])
        bd = _head_mean_matrix()
        half = _lane_half((tm, LANES))
        for cb in range(kw // LANES):
            xc = kv_ref[:, cb * LANES:(cb + 1) * LANES]
            rh = lax.rsqrt(_dot_split(xc * xc, bd) + EPS)
            kn = xc * rh * g_ref[...]
            vc = kv_ref[:, kw + cb * LANES:kw + (cb + 1) * LANES]
            for src, dst in ((kn, k_ref), (vc, v_ref)):
                sw = pltpu.roll(src, HEAD_DIM, 1)
                for hf in range(2):
                    blk = 2 * cb + hf
                    dst[:, blk * LANES:(blk + 1) * LANES] = jnp.where(half == hf, src, sw).astype(bf16)

    row = pl.BlockSpec((tm, n), lambda i: (i, 0))
    return pl.pallas_call(
        body, name=name, grid=(t // tm,),
        in_specs=[pl.BlockSpec((tm, k), lambda i: (i, 0)), pl.BlockSpec((k, n), lambda i: (0, 0)),
                  pl.BlockSpec((1, LANES), lambda i: (0, 0))],
        out_specs=[row, row, row],
        out_shape=[jax.ShapeDtypeStruct((t, n), f32), jax.ShapeDtypeStruct((t, n), bf16), jax.ShapeDtypeStruct((t, n), bf16)],
        compiler_params=_params("parallel"))(h, w, g2)


def q_norm_bwd(dq, qp, g2, *, name, scale, tr=512):
    t, w = qp.shape
    tr = min(tr, t)

    def body(dq_ref, x_ref, g_ref, o_ref, dg_ref):
        i = pl.program_id(0)
        bd = _head_mean_matrix()
        acc = jnp.zeros((1, LANES), f32)
        for cb in range(w // LANES):
            cols = slice(cb * LANES, (cb + 1) * LANES)
            xc = x_ref[:, cols]
            rh = lax.rsqrt(_dot_split(xc * xc, bd) + EPS)
            xh = xc * rh
            dy = dq_ref[:, cols] * scale
            acc = acc + jnp.sum(dy * xh, axis=0, keepdims=True)
            tg = dy * g_ref[...]
            o_ref[:, cols] = (rh * (tg - xh * _dot_split(tg * xh, bd))).astype(bf16)

        @pl.when(i == 0)
        def _():
            dg_ref[...] = acc

        @pl.when(i > 0)
        def _():
            dg_ref[...] += acc

    row = pl.BlockSpec((tr, w), lambda i: (i, 0))
    vec = pl.BlockSpec((1, LANES), lambda i: (0, 0))
    return pl.pallas_call(
        body, name=name, grid=(t // tr,), in_specs=[row, row, vec], out_specs=[row, vec],
        out_shape=[jax.ShapeDtypeStruct((t, w), bf16), jax.ShapeDtypeStruct((1, LANES), f32)],
        compiler_params=_params("arbitrary"))(dq, qp, g2)


def kv_post_bwd(dk2, dv2, kv, g2, *, name, tr=512):
    t, w = kv.shape
    tr = min(tr, t)
    kw = w // 2

    def body(dk_ref, dv_ref, x_ref, g_ref, o_ref, dg_ref):
        i = pl.program_id(0)
        bd = _head_mean_matrix()
        half = _lane_half((tr, LANES))
        acc = jnp.zeros((1, LANES), f32)

        def fold(ref, cb):
            a = ref[:, (2 * cb) * LANES:(2 * cb + 1) * LANES]
            b = ref[:, (2 * cb + 1) * LANES:(2 * cb + 2) * LANES]
            return jnp.where(half == 0, a + pltpu.roll(a, HEAD_DIM, 1), b + pltpu.roll(b, HEAD_DIM, 1))

        for cb in range(kw // LANES):
            cols = slice(cb * LANES, (cb + 1) * LANES)
            xc = x_ref[:, cols]
            rh = lax.rsqrt(_dot_split(xc * xc, bd) + EPS)
            xh = xc * rh
            dy = fold(dk_ref, cb)
            acc = acc + jnp.sum(dy * xh, axis=0, keepdims=True)
            tg = dy * g_ref[...]
            o_ref[:, cols] = (rh * (tg - xh * _dot_split(tg * xh, bd))).astype(bf16)
            o_ref[:, kw + cb * LANES:kw + (cb + 1) * LANES] = fold(dv_ref, cb).astype(bf16)

        @pl.when(i == 0)
        def _():
            dg_ref[...] = acc

        @pl.when(i > 0)
        def _():
            dg_ref[...] += acc

    dup = pl.BlockSpec((tr, 2 * kw), lambda i: (i, 0))
    row = pl.BlockSpec((tr, w), lambda i: (i, 0))
    vec = pl.BlockSpec((1, LANES), lambda i: (0, 0))
    return pl.pallas_call(
        body, name=name, grid=(t // tr,), in_specs=[dup, dup, row, vec], out_specs=[row, vec],
        out_shape=[jax.ShapeDtypeStruct((t, w), bf16), jax.ShapeDtypeStruct((1, LANES), f32)],
        compiler_params=_params("arbitrary"))(dk2, dv2, kv, g2)


def _slope(h):
    return 2.0 ** (-8.0 * (h + 1) / N_Q_HEADS)


GROUP_ROWS = Q_PER_KV * CHUNK


def _band_mask(n):
    tq = lax.broadcasted_iota(jnp.int32, (GROUP_ROWS, 2 * CHUNK), 0) % CHUNK
    jk = lax.broadcasted_iota(jnp.int32, (GROUP_ROWS, 2 * CHUNK), 1)
    dist = tq + CHUNK - jk
    ok = (dist >= 0) & (dist < CHUNK) & jnp.logical_not((n == 0) & (jk < CHUNK))
    return dist.astype(f32), ok


def _band(ref, n, kh):
    p0 = pl.multiple_of(jnp.maximum(n - 1, 0) * CHUNK, CHUNK)
    c0 = pl.multiple_of(n * CHUNK, CHUNK)
    cols = slice(kh * LANES, (kh + 1) * LANES)
    return jnp.concatenate([ref[pl.ds(p0, CHUNK), cols], ref[pl.ds(c0, CHUNK), cols]], axis=0)


def _stack_heads(ref, kh, half):
    parts = []
    for cb in (2 * kh, 2 * kh + 1):
        xc = ref[:, cb * LANES:(cb + 1) * LANES].astype(f32)
        parts += [jnp.where(half == hf, xc, 0.0).astype(bf16) for hf in range(2)]
    return jnp.concatenate(parts, axis=0)


def _unstack_heads(x4, half):
    return (jnp.where(half == 0, x4[0:CHUNK], x4[CHUNK:2 * CHUNK]),
            jnp.where(half == 0, x4[2 * CHUNK:3 * CHUNK], x4[3 * CHUNK:]))


def _per_head_column(kh, values):
    grp = lax.broadcasted_iota(jnp.int32, (GROUP_ROWS, 1), 0) // CHUNK
    col = jnp.full((GROUP_ROWS, 1), values[0], f32)
    for g in range(1, Q_PER_KV):
        col = jnp.where(grp == g, values[g], col)
    return col


def _softmax_band(q4, kband, dist, ok, slope, sink):
    s = _dot_nt(q4, kband)
    s = jnp.where(ok, s - slope * dist, -jnp.inf)
    m = jnp.maximum(jnp.max(s, axis=1, keepdims=True), sink)
    e = jnp.exp(s - m)
    es = jnp.exp(sink - m)
    inv = 1.0 / (jnp.sum(e, axis=1, keepdims=True) + es)
    return e * inv, es * inv


def attn_fwd(q, k2, v2, sinks, *, name):
    t, w = q.shape
    nb = t // CHUNK

    def body(sink_ref, q_ref, k_ref, v_ref, o_ref):
        n = pl.program_id(0)
        dist, ok = _band_mask(n)
        half = _lane_half((CHUNK, LANES))
        khs = range(N_KV_HEADS)
        heads = [[Q_PER_KV * kh + g for g in range(Q_PER_KV)] for kh in khs]
        q4 = [_stack_heads(q_ref, kh, half) for kh in khs]
        soft = [_softmax_band(q4[kh], _band(k_ref, n, kh), dist, ok, _per_head_column(kh, [_slope(h) for h in heads[kh]]),
                              _per_head_column(kh, [sink_ref[h] for h in heads[kh]])) for kh in khs]
        o4 = [_dot(soft[kh][0].astype(bf16), _band(v_ref, n, kh)) for kh in khs]
        for kh in khs:
            lo, hi = _unstack_heads(o4[kh], half)
            o_ref[:, (2 * kh) * LANES:(2 * kh + 1) * LANES] = lo.astype(bf16)
            o_ref[:, (2 * kh + 1) * LANES:(2 * kh + 2) * LANES] = hi.astype(bf16)

    full = pl.BlockSpec((t, k2.shape[1]), lambda n: (0, 0))
    return pl.pallas_call(
        body, name=name, grid=(nb,),
        in_specs=[pl.BlockSpec(memory_space=pltpu.SMEM), pl.BlockSpec((CHUNK, w), lambda n: (n, 0)), full, full],
        out_specs=pl.BlockSpec((CHUNK, w), lambda n: (n, 0)),
        out_shape=jax.ShapeDtypeStruct((t, w), bf16),
        compiler_params=_params("parallel"))(sinks, q, k2, v2)


def attn_bwd(q, k2, v2, do, sinks, *, name):
    t, w = q.shape
    nb = t // CHUNK
    kw = k2.shape[1]

    def body(sink_ref, q_ref, k_ref, v_ref, do_ref, dq_ref, dk_ref, dv_ref, ds_ref, kc_ref, vc_ref):
        n = pl.program_id(0)

        @pl.when(n == 0)
        def _():
            ds_ref[...] = jnp.zeros_like(ds_ref)
            kc_ref[...] = jnp.zeros_like(kc_ref)
            vc_ref[...] = jnp.zeros_like(vc_ref)
            dk_ref[...] = jnp.zeros_like(dk_ref)
            dv_ref[...] = jnp.zeros_like(dv_ref)

        @pl.when(n == nb)
        def _():
            dk_ref[...] = kc_ref[...]
            dv_ref[...] = vc_ref[...]

        @pl.when(n < nb)
        def _():
            dist, ok = _band_mask(n)
            half = _lane_half((CHUNK, LANES))
            lane = lax.broadcasted_iota(jnp.int32, (1, LANES), 1)
            sink_acc = jnp.zeros((1, LANES), f32)
            khs = range(N_KV_HEADS)
            heads = [[Q_PER_KV * kh + g for g in range(Q_PER_KV)] for kh in khs]
            q4 = [_stack_heads(q_ref, kh, half) for kh in khs]
            do4 = [_stack_heads(do_ref, kh, half) for kh in khs]
            kband = [_band(k_ref, n, kh) for kh in khs]
            vband = [_band(v_ref, n, kh) for kh in khs]
            soft = [_softmax_band(q4[kh], kband[kh], dist, ok, _per_head_column(kh, [_slope(h) for h in heads[kh]]),
                                  _per_head_column(kh, [sink_ref[h] for h in heads[kh]])) for kh in khs]
            dp = [_dot_nt(do4[kh], vband[kh]) for kh in khs]
            delta = [jnp.sum(soft[kh][0] * dp[kh], axis=1, keepdims=True) for kh in khs]
            dsb = [(soft[kh][0] * (dp[kh] - delta[kh])).astype(bf16) for kh in khs]
            dq4 = [_dot(dsb[kh], kband[kh]) for kh in khs]
            dkb = [_dot_tn(dsb[kh], q4[kh]) for kh in khs]
            dvb = [_dot_tn(soft[kh][0].astype(bf16), do4[kh]) for kh in khs]
            for kh in khs:
                sd = soft[kh][1] * delta[kh]
                for g, h in enumerate(heads[kh]):
                    part = jnp.sum(sd[g * CHUNK:(g + 1) * CHUNK], axis=0, keepdims=True)
                    sink_acc = sink_acc + jnp.where(lane == h, -part, 0.0)
                lo, hi = _unstack_heads(dq4[kh], half)
                dq_ref[:, (2 * kh) * LANES:(2 * kh + 1) * LANES] = lo
                dq_ref[:, (2 * kh + 1) * LANES:(2 * kh + 2) * LANES] = hi
                cols = slice(kh * LANES, (kh + 1) * LANES)
                dk_ref[:, cols] = kc_ref[:, cols] + dkb[kh][0:CHUNK]
                dv_ref[:, cols] = vc_ref[:, cols] + dvb[kh][0:CHUNK]
                kc_ref[:, cols] = dkb[kh][CHUNK:]
                vc_ref[:, cols] = dvb[kh][CHUNK:]
            ds_ref[...] += sink_acc

    full = pl.BlockSpec((t, kw), lambda n: (0, 0))
    qblk = pl.BlockSpec((CHUNK, w), lambda n: (jnp.minimum(n, nb - 1), 0))
    kblk = pl.BlockSpec((CHUNK, kw), lambda n: (jnp.maximum(n - 1, 0), 0))
    return pl.pallas_call(
        body, name=name, grid=(nb + 1,),
        in_specs=[pl.BlockSpec(memory_space=pltpu.SMEM), qblk, full, full, qblk],
        out_specs=[qblk, kblk, kblk, pl.BlockSpec((1, LANES), lambda n: (0, 0))],
        out_shape=[jax.ShapeDtypeStruct((t, w), f32), jax.ShapeDtypeStruct((t, kw), f32),
                   jax.ShapeDtypeStruct((t, kw), f32), jax.ShapeDtypeStruct((1, LANES), f32)],
        scratch_shapes=[pltpu.VMEM((CHUNK, kw), f32), pltpu.VMEM((CHUNK, kw), f32)],
        compiler_params=_params("arbitrary"))(sinks, q, k2, v2, do)


N_STEPS = 8


def _row_blocks(shape):
    if len(shape) == 2:
        r, c = shape
        return (r // N_STEPS, c), (lambda s: (s, 0))
    l, r, c = shape
    per = N_STEPS // l
    return (1, r // per, c), (lambda s: (s // per, s % per, 0))


CAST_STEPS = 4


def cast_into_slot(arrays, k_arr, *, name):
    in_specs, out_specs, out_shape, layers = [], [], [], []
    for a in arrays:
        r, c = a.shape[-2:]
        rb = r // CAST_STEPS
        if a.ndim == 2:
            in_specs.append(pl.BlockSpec((rb, c), lambda s, k: (s, 0)))
            layers.append(None)
        else:
            for l in range(a.shape[0]):
                in_specs.append(pl.BlockSpec((1, rb, c), lambda s, k, l=l: (l, s, 0)))
                layers.append(l)
        for _ in range(1 if a.ndim == 2 else a.shape[0]):
            out_specs.append(pl.BlockSpec((1, rb, c), lambda s, k: (k[0], s, 0)))
            out_shape.append(jax.ShapeDtypeStruct((N_SHARDS, r, c), bf16))
    n = len(in_specs)

    def body(k_ref, *refs):
        for i_ref, o_ref, l in zip(refs[:n], refs[n:], layers):
            o_ref[0] = (i_ref[...] if l is None else i_ref[0]).astype(bf16)

    args = []
    for a in arrays:
        args += [a] * (1 if a.ndim == 2 else a.shape[0])
    return pl.pallas_call(
        body, name=name,
        grid_spec=pltpu.PrefetchScalarGridSpec(num_scalar_prefetch=1, grid=(CAST_STEPS,),
                                               in_specs=in_specs, out_specs=out_specs),
        out_shape=out_shape, compiler_params=_params("parallel"))(k_arr, *args)


def adamw(ws, gs, ms, vs, *, name):
    n = len(ws)
    specs, g_specs, g_count = [], [], []
    for w, g_list in zip(ws, gs):
        blk, index = _row_blocks(w.shape)
        specs.append(pl.BlockSpec(blk, index))
        layers = len(g_list)
        per = N_STEPS // layers
        g_count.append(layers)
        for l in range(layers):
            g_specs.append(pl.BlockSpec(blk[-2:], lambda s, l=l, per=per: (jnp.where(s // per == l, s % per, 0), 0)))
    ng = len(g_specs)

    def body(*refs):
        s = pl.program_id(0)
        g_refs = refs[3 * n:3 * n + ng]
        outs = refs[3 * n + ng:]
        off = 0
        for i in range(n):
            w_ref, m_ref, v_ref = refs[i], refs[n + i], refs[2 * n + i]
            go_ref, d_ref, nm_ref, nv_ref = (outs[k * n + i] for k in range(4))
            layers = g_count[i]
            g = g_refs[off][...]
            for l in range(1, layers):
                g = jnp.where(s // (N_STEPS // layers) == l, g_refs[off + l][...], g)
            off += layers
            g = g.reshape(w_ref.shape)
            m = ADAM_B1 * m_ref[...] + (1.0 - ADAM_B1) * g
            v = ADAM_B2 * v_ref[...] + (1.0 - ADAM_B2) * (g * g)
            m_hat = m / ADAM_C1
            v_hat = v / ADAM_C2
            go_ref[...] = g
            d_ref[...] = -ADAM_LR * (m_hat / (jnp.sqrt(v_hat) + ADAM_EPS) + ADAM_WD * w_ref[...])
            nm_ref[...] = m
            nv_ref[...] = v

    outs = pl.pallas_call(
        body, name=name, grid=(N_STEPS,), in_specs=specs * 3 + g_specs, out_specs=specs * 4,
        out_shape=[jax.ShapeDtypeStruct(a.shape, f32) for a in ws] * 4,
        compiler_params=_params("parallel"))(*ws, *ms, *vs, *[g for g_list in gs for g in g_list])
    return [outs[k * n:(k + 1) * n] for k in range(4)]


def _adamw_update(w, g, m, v):
    m = ADAM_B1 * m + (1.0 - ADAM_B1) * g
    v = ADAM_B2 * v + (1.0 - ADAM_B2) * (g * g)
    m_hat = m / ADAM_C1
    v_hat = v / ADAM_C2
    return -ADAM_LR * (m_hat / (jnp.sqrt(v_hat) + ADAM_EPS) + ADAM_WD * w), m, v


def adamw_small(ws, gs, ms, vs, *, name):
    n = len(ws)

    def body(*refs):
        for i in range(n):
            w_ref, g_ref, m_ref, v_ref = (refs[k * n + i] for k in range(4))
            d_ref, nm_ref, nv_ref = (refs[(4 + k) * n + i] for k in range(3))
            d_ref[...], nm_ref[...], nv_ref[...] = _adamw_update(w_ref[...], g_ref[...], m_ref[...], v_ref[...])

    outs = pl.pallas_call(
        body, name=name, out_shape=[jax.ShapeDtypeStruct(a.shape, f32) for a in ws] * 3)(*ws, *gs, *ms, *vs)
    return outs[:n], outs[n:2 * n], outs[2 * n:]


def _place():
    return lax.axis_index("x"), lax.axis_index("y"), lax.axis_index("c")


def gather_shards(bufs, *, name, split):
    n = len(bufs)

    def body(*refs):
        bufs_ = refs[:n]
        isend, irecv, dsend, drecv = refs[2 * n:]
        x, y, c = _place()
        k = 2 * x + y
        peers = [(1 - x, y, c), (x, 1 - y, c), (1 - x, 1 - y, c)]
        peer_k = [2 * (1 - x) + y, 2 * x + (1 - y), 2 * (1 - x) + (1 - y)]

        def slab(a, q, h):
            if not split[a]:
                return bufs_[a].at[q]
            half = bufs_[a].shape[1] // 2
            return bufs_[a].at[q, pl.ds(pl.multiple_of(h * half, 16), half)]

        def ici(a, j, q):
            return pltpu.make_async_remote_copy(
                src_ref=slab(a, q, c), dst_ref=slab(a, q, c), send_sem=isend.at[3 * a + j], recv_sem=irecv.at[3 * a + j],
                device_id=peers[j], device_id_type=MESH)

        def d2d(a, j, h):
            return pltpu.make_async_remote_copy(
                src_ref=slab(a, peer_k[j], h), dst_ref=slab(a, peer_k[j], h), send_sem=dsend.at[3 * a + j],
                recv_sem=drecv.at[3 * a + j], device_id=(x, y, 1 - c), device_id_type=MESH)

        for a in range(n):
            for j in range(3):
                ici(a, j, k).start()
        for a in range(n):
            for j in range(3):
                ici(a, j, peer_k[j]).wait_recv()
                if split[a]:
                    d2d(a, j, c).start()
        for a in range(n):
            for j in range(3):
                if split[a]:
                    d2d(a, j, 1 - c).wait_recv()
        for a in range(n):
            for j in range(3):
                ici(a, j, k).wait_send()
                if split[a]:
                    d2d(a, j, c).wait_send()

    return pl.pallas_call(
        body, name=name, in_specs=[ANY] * n, out_specs=[ANY] * n,
        out_shape=[jax.ShapeDtypeStruct(b.shape, b.dtype) for b in bufs],
        input_output_aliases={i: i for i in range(n)},
        scratch_shapes=[pltpu.SemaphoreType.DMA((3 * n,))] * 4)(*bufs)


HBM = pl.BlockSpec(memory_space=pltpu.HBM)
SEM = pl.BlockSpec(memory_space=pltpu.SEMAPHORE)
DATAFLOW = pltpu.SideEffectType.DATAFLOW_SIDE_EFFECTING


def _chip_peers():
    x, y, c = _place()
    return 2 * x + y, [(1 - x, y, c), (x, 1 - y, c), (1 - x, 1 - y, c)], [2 * (1 - x) + y, 2 * x + (1 - y), 2 * (1 - x) + (1 - y)]


def _half_slab(ref, q, h):
    half = ref.shape[1] // 2
    return ref.at[q, pl.ds(pl.multiple_of(h * half, BF16_ROWS), half)]


def gather_start(bufs, groups, after, *, name):
    n = len(bufs)
    ng = len(groups)

    def body(*refs):
        ins = refs[:n]
        sends, recvs = refs[2 * n + 1:2 * n + 1 + ng], refs[2 * n + 1 + ng:2 * n + 1 + 2 * ng]
        token = refs[-1]
        c = lax.axis_index("c")
        k, peers, _ = _chip_peers()
        for gi, grp in enumerate(groups):
            for pos, a in enumerate(grp):
                for j in range(3):
                    pltpu.make_async_remote_copy(
                        src_ref=_half_slab(ins[a], k, c), dst_ref=_half_slab(ins[a], k, c), send_sem=sends[gi].at[3 * pos + j],
                        recv_sem=recvs[gi].at[3 * pos + j], device_id=peers[j], device_id_type=MESH).start()
        token[...] = jnp.zeros_like(token)

    sems = [pltpu.SemaphoreType.DMA((3 * len(grp),)) for grp in groups]
    outs = pl.pallas_call(
        body, name=name, in_specs=[HBM] * n + [ANY],
        out_specs=[HBM] * n + [SEM] * (2 * ng) + [pl.BlockSpec(memory_space=pltpu.VMEM)],
        out_shape=[pltpu.HBM(b.shape, b.dtype) for b in bufs] + sems + sems + [jax.ShapeDtypeStruct((8, LANES), f32)],
        input_output_aliases={i: i for i in range(n)},
        compiler_params=pltpu.CompilerParams(has_side_effects=DATAFLOW))(
            *[pltpu.with_memory_space_constraint(b, pltpu.HBM) for b in bufs], after)
    return outs[:n], outs[n:n + ng], outs[n + ng:n + 2 * ng], outs[-1]


def gather_wait(bufs, send_sems, recv_sems, after, *, name):
    n = len(bufs)

    def body(*refs):
        ins = refs[:n]
        send, recv = refs[n], refs[n + 1]
        c = lax.axis_index("c")
        k, peers, peer_k = _chip_peers()
        for a in range(n):
            for j in range(3):
                copy = pltpu.make_async_remote_copy(
                    src_ref=_half_slab(ins[a], k, c), dst_ref=_half_slab(ins[a], peer_k[j], c), send_sem=send.at[3 * a + j],
                    recv_sem=recv.at[3 * a + j], device_id=peers[j], device_id_type=MESH)
                copy.wait_send()
                copy.wait_recv()

    return pl.pallas_call(
        body, name=name, in_specs=[HBM] * n + [SEM, SEM, ANY], out_specs=[HBM] * n,
        out_shape=[pltpu.HBM(b.shape, b.dtype) for b in bufs],
        input_output_aliases={i: i for i in range(n)},
        compiler_params=pltpu.CompilerParams(has_side_effects=DATAFLOW))(*bufs, send_sems, recv_sems, after)


def forward_halves(bufs, *, name):
    n = len(bufs)

    def body(*refs):
        bufs_ = refs[:n]
        send, recv = refs[2 * n:]
        x, y, c = _place()
        _, _, peer_k = _chip_peers()

        def copy(a, j, h):
            return pltpu.make_async_remote_copy(
                src_ref=_half_slab(bufs_[a], peer_k[j], h), dst_ref=_half_slab(bufs_[a], peer_k[j], h),
                send_sem=send.at[3 * a + j], recv_sem=recv.at[3 * a + j], device_id=(x, y, 1 - c), device_id_type=MESH)

        for a in range(n):
            for j in range(3):
                copy(a, j, c).start()
        for a in range(n):
            for j in range(3):
                copy(a, j, 1 - c).wait_recv()
        for a in range(n):
            for j in range(3):
                copy(a, j, c).wait_send()

    return pl.pallas_call(
        body, name=name, in_specs=[ANY] * n, out_specs=[ANY] * n,
        out_shape=[jax.ShapeDtypeStruct(b.shape, b.dtype) for b in bufs],
        input_output_aliases={i: i for i in range(n)},
        scratch_shapes=[pltpu.SemaphoreType.DMA((3 * n,))] * 2)(*bufs)


def _forward_copies(bufs_, send, recv, h):
    x, y, c = _place()
    _, _, peer_k = _chip_peers()
    return [pltpu.make_async_remote_copy(
        src_ref=_half_slab(bufs_[a], peer_k[j], h), dst_ref=_half_slab(bufs_[a], peer_k[j], h),
        send_sem=send.at[3 * a + j], recv_sem=recv.at[3 * a + j], device_id=(x, y, 1 - c), device_id_type=MESH)
        for a in range(len(bufs_)) for j in range(3)]


def forward_start(bufs, after, *, name):
    n = len(bufs)

    def body(*refs):
        for cp in _forward_copies(refs[:n], refs[2 * n + 1], refs[2 * n + 2], lax.axis_index("c")):
            cp.start()
        refs[-1][...] = jnp.zeros_like(refs[-1])

    sems = [pltpu.SemaphoreType.DMA((3 * n,))] * 2
    outs = pl.pallas_call(
        body, name=name, in_specs=[HBM] * n + [ANY],
        out_specs=[HBM] * n + [SEM] * 2 + [pl.BlockSpec(memory_space=pltpu.VMEM)],
        out_shape=[pltpu.HBM(b.shape, b.dtype) for b in bufs] + sems + [jax.ShapeDtypeStruct((8, LANES), f32)],
        input_output_aliases={i: i for i in range(n)},
        compiler_params=pltpu.CompilerParams(has_side_effects=DATAFLOW))(*bufs, after)
    return (n, outs[:-1]), outs[-1]


def forward_wait(state, after, *, name):
    n, held = state

    def body(*refs):
        c = lax.axis_index("c")
        for mine, theirs in zip(_forward_copies(refs[:n], refs[n], refs[n + 1], c),
                                _forward_copies(refs[:n], refs[n], refs[n + 1], 1 - c)):
            mine.wait_send()
            theirs.wait_recv()

    return pl.pallas_call(
        body, name=name, in_specs=[HBM] * n + [SEM] * 2 + [ANY], out_specs=[HBM] * n,
        out_shape=[pltpu.HBM(b.shape, b.dtype) for b in held[:n]],
        input_output_aliases={i: i for i in range(n)},
        compiler_params=pltpu.CompilerParams(has_side_effects=DATAFLOW))(*held, after)


def _sibling_copies(srcs, lands, send, recv):
    x, y, c = _place()
    return [pltpu.make_async_remote_copy(src_ref=srcs[a], dst_ref=lands[a], send_sem=send.at[a], recv_sem=recv.at[a],
                                         device_id=(x, y, 1 - c), device_id_type=MESH) for a in range(len(srcs))]


def sibling_start(arrays, after, *, name):
    n = len(arrays)
    lands = [pltpu.with_memory_space_constraint(lax.empty(a.shape, a.dtype), pltpu.HBM) for a in arrays]

    def body(*refs):
        for cp in _sibling_copies(refs[:n], refs[n:2 * n], refs[4 * n + 1], refs[4 * n + 2]):
            cp.start()
        refs[-1][...] = jnp.zeros_like(refs[-1])

    bufs = list(arrays) + lands
    sems = [pltpu.SemaphoreType.DMA((n,))] * 2
    outs = pl.pallas_call(
        body, name=name, in_specs=[HBM] * (2 * n) + [ANY],
        out_specs=[HBM] * (2 * n) + [SEM] * 2 + [pl.BlockSpec(memory_space=pltpu.VMEM)],
        out_shape=[pltpu.HBM(b.shape, b.dtype) for b in bufs] + sems + [jax.ShapeDtypeStruct((8, LANES), f32)],
        input_output_aliases={i: i for i in range(2 * n)},
        compiler_params=pltpu.CompilerParams(has_side_effects=DATAFLOW))(
            *[pltpu.with_memory_space_constraint(b, pltpu.HBM) for b in bufs], after)
    return (n, outs[:-1]), outs[-1]


def sibling_wait(state, after, *, name):
    n, held = state

    def body(*refs):
        for cp in _sibling_copies(refs[:n], refs[n:2 * n], refs[2 * n], refs[2 * n + 1]):
            cp.wait_send()
            cp.wait_recv()

    outs = pl.pallas_call(
        body, name=name, in_specs=[HBM] * (2 * n) + [SEM] * 2 + [ANY], out_specs=[HBM] * (2 * n),
        out_shape=[pltpu.HBM(b.shape, b.dtype) for b in held[:2 * n]],
        input_output_aliases={i: i for i in range(2 * n)},
        compiler_params=pltpu.CompilerParams(has_side_effects=DATAFLOW))(*held, after)
    return outs[n:]


ALL_MASKS = [(mx, my, mc) for mx in (0, 1) for my in (0, 1) for mc in (0, 1)][1:]


def _scatter_copies(srcs, lands, ev, send, recv, esend, erecv):
    x, y, c = _place()
    me = 4 * x + 2 * y + c
    k, peers, peer_k = _chip_peers()
    out = []
    for a in range(len(srcs)):
        for j in range(3):
            out.append(pltpu.make_async_remote_copy(
                src_ref=srcs[a].at[peer_k[j]], dst_ref=lands[a].at[j], send_sem=send.at[3 * a + j],
                recv_sem=recv.at[3 * a + j], device_id=peers[j], device_id_type=MESH))
    start_ev, wait_ev = [], []
    if ev is not None:
        for j, (mx, my, mc) in enumerate(ALL_MASKS):
            peer = (x ^ mx, y ^ my, c ^ mc)
            start_ev.append(pltpu.make_async_remote_copy(
                src_ref=ev.at[me], dst_ref=ev.at[me], send_sem=esend.at[j], recv_sem=erecv.at[j],
                device_id=peer, device_id_type=MESH))
            wait_ev.append(pltpu.make_async_remote_copy(
                src_ref=ev.at[me], dst_ref=ev.at[me ^ (4 * mx + 2 * my + mc)], send_sem=esend.at[j],
                recv_sem=erecv.at[j], device_id=peer, device_id_type=MESH))
    return out, start_ev, wait_ev


def chip_scatter_start(arrays, everyone, after, *, name):
    n = len(arrays)
    ne = 0 if everyone is None else 1
    lands = [pltpu.with_memory_space_constraint(lax.empty((3,) + a.shape[1:], a.dtype), pltpu.HBM) for a in arrays]

    def body(*refs):
        srcs, lands_ = refs[:n], refs[n:2 * n]
        ev = refs[2 * n] if ne else None
        sems = refs[2 * n + ne + 1 + 2 * n + ne:-1]
        send, recv = sems[0], sems[1]
        esend, erecv = (sems[2], sems[3]) if ne else (None, None)
        copies, start_ev, _ = _scatter_copies(srcs, lands_, ev, send, recv, esend, erecv)
        for cp in start_ev + copies:
            cp.start()
        refs[-1][...] = jnp.zeros_like(refs[-1])

    sem_shapes = [pltpu.SemaphoreType.DMA((3 * n,))] * 2 + [pltpu.SemaphoreType.DMA((7,))] * (2 * ne)
    bufs = list(arrays) + lands + ([everyone] if ne else [])
    outs = pl.pallas_call(
        body, name=name, in_specs=[HBM] * len(bufs) + [ANY],
        out_specs=[HBM] * len(bufs) + [SEM] * len(sem_shapes) + [pl.BlockSpec(memory_space=pltpu.VMEM)],
        out_shape=[pltpu.HBM(b.shape, b.dtype) for b in bufs] + sem_shapes + [jax.ShapeDtypeStruct((8, LANES), f32)],
        input_output_aliases={i: i for i in range(len(bufs))},
        compiler_params=pltpu.CompilerParams(has_side_effects=DATAFLOW))(
            *[pltpu.with_memory_space_constraint(b, pltpu.HBM) for b in bufs], after)
    return (n, ne, outs[:-1]), outs[-1]


def chip_scatter_wait(state, after, *, name):
    n, ne, held = state
    nb = 2 * n + ne
    bufs, sems = held[:nb], held[nb:]

    def body(*refs):
        srcs, lands_ = refs[:n], refs[n:2 * n]
        ev = refs[2 * n] if ne else None
        sems_ = refs[nb:nb + len(sems)]
        esend, erecv = (sems_[2], sems_[3]) if ne else (None, None)
        copies, _, wait_ev = _scatter_copies(srcs, lands_, ev, sems_[0], sems_[1], esend, erecv)
        for cp in wait_ev + copies:
            cp.wait_send()
            cp.wait_recv()

    outs = pl.pallas_call(
        body, name=name, in_specs=[HBM] * nb + [SEM] * len(sems) + [ANY], out_specs=[HBM] * nb,
        out_shape=[pltpu.HBM(b.shape, b.dtype) for b in bufs],
        input_output_aliases={i: i for i in range(nb)},
        compiler_params=pltpu.CompilerParams(has_side_effects=DATAFLOW))(*bufs, *sems, after)
    return outs[n:2 * n], (outs[2 * n] if ne else None)


def sibling_merge(bufs, *, name):
    n = len(bufs)

    def body(*refs):
        bufs_ = refs[:n]
        send, recv = refs[2 * n:]
        x, y, c = _place()

        def copy(u, h):
            return pltpu.make_async_remote_copy(
                src_ref=bufs_[u].at[h], dst_ref=bufs_[u].at[h], send_sem=send.at[u], recv_sem=recv.at[u],
                device_id=(x, y, 1 - c), device_id_type=MESH)

        for u in range(n):
            copy(u, c).start()
        for u in range(n):
            copy(u, 1 - c).wait_recv()
        for u in range(n):
            copy(u, c).wait_send()

    return pl.pallas_call(
        body, name=name, in_specs=[ANY] * n, out_specs=[ANY] * n,
        out_shape=[jax.ShapeDtypeStruct(b.shape, b.dtype) for b in bufs],
        input_output_aliases={i: i for i in range(n)},
        scratch_shapes=[pltpu.SemaphoreType.DMA((n,)), pltpu.SemaphoreType.DMA((n,))])(*bufs)


def sum_leading(a, *, name):
    n, r, c = a.shape

    def body(a_ref, o_ref):
        acc = a_ref[0]
        for i in range(1, n):
            acc = acc + a_ref[i]
        o_ref[...] = acc

    rb = r // 2 if r % 16 == 0 else r
    return pl.pallas_call(
        body, name=name, grid=(r // rb,), in_specs=[pl.BlockSpec((n, rb, c), lambda i: (0, i, 0))],
        out_specs=pl.BlockSpec((rb, c), lambda i: (i, 0)), out_shape=jax.ShapeDtypeStruct((r, c), f32),
        compiler_params=_params("parallel"))(a)


def _half_rows(shape):
    return shape[1] // 2 // 2


def rs_add_sibling(grads, recvd, ck_arr, *, name):
    n = len(grads)

    def body(ck_ref, *refs):
        s = pl.program_id(1)
        for u in range(n):
            g_ref, r_ref = refs[u], refs[n + u]
            qb_ref, own_ref = refs[2 * n + u], refs[3 * n + u]
            q = g_ref[0] + r_ref[0].astype(f32)
            qb_ref[0] = q.astype(bf16)

            @pl.when(s == ck_ref[1])
            def _(own_ref=own_ref, q=q):
                own_ref[...] = q

    in_specs = [pl.BlockSpec((1, _half_rows(g.shape), g.shape[2]), lambda r, s, ck: (s, ck[0] * 2 + r, 0)) for g in grads]
    in_specs += [pl.BlockSpec((1, _half_rows(g.shape), g.shape[2]), lambda r, s, ck: (s, r, 0)) for g in grads]
    out_specs = [pl.BlockSpec((1, _half_rows(g.shape), g.shape[2]), lambda r, s, ck: (s, r, 0)) for g in grads]
    out_specs += [pl.BlockSpec((_half_rows(g.shape), g.shape[2]), lambda r, s, ck: (r, 0)) for g in grads]
    outs = pl.pallas_call(
        body, name=name,
        grid_spec=pltpu.PrefetchScalarGridSpec(num_scalar_prefetch=1, grid=(2, N_SHARDS),
                                               in_specs=in_specs, out_specs=out_specs),
        out_shape=[jax.ShapeDtypeStruct((N_SHARDS, g.shape[1] // 2, g.shape[2]), bf16) for g in grads]
        + [jax.ShapeDtypeStruct((g.shape[1] // 2, g.shape[2]), f32) for g in grads],
        compiler_params=_params("parallel", "arbitrary"))(ck_arr, *grads, *recvd)
    return outs[:n], outs[n:]


def rs_sum_chips(owns, recvd, ck_arr, *, name):
    n = len(owns)

    def body(ck_ref, *refs):
        for u in range(n):
            own_ref, r_ref, o_ref = refs[u], refs[n + u], refs[2 * n + u]
            o_ref[0] = ((own_ref[...] + r_ref[0].astype(f32)) + r_ref[1].astype(f32)) + r_ref[2].astype(f32)

    in_specs = [pl.BlockSpec((o.shape[0] // 2, o.shape[1]), lambda r, ck: (r, 0)) for o in owns]
    in_specs += [pl.BlockSpec((3, o.shape[0] // 2, o.shape[1]), lambda r, ck: (0, r, 0)) for o in owns]
    out_specs = [pl.BlockSpec((1, o.shape[0] // 2, o.shape[1]), lambda r, ck: (ck[0], r, 0)) for o in owns]
    return pl.pallas_call(
        body, name=name,
        grid_spec=pltpu.PrefetchScalarGridSpec(num_scalar_prefetch=1, grid=(2,), in_specs=in_specs, out_specs=out_specs),
        out_shape=[jax.ShapeDtypeStruct((2,) + o.shape, f32) for o in owns],
        compiler_params=_params("parallel"))(ck_arr, *owns, *recvd)


SMALL = ("a_norm", "a_v_norm", "a_w_s", "a_b_s", "f_norm", "f_conv_w", "f_conv_b", "kv_norm", "k_norm",
         "b_norm", "b_q_norm", "b_sinks")
BIG = ("a_w_in", "a_w_out", "f_w_in", "f_w_out", "w_kv", "b_w_q", "b_w_o")
PACK_COLS = 1024
PACK_ROWS = 8 * N_STEPS


def _pack(parts, rows=PACK_ROWS):
    flat = jnp.concatenate([p.reshape(-1).astype(f32) for p in parts])
    pad = (-flat.shape[0]) % (rows * PACK_COLS)
    return jnp.pad(flat, (0, pad)).reshape(-1, PACK_COLS)


def _unpack(packed, shapes):
    flat = packed.reshape(-1)
    out, off = [], 0
    for s in shapes:
        size = math.prod(s)
        out.append(flat[off:off + size].reshape(s))
        off += size
    return out


def _behind(value, token):
    return lax.optimization_barrier((value, token))[0]


def _ffn_fwd(x, g, h, r, w_in4, conv_w, conv_b, f, tag):
    wg, wu = conv_w[:, :f], conv_w[:, f:]
    bg, bu = conv_b[None, :f], conv_b[None, f:]
    pg, pu, gate, up, a = ffn_in_fused(h, w_in4, wg, wu, bg, bu, name=f"ffn{tag}_in")
    return a, (x, g, h, r, pg, pu, gate, up, a, wg, wu)


def _ffn_bwd(dy, saved, w_in4, w_out, c_arr, tag, exchange=False):
    x, g, h, r, pg, pu, gate, up, a, wg, wu = saved
    f = w_out.shape[0]
    d_w_out = mm_tn(a, [dy], c_arr, name=f"ffn{tag}_dwout", n_s=w_out.shape[1], shard_rows=f // N_SHARDS, tki=f // 2)
    dpg, dpu, sg, su = ffn_gate_bwd(dy, w_out, pg, pu, gate, up, wg, wu, name=f"ffn{tag}_dgate")
    d_w_in = mm_tn(h, [dpg, dpu], c_arr, name=f"ffn{tag}_dwin", n_s=w_in4.shape[2], shard_rows=h.shape[1])
    state = None
    if exchange:
        state, token = sibling_start([d_w_in[1], d_w_out[1]], d_w_in[0], name=f"rs_sibling_start_ffn{tag}")
        g = _behind(g, token)
    dx, dg = mm_nt_rms_bwd([dpg, dpu], w_in4, x, r, g, dy, name=f"ffn{tag}_dh")
    d_conv_w = jnp.concatenate([sg[0:3], su[0:3]], axis=1)
    d_conv_b = jnp.concatenate([sg[3], su[3]], axis=0)
    return dx, dg, d_w_in, d_conv_w, d_conv_b, d_w_out, state


def _rs_front(pairs, sibling_state, after, c_arr, tag):
    units = [full.reshape(N_SHARDS, -1, full.shape[-1]) for full, _ in pairs]
    from_sib = sibling_wait(sibling_state, after, name=f"rs_sibling_wait{tag}")
    return rs_add_sibling(units, from_sib, c_arr, name=f"rs_add{tag}")


def _rs_back(own, from_chips, c_arr, tag):
    halves = rs_sum_chips(list(own), list(from_chips), c_arr, name=f"rs_sum{tag}")
    return [m.reshape(-1, m.shape[2]) for m in sibling_merge(list(halves), name=f"rs_merge{tag}")]


def kernel(x, a_norm, a_w_in, a_v_norm, a_w_s, a_b_s, a_w_out, f_norm, f_w_in, f_conv_w, f_conv_b, f_w_out, kv_norm, w_kv, k_norm, b_norm, b_w_q, b_q_norm, b_sinks, b_w_o, loss_target, m_a_norm, m_a_w_in, m_a_v_norm, m_a_w_s, m_a_b_s, m_a_w_out, m_f_norm, m_f_w_in, m_f_conv_w, m_f_conv_b, m_f_w_out, m_kv_norm, m_w_kv, m_k_norm, m_b_norm, m_b_w_q, m_b_q_norm, m_b_sinks, m_b_w_o, v_a_norm, v_a_w_in, v_a_v_norm, v_a_w_s, v_a_b_s, v_a_w_out, v_f_norm, v_f_w_in, v_f_conv_w, v_f_conv_b, v_f_w_out, v_kv_norm, v_w_kv, v_k_norm, v_b_norm, v_b_w_q, v_b_q_norm, v_b_sinks, v_b_w_o):
    args = dict(locals())
    weights = {n: args[n] for n in SMALL + BIG}
    moms = {n: args["m_" + n] for n in SMALL + BIG}
    vars_ = {n: args["v_" + n] for n in SMALL + BIG}
    t, d = x.shape[1], x.shape[2]
    xi, yi, ci = _place()
    chip = 2 * xi + yi

    big_local = [a_w_in[0], a_w_out[0], f_w_in, f_w_out, w_kv, b_w_q[0], b_w_o[0]]
    c_arr = jnp.stack([ci, chip]).astype(jnp.int32)
    k_arr = jnp.stack([chip]).astype(jnp.int32)
    b_ain, b_aout, b_fin0, b_fin1, b_fout0, b_fout1, b_kv, b_q, b_o = cast_into_slot(big_local, k_arr, name="cast_weights")
    small_cols = _pack([a_norm, a_v_norm, f_conv_w], rows=8)
    b_small = lax.dynamic_update_slice(jnp.zeros((N_SHARDS,) + small_cols.shape, f32), small_cols[None], (chip, 0, 0))
    g_small, w_a_in, g_a_w_out = gather_shards([b_small, b_ain, b_aout], name="gather_first", split=[False, True, True])
    later, send_sems, recv_sems, token = gather_start([b_fin0, b_fout0, b_kv, b_q, b_o, b_fin1, b_fout1],
                                                      [[0], [1], [2, 3, 4], [5, 6]], g_small, name="gather_start")
    ns_cols = a_norm.shape[1]
    nf_cols = f_conv_w.shape[2]
    parts = [_unpack(g_small[k], [a_norm.shape, a_v_norm.shape, f_conv_w.shape]) for k in range(N_SHARDS)]
    a_norm_f = jnp.concatenate([p[0] for p in parts], axis=1) + token[0, 0]
    a_v_norm_f = jnp.concatenate([p[1] for p in parts], axis=1)
    conv_w_f = jnp.concatenate([p[2] for p in parts], axis=2)

    x0 = x[0]
    tril = jnp.tril(jnp.ones((CHUNK, CHUNK), dtype=bool))
    wc = jnp.where(tril[None], a_w_s[0], 0.0).astype(bf16)
    bt = a_b_s[0].T
    kg2 = jnp.tile(k_norm, 2)[None]
    qg2 = jnp.tile(b_q_norm[0], 2)[None]

    (h_a,), r_a = rms_fwd(x0, [a_norm_f], name="a_norm")
    zuv = mm_nn(h_a, w_a_in, name="a_in")
    y_a = sgu_gate_fwd(zuv, a_v_norm_f, wc, bt, name="a_gate")
    w_a_out = g_a_w_out.reshape(1, -1, d)
    f = f_w_out.shape[1] * N_SHARDS
    fwd0, tok0 = forward_start(gather_wait(later[0:1], send_sems[0], recv_sems[0], y_a, name="gather_wait_0"), y_a,
                               name="gather_forward_start_0")
    x1, (h_f0,), r_f0 = mm_residual(y_a, w_a_out[0], x0, name="a_out", gains=[_behind(f_norm[0:1], tok0)])
    (g_fin0,) = forward_wait(fwd0, x1, name="gather_forward_wait_0")
    w_f_in = [g_fin0, None]
    fwd1a, tok1a = forward_start(gather_wait(later[1:2], send_sems[1], recv_sems[1], x1, name="gather_wait_1"), x1,
                                 name="gather_forward_start_1")
    a0, ffn0 = _ffn_fwd(x1, f_norm[0:1], h_f0, r_f0, w_f_in[0], conv_w_f[0], _behind(f_conv_b[0], tok1a), f, "0")
    (g_fout0,) = forward_wait(fwd1a, a0, name="gather_forward_wait_1")
    w_f_out = [g_fout0.reshape(-1, d), None]
    fwd1, tok1 = forward_start(gather_wait(later[2:5], send_sems[2], recv_sems[2], g_fout0, name="gather_wait_1b"), a0,
                               name="gather_forward_start_1b")
    x2, (h_k, h_q), r_b = mm_residual(a0, w_f_out[0], x1, name="ffn0_out", gains=[_behind(kv_norm[None], tok1), b_norm])
    g_w_kv, g_b_w_q, g_b_w_o = forward_wait(fwd1, x2, name="gather_forward_wait_1b")
    w_kv_f = g_w_kv.reshape(1, d, -1)
    w_q_f = g_b_w_q.reshape(1, d, -1)
    w_o_f = g_b_w_o.reshape(1, -1, d)
    kv, k2, v2 = kv_proj_post(h_k, w_kv_f[0], kg2, name="kv_proj")
    qp, qn = q_proj_norm(h_q, w_q_f[0], qg2, name="q_proj", scale=HEAD_DIM ** -0.5)
    fwd2, tok2 = forward_start(gather_wait(later[5:7], send_sems[3], recv_sems[3], qn, name="gather_wait_2"), qn,
                               name="gather_forward_start_2")
    o = attn_fwd(qn, k2, v2, _behind(b_sinks[0], tok2), name="attn")
    x3, (h_f1,), r_f1 = mm_residual(o, w_o_f[0], x2, name="o_proj", gains=[f_norm[1:2]])
    g_fin1, g_fout1 = forward_wait(fwd2, x3, name="gather_forward_wait_2")
    w_f_in[1] = g_fin1
    w_f_out[1] = g_fout1.reshape(-1, d)
    a1, ffn1 = _ffn_fwd(x3, f_norm[1:2], h_f1, r_f1, w_f_in[1], conv_w_f[1], f_conv_b[1], f, "1")
    dx4, sq = mm_residual(a1, w_f_out[1], x3, name="ffn1_out", target=loss_target[0])
    loss_part = (0.5 * jnp.sum(sq) / d).reshape(1)

    proj_rows = d // N_SHARDS
    dx3, d_fn1, d_fwin1, d_cw1, d_cb1, d_fwout1, _ = _ffn_bwd(dx4, ffn1, w_f_in[1], w_f_out[1], c_arr, "1")
    do = mm_nt([dx3], w_o_f, name="o_proj_dx")
    d_w_o = mm_tn(o, [dx3], c_arr, name="o_proj_dw", n_s=d, shard_rows=o.shape[1] // N_SHARDS)
    dqn, dk2, dv2, dsink = attn_bwd(qn, k2, v2, do, b_sinks[0], name="attn_bwd")
    dqp, dqg = q_norm_bwd(dqn, qp, qg2, name="q_norm_bwd", scale=HEAD_DIM ** -0.5)
    dkv, dkg = kv_post_bwd(dk2, dv2, kv, kg2, name="kv_post_bwd")
    d_w_q = mm_tn(h_q, [dqp], c_arr, name="q_proj_dw", n_s=w_q_f.shape[2], shard_rows=proj_rows)
    d_w_kv = mm_tn(h_k, [dkv], c_arr, name="kv_proj_dw", n_s=w_kv_f.shape[2], shard_rows=proj_rows)
    group1 = [d_fwin1, d_fwout1, d_w_kv, d_w_q, d_w_o]
    sib1, token_s1 = sibling_start([half for _, half in group1], d_w_kv[0], name="rs_sibling_start1")
    dh_k = mm_nt([dkv], w_kv_f, name="kv_proj_dx")
    dx2, d_bn, d_kvn = mm_nt_rms_bwd([dqp], w_q_f, x2, r_b, _behind(b_norm, token_s1), dx3, name="q_proj_dx",
                                     extra=(dh_k, kv_norm[None]))
    chip_bf1, own1 = _rs_front(group1, sib1, dx2, c_arr, "1")
    scatter1, token1 = chip_scatter_start(list(chip_bf1), None, dx2, name="rs_chips_start1")
    ffn0 = ffn0[:9] + (_behind(ffn0[9], token1),) + ffn0[10:]
    dx1, d_fn0, d_fwin0, d_cw0, d_cb0, d_fwout0, sib2 = _ffn_bwd(dx2, ffn0, w_f_in[0], w_f_out[0], c_arr, "0", exchange=True)
    chip_bf2, own2 = _rs_front([d_fwin0, d_fwout0], sib2, dx1, c_arr, "2")
    scatter2, token2 = chip_scatter_start(list(chip_bf2), None, dx1, name="rs_chips_start2")
    a_v_norm_f = _behind(a_v_norm_f, token2)
    dy_a = mm_nt([dx1], w_a_out, name="a_out_dx")
    d_w_aout = mm_tn(y_a, [dx1], c_arr, name="a_out_dw", n_s=d, shard_rows=y_a.shape[1] // N_SHARDS)
    dzu, dzv, d_avn, d_ws, d_bt = sgu_gate_bwd(zuv, dy_a, a_v_norm_f, wc, bt, name="a_gate_bwd")
    d_w_ain = mm_tn(h_a, [dzu, dzv], c_arr, name="a_in_dw", n_s=w_a_in.shape[2], shard_rows=d)
    sib3, token_s3 = sibling_start([d_w_ain[1], d_w_aout[1]], d_w_ain[0], name="rs_sibling_start3")
    dx0, d_an = mm_nt_rms_bwd([dzu, dzv], w_a_in, x0, r_a, _behind(a_norm_f, token_s3), dx1, name="a_in_dx")
    grad_x = dx0[None]

    chip_bf3, own3 = _rs_front([d_w_ain, d_w_aout], sib3, dx0, c_arr, "3")
    d_fn = jnp.concatenate([d_fn0, d_fn1], axis=0)
    d_cw = jnp.stack([d_cw0, d_cw1])
    d_cb = jnp.stack([d_cb0, d_cb1])
    d_kg = (dkg[0, :HEAD_DIM] + dkg[0, HEAD_DIM:])
    d_qg = (dqg[0, :HEAD_DIM] + dqg[0, HEAD_DIM:])[None]
    small_full = [d_an, d_avn, d_ws[None], d_bt.T[None], d_fn, d_cw, d_cb, d_kvn[0], d_kg, d_bn, d_qg,
                  dsink[:, :N_Q_HEADS], loss_part]
    packed = _pack(small_full)
    me = 4 * xi + 2 * yi + ci
    everyone = lax.dynamic_update_slice(lax.empty((N_DEV,) + packed.shape, f32), packed[None], (me, 0, 0))
    scatter3, token3 = chip_scatter_start(list(chip_bf3), everyone, own3[0], name="rs_chips_start3")
    from_chips1, _ = chip_scatter_wait(scatter1, token3, name="rs_chips_wait1")
    from_chips2, _ = chip_scatter_wait(scatter2, from_chips1[0], name="rs_chips_wait2")
    fin1, fout1, gkv, gq, go, fin0, fout0 = _rs_back(list(own1) + list(own2), list(from_chips1) + list(from_chips2),
                                                     c_arr, "12")
    late = ("f_w_in", "f_w_out", "w_kv", "b_w_q", "b_w_o")
    res_late = adamw([weights[n] for n in late], [[fin0, fin1], [fout0, fout1], [gkv], [gq], [go]],
                     [moms[n] for n in late], [vars_[n] for n in late], name="adamw_late")
    from_chips3, from_all = chip_scatter_wait(scatter3, res_late[1][2], name="rs_chips_wait3")
    ain, aout = _rs_back(own3, from_chips3, c_arr, "3")
    first = ("a_w_in", "a_w_out")
    res_first = adamw([weights[n] for n in first], [[ain], [aout]], [moms[n] for n in first],
                      [vars_[n] for n in first], name="adamw_first")
    big = {n: tuple(r[i] for r in res_late) for i, n in enumerate(late)}
    big.update({n: tuple(r[i] for r in res_first) for i, n in enumerate(first)})

    full_shapes = [g.shape for g in small_full]
    small_g = _unpack(sum_leading(from_all, name="small_sum"), full_shapes)
    loss = small_g.pop()[0]
    small_g[0] = lax.dynamic_slice_in_dim(small_g[0], chip * ns_cols, ns_cols, axis=1)
    small_g[1] = lax.dynamic_slice_in_dim(small_g[1], chip * ns_cols, ns_cols, axis=1)
    small_g[5] = lax.dynamic_slice_in_dim(small_g[5], chip * nf_cols, nf_cols, axis=2)
    small_shapes = [weights[n].shape for n in SMALL]
    small_g = [g.reshape(s) for g, s in zip(small_g, small_shapes)]
    flat2 = [(math.prod(s[:-1]), s[-1]) for s in small_shapes]
    small_d, small_m, small_v = adamw_small(
        *[[a.reshape(s2) for a, s2 in zip(group, flat2)]
          for group in ([weights[n] for n in SMALL], small_g, [moms[n] for n in SMALL], [vars_[n] for n in SMALL])],
        name="adamw_small")
    small_d, small_m, small_v = ([a.reshape(s) for a, s in zip(group, small_shapes)]
                                 for group in (small_d, small_m, small_v))

    out = {}
    for i, n in enumerate(SMALL):
        out[n] = (small_g[i], small_d[i], small_m[i], small_v[i])
    out.update(big)
    order = ["a_norm", "a_w_in", "a_v_norm", "a_w_s", "a_b_s", "a_w_out", "f_norm", "f_w_in", "f_conv_w", "f_conv_b",
             "f_w_out", "kv_norm", "w_kv", "k_norm", "b_norm", "b_w_q", "b_q_norm", "b_sinks", "b_w_o"]
    return (loss, grad_x, *[out[n][0] for n in order], *[out[n][1] for n in order],
            *[out[n][2] for n in order], *[out[n][3] for n in order])
```

```python
import functools
import math

import jax
import jax.numpy as jnp
from jax import lax
from jax.experimental import pallas as pl
from jax.experimental.pallas import tpu as pltpu

f32 = jnp.float32
bf16 = jnp.bfloat16
MESH = pl.DeviceIdType.MESH
ANY = pl.BlockSpec(memory_space=pl.ANY)

EPS = 1e-6
LANES = 128
CHUNK = 128
HEAD_DIM = 64
N_Q_HEADS = 16
N_KV_HEADS = 4
Q_PER_KV = N_Q_HEADS // N_KV_HEADS
N_SHARDS = 4
N_DEV = 8

ADAM_LR = 0.001
ADAM_B1 = 0.9
ADAM_B2 = 0.999
ADAM_EPS = 1e-08
ADAM_WD = 0.01
ADAM_STEP = 10
ADAM_C1 = 1.0 - ADAM_B1 ** ADAM_STEP
ADAM_C2 = 1.0 - ADAM_B2 ** ADAM_STEP

_INV_SQRT2 = 1.0 / math.sqrt(2.0)
_INV_SQRT2PI = 1.0 / math.sqrt(2.0 * math.pi)


def _params(*sem):
    return pltpu.CompilerParams(dimension_semantics=sem)


def _gelu(z):
    return 0.5 * z * (1.0 + lax.erf(z * _INV_SQRT2))


def _gelu_and_grad(z):
    cdf = 0.5 * (1.0 + lax.erf(z * _INV_SQRT2))
    return z * cdf, cdf + z * jnp.exp(-0.5 * z * z) * _INV_SQRT2PI


def _dot(a, b):
    return jnp.dot(a, b, preferred_element_type=f32)


def _dot_nt(a, b):
    return lax.dot_general(a, b, (((1,), (1,)), ((), ())), preferred_element_type=f32)


def _dot_tn(a, b):
    return lax.dot_general(a, b, (((0,), (0,)), ((), ())), preferred_element_type=f32)


def _dot_split(a, b):
    hi = a.astype(bf16)
    lo = (a - hi.astype(f32)).astype(bf16)
    return _dot(hi, b) + _dot(lo, b)


VMEM_TILE_BUDGET = 52 * 1024 * 1024
MAX_ROW_TILE = 2048


def _row_tile(m, fixed_bytes, row_bytes):
    tm = min(m, MAX_ROW_TILE)
    while tm > 256 and 2 * (fixed_bytes + tm * row_bytes) > VMEM_TILE_BUDGET:
        tm //= 2
    return tm


def _isz(a):
    return jnp.dtype(a.dtype).itemsize


def mm_nn(a, w3, *, name, s0=0, ns=None, add=None, out_dtype=f32):
    m, k = a.shape
    s_all, _, n_s = w3.shape
    ns = s_all if ns is None else ns
    tm = _row_tile(m, k * n_s * 2, k * _isz(a) + n_s * jnp.dtype(out_dtype).itemsize + (0 if add is None else n_s * 4))

    def body(*refs):
        if add is None:
            a_ref, w_ref, o_ref = refs
            acc = _dot(a_ref[...].astype(bf16), w_ref[0])
        else:
            a_ref, w_ref, add_ref, o_ref = refs
            acc = _dot(a_ref[...].astype(bf16), w_ref[0]) + add_ref[...]
        o_ref[...] = acc.astype(out_dtype)

    in_specs = [pl.BlockSpec((tm, k), lambda j, i: (i, 0)),
                pl.BlockSpec((1, k, n_s), lambda j, i: (s0 + j, 0, 0))]
    args = [a, w3]
    if add is not None:
        in_specs.append(pl.BlockSpec((tm, n_s), lambda j, i: (i, j)))
        args.append(add)
    return pl.pallas_call(
        body, name=name, grid=(ns, m // tm), in_specs=in_specs,
        out_specs=pl.BlockSpec((tm, n_s), lambda j, i: (i, j)),
        out_shape=jax.ShapeDtypeStruct((m, ns * n_s), out_dtype),
        compiler_params=_params("parallel", "parallel"))(*args)


def mm_nt(a_list, w3, *, name, tko=None, add=None, out_dtype=f32):
    s_all, k_out, n_s = w3.shape
    m = a_list[0].shape[0]
    na = len(a_list)
    spa = s_all // na
    tko = k_out if tko is None else tko
    tm = _row_tile(m, tko * n_s * 2, na * n_s * _isz(a_list[0]) + tko * 4 * (1 if add is None else 2))

    def body(*refs):
        a_refs = refs[:na]
        w_ref = refs[na]
        o_ref = refs[-1]
        s = pl.program_id(2)

        @pl.when(s == 0)
        def _():
            if add is None:
                o_ref[...] = jnp.zeros_like(o_ref)
            else:
                o_ref[...] = refs[na + 1][...]

        for idx in range(na):
            @pl.when(s // spa == idx)
            def _(idx=idx):
                o_ref[...] += _dot_nt(a_refs[idx][...].astype(bf16), w_ref[0])

    def a_map(idx):
        return lambda ko, i, s: (i, jnp.clip(s - idx * spa, 0, spa - 1))

    in_specs = [pl.BlockSpec((tm, n_s), a_map(idx)) for idx in range(na)]
    in_specs.append(pl.BlockSpec((1, tko, n_s), lambda ko, i, s: (s, ko, 0)))
    args = list(a_list) + [w3]
    if add is not None:
        in_specs.append(pl.BlockSpec((tm, tko), lambda ko, i, s: (i, ko)))
        args.append(add)
    return pl.pallas_call(
        body, name=name, grid=(k_out // tko, m // tm, s_all), in_specs=in_specs,
        out_specs=pl.BlockSpec((tm, tko), lambda ko, i, s: (i, ko)),
        out_shape=jax.ShapeDtypeStruct((m, k_out), out_dtype),
        compiler_params=_params("parallel", "parallel", "arbitrary"))(*args)


def mm_tn(a, b_list, c_arr, *, name, n_s, shard_rows, tki=None):
    m, k_in = a.shape
    na = len(b_list)
    s_all = sum(b.shape[1] for b in b_list) // n_s
    spa = s_all // na
    tki = k_in if tki is None else tki
    tm = _row_tile(m, tki * n_s * 4, tki * _isz(a) + na * n_s * _isz(b_list[0]))

    nsteps = m // tm
    per_blk = tki // shard_rows
    half = shard_rows // 2

    def body(c_ref, *refs):
        a_ref = refs[0]
        b_refs = refs[1:1 + na]
        o_ref, ob_ref = refs[-2], refs[-1]
        s = pl.program_id(0)
        r = pl.program_id(2)

        @pl.when(r == 0)
        def _():
            o_ref[...] = jnp.zeros_like(o_ref)

        for idx in range(na):
            @pl.when(s // spa == idx)
            def _(idx=idx):
                o_ref[0] += _dot_tn(a_ref[...].astype(bf16), b_refs[idx][...].astype(bf16))

        @pl.when(r == nsteps - 1)
        def _():
            for q in range(per_blk):
                start = pl.multiple_of(q * shard_rows + (1 - c_ref[0]) * half, 16)
                ob_ref[q] = o_ref[0, pl.ds(start, half), :].astype(bf16)

    def b_map(idx):
        def index(s, ki, r, c_ref):
            active = (s // spa) == idx
            return (jnp.where(active, r, 0), jnp.clip(s - idx * spa, 0, spa - 1))
        return index

    in_specs = [pl.BlockSpec((tm, tki), lambda s, ki, r, c_ref: (r, ki))]
    in_specs += [pl.BlockSpec((tm, n_s), b_map(idx)) for idx in range(na)]
    n_blk = k_in // tki
    return pl.pallas_call(
        body, name=name,
        grid_spec=pltpu.PrefetchScalarGridSpec(
            num_scalar_prefetch=1, grid=(s_all, n_blk, nsteps), in_specs=in_specs,
            out_specs=[pl.BlockSpec((1, tki, n_s), lambda s, ki, r, c_ref: (s, ki, 0)),
                       pl.BlockSpec((per_blk, half, n_s), lambda s, ki, r, c_ref: (s * n_blk + ki, 0, 0))]),
        out_shape=[jax.ShapeDtypeStruct((s_all, k_in, n_s), f32),
                   jax.ShapeDtypeStruct((s_all * k_in // shard_rows, half, n_s), bf16)],
        compiler_params=_params("parallel", "parallel", "arbitrary"))(c_arr, a, *b_list)


def mm_nt_rms_bwd(a_list, w3, x, r, g, dx_in, *, name, extra=None):
    s_all, d, n_s = w3.shape
    m = a_list[0].shape[0]
    na = len(a_list)
    spa = s_all // na
    ne = 0 if extra is None else 1
    tm = _row_tile(m, d * n_s * 2, na * n_s * _isz(a_list[0]) + d * 4 * (4 + ne))

    def body(*refs):
        a_refs, w_ref = refs[:na], refs[na]
        x_ref, r_ref, g_ref, dxin_ref = refs[na + 1:na + 5]
        dh2_ref, g2_ref = (refs[na + 5], refs[na + 6]) if ne else (None, None)
        outs = refs[na + 5 + 2 * ne:]
        dx_ref, dg_ref = outs[0], outs[1]
        dg2_ref = outs[2] if ne else None
        acc_ref = outs[-1]
        i, s = pl.program_id(0), pl.program_id(1)

        @pl.when(s == 0)
        def _():
            acc_ref[...] = jnp.zeros_like(acc_ref)

        for idx in range(na):
            @pl.when(s // spa == idx)
            def _(idx=idx):
                acc_ref[...] += _dot_nt(a_refs[idx][...].astype(bf16), w_ref[0])

        @pl.when(s == s_all - 1)
        def _():
            rv = r_ref[...]
            xh = x_ref[...] * rv
            total = dxin_ref[...]
            pairs = [(acc_ref[...], g_ref, dg_ref)] + ([(dh2_ref[...], g2_ref, dg2_ref)] if ne else [])
            for dh, gain_ref, dgain_ref in pairs:
                part = jnp.sum(dh * xh, axis=0, keepdims=True)

                @pl.when(i == 0)
                def _(dgain_ref=dgain_ref, part=part):
                    dgain_ref[...] = part

                @pl.when(i > 0)
                def _(dgain_ref=dgain_ref, part=part):
                    dgain_ref[...] += part

                tg = dh * gain_ref[...]
                total = total + rv * (tg - xh * jnp.mean(tg * xh, axis=1, keepdims=True))
            dx_ref[...] = total

    def a_map(idx):
        return lambda i, s: (i, jnp.clip(s - idx * spa, 0, spa - 1))

    row = pl.BlockSpec((tm, d), lambda i, s: (i, 0))
    vec = pl.BlockSpec((1, d), lambda i, s: (0, 0))
    in_specs = [pl.BlockSpec((tm, n_s), a_map(idx)) for idx in range(na)]
    in_specs += [pl.BlockSpec((1, d, n_s), lambda i, s: (s, 0, 0)), row, pl.BlockSpec((tm, 1), lambda i, s: (i, 0)), vec, row]
    args = list(a_list) + [w3, x, r, g, dx_in]
    if ne:
        in_specs += [row, vec]
        args += list(extra)
    outs = pl.pallas_call(
        body, name=name, grid=(m // tm, s_all), in_specs=in_specs, out_specs=[row] + [vec] * (1 + ne),
        out_shape=[jax.ShapeDtypeStruct((m, d), f32)] + [jax.ShapeDtypeStruct((1, d), f32)] * (1 + ne),
        scratch_shapes=[pltpu.VMEM((tm, d), f32)],
        compiler_params=_params("arbitrary", "arbitrary"))(*args)
    return outs


def mm_residual(a, w, x, *, name, gains=(), target=None):
    m, k = a.shape
    d = w.shape[1]
    ng = len(gains)
    tm = _row_tile(m, k * d * 2, k * _isz(a) + d * 4 * 3 + ng * d * 2)

    def body(*refs):
        a_ref, w_ref, x_ref = refs[:3]
        y = _dot(a_ref[...].astype(bf16), w_ref[...]) + x_ref[...]
        if target is None:
            g_refs = refs[3:3 + ng]
            y_ref = refs[3 + ng]
            h_refs = refs[4 + ng:4 + 2 * ng]
            r_ref = refs[-1]
            y_ref[...] = y
            r = lax.rsqrt(jnp.mean(y * y, axis=1, keepdims=True) + EPS)
            yh = y * r
            for g_ref, h_ref in zip(g_refs, h_refs):
                h_ref[...] = (yh * g_ref[...]).astype(bf16)
            r_ref[...] = r
        else:
            t_ref, dy_ref, s_ref = refs[3:]
            i = pl.program_id(0)
            e = y - t_ref[...]
            dy_ref[...] = e * (1.0 / d)
            part = jnp.sum(e * e, axis=0, keepdims=True)

            @pl.when(i == 0)
            def _():
                s_ref[...] = part

            @pl.when(i > 0)
            def _():
                s_ref[...] += part

    row = pl.BlockSpec((tm, d), lambda i: (i, 0))
    vec = pl.BlockSpec((1, d), lambda i: (0, 0))
    in_specs = [pl.BlockSpec((tm, k), lambda i: (i, 0)), pl.BlockSpec((k, d), lambda i: (0, 0)), row]
    if target is None:
        outs = pl.pallas_call(
            body, name=name, grid=(m // tm,), in_specs=in_specs + [vec] * ng,
            out_specs=[row] * (1 + ng) + [pl.BlockSpec((tm, 1), lambda i: (i, 0))],
            out_shape=[jax.ShapeDtypeStruct((m, d), f32)] + [jax.ShapeDtypeStruct((m, d), bf16)] * ng
            + [jax.ShapeDtypeStruct((m, 1), f32)],
            compiler_params=_params("parallel"))(a, w, x, *gains)
        return outs[0], outs[1:1 + ng], outs[-1]
    return pl.pallas_call(
        body, name=name, grid=(m // tm,), in_specs=in_specs + [row], out_specs=[row, vec],
        out_shape=[jax.ShapeDtypeStruct((m, d), f32), jax.ShapeDtypeStruct((1, d), f32)],
        compiler_params=_params("arbitrary"))(a, w, x, target)


def rms_fwd(x, gains, *, name, tr=512):
    t, d = x.shape
    tr = min(tr, t)
    ng = len(gains)

    def body(*refs):
        x_ref = refs[0]
        g_refs = refs[1:1 + ng]
        h_refs = refs[1 + ng:1 + 2 * ng]
        r_ref = refs[-1]
        xv = x_ref[...]
        r = lax.rsqrt(jnp.mean(xv * xv, axis=1, keepdims=True) + EPS)
        xh = xv * r
        for g_ref, h_ref in zip(g_refs, h_refs):
            h_ref[...] = (xh * g_ref[...]).astype(bf16)
        r_ref[...] = r

    row = pl.BlockSpec((tr, d), lambda i: (i, 0))
    vec = pl.BlockSpec((1, d), lambda i: (0, 0))
    outs = pl.pallas_call(
        body, name=name, grid=(t // tr,), in_specs=[row] + [vec] * ng,
        out_specs=[row] * ng + [pl.BlockSpec((tr, 1), lambda i: (i, 0))],
        out_shape=[jax.ShapeDtypeStruct((t, d), bf16)] * ng + [jax.ShapeDtypeStruct((t, 1), f32)],
        compiler_params=_params("parallel"))(x, *gains)
    return outs[:ng], outs[ng]


def sgu_gate_fwd(zuv, gv, wc, bt, *, name, tr=512):
    t, w = zuv.shape[0], zuv.shape[1] // 2
    tr = min(tr, t)
    groups = w // LANES

    def body(zu_ref, zv_ref, gv_ref, wc_ref, bt_ref, y_ref):
        vp = _gelu(zv_ref[...])
        rv = lax.rsqrt(jnp.mean(vp * vp, axis=1, keepdims=True) + EPS)
        vb = (vp * rv * gv_ref[...]).astype(bf16)
        for c in range(tr // CHUNK):
            rows = slice(c * CHUNK, (c + 1) * CHUNK)
            for g in range(groups):
                cols = slice(g * LANES, (g + 1) * LANES)
                sv = _dot(wc_ref[g], vb[rows, cols]) + bt_ref[:, g:g + 1]
                y_ref[rows, cols] = (_gelu(zu_ref[rows, cols]) * sv).astype(bf16)

    row = pl.BlockSpec((tr, w), lambda i: (i, 0))
    return pl.pallas_call(
        body, name=name, grid=(t // tr,),
        in_specs=[row, pl.BlockSpec((tr, w), lambda i: (i, 1)), pl.BlockSpec((1, w), lambda i: (0, 0)),
                  pl.BlockSpec((groups, CHUNK, CHUNK), lambda i: (0, 0, 0)),
                  pl.BlockSpec((CHUNK, groups), lambda i: (0, 0))],
        out_specs=row, out_shape=jax.ShapeDtypeStruct((t, w), bf16),
        compiler_params=_params("parallel"))(zuv, zuv, gv, wc, bt)


def sgu_gate_bwd(zuv, dy, gv, wc, bt, *, name, tr=512):
    t, w = zuv.shape[0], zuv.shape[1] // 2
    tr = min(tr, t)
    groups = w // LANES
    nsteps = t // tr

    def body(zu_ref, zv_ref, dy_ref, gv_ref, wc_ref, bt_ref,
             dzu_ref, dzv_ref, dgv_ref, dws_ref, dbt_ref, dv_ref, bacc_ref):
        i = pl.program_id(0)

        @pl.when(i == 0)
        def _():
            dgv_ref[...] = jnp.zeros_like(dgv_ref)
            dws_ref[...] = jnp.zeros_like(dws_ref)
            bacc_ref[...] = jnp.zeros_like(bacc_ref)

        vp, vp_grad = _gelu_and_grad(zv_ref[...])
        rv = lax.rsqrt(jnp.mean(vp * vp, axis=1, keepdims=True) + EPS)
        vhat = vp * rv
        vb = (vhat * gv_ref[...]).astype(bf16)
        for c in range(tr // CHUNK):
            rows = slice(c * CHUNK, (c + 1) * CHUNK)
            for g in range(groups):
                cols = slice(g * LANES, (g + 1) * LANES)
                vblk = vb[rows, cols]
                sv = _dot(wc_ref[g], vblk) + bt_ref[:, g:g + 1]
                zub = zu_ref[rows, cols]
                dyb = dy_ref[rows, cols]
                ub, ub_grad = _gelu_and_grad(zub)
                dzu_ref[rows, cols] = (dyb * sv * ub_grad).astype(bf16)
                dsv = dyb * ub
                bacc_ref[:, cols] += dsv
                dsvb = dsv.astype(bf16)
                dv_ref[rows, cols] = _dot_tn(wc_ref[g], dsvb)
                dws_ref[g] += _dot_nt(dsvb, vblk)
        dv = dv_ref[...]
        dgv_ref[...] += jnp.sum(dv * vhat, axis=0, keepdims=True)
        tg = dv * gv_ref[...]
        dvp = rv * (tg - vhat * jnp.mean(tg * vhat, axis=1, keepdims=True))
        dzv_ref[...] = (dvp * vp_grad).astype(bf16)

        @pl.when(i == nsteps - 1)
        def _():
            tt = lax.broadcasted_iota(jnp.int32, (CHUNK, CHUNK), 0)
            ss = lax.broadcasted_iota(jnp.int32, (CHUNK, CHUNK), 1)
            for g in range(groups):
                dws_ref[g] = jnp.where(ss <= tt, dws_ref[g], 0.0)
                dbt_ref[:, g:g + 1] = jnp.sum(bacc_ref[:, g * LANES:(g + 1) * LANES], axis=1, keepdims=True)

    row = pl.BlockSpec((tr, w), lambda i: (i, 0))
    full3 = pl.BlockSpec((groups, CHUNK, CHUNK), lambda i: (0, 0, 0))
    return pl.pallas_call(
        body, name=name, grid=(nsteps,),
        in_specs=[row, pl.BlockSpec((tr, w), lambda i: (i, 1)), row, pl.BlockSpec((1, w), lambda i: (0, 0)), full3,
                  pl.BlockSpec((CHUNK, groups), lambda i: (0, 0))],
        out_specs=[row, row, pl.BlockSpec((1, w), lambda i: (0, 0)), full3,
                   pl.BlockSpec((CHUNK, groups), lambda i: (0, 0))],
        out_shape=[jax.ShapeDtypeStruct((t, w), bf16), jax.ShapeDtypeStruct((t, w), bf16),
                   jax.ShapeDtypeStruct((1, w), f32), jax.ShapeDtypeStruct((groups, CHUNK, CHUNK), f32),
                   jax.ShapeDtypeStruct((CHUNK, groups), f32)],
        scratch_shapes=[pltpu.VMEM((tr, w), f32), pltpu.VMEM((CHUNK, w), f32)],
        compiler_params=_params("arbitrary"))(zuv, zuv, dy, gv, wc, bt)


HALO = 8


def _shift_down(v, halo, k, first):
    r = pltpu.roll(v, k, 0)
    hh = jnp.where(first, 0.0, pltpu.roll(halo, k, 0))
    rid = lax.broadcasted_iota(jnp.int32, (HALO, v.shape[1]), 0)
    head = jnp.where(rid < k, hh, r[0:HALO])
    if v.shape[0] == HALO:
        return head
    return jnp.concatenate([head, r[HALO:]], axis=0)


def _shift_up(v, halo, k, last):
    n = v.shape[0]
    r = pltpu.roll(v, n - k, 0)
    hh = jnp.where(last, 0.0, pltpu.roll(halo, HALO - k, 0))
    rid = lax.broadcasted_iota(jnp.int32, (HALO, v.shape[1]), 0)
    tail = jnp.where(rid >= HALO - k, hh, r[n - HALO:])
    return jnp.concatenate([r[:n - HALO], tail], axis=0)


def _conv(p, halo, w_ref, b_ref, first):
    return (w_ref[2:3, :] * p + w_ref[1:2, :] * _shift_down(p, halo, 1, first)
            + w_ref[0:1, :] * _shift_down(p, halo, 2, first) + b_ref[...])


BF16_ROWS = 16


def ffn_in_fused(h, w_in4, wg, wu, bg, bu, *, name):
    t, k = h.shape
    s_all, _, n_s = w_in4.shape
    half = s_all // 2
    tm = _row_tile(t, 2 * k * n_s * 2, k * 2 + 4 * n_s * 4 + n_s * 2)

    def body(h_ref, hh_ref, wg_ref, wu_ref, cg_ref, cu_ref, bg_ref, bu_ref, pg_ref, pu_ref, gate_ref, up_ref, a_ref):
        first = pl.program_id(1) == 0
        hv, hh = h_ref[...], hh_ref[...]
        outs = []
        for w_ref, c_ref, b_ref, p_ref, o_ref in ((wg_ref, cg_ref, bg_ref, pg_ref, gate_ref),
                                                  (wu_ref, cu_ref, bu_ref, pu_ref, up_ref)):
            p = _dot(hv, w_ref[0])
            p_ref[...] = p
            hu = _conv(p, _dot(hh, w_ref[0])[BF16_ROWS - HALO:], c_ref, b_ref, first)
            o_ref[...] = hu
            outs.append(hu)
        gate, up = outs
        a_ref[...] = (gate * jax.nn.sigmoid(gate) * up).astype(bf16)

    tile = pl.BlockSpec((tm, n_s), lambda j, i: (i, j))
    cw = pl.BlockSpec((3, n_s), lambda j, i: (0, j))
    cb = pl.BlockSpec((1, n_s), lambda j, i: (0, j))
    f = half * n_s
    return pl.pallas_call(
        body, name=name, grid=(half, t // tm),
        in_specs=[pl.BlockSpec((tm, k), lambda j, i: (i, 0)),
                  pl.BlockSpec((BF16_ROWS, k), lambda j, i: (jnp.maximum(i * (tm // BF16_ROWS) - 1, 0), 0)),
                  pl.BlockSpec((1, k, n_s), lambda j, i: (j, 0, 0)),
                  pl.BlockSpec((1, k, n_s), lambda j, i: (j + half, 0, 0)), cw, cw, cb, cb],
        out_specs=[tile] * 5,
        out_shape=[jax.ShapeDtypeStruct((t, f), f32)] * 4 + [jax.ShapeDtypeStruct((t, f), bf16)],
        compiler_params=_params("parallel", "parallel"))(h, h, w_in4, w_in4, wg, wu, bg, bu)


def _gate_grads(gate, up, dav):
    sg = jax.nn.sigmoid(gate)
    return dav * up * (sg * (1.0 + gate * (1.0 - sg))), dav * gate * sg


GATE_BWD_ROWS = 512


def ffn_gate_bwd(dy, w_out, pg, pu, gate, up, wg, wu, *, name):
    t, f = pg.shape
    d = dy.shape[1]
    tr = min(GATE_BWD_ROWS, t)
    nsteps = t // tr
    tc = f // 2

    def body(dy_ref, dyn_ref, w_ref, pg_ref, pu_ref, gate_ref, gaten_ref, up_ref, upn_ref, wg_ref, wu_ref,
             dg_ref, du_ref, sg_ref, su_ref):
        i = pl.program_id(1)
        last = i == nsteps - 1
        w = w_ref[0]
        da = _dot_nt(dy_ref[...].astype(bf16), w)
        da_n = _dot_nt(dyn_ref[...].astype(bf16), w)
        dgate, dup = _gate_grads(gate_ref[...], up_ref[...], da)
        dgate_n, dup_n = _gate_grads(gaten_ref[...], upn_ref[...], da_n)
        rid = lax.broadcasted_iota(jnp.int32, (8, tc), 0)
        for dd, d_n, c_ref, p_ref, o_ref, s_ref in ((dgate, dgate_n, wg_ref, pg_ref, dg_ref, sg_ref),
                                                    (dup, dup_n, wu_ref, pu_ref, du_ref, su_ref)):
            d1, d2 = _shift_up(dd, d_n, 1, last), _shift_up(dd, d_n, 2, last)
            o_ref[...] = (c_ref[2:3, :] * dd + c_ref[1:2, :] * d1 + c_ref[0:1, :] * d2).astype(bf16)
            p = p_ref[...]
            sums = [jnp.sum(d2 * p, axis=0, keepdims=True), jnp.sum(d1 * p, axis=0, keepdims=True),
                    jnp.sum(dd * p, axis=0, keepdims=True), jnp.sum(dd, axis=0, keepdims=True)]
            part = jnp.zeros((8, tc), f32)
            for k, sk in enumerate(sums):
                part = jnp.where(rid == k, sk, part)

            @pl.when(i == 0)
            def _(s_ref=s_ref, part=part):
                s_ref[...] = part

            @pl.when(i > 0)
            def _(s_ref=s_ref, part=part):
                s_ref[...] += part

    def nxt_rows(j, i):
        return (jnp.minimum((i + 1) * (tr // HALO), t // HALO - 1), j)

    tile = pl.BlockSpec((tr, tc), lambda j, i: (i, j))
    nxt = pl.BlockSpec((HALO, tc), nxt_rows)
    wspec = pl.BlockSpec((3, tc), lambda j, i: (0, j))
    stat = pl.BlockSpec((8, tc), lambda j, i: (0, j))
    return pl.pallas_call(
        body, name=name, grid=(2, nsteps),
        in_specs=[pl.BlockSpec((tr, d), lambda j, i: (i, 0)),
                  pl.BlockSpec((HALO, d), lambda j, i: (nxt_rows(j, i)[0], 0)),
                  pl.BlockSpec((1, tc, d), lambda j, i: (j, 0, 0)),
                  tile, tile, tile, nxt, tile, nxt, wspec, wspec],
        out_specs=[tile, tile, stat, stat],
        out_shape=[jax.ShapeDtypeStruct((t, f), bf16), jax.ShapeDtypeStruct((t, f), bf16),
                   jax.ShapeDtypeStruct((8, f), f32), jax.ShapeDtypeStruct((8, f), f32)],
        compiler_params=_params("parallel", "arbitrary"))(
            dy, dy, w_out.reshape(2, tc, d), pg, pu, gate, gate, up, up, wg, wu)


def _head_mean_matrix():
    i = lax.broadcasted_iota(jnp.int32, (LANES, LANES), 0) // HEAD_DIM
    j = lax.broadcasted_iota(jnp.int32, (LANES, LANES), 1) // HEAD_DIM
    return jnp.where(i == j, 1.0 / HEAD_DIM, 0.0).astype(bf16)


def _lane_half(shape):
    return (lax.broadcasted_iota(jnp.int32, shape, 1) % LANES) // HEAD_DIM


def q_proj_norm(h, w, g2, *, name, scale):
    t, k = h.shape
    n = w.shape[1]
    tm = _row_tile(t, k * n * 2, k * 2 + n * 4 + n * 2)

    def body(h_ref, w_ref, g_ref, qp_ref, o_ref):
        qp_ref[...] = _dot(h_ref[...], w_ref[...])
        bd = _head_mean_matrix()
        for cb in range(n // LANES):
            cols = slice(cb * LANES, (cb + 1) * LANES)
            xc = qp_ref[:, cols]
            rh = lax.rsqrt(_dot_split(xc * xc, bd) + EPS)
            o_ref[:, cols] = (xc * rh * g_ref[...] * scale).astype(bf16)

    row = pl.BlockSpec((tm, n), lambda i: (i, 0))
    return pl.pallas_call(
        body, name=name, grid=(t // tm,),
        in_specs=[pl.BlockSpec((tm, k), lambda i: (i, 0)), pl.BlockSpec((k, n), lambda i: (0, 0)),
                  pl.BlockSpec((1, LANES), lambda i: (0, 0))],
        out_specs=[row, row], out_shape=[jax.ShapeDtypeStruct((t, n), f32), jax.ShapeDtypeStruct((t, n), bf16)],
        compiler_params=_params("parallel"))(h, w, g2)


def kv_proj_post(h, w, g2, *, name):
    t, k = h.shape
    n = w.shape[1]
    kw = n // 2
    tm = _row_tile(t, k * n * 2, k * 2 + n * 4 + 2 * n * 2)

    def body(h_ref, w_ref, g_ref, kv_ref, k_ref, v_ref):
        kv_ref[...] = _dot(h_ref[...], w_ref[...])
        bd = _head_mean_matrix()
        half = _lane_half((tm, LANES))
        for cb in range(kw // LANES):
            xc = kv_ref[:, cb * LANES:(cb + 1) * LANES]
            rh = lax.rsqrt(_dot_split(xc * xc, bd) + EPS)
            kn = xc * rh * g_ref[...]
            vc = kv_ref[:, kw + cb * LANES:kw + (cb + 1) * LANES]
            for src, dst in ((kn, k_ref), (vc, v_ref)):
                sw = pltpu.roll(src, HEAD_DIM, 1)
                for hf in range(2):
                    blk = 2 * cb + hf
                    dst[:, blk * LANES:(blk + 1) * LANES] = jnp.where(half == hf, src, sw).astype(bf16)

    row = pl.BlockSpec((tm, n), lambda i: (i, 0))
    return pl.pallas_call(
        body, name=name, grid=(t // tm,),
        in_specs=[pl.BlockSpec((tm, k), lambda i: (i, 0)), pl.BlockSpec((k, n), lambda i: (0, 0)),
                  pl.BlockSpec((1, LANES), lambda i: (0, 0))],
        out_specs=[row, row, row],
        out_shape=[jax.ShapeDtypeStruct((t, n), f32), jax.ShapeDtypeStruct((t, n), bf16), jax.ShapeDtypeStruct((t, n), bf16)],
        compiler_params=_params("parallel"))(h, w, g2)


def q_norm_bwd(dq, qp, g2, *, name, scale, tr=512):
    t, w = qp.shape
    tr = min(tr, t)

    def body(dq_ref, x_ref, g_ref, o_ref, dg_ref):
        i = pl.program_id(0)
        bd = _head_mean_matrix()
        acc = jnp.zeros((1, LANES), f32)
        for cb in range(w // LANES):
            cols = slice(cb * LANES, (cb + 1) * LANES)
            xc = x_ref[:, cols]
            rh = lax.rsqrt(_dot_split(xc * xc, bd) + EPS)
            xh = xc * rh
            dy = dq_ref[:, cols] * scale
            acc = acc + jnp.sum(dy * xh, axis=0, keepdims=True)
            tg = dy * g_ref[...]
            o_ref[:, cols] = (rh * (tg - xh * _dot_split(tg * xh, bd))).astype(bf16)

        @pl.when(i == 0)
        def _():
            dg_ref[...] = acc

        @pl.when(i > 0)
        def _():
            dg_ref[...] += acc

    row = pl.BlockSpec((tr, w), lambda i: (i, 0))
    vec = pl.BlockSpec((1, LANES), lambda i: (0, 0))
    return pl.pallas_call(
        body, name=name, grid=(t // tr,), in_specs=[row, row, vec], out_specs=[row, vec],
        out_shape=[jax.ShapeDtypeStruct((t, w), bf16), jax.ShapeDtypeStruct((1, LANES), f32)],
        compiler_params=_params("arbitrary"))(dq, qp, g2)


def kv_post_bwd(dk2, dv2, kv, g2, *, name, tr=512):
    t, w = kv.shape
    tr = min(tr, t)
    kw = w // 2

    def body(dk_ref, dv_ref, x_ref, g_ref, o_ref, dg_ref):
        i = pl.program_id(0)
        bd = _head_mean_matrix()
        half = _lane_half((tr, LANES))
        acc = jnp.zeros((1, LANES), f32)

        def fold(ref, cb):
            a = ref[:, (2 * cb) * LANES:(2 * cb + 1) * LANES]
            b = ref[:, (2 * cb + 1) * LANES:(2 * cb + 2) * LANES]
            return jnp.where(half == 0, a + pltpu.roll(a, HEAD_DIM, 1), b + pltpu.roll(b, HEAD_DIM, 1))

        for cb in range(kw // LANES):
            cols = slice(cb * LANES, (cb + 1) * LANES)
            xc = x_ref[:, cols]
            rh = lax.rsqrt(_dot_split(xc * xc, bd) + EPS)
            xh = xc * rh
            dy = fold(dk_ref, cb)
            acc = acc + jnp.sum(dy * xh, axis=0, keepdims=True)
            tg = dy * g_ref[...]
            o_ref[:, cols] = (rh * (tg - xh * _dot_split(tg * xh, bd))).astype(bf16)
            o_ref[:, kw + cb * LANES:kw + (cb + 1) * LANES] = fold(dv_ref, cb).astype(bf16)

        @pl.when(i == 0)
        def _():
            dg_ref[...] = acc

        @pl.when(i > 0)
        def _():
            dg_ref[...] += acc

    dup = pl.BlockSpec((tr, 2 * kw), lambda i: (i, 0))
    row = pl.BlockSpec((tr, w), lambda i: (i, 0))
    vec = pl.BlockSpec((1, LANES), lambda i: (0, 0))
    return pl.pallas_call(
        body, name=name, grid=(t // tr,), in_specs=[dup, dup, row, vec], out_specs=[row, vec],
        out_shape=[jax.ShapeDtypeStruct((t, w), bf16), jax.ShapeDtypeStruct((1, LANES), f32)],
        compiler_params=_params("arbitrary"))(dk2, dv2, kv, g2)


def _slope(h):
    return 2.0 ** (-8.0 * (h + 1) / N_Q_HEADS)


GROUP_ROWS = Q_PER_KV * CHUNK


def _band_mask(n):
    tq = lax.broadcasted_iota(jnp.int32, (GROUP_ROWS, 2 * CHUNK), 0) % CHUNK
    jk = lax.broadcasted_iota(jnp.int32, (GROUP_ROWS, 2 * CHUNK), 1)
    dist = tq + CHUNK - jk
    ok = (dist >= 0) & (dist < CHUNK) & jnp.logical_not((n == 0) & (jk < CHUNK))
    return dist.astype(f32), ok


def _band(ref, n, kh):
    p0 = pl.multiple_of(jnp.maximum(n - 1, 0) * CHUNK, CHUNK)
    c0 = pl.multiple_of(n * CHUNK, CHUNK)
    cols = slice(kh * LANES, (kh + 1) * LANES)
    return jnp.concatenate([ref[pl.ds(p0, CHUNK), cols], ref[pl.ds(c0, CHUNK), cols]], axis=0)


def _stack_heads(ref, kh, half):
    parts = []
    for cb in (2 * kh, 2 * kh + 1):
        xc = ref[:, cb * LANES:(cb + 1) * LANES].astype(f32)
        parts += [jnp.where(half == hf, xc, 0.0).astype(bf16) for hf in range(2)]
    return jnp.concatenate(parts, axis=0)


def _unstack_heads(x4, half):
    return (jnp.where(half == 0, x4[0:CHUNK], x4[CHUNK:2 * CHUNK]),
            jnp.where(half == 0, x4[2 * CHUNK:3 * CHUNK], x4[3 * CHUNK:]))


def _per_head_column(kh, values):
    grp = lax.broadcasted_iota(jnp.int32, (GROUP_ROWS, 1), 0) // CHUNK
    col = jnp.full((GROUP_ROWS, 1), values[0], f32)
    for g in range(1, Q_PER_KV):
        col = jnp.where(grp == g, values[g], col)
    return col


def _softmax_band(q4, kband, dist, ok, slope, sink):
    s = _dot_nt(q4, kband)
    s = jnp.where(ok, s - slope * dist, -jnp.inf)
    m = jnp.maximum(jnp.max(s, axis=1, keepdims=True), sink)
    e = jnp.exp(s - m)
    es = jnp.exp(sink - m)
    inv = 1.0 / (jnp.sum(e, axis=1, keepdims=True) + es)
    return e * inv, es * inv


def attn_fwd(q, k2, v2, sinks, *, name):
    t, w = q.shape
    nb = t // CHUNK

    def body(sink_ref, q_ref, k_ref, v_ref, o_ref):
        n = pl.program_id(0)
        dist, ok = _band_mask(n)
        half = _lane_half((CHUNK, LANES))
        khs = range(N_KV_HEADS)
        heads = [[Q_PER_KV * kh + g for g in range(Q_PER_KV)] for kh in khs]
        q4 = [_stack_heads(q_ref, kh, half) for kh in khs]
        soft = [_softmax_band(q4[kh], _band(k_ref, n, kh), dist, ok, _per_head_column(kh, [_slope(h) for h in heads[kh]]),
                              _per_head_column(kh, [sink_ref[h] for h in heads[kh]])) for kh in khs]
        o4 = [_dot(soft[kh][0].astype(bf16), _band(v_ref, n, kh)) for kh in khs]
        for kh in khs:
            lo, hi = _unstack_heads(o4[kh], half)
            o_ref[:, (2 * kh) * LANES:(2 * kh + 1) * LANES] = lo.astype(bf16)
            o_ref[:, (2 * kh + 1) * LANES:(2 * kh + 2) * LANES] = hi.astype(bf16)

    full = pl.BlockSpec((t, k2.shape[1]), lambda n: (0, 0))
    return pl.pallas_call(
        body, name=name, grid=(nb,),
        in_specs=[pl.BlockSpec(memory_space=pltpu.SMEM), pl.BlockSpec((CHUNK, w), lambda n: (n, 0)), full, full],
        out_specs=pl.BlockSpec((CHUNK, w), lambda n: (n, 0)),
        out_shape=jax.ShapeDtypeStruct((t, w), bf16),
        compiler_params=_params("parallel"))(sinks, q, k2, v2)


def attn_bwd(q, k2, v2, do, sinks, *, name):
    t, w = q.shape
    nb = t // CHUNK
    kw = k2.shape[1]

    def body(sink_ref, q_ref, k_ref, v_ref, do_ref, dq_ref, dk_ref, dv_ref, ds_ref, kc_ref, vc_ref):
        n = pl.program_id(0)

        @pl.when(n == 0)
        def _():
            ds_ref[...] = jnp.zeros_like(ds_ref)
            kc_ref[...] = jnp.zeros_like(kc_ref)
            vc_ref[...] = jnp.zeros_like(vc_ref)
            dk_ref[...] = jnp.zeros_like(dk_ref)
            dv_ref[...] = jnp.zeros_like(dv_ref)

        @pl.when(n == nb)
        def _():
            dk_ref[...] = kc_ref[...]
            dv_ref[...] = vc_ref[...]

        @pl.when(n < nb)
        def _():
            dist, ok = _band_mask(n)
            half = _lane_half((CHUNK, LANES))
            lane = lax.broadcasted_iota(jnp.int32, (1, LANES), 1)
            sink_acc = jnp.zeros((1, LANES), f32)
            khs = range(N_KV_HEADS)
            heads = [[Q_PER_KV * kh + g for g in range(Q_PER_KV)] for kh in khs]
            q4 = [_stack_heads(q_ref, kh, half) for kh in khs]
            do4 = [_stack_heads(do_ref, kh, half) for kh in khs]
            kband = [_band(k_ref, n, kh) for kh in khs]
            vband = [_band(v_ref, n, kh) for kh in khs]
            soft = [_softmax_band(q4[kh], kband[kh], dist, ok, _per_head_column(kh, [_slope(h) for h in heads[kh]]),
                                  _per_head_column(kh, [sink_ref[h] for h in heads[kh]])) for kh in khs]
            dp = [_dot_nt(do4[kh], vband[kh]) for kh in khs]
            delta = [jnp.sum(soft[kh][0] * dp[kh], axis=1, keepdims=True) for kh in khs]
            dsb = [(soft[kh][0] * (dp[kh] - delta[kh])).astype(bf16) for kh in khs]
            dq4 = [_dot(dsb[kh], kband[kh]) for kh in khs]
            dkb = [_dot_tn(dsb[kh], q4[kh]) for kh in khs]
            dvb = [_dot_tn(soft[kh][0].astype(bf16), do4[kh]) for kh in khs]
            for kh in khs:
                sd = soft[kh][1] * delta[kh]
                for g, h in enumerate(heads[kh]):
                    part = jnp.sum(sd[g * CHUNK:(g + 1) * CHUNK], axis=0, keepdims=True)
                    sink_acc = sink_acc + jnp.where(lane == h, -part, 0.0)
                lo, hi = _unstack_heads(dq4[kh], half)
                dq_ref[:, (2 * kh) * LANES:(2 * kh + 1) * LANES] = lo
                dq_ref[:, (2 * kh + 1) * LANES:(2 * kh + 2) * LANES] = hi
                cols = slice(kh * LANES, (kh + 1) * LANES)
                dk_ref[:, cols] = kc_ref[:, cols] + dkb[kh][0:CHUNK]
                dv_ref[:, cols] = vc_ref[:, cols] + dvb[kh][0:CHUNK]
                kc_ref[:, cols] = dkb[kh][CHUNK:]
                vc_ref[:, cols] = dvb[kh][CHUNK:]
            ds_ref[...] += sink_acc

    full = pl.BlockSpec((t, kw), lambda n: (0, 0))
    qblk = pl.BlockSpec((CHUNK, w), lambda n: (jnp.minimum(n, nb - 1), 0))
    kblk = pl.BlockSpec((CHUNK, kw), lambda n: (jnp.maximum(n - 1, 0), 0))
    return pl.pallas_call(
        body, name=name, grid=(nb + 1,),
        in_specs=[pl.BlockSpec(memory_space=pltpu.SMEM), qblk, full, full, qblk],
        out_specs=[qblk, kblk, kblk, pl.BlockSpec((1, LANES), lambda n: (0, 0))],
        out_shape=[jax.ShapeDtypeStruct((t, w), f32), jax.ShapeDtypeStruct((t, kw), f32),
                   jax.ShapeDtypeStruct((t, kw), f32), jax.ShapeDtypeStruct((1, LANES), f32)],
        scratch_shapes=[pltpu.VMEM((CHUNK, kw), f32), pltpu.VMEM((CHUNK, kw), f32)],
        compiler_params=_params("arbitrary"))(sinks, q, k2, v2, do)


N_STEPS = 8


def _row_blocks(shape):
    if len(shape) == 2:
        r, c = shape
        return (r // N_STEPS, c), (lambda s: (s, 0))
    l, r, c = shape
    per = N_STEPS // l
    return (1, r // per, c), (lambda s: (s // per, s % per, 0))


CAST_STEPS = 4


def cast_into_slot(arrays, k_arr, *, name):
    in_specs, out_specs, out_shape, layers = [], [], [], []
    for a in arrays:
        r, c = a.shape[-2:]
        rb = r // CAST_STEPS
        if a.ndim == 2:
            in_specs.append(pl.BlockSpec((rb, c), lambda s, k: (s, 0)))
            layers.append(None)
        else:
            for l in range(a.shape[0]):
                in_specs.append(pl.BlockSpec((1, rb, c), lambda s, k, l=l: (l, s, 0)))
                layers.append(l)
        for _ in range(1 if a.ndim == 2 else a.shape[0]):
            out_specs.append(pl.BlockSpec((1, rb, c), lambda s, k: (k[0], s, 0)))
            out_shape.append(jax.ShapeDtypeStruct((N_SHARDS, r, c), bf16))
    n = len(in_specs)

    def body(k_ref, *refs):
        for i_ref, o_ref, l in zip(refs[:n], refs[n:], layers):
            o_ref[0] = (i_ref[...] if l is None else i_ref[0]).astype(bf16)

    args = []
    for a in arrays:
        args += [a] * (1 if a.ndim == 2 else a.shape[0])
    return pl.pallas_call(
        body, name=name,
        grid_spec=pltpu.PrefetchScalarGridSpec(num_scalar_prefetch=1, grid=(CAST_STEPS,),
                                               in_specs=in_specs, out_specs=out_specs),
        out_shape=out_shape, compiler_params=_params("parallel"))(k_arr, *args)


def adamw(ws, gs, ms, vs, *, name):
    n = len(ws)
    specs, g_specs, g_count = [], [], []
    for w, g_list in zip(ws, gs):
        blk, index = _row_blocks(w.shape)
        specs.append(pl.BlockSpec(blk, index))
        layers = len(g_list)
        per = N_STEPS // layers
        g_count.append(layers)
        for l in range(layers):
            g_specs.append(pl.BlockSpec(blk[-2:], lambda s, l=l, per=per: (jnp.where(s // per == l, s % per, 0), 0)))
    ng = len(g_specs)

    def body(*refs):
        s = pl.program_id(0)
        g_refs = refs[3 * n:3 * n + ng]
        outs = refs[3 * n + ng:]
        off = 0
        for i in range(n):
            w_ref, m_ref, v_ref = refs[i], refs[n + i], refs[2 * n + i]
            go_ref, d_ref, nm_ref, nv_ref = (outs[k * n + i] for k in range(4))
            layers = g_count[i]
            g = g_refs[off][...]
            for l in range(1, layers):
                g = jnp.where(s // (N_STEPS // layers) == l, g_refs[off + l][...], g)
            off += layers
            g = g.reshape(w_ref.shape)
            m = ADAM_B1 * m_ref[...] + (1.0 - ADAM_B1) * g
            v = ADAM_B2 * v_ref[...] + (1.0 - ADAM_B2) * (g * g)
            m_hat = m / ADAM_C1
            v_hat = v / ADAM_C2
            go_ref[...] = g
            d_ref[...] = -ADAM_LR * (m_hat / (jnp.sqrt(v_hat) + ADAM_EPS) + ADAM_WD * w_ref[...])
            nm_ref[...] = m
            nv_ref[...] = v

    outs = pl.pallas_call(
        body, name=name, grid=(N_STEPS,), in_specs=specs * 3 + g_specs, out_specs=specs * 4,
        out_shape=[jax.ShapeDtypeStruct(a.shape, f32) for a in ws] * 4,
        compiler_params=_params("parallel"))(*ws, *ms, *vs, *[g for g_list in gs for g in g_list])
    return [outs[k * n:(k + 1) * n] for k in range(4)]


def _adamw_update(w, g, m, v):
    m = ADAM_B1 * m + (1.0 - ADAM_B1) * g
    v = ADAM_B2 * v + (1.0 - ADAM_B2) * (g * g)
    m_hat = m / ADAM_C1
    v_hat = v / ADAM_C2
    return -ADAM_LR * (m_hat / (jnp.sqrt(v_hat) + ADAM_EPS) + ADAM_WD * w), m, v


def adamw_small(ws, gs, ms, vs, *, name):
    n = len(ws)

    def body(*refs):
        for i in range(n):
            w_ref, g_ref, m_ref, v_ref = (refs[k * n + i] for k in range(4))
            d_ref, nm_ref, nv_ref = (refs[(4 + k) * n + i] for k in range(3))
            d_ref[...], nm_ref[...], nv_ref[...] = _adamw_update(w_ref[...], g_ref[...], m_ref[...], v_ref[...])

    outs = pl.pallas_call(
        body, name=name, out_shape=[jax.ShapeDtypeStruct(a.shape, f32) for a in ws] * 3)(*ws, *gs, *ms, *vs)
    return outs[:n], outs[n:2 * n], outs[2 * n:]


def _place():
    return lax.axis_index("x"), lax.axis_index("y"), lax.axis_index("c")


def gather_shards(bufs, *, name, split):
    n = len(bufs)

    def body(*refs):
        bufs_ = refs[:n]
        isend, irecv, dsend, drecv = refs[2 * n:]
        x, y, c = _place()
        k = 2 * x + y
        peers = [(1 - x, y, c), (x, 1 - y, c), (1 - x, 1 - y, c)]
        peer_k = [2 * (1 - x) + y, 2 * x + (1 - y), 2 * (1 - x) + (1 - y)]

        def slab(a, q, h):
            if not split[a]:
                return bufs_[a].at[q]
            half = bufs_[a].shape[1] // 2
            return bufs_[a].at[q, pl.ds(pl.multiple_of(h * half, 16), half)]

        def ici(a, j, q):
            return pltpu.make_async_remote_copy(
                src_ref=slab(a, q, c), dst_ref=slab(a, q, c), send_sem=isend.at[3 * a + j], recv_sem=irecv.at[3 * a + j],
                device_id=peers[j], device_id_type=MESH)

        def d2d(a, j, h):
            return pltpu.make_async_remote_copy(
                src_ref=slab(a, peer_k[j], h), dst_ref=slab(a, peer_k[j], h), send_sem=dsend.at[3 * a + j],
                recv_sem=drecv.at[3 * a + j], device_id=(x, y, 1 - c), device_id_type=MESH)

        for a in range(n):
            for j in range(3):
                ici(a, j, k).start()
        for a in range(n):
            for j in range(3):
                ici(a, j, peer_k[j]).wait_recv()
                if split[a]:
                    d2d(a, j, c).start()
        for a in range(n):
            for j in range(3):
                if split[a]:
                    d2d(a, j, 1 - c).wait_recv()
        for a in range(n):
            for j in range(3):
                ici(a, j, k).wait_send()
                if split[a]:
                    d2d(a, j, c).wait_send()

    return pl.pallas_call(
        body, name=name, in_specs=[ANY] * n, out_specs=[ANY] * n,
        out_shape=[jax.ShapeDtypeStruct(b.shape, b.dtype) for b in bufs],
        input_output_aliases={i: i for i in range(n)},
        scratch_shapes=[pltpu.SemaphoreType.DMA((3 * n,))] * 4)(*bufs)


HBM = pl.BlockSpec(memory_space=pltpu.HBM)
SEM = pl.BlockSpec(memory_space=pltpu.SEMAPHORE)
DATAFLOW = pltpu.SideEffectType.DATAFLOW_SIDE_EFFECTING


def _chip_peers():
    x, y, c = _place()
    return 2 * x + y, [(1 - x, y, c), (x, 1 - y, c), (1 - x, 1 - y, c)], [2 * (1 - x) + y, 2 * x + (1 - y), 2 * (1 - x) + (1 - y)]


def _half_slab(ref, q, h):
    half = ref.shape[1] // 2
    return ref.at[q, pl.ds(pl.multiple_of(h * half, BF16_ROWS), half)]


def gather_start(bufs, groups, after, *, name):
    n = len(bufs)
    ng = len(groups)

    def body(*refs):
        ins = refs[:n]
        sends, recvs = refs[2 * n + 1:2 * n + 1 + ng], refs[2 * n + 1 + ng:2 * n + 1 + 2 * ng]
        token = refs[-1]
        c = lax.axis_index("c")
        k, peers, _ = _chip_peers()
        for gi, grp in enumerate(groups):
            for pos, a in enumerate(grp):
                for j in range(3):
                    pltpu.make_async_remote_copy(
                        src_ref=_half_slab(ins[a], k, c), dst_ref=_half_slab(ins[a], k, c), send_sem=sends[gi].at[3 * pos + j],
                        recv_sem=recvs[gi].at[3 * pos + j], device_id=peers[j], device_id_type=MESH).start()
        token[...] = jnp.zeros_like(token)

    sems = [pltpu.SemaphoreType.DMA((3 * len(grp),)) for grp in groups]
    outs = pl.pallas_call(
        body, name=name, in_specs=[HBM] * n + [ANY],
        out_specs=[HBM] * n + [SEM] * (2 * ng) + [pl.BlockSpec(memory_space=pltpu.VMEM)],
        out_shape=[pltpu.HBM(b.shape, b.dtype) for b in bufs] + sems + sems + [jax.ShapeDtypeStruct((8, LANES), f32)],
        input_output_aliases={i: i for i in range(n)},
        compiler_params=pltpu.CompilerParams(has_side_effects=DATAFLOW))(
            *[pltpu.with_memory_space_constraint(b, pltpu.HBM) for b in bufs], after)
    return outs[:n], outs[n:n + ng], outs[n + ng:n + 2 * ng], outs[-1]


def gather_wait(bufs, send_sems, recv_sems, after, *, name):
    n = len(bufs)

    def body(*refs):
        ins = refs[:n]
        send, recv = refs[n], refs[n + 1]
        c = lax.axis_index("c")
        k, peers, peer_k = _chip_peers()
        for a in range(n):
            for j in range(3):
                copy = pltpu.make_async_remote_copy(
                    src_ref=_half_slab(ins[a], k, c), dst_ref=_half_slab(ins[a], peer_k[j], c), send_sem=send.at[3 * a + j],
                    recv_sem=recv.at[3 * a + j], device_id=peers[j], device_id_type=MESH)
                copy.wait_send()
                copy.wait_recv()

    return pl.pallas_call(
        body, name=name, in_specs=[HBM] * n + [SEM, SEM, ANY], out_specs=[HBM] * n,
        out_shape=[pltpu.HBM(b.shape, b.dtype) for b in bufs],
        input_output_aliases={i: i for i in range(n)},
        compiler_params=pltpu.CompilerParams(has_side_effects=DATAFLOW))(*bufs, send_sems, recv_sems, after)


def forward_halves(bufs, *, name):
    n = len(bufs)

    def body(*refs):
        bufs_ = refs[:n]
        send, recv = refs[2 * n:]
        x, y, c = _place()
        _, _, peer_k = _chip_peers()

        def copy(a, j, h):
            return pltpu.make_async_remote_copy(
                src_ref=_half_slab(bufs_[a], peer_k[j], h), dst_ref=_half_slab(bufs_[a], peer_k[j], h),
                send_sem=send.at[3 * a + j], recv_sem=recv.at[3 * a + j], device_id=(x, y, 1 - c), device_id_type=MESH)

        for a in range(n):
            for j in range(3):
                copy(a, j, c).start()
        for a in range(n):
            for j in range(3):
                copy(a, j, 1 - c).wait_recv()
        for a in range(n):
            for j in range(3):
                copy(a, j, c).wait_send()

    return pl.pallas_call(
        body, name=name, in_specs=[ANY] * n, out_specs=[ANY] * n,
        out_shape=[jax.ShapeDtypeStruct(b.shape, b.dtype) for b in bufs],
        input_output_aliases={i: i for i in range(n)},
        scratch_shapes=[pltpu.SemaphoreType.DMA((3 * n,))] * 2)(*bufs)


def _forward_copies(bufs_, send, recv, h):
    x, y, c = _place()
    _, _, peer_k = _chip_peers()
    return [pltpu.make_async_remote_copy(
        src_ref=_half_slab(bufs_[a], peer_k[j], h), dst_ref=_half_slab(bufs_[a], peer_k[j], h),
        send_sem=send.at[3 * a + j], recv_sem=recv.at[3 * a + j], device_id=(x, y, 1 - c), device_id_type=MESH)
        for a in range(len(bufs_)) for j in range(3)]


def forward_start(bufs, after, *, name):
    n = len(bufs)

    def body(*refs):
        for cp in _forward_copies(refs[:n], refs[2 * n + 1], refs[2 * n + 2], lax.axis_index("c")):
            cp.start()
        refs[-1][...] = jnp.zeros_like(refs[-1])

    sems = [pltpu.SemaphoreType.DMA((3 * n,))] * 2
    outs = pl.pallas_call(
        body, name=name, in_specs=[HBM] * n + [ANY],
        out_specs=[HBM] * n + [SEM] * 2 + [pl.BlockSpec(memory_space=pltpu.VMEM)],
        out_shape=[pltpu.HBM(b.shape, b.dtype) for b in bufs] + sems + [jax.ShapeDtypeStruct((8, LANES), f32)],
        input_output_aliases={i: i for i in range(n)},
        compiler_params=pltpu.CompilerParams(has_side_effects=DATAFLOW))(*bufs, after)
    return (n, outs[:-1]), outs[-1]


def forward_wait(state, after, *, name):
    n, held = state

    def body(*refs):
        c = lax.axis_index("c")
        for mine, theirs in zip(_forward_copies(refs[:n], refs[n], refs[n + 1], c),
                                _forward_copies(refs[:n], refs[n], refs[n + 1], 1 - c)):
            mine.wait_send()
            theirs.wait_recv()

    return pl.pallas_call(
        body, name=name, in_specs=[HBM] * n + [SEM] * 2 + [ANY], out_specs=[HBM] * n,
        out_shape=[pltpu.HBM(b.shape, b.dtype) for b in held[:n]],
        input_output_aliases={i: i for i in range(n)},
        compiler_params=pltpu.CompilerParams(has_side_effects=DATAFLOW))(*held, after)


def _sibling_copies(srcs, lands, send, recv):
    x, y, c = _place()
    return [pltpu.make_async_remote_copy(src_ref=srcs[a], dst_ref=lands[a], send_sem=send.at[a], recv_sem=recv.at[a],
                                         device_id=(x, y, 1 - c), device_id_type=MESH) for a in range(len(srcs))]


def sibling_start(arrays, after, *, name):
    n = len(arrays)
    lands = [pltpu.with_memory_space_constraint(lax.empty(a.shape, a.dtype), pltpu.HBM) for a in arrays]

    def body(*refs):
        for cp in _sibling_copies(refs[:n], refs[n:2 * n], refs[4 * n + 1], refs[4 * n + 2]):
            cp.start()
        refs[-1][...] = jnp.zeros_like(refs[-1])

    bufs = list(arrays) + lands
    sems = [pltpu.SemaphoreType.DMA((n,))] * 2
    outs = pl.pallas_call(
        body, name=name, in_specs=[HBM] * (2 * n) + [ANY],
        out_specs=[HBM] * (2 * n) + [SEM] * 2 + [pl.BlockSpec(memory_space=pltpu.VMEM)],
        out_shape=[pltpu.HBM(b.shape, b.dtype) for b in bufs] + sems + [jax.ShapeDtypeStruct((8, LANES), f32)],
        input_output_aliases={i: i for i in range(2 * n)},
        compiler_params=pltpu.CompilerParams(has_side_effects=DATAFLOW))(
            *[pltpu.with_memory_space_constraint(b, pltpu.HBM) for b in bufs], after)
    return (n, outs[:-1]), outs[-1]


def sibling_wait(state, after, *, name):
    n, held = state

    def body(*refs):
        for cp in _sibling_copies(refs[:n], refs[n:2 * n], refs[2 * n], refs[2 * n + 1]):
            cp.wait_send()
            cp.wait_recv()

    outs = pl.pallas_call(
        body, name=name, in_specs=[HBM] * (2 * n) + [SEM] * 2 + [ANY], out_specs=[HBM] * (2 * n),
        out_shape=[pltpu.HBM(b.shape, b.dtype) for b in held[:2 * n]],
        input_output_aliases={i: i for i in range(2 * n)},
        compiler_params=pltpu.CompilerParams(has_side_effects=DATAFLOW))(*held, after)
    return outs[n:]


ALL_MASKS = [(mx, my, mc) for mx in (0, 1) for my in (0, 1) for mc in (0, 1)][1:]


def _scatter_copies(srcs, lands, ev, send, recv, esend, erecv):
    x, y, c = _place()
    me = 4 * x + 2 * y + c
    k, peers, peer_k = _chip_peers()
    out = []
    for a in range(len(srcs)):
        for j in range(3):
            out.append(pltpu.make_async_remote_copy(
                src_ref=srcs[a].at[peer_k[j]], dst_ref=lands[a].at[j], send_sem=send.at[3 * a + j],
                recv_sem=recv.at[3 * a + j], device_id=peers[j], device_id_type=MESH))
    start_ev, wait_ev = [], []
    if ev is not None:
        for j, (mx, my, mc) in enumerate(ALL_MASKS):
            peer = (x ^ mx, y ^ my, c ^ mc)
            start_ev.append(pltpu.make_async_remote_copy(
                src_ref=ev.at[me], dst_ref=ev.at[me], send_sem=esend.at[j], recv_sem=erecv.at[j],
                device_id=peer, device_id_type=MESH))
            wait_ev.append(pltpu.make_async_remote_copy(
                src_ref=ev.at[me], dst_ref=ev.at[me ^ (4 * mx + 2 * my + mc)], send_sem=esend.at[j],
                recv_sem=erecv.at[j], device_id=peer, device_id_type=MESH))
    return out, start_ev, wait_ev


def chip_scatter_start(arrays, everyone, after, *, name):
    n = len(arrays)
    ne = 0 if everyone is None else 1
    lands = [pltpu.with_memory_space_constraint(lax.empty((3,) + a.shape[1:], a.dtype), pltpu.HBM) for a in arrays]

    def body(*refs):
        srcs, lands_ = refs[:n], refs[n:2 * n]
        ev = refs[2 * n] if ne else None
        sems = refs[2 * n + ne + 1 + 2 * n + ne:-1]
        send, recv = sems[0], sems[1]
        esend, erecv = (sems[2], sems[3]) if ne else (None, None)
        copies, start_ev, _ = _scatter_copies(srcs, lands_, ev, send, recv, esend, erecv)
        for cp in start_ev + copies:
            cp.start()
        refs[-1][...] = jnp.zeros_like(refs[-1])

    sem_shapes = [pltpu.SemaphoreType.DMA((3 * n,))] * 2 + [pltpu.SemaphoreType.DMA((7,))] * (2 * ne)
    bufs = list(arrays) + lands + ([everyone] if ne else [])
    outs = pl.pallas_call(
        body, name=name, in_specs=[HBM] * len(bufs) + [ANY],
        out_specs=[HBM] * len(bufs) + [SEM] * len(sem_shapes) + [pl.BlockSpec(memory_space=pltpu.VMEM)],
        out_shape=[pltpu.HBM(b.shape, b.dtype) for b in bufs] + sem_shapes + [jax.ShapeDtypeStruct((8, LANES), f32)],
        input_output_aliases={i: i for i in range(len(bufs))},
        compiler_params=pltpu.CompilerParams(has_side_effects=DATAFLOW))(
            *[pltpu.with_memory_space_constraint(b, pltpu.HBM) for b in bufs], after)
    return (n, ne, outs[:-1]), outs[-1]


def chip_scatter_wait(state, after, *, name):
    n, ne, held = state
    nb = 2 * n + ne
    bufs, sems = held[:nb], held[nb:]

    def body(*refs):
        srcs, lands_ = refs[:n], refs[n:2 * n]
        ev = refs[2 * n] if ne else None
        sems_ = refs[nb:nb + len(sems)]
        esend, erecv = (sems_[2], sems_[3]) if ne else (None, None)
        copies, _, wait_ev = _scatter_copies(srcs, lands_, ev, sems_[0], sems_[1], esend, erecv)
        for cp in wait_ev + copies:
            cp.wait_send()
            cp.wait_recv()

    outs = pl.pallas_call(
        body, name=name, in_specs=[HBM] * nb + [SEM] * len(sems) + [ANY], out_specs=[HBM] * nb,
        out_shape=[pltpu.HBM(b.shape, b.dtype) for b in bufs],
        input_output_aliases={i: i for i in range(nb)},
        compiler_params=pltpu.CompilerParams(has_side_effects=DATAFLOW))(*bufs, *sems, after)
    return outs[n:2 * n], (outs[2 * n] if ne else None)


def sibling_merge(bufs, *, name):
    n = len(bufs)

    def body(*refs):
        bufs_ = refs[:n]
        send, recv = refs[2 * n:]
        x, y, c = _place()

        def copy(u, h):
            return pltpu.make_async_remote_copy(
                src_ref=bufs_[u].at[h], dst_ref=bufs_[u].at[h], send_sem=send.at[u], recv_sem=recv.at[u],
                device_id=(x, y, 1 - c), device_id_type=MESH)

        for u in range(n):
            copy(u, c).start()
        for u in range(n):
            copy(u, 1 - c).wait_recv()
        for u in range(n):
            copy(u, c).wait_send()

    return pl.pallas_call(
        body, name=name, in_specs=[ANY] * n, out_specs=[ANY] * n,
        out_shape=[jax.ShapeDtypeStruct(b.shape, b.dtype) for b in bufs],
        input_output_aliases={i: i for i in range(n)},
        scratch_shapes=[pltpu.SemaphoreType.DMA((n,)), pltpu.SemaphoreType.DMA((n,))])(*bufs)


def sum_leading(a, *, name):
    n, r, c = a.shape

    def body(a_ref, o_ref):
        acc = a_ref[0]
        for i in range(1, n):
            acc = acc + a_ref[i]
        o_ref[...] = acc

    rb = r // 2 if r % 16 == 0 else r
    return pl.pallas_call(
        body, name=name, grid=(r // rb,), in_specs=[pl.BlockSpec((n, rb, c), lambda i: (0, i, 0))],
        out_specs=pl.BlockSpec((rb, c), lambda i: (i, 0)), out_shape=jax.ShapeDtypeStruct((r, c), f32),
        compiler_params=_params("parallel"))(a)


def _half_rows(shape):
    return shape[1] // 2 // 2


def rs_add_sibling(grads, recvd, ck_arr, *, name):
    n = len(grads)

    def body(ck_ref, *refs):
        s = pl.program_id(1)
        for u in range(n):
            g_ref, r_ref = refs[u], refs[n + u]
            qb_ref, own_ref = refs[2 * n + u], refs[3 * n + u]
            q = g_ref[0] + r_ref[0].astype(f32)
            qb_ref[0] = q.astype(bf16)

            @pl.when(s == ck_ref[1])
            def _(own_ref=own_ref, q=q):
                own_ref[...] = q

    in_specs = [pl.BlockSpec((1, _half_rows(g.shape), g.shape[2]), lambda r, s, ck: (s, ck[0] * 2 + r, 0)) for g in grads]
    in_specs += [pl.BlockSpec((1, _half_rows(g.shape), g.shape[2]), lambda r, s, ck: (s, r, 0)) for g in grads]
    out_specs = [pl.BlockSpec((1, _half_rows(g.shape), g.shape[2]), lambda r, s, ck: (s, r, 0)) for g in grads]
    out_specs += [pl.BlockSpec((_half_rows(g.shape), g.shape[2]), lambda r, s, ck: (r, 0)) for g in grads]
    outs = pl.pallas_call(
        body, name=name,
        grid_spec=pltpu.PrefetchScalarGridSpec(num_scalar_prefetch=1, grid=(2, N_SHARDS),
                                               in_specs=in_specs, out_specs=out_specs),
        out_shape=[jax.ShapeDtypeStruct((N_SHARDS, g.shape[1] // 2, g.shape[2]), bf16) for g in grads]
        + [jax.ShapeDtypeStruct((g.shape[1] // 2, g.shape[2]), f32) for g in grads],
        compiler_params=_params("parallel", "arbitrary"))(ck_arr, *grads, *recvd)
    return outs[:n], outs[n:]


def rs_sum_chips(owns, recvd, ck_arr, *, name):
    n = len(owns)

    def body(ck_ref, *refs):
        for u in range(n):
            own_ref, r_ref, o_ref = refs[u], refs[n + u], refs[2 * n + u]
            o_ref[0] = ((own_ref[...] + r_ref[0].astype(f32)) + r_ref[1].astype(f32)) + r_ref[2].astype(f32)

    in_specs = [pl.BlockSpec((o.shape[0] // 2, o.shape[1]), lambda r, ck: (r, 0)) for o in owns]
    in_specs += [pl.BlockSpec((3, o.shape[0] // 2, o.shape[1]), lambda r, ck: (0, r, 0)) for o in owns]
    out_specs = [pl.BlockSpec((1, o.shape[0] // 2, o.shape[1]), lambda r, ck: (ck[0], r, 0)) for o in owns]
    return pl.pallas_call(
        body, name=name,
        grid_spec=pltpu.PrefetchScalarGridSpec(num_scalar_prefetch=1, grid=(2,), in_specs=in_specs, out_specs=out_specs),
        out_shape=[jax.ShapeDtypeStruct((2,) + o.shape, f32) for o in owns],
        compiler_params=_params("parallel"))(ck_arr, *owns, *recvd)


SMALL = ("a_norm", "a_v_norm", "a_w_s", "a_b_s", "f_norm", "f_conv_w", "f_conv_b", "kv_norm", "k_norm",
         "b_norm", "b_q_norm", "b_sinks")
BIG = ("a_w_in", "a_w_out", "f_w_in", "f_w_out", "w_kv", "b_w_q", "b_w_o")
PACK_COLS = 1024
PACK_ROWS = 8 * N_STEPS


def _pack(parts, rows=PACK_ROWS):
    flat = jnp.concatenate([p.reshape(-1).astype(f32) for p in parts])
    pad = (-flat.shape[0]) % (rows * PACK_COLS)
    return jnp.pad(flat, (0, pad)).reshape(-1, PACK_COLS)


def _unpack(packed, shapes):
    flat = packed.reshape(-1)
    out, off = [], 0
    for s in shapes:
        size = math.prod(s)
        out.append(flat[off:off + size].reshape(s))
        off += size
    return out


def _behind(value, token):
    return lax.optimization_barrier((value, token))[0]


def _ffn_fwd(x, g, h, r, w_in4, conv_w, conv_b, f, tag):
    wg, wu = conv_w[:, :f], conv_w[:, f:]
    bg, bu = conv_b[None, :f], conv_b[None, f:]
    pg, pu, gate, up, a = ffn_in_fused(h, w_in4, wg, wu, bg, bu, name=f"ffn{tag}_in")
    return a, (x, g, h, r, pg, pu, gate, up, a, wg, wu)


def _ffn_bwd(dy, saved, w_in4, w_out, c_arr, tag, exchange=False):
    x, g, h, r, pg, pu, gate, up, a, wg, wu = saved
    f = w_out.shape[0]
    d_w_out = mm_tn(a, [dy], c_arr, name=f"ffn{tag}_dwout", n_s=w_out.shape[1], shard_rows=f // N_SHARDS, tki=f // 2)
    dpg, dpu, sg, su = ffn_gate_bwd(dy, w_out, pg, pu, gate, up, wg, wu, name=f"ffn{tag}_dgate")
    d_w_in = mm_tn(h, [dpg, dpu], c_arr, name=f"ffn{tag}_dwin", n_s=w_in4.shape[2], shard_rows=h.shape[1])
    state = None
    if exchange:
        state, token = sibling_start([d_w_in[1], d_w_out[1]], d_w_in[0], name=f"rs_sibling_start_ffn{tag}")
        g = _behind(g, token)
    dx, dg = mm_nt_rms_bwd([dpg, dpu], w_in4, x, r, g, dy, name=f"ffn{tag}_dh")
    d_conv_w = jnp.concatenate([sg[0:3], su[0:3]], axis=1)
    d_conv_b = jnp.concatenate([sg[3], su[3]], axis=0)
    return dx, dg, d_w_in, d_conv_w, d_conv_b, d_w_out, state


def _rs_front(pairs, sibling_state, after, c_arr, tag):
    units = [full.reshape(N_SHARDS, -1, full.shape[-1]) for full, _ in pairs]
    from_sib = sibling_wait(sibling_state, after, name=f"rs_sibling_wait{tag}")
    return rs_add_sibling(units, from_sib, c_arr, name=f"rs_add{tag}")


def _rs_back(own, from_chips, c_arr, tag):
    halves = rs_sum_chips(list(own), list(from_chips), c_arr, name=f"rs_sum{tag}")
    return [m.reshape(-1, m.shape[2]) for m in sibling_merge(list(halves), name=f"rs_merge{tag}")]


def kernel(x, a_norm, a_w_in, a_v_norm, a_w_s, a_b_s, a_w_out, f_norm, f_w_in, f_conv_w, f_conv_b, f_w_out, kv_norm, w_kv, k_norm, b_norm, b_w_q, b_q_norm, b_sinks, b_w_o, loss_target, m_a_norm, m_a_w_in, m_a_v_norm, m_a_w_s, m_a_b_s, m_a_w_out, m_f_norm, m_f_w_in, m_f_conv_w, m_f_conv_b, m_f_w_out, m_kv_norm, m_w_kv, m_k_norm, m_b_norm, m_b_w_q, m_b_q_norm, m_b_sinks, m_b_w_o, v_a_norm, v_a_w_in, v_a_v_norm, v_a_w_s, v_a_b_s, v_a_w_out, v_f_norm, v_f_w_in, v_f_conv_w, v_f_conv_b, v_f_w_out, v_kv_norm, v_w_kv, v_k_norm, v_b_norm, v_b_w_q, v_b_q_norm, v_b_sinks, v_b_w_o):
    args = dict(locals())
    weights = {n: args[n] for n in SMALL + BIG}
    moms = {n: args["m_" + n] for n in SMALL + BIG}
    vars_ = {n: args["v_" + n] for n in SMALL + BIG}
    t, d = x.shape[1], x.shape[2]
    xi, yi, ci = _place()
    chip = 2 * xi + yi

    big_local = [a_w_in[0], a_w_out[0], f_w_in, f_w_out, w_kv, b_w_q[0], b_w_o[0]]
    c_arr = jnp.stack([ci, chip]).astype(jnp.int32)
    k_arr = jnp.stack([chip]).astype(jnp.int32)
    b_ain, b_aout = cast_into_slot(big_local[:2], k_arr, name="cast_first")
    small_cols = _pack([a_norm, a_v_norm, f_conv_w], rows=2 * BF16_ROWS)
    b_small = lax.dynamic_update_slice(jnp.zeros((N_SHARDS,) + small_cols.shape, f32), small_cols[None], (chip, 0, 0))
    first, send_first, recv_first, token_first = gather_start([b_small, b_ain, b_aout], [[0, 1, 2]], small_cols,
                                                              name="gather_start_first")
    b_fin0, b_fin1, b_fout0, b_fout1, b_kv, b_q, b_o = cast_into_slot(
        [_behind(big_local[2], token_first)] + big_local[3:], k_arr, name="cast_weights")
    g_small, w_a_in, g_a_w_out = forward_halves(
        gather_wait(first, send_first[0], recv_first[0], b_kv, name="gather_wait_first"), name="gather_forward_first")
    later, send_sems, recv_sems, token = gather_start([b_fin0, b_fout0, b_kv, b_q, b_o, b_fin1, b_fout1],
                                                      [[0], [1], [2, 3, 4], [5, 6]], g_small, name="gather_start")
    ns_cols = a_norm.shape[1]
    nf_cols = f_conv_w.shape[2]
    parts = [_unpack(g_small[k], [a_norm.shape, a_v_norm.shape, f_conv_w.shape]) for k in range(N_SHARDS)]
    a_norm_f = jnp.concatenate([p[0] for p in parts], axis=1) + token[0, 0]
    a_v_norm_f = jnp.concatenate([p[1] for p in parts], axis=1)
    conv_w_f = jnp.concatenate([p[2] for p in parts], axis=2)

    x0 = x[0]
    tril = jnp.tril(jnp.ones((CHUNK, CHUNK), dtype=bool))
    wc = jnp.where(tril[None], a_w_s[0], 0.0).astype(bf16)
    bt = a_b_s[0].T
    kg2 = jnp.tile(k_norm, 2)[None]
    qg2 = jnp.tile(b_q_norm[0], 2)[None]

    (h_a,), r_a = rms_fwd(x0, [a_norm_f], name="a_norm")
    zuv = mm_nn(h_a, w_a_in, name="a_in")
    y_a = sgu_gate_fwd(zuv, a_v_norm_f, wc, bt, name="a_gate")
    w_a_out = g_a_w_out.reshape(1, -1, d)
    f = f_w_out.shape[1] * N_SHARDS
    fwd0, tok0 = forward_start(gather_wait(later[0:1], send_sems[0], recv_sems[0], y_a, name="gather_wait_0"), y_a,
                               name="gather_forward_start_0")
    x1, (h_f0,), r_f0 = mm_residual(y_a, w_a_out[0], x0, name="a_out", gains=[_behind(f_norm[0:1], tok0)])
    (g_fin0,) = forward_wait(fwd0, x1, name="gather_forward_wait_0")
    w_f_in = [g_fin0, None]
    a0, ffn0 = _ffn_fwd(x1, f_norm[0:1], h_f0, r_f0, w_f_in[0], conv_w_f[0], f_conv_b[0], f, "0")
    (g_fout0,) = forward_halves(gather_wait(later[1:2], send_sems[1], recv_sems[1], a0, name="gather_wait_1"),
                                name="gather_forward_1")
    w_f_out = [g_fout0.reshape(-1, d), None]
    fwd1, tok1 = forward_start(gather_wait(later[2:5], send_sems[2], recv_sems[2], g_fout0, name="gather_wait_1b"), a0,
                               name="gather_forward_start_1b")
    x2, (h_k, h_q), r_b = mm_residual(a0, w_f_out[0], x1, name="ffn0_out", gains=[_behind(kv_norm[None], tok1), b_norm])
    g_w_kv, g_b_w_q, g_b_w_o = forward_wait(fwd1, x2, name="gather_forward_wait_1b")
    w_kv_f = g_w_kv.reshape(1, d, -1)
    w_q_f = g_b_w_q.reshape(1, d, -1)
    w_o_f = g_b_w_o.reshape(1, -1, d)
    kv, k2, v2 = kv_proj_post(h_k, w_kv_f[0], kg2, name="kv_proj")
    qp, qn = q_proj_norm(h_q, w_q_f[0], qg2, name="q_proj", scale=HEAD_DIM ** -0.5)
    fwd2, tok2 = forward_start(gather_wait(later[5:7], send_sems[3], recv_sems[3], qn, name="gather_wait_2"), qn,
                               name="gather_forward_start_2")
    o = attn_fwd(qn, k2, v2, _behind(b_sinks[0], tok2), name="attn")
    x3, (h_f1,), r_f1 = mm_residual(o, w_o_f[0], x2, name="o_proj", gains=[f_norm[1:2]])
    g_fin1, g_fout1 = forward_wait(fwd2, x3, name="gather_forward_wait_2")
    w_f_in[1] = g_fin1
    w_f_out[1] = g_fout1.reshape(-1, d)
    a1, ffn1 = _ffn_fwd(x3, f_norm[1:2], h_f1, r_f1, w_f_in[1], conv_w_f[1], f_conv_b[1], f, "1")
    dx4, sq = mm_residual(a1, w_f_out[1], x3, name="ffn1_out", target=loss_target[0])
    loss_part = (0.5 * jnp.sum(sq) / d).reshape(1)

    proj_rows = d // N_SHARDS
    dx3, d_fn1, d_fwin1, d_cw1, d_cb1, d_fwout1, _ = _ffn_bwd(dx4, ffn1, w_f_in[1], w_f_out[1], c_arr, "1")
    do = mm_nt([dx3], w_o_f, name="o_proj_dx")
    d_w_o = mm_tn(o, [dx3], c_arr, name="o_proj_dw", n_s=d, shard_rows=o.shape[1] // N_SHARDS)
    dqn, dk2, dv2, dsink = attn_bwd(qn, k2, v2, do, b_sinks[0], name="attn_bwd")
    dqp, dqg = q_norm_bwd(dqn, qp, qg2, name="q_norm_bwd", scale=HEAD_DIM ** -0.5)
    dkv, dkg = kv_post_bwd(dk2, dv2, kv, kg2, name="kv_post_bwd")
    d_w_q = mm_tn(h_q, [dqp], c_arr, name="q_proj_dw", n_s=w_q_f.shape[2], shard_rows=proj_rows)
    d_w_kv = mm_tn(h_k, [dkv], c_arr, name="kv_proj_dw", n_s=w_kv_f.shape[2], shard_rows=proj_rows)
    group1 = [d_fwin1, d_fwout1, d_w_kv, d_w_q, d_w_o]
    sib1, token_s1 = sibling_start([half for _, half in group1], d_w_kv[0], name="rs_sibling_start1")
    dh_k = mm_nt([dkv], w_kv_f, name="kv_proj_dx")
    dx2, d_bn, d_kvn = mm_nt_rms_bwd([dqp], w_q_f, x2, r_b, _behind(b_norm, token_s1), dx3, name="q_proj_dx",
                                     extra=(dh_k, kv_norm[None]))
    chip_bf1, own1 = _rs_front(group1, sib1, dx2, c_arr, "1")
    scatter1, token1 = chip_scatter_start(list(chip_bf1), None, dx2, name="rs_chips_start1")
    ffn0 = ffn0[:9] + (_behind(ffn0[9], token1),) + ffn0[10:]
    dx1, d_fn0, d_fwin0, d_cw0, d_cb0, d_fwout0, sib2 = _ffn_bwd(dx2, ffn0, w_f_in[0], w_f_out[0], c_arr, "0", exchange=True)
    chip_bf2, own2 = _rs_front([d_fwin0, d_fwout0], sib2, dx1, c_arr, "2")
    scatter2, token2 = chip_scatter_start(list(chip_bf2), None, dx1, name="rs_chips_start2")
    a_v_norm_f = _behind(a_v_norm_f, token2)
    dy_a = mm_nt([dx1], w_a_out, name="a_out_dx")
    d_w_aout = mm_tn(y_a, [dx1], c_arr, name="a_out_dw", n_s=d, shard_rows=y_a.shape[1] // N_SHARDS)
    dzu, dzv, d_avn, d_ws, d_bt = sgu_gate_bwd(zuv, dy_a, a_v_norm_f, wc, bt, name="a_gate_bwd")
    d_w_ain = mm_tn(h_a, [dzu, dzv], c_arr, name="a_in_dw", n_s=w_a_in.shape[2], shard_rows=d)
    sib3, token_s3 = sibling_start([d_w_ain[1], d_w_aout[1]], d_w_ain[0], name="rs_sibling_start3")
    dx0, d_an = mm_nt_rms_bwd([dzu, dzv], w_a_in, x0, r_a, _behind(a_norm_f, token_s3), dx1, name="a_in_dx")
    grad_x = dx0[None]

    chip_bf3, own3 = _rs_front([d_w_ain, d_w_aout], sib3, dx0, c_arr, "3")
    d_fn = jnp.concatenate([d_fn0, d_fn1], axis=0)
    d_cw = jnp.stack([d_cw0, d_cw1])
    d_cb = jnp.stack([d_cb0, d_cb1])
    d_kg = (dkg[0, :HEAD_DIM] + dkg[0, HEAD_DIM:])
    d_qg = (dqg[0, :HEAD_DIM] + dqg[0, HEAD_DIM:])[None]
    small_full = [d_an, d_avn, d_ws[None], d_bt.T[None], d_fn, d_cw, d_cb, d_kvn[0], d_kg, d_bn, d_qg,
                  dsink[:, :N_Q_HEADS], loss_part]
    packed = _pack(small_full)
    me = 4 * xi + 2 * yi + ci
    everyone = lax.dynamic_update_slice(lax.empty((N_DEV,) + packed.shape, f32), packed[None], (me, 0, 0))
    scatter3, token3 = chip_scatter_start(list(chip_bf3), everyone, own3[0], name="rs_chips_start3")
    from_chips1, _ = chip_scatter_wait(scatter1, token3, name="rs_chips_wait1")
    from_chips2, _ = chip_scatter_wait(scatter2, from_chips1[0], name="rs_chips_wait2")
    fin1, fout1, gkv, gq, go, fin0, fout0 = _rs_back(list(own1) + list(own2), list(from_chips1) + list(from_chips2),
                                                     c_arr, "12")
    late = ("f_w_in", "f_w_out", "w_kv", "b_w_q", "b_w_o")
    res_late = adamw([weights[n] for n in late], [[fin0, fin1], [fout0, fout1], [gkv], [gq], [go]],
                     [moms[n] for n in late], [vars_[n] for n in late], name="adamw_late")
    from_chips3, from_all = chip_scatter_wait(scatter3, res_late[1][2], name="rs_chips_wait3")
    ain, aout = _rs_back(own3, from_chips3, c_arr, "3")
    first = ("a_w_in", "a_w_out")
    res_first = adamw([weights[n] for n in first], [[ain], [aout]], [moms[n] for n in first],
                      [vars_[n] for n in first], name="adamw_first")
    big = {n: tuple(r[i] for r in res_late) for i, n in enumerate(late)}
    big.update({n: tuple(r[i] for r in res_first) for i, n in enumerate(first)})

    full_shapes = [g.shape for g in small_full]
    small_g = _unpack(sum_leading(from_all, name="small_sum"), full_shapes)
    loss = small_g.pop()[0]
    small_g[0] = lax.dynamic_slice_in_dim(small_g[0], chip * ns_cols, ns_cols, axis=1)
    small_g[1] = lax.dynamic_slice_in_dim(small_g[1], chip * ns_cols, ns_cols, axis=1)
    small_g[5] = lax.dynamic_slice_in_dim(small_g[5], chip * nf_cols, nf_cols, axis=2)
    small_shapes = [weights[n].shape for n in SMALL]
    small_g = [g.reshape(s) for g, s in zip(small_g, small_shapes)]
    flat2 = [(math.prod(s[:-1]), s[-1]) for s in small_shapes]
    small_d, small_m, small_v = adamw_small(
        *[[a.reshape(s2) for a, s2 in zip(group, flat2)]
          for group in ([weights[n] for n in SMALL], small_g, [moms[n] for n in SMALL], [vars_[n] for n in SMALL])],
        name="adamw_small")
    small_d, small_m, small_v = ([a.reshape(s) for a, s in zip(group, small_shapes)]
                                 for group in (small_d, small_m, small_v))

    out = {}
    for i, n in enumerate(SMALL):
        out[n] = (small_g[i], small_d[i], small_m[i], small_v[i])
    out.update(big)
    order = ["a_norm", "a_w_in", "a_v_norm", "a_w_s", "a_b_s", "a_w_out", "f_norm", "f_w_in", "f_conv_w", "f_conv_b",
             "f_w_out", "kv_norm", "w_kv", "k_norm", "b_norm", "b_w_q", "b_q_norm", "b_sinks", "b_w_o"]
    return (loss, grad_x, *[out[n][0] for n in order], *[out[n][1] for n in order],
            *[out[n][2] for n in order], *[out[n][3] for n in order])
```
